```python
import math
import jax, jax.numpy as jnp
from jax import lax
import numpy as np

D_MODEL = 1024
BATCH = 8
SEQ = 8192
DEPTH = 1

ATTN_HEADS = 16
ATTN_KV_HEADS = 4
HEAD_DIM = 64
ATTN_WIDTH = ATTN_HEADS * HEAD_DIM
KV_WIDTH = ATTN_KV_HEADS * HEAD_DIM
WINDOW = 128
BLOCK = 128
REL_BUCKETS = 32
REL_MAX_DIST = 128
SSM_EXPAND = 2
SSM_WIDTH = SSM_EXPAND * D_MODEL
SSM_HEAD_DIM = 64
SSM_HEADS = SSM_WIDTH // SSM_HEAD_DIM
SSM_GROUPS = 4
SSM_HEADS_PER_GROUP = SSM_HEADS // SSM_GROUPS
SSM_STATE = 128
CONV_WIDTH = 4
CHUNK = 128
XBC_WIDTH = SSM_WIDTH + 2 * SSM_GROUPS * SSM_STATE
DT_MIN = 0.001
DT_MAX = 0.1
N_BRANCH = 2
IN_WIDTH = ATTN_WIDTH + 2 * KV_WIDTH + ATTN_WIDTH + SSM_WIDTH + XBC_WIDTH + SSM_HEADS + N_BRANCH * D_MODEL
EPS = 1e-6

kernel_name = "hybrid_swa_sink_ssd_gated_merge"


def rms_norm(x, w, eps=EPS):
    xf = x.astype(jnp.float32)
    xf = xf * lax.rsqrt(jnp.mean(xf * xf, axis=-1, keepdims=True) + eps)
    return (xf * w.astype(jnp.float32)).astype(x.dtype)


def t5_causal_bucket(dist):
    n = jnp.maximum(dist, 0)
    max_exact = REL_BUCKETS // 2
    nf = jnp.maximum(n, 1).astype(jnp.float32)
    large = max_exact + (jnp.log(nf / max_exact) / math.log(REL_MAX_DIST / max_exact)
                         * (REL_BUCKETS - max_exact)).astype(jnp.int32)
    large = jnp.minimum(large, REL_BUCKETS - 1)
    return jnp.where(n < max_exact, n, large)


def sliding_window_attention(q, k, v, q_norm_w, k_norm_w, rel_bias, sinks):
    b, s = q.shape[0], q.shape[1]
    nb = s // BLOCK
    grp = ATTN_HEADS // ATTN_KV_HEADS
    q = rms_norm(q, q_norm_w)
    k = rms_norm(k, k_norm_w)
    qb = q.reshape(b, nb, BLOCK, ATTN_KV_HEADS, grp, HEAD_DIM)

    def banded(t):
        tb = t.reshape(b, nb, BLOCK, ATTN_KV_HEADS, HEAD_DIM)
        prev = jnp.concatenate([jnp.zeros_like(tb[:, :1]), tb[:, :-1]], axis=1)
        return jnp.concatenate([prev, tb], axis=2)

    kb, vb = banded(k), banded(v)
    scores = jnp.einsum("bnqhgd,bnkhd->bnhgqk", qb, kb).astype(jnp.float32) * (HEAD_DIM ** -0.5)

    qi = jnp.arange(BLOCK)[:, None]
    kj = jnp.arange(2 * BLOCK)[None, :]
    dist = qi + BLOCK - kj
    key_pos = jnp.arange(nb)[:, None, None] * BLOCK - BLOCK + kj[None]
    mask = (dist >= 0) & (dist < WINDOW) & (key_pos >= 0)
    bias = rel_bias.astype(jnp.float32)[t5_causal_bucket(dist)]
    bias = bias.reshape(BLOCK, 2 * BLOCK, ATTN_KV_HEADS, grp).transpose(2, 3, 0, 1)
    scores = jnp.where(mask[None, :, None, None], scores + bias[None, None], -jnp.inf)

    sink = sinks.astype(jnp.float32).reshape(ATTN_KV_HEADS, grp)[None, None, :, :, None, None]
    m = jnp.maximum(scores.max(axis=-1, keepdims=True), sink)
    p = jnp.exp(scores - m)
    probs = p / (p.sum(axis=-1, keepdims=True) + jnp.exp(sink - m))
    out = jnp.einsum("bnhgqk,bnkhd->bnqhgd", probs.astype(v.dtype), vb)
    return out.reshape(b, s, ATTN_WIDTH)


def causal_depthwise_conv(x, w, bias):
    out = lax.conv_general_dilated(
        x, w[:, None, :].astype(x.dtype), window_strides=(1,), padding=[(CONV_WIDTH - 1, 0)],
        dimension_numbers=("NWC", "WIO", "NWC"), feature_group_count=x.shape[-1])
    return out + bias


def ssd_chunked(xs, dt, a, bm, cm):
    b, s = xs.shape[0], xs.shape[1]
    nc = s // CHUNK
    G, R, P, N = SSM_GROUPS, SSM_HEADS_PER_GROUP, SSM_HEAD_DIM, SSM_STATE
    xs = xs.reshape(b, nc, CHUNK, G, R, P).astype(jnp.float32)
    dt = dt.reshape(b, nc, CHUNK, G, R)
    bm = bm.reshape(b, nc, CHUNK, G, N).astype(jnp.float32)
    cm = cm.reshape(b, nc, CHUNK, G, N).astype(jnp.float32)
    a_cum = jnp.cumsum(dt * a, axis=2)
    xdt = xs * dt[..., None]
    causal = jnp.tril(jnp.ones((CHUNK, CHUNK), dtype=bool))[:, :, None, None]
    seg = a_cum[:, :, :, None] - a_cum[:, :, None, :]
    decay = jnp.exp(jnp.where(causal, seg, -jnp.inf))
    cb = jnp.einsum("bclgn,bcsgn->bclsg", cm, bm)
    y_diag = jnp.einsum("bclsg,bclsgr,bcsgrp->bclgrp", cb, decay, xdt)
    decay_to_end = jnp.exp(a_cum[:, :, -1:] - a_cum)
    states = jnp.einsum("bclgn,bclgr,bclgrp->bcgrpn", bm, decay_to_end, xdt)
    chunk_decay = jnp.exp(a_cum[:, :, -1])

    def step(h, inp):
        st, dec = inp
        return h * dec[..., None, None] + st, h

    h0 = jnp.zeros((b, G, R, P, N), jnp.float32)
    _, h_prev = lax.scan(step, h0, (jnp.moveaxis(states, 1, 0), jnp.moveaxis(chunk_decay, 1, 0)))
    h_prev = jnp.moveaxis(h_prev, 0, 1)
    y_off = jnp.einsum("bclgn,bcgrpn,bclgr->bclgrp", cm, h_prev, jnp.exp(a_cum))
    return (y_diag + y_off).reshape(b, s, G, R, P)


def hybrid_layer(x, c, w_ada, b_ada, norm_w, w_in, q_norm_w, k_norm_w, rel_bias, sinks,
                 conv_w, conv_b, dt_bias, a_log, d_skip, ssm_norm_w,
                 w_attn_proj, w_ssm_proj, w_out):
    b, s, _ = x.shape
    G, R, P, N = SSM_GROUPS, SSM_HEADS_PER_GROUP, SSM_HEAD_DIM, SSM_STATE
    mod = jax.nn.silu(c) @ w_ada + b_ada
    shift, scale, gate = jnp.split(mod, 3, axis=-1)
    h = rms_norm(x, norm_w) * (1 + scale[:, None]) + shift[:, None]
    proj = h @ w_in
    widths = [ATTN_WIDTH, KV_WIDTH, KV_WIDTH, ATTN_WIDTH, SSM_WIDTH, XBC_WIDTH, SSM_HEADS, D_MODEL, D_MODEL]
    q, k, v, z_a, z_m, xbc, dt_raw, g_a, g_b = jnp.split(proj, list(np.cumsum(widths)[:-1]), axis=-1)

    y_a = sliding_window_attention(q.reshape(b, s, ATTN_HEADS, HEAD_DIM),
                                   k.reshape(b, s, ATTN_KV_HEADS, HEAD_DIM),
                                   v.reshape(b, s, ATTN_KV_HEADS, HEAD_DIM),
                                   q_norm_w, k_norm_w, rel_bias, sinks)
    y_a = (y_a * jax.nn.silu(z_a)) @ w_attn_proj

    xbc = jax.nn.silu(causal_depthwise_conv(xbc, conv_w, conv_b))
    xs, bm, cm = jnp.split(xbc, [SSM_WIDTH, SSM_WIDTH + G * N], axis=-1)
    xs = xs.reshape(b, s, G, R, P)
    dt = jax.nn.softplus((dt_raw + dt_bias).astype(jnp.float32)).reshape(b, s, G, R)
    a = -jnp.exp(a_log.astype(jnp.float32)).reshape(G, R)
    y = ssd_chunked(xs, dt, a, bm.reshape(b, s, G, N), cm.reshape(b, s, G, N))
    y = y + d_skip.astype(jnp.float32).reshape(G, R)[:, :, None] * xs.astype(jnp.float32)
    y = y.reshape(b, s, SSM_WIDTH).astype(x.dtype) * jax.nn.silu(z_m)
    y = rms_norm(y.reshape(b, s, G, SSM_WIDTH // G), ssm_norm_w.reshape(G, SSM_WIDTH // G))
    y_b = y.reshape(b, s, SSM_WIDTH) @ w_ssm_proj

    merged = jax.nn.sigmoid(g_a) * y_a + jax.nn.sigmoid(g_b) * y_b
    return x + gate[:, None] * (merged @ w_out)


def _fwd_setup_inputs(seed: int = 0) -> dict:
    key = jax.random.key(seed)
    ks = jax.random.split(key, 20)
    nrm = jax.random.normal
    L, D = DEPTH, D_MODEL
    dt0 = jnp.exp(jax.random.uniform(ks[12], (L, SSM_HEADS)) * (math.log(DT_MAX) - math.log(DT_MIN)) + math.log(DT_MIN))
    return {
        "x": nrm(ks[0], (BATCH, SEQ, D), jnp.float32),
        "c": nrm(ks[1], (BATCH, D), jnp.float32),
        "w_ada": nrm(ks[2], (L, D, 3 * D)) * (0.5 * D ** -0.5),
        "b_ada": 0.01 * nrm(ks[3], (L, 3 * D)),
        "norm_w": 1 + 0.02 * nrm(ks[4], (L, D)),
        "w_in": nrm(ks[5], (L, D, IN_WIDTH)) * D ** -0.5,
        "q_norm_w": 1 + 0.02 * nrm(ks[6], (L, HEAD_DIM)),
        "k_norm_w": 1 + 0.02 * nrm(ks[7], (L, HEAD_DIM)),
        "rel_bias": 0.5 * nrm(ks[8], (REL_BUCKETS, ATTN_HEADS)),
        "sinks": 0.5 * nrm(ks[9], (L, ATTN_HEADS)),
        "conv_w": nrm(ks[10], (L, CONV_WIDTH, XBC_WIDTH)) * CONV_WIDTH ** -0.5,
        "conv_b": 0.01 * nrm(ks[11], (L, XBC_WIDTH)),
        "dt_bias": dt0 + jnp.log(-jnp.expm1(-dt0)),
        "a_log": jnp.log(jax.random.uniform(ks[13], (L, SSM_HEADS), minval=1.0, maxval=16.0)),
        "d_skip": 1 + 0.1 * nrm(ks[14], (L, SSM_HEADS)),
        "ssm_norm_w": 1 + 0.02 * nrm(ks[15], (L, SSM_WIDTH)),
        "w_attn_proj": nrm(ks[16], (L, ATTN_WIDTH, D)) * ATTN_WIDTH ** -0.5,
        "w_ssm_proj": nrm(ks[17], (L, SSM_WIDTH, D)) * SSM_WIDTH ** -0.5,
        "w_out": nrm(ks[18], (L, D, D)) * D ** -0.5,
    }


def _fwd_reference(x, c, w_ada, b_ada, norm_w, w_in, q_norm_w, k_norm_w, rel_bias, sinks,
              conv_w, conv_b, dt_bias, a_log, d_skip, ssm_norm_w,
              w_attn_proj, w_ssm_proj, w_out):
    for i in range(DEPTH):
        x = hybrid_layer(x, c, w_ada[i], b_ada[i], norm_w[i], w_in[i], q_norm_w[i], k_norm_w[i],
                         rel_bias, sinks[i], conv_w[i], conv_b[i], dt_bias[i], a_log[i], d_skip[i],
                         ssm_norm_w[i], w_attn_proj[i], w_ssm_proj[i], w_out[i])
    return x


import jax as _jax
import jax.numpy as _jnp

TWIN_FORMAT = 'train_step'
FWD_PARAMS = ['x', 'c', 'w_ada', 'b_ada', 'norm_w', 'w_in', 'q_norm_w', 'k_norm_w', 'rel_bias', 'sinks', 'conv_w', 'conv_b', 'dt_bias', 'a_log', 'd_skip', 'ssm_norm_w', 'w_attn_proj', 'w_ssm_proj', 'w_out']
TWIN_WEIGHTS = ['w_ada', 'b_ada', 'norm_w', 'w_in', 'q_norm_w', 'k_norm_w', 'rel_bias', 'sinks', 'conv_w', 'conv_b', 'dt_bias', 'a_log', 'd_skip', 'ssm_norm_w', 'w_attn_proj', 'w_ssm_proj', 'w_out']
TWIN_DIFF_INPUT = 'x'
TWIN_INPUTS = ['x', 'c', 'w_ada', 'b_ada', 'norm_w', 'w_in', 'q_norm_w', 'k_norm_w', 'rel_bias', 'sinks', 'conv_w', 'conv_b', 'dt_bias', 'a_log', 'd_skip', 'ssm_norm_w', 'w_attn_proj', 'w_ssm_proj', 'w_out', 'loss_target', 'm_w_ada', 'm_b_ada', 'm_norm_w', 'm_w_in', 'm_q_norm_w', 'm_k_norm_w', 'm_rel_bias', 'm_sinks', 'm_conv_w', 'm_conv_b', 'm_dt_bias', 'm_a_log', 'm_d_skip', 'm_ssm_norm_w', 'm_w_attn_proj', 'm_w_ssm_proj', 'm_w_out', 'v_w_ada', 'v_b_ada', 'v_norm_w', 'v_w_in', 'v_q_norm_w', 'v_k_norm_w', 'v_rel_bias', 'v_sinks', 'v_conv_w', 'v_conv_b', 'v_dt_bias', 'v_a_log', 'v_d_skip', 'v_ssm_norm_w', 'v_w_attn_proj', 'v_w_ssm_proj', 'v_w_out']
TWIN_OUTPUTS = ['loss', 'grad_x', 'grad_w_ada', 'grad_b_ada', 'grad_norm_w', 'grad_w_in', 'grad_q_norm_w', 'grad_k_norm_w', 'grad_rel_bias', 'grad_sinks', 'grad_conv_w', 'grad_conv_b', 'grad_dt_bias', 'grad_a_log', 'grad_d_skip', 'grad_ssm_norm_w', 'grad_w_attn_proj', 'grad_w_ssm_proj', 'grad_w_out', 'delta_w_ada', 'delta_b_ada', 'delta_norm_w', 'delta_w_in', 'delta_q_norm_w', 'delta_k_norm_w', 'delta_rel_bias', 'delta_sinks', 'delta_conv_w', 'delta_conv_b', 'delta_dt_bias', 'delta_a_log', 'delta_d_skip', 'delta_ssm_norm_w', 'delta_w_attn_proj', 'delta_w_ssm_proj', 'delta_w_out', 'new_m_w_ada', 'new_m_b_ada', 'new_m_norm_w', 'new_m_w_in', 'new_m_q_norm_w', 'new_m_k_norm_w', 'new_m_rel_bias', 'new_m_sinks', 'new_m_conv_w', 'new_m_conv_b', 'new_m_dt_bias', 'new_m_a_log', 'new_m_d_skip', 'new_m_ssm_norm_w', 'new_m_w_attn_proj', 'new_m_w_ssm_proj', 'new_m_w_out', 'new_v_w_ada', 'new_v_b_ada', 'new_v_norm_w', 'new_v_w_in', 'new_v_q_norm_w', 'new_v_k_norm_w', 'new_v_rel_bias', 'new_v_sinks', 'new_v_conv_w', 'new_v_conv_b', 'new_v_dt_bias', 'new_v_a_log', 'new_v_d_skip', 'new_v_ssm_norm_w', 'new_v_w_attn_proj', 'new_v_w_ssm_proj', 'new_v_w_out']
TWIN_LEAF_KINDS = {'loss': 'loss', 'grad_x': 'grad_x', 'grad_w_ada': 'grad_w', 'grad_b_ada': 'grad_w', 'grad_norm_w': 'grad_w', 'grad_w_in': 'grad_w', 'grad_q_norm_w': 'grad_w', 'grad_k_norm_w': 'grad_w', 'grad_rel_bias': 'grad_w', 'grad_sinks': 'grad_w', 'grad_conv_w': 'grad_w', 'grad_conv_b': 'grad_w', 'grad_dt_bias': 'grad_w', 'grad_a_log': 'grad_w', 'grad_d_skip': 'grad_w', 'grad_ssm_norm_w': 'grad_w', 'grad_w_attn_proj': 'grad_w', 'grad_w_ssm_proj': 'grad_w', 'grad_w_out': 'grad_w', 'delta_w_ada': 'delta_w', 'delta_b_ada': 'delta_w', 'delta_norm_w': 'delta_w', 'delta_w_in': 'delta_w', 'delta_q_norm_w': 'delta_w', 'delta_k_norm_w': 'delta_w', 'delta_rel_bias': 'delta_w', 'delta_sinks': 'delta_w', 'delta_conv_w': 'delta_w', 'delta_conv_b': 'delta_w', 'delta_dt_bias': 'delta_w', 'delta_a_log': 'delta_w', 'delta_d_skip': 'delta_w', 'delta_ssm_norm_w': 'delta_w', 'delta_w_attn_proj': 'delta_w', 'delta_w_ssm_proj': 'delta_w', 'delta_w_out': 'delta_w', 'new_m_w_ada': 'new_m', 'new_m_b_ada': 'new_m', 'new_m_norm_w': 'new_m', 'new_m_w_in': 'new_m', 'new_m_q_norm_w': 'new_m', 'new_m_k_norm_w': 'new_m', 'new_m_rel_bias': 'new_m', 'new_m_sinks': 'new_m', 'new_m_conv_w': 'new_m', 'new_m_conv_b': 'new_m', 'new_m_dt_bias': 'new_m', 'new_m_a_log': 'new_m', 'new_m_d_skip': 'new_m', 'new_m_ssm_norm_w': 'new_m', 'new_m_w_attn_proj': 'new_m', 'new_m_w_ssm_proj': 'new_m', 'new_m_w_out': 'new_m', 'new_v_w_ada': 'new_v', 'new_v_b_ada': 'new_v', 'new_v_norm_w': 'new_v', 'new_v_w_in': 'new_v', 'new_v_q_norm_w': 'new_v', 'new_v_k_norm_w': 'new_v', 'new_v_rel_bias': 'new_v', 'new_v_sinks': 'new_v', 'new_v_conv_w': 'new_v', 'new_v_conv_b': 'new_v', 'new_v_dt_bias': 'new_v', 'new_v_a_log': 'new_v', 'new_v_d_skip': 'new_v', 'new_v_ssm_norm_w': 'new_v', 'new_v_w_attn_proj': 'new_v', 'new_v_w_ssm_proj': 'new_v', 'new_v_w_out': 'new_v'}


def _forward(args):
    return _fwd_reference(*[args[k] for k in FWD_PARAMS])


def _output_shape():
    def fwd():
        inp = _fwd_setup_inputs(0)
        return _fwd_reference(*[inp[k] for k in FWD_PARAMS])
    out = _jax.eval_shape(fwd)
    return out.shape, out.dtype

N_MICROBATCH = 1
ADAM_LR = 0.001
ADAM_B1 = 0.9
ADAM_B2 = 0.999
ADAM_EPS = 1e-08
ADAM_WD = 0.01
ADAM_STEP = 10
PER_EXAMPLE_BATCH_AXIS = {'x': 0, 'c': 0, 'loss_target': 0}
SHARED_INPUTS = []
_WEIGHT_DTYPES = {'w_ada': _jnp.float32, 'b_ada': _jnp.float32, 'norm_w': _jnp.float32, 'w_in': _jnp.float32, 'q_norm_w': _jnp.float32, 'k_norm_w': _jnp.float32, 'rel_bias': _jnp.float32, 'sinks': _jnp.float32, 'conv_w': _jnp.float32, 'conv_b': _jnp.float32, 'dt_bias': _jnp.float32, 'a_log': _jnp.float32, 'd_skip': _jnp.float32, 'ssm_norm_w': _jnp.float32, 'w_attn_proj': _jnp.float32, 'w_ssm_proj': _jnp.float32, 'w_out': _jnp.float32}
MOMENT_SCALE = {'w_ada': 7.388586e-01, 'b_ada': 1.632251e+00, 'norm_w': 2.320678e-01, 'w_in': 4.333790e-02, 'q_norm_w': 1.803428e-01, 'k_norm_w': 1.806376e-01, 'rel_bias': 1.260647e-02, 'sinks': 3.256386e-02, 'conv_w': 6.152173e-02, 'conv_b': 1.464482e-01, 'dt_bias': 1.180146e-01, 'a_log': 6.997130e-01, 'd_skip': 4.263980e-01, 'ssm_norm_w': 1.359256e+00, 'w_attn_proj': 2.494353e-02, 'w_ssm_proj': 1.341638e-01, 'w_out': 1.162864e-01}


def _to_microbatches(a, axis):
    t = _jnp.moveaxis(a, axis, 0)
    t = t.reshape((N_MICROBATCH, t.shape[0] // N_MICROBATCH) + t.shape[1:])
    return _jnp.moveaxis(t, 1, axis + 1)


def setup_inputs(seed: int = 0) -> dict:
    inp = _fwd_setup_inputs(seed)
    key = _jax.random.fold_in(_jax.random.key(seed), 7919)
    shape, _ = _output_shape()
    out = dict(inp)
    out["loss_target"] = _jax.random.normal(_jax.random.fold_in(key, 0), shape, _jnp.float32)
    for i, name in enumerate(TWIN_WEIGHTS):
        w = inp[name].astype(_jnp.float32)
        if MOMENT_SCALE is None:
            s = _jnp.sqrt(_jnp.mean(_jnp.square(w)) + 1e-30)
        else:
            s = MOMENT_SCALE[name]
        km, kv = _jax.random.split(_jax.random.fold_in(key, i + 1))
        out[name] = w
        out["m_" + name] = s * _jax.random.normal(km, w.shape, _jnp.float32)
        out["v_" + name] = (s * s) * _jax.random.uniform(kv, w.shape, _jnp.float32, 0.5, 1.5)
    if N_MICROBATCH > 1:
        for name, axis in PER_EXAMPLE_BATCH_AXIS.items():
            out[name] = _to_microbatches(out[name], axis)
    return {'x': out['x'], 'c': out['c'], 'w_ada': out['w_ada'], 'b_ada': out['b_ada'], 'norm_w': out['norm_w'], 'w_in': out['w_in'], 'q_norm_w': out['q_norm_w'], 'k_norm_w': out['k_norm_w'], 'rel_bias': out['rel_bias'], 'sinks': out['sinks'], 'conv_w': out['conv_w'], 'conv_b': out['conv_b'], 'dt_bias': out['dt_bias'], 'a_log': out['a_log'], 'd_skip': out['d_skip'], 'ssm_norm_w': out['ssm_norm_w'], 'w_attn_proj': out['w_attn_proj'], 'w_ssm_proj': out['w_ssm_proj'], 'w_out': out['w_out'], 'loss_target': out['loss_target'], 'm_w_ada': out['m_w_ada'], 'm_b_ada': out['m_b_ada'], 'm_norm_w': out['m_norm_w'], 'm_w_in': out['m_w_in'], 'm_q_norm_w': out['m_q_norm_w'], 'm_k_norm_w': out['m_k_norm_w'], 'm_rel_bias': out['m_rel_bias'], 'm_sinks': out['m_sinks'], 'm_conv_w': out['m_conv_w'], 'm_conv_b': out['m_conv_b'], 'm_dt_bias': out['m_dt_bias'], 'm_a_log': out['m_a_log'], 'm_d_skip': out['m_d_skip'], 'm_ssm_norm_w': out['m_ssm_norm_w'], 'm_w_attn_proj': out['m_w_attn_proj'], 'm_w_ssm_proj': out['m_w_ssm_proj'], 'm_w_out': out['m_w_out'], 'v_w_ada': out['v_w_ada'], 'v_b_ada': out['v_b_ada'], 'v_norm_w': out['v_norm_w'], 'v_w_in': out['v_w_in'], 'v_q_norm_w': out['v_q_norm_w'], 'v_k_norm_w': out['v_k_norm_w'], 'v_rel_bias': out['v_rel_bias'], 'v_sinks': out['v_sinks'], 'v_conv_w': out['v_conv_w'], 'v_conv_b': out['v_conv_b'], 'v_dt_bias': out['v_dt_bias'], 'v_a_log': out['v_a_log'], 'v_d_skip': out['v_d_skip'], 'v_ssm_norm_w': out['v_ssm_norm_w'], 'v_w_attn_proj': out['v_w_attn_proj'], 'v_w_ssm_proj': out['v_w_ssm_proj'], 'v_w_out': out['v_w_out']}


def _loss(weights, diff, rest, loss_target):
    with _jax.named_scope("forward"):
        args = {**rest, TWIN_DIFF_INPUT: diff, **{k: w.astype(_WEIGHT_DTYPES[k]) for k, w in weights.items()}}
        y = _forward(args)
    with _jax.named_scope("loss_head"):
        err = _jnp.square(y.astype(_jnp.float32) - loss_target)
        return 0.5 * _jnp.sum(_jnp.mean(err, axis=-1)) if err.ndim else 0.5 * err


def _adamw(w, g, m, v):
    m = ADAM_B1 * m + (1.0 - ADAM_B1) * g
    v = ADAM_B2 * v + (1.0 - ADAM_B2) * _jnp.square(g)
    m_hat = m / (1.0 - ADAM_B1 ** ADAM_STEP)
    v_hat = v / (1.0 - ADAM_B2 ** ADAM_STEP)
    delta = -ADAM_LR * (m_hat / (_jnp.sqrt(v_hat) + ADAM_EPS) + ADAM_WD * w)
    return delta, m, v


def reference(x, c, w_ada, b_ada, norm_w, w_in, q_norm_w, k_norm_w, rel_bias, sinks, conv_w, conv_b, dt_bias, a_log, d_skip, ssm_norm_w, w_attn_proj, w_ssm_proj, w_out, loss_target, m_w_ada, m_b_ada, m_norm_w, m_w_in, m_q_norm_w, m_k_norm_w, m_rel_bias, m_sinks, m_conv_w, m_conv_b, m_dt_bias, m_a_log, m_d_skip, m_ssm_norm_w, m_w_attn_proj, m_w_ssm_proj, m_w_out, v_w_ada, v_b_ada, v_norm_w, v_w_in, v_q_norm_w, v_k_norm_w, v_rel_bias, v_sinks, v_conv_w, v_conv_b, v_dt_bias, v_a_log, v_d_skip, v_ssm_norm_w, v_w_attn_proj, v_w_ssm_proj, v_w_out):
    given = dict(x=x, c=c, w_ada=w_ada, b_ada=b_ada, norm_w=norm_w, w_in=w_in, q_norm_w=q_norm_w, k_norm_w=k_norm_w, rel_bias=rel_bias, sinks=sinks, conv_w=conv_w, conv_b=conv_b, dt_bias=dt_bias, a_log=a_log, d_skip=d_skip, ssm_norm_w=ssm_norm_w, w_attn_proj=w_attn_proj, w_ssm_proj=w_ssm_proj, w_out=w_out, loss_target=loss_target, m_w_ada=m_w_ada, m_b_ada=m_b_ada, m_norm_w=m_norm_w, m_w_in=m_w_in, m_q_norm_w=m_q_norm_w, m_k_norm_w=m_k_norm_w, m_rel_bias=m_rel_bias, m_sinks=m_sinks, m_conv_w=m_conv_w, m_conv_b=m_conv_b, m_dt_bias=m_dt_bias, m_a_log=m_a_log, m_d_skip=m_d_skip, m_ssm_norm_w=m_ssm_norm_w, m_w_attn_proj=m_w_attn_proj, m_w_ssm_proj=m_w_ssm_proj, m_w_out=m_w_out, v_w_ada=v_w_ada, v_b_ada=v_b_ada, v_norm_w=v_norm_w, v_w_in=v_w_in, v_q_norm_w=v_q_norm_w, v_k_norm_w=v_k_norm_w, v_rel_bias=v_rel_bias, v_sinks=v_sinks, v_conv_w=v_conv_w, v_conv_b=v_conv_b, v_dt_bias=v_dt_bias, v_a_log=v_a_log, v_d_skip=v_d_skip, v_ssm_norm_w=v_ssm_norm_w, v_w_attn_proj=v_w_attn_proj, v_w_ssm_proj=v_w_ssm_proj, v_w_out=v_w_out)
    weights = {n: given[n] for n in TWIN_WEIGHTS}
    shared = {n: given[n] for n in SHARED_INPUTS}
    per_example = {n: given[n] for n in ['x', 'c']}
    grad_fn = _jax.value_and_grad(_loss, argnums=(0, 1))

    def one_microbatch(ex, loss_target):
        ex = dict(ex)
        diff = ex.pop(TWIN_DIFF_INPUT)
        return grad_fn(weights, diff, {**shared, **ex}, loss_target)

    if N_MICROBATCH == 1:
        loss, (grad_w, grad_x) = one_microbatch(per_example, given["loss_target"])
    else:
        def body(carry, xs):
            loss_sum, grad_sum = carry
            l_k, (gw_k, gx_k) = one_microbatch(xs[0], xs[1])
            with _jax.named_scope("update"):
                return (loss_sum + l_k, _jax.tree.map(_jnp.add, grad_sum, gw_k)), gx_k

        init = (_jnp.zeros((), _jnp.float32), _jax.tree.map(_jnp.zeros_like, weights))
        (loss, grad_w), grad_x = _jax.lax.scan(body, init, (per_example, given["loss_target"]))
    with _jax.named_scope("update"):
        delta_w, new_m, new_v = {}, {}, {}
        for n in TWIN_WEIGHTS:
            delta_w[n], new_m[n], new_v[n] = _adamw(weights[n], grad_w[n], given["m_" + n], given["v_" + n])
    return (loss, grad_x, *[grad_w[n] for n in TWIN_WEIGHTS], *[delta_w[n] for n in TWIN_WEIGHTS],
            *[new_m[n] for n in TWIN_WEIGHTS], *[new_v[n] for n in TWIN_WEIGHTS])
```

```python
import functools
import math

import numpy as np
import jax
import jax.numpy as jnp
from jax import lax
from jax.experimental import pallas as pl
from jax.experimental.pallas import tpu as pltpu

F32 = jnp.float32
BF = jnp.bfloat16
HI = lax.Precision.HIGHEST

D_MODEL = 1024
ATTN_HEADS = 16
KV_HEADS = 4
GRP = ATTN_HEADS // KV_HEADS
HEAD_DIM = 64
ATTN_W = ATTN_HEADS * HEAD_DIM
KV_W = KV_HEADS * HEAD_DIM
BLOCK = 128
REL_BUCKETS = 32
REL_MAX_DIST = 128
SSM_W = 2048
SSM_P = 64
SSM_HEADS = 32
SSM_G = 4
SSM_R = 8
SSM_N = 128
CONV_K = 4
XBC_W = SSM_W + 2 * SSM_G * SSM_N
SEG_W = (ATTN_W, KV_W, KV_W, ATTN_W, SSM_W, XBC_W, SSM_HEADS, D_MODEL, D_MODEL)
SEG_OFF = tuple(int(v) for v in np.cumsum((0,) + SEG_W))
IN_W = SEG_OFF[-1]
EPS = 1e-6
N_DEV = 8
ADAM_LR, ADAM_B1, ADAM_B2, ADAM_EPS, ADAM_WD, ADAM_STEP = 0.001, 0.9, 0.999, 1e-08, 0.01, 10
VMEM_LIMIT = 60 * 1024 * 1024
MESH = pl.DeviceIdType.MESH
ANY = pl.BlockSpec(memory_space=pl.ANY)


def _dot(a, b, precision=None):
    return jnp.dot(a, b, preferred_element_type=F32, precision=precision)


def _dot_nt(a, b, precision=None):
    return lax.dot_general(a, b, (((1,), (1,)), ((), ())), preferred_element_type=F32, precision=precision)


def _dot_tn(a, b, precision=None):
    return lax.dot_general(a, b, (((0,), (0,)), ((), ())), preferred_element_type=F32, precision=precision)


def _bf(a):
    return a.astype(BF)


def _sig(a):
    return 1.0 / (1.0 + jnp.exp(-a))


def _params(**kw):
    return pltpu.CompilerParams(vmem_limit_bytes=VMEM_LIMIT, **kw)


def _full(shape):
    nd = len(shape)
    return pl.BlockSpec(shape, lambda i: (0,) * nd)


def _rows(tm, w):
    return pl.BlockSpec((tm, w), lambda i: (i, 0))


def _inproj(x, norm_w, scale, shift, ws, tm=256):
    s = x.shape[0]

    def body(x_ref, nw_ref, sc_ref, sh_ref, *rest):
        w_hbm, outs, h_ref, w_vm, sem = rest[:9], rest[9:18], rest[18], rest[19:28], rest[28]

        @pl.when(pl.program_id(0) == 0)
        def _():
            cps = [pltpu.make_async_copy(w_hbm[j], w_vm[j], sem.at[j]) for j in range(9)]
            for cp in cps:
                cp.start()
            for cp in cps:
                cp.wait()

        xv = x_ref[...]
        r = lax.rsqrt(jnp.mean(xv * xv, axis=-1, keepdims=True) + EPS)
        h = xv * r * (nw_ref[...] * (1.0 + sc_ref[...])) + sh_ref[...]
        hb = _bf(h)
        h_ref[...] = hb
        for j in range(9):
            outs[j][...] = _dot(hb, w_vm[j][...])

    vec = _full((1, D_MODEL))
    return pl.pallas_call(
        body, name="inproj", grid=(s // tm,),
        in_specs=[_rows(tm, D_MODEL), vec, vec, vec] + [ANY] * 9,
        out_specs=[_rows(tm, w) for w in SEG_W] + [_rows(tm, D_MODEL)],
        out_shape=[jax.ShapeDtypeStruct((s, w), F32) for w in SEG_W] + [jax.ShapeDtypeStruct((s, D_MODEL), BF)],
        scratch_shapes=[pltpu.VMEM((D_MODEL, w), BF) for w in SEG_W] + [pltpu.SemaphoreType.DMA((9,))],
        compiler_params=_params(dimension_semantics=("arbitrary",)),
    )(x, norm_w, scale, shift, *ws)


def _bucket_onehot_t():
    qi = jnp.arange(BLOCK)[:, None]
    kj = jnp.arange(2 * BLOCK)[None, :]
    dist = qi + BLOCK - kj
    n = jnp.maximum(dist, 0)
    max_exact = REL_BUCKETS // 2
    nf = jnp.maximum(n, 1).astype(F32)
    large = max_exact + (jnp.log(nf / max_exact) / math.log(REL_MAX_DIST / max_exact)
                         * (REL_BUCKETS - max_exact)).astype(jnp.int32)
    large = jnp.minimum(large, REL_BUCKETS - 1)
    bucket = jnp.where(n < max_exact, n, large).reshape(1, BLOCK * 2 * BLOCK)
    return (bucket == jnp.arange(REL_BUCKETS)[:, None]).astype(F32)


def _bias_dense(rel_bias_t, oh_t):
    def body(rb_ref, oh_ref, o_ref):
        o_ref[...] = _dot(rb_ref[...], oh_ref[...], HI)

    return pl.pallas_call(
        body, name="bias_dense", out_shape=jax.ShapeDtypeStruct((ATTN_HEADS, BLOCK * 2 * BLOCK), F32),
        compiler_params=_params(),
    )(rel_bias_t, oh_t)


def _bias_grad(ds_sum, oh_t):
    def body(ds_ref, oh_ref, o_ref):
        o_ref[...] = _dot_nt(ds_ref[...], oh_ref[...], HI)

    return pl.pallas_call(
        body, name="bias_grad", out_shape=jax.ShapeDtypeStruct((ATTN_HEADS, REL_BUCKETS), F32),
        compiler_params=_params(),
    )(ds_sum, oh_t)


def _head_norm(t, w):
    r = lax.rsqrt(jnp.mean(t * t, axis=-1, keepdims=True) + EPS)
    return t * r * w, r


def _window_mask(first):
    qi = lax.broadcasted_iota(jnp.int32, (BLOCK, 2 * BLOCK), 0)
    kj = lax.broadcasted_iota(jnp.int32, (BLOCK, 2 * BLOCK), 1)
    prev_ok = jnp.logical_and(kj > qi, jnp.logical_not(first))
    cur_ok = jnp.logical_and(kj >= BLOCK, kj - BLOCK <= qi)
    return jnp.logical_or(jnp.logical_and(kj < BLOCK, prev_ok), cur_ok)


def _attn_fwd(q, k, v, bias, sinks, qnw, knw):
    s = q.shape[0]
    nb = s // BLOCK

    def body(q_ref, kp_ref, kc_ref, vp_ref, vc_ref, b_ref, sk_ref, qw_ref, kw_ref, o_ref, lse_ref):
        i = pl.program_id(0)
        mask = _window_mask(i == 0)
        qw, kw = qw_ref[...], kw_ref[...]
        lses = []
        for hk in range(KV_HEADS):
            ks = slice(hk * HEAD_DIM, (hk + 1) * HEAD_DIM)
            kk = jnp.concatenate([kp_ref[:, ks], kc_ref[:, ks]], axis=0)
            vv = _bf(jnp.concatenate([vp_ref[:, ks], vc_ref[:, ks]], axis=0))
            kn = _bf(_head_norm(kk, kw)[0])
            for g in range(GRP):
                h = hk * GRP + g
                hs = slice(h * HEAD_DIM, (h + 1) * HEAD_DIM)
                qn = _bf(_head_norm(q_ref[:, hs], qw)[0])
                sc = _dot_nt(qn, kn) * (HEAD_DIM ** -0.5) + b_ref[h]
                sc = jnp.where(mask, sc, -1e30)
                sink = sk_ref[0, h]
                m = jnp.maximum(jnp.max(sc, axis=-1, keepdims=True), sink)
                p = jnp.exp(sc - m)
                den = jnp.sum(p, axis=-1, keepdims=True) + jnp.exp(sink - m)
                o_ref[:, hs] = _dot(_bf(p), vv) / den
                lses.append(m + jnp.log(den))
        lse_ref[...] = jnp.concatenate(lses, axis=1)

    cur = lambda w: pl.BlockSpec((BLOCK, w), lambda i: (i, 0))
    prev = lambda w: pl.BlockSpec((BLOCK, w), lambda i: (jnp.maximum(i - 1, 0), 0))
    return pl.pallas_call(
        body, name="attn_fwd", grid=(nb,),
        in_specs=[cur(ATTN_W), prev(KV_W), cur(KV_W), prev(KV_W), cur(KV_W),
                  pl.BlockSpec((ATTN_HEADS, BLOCK, 2 * BLOCK), lambda i: (0, 0, 0)),
                  pl.BlockSpec(memory_space=pltpu.SMEM), _full((1, HEAD_DIM)), _full((1, HEAD_DIM))],
        out_specs=[cur(ATTN_W), cur(ATTN_HEADS)],
        out_shape=[jax.ShapeDtypeStruct((s, ATTN_W), F32), jax.ShapeDtypeStruct((s, ATTN_HEADS), F32)],
        compiler_params=_params(dimension_semantics=("arbitrary",)),
    )(q, k, k, v, v, bias, sinks, qnw, knw)


def _conv_taps(ext_ref, xbc, tail):
    ext_ref[0:8, :] = tail
    ext_ref[8:8 + BLOCK, :] = xbc
    return [ext_ref[5 + j:5 + j + BLOCK, :] for j in range(CONV_K)]


def _softplus(u):
    return jnp.maximum(u, 0.0) + jnp.log(1.0 + jnp.exp(-jnp.abs(u)))


def _tril():
    r = lax.broadcasted_iota(jnp.int32, (BLOCK, BLOCK), 0)
    c = lax.broadcasted_iota(jnp.int32, (BLOCK, BLOCK), 1)
    return r >= c


def _ssd_common(taps, cw_ref, cb_ref, dtr_ref, dtb_ref, alog_ref, e_ref):
    conv = cb_ref[...] + sum(taps[j] * cw_ref[j:j + 1, :] for j in range(CONV_K))
    sg = _sig(conv)
    xact = conv * sg
    u = dtr_ref[...] + dtb_ref[...]
    dt = _softplus(u)
    a = -jnp.exp(alog_ref[...])
    trilb = _tril()
    acum = _dot(trilb.astype(F32), dt * a, HI)
    both = _dot(jnp.concatenate([dt, acum], axis=0), e_ref[...], HI)
    dt_x, acum_x = both[:BLOCK], both[BLOCK:]
    return conv, sg, xact, u, dt, a, trilb, acum, dt_x, acum_x


def _ssd_fwd(xbc, dt_raw, conv_w, conv_b, dt_bias, a_log, dsk_x, e_mat):
    s = xbc.shape[0]
    nc = s // BLOCK

    def body(x_ref, tail_ref, dtr_ref, cw_ref, cb_ref, dtb_ref, alog_ref, dsk_ref, e_ref,
             y_ref, hp_ref, hst, ext):
        i = pl.program_id(0)

        @pl.when(i == 0)
        def _():
            hst[...] = jnp.zeros_like(hst)

        tail = jnp.where(i > 0, tail_ref[...], 0.0)
        taps = _conv_taps(ext, x_ref[...], tail)
        _, _, xact, _, _, _, trilb, acum, dt_x, acum_x = _ssd_common(
            taps, cw_ref, cb_ref, dtr_ref, dtb_ref, alog_ref, e_ref)
        xs = xact[:, :SSM_W]
        acum_t = acum.T
        ea_x = jnp.exp(acum_x)
        last_x = acum_x[BLOCK - 1:BLOCK, :]
        xdt = xs * dt_x
        xw = xdt * jnp.exp(last_x - acum_x)
        cd_x = jnp.exp(last_x)
        hprev = hst[...]
        hp_ref[0] = hprev
        dsk = dsk_ref[...]
        for g in range(SSM_G):
            bg = _bf(xact[:, SSM_W + g * SSM_N:SSM_W + (g + 1) * SSM_N])
            cg = _bf(xact[:, SSM_W + SSM_G * SSM_N + g * SSM_N:SSM_W + SSM_G * SSM_N + (g + 1) * SSM_N])
            sl = slice(g * SSM_R * SSM_P, (g + 1) * SSM_R * SSM_P)
            cb = _dot_nt(cg, bg)
            yoff = _dot(cg, _bf(hprev[:, sl])) * ea_x[:, sl]
            hst[:, sl] = hprev[:, sl] * cd_x[:, sl] + _dot_tn(bg, _bf(xw[:, sl]))
            for r in range(SSM_R):
                hh = g * SSM_R + r
                hs = slice(hh * SSM_P, (hh + 1) * SSM_P)
                seg = jnp.where(trilb, acum[:, hh:hh + 1] - acum_t[hh:hh + 1, :], -1e30)
                mm = cb * jnp.exp(seg)
                yd = _dot(_bf(mm), _bf(xdt[:, hs]))
                y_ref[:, hs] = yd + yoff[:, r * SSM_P:(r + 1) * SSM_P] + dsk[:, hs] * xs[:, hs]

    chunk = lambda w: pl.BlockSpec((BLOCK, w), lambda i: (i, 0))
    return pl.pallas_call(
        body, name="ssd_fwd", grid=(nc,),
        in_specs=[chunk(XBC_W), pl.BlockSpec((8, XBC_W), lambda i: (jnp.maximum(i * (BLOCK // 8) - 1, 0), 0)),
                  chunk(SSM_HEADS), _full((CONV_K, XBC_W)), _full((1, XBC_W)), _full((1, SSM_HEADS)),
                  _full((1, SSM_HEADS)), _full((1, SSM_W)), _full((SSM_HEADS, SSM_W))],
        out_specs=[chunk(SSM_W), pl.BlockSpec((1, SSM_N, SSM_W), lambda i: (i, 0, 0))],
        out_shape=[jax.ShapeDtypeStruct((s, SSM_W), F32), jax.ShapeDtypeStruct((nc, SSM_N, SSM_W), F32)],
        scratch_shapes=[pltpu.VMEM((SSM_N, SSM_W), F32), pltpu.VMEM((BLOCK + 8, XBC_W), F32)],
        compiler_params=_params(dimension_semantics=("arbitrary",)),
    )(xbc, xbc, dt_raw, conv_w, conv_b, dt_bias, a_log, dsk_x, e_mat)


def _dsilu(z, sg):
    return sg * (1.0 + z * (1.0 - sg))


def _mid(x, tgt, o_att, za, ypre, zm, ga, gb, gate, ssm_nw, wap, wsp, wout, tm=128):
    s = x.shape[0]
    gw = SSM_W // SSM_G

    def body(x_ref, t_ref, o_ref, za_ref, yp_ref, zm_ref, ga_ref, gb_ref, gate_ref, nw_ref, wap_h, wsp_h, wout_h,
             dout_ref, do_ref, dza_ref, dyp_ref, dzm_ref, dga_ref, dgb_ref, gwout_h, gwap_h, gwsp_h,
             gnw_ref, dgate_ref, loss_ref, wap_v, wsp_v, wout_v, a_out, a_ap, a_sp, sem):
        i = pl.program_id(0)

        @pl.when(i == 0)
        def _():
            cps = [pltpu.make_async_copy(a, b, sem.at[j])
                   for j, (a, b) in enumerate(((wap_h, wap_v), (wsp_h, wsp_v), (wout_h, wout_v)))]
            for cp in cps:
                cp.start()
            a_out[...] = jnp.zeros_like(a_out)
            a_ap[...] = jnp.zeros_like(a_ap)
            a_sp[...] = jnp.zeros_like(a_sp)
            gnw_ref[...] = jnp.zeros_like(gnw_ref)
            dgate_ref[...] = jnp.zeros_like(dgate_ref)
            loss_ref[...] = jnp.zeros_like(loss_ref)
            for cp in cps:
                cp.wait()

        gate = gate_ref[...]
        nw = nw_ref[...]
        o_att = o_ref[...]
        z_a = za_ref[...]
        s_a = _sig(z_a)
        silu_a = z_a * s_a
        yag = _bf(o_att * silu_a)
        y_a = _dot(yag, wap_v[...])
        ypre = yp_ref[...]
        z_m = zm_ref[...]
        s_m = _sig(z_m)
        silu_m = z_m * s_m
        yg = ypre * silu_m
        rinv = jnp.concatenate(
            [jnp.broadcast_to(lax.rsqrt(jnp.mean(yg[:, g * gw:(g + 1) * gw] ** 2, axis=-1, keepdims=True) + EPS), (tm, gw))
             for g in range(SSM_G)], axis=1)
        ynr = yg * rinv
        yn = _bf(ynr * nw)
        y_b = _dot(yn, wsp_v[...])
        g_a = _sig(ga_ref[...])
        g_b = _sig(gb_ref[...])
        merged = _bf(g_a * y_a + g_b * y_b)
        o = _dot(merged, wout_v[...])
        diff = x_ref[...] + gate * o - t_ref[...]
        loss_ref[...] += (0.5 / D_MODEL) * jnp.sum(diff * diff, axis=(0, 1), keepdims=True)
        dout = diff * (1.0 / D_MODEL)
        dout_ref[...] = dout
        dgate_ref[...] += jnp.sum(dout * o, axis=0, keepdims=True)
        d_o = _bf(dout * gate)
        dmerged = _dot_nt(d_o, wout_v[...])
        a_out[...] += _dot_tn(merged, d_o)
        dy_a = dmerged * g_a
        dy_b = dmerged * g_b
        dga_ref[...] = _bf(dy_a * y_a * (1.0 - g_a))
        dgb_ref[...] = _bf(dy_b * y_b * (1.0 - g_b))
        dy_a = _bf(dy_a)
        dy_b = _bf(dy_b)
        a_ap[...] += _dot_tn(yag, dy_a)
        dyag = _dot_nt(dy_a, wap_v[...])
        do_ref[...] = dyag * silu_a
        dza_ref[...] = _bf(dyag * o_att * _dsilu(z_a, s_a))
        a_sp[...] += _dot_tn(yn, dy_b)
        dyn = _dot_nt(dy_b, wsp_v[...])
        gnw_ref[...] += jnp.sum(dyn * ynr, axis=0, keepdims=True)
        dynw = dyn * nw
        corr = jnp.concatenate(
            [jnp.broadcast_to(jnp.mean((dynw * ynr)[:, g * gw:(g + 1) * gw], axis=-1, keepdims=True), (tm, gw))
             for g in range(SSM_G)], axis=1)
        dyg = rinv * (dynw - ynr * corr)
        dyp_ref[...] = dyg * silu_m
        dzm_ref[...] = _bf(dyg * ypre * _dsilu(z_m, s_m))

        @pl.when(i == pl.num_programs(0) - 1)
        def _():
            cps = [pltpu.make_async_copy(a, b, sem.at[j])
                   for j, (a, b) in enumerate(((a_out, gwout_h), (a_ap, gwap_h), (a_sp, gwsp_h)))]
            for cp in cps:
                cp.start()
            for cp in cps:
                cp.wait()

    r1, r2 = _rows(tm, D_MODEL), _rows(tm, SSM_W)
    sd = jax.ShapeDtypeStruct
    return pl.pallas_call(
        body, name="mid", grid=(s // tm,),
        in_specs=[r1, r1, r1, r1, r2, r2, r1, r1, _full((1, D_MODEL)), _full((1, SSM_W)), ANY, ANY, ANY],
        out_specs=[r1, r1, r1, r2, r2, r1, r1, ANY, ANY, ANY, _full((1, SSM_W)), _full((1, D_MODEL)), _full((1, 1))],
        out_shape=[sd((s, D_MODEL), F32), sd((s, ATTN_W), F32), sd((s, ATTN_W), BF), sd((s, SSM_W), F32),
                   sd((s, SSM_W), BF), sd((s, D_MODEL), BF), sd((s, D_MODEL), BF),
                   sd((D_MODEL, D_MODEL), F32), sd((ATTN_W, D_MODEL), F32), sd((SSM_W, D_MODEL), F32),
                   sd((1, SSM_W), F32), sd((1, D_MODEL), F32), sd((1, 1), F32)],
        scratch_shapes=[pltpu.VMEM((ATTN_W, D_MODEL), BF), pltpu.VMEM((SSM_W, D_MODEL), BF), pltpu.VMEM((D_MODEL, D_MODEL), BF),
                        pltpu.VMEM((D_MODEL, D_MODEL), F32), pltpu.VMEM((ATTN_W, D_MODEL), F32),
                        pltpu.VMEM((SSM_W, D_MODEL), F32), pltpu.SemaphoreType.DMA((3,))],
        compiler_params=_params(dimension_semantics=("arbitrary",)),
    )(x, tgt, o_att, za, ypre, zm, ga, gb, gate, ssm_nw, wap, wsp, wout)


def _norm_bwd(t, r, w, d):
    wd = d * w
    dt = r * wd - t * (r * r * r) * jnp.mean(t * wd, axis=-1, keepdims=True)
    return dt, jnp.sum(d * t * r, axis=0, keepdims=True)


def _attn_bwd(q, k, v, bias, sinks, qnw, knw, o_att, lse, d_o):
    s = q.shape[0]
    nb = s // BLOCK

    def body(q_ref, kp_ref, kc_ref, vp_ref, vc_ref, b_ref, sk_ref, qw_ref, kw_ref, o_ref, lse_ref, do_ref,
             dq_ref, dk_ref, dv_ref, dss_ref, gqw_ref, gkw_ref, gsk_ref, ckn, cv):
        i = pl.program_id(0)
        qw, kw = qw_ref[...], kw_ref[...]

        @pl.when(i == 0)
        def _():
            ckn[...] = jnp.zeros_like(ckn)
            cv[...] = jnp.zeros_like(cv)
            dss_ref[...] = jnp.zeros_like(dss_ref)
            gqw_ref[...] = jnp.zeros_like(gqw_ref)
            gkw_ref[...] = jnp.zeros_like(gkw_ref)
            gsk_ref[...] = jnp.zeros_like(gsk_ref)

        @pl.when(i < nb)
        def _():
            mask = _window_mask(i == 0)
            gsk = []
            gqw = jnp.zeros((1, HEAD_DIM), F32)
            gkw = jnp.zeros((1, HEAD_DIM), F32)
            for hk in range(KV_HEADS):
                ks = slice(hk * HEAD_DIM, (hk + 1) * HEAD_DIM)
                kprev = kp_ref[:, ks]
                kk = jnp.concatenate([kprev, kc_ref[:, ks]], axis=0)
                vv = _bf(jnp.concatenate([vp_ref[:, ks], vc_ref[:, ks]], axis=0))
                knf, rk = _head_norm(kk, kw)
                kn = _bf(knf)
                dkn = jnp.zeros((2 * BLOCK, HEAD_DIM), F32)
                dvv = jnp.zeros((2 * BLOCK, HEAD_DIM), F32)
                for g in range(GRP):
                    h = hk * GRP + g
                    hs = slice(h * HEAD_DIM, (h + 1) * HEAD_DIM)
                    qh = q_ref[:, hs]
                    qnf, rq = _head_norm(qh, qw)
                    qn = _bf(qnf)
                    sc = _dot_nt(qn, kn) * (HEAD_DIM ** -0.5) + b_ref[h]
                    lse_h = lse_ref[:, h:h + 1]
                    p = jnp.where(mask, jnp.exp(sc - lse_h), 0.0)
                    d_oh = do_ref[:, hs]
                    delta = jnp.sum(d_oh * o_ref[:, hs], axis=-1, keepdims=True)
                    d_ob = _bf(d_oh)
                    dp = _dot_nt(d_ob, vv)
                    ds = p * (dp - delta)
                    gsk.append(jnp.sum(-jnp.exp(sk_ref[0, h] - lse_h) * delta, axis=(0, 1), keepdims=True))
                    dss_ref[h] += ds
                    dsb = _bf(ds)
                    dvv = dvv + _dot_tn(_bf(p), d_ob)
                    dkn = dkn + _dot_tn(dsb, qn) * (HEAD_DIM ** -0.5)
                    dqn = _dot(dsb, kn) * (HEAD_DIM ** -0.5)
                    dq, gq = _norm_bwd(qh, rq, qw, dqn)
                    dq_ref[:, hs] = _bf(dq)
                    gqw = gqw + gq
                dk, gk = _norm_bwd(kprev, rk[:BLOCK], kw, ckn[:, ks] + dkn[:BLOCK])
                dk_ref[:, ks] = _bf(dk)
                gkw = gkw + gk
                dv_ref[:, ks] = _bf(cv[:, ks] + dvv[:BLOCK])
                ckn[:, ks] = dkn[BLOCK:]
                cv[:, ks] = dvv[BLOCK:]
            gsk_ref[...] += jnp.concatenate(gsk, axis=1)
            gqw_ref[...] += gqw
            gkw_ref[...] += gkw

        @pl.when(i == nb)
        def _():
            gkw = jnp.zeros((1, HEAD_DIM), F32)
            for hk in range(KV_HEADS):
                ks = slice(hk * HEAD_DIM, (hk + 1) * HEAD_DIM)
                kc = kc_ref[:, ks]
                dk, gk = _norm_bwd(kc, _head_norm(kc, kw)[1], kw, ckn[:, ks])
                dk_ref[:, ks] = _bf(dk)
                gkw = gkw + gk
            dv_ref[...] = _bf(cv[...])
            gkw_ref[...] += gkw

    last = nb - 1
    cur = lambda w: pl.BlockSpec((BLOCK, w), lambda i: (jnp.minimum(i, last), 0))
    prev = lambda w: pl.BlockSpec((BLOCK, w), lambda i: (jnp.maximum(jnp.minimum(i, last) - 1, 0), 0))
    late = lambda w: pl.BlockSpec((BLOCK, w), lambda i: (jnp.maximum(i - 1, 0), 0))
    sd = jax.ShapeDtypeStruct
    return pl.pallas_call(
        body, name="attn_bwd", grid=(nb + 1,),
        in_specs=[cur(ATTN_W), prev(KV_W), cur(KV_W), prev(KV_W), cur(KV_W),
                  pl.BlockSpec((ATTN_HEADS, BLOCK, 2 * BLOCK), lambda i: (0, 0, 0)),
                  pl.BlockSpec(memory_space=pltpu.SMEM), _full((1, HEAD_DIM)), _full((1, HEAD_DIM)),
                  cur(ATTN_W), cur(ATTN_HEADS), cur(ATTN_W)],
        out_specs=[cur(ATTN_W), late(KV_W), late(KV_W),
                   pl.BlockSpec((ATTN_HEADS, BLOCK, 2 * BLOCK), lambda i: (0, 0, 0)),
                   _full((1, HEAD_DIM)), _full((1, HEAD_DIM)), _full((1, ATTN_HEADS))],
        out_shape=[sd((s, ATTN_W), BF), sd((s, KV_W), BF), sd((s, KV_W), BF),
                   sd((ATTN_HEADS, BLOCK, 2 * BLOCK), F32), sd((1, HEAD_DIM), F32), sd((1, HEAD_DIM), F32),
                   sd((1, ATTN_HEADS), F32)],
        scratch_shapes=[pltpu.VMEM((BLOCK, KV_W), F32), pltpu.VMEM((BLOCK, KV_W), F32)],
        compiler_params=_params(dimension_semantics=("arbitrary",)),
    )(q, k, k, v, v, bias, sinks, qnw, knw, o_att, lse, d_o)


def _ssd_bwd(xbc, dt_raw, conv_w, conv_b, dt_bias, a_log, dsk_x, e_mat, hprev_all, dy_all):
    s = xbc.shape[0]
    nc = s // BLOCK
    gw = SSM_R * SSM_P
    b0, c0 = SSM_W, SSM_W + SSM_G * SSM_N

    def body(x_ref, tail_ref, dtr_ref, cw_ref, cb_ref, dtb_ref, alog_ref, dsk_ref, e_ref, hp_ref, dy_ref,
             dx_ref, ddt_ref, gcw_ref, gcb_ref, gdtb_ref, galog_ref, gdsk_ref,
             dh, nhead, ext, ext2, gdskx, dxdt_s, dbc_s):
        i = pl.program_id(0)
        c = nc - 1 - i

        @pl.when(i == 0)
        def _():
            for ref in (dh, nhead, gdskx, gcw_ref, gcb_ref, gdtb_ref, galog_ref, gdsk_ref):
                ref[...] = jnp.zeros_like(ref)

        tail = jnp.where(c > 0, tail_ref[...], 0.0)
        taps = _conv_taps(ext, x_ref[...], tail)
        conv, sg, xact, u, dt, a, trilb, acum, dt_x, acum_x = _ssd_common(
            taps, cw_ref, cb_ref, dtr_ref, dtb_ref, alog_ref, e_ref)
        xs = xact[:, :SSM_W]
        acum_t = acum.T
        ea_x = jnp.exp(acum_x)
        last_x = acum_x[BLOCK - 1:BLOCK, :]
        dte_x = jnp.exp(last_x - acum_x)
        cd_x = jnp.exp(last_x)
        xdt = xs * dt_x
        xw = xdt * dte_x
        hprev = hp_ref[0]
        dhn = dh[...]
        dy = dy_ref[...]
        gdskx[...] += jnp.sum(dy * xs, axis=0, keepdims=True)
        dyea = dy * ea_x
        lane = lax.broadcasted_iota(jnp.int32, (BLOCK, SSM_HEADS), 1)
        dacum = jnp.zeros((BLOCK, SSM_HEADS), F32)
        dacc_x, dlast_x = [], []
        for g in range(SSM_G):
            bgf = xact[:, b0 + g * SSM_N:b0 + (g + 1) * SSM_N]
            cgf = xact[:, c0 + g * SSM_N:c0 + (g + 1) * SSM_N]
            bg, cg = _bf(bgf), _bf(cgf)
            sl = slice(g * gw, (g + 1) * gw)
            hpg, dhg, dyeag = _bf(hprev[:, sl]), _bf(dhn[:, sl]), _bf(dyea[:, sl])
            cb = _dot_nt(cg, bg)
            gmat = _dot(cg, hpg)
            dxw = _dot(bg, dhg)
            dxdt_s[:, sl] = dxw * dte_x[:, sl]
            dacc_x.append(dy[:, sl] * gmat * ea_x[:, sl] - dxw * xw[:, sl])
            dlast_x.append(jnp.sum(dxw * xw[:, sl], axis=0, keepdims=True)
                           + jnp.sum(dhn[:, sl] * hprev[:, sl], axis=0, keepdims=True) * cd_x[:, sl])
            dcg = _dot_nt(dyeag, hpg)
            dbg = _dot_nt(_bf(xw[:, sl]), dhg)
            dh[:, sl] = dhn[:, sl] * cd_x[:, sl] + _dot_tn(cg, dyeag)
            dcb = jnp.zeros((BLOCK, BLOCK), F32)
            for r in range(SSM_R):
                hh = g * SSM_R + r
                hs = slice(hh * SSM_P, (hh + 1) * SSM_P)
                seg = jnp.where(trilb, acum[:, hh:hh + 1] - acum_t[hh:hh + 1, :], -1e30)
                lm = jnp.exp(seg)
                mm = cb * lm
                dyh = _bf(dy[:, hs])
                dm = _dot_nt(dyh, _bf(xdt[:, hs]))
                dxdt_s[:, hs] += _dot_tn(_bf(mm), dyh)
                wm = dm * mm
                dcb = dcb + dm * lm
                dacum = dacum + _dot(wm - wm.T, (lane == hh).astype(F32), HI)
            dcbb = _bf(dcb)
            dbc_s[:, g * SSM_N:(g + 1) * SSM_N] = dbg + _dot_tn(dcbb, cg)
            dbc_s[:, SSM_G * SSM_N + g * SSM_N:SSM_G * SSM_N + (g + 1) * SSM_N] = dcg + _dot(dcbb, bg)
        dxdt = dxdt_s[...]
        dxs = dy * dsk_ref[...] + dxdt * dt_x
        red = _dot_nt(jnp.concatenate(
            [dxdt * xs, jnp.concatenate(dacc_x, axis=1),
             jnp.broadcast_to(jnp.concatenate(dlast_x, axis=1), (8, SSM_W))], axis=0), e_ref[...], HI)
        row = lax.broadcasted_iota(jnp.int32, (BLOCK, SSM_HEADS), 0)
        dacum = dacum + red[BLOCK:2 * BLOCK] + jnp.where(row == BLOCK - 1, red[2 * BLOCK:2 * BLOCK + 1], 0.0)
        ddta = _dot_tn(trilb.astype(F32), dacum, HI)
        ddt = red[:BLOCK] + ddta * a
        galog_ref[...] += jnp.sum(ddta * dt, axis=0, keepdims=True) * a
        du = ddt * _sig(u)
        ddt_ref[...] = _bf(du)
        gdtb_ref[...] += jnp.sum(du, axis=0, keepdims=True)
        dconv = jnp.concatenate([dxs, dbc_s[...]], axis=1) * _dsilu(conv, sg)
        gcb_ref[...] += jnp.sum(dconv, axis=0, keepdims=True)
        gcw_ref[...] += jnp.concatenate([jnp.sum(dconv * taps[j], axis=0, keepdims=True) for j in range(CONV_K)], axis=0)
        ext2[0:BLOCK, :] = dconv
        ext2[BLOCK:BLOCK + 8, :] = nhead[...]
        dx_ref[...] = _bf(sum(ext2[3 - j:3 - j + BLOCK, :] * cw_ref[j:j + 1, :] for j in range(CONV_K)))
        nhead[...] = dconv[0:8]

        @pl.when(i == nc - 1)
        def _():
            gdsk_ref[...] = _dot_nt(jnp.broadcast_to(gdskx[...], (8, SSM_W)), e_ref[...], HI)[0:1]

    chunk = lambda w: pl.BlockSpec((BLOCK, w), lambda i: (nc - 1 - i, 0))
    sd = jax.ShapeDtypeStruct
    return pl.pallas_call(
        body, name="ssd_bwd", grid=(nc,),
        in_specs=[chunk(XBC_W), pl.BlockSpec((8, XBC_W), lambda i: (jnp.maximum((nc - 1 - i) * (BLOCK // 8) - 1, 0), 0)),
                  chunk(SSM_HEADS), _full((CONV_K, XBC_W)), _full((1, XBC_W)), _full((1, SSM_HEADS)),
                  _full((1, SSM_HEADS)), _full((1, SSM_W)), _full((SSM_HEADS, SSM_W)),
                  pl.BlockSpec((1, SSM_N, SSM_W), lambda i: (nc - 1 - i, 0, 0)), chunk(SSM_W)],
        out_specs=[chunk(XBC_W), chunk(SSM_HEADS), _full((CONV_K, XBC_W)), _full((1, XBC_W)),
                   _full((1, SSM_HEADS)), _full((1, SSM_HEADS)), _full((1, SSM_HEADS))],
        out_shape=[sd((s, XBC_W), BF), sd((s, SSM_HEADS), BF), sd((CONV_K, XBC_W), F32), sd((1, XBC_W), F32),
                   sd((1, SSM_HEADS), F32), sd((1, SSM_HEADS), F32), sd((1, SSM_HEADS), F32)],
        scratch_shapes=[pltpu.VMEM((SSM_N, SSM_W), F32), pltpu.VMEM((8, XBC_W), F32),
                        pltpu.VMEM((BLOCK + 8, XBC_W), F32), pltpu.VMEM((BLOCK + 8, XBC_W), F32),
                        pltpu.VMEM((1, SSM_W), F32), pltpu.VMEM((BLOCK, SSM_W), F32),
                        pltpu.VMEM((BLOCK, 2 * SSM_G * SSM_N), F32)],
        compiler_params=_params(dimension_semantics=("arbitrary",)),
    )(xbc, xbc, dt_raw, conv_w, conv_b, dt_bias, a_log, dsk_x, e_mat, hprev_all, dy_all)


def _dh(x, dout, norm_w, scale, dsegs, ws, tm=256):
    s = x.shape[0]

    def body(x_ref, dout_ref, nw_ref, sc_ref, *rest):
        d_refs, w_hbm = rest[:9], rest[9:18]
        gx_ref, dshift_ref, dscale_ref, gnw_ref = rest[18:22]
        w_vm, sem = rest[22:31], rest[31]

        @pl.when(pl.program_id(0) == 0)
        def _():
            cps = [pltpu.make_async_copy(w_hbm[j], w_vm[j], sem.at[j]) for j in range(9)]
            for cp in cps:
                cp.start()
            for ref in (dshift_ref, dscale_ref, gnw_ref):
                ref[...] = jnp.zeros_like(ref)
            for cp in cps:
                cp.wait()

        dh = _dot_nt(d_refs[0][...], w_vm[0][...])
        for j in range(1, 9):
            dh = dh + _dot_nt(d_refs[j][...], w_vm[j][...])
        xv = x_ref[...]
        r = lax.rsqrt(jnp.mean(xv * xv, axis=-1, keepdims=True) + EPS)
        xn = xv * r
        nw = nw_ref[...]
        sc1 = 1.0 + sc_ref[...]
        dshift_ref[...] += jnp.sum(dh, axis=0, keepdims=True)
        dhxn = jnp.sum(dh * xn, axis=0, keepdims=True)
        dscale_ref[...] += dhxn * nw
        gnw_ref[...] += dhxn * sc1
        dxn = dh * (nw * sc1)
        gx_ref[...] = dout_ref[...] + r * (dxn - xn * jnp.mean(xn * dxn, axis=-1, keepdims=True))

    vec = _full((1, D_MODEL))
    sd = jax.ShapeDtypeStruct
    return pl.pallas_call(
        body, name="dh", grid=(s // tm,),
        in_specs=[_rows(tm, D_MODEL), _rows(tm, D_MODEL), vec, vec] + [_rows(tm, w) for w in SEG_W] + [ANY] * 9,
        out_specs=[_rows(tm, D_MODEL), vec, vec, vec],
        out_shape=[sd((s, D_MODEL), F32), sd((1, D_MODEL), F32), sd((1, D_MODEL), F32), sd((1, D_MODEL), F32)],
        scratch_shapes=[pltpu.VMEM((D_MODEL, w), BF) for w in SEG_W] + [pltpu.SemaphoreType.DMA((9,))],
        compiler_params=_params(dimension_semantics=("arbitrary",)),
    )(x, dout, norm_w, scale, *dsegs, *ws)


def _gw_seg(h_t, dseg, name, tm=512):
    s, w = dseg.shape
    tn = min(w, 1024)
    tm = min(tm, s)
    nm = s // tm

    def body(h_ref, d_ref, o_ref):
        @pl.when(pl.program_id(1) == 0)
        def _():
            o_ref[...] = jnp.zeros_like(o_ref)

        o_ref[...] += _dot(h_ref[...], d_ref[...])

    return pl.pallas_call(
        body, name=name, grid=(w // tn, nm),
        in_specs=[pl.BlockSpec((D_MODEL, tm), lambda n, m: (0, m)), pl.BlockSpec((tm, tn), lambda n, m: (m, n))],
        out_specs=pl.BlockSpec((D_MODEL, tn), lambda n, m: (0, n)),
        out_shape=jax.ShapeDtypeStruct((D_MODEL, w), F32),
        compiler_params=_params(dimension_semantics=("arbitrary", "arbitrary")),
    )(h_t, dseg)


def _gw_in(h, dsegs):
    h_t = h.T
    return [_gw_seg(h_t, d, "gw_in_%d" % j) for j, d in enumerate(dsegs)]


def _local_step(x, tgt, shift, scale, gate, ws, wap, wsp, wout, norm_w, qnw, knw, rel_bias, sinks,
                conv_w, conv_b, dt_bias, a_log, d_skip, ssm_nw):
    oh_t = _bucket_onehot_t()
    bias = _bias_dense(rel_bias.T, oh_t).reshape(ATTN_HEADS, BLOCK, 2 * BLOCK)
    *segs, h = _inproj(x, norm_w, scale, shift, ws)
    q, k, v, za, zm, xbc, dtr, ga, gb = segs
    o_att, lse = _attn_fwd(q, k, v, bias, sinks, qnw, knw)
    e_mat = jnp.repeat(jnp.eye(SSM_HEADS, dtype=F32), SSM_P, axis=1)
    dsk_x = jnp.repeat(d_skip, SSM_P, axis=1)
    ypre, hprev = _ssd_fwd(xbc, dtr, conv_w, conv_b, dt_bias, a_log, dsk_x, e_mat)
    (dout, d_o, dza, dyp, dzm, dga, dgb, g_wout, g_wap, g_wsp, g_ssm_nw, dgate, loss) = _mid(
        x, tgt, o_att, za, ypre, zm, ga, gb, gate, ssm_nw, wap, wsp, wout)
    dq, dk, dv, dss, g_qnw, g_knw, g_sinks = _attn_bwd(q, k, v, bias, sinks, qnw, knw, o_att, lse, d_o)
    g_rel = _bias_grad(dss.reshape(ATTN_HEADS, BLOCK * 2 * BLOCK), oh_t).T
    dxbc, ddt, g_cw, g_cb, g_dtb, g_alog, g_dsk = _ssd_bwd(
        xbc, dtr, conv_w, conv_b, dt_bias, a_log, dsk_x, e_mat, hprev, dyp)
    dsegs = (dq, dk, dv, dza, dzm, dxbc, ddt, dga, dgb)
    gx, dshift, dscale, g_nw = _dh(x, dout, norm_w, scale, dsegs, ws)
    g_ws = _gw_in(h, dsegs)
    return dict(loss=loss, grad_x=gx, dmod=jnp.concatenate([dshift, dscale, dgate], axis=1), g_ws=g_ws,
                g_wap=g_wap, g_wsp=g_wsp, g_wout=g_wout, g_norm_w=g_nw, g_qnw=g_qnw, g_knw=g_knw, g_rel=g_rel,
                g_sinks=g_sinks, g_conv_w=g_cw, g_conv_b=g_cb, g_dt_bias=g_dtb, g_a_log=g_alog, g_d_skip=g_dsk,
                g_ssm_nw=g_ssm_nw)


def _me():
    return lax.axis_index("x"), lax.axis_index("y"), lax.axis_index("c")


def _flip(v, bit):
    return 1 - v if bit else v


def _ag_direct(v, name):
    def body(v_ref, out_ref, send_sems, recv_sems, local_sem):
        x, y, c = _me()
        me = 4 * x + 2 * y + c
        mine = pltpu.make_async_copy(v_ref, out_ref.at[me], local_sem)
        mine.start()
        peers = [(_flip(x, k >> 2 & 1), _flip(y, k >> 1 & 1), _flip(c, k & 1)) for k in range(1, N_DEV)]
        sends = [pltpu.make_async_remote_copy(
            src_ref=v_ref, dst_ref=out_ref.at[me], send_sem=send_sems.at[j], recv_sem=recv_sems.at[j],
            device_id=p, device_id_type=MESH) for j, p in enumerate(peers)]
        for cp in sends:
            cp.start()
        for j, (px, py, pc) in enumerate(peers):
            pltpu.make_async_remote_copy(
                src_ref=v_ref, dst_ref=out_ref.at[4 * px + 2 * py + pc], send_sem=send_sems.at[j],
                recv_sem=recv_sems.at[j], device_id=(px, py, pc), device_id_type=MESH).wait_recv()
        for cp in sends:
            cp.wait_send()
        mine.wait()

    vm = pl.BlockSpec(memory_space=pltpu.VMEM)
    return pl.pallas_call(
        body, name=name, out_shape=jax.ShapeDtypeStruct((N_DEV,) + v.shape, v.dtype),
        in_specs=[vm], out_specs=vm,
        scratch_shapes=[pltpu.SemaphoreType.DMA((N_DEV - 1,)), pltpu.SemaphoreType.DMA((N_DEV - 1,)),
                        pltpu.SemaphoreType.DMA],
        compiler_params=_params(),
    )(v)


def _ag_two_level(v, name):
    def body(v_ref, out_ref, send_sems, recv_sems, local_sem):
        x, y, c = _me()
        me, sibling = (x, y, c), (x, y, 1 - c)
        chips = [(1 - x, y), (x, 1 - y), (1 - x, 1 - y)]

        def slot(px, py, pc):
            return out_ref.at[4 * px + 2 * py + pc]

        def copy(k, block, to, src=None):
            return pltpu.make_async_remote_copy(
                src_ref=slot(*block) if src is None else src, dst_ref=slot(*block),
                send_sem=send_sems.at[k], recv_sem=recv_sems.at[k], device_id=to, device_id_type=MESH)

        mine = pltpu.make_async_copy(v_ref, slot(*me), local_sem)
        mine.start()
        first = [copy(0, me, sibling, src=v_ref)]
        first += [copy(1 + j, me, (*chip, c), src=v_ref) for j, chip in enumerate(chips)]
        for cp in first:
            cp.start()
        passed = [copy(4 + j, (*chip, c), sibling) for j, chip in enumerate(chips)]
        for j, chip in enumerate(chips):
            copy(1 + j, (*chip, c), me).wait_recv()
            passed[j].start()
        copy(0, sibling, me).wait_recv()
        for j, chip in enumerate(chips):
            copy(4 + j, (*chip, 1 - c), me).wait_recv()
        for cp in first + passed:
            cp.wait_send()
        mine.wait()

    return pl.pallas_call(
        body, name=name, out_shape=jax.ShapeDtypeStruct((N_DEV,) + v.shape, v.dtype),
        in_specs=[ANY], out_specs=ANY,
        scratch_shapes=[pltpu.SemaphoreType.DMA((7,)), pltpu.SemaphoreType.DMA((7,)), pltpu.SemaphoreType.DMA],
        compiler_params=_params(),
    )(v)


def _rs_sibling(g, name):
    def body(g_ref, out_ref, send_sems, recv_sems):
        x, y, c = _me()
        cps = [pltpu.make_async_remote_copy(
            src_ref=g_ref.at[2 * ch + 1 - c], dst_ref=out_ref.at[ch], send_sem=send_sems.at[ch],
            recv_sem=recv_sems.at[ch], device_id=(x, y, 1 - c), device_id_type=MESH) for ch in range(4)]
        for cp in cps:
            cp.start()
        for cp in cps:
            cp.wait()

    return pl.pallas_call(
        body, name=name, out_shape=jax.ShapeDtypeStruct((4,) + g.shape[1:], g.dtype),
        in_specs=[ANY], out_specs=ANY,
        scratch_shapes=[pltpu.SemaphoreType.DMA((4,)), pltpu.SemaphoreType.DMA((4,))],
        compiler_params=_params(),
    )(g)


def _add_sibling(g, got, name):
    _, r, n = g.shape
    tr = min(r, 256)

    def body(c_ref, a_ref, b_ref, o_ref):
        o_ref[...] = a_ref[...] + b_ref[...]

    grid_spec = pltpu.PrefetchScalarGridSpec(
        num_scalar_prefetch=1, grid=(4, r // tr),
        in_specs=[pl.BlockSpec((1, tr, n), lambda ch, i, c_ref: (2 * ch + c_ref[0], i, 0)),
                  pl.BlockSpec((1, tr, n), lambda ch, i, c_ref: (ch, i, 0))],
        out_specs=pl.BlockSpec((1, tr, n), lambda ch, i, c_ref: (ch, i, 0)))
    return pl.pallas_call(
        body, name=name, grid_spec=grid_spec, out_shape=jax.ShapeDtypeStruct((4, r, n), g.dtype),
        compiler_params=_params(dimension_semantics=("arbitrary", "arbitrary")),
    )(lax.axis_index("c").reshape(1).astype(jnp.int32), g, got)


def _rs_chips(p, name):
    def body(p_ref, out_ref, send_sems, recv_sems, local_sem):
        x, y, c = _me()
        my_chip = 2 * x + y
        mine = pltpu.make_async_copy(p_ref.at[my_chip], out_ref.at[my_chip], local_sem)
        mine.start()
        chips = [(1 - x, y), (x, 1 - y), (1 - x, 1 - y)]
        sends = [pltpu.make_async_remote_copy(
            src_ref=p_ref.at[2 * px + py], dst_ref=out_ref.at[my_chip], send_sem=send_sems.at[j],
            recv_sem=recv_sems.at[j], device_id=(px, py, c), device_id_type=MESH) for j, (px, py) in enumerate(chips)]
        for cp in sends:
            cp.start()
        for j, (px, py) in enumerate(chips):
            pltpu.make_async_remote_copy(
                src_ref=p_ref.at[my_chip], dst_ref=out_ref.at[2 * px + py], send_sem=send_sems.at[j],
                recv_sem=recv_sems.at[j], device_id=(px, py, c), device_id_type=MESH).wait_recv()
        for cp in sends:
            cp.wait_send()
        mine.wait()

    return pl.pallas_call(
        body, name=name, out_shape=jax.ShapeDtypeStruct(p.shape, p.dtype),
        in_specs=[ANY], out_specs=ANY,
        scratch_shapes=[pltpu.SemaphoreType.DMA((3,)), pltpu.SemaphoreType.DMA((3,)), pltpu.SemaphoreType.DMA],
        compiler_params=_params(),
    )(p)


def _reduce_scatter(g, name):
    got = _rs_sibling(g, name + "_sib")
    return _rs_chips(_add_sibling(g, got, name + "_add"), name + "_chips")


def _silu(a):
    return a * _sig(a)


def _mod_piece(c_all, w_ada, b_piece):
    def body(c_ref, w_ref, b_ref, o_ref):
        o_ref[...] = _dot(_bf(_silu(c_ref[...])), _bf(w_ref[...])) + b_ref[...]

    return pl.pallas_call(
        body, name="mod_piece", out_shape=jax.ShapeDtypeStruct((c_all.shape[0], w_ada.shape[1]), F32),
        compiler_params=_params(),
    )(c_all, w_ada, b_piece)


def _gw_ada(c_all, dmod_piece):
    def body(c_ref, d_ref, o_ref):
        o_ref[...] = _dot_tn(_bf(_silu(c_ref[...])), _bf(d_ref[...]))

    return pl.pallas_call(
        body, name="gw_ada", out_shape=jax.ShapeDtypeStruct((c_all.shape[1], dmod_piece.shape[1]), F32),
        compiler_params=_params(),
    )(c_all, dmod_piece)


def _adam(parts, w, m, v, name):
    k, r, n = parts.shape
    tr = r if r <= 256 else 256
    assert r % tr == 0

    def body(p_ref, w_ref, m_ref, v_ref, g_ref, d_ref, nm_ref, nv_ref):
        g = p_ref[0]
        for j in range(1, k):
            g = g + p_ref[j]
        m_new = ADAM_B1 * m_ref[...] + (1.0 - ADAM_B1) * g
        v_new = ADAM_B2 * v_ref[...] + (1.0 - ADAM_B2) * jnp.square(g)
        m_hat = m_new / (1.0 - ADAM_B1 ** ADAM_STEP)
        v_hat = v_new / (1.0 - ADAM_B2 ** ADAM_STEP)
        g_ref[...] = g
        d_ref[...] = -ADAM_LR * (m_hat / (jnp.sqrt(v_hat) + ADAM_EPS) + ADAM_WD * w_ref[...])
        nm_ref[...] = m_new
        nv_ref[...] = v_new

    blk = pl.BlockSpec((tr, n), lambda i: (i, 0))
    return pl.pallas_call(
        body, name=name, grid=(r // tr,),
        in_specs=[pl.BlockSpec((k, tr, n), lambda i: (0, i, 0)), blk, blk, blk],
        out_specs=[blk, blk, blk, blk],
        out_shape=[jax.ShapeDtypeStruct((r, n), F32)] * 4,
        compiler_params=_params(dimension_semantics=("arbitrary",)),
    )(parts, w, m, v)


_SMALL = (("b_ada", 3 * D_MODEL), ("norm_w", D_MODEL), ("q_norm_w", HEAD_DIM), ("k_norm_w", HEAD_DIM),
          ("rel_bias", REL_BUCKETS * ATTN_HEADS), ("sinks", ATTN_HEADS), ("conv_b", XBC_W), ("dt_bias", SSM_HEADS),
          ("a_log", SSM_HEADS), ("d_skip", SSM_HEADS), ("ssm_norm_w", SSM_W))
_SMALL_N = sum(n for _, n in _SMALL)
_SMALL_PAD = -(-_SMALL_N // 128) * 128
_PACK_N = _SMALL_PAD + CONV_K * XBC_W


def _pack_small(d):
    parts = [d[name].reshape(1, n) for name, n in _SMALL]
    return jnp.concatenate(parts + [jnp.zeros((1, _SMALL_PAD - _SMALL_N), F32)], axis=1)


def _unpack_small(vec, shapes):
    out, off = {}, 0
    for name, n in _SMALL:
        out[name] = vec[:, off:off + n].reshape(shapes[name])
        off += n
    return out


WEIGHTS = ("w_ada", "b_ada", "norm_w", "w_in", "q_norm_w", "k_norm_w", "rel_bias", "sinks", "conv_w", "conv_b",
           "dt_bias", "a_log", "d_skip", "ssm_norm_w", "w_attn_proj", "w_ssm_proj", "w_out")


def kernel(x, c, w_ada, b_ada, norm_w, w_in, q_norm_w, k_norm_w, rel_bias, sinks, conv_w, conv_b, dt_bias, a_log, d_skip, ssm_norm_w, w_attn_proj, w_ssm_proj, w_out, loss_target, m_w_ada, m_b_ada, m_norm_w, m_w_in, m_q_norm_w, m_k_norm_w, m_rel_bias, m_sinks, m_conv_w, m_conv_b, m_dt_bias, m_a_log, m_d_skip, m_ssm_norm_w, m_w_attn_proj, m_w_ssm_proj, m_w_out, v_w_ada, v_b_ada, v_norm_w, v_w_in, v_q_norm_w, v_k_norm_w, v_rel_bias, v_sinks, v_conv_w, v_conv_b, v_dt_bias, v_a_log, v_d_skip, v_ssm_norm_w, v_w_attn_proj, v_w_ssm_proj, v_w_out):
    w = dict(w_ada=w_ada, b_ada=b_ada, norm_w=norm_w, w_in=w_in, q_norm_w=q_norm_w, k_norm_w=k_norm_w,
             rel_bias=rel_bias, sinks=sinks, conv_w=conv_w, conv_b=conv_b, dt_bias=dt_bias, a_log=a_log,
             d_skip=d_skip, ssm_norm_w=ssm_norm_w, w_attn_proj=w_attn_proj, w_ssm_proj=w_ssm_proj, w_out=w_out)
    m = dict(w_ada=m_w_ada, b_ada=m_b_ada, norm_w=m_norm_w, w_in=m_w_in, q_norm_w=m_q_norm_w, k_norm_w=m_k_norm_w,
             rel_bias=m_rel_bias, sinks=m_sinks, conv_w=m_conv_w, conv_b=m_conv_b, dt_bias=m_dt_bias, a_log=m_a_log,
             d_skip=m_d_skip, ssm_norm_w=m_ssm_norm_w, w_attn_proj=m_w_attn_proj, w_ssm_proj=m_w_ssm_proj, w_out=m_w_out)
    v = dict(w_ada=v_w_ada, b_ada=v_b_ada, norm_w=v_norm_w, w_in=v_w_in, q_norm_w=v_q_norm_w, k_norm_w=v_k_norm_w,
             rel_bias=v_rel_bias, sinks=v_sinks, conv_w=v_conv_w, conv_b=v_conv_b, dt_bias=v_dt_bias, a_log=v_a_log,
             d_skip=v_d_skip, ssm_norm_w=v_ssm_norm_w, w_attn_proj=v_w_attn_proj, w_ssm_proj=v_w_ssm_proj, w_out=v_w_out)
    me = 4 * lax.axis_index("x") + 2 * lax.axis_index("y") + lax.axis_index("c")
    ada_n = w_ada.shape[2]
    in_n = w_in.shape[2]
    cw_n = conv_w.shape[2]

    c_all = _ag_direct(c, "ag_c").reshape(N_DEV, D_MODEL)
    b_piece = lax.dynamic_slice_in_dim(b_ada, me * ada_n, ada_n, axis=1)
    mod_all = _ag_direct(_mod_piece(c_all, w_ada[0], b_piece), "ag_mod")
    mod = lax.dynamic_index_in_dim(mod_all, me, axis=1, keepdims=False).reshape(1, 3 * D_MODEL)
    shift, scale, gate = mod[:, :D_MODEL], mod[:, D_MODEL:2 * D_MODEL], mod[:, 2 * D_MODEL:]

    w_in_all = _ag_two_level(w_in[0].astype(BF), "ag_w_in")
    w_in_full = w_in_all.transpose(1, 0, 2).reshape(D_MODEL, N_DEV * in_n)
    ws = [w_in_full[:, SEG_OFF[j]:SEG_OFF[j + 1]] for j in range(9)]
    rows = jnp.concatenate([w_attn_proj[0], w_ssm_proj[0], w_out[0]], axis=0).astype(BF)
    rows_all = _ag_two_level(rows, "ag_w_rows")
    r_ap, r_sp = w_attn_proj.shape[1], w_ssm_proj.shape[1]
    wap = rows_all[:, :r_ap].reshape(ATTN_W, D_MODEL)
    wsp = rows_all[:, r_ap:r_ap + r_sp].reshape(SSM_W, D_MODEL)
    wout = rows_all[:, r_ap + r_sp:].reshape(D_MODEL, D_MODEL)
    conv_w_full = _ag_direct(conv_w[0], "ag_conv_w").transpose(1, 0, 2).reshape(CONV_K, XBC_W)

    r = _local_step(x[0], loss_target[0], shift, scale, gate, ws, wap, wsp, wout, norm_w, q_norm_w, k_norm_w,
                    rel_bias, sinks, conv_w_full, conv_b, dt_bias, a_log, d_skip, ssm_norm_w)

    loss = lax.psum(r["loss"][0, 0], ("x", "y", "c"))

    small = dict(b_ada=r["dmod"], norm_w=r["g_norm_w"], q_norm_w=r["g_qnw"], k_norm_w=r["g_knw"], rel_bias=r["g_rel"],
                 sinks=r["g_sinks"], conv_b=r["g_conv_b"], dt_bias=r["g_dt_bias"], a_log=r["g_a_log"],
                 d_skip=r["g_d_skip"], ssm_norm_w=r["g_ssm_nw"])
    pack = jnp.concatenate([_pack_small(small), r["g_conv_w"].reshape(1, CONV_K * XBC_W)], axis=1)
    pack_all = _ag_direct(pack, "ag_small")
    shapes = {name: w[name].shape for name, _ in _SMALL}
    res = {}
    g_s, d_s, m_s, v_s = _adam(pack_all[:, :, :_SMALL_PAD], _pack_small(w), _pack_small(m), _pack_small(v), "adam_small")
    for name, arr in _unpack_small(g_s, shapes).items():
        res[name] = [arr]
    for vec in (d_s, m_s, v_s):
        for name, arr in _unpack_small(vec, shapes).items():
            res[name].append(arr)
    cw_parts = pack_all[:, 0, _SMALL_PAD:].reshape(N_DEV, CONV_K, XBC_W)
    cw_mine = lax.dynamic_slice_in_dim(cw_parts, me * cw_n, cw_n, axis=2)
    res["conv_w"] = [a[None] for a in _adam(cw_mine, conv_w[0], m_conv_w[0], v_conv_w[0], "adam_conv_w")]

    dmod_piece = lax.dynamic_slice_in_dim(pack_all[:, 0, :3 * D_MODEL], me * ada_n, ada_n, axis=1)
    g_ada = _gw_ada(c_all, dmod_piece)
    res["w_ada"] = [a[None] for a in _adam(g_ada[None], w_ada[0], m_w_ada[0], v_w_ada[0], "adam_w_ada")]

    g_in = jnp.concatenate(r["g_ws"], axis=1).reshape(D_MODEL, N_DEV, in_n).transpose(1, 0, 2)
    res["w_in"] = [a[None] for a in _adam(_reduce_scatter(g_in, "rs_w_in"), w_in[0], m_w_in[0], v_w_in[0], "adam_w_in")]
    g_rows = jnp.concatenate([r["g_wap"].reshape(N_DEV, r_ap, D_MODEL), r["g_wsp"].reshape(N_DEV, r_sp, D_MODEL),
                              r["g_wout"].reshape(N_DEV, r_ap, D_MODEL)], axis=1)
    cat = lambda d: jnp.concatenate([d["w_attn_proj"][0], d["w_ssm_proj"][0], d["w_out"][0]], axis=0)
    rows_res = _adam(_reduce_scatter(g_rows, "rs_w_rows"), cat(w), cat(m), cat(v), "adam_w_rows")
    res["w_attn_proj"] = [a[None, :r_ap] for a in rows_res]
    res["w_ssm_proj"] = [a[None, r_ap:r_ap + r_sp] for a in rows_res]
    res["w_out"] = [a[None, r_ap + r_sp:] for a in rows_res]

    outs = [loss, r["grad_x"][None]]
    for j in range(4):
        outs += [res[name][j] for name in WEIGHTS]
    return tuple(outs)
```

```python
import functools
import math

import numpy as np
import jax
import jax.numpy as jnp
from jax import lax
from jax.experimental import pallas as pl
from jax.experimental.pallas import tpu as pltpu

F32 = jnp.float32
BF = jnp.bfloat16
HI = lax.Precision.HIGHEST

D_MODEL = 1024
ATTN_HEADS = 16
KV_HEADS = 4
GRP = ATTN_HEADS // KV_HEADS
HEAD_DIM = 64
ATTN_W = ATTN_HEADS * HEAD_DIM
KV_W = KV_HEADS * HEAD_DIM
BLOCK = 128
REL_BUCKETS = 32
REL_MAX_DIST = 128
SSM_W = 2048
SSM_P = 64
SSM_HEADS = 32
SSM_G = 4
SSM_R = 8
SSM_N = 128
CONV_K = 4
XBC_W = SSM_W + 2 * SSM_G * SSM_N
SEG_W = (ATTN_W, KV_W, KV_W, ATTN_W, SSM_W, XBC_W, SSM_HEADS, D_MODEL, D_MODEL)
SEG_OFF = tuple(int(v) for v in np.cumsum((0,) + SEG_W))
IN_W = SEG_OFF[-1]
EPS = 1e-6
N_DEV = 8
ADAM_LR, ADAM_B1, ADAM_B2, ADAM_EPS, ADAM_WD, ADAM_STEP = 0.001, 0.9, 0.999, 1e-08, 0.01, 10
VMEM_LIMIT = 60 * 1024 * 1024
MESH = pl.DeviceIdType.MESH
ANY = pl.BlockSpec(memory_space=pl.ANY)


def _dot(a, b, precision=None):
    return jnp.dot(a, b, preferred_element_type=F32, precision=precision)


def _dot_nt(a, b, precision=None):
    return lax.dot_general(a, b, (((1,), (1,)), ((), ())), preferred_element_type=F32, precision=precision)


def _dot_tn(a, b, precision=None):
    return lax.dot_general(a, b, (((0,), (0,)), ((), ())), preferred_element_type=F32, precision=precision)


def _bf(a):
    return a.astype(BF)


def _sig(a):
    return 1.0 / (1.0 + jnp.exp(-a))


def _params(**kw):
    return pltpu.CompilerParams(vmem_limit_bytes=VMEM_LIMIT, **kw)


def _full(shape):
    nd = len(shape)
    return pl.BlockSpec(shape, lambda i: (0,) * nd)


def _rows(tm, w):
    return pl.BlockSpec((tm, w), lambda i: (i, 0))


def _inproj(x, norm_w, scale, shift, ws, tm=256):
    s = x.shape[0]

    def body(x_ref, nw_ref, sc_ref, sh_ref, *rest):
        w_hbm, outs, h_ref, w_vm, sem = rest[:9], rest[9:18], rest[18], rest[19:28], rest[28]

        @pl.when(pl.program_id(0) == 0)
        def _():
            cps = [pltpu.make_async_copy(w_hbm[j], w_vm[j], sem.at[j]) for j in range(9)]
            for cp in cps:
                cp.start()
            for cp in cps:
                cp.wait()

        xv = x_ref[...]
        r = lax.rsqrt(jnp.mean(xv * xv, axis=-1, keepdims=True) + EPS)
        h = xv * r * (nw_ref[...] * (1.0 + sc_ref[...])) + sh_ref[...]
        hb = _bf(h)
        h_ref[...] = hb
        for j in range(9):
            outs[j][...] = _dot(hb, w_vm[j][...])

    vec = _full((1, D_MODEL))
    return pl.pallas_call(
        body, name="inproj", grid=(s // tm,),
        in_specs=[_rows(tm, D_MODEL), vec, vec, vec] + [ANY] * 9,
        out_specs=[_rows(tm, w) for w in SEG_W] + [_rows(tm, D_MODEL)],
        out_shape=[jax.ShapeDtypeStruct((s, w), F32) for w in SEG_W] + [jax.ShapeDtypeStruct((s, D_MODEL), BF)],
        scratch_shapes=[pltpu.VMEM((D_MODEL, w), BF) for w in SEG_W] + [pltpu.SemaphoreType.DMA((9,))],
        compiler_params=_params(dimension_semantics=("arbitrary",)),
    )(x, norm_w, scale, shift, *ws)


def _bucket_onehot_t():
    qi = jnp.arange(BLOCK)[:, None]
    kj = jnp.arange(2 * BLOCK)[None, :]
    dist = qi + BLOCK - kj
    n = jnp.maximum(dist, 0)
    max_exact = REL_BUCKETS // 2
    nf = jnp.maximum(n, 1).astype(F32)
    large = max_exact + (jnp.log(nf / max_exact) / math.log(REL_MAX_DIST / max_exact)
                         * (REL_BUCKETS - max_exact)).astype(jnp.int32)
    large = jnp.minimum(large, REL_BUCKETS - 1)
    bucket = jnp.where(n < max_exact, n, large).reshape(1, BLOCK * 2 * BLOCK)
    return (bucket == jnp.arange(REL_BUCKETS)[:, None]).astype(F32)


def _bias_dense(rel_bias_t, oh_t):
    def body(rb_ref, oh_ref, o_ref):
        o_ref[...] = _dot(rb_ref[...], oh_ref[...], HI)

    return pl.pallas_call(
        body, name="bias_dense", out_shape=jax.ShapeDtypeStruct((ATTN_HEADS, BLOCK * 2 * BLOCK), F32),
        compiler_params=_params(),
    )(rel_bias_t, oh_t)


def _bias_grad(ds_sum, oh_t):
    def body(ds_ref, oh_ref, o_ref):
        o_ref[...] = _dot_nt(ds_ref[...], oh_ref[...], HI)

    return pl.pallas_call(
        body, name="bias_grad", out_shape=jax.ShapeDtypeStruct((ATTN_HEADS, REL_BUCKETS), F32),
        compiler_params=_params(),
    )(ds_sum, oh_t)


def _group_sum(a, e):
    hi = _bf(a)
    return _dot(hi, e) + _dot(_bf(a - hi.astype(F32)), e)


def _group_bcast(a, e3t):
    hi = _bf(a)
    r1 = a - hi.astype(F32)
    mid = _bf(r1)
    return _dot(jnp.concatenate([hi, mid, _bf(r1 - mid.astype(F32))], axis=1), e3t)


def _membership(width, group, ngroups):
    e = (jnp.arange(width)[:, None] // group == jnp.arange(ngroups)[None, :]).astype(BF)
    return e, jnp.tile(e.T, (3, 1))


def _fold(width, group):
    return (jnp.arange(width)[:, None] % group == jnp.arange(group)[None, :]).astype(BF)


def _heads_norm(t, w_x, e, e3t):
    r = lax.rsqrt(_group_sum(t * t, e) * (1.0 / HEAD_DIM) + EPS)
    r_x = _group_bcast(r, e3t)
    return t * r_x * w_x, r_x


def _heads_norm_bwd(t, r_x, w_x, d, e, e3t):
    wd = d * w_x
    corr = _group_bcast(_group_sum(t * wd, e) * (1.0 / HEAD_DIM), e3t)
    return r_x * wd - t * (r_x * r_x * r_x) * corr, jnp.sum(d * t * r_x, axis=0, keepdims=True)


def _stack_heads(a, hk):
    return jnp.concatenate([a[:, (hk * GRP + g) * HEAD_DIM:(hk * GRP + g + 1) * HEAD_DIM] for g in range(GRP)], axis=0)


def _stack_cols(a, hk):
    return jnp.concatenate([a[:, hk * GRP + g:hk * GRP + g + 1] for g in range(GRP)], axis=0)


def _window_mask(first):
    qi = jnp.bitwise_and(lax.broadcasted_iota(jnp.int32, (GRP * BLOCK, 2 * BLOCK), 0), BLOCK - 1)
    kj = lax.broadcasted_iota(jnp.int32, (GRP * BLOCK, 2 * BLOCK), 1)
    prev_ok = jnp.logical_and(kj > qi, jnp.logical_not(first))
    cur_ok = jnp.logical_and(kj >= BLOCK, kj - BLOCK <= qi)
    return jnp.logical_or(jnp.logical_and(kj < BLOCK, prev_ok), cur_ok)


def _attn_consts(qnw, knw):
    eq, eq3t = _membership(ATTN_W, HEAD_DIM, ATTN_HEADS)
    ek, ek3t = _membership(KV_W, HEAD_DIM, ATTN_HEADS)
    return (jnp.tile(qnw, (1, ATTN_HEADS)), jnp.tile(knw, (1, KV_HEADS)), eq, eq3t, ek, ek3t)


def _attn_fwd(q, k, v, bias, sinks, consts):
    s = q.shape[0]
    nb = s // BLOCK

    def body(q_ref, kp_ref, kc_ref, vp_ref, vc_ref, b_ref, sk_ref, qw_ref, kw_ref, eq_ref, eq3_ref, ek_ref, ek3_ref,
             o_ref, lse_ref):
        i = pl.program_id(0)
        mask = _window_mask(i == 0)
        qn = _bf(_heads_norm(q_ref[...], qw_ref[...], eq_ref[...], eq3_ref[...])[0])
        kn = _bf(_heads_norm(jnp.concatenate([kp_ref[...], kc_ref[...]], axis=0), kw_ref[...], ek_ref[...], ek3_ref[...])[0])
        vv = _bf(jnp.concatenate([vp_ref[...], vc_ref[...]], axis=0))
        lses = []
        for hk in range(KV_HEADS):
            ks = slice(hk * HEAD_DIM, (hk + 1) * HEAD_DIM)
            sc = _dot_nt(_stack_heads(qn, hk), kn[:, ks]) * (HEAD_DIM ** -0.5)
            sc = sc + b_ref[hk * GRP:(hk + 1) * GRP].reshape(GRP * BLOCK, 2 * BLOCK)
            sc = jnp.where(mask, sc, -1e30)
            sink = jnp.concatenate([jnp.full((BLOCK, 1), sk_ref[0, hk * GRP + g], F32) for g in range(GRP)], axis=0)
            m = jnp.maximum(jnp.max(sc, axis=-1, keepdims=True), sink)
            p = jnp.exp(sc - m)
            den = jnp.sum(p, axis=-1, keepdims=True) + jnp.exp(sink - m)
            out = _dot(_bf(p), vv[:, ks]) / den
            lse = m + jnp.log(den)
            for g in range(GRP):
                h = hk * GRP + g
                o_ref[:, h * HEAD_DIM:(h + 1) * HEAD_DIM] = out[g * BLOCK:(g + 1) * BLOCK]
                lses.append(lse[g * BLOCK:(g + 1) * BLOCK])
        lse_ref[...] = jnp.concatenate(lses, axis=1)

    cur = lambda w: pl.BlockSpec((BLOCK, w), lambda i: (i, 0))
    prev = lambda w: pl.BlockSpec((BLOCK, w), lambda i: (jnp.maximum(i - 1, 0), 0))
    return pl.pallas_call(
        body, name="attn_fwd", grid=(nb,),
        in_specs=[cur(ATTN_W), prev(KV_W), cur(KV_W), prev(KV_W), cur(KV_W),
                  pl.BlockSpec((ATTN_HEADS, BLOCK, 2 * BLOCK), lambda i: (0, 0, 0)),
                  pl.BlockSpec(memory_space=pltpu.SMEM)] + [_full(c.shape) for c in consts],
        out_specs=[cur(ATTN_W), cur(ATTN_HEADS)],
        out_shape=[jax.ShapeDtypeStruct((s, ATTN_W), F32), jax.ShapeDtypeStruct((s, ATTN_HEADS), F32)],
        compiler_params=_params(dimension_semantics=("arbitrary",)),
    )(q, k, k, v, v, bias, sinks, *consts)


def _conv_taps(ext_ref, xbc, tail):
    ext_ref[0:8, :] = tail
    ext_ref[8:8 + BLOCK, :] = xbc
    return [ext_ref[5 + j:5 + j + BLOCK, :] for j in range(CONV_K)]


def _softplus(u):
    return jnp.maximum(u, 0.0) + jnp.log(1.0 + jnp.exp(-jnp.abs(u)))


def _tril():
    r = lax.broadcasted_iota(jnp.int32, (BLOCK, BLOCK), 0)
    c = lax.broadcasted_iota(jnp.int32, (BLOCK, BLOCK), 1)
    return r >= c


def _triu():
    r = lax.broadcasted_iota(jnp.int32, (BLOCK, BLOCK), 0)
    c = lax.broadcasted_iota(jnp.int32, (BLOCK, BLOCK), 1)
    return r <= c


def _exact_left(m01, a):
    hi = _bf(a)
    r1 = a - hi.astype(F32)
    mid = _bf(r1)
    return _dot(m01, hi) + _dot(m01, mid) + _dot(m01, _bf(r1 - mid.astype(F32)))


def _ssd_common(taps, cw_ref, cb_ref, dtr_ref, dtb_ref, alog_ref, e3_ref):
    conv = cb_ref[...] + sum(taps[j] * cw_ref[j:j + 1, :] for j in range(CONV_K))
    sg = _sig(conv)
    xact = conv * sg
    u = dtr_ref[...] + dtb_ref[...]
    dt = _softplus(u)
    a = -jnp.exp(alog_ref[...])
    trilb = _tril()
    acum = _exact_left(trilb.astype(BF), dt * a)
    both = _group_bcast(jnp.concatenate([dt, acum], axis=0), e3_ref[...])
    dt_x, acum_x = both[:BLOCK], both[BLOCK:]
    return conv, sg, xact, u, dt, a, trilb, acum, dt_x, acum_x


def _ssd_fwd(xbc, dt_raw, conv_w, conv_b, dt_bias, a_log, dsk_x, e3t):
    s = xbc.shape[0]
    nc = s // BLOCK

    def body(x_ref, tail_ref, dtr_ref, cw_ref, cb_ref, dtb_ref, alog_ref, dsk_ref, e3_ref,
             y_ref, hp_ref, hst, ext):
        i = pl.program_id(0)

        @pl.when(i == 0)
        def _():
            hst[...] = jnp.zeros_like(hst)

        tail = jnp.where(i > 0, tail_ref[...], 0.0)
        taps = _conv_taps(ext, x_ref[...], tail)
        _, _, xact, _, _, _, trilb, acum, dt_x, acum_x = _ssd_common(
            taps, cw_ref, cb_ref, dtr_ref, dtb_ref, alog_ref, e3_ref)
        xs = xact[:, :SSM_W]
        acum_t = acum.T
        ea_x = jnp.exp(acum_x)
        last_x = acum_x[BLOCK - 1:BLOCK, :]
        xdt = xs * dt_x
        xw = xdt * jnp.exp(last_x - acum_x)
        cd_x = jnp.exp(last_x)
        hprev = hst[...]
        hp_ref[0] = hprev
        dsk = dsk_ref[...]
        for g in range(SSM_G):
            bg = _bf(xact[:, SSM_W + g * SSM_N:SSM_W + (g + 1) * SSM_N])
            cg = _bf(xact[:, SSM_W + SSM_G * SSM_N + g * SSM_N:SSM_W + SSM_G * SSM_N + (g + 1) * SSM_N])
            sl = slice(g * SSM_R * SSM_P, (g + 1) * SSM_R * SSM_P)
            cb = _dot_nt(cg, bg)
            yoff = _dot(cg, _bf(hprev[:, sl])) * ea_x[:, sl]
            hst[:, sl] = hprev[:, sl] * cd_x[:, sl] + _dot_tn(bg, _bf(xw[:, sl]))
            for r in range(SSM_R):
                hh = g * SSM_R + r
                hs = slice(hh * SSM_P, (hh + 1) * SSM_P)
                seg = jnp.where(trilb, acum[:, hh:hh + 1] - acum_t[hh:hh + 1, :], -1e30)
                mm = cb * jnp.exp(seg)
                yd = _dot(_bf(mm), _bf(xdt[:, hs]))
                y_ref[:, hs] = yd + yoff[:, r * SSM_P:(r + 1) * SSM_P] + dsk[:, hs] * xs[:, hs]

    chunk = lambda w: pl.BlockSpec((BLOCK, w), lambda i: (i, 0))
    return pl.pallas_call(
        body, name="ssd_fwd", grid=(nc,),
        in_specs=[chunk(XBC_W), pl.BlockSpec((8, XBC_W), lambda i: (jnp.maximum(i * (BLOCK // 8) - 1, 0), 0)),
                  chunk(SSM_HEADS), _full((CONV_K, XBC_W)), _full((1, XBC_W)), _full((1, SSM_HEADS)),
                  _full((1, SSM_HEADS)), _full((1, SSM_W)), _full((3 * SSM_HEADS, SSM_W))],
        out_specs=[chunk(SSM_W), pl.BlockSpec((1, SSM_N, SSM_W), lambda i: (i, 0, 0))],
        out_shape=[jax.ShapeDtypeStruct((s, SSM_W), F32), jax.ShapeDtypeStruct((nc, SSM_N, SSM_W), F32)],
        scratch_shapes=[pltpu.VMEM((SSM_N, SSM_W), F32), pltpu.VMEM((BLOCK + 8, XBC_W), F32)],
        compiler_params=_params(dimension_semantics=("arbitrary",)),
    )(xbc, xbc, dt_raw, conv_w, conv_b, dt_bias, a_log, dsk_x, e3t)


def _dsilu(z, sg):
    return sg * (1.0 + z * (1.0 - sg))


def _mid(x, tgt, o_att, za, ypre, zm, ga, gb, gate, ssm_nw, wap, wsp, wout, tm=128):
    s = x.shape[0]
    gw = SSM_W // SSM_G

    def body(x_ref, t_ref, o_ref, za_ref, yp_ref, zm_ref, ga_ref, gb_ref, gate_ref, nw_ref, wap_h, wsp_h, wout_h,
             dout_ref, do_ref, dza_ref, dyp_ref, dzm_ref, dga_ref, dgb_ref, gwout_h, gwap_h, gwsp_h,
             gnw_ref, dgate_ref, loss_ref, wap_v, wsp_v, wout_v, a_out, a_ap, a_sp, sem):
        i = pl.program_id(0)

        @pl.when(i == 0)
        def _():
            cps = [pltpu.make_async_copy(a, b, sem.at[j])
                   for j, (a, b) in enumerate(((wap_h, wap_v), (wsp_h, wsp_v), (wout_h, wout_v)))]
            for cp in cps:
                cp.start()
            a_out[...] = jnp.zeros_like(a_out)
            a_ap[...] = jnp.zeros_like(a_ap)
            a_sp[...] = jnp.zeros_like(a_sp)
            gnw_ref[...] = jnp.zeros_like(gnw_ref)
            dgate_ref[...] = jnp.zeros_like(dgate_ref)
            loss_ref[...] = jnp.zeros_like(loss_ref)
            for cp in cps:
                cp.wait()

        gate = gate_ref[...]
        nw = nw_ref[...]
        o_att = o_ref[...]
        z_a = za_ref[...]
        s_a = _sig(z_a)
        silu_a = z_a * s_a
        yag = _bf(o_att * silu_a)
        y_a = _dot(yag, wap_v[...])
        ypre = yp_ref[...]
        z_m = zm_ref[...]
        s_m = _sig(z_m)
        silu_m = z_m * s_m
        yg = ypre * silu_m
        rinv = jnp.concatenate(
            [jnp.broadcast_to(lax.rsqrt(jnp.mean(yg[:, g * gw:(g + 1) * gw] ** 2, axis=-1, keepdims=True) + EPS), (tm, gw))
             for g in range(SSM_G)], axis=1)
        ynr = yg * rinv
        yn = _bf(ynr * nw)
        y_b = _dot(yn, wsp_v[...])
        g_a = _sig(ga_ref[...])
        g_b = _sig(gb_ref[...])
        merged = _bf(g_a * y_a + g_b * y_b)
        o = _dot(merged, wout_v[...])
        diff = x_ref[...] + gate * o - t_ref[...]
        loss_ref[...] += (0.5 / D_MODEL) * jnp.sum(diff * diff, axis=(0, 1), keepdims=True)
        dout = diff * (1.0 / D_MODEL)
        dout_ref[...] = dout
        dgate_ref[...] += jnp.sum(dout * o, axis=0, keepdims=True)
        d_o = _bf(dout * gate)
        dmerged = _dot_nt(d_o, wout_v[...])
        a_out[...] += _dot_tn(merged, d_o)
        dy_a = dmerged * g_a
        dy_b = dmerged * g_b
        dga_ref[...] = _bf(dy_a * y_a * (1.0 - g_a))
        dgb_ref[...] = _bf(dy_b * y_b * (1.0 - g_b))
        dy_a = _bf(dy_a)
        dy_b = _bf(dy_b)
        a_ap[...] += _dot_tn(yag, dy_a)
        dyag = _dot_nt(dy_a, wap_v[...])
        do_ref[...] = dyag * silu_a
        dza_ref[...] = _bf(dyag * o_att * _dsilu(z_a, s_a))
        a_sp[...] += _dot_tn(yn, dy_b)
        dyn = _dot_nt(dy_b, wsp_v[...])
        gnw_ref[...] += jnp.sum(dyn * ynr, axis=0, keepdims=True)
        dynw = dyn * nw
        corr = jnp.concatenate(
            [jnp.broadcast_to(jnp.mean((dynw * ynr)[:, g * gw:(g + 1) * gw], axis=-1, keepdims=True), (tm, gw))
             for g in range(SSM_G)], axis=1)
        dyg = rinv * (dynw - ynr * corr)
        dyp_ref[...] = dyg * silu_m
        dzm_ref[...] = _bf(dyg * ypre * _dsilu(z_m, s_m))

        @pl.when(i == pl.num_programs(0) - 1)
        def _():
            cps = [pltpu.make_async_copy(a, b, sem.at[j])
                   for j, (a, b) in enumerate(((a_out, gwout_h), (a_ap, gwap_h), (a_sp, gwsp_h)))]
            for cp in cps:
                cp.start()
            for cp in cps:
                cp.wait()

    r1, r2 = _rows(tm, D_MODEL), _rows(tm, SSM_W)
    sd = jax.ShapeDtypeStruct
    return pl.pallas_call(
        body, name="mid", grid=(s // tm,),
        in_specs=[r1, r1, r1, r1, r2, r2, r1, r1, _full((1, D_MODEL)), _full((1, SSM_W)), ANY, ANY, ANY],
        out_specs=[r1, r1, r1, r2, r2, r1, r1, ANY, ANY, ANY, _full((1, SSM_W)), _full((1, D_MODEL)), _full((1, 1))],
        out_shape=[sd((s, D_MODEL), F32), sd((s, ATTN_W), F32), sd((s, ATTN_W), BF), sd((s, SSM_W), F32),
                   sd((s, SSM_W), BF), sd((s, D_MODEL), BF), sd((s, D_MODEL), BF),
                   sd((D_MODEL, D_MODEL), F32), sd((ATTN_W, D_MODEL), F32), sd((SSM_W, D_MODEL), F32),
                   sd((1, SSM_W), F32), sd((1, D_MODEL), F32), sd((1, 1), F32)],
        scratch_shapes=[pltpu.VMEM((ATTN_W, D_MODEL), BF), pltpu.VMEM((SSM_W, D_MODEL), BF), pltpu.VMEM((D_MODEL, D_MODEL), BF),
                        pltpu.VMEM((D_MODEL, D_MODEL), F32), pltpu.VMEM((ATTN_W, D_MODEL), F32),
                        pltpu.VMEM((SSM_W, D_MODEL), F32), pltpu.SemaphoreType.DMA((3,))],
        compiler_params=_params(dimension_semantics=("arbitrary",)),
    )(x, tgt, o_att, za, ypre, zm, ga, gb, gate, ssm_nw, wap, wsp, wout)


def _attn_bwd(q, k, v, bias, sinks, consts, o_att, lse, d_o):
    s = q.shape[0]
    nb = s // BLOCK
    folds = (_fold(ATTN_W, HEAD_DIM), _fold(KV_W, HEAD_DIM))

    def body(q_ref, kp_ref, kc_ref, vp_ref, vc_ref, b_ref, skv_ref, qw_ref, kw_ref, eq_ref, eq3_ref, ek_ref, ek3_ref,
             fq_ref, fk_ref, o_ref, lse_ref, do_ref,
             dq_ref, dk_ref, dv_ref, dss_ref, gqw_ref, gkw_ref, gsk_ref, ckn, cv, dqn_s, dkn_s, dv_s, gq_x, gk_x):
        i = pl.program_id(0)
        kw, ek, ek3 = kw_ref[...], ek_ref[...], ek3_ref[...]

        @pl.when(i == 0)
        def _():
            for ref in (ckn, cv, dss_ref, gq_x, gk_x, gsk_ref):
                ref[...] = jnp.zeros_like(ref)

        @pl.when(i < nb)
        def _():
            mask = _window_mask(i == 0)
            qw, eq, eq3 = qw_ref[...], eq_ref[...], eq3_ref[...]
            qf = q_ref[...]
            qnf, rq_x = _heads_norm(qf, qw, eq, eq3)
            qn = _bf(qnf)
            kf = jnp.concatenate([kp_ref[...], kc_ref[...]], axis=0)
            knf, rk_x = _heads_norm(kf, kw, ek, ek3)
            kn = _bf(knf)
            vv = _bf(jnp.concatenate([vp_ref[...], vc_ref[...]], axis=0))
            d_of = do_ref[...]
            d_ob = _bf(d_of)
            lse_all = lse_ref[...]
            delta = _group_sum(d_of * o_ref[...], eq)
            gsk_ref[...] += jnp.sum(-jnp.exp(skv_ref[...] - lse_all) * delta, axis=0, keepdims=True)
            for hk in range(KV_HEADS):
                ks = slice(hk * HEAD_DIM, (hk + 1) * HEAD_DIM)
                qg = _stack_heads(qn, hk)
                sc = _dot_nt(qg, kn[:, ks]) * (HEAD_DIM ** -0.5)
                sc = sc + b_ref[hk * GRP:(hk + 1) * GRP].reshape(GRP * BLOCK, 2 * BLOCK)
                p = jnp.where(mask, jnp.exp(sc - _stack_cols(lse_all, hk)), 0.0)
                d_og = _stack_heads(d_ob, hk)
                ds = p * (_dot_nt(d_og, vv[:, ks]) - _stack_cols(delta, hk))
                dss_ref[hk * GRP:(hk + 1) * GRP] += ds.reshape(GRP, BLOCK, 2 * BLOCK)
                dsb = _bf(ds)
                dv_s[:, ks] = _dot_tn(_bf(p), d_og)
                dkn_s[:, ks] = _dot_tn(dsb, qg) * (HEAD_DIM ** -0.5)
                dqn = _dot(dsb, kn[:, ks]) * (HEAD_DIM ** -0.5)
                for g in range(GRP):
                    h = hk * GRP + g
                    dqn_s[:, h * HEAD_DIM:(h + 1) * HEAD_DIM] = dqn[g * BLOCK:(g + 1) * BLOCK]
            dq, gq = _heads_norm_bwd(qf, rq_x, qw, dqn_s[...], eq, eq3)
            dq_ref[...] = _bf(dq)
            gq_x[...] += gq
            dk, gk = _heads_norm_bwd(kf[:BLOCK], rk_x[:BLOCK], kw, ckn[...] + dkn_s[0:BLOCK, :], ek, ek3)
            dk_ref[...] = _bf(dk)
            gk_x[...] += gk
            dv_ref[...] = _bf(cv[...] + dv_s[0:BLOCK, :])
            ckn[...] = dkn_s[BLOCK:2 * BLOCK, :]
            cv[...] = dv_s[BLOCK:2 * BLOCK, :]

        @pl.when(i == nb)
        def _():
            kc = kc_ref[...]
            dk, gk = _heads_norm_bwd(kc, _heads_norm(kc, kw, ek, ek3)[1], kw, ckn[...], ek, ek3)
            dk_ref[...] = _bf(dk)
            dv_ref[...] = _bf(cv[...])
            gqw_ref[...] = _group_sum(jnp.broadcast_to(gq_x[...], (8, ATTN_W)), fq_ref[...])[0:1]
            gkw_ref[...] = _group_sum(jnp.broadcast_to(gk_x[...] + gk, (8, KV_W)), fk_ref[...])[0:1]

    last = nb - 1
    cur = lambda w: pl.BlockSpec((BLOCK, w), lambda i: (jnp.minimum(i, last), 0))
    prev = lambda w: pl.BlockSpec((BLOCK, w), lambda i: (jnp.maximum(jnp.minimum(i, last) - 1, 0), 0))
    late = lambda w: pl.BlockSpec((BLOCK, w), lambda i: (jnp.maximum(i - 1, 0), 0))
    sd = jax.ShapeDtypeStruct
    return pl.pallas_call(
        body, name="attn_bwd", grid=(nb + 1,),
        in_specs=[cur(ATTN_W), prev(KV_W), cur(KV_W), prev(KV_W), cur(KV_W),
                  pl.BlockSpec((ATTN_HEADS, BLOCK, 2 * BLOCK), lambda i: (0, 0, 0)), _full((1, ATTN_HEADS))]
                 + [_full(c.shape) for c in consts + folds] + [cur(ATTN_W), cur(ATTN_HEADS), cur(ATTN_W)],
        out_specs=[cur(ATTN_W), late(KV_W), late(KV_W),
                   pl.BlockSpec((ATTN_HEADS, BLOCK, 2 * BLOCK), lambda i: (0, 0, 0)),
                   _full((1, HEAD_DIM)), _full((1, HEAD_DIM)), _full((1, ATTN_HEADS))],
        out_shape=[sd((s, ATTN_W), BF), sd((s, KV_W), BF), sd((s, KV_W), BF),
                   sd((ATTN_HEADS, BLOCK, 2 * BLOCK), F32), sd((1, HEAD_DIM), F32), sd((1, HEAD_DIM), F32),
                   sd((1, ATTN_HEADS), F32)],
        scratch_shapes=[pltpu.VMEM((BLOCK, KV_W), F32), pltpu.VMEM((BLOCK, KV_W), F32),
                        pltpu.VMEM((BLOCK, ATTN_W), F32), pltpu.VMEM((2 * BLOCK, KV_W), F32),
                        pltpu.VMEM((2 * BLOCK, KV_W), F32), pltpu.VMEM((1, ATTN_W), F32), pltpu.VMEM((1, KV_W), F32)],
        compiler_params=_params(dimension_semantics=("arbitrary",)),
    )(q, k, k, v, v, bias, sinks, *consts, *folds, o_att, lse, d_o)


def _ssd_bwd(xbc, dt_raw, conv_w, conv_b, dt_bias, a_log, dsk_x, e_mat, e3t, hprev_all, dy_all):
    s = xbc.shape[0]
    nc = s // BLOCK
    gw = SSM_R * SSM_P
    b0, c0 = SSM_W, SSM_W + SSM_G * SSM_N

    def body(x_ref, tail_ref, dtr_ref, cw_ref, cb_ref, dtb_ref, alog_ref, dsk_ref, e_ref, e3_ref, hp_ref, dy_ref,
             dx_ref, ddt_ref, gcw_ref, gcb_ref, gdtb_ref, galog_ref, gdsk_ref,
             dh, nhead, ext, ext2, gdskx, dxdt_s, dbc_s):
        i = pl.program_id(0)
        c = nc - 1 - i

        @pl.when(i == 0)
        def _():
            for ref in (dh, nhead, gdskx, gcw_ref, gcb_ref, gdtb_ref, galog_ref, gdsk_ref):
                ref[...] = jnp.zeros_like(ref)

        tail = jnp.where(c > 0, tail_ref[...], 0.0)
        taps = _conv_taps(ext, x_ref[...], tail)
        conv, sg, xact, u, dt, a, trilb, acum, dt_x, acum_x = _ssd_common(
            taps, cw_ref, cb_ref, dtr_ref, dtb_ref, alog_ref, e3_ref)
        xs = xact[:, :SSM_W]
        acum_t = acum.T
        ea_x = jnp.exp(acum_x)
        last_x = acum_x[BLOCK - 1:BLOCK, :]
        dte_x = jnp.exp(last_x - acum_x)
        cd_x = jnp.exp(last_x)
        xdt = xs * dt_x
        xw = xdt * dte_x
        hprev = hp_ref[0]
        dhn = dh[...]
        dy = dy_ref[...]
        gdskx[...] += jnp.sum(dy * xs, axis=0, keepdims=True)
        dyea = dy * ea_x
        lane = lax.broadcasted_iota(jnp.int32, (BLOCK, SSM_HEADS), 1)
        dacum = jnp.zeros((BLOCK, SSM_HEADS), F32)
        dacc_x, dlast_x = [], []
        for g in range(SSM_G):
            bgf = xact[:, b0 + g * SSM_N:b0 + (g + 1) * SSM_N]
            cgf = xact[:, c0 + g * SSM_N:c0 + (g + 1) * SSM_N]
            bg, cg = _bf(bgf), _bf(cgf)
            sl = slice(g * gw, (g + 1) * gw)
            hpg, dhg, dyeag = _bf(hprev[:, sl]), _bf(dhn[:, sl]), _bf(dyea[:, sl])
            cb = _dot_nt(cg, bg)
            gmat = _dot(cg, hpg)
            dxw = _dot(bg, dhg)
            dxdt_s[:, sl] = dxw * dte_x[:, sl]
            dacc_x.append(dy[:, sl] * gmat * ea_x[:, sl] - dxw * xw[:, sl])
            dlast_x.append(jnp.sum(dxw * xw[:, sl], axis=0, keepdims=True)
                           + jnp.sum(dhn[:, sl] * hprev[:, sl], axis=0, keepdims=True) * cd_x[:, sl])
            dcg = _dot_nt(dyeag, hpg)
            dbg = _dot_nt(_bf(xw[:, sl]), dhg)
            dh[:, sl] = dhn[:, sl] * cd_x[:, sl] + _dot_tn(cg, dyeag)
            dcb = jnp.zeros((BLOCK, BLOCK), F32)
            for r in range(SSM_R):
                hh = g * SSM_R + r
                hs = slice(hh * SSM_P, (hh + 1) * SSM_P)
                seg = jnp.where(trilb, acum[:, hh:hh + 1] - acum_t[hh:hh + 1, :], -1e30)
                lm = jnp.exp(seg)
                mm = cb * lm
                dyh = _bf(dy[:, hs])
                dm = _dot_nt(dyh, _bf(xdt[:, hs]))
                dxdt_s[:, hs] += _dot_tn(_bf(mm), dyh)
                wm = dm * mm
                dcb = dcb + dm * lm
                dacum = dacum + _group_sum(wm - wm.T, (lane == hh).astype(BF))
            dcbb = _bf(dcb)
            dbc_s[:, g * SSM_N:(g + 1) * SSM_N] = dbg + _dot_tn(dcbb, cg)
            dbc_s[:, SSM_G * SSM_N + g * SSM_N:SSM_G * SSM_N + (g + 1) * SSM_N] = dcg + _dot(dcbb, bg)
        dxdt = dxdt_s[...]
        dxs = dy * dsk_ref[...] + dxdt * dt_x
        red = _group_sum(jnp.concatenate(
            [dxdt * xs, jnp.concatenate(dacc_x, axis=1),
             jnp.broadcast_to(jnp.concatenate(dlast_x, axis=1), (8, SSM_W))], axis=0), e_ref[...])
        row = lax.broadcasted_iota(jnp.int32, (BLOCK, SSM_HEADS), 0)
        dacum = dacum + red[BLOCK:2 * BLOCK] + jnp.where(row == BLOCK - 1, red[2 * BLOCK:2 * BLOCK + 1], 0.0)
        ddta = _exact_left(_triu().astype(BF), dacum)
        ddt = red[:BLOCK] + ddta * a
        galog_ref[...] += jnp.sum(ddta * dt, axis=0, keepdims=True) * a
        du = ddt * _sig(u)
        ddt_ref[...] = _bf(du)
        gdtb_ref[...] += jnp.sum(du, axis=0, keepdims=True)
        dconv = jnp.concatenate([dxs, dbc_s[...]], axis=1) * _dsilu(conv, sg)
        gcb_ref[...] += jnp.sum(dconv, axis=0, keepdims=True)
        gcw_ref[...] += jnp.concatenate([jnp.sum(dconv * taps[j], axis=0, keepdims=True) for j in range(CONV_K)], axis=0)
        ext2[0:BLOCK, :] = dconv
        ext2[BLOCK:BLOCK + 8, :] = nhead[...]
        dx_ref[...] = _bf(sum(ext2[3 - j:3 - j + BLOCK, :] * cw_ref[j:j + 1, :] for j in range(CONV_K)))
        nhead[...] = dconv[0:8]

        @pl.when(i == nc - 1)
        def _():
            gdsk_ref[...] = _group_sum(jnp.broadcast_to(gdskx[...], (8, SSM_W)), e_ref[...])[0:1]

    chunk = lambda w: pl.BlockSpec((BLOCK, w), lambda i: (nc - 1 - i, 0))
    sd = jax.ShapeDtypeStruct
    return pl.pallas_call(
        body, name="ssd_bwd", grid=(nc,),
        in_specs=[chunk(XBC_W), pl.BlockSpec((8, XBC_W), lambda i: (jnp.maximum((nc - 1 - i) * (BLOCK // 8) - 1, 0), 0)),
                  chunk(SSM_HEADS), _full((CONV_K, XBC_W)), _full((1, XBC_W)), _full((1, SSM_HEADS)),
                  _full((1, SSM_HEADS)), _full((1, SSM_W)), _full((SSM_W, SSM_HEADS)), _full((3 * SSM_HEADS, SSM_W)),
                  pl.BlockSpec((1, SSM_N, SSM_W), lambda i: (nc - 1 - i, 0, 0)), chunk(SSM_W)],
        out_specs=[chunk(XBC_W), chunk(SSM_HEADS), _full((CONV_K, XBC_W)), _full((1, XBC_W)),
                   _full((1, SSM_HEADS)), _full((1, SSM_HEADS)), _full((1, SSM_HEADS))],
        out_shape=[sd((s, XBC_W), BF), sd((s, SSM_HEADS), BF), sd((CONV_K, XBC_W), F32), sd((1, XBC_W), F32),
                   sd((1, SSM_HEADS), F32), sd((1, SSM_HEADS), F32), sd((1, SSM_HEADS), F32)],
        scratch_shapes=[pltpu.VMEM((SSM_N, SSM_W), F32), pltpu.VMEM((8, XBC_W), F32),
                        pltpu.VMEM((BLOCK + 8, XBC_W), F32), pltpu.VMEM((BLOCK + 8, XBC_W), F32),
                        pltpu.VMEM((1, SSM_W), F32), pltpu.VMEM((BLOCK, SSM_W), F32),
                        pltpu.VMEM((BLOCK, 2 * SSM_G * SSM_N), F32)],
        compiler_params=_params(dimension_semantics=("arbitrary",)),
    )(xbc, xbc, dt_raw, conv_w, conv_b, dt_bias, a_log, dsk_x, e_mat, e3t, hprev_all, dy_all)


def _dh(x, dout, norm_w, scale, dsegs, ws, tm=256):
    s = x.shape[0]

    def body(x_ref, dout_ref, nw_ref, sc_ref, *rest):
        d_refs, w_hbm = rest[:9], rest[9:18]
        gx_ref, dshift_ref, dscale_ref, gnw_ref = rest[18:22]
        w_vm, sem = rest[22:31], rest[31]

        @pl.when(pl.program_id(0) == 0)
        def _():
            cps = [pltpu.make_async_copy(w_hbm[j], w_vm[j], sem.at[j]) for j in range(9)]
            for cp in cps:
                cp.start()
            for ref in (dshift_ref, dscale_ref, gnw_ref):
                ref[...] = jnp.zeros_like(ref)
            for cp in cps:
                cp.wait()

        dh = _dot_nt(d_refs[0][...], w_vm[0][...])
        for j in range(1, 9):
            dh = dh + _dot_nt(d_refs[j][...], w_vm[j][...])
        xv = x_ref[...]
        r = lax.rsqrt(jnp.mean(xv * xv, axis=-1, keepdims=True) + EPS)
        xn = xv * r
        nw = nw_ref[...]
        sc1 = 1.0 + sc_ref[...]
        dshift_ref[...] += jnp.sum(dh, axis=0, keepdims=True)
        dhxn = jnp.sum(dh * xn, axis=0, keepdims=True)
        dscale_ref[...] += dhxn * nw
        gnw_ref[...] += dhxn * sc1
        dxn = dh * (nw * sc1)
        gx_ref[...] = dout_ref[...] + r * (dxn - xn * jnp.mean(xn * dxn, axis=-1, keepdims=True))

    vec = _full((1, D_MODEL))
    sd = jax.ShapeDtypeStruct
    return pl.pallas_call(
        body, name="dh", grid=(s // tm,),
        in_specs=[_rows(tm, D_MODEL), _rows(tm, D_MODEL), vec, vec] + [_rows(tm, w) for w in SEG_W] + [ANY] * 9,
        out_specs=[_rows(tm, D_MODEL), vec, vec, vec],
        out_shape=[sd((s, D_MODEL), F32), sd((1, D_MODEL), F32), sd((1, D_MODEL), F32), sd((1, D_MODEL), F32)],
        scratch_shapes=[pltpu.VMEM((D_MODEL, w), BF) for w in SEG_W] + [pltpu.SemaphoreType.DMA((9,))],
        compiler_params=_params(dimension_semantics=("arbitrary",)),
    )(x, dout, norm_w, scale, *dsegs, *ws)


def _gw_seg(h_t, dseg, name, tm=512):
    s, w = dseg.shape
    tn = min(w, 1024)
    tm = min(tm, s)
    nm = s // tm

    def body(h_ref, d_ref, o_ref):
        @pl.when(pl.program_id(1) == 0)
        def _():
            o_ref[...] = jnp.zeros_like(o_ref)

        o_ref[...] += _dot(h_ref[...], d_ref[...])

    return pl.pallas_call(
        body, name=name, grid=(w // tn, nm),
        in_specs=[pl.BlockSpec((D_MODEL, tm), lambda n, m: (0, m)), pl.BlockSpec((tm, tn), lambda n, m: (m, n))],
        out_specs=pl.BlockSpec((D_MODEL, tn), lambda n, m: (0, n)),
        out_shape=jax.ShapeDtypeStruct((D_MODEL, w), F32),
        compiler_params=_params(dimension_semantics=("arbitrary", "arbitrary")),
    )(h_t, dseg)


def _gw_in(h, dsegs):
    h_t = h.T
    return [_gw_seg(h_t, d, "gw_in_%d" % j) for j, d in enumerate(dsegs)]


def _local_step(x, tgt, shift, scale, gate, ws, wap, wsp, wout, norm_w, qnw, knw, rel_bias, sinks,
                conv_w, conv_b, dt_bias, a_log, d_skip, ssm_nw):
    oh_t = _bucket_onehot_t()
    bias = _bias_dense(rel_bias.T, oh_t).reshape(ATTN_HEADS, BLOCK, 2 * BLOCK)
    *segs, h = _inproj(x, norm_w, scale, shift, ws)
    q, k, v, za, zm, xbc, dtr, ga, gb = segs
    consts = _attn_consts(qnw, knw)
    o_att, lse = _attn_fwd(q, k, v, bias, sinks, consts)
    e_mat, e3t = _membership(SSM_W, SSM_P, SSM_HEADS)
    dsk_x = jnp.repeat(d_skip, SSM_P, axis=1)
    ypre, hprev = _ssd_fwd(xbc, dtr, conv_w, conv_b, dt_bias, a_log, dsk_x, e3t)
    (dout, d_o, dza, dyp, dzm, dga, dgb, g_wout, g_wap, g_wsp, g_ssm_nw, dgate, loss) = _mid(
        x, tgt, o_att, za, ypre, zm, ga, gb, gate, ssm_nw, wap, wsp, wout)
    dq, dk, dv, dss, g_qnw, g_knw, g_sinks = _attn_bwd(q, k, v, bias, sinks, consts, o_att, lse, d_o)
    g_rel = _bias_grad(dss.reshape(ATTN_HEADS, BLOCK * 2 * BLOCK), oh_t).T
    dxbc, ddt, g_cw, g_cb, g_dtb, g_alog, g_dsk = _ssd_bwd(
        xbc, dtr, conv_w, conv_b, dt_bias, a_log, dsk_x, e_mat, e3t, hprev, dyp)
    dsegs = (dq, dk, dv, dza, dzm, dxbc, ddt, dga, dgb)
    gx, dshift, dscale, g_nw = _dh(x, dout, norm_w, scale, dsegs, ws)
    g_ws = _gw_in(h, dsegs)
    return dict(loss=loss, grad_x=gx, dmod=jnp.concatenate([dshift, dscale, dgate], axis=1), g_ws=g_ws,
                g_wap=g_wap, g_wsp=g_wsp, g_wout=g_wout, g_norm_w=g_nw, g_qnw=g_qnw, g_knw=g_knw, g_rel=g_rel,
                g_sinks=g_sinks, g_conv_w=g_cw, g_conv_b=g_cb, g_dt_bias=g_dtb, g_a_log=g_alog, g_d_skip=g_dsk,
                g_ssm_nw=g_ssm_nw)


def _me():
    return lax.axis_index("x"), lax.axis_index("y"), lax.axis_index("c")


def _flip(v, bit):
    return 1 - v if bit else v


def _ag_direct(v, name):
    def body(v_ref, out_ref, send_sems, recv_sems, local_sem):
        x, y, c = _me()
        me = 4 * x + 2 * y + c
        mine = pltpu.make_async_copy(v_ref, out_ref.at[me], local_sem)
        mine.start()
        peers = [(_flip(x, k >> 2 & 1), _flip(y, k >> 1 & 1), _flip(c, k & 1)) for k in range(1, N_DEV)]
        sends = [pltpu.make_async_remote_copy(
            src_ref=v_ref, dst_ref=out_ref.at[me], send_sem=send_sems.at[j], recv_sem=recv_sems.at[j],
            device_id=p, device_id_type=MESH) for j, p in enumerate(peers)]
        for cp in sends:
            cp.start()
        for j, (px, py, pc) in enumerate(peers):
            pltpu.make_async_remote_copy(
                src_ref=v_ref, dst_ref=out_ref.at[4 * px + 2 * py + pc], send_sem=send_sems.at[j],
                recv_sem=recv_sems.at[j], device_id=(px, py, pc), device_id_type=MESH).wait_recv()
        for cp in sends:
            cp.wait_send()
        mine.wait()

    vm = pl.BlockSpec(memory_space=pltpu.VMEM)
    return pl.pallas_call(
        body, name=name, out_shape=jax.ShapeDtypeStruct((N_DEV,) + v.shape, v.dtype),
        in_specs=[vm], out_specs=vm,
        scratch_shapes=[pltpu.SemaphoreType.DMA((N_DEV - 1,)), pltpu.SemaphoreType.DMA((N_DEV - 1,)),
                        pltpu.SemaphoreType.DMA],
        compiler_params=_params(),
    )(v)


def _ag_two_level(v, name):
    def body(v_ref, out_ref, send_sems, recv_sems, local_sem):
        x, y, c = _me()
        me, sibling = (x, y, c), (x, y, 1 - c)
        chips = [(1 - x, y), (x, 1 - y), (1 - x, 1 - y)]

        def slot(px, py, pc):
            return out_ref.at[4 * px + 2 * py + pc]

        def copy(k, block, to, src=None):
            return pltpu.make_async_remote_copy(
                src_ref=slot(*block) if src is None else src, dst_ref=slot(*block),
                send_sem=send_sems.at[k], recv_sem=recv_sems.at[k], device_id=to, device_id_type=MESH)

        mine = pltpu.make_async_copy(v_ref, slot(*me), local_sem)
        mine.start()
        first = [copy(0, me, sibling, src=v_ref)]
        first += [copy(1 + j, me, (*chip, c), src=v_ref) for j, chip in enumerate(chips)]
        for cp in first:
            cp.start()
        passed = [copy(4 + j, (*chip, c), sibling) for j, chip in enumerate(chips)]
        for j, chip in enumerate(chips):
            copy(1 + j, (*chip, c), me).wait_recv()
            passed[j].start()
        copy(0, sibling, me).wait_recv()
        for j, chip in enumerate(chips):
            copy(4 + j, (*chip, 1 - c), me).wait_recv()
        for cp in first + passed:
            cp.wait_send()
        mine.wait()

    return pl.pallas_call(
        body, name=name, out_shape=jax.ShapeDtypeStruct((N_DEV,) + v.shape, v.dtype),
        in_specs=[ANY], out_specs=ANY,
        scratch_shapes=[pltpu.SemaphoreType.DMA((7,)), pltpu.SemaphoreType.DMA((7,)), pltpu.SemaphoreType.DMA],
        compiler_params=_params(),
    )(v)


def _rs_sibling(g, name):
    def body(g_ref, out_ref, send_sems, recv_sems):
        x, y, c = _me()
        cps = [pltpu.make_async_remote_copy(
            src_ref=g_ref.at[2 * ch + 1 - c], dst_ref=out_ref.at[ch], send_sem=send_sems.at[ch],
            recv_sem=recv_sems.at[ch], device_id=(x, y, 1 - c), device_id_type=MESH) for ch in range(4)]
        for cp in cps:
            cp.start()
        for cp in cps:
            cp.wait()

    return pl.pallas_call(
        body, name=name, out_shape=jax.ShapeDtypeStruct((4,) + g.shape[1:], g.dtype),
        in_specs=[ANY], out_specs=ANY,
        scratch_shapes=[pltpu.SemaphoreType.DMA((4,)), pltpu.SemaphoreType.DMA((4,))],
        compiler_params=_params(),
    )(g)


def _add_sibling(g, got, name):
    _, r, n = g.shape
    tr = min(r, 256)

    def body(c_ref, a_ref, b_ref, o_ref):
        o_ref[...] = a_ref[...] + b_ref[...]

    grid_spec = pltpu.PrefetchScalarGridSpec(
        num_scalar_prefetch=1, grid=(4, r // tr),
        in_specs=[pl.BlockSpec((1, tr, n), lambda ch, i, c_ref: (2 * ch + c_ref[0], i, 0)),
                  pl.BlockSpec((1, tr, n), lambda ch, i, c_ref: (ch, i, 0))],
        out_specs=pl.BlockSpec((1, tr, n), lambda ch, i, c_ref: (ch, i, 0)))
    return pl.pallas_call(
        body, name=name, grid_spec=grid_spec, out_shape=jax.ShapeDtypeStruct((4, r, n), g.dtype),
        compiler_params=_params(dimension_semantics=("arbitrary", "arbitrary")),
    )(lax.axis_index("c").reshape(1).astype(jnp.int32), g, got)


def _rs_chips(p, name):
    def body(p_ref, out_ref, send_sems, recv_sems, local_sem):
        x, y, c = _me()
        my_chip = 2 * x + y
        mine = pltpu.make_async_copy(p_ref.at[my_chip], out_ref.at[my_chip], local_sem)
        mine.start()
        chips = [(1 - x, y), (x, 1 - y), (1 - x, 1 - y)]
        sends = [pltpu.make_async_remote_copy(
            src_ref=p_ref.at[2 * px + py], dst_ref=out_ref.at[my_chip], send_sem=send_sems.at[j],
            recv_sem=recv_sems.at[j], device_id=(px, py, c), device_id_type=MESH) for j, (px, py) in enumerate(chips)]
        for cp in sends:
            cp.start()
        for j, (px, py) in enumerate(chips):
            pltpu.make_async_remote_copy(
                src_ref=p_ref.at[my_chip], dst_ref=out_ref.at[2 * px + py], send_sem=send_sems.at[j],
                recv_sem=recv_sems.at[j], device_id=(px, py, c), device_id_type=MESH).wait_recv()
        for cp in sends:
            cp.wait_send()
        mine.wait()

    return pl.pallas_call(
        body, name=name, out_shape=jax.ShapeDtypeStruct(p.shape, p.dtype),
        in_specs=[ANY], out_specs=ANY,
        scratch_shapes=[pltpu.SemaphoreType.DMA((3,)), pltpu.SemaphoreType.DMA((3,)), pltpu.SemaphoreType.DMA],
        compiler_params=_params(),
    )(p)


def _reduce_scatter(g, name):
    got = _rs_sibling(g, name + "_sib")
    return _rs_chips(_add_sibling(g, got, name + "_add"), name + "_chips")


def _silu(a):
    return a * _sig(a)


def _mod_piece(c_all, w_ada, b_piece):
    def body(c_ref, w_ref, b_ref, o_ref):
        o_ref[...] = _dot(_bf(_silu(c_ref[...])), _bf(w_ref[...])) + b_ref[...]

    return pl.pallas_call(
        body, name="mod_piece", out_shape=jax.ShapeDtypeStruct((c_all.shape[0], w_ada.shape[1]), F32),
        compiler_params=_params(),
    )(c_all, w_ada, b_piece)


def _gw_ada(c_all, dmod_piece):
    def body(c_ref, d_ref, o_ref):
        o_ref[...] = _dot_tn(_bf(_silu(c_ref[...])), _bf(d_ref[...]))

    return pl.pallas_call(
        body, name="gw_ada", out_shape=jax.ShapeDtypeStruct((c_all.shape[1], dmod_piece.shape[1]), F32),
        compiler_params=_params(),
    )(c_all, dmod_piece)


def _adam(parts, w, m, v, name):
    k, r, n = parts.shape
    tr = r if r <= 256 else 256
    assert r % tr == 0

    def body(p_ref, w_ref, m_ref, v_ref, g_ref, d_ref, nm_ref, nv_ref):
        g = p_ref[0]
        for j in range(1, k):
            g = g + p_ref[j]
        m_new = ADAM_B1 * m_ref[...] + (1.0 - ADAM_B1) * g
        v_new = ADAM_B2 * v_ref[...] + (1.0 - ADAM_B2) * jnp.square(g)
        m_hat = m_new / (1.0 - ADAM_B1 ** ADAM_STEP)
        v_hat = v_new / (1.0 - ADAM_B2 ** ADAM_STEP)
        g_ref[...] = g
        d_ref[...] = -ADAM_LR * (m_hat / (jnp.sqrt(v_hat) + ADAM_EPS) + ADAM_WD * w_ref[...])
        nm_ref[...] = m_new
        nv_ref[...] = v_new

    blk = pl.BlockSpec((tr, n), lambda i: (i, 0))
    return pl.pallas_call(
        body, name=name, grid=(r // tr,),
        in_specs=[pl.BlockSpec((k, tr, n), lambda i: (0, i, 0)), blk, blk, blk],
        out_specs=[blk, blk, blk, blk],
        out_shape=[jax.ShapeDtypeStruct((r, n), F32)] * 4,
        compiler_params=_params(dimension_semantics=("arbitrary",)),
    )(parts, w, m, v)


_SMALL = (("b_ada", 3 * D_MODEL), ("norm_w", D_MODEL), ("q_norm_w", HEAD_DIM), ("k_norm_w", HEAD_DIM),
          ("rel_bias", REL_BUCKETS * ATTN_HEADS), ("sinks", ATTN_HEADS), ("conv_b", XBC_W), ("dt_bias", SSM_HEADS),
          ("a_log", SSM_HEADS), ("d_skip", SSM_HEADS), ("ssm_norm_w", SSM_W))
_SMALL_N = sum(n for _, n in _SMALL)
_SMALL_PAD = -(-_SMALL_N // 128) * 128
_PACK_N = _SMALL_PAD + CONV_K * XBC_W


def _pack_small(d):
    parts = [d[name].reshape(1, n) for name, n in _SMALL]
    return jnp.concatenate(parts + [jnp.zeros((1, _SMALL_PAD - _SMALL_N), F32)], axis=1)


def _unpack_small(vec, shapes):
    out, off = {}, 0
    for name, n in _SMALL:
        out[name] = vec[:, off:off + n].reshape(shapes[name])
        off += n
    return out


WEIGHTS = ("w_ada", "b_ada", "norm_w", "w_in", "q_norm_w", "k_norm_w", "rel_bias", "sinks", "conv_w", "conv_b",
           "dt_bias", "a_log", "d_skip", "ssm_norm_w", "w_attn_proj", "w_ssm_proj", "w_out")


def kernel(x, c, w_ada, b_ada, norm_w, w_in, q_norm_w, k_norm_w, rel_bias, sinks, conv_w, conv_b, dt_bias, a_log, d_skip, ssm_norm_w, w_attn_proj, w_ssm_proj, w_out, loss_target, m_w_ada, m_b_ada, m_norm_w, m_w_in, m_q_norm_w, m_k_norm_w, m_rel_bias, m_sinks, m_conv_w, m_conv_b, m_dt_bias, m_a_log, m_d_skip, m_ssm_norm_w, m_w_attn_proj, m_w_ssm_proj, m_w_out, v_w_ada, v_b_ada, v_norm_w, v_w_in, v_q_norm_w, v_k_norm_w, v_rel_bias, v_sinks, v_conv_w, v_conv_b, v_dt_bias, v_a_log, v_d_skip, v_ssm_norm_w, v_w_attn_proj, v_w_ssm_proj, v_w_out):
    w = dict(w_ada=w_ada, b_ada=b_ada, norm_w=norm_w, w_in=w_in, q_norm_w=q_norm_w, k_norm_w=k_norm_w,
             rel_bias=rel_bias, sinks=sinks, conv_w=conv_w, conv_b=conv_b, dt_bias=dt_bias, a_log=a_log,
             d_skip=d_skip, ssm_norm_w=ssm_norm_w, w_attn_proj=w_attn_proj, w_ssm_proj=w_ssm_proj, w_out=w_out)
    m = dict(w_ada=m_w_ada, b_ada=m_b_ada, norm_w=m_norm_w, w_in=m_w_in, q_norm_w=m_q_norm_w, k_norm_w=m_k_norm_w,
             rel_bias=m_rel_bias, sinks=m_sinks, conv_w=m_conv_w, conv_b=m_conv_b, dt_bias=m_dt_bias, a_log=m_a_log,
             d_skip=m_d_skip, ssm_norm_w=m_ssm_norm_w, w_attn_proj=m_w_attn_proj, w_ssm_proj=m_w_ssm_proj, w_out=m_w_out)
    v = dict(w_ada=v_w_ada, b_ada=v_b_ada, norm_w=v_norm_w, w_in=v_w_in, q_norm_w=v_q_norm_w, k_norm_w=v_k_norm_w,
             rel_bias=v_rel_bias, sinks=v_sinks, conv_w=v_conv_w, conv_b=v_conv_b, dt_bias=v_dt_bias, a_log=v_a_log,
             d_skip=v_d_skip, ssm_norm_w=v_ssm_norm_w, w_attn_proj=v_w_attn_proj, w_ssm_proj=v_w_ssm_proj, w_out=v_w_out)
    me = 4 * lax.axis_index("x") + 2 * lax.axis_index("y") + lax.axis_index("c")
    ada_n = w_ada.shape[2]
    in_n = w_in.shape[2]
    cw_n = conv_w.shape[2]

    c_all = _ag_direct(c, "ag_c").reshape(N_DEV, D_MODEL)
    b_piece = lax.dynamic_slice_in_dim(b_ada, me * ada_n, ada_n, axis=1)
    mod_all = _ag_direct(_mod_piece(c_all, w_ada[0], b_piece), "ag_mod")
    mod = lax.dynamic_index_in_dim(mod_all, me, axis=1, keepdims=False).reshape(1, 3 * D_MODEL)
    shift, scale, gate = mod[:, :D_MODEL], mod[:, D_MODEL:2 * D_MODEL], mod[:, 2 * D_MODEL:]

    w_in_all = _ag_two_level(w_in[0].astype(BF), "ag_w_in")
    w_in_full = w_in_all.transpose(1, 0, 2).reshape(D_MODEL, N_DEV * in_n)
    ws = [w_in_full[:, SEG_OFF[j]:SEG_OFF[j + 1]] for j in range(9)]
    rows = jnp.concatenate([w_attn_proj[0], w_ssm_proj[0], w_out[0]], axis=0).astype(BF)
    rows_all = _ag_two_level(rows, "ag_w_rows")
    r_ap, r_sp = w_attn_proj.shape[1], w_ssm_proj.shape[1]
    wap = rows_all[:, :r_ap].reshape(ATTN_W, D_MODEL)
    wsp = rows_all[:, r_ap:r_ap + r_sp].reshape(SSM_W, D_MODEL)
    wout = rows_all[:, r_ap + r_sp:].reshape(D_MODEL, D_MODEL)
    conv_w_full = _ag_direct(conv_w[0], "ag_conv_w").transpose(1, 0, 2).reshape(CONV_K, XBC_W)

    r = _local_step(x[0], loss_target[0], shift, scale, gate, ws, wap, wsp, wout, norm_w, q_norm_w, k_norm_w,
                    rel_bias, sinks, conv_w_full, conv_b, dt_bias, a_log, d_skip, ssm_norm_w)

    loss = lax.psum(r["loss"][0, 0], ("x", "y", "c"))

    small = dict(b_ada=r["dmod"], norm_w=r["g_norm_w"], q_norm_w=r["g_qnw"], k_norm_w=r["g_knw"], rel_bias=r["g_rel"],
                 sinks=r["g_sinks"], conv_b=r["g_conv_b"], dt_bias=r["g_dt_bias"], a_log=r["g_a_log"],
                 d_skip=r["g_d_skip"], ssm_norm_w=r["g_ssm_nw"])
    pack = jnp.concatenate([_pack_small(small), r["g_conv_w"].reshape(1, CONV_K * XBC_W)], axis=1)
    pack_all = _ag_direct(pack, "ag_small")
    shapes = {name: w[name].shape for name, _ in _SMALL}
    res = {}
    g_s, d_s, m_s, v_s = _adam(pack_all[:, :, :_SMALL_PAD], _pack_small(w), _pack_small(m), _pack_small(v), "adam_small")
    for name, arr in _unpack_small(g_s, shapes).items():
        res[name] = [arr]
    for vec in (d_s, m_s, v_s):
        for name, arr in _unpack_small(vec, shapes).items():
            res[name].append(arr)
    cw_parts = pack_all[:, 0, _SMALL_PAD:].reshape(N_DEV, CONV_K, XBC_W)
    cw_mine = lax.dynamic_slice_in_dim(cw_parts, me * cw_n, cw_n, axis=2)
    res["conv_w"] = [a[None] for a in _adam(cw_mine, conv_w[0], m_conv_w[0], v_conv_w[0], "adam_conv_w")]

    dmod_piece = lax.dynamic_slice_in_dim(pack_all[:, 0, :3 * D_MODEL], me * ada_n, ada_n, axis=1)
    g_ada = _gw_ada(c_all, dmod_piece)
    res["w_ada"] = [a[None] for a in _adam(g_ada[None], w_ada[0], m_w_ada[0], v_w_ada[0], "adam_w_ada")]

    g_in = jnp.concatenate(r["g_ws"], axis=1).reshape(D_MODEL, N_DEV, in_n).transpose(1, 0, 2)
    res["w_in"] = [a[None] for a in _adam(_reduce_scatter(g_in, "rs_w_in"), w_in[0], m_w_in[0], v_w_in[0], "adam_w_in")]
    g_rows = jnp.concatenate([r["g_wap"].reshape(N_DEV, r_ap, D_MODEL), r["g_wsp"].reshape(N_DEV, r_sp, D_MODEL),
                              r["g_wout"].reshape(N_DEV, r_ap, D_MODEL)], axis=1)
    cat = lambda d: jnp.concatenate([d["w_attn_proj"][0], d["w_ssm_proj"][0], d["w_out"][0]], axis=0)
    rows_res = _adam(_reduce_scatter(g_rows, "rs_w_rows"), cat(w), cat(m), cat(v), "adam_w_rows")
    res["w_attn_proj"] = [a[None, :r_ap] for a in rows_res]
    res["w_ssm_proj"] = [a[None, r_ap:r_ap + r_sp] for a in rows_res]
    res["w_out"] = [a[None, r_ap + r_sp:] for a in rows_res]

    outs = [loss, r["grad_x"][None]]
    for j in range(4):
        outs += [res[name][j] for name in WEIGHTS]
    return tuple(outs)
```

```python
import functools
import math

import numpy as np
import jax
import jax.numpy as jnp
from jax import lax
from jax.experimental import pallas as pl
from jax.experimental.pallas import tpu as pltpu

F32 = jnp.float32
BF = jnp.bfloat16
HI = lax.Precision.HIGHEST

D_MODEL = 1024
ATTN_HEADS = 16
KV_HEADS = 4
GRP = ATTN_HEADS // KV_HEADS
HEAD_DIM = 64
ATTN_W = ATTN_HEADS * HEAD_DIM
KV_W = KV_HEADS * HEAD_DIM
BLOCK = 128
REL_BUCKETS = 32
REL_MAX_DIST = 128
SSM_W = 2048
SSM_P = 64
SSM_HEADS = 32
SSM_G = 4
SSM_R = 8
SSM_N = 128
CONV_K = 4
XBC_W = SSM_W + 2 * SSM_G * SSM_N
SEG_W = (ATTN_W, KV_W, KV_W, ATTN_W, SSM_W, XBC_W, SSM_HEADS, D_MODEL, D_MODEL)
SEG_OFF = tuple(int(v) for v in np.cumsum((0,) + SEG_W))
IN_W = SEG_OFF[-1]
EPS = 1e-6
N_DEV = 8
ADAM_LR, ADAM_B1, ADAM_B2, ADAM_EPS, ADAM_WD, ADAM_STEP = 0.001, 0.9, 0.999, 1e-08, 0.01, 10
VMEM_LIMIT = 60 * 1024 * 1024
MESH = pl.DeviceIdType.MESH
ANY = pl.BlockSpec(memory_space=pl.ANY)


def _dot(a, b, precision=None):
    return jnp.dot(a, b, preferred_element_type=F32, precision=precision)


def _dot_nt(a, b, precision=None):
    return lax.dot_general(a, b, (((1,), (1,)), ((), ())), preferred_element_type=F32, precision=precision)


def _dot_tn(a, b, precision=None):
    return lax.dot_general(a, b, (((0,), (0,)), ((), ())), preferred_element_type=F32, precision=precision)


def _bf(a):
    return a.astype(BF)


def _sig(a):
    return 1.0 / (1.0 + jnp.exp(-a))


def _params(**kw):
    return pltpu.CompilerParams(vmem_limit_bytes=VMEM_LIMIT, **kw)


def _full(shape):
    nd = len(shape)
    return pl.BlockSpec(shape, lambda i: (0,) * nd)


def _rows(tm, w):
    return pl.BlockSpec((tm, w), lambda i: (i, 0))


def _inproj(x, norm_w, scale, shift, ws, tm=256):
    s = x.shape[0]

    def body(x_ref, nw_ref, sc_ref, sh_ref, *rest):
        w_hbm, outs, h_ref, w_vm, sem = rest[:9], rest[9:18], rest[18], rest[19:28], rest[28]

        @pl.when(pl.program_id(0) == 0)
        def _():
            cps = [pltpu.make_async_copy(w_hbm[j], w_vm[j], sem.at[j]) for j in range(9)]
            for cp in cps:
                cp.start()
            for cp in cps:
                cp.wait()

        xv = x_ref[...]
        r = lax.rsqrt(jnp.mean(xv * xv, axis=-1, keepdims=True) + EPS)
        h = xv * r * (nw_ref[...] * (1.0 + sc_ref[...])) + sh_ref[...]
        hb = _bf(h)
        h_ref[...] = hb
        for j in range(9):
            outs[j][...] = _dot(hb, w_vm[j][...])

    vec = _full((1, D_MODEL))
    return pl.pallas_call(
        body, name="inproj", grid=(s // tm,),
        in_specs=[_rows(tm, D_MODEL), vec, vec, vec] + [ANY] * 9,
        out_specs=[_rows(tm, w) for w in SEG_W] + [_rows(tm, D_MODEL)],
        out_shape=[jax.ShapeDtypeStruct((s, w), F32) for w in SEG_W] + [jax.ShapeDtypeStruct((s, D_MODEL), BF)],
        scratch_shapes=[pltpu.VMEM((D_MODEL, w), BF) for w in SEG_W] + [pltpu.SemaphoreType.DMA((9,))],
        compiler_params=_params(dimension_semantics=("arbitrary",)),
    )(x, norm_w, scale, shift, *ws)


def _bucket_onehot_t():
    qi = jnp.arange(BLOCK)[:, None]
    kj = jnp.arange(2 * BLOCK)[None, :]
    dist = qi + BLOCK - kj
    n = jnp.maximum(dist, 0)
    max_exact = REL_BUCKETS // 2
    nf = jnp.maximum(n, 1).astype(F32)
    large = max_exact + (jnp.log(nf / max_exact) / math.log(REL_MAX_DIST / max_exact)
                         * (REL_BUCKETS - max_exact)).astype(jnp.int32)
    large = jnp.minimum(large, REL_BUCKETS - 1)
    bucket = jnp.where(n < max_exact, n, large).reshape(1, BLOCK * 2 * BLOCK)
    return (bucket == jnp.arange(REL_BUCKETS)[:, None]).astype(F32)


def _bias_dense(rel_bias_t, oh_t):
    def body(rb_ref, oh_ref, o_ref):
        o_ref[...] = _dot(rb_ref[...], oh_ref[...], HI)

    return pl.pallas_call(
        body, name="bias_dense", out_shape=jax.ShapeDtypeStruct((ATTN_HEADS, BLOCK * 2 * BLOCK), F32),
        compiler_params=_params(),
    )(rel_bias_t, oh_t)


def _bias_grad(ds_sum, oh_t):
    def body(ds_ref, oh_ref, o_ref):
        o_ref[...] = _dot_nt(ds_ref[...], oh_ref[...], HI)

    return pl.pallas_call(
        body, name="bias_grad", out_shape=jax.ShapeDtypeStruct((ATTN_HEADS, REL_BUCKETS), F32),
        compiler_params=_params(),
    )(ds_sum, oh_t)


def _group_sum(a, e):
    hi = _bf(a)
    return _dot(hi, e) + _dot(_bf(a - hi.astype(F32)), e)


def _group_bcast(a, e3t):
    hi = _bf(a)
    r1 = a - hi.astype(F32)
    mid = _bf(r1)
    return _dot(jnp.concatenate([hi, mid, _bf(r1 - mid.astype(F32))], axis=1), e3t)


def _membership(width, group, ngroups):
    e = (jnp.arange(width)[:, None] // group == jnp.arange(ngroups)[None, :]).astype(BF)
    return e, jnp.tile(e.T, (3, 1))


def _fold(width, group):
    return (jnp.arange(width)[:, None] % group == jnp.arange(group)[None, :]).astype(BF)


def _heads_norm(t, w_x, e, e3t):
    r = lax.rsqrt(_group_sum(t * t, e) * (1.0 / HEAD_DIM) + EPS)
    r_x = _group_bcast(r, e3t)
    return t * r_x * w_x, r_x


def _heads_norm_bwd(t, r_x, w_x, d, e, e3t):
    wd = d * w_x
    corr = _group_bcast(_group_sum(t * wd, e) * (1.0 / HEAD_DIM), e3t)
    return r_x * wd - t * (r_x * r_x * r_x) * corr, jnp.sum(d * t * r_x, axis=0, keepdims=True)


def _stack_heads(a, hk):
    return jnp.concatenate([a[:, (hk * GRP + g) * HEAD_DIM:(hk * GRP + g + 1) * HEAD_DIM] for g in range(GRP)], axis=0)


def _stack_cols(a, hk):
    return jnp.concatenate([a[:, hk * GRP + g:hk * GRP + g + 1] for g in range(GRP)], axis=0)


def _window_mask(first):
    qi = jnp.bitwise_and(lax.broadcasted_iota(jnp.int32, (GRP * BLOCK, 2 * BLOCK), 0), BLOCK - 1)
    kj = lax.broadcasted_iota(jnp.int32, (GRP * BLOCK, 2 * BLOCK), 1)
    prev_ok = jnp.logical_and(kj > qi, jnp.logical_not(first))
    cur_ok = jnp.logical_and(kj >= BLOCK, kj - BLOCK <= qi)
    return jnp.logical_or(jnp.logical_and(kj < BLOCK, prev_ok), cur_ok)


def _attn_consts(qnw, knw):
    eq, eq3t = _membership(ATTN_W, HEAD_DIM, ATTN_HEADS)
    ek, ek3t = _membership(KV_W, HEAD_DIM, ATTN_HEADS)
    return (jnp.tile(qnw, (1, ATTN_HEADS)), jnp.tile(knw, (1, KV_HEADS)), eq, eq3t, ek, ek3t)


def _attn_fwd(q, k, v, bias, sinks, consts):
    s = q.shape[0]
    nb = s // BLOCK

    def body(q_ref, kp_ref, kc_ref, vp_ref, vc_ref, b_ref, sk_ref, qw_ref, kw_ref, eq_ref, eq3_ref, ek_ref, ek3_ref,
             o_ref, lse_ref):
        i = pl.program_id(0)
        mask = _window_mask(i == 0)
        qn = _bf(_heads_norm(q_ref[...], qw_ref[...], eq_ref[...], eq3_ref[...])[0])
        kn = _bf(_heads_norm(jnp.concatenate([kp_ref[...], kc_ref[...]], axis=0), kw_ref[...], ek_ref[...], ek3_ref[...])[0])
        vv = _bf(jnp.concatenate([vp_ref[...], vc_ref[...]], axis=0))
        lses = []
        for hk in range(KV_HEADS):
            ks = slice(hk * HEAD_DIM, (hk + 1) * HEAD_DIM)
            sc = _dot_nt(_stack_heads(qn, hk), kn[:, ks]) * (HEAD_DIM ** -0.5)
            sc = sc + b_ref[hk * GRP:(hk + 1) * GRP].reshape(GRP * BLOCK, 2 * BLOCK)
            sc = jnp.where(mask, sc, -1e30)
            sink = jnp.concatenate([jnp.full((BLOCK, 1), sk_ref[0, hk * GRP + g], F32) for g in range(GRP)], axis=0)
            m = jnp.maximum(jnp.max(sc, axis=-1, keepdims=True), sink)
            p = jnp.exp(sc - m)
            den = jnp.sum(p, axis=-1, keepdims=True) + jnp.exp(sink - m)
            out = _dot(_bf(p), vv[:, ks]) / den
            lse = m + jnp.log(den)
            for g in range(GRP):
                h = hk * GRP + g
                o_ref[:, h * HEAD_DIM:(h + 1) * HEAD_DIM] = out[g * BLOCK:(g + 1) * BLOCK]
                lses.append(lse[g * BLOCK:(g + 1) * BLOCK])
        lse_ref[...] = jnp.concatenate(lses, axis=1)

    cur = lambda w: pl.BlockSpec((BLOCK, w), lambda i: (i, 0))
    prev = lambda w: pl.BlockSpec((BLOCK, w), lambda i: (jnp.maximum(i - 1, 0), 0))
    return pl.pallas_call(
        body, name="attn_fwd", grid=(nb,),
        in_specs=[cur(ATTN_W), prev(KV_W), cur(KV_W), prev(KV_W), cur(KV_W),
                  pl.BlockSpec((ATTN_HEADS, BLOCK, 2 * BLOCK), lambda i: (0, 0, 0)),
                  pl.BlockSpec(memory_space=pltpu.SMEM)] + [_full(c.shape) for c in consts],
        out_specs=[cur(ATTN_W), cur(ATTN_HEADS)],
        out_shape=[jax.ShapeDtypeStruct((s, ATTN_W), F32), jax.ShapeDtypeStruct((s, ATTN_HEADS), F32)],
        compiler_params=_params(dimension_semantics=("arbitrary",)),
    )(q, k, k, v, v, bias, sinks, *consts)


def _conv_taps(ext_ref, xbc, tail):
    ext_ref[0:8, :] = tail
    ext_ref[8:8 + BLOCK, :] = xbc
    return [ext_ref[5 + j:5 + j + BLOCK, :] for j in range(CONV_K)]


def _softplus(u):
    return jnp.maximum(u, 0.0) + jnp.log(1.0 + jnp.exp(-jnp.abs(u)))


def _tril():
    r = lax.broadcasted_iota(jnp.int32, (BLOCK, BLOCK), 0)
    c = lax.broadcasted_iota(jnp.int32, (BLOCK, BLOCK), 1)
    return r >= c


def _triu():
    r = lax.broadcasted_iota(jnp.int32, (BLOCK, BLOCK), 0)
    c = lax.broadcasted_iota(jnp.int32, (BLOCK, BLOCK), 1)
    return r <= c


def _exact_left(m01, a):
    hi = _bf(a)
    r1 = a - hi.astype(F32)
    mid = _bf(r1)
    return _dot(m01, hi) + _dot(m01, mid) + _dot(m01, _bf(r1 - mid.astype(F32)))


def _ssd_common(conv, dtr_ref, dtb_ref, alog_ref, e3_ref):
    sg = _sig(conv)
    xact = conv * sg
    u = dtr_ref[...] + dtb_ref[...]
    dt = _softplus(u)
    a = -jnp.exp(alog_ref[...])
    trilb = _tril()
    acum = _exact_left(trilb.astype(BF), dt * a)
    both = _group_bcast(jnp.concatenate([dt, acum], axis=0), e3_ref[...])
    dt_x, acum_x = both[:BLOCK], both[BLOCK:]
    return sg, xact, u, dt, a, trilb, acum, dt_x, acum_x


def _ssd_fwd(xbc, dt_raw, conv_w, conv_b, dt_bias, a_log, dsk_x, e3t):
    s = xbc.shape[0]
    nc = s // BLOCK

    def body(x_ref, tail_ref, dtr_ref, cw_ref, cb_ref, dtb_ref, alog_ref, dsk_ref, e3_ref,
             y_ref, hp_ref, conv_ref, hst, ext):
        i = pl.program_id(0)

        @pl.when(i == 0)
        def _():
            hst[...] = jnp.zeros_like(hst)

        tail = jnp.where(i > 0, tail_ref[...], 0.0)
        taps = _conv_taps(ext, x_ref[...], tail)
        conv = cb_ref[...] + sum(taps[j] * cw_ref[j:j + 1, :] for j in range(CONV_K))
        conv_ref[...] = conv
        _, xact, _, _, _, trilb, acum, dt_x, acum_x = _ssd_common(conv, dtr_ref, dtb_ref, alog_ref, e3_ref)
        xs = xact[:, :SSM_W]
        acum_t = acum.T
        ea_x = jnp.exp(acum_x)
        last_x = acum_x[BLOCK - 1:BLOCK, :]
        xdt = xs * dt_x
        xw = xdt * jnp.exp(last_x - acum_x)
        cd_x = jnp.exp(last_x)
        hprev = hst[...]
        hp_ref[0] = hprev
        dsk = dsk_ref[...]
        for g in range(SSM_G):
            bg = _bf(xact[:, SSM_W + g * SSM_N:SSM_W + (g + 1) * SSM_N])
            cg = _bf(xact[:, SSM_W + SSM_G * SSM_N + g * SSM_N:SSM_W + SSM_G * SSM_N + (g + 1) * SSM_N])
            sl = slice(g * SSM_R * SSM_P, (g + 1) * SSM_R * SSM_P)
            cb = _dot_nt(cg, bg)
            yoff = _dot(cg, _bf(hprev[:, sl])) * ea_x[:, sl]
            hst[:, sl] = hprev[:, sl] * cd_x[:, sl] + _dot_tn(bg, _bf(xw[:, sl]))
            for r in range(SSM_R):
                hh = g * SSM_R + r
                hs = slice(hh * SSM_P, (hh + 1) * SSM_P)
                seg = jnp.where(trilb, acum[:, hh:hh + 1] - acum_t[hh:hh + 1, :], -1e30)
                mm = cb * jnp.exp(seg)
                yd = _dot(_bf(mm), _bf(xdt[:, hs]))
                y_ref[:, hs] = yd + yoff[:, r * SSM_P:(r + 1) * SSM_P] + dsk[:, hs] * xs[:, hs]

    chunk = lambda w: pl.BlockSpec((BLOCK, w), lambda i: (i, 0))
    return pl.pallas_call(
        body, name="ssd_fwd", grid=(nc,),
        in_specs=[chunk(XBC_W), pl.BlockSpec((8, XBC_W), lambda i: (jnp.maximum(i * (BLOCK // 8) - 1, 0), 0)),
                  chunk(SSM_HEADS), _full((CONV_K, XBC_W)), _full((1, XBC_W)), _full((1, SSM_HEADS)),
                  _full((1, SSM_HEADS)), _full((1, SSM_W)), _full((3 * SSM_HEADS, SSM_W))],
        out_specs=[chunk(SSM_W), pl.BlockSpec((1, SSM_N, SSM_W), lambda i: (i, 0, 0)), chunk(XBC_W)],
        out_shape=[jax.ShapeDtypeStruct((s, SSM_W), F32), jax.ShapeDtypeStruct((nc, SSM_N, SSM_W), F32),
                   jax.ShapeDtypeStruct((s, XBC_W), F32)],
        scratch_shapes=[pltpu.VMEM((SSM_N, SSM_W), F32), pltpu.VMEM((BLOCK + 8, XBC_W), F32)],
        compiler_params=_params(dimension_semantics=("arbitrary",)),
    )(xbc, xbc, dt_raw, conv_w, conv_b, dt_bias, a_log, dsk_x, e3t)


def _dsilu(z, sg):
    return sg * (1.0 + z * (1.0 - sg))


def _mid(x, tgt, o_att, za, ypre, zm, ga, gb, gate, ssm_nw, wap, wsp, wout, tm=128):
    s = x.shape[0]
    gw = SSM_W // SSM_G

    def body(x_ref, t_ref, o_ref, za_ref, yp_ref, zm_ref, ga_ref, gb_ref, gate_ref, nw_ref, wap_h, wsp_h, wout_h,
             dout_ref, do_ref, dza_ref, dyp_ref, dzm_ref, dga_ref, dgb_ref, gwout_h, gwap_h, gwsp_h,
             gnw_ref, dgate_ref, loss_ref, wap_v, wsp_v, wout_v, a_out, a_ap, a_sp, sem):
        i = pl.program_id(0)

        @pl.when(i == 0)
        def _():
            cps = [pltpu.make_async_copy(a, b, sem.at[j])
                   for j, (a, b) in enumerate(((wap_h, wap_v), (wsp_h, wsp_v), (wout_h, wout_v)))]
            for cp in cps:
                cp.start()
            a_out[...] = jnp.zeros_like(a_out)
            a_ap[...] = jnp.zeros_like(a_ap)
            a_sp[...] = jnp.zeros_like(a_sp)
            gnw_ref[...] = jnp.zeros_like(gnw_ref)
            dgate_ref[...] = jnp.zeros_like(dgate_ref)
            loss_ref[...] = jnp.zeros_like(loss_ref)
            for cp in cps:
                cp.wait()

        gate = gate_ref[...]
        nw = nw_ref[...]
        o_att = o_ref[...]
        z_a = za_ref[...]
        s_a = _sig(z_a)
        silu_a = z_a * s_a
        yag = _bf(o_att * silu_a)
        y_a = _dot(yag, wap_v[...])
        ypre = yp_ref[...]
        z_m = zm_ref[...]
        s_m = _sig(z_m)
        silu_m = z_m * s_m
        yg = ypre * silu_m
        rinv = jnp.concatenate(
            [jnp.broadcast_to(lax.rsqrt(jnp.mean(yg[:, g * gw:(g + 1) * gw] ** 2, axis=-1, keepdims=True) + EPS), (tm, gw))
             for g in range(SSM_G)], axis=1)
        ynr = yg * rinv
        yn = _bf(ynr * nw)
        y_b = _dot(yn, wsp_v[...])
        g_a = _sig(ga_ref[...])
        g_b = _sig(gb_ref[...])
        merged = _bf(g_a * y_a + g_b * y_b)
        o = _dot(merged, wout_v[...])
        diff = x_ref[...] + gate * o - t_ref[...]
        loss_ref[...] += (0.5 / D_MODEL) * jnp.sum(diff * diff, axis=(0, 1), keepdims=True)
        dout = diff * (1.0 / D_MODEL)
        dout_ref[...] = dout
        dgate_ref[...] += jnp.sum(dout * o, axis=0, keepdims=True)
        d_o = _bf(dout * gate)
        dmerged = _dot_nt(d_o, wout_v[...])
        a_out[...] += _dot_tn(merged, d_o)
        dy_a = dmerged * g_a
        dy_b = dmerged * g_b
        dga_ref[...] = _bf(dy_a * y_a * (1.0 - g_a))
        dgb_ref[...] = _bf(dy_b * y_b * (1.0 - g_b))
        dy_a = _bf(dy_a)
        dy_b = _bf(dy_b)
        a_ap[...] += _dot_tn(yag, dy_a)
        dyag = _dot_nt(dy_a, wap_v[...])
        do_ref[...] = dyag * silu_a
        dza_ref[...] = _bf(dyag * o_att * _dsilu(z_a, s_a))
        a_sp[...] += _dot_tn(yn, dy_b)
        dyn = _dot_nt(dy_b, wsp_v[...])
        gnw_ref[...] += jnp.sum(dyn * ynr, axis=0, keepdims=True)
        dynw = dyn * nw
        corr = jnp.concatenate(
            [jnp.broadcast_to(jnp.mean((dynw * ynr)[:, g * gw:(g + 1) * gw], axis=-1, keepdims=True), (tm, gw))
             for g in range(SSM_G)], axis=1)
        dyg = rinv * (dynw - ynr * corr)
        dyp_ref[...] = dyg * silu_m
        dzm_ref[...] = _bf(dyg * ypre * _dsilu(z_m, s_m))

        @pl.when(i == pl.num_programs(0) - 1)
        def _():
            cps = [pltpu.make_async_copy(a, b, sem.at[j])
                   for j, (a, b) in enumerate(((a_out, gwout_h), (a_ap, gwap_h), (a_sp, gwsp_h)))]
            for cp in cps:
                cp.start()
            for cp in cps:
                cp.wait()

    r1, r2 = _rows(tm, D_MODEL), _rows(tm, SSM_W)
    sd = jax.ShapeDtypeStruct
    return pl.pallas_call(
        body, name="mid", grid=(s // tm,),
        in_specs=[r1, r1, r1, r1, r2, r2, r1, r1, _full((1, D_MODEL)), _full((1, SSM_W)), ANY, ANY, ANY],
        out_specs=[r1, r1, r1, r2, r2, r1, r1, ANY, ANY, ANY, _full((1, SSM_W)), _full((1, D_MODEL)), _full((1, 1))],
        out_shape=[sd((s, D_MODEL), F32), sd((s, ATTN_W), F32), sd((s, ATTN_W), BF), sd((s, SSM_W), F32),
                   sd((s, SSM_W), BF), sd((s, D_MODEL), BF), sd((s, D_MODEL), BF),
                   sd((D_MODEL, D_MODEL), F32), sd((ATTN_W, D_MODEL), F32), sd((SSM_W, D_MODEL), F32),
                   sd((1, SSM_W), F32), sd((1, D_MODEL), F32), sd((1, 1), F32)],
        scratch_shapes=[pltpu.VMEM((ATTN_W, D_MODEL), BF), pltpu.VMEM((SSM_W, D_MODEL), BF), pltpu.VMEM((D_MODEL, D_MODEL), BF),
                        pltpu.VMEM((D_MODEL, D_MODEL), F32), pltpu.VMEM((ATTN_W, D_MODEL), F32),
                        pltpu.VMEM((SSM_W, D_MODEL), F32), pltpu.SemaphoreType.DMA((3,))],
        compiler_params=_params(dimension_semantics=("arbitrary",)),
    )(x, tgt, o_att, za, ypre, zm, ga, gb, gate, ssm_nw, wap, wsp, wout)


def _attn_bwd(q, k, v, bias, sinks, consts, o_att, lse, d_o):
    s = q.shape[0]
    nb = s // BLOCK
    folds = (_fold(ATTN_W, HEAD_DIM), _fold(KV_W, HEAD_DIM))

    def body(q_ref, kp_ref, kc_ref, vp_ref, vc_ref, b_ref, skv_ref, qw_ref, kw_ref, eq_ref, eq3_ref, ek_ref, ek3_ref,
             fq_ref, fk_ref, o_ref, lse_ref, do_ref,
             dq_ref, dk_ref, dv_ref, dss_ref, gqw_ref, gkw_ref, gsk_ref, ckn, cv, dqn_s, dkn_s, dv_s, gq_x, gk_x):
        i = pl.program_id(0)
        kw, ek, ek3 = kw_ref[...], ek_ref[...], ek3_ref[...]

        @pl.when(i == 0)
        def _():
            for ref in (ckn, cv, dss_ref, gq_x, gk_x, gsk_ref):
                ref[...] = jnp.zeros_like(ref)

        @pl.when(i < nb)
        def _():
            mask = _window_mask(i == 0)
            qw, eq, eq3 = qw_ref[...], eq_ref[...], eq3_ref[...]
            qf = q_ref[...]
            qnf, rq_x = _heads_norm(qf, qw, eq, eq3)
            qn = _bf(qnf)
            kf = jnp.concatenate([kp_ref[...], kc_ref[...]], axis=0)
            knf, rk_x = _heads_norm(kf, kw, ek, ek3)
            kn = _bf(knf)
            vv = _bf(jnp.concatenate([vp_ref[...], vc_ref[...]], axis=0))
            d_of = do_ref[...]
            d_ob = _bf(d_of)
            lse_all = lse_ref[...]
            delta = _group_sum(d_of * o_ref[...], eq)
            gsk_ref[...] += jnp.sum(-jnp.exp(skv_ref[...] - lse_all) * delta, axis=0, keepdims=True)
            for hk in range(KV_HEADS):
                ks = slice(hk * HEAD_DIM, (hk + 1) * HEAD_DIM)
                qg = _stack_heads(qn, hk)
                sc = _dot_nt(qg, kn[:, ks]) * (HEAD_DIM ** -0.5)
                sc = sc + b_ref[hk * GRP:(hk + 1) * GRP].reshape(GRP * BLOCK, 2 * BLOCK)
                p = jnp.where(mask, jnp.exp(sc - _stack_cols(lse_all, hk)), 0.0)
                d_og = _stack_heads(d_ob, hk)
                ds = p * (_dot_nt(d_og, vv[:, ks]) - _stack_cols(delta, hk))
                dss_ref[hk * GRP:(hk + 1) * GRP] += ds.reshape(GRP, BLOCK, 2 * BLOCK)
                dsb = _bf(ds)
                dv_s[:, ks] = _dot_tn(_bf(p), d_og)
                dkn_s[:, ks] = _dot_tn(dsb, qg) * (HEAD_DIM ** -0.5)
                dqn = _dot(dsb, kn[:, ks]) * (HEAD_DIM ** -0.5)
                for g in range(GRP):
                    h = hk * GRP + g
                    dqn_s[:, h * HEAD_DIM:(h + 1) * HEAD_DIM] = dqn[g * BLOCK:(g + 1) * BLOCK]
            dq, gq = _heads_norm_bwd(qf, rq_x, qw, dqn_s[...], eq, eq3)
            dq_ref[...] = _bf(dq)
            gq_x[...] += gq
            dk, gk = _heads_norm_bwd(kf[:BLOCK], rk_x[:BLOCK], kw, ckn[...] + dkn_s[0:BLOCK, :], ek, ek3)
            dk_ref[...] = _bf(dk)
            gk_x[...] += gk
            dv_ref[...] = _bf(cv[...] + dv_s[0:BLOCK, :])
            ckn[...] = dkn_s[BLOCK:2 * BLOCK, :]
            cv[...] = dv_s[BLOCK:2 * BLOCK, :]

        @pl.when(i == nb)
        def _():
            kc = kc_ref[...]
            dk, gk = _heads_norm_bwd(kc, _heads_norm(kc, kw, ek, ek3)[1], kw, ckn[...], ek, ek3)
            dk_ref[...] = _bf(dk)
            dv_ref[...] = _bf(cv[...])
            gqw_ref[...] = _group_sum(jnp.broadcast_to(gq_x[...], (8, ATTN_W)), fq_ref[...])[0:1]
            gkw_ref[...] = _group_sum(jnp.broadcast_to(gk_x[...] + gk, (8, KV_W)), fk_ref[...])[0:1]

    last = nb - 1
    cur = lambda w: pl.BlockSpec((BLOCK, w), lambda i: (jnp.minimum(i, last), 0))
    prev = lambda w: pl.BlockSpec((BLOCK, w), lambda i: (jnp.maximum(jnp.minimum(i, last) - 1, 0), 0))
    late = lambda w: pl.BlockSpec((BLOCK, w), lambda i: (jnp.maximum(i - 1, 0), 0))
    sd = jax.ShapeDtypeStruct
    return pl.pallas_call(
        body, name="attn_bwd", grid=(nb + 1,),
        in_specs=[cur(ATTN_W), prev(KV_W), cur(KV_W), prev(KV_W), cur(KV_W),
                  pl.BlockSpec((ATTN_HEADS, BLOCK, 2 * BLOCK), lambda i: (0, 0, 0)), _full((1, ATTN_HEADS))]
                 + [_full(c.shape) for c in consts + folds] + [cur(ATTN_W), cur(ATTN_HEADS), cur(ATTN_W)],
        out_specs=[cur(ATTN_W), late(KV_W), late(KV_W),
                   pl.BlockSpec((ATTN_HEADS, BLOCK, 2 * BLOCK), lambda i: (0, 0, 0)),
                   _full((1, HEAD_DIM)), _full((1, HEAD_DIM)), _full((1, ATTN_HEADS))],
        out_shape=[sd((s, ATTN_W), BF), sd((s, KV_W), BF), sd((s, KV_W), BF),
                   sd((ATTN_HEADS, BLOCK, 2 * BLOCK), F32), sd((1, HEAD_DIM), F32), sd((1, HEAD_DIM), F32),
                   sd((1, ATTN_HEADS), F32)],
        scratch_shapes=[pltpu.VMEM((BLOCK, KV_W), F32), pltpu.VMEM((BLOCK, KV_W), F32),
                        pltpu.VMEM((BLOCK, ATTN_W), F32), pltpu.VMEM((2 * BLOCK, KV_W), F32),
                        pltpu.VMEM((2 * BLOCK, KV_W), F32), pltpu.VMEM((1, ATTN_W), F32), pltpu.VMEM((1, KV_W), F32)],
        compiler_params=_params(dimension_semantics=("arbitrary",)),
    )(q, k, k, v, v, bias, sinks, *consts, *folds, o_att, lse, d_o)


def _ssd_bwd(xbc, conv_all, dt_raw, conv_w, dt_bias, a_log, dsk_x, e_mat, e3t, hprev_all, dy_all):
    s = xbc.shape[0]
    nc = s // BLOCK
    gw = SSM_R * SSM_P
    b0, c0 = SSM_W, SSM_W + SSM_G * SSM_N

    def body(x_ref, conv_ref, dtr_ref, cw_ref, dtb_ref, alog_ref, dsk_ref, e_ref, e3_ref, hp_ref, dy_ref,
             dx_ref, ddt_ref, gcw_ref, gcb_ref, gdtb_ref, galog_ref, gdsk_ref,
             dh, nhead, ext2, gdskx, dxdt_s, dbc_s):
        i = pl.program_id(0)
        c = nc - 1 - i

        @pl.when(i == 0)
        def _():
            for ref in (dh, nhead, gdskx, gcw_ref, gcb_ref, gdtb_ref, galog_ref, gdsk_ref):
                ref[...] = jnp.zeros_like(ref)

        conv = conv_ref[...]
        sg, xact, u, dt, a, trilb, acum, dt_x, acum_x = _ssd_common(conv, dtr_ref, dtb_ref, alog_ref, e3_ref)
        xs = xact[:, :SSM_W]
        acum_t = acum.T
        ea_x = jnp.exp(acum_x)
        last_x = acum_x[BLOCK - 1:BLOCK, :]
        dte_x = jnp.exp(last_x - acum_x)
        cd_x = jnp.exp(last_x)
        xdt = xs * dt_x
        xw = xdt * dte_x
        hprev = hp_ref[0]
        dhn = dh[...]
        dy = dy_ref[...]
        gdskx[...] += jnp.sum(dy * xs, axis=0, keepdims=True)
        dyea = dy * ea_x
        lane = lax.broadcasted_iota(jnp.int32, (BLOCK, SSM_HEADS), 1)
        dacum = jnp.zeros((BLOCK, SSM_HEADS), F32)
        dacc_x, dlast_x = [], []
        for g in range(SSM_G):
            bgf = xact[:, b0 + g * SSM_N:b0 + (g + 1) * SSM_N]
            cgf = xact[:, c0 + g * SSM_N:c0 + (g + 1) * SSM_N]
            bg, cg = _bf(bgf), _bf(cgf)
            sl = slice(g * gw, (g + 1) * gw)
            hpg, dhg, dyeag = _bf(hprev[:, sl]), _bf(dhn[:, sl]), _bf(dyea[:, sl])
            cb = _dot_nt(cg, bg)
            gmat = _dot(cg, hpg)
            dxw = _dot(bg, dhg)
            dxdt_s[:, sl] = dxw * dte_x[:, sl]
            dacc_x.append(dy[:, sl] * gmat * ea_x[:, sl] - dxw * xw[:, sl])
            dlast_x.append(jnp.sum(dxw * xw[:, sl], axis=0, keepdims=True)
                           + jnp.sum(dhn[:, sl] * hprev[:, sl], axis=0, keepdims=True) * cd_x[:, sl])
            dcg = _dot_nt(dyeag, hpg)
            dbg = _dot_nt(_bf(xw[:, sl]), dhg)
            dh[:, sl] = dhn[:, sl] * cd_x[:, sl] + _dot_tn(cg, dyeag)
            dcb = jnp.zeros((BLOCK, BLOCK), F32)
            for r in range(SSM_R):
                hh = g * SSM_R + r
                hs = slice(hh * SSM_P, (hh + 1) * SSM_P)
                seg = jnp.where(trilb, acum[:, hh:hh + 1] - acum_t[hh:hh + 1, :], -1e30)
                lm = jnp.exp(seg)
                mm = cb * lm
                dyh = _bf(dy[:, hs])
                dm = _dot_nt(dyh, _bf(xdt[:, hs]))
                dxdt_s[:, hs] += _dot_tn(_bf(mm), dyh)
                wm = dm * mm
                dcb = dcb + dm * lm
                dacum = dacum + _group_sum(wm - wm.T, (lane == hh).astype(BF))
            dcbb = _bf(dcb)
            dbc_s[:, g * SSM_N:(g + 1) * SSM_N] = dbg + _dot_tn(dcbb, cg)
            dbc_s[:, SSM_G * SSM_N + g * SSM_N:SSM_G * SSM_N + (g + 1) * SSM_N] = dcg + _dot(dcbb, bg)
        dxdt = dxdt_s[...]
        dxs = dy * dsk_ref[...] + dxdt * dt_x
        red = _group_sum(jnp.concatenate(
            [dxdt * xs, jnp.concatenate(dacc_x, axis=1),
             jnp.broadcast_to(jnp.concatenate(dlast_x, axis=1), (8, SSM_W))], axis=0), e_ref[...])
        row = lax.broadcasted_iota(jnp.int32, (BLOCK, SSM_HEADS), 0)
        dacum = dacum + red[BLOCK:2 * BLOCK] + jnp.where(row == BLOCK - 1, red[2 * BLOCK:2 * BLOCK + 1], 0.0)
        ddta = _exact_left(_triu().astype(BF), dacum)
        ddt = red[:BLOCK] + ddta * a
        galog_ref[...] += jnp.sum(ddta * dt, axis=0, keepdims=True) * a
        du = ddt * _sig(u)
        ddt_ref[...] = _bf(du)
        gdtb_ref[...] += jnp.sum(du, axis=0, keepdims=True)
        dconv = jnp.concatenate([dxs, dbc_s[...]], axis=1) * _dsilu(conv, sg)
        gcb_ref[...] += jnp.sum(dconv, axis=0, keepdims=True)
        ext2[0:BLOCK, :] = dconv
        ext2[BLOCK:BLOCK + 8, :] = nhead[...]
        ahead = [ext2[3 - j:3 - j + BLOCK, :] for j in range(CONV_K)]
        dx_ref[...] = _bf(sum(ahead[j] * cw_ref[j:j + 1, :] for j in range(CONV_K)))
        xraw = x_ref[...]
        gcw_ref[...] += jnp.concatenate([jnp.sum(ahead[j] * xraw, axis=0, keepdims=True) for j in range(CONV_K)], axis=0)
        nhead[...] = dconv[0:8]

        @pl.when(i == nc - 1)
        def _():
            gdsk_ref[...] = _group_sum(jnp.broadcast_to(gdskx[...], (8, SSM_W)), e_ref[...])[0:1]

    chunk = lambda w: pl.BlockSpec((BLOCK, w), lambda i: (nc - 1 - i, 0))
    sd = jax.ShapeDtypeStruct
    return pl.pallas_call(
        body, name="ssd_bwd", grid=(nc,),
        in_specs=[chunk(XBC_W), chunk(XBC_W),
                  chunk(SSM_HEADS), _full((CONV_K, XBC_W)), _full((1, SSM_HEADS)),
                  _full((1, SSM_HEADS)), _full((1, SSM_W)), _full((SSM_W, SSM_HEADS)), _full((3 * SSM_HEADS, SSM_W)),
                  pl.BlockSpec((1, SSM_N, SSM_W), lambda i: (nc - 1 - i, 0, 0)), chunk(SSM_W)],
        out_specs=[chunk(XBC_W), chunk(SSM_HEADS), _full((CONV_K, XBC_W)), _full((1, XBC_W)),
                   _full((1, SSM_HEADS)), _full((1, SSM_HEADS)), _full((1, SSM_HEADS))],
        out_shape=[sd((s, XBC_W), BF), sd((s, SSM_HEADS), BF), sd((CONV_K, XBC_W), F32), sd((1, XBC_W), F32),
                   sd((1, SSM_HEADS), F32), sd((1, SSM_HEADS), F32), sd((1, SSM_HEADS), F32)],
        scratch_shapes=[pltpu.VMEM((SSM_N, SSM_W), F32), pltpu.VMEM((8, XBC_W), F32),
                        pltpu.VMEM((BLOCK + 8, XBC_W), F32),
                        pltpu.VMEM((1, SSM_W), F32), pltpu.VMEM((BLOCK, SSM_W), F32),
                        pltpu.VMEM((BLOCK, 2 * SSM_G * SSM_N), F32)],
        compiler_params=_params(dimension_semantics=("arbitrary",)),
    )(xbc, conv_all, dt_raw, conv_w, dt_bias, a_log, dsk_x, e_mat, e3t, hprev_all, dy_all)


def _dh(x, dout, norm_w, scale, dsegs, ws, tm=256):
    s = x.shape[0]

    def body(x_ref, dout_ref, nw_ref, sc_ref, *rest):
        d_refs, w_hbm = rest[:9], rest[9:18]
        gx_ref, dshift_ref, dscale_ref, gnw_ref = rest[18:22]
        w_vm, sem = rest[22:31], rest[31]

        @pl.when(pl.program_id(0) == 0)
        def _():
            cps = [pltpu.make_async_copy(w_hbm[j], w_vm[j], sem.at[j]) for j in range(9)]
            for cp in cps:
                cp.start()
            for ref in (dshift_ref, dscale_ref, gnw_ref):
                ref[...] = jnp.zeros_like(ref)
            for cp in cps:
                cp.wait()

        dh = _dot_nt(d_refs[0][...], w_vm[0][...])
        for j in range(1, 9):
            dh = dh + _dot_nt(d_refs[j][...], w_vm[j][...])
        xv = x_ref[...]
        r = lax.rsqrt(jnp.mean(xv * xv, axis=-1, keepdims=True) + EPS)
        xn = xv * r
        nw = nw_ref[...]
        sc1 = 1.0 + sc_ref[...]
        dshift_ref[...] += jnp.sum(dh, axis=0, keepdims=True)
        dhxn = jnp.sum(dh * xn, axis=0, keepdims=True)
        dscale_ref[...] += dhxn * nw
        gnw_ref[...] += dhxn * sc1
        dxn = dh * (nw * sc1)
        gx_ref[...] = dout_ref[...] + r * (dxn - xn * jnp.mean(xn * dxn, axis=-1, keepdims=True))

    vec = _full((1, D_MODEL))
    sd = jax.ShapeDtypeStruct
    return pl.pallas_call(
        body, name="dh", grid=(s // tm,),
        in_specs=[_rows(tm, D_MODEL), _rows(tm, D_MODEL), vec, vec] + [_rows(tm, w) for w in SEG_W] + [ANY] * 9,
        out_specs=[_rows(tm, D_MODEL), vec, vec, vec],
        out_shape=[sd((s, D_MODEL), F32), sd((1, D_MODEL), F32), sd((1, D_MODEL), F32), sd((1, D_MODEL), F32)],
        scratch_shapes=[pltpu.VMEM((D_MODEL, w), BF) for w in SEG_W] + [pltpu.SemaphoreType.DMA((9,))],
        compiler_params=_params(dimension_semantics=("arbitrary",)),
    )(x, dout, norm_w, scale, *dsegs, *ws)


def _gw_seg(h_t, dseg, name, tm=512):
    s, w = dseg.shape
    tn = min(w, 1024)
    tm = min(tm, s)
    nm = s // tm

    def body(h_ref, d_ref, o_ref):
        @pl.when(pl.program_id(1) == 0)
        def _():
            o_ref[...] = jnp.zeros_like(o_ref)

        o_ref[...] += _dot(h_ref[...], d_ref[...])

    return pl.pallas_call(
        body, name=name, grid=(w // tn, nm),
        in_specs=[pl.BlockSpec((D_MODEL, tm), lambda n, m: (0, m)), pl.BlockSpec((tm, tn), lambda n, m: (m, n))],
        out_specs=pl.BlockSpec((D_MODEL, tn), lambda n, m: (0, n)),
        out_shape=jax.ShapeDtypeStruct((D_MODEL, w), F32),
        compiler_params=_params(dimension_semantics=("arbitrary", "arbitrary")),
    )(h_t, dseg)


def _gw_in(h, dsegs):
    h_t = h.T
    return [_gw_seg(h_t, d, "gw_in_%d" % j) for j, d in enumerate(dsegs)]


def _local_step(x, tgt, shift, scale, gate, ws, rows_fn, norm_w, qnw, knw, rel_bias, sinks,
                conv_w, conv_b, dt_bias, a_log, d_skip, ssm_nw, after_mid=None, after_gw=None):
    oh_t = _bucket_onehot_t()
    bias = _bias_dense(rel_bias.T, oh_t).reshape(ATTN_HEADS, BLOCK, 2 * BLOCK)
    *segs, h = _inproj(x, norm_w, scale, shift, ws)
    q, k, v, za, zm, xbc, dtr, ga, gb = segs
    consts = _attn_consts(qnw, knw)
    o_att, lse = _attn_fwd(q, k, v, bias, sinks, consts)
    e_mat, e3t = _membership(SSM_W, SSM_P, SSM_HEADS)
    dsk_x = jnp.repeat(d_skip, SSM_P, axis=1)
    ypre, hprev, conv = _ssd_fwd(xbc, dtr, conv_w, conv_b, dt_bias, a_log, dsk_x, e3t)
    wap, wsp, wout = rows_fn(ypre)
    (dout, d_o, dza, dyp, dzm, dga, dgb, g_wout, g_wap, g_wsp, g_ssm_nw, dgate, loss) = _mid(
        x, tgt, o_att, za, ypre, zm, ga, gb, gate, ssm_nw, wap, wsp, wout)
    zero = after_mid(g_wap, g_wsp, g_wout) if after_mid is not None else 0.0
    dq, dk, dv, dss, g_qnw, g_knw, g_sinks = _attn_bwd(q, k, v, bias, sinks + zero, consts, o_att, lse, d_o)
    g_rel = _bias_grad(dss.reshape(ATTN_HEADS, BLOCK * 2 * BLOCK), oh_t).T
    dxbc, ddt, g_cw, g_cb, g_dtb, g_alog, g_dsk = _ssd_bwd(
        xbc, conv, dtr, conv_w, dt_bias, a_log, dsk_x, e_mat, e3t, hprev, dyp)
    dsegs = (dq, dk, dv, dza, dzm, dxbc, ddt, dga, dgb)
    g_ws = _gw_in(h, dsegs)
    zero = after_gw(g_ws) if after_gw is not None else 0.0
    gx, dshift, dscale, g_nw = _dh(x, dout, norm_w + zero, scale, dsegs, ws)
    return dict(loss=loss, grad_x=gx, dmod=jnp.concatenate([dshift, dscale, dgate], axis=1), g_ws=g_ws,
                g_wap=g_wap, g_wsp=g_wsp, g_wout=g_wout, g_norm_w=g_nw, g_qnw=g_qnw, g_knw=g_knw, g_rel=g_rel,
                g_sinks=g_sinks, g_conv_w=g_cw, g_conv_b=g_cb, g_dt_bias=g_dtb, g_a_log=g_alog, g_d_skip=g_dsk,
                g_ssm_nw=g_ssm_nw)


def _me():
    return lax.axis_index("x"), lax.axis_index("y"), lax.axis_index("c")


def _flip(v, bit):
    return 1 - v if bit else v


def _ag_direct(v, name):
    def body(v_ref, out_ref, send_sems, recv_sems, local_sem):
        x, y, c = _me()
        me = 4 * x + 2 * y + c
        mine = pltpu.make_async_copy(v_ref, out_ref.at[me], local_sem)
        mine.start()
        peers = [(_flip(x, k >> 2 & 1), _flip(y, k >> 1 & 1), _flip(c, k & 1)) for k in range(1, N_DEV)]
        sends = [pltpu.make_async_remote_copy(
            src_ref=v_ref, dst_ref=out_ref.at[me], send_sem=send_sems.at[j], recv_sem=recv_sems.at[j],
            device_id=p, device_id_type=MESH) for j, p in enumerate(peers)]
        for cp in sends:
            cp.start()
        for j, (px, py, pc) in enumerate(peers):
            pltpu.make_async_remote_copy(
                src_ref=v_ref, dst_ref=out_ref.at[4 * px + 2 * py + pc], send_sem=send_sems.at[j],
                recv_sem=recv_sems.at[j], device_id=(px, py, pc), device_id_type=MESH).wait_recv()
        for cp in sends:
            cp.wait_send()
        mine.wait()

    vm = pl.BlockSpec(memory_space=pltpu.VMEM)
    return pl.pallas_call(
        body, name=name, out_shape=jax.ShapeDtypeStruct((N_DEV,) + v.shape, v.dtype),
        in_specs=[vm], out_specs=vm,
        scratch_shapes=[pltpu.SemaphoreType.DMA((N_DEV - 1,)), pltpu.SemaphoreType.DMA((N_DEV - 1,)),
                        pltpu.SemaphoreType.DMA],
        compiler_params=_params(),
    )(v)


def _ag_two_level(v, name):
    def body(v_ref, out_ref, send_sems, recv_sems, local_sem):
        x, y, c = _me()
        me, sibling = (x, y, c), (x, y, 1 - c)
        chips = [(1 - x, y), (x, 1 - y), (1 - x, 1 - y)]

        def slot(px, py, pc):
            return out_ref.at[4 * px + 2 * py + pc]

        def copy(k, block, to, src=None):
            return pltpu.make_async_remote_copy(
                src_ref=slot(*block) if src is None else src, dst_ref=slot(*block),
                send_sem=send_sems.at[k], recv_sem=recv_sems.at[k], device_id=to, device_id_type=MESH)

        mine = pltpu.make_async_copy(v_ref, slot(*me), local_sem)
        mine.start()
        first = [copy(0, me, sibling, src=v_ref)]
        first += [copy(1 + j, me, (*chip, c), src=v_ref) for j, chip in enumerate(chips)]
        for cp in first:
            cp.start()
        passed = [copy(4 + j, (*chip, c), sibling) for j, chip in enumerate(chips)]
        for j, chip in enumerate(chips):
            copy(1 + j, (*chip, c), me).wait_recv()
            passed[j].start()
        copy(0, sibling, me).wait_recv()
        for j, chip in enumerate(chips):
            copy(4 + j, (*chip, 1 - c), me).wait_recv()
        for cp in first + passed:
            cp.wait_send()
        mine.wait()

    return pl.pallas_call(
        body, name=name, out_shape=jax.ShapeDtypeStruct((N_DEV,) + v.shape, v.dtype),
        in_specs=[ANY], out_specs=ANY,
        scratch_shapes=[pltpu.SemaphoreType.DMA((7,)), pltpu.SemaphoreType.DMA((7,)), pltpu.SemaphoreType.DMA],
        compiler_params=_params(),
    )(v)


def _rs_sibling(g, name):
    def body(g_ref, out_ref, send_sems, recv_sems):
        x, y, c = _me()
        cps = [pltpu.make_async_remote_copy(
            src_ref=g_ref.at[2 * ch + 1 - c], dst_ref=out_ref.at[ch], send_sem=send_sems.at[ch],
            recv_sem=recv_sems.at[ch], device_id=(x, y, 1 - c), device_id_type=MESH) for ch in range(4)]
        for cp in cps:
            cp.start()
        for cp in cps:
            cp.wait()

    return pl.pallas_call(
        body, name=name, out_shape=jax.ShapeDtypeStruct((4,) + g.shape[1:], g.dtype),
        in_specs=[ANY], out_specs=ANY,
        scratch_shapes=[pltpu.SemaphoreType.DMA((4,)), pltpu.SemaphoreType.DMA((4,))],
        compiler_params=_params(),
    )(g)


def _add_sibling(g, got, name):
    _, r, n = g.shape
    tr = min(r, 256)

    def body(c_ref, a_ref, b_ref, o_ref):
        o_ref[...] = a_ref[...] + b_ref[...]

    grid_spec = pltpu.PrefetchScalarGridSpec(
        num_scalar_prefetch=1, grid=(4, r // tr),
        in_specs=[pl.BlockSpec((1, tr, n), lambda ch, i, c_ref: (2 * ch + c_ref[0], i, 0)),
                  pl.BlockSpec((1, tr, n), lambda ch, i, c_ref: (ch, i, 0))],
        out_specs=pl.BlockSpec((1, tr, n), lambda ch, i, c_ref: (ch, i, 0)))
    return pl.pallas_call(
        body, name=name, grid_spec=grid_spec, out_shape=jax.ShapeDtypeStruct((4, r, n), g.dtype),
        compiler_params=_params(dimension_semantics=("arbitrary", "arbitrary")),
    )(lax.axis_index("c").reshape(1).astype(jnp.int32), g, got)


def _rs_chips(p, name):
    def body(p_ref, out_ref, send_sems, recv_sems, local_sem):
        x, y, c = _me()
        my_chip = 2 * x + y
        mine = pltpu.make_async_copy(p_ref.at[my_chip], out_ref.at[my_chip], local_sem)
        mine.start()
        chips = [(1 - x, y), (x, 1 - y), (1 - x, 1 - y)]
        sends = [pltpu.make_async_remote_copy(
            src_ref=p_ref.at[2 * px + py], dst_ref=out_ref.at[my_chip], send_sem=send_sems.at[j],
            recv_sem=recv_sems.at[j], device_id=(px, py, c), device_id_type=MESH) for j, (px, py) in enumerate(chips)]
        for cp in sends:
            cp.start()
        for j, (px, py) in enumerate(chips):
            pltpu.make_async_remote_copy(
                src_ref=p_ref.at[my_chip], dst_ref=out_ref.at[2 * px + py], send_sem=send_sems.at[j],
                recv_sem=recv_sems.at[j], device_id=(px, py, c), device_id_type=MESH).wait_recv()
        for cp in sends:
            cp.wait_send()
        mine.wait()

    return pl.pallas_call(
        body, name=name, out_shape=jax.ShapeDtypeStruct(p.shape, p.dtype),
        in_specs=[ANY], out_specs=ANY,
        scratch_shapes=[pltpu.SemaphoreType.DMA((3,)), pltpu.SemaphoreType.DMA((3,)), pltpu.SemaphoreType.DMA],
        compiler_params=_params(),
    )(p)


HBM = pl.BlockSpec(memory_space=pltpu.HBM)
SEM = pl.BlockSpec(memory_space=pltpu.SEMAPHORE)
EFFECT = pltpu.SideEffectType.DATAFLOW_SIDE_EFFECTING


def _peers(x, y, c):
    return [(_flip(x, k >> 2 & 1), _flip(y, k >> 1 & 1), _flip(c, k & 1)) for k in range(1, N_DEV)]


def _exchange_start(src, land, gather, name):
    def body(src_ref, land_ref, send_sems, recv_sems, src_thru, land_thru, token):
        x, y, c = _me()
        me = 4 * x + 2 * y + c
        for j, (px, py, pc) in enumerate(_peers(x, y, c)):
            pltpu.make_async_remote_copy(
                src_ref=src_ref if gather else src_ref.at[4 * px + 2 * py + pc], dst_ref=land_ref.at[me],
                send_sem=send_sems.at[j], recv_sem=recv_sems.at[j], device_id=(px, py, pc), device_id_type=MESH).start()
        token[...] = jnp.zeros_like(token)

    sems = pltpu.SemaphoreType.DMA((N_DEV - 1,))
    out = pl.pallas_call(
        body, name=name,
        out_shape=(sems, sems, pltpu.HBM(src.shape, src.dtype), pltpu.HBM(land.shape, land.dtype),
                   jax.ShapeDtypeStruct((8, 128), F32)),
        in_specs=(HBM, HBM), out_specs=(SEM, SEM, HBM, HBM, pl.BlockSpec(memory_space=pltpu.VMEM)),
        input_output_aliases={0: 2, 1: 3},
        compiler_params=pltpu.CompilerParams(has_side_effects=EFFECT),
    )(pltpu.with_memory_space_constraint(src, pltpu.HBM), pltpu.with_memory_space_constraint(land, pltpu.HBM))
    return out[:4], out[4][0, 0]


def _exchange_wait(started, after, gather, name):
    send_sems, recv_sems, src_thru, land_thru = started

    def body(src_ref, land_ref, send_sems, recv_sems, after_ref, src_dead, got_ref):
        x, y, c = _me()
        for j, (px, py, pc) in enumerate(_peers(x, y, c)):
            pid = 4 * px + 2 * py + pc
            cp = pltpu.make_async_remote_copy(
                src_ref=src_ref if gather else src_ref.at[pid], dst_ref=land_ref.at[pid],
                send_sem=send_sems.at[j], recv_sem=recv_sems.at[j], device_id=(px, py, pc), device_id_type=MESH)
            cp.wait_send()
            cp.wait_recv()

    return pl.pallas_call(
        body, name=name,
        out_shape=(pltpu.HBM(src_thru.shape, src_thru.dtype), pltpu.HBM(land_thru.shape, land_thru.dtype)),
        in_specs=(HBM, HBM, SEM, SEM, ANY), out_specs=(HBM, HBM), input_output_aliases={0: 0, 1: 1},
        compiler_params=pltpu.CompilerParams(has_side_effects=EFFECT),
    )(src_thru, land_thru, send_sems, recv_sems, after)[1]


def _reduce_scatter(g, name):
    got = _rs_sibling(g, name + "_sib")
    return _rs_chips(_add_sibling(g, got, name + "_add"), name + "_chips")


def _silu(a):
    return a * _sig(a)


def _mod_piece(c_all, w_ada, b_piece):
    def body(c_ref, w_ref, b_ref, o_ref):
        o_ref[...] = _dot(_bf(_silu(c_ref[...])), _bf(w_ref[...])) + b_ref[...]

    return pl.pallas_call(
        body, name="mod_piece", out_shape=jax.ShapeDtypeStruct((c_all.shape[0], w_ada.shape[1]), F32),
        compiler_params=_params(),
    )(c_all, w_ada, b_piece)


def _gw_ada(c_all, dmod_piece):
    def body(c_ref, d_ref, o_ref):
        o_ref[...] = _dot_tn(_bf(_silu(c_ref[...])), _bf(d_ref[...]))

    return pl.pallas_call(
        body, name="gw_ada", out_shape=jax.ShapeDtypeStruct((c_all.shape[1], dmod_piece.shape[1]), F32),
        compiler_params=_params(),
    )(c_all, dmod_piece)


def _adam(parts, w, m, v, name):
    k, r, n = parts.shape
    tr = r if r <= 256 else 256
    assert r % tr == 0

    def body(p_ref, w_ref, m_ref, v_ref, g_ref, d_ref, nm_ref, nv_ref):
        g = p_ref[0].astype(F32)
        for j in range(1, k):
            g = g + p_ref[j].astype(F32)
        m_new = ADAM_B1 * m_ref[...] + (1.0 - ADAM_B1) * g
        v_new = ADAM_B2 * v_ref[...] + (1.0 - ADAM_B2) * jnp.square(g)
        m_hat = m_new / (1.0 - ADAM_B1 ** ADAM_STEP)
        v_hat = v_new / (1.0 - ADAM_B2 ** ADAM_STEP)
        g_ref[...] = g
        d_ref[...] = -ADAM_LR * (m_hat / (jnp.sqrt(v_hat) + ADAM_EPS) + ADAM_WD * w_ref[...])
        nm_ref[...] = m_new
        nv_ref[...] = v_new

    blk = pl.BlockSpec((tr, n), lambda i: (i, 0))
    return pl.pallas_call(
        body, name=name, grid=(r // tr,),
        in_specs=[pl.BlockSpec((k, tr, n), lambda i: (0, i, 0)), blk, blk, blk],
        out_specs=[blk, blk, blk, blk],
        out_shape=[jax.ShapeDtypeStruct((r, n), F32)] * 4,
        compiler_params=_params(dimension_semantics=("arbitrary",)),
    )(parts, w, m, v)


_SMALL = (("b_ada", 3 * D_MODEL), ("norm_w", D_MODEL), ("q_norm_w", HEAD_DIM), ("k_norm_w", HEAD_DIM),
          ("rel_bias", REL_BUCKETS * ATTN_HEADS), ("sinks", ATTN_HEADS), ("conv_b", XBC_W), ("dt_bias", SSM_HEADS),
          ("a_log", SSM_HEADS), ("d_skip", SSM_HEADS), ("ssm_norm_w", SSM_W))
_SMALL_N = sum(n for _, n in _SMALL)
_SMALL_PAD = -(-_SMALL_N // 128) * 128
_PACK_N = _SMALL_PAD + CONV_K * XBC_W


def _pack_small(d):
    parts = [d[name].reshape(1, n) for name, n in _SMALL]
    return jnp.concatenate(parts + [jnp.zeros((1, _SMALL_PAD - _SMALL_N), F32)], axis=1)


def _unpack_small(vec, shapes):
    out, off = {}, 0
    for name, n in _SMALL:
        out[name] = vec[:, off:off + n].reshape(shapes[name])
        off += n
    return out


WEIGHTS = ("w_ada", "b_ada", "norm_w", "w_in", "q_norm_w", "k_norm_w", "rel_bias", "sinks", "conv_w", "conv_b",
           "dt_bias", "a_log", "d_skip", "ssm_norm_w", "w_attn_proj", "w_ssm_proj", "w_out")


def kernel(x, c, w_ada, b_ada, norm_w, w_in, q_norm_w, k_norm_w, rel_bias, sinks, conv_w, conv_b, dt_bias, a_log, d_skip, ssm_norm_w, w_attn_proj, w_ssm_proj, w_out, loss_target, m_w_ada, m_b_ada, m_norm_w, m_w_in, m_q_norm_w, m_k_norm_w, m_rel_bias, m_sinks, m_conv_w, m_conv_b, m_dt_bias, m_a_log, m_d_skip, m_ssm_norm_w, m_w_attn_proj, m_w_ssm_proj, m_w_out, v_w_ada, v_b_ada, v_norm_w, v_w_in, v_q_norm_w, v_k_norm_w, v_rel_bias, v_sinks, v_conv_w, v_conv_b, v_dt_bias, v_a_log, v_d_skip, v_ssm_norm_w, v_w_attn_proj, v_w_ssm_proj, v_w_out):
    w = dict(w_ada=w_ada, b_ada=b_ada, norm_w=norm_w, w_in=w_in, q_norm_w=q_norm_w, k_norm_w=k_norm_w,
             rel_bias=rel_bias, sinks=sinks, conv_w=conv_w, conv_b=conv_b, dt_bias=dt_bias, a_log=a_log,
             d_skip=d_skip, ssm_norm_w=ssm_norm_w, w_attn_proj=w_attn_proj, w_ssm_proj=w_ssm_proj, w_out=w_out)
    m = dict(w_ada=m_w_ada, b_ada=m_b_ada, norm_w=m_norm_w, w_in=m_w_in, q_norm_w=m_q_norm_w, k_norm_w=m_k_norm_w,
             rel_bias=m_rel_bias, sinks=m_sinks, conv_w=m_conv_w, conv_b=m_conv_b, dt_bias=m_dt_bias, a_log=m_a_log,
             d_skip=m_d_skip, ssm_norm_w=m_ssm_norm_w, w_attn_proj=m_w_attn_proj, w_ssm_proj=m_w_ssm_proj, w_out=m_w_out)
    v = dict(w_ada=v_w_ada, b_ada=v_b_ada, norm_w=v_norm_w, w_in=v_w_in, q_norm_w=v_q_norm_w, k_norm_w=v_k_norm_w,
             rel_bias=v_rel_bias, sinks=v_sinks, conv_w=v_conv_w, conv_b=v_conv_b, dt_bias=v_dt_bias, a_log=v_a_log,
             d_skip=v_d_skip, ssm_norm_w=v_ssm_norm_w, w_attn_proj=v_w_attn_proj, w_ssm_proj=v_w_ssm_proj, w_out=v_w_out)
    me = 4 * lax.axis_index("x") + 2 * lax.axis_index("y") + lax.axis_index("c")
    ada_n = w_ada.shape[2]
    in_n = w_in.shape[2]
    cw_n = conv_w.shape[2]

    c_all = _ag_direct(c, "ag_c").reshape(N_DEV, D_MODEL)
    b_piece = lax.dynamic_slice_in_dim(b_ada, me * ada_n, ada_n, axis=1)
    mod_all = _ag_direct(_mod_piece(c_all, w_ada[0], b_piece), "ag_mod")
    mod = lax.dynamic_index_in_dim(mod_all, me, axis=1, keepdims=False).reshape(1, 3 * D_MODEL)
    shift, scale, gate = mod[:, :D_MODEL], mod[:, D_MODEL:2 * D_MODEL], mod[:, 2 * D_MODEL:]

    w_in_all = _ag_two_level(w_in[0].astype(BF), "ag_w_in")
    w_in_full = w_in_all.transpose(1, 0, 2).reshape(D_MODEL, N_DEV * in_n)
    ws = [w_in_full[:, SEG_OFF[j]:SEG_OFF[j + 1]] for j in range(9)]
    conv_w_full = _ag_direct(conv_w[0], "ag_conv_w").transpose(1, 0, 2).reshape(CONV_K, XBC_W)

    def with_mine(blocks, mine):
        return lax.dynamic_update_index_in_dim(jnp.zeros(blocks, mine.dtype), mine, me, axis=0)

    rows = jnp.concatenate([w_attn_proj[0], w_ssm_proj[0], w_out[0]], axis=0).astype(BF)
    r_ap, r_sp = w_attn_proj.shape[1], w_ssm_proj.shape[1]
    rows_started, zero = _exchange_start(rows, with_mine((N_DEV,) + rows.shape, rows), True, "ag_rows_start")

    def rows_fn(after):
        rows_all = _exchange_wait(rows_started, after, True, "ag_rows_wait")
        return (rows_all[:, :r_ap].reshape(ATTN_W, D_MODEL), rows_all[:, r_ap:r_ap + r_sp].reshape(SSM_W, D_MODEL),
                rows_all[:, r_ap + r_sp:].reshape(D_MODEL, D_MODEL))

    started = {}

    def send_blocks(key, g, name):
        g = g.astype(BF)
        started[key], zero = _exchange_start(
            g, with_mine(g.shape, lax.dynamic_index_in_dim(g, me, axis=0, keepdims=False)), False, name)
        return zero

    def after_mid(g_wap, g_wsp, g_wout):
        return send_blocks("rows", jnp.concatenate(
            [g_wap.reshape(N_DEV, r_ap, D_MODEL), g_wsp.reshape(N_DEV, r_sp, D_MODEL),
             g_wout.reshape(N_DEV, r_ap, D_MODEL)], axis=1), "rs_rows_start")

    def after_gw(g_ws):
        return send_blocks("in", jnp.concatenate(g_ws, axis=1).reshape(D_MODEL, N_DEV, in_n).transpose(1, 0, 2),
                           "rs_in_start")

    r = _local_step(x[0], loss_target[0], shift, scale + zero, gate, ws, rows_fn, norm_w, q_norm_w, k_norm_w,
                    rel_bias, sinks, conv_w_full, conv_b, dt_bias, a_log, d_skip, ssm_norm_w, after_mid, after_gw)

    loss = lax.psum(r["loss"][0, 0], ("x", "y", "c"))

    small = dict(b_ada=r["dmod"], norm_w=r["g_norm_w"], q_norm_w=r["g_qnw"], k_norm_w=r["g_knw"], rel_bias=r["g_rel"],
                 sinks=r["g_sinks"], conv_b=r["g_conv_b"], dt_bias=r["g_dt_bias"], a_log=r["g_a_log"],
                 d_skip=r["g_d_skip"], ssm_norm_w=r["g_ssm_nw"])
    pack = jnp.concatenate([_pack_small(small), r["g_conv_w"].reshape(1, CONV_K * XBC_W)], axis=1)
    pack_all = _ag_direct(pack, "ag_small")
    shapes = {name: w[name].shape for name, _ in _SMALL}
    res = {}
    g_s, d_s, m_s, v_s = _adam(pack_all[:, :, :_SMALL_PAD], _pack_small(w), _pack_small(m), _pack_small(v), "adam_small")
    for name, arr in _unpack_small(g_s, shapes).items():
        res[name] = [arr]
    for vec in (d_s, m_s, v_s):
        for name, arr in _unpack_small(vec, shapes).items():
            res[name].append(arr)
    cw_parts = pack_all[:, 0, _SMALL_PAD:].reshape(N_DEV, CONV_K, XBC_W)
    cw_mine = lax.dynamic_slice_in_dim(cw_parts, me * cw_n, cw_n, axis=2)
    res["conv_w"] = [a[None] for a in _adam(cw_mine, conv_w[0], m_conv_w[0], v_conv_w[0], "adam_conv_w")]

    dmod_piece = lax.dynamic_slice_in_dim(pack_all[:, 0, :3 * D_MODEL], me * ada_n, ada_n, axis=1)
    g_ada = _gw_ada(c_all, dmod_piece)
    res["w_ada"] = [a[None] for a in _adam(g_ada[None], w_ada[0], m_w_ada[0], v_w_ada[0], "adam_w_ada")]

    cat = lambda d: jnp.concatenate([d["w_attn_proj"][0], d["w_ssm_proj"][0], d["w_out"][0]], axis=0)
    rows_res = _adam(_exchange_wait(started["rows"], g_ada, False, "rs_rows_wait"), cat(w), cat(m), cat(v), "adam_w_rows")
    res["w_in"] = [a[None] for a in _adam(_exchange_wait(started["in"], rows_res[0], False, "rs_in_wait"),
                                          w_in[0], m_w_in[0], v_w_in[0], "adam_w_in")]
    res["w_attn_proj"] = [a[None, :r_ap] for a in rows_res]
    res["w_ssm_proj"] = [a[None, r_ap:r_ap + r_sp] for a in rows_res]
    res["w_out"] = [a[None, r_ap + r_sp:] for a in rows_res]

    outs = [loss, r["grad_x"][None]]
    for j in range(4):
        outs += [res[name][j] for name in WEIGHTS]
    return tuple(outs)
```

```python
import functools
import math

import numpy as np
import jax
import jax.numpy as jnp
from jax import lax
from jax.experimental import pallas as pl
from jax.experimental.pallas import tpu as pltpu

F32 = jnp.float32
BF = jnp.bfloat16
HI = lax.Precision.HIGHEST

D_MODEL = 1024
ATTN_HEADS = 16
KV_HEADS = 4
GRP = ATTN_HEADS // KV_HEADS
HEAD_DIM = 64
ATTN_W = ATTN_HEADS * HEAD_DIM
KV_W = KV_HEADS * HEAD_DIM
BLOCK = 128
REL_BUCKETS = 32
REL_MAX_DIST = 128
SSM_W = 2048
SSM_P = 64
SSM_HEADS = 32
SSM_G = 4
SSM_R = 8
SSM_N = 128
CONV_K = 4
XBC_W = SSM_W + 2 * SSM_G * SSM_N
SEG_W = (ATTN_W, KV_W, KV_W, ATTN_W, SSM_W, XBC_W, SSM_HEADS, D_MODEL, D_MODEL)
SEG_OFF = tuple(int(v) for v in np.cumsum((0,) + SEG_W))
IN_W = SEG_OFF[-1]
GATE_SEGS = (3, 4, 7, 8)
EPS = 1e-6
N_DEV = 8
ADAM_LR, ADAM_B1, ADAM_B2, ADAM_EPS, ADAM_WD, ADAM_STEP = 0.001, 0.9, 0.999, 1e-08, 0.01, 10
VMEM_LIMIT = 60 * 1024 * 1024
MESH = pl.DeviceIdType.MESH
ANY = pl.BlockSpec(memory_space=pl.ANY)


def _dot(a, b, precision=None):
    return jnp.dot(a, b, preferred_element_type=F32, precision=precision)


def _dot_nt(a, b, precision=None):
    return lax.dot_general(a, b, (((1,), (1,)), ((), ())), preferred_element_type=F32, precision=precision)


def _dot_tn(a, b, precision=None):
    return lax.dot_general(a, b, (((0,), (0,)), ((), ())), preferred_element_type=F32, precision=precision)


def _bf(a):
    return a.astype(BF)


def _sig(a):
    return 1.0 / (1.0 + jnp.exp(-a))


def _params(**kw):
    return pltpu.CompilerParams(vmem_limit_bytes=VMEM_LIMIT, **kw)


def _full(shape):
    nd = len(shape)
    return pl.BlockSpec(shape, lambda i: (0,) * nd)


def _rows(tm, w):
    return pl.BlockSpec((tm, w), lambda i: (i, 0))


def _inproj(x, norm_w, scale, shift, w_t, tm=256):
    s = x.shape[0]

    def body(x_ref, nw_ref, sc_ref, sh_ref, w_hbm, *rest):
        outs, h_ref, w_vm, sem = rest[:9], rest[9], rest[10], rest[11]

        @pl.when(pl.program_id(0) == 0)
        def _():
            cp = pltpu.make_async_copy(w_hbm, w_vm, sem)
            cp.start()
            cp.wait()

        xv = x_ref[...]
        r = lax.rsqrt(jnp.mean(xv * xv, axis=-1, keepdims=True) + EPS)
        h = xv * r * (nw_ref[...] * (1.0 + sc_ref[...])) + sh_ref[...]
        hb = _bf(h)
        h_ref[...] = hb
        for j in range(9):
            outs[j][...] = _dot_nt(hb, w_vm[SEG_OFF[j]:SEG_OFF[j + 1], :]).astype(outs[j].dtype)

    vec = _full((1, D_MODEL))
    return pl.pallas_call(
        body, name="inproj", grid=(s // tm,),
        in_specs=[_rows(tm, D_MODEL), vec, vec, vec, ANY],
        out_specs=[_rows(tm, w) for w in SEG_W] + [_rows(tm, D_MODEL)],
        out_shape=[jax.ShapeDtypeStruct((s, w), BF if j in GATE_SEGS else F32) for j, w in enumerate(SEG_W)]
                  + [jax.ShapeDtypeStruct((s, D_MODEL), BF)],
        scratch_shapes=[pltpu.VMEM((IN_W, D_MODEL), BF), pltpu.SemaphoreType.DMA],
        compiler_params=_params(dimension_semantics=("arbitrary",)),
    )(x, norm_w, scale, shift, w_t)


def _bucket_onehot_t():
    qi = jnp.arange(BLOCK)[:, None]
    kj = jnp.arange(2 * BLOCK)[None, :]
    dist = qi + BLOCK - kj
    n = jnp.maximum(dist, 0)
    max_exact = REL_BUCKETS // 2
    nf = jnp.maximum(n, 1).astype(F32)
    large = max_exact + (jnp.log(nf / max_exact) / math.log(REL_MAX_DIST / max_exact)
                         * (REL_BUCKETS - max_exact)).astype(jnp.int32)
    large = jnp.minimum(large, REL_BUCKETS - 1)
    bucket = jnp.where(n < max_exact, n, large).reshape(1, BLOCK * 2 * BLOCK)
    return (bucket == jnp.arange(REL_BUCKETS)[:, None]).astype(F32)


def _bias_dense(rel_bias_t, oh_t):
    def body(rb_ref, oh_ref, o_ref):
        o_ref[...] = _dot(rb_ref[...], oh_ref[...], HI)

    return pl.pallas_call(
        body, name="bias_dense", out_shape=jax.ShapeDtypeStruct((ATTN_HEADS, BLOCK * 2 * BLOCK), F32),
        compiler_params=_params(),
    )(rel_bias_t, oh_t)


def _bias_grad(ds_sum, oh_t):
    def body(ds_ref, oh_ref, o_ref):
        o_ref[...] = _dot_nt(ds_ref[...], oh_ref[...], HI)

    return pl.pallas_call(
        body, name="bias_grad", out_shape=jax.ShapeDtypeStruct((ATTN_HEADS, REL_BUCKETS), F32),
        compiler_params=_params(),
    )(ds_sum, oh_t)


def _group_sum(a, e):
    hi = _bf(a)
    return _dot(hi, e) + _dot(_bf(a - hi.astype(F32)), e)


def _group_bcast(a, e3t):
    hi = _bf(a)
    r1 = a - hi.astype(F32)
    mid = _bf(r1)
    return _dot(jnp.concatenate([hi, mid, _bf(r1 - mid.astype(F32))], axis=1), e3t)


def _membership(width, group, ngroups):
    e = (jnp.arange(width)[:, None] // group == jnp.arange(ngroups)[None, :]).astype(BF)
    return e, jnp.tile(e.T, (3, 1))


def _fold(width, group):
    return (jnp.arange(width)[:, None] % group == jnp.arange(group)[None, :]).astype(BF)


def _heads_norm(t, w_x, e, e3t):
    r = lax.rsqrt(_group_sum(t * t, e) * (1.0 / HEAD_DIM) + EPS)
    r_x = _group_bcast(r, e3t)
    return t * r_x * w_x, r_x


def _heads_norm_bwd(t, r_x, w_x, d, e, e3t):
    wd = d * w_x
    corr = _group_bcast(_group_sum(t * wd, e) * (1.0 / HEAD_DIM), e3t)
    return r_x * wd - t * (r_x * r_x * r_x) * corr, jnp.sum(d * t * r_x, axis=0, keepdims=True)


def _stack_heads(a, hk):
    return jnp.concatenate([a[:, (hk * GRP + g) * HEAD_DIM:(hk * GRP + g + 1) * HEAD_DIM] for g in range(GRP)], axis=0)


def _stack_cols(a, hk):
    return jnp.concatenate([a[:, hk * GRP + g:hk * GRP + g + 1] for g in range(GRP)], axis=0)


def _window_mask(first):
    qi = jnp.bitwise_and(lax.broadcasted_iota(jnp.int32, (GRP * BLOCK, 2 * BLOCK), 0), BLOCK - 1)
    kj = lax.broadcasted_iota(jnp.int32, (GRP * BLOCK, 2 * BLOCK), 1)
    prev_ok = jnp.logical_and(kj > qi, jnp.logical_not(first))
    cur_ok = jnp.logical_and(kj >= BLOCK, kj - BLOCK <= qi)
    return jnp.logical_or(jnp.logical_and(kj < BLOCK, prev_ok), cur_ok)


def _attn_consts(qnw, knw):
    eq, eq3t = _membership(ATTN_W, HEAD_DIM, ATTN_HEADS)
    ek, ek3t = _membership(KV_W, HEAD_DIM, ATTN_HEADS)
    return (jnp.tile(qnw, (1, ATTN_HEADS)), jnp.tile(knw, (1, KV_HEADS)), eq, eq3t, ek, ek3t)


def _attn_fwd(q, k, v, bias, sinks, consts):
    s = q.shape[0]
    nb = s // BLOCK

    def body(q_ref, kp_ref, kc_ref, vp_ref, vc_ref, b_ref, sk_ref, qw_ref, kw_ref, eq_ref, eq3_ref, ek_ref, ek3_ref,
             o_ref, lse_ref):
        i = pl.program_id(0)
        mask = _window_mask(i == 0)
        qn = _bf(_heads_norm(q_ref[...], qw_ref[...], eq_ref[...], eq3_ref[...])[0])
        kn = _bf(_heads_norm(jnp.concatenate([kp_ref[...], kc_ref[...]], axis=0), kw_ref[...], ek_ref[...], ek3_ref[...])[0])
        vv = _bf(jnp.concatenate([vp_ref[...], vc_ref[...]], axis=0))
        lses = []
        for hk in range(KV_HEADS):
            ks = slice(hk * HEAD_DIM, (hk + 1) * HEAD_DIM)
            sc = _dot_nt(_stack_heads(qn, hk), kn[:, ks]) * (HEAD_DIM ** -0.5)
            sc = sc + b_ref[hk * GRP:(hk + 1) * GRP].reshape(GRP * BLOCK, 2 * BLOCK)
            sc = jnp.where(mask, sc, -1e30)
            sink = jnp.concatenate([jnp.full((BLOCK, 1), sk_ref[0, hk * GRP + g], F32) for g in range(GRP)], axis=0)
            m = jnp.maximum(jnp.max(sc, axis=-1, keepdims=True), sink)
            p = jnp.exp(sc - m)
            den = jnp.sum(p, axis=-1, keepdims=True) + jnp.exp(sink - m)
            out = _dot(_bf(p), vv[:, ks]) / den
            lse = m + jnp.log(den)
            for g in range(GRP):
                h = hk * GRP + g
                o_ref[:, h * HEAD_DIM:(h + 1) * HEAD_DIM] = out[g * BLOCK:(g + 1) * BLOCK]
                lses.append(lse[g * BLOCK:(g + 1) * BLOCK])
        lse_ref[...] = jnp.concatenate(lses, axis=1)

    cur = lambda w: pl.BlockSpec((BLOCK, w), lambda i: (i, 0))
    prev = lambda w: pl.BlockSpec((BLOCK, w), lambda i: (jnp.maximum(i - 1, 0), 0))
    return pl.pallas_call(
        body, name="attn_fwd", grid=(nb,),
        in_specs=[cur(ATTN_W), prev(KV_W), cur(KV_W), prev(KV_W), cur(KV_W),
                  pl.BlockSpec((ATTN_HEADS, BLOCK, 2 * BLOCK), lambda i: (0, 0, 0)),
                  pl.BlockSpec(memory_space=pltpu.SMEM)] + [_full(c.shape) for c in consts],
        out_specs=[cur(ATTN_W), cur(ATTN_HEADS)],
        out_shape=[jax.ShapeDtypeStruct((s, ATTN_W), F32), jax.ShapeDtypeStruct((s, ATTN_HEADS), F32)],
        compiler_params=_params(dimension_semantics=("arbitrary",)),
    )(q, k, k, v, v, bias, sinks, *consts)


def _conv_taps(ext_ref, xbc, tail):
    ext_ref[0:8, :] = tail
    ext_ref[8:8 + BLOCK, :] = xbc
    return [ext_ref[5 + j:5 + j + BLOCK, :] for j in range(CONV_K)]


def _softplus(u):
    return jnp.maximum(u, 0.0) + jnp.log(1.0 + jnp.exp(-jnp.abs(u)))


def _tril():
    r = lax.broadcasted_iota(jnp.int32, (BLOCK, BLOCK), 0)
    c = lax.broadcasted_iota(jnp.int32, (BLOCK, BLOCK), 1)
    return r >= c


def _triu():
    r = lax.broadcasted_iota(jnp.int32, (BLOCK, BLOCK), 0)
    c = lax.broadcasted_iota(jnp.int32, (BLOCK, BLOCK), 1)
    return r <= c


def _exact_left(m01, a):
    hi = _bf(a)
    r1 = a - hi.astype(F32)
    mid = _bf(r1)
    return _dot(m01, hi) + _dot(m01, mid) + _dot(m01, _bf(r1 - mid.astype(F32)))


def _ssd_common(conv, dtr_ref, dtb_ref, alog_ref, e3_ref):
    sg = _sig(conv)
    xact = conv * sg
    u = dtr_ref[...] + dtb_ref[...]
    dt = _softplus(u)
    a = -jnp.exp(alog_ref[...])
    trilb = _tril()
    acum = _exact_left(trilb.astype(BF), dt * a)
    both = _group_bcast(jnp.concatenate([dt, acum], axis=0), e3_ref[...])
    dt_x, acum_x = both[:BLOCK], both[BLOCK:]
    return sg, xact, u, dt, a, trilb, acum, dt_x, acum_x


def _ssd_fwd(xbc, dt_raw, conv_w, conv_b, dt_bias, a_log, dsk_x, e3t):
    s = xbc.shape[0]
    nc = s // BLOCK

    def body(x_ref, tail_ref, dtr_ref, cw_ref, cb_ref, dtb_ref, alog_ref, dsk_ref, e3_ref,
             y_ref, hp_ref, conv_ref, hst, ext):
        i = pl.program_id(0)

        @pl.when(i == 0)
        def _():
            hst[...] = jnp.zeros_like(hst)

        tail = jnp.where(i > 0, tail_ref[...], 0.0)
        taps = _conv_taps(ext, x_ref[...], tail)
        conv = cb_ref[...] + sum(taps[j] * cw_ref[j:j + 1, :] for j in range(CONV_K))
        conv_ref[...] = conv
        _, xact, _, _, _, trilb, acum, dt_x, acum_x = _ssd_common(conv, dtr_ref, dtb_ref, alog_ref, e3_ref)
        xs = xact[:, :SSM_W]
        acum_t = acum.T
        ea_x = jnp.exp(acum_x)
        last_x = acum_x[BLOCK - 1:BLOCK, :]
        xdt = xs * dt_x
        xw = xdt * jnp.exp(last_x - acum_x)
        cd_x = jnp.exp(last_x)
        hprev = hst[...]
        hp_ref[0] = hprev
        dsk = dsk_ref[...]
        for g in range(SSM_G):
            bg = _bf(xact[:, SSM_W + g * SSM_N:SSM_W + (g + 1) * SSM_N])
            cg = _bf(xact[:, SSM_W + SSM_G * SSM_N + g * SSM_N:SSM_W + SSM_G * SSM_N + (g + 1) * SSM_N])
            sl = slice(g * SSM_R * SSM_P, (g + 1) * SSM_R * SSM_P)
            cb = _dot_nt(cg, bg)
            yoff = _dot(cg, _bf(hprev[:, sl])) * ea_x[:, sl]
            hst[:, sl] = hprev[:, sl] * cd_x[:, sl] + _dot_tn(bg, _bf(xw[:, sl]))
            for r in range(SSM_R):
                hh = g * SSM_R + r
                hs = slice(hh * SSM_P, (hh + 1) * SSM_P)
                seg = jnp.where(trilb, acum[:, hh:hh + 1] - acum_t[hh:hh + 1, :], -1e30)
                mm = cb * jnp.exp(seg)
                yd = _dot(_bf(mm), _bf(xdt[:, hs]))
                y_ref[:, hs] = yd + yoff[:, r * SSM_P:(r + 1) * SSM_P] + dsk[:, hs] * xs[:, hs]

    chunk = lambda w: pl.BlockSpec((BLOCK, w), lambda i: (i, 0))
    return pl.pallas_call(
        body, name="ssd_fwd", grid=(nc,),
        in_specs=[chunk(XBC_W), pl.BlockSpec((8, XBC_W), lambda i: (jnp.maximum(i * (BLOCK // 8) - 1, 0), 0)),
                  chunk(SSM_HEADS), _full((CONV_K, XBC_W)), _full((1, XBC_W)), _full((1, SSM_HEADS)),
                  _full((1, SSM_HEADS)), _full((1, SSM_W)), _full((3 * SSM_HEADS, SSM_W))],
        out_specs=[chunk(SSM_W), pl.BlockSpec((1, SSM_N, SSM_W), lambda i: (i, 0, 0)), chunk(XBC_W)],
        out_shape=[jax.ShapeDtypeStruct((s, SSM_W), F32), jax.ShapeDtypeStruct((nc, SSM_N, SSM_W), F32),
                   jax.ShapeDtypeStruct((s, XBC_W), F32)],
        scratch_shapes=[pltpu.VMEM((SSM_N, SSM_W), F32), pltpu.VMEM((BLOCK + 8, XBC_W), F32)],
        compiler_params=_params(dimension_semantics=("arbitrary",)),
    )(xbc, xbc, dt_raw, conv_w, conv_b, dt_bias, a_log, dsk_x, e3t)


def _dsilu(z, sg):
    return sg * (1.0 + z * (1.0 - sg))


def _mid(x, tgt, o_att, za, ypre, zm, ga, gb, gate, ssm_nw, wap, wsp, wout, tm=256):
    s = x.shape[0]
    gw = SSM_W // SSM_G

    def body(x_ref, t_ref, o_ref, za_ref, yp_ref, zm_ref, ga_ref, gb_ref, gate_ref, nw_ref, wap_h, wsp_h, wout_h,
             dout_ref, do_ref, dza_ref, dyp_ref, dzm_ref, dga_ref, dgb_ref,
             yag_ref, dya_ref, yn_ref, dyb_ref, mg_ref, dob_ref, gnw_ref, dgate_ref, loss_ref,
             wap_v, wsp_v, wout_v, sem):
        i = pl.program_id(0)

        @pl.when(i == 0)
        def _():
            cps = [pltpu.make_async_copy(a, b, sem.at[j])
                   for j, (a, b) in enumerate(((wap_h, wap_v), (wsp_h, wsp_v), (wout_h, wout_v)))]
            for cp in cps:
                cp.start()
            gnw_ref[...] = jnp.zeros_like(gnw_ref)
            dgate_ref[...] = jnp.zeros_like(dgate_ref)
            loss_ref[...] = jnp.zeros_like(loss_ref)
            for cp in cps:
                cp.wait()

        gate = gate_ref[...]
        nw = nw_ref[...]
        o_att = o_ref[...]
        z_a = za_ref[...].astype(F32)
        s_a = _sig(z_a)
        silu_a = z_a * s_a
        yag = _bf(o_att * silu_a)
        yag_ref[...] = yag
        y_a = _dot(yag, wap_v[...])
        ypre = yp_ref[...]
        z_m = zm_ref[...].astype(F32)
        s_m = _sig(z_m)
        silu_m = z_m * s_m
        yg = ypre * silu_m
        rinv = jnp.concatenate(
            [jnp.broadcast_to(lax.rsqrt(jnp.mean(yg[:, g * gw:(g + 1) * gw] ** 2, axis=-1, keepdims=True) + EPS), (tm, gw))
             for g in range(SSM_G)], axis=1)
        ynr = yg * rinv
        yn = _bf(ynr * nw)
        yn_ref[...] = yn
        y_b = _dot(yn, wsp_v[...])
        g_a = _sig(ga_ref[...].astype(F32))
        g_b = _sig(gb_ref[...].astype(F32))
        merged = _bf(g_a * y_a + g_b * y_b)
        mg_ref[...] = merged
        o = _dot(merged, wout_v[...])
        diff = x_ref[...] + gate * o - t_ref[...]
        loss_ref[...] += (0.5 / D_MODEL) * jnp.sum(diff * diff, axis=(0, 1), keepdims=True)
        dout = diff * (1.0 / D_MODEL)
        dout_ref[...] = dout
        dgate_ref[...] += jnp.sum(dout * o, axis=0, keepdims=True)
        d_o = _bf(dout * gate)
        dob_ref[...] = d_o
        dmerged = _dot_nt(d_o, wout_v[...])
        dy_a = dmerged * g_a
        dy_b = dmerged * g_b
        dga_ref[...] = _bf(dy_a * y_a * (1.0 - g_a))
        dgb_ref[...] = _bf(dy_b * y_b * (1.0 - g_b))
        dy_a = _bf(dy_a)
        dy_b = _bf(dy_b)
        dya_ref[...] = dy_a
        dyb_ref[...] = dy_b
        dyag = _dot_nt(dy_a, wap_v[...])
        do_ref[...] = dyag * silu_a
        dza_ref[...] = _bf(dyag * o_att * _dsilu(z_a, s_a))
        dyn = _dot_nt(dy_b, wsp_v[...])
        gnw_ref[...] += jnp.sum(dyn * ynr, axis=0, keepdims=True)
        dynw = dyn * nw
        corr = jnp.concatenate(
            [jnp.broadcast_to(jnp.mean((dynw * ynr)[:, g * gw:(g + 1) * gw], axis=-1, keepdims=True), (tm, gw))
             for g in range(SSM_G)], axis=1)
        dyg = rinv * (dynw - ynr * corr)
        dyp_ref[...] = dyg * silu_m
        dzm_ref[...] = _bf(dyg * ypre * _dsilu(z_m, s_m))

    r1, r2 = _rows(tm, D_MODEL), _rows(tm, SSM_W)
    sd = jax.ShapeDtypeStruct
    return pl.pallas_call(
        body, name="mid", grid=(s // tm,),
        in_specs=[r1, r1, r1, r1, r2, r2, r1, r1, _full((1, D_MODEL)), _full((1, SSM_W)), ANY, ANY, ANY],
        out_specs=[r1, r1, r1, r2, r2, r1, r1, r1, r1, r2, r1, r1, r1,
                   _full((1, SSM_W)), _full((1, D_MODEL)), _full((1, 1))],
        out_shape=[sd((s, D_MODEL), F32), sd((s, ATTN_W), F32), sd((s, ATTN_W), BF), sd((s, SSM_W), F32),
                   sd((s, SSM_W), BF), sd((s, D_MODEL), BF), sd((s, D_MODEL), BF),
                   sd((s, ATTN_W), BF), sd((s, D_MODEL), BF), sd((s, SSM_W), BF), sd((s, D_MODEL), BF),
                   sd((s, D_MODEL), BF), sd((s, D_MODEL), BF),
                   sd((1, SSM_W), F32), sd((1, D_MODEL), F32), sd((1, 1), F32)],
        scratch_shapes=[pltpu.VMEM((ATTN_W, D_MODEL), BF), pltpu.VMEM((SSM_W, D_MODEL), BF), pltpu.VMEM((D_MODEL, D_MODEL), BF),
                        pltpu.SemaphoreType.DMA((3,))],
        compiler_params=_params(dimension_semantics=("arbitrary",)),
    )(x, tgt, o_att, za, ypre, zm, ga, gb, gate, ssm_nw, wap, wsp, wout)


def _attn_bwd(q, k, v, bias, sinks, consts, o_att, lse, d_o):
    s = q.shape[0]
    nb = s // BLOCK
    folds = (_fold(ATTN_W, HEAD_DIM), _fold(KV_W, HEAD_DIM))

    def body(q_ref, kp_ref, kc_ref, vp_ref, vc_ref, b_ref, skv_ref, qw_ref, kw_ref, eq_ref, eq3_ref, ek_ref, ek3_ref,
             fq_ref, fk_ref, o_ref, lse_ref, do_ref,
             dq_ref, dk_ref, dv_ref, dss_ref, gqw_ref, gkw_ref, gsk_ref, ckn, cv, dqn_s, dkn_s, dv_s, gq_x, gk_x):
        i = pl.program_id(0)
        kw, ek, ek3 = kw_ref[...], ek_ref[...], ek3_ref[...]

        @pl.when(i == 0)
        def _():
            for ref in (ckn, cv, dss_ref, gq_x, gk_x, gsk_ref):
                ref[...] = jnp.zeros_like(ref)

        @pl.when(i < nb)
        def _():
            mask = _window_mask(i == 0)
            qw, eq, eq3 = qw_ref[...], eq_ref[...], eq3_ref[...]
            qf = q_ref[...]
            qnf, rq_x = _heads_norm(qf, qw, eq, eq3)
            qn = _bf(qnf)
            kf = jnp.concatenate([kp_ref[...], kc_ref[...]], axis=0)
            knf, rk_x = _heads_norm(kf, kw, ek, ek3)
            kn = _bf(knf)
            vv = _bf(jnp.concatenate([vp_ref[...], vc_ref[...]], axis=0))
            d_of = do_ref[...]
            d_ob = _bf(d_of)
            lse_all = lse_ref[...]
            delta = _group_sum(d_of * o_ref[...], eq)
            gsk_ref[...] += jnp.sum(-jnp.exp(skv_ref[...] - lse_all) * delta, axis=0, keepdims=True)
            for hk in range(KV_HEADS):
                ks = slice(hk * HEAD_DIM, (hk + 1) * HEAD_DIM)
                qg = _stack_heads(qn, hk)
                sc = _dot_nt(qg, kn[:, ks]) * (HEAD_DIM ** -0.5)
                sc = sc + b_ref[hk * GRP:(hk + 1) * GRP].reshape(GRP * BLOCK, 2 * BLOCK)
                p = jnp.where(mask, jnp.exp(sc - _stack_cols(lse_all, hk)), 0.0)
                d_og = _stack_heads(d_ob, hk)
                ds = p * (_dot_nt(d_og, vv[:, ks]) - _stack_cols(delta, hk))
                dss_ref[hk * GRP:(hk + 1) * GRP] += ds.reshape(GRP, BLOCK, 2 * BLOCK)
                dsb = _bf(ds)
                dv_s[:, ks] = _dot_tn(_bf(p), d_og)
                dkn_s[:, ks] = _dot_tn(dsb, qg) * (HEAD_DIM ** -0.5)
                dqn = _dot(dsb, kn[:, ks]) * (HEAD_DIM ** -0.5)
                for g in range(GRP):
                    h = hk * GRP + g
                    dqn_s[:, h * HEAD_DIM:(h + 1) * HEAD_DIM] = dqn[g * BLOCK:(g + 1) * BLOCK]
            dq, gq = _heads_norm_bwd(qf, rq_x, qw, dqn_s[...], eq, eq3)
            dq_ref[...] = _bf(dq)
            gq_x[...] += gq
            dk, gk = _heads_norm_bwd(kf[:BLOCK], rk_x[:BLOCK], kw, ckn[...] + dkn_s[0:BLOCK, :], ek, ek3)
            dk_ref[...] = _bf(dk)
            gk_x[...] += gk
            dv_ref[...] = _bf(cv[...] + dv_s[0:BLOCK, :])
            ckn[...] = dkn_s[BLOCK:2 * BLOCK, :]
            cv[...] = dv_s[BLOCK:2 * BLOCK, :]

        @pl.when(i == nb)
        def _():
            kc = kc_ref[...]
            dk, gk = _heads_norm_bwd(kc, _heads_norm(kc, kw, ek, ek3)[1], kw, ckn[...], ek, ek3)
            dk_ref[...] = _bf(dk)
            dv_ref[...] = _bf(cv[...])
            gqw_ref[...] = _group_sum(jnp.broadcast_to(gq_x[...], (8, ATTN_W)), fq_ref[...])[0:1]
            gkw_ref[...] = _group_sum(jnp.broadcast_to(gk_x[...] + gk, (8, KV_W)), fk_ref[...])[0:1]

    last = nb - 1
    cur = lambda w: pl.BlockSpec((BLOCK, w), lambda i: (jnp.minimum(i, last), 0))
    prev = lambda w: pl.BlockSpec((BLOCK, w), lambda i: (jnp.maximum(jnp.minimum(i, last) - 1, 0), 0))
    late = lambda w: pl.BlockSpec((BLOCK, w), lambda i: (jnp.maximum(i - 1, 0), 0))
    sd = jax.ShapeDtypeStruct
    return pl.pallas_call(
        body, name="attn_bwd", grid=(nb + 1,),
        in_specs=[cur(ATTN_W), prev(KV_W), cur(KV_W), prev(KV_W), cur(KV_W),
                  pl.BlockSpec((ATTN_HEADS, BLOCK, 2 * BLOCK), lambda i: (0, 0, 0)), _full((1, ATTN_HEADS))]
                 + [_full(c.shape) for c in consts + folds] + [cur(ATTN_W), cur(ATTN_HEADS), cur(ATTN_W)],
        out_specs=[cur(ATTN_W), late(KV_W), late(KV_W),
                   pl.BlockSpec((ATTN_HEADS, BLOCK, 2 * BLOCK), lambda i: (0, 0, 0)),
                   _full((1, HEAD_DIM)), _full((1, HEAD_DIM)), _full((1, ATTN_HEADS))],
        out_shape=[sd((s, ATTN_W), BF), sd((s, KV_W), BF), sd((s, KV_W), BF),
                   sd((ATTN_HEADS, BLOCK, 2 * BLOCK), F32), sd((1, HEAD_DIM), F32), sd((1, HEAD_DIM), F32),
                   sd((1, ATTN_HEADS), F32)],
        scratch_shapes=[pltpu.VMEM((BLOCK, KV_W), F32), pltpu.VMEM((BLOCK, KV_W), F32),
                        pltpu.VMEM((BLOCK, ATTN_W), F32), pltpu.VMEM((2 * BLOCK, KV_W), F32),
                        pltpu.VMEM((2 * BLOCK, KV_W), F32), pltpu.VMEM((1, ATTN_W), F32), pltpu.VMEM((1, KV_W), F32)],
        compiler_params=_params(dimension_semantics=("arbitrary",)),
    )(q, k, k, v, v, bias, sinks, *consts, *folds, o_att, lse, d_o)


def _ssd_bwd(xbc, conv_all, dt_raw, conv_w, dt_bias, a_log, dsk_x, e_mat, e3t, hprev_all, dy_all):
    s = xbc.shape[0]
    nc = s // BLOCK
    gw = SSM_R * SSM_P
    b0, c0 = SSM_W, SSM_W + SSM_G * SSM_N

    def body(x_ref, conv_ref, dtr_ref, cw_ref, dtb_ref, alog_ref, dsk_ref, e_ref, e3_ref, hp_ref, dy_ref,
             dx_ref, ddt_ref, gcw_ref, gcb_ref, gdtb_ref, galog_ref, gdsk_ref,
             dh, nhead, ext2, gdskx, dxdt_s, dbc_s):
        i = pl.program_id(0)
        c = nc - 1 - i

        @pl.when(i == 0)
        def _():
            for ref in (dh, nhead, gdskx, gcw_ref, gcb_ref, gdtb_ref, galog_ref, gdsk_ref):
                ref[...] = jnp.zeros_like(ref)

        conv = conv_ref[...]
        sg, xact, u, dt, a, trilb, acum, dt_x, acum_x = _ssd_common(conv, dtr_ref, dtb_ref, alog_ref, e3_ref)
        xs = xact[:, :SSM_W]
        acum_t = acum.T
        ea_x = jnp.exp(acum_x)
        last_x = acum_x[BLOCK - 1:BLOCK, :]
        dte_x = jnp.exp(last_x - acum_x)
        cd_x = jnp.exp(last_x)
        xdt = xs * dt_x
        xw = xdt * dte_x
        hprev = hp_ref[0]
        dhn = dh[...]
        dy = dy_ref[...]
        gdskx[...] += jnp.sum(dy * xs, axis=0, keepdims=True)
        dyea = dy * ea_x
        lane = lax.broadcasted_iota(jnp.int32, (BLOCK, SSM_HEADS), 1)
        dacum = jnp.zeros((BLOCK, SSM_HEADS), F32)
        dacc_x, dlast_x = [], []
        for g in range(SSM_G):
            bgf = xact[:, b0 + g * SSM_N:b0 + (g + 1) * SSM_N]
            cgf = xact[:, c0 + g * SSM_N:c0 + (g + 1) * SSM_N]
            bg, cg = _bf(bgf), _bf(cgf)
            sl = slice(g * gw, (g + 1) * gw)
            hpg, dhg, dyeag = _bf(hprev[:, sl]), _bf(dhn[:, sl]), _bf(dyea[:, sl])
            cb = _dot_nt(cg, bg)
            gmat = _dot(cg, hpg)
            dxw = _dot(bg, dhg)
            dxdt_s[:, sl] = dxw * dte_x[:, sl]
            dacc_x.append(dy[:, sl] * gmat * ea_x[:, sl] - dxw * xw[:, sl])
            dlast_x.append(jnp.sum(dxw * xw[:, sl], axis=0, keepdims=True)
                           + jnp.sum(dhn[:, sl] * hprev[:, sl], axis=0, keepdims=True) * cd_x[:, sl])
            dcg = _dot_nt(dyeag, hpg)
            dbg = _dot_nt(_bf(xw[:, sl]), dhg)
            dh[:, sl] = dhn[:, sl] * cd_x[:, sl] + _dot_tn(cg, dyeag)
            dcb = jnp.zeros((BLOCK, BLOCK), F32)
            for r in range(SSM_R):
                hh = g * SSM_R + r
                hs = slice(hh * SSM_P, (hh + 1) * SSM_P)
                seg = jnp.where(trilb, acum[:, hh:hh + 1] - acum_t[hh:hh + 1, :], -1e30)
                lm = jnp.exp(seg)
                mm = cb * lm
                dyh = _bf(dy[:, hs])
                dm = _dot_nt(dyh, _bf(xdt[:, hs]))
                dxdt_s[:, hs] += _dot_tn(_bf(mm), dyh)
                wm = dm * mm
                dcb = dcb + dm * lm
                dacum = dacum + _group_sum(wm - wm.T, (lane == hh).astype(BF))
            dcbb = _bf(dcb)
            dbc_s[:, g * SSM_N:(g + 1) * SSM_N] = dbg + _dot_tn(dcbb, cg)
            dbc_s[:, SSM_G * SSM_N + g * SSM_N:SSM_G * SSM_N + (g + 1) * SSM_N] = dcg + _dot(dcbb, bg)
        dxdt = dxdt_s[...]
        dxs = dy * dsk_ref[...] + dxdt * dt_x
        red = _group_sum(jnp.concatenate(
            [dxdt * xs, jnp.concatenate(dacc_x, axis=1),
             jnp.broadcast_to(jnp.concatenate(dlast_x, axis=1), (8, SSM_W))], axis=0), e_ref[...])
        row = lax.broadcasted_iota(jnp.int32, (BLOCK, SSM_HEADS), 0)
        dacum = dacum + red[BLOCK:2 * BLOCK] + jnp.where(row == BLOCK - 1, red[2 * BLOCK:2 * BLOCK + 1], 0.0)
        ddta = _exact_left(_triu().astype(BF), dacum)
        ddt = red[:BLOCK] + ddta * a
        galog_ref[...] += jnp.sum(ddta * dt, axis=0, keepdims=True) * a
        du = ddt * _sig(u)
        ddt_ref[...] = _bf(du)
        gdtb_ref[...] += jnp.sum(du, axis=0, keepdims=True)
        dconv = jnp.concatenate([dxs, dbc_s[...]], axis=1) * _dsilu(conv, sg)
        gcb_ref[...] += jnp.sum(dconv, axis=0, keepdims=True)
        ext2[0:BLOCK, :] = dconv
        ext2[BLOCK:BLOCK + 8, :] = nhead[...]
        ahead = [ext2[3 - j:3 - j + BLOCK, :] for j in range(CONV_K)]
        dx_ref[...] = _bf(sum(ahead[j] * cw_ref[j:j + 1, :] for j in range(CONV_K)))
        xraw = x_ref[...]
        gcw_ref[...] += jnp.concatenate([jnp.sum(ahead[j] * xraw, axis=0, keepdims=True) for j in range(CONV_K)], axis=0)
        nhead[...] = dconv[0:8]

        @pl.when(i == nc - 1)
        def _():
            gdsk_ref[...] = _group_sum(jnp.broadcast_to(gdskx[...], (8, SSM_W)), e_ref[...])[0:1]

    chunk = lambda w: pl.BlockSpec((BLOCK, w), lambda i: (nc - 1 - i, 0))
    sd = jax.ShapeDtypeStruct
    return pl.pallas_call(
        body, name="ssd_bwd", grid=(nc,),
        in_specs=[chunk(XBC_W), chunk(XBC_W),
                  chunk(SSM_HEADS), _full((CONV_K, XBC_W)), _full((1, SSM_HEADS)),
                  _full((1, SSM_HEADS)), _full((1, SSM_W)), _full((SSM_W, SSM_HEADS)), _full((3 * SSM_HEADS, SSM_W)),
                  pl.BlockSpec((1, SSM_N, SSM_W), lambda i: (nc - 1 - i, 0, 0)), chunk(SSM_W)],
        out_specs=[chunk(XBC_W), chunk(SSM_HEADS), _full((CONV_K, XBC_W)), _full((1, XBC_W)),
                   _full((1, SSM_HEADS)), _full((1, SSM_HEADS)), _full((1, SSM_HEADS))],
        out_shape=[sd((s, XBC_W), BF), sd((s, SSM_HEADS), BF), sd((CONV_K, XBC_W), F32), sd((1, XBC_W), F32),
                   sd((1, SSM_HEADS), F32), sd((1, SSM_HEADS), F32), sd((1, SSM_HEADS), F32)],
        scratch_shapes=[pltpu.VMEM((SSM_N, SSM_W), F32), pltpu.VMEM((8, XBC_W), F32),
                        pltpu.VMEM((BLOCK + 8, XBC_W), F32),
                        pltpu.VMEM((1, SSM_W), F32), pltpu.VMEM((BLOCK, SSM_W), F32),
                        pltpu.VMEM((BLOCK, 2 * SSM_G * SSM_N), F32)],
        compiler_params=_params(dimension_semantics=("arbitrary",)),
    )(xbc, conv_all, dt_raw, conv_w, dt_bias, a_log, dsk_x, e_mat, e3t, hprev_all, dy_all)


def _dh(x, dout, norm_w, scale, dsegs, w_t, tm=256):
    s = x.shape[0]

    def body(x_ref, dout_ref, nw_ref, sc_ref, *rest):
        d_refs, w_hbm = rest[:9], rest[9]
        gx_ref, dshift_ref, dscale_ref, gnw_ref = rest[10:14]
        w_vm, sem = rest[14], rest[15]

        @pl.when(pl.program_id(0) == 0)
        def _():
            cp = pltpu.make_async_copy(w_hbm, w_vm, sem)
            cp.start()
            for ref in (dshift_ref, dscale_ref, gnw_ref):
                ref[...] = jnp.zeros_like(ref)
            cp.wait()

        dh = _dot(d_refs[0][...], w_vm[SEG_OFF[0]:SEG_OFF[1], :])
        for j in range(1, 9):
            dh = dh + _dot(d_refs[j][...], w_vm[SEG_OFF[j]:SEG_OFF[j + 1], :])
        xv = x_ref[...]
        r = lax.rsqrt(jnp.mean(xv * xv, axis=-1, keepdims=True) + EPS)
        xn = xv * r
        nw = nw_ref[...]
        sc1 = 1.0 + sc_ref[...]
        dshift_ref[...] += jnp.sum(dh, axis=0, keepdims=True)
        dhxn = jnp.sum(dh * xn, axis=0, keepdims=True)
        dscale_ref[...] += dhxn * nw
        gnw_ref[...] += dhxn * sc1
        dxn = dh * (nw * sc1)
        gx_ref[...] = dout_ref[...] + r * (dxn - xn * jnp.mean(xn * dxn, axis=-1, keepdims=True))

    vec = _full((1, D_MODEL))
    sd = jax.ShapeDtypeStruct
    return pl.pallas_call(
        body, name="dh", grid=(s // tm,),
        in_specs=[_rows(tm, D_MODEL), _rows(tm, D_MODEL), vec, vec] + [_rows(tm, w) for w in SEG_W] + [ANY],
        out_specs=[_rows(tm, D_MODEL), vec, vec, vec],
        out_shape=[sd((s, D_MODEL), F32), sd((1, D_MODEL), F32), sd((1, D_MODEL), F32), sd((1, D_MODEL), F32)],
        scratch_shapes=[pltpu.VMEM((IN_W, D_MODEL), BF), pltpu.SemaphoreType.DMA],
        compiler_params=_params(dimension_semantics=("arbitrary",)),
    )(x, dout, norm_w, scale, *dsegs, w_t)


def _gw_seg(h, dseg, name, tm=512):
    s, w = dseg.shape
    tn = min(w, 1024)
    tm = min(tm, s)

    def body(h_ref, d_ref, o_ref):
        @pl.when(pl.program_id(1) == 0)
        def _():
            o_ref[...] = jnp.zeros_like(o_ref)

        o_ref[...] += _dot_tn(d_ref[...], h_ref[...])

    return pl.pallas_call(
        body, name=name, grid=(w // tn, s // tm),
        in_specs=[pl.BlockSpec((tm, D_MODEL), lambda n, m: (m, 0)), pl.BlockSpec((tm, tn), lambda n, m: (m, n))],
        out_specs=pl.BlockSpec((tn, D_MODEL), lambda n, m: (n, 0)),
        out_shape=jax.ShapeDtypeStruct((w, D_MODEL), F32),
        compiler_params=_params(dimension_semantics=("arbitrary", "arbitrary")),
    )(h, dseg)


def _gw_in(h, dsegs):
    return [_gw_seg(h, d, "gw_in_%d" % j) for j, d in enumerate(dsegs)]


def _local_step(x, tgt, shift, scale, gate, w_t, rows_fn, norm_w, qnw, knw, rel_bias, sinks,
                conv_w, conv_b, dt_bias, a_log, d_skip, ssm_nw, after_mid=None, after_gw=None):
    oh_t = _bucket_onehot_t()
    bias = _bias_dense(rel_bias.T, oh_t).reshape(ATTN_HEADS, BLOCK, 2 * BLOCK)
    *segs, h = _inproj(x, norm_w, scale, shift, w_t)
    q, k, v, za, zm, xbc, dtr, ga, gb = segs
    consts = _attn_consts(qnw, knw)
    o_att, lse = _attn_fwd(q, k, v, bias, sinks, consts)
    e_mat, e3t = _membership(SSM_W, SSM_P, SSM_HEADS)
    dsk_x = jnp.repeat(d_skip, SSM_P, axis=1)
    ypre, hprev, conv = _ssd_fwd(xbc, dtr, conv_w, conv_b, dt_bias, a_log, dsk_x, e3t)
    wap, wsp, wout = rows_fn(ypre)
    (dout, d_o, dza, dyp, dzm, dga, dgb, yag, dy_a, yn, dy_b, merged, dob, g_ssm_nw, dgate, loss) = _mid(
        x, tgt, o_att, za, ypre, zm, ga, gb, gate, ssm_nw, wap, wsp, wout)
    g_wap = _gw_seg(dy_a, yag, "gw_attn_proj")
    g_wsp = _gw_seg(dy_b, yn, "gw_ssm_proj")
    g_wout = _gw_seg(dob, merged, "gw_out")
    zero = after_mid(g_wap, g_wsp, g_wout) if after_mid is not None else 0.0
    dq, dk, dv, dss, g_qnw, g_knw, g_sinks = _attn_bwd(q, k, v, bias, sinks + zero, consts, o_att, lse, d_o)
    g_rel = _bias_grad(dss.reshape(ATTN_HEADS, BLOCK * 2 * BLOCK), oh_t).T
    dxbc, ddt, g_cw, g_cb, g_dtb, g_alog, g_dsk = _ssd_bwd(
        xbc, conv, dtr, conv_w, dt_bias, a_log, dsk_x, e_mat, e3t, hprev, dyp)
    dsegs = (dq, dk, dv, dza, dzm, dxbc, ddt, dga, dgb)
    g_ws = _gw_in(h, dsegs)
    zero = after_gw(g_ws) if after_gw is not None else 0.0
    gx, dshift, dscale, g_nw = _dh(x, dout, norm_w + zero, scale, dsegs, w_t)
    return dict(loss=loss, grad_x=gx, dmod=jnp.concatenate([dshift, dscale, dgate], axis=1), g_ws=g_ws,
                g_wap=g_wap, g_wsp=g_wsp, g_wout=g_wout, g_norm_w=g_nw, g_qnw=g_qnw, g_knw=g_knw, g_rel=g_rel,
                g_sinks=g_sinks, g_conv_w=g_cw, g_conv_b=g_cb, g_dt_bias=g_dtb, g_a_log=g_alog, g_d_skip=g_dsk,
                g_ssm_nw=g_ssm_nw)


def _me():
    return lax.axis_index("x"), lax.axis_index("y"), lax.axis_index("c")


def _flip(v, bit):
    return 1 - v if bit else v


def _ag_direct(v, name):
    def body(v_ref, out_ref, send_sems, recv_sems, local_sem):
        x, y, c = _me()
        me = 4 * x + 2 * y + c
        mine = pltpu.make_async_copy(v_ref, out_ref.at[me], local_sem)
        mine.start()
        peers = [(_flip(x, k >> 2 & 1), _flip(y, k >> 1 & 1), _flip(c, k & 1)) for k in range(1, N_DEV)]
        sends = [pltpu.make_async_remote_copy(
            src_ref=v_ref, dst_ref=out_ref.at[me], send_sem=send_sems.at[j], recv_sem=recv_sems.at[j],
            device_id=p, device_id_type=MESH) for j, p in enumerate(peers)]
        for cp in sends:
            cp.start()
        for j, (px, py, pc) in enumerate(peers):
            pltpu.make_async_remote_copy(
                src_ref=v_ref, dst_ref=out_ref.at[4 * px + 2 * py + pc], send_sem=send_sems.at[j],
                recv_sem=recv_sems.at[j], device_id=(px, py, pc), device_id_type=MESH).wait_recv()
        for cp in sends:
            cp.wait_send()
        mine.wait()

    vm = pl.BlockSpec(memory_space=pltpu.VMEM)
    return pl.pallas_call(
        body, name=name, out_shape=jax.ShapeDtypeStruct((N_DEV,) + v.shape, v.dtype),
        in_specs=[vm], out_specs=vm,
        scratch_shapes=[pltpu.SemaphoreType.DMA((N_DEV - 1,)), pltpu.SemaphoreType.DMA((N_DEV - 1,)),
                        pltpu.SemaphoreType.DMA],
        compiler_params=_params(),
    )(v)


def _ag_two_level(v, name):
    def body(v_ref, out_ref, send_sems, recv_sems, local_sem):
        x, y, c = _me()
        me, sibling = (x, y, c), (x, y, 1 - c)
        chips = [(1 - x, y), (x, 1 - y), (1 - x, 1 - y)]

        def slot(px, py, pc):
            return out_ref.at[4 * px + 2 * py + pc]

        def copy(k, block, to, src=None):
            return pltpu.make_async_remote_copy(
                src_ref=slot(*block) if src is None else src, dst_ref=slot(*block),
                send_sem=send_sems.at[k], recv_sem=recv_sems.at[k], device_id=to, device_id_type=MESH)

        mine = pltpu.make_async_copy(v_ref, slot(*me), local_sem)
        mine.start()
        first = [copy(0, me, sibling, src=v_ref)]
        first += [copy(1 + j, me, (*chip, c), src=v_ref) for j, chip in enumerate(chips)]
        for cp in first:
            cp.start()
        passed = [copy(4 + j, (*chip, c), sibling) for j, chip in enumerate(chips)]
        for j, chip in enumerate(chips):
            copy(1 + j, (*chip, c), me).wait_recv()
            passed[j].start()
        copy(0, sibling, me).wait_recv()
        for j, chip in enumerate(chips):
            copy(4 + j, (*chip, 1 - c), me).wait_recv()
        for cp in first + passed:
            cp.wait_send()
        mine.wait()

    return pl.pallas_call(
        body, name=name, out_shape=jax.ShapeDtypeStruct((N_DEV,) + v.shape, v.dtype),
        in_specs=[ANY], out_specs=ANY,
        scratch_shapes=[pltpu.SemaphoreType.DMA((7,)), pltpu.SemaphoreType.DMA((7,)), pltpu.SemaphoreType.DMA],
        compiler_params=_params(),
    )(v)


def _rs_sibling(g, name):
    def body(g_ref, out_ref, send_sems, recv_sems):
        x, y, c = _me()
        cps = [pltpu.make_async_remote_copy(
            src_ref=g_ref.at[2 * ch + 1 - c], dst_ref=out_ref.at[ch], send_sem=send_sems.at[ch],
            recv_sem=recv_sems.at[ch], device_id=(x, y, 1 - c), device_id_type=MESH) for ch in range(4)]
        for cp in cps:
            cp.start()
        for cp in cps:
            cp.wait()

    return pl.pallas_call(
        body, name=name, out_shape=jax.ShapeDtypeStruct((4,) + g.shape[1:], g.dtype),
        in_specs=[ANY], out_specs=ANY,
        scratch_shapes=[pltpu.SemaphoreType.DMA((4,)), pltpu.SemaphoreType.DMA((4,))],
        compiler_params=_params(),
    )(g)


def _add_sibling(g, got, name):
    _, r, n = g.shape
    tr = min(r, 256)

    def body(c_ref, a_ref, b_ref, o_ref):
        o_ref[...] = a_ref[...] + b_ref[...]

    grid_spec = pltpu.PrefetchScalarGridSpec(
        num_scalar_prefetch=1, grid=(4, r // tr),
        in_specs=[pl.BlockSpec((1, tr, n), lambda ch, i, c_ref: (2 * ch + c_ref[0], i, 0)),
                  pl.BlockSpec((1, tr, n), lambda ch, i, c_ref: (ch, i, 0))],
        out_specs=pl.BlockSpec((1, tr, n), lambda ch, i, c_ref: (ch, i, 0)))
    return pl.pallas_call(
        body, name=name, grid_spec=grid_spec, out_shape=jax.ShapeDtypeStruct((4, r, n), g.dtype),
        compiler_params=_params(dimension_semantics=("arbitrary", "arbitrary")),
    )(lax.axis_index("c").reshape(1).astype(jnp.int32), g, got)


def _rs_chips(p, name):
    def body(p_ref, out_ref, send_sems, recv_sems, local_sem):
        x, y, c = _me()
        my_chip = 2 * x + y
        mine = pltpu.make_async_copy(p_ref.at[my_chip], out_ref.at[my_chip], local_sem)
        mine.start()
        chips = [(1 - x, y), (x, 1 - y), (1 - x, 1 - y)]
        sends = [pltpu.make_async_remote_copy(
            src_ref=p_ref.at[2 * px + py], dst_ref=out_ref.at[my_chip], send_sem=send_sems.at[j],
            recv_sem=recv_sems.at[j], device_id=(px, py, c), device_id_type=MESH) for j, (px, py) in enumerate(chips)]
        for cp in sends:
            cp.start()
        for j, (px, py) in enumerate(chips):
            pltpu.make_async_remote_copy(
                src_ref=p_ref.at[my_chip], dst_ref=out_ref.at[2 * px + py], send_sem=send_sems.at[j],
                recv_sem=recv_sems.at[j], device_id=(px, py, c), device_id_type=MESH).wait_recv()
        for cp in sends:
            cp.wait_send()
        mine.wait()

    return pl.pallas_call(
        body, name=name, out_shape=jax.ShapeDtypeStruct(p.shape, p.dtype),
        in_specs=[ANY], out_specs=ANY,
        scratch_shapes=[pltpu.SemaphoreType.DMA((3,)), pltpu.SemaphoreType.DMA((3,)), pltpu.SemaphoreType.DMA],
        compiler_params=_params(),
    )(p)


HBM = pl.BlockSpec(memory_space=pltpu.HBM)
SEM = pl.BlockSpec(memory_space=pltpu.SEMAPHORE)
EFFECT = pltpu.SideEffectType.DATAFLOW_SIDE_EFFECTING


def _peers(x, y, c):
    return [(_flip(x, k >> 2 & 1), _flip(y, k >> 1 & 1), _flip(c, k & 1)) for k in range(1, N_DEV)]


def _exchange_start(src, land, gather, name):
    def body(src_ref, land_ref, send_sems, recv_sems, src_thru, land_thru, token):
        x, y, c = _me()
        me = 4 * x + 2 * y + c
        for j, (px, py, pc) in enumerate(_peers(x, y, c)):
            pltpu.make_async_remote_copy(
                src_ref=src_ref if gather else src_ref.at[4 * px + 2 * py + pc], dst_ref=land_ref.at[me],
                send_sem=send_sems.at[j], recv_sem=recv_sems.at[j], device_id=(px, py, pc), device_id_type=MESH).start()
        token[...] = jnp.zeros_like(token)

    sems = pltpu.SemaphoreType.DMA((N_DEV - 1,))
    out = pl.pallas_call(
        body, name=name,
        out_shape=(sems, sems, pltpu.HBM(src.shape, src.dtype), pltpu.HBM(land.shape, land.dtype),
                   jax.ShapeDtypeStruct((8, 128), F32)),
        in_specs=(HBM, HBM), out_specs=(SEM, SEM, HBM, HBM, pl.BlockSpec(memory_space=pltpu.VMEM)),
        input_output_aliases={0: 2, 1: 3},
        compiler_params=pltpu.CompilerParams(has_side_effects=EFFECT),
    )(pltpu.with_memory_space_constraint(src, pltpu.HBM), pltpu.with_memory_space_constraint(land, pltpu.HBM))
    return out[:4], out[4][0, 0]


def _exchange_wait(started, after, gather, name):
    send_sems, recv_sems, src_thru, land_thru = started

    def body(src_ref, land_ref, send_sems, recv_sems, after_ref, src_dead, got_ref):
        x, y, c = _me()
        for j, (px, py, pc) in enumerate(_peers(x, y, c)):
            pid = 4 * px + 2 * py + pc
            cp = pltpu.make_async_remote_copy(
                src_ref=src_ref if gather else src_ref.at[pid], dst_ref=land_ref.at[pid],
                send_sem=send_sems.at[j], recv_sem=recv_sems.at[j], device_id=(px, py, pc), device_id_type=MESH)
            cp.wait_send()
            cp.wait_recv()

    return pl.pallas_call(
        body, name=name,
        out_shape=(pltpu.HBM(src_thru.shape, src_thru.dtype), pltpu.HBM(land_thru.shape, land_thru.dtype)),
        in_specs=(HBM, HBM, SEM, SEM, ANY), out_specs=(HBM, HBM), input_output_aliases={0: 0, 1: 1},
        compiler_params=pltpu.CompilerParams(has_side_effects=EFFECT),
    )(src_thru, land_thru, send_sems, recv_sems, after)[1]


def _reduce_scatter(g, name):
    got = _rs_sibling(g, name + "_sib")
    return _rs_chips(_add_sibling(g, got, name + "_add"), name + "_chips")


def _silu(a):
    return a * _sig(a)


def _mod_piece(c_all, w_ada, b_piece):
    def body(c_ref, w_ref, b_ref, o_ref):
        o_ref[...] = _dot(_bf(_silu(c_ref[...])), _bf(w_ref[...])) + b_ref[...]

    return pl.pallas_call(
        body, name="mod_piece", out_shape=jax.ShapeDtypeStruct((c_all.shape[0], w_ada.shape[1]), F32),
        compiler_params=_params(),
    )(c_all, w_ada, b_piece)


def _gw_ada(c_all, dmod_piece):
    def body(c_ref, d_ref, o_ref):
        o_ref[...] = _dot_tn(_bf(_silu(c_ref[...])), _bf(d_ref[...]))

    return pl.pallas_call(
        body, name="gw_ada", out_shape=jax.ShapeDtypeStruct((c_all.shape[1], dmod_piece.shape[1]), F32),
        compiler_params=_params(),
    )(c_all, dmod_piece)


def _adam(parts, w, m, v, name):
    k, r, n = parts.shape
    if r <= 256 or r % 256 == 0:
        tr, tn = min(r, 256), n
    else:
        tr, tn = r, 256
    assert r % tr == 0 and n % tn == 0

    def body(p_ref, w_ref, m_ref, v_ref, g_ref, d_ref, nm_ref, nv_ref):
        g = p_ref[0].astype(F32)
        for j in range(1, k):
            g = g + p_ref[j].astype(F32)
        m_new = ADAM_B1 * m_ref[...] + (1.0 - ADAM_B1) * g
        v_new = ADAM_B2 * v_ref[...] + (1.0 - ADAM_B2) * jnp.square(g)
        m_hat = m_new / (1.0 - ADAM_B1 ** ADAM_STEP)
        v_hat = v_new / (1.0 - ADAM_B2 ** ADAM_STEP)
        g_ref[...] = g
        d_ref[...] = -ADAM_LR * (m_hat / (jnp.sqrt(v_hat) + ADAM_EPS) + ADAM_WD * w_ref[...])
        nm_ref[...] = m_new
        nv_ref[...] = v_new

    blk = pl.BlockSpec((tr, tn), lambda i, j: (i, j))
    return pl.pallas_call(
        body, name=name, grid=(r // tr, n // tn),
        in_specs=[pl.BlockSpec((k, tr, tn), lambda i, j: (0, i, j)), blk, blk, blk],
        out_specs=[blk, blk, blk, blk],
        out_shape=[jax.ShapeDtypeStruct((r, n), F32)] * 4,
        compiler_params=_params(dimension_semantics=("arbitrary", "arbitrary")),
    )(parts, w, m, v)


_SMALL = (("b_ada", 3 * D_MODEL), ("norm_w", D_MODEL), ("q_norm_w", HEAD_DIM), ("k_norm_w", HEAD_DIM),
          ("rel_bias", REL_BUCKETS * ATTN_HEADS), ("sinks", ATTN_HEADS), ("conv_b", XBC_W), ("dt_bias", SSM_HEADS),
          ("a_log", SSM_HEADS), ("d_skip", SSM_HEADS), ("ssm_norm_w", SSM_W))
_SMALL_N = sum(n for _, n in _SMALL)
_SMALL_PAD = -(-_SMALL_N // 128) * 128
_PACK_N = _SMALL_PAD + CONV_K * XBC_W


def _pack_small(d):
    parts = [d[name].reshape(1, n) for name, n in _SMALL]
    return jnp.concatenate(parts + [jnp.zeros((1, _SMALL_PAD - _SMALL_N), F32)], axis=1)


def _unpack_small(vec, shapes):
    out, off = {}, 0
    for name, n in _SMALL:
        out[name] = vec[:, off:off + n].reshape(shapes[name])
        off += n
    return out


WEIGHTS = ("w_ada", "b_ada", "norm_w", "w_in", "q_norm_w", "k_norm_w", "rel_bias", "sinks", "conv_w", "conv_b",
           "dt_bias", "a_log", "d_skip", "ssm_norm_w", "w_attn_proj", "w_ssm_proj", "w_out")


def kernel(x, c, w_ada, b_ada, norm_w, w_in, q_norm_w, k_norm_w, rel_bias, sinks, conv_w, conv_b, dt_bias, a_log, d_skip, ssm_norm_w, w_attn_proj, w_ssm_proj, w_out, loss_target, m_w_ada, m_b_ada, m_norm_w, m_w_in, m_q_norm_w, m_k_norm_w, m_rel_bias, m_sinks, m_conv_w, m_conv_b, m_dt_bias, m_a_log, m_d_skip, m_ssm_norm_w, m_w_attn_proj, m_w_ssm_proj, m_w_out, v_w_ada, v_b_ada, v_norm_w, v_w_in, v_q_norm_w, v_k_norm_w, v_rel_bias, v_sinks, v_conv_w, v_conv_b, v_dt_bias, v_a_log, v_d_skip, v_ssm_norm_w, v_w_attn_proj, v_w_ssm_proj, v_w_out):
    w = dict(w_ada=w_ada, b_ada=b_ada, norm_w=norm_w, w_in=w_in, q_norm_w=q_norm_w, k_norm_w=k_norm_w,
             rel_bias=rel_bias, sinks=sinks, conv_w=conv_w, conv_b=conv_b, dt_bias=dt_bias, a_log=a_log,
             d_skip=d_skip, ssm_norm_w=ssm_norm_w, w_attn_proj=w_attn_proj, w_ssm_proj=w_ssm_proj, w_out=w_out)
    m = dict(w_ada=m_w_ada, b_ada=m_b_ada, norm_w=m_norm_w, w_in=m_w_in, q_norm_w=m_q_norm_w, k_norm_w=m_k_norm_w,
             rel_bias=m_rel_bias, sinks=m_sinks, conv_w=m_conv_w, conv_b=m_conv_b, dt_bias=m_dt_bias, a_log=m_a_log,
             d_skip=m_d_skip, ssm_norm_w=m_ssm_norm_w, w_attn_proj=m_w_attn_proj, w_ssm_proj=m_w_ssm_proj, w_out=m_w_out)
    v = dict(w_ada=v_w_ada, b_ada=v_b_ada, norm_w=v_norm_w, w_in=v_w_in, q_norm_w=v_q_norm_w, k_norm_w=v_k_norm_w,
             rel_bias=v_rel_bias, sinks=v_sinks, conv_w=v_conv_w, conv_b=v_conv_b, dt_bias=v_dt_bias, a_log=v_a_log,
             d_skip=v_d_skip, ssm_norm_w=v_ssm_norm_w, w_attn_proj=v_w_attn_proj, w_ssm_proj=v_w_ssm_proj, w_out=v_w_out)
    me = 4 * lax.axis_index("x") + 2 * lax.axis_index("y") + lax.axis_index("c")
    ada_n = w_ada.shape[2]
    in_n = w_in.shape[2]
    cw_n = conv_w.shape[2]

    first = _ag_direct(jnp.concatenate([c, conv_w[0].reshape(1, CONV_K * cw_n)], axis=1), "ag_c")[:, 0]
    c_all = first[:, :D_MODEL]
    conv_w_full = first[:, D_MODEL:].reshape(N_DEV, CONV_K, cw_n).transpose(1, 0, 2).reshape(CONV_K, XBC_W)
    b_piece = lax.dynamic_slice_in_dim(b_ada, me * ada_n, ada_n, axis=1)
    mod_all = _ag_direct(_mod_piece(c_all, w_ada[0], b_piece), "ag_mod")
    mod = lax.dynamic_index_in_dim(mod_all, me, axis=1, keepdims=False).reshape(1, 3 * D_MODEL)
    shift, scale, gate = mod[:, :D_MODEL], mod[:, D_MODEL:2 * D_MODEL], mod[:, 2 * D_MODEL:]

    w_t = _ag_two_level(w_in[0].T.astype(BF), "ag_w_in").reshape(N_DEV * in_n, D_MODEL)

    def with_mine(blocks, mine):
        return lax.dynamic_update_index_in_dim(jnp.zeros(blocks, mine.dtype), mine, me, axis=0)

    rows = jnp.concatenate([w_attn_proj[0], w_ssm_proj[0], w_out[0]], axis=0).astype(BF)
    r_ap, r_sp = w_attn_proj.shape[1], w_ssm_proj.shape[1]
    rows_started, zero = _exchange_start(rows, with_mine((N_DEV,) + rows.shape, rows), True, "ag_rows_start")

    def rows_fn(after):
        rows_all = _exchange_wait(rows_started, after, True, "ag_rows_wait")
        return (rows_all[:, :r_ap].reshape(ATTN_W, D_MODEL), rows_all[:, r_ap:r_ap + r_sp].reshape(SSM_W, D_MODEL),
                rows_all[:, r_ap + r_sp:].reshape(D_MODEL, D_MODEL))

    started = {}

    def send_blocks(key, g, name):
        g = g.astype(BF)
        started[key], zero = _exchange_start(
            g, with_mine(g.shape, lax.dynamic_index_in_dim(g, me, axis=0, keepdims=False)), False, name)
        return zero

    def after_mid(g_wap, g_wsp, g_wout):
        return send_blocks("rows", jnp.concatenate(
            [g_wap.reshape(N_DEV, r_ap, D_MODEL), g_wsp.reshape(N_DEV, r_sp, D_MODEL),
             g_wout.reshape(N_DEV, r_ap, D_MODEL)], axis=1), "rs_rows_start")

    def after_gw(g_ws):
        return send_blocks("in", jnp.concatenate(g_ws, axis=0).reshape(N_DEV, in_n, D_MODEL), "rs_in_start")

    r = _local_step(x[0], loss_target[0], shift, scale + zero, gate, w_t, rows_fn, norm_w, q_norm_w, k_norm_w,
                    rel_bias, sinks, conv_w_full, conv_b, dt_bias, a_log, d_skip, ssm_norm_w, after_mid, after_gw)

    loss = lax.psum(r["loss"][0, 0], ("x", "y", "c"))

    small = dict(b_ada=r["dmod"], norm_w=r["g_norm_w"], q_norm_w=r["g_qnw"], k_norm_w=r["g_knw"], rel_bias=r["g_rel"],
                 sinks=r["g_sinks"], conv_b=r["g_conv_b"], dt_bias=r["g_dt_bias"], a_log=r["g_a_log"],
                 d_skip=r["g_d_skip"], ssm_norm_w=r["g_ssm_nw"])
    pack = jnp.concatenate([_pack_small(small), r["g_conv_w"].reshape(1, CONV_K * XBC_W)], axis=1)
    pack_all = _ag_direct(pack, "ag_small")
    shapes = {name: w[name].shape for name, _ in _SMALL}
    res = {}
    g_s, d_s, m_s, v_s = _adam(pack_all[:, :, :_SMALL_PAD], _pack_small(w), _pack_small(m), _pack_small(v), "adam_small")
    for name, arr in _unpack_small(g_s, shapes).items():
        res[name] = [arr]
    for vec in (d_s, m_s, v_s):
        for name, arr in _unpack_small(vec, shapes).items():
            res[name].append(arr)
    cw_parts = pack_all[:, 0, _SMALL_PAD:].reshape(N_DEV, CONV_K, XBC_W)
    cw_mine = lax.dynamic_slice_in_dim(cw_parts, me * cw_n, cw_n, axis=2)
    res["conv_w"] = [a[None] for a in _adam(cw_mine, conv_w[0], m_conv_w[0], v_conv_w[0], "adam_conv_w")]

    dmod_piece = lax.dynamic_slice_in_dim(pack_all[:, 0, :3 * D_MODEL], me * ada_n, ada_n, axis=1)
    g_ada = _gw_ada(c_all, dmod_piece)
    res["w_ada"] = [a[None] for a in _adam(g_ada[None], w_ada[0], m_w_ada[0], v_w_ada[0], "adam_w_ada")]

    cat = lambda d: jnp.concatenate([d["w_attn_proj"][0], d["w_ssm_proj"][0], d["w_out"][0]], axis=0)
    rows_res = _adam(_exchange_wait(started["rows"], g_ada, False, "rs_rows_wait"), cat(w), cat(m), cat(v), "adam_w_rows")
    res["w_in"] = [a.T[None] for a in _adam(_exchange_wait(started["in"], rows_res[0], False, "rs_in_wait"),
                                            w_in[0].T, m_w_in[0].T, v_w_in[0].T, "adam_w_in")]
    res["w_attn_proj"] = [a[None, :r_ap] for a in rows_res]
    res["w_ssm_proj"] = [a[None, r_ap:r_ap + r_sp] for a in rows_res]
    res["w_out"] = [a[None, r_ap + r_sp:] for a in rows_res]

    outs = [loss, r["grad_x"][None]]
    for j in range(4):
        outs += [res[name][j] for name in WEIGHTS]
    return tuple(outs)
```

```python
import functools
import math

import numpy as np
import jax
import jax.numpy as jnp
from jax import lax
from jax.experimental import pallas as pl
from jax.experimental.pallas import tpu as pltpu

F32 = jnp.float32
BF = jnp.bfloat16
HI = lax.Precision.HIGHEST

D_MODEL = 1024
ATTN_HEADS = 16
KV_HEADS = 4
GRP = ATTN_HEADS // KV_HEADS
HEAD_DIM = 64
ATTN_W = ATTN_HEADS * HEAD_DIM
KV_W = KV_HEADS * HEAD_DIM
BLOCK = 128
REL_BUCKETS = 32
REL_MAX_DIST = 128
SSM_W = 2048
SSM_P = 64
SSM_HEADS = 32
SSM_G = 4
SSM_R = 8
SSM_N = 128
CONV_K = 4
XBC_W = SSM_W + 2 * SSM_G * SSM_N
SEG_W = (ATTN_W, KV_W, KV_W, ATTN_W, SSM_W, XBC_W, SSM_HEADS, D_MODEL, D_MODEL)
SEG_OFF = tuple(int(v) for v in np.cumsum((0,) + SEG_W))
IN_W = SEG_OFF[-1]
GATE_SEGS = (3, 4, 7, 8)
EPS = 1e-6
N_DEV = 8
ADAM_LR, ADAM_B1, ADAM_B2, ADAM_EPS, ADAM_WD, ADAM_STEP = 0.001, 0.9, 0.999, 1e-08, 0.01, 10
VMEM_LIMIT = 60 * 1024 * 1024
MESH = pl.DeviceIdType.MESH
ANY = pl.BlockSpec(memory_space=pl.ANY)


def _dot(a, b, precision=None):
    return jnp.dot(a, b, preferred_element_type=F32, precision=precision)


def _dot_nt(a, b, precision=None):
    return lax.dot_general(a, b, (((1,), (1,)), ((), ())), preferred_element_type=F32, precision=precision)


def _dot_tn(a, b, precision=None):
    return lax.dot_general(a, b, (((0,), (0,)), ((), ())), preferred_element_type=F32, precision=precision)


def _bf(a):
    return a.astype(BF)


def _sig(a):
    return 1.0 / (1.0 + jnp.exp(-a))


def _params(**kw):
    return pltpu.CompilerParams(vmem_limit_bytes=VMEM_LIMIT, **kw)


def _full(shape):
    nd = len(shape)
    return pl.BlockSpec(shape, lambda i: (0,) * nd)


def _rows(tm, w):
    return pl.BlockSpec((tm, w), lambda i: (i, 0))


def _inproj(x, norm_w, scale, shift, w_t, tm=256):
    s = x.shape[0]

    def body(x_ref, nw_ref, sc_ref, sh_ref, w_hbm, *rest):
        outs, h_ref, w_vm, sem = rest[:9], rest[9], rest[10], rest[11]

        @pl.when(pl.program_id(0) == 0)
        def _():
            cp = pltpu.make_async_copy(w_hbm, w_vm, sem)
            cp.start()
            cp.wait()

        xv = x_ref[...]
        r = lax.rsqrt(jnp.mean(xv * xv, axis=-1, keepdims=True) + EPS)
        h = xv * r * (nw_ref[...] * (1.0 + sc_ref[...])) + sh_ref[...]
        hb = _bf(h)
        h_ref[...] = hb
        for j in range(9):
            outs[j][...] = _dot_nt(hb, w_vm[SEG_OFF[j]:SEG_OFF[j + 1], :]).astype(outs[j].dtype)

    vec = _full((1, D_MODEL))
    return pl.pallas_call(
        body, name="inproj", grid=(s // tm,),
        in_specs=[_rows(tm, D_MODEL), vec, vec, vec, ANY],
        out_specs=[_rows(tm, w) for w in SEG_W] + [_rows(tm, D_MODEL)],
        out_shape=[jax.ShapeDtypeStruct((s, w), BF if j in GATE_SEGS else F32) for j, w in enumerate(SEG_W)]
                  + [jax.ShapeDtypeStruct((s, D_MODEL), BF)],
        scratch_shapes=[pltpu.VMEM((IN_W, D_MODEL), BF), pltpu.SemaphoreType.DMA],
        compiler_params=_params(dimension_semantics=("arbitrary",)),
    )(x, norm_w, scale, shift, w_t)


def _bucket_onehot_t():
    qi = jnp.arange(BLOCK)[:, None]
    kj = jnp.arange(2 * BLOCK)[None, :]
    dist = qi + BLOCK - kj
    n = jnp.maximum(dist, 0)
    max_exact = REL_BUCKETS // 2
    nf = jnp.maximum(n, 1).astype(F32)
    large = max_exact + (jnp.log(nf / max_exact) / math.log(REL_MAX_DIST / max_exact)
                         * (REL_BUCKETS - max_exact)).astype(jnp.int32)
    large = jnp.minimum(large, REL_BUCKETS - 1)
    bucket = jnp.where(n < max_exact, n, large).reshape(1, BLOCK * 2 * BLOCK)
    return (bucket == jnp.arange(REL_BUCKETS)[:, None]).astype(F32)


def _bias_dense(rel_bias_t, oh_t):
    def body(rb_ref, oh_ref, o_ref):
        o_ref[...] = _dot(rb_ref[...], oh_ref[...], HI)

    return pl.pallas_call(
        body, name="bias_dense", out_shape=jax.ShapeDtypeStruct((ATTN_HEADS, BLOCK * 2 * BLOCK), F32),
        compiler_params=_params(),
    )(rel_bias_t, oh_t)


def _bias_grad(ds_sum, oh_t):
    def body(ds_ref, oh_ref, o_ref):
        o_ref[...] = _dot_nt(ds_ref[...], oh_ref[...], HI)

    return pl.pallas_call(
        body, name="bias_grad", out_shape=jax.ShapeDtypeStruct((ATTN_HEADS, REL_BUCKETS), F32),
        compiler_params=_params(),
    )(ds_sum, oh_t)


def _group_sum(a, e):
    hi = _bf(a)
    return _dot(hi, e) + _dot(_bf(a - hi.astype(F32)), e)


def _group_bcast(a, e3t):
    hi = _bf(a)
    r1 = a - hi.astype(F32)
    mid = _bf(r1)
    return _dot(jnp.concatenate([hi, mid, _bf(r1 - mid.astype(F32))], axis=1), e3t)


def _membership(width, group, ngroups):
    e = (jnp.arange(width)[:, None] // group == jnp.arange(ngroups)[None, :]).astype(BF)
    return e, jnp.tile(e.T, (3, 1))


def _fold(width, group):
    return (jnp.arange(width)[:, None] % group == jnp.arange(group)[None, :]).astype(BF)


def _heads_norm(t, w_x, e, e3t):
    r = lax.rsqrt(_group_sum(t * t, e) * (1.0 / HEAD_DIM) + EPS)
    r_x = _group_bcast(r, e3t)
    return t * r_x * w_x, r_x


def _heads_norm_bwd(t, r_x, w_x, d, e, e3t):
    wd = d * w_x
    corr = _group_bcast(_group_sum(t * wd, e) * (1.0 / HEAD_DIM), e3t)
    return r_x * wd - t * (r_x * r_x * r_x) * corr, jnp.sum(d * t * r_x, axis=0, keepdims=True)


def _stack_heads(a, hk):
    return jnp.concatenate([a[:, (hk * GRP + g) * HEAD_DIM:(hk * GRP + g + 1) * HEAD_DIM] for g in range(GRP)], axis=0)


def _stack_cols(a, hk):
    return jnp.concatenate([a[:, hk * GRP + g:hk * GRP + g + 1] for g in range(GRP)], axis=0)


def _window_mask(first):
    qi = jnp.bitwise_and(lax.broadcasted_iota(jnp.int32, (GRP * BLOCK, 2 * BLOCK), 0), BLOCK - 1)
    kj = lax.broadcasted_iota(jnp.int32, (GRP * BLOCK, 2 * BLOCK), 1)
    prev_ok = jnp.logical_and(kj > qi, jnp.logical_not(first))
    cur_ok = jnp.logical_and(kj >= BLOCK, kj - BLOCK <= qi)
    return jnp.logical_or(jnp.logical_and(kj < BLOCK, prev_ok), cur_ok)


def _attn_consts(qnw, knw):
    eq, eq3t = _membership(ATTN_W, HEAD_DIM, ATTN_HEADS)
    ek, ek3t = _membership(KV_W, HEAD_DIM, ATTN_HEADS)
    return (jnp.tile(qnw, (1, ATTN_HEADS)), jnp.tile(knw, (1, KV_HEADS)), eq, eq3t, ek, ek3t)


def _attn_fwd(q, k, v, bias, sinks, consts):
    s = q.shape[0]
    nb = s // BLOCK
    gq = GRP * BLOCK
    bias_t = bias.reshape(KV_HEADS, GRP, BLOCK, 2 * BLOCK).transpose(0, 3, 1, 2).reshape(KV_HEADS, 2 * BLOCK, gq)
    sink_rows = jnp.repeat(sinks.reshape(KV_HEADS, GRP), BLOCK, axis=1).reshape(KV_HEADS, 1, gq)
    eye = jnp.eye(gq, dtype=BF)

    def body(q_ref, kp_ref, kc_ref, vp_ref, vc_ref, b_ref, bt_ref, sk_ref, skr_ref, eye_ref,
             qw_ref, kw_ref, eq_ref, eq3_ref, ek_ref, ek3_ref, o_ref, lse_ref):
        i = pl.program_id(0)
        mask = _window_mask(i == 0)
        kj = lax.broadcasted_iota(jnp.int32, (2 * BLOCK, gq), 0)
        qi = jnp.bitwise_and(lax.broadcasted_iota(jnp.int32, (2 * BLOCK, gq), 1), BLOCK - 1)
        mask_t = jnp.logical_or(jnp.logical_and(kj < BLOCK, jnp.logical_and(kj > qi, i > 0)),
                                jnp.logical_and(kj >= BLOCK, kj - BLOCK <= qi))
        qn = _bf(_heads_norm(q_ref[...], qw_ref[...], eq_ref[...], eq3_ref[...])[0])
        kn = _bf(_heads_norm(jnp.concatenate([kp_ref[...], kc_ref[...]], axis=0), kw_ref[...], ek_ref[...], ek3_ref[...])[0])
        vv = _bf(jnp.concatenate([vp_ref[...], vc_ref[...]], axis=0))
        ones = jnp.ones((2 * BLOCK, HEAD_DIM), BF)
        lses = []
        for hk in range(KV_HEADS):
            ks = slice(hk * HEAD_DIM, (hk + 1) * HEAD_DIM)
            qg = _stack_heads(qn, hk)
            sc_t = jnp.where(mask_t, _dot_nt(kn[:, ks], qg) * (HEAD_DIM ** -0.5) + bt_ref[hk], -1e30)
            m_row = jnp.maximum(jnp.max(sc_t, axis=0, keepdims=True), skr_ref[hk])
            m = _dot_nt(eye_ref[...], _bf(jnp.broadcast_to(m_row, (8, gq))))[:, 0:1]
            sc = _dot_nt(qg, kn[:, ks]) * (HEAD_DIM ** -0.5)
            sc = sc + b_ref[hk * GRP:(hk + 1) * GRP].reshape(gq, 2 * BLOCK)
            p = _bf(jnp.exp(jnp.where(mask, sc, -1e30) - m))
            sink = jnp.concatenate([jnp.full((BLOCK, 1), sk_ref[0, hk * GRP + g], F32) for g in range(GRP)], axis=0)
            pv = _dot(p, jnp.concatenate([vv[:, ks], ones], axis=1))
            den = pv[:, HEAD_DIM:HEAD_DIM + 1] + jnp.exp(sink - m)
            out = pv[:, :HEAD_DIM] * (1.0 / den)
            lse = m + jnp.log(den)
            for g in range(GRP):
                h = hk * GRP + g
                o_ref[:, h * HEAD_DIM:(h + 1) * HEAD_DIM] = out[g * BLOCK:(g + 1) * BLOCK]
                lses.append(lse[g * BLOCK:(g + 1) * BLOCK])
        lse_ref[...] = jnp.concatenate(lses, axis=1)

    cur = lambda w: pl.BlockSpec((BLOCK, w), lambda i: (i, 0))
    prev = lambda w: pl.BlockSpec((BLOCK, w), lambda i: (jnp.maximum(i - 1, 0), 0))
    whole = lambda a: pl.BlockSpec(a.shape, lambda i: (0,) * a.ndim)
    return pl.pallas_call(
        body, name="attn_fwd", grid=(nb,),
        in_specs=[cur(ATTN_W), prev(KV_W), cur(KV_W), prev(KV_W), cur(KV_W), whole(bias), whole(bias_t),
                  pl.BlockSpec(memory_space=pltpu.SMEM), whole(sink_rows), whole(eye)] + [_full(c.shape) for c in consts],
        out_specs=[cur(ATTN_W), cur(ATTN_HEADS)],
        out_shape=[jax.ShapeDtypeStruct((s, ATTN_W), F32), jax.ShapeDtypeStruct((s, ATTN_HEADS), F32)],
        compiler_params=_params(dimension_semantics=("arbitrary",)),
    )(q, k, k, v, v, bias, bias_t, sinks, sink_rows, eye, *consts)


def _conv_taps(ext_ref, xbc, tail):
    ext_ref[0:8, :] = tail
    ext_ref[8:8 + BLOCK, :] = xbc
    return [ext_ref[5 + j:5 + j + BLOCK, :] for j in range(CONV_K)]


def _softplus(u):
    return jnp.maximum(u, 0.0) + jnp.log(1.0 + jnp.exp(-jnp.abs(u)))


def _tril():
    r = lax.broadcasted_iota(jnp.int32, (BLOCK, BLOCK), 0)
    c = lax.broadcasted_iota(jnp.int32, (BLOCK, BLOCK), 1)
    return r >= c


def _triu():
    r = lax.broadcasted_iota(jnp.int32, (BLOCK, BLOCK), 0)
    c = lax.broadcasted_iota(jnp.int32, (BLOCK, BLOCK), 1)
    return r <= c


def _exact_left(m01, a):
    hi = _bf(a)
    r1 = a - hi.astype(F32)
    mid = _bf(r1)
    return _dot(m01, hi) + _dot(m01, mid) + _dot(m01, _bf(r1 - mid.astype(F32)))


def _ssd_common(conv, dtr_ref, dtb_ref, alog_ref, e3_ref):
    sg = _sig(conv)
    xact = conv * sg
    u = dtr_ref[...] + dtb_ref[...]
    dt = _softplus(u)
    a = -jnp.exp(alog_ref[...])
    trilb = _tril()
    acum = _exact_left(trilb.astype(BF), dt * a)
    both = _group_bcast(jnp.concatenate([dt, acum], axis=0), e3_ref[...])
    dt_x, acum_x = both[:BLOCK], both[BLOCK:]
    return sg, xact, u, dt, a, trilb, acum, dt_x, acum_x


def _ssd_fwd(xbc, dt_raw, conv_w, conv_b, dt_bias, a_log, dsk_x, e3t):
    s = xbc.shape[0]
    nc = s // BLOCK

    def body(x_ref, tail_ref, dtr_ref, cw_ref, cb_ref, dtb_ref, alog_ref, dsk_ref, e3_ref,
             y_ref, hp_ref, conv_ref, hst, ext):
        i = pl.program_id(0)

        @pl.when(i == 0)
        def _():
            hst[...] = jnp.zeros_like(hst)

        tail = jnp.where(i > 0, tail_ref[...], 0.0)
        taps = _conv_taps(ext, x_ref[...], tail)
        conv = cb_ref[...] + sum(taps[j] * cw_ref[j:j + 1, :] for j in range(CONV_K))
        conv_ref[...] = conv
        _, xact, _, _, _, trilb, acum, dt_x, acum_x = _ssd_common(conv, dtr_ref, dtb_ref, alog_ref, e3_ref)
        xs = xact[:, :SSM_W]
        acum_t = acum.T
        ea_x = jnp.exp(acum_x)
        last_x = acum_x[BLOCK - 1:BLOCK, :]
        xdt = xs * dt_x
        xw = xdt * jnp.exp(last_x - acum_x)
        cd_x = jnp.exp(last_x)
        hprev = hst[...]
        hp_ref[0] = hprev
        dsk = dsk_ref[...]
        for g in range(SSM_G):
            bg = _bf(xact[:, SSM_W + g * SSM_N:SSM_W + (g + 1) * SSM_N])
            cg = _bf(xact[:, SSM_W + SSM_G * SSM_N + g * SSM_N:SSM_W + SSM_G * SSM_N + (g + 1) * SSM_N])
            sl = slice(g * SSM_R * SSM_P, (g + 1) * SSM_R * SSM_P)
            cb = _dot_nt(cg, bg)
            yoff = _dot(cg, _bf(hprev[:, sl])) * ea_x[:, sl]
            hst[:, sl] = hprev[:, sl] * cd_x[:, sl] + _dot_tn(bg, _bf(xw[:, sl]))
            for r in range(SSM_R):
                hh = g * SSM_R + r
                hs = slice(hh * SSM_P, (hh + 1) * SSM_P)
                seg = jnp.where(trilb, acum[:, hh:hh + 1] - acum_t[hh:hh + 1, :], -1e30)
                mm = cb * jnp.exp(seg)
                yd = _dot(_bf(mm), _bf(xdt[:, hs]))
                y_ref[:, hs] = yd + yoff[:, r * SSM_P:(r + 1) * SSM_P] + dsk[:, hs] * xs[:, hs]

    chunk = lambda w: pl.BlockSpec((BLOCK, w), lambda i: (i, 0))
    return pl.pallas_call(
        body, name="ssd_fwd", grid=(nc,),
        in_specs=[chunk(XBC_W), pl.BlockSpec((8, XBC_W), lambda i: (jnp.maximum(i * (BLOCK // 8) - 1, 0), 0)),
                  chunk(SSM_HEADS), _full((CONV_K, XBC_W)), _full((1, XBC_W)), _full((1, SSM_HEADS)),
                  _full((1, SSM_HEADS)), _full((1, SSM_W)), _full((3 * SSM_HEADS, SSM_W))],
        out_specs=[chunk(SSM_W), pl.BlockSpec((1, SSM_N, SSM_W), lambda i: (i, 0, 0)), chunk(XBC_W)],
        out_shape=[jax.ShapeDtypeStruct((s, SSM_W), F32), jax.ShapeDtypeStruct((nc, SSM_N, SSM_W), F32),
                   jax.ShapeDtypeStruct((s, XBC_W), F32)],
        scratch_shapes=[pltpu.VMEM((SSM_N, SSM_W), F32), pltpu.VMEM((BLOCK + 8, XBC_W), F32)],
        compiler_params=_params(dimension_semantics=("arbitrary",)),
    )(xbc, xbc, dt_raw, conv_w, conv_b, dt_bias, a_log, dsk_x, e3t)


def _dsilu(z, sg):
    return sg * (1.0 + z * (1.0 - sg))


def _mid(x, tgt, o_att, za, ypre, zm, ga, gb, gate, ssm_nw, wap, wsp, wout, tm=256):
    s = x.shape[0]
    gw = SSM_W // SSM_G

    def body(x_ref, t_ref, o_ref, za_ref, yp_ref, zm_ref, ga_ref, gb_ref, gate_ref, nw_ref, wap_h, wsp_h, wout_h,
             dout_ref, do_ref, dza_ref, dyp_ref, dzm_ref, dga_ref, dgb_ref,
             yag_ref, dya_ref, yn_ref, dyb_ref, mg_ref, dob_ref, gnw_ref, dgate_ref, loss_ref,
             wap_v, wsp_v, wout_v, sem):
        i = pl.program_id(0)

        @pl.when(i == 0)
        def _():
            cps = [pltpu.make_async_copy(a, b, sem.at[j])
                   for j, (a, b) in enumerate(((wap_h, wap_v), (wsp_h, wsp_v), (wout_h, wout_v)))]
            for cp in cps:
                cp.start()
            gnw_ref[...] = jnp.zeros_like(gnw_ref)
            dgate_ref[...] = jnp.zeros_like(dgate_ref)
            loss_ref[...] = jnp.zeros_like(loss_ref)
            for cp in cps:
                cp.wait()

        gate = gate_ref[...]
        nw = nw_ref[...]
        o_att = o_ref[...]
        z_a = za_ref[...].astype(F32)
        s_a = _sig(z_a)
        silu_a = z_a * s_a
        yag = _bf(o_att * silu_a)
        yag_ref[...] = yag
        y_a = _dot(yag, wap_v[...])
        ypre = yp_ref[...]
        z_m = zm_ref[...].astype(F32)
        s_m = _sig(z_m)
        silu_m = z_m * s_m
        yg = ypre * silu_m
        rinv = jnp.concatenate(
            [jnp.broadcast_to(lax.rsqrt(jnp.mean(yg[:, g * gw:(g + 1) * gw] ** 2, axis=-1, keepdims=True) + EPS), (tm, gw))
             for g in range(SSM_G)], axis=1)
        ynr = yg * rinv
        yn = _bf(ynr * nw)
        yn_ref[...] = yn
        y_b = _dot(yn, wsp_v[...])
        g_a = _sig(ga_ref[...].astype(F32))
        g_b = _sig(gb_ref[...].astype(F32))
        merged = _bf(g_a * y_a + g_b * y_b)
        mg_ref[...] = merged
        o = _dot(merged, wout_v[...])
        diff = x_ref[...] + gate * o - t_ref[...]
        loss_ref[...] += (0.5 / D_MODEL) * jnp.sum(diff * diff, axis=(0, 1), keepdims=True)
        dout = diff * (1.0 / D_MODEL)
        dout_ref[...] = dout
        dgate_ref[...] += jnp.sum(dout * o, axis=0, keepdims=True)
        d_o = _bf(dout * gate)
        dob_ref[...] = d_o
        dmerged = _dot_nt(d_o, wout_v[...])
        dy_a = dmerged * g_a
        dy_b = dmerged * g_b
        dga_ref[...] = _bf(dy_a * y_a * (1.0 - g_a))
        dgb_ref[...] = _bf(dy_b * y_b * (1.0 - g_b))
        dy_a = _bf(dy_a)
        dy_b = _bf(dy_b)
        dya_ref[...] = dy_a
        dyb_ref[...] = dy_b
        dyag = _dot_nt(dy_a, wap_v[...])
        do_ref[...] = dyag * silu_a
        dza_ref[...] = _bf(dyag * o_att * _dsilu(z_a, s_a))
        dyn = _dot_nt(dy_b, wsp_v[...])
        gnw_ref[...] += jnp.sum(dyn * ynr, axis=0, keepdims=True)
        dynw = dyn * nw
        corr = jnp.concatenate(
            [jnp.broadcast_to(jnp.mean((dynw * ynr)[:, g * gw:(g + 1) * gw], axis=-1, keepdims=True), (tm, gw))
             for g in range(SSM_G)], axis=1)
        dyg = rinv * (dynw - ynr * corr)
        dyp_ref[...] = dyg * silu_m
        dzm_ref[...] = _bf(dyg * ypre * _dsilu(z_m, s_m))

    r1, r2 = _rows(tm, D_MODEL), _rows(tm, SSM_W)
    sd = jax.ShapeDtypeStruct
    return pl.pallas_call(
        body, name="mid", grid=(s // tm,),
        in_specs=[r1, r1, r1, r1, r2, r2, r1, r1, _full((1, D_MODEL)), _full((1, SSM_W)), ANY, ANY, ANY],
        out_specs=[r1, r1, r1, r2, r2, r1, r1, r1, r1, r2, r1, r1, r1,
                   _full((1, SSM_W)), _full((1, D_MODEL)), _full((1, 1))],
        out_shape=[sd((s, D_MODEL), F32), sd((s, ATTN_W), F32), sd((s, ATTN_W), BF), sd((s, SSM_W), F32),
                   sd((s, SSM_W), BF), sd((s, D_MODEL), BF), sd((s, D_MODEL), BF),
                   sd((s, ATTN_W), BF), sd((s, D_MODEL), BF), sd((s, SSM_W), BF), sd((s, D_MODEL), BF),
                   sd((s, D_MODEL), BF), sd((s, D_MODEL), BF),
                   sd((1, SSM_W), F32), sd((1, D_MODEL), F32), sd((1, 1), F32)],
        scratch_shapes=[pltpu.VMEM((ATTN_W, D_MODEL), BF), pltpu.VMEM((SSM_W, D_MODEL), BF), pltpu.VMEM((D_MODEL, D_MODEL), BF),
                        pltpu.SemaphoreType.DMA((3,))],
        compiler_params=_params(dimension_semantics=("arbitrary",)),
    )(x, tgt, o_att, za, ypre, zm, ga, gb, gate, ssm_nw, wap, wsp, wout)


def _attn_bwd(q, k, v, bias, sinks, consts, o_att, lse, d_o):
    s = q.shape[0]
    nb = s // BLOCK
    folds = (_fold(ATTN_W, HEAD_DIM), _fold(KV_W, HEAD_DIM))

    def body(q_ref, kp_ref, kc_ref, vp_ref, vc_ref, b_ref, skv_ref, qw_ref, kw_ref, eq_ref, eq3_ref, ek_ref, ek3_ref,
             fq_ref, fk_ref, o_ref, lse_ref, do_ref,
             dq_ref, dk_ref, dv_ref, dss_ref, gqw_ref, gkw_ref, gsk_ref, ckn, cv, dqn_s, dkn_s, dv_s, gq_x, gk_x):
        i = pl.program_id(0)
        kw, ek, ek3 = kw_ref[...], ek_ref[...], ek3_ref[...]

        @pl.when(i == 0)
        def _():
            for ref in (ckn, cv, dss_ref, gq_x, gk_x, gsk_ref):
                ref[...] = jnp.zeros_like(ref)

        @pl.when(i < nb)
        def _():
            mask = _window_mask(i == 0)
            qw, eq, eq3 = qw_ref[...], eq_ref[...], eq3_ref[...]
            qf = q_ref[...]
            qnf, rq_x = _heads_norm(qf, qw, eq, eq3)
            qn = _bf(qnf)
            kf = jnp.concatenate([kp_ref[...], kc_ref[...]], axis=0)
            knf, rk_x = _heads_norm(kf, kw, ek, ek3)
            kn = _bf(knf)
            vv = _bf(jnp.concatenate([vp_ref[...], vc_ref[...]], axis=0))
            d_of = do_ref[...]
            d_ob = _bf(d_of)
            lse_all = lse_ref[...]
            delta = _group_sum(d_of * o_ref[...], eq)
            gsk_ref[...] += jnp.sum(-jnp.exp(skv_ref[...] - lse_all) * delta, axis=0, keepdims=True)
            for hk in range(KV_HEADS):
                ks = slice(hk * HEAD_DIM, (hk + 1) * HEAD_DIM)
                qg = _stack_heads(qn, hk)
                sc = _dot_nt(qg, kn[:, ks]) * (HEAD_DIM ** -0.5)
                sc = sc + b_ref[hk * GRP:(hk + 1) * GRP].reshape(GRP * BLOCK, 2 * BLOCK)
                p = jnp.where(mask, jnp.exp(sc - _stack_cols(lse_all, hk)), 0.0)
                d_og = _stack_heads(d_ob, hk)
                ds = p * (_dot_nt(d_og, vv[:, ks]) - _stack_cols(delta, hk))
                dss_ref[hk * GRP:(hk + 1) * GRP] += ds.reshape(GRP, BLOCK, 2 * BLOCK)
                dsb = _bf(ds)
                dv_s[:, ks] = _dot_tn(_bf(p), d_og)
                dkn_s[:, ks] = _dot_tn(dsb, qg) * (HEAD_DIM ** -0.5)
                dqn = _dot(dsb, kn[:, ks]) * (HEAD_DIM ** -0.5)
                for g in range(GRP):
                    h = hk * GRP + g
                    dqn_s[:, h * HEAD_DIM:(h + 1) * HEAD_DIM] = dqn[g * BLOCK:(g + 1) * BLOCK]
            dq, gq = _heads_norm_bwd(qf, rq_x, qw, dqn_s[...], eq, eq3)
            dq_ref[...] = _bf(dq)
            gq_x[...] += gq
            dk, gk = _heads_norm_bwd(kf[:BLOCK], rk_x[:BLOCK], kw, ckn[...] + dkn_s[0:BLOCK, :], ek, ek3)
            dk_ref[...] = _bf(dk)
            gk_x[...] += gk
            dv_ref[...] = _bf(cv[...] + dv_s[0:BLOCK, :])
            ckn[...] = dkn_s[BLOCK:2 * BLOCK, :]
            cv[...] = dv_s[BLOCK:2 * BLOCK, :]

        @pl.when(i == nb)
        def _():
            kc = kc_ref[...]
            dk, gk = _heads_norm_bwd(kc, _heads_norm(kc, kw, ek, ek3)[1], kw, ckn[...], ek, ek3)
            dk_ref[...] = _bf(dk)
            dv_ref[...] = _bf(cv[...])
            gqw_ref[...] = _group_sum(jnp.broadcast_to(gq_x[...], (8, ATTN_W)), fq_ref[...])[0:1]
            gkw_ref[...] = _group_sum(jnp.broadcast_to(gk_x[...] + gk, (8, KV_W)), fk_ref[...])[0:1]

    last = nb - 1
    cur = lambda w: pl.BlockSpec((BLOCK, w), lambda i: (jnp.minimum(i, last), 0))
    prev = lambda w: pl.BlockSpec((BLOCK, w), lambda i: (jnp.maximum(jnp.minimum(i, last) - 1, 0), 0))
    late = lambda w: pl.BlockSpec((BLOCK, w), lambda i: (jnp.maximum(i - 1, 0), 0))
    sd = jax.ShapeDtypeStruct
    return pl.pallas_call(
        body, name="attn_bwd", grid=(nb + 1,),
        in_specs=[cur(ATTN_W), prev(KV_W), cur(KV_W), prev(KV_W), cur(KV_W),
                  pl.BlockSpec((ATTN_HEADS, BLOCK, 2 * BLOCK), lambda i: (0, 0, 0)), _full((1, ATTN_HEADS))]
                 + [_full(c.shape) for c in consts + folds] + [cur(ATTN_W), cur(ATTN_HEADS), cur(ATTN_W)],
        out_specs=[cur(ATTN_W), late(KV_W), late(KV_W),
                   pl.BlockSpec((ATTN_HEADS, BLOCK, 2 * BLOCK), lambda i: (0, 0, 0)),
                   _full((1, HEAD_DIM)), _full((1, HEAD_DIM)), _full((1, ATTN_HEADS))],
        out_shape=[sd((s, ATTN_W), BF), sd((s, KV_W), BF), sd((s, KV_W), BF),
                   sd((ATTN_HEADS, BLOCK, 2 * BLOCK), F32), sd((1, HEAD_DIM), F32), sd((1, HEAD_DIM), F32),
                   sd((1, ATTN_HEADS), F32)],
        scratch_shapes=[pltpu.VMEM((BLOCK, KV_W), F32), pltpu.VMEM((BLOCK, KV_W), F32),
                        pltpu.VMEM((BLOCK, ATTN_W), F32), pltpu.VMEM((2 * BLOCK, KV_W), F32),
                        pltpu.VMEM((2 * BLOCK, KV_W), F32), pltpu.VMEM((1, ATTN_W), F32), pltpu.VMEM((1, KV_W), F32)],
        compiler_params=_params(dimension_semantics=("arbitrary",)),
    )(q, k, k, v, v, bias, sinks, *consts, *folds, o_att, lse, d_o)


def _ssd_bwd(xbc, conv_all, dt_raw, conv_w, dt_bias, a_log, dsk_x, e_mat, e3t, hprev_all, dy_all):
    s = xbc.shape[0]
    nc = s // BLOCK
    gw = SSM_R * SSM_P
    b0, c0 = SSM_W, SSM_W + SSM_G * SSM_N

    def body(x_ref, conv_ref, dtr_ref, cw_ref, dtb_ref, alog_ref, dsk_ref, e_ref, e3_ref, hp_ref, dy_ref,
             dx_ref, ddt_ref, gcw_ref, gcb_ref, gdtb_ref, galog_ref, gdsk_ref,
             dh, nhead, ext2, gdskx, dxdt_s, dbc_s):
        i = pl.program_id(0)
        c = nc - 1 - i

        @pl.when(i == 0)
        def _():
            for ref in (dh, nhead, gdskx, gcw_ref, gcb_ref, gdtb_ref, galog_ref, gdsk_ref):
                ref[...] = jnp.zeros_like(ref)

        conv = conv_ref[...]
        sg, xact, u, dt, a, trilb, acum, dt_x, acum_x = _ssd_common(conv, dtr_ref, dtb_ref, alog_ref, e3_ref)
        xs = xact[:, :SSM_W]
        acum_t = acum.T
        ea_x = jnp.exp(acum_x)
        last_x = acum_x[BLOCK - 1:BLOCK, :]
        dte_x = jnp.exp(last_x - acum_x)
        cd_x = jnp.exp(last_x)
        xdt = xs * dt_x
        xw = xdt * dte_x
        hprev = hp_ref[0]
        dhn = dh[...]
        dy = dy_ref[...]
        gdskx[...] += jnp.sum(dy * xs, axis=0, keepdims=True)
        dyea = dy * ea_x
        lane = lax.broadcasted_iota(jnp.int32, (BLOCK, SSM_HEADS), 1)
        dacum = jnp.zeros((BLOCK, SSM_HEADS), F32)
        dacc_x, dlast_x = [], []
        for g in range(SSM_G):
            bgf = xact[:, b0 + g * SSM_N:b0 + (g + 1) * SSM_N]
            cgf = xact[:, c0 + g * SSM_N:c0 + (g + 1) * SSM_N]
            bg, cg = _bf(bgf), _bf(cgf)
            sl = slice(g * gw, (g + 1) * gw)
            hpg, dhg, dyeag = _bf(hprev[:, sl]), _bf(dhn[:, sl]), _bf(dyea[:, sl])
            cb = _dot_nt(cg, bg)
            gmat = _dot(cg, hpg)
            dxw = _dot(bg, dhg)
            dxdt_s[:, sl] = dxw * dte_x[:, sl]
            dacc_x.append(dy[:, sl] * gmat * ea_x[:, sl] - dxw * xw[:, sl])
            dlast_x.append(jnp.sum(dxw * xw[:, sl], axis=0, keepdims=True)
                           + jnp.sum(dhn[:, sl] * hprev[:, sl], axis=0, keepdims=True) * cd_x[:, sl])
            dcg = _dot_nt(dyeag, hpg)
            dbg = _dot_nt(_bf(xw[:, sl]), dhg)
            dh[:, sl] = dhn[:, sl] * cd_x[:, sl] + _dot_tn(cg, dyeag)
            dcb = jnp.zeros((BLOCK, BLOCK), F32)
            for r in range(SSM_R):
                hh = g * SSM_R + r
                hs = slice(hh * SSM_P, (hh + 1) * SSM_P)
                seg = jnp.where(trilb, acum[:, hh:hh + 1] - acum_t[hh:hh + 1, :], -1e30)
                lm = jnp.exp(seg)
                mm = cb * lm
                dyh = _bf(dy[:, hs])
                dm = _dot_nt(dyh, _bf(xdt[:, hs]))
                dxdt_s[:, hs] += _dot_tn(_bf(mm), dyh)
                wm = dm * mm
                dcb = dcb + dm * lm
                dacum = dacum + _group_sum(wm - wm.T, (lane == hh).astype(BF))
            dcbb = _bf(dcb)
            dbc_s[:, g * SSM_N:(g + 1) * SSM_N] = dbg + _dot_tn(dcbb, cg)
            dbc_s[:, SSM_G * SSM_N + g * SSM_N:SSM_G * SSM_N + (g + 1) * SSM_N] = dcg + _dot(dcbb, bg)
        dxdt = dxdt_s[...]
        dxs = dy * dsk_ref[...] + dxdt * dt_x
        red = _group_sum(jnp.concatenate(
            [dxdt * xs, jnp.concatenate(dacc_x, axis=1),
             jnp.broadcast_to(jnp.concatenate(dlast_x, axis=1), (8, SSM_W))], axis=0), e_ref[...])
        row = lax.broadcasted_iota(jnp.int32, (BLOCK, SSM_HEADS), 0)
        dacum = dacum + red[BLOCK:2 * BLOCK] + jnp.where(row == BLOCK - 1, red[2 * BLOCK:2 * BLOCK + 1], 0.0)
        ddta = _exact_left(_triu().astype(BF), dacum)
        ddt = red[:BLOCK] + ddta * a
        galog_ref[...] += jnp.sum(ddta * dt, axis=0, keepdims=True) * a
        du = ddt * _sig(u)
        ddt_ref[...] = _bf(du)
        gdtb_ref[...] += jnp.sum(du, axis=0, keepdims=True)
        dconv = jnp.concatenate([dxs, dbc_s[...]], axis=1) * _dsilu(conv, sg)
        gcb_ref[...] += jnp.sum(dconv, axis=0, keepdims=True)
        ext2[0:BLOCK, :] = dconv
        ext2[BLOCK:BLOCK + 8, :] = nhead[...]
        ahead = [ext2[3 - j:3 - j + BLOCK, :] for j in range(CONV_K)]
        dx_ref[...] = _bf(sum(ahead[j] * cw_ref[j:j + 1, :] for j in range(CONV_K)))
        xraw = x_ref[...]
        gcw_ref[...] += jnp.concatenate([jnp.sum(ahead[j] * xraw, axis=0, keepdims=True) for j in range(CONV_K)], axis=0)
        nhead[...] = dconv[0:8]

        @pl.when(i == nc - 1)
        def _():
            gdsk_ref[...] = _group_sum(jnp.broadcast_to(gdskx[...], (8, SSM_W)), e_ref[...])[0:1]

    chunk = lambda w: pl.BlockSpec((BLOCK, w), lambda i: (nc - 1 - i, 0))
    sd = jax.ShapeDtypeStruct
    return pl.pallas_call(
        body, name="ssd_bwd", grid=(nc,),
        in_specs=[chunk(XBC_W), chunk(XBC_W),
                  chunk(SSM_HEADS), _full((CONV_K, XBC_W)), _full((1, SSM_HEADS)),
                  _full((1, SSM_HEADS)), _full((1, SSM_W)), _full((SSM_W, SSM_HEADS)), _full((3 * SSM_HEADS, SSM_W)),
                  pl.BlockSpec((1, SSM_N, SSM_W), lambda i: (nc - 1 - i, 0, 0)), chunk(SSM_W)],
        out_specs=[chunk(XBC_W), chunk(SSM_HEADS), _full((CONV_K, XBC_W)), _full((1, XBC_W)),
                   _full((1, SSM_HEADS)), _full((1, SSM_HEADS)), _full((1, SSM_HEADS))],
        out_shape=[sd((s, XBC_W), BF), sd((s, SSM_HEADS), BF), sd((CONV_K, XBC_W), F32), sd((1, XBC_W), F32),
                   sd((1, SSM_HEADS), F32), sd((1, SSM_HEADS), F32), sd((1, SSM_HEADS), F32)],
        scratch_shapes=[pltpu.VMEM((SSM_N, SSM_W), F32), pltpu.VMEM((8, XBC_W), F32),
                        pltpu.VMEM((BLOCK + 8, XBC_W), F32),
                        pltpu.VMEM((1, SSM_W), F32), pltpu.VMEM((BLOCK, SSM_W), F32),
                        pltpu.VMEM((BLOCK, 2 * SSM_G * SSM_N), F32)],
        compiler_params=_params(dimension_semantics=("arbitrary",)),
    )(xbc, conv_all, dt_raw, conv_w, dt_bias, a_log, dsk_x, e_mat, e3t, hprev_all, dy_all)


def _dh(x, dout, norm_w, scale, dsegs, w_t, tm=256):
    s = x.shape[0]

    def body(x_ref, dout_ref, nw_ref, sc_ref, *rest):
        d_refs, w_hbm = rest[:9], rest[9]
        gx_ref, dshift_ref, dscale_ref, gnw_ref = rest[10:14]
        w_vm, sem = rest[14], rest[15]

        @pl.when(pl.program_id(0) == 0)
        def _():
            cp = pltpu.make_async_copy(w_hbm, w_vm, sem)
            cp.start()
            for ref in (dshift_ref, dscale_ref, gnw_ref):
                ref[...] = jnp.zeros_like(ref)
            cp.wait()

        dh = _dot(d_refs[0][...], w_vm[SEG_OFF[0]:SEG_OFF[1], :])
        for j in range(1, 9):
            dh = dh + _dot(d_refs[j][...], w_vm[SEG_OFF[j]:SEG_OFF[j + 1], :])
        xv = x_ref[...]
        r = lax.rsqrt(jnp.mean(xv * xv, axis=-1, keepdims=True) + EPS)
        xn = xv * r
        nw = nw_ref[...]
        sc1 = 1.0 + sc_ref[...]
        dshift_ref[...] += jnp.sum(dh, axis=0, keepdims=True)
        dhxn = jnp.sum(dh * xn, axis=0, keepdims=True)
        dscale_ref[...] += dhxn * nw
        gnw_ref[...] += dhxn * sc1
        dxn = dh * (nw * sc1)
        gx_ref[...] = dout_ref[...] + r * (dxn - xn * jnp.mean(xn * dxn, axis=-1, keepdims=True))

    vec = _full((1, D_MODEL))
    sd = jax.ShapeDtypeStruct
    return pl.pallas_call(
        body, name="dh", grid=(s // tm,),
        in_specs=[_rows(tm, D_MODEL), _rows(tm, D_MODEL), vec, vec] + [_rows(tm, w) for w in SEG_W] + [ANY],
        out_specs=[_rows(tm, D_MODEL), vec, vec, vec],
        out_shape=[sd((s, D_MODEL), F32), sd((1, D_MODEL), F32), sd((1, D_MODEL), F32), sd((1, D_MODEL), F32)],
        scratch_shapes=[pltpu.VMEM((IN_W, D_MODEL), BF), pltpu.SemaphoreType.DMA],
        compiler_params=_params(dimension_semantics=("arbitrary",)),
    )(x, dout, norm_w, scale, *dsegs, w_t)


def _gw_seg(h, dseg, name, tm=1024):
    s, w = dseg.shape
    tn = min(w, 1024)
    tm = min(tm, s)

    def body(h_ref, d_ref, o_ref):
        @pl.when(pl.program_id(1) == 0)
        def _():
            o_ref[...] = jnp.zeros_like(o_ref)

        o_ref[...] += _dot_tn(d_ref[...], h_ref[...])

    return pl.pallas_call(
        body, name=name, grid=(w // tn, s // tm),
        in_specs=[pl.BlockSpec((tm, D_MODEL), lambda n, m: (m, 0)), pl.BlockSpec((tm, tn), lambda n, m: (m, n))],
        out_specs=pl.BlockSpec((tn, D_MODEL), lambda n, m: (n, 0)),
        out_shape=jax.ShapeDtypeStruct((w, D_MODEL), F32),
        compiler_params=_params(dimension_semantics=("arbitrary", "arbitrary")),
    )(h, dseg)


def _gw_in(h, dsegs):
    return [_gw_seg(h, d, "gw_in_%d" % j) for j, d in enumerate(dsegs)]


def _local_step(x, tgt, shift, scale, gate, w_t, rows_fn, norm_w, qnw, knw, rel_bias, sinks,
                conv_w, conv_b, dt_bias, a_log, d_skip, ssm_nw, after_mid=None, after_gw=None):
    oh_t = _bucket_onehot_t()
    bias = _bias_dense(rel_bias.T, oh_t).reshape(ATTN_HEADS, BLOCK, 2 * BLOCK)
    *segs, h = _inproj(x, norm_w, scale, shift, w_t)
    q, k, v, za, zm, xbc, dtr, ga, gb = segs
    consts = _attn_consts(qnw, knw)
    o_att, lse = _attn_fwd(q, k, v, bias, sinks, consts)
    e_mat, e3t = _membership(SSM_W, SSM_P, SSM_HEADS)
    dsk_x = jnp.repeat(d_skip, SSM_P, axis=1)
    ypre, hprev, conv = _ssd_fwd(xbc, dtr, conv_w, conv_b, dt_bias, a_log, dsk_x, e3t)
    wap, wsp, wout = rows_fn(ypre)
    (dout, d_o, dza, dyp, dzm, dga, dgb, yag, dy_a, yn, dy_b, merged, dob, g_ssm_nw, dgate, loss) = _mid(
        x, tgt, o_att, za, ypre, zm, ga, gb, gate, ssm_nw, wap, wsp, wout)
    g_wap = _gw_seg(dy_a, yag, "gw_attn_proj")
    g_wsp = _gw_seg(dy_b, yn, "gw_ssm_proj")
    g_wout = _gw_seg(dob, merged, "gw_out")
    zero = after_mid(g_wap, g_wsp, g_wout) if after_mid is not None else 0.0
    dq, dk, dv, dss, g_qnw, g_knw, g_sinks = _attn_bwd(q, k, v, bias, sinks + zero, consts, o_att, lse, d_o)
    g_rel = _bias_grad(dss.reshape(ATTN_HEADS, BLOCK * 2 * BLOCK), oh_t).T
    dxbc, ddt, g_cw, g_cb, g_dtb, g_alog, g_dsk = _ssd_bwd(
        xbc, conv, dtr, conv_w, dt_bias, a_log, dsk_x, e_mat, e3t, hprev, dyp)
    dsegs = (dq, dk, dv, dza, dzm, dxbc, ddt, dga, dgb)
    g_ws = _gw_in(h, dsegs)
    zero = after_gw(g_ws) if after_gw is not None else 0.0
    gx, dshift, dscale, g_nw = _dh(x, dout, norm_w + zero, scale, dsegs, w_t)
    return dict(loss=loss, grad_x=gx, dmod=jnp.concatenate([dshift, dscale, dgate], axis=1), g_ws=g_ws,
                g_wap=g_wap, g_wsp=g_wsp, g_wout=g_wout, g_norm_w=g_nw, g_qnw=g_qnw, g_knw=g_knw, g_rel=g_rel,
                g_sinks=g_sinks, g_conv_w=g_cw, g_conv_b=g_cb, g_dt_bias=g_dtb, g_a_log=g_alog, g_d_skip=g_dsk,
                g_ssm_nw=g_ssm_nw)


def _me():
    return lax.axis_index("x"), lax.axis_index("y"), lax.axis_index("c")


def _flip(v, bit):
    return 1 - v if bit else v


def _ag_direct(v, name):
    def body(v_ref, out_ref, send_sems, recv_sems, local_sem):
        x, y, c = _me()
        me = 4 * x + 2 * y + c
        mine = pltpu.make_async_copy(v_ref, out_ref.at[me], local_sem)
        mine.start()
        peers = [(_flip(x, k >> 2 & 1), _flip(y, k >> 1 & 1), _flip(c, k & 1)) for k in range(1, N_DEV)]
        sends = [pltpu.make_async_remote_copy(
            src_ref=v_ref, dst_ref=out_ref.at[me], send_sem=send_sems.at[j], recv_sem=recv_sems.at[j],
            device_id=p, device_id_type=MESH) for j, p in enumerate(peers)]
        for cp in sends:
            cp.start()
        for j, (px, py, pc) in enumerate(peers):
            pltpu.make_async_remote_copy(
                src_ref=v_ref, dst_ref=out_ref.at[4 * px + 2 * py + pc], send_sem=send_sems.at[j],
                recv_sem=recv_sems.at[j], device_id=(px, py, pc), device_id_type=MESH).wait_recv()
        for cp in sends:
            cp.wait_send()
        mine.wait()

    vm = pl.BlockSpec(memory_space=pltpu.VMEM)
    return pl.pallas_call(
        body, name=name, out_shape=jax.ShapeDtypeStruct((N_DEV,) + v.shape, v.dtype),
        in_specs=[vm], out_specs=vm,
        scratch_shapes=[pltpu.SemaphoreType.DMA((N_DEV - 1,)), pltpu.SemaphoreType.DMA((N_DEV - 1,)),
                        pltpu.SemaphoreType.DMA],
        compiler_params=_params(),
    )(v)


def _ag_two_level(v, name):
    def body(v_ref, out_ref, token, send_sems, recv_sems, local_sem):
        token[...] = jnp.zeros_like(token)
        x, y, c = _me()
        me, sibling = (x, y, c), (x, y, 1 - c)
        chips = [(1 - x, y), (x, 1 - y), (1 - x, 1 - y)]

        def slot(px, py, pc):
            return out_ref.at[4 * px + 2 * py + pc]

        def copy(k, block, to, src=None):
            return pltpu.make_async_remote_copy(
                src_ref=slot(*block) if src is None else src, dst_ref=slot(*block),
                send_sem=send_sems.at[k], recv_sem=recv_sems.at[k], device_id=to, device_id_type=MESH)

        mine = pltpu.make_async_copy(v_ref, slot(*me), local_sem)
        mine.start()
        first = [copy(0, me, sibling, src=v_ref)]
        first += [copy(1 + j, me, (*chip, c), src=v_ref) for j, chip in enumerate(chips)]
        for cp in first:
            cp.start()
        passed = [copy(4 + j, (*chip, c), sibling) for j, chip in enumerate(chips)]
        for j, chip in enumerate(chips):
            copy(1 + j, (*chip, c), me).wait_recv()
            passed[j].start()
        copy(0, sibling, me).wait_recv()
        for j, chip in enumerate(chips):
            copy(4 + j, (*chip, 1 - c), me).wait_recv()
        for cp in first + passed:
            cp.wait_send()
        mine.wait()

    out, token = pl.pallas_call(
        body, name=name,
        out_shape=(jax.ShapeDtypeStruct((N_DEV,) + v.shape, v.dtype), jax.ShapeDtypeStruct((8, 128), v.dtype)),
        in_specs=[ANY], out_specs=(ANY, pl.BlockSpec(memory_space=pltpu.VMEM)),
        scratch_shapes=[pltpu.SemaphoreType.DMA((7,)), pltpu.SemaphoreType.DMA((7,)), pltpu.SemaphoreType.DMA],
        compiler_params=_params(),
    )(v)
    return out, token[0:1, 0:1]


def _rs_sibling(g, name):
    def body(g_ref, out_ref, send_sems, recv_sems):
        x, y, c = _me()
        cps = [pltpu.make_async_remote_copy(
            src_ref=g_ref.at[2 * ch + 1 - c], dst_ref=out_ref.at[ch], send_sem=send_sems.at[ch],
            recv_sem=recv_sems.at[ch], device_id=(x, y, 1 - c), device_id_type=MESH) for ch in range(4)]
        for cp in cps:
            cp.start()
        for cp in cps:
            cp.wait()

    return pl.pallas_call(
        body, name=name, out_shape=jax.ShapeDtypeStruct((4,) + g.shape[1:], g.dtype),
        in_specs=[ANY], out_specs=ANY,
        scratch_shapes=[pltpu.SemaphoreType.DMA((4,)), pltpu.SemaphoreType.DMA((4,))],
        compiler_params=_params(),
    )(g)


def _add_sibling(g, got, name):
    _, r, n = g.shape
    tr = min(r, 256)

    def body(c_ref, a_ref, b_ref, o_ref):
        o_ref[...] = a_ref[...] + b_ref[...]

    grid_spec = pltpu.PrefetchScalarGridSpec(
        num_scalar_prefetch=1, grid=(4, r // tr),
        in_specs=[pl.BlockSpec((1, tr, n), lambda ch, i, c_ref: (2 * ch + c_ref[0], i, 0)),
                  pl.BlockSpec((1, tr, n), lambda ch, i, c_ref: (ch, i, 0))],
        out_specs=pl.BlockSpec((1, tr, n), lambda ch, i, c_ref: (ch, i, 0)))
    return pl.pallas_call(
        body, name=name, grid_spec=grid_spec, out_shape=jax.ShapeDtypeStruct((4, r, n), g.dtype),
        compiler_params=_params(dimension_semantics=("arbitrary", "arbitrary")),
    )(lax.axis_index("c").reshape(1).astype(jnp.int32), g, got)


def _rs_chips(p, name):
    def body(p_ref, out_ref, send_sems, recv_sems, local_sem):
        x, y, c = _me()
        my_chip = 2 * x + y
        mine = pltpu.make_async_copy(p_ref.at[my_chip], out_ref.at[my_chip], local_sem)
        mine.start()
        chips = [(1 - x, y), (x, 1 - y), (1 - x, 1 - y)]
        sends = [pltpu.make_async_remote_copy(
            src_ref=p_ref.at[2 * px + py], dst_ref=out_ref.at[my_chip], send_sem=send_sems.at[j],
            recv_sem=recv_sems.at[j], device_id=(px, py, c), device_id_type=MESH) for j, (px, py) in enumerate(chips)]
        for cp in sends:
            cp.start()
        for j, (px, py) in enumerate(chips):
            pltpu.make_async_remote_copy(
                src_ref=p_ref.at[my_chip], dst_ref=out_ref.at[2 * px + py], send_sem=send_sems.at[j],
                recv_sem=recv_sems.at[j], device_id=(px, py, c), device_id_type=MESH).wait_recv()
        for cp in sends:
            cp.wait_send()
        mine.wait()

    return pl.pallas_call(
        body, name=name, out_shape=jax.ShapeDtypeStruct(p.shape, p.dtype),
        in_specs=[ANY], out_specs=ANY,
        scratch_shapes=[pltpu.SemaphoreType.DMA((3,)), pltpu.SemaphoreType.DMA((3,)), pltpu.SemaphoreType.DMA],
        compiler_params=_params(),
    )(p)


HBM = pl.BlockSpec(memory_space=pltpu.HBM)
SEM = pl.BlockSpec(memory_space=pltpu.SEMAPHORE)
EFFECT = pltpu.SideEffectType.DATAFLOW_SIDE_EFFECTING


def _peers(x, y, c):
    return [(_flip(x, k >> 2 & 1), _flip(y, k >> 1 & 1), _flip(c, k & 1)) for k in range(1, N_DEV)]


def _exchange_start(src, land, gather, name):
    def body(src_ref, land_ref, send_sems, recv_sems, src_thru, land_thru, token):
        x, y, c = _me()
        me = 4 * x + 2 * y + c
        for j, (px, py, pc) in enumerate(_peers(x, y, c)):
            pltpu.make_async_remote_copy(
                src_ref=src_ref if gather else src_ref.at[4 * px + 2 * py + pc], dst_ref=land_ref.at[me],
                send_sem=send_sems.at[j], recv_sem=recv_sems.at[j], device_id=(px, py, pc), device_id_type=MESH).start()
        token[...] = jnp.zeros_like(token)

    sems = pltpu.SemaphoreType.DMA((N_DEV - 1,))
    out = pl.pallas_call(
        body, name=name,
        out_shape=(sems, sems, pltpu.HBM(src.shape, src.dtype), pltpu.HBM(land.shape, land.dtype),
                   jax.ShapeDtypeStruct((8, 128), F32)),
        in_specs=(HBM, HBM), out_specs=(SEM, SEM, HBM, HBM, pl.BlockSpec(memory_space=pltpu.VMEM)),
        input_output_aliases={0: 2, 1: 3},
        compiler_params=pltpu.CompilerParams(has_side_effects=EFFECT),
    )(pltpu.with_memory_space_constraint(src, pltpu.HBM), pltpu.with_memory_space_constraint(land, pltpu.HBM))
    return out[:4], out[4][0, 0]


def _exchange_wait(started, after, gather, name):
    send_sems, recv_sems, src_thru, land_thru = started

    def body(src_ref, land_ref, send_sems, recv_sems, after_ref, src_dead, got_ref):
        x, y, c = _me()
        for j, (px, py, pc) in enumerate(_peers(x, y, c)):
            pid = 4 * px + 2 * py + pc
            cp = pltpu.make_async_remote_copy(
                src_ref=src_ref if gather else src_ref.at[pid], dst_ref=land_ref.at[pid],
                send_sem=send_sems.at[j], recv_sem=recv_sems.at[j], device_id=(px, py, pc), device_id_type=MESH)
            cp.wait_send()
            cp.wait_recv()

    return pl.pallas_call(
        body, name=name,
        out_shape=(pltpu.HBM(src_thru.shape, src_thru.dtype), pltpu.HBM(land_thru.shape, land_thru.dtype)),
        in_specs=(HBM, HBM, SEM, SEM, ANY), out_specs=(HBM, HBM), input_output_aliases={0: 0, 1: 1},
        compiler_params=pltpu.CompilerParams(has_side_effects=EFFECT),
    )(src_thru, land_thru, send_sems, recv_sems, after)[1]


def _reduce_scatter(g, name):
    got = _rs_sibling(g, name + "_sib")
    return _rs_chips(_add_sibling(g, got, name + "_add"), name + "_chips")


def _silu(a):
    return a * _sig(a)


def _mod_piece(c_all, w_ada, b_piece):
    def body(c_ref, w_ref, b_ref, o_ref):
        o_ref[...] = _dot(_bf(_silu(c_ref[...])), _bf(w_ref[...])) + b_ref[...]

    return pl.pallas_call(
        body, name="mod_piece", out_shape=jax.ShapeDtypeStruct((c_all.shape[0], w_ada.shape[1]), F32),
        compiler_params=_params(),
    )(c_all, w_ada, b_piece)


def _gw_ada(c_all, dmod_piece):
    def body(c_ref, d_ref, o_ref):
        o_ref[...] = _dot_tn(_bf(_silu(c_ref[...])), _bf(d_ref[...]))

    return pl.pallas_call(
        body, name="gw_ada", out_shape=jax.ShapeDtypeStruct((c_all.shape[1], dmod_piece.shape[1]), F32),
        compiler_params=_params(),
    )(c_all, dmod_piece)


def _adam(parts, w, m, v, name):
    k, r, n = parts.shape
    if r <= 256 or r % 256 == 0:
        tr, tn = min(r, 256), n
    else:
        tr, tn = r, 256
    assert r % tr == 0 and n % tn == 0

    def body(p_ref, w_ref, m_ref, v_ref, g_ref, d_ref, nm_ref, nv_ref):
        g = p_ref[0].astype(F32)
        for j in range(1, k):
            g = g + p_ref[j].astype(F32)
        m_new = ADAM_B1 * m_ref[...] + (1.0 - ADAM_B1) * g
        v_new = ADAM_B2 * v_ref[...] + (1.0 - ADAM_B2) * jnp.square(g)
        m_hat = m_new / (1.0 - ADAM_B1 ** ADAM_STEP)
        v_hat = v_new / (1.0 - ADAM_B2 ** ADAM_STEP)
        g_ref[...] = g
        d_ref[...] = -ADAM_LR * (m_hat / (jnp.sqrt(v_hat) + ADAM_EPS) + ADAM_WD * w_ref[...])
        nm_ref[...] = m_new
        nv_ref[...] = v_new

    blk = pl.BlockSpec((tr, tn), lambda i, j: (i, j))
    return pl.pallas_call(
        body, name=name, grid=(r // tr, n // tn),
        in_specs=[pl.BlockSpec((k, tr, tn), lambda i, j: (0, i, j)), blk, blk, blk],
        out_specs=[blk, blk, blk, blk],
        out_shape=[jax.ShapeDtypeStruct((r, n), F32)] * 4,
        compiler_params=_params(dimension_semantics=("arbitrary", "arbitrary")),
    )(parts, w, m, v)


_SMALL = (("b_ada", 3 * D_MODEL), ("norm_w", D_MODEL), ("q_norm_w", HEAD_DIM), ("k_norm_w", HEAD_DIM),
          ("rel_bias", REL_BUCKETS * ATTN_HEADS), ("sinks", ATTN_HEADS), ("conv_b", XBC_W), ("dt_bias", SSM_HEADS),
          ("a_log", SSM_HEADS), ("d_skip", SSM_HEADS), ("ssm_norm_w", SSM_W))
_SMALL_N = sum(n for _, n in _SMALL)
_SMALL_PAD = -(-_SMALL_N // 128) * 128
_PACK_N = _SMALL_PAD + CONV_K * XBC_W


def _pack_small(d):
    parts = [d[name].reshape(1, n) for name, n in _SMALL]
    return jnp.concatenate(parts + [jnp.zeros((1, _SMALL_PAD - _SMALL_N), F32)], axis=1)


def _unpack_small(vec, shapes):
    out, off = {}, 0
    for name, n in _SMALL:
        out[name] = vec[:, off:off + n].reshape(shapes[name])
        off += n
    return out


WEIGHTS = ("w_ada", "b_ada", "norm_w", "w_in", "q_norm_w", "k_norm_w", "rel_bias", "sinks", "conv_w", "conv_b",
           "dt_bias", "a_log", "d_skip", "ssm_norm_w", "w_attn_proj", "w_ssm_proj", "w_out")


def kernel(x, c, w_ada, b_ada, norm_w, w_in, q_norm_w, k_norm_w, rel_bias, sinks, conv_w, conv_b, dt_bias, a_log, d_skip, ssm_norm_w, w_attn_proj, w_ssm_proj, w_out, loss_target, m_w_ada, m_b_ada, m_norm_w, m_w_in, m_q_norm_w, m_k_norm_w, m_rel_bias, m_sinks, m_conv_w, m_conv_b, m_dt_bias, m_a_log, m_d_skip, m_ssm_norm_w, m_w_attn_proj, m_w_ssm_proj, m_w_out, v_w_ada, v_b_ada, v_norm_w, v_w_in, v_q_norm_w, v_k_norm_w, v_rel_bias, v_sinks, v_conv_w, v_conv_b, v_dt_bias, v_a_log, v_d_skip, v_ssm_norm_w, v_w_attn_proj, v_w_ssm_proj, v_w_out):
    w = dict(w_ada=w_ada, b_ada=b_ada, norm_w=norm_w, w_in=w_in, q_norm_w=q_norm_w, k_norm_w=k_norm_w,
             rel_bias=rel_bias, sinks=sinks, conv_w=conv_w, conv_b=conv_b, dt_bias=dt_bias, a_log=a_log,
             d_skip=d_skip, ssm_norm_w=ssm_norm_w, w_attn_proj=w_attn_proj, w_ssm_proj=w_ssm_proj, w_out=w_out)
    m = dict(w_ada=m_w_ada, b_ada=m_b_ada, norm_w=m_norm_w, w_in=m_w_in, q_norm_w=m_q_norm_w, k_norm_w=m_k_norm_w,
             rel_bias=m_rel_bias, sinks=m_sinks, conv_w=m_conv_w, conv_b=m_conv_b, dt_bias=m_dt_bias, a_log=m_a_log,
             d_skip=m_d_skip, ssm_norm_w=m_ssm_norm_w, w_attn_proj=m_w_attn_proj, w_ssm_proj=m_w_ssm_proj, w_out=m_w_out)
    v = dict(w_ada=v_w_ada, b_ada=v_b_ada, norm_w=v_norm_w, w_in=v_w_in, q_norm_w=v_q_norm_w, k_norm_w=v_k_norm_w,
             rel_bias=v_rel_bias, sinks=v_sinks, conv_w=v_conv_w, conv_b=v_conv_b, dt_bias=v_dt_bias, a_log=v_a_log,
             d_skip=v_d_skip, ssm_norm_w=v_ssm_norm_w, w_attn_proj=v_w_attn_proj, w_ssm_proj=v_w_ssm_proj, w_out=v_w_out)
    me = 4 * lax.axis_index("x") + 2 * lax.axis_index("y") + lax.axis_index("c")
    ada_n = w_ada.shape[2]
    in_n = w_in.shape[2]
    cw_n = conv_w.shape[2]

    first = _ag_direct(jnp.concatenate([c, conv_w[0].reshape(1, CONV_K * cw_n)], axis=1), "ag_c")[:, 0]
    c_all = first[:, :D_MODEL]
    conv_w_full = first[:, D_MODEL:].reshape(N_DEV, CONV_K, cw_n).transpose(1, 0, 2).reshape(CONV_K, XBC_W)
    b_piece = lax.dynamic_slice_in_dim(b_ada, me * ada_n, ada_n, axis=1)
    mod_all = _ag_direct(_mod_piece(c_all, w_ada[0], b_piece), "ag_mod")
    mod = lax.dynamic_index_in_dim(mod_all, me, axis=1, keepdims=False).reshape(1, 3 * D_MODEL)
    shift, scale, gate = mod[:, :D_MODEL], mod[:, D_MODEL:2 * D_MODEL], mod[:, 2 * D_MODEL:]

    w_t, zero = _ag_two_level(w_in[0].T.astype(BF), "ag_w_in")
    w_t = w_t.reshape(N_DEV * in_n, D_MODEL)

    def with_mine(blocks, mine):
        return lax.dynamic_update_index_in_dim(jnp.zeros(blocks, mine.dtype), mine, me, axis=0)

    rows = jnp.concatenate([w_attn_proj[0], w_ssm_proj[0], w_out[0]], axis=0).astype(BF) + zero
    r_ap, r_sp = w_attn_proj.shape[1], w_ssm_proj.shape[1]
    rows_started, zero = _exchange_start(rows, with_mine((N_DEV,) + rows.shape, rows), True, "ag_rows_start")

    def rows_fn(after):
        rows_all = _exchange_wait(rows_started, after, True, "ag_rows_wait")
        return (rows_all[:, :r_ap].reshape(ATTN_W, D_MODEL), rows_all[:, r_ap:r_ap + r_sp].reshape(SSM_W, D_MODEL),
                rows_all[:, r_ap + r_sp:].reshape(D_MODEL, D_MODEL))

    started = {}

    def send_blocks(key, g, name):
        g = g.astype(BF)
        started[key], zero = _exchange_start(
            g, with_mine(g.shape, lax.dynamic_index_in_dim(g, me, axis=0, keepdims=False)), False, name)
        return zero

    def after_mid(g_wap, g_wsp, g_wout):
        return send_blocks("rows", jnp.concatenate(
            [g_wap.reshape(N_DEV, r_ap, D_MODEL), g_wsp.reshape(N_DEV, r_sp, D_MODEL),
             g_wout.reshape(N_DEV, r_ap, D_MODEL)], axis=1), "rs_rows_start")

    def after_gw(g_ws):
        return send_blocks("in", jnp.concatenate(g_ws, axis=0).reshape(N_DEV, in_n, D_MODEL), "rs_in_start")

    r = _local_step(x[0], loss_target[0], shift, scale + zero, gate, w_t, rows_fn, norm_w, q_norm_w, k_norm_w,
                    rel_bias, sinks, conv_w_full, conv_b, dt_bias, a_log, d_skip, ssm_norm_w, after_mid, after_gw)

    loss = lax.psum(r["loss"][0, 0], ("x", "y", "c"))

    small = dict(b_ada=r["dmod"], norm_w=r["g_norm_w"], q_norm_w=r["g_qnw"], k_norm_w=r["g_knw"], rel_bias=r["g_rel"],
                 sinks=r["g_sinks"], conv_b=r["g_conv_b"], dt_bias=r["g_dt_bias"], a_log=r["g_a_log"],
                 d_skip=r["g_d_skip"], ssm_norm_w=r["g_ssm_nw"])
    pack = jnp.concatenate([_pack_small(small), r["g_conv_w"].reshape(1, CONV_K * XBC_W)], axis=1)
    pack_all = _ag_direct(pack, "ag_small")
    shapes = {name: w[name].shape for name, _ in _SMALL}
    res = {}
    g_s, d_s, m_s, v_s = _adam(pack_all[:, :, :_SMALL_PAD], _pack_small(w), _pack_small(m), _pack_small(v), "adam_small")
    for name, arr in _unpack_small(g_s, shapes).items():
        res[name] = [arr]
    for vec in (d_s, m_s, v_s):
        for name, arr in _unpack_small(vec, shapes).items():
            res[name].append(arr)
    cw_parts = pack_all[:, 0, _SMALL_PAD:].reshape(N_DEV, CONV_K, XBC_W)
    cw_mine = lax.dynamic_slice_in_dim(cw_parts, me * cw_n, cw_n, axis=2)
    res["conv_w"] = [a[None] for a in _adam(cw_mine, conv_w[0], m_conv_w[0], v_conv_w[0], "adam_conv_w")]

    dmod_piece = lax.dynamic_slice_in_dim(pack_all[:, 0, :3 * D_MODEL], me * ada_n, ada_n, axis=1)
    g_ada = _gw_ada(c_all, dmod_piece)
    res["w_ada"] = [a[None] for a in _adam(g_ada[None], w_ada[0], m_w_ada[0], v_w_ada[0], "adam_w_ada")]

    cat = lambda d: jnp.concatenate([d["w_attn_proj"][0], d["w_ssm_proj"][0], d["w_out"][0]], axis=0)
    rows_res = _adam(_exchange_wait(started["rows"], g_ada, False, "rs_rows_wait"), cat(w), cat(m), cat(v), "adam_w_rows")
    res["w_in"] = [a.T[None] for a in _adam(_exchange_wait(started["in"], rows_res[0], False, "rs_in_wait"),
                                            w_in[0].T, m_w_in[0].T, v_w_in[0].T, "adam_w_in")]
    res["w_attn_proj"] = [a[None, :r_ap] for a in rows_res]
    res["w_ssm_proj"] = [a[None, r_ap:r_ap + r_sp] for a in rows_res]
    res["w_out"] = [a[None, r_ap + r_sp:] for a in rows_res]

    outs = [loss, r["grad_x"][None]]
    for j in range(4):
        outs += [res[name][j] for name in WEIGHTS]
    return tuple(outs)
```

```python
import functools
import math

import numpy as np
import jax
import jax.numpy as jnp
from jax import lax
from jax.experimental import pallas as pl
from jax.experimental.pallas import tpu as pltpu

F32 = jnp.float32
BF = jnp.bfloat16
HI = lax.Precision.HIGHEST

D_MODEL = 1024
ATTN_HEADS = 16
KV_HEADS = 4
GRP = ATTN_HEADS // KV_HEADS
HEAD_DIM = 64
ATTN_W = ATTN_HEADS * HEAD_DIM
KV_W = KV_HEADS * HEAD_DIM
BLOCK = 128
REL_BUCKETS = 32
REL_MAX_DIST = 128
SSM_W = 2048
SSM_P = 64
SSM_HEADS = 32
SSM_G = 4
SSM_R = 8
SSM_N = 128
CONV_K = 4
XBC_W = SSM_W + 2 * SSM_G * SSM_N
SEG_W = (ATTN_W, KV_W, KV_W, ATTN_W, SSM_W, XBC_W, SSM_HEADS, D_MODEL, D_MODEL)
SEG_OFF = tuple(int(v) for v in np.cumsum((0,) + SEG_W))
IN_W = SEG_OFF[-1]
GATE_SEGS = (3, 4, 7, 8)
EPS = 1e-6
N_DEV = 8
ADAM_LR, ADAM_B1, ADAM_B2, ADAM_EPS, ADAM_WD, ADAM_STEP = 0.001, 0.9, 0.999, 1e-08, 0.01, 10
VMEM_LIMIT = 60 * 1024 * 1024
MESH = pl.DeviceIdType.MESH
ANY = pl.BlockSpec(memory_space=pl.ANY)


def _dot(a, b, precision=None):
    return jnp.dot(a, b, preferred_element_type=F32, precision=precision)


def _dot_nt(a, b, precision=None):
    return lax.dot_general(a, b, (((1,), (1,)), ((), ())), preferred_element_type=F32, precision=precision)


def _dot_tn(a, b, precision=None):
    return lax.dot_general(a, b, (((0,), (0,)), ((), ())), preferred_element_type=F32, precision=precision)


def _bf(a):
    return a.astype(BF)


def _sig(a):
    return 0.5 * jnp.tanh(0.5 * a) + 0.5


def _params(**kw):
    return pltpu.CompilerParams(vmem_limit_bytes=VMEM_LIMIT, **kw)


def _full(shape):
    nd = len(shape)
    return pl.BlockSpec(shape, lambda i: (0,) * nd)


def _rows(tm, w):
    return pl.BlockSpec((tm, w), lambda i: (i, 0))


def _inproj(x, norm_w, scale, shift, w_t, tm=256):
    s = x.shape[0]

    def body(x_ref, nw_ref, sc_ref, sh_ref, w_hbm, *rest):
        outs, h_ref, w_vm, sem = rest[:9], rest[9], rest[10], rest[11]

        @pl.when(pl.program_id(0) == 0)
        def _():
            cp = pltpu.make_async_copy(w_hbm, w_vm, sem)
            cp.start()
            cp.wait()

        xv = x_ref[...]
        r = lax.rsqrt(jnp.mean(xv * xv, axis=-1, keepdims=True) + EPS)
        h = xv * r * (nw_ref[...] * (1.0 + sc_ref[...])) + sh_ref[...]
        hb = _bf(h)
        h_ref[...] = hb
        for j in range(9):
            outs[j][...] = _dot_nt(hb, w_vm[SEG_OFF[j]:SEG_OFF[j + 1], :]).astype(outs[j].dtype)

    vec = _full((1, D_MODEL))
    return pl.pallas_call(
        body, name="inproj", grid=(s // tm,),
        in_specs=[_rows(tm, D_MODEL), vec, vec, vec, ANY],
        out_specs=[_rows(tm, w) for w in SEG_W] + [_rows(tm, D_MODEL)],
        out_shape=[jax.ShapeDtypeStruct((s, w), BF if j in GATE_SEGS else F32) for j, w in enumerate(SEG_W)]
                  + [jax.ShapeDtypeStruct((s, D_MODEL), BF)],
        scratch_shapes=[pltpu.VMEM((IN_W, D_MODEL), BF), pltpu.SemaphoreType.DMA],
        compiler_params=_params(dimension_semantics=("arbitrary",)),
    )(x, norm_w, scale, shift, w_t)


def _bucket_onehot_t():
    qi = jnp.arange(BLOCK)[:, None]
    kj = jnp.arange(2 * BLOCK)[None, :]
    dist = qi + BLOCK - kj
    n = jnp.maximum(dist, 0)
    max_exact = REL_BUCKETS // 2
    nf = jnp.maximum(n, 1).astype(F32)
    large = max_exact + (jnp.log(nf / max_exact) / math.log(REL_MAX_DIST / max_exact)
                         * (REL_BUCKETS - max_exact)).astype(jnp.int32)
    large = jnp.minimum(large, REL_BUCKETS - 1)
    bucket = jnp.where(n < max_exact, n, large).reshape(1, BLOCK * 2 * BLOCK)
    return (bucket == jnp.arange(REL_BUCKETS)[:, None]).astype(F32)


def _bias_dense(rel_bias_t, oh_t):
    def body(rb_ref, oh_ref, o_ref):
        o_ref[...] = _dot(rb_ref[...], oh_ref[...], HI)

    return pl.pallas_call(
        body, name="bias_dense", out_shape=jax.ShapeDtypeStruct((ATTN_HEADS, BLOCK * 2 * BLOCK), F32),
        compiler_params=_params(),
    )(rel_bias_t, oh_t)


def _bias_grad(ds_sum, oh_t):
    def body(ds_ref, oh_ref, o_ref):
        o_ref[...] = _dot_nt(ds_ref[...], oh_ref[...], HI)

    return pl.pallas_call(
        body, name="bias_grad", out_shape=jax.ShapeDtypeStruct((ATTN_HEADS, REL_BUCKETS), F32),
        compiler_params=_params(),
    )(ds_sum, oh_t)


def _group_sum(a, e):
    hi = _bf(a)
    return _dot(hi, e) + _dot(_bf(a - hi.astype(F32)), e)


def _group_bcast(a, e3t):
    hi = _bf(a)
    r1 = a - hi.astype(F32)
    mid = _bf(r1)
    return _dot(jnp.concatenate([hi, mid, _bf(r1 - mid.astype(F32))], axis=1), e3t)


def _membership(width, group, ngroups):
    e = (jnp.arange(width)[:, None] // group == jnp.arange(ngroups)[None, :]).astype(BF)
    return e, jnp.tile(e.T, (3, 1))


def _fold(width, group):
    return (jnp.arange(width)[:, None] % group == jnp.arange(group)[None, :]).astype(BF)


def _heads_norm(t, w_x, e, e3t):
    r = lax.rsqrt(_group_sum(t * t, e) * (1.0 / HEAD_DIM) + EPS)
    r_x = _group_bcast(r, e3t)
    return t * r_x * w_x, r_x


def _heads_norm_bwd(t, r_x, w_x, d, e, e3t):
    wd = d * w_x
    corr = _group_bcast(_group_sum(t * wd, e) * (1.0 / HEAD_DIM), e3t)
    return r_x * wd - t * (r_x * r_x * r_x) * corr, jnp.sum(d * t * r_x, axis=0, keepdims=True)


def _stack_heads(a, hk):
    return jnp.concatenate([a[:, (hk * GRP + g) * HEAD_DIM:(hk * GRP + g + 1) * HEAD_DIM] for g in range(GRP)], axis=0)


def _stack_cols(a, hk):
    return jnp.concatenate([a[:, hk * GRP + g:hk * GRP + g + 1] for g in range(GRP)], axis=0)


def _window_mask(first):
    qi = jnp.bitwise_and(lax.broadcasted_iota(jnp.int32, (GRP * BLOCK, 2 * BLOCK), 0), BLOCK - 1)
    kj = lax.broadcasted_iota(jnp.int32, (GRP * BLOCK, 2 * BLOCK), 1)
    prev_ok = jnp.logical_and(kj > qi, jnp.logical_not(first))
    cur_ok = jnp.logical_and(kj >= BLOCK, kj - BLOCK <= qi)
    return jnp.logical_or(jnp.logical_and(kj < BLOCK, prev_ok), cur_ok)


def _attn_consts(qnw, knw):
    eq, eq3t = _membership(ATTN_W, HEAD_DIM, ATTN_HEADS)
    ek, ek3t = _membership(KV_W, HEAD_DIM, ATTN_HEADS)
    return (jnp.tile(qnw, (1, ATTN_HEADS)), jnp.tile(knw, (1, KV_HEADS)), eq, eq3t, ek, ek3t)


def _attn_fwd(q, k, v, bias, sinks, consts):
    s = q.shape[0]
    nb = s // BLOCK
    gq = GRP * BLOCK
    bias_t = bias.reshape(KV_HEADS, GRP, BLOCK, 2 * BLOCK).transpose(0, 3, 1, 2).reshape(KV_HEADS, 2 * BLOCK, gq)
    sink_rows = jnp.repeat(sinks.reshape(KV_HEADS, GRP), BLOCK, axis=1).reshape(KV_HEADS, 1, gq)
    eye = jnp.eye(gq, dtype=BF)

    def body(q_ref, kp_ref, kc_ref, vp_ref, vc_ref, b_ref, bt_ref, sk_ref, skr_ref, eye_ref,
             qw_ref, kw_ref, eq_ref, eq3_ref, ek_ref, ek3_ref, o_ref, lse_ref):
        i = pl.program_id(0)
        mask = _window_mask(i == 0)
        kj = lax.broadcasted_iota(jnp.int32, (2 * BLOCK, gq), 0)
        qi = jnp.bitwise_and(lax.broadcasted_iota(jnp.int32, (2 * BLOCK, gq), 1), BLOCK - 1)
        mask_t = jnp.logical_or(jnp.logical_and(kj < BLOCK, jnp.logical_and(kj > qi, i > 0)),
                                jnp.logical_and(kj >= BLOCK, kj - BLOCK <= qi))
        qn = _bf(_heads_norm(q_ref[...], qw_ref[...], eq_ref[...], eq3_ref[...])[0])
        kn = _bf(_heads_norm(jnp.concatenate([kp_ref[...], kc_ref[...]], axis=0), kw_ref[...], ek_ref[...], ek3_ref[...])[0])
        vv = _bf(jnp.concatenate([vp_ref[...], vc_ref[...]], axis=0))
        ones = jnp.ones((2 * BLOCK, HEAD_DIM), BF)
        lses = []
        for hk in range(KV_HEADS):
            ks = slice(hk * HEAD_DIM, (hk + 1) * HEAD_DIM)
            qg = _stack_heads(qn, hk)
            sc_t = jnp.where(mask_t, _dot_nt(kn[:, ks], qg) * (HEAD_DIM ** -0.5) + bt_ref[hk], -1e30)
            m_row = jnp.maximum(jnp.max(sc_t, axis=0, keepdims=True), skr_ref[hk])
            m = _dot_nt(eye_ref[...], _bf(jnp.broadcast_to(m_row, (8, gq))))[:, 0:1]
            sc = _dot_nt(qg, kn[:, ks]) * (HEAD_DIM ** -0.5)
            sc = sc + b_ref[hk * GRP:(hk + 1) * GRP].reshape(gq, 2 * BLOCK)
            p = _bf(jnp.exp(jnp.where(mask, sc, -1e30) - m))
            sink = jnp.concatenate([jnp.full((BLOCK, 1), sk_ref[0, hk * GRP + g], F32) for g in range(GRP)], axis=0)
            pv = _dot(p, jnp.concatenate([vv[:, ks], ones], axis=1))
            den = pv[:, HEAD_DIM:HEAD_DIM + 1] + jnp.exp(sink - m)
            out = pv[:, :HEAD_DIM] * (1.0 / den)
            lse = m + jnp.log(den)
            for g in range(GRP):
                h = hk * GRP + g
                o_ref[:, h * HEAD_DIM:(h + 1) * HEAD_DIM] = out[g * BLOCK:(g + 1) * BLOCK]
                lses.append(lse[g * BLOCK:(g + 1) * BLOCK])
        lse_ref[...] = jnp.concatenate(lses, axis=1)

    cur = lambda w: pl.BlockSpec((BLOCK, w), lambda i: (i, 0))
    prev = lambda w: pl.BlockSpec((BLOCK, w), lambda i: (jnp.maximum(i - 1, 0), 0))
    whole = lambda a: pl.BlockSpec(a.shape, lambda i: (0,) * a.ndim)
    return pl.pallas_call(
        body, name="attn_fwd", grid=(nb,),
        in_specs=[cur(ATTN_W), prev(KV_W), cur(KV_W), prev(KV_W), cur(KV_W), whole(bias), whole(bias_t),
                  pl.BlockSpec(memory_space=pltpu.SMEM), whole(sink_rows), whole(eye)] + [_full(c.shape) for c in consts],
        out_specs=[cur(ATTN_W), cur(ATTN_HEADS)],
        out_shape=[jax.ShapeDtypeStruct((s, ATTN_W), F32), jax.ShapeDtypeStruct((s, ATTN_HEADS), F32)],
        compiler_params=_params(dimension_semantics=("arbitrary",)),
    )(q, k, k, v, v, bias, bias_t, sinks, sink_rows, eye, *consts)


def _conv_taps(xbc, tail):
    ext = jnp.concatenate([tail, xbc], axis=0)
    return [pltpu.roll(ext, CONV_K - 1 - j, axis=0)[8:8 + BLOCK] if j < CONV_K - 1 else xbc for j in range(CONV_K)]


def _softplus(u):
    return jnp.maximum(u, 0.0) + jnp.log(1.0 + jnp.exp(-jnp.abs(u)))


def _tril():
    r = lax.broadcasted_iota(jnp.int32, (BLOCK, BLOCK), 0)
    c = lax.broadcasted_iota(jnp.int32, (BLOCK, BLOCK), 1)
    return r >= c


def _triu():
    r = lax.broadcasted_iota(jnp.int32, (BLOCK, BLOCK), 0)
    c = lax.broadcasted_iota(jnp.int32, (BLOCK, BLOCK), 1)
    return r <= c


def _exact_left(m01, a):
    hi = _bf(a)
    r1 = a - hi.astype(F32)
    mid = _bf(r1)
    return _dot(m01, hi) + _dot(m01, mid) + _dot(m01, _bf(r1 - mid.astype(F32)))


def _ssd_common(conv, dtr_ref, dtb_ref, alog_ref, e3_ref):
    sg = _sig(conv)
    xact = conv * sg
    u = dtr_ref[...] + dtb_ref[...]
    dt = _softplus(u)
    a = -jnp.exp(alog_ref[...])
    trilb = _tril()
    acum = _exact_left(trilb.astype(BF), dt * a)
    both = _group_bcast(jnp.concatenate([dt, acum], axis=0), e3_ref[...])
    dt_x, acum_x = both[:BLOCK], both[BLOCK:]
    return sg, xact, u, dt, a, trilb, acum, dt_x, acum_x


def _ssd_fwd(xbc, dt_raw, conv_w, conv_b, dt_bias, a_log, dsk_x, e3t):
    s = xbc.shape[0]
    nc = s // BLOCK

    def body(x_ref, tail_ref, dtr_ref, cw_ref, cb_ref, dtb_ref, alog_ref, dsk_ref, e3_ref,
             y_ref, hp_ref, conv_ref, hst):
        i = pl.program_id(0)

        @pl.when(i == 0)
        def _():
            hst[...] = jnp.zeros_like(hst)

        tail = jnp.where(i > 0, tail_ref[...], 0.0)
        taps = _conv_taps(x_ref[...], tail)
        conv = cb_ref[...] + sum(taps[j] * cw_ref[j:j + 1, :] for j in range(CONV_K))
        conv_ref[...] = conv
        _, xact, _, _, _, trilb, acum, dt_x, acum_x = _ssd_common(conv, dtr_ref, dtb_ref, alog_ref, e3_ref)
        xs = xact[:, :SSM_W]
        acum_t = acum.T
        ea_x = jnp.exp(acum_x)
        last_x = acum_x[BLOCK - 1:BLOCK, :]
        xdt = xs * dt_x
        xw = xdt * jnp.exp(last_x - acum_x)
        cd_x = jnp.exp(last_x)
        hprev = hst[...]
        hp_ref[0] = hprev
        dsk = dsk_ref[...]
        for g in range(SSM_G):
            bg = _bf(xact[:, SSM_W + g * SSM_N:SSM_W + (g + 1) * SSM_N])
            cg = _bf(xact[:, SSM_W + SSM_G * SSM_N + g * SSM_N:SSM_W + SSM_G * SSM_N + (g + 1) * SSM_N])
            sl = slice(g * SSM_R * SSM_P, (g + 1) * SSM_R * SSM_P)
            cb = _dot_nt(cg, bg)
            yoff = _dot(cg, _bf(hprev[:, sl])) * ea_x[:, sl]
            hst[:, sl] = hprev[:, sl] * cd_x[:, sl] + _dot_tn(bg, _bf(xw[:, sl]))
            for r in range(SSM_R):
                hh = g * SSM_R + r
                hs = slice(hh * SSM_P, (hh + 1) * SSM_P)
                seg = jnp.where(trilb, acum[:, hh:hh + 1] - acum_t[hh:hh + 1, :], -1e30)
                mm = cb * jnp.exp(seg)
                yd = _dot(_bf(mm), _bf(xdt[:, hs]))
                y_ref[:, hs] = yd + yoff[:, r * SSM_P:(r + 1) * SSM_P] + dsk[:, hs] * xs[:, hs]

    chunk = lambda w: pl.BlockSpec((BLOCK, w), lambda i: (i, 0))
    return pl.pallas_call(
        body, name="ssd_fwd", grid=(nc,),
        in_specs=[chunk(XBC_W), pl.BlockSpec((8, XBC_W), lambda i: (jnp.maximum(i * (BLOCK // 8) - 1, 0), 0)),
                  chunk(SSM_HEADS), _full((CONV_K, XBC_W)), _full((1, XBC_W)), _full((1, SSM_HEADS)),
                  _full((1, SSM_HEADS)), _full((1, SSM_W)), _full((3 * SSM_HEADS, SSM_W))],
        out_specs=[chunk(SSM_W), pl.BlockSpec((1, SSM_N, SSM_W), lambda i: (i, 0, 0)), chunk(XBC_W)],
        out_shape=[jax.ShapeDtypeStruct((s, SSM_W), F32), jax.ShapeDtypeStruct((nc, SSM_N, SSM_W), F32),
                   jax.ShapeDtypeStruct((s, XBC_W), F32)],
        scratch_shapes=[pltpu.VMEM((SSM_N, SSM_W), F32)],
        compiler_params=_params(dimension_semantics=("arbitrary",)),
    )(xbc, xbc, dt_raw, conv_w, conv_b, dt_bias, a_log, dsk_x, e3t)


def _dsilu(z, sg):
    return sg * (1.0 + z * (1.0 - sg))


def _mid(x, tgt, o_att, za, ypre, zm, ga, gb, gate, ssm_nw, wap, wsp, wout, tm=256):
    s = x.shape[0]
    gw = SSM_W // SSM_G

    def body(x_ref, t_ref, o_ref, za_ref, yp_ref, zm_ref, ga_ref, gb_ref, gate_ref, nw_ref, wap_h, wsp_h, wout_h,
             dout_ref, do_ref, dza_ref, dyp_ref, dzm_ref, dga_ref, dgb_ref,
             yag_ref, dya_ref, yn_ref, dyb_ref, mg_ref, dob_ref, gnw_ref, dgate_ref, loss_ref,
             wap_v, wsp_v, wout_v, sem):
        i = pl.program_id(0)

        @pl.when(i == 0)
        def _():
            cps = [pltpu.make_async_copy(a, b, sem.at[j])
                   for j, (a, b) in enumerate(((wap_h, wap_v), (wsp_h, wsp_v), (wout_h, wout_v)))]
            for cp in cps:
                cp.start()
            gnw_ref[...] = jnp.zeros_like(gnw_ref)
            dgate_ref[...] = jnp.zeros_like(dgate_ref)
            loss_ref[...] = jnp.zeros_like(loss_ref)
            for cp in cps:
                cp.wait()

        gate = gate_ref[...]
        nw = nw_ref[...]
        o_att = o_ref[...]
        z_a = za_ref[...].astype(F32)
        s_a = _sig(z_a)
        silu_a = z_a * s_a
        yag = _bf(o_att * silu_a)
        yag_ref[...] = yag
        y_a = _dot(yag, wap_v[...])
        ypre = yp_ref[...]
        z_m = zm_ref[...].astype(F32)
        s_m = _sig(z_m)
        silu_m = z_m * s_m
        yg = ypre * silu_m
        rinv = jnp.concatenate(
            [jnp.broadcast_to(lax.rsqrt(jnp.mean(yg[:, g * gw:(g + 1) * gw] ** 2, axis=-1, keepdims=True) + EPS), (tm, gw))
             for g in range(SSM_G)], axis=1)
        ynr = yg * rinv
        yn = _bf(ynr * nw)
        yn_ref[...] = yn
        y_b = _dot(yn, wsp_v[...])
        g_a = _sig(ga_ref[...].astype(F32))
        g_b = _sig(gb_ref[...].astype(F32))
        merged = _bf(g_a * y_a + g_b * y_b)
        mg_ref[...] = merged
        o = _dot(merged, wout_v[...])
        diff = x_ref[...] + gate * o - t_ref[...]
        loss_ref[...] += (0.5 / D_MODEL) * jnp.sum(diff * diff, axis=(0, 1), keepdims=True)
        dout = diff * (1.0 / D_MODEL)
        dout_ref[...] = dout
        dgate_ref[...] += jnp.sum(dout * o, axis=0, keepdims=True)
        d_o = _bf(dout * gate)
        dob_ref[...] = d_o
        dmerged = _dot_nt(d_o, wout_v[...])
        dy_a = dmerged * g_a
        dy_b = dmerged * g_b
        dga_ref[...] = _bf(dy_a * y_a * (1.0 - g_a))
        dgb_ref[...] = _bf(dy_b * y_b * (1.0 - g_b))
        dy_a = _bf(dy_a)
        dy_b = _bf(dy_b)
        dya_ref[...] = dy_a
        dyb_ref[...] = dy_b
        dyag = _dot_nt(dy_a, wap_v[...])
        do_ref[...] = dyag * silu_a
        dza_ref[...] = _bf(dyag * o_att * _dsilu(z_a, s_a))
        dyn = _dot_nt(dy_b, wsp_v[...])
        gnw_ref[...] += jnp.sum(dyn * ynr, axis=0, keepdims=True)
        dynw = dyn * nw
        corr = jnp.concatenate(
            [jnp.broadcast_to(jnp.mean((dynw * ynr)[:, g * gw:(g + 1) * gw], axis=-1, keepdims=True), (tm, gw))
             for g in range(SSM_G)], axis=1)
        dyg = rinv * (dynw - ynr * corr)
        dyp_ref[...] = dyg * silu_m
        dzm_ref[...] = _bf(dyg * ypre * _dsilu(z_m, s_m))

    r1, r2 = _rows(tm, D_MODEL), _rows(tm, SSM_W)
    sd = jax.ShapeDtypeStruct
    return pl.pallas_call(
        body, name="mid", grid=(s // tm,),
        in_specs=[r1, r1, r1, r1, r2, r2, r1, r1, _full((1, D_MODEL)), _full((1, SSM_W)), ANY, ANY, ANY],
        out_specs=[r1, r1, r1, r2, r2, r1, r1, r1, r1, r2, r1, r1, r1,
                   _full((1, SSM_W)), _full((1, D_MODEL)), _full((1, 1))],
        out_shape=[sd((s, D_MODEL), F32), sd((s, ATTN_W), F32), sd((s, ATTN_W), BF), sd((s, SSM_W), F32),
                   sd((s, SSM_W), BF), sd((s, D_MODEL), BF), sd((s, D_MODEL), BF),
                   sd((s, ATTN_W), BF), sd((s, D_MODEL), BF), sd((s, SSM_W), BF), sd((s, D_MODEL), BF),
                   sd((s, D_MODEL), BF), sd((s, D_MODEL), BF),
                   sd((1, SSM_W), F32), sd((1, D_MODEL), F32), sd((1, 1), F32)],
        scratch_shapes=[pltpu.VMEM((ATTN_W, D_MODEL), BF), pltpu.VMEM((SSM_W, D_MODEL), BF), pltpu.VMEM((D_MODEL, D_MODEL), BF),
                        pltpu.SemaphoreType.DMA((3,))],
        compiler_params=_params(dimension_semantics=("arbitrary",)),
    )(x, tgt, o_att, za, ypre, zm, ga, gb, gate, ssm_nw, wap, wsp, wout)


def _attn_bwd(q, k, v, bias, sinks, consts, o_att, lse, d_o):
    s = q.shape[0]
    nb = s // BLOCK
    folds = (_fold(ATTN_W, HEAD_DIM), _fold(KV_W, HEAD_DIM))

    def body(q_ref, kp_ref, kc_ref, vp_ref, vc_ref, b_ref, skv_ref, qw_ref, kw_ref, eq_ref, eq3_ref, ek_ref, ek3_ref,
             fq_ref, fk_ref, o_ref, lse_ref, do_ref,
             dq_ref, dk_ref, dv_ref, dss_ref, gqw_ref, gkw_ref, gsk_ref, ckn, cv, dqn_s, dkn_s, dv_s, gq_x, gk_x):
        i = pl.program_id(0)
        kw, ek, ek3 = kw_ref[...], ek_ref[...], ek3_ref[...]

        @pl.when(i == 0)
        def _():
            for ref in (ckn, cv, dss_ref, gq_x, gk_x, gsk_ref):
                ref[...] = jnp.zeros_like(ref)

        @pl.when(i < nb)
        def _():
            mask = _window_mask(i == 0)
            qw, eq, eq3 = qw_ref[...], eq_ref[...], eq3_ref[...]
            qf = q_ref[...]
            qnf, rq_x = _heads_norm(qf, qw, eq, eq3)
            qn = _bf(qnf)
            kf = jnp.concatenate([kp_ref[...], kc_ref[...]], axis=0)
            knf, rk_x = _heads_norm(kf, kw, ek, ek3)
            kn = _bf(knf)
            vv = _bf(jnp.concatenate([vp_ref[...], vc_ref[...]], axis=0))
            d_of = do_ref[...]
            d_ob = _bf(d_of)
            lse_all = lse_ref[...]
            delta = _group_sum(d_of * o_ref[...], eq)
            gsk_ref[...] += jnp.sum(-jnp.exp(skv_ref[...] - lse_all) * delta, axis=0, keepdims=True)
            for hk in range(KV_HEADS):
                ks = slice(hk * HEAD_DIM, (hk + 1) * HEAD_DIM)
                qg = _stack_heads(qn, hk)
                sc = _dot_nt(qg, kn[:, ks]) * (HEAD_DIM ** -0.5)
                sc = sc + b_ref[hk * GRP:(hk + 1) * GRP].reshape(GRP * BLOCK, 2 * BLOCK)
                p = jnp.where(mask, jnp.exp(sc - _stack_cols(lse_all, hk)), 0.0)
                d_og = _stack_heads(d_ob, hk)
                ds = p * (_dot_nt(d_og, vv[:, ks]) - _stack_cols(delta, hk))
                dss_ref[hk * GRP:(hk + 1) * GRP] += ds.reshape(GRP, BLOCK, 2 * BLOCK)
                dsb = _bf(ds)
                dv_s[:, ks] = _dot_tn(_bf(p), d_og)
                dkn_s[:, ks] = _dot_tn(dsb, qg) * (HEAD_DIM ** -0.5)
                dqn = _dot(dsb, kn[:, ks]) * (HEAD_DIM ** -0.5)
                for g in range(GRP):
                    h = hk * GRP + g
                    dqn_s[:, h * HEAD_DIM:(h + 1) * HEAD_DIM] = dqn[g * BLOCK:(g + 1) * BLOCK]
            dq, gq = _heads_norm_bwd(qf, rq_x, qw, dqn_s[...], eq, eq3)
            dq_ref[...] = _bf(dq)
            gq_x[...] += gq
            dk, gk = _heads_norm_bwd(kf[:BLOCK], rk_x[:BLOCK], kw, ckn[...] + dkn_s[0:BLOCK, :], ek, ek3)
            dk_ref[...] = _bf(dk)
            gk_x[...] += gk
            dv_ref[...] = _bf(cv[...] + dv_s[0:BLOCK, :])
            ckn[...] = dkn_s[BLOCK:2 * BLOCK, :]
            cv[...] = dv_s[BLOCK:2 * BLOCK, :]

        @pl.when(i == nb)
        def _():
            kc = kc_ref[...]
            dk, gk = _heads_norm_bwd(kc, _heads_norm(kc, kw, ek, ek3)[1], kw, ckn[...], ek, ek3)
            dk_ref[...] = _bf(dk)
            dv_ref[...] = _bf(cv[...])
            gqw_ref[...] = _group_sum(jnp.broadcast_to(gq_x[...], (8, ATTN_W)), fq_ref[...])[0:1]
            gkw_ref[...] = _group_sum(jnp.broadcast_to(gk_x[...] + gk, (8, KV_W)), fk_ref[...])[0:1]

    last = nb - 1
    cur = lambda w: pl.BlockSpec((BLOCK, w), lambda i: (jnp.minimum(i, last), 0))
    prev = lambda w: pl.BlockSpec((BLOCK, w), lambda i: (jnp.maximum(jnp.minimum(i, last) - 1, 0), 0))
    late = lambda w: pl.BlockSpec((BLOCK, w), lambda i: (jnp.maximum(i - 1, 0), 0))
    sd = jax.ShapeDtypeStruct
    return pl.pallas_call(
        body, name="attn_bwd", grid=(nb + 1,),
        in_specs=[cur(ATTN_W), prev(KV_W), cur(KV_W), prev(KV_W), cur(KV_W),
                  pl.BlockSpec((ATTN_HEADS, BLOCK, 2 * BLOCK), lambda i: (0, 0, 0)), _full((1, ATTN_HEADS))]
                 + [_full(c.shape) for c in consts + folds] + [cur(ATTN_W), cur(ATTN_HEADS), cur(ATTN_W)],
        out_specs=[cur(ATTN_W), late(KV_W), late(KV_W),
                   pl.BlockSpec((ATTN_HEADS, BLOCK, 2 * BLOCK), lambda i: (0, 0, 0)),
                   _full((1, HEAD_DIM)), _full((1, HEAD_DIM)), _full((1, ATTN_HEADS))],
        out_shape=[sd((s, ATTN_W), BF), sd((s, KV_W), BF), sd((s, KV_W), BF),
                   sd((ATTN_HEADS, BLOCK, 2 * BLOCK), F32), sd((1, HEAD_DIM), F32), sd((1, HEAD_DIM), F32),
                   sd((1, ATTN_HEADS), F32)],
        scratch_shapes=[pltpu.VMEM((BLOCK, KV_W), F32), pltpu.VMEM((BLOCK, KV_W), F32),
                        pltpu.VMEM((BLOCK, ATTN_W), F32), pltpu.VMEM((2 * BLOCK, KV_W), F32),
                        pltpu.VMEM((2 * BLOCK, KV_W), F32), pltpu.VMEM((1, ATTN_W), F32), pltpu.VMEM((1, KV_W), F32)],
        compiler_params=_params(dimension_semantics=("arbitrary",)),
    )(q, k, k, v, v, bias, sinks, *consts, *folds, o_att, lse, d_o)


def _ssd_bwd(xbc, conv_all, dt_raw, conv_w, dt_bias, a_log, dsk_x, e_mat, e3t, hprev_all, dy_all):
    s = xbc.shape[0]
    nc = s // BLOCK
    gw = SSM_R * SSM_P
    b0, c0 = SSM_W, SSM_W + SSM_G * SSM_N

    def body(x_ref, conv_ref, dtr_ref, cw_ref, dtb_ref, alog_ref, dsk_ref, e_ref, e3_ref, hp_ref, dy_ref,
             dx_ref, ddt_ref, gcw_ref, gcb_ref, gdtb_ref, galog_ref, gdsk_ref,
             dh, nhead, gdskx, dxdt_s, dbc_s):
        i = pl.program_id(0)
        c = nc - 1 - i

        @pl.when(i == 0)
        def _():
            for ref in (dh, nhead, gdskx, gcw_ref, gcb_ref, gdtb_ref, galog_ref, gdsk_ref):
                ref[...] = jnp.zeros_like(ref)

        conv = conv_ref[...]
        sg, xact, u, dt, a, trilb, acum, dt_x, acum_x = _ssd_common(conv, dtr_ref, dtb_ref, alog_ref, e3_ref)
        xs = xact[:, :SSM_W]
        acum_t = acum.T
        ea_x = jnp.exp(acum_x)
        last_x = acum_x[BLOCK - 1:BLOCK, :]
        dte_x = jnp.exp(last_x - acum_x)
        cd_x = jnp.exp(last_x)
        xdt = xs * dt_x
        xw = xdt * dte_x
        hprev = hp_ref[0]
        dhn = dh[...]
        dy = dy_ref[...]
        gdskx[...] += jnp.sum(dy * xs, axis=0, keepdims=True)
        dyea = dy * ea_x
        lane = lax.broadcasted_iota(jnp.int32, (BLOCK, SSM_HEADS), 1)
        dacum = jnp.zeros((BLOCK, SSM_HEADS), F32)
        dacc_x, dlast_x = [], []
        for g in range(SSM_G):
            bgf = xact[:, b0 + g * SSM_N:b0 + (g + 1) * SSM_N]
            cgf = xact[:, c0 + g * SSM_N:c0 + (g + 1) * SSM_N]
            bg, cg = _bf(bgf), _bf(cgf)
            sl = slice(g * gw, (g + 1) * gw)
            hpg, dhg, dyeag = _bf(hprev[:, sl]), _bf(dhn[:, sl]), _bf(dyea[:, sl])
            cb = _dot_nt(cg, bg)
            gmat = _dot(cg, hpg)
            dxw = _dot(bg, dhg)
            dxdt_s[:, sl] = dxw * dte_x[:, sl]
            dacc_x.append(dy[:, sl] * gmat * ea_x[:, sl] - dxw * xw[:, sl])
            dlast_x.append(jnp.sum(dxw * xw[:, sl], axis=0, keepdims=True)
                           + jnp.sum(dhn[:, sl] * hprev[:, sl], axis=0, keepdims=True) * cd_x[:, sl])
            dcg = _dot_nt(dyeag, hpg)
            dbg = _dot_nt(_bf(xw[:, sl]), dhg)
            dh[:, sl] = dhn[:, sl] * cd_x[:, sl] + _dot_tn(cg, dyeag)
            dcb = jnp.zeros((BLOCK, BLOCK), F32)
            for r in range(SSM_R):
                hh = g * SSM_R + r
                hs = slice(hh * SSM_P, (hh + 1) * SSM_P)
                seg = jnp.where(trilb, acum[:, hh:hh + 1] - acum_t[hh:hh + 1, :], -1e30)
                lm = jnp.exp(seg)
                mm = cb * lm
                dyh = _bf(dy[:, hs])
                dm = _dot_nt(dyh, _bf(xdt[:, hs]))
                dxdt_s[:, hs] += _dot_tn(_bf(mm), dyh)
                wm = dm * mm
                dcb = dcb + dm * lm
                dacum = dacum + _group_sum(wm - wm.T, (lane == hh).astype(BF))
            dcbb = _bf(dcb)
            dbc_s[:, g * SSM_N:(g + 1) * SSM_N] = dbg + _dot_tn(dcbb, cg)
            dbc_s[:, SSM_G * SSM_N + g * SSM_N:SSM_G * SSM_N + (g + 1) * SSM_N] = dcg + _dot(dcbb, bg)
        dxdt = dxdt_s[...]
        dxs = dy * dsk_ref[...] + dxdt * dt_x
        red = _group_sum(jnp.concatenate(
            [dxdt * xs, jnp.concatenate(dacc_x, axis=1),
             jnp.broadcast_to(jnp.concatenate(dlast_x, axis=1), (8, SSM_W))], axis=0), e_ref[...])
        row = lax.broadcasted_iota(jnp.int32, (BLOCK, SSM_HEADS), 0)
        dacum = dacum + red[BLOCK:2 * BLOCK] + jnp.where(row == BLOCK - 1, red[2 * BLOCK:2 * BLOCK + 1], 0.0)
        ddta = _exact_left(_triu().astype(BF), dacum)
        ddt = red[:BLOCK] + ddta * a
        galog_ref[...] += jnp.sum(ddta * dt, axis=0, keepdims=True) * a
        du = ddt * _sig(u)
        ddt_ref[...] = _bf(du)
        gdtb_ref[...] += jnp.sum(du, axis=0, keepdims=True)
        dconv = jnp.concatenate([dxs, dbc_s[...]], axis=1) * _dsilu(conv, sg)
        gcb_ref[...] += jnp.sum(dconv, axis=0, keepdims=True)
        ext2 = jnp.concatenate([dconv, nhead[...]], axis=0)
        ahead = [pltpu.roll(ext2, BLOCK + 8 - (CONV_K - 1 - j), axis=0)[0:BLOCK] if j < CONV_K - 1 else dconv
                 for j in range(CONV_K)]
        dx_ref[...] = _bf(sum(ahead[j] * cw_ref[j:j + 1, :] for j in range(CONV_K)))
        xraw = x_ref[...]
        gcw_ref[...] += jnp.concatenate([jnp.sum(ahead[j] * xraw, axis=0, keepdims=True) for j in range(CONV_K)], axis=0)
        nhead[...] = dconv[0:8]

        @pl.when(i == nc - 1)
        def _():
            gdsk_ref[...] = _group_sum(jnp.broadcast_to(gdskx[...], (8, SSM_W)), e_ref[...])[0:1]

    chunk = lambda w: pl.BlockSpec((BLOCK, w), lambda i: (nc - 1 - i, 0))
    sd = jax.ShapeDtypeStruct
    return pl.pallas_call(
        body, name="ssd_bwd", grid=(nc,),
        in_specs=[chunk(XBC_W), chunk(XBC_W),
                  chunk(SSM_HEADS), _full((CONV_K, XBC_W)), _full((1, SSM_HEADS)),
                  _full((1, SSM_HEADS)), _full((1, SSM_W)), _full((SSM_W, SSM_HEADS)), _full((3 * SSM_HEADS, SSM_W)),
                  pl.BlockSpec((1, SSM_N, SSM_W), lambda i: (nc - 1 - i, 0, 0)), chunk(SSM_W)],
        out_specs=[chunk(XBC_W), chunk(SSM_HEADS), _full((CONV_K, XBC_W)), _full((1, XBC_W)),
                   _full((1, SSM_HEADS)), _full((1, SSM_HEADS)), _full((1, SSM_HEADS))],
        out_shape=[sd((s, XBC_W), BF), sd((s, SSM_HEADS), BF), sd((CONV_K, XBC_W), F32), sd((1, XBC_W), F32),
                   sd((1, SSM_HEADS), F32), sd((1, SSM_HEADS), F32), sd((1, SSM_HEADS), F32)],
        scratch_shapes=[pltpu.VMEM((SSM_N, SSM_W), F32), pltpu.VMEM((8, XBC_W), F32),
                        pltpu.VMEM((1, SSM_W), F32), pltpu.VMEM((BLOCK, SSM_W), F32),
                        pltpu.VMEM((BLOCK, 2 * SSM_G * SSM_N), F32)],
        compiler_params=_params(dimension_semantics=("arbitrary",)),
    )(xbc, conv_all, dt_raw, conv_w, dt_bias, a_log, dsk_x, e_mat, e3t, hprev_all, dy_all)


def _dh(x, dout, norm_w, scale, dsegs, w_t, tm=256):
    s = x.shape[0]

    def body(x_ref, dout_ref, nw_ref, sc_ref, *rest):
        d_refs, w_hbm = rest[:9], rest[9]
        gx_ref, dshift_ref, dscale_ref, gnw_ref = rest[10:14]
        w_vm, sem = rest[14], rest[15]

        @pl.when(pl.program_id(0) == 0)
        def _():
            cp = pltpu.make_async_copy(w_hbm, w_vm, sem)
            cp.start()
            for ref in (dshift_ref, dscale_ref, gnw_ref):
                ref[...] = jnp.zeros_like(ref)
            cp.wait()

        dh = _dot(d_refs[0][...], w_vm[SEG_OFF[0]:SEG_OFF[1], :])
        for j in range(1, 9):
            dh = dh + _dot(d_refs[j][...], w_vm[SEG_OFF[j]:SEG_OFF[j + 1], :])
        xv = x_ref[...]
        r = lax.rsqrt(jnp.mean(xv * xv, axis=-1, keepdims=True) + EPS)
        xn = xv * r
        nw = nw_ref[...]
        sc1 = 1.0 + sc_ref[...]
        dshift_ref[...] += jnp.sum(dh, axis=0, keepdims=True)
        dhxn = jnp.sum(dh * xn, axis=0, keepdims=True)
        dscale_ref[...] += dhxn * nw
        gnw_ref[...] += dhxn * sc1
        dxn = dh * (nw * sc1)
        gx_ref[...] = dout_ref[...] + r * (dxn - xn * jnp.mean(xn * dxn, axis=-1, keepdims=True))

    vec = _full((1, D_MODEL))
    sd = jax.ShapeDtypeStruct
    return pl.pallas_call(
        body, name="dh", grid=(s // tm,),
        in_specs=[_rows(tm, D_MODEL), _rows(tm, D_MODEL), vec, vec] + [_rows(tm, w) for w in SEG_W] + [ANY],
        out_specs=[_rows(tm, D_MODEL), vec, vec, vec],
        out_shape=[sd((s, D_MODEL), F32), sd((1, D_MODEL), F32), sd((1, D_MODEL), F32), sd((1, D_MODEL), F32)],
        scratch_shapes=[pltpu.VMEM((IN_W, D_MODEL), BF), pltpu.SemaphoreType.DMA],
        compiler_params=_params(dimension_semantics=("arbitrary",)),
    )(x, dout, norm_w, scale, *dsegs, w_t)


def _gw_seg(h, dseg, name, tm=1024):
    s, w = dseg.shape
    tn = min(w, 1024)
    tm = min(tm, s)

    def body(h_ref, d_ref, o_ref):
        @pl.when(pl.program_id(1) == 0)
        def _():
            o_ref[...] = jnp.zeros_like(o_ref)

        o_ref[...] += _dot_tn(d_ref[...], h_ref[...])

    return pl.pallas_call(
        body, name=name, grid=(w // tn, s // tm),
        in_specs=[pl.BlockSpec((tm, D_MODEL), lambda n, m: (m, 0)), pl.BlockSpec((tm, tn), lambda n, m: (m, n))],
        out_specs=pl.BlockSpec((tn, D_MODEL), lambda n, m: (n, 0)),
        out_shape=jax.ShapeDtypeStruct((w, D_MODEL), F32),
        compiler_params=_params(dimension_semantics=("arbitrary", "arbitrary")),
    )(h, dseg)


def _gw_in(h, dsegs):
    return [_gw_seg(h, d, "gw_in_%d" % j) for j, d in enumerate(dsegs)]


def _local_step(x, tgt, shift, scale, gate, w_t, rows_fn, norm_w, qnw, knw, rel_bias, sinks,
                conv_w, conv_b, dt_bias, a_log, d_skip, ssm_nw, after_mid=None, after_gw=None):
    oh_t = _bucket_onehot_t()
    bias = _bias_dense(rel_bias.T, oh_t).reshape(ATTN_HEADS, BLOCK, 2 * BLOCK)
    *segs, h = _inproj(x, norm_w, scale, shift, w_t)
    q, k, v, za, zm, xbc, dtr, ga, gb = segs
    consts = _attn_consts(qnw, knw)
    o_att, lse = _attn_fwd(q, k, v, bias, sinks, consts)
    e_mat, e3t = _membership(SSM_W, SSM_P, SSM_HEADS)
    dsk_x = jnp.repeat(d_skip, SSM_P, axis=1)
    ypre, hprev, conv = _ssd_fwd(xbc, dtr, conv_w, conv_b, dt_bias, a_log, dsk_x, e3t)
    wap, wsp, wout = rows_fn(ypre)
    (dout, d_o, dza, dyp, dzm, dga, dgb, yag, dy_a, yn, dy_b, merged, dob, g_ssm_nw, dgate, loss) = _mid(
        x, tgt, o_att, za, ypre, zm, ga, gb, gate, ssm_nw, wap, wsp, wout)
    g_wap = _gw_seg(dy_a, yag, "gw_attn_proj")
    g_wsp = _gw_seg(dy_b, yn, "gw_ssm_proj")
    g_wout = _gw_seg(dob, merged, "gw_out")
    zero = after_mid(g_wap, g_wsp, g_wout) if after_mid is not None else 0.0
    dq, dk, dv, dss, g_qnw, g_knw, g_sinks = _attn_bwd(q, k, v, bias, sinks + zero, consts, o_att, lse, d_o)
    g_rel = _bias_grad(dss.reshape(ATTN_HEADS, BLOCK * 2 * BLOCK), oh_t).T
    dxbc, ddt, g_cw, g_cb, g_dtb, g_alog, g_dsk = _ssd_bwd(
        xbc, conv, dtr, conv_w, dt_bias, a_log, dsk_x, e_mat, e3t, hprev, dyp)
    dsegs = (dq, dk, dv, dza, dzm, dxbc, ddt, dga, dgb)
    g_ws = _gw_in(h, dsegs)
    zero = after_gw(g_ws) if after_gw is not None else 0.0
    gx, dshift, dscale, g_nw = _dh(x, dout, norm_w + zero, scale, dsegs, w_t)
    return dict(loss=loss, grad_x=gx, dmod=jnp.concatenate([dshift, dscale, dgate], axis=1), g_ws=g_ws,
                g_wap=g_wap, g_wsp=g_wsp, g_wout=g_wout, g_norm_w=g_nw, g_qnw=g_qnw, g_knw=g_knw, g_rel=g_rel,
                g_sinks=g_sinks, g_conv_w=g_cw, g_conv_b=g_cb, g_dt_bias=g_dtb, g_a_log=g_alog, g_d_skip=g_dsk,
                g_ssm_nw=g_ssm_nw)


def _me():
    return lax.axis_index("x"), lax.axis_index("y"), lax.axis_index("c")


def _flip(v, bit):
    return 1 - v if bit else v


def _ag_direct(v, name):
    def body(v_ref, out_ref, send_sems, recv_sems, local_sem):
        x, y, c = _me()
        me = 4 * x + 2 * y + c
        mine = pltpu.make_async_copy(v_ref, out_ref.at[me], local_sem)
        mine.start()
        peers = [(_flip(x, k >> 2 & 1), _flip(y, k >> 1 & 1), _flip(c, k & 1)) for k in range(1, N_DEV)]
        sends = [pltpu.make_async_remote_copy(
            src_ref=v_ref, dst_ref=out_ref.at[me], send_sem=send_sems.at[j], recv_sem=recv_sems.at[j],
            device_id=p, device_id_type=MESH) for j, p in enumerate(peers)]
        for cp in sends:
            cp.start()
        for j, (px, py, pc) in enumerate(peers):
            pltpu.make_async_remote_copy(
                src_ref=v_ref, dst_ref=out_ref.at[4 * px + 2 * py + pc], send_sem=send_sems.at[j],
                recv_sem=recv_sems.at[j], device_id=(px, py, pc), device_id_type=MESH).wait_recv()
        for cp in sends:
            cp.wait_send()
        mine.wait()

    vm = pl.BlockSpec(memory_space=pltpu.VMEM)
    return pl.pallas_call(
        body, name=name, out_shape=jax.ShapeDtypeStruct((N_DEV,) + v.shape, v.dtype),
        in_specs=[vm], out_specs=vm,
        scratch_shapes=[pltpu.SemaphoreType.DMA((N_DEV - 1,)), pltpu.SemaphoreType.DMA((N_DEV - 1,)),
                        pltpu.SemaphoreType.DMA],
        compiler_params=_params(),
    )(v)


def _ag_two_level(v, name):
    def body(v_ref, out_ref, token, send_sems, recv_sems, local_sem):
        token[...] = jnp.zeros_like(token)
        x, y, c = _me()
        me, sibling = (x, y, c), (x, y, 1 - c)
        chips = [(1 - x, y), (x, 1 - y), (1 - x, 1 - y)]

        def slot(px, py, pc):
            return out_ref.at[4 * px + 2 * py + pc]

        def copy(k, block, to, src=None):
            return pltpu.make_async_remote_copy(
                src_ref=slot(*block) if src is None else src, dst_ref=slot(*block),
                send_sem=send_sems.at[k], recv_sem=recv_sems.at[k], device_id=to, device_id_type=MESH)

        mine = pltpu.make_async_copy(v_ref, slot(*me), local_sem)
        mine.start()
        first = [copy(0, me, sibling, src=v_ref)]
        first += [copy(1 + j, me, (*chip, c), src=v_ref) for j, chip in enumerate(chips)]
        for cp in first:
            cp.start()
        passed = [copy(4 + j, (*chip, c), sibling) for j, chip in enumerate(chips)]
        for j, chip in enumerate(chips):
            copy(1 + j, (*chip, c), me).wait_recv()
            passed[j].start()
        copy(0, sibling, me).wait_recv()
        for j, chip in enumerate(chips):
            copy(4 + j, (*chip, 1 - c), me).wait_recv()
        for cp in first + passed:
            cp.wait_send()
        mine.wait()

    out, token = pl.pallas_call(
        body, name=name,
        out_shape=(jax.ShapeDtypeStruct((N_DEV,) + v.shape, v.dtype), jax.ShapeDtypeStruct((8, 128), v.dtype)),
        in_specs=[ANY], out_specs=(ANY, pl.BlockSpec(memory_space=pltpu.VMEM)),
        scratch_shapes=[pltpu.SemaphoreType.DMA((7,)), pltpu.SemaphoreType.DMA((7,)), pltpu.SemaphoreType.DMA],
        compiler_params=_params(),
    )(v)
    return out, token[0:1, 0:1]


def _rs_sibling(g, name):
    def body(g_ref, out_ref, send_sems, recv_sems):
        x, y, c = _me()
        cps = [pltpu.make_async_remote_copy(
            src_ref=g_ref.at[2 * ch + 1 - c], dst_ref=out_ref.at[ch], send_sem=send_sems.at[ch],
            recv_sem=recv_sems.at[ch], device_id=(x, y, 1 - c), device_id_type=MESH) for ch in range(4)]
        for cp in cps:
            cp.start()
        for cp in cps:
            cp.wait()

    return pl.pallas_call(
        body, name=name, out_shape=jax.ShapeDtypeStruct((4,) + g.shape[1:], g.dtype),
        in_specs=[ANY], out_specs=ANY,
        scratch_shapes=[pltpu.SemaphoreType.DMA((4,)), pltpu.SemaphoreType.DMA((4,))],
        compiler_params=_params(),
    )(g)


def _add_sibling(g, got, name):
    _, r, n = g.shape
    tr = min(r, 256)

    def body(c_ref, a_ref, b_ref, o_ref):
        o_ref[...] = a_ref[...] + b_ref[...]

    grid_spec = pltpu.PrefetchScalarGridSpec(
        num_scalar_prefetch=1, grid=(4, r // tr),
        in_specs=[pl.BlockSpec((1, tr, n), lambda ch, i, c_ref: (2 * ch + c_ref[0], i, 0)),
                  pl.BlockSpec((1, tr, n), lambda ch, i, c_ref: (ch, i, 0))],
        out_specs=pl.BlockSpec((1, tr, n), lambda ch, i, c_ref: (ch, i, 0)))
    return pl.pallas_call(
        body, name=name, grid_spec=grid_spec, out_shape=jax.ShapeDtypeStruct((4, r, n), g.dtype),
        compiler_params=_params(dimension_semantics=("arbitrary", "arbitrary")),
    )(lax.axis_index("c").reshape(1).astype(jnp.int32), g, got)


def _rs_chips(p, name):
    def body(p_ref, out_ref, send_sems, recv_sems, local_sem):
        x, y, c = _me()
        my_chip = 2 * x + y
        mine = pltpu.make_async_copy(p_ref.at[my_chip], out_ref.at[my_chip], local_sem)
        mine.start()
        chips = [(1 - x, y), (x, 1 - y), (1 - x, 1 - y)]
        sends = [pltpu.make_async_remote_copy(
            src_ref=p_ref.at[2 * px + py], dst_ref=out_ref.at[my_chip], send_sem=send_sems.at[j],
            recv_sem=recv_sems.at[j], device_id=(px, py, c), device_id_type=MESH) for j, (px, py) in enumerate(chips)]
        for cp in sends:
            cp.start()
        for j, (px, py) in enumerate(chips):
            pltpu.make_async_remote_copy(
                src_ref=p_ref.at[my_chip], dst_ref=out_ref.at[2 * px + py], send_sem=send_sems.at[j],
                recv_sem=recv_sems.at[j], device_id=(px, py, c), device_id_type=MESH).wait_recv()
        for cp in sends:
            cp.wait_send()
        mine.wait()

    return pl.pallas_call(
        body, name=name, out_shape=jax.ShapeDtypeStruct(p.shape, p.dtype),
        in_specs=[ANY], out_specs=ANY,
        scratch_shapes=[pltpu.SemaphoreType.DMA((3,)), pltpu.SemaphoreType.DMA((3,)), pltpu.SemaphoreType.DMA],
        compiler_params=_params(),
    )(p)


HBM = pl.BlockSpec(memory_space=pltpu.HBM)
SEM = pl.BlockSpec(memory_space=pltpu.SEMAPHORE)
EFFECT = pltpu.SideEffectType.DATAFLOW_SIDE_EFFECTING


def _peers(x, y, c):
    return [(_flip(x, k >> 2 & 1), _flip(y, k >> 1 & 1), _flip(c, k & 1)) for k in range(1, N_DEV)]


def _exchange_start(src, land, gather, name):
    def body(src_ref, land_ref, send_sems, recv_sems, src_thru, land_thru, token):
        x, y, c = _me()
        me = 4 * x + 2 * y + c
        for j, (px, py, pc) in enumerate(_peers(x, y, c)):
            pltpu.make_async_remote_copy(
                src_ref=src_ref if gather else src_ref.at[4 * px + 2 * py + pc], dst_ref=land_ref.at[me],
                send_sem=send_sems.at[j], recv_sem=recv_sems.at[j], device_id=(px, py, pc), device_id_type=MESH).start()
        token[...] = jnp.zeros_like(token)

    sems = pltpu.SemaphoreType.DMA((N_DEV - 1,))
    out = pl.pallas_call(
        body, name=name,
        out_shape=(sems, sems, pltpu.HBM(src.shape, src.dtype), pltpu.HBM(land.shape, land.dtype),
                   jax.ShapeDtypeStruct((8, 128), F32)),
        in_specs=(HBM, HBM), out_specs=(SEM, SEM, HBM, HBM, pl.BlockSpec(memory_space=pltpu.VMEM)),
        input_output_aliases={0: 2, 1: 3},
        compiler_params=pltpu.CompilerParams(has_side_effects=EFFECT),
    )(pltpu.with_memory_space_constraint(src, pltpu.HBM), pltpu.with_memory_space_constraint(land, pltpu.HBM))
    return out[:4], out[4][0, 0]


def _exchange_wait(started, after, gather, name):
    send_sems, recv_sems, src_thru, land_thru = started

    def body(src_ref, land_ref, send_sems, recv_sems, after_ref, src_dead, got_ref):
        x, y, c = _me()
        for j, (px, py, pc) in enumerate(_peers(x, y, c)):
            pid = 4 * px + 2 * py + pc
            cp = pltpu.make_async_remote_copy(
                src_ref=src_ref if gather else src_ref.at[pid], dst_ref=land_ref.at[pid],
                send_sem=send_sems.at[j], recv_sem=recv_sems.at[j], device_id=(px, py, pc), device_id_type=MESH)
            cp.wait_send()
            cp.wait_recv()

    return pl.pallas_call(
        body, name=name,
        out_shape=(pltpu.HBM(src_thru.shape, src_thru.dtype), pltpu.HBM(land_thru.shape, land_thru.dtype)),
        in_specs=(HBM, HBM, SEM, SEM, ANY), out_specs=(HBM, HBM), input_output_aliases={0: 0, 1: 1},
        compiler_params=pltpu.CompilerParams(has_side_effects=EFFECT),
    )(src_thru, land_thru, send_sems, recv_sems, after)[1]


def _reduce_scatter(g, name):
    got = _rs_sibling(g, name + "_sib")
    return _rs_chips(_add_sibling(g, got, name + "_add"), name + "_chips")


def _silu(a):
    return a * _sig(a)


def _mod_piece(c_all, w_ada, b_piece):
    def body(c_ref, w_ref, b_ref, o_ref):
        o_ref[...] = _dot(_bf(_silu(c_ref[...])), _bf(w_ref[...])) + b_ref[...]

    return pl.pallas_call(
        body, name="mod_piece", out_shape=jax.ShapeDtypeStruct((c_all.shape[0], w_ada.shape[1]), F32),
        compiler_params=_params(),
    )(c_all, w_ada, b_piece)


def _gw_ada(c_all, dmod_piece):
    def body(c_ref, d_ref, o_ref):
        o_ref[...] = _dot_tn(_bf(_silu(c_ref[...])), _bf(d_ref[...]))

    return pl.pallas_call(
        body, name="gw_ada", out_shape=jax.ShapeDtypeStruct((c_all.shape[1], dmod_piece.shape[1]), F32),
        compiler_params=_params(),
    )(c_all, dmod_piece)


def _adam(parts, w, m, v, name):
    k, r, n = parts.shape
    if r <= 256 or r % 256 == 0:
        tr, tn = min(r, 256), n
    else:
        tr, tn = r, 256
    assert r % tr == 0 and n % tn == 0

    def body(p_ref, w_ref, m_ref, v_ref, g_ref, d_ref, nm_ref, nv_ref):
        g = p_ref[0].astype(F32)
        for j in range(1, k):
            g = g + p_ref[j].astype(F32)
        m_new = ADAM_B1 * m_ref[...] + (1.0 - ADAM_B1) * g
        v_new = ADAM_B2 * v_ref[...] + (1.0 - ADAM_B2) * jnp.square(g)
        m_hat = m_new / (1.0 - ADAM_B1 ** ADAM_STEP)
        v_hat = v_new / (1.0 - ADAM_B2 ** ADAM_STEP)
        g_ref[...] = g
        d_ref[...] = -ADAM_LR * (m_hat / (jnp.sqrt(v_hat) + ADAM_EPS) + ADAM_WD * w_ref[...])
        nm_ref[...] = m_new
        nv_ref[...] = v_new

    blk = pl.BlockSpec((tr, tn), lambda i, j: (i, j))
    return pl.pallas_call(
        body, name=name, grid=(r // tr, n // tn),
        in_specs=[pl.BlockSpec((k, tr, tn), lambda i, j: (0, i, j)), blk, blk, blk],
        out_specs=[blk, blk, blk, blk],
        out_shape=[jax.ShapeDtypeStruct((r, n), F32)] * 4,
        compiler_params=_params(dimension_semantics=("arbitrary", "arbitrary")),
    )(parts, w, m, v)


_SMALL = (("b_ada", 3 * D_MODEL), ("norm_w", D_MODEL), ("q_norm_w", HEAD_DIM), ("k_norm_w", HEAD_DIM),
          ("rel_bias", REL_BUCKETS * ATTN_HEADS), ("sinks", ATTN_HEADS), ("conv_b", XBC_W), ("dt_bias", SSM_HEADS),
          ("a_log", SSM_HEADS), ("d_skip", SSM_HEADS), ("ssm_norm_w", SSM_W))
_SMALL_N = sum(n for _, n in _SMALL)
_SMALL_PAD = -(-_SMALL_N // 128) * 128
_PACK_N = _SMALL_PAD + CONV_K * XBC_W


def _pack_small(d):
    parts = [d[name].reshape(1, n) for name, n in _SMALL]
    return jnp.concatenate(parts + [jnp.zeros((1, _SMALL_PAD - _SMALL_N), F32)], axis=1)


def _unpack_small(vec, shapes):
    out, off = {}, 0
    for name, n in _SMALL:
        out[name] = vec[:, off:off + n].reshape(shapes[name])
        off += n
    return out


WEIGHTS = ("w_ada", "b_ada", "norm_w", "w_in", "q_norm_w", "k_norm_w", "rel_bias", "sinks", "conv_w", "conv_b",
           "dt_bias", "a_log", "d_skip", "ssm_norm_w", "w_attn_proj", "w_ssm_proj", "w_out")


def kernel(x, c, w_ada, b_ada, norm_w, w_in, q_norm_w, k_norm_w, rel_bias, sinks, conv_w, conv_b, dt_bias, a_log, d_skip, ssm_norm_w, w_attn_proj, w_ssm_proj, w_out, loss_target, m_w_ada, m_b_ada, m_norm_w, m_w_in, m_q_norm_w, m_k_norm_w, m_rel_bias, m_sinks, m_conv_w, m_conv_b, m_dt_bias, m_a_log, m_d_skip, m_ssm_norm_w, m_w_attn_proj, m_w_ssm_proj, m_w_out, v_w_ada, v_b_ada, v_norm_w, v_w_in, v_q_norm_w, v_k_norm_w, v_rel_bias, v_sinks, v_conv_w, v_conv_b, v_dt_bias, v_a_log, v_d_skip, v_ssm_norm_w, v_w_attn_proj, v_w_ssm_proj, v_w_out):
    w = dict(w_ada=w_ada, b_ada=b_ada, norm_w=norm_w, w_in=w_in, q_norm_w=q_norm_w, k_norm_w=k_norm_w,
             rel_bias=rel_bias, sinks=sinks, conv_w=conv_w, conv_b=conv_b, dt_bias=dt_bias, a_log=a_log,
             d_skip=d_skip, ssm_norm_w=ssm_norm_w, w_attn_proj=w_attn_proj, w_ssm_proj=w_ssm_proj, w_out=w_out)
    m = dict(w_ada=m_w_ada, b_ada=m_b_ada, norm_w=m_norm_w, w_in=m_w_in, q_norm_w=m_q_norm_w, k_norm_w=m_k_norm_w,
             rel_bias=m_rel_bias, sinks=m_sinks, conv_w=m_conv_w, conv_b=m_conv_b, dt_bias=m_dt_bias, a_log=m_a_log,
             d_skip=m_d_skip, ssm_norm_w=m_ssm_norm_w, w_attn_proj=m_w_attn_proj, w_ssm_proj=m_w_ssm_proj, w_out=m_w_out)
    v = dict(w_ada=v_w_ada, b_ada=v_b_ada, norm_w=v_norm_w, w_in=v_w_in, q_norm_w=v_q_norm_w, k_norm_w=v_k_norm_w,
             rel_bias=v_rel_bias, sinks=v_sinks, conv_w=v_conv_w, conv_b=v_conv_b, dt_bias=v_dt_bias, a_log=v_a_log,
             d_skip=v_d_skip, ssm_norm_w=v_ssm_norm_w, w_attn_proj=v_w_attn_proj, w_ssm_proj=v_w_ssm_proj, w_out=v_w_out)
    me = 4 * lax.axis_index("x") + 2 * lax.axis_index("y") + lax.axis_index("c")
    ada_n = w_ada.shape[2]
    in_n = w_in.shape[2]
    cw_n = conv_w.shape[2]

    first = _ag_direct(jnp.concatenate([c, conv_w[0].reshape(1, CONV_K * cw_n)], axis=1), "ag_c")[:, 0]
    c_all = first[:, :D_MODEL]
    conv_w_full = first[:, D_MODEL:].reshape(N_DEV, CONV_K, cw_n).transpose(1, 0, 2).reshape(CONV_K, XBC_W)
    b_piece = lax.dynamic_slice_in_dim(b_ada, me * ada_n, ada_n, axis=1)
    mod_all = _ag_direct(_mod_piece(c_all, w_ada[0], b_piece), "ag_mod")
    mod = lax.dynamic_index_in_dim(mod_all, me, axis=1, keepdims=False).reshape(1, 3 * D_MODEL)
    shift, scale, gate = mod[:, :D_MODEL], mod[:, D_MODEL:2 * D_MODEL], mod[:, 2 * D_MODEL:]

    w_t, zero = _ag_two_level(w_in[0].T.astype(BF), "ag_w_in")
    w_t = w_t.reshape(N_DEV * in_n, D_MODEL)

    def with_mine(blocks, mine):
        return lax.dynamic_update_index_in_dim(jnp.zeros(blocks, mine.dtype), mine, me, axis=0)

    rows = jnp.concatenate([w_attn_proj[0], w_ssm_proj[0], w_out[0]], axis=0).astype(BF) + zero
    r_ap, r_sp = w_attn_proj.shape[1], w_ssm_proj.shape[1]
    rows_started, zero = _exchange_start(rows, with_mine((N_DEV,) + rows.shape, rows), True, "ag_rows_start")

    def rows_fn(after):
        rows_all = _exchange_wait(rows_started, after, True, "ag_rows_wait")
        return (rows_all[:, :r_ap].reshape(ATTN_W, D_MODEL), rows_all[:, r_ap:r_ap + r_sp].reshape(SSM_W, D_MODEL),
                rows_all[:, r_ap + r_sp:].reshape(D_MODEL, D_MODEL))

    started = {}

    def send_blocks(key, g, name):
        g = g.astype(BF)
        started[key], zero = _exchange_start(
            g, with_mine(g.shape, lax.dynamic_index_in_dim(g, me, axis=0, keepdims=False)), False, name)
        return zero

    def after_mid(g_wap, g_wsp, g_wout):
        return send_blocks("rows", jnp.concatenate(
            [g_wap.reshape(N_DEV, r_ap, D_MODEL), g_wsp.reshape(N_DEV, r_sp, D_MODEL),
             g_wout.reshape(N_DEV, r_ap, D_MODEL)], axis=1), "rs_rows_start")

    def after_gw(g_ws):
        return send_blocks("in", jnp.concatenate(g_ws, axis=0).reshape(N_DEV, in_n, D_MODEL), "rs_in_start")

    r = _local_step(x[0], loss_target[0], shift, scale + zero, gate, w_t, rows_fn, norm_w, q_norm_w, k_norm_w,
                    rel_bias, sinks, conv_w_full, conv_b, dt_bias, a_log, d_skip, ssm_norm_w, after_mid, after_gw)

    loss = lax.psum(r["loss"][0, 0], ("x", "y", "c"))

    small = dict(b_ada=r["dmod"], norm_w=r["g_norm_w"], q_norm_w=r["g_qnw"], k_norm_w=r["g_knw"], rel_bias=r["g_rel"],
                 sinks=r["g_sinks"], conv_b=r["g_conv_b"], dt_bias=r["g_dt_bias"], a_log=r["g_a_log"],
                 d_skip=r["g_d_skip"], ssm_norm_w=r["g_ssm_nw"])
    pack = jnp.concatenate([_pack_small(small), r["g_conv_w"].reshape(1, CONV_K * XBC_W)], axis=1)
    pack_all = _ag_direct(pack, "ag_small")
    shapes = {name: w[name].shape for name, _ in _SMALL}
    res = {}
    g_s, d_s, m_s, v_s = _adam(pack_all[:, :, :_SMALL_PAD], _pack_small(w), _pack_small(m), _pack_small(v), "adam_small")
    for name, arr in _unpack_small(g_s, shapes).items():
        res[name] = [arr]
    for vec in (d_s, m_s, v_s):
        for name, arr in _unpack_small(vec, shapes).items():
            res[name].append(arr)
    cw_parts = pack_all[:, 0, _SMALL_PAD:].reshape(N_DEV, CONV_K, XBC_W)
    cw_mine = lax.dynamic_slice_in_dim(cw_parts, me * cw_n, cw_n, axis=2)
    res["conv_w"] = [a[None] for a in _adam(cw_mine, conv_w[0], m_conv_w[0], v_conv_w[0], "adam_conv_w")]

    dmod_piece = lax.dynamic_slice_in_dim(pack_all[:, 0, :3 * D_MODEL], me * ada_n, ada_n, axis=1)
    g_ada = _gw_ada(c_all, dmod_piece)
    res["w_ada"] = [a[None] for a in _adam(g_ada[None], w_ada[0], m_w_ada[0], v_w_ada[0], "adam_w_ada")]

    cat = lambda d: jnp.concatenate([d["w_attn_proj"][0], d["w_ssm_proj"][0], d["w_out"][0]], axis=0)
    rows_res = _adam(_exchange_wait(started["rows"], g_ada, False, "rs_rows_wait"), cat(w), cat(m), cat(v), "adam_w_rows")
    res["w_in"] = [a.T[None] for a in _adam(_exchange_wait(started["in"], rows_res[0], False, "rs_in_wait"),
                                            w_in[0].T, m_w_in[0].T, v_w_in[0].T, "adam_w_in")]
    res["w_attn_proj"] = [a[None, :r_ap] for a in rows_res]
    res["w_ssm_proj"] = [a[None, r_ap:r_ap + r_sp] for a in rows_res]
    res["w_out"] = [a[None, r_ap + r_sp:] for a in rows_res]

    outs = [loss, r["grad_x"][None]]
    for j in range(4):
        outs += [res[name][j] for name in WEIGHTS]
    return tuple(outs)
```

```python
import functools
import math

import numpy as np
import jax
import jax.numpy as jnp
from jax import lax
from jax.experimental import pallas as pl
from jax.experimental.pallas import tpu as pltpu

F32 = jnp.float32
BF = jnp.bfloat16
HI = lax.Precision.HIGHEST

D_MODEL = 1024
ATTN_HEADS = 16
KV_HEADS = 4
GRP = ATTN_HEADS // KV_HEADS
HEAD_DIM = 64
ATTN_W = ATTN_HEADS * HEAD_DIM
KV_W = KV_HEADS * HEAD_DIM
BLOCK = 128
REL_BUCKETS = 32
REL_MAX_DIST = 128
SSM_W = 2048
SSM_P = 64
SSM_HEADS = 32
SSM_G = 4
SSM_R = 8
SSM_N = 128
CONV_K = 4
XBC_W = SSM_W + 2 * SSM_G * SSM_N
SEG_W = (ATTN_W, KV_W, KV_W, ATTN_W, SSM_W, XBC_W, SSM_HEADS, D_MODEL, D_MODEL)
SEG_OFF = tuple(int(v) for v in np.cumsum((0,) + SEG_W))
IN_W = SEG_OFF[-1]
GATE_SEGS = (3, 4, 7, 8)
EPS = 1e-6
N_DEV = 8
ADAM_LR, ADAM_B1, ADAM_B2, ADAM_EPS, ADAM_WD, ADAM_STEP = 0.001, 0.9, 0.999, 1e-08, 0.01, 10
VMEM_LIMIT = 60 * 1024 * 1024
MESH = pl.DeviceIdType.MESH
ANY = pl.BlockSpec(memory_space=pl.ANY)


def _dot(a, b, precision=None):
    return jnp.dot(a, b, preferred_element_type=F32, precision=precision)


def _dot_nt(a, b, precision=None):
    return lax.dot_general(a, b, (((1,), (1,)), ((), ())), preferred_element_type=F32, precision=precision)


def _dot_tn(a, b, precision=None):
    return lax.dot_general(a, b, (((0,), (0,)), ((), ())), preferred_element_type=F32, precision=precision)


def _bf(a):
    return a.astype(BF)


def _sig(a):
    return 0.5 * jnp.tanh(0.5 * a) + 0.5


def _params(**kw):
    return pltpu.CompilerParams(vmem_limit_bytes=VMEM_LIMIT, **kw)


def _full(shape):
    nd = len(shape)
    return pl.BlockSpec(shape, lambda i: (0,) * nd)


def _rows(tm, w):
    return pl.BlockSpec((tm, w), lambda i: (i, 0))


def _inproj(x, norm_w, scale, shift, w_t, tm=256):
    s = x.shape[0]

    def body(x_ref, nw_ref, sc_ref, sh_ref, w_hbm, *rest):
        outs, h_ref, w_vm, sem = rest[:9], rest[9], rest[10], rest[11]

        @pl.when(pl.program_id(0) == 0)
        def _():
            cp = pltpu.make_async_copy(w_hbm, w_vm, sem)
            cp.start()
            cp.wait()

        xv = x_ref[...]
        r = lax.rsqrt(jnp.mean(xv * xv, axis=-1, keepdims=True) + EPS)
        h = xv * r * (nw_ref[...] * (1.0 + sc_ref[...])) + sh_ref[...]
        hb = _bf(h)
        h_ref[...] = hb
        for j in range(9):
            outs[j][...] = _dot_nt(hb, w_vm[SEG_OFF[j]:SEG_OFF[j + 1], :]).astype(outs[j].dtype)

    vec = _full((1, D_MODEL))
    return pl.pallas_call(
        body, name="inproj", grid=(s // tm,),
        in_specs=[_rows(tm, D_MODEL), vec, vec, vec, ANY],
        out_specs=[_rows(tm, w) for w in SEG_W] + [_rows(tm, D_MODEL)],
        out_shape=[jax.ShapeDtypeStruct((s, w), BF if j in GATE_SEGS else F32) for j, w in enumerate(SEG_W)]
                  + [jax.ShapeDtypeStruct((s, D_MODEL), BF)],
        scratch_shapes=[pltpu.VMEM((IN_W, D_MODEL), BF), pltpu.SemaphoreType.DMA],
        compiler_params=_params(dimension_semantics=("arbitrary",)),
    )(x, norm_w, scale, shift, w_t)


def _bucket_onehot_t():
    qi = jnp.arange(BLOCK)[:, None]
    kj = jnp.arange(2 * BLOCK)[None, :]
    dist = qi + BLOCK - kj
    n = jnp.maximum(dist, 0)
    max_exact = REL_BUCKETS // 2
    nf = jnp.maximum(n, 1).astype(F32)
    large = max_exact + (jnp.log(nf / max_exact) / math.log(REL_MAX_DIST / max_exact)
                         * (REL_BUCKETS - max_exact)).astype(jnp.int32)
    large = jnp.minimum(large, REL_BUCKETS - 1)
    bucket = jnp.where(n < max_exact, n, large).reshape(1, BLOCK * 2 * BLOCK)
    return (bucket == jnp.arange(REL_BUCKETS)[:, None]).astype(F32)


def _bias_dense(rel_bias_t, oh_t):
    def body(rb_ref, oh_ref, o_ref):
        o_ref[...] = _dot(rb_ref[...], oh_ref[...], HI)

    return pl.pallas_call(
        body, name="bias_dense", out_shape=jax.ShapeDtypeStruct((ATTN_HEADS, BLOCK * 2 * BLOCK), F32),
        compiler_params=_params(),
    )(rel_bias_t, oh_t)


def _bias_grad(ds_sum, oh_t):
    def body(ds_ref, oh_ref, o_ref):
        o_ref[...] = _dot_nt(ds_ref[...], oh_ref[...], HI)

    return pl.pallas_call(
        body, name="bias_grad", out_shape=jax.ShapeDtypeStruct((ATTN_HEADS, REL_BUCKETS), F32),
        compiler_params=_params(),
    )(ds_sum, oh_t)


def _group_sum(a, e):
    hi = _bf(a)
    return _dot(hi, e) + _dot(_bf(a - hi.astype(F32)), e)


def _group_bcast(a, e3t):
    hi = _bf(a)
    r1 = a - hi.astype(F32)
    mid = _bf(r1)
    return _dot(jnp.concatenate([hi, mid, _bf(r1 - mid.astype(F32))], axis=1), e3t)


def _membership(width, group, ngroups):
    e = (jnp.arange(width)[:, None] // group == jnp.arange(ngroups)[None, :]).astype(BF)
    return e, jnp.tile(e.T, (3, 1))


def _fold(width, group):
    return (jnp.arange(width)[:, None] % group == jnp.arange(group)[None, :]).astype(BF)


def _heads_norm(t, w_x, e, e3t):
    r = lax.rsqrt(_group_sum(t * t, e) * (1.0 / HEAD_DIM) + EPS)
    r_x = _group_bcast(r, e3t)
    return t * r_x * w_x, r_x


def _heads_norm_bwd(t, r_x, w_x, d, e, e3t):
    wd = d * w_x
    corr = _group_bcast(_group_sum(t * wd, e) * (1.0 / HEAD_DIM), e3t)
    return r_x * wd - t * (r_x * r_x * r_x) * corr, jnp.sum(d * t * r_x, axis=0, keepdims=True)


def _stack_heads(a, hk):
    return jnp.concatenate([a[:, (hk * GRP + g) * HEAD_DIM:(hk * GRP + g + 1) * HEAD_DIM] for g in range(GRP)], axis=0)


def _stack_cols(a, hk):
    return jnp.concatenate([a[:, hk * GRP + g:hk * GRP + g + 1] for g in range(GRP)], axis=0)


def _window_mask(first):
    qi = jnp.bitwise_and(lax.broadcasted_iota(jnp.int32, (GRP * BLOCK, 2 * BLOCK), 0), BLOCK - 1)
    kj = lax.broadcasted_iota(jnp.int32, (GRP * BLOCK, 2 * BLOCK), 1)
    prev_ok = jnp.logical_and(kj > qi, jnp.logical_not(first))
    cur_ok = jnp.logical_and(kj >= BLOCK, kj - BLOCK <= qi)
    return jnp.logical_or(jnp.logical_and(kj < BLOCK, prev_ok), cur_ok)


def _attn_consts(qnw, knw):
    eq, eq3t = _membership(ATTN_W, HEAD_DIM, ATTN_HEADS)
    ek, ek3t = _membership(KV_W, HEAD_DIM, ATTN_HEADS)
    return (jnp.tile(qnw, (1, ATTN_HEADS)), jnp.tile(knw, (1, KV_HEADS)), eq, eq3t, ek, ek3t)


def _attn_fwd(q, k, v, bias, sinks, consts):
    s = q.shape[0]
    nb = s // BLOCK
    gq = GRP * BLOCK
    bias_t = bias.reshape(KV_HEADS, GRP, BLOCK, 2 * BLOCK).transpose(0, 3, 1, 2).reshape(KV_HEADS, 2 * BLOCK, gq)
    sink_rows = jnp.repeat(sinks.reshape(KV_HEADS, GRP), BLOCK, axis=1).reshape(KV_HEADS, 1, gq)
    eye = jnp.eye(BLOCK, dtype=BF)

    def body(q_ref, kp_ref, kc_ref, vp_ref, vc_ref, b_ref, bt_ref, sk_ref, skr_ref, eye_ref,
             qw_ref, kw_ref, eq_ref, eq3_ref, ek_ref, ek3_ref, o_ref, lse_ref):
        i = pl.program_id(0)
        mask = _window_mask(i == 0)
        kj = lax.broadcasted_iota(jnp.int32, (2 * BLOCK, gq), 0)
        qi = jnp.bitwise_and(lax.broadcasted_iota(jnp.int32, (2 * BLOCK, gq), 1), BLOCK - 1)
        mask_t = jnp.logical_or(jnp.logical_and(kj < BLOCK, jnp.logical_and(kj > qi, i > 0)),
                                jnp.logical_and(kj >= BLOCK, kj - BLOCK <= qi))
        qn = _bf(_heads_norm(q_ref[...], qw_ref[...], eq_ref[...], eq3_ref[...])[0])
        kn = _bf(_heads_norm(jnp.concatenate([kp_ref[...], kc_ref[...]], axis=0), kw_ref[...], ek_ref[...], ek3_ref[...])[0])
        vv = _bf(jnp.concatenate([vp_ref[...], vc_ref[...]], axis=0))
        ones = jnp.ones((2 * BLOCK, HEAD_DIM), BF)
        lses = []
        for hk in range(KV_HEADS):
            ks = slice(hk * HEAD_DIM, (hk + 1) * HEAD_DIM)
            qg = _stack_heads(qn, hk)
            sc_t = jnp.where(mask_t, _dot_nt(kn[:, ks], qg) * (HEAD_DIM ** -0.5) + bt_ref[hk], -1e30)
            m_row = jnp.maximum(jnp.max(sc_t, axis=0, keepdims=True), skr_ref[hk])
            m8 = _bf(jnp.broadcast_to(m_row, (8, gq)))
            m = jnp.concatenate([_dot_nt(eye_ref[...], m8[:, g * BLOCK:(g + 1) * BLOCK])[:, 0:1] for g in range(GRP)], axis=0)
            sc = _dot_nt(qg, kn[:, ks]) * (HEAD_DIM ** -0.5)
            sc = sc + b_ref[hk * GRP:(hk + 1) * GRP].reshape(gq, 2 * BLOCK)
            p = _bf(jnp.exp(jnp.where(mask, sc, -1e30) - m))
            sink = jnp.concatenate([jnp.full((BLOCK, 1), sk_ref[0, hk * GRP + g], F32) for g in range(GRP)], axis=0)
            pv = _dot(p, jnp.concatenate([vv[:, ks], ones], axis=1))
            den = pv[:, HEAD_DIM:HEAD_DIM + 1] + jnp.exp(sink - m)
            out = pv[:, :HEAD_DIM] * (1.0 / den)
            lse = m + jnp.log(den)
            for g in range(GRP):
                h = hk * GRP + g
                o_ref[:, h * HEAD_DIM:(h + 1) * HEAD_DIM] = out[g * BLOCK:(g + 1) * BLOCK]
                lses.append(lse[g * BLOCK:(g + 1) * BLOCK])
        lse_ref[...] = jnp.concatenate(lses, axis=1)

    cur = lambda w: pl.BlockSpec((BLOCK, w), lambda i: (i, 0))
    prev = lambda w: pl.BlockSpec((BLOCK, w), lambda i: (jnp.maximum(i - 1, 0), 0))
    whole = lambda a: pl.BlockSpec(a.shape, lambda i: (0,) * a.ndim)
    return pl.pallas_call(
        body, name="attn_fwd", grid=(nb,),
        in_specs=[cur(ATTN_W), prev(KV_W), cur(KV_W), prev(KV_W), cur(KV_W), whole(bias), whole(bias_t),
                  pl.BlockSpec(memory_space=pltpu.SMEM), whole(sink_rows), whole(eye)] + [_full(c.shape) for c in consts],
        out_specs=[cur(ATTN_W), cur(ATTN_HEADS)],
        out_shape=[jax.ShapeDtypeStruct((s, ATTN_W), F32), jax.ShapeDtypeStruct((s, ATTN_HEADS), F32)],
        compiler_params=_params(dimension_semantics=("arbitrary",)),
    )(q, k, k, v, v, bias, bias_t, sinks, sink_rows, eye, *consts)


def _conv_taps(xbc, tail):
    ext = jnp.concatenate([tail, xbc], axis=0)
    return [pltpu.roll(ext, CONV_K - 1 - j, axis=0)[8:8 + BLOCK] if j < CONV_K - 1 else xbc for j in range(CONV_K)]


def _softplus(u):
    return jnp.maximum(u, 0.0) + jnp.log(1.0 + jnp.exp(-jnp.abs(u)))


def _tril():
    r = lax.broadcasted_iota(jnp.int32, (BLOCK, BLOCK), 0)
    c = lax.broadcasted_iota(jnp.int32, (BLOCK, BLOCK), 1)
    return r >= c


def _triu():
    r = lax.broadcasted_iota(jnp.int32, (BLOCK, BLOCK), 0)
    c = lax.broadcasted_iota(jnp.int32, (BLOCK, BLOCK), 1)
    return r <= c


def _exact_left(m01, a):
    hi = _bf(a)
    r1 = a - hi.astype(F32)
    mid = _bf(r1)
    return _dot(m01, hi) + _dot(m01, mid) + _dot(m01, _bf(r1 - mid.astype(F32)))


def _ssd_common(conv, dtr_ref, dtb_ref, alog_ref, e3_ref):
    sg = _sig(conv)
    xact = conv * sg
    u = dtr_ref[...] + dtb_ref[...]
    dt = _softplus(u)
    a = -jnp.exp(alog_ref[...])
    trilb = _tril()
    acum = _exact_left(trilb.astype(BF), dt * a)
    both = _group_bcast(jnp.concatenate([dt, acum], axis=0), e3_ref[...])
    dt_x, acum_x = both[:BLOCK], both[BLOCK:]
    return sg, xact, u, dt, a, trilb, acum, dt_x, acum_x


def _ssd_fwd(xbc, dt_raw, conv_w, conv_b, dt_bias, a_log, dsk_x, e3t):
    s = xbc.shape[0]
    nc = s // BLOCK

    def body(x_ref, tail_ref, dtr_ref, cw_ref, cb_ref, dtb_ref, alog_ref, dsk_ref, e3_ref,
             y_ref, hp_ref, conv_ref, hst):
        i = pl.program_id(0)

        @pl.when(i == 0)
        def _():
            hst[...] = jnp.zeros_like(hst)

        tail = jnp.where(i > 0, tail_ref[...], 0.0)
        taps = _conv_taps(x_ref[...], tail)
        conv = cb_ref[...] + sum(taps[j] * cw_ref[j:j + 1, :] for j in range(CONV_K))
        conv_ref[...] = conv
        _, xact, _, _, _, trilb, acum, dt_x, acum_x = _ssd_common(conv, dtr_ref, dtb_ref, alog_ref, e3_ref)
        xs = xact[:, :SSM_W]
        acum_t = acum.T
        ea_x = jnp.exp(acum_x)
        last_x = acum_x[BLOCK - 1:BLOCK, :]
        xdt = xs * dt_x
        xw = xdt * jnp.exp(last_x - acum_x)
        cd_x = jnp.exp(last_x)
        hprev = hst[...]
        hp_ref[0] = hprev
        dsk = dsk_ref[...]
        for g in range(SSM_G):
            bg = _bf(xact[:, SSM_W + g * SSM_N:SSM_W + (g + 1) * SSM_N])
            cg = _bf(xact[:, SSM_W + SSM_G * SSM_N + g * SSM_N:SSM_W + SSM_G * SSM_N + (g + 1) * SSM_N])
            sl = slice(g * SSM_R * SSM_P, (g + 1) * SSM_R * SSM_P)
            cb = _dot_nt(cg, bg)
            yoff = _dot(cg, _bf(hprev[:, sl])) * ea_x[:, sl]
            hst[:, sl] = hprev[:, sl] * cd_x[:, sl] + _dot_tn(bg, _bf(xw[:, sl]))
            for r in range(SSM_R):
                hh = g * SSM_R + r
                hs = slice(hh * SSM_P, (hh + 1) * SSM_P)
                seg = jnp.where(trilb, acum[:, hh:hh + 1] - acum_t[hh:hh + 1, :], -1e30)
                mm = cb * jnp.exp(seg)
                yd = _dot(_bf(mm), _bf(xdt[:, hs]))
                y_ref[:, hs] = yd + yoff[:, r * SSM_P:(r + 1) * SSM_P] + dsk[:, hs] * xs[:, hs]

    chunk = lambda w: pl.BlockSpec((BLOCK, w), lambda i: (i, 0))
    return pl.pallas_call(
        body, name="ssd_fwd", grid=(nc,),
        in_specs=[chunk(XBC_W), pl.BlockSpec((8, XBC_W), lambda i: (jnp.maximum(i * (BLOCK // 8) - 1, 0), 0)),
                  chunk(SSM_HEADS), _full((CONV_K, XBC_W)), _full((1, XBC_W)), _full((1, SSM_HEADS)),
                  _full((1, SSM_HEADS)), _full((1, SSM_W)), _full((3 * SSM_HEADS, SSM_W))],
        out_specs=[chunk(SSM_W), pl.BlockSpec((1, SSM_N, SSM_W), lambda i: (i, 0, 0)), chunk(XBC_W)],
        out_shape=[jax.ShapeDtypeStruct((s, SSM_W), F32), jax.ShapeDtypeStruct((nc, SSM_N, SSM_W), F32),
                   jax.ShapeDtypeStruct((s, XBC_W), F32)],
        scratch_shapes=[pltpu.VMEM((SSM_N, SSM_W), F32)],
        compiler_params=_params(dimension_semantics=("arbitrary",)),
    )(xbc, xbc, dt_raw, conv_w, conv_b, dt_bias, a_log, dsk_x, e3t)


def _dsilu(z, sg):
    return sg * (1.0 + z * (1.0 - sg))


def _mid(x, tgt, o_att, za, ypre, zm, ga, gb, gate, ssm_nw, wap, wsp, wout, tm=256):
    s = x.shape[0]
    gw = SSM_W // SSM_G

    def body(x_ref, t_ref, o_ref, za_ref, yp_ref, zm_ref, ga_ref, gb_ref, gate_ref, nw_ref, wap_h, wsp_h, wout_h,
             dout_ref, do_ref, dza_ref, dyp_ref, dzm_ref, dga_ref, dgb_ref,
             yag_ref, dya_ref, yn_ref, dyb_ref, mg_ref, dob_ref, gnw_ref, dgate_ref, loss_ref,
             wap_v, wsp_v, wout_v, sem):
        i = pl.program_id(0)

        @pl.when(i == 0)
        def _():
            cps = [pltpu.make_async_copy(a, b, sem.at[j])
                   for j, (a, b) in enumerate(((wap_h, wap_v), (wsp_h, wsp_v), (wout_h, wout_v)))]
            for cp in cps:
                cp.start()
            gnw_ref[...] = jnp.zeros_like(gnw_ref)
            dgate_ref[...] = jnp.zeros_like(dgate_ref)
            loss_ref[...] = jnp.zeros_like(loss_ref)
            for cp in cps:
                cp.wait()

        gate = gate_ref[...]
        nw = nw_ref[...]
        o_att = o_ref[...]
        z_a = za_ref[...].astype(F32)
        s_a = _sig(z_a)
        silu_a = z_a * s_a
        yag = _bf(o_att * silu_a)
        yag_ref[...] = yag
        y_a = _dot(yag, wap_v[...])
        ypre = yp_ref[...]
        z_m = zm_ref[...].astype(F32)
        s_m = _sig(z_m)
        silu_m = z_m * s_m
        yg = ypre * silu_m
        rinv = jnp.concatenate(
            [jnp.broadcast_to(lax.rsqrt(jnp.mean(yg[:, g * gw:(g + 1) * gw] ** 2, axis=-1, keepdims=True) + EPS), (tm, gw))
             for g in range(SSM_G)], axis=1)
        ynr = yg * rinv
        yn = _bf(ynr * nw)
        yn_ref[...] = yn
        y_b = _dot(yn, wsp_v[...])
        g_a = _sig(ga_ref[...].astype(F32))
        g_b = _sig(gb_ref[...].astype(F32))
        merged = _bf(g_a * y_a + g_b * y_b)
        mg_ref[...] = merged
        o = _dot(merged, wout_v[...])
        diff = x_ref[...] + gate * o - t_ref[...]
        loss_ref[...] += (0.5 / D_MODEL) * jnp.sum(diff * diff, axis=(0, 1), keepdims=True)
        dout = diff * (1.0 / D_MODEL)
        dout_ref[...] = dout
        dgate_ref[...] += jnp.sum(dout * o, axis=0, keepdims=True)
        d_o = _bf(dout * gate)
        dob_ref[...] = d_o
        dmerged = _dot_nt(d_o, wout_v[...])
        dy_a = dmerged * g_a
        dy_b = dmerged * g_b
        dga_ref[...] = _bf(dy_a * y_a * (1.0 - g_a))
        dgb_ref[...] = _bf(dy_b * y_b * (1.0 - g_b))
        dy_a = _bf(dy_a)
        dy_b = _bf(dy_b)
        dya_ref[...] = dy_a
        dyb_ref[...] = dy_b
        dyag = _dot_nt(dy_a, wap_v[...])
        do_ref[...] = dyag * silu_a
        dza_ref[...] = _bf(dyag * o_att * _dsilu(z_a, s_a))
        dyn = _dot_nt(dy_b, wsp_v[...])
        gnw_ref[...] += jnp.sum(dyn * ynr, axis=0, keepdims=True)
        dynw = dyn * nw
        corr = jnp.concatenate(
            [jnp.broadcast_to(jnp.mean((dynw * ynr)[:, g * gw:(g + 1) * gw], axis=-1, keepdims=True), (tm, gw))
             for g in range(SSM_G)], axis=1)
        dyg = rinv * (dynw - ynr * corr)
        dyp_ref[...] = dyg * silu_m
        dzm_ref[...] = _bf(dyg * ypre * _dsilu(z_m, s_m))

    r1, r2 = _rows(tm, D_MODEL), _rows(tm, SSM_W)
    sd = jax.ShapeDtypeStruct
    return pl.pallas_call(
        body, name="mid", grid=(s // tm,),
        in_specs=[r1, r1, r1, r1, r2, r2, r1, r1, _full((1, D_MODEL)), _full((1, SSM_W)), ANY, ANY, ANY],
        out_specs=[r1, r1, r1, r2, r2, r1, r1, r1, r1, r2, r1, r1, r1,
                   _full((1, SSM_W)), _full((1, D_MODEL)), _full((1, 1))],
        out_shape=[sd((s, D_MODEL), F32), sd((s, ATTN_W), F32), sd((s, ATTN_W), BF), sd((s, SSM_W), F32),
                   sd((s, SSM_W), BF), sd((s, D_MODEL), BF), sd((s, D_MODEL), BF),
                   sd((s, ATTN_W), BF), sd((s, D_MODEL), BF), sd((s, SSM_W), BF), sd((s, D_MODEL), BF),
                   sd((s, D_MODEL), BF), sd((s, D_MODEL), BF),
                   sd((1, SSM_W), F32), sd((1, D_MODEL), F32), sd((1, 1), F32)],
        scratch_shapes=[pltpu.VMEM((ATTN_W, D_MODEL), BF), pltpu.VMEM((SSM_W, D_MODEL), BF), pltpu.VMEM((D_MODEL, D_MODEL), BF),
                        pltpu.SemaphoreType.DMA((3,))],
        compiler_params=_params(dimension_semantics=("arbitrary",)),
    )(x, tgt, o_att, za, ypre, zm, ga, gb, gate, ssm_nw, wap, wsp, wout)


def _attn_bwd(q, k, v, bias, sinks, consts, o_att, lse, d_o):
    s = q.shape[0]
    nb = s // BLOCK
    folds = (_fold(ATTN_W, HEAD_DIM), _fold(KV_W, HEAD_DIM))

    def body(q_ref, kp_ref, kc_ref, vp_ref, vc_ref, b_ref, skv_ref, qw_ref, kw_ref, eq_ref, eq3_ref, ek_ref, ek3_ref,
             fq_ref, fk_ref, o_ref, lse_ref, do_ref,
             dq_ref, dk_ref, dv_ref, dss_ref, gqw_ref, gkw_ref, gsk_ref, ckn, cv, dqn_s, dkn_s, dv_s, gq_x, gk_x):
        i = pl.program_id(0)
        kw, ek, ek3 = kw_ref[...], ek_ref[...], ek3_ref[...]

        @pl.when(i == 0)
        def _():
            for ref in (ckn, cv, dss_ref, gq_x, gk_x, gsk_ref):
                ref[...] = jnp.zeros_like(ref)

        @pl.when(i < nb)
        def _():
            mask = _window_mask(i == 0)
            qw, eq, eq3 = qw_ref[...], eq_ref[...], eq3_ref[...]
            qf = q_ref[...]
            qnf, rq_x = _heads_norm(qf, qw, eq, eq3)
            qn = _bf(qnf)
            kf = jnp.concatenate([kp_ref[...], kc_ref[...]], axis=0)
            knf, rk_x = _heads_norm(kf, kw, ek, ek3)
            kn = _bf(knf)
            vv = _bf(jnp.concatenate([vp_ref[...], vc_ref[...]], axis=0))
            d_of = do_ref[...]
            d_ob = _bf(d_of)
            lse_all = lse_ref[...]
            delta = _group_sum(d_of * o_ref[...], eq)
            gsk_ref[...] += jnp.sum(-jnp.exp(skv_ref[...] - lse_all) * delta, axis=0, keepdims=True)
            for hk in range(KV_HEADS):
                ks = slice(hk * HEAD_DIM, (hk + 1) * HEAD_DIM)
                qg = _stack_heads(qn, hk)
                sc = _dot_nt(qg, kn[:, ks]) * (HEAD_DIM ** -0.5)
                sc = sc + b_ref[hk * GRP:(hk + 1) * GRP].reshape(GRP * BLOCK, 2 * BLOCK)
                p = jnp.where(mask, jnp.exp(sc - _stack_cols(lse_all, hk)), 0.0)
                d_og = _stack_heads(d_ob, hk)
                ds = p * (_dot_nt(d_og, vv[:, ks]) - _stack_cols(delta, hk))
                dss_ref[hk * GRP:(hk + 1) * GRP] += ds.reshape(GRP, BLOCK, 2 * BLOCK)
                dsb = _bf(ds)
                dv_s[:, ks] = _dot_tn(_bf(p), d_og)
                dkn_s[:, ks] = _dot_tn(dsb, qg) * (HEAD_DIM ** -0.5)
                dqn = _dot(dsb, kn[:, ks]) * (HEAD_DIM ** -0.5)
                for g in range(GRP):
                    h = hk * GRP + g
                    dqn_s[:, h * HEAD_DIM:(h + 1) * HEAD_DIM] = dqn[g * BLOCK:(g + 1) * BLOCK]
            dq, gq = _heads_norm_bwd(qf, rq_x, qw, dqn_s[...], eq, eq3)
            dq_ref[...] = _bf(dq)
            gq_x[...] += gq
            dk, gk = _heads_norm_bwd(kf[:BLOCK], rk_x[:BLOCK], kw, ckn[...] + dkn_s[0:BLOCK, :], ek, ek3)
            dk_ref[...] = _bf(dk)
            gk_x[...] += gk
            dv_ref[...] = _bf(cv[...] + dv_s[0:BLOCK, :])
            ckn[...] = dkn_s[BLOCK:2 * BLOCK, :]
            cv[...] = dv_s[BLOCK:2 * BLOCK, :]

        @pl.when(i == nb)
        def _():
            kc = kc_ref[...]
            dk, gk = _heads_norm_bwd(kc, _heads_norm(kc, kw, ek, ek3)[1], kw, ckn[...], ek, ek3)
            dk_ref[...] = _bf(dk)
            dv_ref[...] = _bf(cv[...])
            gqw_ref[...] = _group_sum(jnp.broadcast_to(gq_x[...], (8, ATTN_W)), fq_ref[...])[0:1]
            gkw_ref[...] = _group_sum(jnp.broadcast_to(gk_x[...] + gk, (8, KV_W)), fk_ref[...])[0:1]

    last = nb - 1
    cur = lambda w: pl.BlockSpec((BLOCK, w), lambda i: (jnp.minimum(i, last), 0))
    prev = lambda w: pl.BlockSpec((BLOCK, w), lambda i: (jnp.maximum(jnp.minimum(i, last) - 1, 0), 0))
    late = lambda w: pl.BlockSpec((BLOCK, w), lambda i: (jnp.maximum(i - 1, 0), 0))
    sd = jax.ShapeDtypeStruct
    return pl.pallas_call(
        body, name="attn_bwd", grid=(nb + 1,),
        in_specs=[cur(ATTN_W), prev(KV_W), cur(KV_W), prev(KV_W), cur(KV_W),
                  pl.BlockSpec((ATTN_HEADS, BLOCK, 2 * BLOCK), lambda i: (0, 0, 0)), _full((1, ATTN_HEADS))]
                 + [_full(c.shape) for c in consts + folds] + [cur(ATTN_W), cur(ATTN_HEADS), cur(ATTN_W)],
        out_specs=[cur(ATTN_W), late(KV_W), late(KV_W),
                   pl.BlockSpec((ATTN_HEADS, BLOCK, 2 * BLOCK), lambda i: (0, 0, 0)),
                   _full((1, HEAD_DIM)), _full((1, HEAD_DIM)), _full((1, ATTN_HEADS))],
        out_shape=[sd((s, ATTN_W), BF), sd((s, KV_W), BF), sd((s, KV_W), BF),
                   sd((ATTN_HEADS, BLOCK, 2 * BLOCK), F32), sd((1, HEAD_DIM), F32), sd((1, HEAD_DIM), F32),
                   sd((1, ATTN_HEADS), F32)],
        scratch_shapes=[pltpu.VMEM((BLOCK, KV_W), F32), pltpu.VMEM((BLOCK, KV_W), F32),
                        pltpu.VMEM((BLOCK, ATTN_W), F32), pltpu.VMEM((2 * BLOCK, KV_W), F32),
                        pltpu.VMEM((2 * BLOCK, KV_W), F32), pltpu.VMEM((1, ATTN_W), F32), pltpu.VMEM((1, KV_W), F32)],
        compiler_params=_params(dimension_semantics=("arbitrary",)),
    )(q, k, k, v, v, bias, sinks, *consts, *folds, o_att, lse, d_o)


def _ssd_bwd(xbc, conv_all, dt_raw, conv_w, dt_bias, a_log, dsk_x, e_mat, e3t, hprev_all, dy_all):
    s = xbc.shape[0]
    nc = s // BLOCK
    gw = SSM_R * SSM_P
    b0, c0 = SSM_W, SSM_W + SSM_G * SSM_N

    def body(x_ref, conv_ref, dtr_ref, cw_ref, dtb_ref, alog_ref, dsk_ref, e_ref, e3_ref, hp_ref, dy_ref,
             dx_ref, ddt_ref, gcw_ref, gcb_ref, gdtb_ref, galog_ref, gdsk_ref,
             dh, nhead, gdskx, dxdt_s, dbc_s):
        i = pl.program_id(0)
        c = nc - 1 - i

        @pl.when(i == 0)
        def _():
            for ref in (dh, nhead, gdskx, gcw_ref, gcb_ref, gdtb_ref, galog_ref, gdsk_ref):
                ref[...] = jnp.zeros_like(ref)

        conv = conv_ref[...]
        sg, xact, u, dt, a, trilb, acum, dt_x, acum_x = _ssd_common(conv, dtr_ref, dtb_ref, alog_ref, e3_ref)
        xs = xact[:, :SSM_W]
        acum_t = acum.T
        ea_x = jnp.exp(acum_x)
        last_x = acum_x[BLOCK - 1:BLOCK, :]
        dte_x = jnp.exp(last_x - acum_x)
        cd_x = jnp.exp(last_x)
        xdt = xs * dt_x
        xw = xdt * dte_x
        hprev = hp_ref[0]
        dhn = dh[...]
        dy = dy_ref[...]
        gdskx[...] += jnp.sum(dy * xs, axis=0, keepdims=True)
        dyea = dy * ea_x
        lane = lax.broadcasted_iota(jnp.int32, (BLOCK, SSM_HEADS), 1)
        dacum = jnp.zeros((BLOCK, SSM_HEADS), F32)
        dacc_x, dlast_x = [], []
        for g in range(SSM_G):
            bgf = xact[:, b0 + g * SSM_N:b0 + (g + 1) * SSM_N]
            cgf = xact[:, c0 + g * SSM_N:c0 + (g + 1) * SSM_N]
            bg, cg = _bf(bgf), _bf(cgf)
            sl = slice(g * gw, (g + 1) * gw)
            hpg, dhg, dyeag = _bf(hprev[:, sl]), _bf(dhn[:, sl]), _bf(dyea[:, sl])
            cb = _dot_nt(cg, bg)
            gmat = _dot(cg, hpg)
            dxw = _dot(bg, dhg)
            dxdt_s[:, sl] = dxw * dte_x[:, sl]
            dacc_x.append(dy[:, sl] * gmat * ea_x[:, sl] - dxw * xw[:, sl])
            dlast_x.append(jnp.sum(dxw * xw[:, sl], axis=0, keepdims=True)
                           + jnp.sum(dhn[:, sl] * hprev[:, sl], axis=0, keepdims=True) * cd_x[:, sl])
            dcg = _dot_nt(dyeag, hpg)
            dbg = _dot_nt(_bf(xw[:, sl]), dhg)
            dh[:, sl] = dhn[:, sl] * cd_x[:, sl] + _dot_tn(cg, dyeag)
            dcb = jnp.zeros((BLOCK, BLOCK), F32)
            for r in range(SSM_R):
                hh = g * SSM_R + r
                hs = slice(hh * SSM_P, (hh + 1) * SSM_P)
                seg = jnp.where(trilb, acum[:, hh:hh + 1] - acum_t[hh:hh + 1, :], -1e30)
                lm = jnp.exp(seg)
                mm = cb * lm
                dyh = _bf(dy[:, hs])
                dm = _dot_nt(dyh, _bf(xdt[:, hs]))
                dxdt_s[:, hs] += _dot_tn(_bf(mm), dyh)
                wm = dm * mm
                dcb = dcb + dm * lm
                dacum = dacum + _group_sum(wm - wm.T, (lane == hh).astype(BF))
            dcbb = _bf(dcb)
            dbc_s[:, g * SSM_N:(g + 1) * SSM_N] = dbg + _dot_tn(dcbb, cg)
            dbc_s[:, SSM_G * SSM_N + g * SSM_N:SSM_G * SSM_N + (g + 1) * SSM_N] = dcg + _dot(dcbb, bg)
        dxdt = dxdt_s[...]
        dxs = dy * dsk_ref[...] + dxdt * dt_x
        red = _group_sum(jnp.concatenate(
            [dxdt * xs, jnp.concatenate(dacc_x, axis=1),
             jnp.broadcast_to(jnp.concatenate(dlast_x, axis=1), (8, SSM_W))], axis=0), e_ref[...])
        row = lax.broadcasted_iota(jnp.int32, (BLOCK, SSM_HEADS), 0)
        dacum = dacum + red[BLOCK:2 * BLOCK] + jnp.where(row == BLOCK - 1, red[2 * BLOCK:2 * BLOCK + 1], 0.0)
        ddta = _exact_left(_triu().astype(BF), dacum)
        ddt = red[:BLOCK] + ddta * a
        galog_ref[...] += jnp.sum(ddta * dt, axis=0, keepdims=True) * a
        du = ddt * _sig(u)
        ddt_ref[...] = _bf(du)
        gdtb_ref[...] += jnp.sum(du, axis=0, keepdims=True)
        dconv = jnp.concatenate([dxs, dbc_s[...]], axis=1) * _dsilu(conv, sg)
        gcb_ref[...] += jnp.sum(dconv, axis=0, keepdims=True)
        ext2 = jnp.concatenate([dconv, nhead[...]], axis=0)
        ahead = [pltpu.roll(ext2, BLOCK + 8 - (CONV_K - 1 - j), axis=0)[0:BLOCK] if j < CONV_K - 1 else dconv
                 for j in range(CONV_K)]
        dx_ref[...] = _bf(sum(ahead[j] * cw_ref[j:j + 1, :] for j in range(CONV_K)))
        xraw = x_ref[...]
        gcw_ref[...] += jnp.concatenate([jnp.sum(ahead[j] * xraw, axis=0, keepdims=True) for j in range(CONV_K)], axis=0)
        nhead[...] = dconv[0:8]

        @pl.when(i == nc - 1)
        def _():
            gdsk_ref[...] = _group_sum(jnp.broadcast_to(gdskx[...], (8, SSM_W)), e_ref[...])[0:1]

    chunk = lambda w: pl.BlockSpec((BLOCK, w), lambda i: (nc - 1 - i, 0))
    sd = jax.ShapeDtypeStruct
    return pl.pallas_call(
        body, name="ssd_bwd", grid=(nc,),
        in_specs=[chunk(XBC_W), chunk(XBC_W),
                  chunk(SSM_HEADS), _full((CONV_K, XBC_W)), _full((1, SSM_HEADS)),
                  _full((1, SSM_HEADS)), _full((1, SSM_W)), _full((SSM_W, SSM_HEADS)), _full((3 * SSM_HEADS, SSM_W)),
                  pl.BlockSpec((1, SSM_N, SSM_W), lambda i: (nc - 1 - i, 0, 0)), chunk(SSM_W)],
        out_specs=[chunk(XBC_W), chunk(SSM_HEADS), _full((CONV_K, XBC_W)), _full((1, XBC_W)),
                   _full((1, SSM_HEADS)), _full((1, SSM_HEADS)), _full((1, SSM_HEADS))],
        out_shape=[sd((s, XBC_W), BF), sd((s, SSM_HEADS), BF), sd((CONV_K, XBC_W), F32), sd((1, XBC_W), F32),
                   sd((1, SSM_HEADS), F32), sd((1, SSM_HEADS), F32), sd((1, SSM_HEADS), F32)],
        scratch_shapes=[pltpu.VMEM((SSM_N, SSM_W), F32), pltpu.VMEM((8, XBC_W), F32),
                        pltpu.VMEM((1, SSM_W), F32), pltpu.VMEM((BLOCK, SSM_W), F32),
                        pltpu.VMEM((BLOCK, 2 * SSM_G * SSM_N), F32)],
        compiler_params=_params(dimension_semantics=("arbitrary",)),
    )(xbc, conv_all, dt_raw, conv_w, dt_bias, a_log, dsk_x, e_mat, e3t, hprev_all, dy_all)


def _dh(x, dout, norm_w, scale, dsegs, w_t, tm=256):
    s = x.shape[0]

    def body(x_ref, dout_ref, nw_ref, sc_ref, *rest):
        d_refs, w_hbm = rest[:9], rest[9]
        gx_ref, dshift_ref, dscale_ref, gnw_ref = rest[10:14]
        w_vm, sem = rest[14], rest[15]

        @pl.when(pl.program_id(0) == 0)
        def _():
            cp = pltpu.make_async_copy(w_hbm, w_vm, sem)
            cp.start()
            for ref in (dshift_ref, dscale_ref, gnw_ref):
                ref[...] = jnp.zeros_like(ref)
            cp.wait()

        dh = _dot(d_refs[0][...], w_vm[SEG_OFF[0]:SEG_OFF[1], :])
        for j in range(1, 9):
            dh = dh + _dot(d_refs[j][...], w_vm[SEG_OFF[j]:SEG_OFF[j + 1], :])
        xv = x_ref[...]
        r = lax.rsqrt(jnp.mean(xv * xv, axis=-1, keepdims=True) + EPS)
        xn = xv * r
        nw = nw_ref[...]
        sc1 = 1.0 + sc_ref[...]
        dshift_ref[...] += jnp.sum(dh, axis=0, keepdims=True)
        dhxn = jnp.sum(dh * xn, axis=0, keepdims=True)
        dscale_ref[...] += dhxn * nw
        gnw_ref[...] += dhxn * sc1
        dxn = dh * (nw * sc1)
        gx_ref[...] = dout_ref[...] + r * (dxn - xn * jnp.mean(xn * dxn, axis=-1, keepdims=True))

    vec = _full((1, D_MODEL))
    sd = jax.ShapeDtypeStruct
    return pl.pallas_call(
        body, name="dh", grid=(s // tm,),
        in_specs=[_rows(tm, D_MODEL), _rows(tm, D_MODEL), vec, vec] + [_rows(tm, w) for w in SEG_W] + [ANY],
        out_specs=[_rows(tm, D_MODEL), vec, vec, vec],
        out_shape=[sd((s, D_MODEL), F32), sd((1, D_MODEL), F32), sd((1, D_MODEL), F32), sd((1, D_MODEL), F32)],
        scratch_shapes=[pltpu.VMEM((IN_W, D_MODEL), BF), pltpu.SemaphoreType.DMA],
        compiler_params=_params(dimension_semantics=("arbitrary",)),
    )(x, dout, norm_w, scale, *dsegs, w_t)


def _gw_seg(h, dseg, name, tm=1024):
    s, w = dseg.shape
    tn = min(w, 1024)
    tm = min(tm, s)

    def body(h_ref, d_ref, o_ref):
        @pl.when(pl.program_id(1) == 0)
        def _():
            o_ref[...] = jnp.zeros_like(o_ref)

        o_ref[...] += _dot_tn(d_ref[...], h_ref[...])

    return pl.pallas_call(
        body, name=name, grid=(w // tn, s // tm),
        in_specs=[pl.BlockSpec((tm, D_MODEL), lambda n, m: (m, 0)), pl.BlockSpec((tm, tn), lambda n, m: (m, n))],
        out_specs=pl.BlockSpec((tn, D_MODEL), lambda n, m: (n, 0)),
        out_shape=jax.ShapeDtypeStruct((w, D_MODEL), F32),
        compiler_params=_params(dimension_semantics=("arbitrary", "arbitrary")),
    )(h, dseg)


def _gw_in(h, dsegs):
    return [_gw_seg(h, d, "gw_in_%d" % j) for j, d in enumerate(dsegs)]


def _local_step(x, tgt, shift, scale, gate, w_t, rows_fn, norm_w, qnw, knw, rel_bias, sinks,
                conv_w, conv_b, dt_bias, a_log, d_skip, ssm_nw, after_mid=None, after_gw=None):
    oh_t = _bucket_onehot_t()
    bias = _bias_dense(rel_bias.T, oh_t).reshape(ATTN_HEADS, BLOCK, 2 * BLOCK)
    *segs, h = _inproj(x, norm_w, scale, shift, w_t)
    q, k, v, za, zm, xbc, dtr, ga, gb = segs
    consts = _attn_consts(qnw, knw)
    o_att, lse = _attn_fwd(q, k, v, bias, sinks, consts)
    e_mat, e3t = _membership(SSM_W, SSM_P, SSM_HEADS)
    dsk_x = jnp.repeat(d_skip, SSM_P, axis=1)
    ypre, hprev, conv = _ssd_fwd(xbc, dtr, conv_w, conv_b, dt_bias, a_log, dsk_x, e3t)
    wap, wsp, wout = rows_fn(ypre)
    (dout, d_o, dza, dyp, dzm, dga, dgb, yag, dy_a, yn, dy_b, merged, dob, g_ssm_nw, dgate, loss) = _mid(
        x, tgt, o_att, za, ypre, zm, ga, gb, gate, ssm_nw, wap, wsp, wout)
    g_wap = _gw_seg(dy_a, yag, "gw_attn_proj")
    g_wsp = _gw_seg(dy_b, yn, "gw_ssm_proj")
    g_wout = _gw_seg(dob, merged, "gw_out")
    zero = after_mid(g_wap, g_wsp, g_wout) if after_mid is not None else 0.0
    dq, dk, dv, dss, g_qnw, g_knw, g_sinks = _attn_bwd(q, k, v, bias, sinks + zero, consts, o_att, lse, d_o)
    g_rel = _bias_grad(dss.reshape(ATTN_HEADS, BLOCK * 2 * BLOCK), oh_t).T
    dxbc, ddt, g_cw, g_cb, g_dtb, g_alog, g_dsk = _ssd_bwd(
        xbc, conv, dtr, conv_w, dt_bias, a_log, dsk_x, e_mat, e3t, hprev, dyp)
    dsegs = (dq, dk, dv, dza, dzm, dxbc, ddt, dga, dgb)
    g_ws = _gw_in(h, dsegs)
    zero = after_gw(g_ws) if after_gw is not None else 0.0
    gx, dshift, dscale, g_nw = _dh(x, dout, norm_w + zero, scale, dsegs, w_t)
    return dict(loss=loss, grad_x=gx, dmod=jnp.concatenate([dshift, dscale, dgate], axis=1), g_ws=g_ws,
                g_wap=g_wap, g_wsp=g_wsp, g_wout=g_wout, g_norm_w=g_nw, g_qnw=g_qnw, g_knw=g_knw, g_rel=g_rel,
                g_sinks=g_sinks, g_conv_w=g_cw, g_conv_b=g_cb, g_dt_bias=g_dtb, g_a_log=g_alog, g_d_skip=g_dsk,
                g_ssm_nw=g_ssm_nw)


def _me():
    return lax.axis_index("x"), lax.axis_index("y"), lax.axis_index("c")


def _flip(v, bit):
    return 1 - v if bit else v


def _ag_direct(v, name):
    def body(v_ref, out_ref, send_sems, recv_sems, local_sem):
        x, y, c = _me()
        me = 4 * x + 2 * y + c
        mine = pltpu.make_async_copy(v_ref, out_ref.at[me], local_sem)
        mine.start()
        peers = [(_flip(x, k >> 2 & 1), _flip(y, k >> 1 & 1), _flip(c, k & 1)) for k in range(1, N_DEV)]
        sends = [pltpu.make_async_remote_copy(
            src_ref=v_ref, dst_ref=out_ref.at[me], send_sem=send_sems.at[j], recv_sem=recv_sems.at[j],
            device_id=p, device_id_type=MESH) for j, p in enumerate(peers)]
        for cp in sends:
            cp.start()
        for j, (px, py, pc) in enumerate(peers):
            pltpu.make_async_remote_copy(
                src_ref=v_ref, dst_ref=out_ref.at[4 * px + 2 * py + pc], send_sem=send_sems.at[j],
                recv_sem=recv_sems.at[j], device_id=(px, py, pc), device_id_type=MESH).wait_recv()
        for cp in sends:
            cp.wait_send()
        mine.wait()

    vm = pl.BlockSpec(memory_space=pltpu.VMEM)
    return pl.pallas_call(
        body, name=name, out_shape=jax.ShapeDtypeStruct((N_DEV,) + v.shape, v.dtype),
        in_specs=[vm], out_specs=vm,
        scratch_shapes=[pltpu.SemaphoreType.DMA((N_DEV - 1,)), pltpu.SemaphoreType.DMA((N_DEV - 1,)),
                        pltpu.SemaphoreType.DMA],
        compiler_params=_params(),
    )(v)


def _ag_two_level(v, name):
    def body(v_ref, out_ref, token, send_sems, recv_sems, local_sem):
        token[...] = jnp.zeros_like(token)
        x, y, c = _me()
        me, sibling = (x, y, c), (x, y, 1 - c)
        chips = [(1 - x, y), (x, 1 - y), (1 - x, 1 - y)]

        def slot(px, py, pc):
            return out_ref.at[4 * px + 2 * py + pc]

        def copy(k, block, to, src=None):
            return pltpu.make_async_remote_copy(
                src_ref=slot(*block) if src is None else src, dst_ref=slot(*block),
                send_sem=send_sems.at[k], recv_sem=recv_sems.at[k], device_id=to, device_id_type=MESH)

        mine = pltpu.make_async_copy(v_ref, slot(*me), local_sem)
        mine.start()
        first = [copy(0, me, sibling, src=v_ref)]
        first += [copy(1 + j, me, (*chip, c), src=v_ref) for j, chip in enumerate(chips)]
        for cp in first:
            cp.start()
        passed = [copy(4 + j, (*chip, c), sibling) for j, chip in enumerate(chips)]
        for j, chip in enumerate(chips):
            copy(1 + j, (*chip, c), me).wait_recv()
            passed[j].start()
        copy(0, sibling, me).wait_recv()
        for j, chip in enumerate(chips):
            copy(4 + j, (*chip, 1 - c), me).wait_recv()
        for cp in first + passed:
            cp.wait_send()
        mine.wait()

    out, token = pl.pallas_call(
        body, name=name,
        out_shape=(jax.ShapeDtypeStruct((N_DEV,) + v.shape, v.dtype), jax.ShapeDtypeStruct((8, 128), v.dtype)),
        in_specs=[ANY], out_specs=(ANY, pl.BlockSpec(memory_space=pltpu.VMEM)),
        scratch_shapes=[pltpu.SemaphoreType.DMA((7,)), pltpu.SemaphoreType.DMA((7,)), pltpu.SemaphoreType.DMA],
        compiler_params=_params(),
    )(v)
    return out, token[0:1, 0:1]


def _rs_sibling(g, name):
    def body(g_ref, out_ref, send_sems, recv_sems):
        x, y, c = _me()
        cps = [pltpu.make_async_remote_copy(
            src_ref=g_ref.at[2 * ch + 1 - c], dst_ref=out_ref.at[ch], send_sem=send_sems.at[ch],
            recv_sem=recv_sems.at[ch], device_id=(x, y, 1 - c), device_id_type=MESH) for ch in range(4)]
        for cp in cps:
            cp.start()
        for cp in cps:
            cp.wait()

    return pl.pallas_call(
        body, name=name, out_shape=jax.ShapeDtypeStruct((4,) + g.shape[1:], g.dtype),
        in_specs=[ANY], out_specs=ANY,
        scratch_shapes=[pltpu.SemaphoreType.DMA((4,)), pltpu.SemaphoreType.DMA((4,))],
        compiler_params=_params(),
    )(g)


def _add_sibling(g, got, name):
    _, r, n = g.shape
    tr = min(r, 256)

    def body(c_ref, a_ref, b_ref, o_ref):
        o_ref[...] = a_ref[...] + b_ref[...]

    grid_spec = pltpu.PrefetchScalarGridSpec(
        num_scalar_prefetch=1, grid=(4, r // tr),
        in_specs=[pl.BlockSpec((1, tr, n), lambda ch, i, c_ref: (2 * ch + c_ref[0], i, 0)),
                  pl.BlockSpec((1, tr, n), lambda ch, i, c_ref: (ch, i, 0))],
        out_specs=pl.BlockSpec((1, tr, n), lambda ch, i, c_ref: (ch, i, 0)))
    return pl.pallas_call(
        body, name=name, grid_spec=grid_spec, out_shape=jax.ShapeDtypeStruct((4, r, n), g.dtype),
        compiler_params=_params(dimension_semantics=("arbitrary", "arbitrary")),
    )(lax.axis_index("c").reshape(1).astype(jnp.int32), g, got)


def _rs_chips(p, name):
    def body(p_ref, out_ref, send_sems, recv_sems, local_sem):
        x, y, c = _me()
        my_chip = 2 * x + y
        mine = pltpu.make_async_copy(p_ref.at[my_chip], out_ref.at[my_chip], local_sem)
        mine.start()
        chips = [(1 - x, y), (x, 1 - y), (1 - x, 1 - y)]
        sends = [pltpu.make_async_remote_copy(
            src_ref=p_ref.at[2 * px + py], dst_ref=out_ref.at[my_chip], send_sem=send_sems.at[j],
            recv_sem=recv_sems.at[j], device_id=(px, py, c), device_id_type=MESH) for j, (px, py) in enumerate(chips)]
        for cp in sends:
            cp.start()
        for j, (px, py) in enumerate(chips):
            pltpu.make_async_remote_copy(
                src_ref=p_ref.at[my_chip], dst_ref=out_ref.at[2 * px + py], send_sem=send_sems.at[j],
                recv_sem=recv_sems.at[j], device_id=(px, py, c), device_id_type=MESH).wait_recv()
        for cp in sends:
            cp.wait_send()
        mine.wait()

    return pl.pallas_call(
        body, name=name, out_shape=jax.ShapeDtypeStruct(p.shape, p.dtype),
        in_specs=[ANY], out_specs=ANY,
        scratch_shapes=[pltpu.SemaphoreType.DMA((3,)), pltpu.SemaphoreType.DMA((3,)), pltpu.SemaphoreType.DMA],
        compiler_params=_params(),
    )(p)


HBM = pl.BlockSpec(memory_space=pltpu.HBM)
SEM = pl.BlockSpec(memory_space=pltpu.SEMAPHORE)
EFFECT = pltpu.SideEffectType.DATAFLOW_SIDE_EFFECTING


def _peers(x, y, c):
    return [(_flip(x, k >> 2 & 1), _flip(y, k >> 1 & 1), _flip(c, k & 1)) for k in range(1, N_DEV)]


def _exchange_start(src, land, gather, name):
    def body(src_ref, land_ref, send_sems, recv_sems, src_thru, land_thru, token):
        x, y, c = _me()
        me = 4 * x + 2 * y + c
        for j, (px, py, pc) in enumerate(_peers(x, y, c)):
            pltpu.make_async_remote_copy(
                src_ref=src_ref if gather else src_ref.at[4 * px + 2 * py + pc], dst_ref=land_ref.at[me],
                send_sem=send_sems.at[j], recv_sem=recv_sems.at[j], device_id=(px, py, pc), device_id_type=MESH).start()
        token[...] = jnp.zeros_like(token)

    sems = pltpu.SemaphoreType.DMA((N_DEV - 1,))
    out = pl.pallas_call(
        body, name=name,
        out_shape=(sems, sems, pltpu.HBM(src.shape, src.dtype), pltpu.HBM(land.shape, land.dtype),
                   jax.ShapeDtypeStruct((8, 128), F32)),
        in_specs=(HBM, HBM), out_specs=(SEM, SEM, HBM, HBM, pl.BlockSpec(memory_space=pltpu.VMEM)),
        input_output_aliases={0: 2, 1: 3},
        compiler_params=pltpu.CompilerParams(has_side_effects=EFFECT),
    )(pltpu.with_memory_space_constraint(src, pltpu.HBM), pltpu.with_memory_space_constraint(land, pltpu.HBM))
    return out[:4], out[4][0, 0]


def _exchange_wait(started, after, gather, name):
    send_sems, recv_sems, src_thru, land_thru = started

    def body(src_ref, land_ref, send_sems, recv_sems, after_ref, src_dead, got_ref):
        x, y, c = _me()
        for j, (px, py, pc) in enumerate(_peers(x, y, c)):
            pid = 4 * px + 2 * py + pc
            cp = pltpu.make_async_remote_copy(
                src_ref=src_ref if gather else src_ref.at[pid], dst_ref=land_ref.at[pid],
                send_sem=send_sems.at[j], recv_sem=recv_sems.at[j], device_id=(px, py, pc), device_id_type=MESH)
            cp.wait_send()
            cp.wait_recv()

    return pl.pallas_call(
        body, name=name,
        out_shape=(pltpu.HBM(src_thru.shape, src_thru.dtype), pltpu.HBM(land_thru.shape, land_thru.dtype)),
        in_specs=(HBM, HBM, SEM, SEM, ANY), out_specs=(HBM, HBM), input_output_aliases={0: 0, 1: 1},
        compiler_params=pltpu.CompilerParams(has_side_effects=EFFECT),
    )(src_thru, land_thru, send_sems, recv_sems, after)[1]


def _reduce_scatter(g, name):
    got = _rs_sibling(g, name + "_sib")
    return _rs_chips(_add_sibling(g, got, name + "_add"), name + "_chips")


def _silu(a):
    return a * _sig(a)


def _mod_piece(c_all, w_ada, b_piece):
    def body(c_ref, w_ref, b_ref, o_ref):
        o_ref[...] = _dot(_bf(_silu(c_ref[...])), _bf(w_ref[...])) + b_ref[...]

    return pl.pallas_call(
        body, name="mod_piece", out_shape=jax.ShapeDtypeStruct((c_all.shape[0], w_ada.shape[1]), F32),
        compiler_params=_params(),
    )(c_all, w_ada, b_piece)


def _gw_ada(c_all, dmod_piece):
    def body(c_ref, d_ref, o_ref):
        o_ref[...] = _dot_tn(_bf(_silu(c_ref[...])), _bf(d_ref[...]))

    return pl.pallas_call(
        body, name="gw_ada", out_shape=jax.ShapeDtypeStruct((c_all.shape[1], dmod_piece.shape[1]), F32),
        compiler_params=_params(),
    )(c_all, dmod_piece)


def _adam(parts, w, m, v, name):
    k, r, n = parts.shape
    if r <= 256 or r % 256 == 0:
        tr, tn = min(r, 256), n
    else:
        tr, tn = r, 256
    assert r % tr == 0 and n % tn == 0

    def body(p_ref, w_ref, m_ref, v_ref, g_ref, d_ref, nm_ref, nv_ref):
        g = p_ref[0].astype(F32)
        for j in range(1, k):
            g = g + p_ref[j].astype(F32)
        g_ref[...] = g
        d_ref[...], nm_ref[...], nv_ref[...] = _adam_math(g, w_ref[...], m_ref[...], v_ref[...])

    blk = pl.BlockSpec((tr, tn), lambda i, j: (i, j))
    return pl.pallas_call(
        body, name=name, grid=(r // tr, n // tn),
        in_specs=[pl.BlockSpec((k, tr, tn), lambda i, j: (0, i, j)), blk, blk, blk],
        out_specs=[blk, blk, blk, blk],
        out_shape=[jax.ShapeDtypeStruct((r, n), F32)] * 4,
        compiler_params=_params(dimension_semantics=("arbitrary", "arbitrary")),
    )(parts, w, m, v)


def _adam_math(g, w, m, v):
    m_new = ADAM_B1 * m + (1.0 - ADAM_B1) * g
    v_new = ADAM_B2 * v + (1.0 - ADAM_B2) * jnp.square(g)
    m_hat = m_new / (1.0 - ADAM_B1 ** ADAM_STEP)
    v_hat = v_new / (1.0 - ADAM_B2 ** ADAM_STEP)
    return -ADAM_LR * (m_hat / (jnp.sqrt(v_hat) + ADAM_EPS) + ADAM_WD * w), m_new, v_new


_SMALL = (("b_ada", 3 * D_MODEL), ("norm_w", D_MODEL), ("q_norm_w", HEAD_DIM), ("k_norm_w", HEAD_DIM),
          ("rel_bias", REL_BUCKETS * ATTN_HEADS), ("sinks", ATTN_HEADS), ("conv_b", XBC_W), ("dt_bias", SSM_HEADS),
          ("a_log", SSM_HEADS), ("d_skip", SSM_HEADS), ("ssm_norm_w", SSM_W))
_SLOT = tuple(-(-n // 128) * 128 for _, n in _SMALL)
_SLOT_OFF = tuple(int(o) for o in np.cumsum((0,) + _SLOT))
_LOSS_OFF = _SLOT_OFF[-1]
_CW_OFF = _LOSS_OFF + 128
_PACK_N = _CW_OFF + CONV_K * XBC_W


def _pack_partials(small, loss, g_conv_w):
    parts = []
    for (name, n), slot in zip(_SMALL, _SLOT):
        parts.append(small[name].reshape(1, n))
        if slot > n:
            parts.append(jnp.zeros((1, slot - n), F32))
    parts += [loss.reshape(1, 1), jnp.zeros((1, 127), F32), g_conv_w.reshape(1, CONV_K * XBC_W)]
    return jnp.concatenate(parts, axis=1)


def _adam_small(pack_all, w, m, v):
    names = [name for name, _ in _SMALL]

    def body(p_ref, *rest):
        ins, outs = rest[:3 * len(names)], rest[3 * len(names):]

        def total(off, n):
            g = p_ref[0, :, off:off + n]
            for d in range(1, N_DEV):
                g = g + p_ref[d, :, off:off + n]
            return g

        for j, (name, n) in enumerate(_SMALL):
            g = total(_SLOT_OFF[j], n)
            delta, m_new, v_new = _adam_math(g, ins[3 * j][...], ins[3 * j + 1][...], ins[3 * j + 2][...])
            outs[4 * j][...] = g
            outs[4 * j + 1][...] = delta
            outs[4 * j + 2][...] = m_new
            outs[4 * j + 3][...] = v_new
        outs[-1][...] = total(_LOSS_OFF, 1)

    flat = []
    for name, n in _SMALL:
        flat += [w[name].reshape(1, n), m[name].reshape(1, n), v[name].reshape(1, n)]
    out_shape = [jax.ShapeDtypeStruct((1, n), F32) for _, n in _SMALL for _ in range(4)] + [jax.ShapeDtypeStruct((1, 1), F32)]
    out = pl.pallas_call(body, name="adam_small", out_shape=out_shape, compiler_params=_params())(pack_all, *flat)
    res = {name: [out[4 * j + t].reshape(w[name].shape) for t in range(4)] for j, name in enumerate(names)}
    return res, out[-1]


WEIGHTS = ("w_ada", "b_ada", "norm_w", "w_in", "q_norm_w", "k_norm_w", "rel_bias", "sinks", "conv_w", "conv_b",
           "dt_bias", "a_log", "d_skip", "ssm_norm_w", "w_attn_proj", "w_ssm_proj", "w_out")


def kernel(x, c, w_ada, b_ada, norm_w, w_in, q_norm_w, k_norm_w, rel_bias, sinks, conv_w, conv_b, dt_bias, a_log, d_skip, ssm_norm_w, w_attn_proj, w_ssm_proj, w_out, loss_target, m_w_ada, m_b_ada, m_norm_w, m_w_in, m_q_norm_w, m_k_norm_w, m_rel_bias, m_sinks, m_conv_w, m_conv_b, m_dt_bias, m_a_log, m_d_skip, m_ssm_norm_w, m_w_attn_proj, m_w_ssm_proj, m_w_out, v_w_ada, v_b_ada, v_norm_w, v_w_in, v_q_norm_w, v_k_norm_w, v_rel_bias, v_sinks, v_conv_w, v_conv_b, v_dt_bias, v_a_log, v_d_skip, v_ssm_norm_w, v_w_attn_proj, v_w_ssm_proj, v_w_out):
    w = dict(w_ada=w_ada, b_ada=b_ada, norm_w=norm_w, w_in=w_in, q_norm_w=q_norm_w, k_norm_w=k_norm_w,
             rel_bias=rel_bias, sinks=sinks, conv_w=conv_w, conv_b=conv_b, dt_bias=dt_bias, a_log=a_log,
             d_skip=d_skip, ssm_norm_w=ssm_norm_w, w_attn_proj=w_attn_proj, w_ssm_proj=w_ssm_proj, w_out=w_out)
    m = dict(w_ada=m_w_ada, b_ada=m_b_ada, norm_w=m_norm_w, w_in=m_w_in, q_norm_w=m_q_norm_w, k_norm_w=m_k_norm_w,
             rel_bias=m_rel_bias, sinks=m_sinks, conv_w=m_conv_w, conv_b=m_conv_b, dt_bias=m_dt_bias, a_log=m_a_log,
             d_skip=m_d_skip, ssm_norm_w=m_ssm_norm_w, w_attn_proj=m_w_attn_proj, w_ssm_proj=m_w_ssm_proj, w_out=m_w_out)
    v = dict(w_ada=v_w_ada, b_ada=v_b_ada, norm_w=v_norm_w, w_in=v_w_in, q_norm_w=v_q_norm_w, k_norm_w=v_k_norm_w,
             rel_bias=v_rel_bias, sinks=v_sinks, conv_w=v_conv_w, conv_b=v_conv_b, dt_bias=v_dt_bias, a_log=v_a_log,
             d_skip=v_d_skip, ssm_norm_w=v_ssm_norm_w, w_attn_proj=v_w_attn_proj, w_ssm_proj=v_w_ssm_proj, w_out=v_w_out)
    me = 4 * lax.axis_index("x") + 2 * lax.axis_index("y") + lax.axis_index("c")
    ada_n = w_ada.shape[2]
    in_n = w_in.shape[2]
    cw_n = conv_w.shape[2]

    first = _ag_direct(jnp.concatenate([c, conv_w[0].reshape(1, CONV_K * cw_n)], axis=1), "ag_c")[:, 0]
    c_all = first[:, :D_MODEL]
    conv_w_full = first[:, D_MODEL:].reshape(N_DEV, CONV_K, cw_n).transpose(1, 0, 2).reshape(CONV_K, XBC_W)
    b_piece = lax.dynamic_slice_in_dim(b_ada, me * ada_n, ada_n, axis=1)
    mod_all = _ag_direct(_mod_piece(c_all, w_ada[0], b_piece), "ag_mod")
    mod = lax.dynamic_index_in_dim(mod_all, me, axis=1, keepdims=False).reshape(1, 3 * D_MODEL)
    shift, scale, gate = mod[:, :D_MODEL], mod[:, D_MODEL:2 * D_MODEL], mod[:, 2 * D_MODEL:]

    w_t, zero = _ag_two_level(w_in[0].T.astype(BF), "ag_w_in")
    w_t = w_t.reshape(N_DEV * in_n, D_MODEL)

    def with_mine(blocks, mine):
        return lax.dynamic_update_index_in_dim(jnp.zeros(blocks, mine.dtype), mine, me, axis=0)

    rows = jnp.concatenate([w_attn_proj[0], w_ssm_proj[0], w_out[0]], axis=0).astype(BF) + zero
    r_ap, r_sp = w_attn_proj.shape[1], w_ssm_proj.shape[1]
    rows_started, zero = _exchange_start(rows, with_mine((N_DEV,) + rows.shape, rows), True, "ag_rows_start")

    def rows_fn(after):
        rows_all = _exchange_wait(rows_started, after, True, "ag_rows_wait")
        return (rows_all[:, :r_ap].reshape(ATTN_W, D_MODEL), rows_all[:, r_ap:r_ap + r_sp].reshape(SSM_W, D_MODEL),
                rows_all[:, r_ap + r_sp:].reshape(D_MODEL, D_MODEL))

    started = {}

    def send_blocks(key, g, name):
        g = g.astype(BF)
        started[key], zero = _exchange_start(
            g, with_mine(g.shape, lax.dynamic_index_in_dim(g, me, axis=0, keepdims=False)), False, name)
        return zero

    def after_mid(g_wap, g_wsp, g_wout):
        return send_blocks("rows", jnp.concatenate(
            [g_wap.reshape(N_DEV, r_ap, D_MODEL), g_wsp.reshape(N_DEV, r_sp, D_MODEL),
             g_wout.reshape(N_DEV, r_ap, D_MODEL)], axis=1), "rs_rows_start")

    def after_gw(g_ws):
        return send_blocks("in", jnp.concatenate(g_ws, axis=0).reshape(N_DEV, in_n, D_MODEL), "rs_in_start")

    r = _local_step(x[0], loss_target[0], shift, scale + zero, gate, w_t, rows_fn, norm_w, q_norm_w, k_norm_w,
                    rel_bias, sinks, conv_w_full, conv_b, dt_bias, a_log, d_skip, ssm_norm_w, after_mid, after_gw)

    small = dict(b_ada=r["dmod"], norm_w=r["g_norm_w"], q_norm_w=r["g_qnw"], k_norm_w=r["g_knw"], rel_bias=r["g_rel"],
                 sinks=r["g_sinks"], conv_b=r["g_conv_b"], dt_bias=r["g_dt_bias"], a_log=r["g_a_log"],
                 d_skip=r["g_d_skip"], ssm_norm_w=r["g_ssm_nw"])
    pack_all = _ag_direct(_pack_partials(small, r["loss"], r["g_conv_w"]), "ag_small")
    res, loss = _adam_small(pack_all, w, m, v)
    loss = loss[0, 0]
    cw_parts = pack_all[:, 0, _CW_OFF:].reshape(N_DEV, CONV_K, XBC_W)
    cw_mine = lax.dynamic_slice_in_dim(cw_parts, me * cw_n, cw_n, axis=2)
    res["conv_w"] = [a[None] for a in _adam(cw_mine, conv_w[0], m_conv_w[0], v_conv_w[0], "adam_conv_w")]

    dmod_piece = lax.dynamic_slice_in_dim(pack_all[:, 0, :3 * D_MODEL], me * ada_n, ada_n, axis=1)
    g_ada = _gw_ada(c_all, dmod_piece)
    res["w_ada"] = [a[None] for a in _adam(g_ada[None], w_ada[0], m_w_ada[0], v_w_ada[0], "adam_w_ada")]

    cat = lambda d: jnp.concatenate([d["w_attn_proj"][0], d["w_ssm_proj"][0], d["w_out"][0]], axis=0)
    rows_res = _adam(_exchange_wait(started["rows"], g_ada, False, "rs_rows_wait"), cat(w), cat(m), cat(v), "adam_w_rows")
    res["w_in"] = [a.T[None] for a in _adam(_exchange_wait(started["in"], rows_res[0], False, "rs_in_wait"),
                                            w_in[0].T, m_w_in[0].T, v_w_in[0].T, "adam_w_in")]
    res["w_attn_proj"] = [a[None, :r_ap] for a in rows_res]
    res["w_ssm_proj"] = [a[None, r_ap:r_ap + r_sp] for a in rows_res]
    res["w_out"] = [a[None, r_ap + r_sp:] for a in rows_res]

    outs = [loss, r["grad_x"][None]]
    for j in range(4):
        outs += [res[name][j] for name in WEIGHTS]
    return tuple(outs)
```

```python
import functools
import math

import numpy as np
import jax
import jax.numpy as jnp
from jax import lax
from jax.experimental import pallas as pl
from jax.experimental.pallas import tpu as pltpu

F32 = jnp.float32
BF = jnp.bfloat16
HI = lax.Precision.HIGHEST

D_MODEL = 1024
ATTN_HEADS = 16
KV_HEADS = 4
GRP = ATTN_HEADS // KV_HEADS
HEAD_DIM = 64
ATTN_W = ATTN_HEADS * HEAD_DIM
KV_W = KV_HEADS * HEAD_DIM
BLOCK = 128
REL_BUCKETS = 32
REL_MAX_DIST = 128
SSM_W = 2048
SSM_P = 64
SSM_HEADS = 32
SSM_G = 4
SSM_R = 8
SSM_N = 128
CONV_K = 4
XBC_W = SSM_W + 2 * SSM_G * SSM_N
SEG_W = (ATTN_W, KV_W, KV_W, ATTN_W, SSM_W, XBC_W, SSM_HEADS, D_MODEL, D_MODEL)
SEG_OFF = tuple(int(v) for v in np.cumsum((0,) + SEG_W))
IN_W = SEG_OFF[-1]
GATE_SEGS = (3, 4, 7, 8)
EPS = 1e-6
N_DEV = 8
ADAM_LR, ADAM_B1, ADAM_B2, ADAM_EPS, ADAM_WD, ADAM_STEP = 0.001, 0.9, 0.999, 1e-08, 0.01, 10
VMEM_LIMIT = 60 * 1024 * 1024
MESH = pl.DeviceIdType.MESH
ANY = pl.BlockSpec(memory_space=pl.ANY)


def _dot(a, b, precision=None):
    return jnp.dot(a, b, preferred_element_type=F32, precision=precision)


def _dot_nt(a, b, precision=None):
    return lax.dot_general(a, b, (((1,), (1,)), ((), ())), preferred_element_type=F32, precision=precision)


def _dot_tn(a, b, precision=None):
    return lax.dot_general(a, b, (((0,), (0,)), ((), ())), preferred_element_type=F32, precision=precision)


def _bf(a):
    return a.astype(BF)


def _sig(a):
    return 0.5 * jnp.tanh(0.5 * a) + 0.5


def _params(**kw):
    return pltpu.CompilerParams(vmem_limit_bytes=VMEM_LIMIT, **kw)


def _full(shape):
    nd = len(shape)
    return pl.BlockSpec(shape, lambda i: (0,) * nd)


def _rows(tm, w):
    return pl.BlockSpec((tm, w), lambda i: (i, 0))


def _inproj(x, norm_w, scale, shift, w_t, tm=256):
    s = x.shape[0]

    def body(x_ref, nw_ref, sc_ref, sh_ref, w_hbm, *rest):
        outs, h_ref, w_vm, sem = rest[:9], rest[9], rest[10], rest[11]
        first = pl.program_id(0) == 0
        cps = [pltpu.make_async_copy(w_hbm.at[SEG_OFF[j]:SEG_OFF[j + 1], :], w_vm.at[SEG_OFF[j]:SEG_OFF[j + 1], :], sem.at[j])
               for j in range(9)]

        def tile(waiting):
            xv = x_ref[...]
            r = lax.rsqrt(jnp.mean(xv * xv, axis=-1, keepdims=True) + EPS)
            h = xv * r * (nw_ref[...] * (1.0 + sc_ref[...])) + sh_ref[...]
            hb = _bf(h)
            h_ref[...] = hb
            for j in range(9):
                if waiting:
                    cps[j].wait()
                outs[j][...] = _dot_nt(hb, w_vm[SEG_OFF[j]:SEG_OFF[j + 1], :]).astype(outs[j].dtype)

        @pl.when(first)
        def _():
            for cp in cps:
                cp.start()
            tile(True)

        @pl.when(jnp.logical_not(first))
        def _():
            tile(False)

    vec = _full((1, D_MODEL))
    return pl.pallas_call(
        body, name="inproj", grid=(s // tm,),
        in_specs=[_rows(tm, D_MODEL), vec, vec, vec, ANY],
        out_specs=[_rows(tm, w) for w in SEG_W] + [_rows(tm, D_MODEL)],
        out_shape=[jax.ShapeDtypeStruct((s, w), BF if j in GATE_SEGS else F32) for j, w in enumerate(SEG_W)]
                  + [jax.ShapeDtypeStruct((s, D_MODEL), BF)],
        scratch_shapes=[pltpu.VMEM((IN_W, D_MODEL), BF), pltpu.SemaphoreType.DMA((9,))],
        compiler_params=_params(dimension_semantics=("arbitrary",)),
    )(x, norm_w, scale, shift, w_t)


def _bucket_onehot_t():
    qi = jnp.arange(BLOCK)[:, None]
    kj = jnp.arange(2 * BLOCK)[None, :]
    dist = qi + BLOCK - kj
    n = jnp.maximum(dist, 0)
    max_exact = REL_BUCKETS // 2
    nf = jnp.maximum(n, 1).astype(F32)
    large = max_exact + (jnp.log(nf / max_exact) / math.log(REL_MAX_DIST / max_exact)
                         * (REL_BUCKETS - max_exact)).astype(jnp.int32)
    large = jnp.minimum(large, REL_BUCKETS - 1)
    bucket = jnp.where(n < max_exact, n, large).reshape(1, BLOCK * 2 * BLOCK)
    return (bucket == jnp.arange(REL_BUCKETS)[:, None]).astype(F32)


def _bias_dense(rel_bias_t, oh_t):
    def body(rb_ref, oh_ref, o_ref):
        o_ref[...] = _dot(rb_ref[...], oh_ref[...], HI)

    return pl.pallas_call(
        body, name="bias_dense", out_shape=jax.ShapeDtypeStruct((ATTN_HEADS, BLOCK * 2 * BLOCK), F32),
        compiler_params=_params(),
    )(rel_bias_t, oh_t)


def _bias_grad(ds_sum, oh_t):
    def body(ds_ref, oh_ref, o_ref):
        o_ref[...] = _dot_nt(ds_ref[...], oh_ref[...], HI)

    return pl.pallas_call(
        body, name="bias_grad", out_shape=jax.ShapeDtypeStruct((ATTN_HEADS, REL_BUCKETS), F32),
        compiler_params=_params(),
    )(ds_sum, oh_t)


def _group_sum(a, e):
    hi = _bf(a)
    return _dot(hi, e) + _dot(_bf(a - hi.astype(F32)), e)


def _group_bcast(a, e3t):
    hi = _bf(a)
    r1 = a - hi.astype(F32)
    mid = _bf(r1)
    return _dot(jnp.concatenate([hi, mid, _bf(r1 - mid.astype(F32))], axis=1), e3t)


def _membership(width, group, ngroups):
    e = (jnp.arange(width)[:, None] // group == jnp.arange(ngroups)[None, :]).astype(BF)
    return e, jnp.tile(e.T, (3, 1))


def _fold(width, group):
    return (jnp.arange(width)[:, None] % group == jnp.arange(group)[None, :]).astype(BF)


def _heads_norm(t, w_x, e, e3t):
    r = lax.rsqrt(_group_sum(t * t, e) * (1.0 / HEAD_DIM) + EPS)
    r_x = _group_bcast(r, e3t)
    return t * r_x * w_x, r_x


def _heads_norm_bwd(t, r_x, w_x, d, e, e3t):
    wd = d * w_x
    corr = _group_bcast(_group_sum(t * wd, e) * (1.0 / HEAD_DIM), e3t)
    return r_x * wd - t * (r_x * r_x * r_x) * corr, jnp.sum(d * t * r_x, axis=0, keepdims=True)


def _stack_heads(a, hk):
    return jnp.concatenate([a[:, (hk * GRP + g) * HEAD_DIM:(hk * GRP + g + 1) * HEAD_DIM] for g in range(GRP)], axis=0)


def _stack_cols(a, hk):
    return jnp.concatenate([a[:, hk * GRP + g:hk * GRP + g + 1] for g in range(GRP)], axis=0)


def _window_mask(first):
    qi = jnp.bitwise_and(lax.broadcasted_iota(jnp.int32, (GRP * BLOCK, 2 * BLOCK), 0), BLOCK - 1)
    kj = lax.broadcasted_iota(jnp.int32, (GRP * BLOCK, 2 * BLOCK), 1)
    prev_ok = jnp.logical_and(kj > qi, jnp.logical_not(first))
    cur_ok = jnp.logical_and(kj >= BLOCK, kj - BLOCK <= qi)
    return jnp.logical_or(jnp.logical_and(kj < BLOCK, prev_ok), cur_ok)


def _attn_consts(qnw, knw):
    eq, eq3t = _membership(ATTN_W, HEAD_DIM, ATTN_HEADS)
    ek, ek3t = _membership(KV_W, HEAD_DIM, ATTN_HEADS)
    return (jnp.tile(qnw, (1, ATTN_HEADS)), jnp.tile(knw, (1, KV_HEADS)), eq, eq3t, ek, ek3t)


def _attn_fwd(q, k, v, bias, sinks, consts):
    s = q.shape[0]
    nb = s // BLOCK
    gq = GRP * BLOCK
    bias_t = bias.reshape(KV_HEADS, GRP, BLOCK, 2 * BLOCK).transpose(0, 3, 1, 2).reshape(KV_HEADS, 2 * BLOCK, gq)
    sink_rows = jnp.repeat(sinks.reshape(KV_HEADS, GRP), BLOCK, axis=1).reshape(KV_HEADS, 1, gq)
    eye = jnp.eye(BLOCK, dtype=BF)

    def body(q_ref, kp_ref, kc_ref, vp_ref, vc_ref, b_ref, bt_ref, sk_ref, skr_ref, eye_ref,
             qw_ref, kw_ref, eq_ref, eq3_ref, ek_ref, ek3_ref, o_ref, lse_ref):
        i = pl.program_id(0)
        mask = _window_mask(i == 0)
        kj = lax.broadcasted_iota(jnp.int32, (2 * BLOCK, gq), 0)
        qi = jnp.bitwise_and(lax.broadcasted_iota(jnp.int32, (2 * BLOCK, gq), 1), BLOCK - 1)
        mask_t = jnp.logical_or(jnp.logical_and(kj < BLOCK, jnp.logical_and(kj > qi, i > 0)),
                                jnp.logical_and(kj >= BLOCK, kj - BLOCK <= qi))
        qn = _bf(_heads_norm(q_ref[...], qw_ref[...], eq_ref[...], eq3_ref[...])[0])
        kn = _bf(_heads_norm(jnp.concatenate([kp_ref[...], kc_ref[...]], axis=0), kw_ref[...], ek_ref[...], ek3_ref[...])[0])
        vv = _bf(jnp.concatenate([vp_ref[...], vc_ref[...]], axis=0))
        ones = jnp.ones((2 * BLOCK, HEAD_DIM), BF)
        lses = []
        for hk in range(KV_HEADS):
            ks = slice(hk * HEAD_DIM, (hk + 1) * HEAD_DIM)
            qg = _stack_heads(qn, hk)
            sc_t = jnp.where(mask_t, _dot_nt(kn[:, ks], qg) * (HEAD_DIM ** -0.5) + bt_ref[hk], -1e30)
            m_row = jnp.maximum(jnp.max(sc_t, axis=0, keepdims=True), skr_ref[hk])
            m8 = _bf(jnp.broadcast_to(m_row, (8, gq)))
            m = jnp.concatenate([_dot_nt(eye_ref[...], m8[:, g * BLOCK:(g + 1) * BLOCK])[:, 0:1] for g in range(GRP)], axis=0)
            sc = _dot_nt(qg, kn[:, ks]) * (HEAD_DIM ** -0.5)
            sc = sc + b_ref[hk * GRP:(hk + 1) * GRP].reshape(gq, 2 * BLOCK)
            p = _bf(jnp.exp(jnp.where(mask, sc, -1e30) - m))
            sink = jnp.concatenate([jnp.full((BLOCK, 1), sk_ref[0, hk * GRP + g], F32) for g in range(GRP)], axis=0)
            pv = _dot(p, jnp.concatenate([vv[:, ks], ones], axis=1))
            den = pv[:, HEAD_DIM:HEAD_DIM + 1] + jnp.exp(sink - m)
            out = pv[:, :HEAD_DIM] * (1.0 / den)
            lse = m + jnp.log(den)
            for g in range(GRP):
                h = hk * GRP + g
                o_ref[:, h * HEAD_DIM:(h + 1) * HEAD_DIM] = out[g * BLOCK:(g + 1) * BLOCK]
                lses.append(lse[g * BLOCK:(g + 1) * BLOCK])
        lse_ref[...] = jnp.concatenate(lses, axis=1)

    cur = lambda w: pl.BlockSpec((BLOCK, w), lambda i: (i, 0))
    prev = lambda w: pl.BlockSpec((BLOCK, w), lambda i: (jnp.maximum(i - 1, 0), 0))
    whole = lambda a: pl.BlockSpec(a.shape, lambda i: (0,) * a.ndim)
    return pl.pallas_call(
        body, name="attn_fwd", grid=(nb,),
        in_specs=[cur(ATTN_W), prev(KV_W), cur(KV_W), prev(KV_W), cur(KV_W), whole(bias), whole(bias_t),
                  pl.BlockSpec(memory_space=pltpu.SMEM), whole(sink_rows), whole(eye)] + [_full(c.shape) for c in consts],
        out_specs=[cur(ATTN_W), cur(ATTN_HEADS)],
        out_shape=[jax.ShapeDtypeStruct((s, ATTN_W), F32), jax.ShapeDtypeStruct((s, ATTN_HEADS), F32)],
        compiler_params=_params(dimension_semantics=("arbitrary",)),
    )(q, k, k, v, v, bias, bias_t, sinks, sink_rows, eye, *consts)


def _conv_taps(xbc, tail):
    ext = jnp.concatenate([tail, xbc], axis=0)
    return [pltpu.roll(ext, CONV_K - 1 - j, axis=0)[8:8 + BLOCK] if j < CONV_K - 1 else xbc for j in range(CONV_K)]


def _softplus(u):
    return jnp.maximum(u, 0.0) + jnp.log(1.0 + jnp.exp(-jnp.abs(u)))


def _tril():
    r = lax.broadcasted_iota(jnp.int32, (BLOCK, BLOCK), 0)
    c = lax.broadcasted_iota(jnp.int32, (BLOCK, BLOCK), 1)
    return r >= c


def _triu():
    r = lax.broadcasted_iota(jnp.int32, (BLOCK, BLOCK), 0)
    c = lax.broadcasted_iota(jnp.int32, (BLOCK, BLOCK), 1)
    return r <= c


def _exact_left(m01, a):
    hi = _bf(a)
    r1 = a - hi.astype(F32)
    mid = _bf(r1)
    return _dot(m01, hi) + _dot(m01, mid) + _dot(m01, _bf(r1 - mid.astype(F32)))


def _ssd_common(conv, dtr, dtb_ref, alog_ref, e3_ref):
    sg = _sig(conv)
    xact = conv * sg
    u = dtr + dtb_ref[...]
    dt = _softplus(u)
    a = -jnp.exp(alog_ref[...])
    trilb = _tril()
    acum = _exact_left(trilb.astype(BF), dt * a)
    both = _group_bcast(jnp.concatenate([dt, acum], axis=0), e3_ref[...])
    dt_x, acum_x = both[:BLOCK], both[BLOCK:]
    return sg, xact, u, dt, a, trilb, acum, dt_x, acum_x


SSD_CH = 2


def _ssd_fwd(xbc, dt_raw, conv_w, conv_b, dt_bias, a_log, dsk_x, e3t):
    s = xbc.shape[0]
    nc = s // BLOCK
    ch = SSD_CH if nc % SSD_CH == 0 else 1
    rows = ch * BLOCK

    def body(x_ref, tail_ref, dtr_ref, cw_ref, cb_ref, dtb_ref, alog_ref, dsk_ref, e3_ref,
             y_ref, hp_ref, conv_ref, hst, yd_s, yoff_s):
        i = pl.program_id(0)

        @pl.when(i == 0)
        def _():
            hst[...] = jnp.zeros_like(hst)

        for j in range(ch):
            rs = slice(j * BLOCK, (j + 1) * BLOCK)
            tail = jnp.where(i > 0, tail_ref[...], 0.0) if j == 0 else x_ref[j * BLOCK - 8:j * BLOCK, :]
            taps = _conv_taps(x_ref[rs, :], tail)
            conv = cb_ref[...] + sum(taps[t] * cw_ref[t:t + 1, :] for t in range(CONV_K))
            conv_ref[rs, :] = conv
            _, xact, _, _, _, trilb, acum, dt_x, acum_x = _ssd_common(conv, dtr_ref[rs, :], dtb_ref, alog_ref, e3_ref)
            xs = xact[:, :SSM_W]
            acum_t = acum.T
            ea_x = jnp.exp(acum_x)
            last_x = acum_x[BLOCK - 1:BLOCK, :]
            xdt = xs * dt_x
            xw = xdt * jnp.exp(last_x - acum_x)
            cd_x = jnp.exp(last_x)
            hprev = hst[...]
            hp_ref[j] = hprev
            for g in range(SSM_G):
                bg = _bf(xact[:, SSM_W + g * SSM_N:SSM_W + (g + 1) * SSM_N])
                cg = _bf(xact[:, SSM_W + SSM_G * SSM_N + g * SSM_N:SSM_W + SSM_G * SSM_N + (g + 1) * SSM_N])
                sl = slice(g * SSM_R * SSM_P, (g + 1) * SSM_R * SSM_P)
                cb = _dot_nt(cg, bg)
                yoff_s[:, sl] = _dot(cg, _bf(hprev[:, sl])) * ea_x[:, sl]
                hst[:, sl] = hprev[:, sl] * cd_x[:, sl] + _dot_tn(bg, _bf(xw[:, sl]))
                for r in range(SSM_R):
                    hh = g * SSM_R + r
                    hs = slice(hh * SSM_P, (hh + 1) * SSM_P)
                    seg = jnp.where(trilb, acum[:, hh:hh + 1] - acum_t[hh:hh + 1, :], -1e30)
                    yd_s[:, hs] = _dot(_bf(cb * jnp.exp(seg)), _bf(xdt[:, hs]))
            y_ref[rs, :] = yd_s[...] + yoff_s[...] + dsk_ref[...] * xs

    blk = lambda w: pl.BlockSpec((rows, w), lambda i: (i, 0))
    return pl.pallas_call(
        body, name="ssd_fwd", grid=(nc // ch,),
        in_specs=[blk(XBC_W), pl.BlockSpec((8, XBC_W), lambda i: (jnp.maximum(i * (rows // 8) - 1, 0), 0)),
                  blk(SSM_HEADS), _full((CONV_K, XBC_W)), _full((1, XBC_W)), _full((1, SSM_HEADS)),
                  _full((1, SSM_HEADS)), _full((1, SSM_W)), _full((3 * SSM_HEADS, SSM_W))],
        out_specs=[blk(SSM_W), pl.BlockSpec((ch, SSM_N, SSM_W), lambda i: (i, 0, 0)), blk(XBC_W)],
        out_shape=[jax.ShapeDtypeStruct((s, SSM_W), F32), jax.ShapeDtypeStruct((nc, SSM_N, SSM_W), F32),
                   jax.ShapeDtypeStruct((s, XBC_W), F32)],
        scratch_shapes=[pltpu.VMEM((SSM_N, SSM_W), F32), pltpu.VMEM((BLOCK, SSM_W), F32), pltpu.VMEM((BLOCK, SSM_W), F32)],
        compiler_params=_params(dimension_semantics=("arbitrary",)),
    )(xbc, xbc, dt_raw, conv_w, conv_b, dt_bias, a_log, dsk_x, e3t)


def _dsilu(z, sg):
    return sg * (1.0 + z * (1.0 - sg))


def _mid(x, tgt, o_att, za, ypre, zm, ga, gb, gate, ssm_nw, wap, wsp, wout, tm=256):
    s = x.shape[0]
    gw = SSM_W // SSM_G

    def body(x_ref, t_ref, o_ref, za_ref, yp_ref, zm_ref, ga_ref, gb_ref, gate_ref, nw_ref, wap_h, wsp_h, wout_h,
             dout_ref, do_ref, dza_ref, dyp_ref, dzm_ref, dga_ref, dgb_ref,
             yag_ref, dya_ref, yn_ref, dyb_ref, mg_ref, dob_ref, gnw_ref, dgate_ref, loss_ref,
             wap_v, wsp_v, wout_v, sem):
        i = pl.program_id(0)

        @pl.when(i == 0)
        def _():
            cps = [pltpu.make_async_copy(a, b, sem.at[j])
                   for j, (a, b) in enumerate(((wap_h, wap_v), (wsp_h, wsp_v), (wout_h, wout_v)))]
            for cp in cps:
                cp.start()
            gnw_ref[...] = jnp.zeros_like(gnw_ref)
            dgate_ref[...] = jnp.zeros_like(dgate_ref)
            loss_ref[...] = jnp.zeros_like(loss_ref)
            for cp in cps:
                cp.wait()

        gate = gate_ref[...]
        nw = nw_ref[...]
        o_att = o_ref[...]
        z_a = za_ref[...].astype(F32)
        s_a = _sig(z_a)
        silu_a = z_a * s_a
        yag = _bf(o_att * silu_a)
        yag_ref[...] = yag
        y_a = _dot(yag, wap_v[...])
        ypre = yp_ref[...]
        z_m = zm_ref[...].astype(F32)
        s_m = _sig(z_m)
        silu_m = z_m * s_m
        yg = ypre * silu_m
        rinv = jnp.concatenate(
            [jnp.broadcast_to(lax.rsqrt(jnp.mean(yg[:, g * gw:(g + 1) * gw] ** 2, axis=-1, keepdims=True) + EPS), (tm, gw))
             for g in range(SSM_G)], axis=1)
        ynr = yg * rinv
        yn = _bf(ynr * nw)
        yn_ref[...] = yn
        y_b = _dot(yn, wsp_v[...])
        g_a = _sig(ga_ref[...].astype(F32))
        g_b = _sig(gb_ref[...].astype(F32))
        merged = _bf(g_a * y_a + g_b * y_b)
        mg_ref[...] = merged
        o = _dot(merged, wout_v[...])
        diff = x_ref[...] + gate * o - t_ref[...]
        loss_ref[...] += (0.5 / D_MODEL) * jnp.sum(diff * diff, axis=(0, 1), keepdims=True)
        dout = diff * (1.0 / D_MODEL)
        dout_ref[...] = dout
        dgate_ref[...] += jnp.sum(dout * o, axis=0, keepdims=True)
        d_o = _bf(dout * gate)
        dob_ref[...] = d_o
        dmerged = _dot_nt(d_o, wout_v[...])
        dy_a = dmerged * g_a
        dy_b = dmerged * g_b
        dga_ref[...] = _bf(dy_a * y_a * (1.0 - g_a))
        dgb_ref[...] = _bf(dy_b * y_b * (1.0 - g_b))
        dy_a = _bf(dy_a)
        dy_b = _bf(dy_b)
        dya_ref[...] = dy_a
        dyb_ref[...] = dy_b
        dyag = _dot_nt(dy_a, wap_v[...])
        do_ref[...] = dyag * silu_a
        dza_ref[...] = _bf(dyag * o_att * _dsilu(z_a, s_a))
        dyn = _dot_nt(dy_b, wsp_v[...])
        gnw_ref[...] += jnp.sum(dyn * ynr, axis=0, keepdims=True)
        dynw = dyn * nw
        corr = jnp.concatenate(
            [jnp.broadcast_to(jnp.mean((dynw * ynr)[:, g * gw:(g + 1) * gw], axis=-1, keepdims=True), (tm, gw))
             for g in range(SSM_G)], axis=1)
        dyg = rinv * (dynw - ynr * corr)
        dyp_ref[...] = dyg * silu_m
        dzm_ref[...] = _bf(dyg * ypre * _dsilu(z_m, s_m))

    r1, r2 = _rows(tm, D_MODEL), _rows(tm, SSM_W)
    sd = jax.ShapeDtypeStruct
    return pl.pallas_call(
        body, name="mid", grid=(s // tm,),
        in_specs=[r1, r1, r1, r1, r2, r2, r1, r1, _full((1, D_MODEL)), _full((1, SSM_W)), ANY, ANY, ANY],
        out_specs=[r1, r1, r1, r2, r2, r1, r1, r1, r1, r2, r1, r1, r1,
                   _full((1, SSM_W)), _full((1, D_MODEL)), _full((1, 1))],
        out_shape=[sd((s, D_MODEL), F32), sd((s, ATTN_W), F32), sd((s, ATTN_W), BF), sd((s, SSM_W), F32),
                   sd((s, SSM_W), BF), sd((s, D_MODEL), BF), sd((s, D_MODEL), BF),
                   sd((s, ATTN_W), BF), sd((s, D_MODEL), BF), sd((s, SSM_W), BF), sd((s, D_MODEL), BF),
                   sd((s, D_MODEL), BF), sd((s, D_MODEL), BF),
                   sd((1, SSM_W), F32), sd((1, D_MODEL), F32), sd((1, 1), F32)],
        scratch_shapes=[pltpu.VMEM((ATTN_W, D_MODEL), BF), pltpu.VMEM((SSM_W, D_MODEL), BF), pltpu.VMEM((D_MODEL, D_MODEL), BF),
                        pltpu.SemaphoreType.DMA((3,))],
        compiler_params=_params(dimension_semantics=("arbitrary",)),
    )(x, tgt, o_att, za, ypre, zm, ga, gb, gate, ssm_nw, wap, wsp, wout)


def _attn_bwd(q, k, v, bias, sinks, consts, o_att, lse, d_o):
    s = q.shape[0]
    nb = s // BLOCK
    folds = (_fold(ATTN_W, HEAD_DIM), _fold(KV_W, HEAD_DIM))

    def body(q_ref, kp_ref, kc_ref, vp_ref, vc_ref, b_ref, skv_ref, qw_ref, kw_ref, eq_ref, eq3_ref, ek_ref, ek3_ref,
             fq_ref, fk_ref, o_ref, lse_ref, do_ref,
             dq_ref, dk_ref, dv_ref, dss_ref, gqw_ref, gkw_ref, gsk_ref, ckn, cv, dqn_s, dkn_s, dv_s, gq_x, gk_x):
        i = pl.program_id(0)
        kw, ek, ek3 = kw_ref[...], ek_ref[...], ek3_ref[...]

        @pl.when(i == 0)
        def _():
            for ref in (ckn, cv, dss_ref, gq_x, gk_x, gsk_ref):
                ref[...] = jnp.zeros_like(ref)

        @pl.when(i < nb)
        def _():
            mask = _window_mask(i == 0)
            qw, eq, eq3 = qw_ref[...], eq_ref[...], eq3_ref[...]
            qf = q_ref[...]
            qnf, rq_x = _heads_norm(qf, qw, eq, eq3)
            qn = _bf(qnf)
            kf = jnp.concatenate([kp_ref[...], kc_ref[...]], axis=0)
            knf, rk_x = _heads_norm(kf, kw, ek, ek3)
            kn = _bf(knf)
            vv = _bf(jnp.concatenate([vp_ref[...], vc_ref[...]], axis=0))
            d_of = do_ref[...]
            d_ob = _bf(d_of)
            lse_all = lse_ref[...]
            delta = _group_sum(d_of * o_ref[...], eq)
            gsk_ref[...] += jnp.sum(-jnp.exp(skv_ref[...] - lse_all) * delta, axis=0, keepdims=True)
            for hk in range(KV_HEADS):
                ks = slice(hk * HEAD_DIM, (hk + 1) * HEAD_DIM)
                qg = _stack_heads(qn, hk)
                sc = _dot_nt(qg, kn[:, ks]) * (HEAD_DIM ** -0.5)
                sc = sc + b_ref[hk * GRP:(hk + 1) * GRP].reshape(GRP * BLOCK, 2 * BLOCK)
                p = jnp.where(mask, jnp.exp(sc - _stack_cols(lse_all, hk)), 0.0)
                d_og = _stack_heads(d_ob, hk)
                ds = p * (_dot_nt(d_og, vv[:, ks]) - _stack_cols(delta, hk))
                dss_ref[hk * GRP:(hk + 1) * GRP] += ds.reshape(GRP, BLOCK, 2 * BLOCK)
                dsb = _bf(ds)
                dv_s[:, ks] = _dot_tn(_bf(p), d_og)
                dkn_s[:, ks] = _dot_tn(dsb, qg) * (HEAD_DIM ** -0.5)
                dqn = _dot(dsb, kn[:, ks]) * (HEAD_DIM ** -0.5)
                for g in range(GRP):
                    h = hk * GRP + g
                    dqn_s[:, h * HEAD_DIM:(h + 1) * HEAD_DIM] = dqn[g * BLOCK:(g + 1) * BLOCK]
            dq, gq = _heads_norm_bwd(qf, rq_x, qw, dqn_s[...], eq, eq3)
            dq_ref[...] = _bf(dq)
            gq_x[...] += gq
            dk, gk = _heads_norm_bwd(kf[:BLOCK], rk_x[:BLOCK], kw, ckn[...] + dkn_s[0:BLOCK, :], ek, ek3)
            dk_ref[...] = _bf(dk)
            gk_x[...] += gk
            dv_ref[...] = _bf(cv[...] + dv_s[0:BLOCK, :])
            ckn[...] = dkn_s[BLOCK:2 * BLOCK, :]
            cv[...] = dv_s[BLOCK:2 * BLOCK, :]

        @pl.when(i == nb)
        def _():
            kc = kc_ref[...]
            dk, gk = _heads_norm_bwd(kc, _heads_norm(kc, kw, ek, ek3)[1], kw, ckn[...], ek, ek3)
            dk_ref[...] = _bf(dk)
            dv_ref[...] = _bf(cv[...])
            gqw_ref[...] = _group_sum(jnp.broadcast_to(gq_x[...], (8, ATTN_W)), fq_ref[...])[0:1]
            gkw_ref[...] = _group_sum(jnp.broadcast_to(gk_x[...] + gk, (8, KV_W)), fk_ref[...])[0:1]

    last = nb - 1
    cur = lambda w: pl.BlockSpec((BLOCK, w), lambda i: (jnp.minimum(i, last), 0))
    prev = lambda w: pl.BlockSpec((BLOCK, w), lambda i: (jnp.maximum(jnp.minimum(i, last) - 1, 0), 0))
    late = lambda w: pl.BlockSpec((BLOCK, w), lambda i: (jnp.maximum(i - 1, 0), 0))
    sd = jax.ShapeDtypeStruct
    return pl.pallas_call(
        body, name="attn_bwd", grid=(nb + 1,),
        in_specs=[cur(ATTN_W), prev(KV_W), cur(KV_W), prev(KV_W), cur(KV_W),
                  pl.BlockSpec((ATTN_HEADS, BLOCK, 2 * BLOCK), lambda i: (0, 0, 0)), _full((1, ATTN_HEADS))]
                 + [_full(c.shape) for c in consts + folds] + [cur(ATTN_W), cur(ATTN_HEADS), cur(ATTN_W)],
        out_specs=[cur(ATTN_W), late(KV_W), late(KV_W),
                   pl.BlockSpec((ATTN_HEADS, BLOCK, 2 * BLOCK), lambda i: (0, 0, 0)),
                   _full((1, HEAD_DIM)), _full((1, HEAD_DIM)), _full((1, ATTN_HEADS))],
        out_shape=[sd((s, ATTN_W), BF), sd((s, KV_W), BF), sd((s, KV_W), BF),
                   sd((ATTN_HEADS, BLOCK, 2 * BLOCK), F32), sd((1, HEAD_DIM), F32), sd((1, HEAD_DIM), F32),
                   sd((1, ATTN_HEADS), F32)],
        scratch_shapes=[pltpu.VMEM((BLOCK, KV_W), F32), pltpu.VMEM((BLOCK, KV_W), F32),
                        pltpu.VMEM((BLOCK, ATTN_W), F32), pltpu.VMEM((2 * BLOCK, KV_W), F32),
                        pltpu.VMEM((2 * BLOCK, KV_W), F32), pltpu.VMEM((1, ATTN_W), F32), pltpu.VMEM((1, KV_W), F32)],
        compiler_params=_params(dimension_semantics=("arbitrary",)),
    )(q, k, k, v, v, bias, sinks, *consts, *folds, o_att, lse, d_o)


def _ssd_bwd(xbc, conv_all, dt_raw, conv_w, dt_bias, a_log, dsk_x, e_mat, e3t, hprev_all, dy_all):
    s = xbc.shape[0]
    nc = s // BLOCK
    ch = 1
    rows = ch * BLOCK
    nsteps = nc // ch
    gw = SSM_R * SSM_P
    b0, c0 = SSM_W, SSM_W + SSM_G * SSM_N

    def body(x_ref, conv_ref, dtr_ref, cw_ref, dtb_ref, alog_ref, dsk_ref, e_ref, e3_ref, hp_ref, dy_ref,
             dx_ref, ddt_ref, gcw_ref, gcb_ref, gdtb_ref, galog_ref, gdsk_ref,
             dh, nhead, gdskx, dxdt_s, dbc_s, dxd_s):
        def chunk_bwd(j):
            rs = slice(j * BLOCK, (j + 1) * BLOCK)
            conv = conv_ref[rs, :]
            sg, xact, u, dt, a, trilb, acum, dt_x, acum_x = _ssd_common(conv, dtr_ref[rs, :], dtb_ref, alog_ref, e3_ref)
            xs = xact[:, :SSM_W]
            acum_t = acum.T
            ea_x = jnp.exp(acum_x)
            last_x = acum_x[BLOCK - 1:BLOCK, :]
            dte_x = jnp.exp(last_x - acum_x)
            cd_x = jnp.exp(last_x)
            xdt = xs * dt_x
            xw = xdt * dte_x
            hprev = hp_ref[j]
            dhn = dh[...]
            dy = dy_ref[rs, :]
            gdskx[...] += jnp.sum(dy * xs, axis=0, keepdims=True)
            dyea = dy * ea_x
            lane = lax.broadcasted_iota(jnp.int32, (BLOCK, SSM_HEADS), 1)
            dacum = jnp.zeros((BLOCK, SSM_HEADS), F32)
            dacc_x, dlast_x = [], []
            for g in range(SSM_G):
                bgf = xact[:, b0 + g * SSM_N:b0 + (g + 1) * SSM_N]
                cgf = xact[:, c0 + g * SSM_N:c0 + (g + 1) * SSM_N]
                bg, cg = _bf(bgf), _bf(cgf)
                sl = slice(g * gw, (g + 1) * gw)
                hpg, dhg, dyeag = _bf(hprev[:, sl]), _bf(dhn[:, sl]), _bf(dyea[:, sl])
                cb = _dot_nt(cg, bg)
                gmat = _dot(cg, hpg)
                dxw = _dot(bg, dhg)
                dxdt_s[:, sl] = dxw * dte_x[:, sl]
                dacc_x.append(dy[:, sl] * gmat * ea_x[:, sl] - dxw * xw[:, sl])
                dlast_x.append(jnp.sum(dxw * xw[:, sl], axis=0, keepdims=True)
                               + jnp.sum(dhn[:, sl] * hprev[:, sl], axis=0, keepdims=True) * cd_x[:, sl])
                dcg = _dot_nt(dyeag, hpg)
                dbg = _dot_nt(_bf(xw[:, sl]), dhg)
                dh[:, sl] = dhn[:, sl] * cd_x[:, sl] + _dot_tn(cg, dyeag)
                dcb = jnp.zeros((BLOCK, BLOCK), F32)
                for r in range(SSM_R):
                    hh = g * SSM_R + r
                    hs = slice(hh * SSM_P, (hh + 1) * SSM_P)
                    seg = jnp.where(trilb, acum[:, hh:hh + 1] - acum_t[hh:hh + 1, :], -1e30)
                    lm = jnp.exp(seg)
                    mm = cb * lm
                    dyh = _bf(dy[:, hs])
                    dm = _dot_nt(dyh, _bf(xdt[:, hs]))
                    dxd_s[:, hs] = _dot_tn(_bf(mm), dyh)
                    wm = dm * mm
                    dcb = dcb + dm * lm
                    dacum = dacum + _group_sum(wm - wm.T, (lane == hh).astype(BF))
                dcbb = _bf(dcb)
                dbc_s[:, g * SSM_N:(g + 1) * SSM_N] = dbg + _dot_tn(dcbb, cg)
                dbc_s[:, SSM_G * SSM_N + g * SSM_N:SSM_G * SSM_N + (g + 1) * SSM_N] = dcg + _dot(dcbb, bg)
            dxdt = dxdt_s[...] + dxd_s[...]
            dxs = dy * dsk_ref[...] + dxdt * dt_x
            red = _group_sum(jnp.concatenate(
                [dxdt * xs, jnp.concatenate(dacc_x, axis=1),
                 jnp.broadcast_to(jnp.concatenate(dlast_x, axis=1), (8, SSM_W))], axis=0), e_ref[...])
            row = lax.broadcasted_iota(jnp.int32, (BLOCK, SSM_HEADS), 0)
            dacum = dacum + red[BLOCK:2 * BLOCK] + jnp.where(row == BLOCK - 1, red[2 * BLOCK:2 * BLOCK + 1], 0.0)
            ddta = _exact_left(_triu().astype(BF), dacum)
            ddt = red[:BLOCK] + ddta * a
            galog_ref[...] += jnp.sum(ddta * dt, axis=0, keepdims=True) * a
            du = ddt * _sig(u)
            ddt_ref[rs, :] = _bf(du)
            gdtb_ref[...] += jnp.sum(du, axis=0, keepdims=True)
            dconv = jnp.concatenate([dxs, dbc_s[...]], axis=1) * _dsilu(conv, sg)
            gcb_ref[...] += jnp.sum(dconv, axis=0, keepdims=True)
            ext2 = jnp.concatenate([dconv, nhead[...]], axis=0)
            ahead = [pltpu.roll(ext2, BLOCK + 8 - (CONV_K - 1 - j), axis=0)[0:BLOCK] if j < CONV_K - 1 else dconv
                     for j in range(CONV_K)]
            dx_ref[rs, :] = _bf(sum(ahead[j] * cw_ref[j:j + 1, :] for j in range(CONV_K)))
            xraw = x_ref[rs, :]
            gcw_ref[...] += jnp.concatenate([jnp.sum(ahead[j] * xraw, axis=0, keepdims=True) for j in range(CONV_K)], axis=0)
            nhead[...] = dconv[0:8]

        i = pl.program_id(0)

        @pl.when(i == 0)
        def _():
            for ref in (dh, nhead, gdskx, gcw_ref, gcb_ref, gdtb_ref, galog_ref, gdsk_ref):
                ref[...] = jnp.zeros_like(ref)

        for j in reversed(range(ch)):
            chunk_bwd(j)

        @pl.when(i == nsteps - 1)
        def _():
            gdsk_ref[...] = _group_sum(jnp.broadcast_to(gdskx[...], (8, SSM_W)), e_ref[...])[0:1]

    chunk = lambda w: pl.BlockSpec((rows, w), lambda i: (nsteps - 1 - i, 0))
    sd = jax.ShapeDtypeStruct
    return pl.pallas_call(
        body, name="ssd_bwd", grid=(nsteps,),
        in_specs=[chunk(XBC_W), chunk(XBC_W),
                  chunk(SSM_HEADS), _full((CONV_K, XBC_W)), _full((1, SSM_HEADS)),
                  _full((1, SSM_HEADS)), _full((1, SSM_W)), _full((SSM_W, SSM_HEADS)), _full((3 * SSM_HEADS, SSM_W)),
                  pl.BlockSpec((ch, SSM_N, SSM_W), lambda i: (nsteps - 1 - i, 0, 0)), chunk(SSM_W)],
        out_specs=[chunk(XBC_W), chunk(SSM_HEADS), _full((CONV_K, XBC_W)), _full((1, XBC_W)),
                   _full((1, SSM_HEADS)), _full((1, SSM_HEADS)), _full((1, SSM_HEADS))],
        out_shape=[sd((s, XBC_W), BF), sd((s, SSM_HEADS), BF), sd((CONV_K, XBC_W), F32), sd((1, XBC_W), F32),
                   sd((1, SSM_HEADS), F32), sd((1, SSM_HEADS), F32), sd((1, SSM_HEADS), F32)],
        scratch_shapes=[pltpu.VMEM((SSM_N, SSM_W), F32), pltpu.VMEM((8, XBC_W), F32),
                        pltpu.VMEM((1, SSM_W), F32), pltpu.VMEM((BLOCK, SSM_W), F32),
                        pltpu.VMEM((BLOCK, 2 * SSM_G * SSM_N), F32), pltpu.VMEM((BLOCK, SSM_W), F32)],
        compiler_params=_params(dimension_semantics=("arbitrary",)),
    )(xbc, conv_all, dt_raw, conv_w, dt_bias, a_log, dsk_x, e_mat, e3t, hprev_all, dy_all)


def _dh(x, dout, norm_w, scale, dsegs, w_t, tm=256):
    s = x.shape[0]

    def body(x_ref, dout_ref, nw_ref, sc_ref, *rest):
        d_refs, w_hbm = rest[:9], rest[9]
        gx_ref, dshift_ref, dscale_ref, gnw_ref = rest[10:14]
        w_vm, sem = rest[14], rest[15]
        first = pl.program_id(0) == 0
        cps = [pltpu.make_async_copy(w_hbm.at[SEG_OFF[j]:SEG_OFF[j + 1], :], w_vm.at[SEG_OFF[j]:SEG_OFF[j + 1], :], sem.at[j])
               for j in range(9)]

        def tile(waiting):
            dh = None
            for j in range(9):
                if waiting:
                    cps[j].wait()
                part = _dot(d_refs[j][...], w_vm[SEG_OFF[j]:SEG_OFF[j + 1], :])
                dh = part if dh is None else dh + part
            xv = x_ref[...]
            r = lax.rsqrt(jnp.mean(xv * xv, axis=-1, keepdims=True) + EPS)
            xn = xv * r
            nw = nw_ref[...]
            sc1 = 1.0 + sc_ref[...]
            dshift_ref[...] += jnp.sum(dh, axis=0, keepdims=True)
            dhxn = jnp.sum(dh * xn, axis=0, keepdims=True)
            dscale_ref[...] += dhxn * nw
            gnw_ref[...] += dhxn * sc1
            dxn = dh * (nw * sc1)
            gx_ref[...] = dout_ref[...] + r * (dxn - xn * jnp.mean(xn * dxn, axis=-1, keepdims=True))

        @pl.when(first)
        def _():
            for cp in cps:
                cp.start()
            for ref in (dshift_ref, dscale_ref, gnw_ref):
                ref[...] = jnp.zeros_like(ref)
            tile(True)

        @pl.when(jnp.logical_not(first))
        def _():
            tile(False)

    vec = _full((1, D_MODEL))
    sd = jax.ShapeDtypeStruct
    return pl.pallas_call(
        body, name="dh", grid=(s // tm,),
        in_specs=[_rows(tm, D_MODEL), _rows(tm, D_MODEL), vec, vec] + [_rows(tm, w) for w in SEG_W] + [ANY],
        out_specs=[_rows(tm, D_MODEL), vec, vec, vec],
        out_shape=[sd((s, D_MODEL), F32), sd((1, D_MODEL), F32), sd((1, D_MODEL), F32), sd((1, D_MODEL), F32)],
        scratch_shapes=[pltpu.VMEM((IN_W, D_MODEL), BF), pltpu.SemaphoreType.DMA((9,))],
        compiler_params=_params(dimension_semantics=("arbitrary",)),
    )(x, dout, norm_w, scale, *dsegs, w_t)


def _gw_seg(h, dseg, name, tm=1024):
    s, w = dseg.shape
    tn = min(w, 1024)
    tm = min(tm, s)

    def body(h_ref, d_ref, o_ref):
        @pl.when(pl.program_id(1) == 0)
        def _():
            o_ref[...] = jnp.zeros_like(o_ref)

        o_ref[...] += _dot_tn(d_ref[...], h_ref[...])

    return pl.pallas_call(
        body, name=name, grid=(w // tn, s // tm),
        in_specs=[pl.BlockSpec((tm, D_MODEL), lambda n, m: (m, 0)), pl.BlockSpec((tm, tn), lambda n, m: (m, n))],
        out_specs=pl.BlockSpec((tn, D_MODEL), lambda n, m: (n, 0)),
        out_shape=jax.ShapeDtypeStruct((w, D_MODEL), F32),
        compiler_params=_params(dimension_semantics=("arbitrary", "arbitrary")),
    )(h, dseg)


def _gw_in(h, dsegs):
    return [_gw_seg(h, d, "gw_in_%d" % j) for j, d in enumerate(dsegs)]


def _local_step(x, tgt, shift, scale, gate, w_t, rows_fn, norm_w, qnw, knw, rel_bias, sinks,
                conv_w, conv_b, dt_bias, a_log, d_skip, ssm_nw, after_mid=None, after_gw=None):
    oh_t = _bucket_onehot_t()
    bias = _bias_dense(rel_bias.T, oh_t).reshape(ATTN_HEADS, BLOCK, 2 * BLOCK)
    *segs, h = _inproj(x, norm_w, scale, shift, w_t)
    q, k, v, za, zm, xbc, dtr, ga, gb = segs
    consts = _attn_consts(qnw, knw)
    o_att, lse = _attn_fwd(q, k, v, bias, sinks, consts)
    e_mat, e3t = _membership(SSM_W, SSM_P, SSM_HEADS)
    dsk_x = jnp.repeat(d_skip, SSM_P, axis=1)
    ypre, hprev, conv = _ssd_fwd(xbc, dtr, conv_w, conv_b, dt_bias, a_log, dsk_x, e3t)
    wap, wsp, wout = rows_fn(ypre)
    (dout, d_o, dza, dyp, dzm, dga, dgb, yag, dy_a, yn, dy_b, merged, dob, g_ssm_nw, dgate, loss) = _mid(
        x, tgt, o_att, za, ypre, zm, ga, gb, gate, ssm_nw, wap, wsp, wout)
    g_wap = _gw_seg(dy_a, yag, "gw_attn_proj")
    g_wsp = _gw_seg(dy_b, yn, "gw_ssm_proj")
    g_wout = _gw_seg(dob, merged, "gw_out")
    zero = after_mid(g_wap, g_wsp, g_wout) if after_mid is not None else 0.0
    dq, dk, dv, dss, g_qnw, g_knw, g_sinks = _attn_bwd(q, k, v, bias, sinks + zero, consts, o_att, lse, d_o)
    g_rel = _bias_grad(dss.reshape(ATTN_HEADS, BLOCK * 2 * BLOCK), oh_t).T
    dxbc, ddt, g_cw, g_cb, g_dtb, g_alog, g_dsk = _ssd_bwd(
        xbc, conv, dtr, conv_w, dt_bias, a_log, dsk_x, e_mat, e3t, hprev, dyp)
    dsegs = (dq, dk, dv, dza, dzm, dxbc, ddt, dga, dgb)
    g_ws = _gw_in(h, dsegs)
    zero = after_gw(g_ws) if after_gw is not None else 0.0
    gx, dshift, dscale, g_nw = _dh(x, dout, norm_w + zero, scale, dsegs, w_t)
    return dict(loss=loss, grad_x=gx, dmod=jnp.concatenate([dshift, dscale, dgate], axis=1), g_ws=g_ws,
                g_wap=g_wap, g_wsp=g_wsp, g_wout=g_wout, g_norm_w=g_nw, g_qnw=g_qnw, g_knw=g_knw, g_rel=g_rel,
                g_sinks=g_sinks, g_conv_w=g_cw, g_conv_b=g_cb, g_dt_bias=g_dtb, g_a_log=g_alog, g_d_skip=g_dsk,
                g_ssm_nw=g_ssm_nw)


def _me():
    return lax.axis_index("x"), lax.axis_index("y"), lax.axis_index("c")


def _flip(v, bit):
    return 1 - v if bit else v


def _ag_direct(v, name):
    def body(v_ref, out_ref, send_sems, recv_sems, local_sem):
        x, y, c = _me()
        me = 4 * x + 2 * y + c
        mine = pltpu.make_async_copy(v_ref, out_ref.at[me], local_sem)
        mine.start()
        peers = [(_flip(x, k >> 2 & 1), _flip(y, k >> 1 & 1), _flip(c, k & 1)) for k in range(1, N_DEV)]
        sends = [pltpu.make_async_remote_copy(
            src_ref=v_ref, dst_ref=out_ref.at[me], send_sem=send_sems.at[j], recv_sem=recv_sems.at[j],
            device_id=p, device_id_type=MESH) for j, p in enumerate(peers)]
        for cp in sends:
            cp.start()
        for j, (px, py, pc) in enumerate(peers):
            pltpu.make_async_remote_copy(
                src_ref=v_ref, dst_ref=out_ref.at[4 * px + 2 * py + pc], send_sem=send_sems.at[j],
                recv_sem=recv_sems.at[j], device_id=(px, py, pc), device_id_type=MESH).wait_recv()
        for cp in sends:
            cp.wait_send()
        mine.wait()

    vm = pl.BlockSpec(memory_space=pltpu.VMEM)
    return pl.pallas_call(
        body, name=name, out_shape=jax.ShapeDtypeStruct((N_DEV,) + v.shape, v.dtype),
        in_specs=[vm], out_specs=vm,
        scratch_shapes=[pltpu.SemaphoreType.DMA((N_DEV - 1,)), pltpu.SemaphoreType.DMA((N_DEV - 1,)),
                        pltpu.SemaphoreType.DMA],
        compiler_params=_params(),
    )(v)


def _ag_two_level(v, name):
    def body(v_ref, out_ref, token, send_sems, recv_sems, local_sem):
        token[...] = jnp.zeros_like(token)
        x, y, c = _me()
        me, sibling = (x, y, c), (x, y, 1 - c)
        chips = [(1 - x, y), (x, 1 - y), (1 - x, 1 - y)]

        def slot(px, py, pc):
            return out_ref.at[4 * px + 2 * py + pc]

        def copy(k, block, to, src=None):
            return pltpu.make_async_remote_copy(
                src_ref=slot(*block) if src is None else src, dst_ref=slot(*block),
                send_sem=send_sems.at[k], recv_sem=recv_sems.at[k], device_id=to, device_id_type=MESH)

        mine = pltpu.make_async_copy(v_ref, slot(*me), local_sem)
        mine.start()
        first = [copy(0, me, sibling, src=v_ref)]
        first += [copy(1 + j, me, (*chip, c), src=v_ref) for j, chip in enumerate(chips)]
        for cp in first:
            cp.start()
        passed = [copy(4 + j, (*chip, c), sibling) for j, chip in enumerate(chips)]
        for j, chip in enumerate(chips):
            copy(1 + j, (*chip, c), me).wait_recv()
            passed[j].start()
        copy(0, sibling, me).wait_recv()
        for j, chip in enumerate(chips):
            copy(4 + j, (*chip, 1 - c), me).wait_recv()
        for cp in first + passed:
            cp.wait_send()
        mine.wait()

    out, token = pl.pallas_call(
        body, name=name,
        out_shape=(jax.ShapeDtypeStruct((N_DEV,) + v.shape, v.dtype), jax.ShapeDtypeStruct((8, 128), v.dtype)),
        in_specs=[ANY], out_specs=(ANY, pl.BlockSpec(memory_space=pltpu.VMEM)),
        scratch_shapes=[pltpu.SemaphoreType.DMA((7,)), pltpu.SemaphoreType.DMA((7,)), pltpu.SemaphoreType.DMA],
        compiler_params=_params(),
    )(v)
    return out, token[0:1, 0:1]


def _rs_sibling(g, name):
    def body(g_ref, out_ref, send_sems, recv_sems):
        x, y, c = _me()
        cps = [pltpu.make_async_remote_copy(
            src_ref=g_ref.at[2 * ch + 1 - c], dst_ref=out_ref.at[ch], send_sem=send_sems.at[ch],
            recv_sem=recv_sems.at[ch], device_id=(x, y, 1 - c), device_id_type=MESH) for ch in range(4)]
        for cp in cps:
            cp.start()
        for cp in cps:
            cp.wait()

    return pl.pallas_call(
        body, name=name, out_shape=jax.ShapeDtypeStruct((4,) + g.shape[1:], g.dtype),
        in_specs=[ANY], out_specs=ANY,
        scratch_shapes=[pltpu.SemaphoreType.DMA((4,)), pltpu.SemaphoreType.DMA((4,))],
        compiler_params=_params(),
    )(g)


def _add_sibling(g, got, name):
    _, r, n = g.shape
    tr = min(r, 256)

    def body(c_ref, a_ref, b_ref, o_ref):
        o_ref[...] = a_ref[...] + b_ref[...]

    grid_spec = pltpu.PrefetchScalarGridSpec(
        num_scalar_prefetch=1, grid=(4, r // tr),
        in_specs=[pl.BlockSpec((1, tr, n), lambda ch, i, c_ref: (2 * ch + c_ref[0], i, 0)),
                  pl.BlockSpec((1, tr, n), lambda ch, i, c_ref: (ch, i, 0))],
        out_specs=pl.BlockSpec((1, tr, n), lambda ch, i, c_ref: (ch, i, 0)))
    return pl.pallas_call(
        body, name=name, grid_spec=grid_spec, out_shape=jax.ShapeDtypeStruct((4, r, n), g.dtype),
        compiler_params=_params(dimension_semantics=("arbitrary", "arbitrary")),
    )(lax.axis_index("c").reshape(1).astype(jnp.int32), g, got)


def _rs_chips(p, name):
    def body(p_ref, out_ref, send_sems, recv_sems, local_sem):
        x, y, c = _me()
        my_chip = 2 * x + y
        mine = pltpu.make_async_copy(p_ref.at[my_chip], out_ref.at[my_chip], local_sem)
        mine.start()
        chips = [(1 - x, y), (x, 1 - y), (1 - x, 1 - y)]
        sends = [pltpu.make_async_remote_copy(
            src_ref=p_ref.at[2 * px + py], dst_ref=out_ref.at[my_chip], send_sem=send_sems.at[j],
            recv_sem=recv_sems.at[j], device_id=(px, py, c), device_id_type=MESH) for j, (px, py) in enumerate(chips)]
        for cp in sends:
            cp.start()
        for j, (px, py) in enumerate(chips):
            pltpu.make_async_remote_copy(
                src_ref=p_ref.at[my_chip], dst_ref=out_ref.at[2 * px + py], send_sem=send_sems.at[j],
                recv_sem=recv_sems.at[j], device_id=(px, py, c), device_id_type=MESH).wait_recv()
        for cp in sends:
            cp.wait_send()
        mine.wait()

    return pl.pallas_call(
        body, name=name, out_shape=jax.ShapeDtypeStruct(p.shape, p.dtype),
        in_specs=[ANY], out_specs=ANY,
        scratch_shapes=[pltpu.SemaphoreType.DMA((3,)), pltpu.SemaphoreType.DMA((3,)), pltpu.SemaphoreType.DMA],
        compiler_params=_params(),
    )(p)


HBM = pl.BlockSpec(memory_space=pltpu.HBM)
SEM = pl.BlockSpec(memory_space=pltpu.SEMAPHORE)
EFFECT = pltpu.SideEffectType.DATAFLOW_SIDE_EFFECTING


def _peers(x, y, c):
    return [(_flip(x, k >> 2 & 1), _flip(y, k >> 1 & 1), _flip(c, k & 1)) for k in range(1, N_DEV)]


def _exchange_start(src, land, gather, name):
    def body(src_ref, land_ref, send_sems, recv_sems, src_thru, land_thru, token):
        x, y, c = _me()
        me = 4 * x + 2 * y + c
        for j, (px, py, pc) in enumerate(_peers(x, y, c)):
            pltpu.make_async_remote_copy(
                src_ref=src_ref if gather else src_ref.at[4 * px + 2 * py + pc], dst_ref=land_ref.at[me],
                send_sem=send_sems.at[j], recv_sem=recv_sems.at[j], device_id=(px, py, pc), device_id_type=MESH).start()
        token[...] = jnp.zeros_like(token)

    sems = pltpu.SemaphoreType.DMA((N_DEV - 1,))
    out = pl.pallas_call(
        body, name=name,
        out_shape=(sems, sems, pltpu.HBM(src.shape, src.dtype), pltpu.HBM(land.shape, land.dtype),
                   jax.ShapeDtypeStruct((8, 128), F32)),
        in_specs=(HBM, HBM), out_specs=(SEM, SEM, HBM, HBM, pl.BlockSpec(memory_space=pltpu.VMEM)),
        input_output_aliases={0: 2, 1: 3},
        compiler_params=pltpu.CompilerParams(has_side_effects=EFFECT),
    )(pltpu.with_memory_space_constraint(src, pltpu.HBM), pltpu.with_memory_space_constraint(land, pltpu.HBM))
    return out[:4], out[4][0, 0]


def _exchange_wait(started, after, gather, name):
    send_sems, recv_sems, src_thru, land_thru = started

    def body(src_ref, land_ref, send_sems, recv_sems, after_ref, src_dead, got_ref):
        x, y, c = _me()
        for j, (px, py, pc) in enumerate(_peers(x, y, c)):
            pid = 4 * px + 2 * py + pc
            cp = pltpu.make_async_remote_copy(
                src_ref=src_ref if gather else src_ref.at[pid], dst_ref=land_ref.at[pid],
                send_sem=send_sems.at[j], recv_sem=recv_sems.at[j], device_id=(px, py, pc), device_id_type=MESH)
            cp.wait_send()
            cp.wait_recv()

    return pl.pallas_call(
        body, name=name,
        out_shape=(pltpu.HBM(src_thru.shape, src_thru.dtype), pltpu.HBM(land_thru.shape, land_thru.dtype)),
        in_specs=(HBM, HBM, SEM, SEM, ANY), out_specs=(HBM, HBM), input_output_aliases={0: 0, 1: 1},
        compiler_params=pltpu.CompilerParams(has_side_effects=EFFECT),
    )(src_thru, land_thru, send_sems, recv_sems, after)[1]


def _reduce_scatter(g, name):
    got = _rs_sibling(g, name + "_sib")
    return _rs_chips(_add_sibling(g, got, name + "_add"), name + "_chips")


def _silu(a):
    return a * _sig(a)


def _mod_piece(c_all, w_ada, b_piece):
    def body(c_ref, w_ref, b_ref, o_ref):
        o_ref[...] = _dot(_bf(_silu(c_ref[...])), _bf(w_ref[...])) + b_ref[...]

    return pl.pallas_call(
        body, name="mod_piece", out_shape=jax.ShapeDtypeStruct((c_all.shape[0], w_ada.shape[1]), F32),
        compiler_params=_params(),
    )(c_all, w_ada, b_piece)


def _gw_ada(c_all, dmod_piece):
    def body(c_ref, d_ref, o_ref):
        o_ref[...] = _dot_tn(_bf(_silu(c_ref[...])), _bf(d_ref[...]))

    return pl.pallas_call(
        body, name="gw_ada", out_shape=jax.ShapeDtypeStruct((c_all.shape[1], dmod_piece.shape[1]), F32),
        compiler_params=_params(),
    )(c_all, dmod_piece)


def _adam(parts, w, m, v, name):
    k, r, n = parts.shape
    if r <= 256 or r % 256 == 0:
        tr, tn = min(r, 256), n
    else:
        tr, tn = r, 256
    assert r % tr == 0 and n % tn == 0

    def body(p_ref, w_ref, m_ref, v_ref, g_ref, d_ref, nm_ref, nv_ref):
        g = p_ref[0].astype(F32)
        for j in range(1, k):
            g = g + p_ref[j].astype(F32)
        g_ref[...] = g
        d_ref[...], nm_ref[...], nv_ref[...] = _adam_math(g, w_ref[...], m_ref[...], v_ref[...])

    blk = pl.BlockSpec((tr, tn), lambda i, j: (i, j))
    return pl.pallas_call(
        body, name=name, grid=(r // tr, n // tn),
        in_specs=[pl.BlockSpec((k, tr, tn), lambda i, j: (0, i, j)), blk, blk, blk],
        out_specs=[blk, blk, blk, blk],
        out_shape=[jax.ShapeDtypeStruct((r, n), F32)] * 4,
        compiler_params=_params(dimension_semantics=("arbitrary", "arbitrary")),
    )(parts, w, m, v)


def _adam_math(g, w, m, v):
    m_new = ADAM_B1 * m + (1.0 - ADAM_B1) * g
    v_new = ADAM_B2 * v + (1.0 - ADAM_B2) * jnp.square(g)
    m_hat = m_new / (1.0 - ADAM_B1 ** ADAM_STEP)
    v_hat = v_new / (1.0 - ADAM_B2 ** ADAM_STEP)
    return -ADAM_LR * (m_hat / (jnp.sqrt(v_hat) + ADAM_EPS) + ADAM_WD * w), m_new, v_new


_SMALL = (("b_ada", 3 * D_MODEL), ("norm_w", D_MODEL), ("q_norm_w", HEAD_DIM), ("k_norm_w", HEAD_DIM),
          ("rel_bias", REL_BUCKETS * ATTN_HEADS), ("sinks", ATTN_HEADS), ("conv_b", XBC_W), ("dt_bias", SSM_HEADS),
          ("a_log", SSM_HEADS), ("d_skip", SSM_HEADS), ("ssm_norm_w", SSM_W))
_SLOT = tuple(-(-n // 128) * 128 for _, n in _SMALL)
_SLOT_OFF = tuple(int(o) for o in np.cumsum((0,) + _SLOT))
_LOSS_OFF = _SLOT_OFF[-1]
_CW_OFF = _LOSS_OFF + 128
_PACK_N = _CW_OFF + CONV_K * XBC_W


def _pack_partials(small, loss, g_conv_w):
    parts = []
    for (name, n), slot in zip(_SMALL, _SLOT):
        parts.append(small[name].reshape(1, n))
        if slot > n:
            parts.append(jnp.zeros((1, slot - n), F32))
    parts += [loss.reshape(1, 1), jnp.zeros((1, 127), F32), g_conv_w.reshape(1, CONV_K * XBC_W)]
    return jnp.concatenate(parts, axis=1)


def _adam_small(pack_all, w, m, v):
    names = [name for name, _ in _SMALL]

    def body(p_ref, *rest):
        ins, outs = rest[:3 * len(names)], rest[3 * len(names):]

        def total(off, n):
            g = p_ref[0, :, off:off + n]
            for d in range(1, N_DEV):
                g = g + p_ref[d, :, off:off + n]
            return g

        for j, (name, n) in enumerate(_SMALL):
            g = total(_SLOT_OFF[j], n)
            delta, m_new, v_new = _adam_math(g, ins[3 * j][...], ins[3 * j + 1][...], ins[3 * j + 2][...])
            outs[4 * j][...] = g
            outs[4 * j + 1][...] = delta
            outs[4 * j + 2][...] = m_new
            outs[4 * j + 3][...] = v_new
        outs[-1][...] = total(_LOSS_OFF, 1)

    flat = []
    for name, n in _SMALL:
        flat += [w[name].reshape(1, n), m[name].reshape(1, n), v[name].reshape(1, n)]
    out_shape = [jax.ShapeDtypeStruct((1, n), F32) for _, n in _SMALL for _ in range(4)] + [jax.ShapeDtypeStruct((1, 1), F32)]
    out = pl.pallas_call(body, name="adam_small", out_shape=out_shape, compiler_params=_params())(pack_all, *flat)
    res = {name: [out[4 * j + t].reshape(w[name].shape) for t in range(4)] for j, name in enumerate(names)}
    return res, out[-1]


WEIGHTS = ("w_ada", "b_ada", "norm_w", "w_in", "q_norm_w", "k_norm_w", "rel_bias", "sinks", "conv_w", "conv_b",
           "dt_bias", "a_log", "d_skip", "ssm_norm_w", "w_attn_proj", "w_ssm_proj", "w_out")


def kernel(x, c, w_ada, b_ada, norm_w, w_in, q_norm_w, k_norm_w, rel_bias, sinks, conv_w, conv_b, dt_bias, a_log, d_skip, ssm_norm_w, w_attn_proj, w_ssm_proj, w_out, loss_target, m_w_ada, m_b_ada, m_norm_w, m_w_in, m_q_norm_w, m_k_norm_w, m_rel_bias, m_sinks, m_conv_w, m_conv_b, m_dt_bias, m_a_log, m_d_skip, m_ssm_norm_w, m_w_attn_proj, m_w_ssm_proj, m_w_out, v_w_ada, v_b_ada, v_norm_w, v_w_in, v_q_norm_w, v_k_norm_w, v_rel_bias, v_sinks, v_conv_w, v_conv_b, v_dt_bias, v_a_log, v_d_skip, v_ssm_norm_w, v_w_attn_proj, v_w_ssm_proj, v_w_out):
    w = dict(w_ada=w_ada, b_ada=b_ada, norm_w=norm_w, w_in=w_in, q_norm_w=q_norm_w, k_norm_w=k_norm_w,
             rel_bias=rel_bias, sinks=sinks, conv_w=conv_w, conv_b=conv_b, dt_bias=dt_bias, a_log=a_log,
             d_skip=d_skip, ssm_norm_w=ssm_norm_w, w_attn_proj=w_attn_proj, w_ssm_proj=w_ssm_proj, w_out=w_out)
    m = dict(w_ada=m_w_ada, b_ada=m_b_ada, norm_w=m_norm_w, w_in=m_w_in, q_norm_w=m_q_norm_w, k_norm_w=m_k_norm_w,
             rel_bias=m_rel_bias, sinks=m_sinks, conv_w=m_conv_w, conv_b=m_conv_b, dt_bias=m_dt_bias, a_log=m_a_log,
             d_skip=m_d_skip, ssm_norm_w=m_ssm_norm_w, w_attn_proj=m_w_attn_proj, w_ssm_proj=m_w_ssm_proj, w_out=m_w_out)
    v = dict(w_ada=v_w_ada, b_ada=v_b_ada, norm_w=v_norm_w, w_in=v_w_in, q_norm_w=v_q_norm_w, k_norm_w=v_k_norm_w,
             rel_bias=v_rel_bias, sinks=v_sinks, conv_w=v_conv_w, conv_b=v_conv_b, dt_bias=v_dt_bias, a_log=v_a_log,
             d_skip=v_d_skip, ssm_norm_w=v_ssm_norm_w, w_attn_proj=v_w_attn_proj, w_ssm_proj=v_w_ssm_proj, w_out=v_w_out)
    me = 4 * lax.axis_index("x") + 2 * lax.axis_index("y") + lax.axis_index("c")
    ada_n = w_ada.shape[2]
    in_n = w_in.shape[2]
    cw_n = conv_w.shape[2]

    first = _ag_direct(jnp.concatenate([c, conv_w[0].reshape(1, CONV_K * cw_n)], axis=1), "ag_c")[:, 0]
    c_all = first[:, :D_MODEL]
    conv_w_full = first[:, D_MODEL:].reshape(N_DEV, CONV_K, cw_n).transpose(1, 0, 2).reshape(CONV_K, XBC_W)
    b_piece = lax.dynamic_slice_in_dim(b_ada, me * ada_n, ada_n, axis=1)
    mod_all = _ag_direct(_mod_piece(c_all, w_ada[0], b_piece), "ag_mod")
    mod = lax.dynamic_index_in_dim(mod_all, me, axis=1, keepdims=False).reshape(1, 3 * D_MODEL)
    shift, scale, gate = mod[:, :D_MODEL], mod[:, D_MODEL:2 * D_MODEL], mod[:, 2 * D_MODEL:]

    w_t, zero = _ag_two_level(w_in[0].T.astype(BF), "ag_w_in")
    w_t = w_t.reshape(N_DEV * in_n, D_MODEL)

    def with_mine(blocks, mine):
        return lax.dynamic_update_index_in_dim(jnp.zeros(blocks, mine.dtype), mine, me, axis=0)

    rows = jnp.concatenate([w_attn_proj[0], w_ssm_proj[0], w_out[0]], axis=0).astype(BF) + zero
    r_ap, r_sp = w_attn_proj.shape[1], w_ssm_proj.shape[1]
    rows_started, zero = _exchange_start(rows, with_mine((N_DEV,) + rows.shape, rows), True, "ag_rows_start")

    def rows_fn(after):
        rows_all = _exchange_wait(rows_started, after, True, "ag_rows_wait")
        return (rows_all[:, :r_ap].reshape(ATTN_W, D_MODEL), rows_all[:, r_ap:r_ap + r_sp].reshape(SSM_W, D_MODEL),
                rows_all[:, r_ap + r_sp:].reshape(D_MODEL, D_MODEL))

    started = {}

    def send_blocks(key, g, name):
        g = g.astype(BF)
        started[key], zero = _exchange_start(
            g, with_mine(g.shape, lax.dynamic_index_in_dim(g, me, axis=0, keepdims=False)), False, name)
        return zero

    def after_mid(g_wap, g_wsp, g_wout):
        return send_blocks("rows", jnp.concatenate(
            [g_wap.reshape(N_DEV, r_ap, D_MODEL), g_wsp.reshape(N_DEV, r_sp, D_MODEL),
             g_wout.reshape(N_DEV, r_ap, D_MODEL)], axis=1), "rs_rows_start")

    def after_gw(g_ws):
        return send_blocks("in", jnp.concatenate(g_ws, axis=0).reshape(N_DEV, in_n, D_MODEL), "rs_in_start")

    r = _local_step(x[0], loss_target[0], shift, scale + zero, gate, w_t, rows_fn, norm_w, q_norm_w, k_norm_w,
                    rel_bias, sinks, conv_w_full, conv_b, dt_bias, a_log, d_skip, ssm_norm_w, after_mid, after_gw)

    small = dict(b_ada=r["dmod"], norm_w=r["g_norm_w"], q_norm_w=r["g_qnw"], k_norm_w=r["g_knw"], rel_bias=r["g_rel"],
                 sinks=r["g_sinks"], conv_b=r["g_conv_b"], dt_bias=r["g_dt_bias"], a_log=r["g_a_log"],
                 d_skip=r["g_d_skip"], ssm_norm_w=r["g_ssm_nw"])
    pack_all = _ag_direct(_pack_partials(small, r["loss"], r["g_conv_w"]), "ag_small")
    res, loss = _adam_small(pack_all, w, m, v)
    loss = loss[0, 0]
    cw_parts = pack_all[:, 0, _CW_OFF:].reshape(N_DEV, CONV_K, XBC_W)
    cw_mine = lax.dynamic_slice_in_dim(cw_parts, me * cw_n, cw_n, axis=2)
    res["conv_w"] = [a[None] for a in _adam(cw_mine, conv_w[0], m_conv_w[0], v_conv_w[0], "adam_conv_w")]

    dmod_piece = lax.dynamic_slice_in_dim(pack_all[:, 0, :3 * D_MODEL], me * ada_n, ada_n, axis=1)
    g_ada = _gw_ada(c_all, dmod_piece)
    res["w_ada"] = [a[None] for a in _adam(g_ada[None], w_ada[0], m_w_ada[0], v_w_ada[0], "adam_w_ada")]

    cat = lambda d: jnp.concatenate([d["w_attn_proj"][0], d["w_ssm_proj"][0], d["w_out"][0]], axis=0)
    rows_res = _adam(_exchange_wait(started["rows"], g_ada, False, "rs_rows_wait"), cat(w), cat(m), cat(v), "adam_w_rows")
    res["w_in"] = [a.T[None] for a in _adam(_exchange_wait(started["in"], rows_res[0], False, "rs_in_wait"),
                                            w_in[0].T, m_w_in[0].T, v_w_in[0].T, "adam_w_in")]
    res["w_attn_proj"] = [a[None, :r_ap] for a in rows_res]
    res["w_ssm_proj"] = [a[None, r_ap:r_ap + r_sp] for a in rows_res]
    res["w_out"] = [a[None, r_ap + r_sp:] for a in rows_res]

    outs = [loss, r["grad_x"][None]]
    for j in range(4):
        outs += [res[name][j] for name in WEIGHTS]
    return tuple(outs)
```

```python
import functools
import math

import numpy as np
import jax
import jax.numpy as jnp
from jax import lax
from jax.experimental import pallas as pl
from jax.experimental.pallas import tpu as pltpu

F32 = jnp.float32
BF = jnp.bfloat16
HI = lax.Precision.HIGHEST

D_MODEL = 1024
ATTN_HEADS = 16
KV_HEADS = 4
GRP = ATTN_HEADS // KV_HEADS
HEAD_DIM = 64
ATTN_W = ATTN_HEADS * HEAD_DIM
KV_W = KV_HEADS * HEAD_DIM
BLOCK = 128
REL_BUCKETS = 32
REL_MAX_DIST = 128
SSM_W = 2048
SSM_P = 64
SSM_HEADS = 32
SSM_G = 4
SSM_R = 8
SSM_N = 128
CONV_K = 4
XBC_W = SSM_W + 2 * SSM_G * SSM_N
SEG_W = (ATTN_W, KV_W, KV_W, ATTN_W, SSM_W, XBC_W, SSM_HEADS, D_MODEL, D_MODEL)
SEG_OFF = tuple(int(v) for v in np.cumsum((0,) + SEG_W))
IN_W = SEG_OFF[-1]
GATE_SEGS = (3, 4, 7, 8)
EPS = 1e-6
N_DEV = 8
ADAM_LR, ADAM_B1, ADAM_B2, ADAM_EPS, ADAM_WD, ADAM_STEP = 0.001, 0.9, 0.999, 1e-08, 0.01, 10
VMEM_LIMIT = 60 * 1024 * 1024
MESH = pl.DeviceIdType.MESH
ANY = pl.BlockSpec(memory_space=pl.ANY)


def _dot(a, b, precision=None):
    return jnp.dot(a, b, preferred_element_type=F32, precision=precision)


def _dot_nt(a, b, precision=None):
    return lax.dot_general(a, b, (((1,), (1,)), ((), ())), preferred_element_type=F32, precision=precision)


def _dot_tn(a, b, precision=None):
    return lax.dot_general(a, b, (((0,), (0,)), ((), ())), preferred_element_type=F32, precision=precision)


def _bf(a):
    return a.astype(BF)


def _sig(a):
    return 0.5 * jnp.tanh(0.5 * a) + 0.5


def _params(**kw):
    return pltpu.CompilerParams(vmem_limit_bytes=VMEM_LIMIT, **kw)


def _full(shape):
    nd = len(shape)
    return pl.BlockSpec(shape, lambda i: (0,) * nd)


def _rows(tm, w):
    return pl.BlockSpec((tm, w), lambda i: (i, 0))


def _inproj(x, norm_w, scale, shift, w_t, tm=256):
    s = x.shape[0]

    def body(x_ref, nw_ref, sc_ref, sh_ref, w_hbm, *rest):
        outs, h_ref, w_vm, sem = rest[:9], rest[9], rest[10], rest[11]
        first = pl.program_id(0) == 0
        cps = [pltpu.make_async_copy(w_hbm.at[SEG_OFF[j]:SEG_OFF[j + 1], :], w_vm.at[SEG_OFF[j]:SEG_OFF[j + 1], :], sem.at[j])
               for j in range(9)]

        def tile(waiting):
            xv = x_ref[...]
            r = lax.rsqrt(jnp.mean(xv * xv, axis=-1, keepdims=True) + EPS)
            h = xv * r * (nw_ref[...] * (1.0 + sc_ref[...])) + sh_ref[...]
            hb = _bf(h)
            h_ref[...] = hb
            for j in range(9):
                if waiting:
                    cps[j].wait()
                outs[j][...] = _dot_nt(hb, w_vm[SEG_OFF[j]:SEG_OFF[j + 1], :]).astype(outs[j].dtype)

        @pl.when(first)
        def _():
            for cp in cps:
                cp.start()
            tile(True)

        @pl.when(jnp.logical_not(first))
        def _():
            tile(False)

    vec = _full((1, D_MODEL))
    return pl.pallas_call(
        body, name="inproj", grid=(s // tm,),
        in_specs=[_rows(tm, D_MODEL), vec, vec, vec, ANY],
        out_specs=[_rows(tm, w) for w in SEG_W] + [_rows(tm, D_MODEL)],
        out_shape=[jax.ShapeDtypeStruct((s, w), BF if j in GATE_SEGS else F32) for j, w in enumerate(SEG_W)]
                  + [jax.ShapeDtypeStruct((s, D_MODEL), BF)],
        scratch_shapes=[pltpu.VMEM((IN_W, D_MODEL), BF), pltpu.SemaphoreType.DMA((9,))],
        compiler_params=_params(dimension_semantics=("arbitrary",)),
    )(x, norm_w, scale, shift, w_t)


def _bucket_onehot_t():
    qi = jnp.arange(BLOCK)[:, None]
    kj = jnp.arange(2 * BLOCK)[None, :]
    dist = qi + BLOCK - kj
    n = jnp.maximum(dist, 0)
    max_exact = REL_BUCKETS // 2
    nf = jnp.maximum(n, 1).astype(F32)
    large = max_exact + (jnp.log(nf / max_exact) / math.log(REL_MAX_DIST / max_exact)
                         * (REL_BUCKETS - max_exact)).astype(jnp.int32)
    large = jnp.minimum(large, REL_BUCKETS - 1)
    bucket = jnp.where(n < max_exact, n, large).reshape(1, BLOCK * 2 * BLOCK)
    return (bucket == jnp.arange(REL_BUCKETS)[:, None]).astype(F32)


def _bias_dense(rel_bias_t, oh_t):
    def body(rb_ref, oh_ref, o_ref):
        o_ref[...] = _dot(rb_ref[...], oh_ref[...], HI)

    return pl.pallas_call(
        body, name="bias_dense", out_shape=jax.ShapeDtypeStruct((ATTN_HEADS, BLOCK * 2 * BLOCK), F32),
        compiler_params=_params(),
    )(rel_bias_t, oh_t)


def _bias_grad(ds_sum, oh_t):
    def body(ds_ref, oh_ref, o_ref):
        o_ref[...] = _dot_nt(ds_ref[...], oh_ref[...], HI)

    return pl.pallas_call(
        body, name="bias_grad", out_shape=jax.ShapeDtypeStruct((ATTN_HEADS, REL_BUCKETS), F32),
        compiler_params=_params(),
    )(ds_sum, oh_t)


def _group_sum(a, e):
    hi = _bf(a)
    return _dot(hi, e) + _dot(_bf(a - hi.astype(F32)), e)


def _group_bcast(a, e3t):
    hi = _bf(a)
    r1 = a - hi.astype(F32)
    mid = _bf(r1)
    return _dot(jnp.concatenate([hi, mid, _bf(r1 - mid.astype(F32))], axis=1), e3t)


def _membership(width, group, ngroups):
    e = (jnp.arange(width)[:, None] // group == jnp.arange(ngroups)[None, :]).astype(BF)
    return e, jnp.tile(e.T, (3, 1))


def _fold(width, group):
    return (jnp.arange(width)[:, None] % group == jnp.arange(group)[None, :]).astype(BF)


def _heads_norm(t, w_x, e, e3t):
    r = lax.rsqrt(_group_sum(t * t, e) * (1.0 / HEAD_DIM) + EPS)
    r_x = _group_bcast(r, e3t)
    return t * r_x * w_x, r_x


def _heads_norm_bwd(t, r_x, w_x, d, e, e3t):
    wd = d * w_x
    corr = _group_bcast(_group_sum(t * wd, e) * (1.0 / HEAD_DIM), e3t)
    return r_x * wd - t * (r_x * r_x * r_x) * corr, jnp.sum(d * t * r_x, axis=0, keepdims=True)


def _stack_heads(a, hk):
    return jnp.concatenate([a[:, (hk * GRP + g) * HEAD_DIM:(hk * GRP + g + 1) * HEAD_DIM] for g in range(GRP)], axis=0)


def _stack_cols(a, hk):
    return jnp.concatenate([a[:, hk * GRP + g:hk * GRP + g + 1] for g in range(GRP)], axis=0)


def _window_mask(first):
    qi = jnp.bitwise_and(lax.broadcasted_iota(jnp.int32, (GRP * BLOCK, 2 * BLOCK), 0), BLOCK - 1)
    kj = lax.broadcasted_iota(jnp.int32, (GRP * BLOCK, 2 * BLOCK), 1)
    prev_ok = jnp.logical_and(kj > qi, jnp.logical_not(first))
    cur_ok = jnp.logical_and(kj >= BLOCK, kj - BLOCK <= qi)
    return jnp.logical_or(jnp.logical_and(kj < BLOCK, prev_ok), cur_ok)


def _attn_consts(qnw, knw):
    eq, eq3t = _membership(ATTN_W, HEAD_DIM, ATTN_HEADS)
    ek, ek3t = _membership(KV_W, HEAD_DIM, ATTN_HEADS)
    return (jnp.tile(qnw, (1, ATTN_HEADS)), jnp.tile(knw, (1, KV_HEADS)), eq, eq3t, ek, ek3t)


def _attn_fwd(q, k, v, bias, sinks, consts):
    s = q.shape[0]
    nb = s // BLOCK
    gq = GRP * BLOCK
    bias_t = bias.reshape(KV_HEADS, GRP, BLOCK, 2 * BLOCK).transpose(0, 3, 1, 2).reshape(KV_HEADS, 2 * BLOCK, gq)
    sink_rows = jnp.repeat(sinks.reshape(KV_HEADS, GRP), BLOCK, axis=1).reshape(KV_HEADS, 1, gq)
    eye = jnp.eye(BLOCK, dtype=BF)

    def body(q_ref, kp_ref, kc_ref, vp_ref, vc_ref, b_ref, bt_ref, sk_ref, skr_ref, eye_ref,
             qw_ref, kw_ref, eq_ref, eq3_ref, ek_ref, ek3_ref, o_ref, lse_ref):
        i = pl.program_id(0)
        mask = _window_mask(i == 0)
        kj = lax.broadcasted_iota(jnp.int32, (2 * BLOCK, gq), 0)
        qi = jnp.bitwise_and(lax.broadcasted_iota(jnp.int32, (2 * BLOCK, gq), 1), BLOCK - 1)
        mask_t = jnp.logical_or(jnp.logical_and(kj < BLOCK, jnp.logical_and(kj > qi, i > 0)),
                                jnp.logical_and(kj >= BLOCK, kj - BLOCK <= qi))
        qn = _bf(_heads_norm(q_ref[...], qw_ref[...], eq_ref[...], eq3_ref[...])[0])
        kn = _bf(_heads_norm(jnp.concatenate([kp_ref[...], kc_ref[...]], axis=0), kw_ref[...], ek_ref[...], ek3_ref[...])[0])
        vv = _bf(jnp.concatenate([vp_ref[...], vc_ref[...]], axis=0))
        ones = jnp.ones((2 * BLOCK, HEAD_DIM), BF)
        lses = []
        kss = [slice(hk * HEAD_DIM, (hk + 1) * HEAD_DIM) for hk in range(KV_HEADS)]
        qgs = [_stack_heads(qn, hk) for hk in range(KV_HEADS)]
        sc_ts = [jnp.where(mask_t, _dot_nt(kn[:, kss[hk]], qgs[hk]) * (HEAD_DIM ** -0.5) + bt_ref[hk], -1e30)
                 for hk in range(KV_HEADS)]
        m8s = [_bf(jnp.broadcast_to(jnp.maximum(jnp.max(sc_ts[hk], axis=0, keepdims=True), skr_ref[hk]), (8, gq)))
               for hk in range(KV_HEADS)]
        ms = [jnp.concatenate([_dot_nt(eye_ref[...], m8[:, g * BLOCK:(g + 1) * BLOCK])[:, 0:1] for g in range(GRP)], axis=0)
              for m8 in m8s]
        scs = [_dot_nt(qgs[hk], kn[:, kss[hk]]) * (HEAD_DIM ** -0.5)
               + b_ref[hk * GRP:(hk + 1) * GRP].reshape(gq, 2 * BLOCK) for hk in range(KV_HEADS)]
        ps = [_bf(jnp.exp(jnp.where(mask, scs[hk], -1e30) - ms[hk])) for hk in range(KV_HEADS)]
        pvs = [_dot(ps[hk], jnp.concatenate([vv[:, kss[hk]], ones], axis=1)) for hk in range(KV_HEADS)]
        for hk in range(KV_HEADS):
            m, pv = ms[hk], pvs[hk]
            sink = jnp.concatenate([jnp.full((BLOCK, 1), sk_ref[0, hk * GRP + g], F32) for g in range(GRP)], axis=0)
            den = pv[:, HEAD_DIM:HEAD_DIM + 1] + jnp.exp(sink - m)
            out = pv[:, :HEAD_DIM] * (1.0 / den)
            lse = m + jnp.log(den)
            for g in range(GRP):
                h = hk * GRP + g
                o_ref[:, h * HEAD_DIM:(h + 1) * HEAD_DIM] = out[g * BLOCK:(g + 1) * BLOCK]
                lses.append(lse[g * BLOCK:(g + 1) * BLOCK])
        lse_ref[...] = jnp.concatenate(lses, axis=1)

    cur = lambda w: pl.BlockSpec((BLOCK, w), lambda i: (i, 0))
    prev = lambda w: pl.BlockSpec((BLOCK, w), lambda i: (jnp.maximum(i - 1, 0), 0))
    whole = lambda a: pl.BlockSpec(a.shape, lambda i: (0,) * a.ndim)
    return pl.pallas_call(
        body, name="attn_fwd", grid=(nb,),
        in_specs=[cur(ATTN_W), prev(KV_W), cur(KV_W), prev(KV_W), cur(KV_W), whole(bias), whole(bias_t),
                  pl.BlockSpec(memory_space=pltpu.SMEM), whole(sink_rows), whole(eye)] + [_full(c.shape) for c in consts],
        out_specs=[cur(ATTN_W), cur(ATTN_HEADS)],
        out_shape=[jax.ShapeDtypeStruct((s, ATTN_W), F32), jax.ShapeDtypeStruct((s, ATTN_HEADS), F32)],
        compiler_params=_params(dimension_semantics=("arbitrary",)),
    )(q, k, k, v, v, bias, bias_t, sinks, sink_rows, eye, *consts)


def _conv_taps(xbc, tail):
    ext = jnp.concatenate([tail, xbc], axis=0)
    return [pltpu.roll(ext, CONV_K - 1 - j, axis=0)[8:8 + BLOCK] if j < CONV_K - 1 else xbc for j in range(CONV_K)]


def _softplus(u):
    return jnp.maximum(u, 0.0) + jnp.log(1.0 + jnp.exp(-jnp.abs(u)))


def _tril():
    r = lax.broadcasted_iota(jnp.int32, (BLOCK, BLOCK), 0)
    c = lax.broadcasted_iota(jnp.int32, (BLOCK, BLOCK), 1)
    return r >= c


def _triu():
    r = lax.broadcasted_iota(jnp.int32, (BLOCK, BLOCK), 0)
    c = lax.broadcasted_iota(jnp.int32, (BLOCK, BLOCK), 1)
    return r <= c


def _exact_left(m01, a):
    hi = _bf(a)
    r1 = a - hi.astype(F32)
    mid = _bf(r1)
    return _dot(m01, hi) + _dot(m01, mid) + _dot(m01, _bf(r1 - mid.astype(F32)))


def _ssd_common(conv, dtr, dtb_ref, alog_ref, e3_ref):
    sg = _sig(conv)
    xact = conv * sg
    u = dtr + dtb_ref[...]
    dt = _softplus(u)
    a = -jnp.exp(alog_ref[...])
    trilb = _tril()
    acum = _exact_left(trilb.astype(BF), dt * a)
    both = _group_bcast(jnp.concatenate([dt, acum], axis=0), e3_ref[...])
    dt_x, acum_x = both[:BLOCK], both[BLOCK:]
    return sg, xact, u, dt, a, trilb, acum, dt_x, acum_x


SSD_CH = 2


def _ssd_fwd(xbc, dt_raw, conv_w, conv_b, dt_bias, a_log, dsk_x, e3t):
    s = xbc.shape[0]
    nc = s // BLOCK
    ch = SSD_CH if nc % SSD_CH == 0 else 1
    rows = ch * BLOCK

    def body(x_ref, tail_ref, dtr_ref, cw_ref, cb_ref, dtb_ref, alog_ref, dsk_ref, e3_ref,
             y_ref, hp_ref, conv_ref, hst, yd_s, yoff_s):
        i = pl.program_id(0)

        @pl.when(i == 0)
        def _():
            hst[...] = jnp.zeros_like(hst)

        for j in range(ch):
            rs = slice(j * BLOCK, (j + 1) * BLOCK)
            tail = jnp.where(i > 0, tail_ref[...], 0.0) if j == 0 else x_ref[j * BLOCK - 8:j * BLOCK, :]
            taps = _conv_taps(x_ref[rs, :], tail)
            conv = cb_ref[...] + sum(taps[t] * cw_ref[t:t + 1, :] for t in range(CONV_K))
            conv_ref[rs, :] = conv
            _, xact, _, _, _, trilb, acum, dt_x, acum_x = _ssd_common(conv, dtr_ref[rs, :], dtb_ref, alog_ref, e3_ref)
            xs = xact[:, :SSM_W]
            acum_t = acum.T
            ea_x = jnp.exp(acum_x)
            last_x = acum_x[BLOCK - 1:BLOCK, :]
            xdt = xs * dt_x
            xw = xdt * jnp.exp(last_x - acum_x)
            cd_x = jnp.exp(last_x)
            hprev = hst[...]
            hp_ref[j] = hprev
            for g in range(SSM_G):
                bg = _bf(xact[:, SSM_W + g * SSM_N:SSM_W + (g + 1) * SSM_N])
                cg = _bf(xact[:, SSM_W + SSM_G * SSM_N + g * SSM_N:SSM_W + SSM_G * SSM_N + (g + 1) * SSM_N])
                sl = slice(g * SSM_R * SSM_P, (g + 1) * SSM_R * SSM_P)
                cb = _dot_nt(cg, bg)
                yoff_s[:, sl] = _dot(cg, _bf(hprev[:, sl])) * ea_x[:, sl]
                hst[:, sl] = hprev[:, sl] * cd_x[:, sl] + _dot_tn(bg, _bf(xw[:, sl]))
                hss = [slice((g * SSM_R + r) * SSM_P, (g * SSM_R + r + 1) * SSM_P) for r in range(SSM_R)]
                mms = [_bf(cb * jnp.exp(jnp.where(trilb, acum[:, g * SSM_R + r:g * SSM_R + r + 1]
                                                  - acum_t[g * SSM_R + r:g * SSM_R + r + 1, :], -1e30)))
                       for r in range(SSM_R)]
                for r in range(SSM_R):
                    yd_s[:, hss[r]] = _dot(mms[r], _bf(xdt[:, hss[r]]))
            y_ref[rs, :] = yd_s[...] + yoff_s[...] + dsk_ref[...] * xs

    blk = lambda w: pl.BlockSpec((rows, w), lambda i: (i, 0))
    return pl.pallas_call(
        body, name="ssd_fwd", grid=(nc // ch,),
        in_specs=[blk(XBC_W), pl.BlockSpec((8, XBC_W), lambda i: (jnp.maximum(i * (rows // 8) - 1, 0), 0)),
                  blk(SSM_HEADS), _full((CONV_K, XBC_W)), _full((1, XBC_W)), _full((1, SSM_HEADS)),
                  _full((1, SSM_HEADS)), _full((1, SSM_W)), _full((3 * SSM_HEADS, SSM_W))],
        out_specs=[blk(SSM_W), pl.BlockSpec((ch, SSM_N, SSM_W), lambda i: (i, 0, 0)), blk(XBC_W)],
        out_shape=[jax.ShapeDtypeStruct((s, SSM_W), F32), jax.ShapeDtypeStruct((nc, SSM_N, SSM_W), F32),
                   jax.ShapeDtypeStruct((s, XBC_W), F32)],
        scratch_shapes=[pltpu.VMEM((SSM_N, SSM_W), F32), pltpu.VMEM((BLOCK, SSM_W), F32), pltpu.VMEM((BLOCK, SSM_W), F32)],
        compiler_params=_params(dimension_semantics=("arbitrary",)),
    )(xbc, xbc, dt_raw, conv_w, conv_b, dt_bias, a_log, dsk_x, e3t)


def _dsilu(z, sg):
    return sg * (1.0 + z * (1.0 - sg))


def _mid(x, tgt, o_att, za, ypre, zm, ga, gb, gate, ssm_nw, rows_all, tm=256):
    s = x.shape[0]
    gw = SSM_W // SSM_G

    r_ap, r_sp = ATTN_W // N_DEV, SSM_W // N_DEV

    def body(x_ref, t_ref, o_ref, za_ref, yp_ref, zm_ref, ga_ref, gb_ref, gate_ref, nw_ref, rows_h,
             dout_ref, do_ref, dza_ref, dyp_ref, dzm_ref, dga_ref, dgb_ref,
             yag_ref, dya_ref, yn_ref, dyb_ref, mg_ref, dob_ref, gnw_ref, dgate_ref, loss_ref,
             wap_v, wsp_v, wout_v, sem):
        i = pl.program_id(0)

        @pl.when(i == 0)
        def _():
            cps = []
            for d in range(N_DEV):
                for j, (dst, r0, rn) in enumerate(((wap_v, 0, r_ap), (wsp_v, r_ap, r_sp), (wout_v, r_ap + r_sp, r_ap))):
                    cps.append(pltpu.make_async_copy(rows_h.at[d, r0:r0 + rn, :], dst.at[d * rn:(d + 1) * rn, :], sem.at[j]))
            for cp in cps:
                cp.start()
            gnw_ref[...] = jnp.zeros_like(gnw_ref)
            dgate_ref[...] = jnp.zeros_like(dgate_ref)
            loss_ref[...] = jnp.zeros_like(loss_ref)
            for cp in cps:
                cp.wait()

        gate = gate_ref[...]
        nw = nw_ref[...]
        o_att = o_ref[...]
        z_a = za_ref[...].astype(F32)
        s_a = _sig(z_a)
        silu_a = z_a * s_a
        yag = _bf(o_att * silu_a)
        yag_ref[...] = yag
        y_a = _dot(yag, wap_v[...])
        ypre = yp_ref[...]
        z_m = zm_ref[...].astype(F32)
        s_m = _sig(z_m)
        silu_m = z_m * s_m
        yg = ypre * silu_m
        rinv = jnp.concatenate(
            [jnp.broadcast_to(lax.rsqrt(jnp.mean(yg[:, g * gw:(g + 1) * gw] ** 2, axis=-1, keepdims=True) + EPS), (tm, gw))
             for g in range(SSM_G)], axis=1)
        ynr = yg * rinv
        yn = _bf(ynr * nw)
        yn_ref[...] = yn
        y_b = _dot(yn, wsp_v[...])
        g_a = _sig(ga_ref[...].astype(F32))
        g_b = _sig(gb_ref[...].astype(F32))
        merged = _bf(g_a * y_a + g_b * y_b)
        mg_ref[...] = merged
        o = _dot(merged, wout_v[...])
        diff = x_ref[...] + gate * o - t_ref[...]
        loss_ref[...] += (0.5 / D_MODEL) * jnp.sum(diff * diff, axis=(0, 1), keepdims=True)
        dout = diff * (1.0 / D_MODEL)
        dout_ref[...] = dout
        dgate_ref[...] += jnp.sum(dout * o, axis=0, keepdims=True)
        d_o = _bf(dout * gate)
        dob_ref[...] = d_o
        dmerged = _dot_nt(d_o, wout_v[...])
        dy_a = dmerged * g_a
        dy_b = dmerged * g_b
        dga_ref[...] = _bf(dy_a * y_a * (1.0 - g_a))
        dgb_ref[...] = _bf(dy_b * y_b * (1.0 - g_b))
        dy_a = _bf(dy_a)
        dy_b = _bf(dy_b)
        dya_ref[...] = dy_a
        dyb_ref[...] = dy_b
        dyag = _dot_nt(dy_a, wap_v[...])
        do_ref[...] = dyag * silu_a
        dza_ref[...] = _bf(dyag * o_att * _dsilu(z_a, s_a))
        dyn = _dot_nt(dy_b, wsp_v[...])
        gnw_ref[...] += jnp.sum(dyn * ynr, axis=0, keepdims=True)
        dynw = dyn * nw
        corr = jnp.concatenate(
            [jnp.broadcast_to(jnp.mean((dynw * ynr)[:, g * gw:(g + 1) * gw], axis=-1, keepdims=True), (tm, gw))
             for g in range(SSM_G)], axis=1)
        dyg = rinv * (dynw - ynr * corr)
        dyp_ref[...] = dyg * silu_m
        dzm_ref[...] = _bf(dyg * ypre * _dsilu(z_m, s_m))

    r1, r2 = _rows(tm, D_MODEL), _rows(tm, SSM_W)
    sd = jax.ShapeDtypeStruct
    return pl.pallas_call(
        body, name="mid", grid=(s // tm,),
        in_specs=[r1, r1, r1, r1, r2, r2, r1, r1, _full((1, D_MODEL)), _full((1, SSM_W)), ANY],
        out_specs=[r1, r1, r1, r2, r2, r1, r1, r1, r1, r2, r1, r1, r1,
                   _full((1, SSM_W)), _full((1, D_MODEL)), _full((1, 1))],
        out_shape=[sd((s, D_MODEL), F32), sd((s, ATTN_W), F32), sd((s, ATTN_W), BF), sd((s, SSM_W), F32),
                   sd((s, SSM_W), BF), sd((s, D_MODEL), BF), sd((s, D_MODEL), BF),
                   sd((s, ATTN_W), BF), sd((s, D_MODEL), BF), sd((s, SSM_W), BF), sd((s, D_MODEL), BF),
                   sd((s, D_MODEL), BF), sd((s, D_MODEL), BF),
                   sd((1, SSM_W), F32), sd((1, D_MODEL), F32), sd((1, 1), F32)],
        scratch_shapes=[pltpu.VMEM((ATTN_W, D_MODEL), BF), pltpu.VMEM((SSM_W, D_MODEL), BF), pltpu.VMEM((D_MODEL, D_MODEL), BF),
                        pltpu.SemaphoreType.DMA((3,))],
        compiler_params=_params(dimension_semantics=("arbitrary",)),
    )(x, tgt, o_att, za, ypre, zm, ga, gb, gate, ssm_nw, rows_all)


def _attn_bwd(q, k, v, bias, sinks, consts, o_att, lse, d_o):
    s = q.shape[0]
    nb = s // BLOCK
    folds = (_fold(ATTN_W, HEAD_DIM), _fold(KV_W, HEAD_DIM))

    def body(q_ref, kp_ref, kc_ref, vp_ref, vc_ref, b_ref, skv_ref, qw_ref, kw_ref, eq_ref, eq3_ref, ek_ref, ek3_ref,
             fq_ref, fk_ref, o_ref, lse_ref, do_ref,
             dq_ref, dk_ref, dv_ref, dss_ref, gqw_ref, gkw_ref, gsk_ref, ckn, cv, dqn_s, dkn_s, dv_s, gq_x, gk_x):
        i = pl.program_id(0)
        kw, ek, ek3 = kw_ref[...], ek_ref[...], ek3_ref[...]

        @pl.when(i == 0)
        def _():
            for ref in (ckn, cv, dss_ref, gq_x, gk_x, gsk_ref):
                ref[...] = jnp.zeros_like(ref)

        @pl.when(i < nb)
        def _():
            mask = _window_mask(i == 0)
            qw, eq, eq3 = qw_ref[...], eq_ref[...], eq3_ref[...]
            qf = q_ref[...]
            qnf, rq_x = _heads_norm(qf, qw, eq, eq3)
            qn = _bf(qnf)
            kf = jnp.concatenate([kp_ref[...], kc_ref[...]], axis=0)
            knf, rk_x = _heads_norm(kf, kw, ek, ek3)
            kn = _bf(knf)
            vv = _bf(jnp.concatenate([vp_ref[...], vc_ref[...]], axis=0))
            d_of = do_ref[...]
            d_ob = _bf(d_of)
            lse_all = lse_ref[...]
            delta = _group_sum(d_of * o_ref[...], eq)
            gsk_ref[...] += jnp.sum(-jnp.exp(skv_ref[...] - lse_all) * delta, axis=0, keepdims=True)
            kss = [slice(hk * HEAD_DIM, (hk + 1) * HEAD_DIM) for hk in range(KV_HEADS)]
            qgs = [_stack_heads(qn, hk) for hk in range(KV_HEADS)]
            d_ogs = [_stack_heads(d_ob, hk) for hk in range(KV_HEADS)]
            scs = [_dot_nt(qgs[hk], kn[:, kss[hk]]) * (HEAD_DIM ** -0.5)
                   + b_ref[hk * GRP:(hk + 1) * GRP].reshape(GRP * BLOCK, 2 * BLOCK) for hk in range(KV_HEADS)]
            dps = [_dot_nt(d_ogs[hk], vv[:, kss[hk]]) for hk in range(KV_HEADS)]
            ps = [jnp.where(mask, jnp.exp(scs[hk] - _stack_cols(lse_all, hk)), 0.0) for hk in range(KV_HEADS)]
            dss = [ps[hk] * (dps[hk] - _stack_cols(delta, hk)) for hk in range(KV_HEADS)]
            pbs = [_bf(p) for p in ps]
            dsbs = [_bf(ds) for ds in dss]
            for hk in range(KV_HEADS):
                dss_ref[hk * GRP:(hk + 1) * GRP] += dss[hk].reshape(GRP, BLOCK, 2 * BLOCK)
            for hk in range(KV_HEADS):
                dv_s[:, kss[hk]] = _dot_tn(pbs[hk], d_ogs[hk])
                dkn_s[:, kss[hk]] = _dot_tn(dsbs[hk], qgs[hk]) * (HEAD_DIM ** -0.5)
            dqns = [_dot(dsbs[hk], kn[:, kss[hk]]) * (HEAD_DIM ** -0.5) for hk in range(KV_HEADS)]
            for hk in range(KV_HEADS):
                for g in range(GRP):
                    h = hk * GRP + g
                    dqn_s[:, h * HEAD_DIM:(h + 1) * HEAD_DIM] = dqns[hk][g * BLOCK:(g + 1) * BLOCK]
            dq, gq = _heads_norm_bwd(qf, rq_x, qw, dqn_s[...], eq, eq3)
            dq_ref[...] = _bf(dq)
            gq_x[...] += gq
            dk, gk = _heads_norm_bwd(kf[:BLOCK], rk_x[:BLOCK], kw, ckn[...] + dkn_s[0:BLOCK, :], ek, ek3)
            dk_ref[...] = _bf(dk)
            gk_x[...] += gk
            dv_ref[...] = _bf(cv[...] + dv_s[0:BLOCK, :])
            ckn[...] = dkn_s[BLOCK:2 * BLOCK, :]
            cv[...] = dv_s[BLOCK:2 * BLOCK, :]

        @pl.when(i == nb)
        def _():
            kc = kc_ref[...]
            dk, gk = _heads_norm_bwd(kc, _heads_norm(kc, kw, ek, ek3)[1], kw, ckn[...], ek, ek3)
            dk_ref[...] = _bf(dk)
            dv_ref[...] = _bf(cv[...])
            gqw_ref[...] = _group_sum(jnp.broadcast_to(gq_x[...], (8, ATTN_W)), fq_ref[...])[0:1]
            gkw_ref[...] = _group_sum(jnp.broadcast_to(gk_x[...] + gk, (8, KV_W)), fk_ref[...])[0:1]

    last = nb - 1
    cur = lambda w: pl.BlockSpec((BLOCK, w), lambda i: (jnp.minimum(i, last), 0))
    prev = lambda w: pl.BlockSpec((BLOCK, w), lambda i: (jnp.maximum(jnp.minimum(i, last) - 1, 0), 0))
    late = lambda w: pl.BlockSpec((BLOCK, w), lambda i: (jnp.maximum(i - 1, 0), 0))
    sd = jax.ShapeDtypeStruct
    return pl.pallas_call(
        body, name="attn_bwd", grid=(nb + 1,),
        in_specs=[cur(ATTN_W), prev(KV_W), cur(KV_W), prev(KV_W), cur(KV_W),
                  pl.BlockSpec((ATTN_HEADS, BLOCK, 2 * BLOCK), lambda i: (0, 0, 0)), _full((1, ATTN_HEADS))]
                 + [_full(c.shape) for c in consts + folds] + [cur(ATTN_W), cur(ATTN_HEADS), cur(ATTN_W)],
        out_specs=[cur(ATTN_W), late(KV_W), late(KV_W),
                   pl.BlockSpec((ATTN_HEADS, BLOCK, 2 * BLOCK), lambda i: (0, 0, 0)),
                   _full((1, HEAD_DIM)), _full((1, HEAD_DIM)), _full((1, ATTN_HEADS))],
        out_shape=[sd((s, ATTN_W), BF), sd((s, KV_W), BF), sd((s, KV_W), BF),
                   sd((ATTN_HEADS, BLOCK, 2 * BLOCK), F32), sd((1, HEAD_DIM), F32), sd((1, HEAD_DIM), F32),
                   sd((1, ATTN_HEADS), F32)],
        scratch_shapes=[pltpu.VMEM((BLOCK, KV_W), F32), pltpu.VMEM((BLOCK, KV_W), F32),
                        pltpu.VMEM((BLOCK, ATTN_W), F32), pltpu.VMEM((2 * BLOCK, KV_W), F32),
                        pltpu.VMEM((2 * BLOCK, KV_W), F32), pltpu.VMEM((1, ATTN_W), F32), pltpu.VMEM((1, KV_W), F32)],
        compiler_params=_params(dimension_semantics=("arbitrary",)),
    )(q, k, k, v, v, bias, sinks, *consts, *folds, o_att, lse, d_o)


def _ssd_bwd(xbc, conv_all, dt_raw, conv_w, dt_bias, a_log, dsk_x, e_mat, e3t, hprev_all, dy_all):
    s = xbc.shape[0]
    nc = s // BLOCK
    ch = 1
    rows = ch * BLOCK
    nsteps = nc // ch
    gw = SSM_R * SSM_P
    b0, c0 = SSM_W, SSM_W + SSM_G * SSM_N

    def body(x_ref, conv_ref, dtr_ref, cw_ref, dtb_ref, alog_ref, dsk_ref, e_ref, e3_ref, hp_ref, dy_ref,
             dx_ref, ddt_ref, gcw_ref, gcb_ref, gdtb_ref, galog_ref, gdsk_ref,
             dh, nhead, gdskx, dxdt_s, dbc_s, dxd_s):
        def chunk_bwd(j):
            rs = slice(j * BLOCK, (j + 1) * BLOCK)
            conv = conv_ref[rs, :]
            sg, xact, u, dt, a, trilb, acum, dt_x, acum_x = _ssd_common(conv, dtr_ref[rs, :], dtb_ref, alog_ref, e3_ref)
            xs = xact[:, :SSM_W]
            acum_t = acum.T
            ea_x = jnp.exp(acum_x)
            last_x = acum_x[BLOCK - 1:BLOCK, :]
            dte_x = jnp.exp(last_x - acum_x)
            cd_x = jnp.exp(last_x)
            xdt = xs * dt_x
            xw = xdt * dte_x
            hprev = hp_ref[j]
            dhn = dh[...]
            dy = dy_ref[rs, :]
            gdskx[...] += jnp.sum(dy * xs, axis=0, keepdims=True)
            dyea = dy * ea_x
            lane = lax.broadcasted_iota(jnp.int32, (BLOCK, SSM_HEADS), 1)
            dacum = jnp.zeros((BLOCK, SSM_HEADS), F32)
            dacc_x, dlast_x = [], []
            sls = [slice(g * gw, (g + 1) * gw) for g in range(SSM_G)]
            bgs = [_bf(xact[:, b0 + g * SSM_N:b0 + (g + 1) * SSM_N]) for g in range(SSM_G)]
            cgs = [_bf(xact[:, c0 + g * SSM_N:c0 + (g + 1) * SSM_N]) for g in range(SSM_G)]
            hpgs = [_bf(hprev[:, sl]) for sl in sls]
            dhgs = [_bf(dhn[:, sl]) for sl in sls]
            dyeags = [_bf(dyea[:, sl]) for sl in sls]
            xwgs = [_bf(xw[:, sl]) for sl in sls]
            cbs = [_dot_nt(cgs[g], bgs[g]) for g in range(SSM_G)]
            gmats = [_dot(cgs[g], hpgs[g]) for g in range(SSM_G)]
            dxws = [_dot(bgs[g], dhgs[g]) for g in range(SSM_G)]
            dcgs = [_dot_nt(dyeags[g], hpgs[g]) for g in range(SSM_G)]
            dbgs = [_dot_nt(xwgs[g], dhgs[g]) for g in range(SSM_G)]
            for g in range(SSM_G):
                sl = sls[g]
                dh[:, sl] = dhn[:, sl] * cd_x[:, sl] + _dot_tn(cgs[g], dyeags[g])
                dxdt_s[:, sl] = dxws[g] * dte_x[:, sl]
                dacc_x.append(dy[:, sl] * gmats[g] * ea_x[:, sl] - dxws[g] * xw[:, sl])
                dlast_x.append(jnp.sum(dxws[g] * xw[:, sl], axis=0, keepdims=True)
                               + jnp.sum(dhn[:, sl] * hprev[:, sl], axis=0, keepdims=True) * cd_x[:, sl])
            for g in range(SSM_G):
                bg, cg, cb, dbg, dcg = bgs[g], cgs[g], cbs[g], dbgs[g], dcgs[g]
                hss = [slice((g * SSM_R + r) * SSM_P, (g * SSM_R + r + 1) * SSM_P) for r in range(SSM_R)]
                lms = [jnp.exp(jnp.where(trilb, acum[:, g * SSM_R + r:g * SSM_R + r + 1]
                                         - acum_t[g * SSM_R + r:g * SSM_R + r + 1, :], -1e30)) for r in range(SSM_R)]
                mms = [cb * lm for lm in lms]
                dyhs = [_bf(dy[:, hs]) for hs in hss]
                dms = [_dot_nt(dyhs[r], _bf(xdt[:, hss[r]])) for r in range(SSM_R)]
                for r in range(SSM_R):
                    dxd_s[:, hss[r]] = _dot_tn(_bf(mms[r]), dyhs[r])
                dcb = sum(dms[r] * lms[r] for r in range(SSM_R))
                wms = [dms[r] * mms[r] for r in range(SSM_R)]
                antis = [wm - wm.T for wm in wms]
                for r in range(SSM_R):
                    dacum = dacum + _group_sum(antis[r], (lane == g * SSM_R + r).astype(BF))
                dcbb = _bf(dcb)
                dbc_s[:, g * SSM_N:(g + 1) * SSM_N] = dbg + _dot_tn(dcbb, cg)
                dbc_s[:, SSM_G * SSM_N + g * SSM_N:SSM_G * SSM_N + (g + 1) * SSM_N] = dcg + _dot(dcbb, bg)
            dxdt = dxdt_s[...] + dxd_s[...]
            dxs = dy * dsk_ref[...] + dxdt * dt_x
            red = _group_sum(jnp.concatenate(
                [dxdt * xs, jnp.concatenate(dacc_x, axis=1),
                 jnp.broadcast_to(jnp.concatenate(dlast_x, axis=1), (8, SSM_W))], axis=0), e_ref[...])
            row = lax.broadcasted_iota(jnp.int32, (BLOCK, SSM_HEADS), 0)
            dacum = dacum + red[BLOCK:2 * BLOCK] + jnp.where(row == BLOCK - 1, red[2 * BLOCK:2 * BLOCK + 1], 0.0)
            ddta = _exact_left(_triu().astype(BF), dacum)
            ddt = red[:BLOCK] + ddta * a
            galog_ref[...] += jnp.sum(ddta * dt, axis=0, keepdims=True) * a
            du = ddt * _sig(u)
            ddt_ref[rs, :] = _bf(du)
            gdtb_ref[...] += jnp.sum(du, axis=0, keepdims=True)
            dconv = jnp.concatenate([dxs, dbc_s[...]], axis=1) * _dsilu(conv, sg)
            gcb_ref[...] += jnp.sum(dconv, axis=0, keepdims=True)
            ext2 = jnp.concatenate([dconv, nhead[...]], axis=0)
            ahead = [pltpu.roll(ext2, BLOCK + 8 - (CONV_K - 1 - j), axis=0)[0:BLOCK] if j < CONV_K - 1 else dconv
                     for j in range(CONV_K)]
            dx_ref[rs, :] = _bf(sum(ahead[j] * cw_ref[j:j + 1, :] for j in range(CONV_K)))
            xraw = x_ref[rs, :]
            gcw_ref[...] += jnp.concatenate([jnp.sum(ahead[j] * xraw, axis=0, keepdims=True) for j in range(CONV_K)], axis=0)
            nhead[...] = dconv[0:8]

        i = pl.program_id(0)

        @pl.when(i == 0)
        def _():
            for ref in (dh, nhead, gdskx, gcw_ref, gcb_ref, gdtb_ref, galog_ref, gdsk_ref):
                ref[...] = jnp.zeros_like(ref)

        for j in reversed(range(ch)):
            chunk_bwd(j)

        @pl.when(i == nsteps - 1)
        def _():
            gdsk_ref[...] = _group_sum(jnp.broadcast_to(gdskx[...], (8, SSM_W)), e_ref[...])[0:1]

    chunk = lambda w: pl.BlockSpec((rows, w), lambda i: (nsteps - 1 - i, 0))
    sd = jax.ShapeDtypeStruct
    return pl.pallas_call(
        body, name="ssd_bwd", grid=(nsteps,),
        in_specs=[chunk(XBC_W), chunk(XBC_W),
                  chunk(SSM_HEADS), _full((CONV_K, XBC_W)), _full((1, SSM_HEADS)),
                  _full((1, SSM_HEADS)), _full((1, SSM_W)), _full((SSM_W, SSM_HEADS)), _full((3 * SSM_HEADS, SSM_W)),
                  pl.BlockSpec((ch, SSM_N, SSM_W), lambda i: (nsteps - 1 - i, 0, 0)), chunk(SSM_W)],
        out_specs=[chunk(XBC_W), chunk(SSM_HEADS), _full((CONV_K, XBC_W)), _full((1, XBC_W)),
                   _full((1, SSM_HEADS)), _full((1, SSM_HEADS)), _full((1, SSM_HEADS))],
        out_shape=[sd((s, XBC_W), BF), sd((s, SSM_HEADS), BF), sd((CONV_K, XBC_W), F32), sd((1, XBC_W), F32),
                   sd((1, SSM_HEADS), F32), sd((1, SSM_HEADS), F32), sd((1, SSM_HEADS), F32)],
        scratch_shapes=[pltpu.VMEM((SSM_N, SSM_W), F32), pltpu.VMEM((8, XBC_W), F32),
                        pltpu.VMEM((1, SSM_W), F32), pltpu.VMEM((BLOCK, SSM_W), F32),
                        pltpu.VMEM((BLOCK, 2 * SSM_G * SSM_N), F32), pltpu.VMEM((BLOCK, SSM_W), F32)],
        compiler_params=_params(dimension_semantics=("arbitrary",)),
    )(xbc, conv_all, dt_raw, conv_w, dt_bias, a_log, dsk_x, e_mat, e3t, hprev_all, dy_all)


def _dh(x, dout, norm_w, scale, dsegs, w_t, tm=256):
    s = x.shape[0]

    def body(x_ref, dout_ref, nw_ref, sc_ref, *rest):
        d_refs, w_hbm = rest[:9], rest[9]
        gx_ref, dshift_ref, dscale_ref, gnw_ref = rest[10:14]
        w_vm, sem = rest[14], rest[15]
        first = pl.program_id(0) == 0
        cps = [pltpu.make_async_copy(w_hbm.at[SEG_OFF[j]:SEG_OFF[j + 1], :], w_vm.at[SEG_OFF[j]:SEG_OFF[j + 1], :], sem.at[j])
               for j in range(9)]

        def tile(waiting):
            dh = None
            for j in range(9):
                if waiting:
                    cps[j].wait()
                part = _dot(d_refs[j][...], w_vm[SEG_OFF[j]:SEG_OFF[j + 1], :])
                dh = part if dh is None else dh + part
            xv = x_ref[...]
            r = lax.rsqrt(jnp.mean(xv * xv, axis=-1, keepdims=True) + EPS)
            xn = xv * r
            nw = nw_ref[...]
            sc1 = 1.0 + sc_ref[...]
            dshift_ref[...] += jnp.sum(dh, axis=0, keepdims=True)
            dhxn = jnp.sum(dh * xn, axis=0, keepdims=True)
            dscale_ref[...] += dhxn * nw
            gnw_ref[...] += dhxn * sc1
            dxn = dh * (nw * sc1)
            gx_ref[...] = dout_ref[...] + r * (dxn - xn * jnp.mean(xn * dxn, axis=-1, keepdims=True))

        @pl.when(first)
        def _():
            for cp in cps:
                cp.start()
            for ref in (dshift_ref, dscale_ref, gnw_ref):
                ref[...] = jnp.zeros_like(ref)
            tile(True)

        @pl.when(jnp.logical_not(first))
        def _():
            tile(False)

    vec = _full((1, D_MODEL))
    sd = jax.ShapeDtypeStruct
    return pl.pallas_call(
        body, name="dh", grid=(s // tm,),
        in_specs=[_rows(tm, D_MODEL), _rows(tm, D_MODEL), vec, vec] + [_rows(tm, w) for w in SEG_W] + [ANY],
        out_specs=[_rows(tm, D_MODEL), vec, vec, vec],
        out_shape=[sd((s, D_MODEL), F32), sd((1, D_MODEL), F32), sd((1, D_MODEL), F32), sd((1, D_MODEL), F32)],
        scratch_shapes=[pltpu.VMEM((IN_W, D_MODEL), BF), pltpu.SemaphoreType.DMA((9,))],
        compiler_params=_params(dimension_semantics=("arbitrary",)),
    )(x, dout, norm_w, scale, *dsegs, w_t)


def _gw_seg(h, dseg, name, tm=1024):
    s, w = dseg.shape
    tn = min(w, 1024)
    tm = min(tm, s)
    nm = s // tm

    def body(h_ref, d_ref, o_ref, acc):
        m = pl.program_id(1)

        @pl.when(m == 0)
        def _():
            acc[...] = jnp.zeros_like(acc)

        acc[...] += _dot_tn(d_ref[...], h_ref[...])

        @pl.when(m == nm - 1)
        def _():
            o_ref[...] = _bf(acc[...])

    return pl.pallas_call(
        body, name=name, grid=(w // tn, nm),
        in_specs=[pl.BlockSpec((tm, D_MODEL), lambda n, m: (m, 0)), pl.BlockSpec((tm, tn), lambda n, m: (m, n))],
        out_specs=pl.BlockSpec((tn, D_MODEL), lambda n, m: (n, 0)),
        out_shape=jax.ShapeDtypeStruct((w, D_MODEL), BF),
        scratch_shapes=[pltpu.VMEM((tn, D_MODEL), F32)],
        compiler_params=_params(dimension_semantics=("arbitrary", "arbitrary")),
    )(h, dseg)


def _gw_in(h, dsegs):
    return [_gw_seg(h, d, "gw_in_%d" % j) for j, d in enumerate(dsegs)]


def _local_step(x, tgt, shift, scale, gate, w_t, rows_fn, norm_w, qnw, knw, rel_bias, sinks,
                conv_w, conv_b, dt_bias, a_log, d_skip, ssm_nw, after_mid=None, after_gw=None):
    oh_t = _bucket_onehot_t()
    bias = _bias_dense(rel_bias.T, oh_t).reshape(ATTN_HEADS, BLOCK, 2 * BLOCK)
    *segs, h = _inproj(x, norm_w, scale, shift, w_t)
    q, k, v, za, zm, xbc, dtr, ga, gb = segs
    consts = _attn_consts(qnw, knw)
    o_att, lse = _attn_fwd(q, k, v, bias, sinks, consts)
    e_mat, e3t = _membership(SSM_W, SSM_P, SSM_HEADS)
    dsk_x = jnp.repeat(d_skip, SSM_P, axis=1)
    ypre, hprev, conv = _ssd_fwd(xbc, dtr, conv_w, conv_b, dt_bias, a_log, dsk_x, e3t)
    (dout, d_o, dza, dyp, dzm, dga, dgb, yag, dy_a, yn, dy_b, merged, dob, g_ssm_nw, dgate, loss) = _mid(
        x, tgt, o_att, za, ypre, zm, ga, gb, gate, ssm_nw, rows_fn(ypre))
    g_wap = _gw_seg(dy_a, yag, "gw_attn_proj")
    g_wsp = _gw_seg(dy_b, yn, "gw_ssm_proj")
    g_wout = _gw_seg(dob, merged, "gw_out")
    zero = after_mid(g_wap, g_wsp, g_wout) if after_mid is not None else 0.0
    dq, dk, dv, dss, g_qnw, g_knw, g_sinks = _attn_bwd(q, k, v, bias, sinks + zero, consts, o_att, lse, d_o)
    g_rel = _bias_grad(dss.reshape(ATTN_HEADS, BLOCK * 2 * BLOCK), oh_t).T
    dxbc, ddt, g_cw, g_cb, g_dtb, g_alog, g_dsk = _ssd_bwd(
        xbc, conv, dtr, conv_w, dt_bias, a_log, dsk_x, e_mat, e3t, hprev, dyp)
    dsegs = (dq, dk, dv, dza, dzm, dxbc, ddt, dga, dgb)
    g_ws = _gw_in(h, dsegs)
    zero = after_gw(g_ws) if after_gw is not None else 0.0
    gx, dshift, dscale, g_nw = _dh(x, dout, norm_w + zero, scale, dsegs, w_t)
    return dict(loss=loss, grad_x=gx, dmod=jnp.concatenate([dshift, dscale, dgate], axis=1), g_ws=g_ws,
                g_wap=g_wap, g_wsp=g_wsp, g_wout=g_wout, g_norm_w=g_nw, g_qnw=g_qnw, g_knw=g_knw, g_rel=g_rel,
                g_sinks=g_sinks, g_conv_w=g_cw, g_conv_b=g_cb, g_dt_bias=g_dtb, g_a_log=g_alog, g_d_skip=g_dsk,
                g_ssm_nw=g_ssm_nw)


def _me():
    return lax.axis_index("x"), lax.axis_index("y"), lax.axis_index("c")


def _flip(v, bit):
    return 1 - v if bit else v


def _ag_direct(v, name):
    def body(v_ref, out_ref, send_sems, recv_sems, local_sem):
        x, y, c = _me()
        me = 4 * x + 2 * y + c
        mine = pltpu.make_async_copy(v_ref, out_ref.at[me], local_sem)
        mine.start()
        peers = [(_flip(x, k >> 2 & 1), _flip(y, k >> 1 & 1), _flip(c, k & 1)) for k in range(1, N_DEV)]
        sends = [pltpu.make_async_remote_copy(
            src_ref=v_ref, dst_ref=out_ref.at[me], send_sem=send_sems.at[j], recv_sem=recv_sems.at[j],
            device_id=p, device_id_type=MESH) for j, p in enumerate(peers)]
        for cp in sends:
            cp.start()
        for j, (px, py, pc) in enumerate(peers):
            pltpu.make_async_remote_copy(
                src_ref=v_ref, dst_ref=out_ref.at[4 * px + 2 * py + pc], send_sem=send_sems.at[j],
                recv_sem=recv_sems.at[j], device_id=(px, py, pc), device_id_type=MESH).wait_recv()
        for cp in sends:
            cp.wait_send()
        mine.wait()

    vm = pl.BlockSpec(memory_space=pltpu.VMEM)
    return pl.pallas_call(
        body, name=name, out_shape=jax.ShapeDtypeStruct((N_DEV,) + v.shape, v.dtype),
        in_specs=[vm], out_specs=vm,
        scratch_shapes=[pltpu.SemaphoreType.DMA((N_DEV - 1,)), pltpu.SemaphoreType.DMA((N_DEV - 1,)),
                        pltpu.SemaphoreType.DMA],
        compiler_params=_params(),
    )(v)


def _ag_two_level(v, name):
    def body(v_ref, out_ref, token, send_sems, recv_sems, local_sem):
        token[...] = jnp.zeros_like(token)
        x, y, c = _me()
        me, sibling = (x, y, c), (x, y, 1 - c)
        chips = [(1 - x, y), (x, 1 - y), (1 - x, 1 - y)]

        def slot(px, py, pc):
            return out_ref.at[4 * px + 2 * py + pc]

        def copy(k, block, to, src=None):
            return pltpu.make_async_remote_copy(
                src_ref=slot(*block) if src is None else src, dst_ref=slot(*block),
                send_sem=send_sems.at[k], recv_sem=recv_sems.at[k], device_id=to, device_id_type=MESH)

        mine = pltpu.make_async_copy(v_ref, slot(*me), local_sem)
        mine.start()
        first = [copy(0, me, sibling, src=v_ref)]
        first += [copy(1 + j, me, (*chip, c), src=v_ref) for j, chip in enumerate(chips)]
        for cp in first:
            cp.start()
        passed = [copy(4 + j, (*chip, c), sibling) for j, chip in enumerate(chips)]
        for j, chip in enumerate(chips):
            copy(1 + j, (*chip, c), me).wait_recv()
            passed[j].start()
        copy(0, sibling, me).wait_recv()
        for j, chip in enumerate(chips):
            copy(4 + j, (*chip, 1 - c), me).wait_recv()
        for cp in first + passed:
            cp.wait_send()
        mine.wait()

    out, token = pl.pallas_call(
        body, name=name,
        out_shape=(jax.ShapeDtypeStruct((N_DEV,) + v.shape, v.dtype), jax.ShapeDtypeStruct((8, 128), v.dtype)),
        in_specs=[ANY], out_specs=(ANY, pl.BlockSpec(memory_space=pltpu.VMEM)),
        scratch_shapes=[pltpu.SemaphoreType.DMA((7,)), pltpu.SemaphoreType.DMA((7,)), pltpu.SemaphoreType.DMA],
        compiler_params=_params(),
    )(v)
    return out, token[0:1, 0:1]


def _rs_sibling(g, name):
    def body(g_ref, out_ref, send_sems, recv_sems):
        x, y, c = _me()
        cps = [pltpu.make_async_remote_copy(
            src_ref=g_ref.at[2 * ch + 1 - c], dst_ref=out_ref.at[ch], send_sem=send_sems.at[ch],
            recv_sem=recv_sems.at[ch], device_id=(x, y, 1 - c), device_id_type=MESH) for ch in range(4)]
        for cp in cps:
            cp.start()
        for cp in cps:
            cp.wait()

    return pl.pallas_call(
        body, name=name, out_shape=jax.ShapeDtypeStruct((4,) + g.shape[1:], g.dtype),
        in_specs=[ANY], out_specs=ANY,
        scratch_shapes=[pltpu.SemaphoreType.DMA((4,)), pltpu.SemaphoreType.DMA((4,))],
        compiler_params=_params(),
    )(g)


def _add_sibling(g, got, name):
    _, r, n = g.shape
    tr = min(r, 256)

    def body(c_ref, a_ref, b_ref, o_ref):
        o_ref[...] = a_ref[...] + b_ref[...]

    grid_spec = pltpu.PrefetchScalarGridSpec(
        num_scalar_prefetch=1, grid=(4, r // tr),
        in_specs=[pl.BlockSpec((1, tr, n), lambda ch, i, c_ref: (2 * ch + c_ref[0], i, 0)),
                  pl.BlockSpec((1, tr, n), lambda ch, i, c_ref: (ch, i, 0))],
        out_specs=pl.BlockSpec((1, tr, n), lambda ch, i, c_ref: (ch, i, 0)))
    return pl.pallas_call(
        body, name=name, grid_spec=grid_spec, out_shape=jax.ShapeDtypeStruct((4, r, n), g.dtype),
        compiler_params=_params(dimension_semantics=("arbitrary", "arbitrary")),
    )(lax.axis_index("c").reshape(1).astype(jnp.int32), g, got)


def _rs_chips(p, name):
    def body(p_ref, out_ref, send_sems, recv_sems, local_sem):
        x, y, c = _me()
        my_chip = 2 * x + y
        mine = pltpu.make_async_copy(p_ref.at[my_chip], out_ref.at[my_chip], local_sem)
        mine.start()
        chips = [(1 - x, y), (x, 1 - y), (1 - x, 1 - y)]
        sends = [pltpu.make_async_remote_copy(
            src_ref=p_ref.at[2 * px + py], dst_ref=out_ref.at[my_chip], send_sem=send_sems.at[j],
            recv_sem=recv_sems.at[j], device_id=(px, py, c), device_id_type=MESH) for j, (px, py) in enumerate(chips)]
        for cp in sends:
            cp.start()
        for j, (px, py) in enumerate(chips):
            pltpu.make_async_remote_copy(
                src_ref=p_ref.at[my_chip], dst_ref=out_ref.at[2 * px + py], send_sem=send_sems.at[j],
                recv_sem=recv_sems.at[j], device_id=(px, py, c), device_id_type=MESH).wait_recv()
        for cp in sends:
            cp.wait_send()
        mine.wait()

    return pl.pallas_call(
        body, name=name, out_shape=jax.ShapeDtypeStruct(p.shape, p.dtype),
        in_specs=[ANY], out_specs=ANY,
        scratch_shapes=[pltpu.SemaphoreType.DMA((3,)), pltpu.SemaphoreType.DMA((3,)), pltpu.SemaphoreType.DMA],
        compiler_params=_params(),
    )(p)


HBM = pl.BlockSpec(memory_space=pltpu.HBM)
SEM = pl.BlockSpec(memory_space=pltpu.SEMAPHORE)
EFFECT = pltpu.SideEffectType.DATAFLOW_SIDE_EFFECTING


def _peers(x, y, c):
    return [(_flip(x, k >> 2 & 1), _flip(y, k >> 1 & 1), _flip(c, k & 1)) for k in range(1, N_DEV)]


def _exchange_start(src, land, gather, name):
    def body(src_ref, land_ref, send_sems, recv_sems, src_thru, land_thru, token):
        x, y, c = _me()
        me = 4 * x + 2 * y + c
        for j, (px, py, pc) in enumerate(_peers(x, y, c)):
            pltpu.make_async_remote_copy(
                src_ref=src_ref if gather else src_ref.at[4 * px + 2 * py + pc], dst_ref=land_ref.at[me],
                send_sem=send_sems.at[j], recv_sem=recv_sems.at[j], device_id=(px, py, pc), device_id_type=MESH).start()
        token[...] = jnp.zeros_like(token)

    sems = pltpu.SemaphoreType.DMA((N_DEV - 1,))
    out = pl.pallas_call(
        body, name=name,
        out_shape=(sems, sems, pltpu.HBM(src.shape, src.dtype), pltpu.HBM(land.shape, land.dtype),
                   jax.ShapeDtypeStruct((8, 128), F32)),
        in_specs=(HBM, HBM), out_specs=(SEM, SEM, HBM, HBM, pl.BlockSpec(memory_space=pltpu.VMEM)),
        input_output_aliases={0: 2, 1: 3},
        compiler_params=pltpu.CompilerParams(has_side_effects=EFFECT),
    )(pltpu.with_memory_space_constraint(src, pltpu.HBM), pltpu.with_memory_space_constraint(land, pltpu.HBM))
    return out[:4], out[4][0, 0]


def _exchange_wait(started, after, gather, name):
    send_sems, recv_sems, src_thru, land_thru = started

    def body(src_ref, land_ref, send_sems, recv_sems, after_ref, src_dead, got_ref):
        x, y, c = _me()
        for j, (px, py, pc) in enumerate(_peers(x, y, c)):
            pid = 4 * px + 2 * py + pc
            cp = pltpu.make_async_remote_copy(
                src_ref=src_ref if gather else src_ref.at[pid], dst_ref=land_ref.at[pid],
                send_sem=send_sems.at[j], recv_sem=recv_sems.at[j], device_id=(px, py, pc), device_id_type=MESH)
            cp.wait_send()
            cp.wait_recv()

    return pl.pallas_call(
        body, name=name,
        out_shape=(pltpu.HBM(src_thru.shape, src_thru.dtype), pltpu.HBM(land_thru.shape, land_thru.dtype)),
        in_specs=(HBM, HBM, SEM, SEM, ANY), out_specs=(HBM, HBM), input_output_aliases={0: 0, 1: 1},
        compiler_params=pltpu.CompilerParams(has_side_effects=EFFECT),
    )(src_thru, land_thru, send_sems, recv_sems, after)[1]


def _reduce_scatter(g, name):
    got = _rs_sibling(g, name + "_sib")
    return _rs_chips(_add_sibling(g, got, name + "_add"), name + "_chips")


def _silu(a):
    return a * _sig(a)


def _mod_piece(c_all, w_ada, b_piece):
    def body(c_ref, w_ref, b_ref, o_ref):
        o_ref[...] = _dot(_bf(_silu(c_ref[...])), _bf(w_ref[...])) + b_ref[...]

    return pl.pallas_call(
        body, name="mod_piece", out_shape=jax.ShapeDtypeStruct((c_all.shape[0], w_ada.shape[1]), F32),
        compiler_params=_params(),
    )(c_all, w_ada, b_piece)


def _gw_ada(c_all, dmod_piece):
    def body(c_ref, d_ref, o_ref):
        o_ref[...] = _dot_tn(_bf(_silu(c_ref[...])), _bf(d_ref[...]))

    return pl.pallas_call(
        body, name="gw_ada", out_shape=jax.ShapeDtypeStruct((c_all.shape[1], dmod_piece.shape[1]), F32),
        compiler_params=_params(),
    )(c_all, dmod_piece)


def _adam(parts, w, m, v, name):
    k, r, n = parts.shape
    if r <= 256 or r % 256 == 0:
        tr, tn = min(r, 256), n
    else:
        tr, tn = r, 256
    assert r % tr == 0 and n % tn == 0

    def body(p_ref, w_ref, m_ref, v_ref, g_ref, d_ref, nm_ref, nv_ref):
        g = p_ref[0].astype(F32)
        for j in range(1, k):
            g = g + p_ref[j].astype(F32)
        g_ref[...] = g
        d_ref[...], nm_ref[...], nv_ref[...] = _adam_math(g, w_ref[...], m_ref[...], v_ref[...])

    blk = pl.BlockSpec((tr, tn), lambda i, j: (i, j))
    return pl.pallas_call(
        body, name=name, grid=(r // tr, n // tn),
        in_specs=[pl.BlockSpec((k, tr, tn), lambda i, j: (0, i, j)), blk, blk, blk],
        out_specs=[blk, blk, blk, blk],
        out_shape=[jax.ShapeDtypeStruct((r, n), F32)] * 4,
        compiler_params=_params(dimension_semantics=("arbitrary", "arbitrary")),
    )(parts, w, m, v)


def _adam_math(g, w, m, v):
    m_new = ADAM_B1 * m + (1.0 - ADAM_B1) * g
    v_new = ADAM_B2 * v + (1.0 - ADAM_B2) * jnp.square(g)
    m_hat = m_new / (1.0 - ADAM_B1 ** ADAM_STEP)
    v_hat = v_new / (1.0 - ADAM_B2 ** ADAM_STEP)
    return -ADAM_LR * (m_hat / (jnp.sqrt(v_hat) + ADAM_EPS) + ADAM_WD * w), m_new, v_new


_SMALL = (("b_ada", 3 * D_MODEL), ("norm_w", D_MODEL), ("q_norm_w", HEAD_DIM), ("k_norm_w", HEAD_DIM),
          ("rel_bias", REL_BUCKETS * ATTN_HEADS), ("sinks", ATTN_HEADS), ("conv_b", XBC_W), ("dt_bias", SSM_HEADS),
          ("a_log", SSM_HEADS), ("d_skip", SSM_HEADS), ("ssm_norm_w", SSM_W))
_SLOT = tuple(-(-n // 128) * 128 for _, n in _SMALL)
_SLOT_OFF = tuple(int(o) for o in np.cumsum((0,) + _SLOT))
_LOSS_OFF = _SLOT_OFF[-1]
_CW_OFF = _LOSS_OFF + 128
_PACK_N = _CW_OFF + CONV_K * XBC_W


def _pack_partials(small, loss, g_conv_w):
    parts = []
    for (name, n), slot in zip(_SMALL, _SLOT):
        parts.append(small[name].reshape(1, n))
        if slot > n:
            parts.append(jnp.zeros((1, slot - n), F32))
    parts += [loss.reshape(1, 1), jnp.zeros((1, 127), F32), g_conv_w.reshape(1, CONV_K * XBC_W)]
    return jnp.concatenate(parts, axis=1)


def _adam_small(pack_all, w, m, v):
    names = [name for name, _ in _SMALL]

    def body(p_ref, *rest):
        ins, outs = rest[:3 * len(names)], rest[3 * len(names):]

        def total(off, n):
            g = p_ref[0, :, off:off + n]
            for d in range(1, N_DEV):
                g = g + p_ref[d, :, off:off + n]
            return g

        for j, (name, n) in enumerate(_SMALL):
            g = total(_SLOT_OFF[j], n)
            delta, m_new, v_new = _adam_math(g, ins[3 * j][...], ins[3 * j + 1][...], ins[3 * j + 2][...])
            outs[4 * j][...] = g
            outs[4 * j + 1][...] = delta
            outs[4 * j + 2][...] = m_new
            outs[4 * j + 3][...] = v_new
        outs[-1][...] = total(_LOSS_OFF, 1)

    flat = []
    for name, n in _SMALL:
        flat += [w[name].reshape(1, n), m[name].reshape(1, n), v[name].reshape(1, n)]
    out_shape = [jax.ShapeDtypeStruct((1, n), F32) for _, n in _SMALL for _ in range(4)] + [jax.ShapeDtypeStruct((1, 1), F32)]
    out = pl.pallas_call(body, name="adam_small", out_shape=out_shape, compiler_params=_params())(pack_all, *flat)
    res = {name: [out[4 * j + t].reshape(w[name].shape) for t in range(4)] for j, name in enumerate(names)}
    return res, out[-1]


WEIGHTS = ("w_ada", "b_ada", "norm_w", "w_in", "q_norm_w", "k_norm_w", "rel_bias", "sinks", "conv_w", "conv_b",
           "dt_bias", "a_log", "d_skip", "ssm_norm_w", "w_attn_proj", "w_ssm_proj", "w_out")


def kernel(x, c, w_ada, b_ada, norm_w, w_in, q_norm_w, k_norm_w, rel_bias, sinks, conv_w, conv_b, dt_bias, a_log, d_skip, ssm_norm_w, w_attn_proj, w_ssm_proj, w_out, loss_target, m_w_ada, m_b_ada, m_norm_w, m_w_in, m_q_norm_w, m_k_norm_w, m_rel_bias, m_sinks, m_conv_w, m_conv_b, m_dt_bias, m_a_log, m_d_skip, m_ssm_norm_w, m_w_attn_proj, m_w_ssm_proj, m_w_out, v_w_ada, v_b_ada, v_norm_w, v_w_in, v_q_norm_w, v_k_norm_w, v_rel_bias, v_sinks, v_conv_w, v_conv_b, v_dt_bias, v_a_log, v_d_skip, v_ssm_norm_w, v_w_attn_proj, v_w_ssm_proj, v_w_out):
    w = dict(w_ada=w_ada, b_ada=b_ada, norm_w=norm_w, w_in=w_in, q_norm_w=q_norm_w, k_norm_w=k_norm_w,
             rel_bias=rel_bias, sinks=sinks, conv_w=conv_w, conv_b=conv_b, dt_bias=dt_bias, a_log=a_log,
             d_skip=d_skip, ssm_norm_w=ssm_norm_w, w_attn_proj=w_attn_proj, w_ssm_proj=w_ssm_proj, w_out=w_out)
    m = dict(w_ada=m_w_ada, b_ada=m_b_ada, norm_w=m_norm_w, w_in=m_w_in, q_norm_w=m_q_norm_w, k_norm_w=m_k_norm_w,
             rel_bias=m_rel_bias, sinks=m_sinks, conv_w=m_conv_w, conv_b=m_conv_b, dt_bias=m_dt_bias, a_log=m_a_log,
             d_skip=m_d_skip, ssm_norm_w=m_ssm_norm_w, w_attn_proj=m_w_attn_proj, w_ssm_proj=m_w_ssm_proj, w_out=m_w_out)
    v = dict(w_ada=v_w_ada, b_ada=v_b_ada, norm_w=v_norm_w, w_in=v_w_in, q_norm_w=v_q_norm_w, k_norm_w=v_k_norm_w,
             rel_bias=v_rel_bias, sinks=v_sinks, conv_w=v_conv_w, conv_b=v_conv_b, dt_bias=v_dt_bias, a_log=v_a_log,
             d_skip=v_d_skip, ssm_norm_w=v_ssm_norm_w, w_attn_proj=v_w_attn_proj, w_ssm_proj=v_w_ssm_proj, w_out=v_w_out)
    me = 4 * lax.axis_index("x") + 2 * lax.axis_index("y") + lax.axis_index("c")
    ada_n = w_ada.shape[2]
    in_n = w_in.shape[2]
    cw_n = conv_w.shape[2]

    first = _ag_direct(jnp.concatenate([c, conv_w[0].reshape(1, CONV_K * cw_n)], axis=1), "ag_c")[:, 0]
    c_all = first[:, :D_MODEL]
    conv_w_full = first[:, D_MODEL:].reshape(N_DEV, CONV_K, cw_n).transpose(1, 0, 2).reshape(CONV_K, XBC_W)
    b_piece = lax.dynamic_slice_in_dim(b_ada, me * ada_n, ada_n, axis=1)
    mod_all = _ag_direct(_mod_piece(c_all, w_ada[0], b_piece), "ag_mod")
    mod = lax.dynamic_index_in_dim(mod_all, me, axis=1, keepdims=False).reshape(1, 3 * D_MODEL)
    shift, scale, gate = mod[:, :D_MODEL], mod[:, D_MODEL:2 * D_MODEL], mod[:, 2 * D_MODEL:]

    w_t, zero = _ag_two_level(w_in[0].T.astype(BF), "ag_w_in")
    w_t = w_t.reshape(N_DEV * in_n, D_MODEL)

    def with_mine(blocks, mine):
        return lax.dynamic_update_index_in_dim(lax.empty(blocks, mine.dtype), mine, me, axis=0)

    rows = jnp.concatenate([w_attn_proj[0], w_ssm_proj[0], w_out[0]], axis=0).astype(BF) + zero
    r_ap, r_sp = w_attn_proj.shape[1], w_ssm_proj.shape[1]
    rows_started, zero = _exchange_start(rows, with_mine((N_DEV,) + rows.shape, rows), True, "ag_rows_start")

    def rows_fn(after):
        return _exchange_wait(rows_started, after, True, "ag_rows_wait")

    started = {}

    def send_blocks(key, g, name):
        started[key], zero = _exchange_start(
            g, with_mine(g.shape, lax.dynamic_index_in_dim(g, me, axis=0, keepdims=False)), False, name)
        return zero

    def after_mid(g_wap, g_wsp, g_wout):
        return send_blocks("rows", jnp.concatenate(
            [g_wap.reshape(N_DEV, r_ap, D_MODEL), g_wsp.reshape(N_DEV, r_sp, D_MODEL),
             g_wout.reshape(N_DEV, r_ap, D_MODEL)], axis=1), "rs_rows_start")

    def after_gw(g_ws):
        return send_blocks("in", jnp.concatenate(g_ws, axis=0).reshape(N_DEV, in_n, D_MODEL), "rs_in_start")

    r = _local_step(x[0], loss_target[0], shift, scale + zero, gate, w_t, rows_fn, norm_w, q_norm_w, k_norm_w,
                    rel_bias, sinks, conv_w_full, conv_b, dt_bias, a_log, d_skip, ssm_norm_w, after_mid, after_gw)

    small = dict(b_ada=r["dmod"], norm_w=r["g_norm_w"], q_norm_w=r["g_qnw"], k_norm_w=r["g_knw"], rel_bias=r["g_rel"],
                 sinks=r["g_sinks"], conv_b=r["g_conv_b"], dt_bias=r["g_dt_bias"], a_log=r["g_a_log"],
                 d_skip=r["g_d_skip"], ssm_norm_w=r["g_ssm_nw"])
    pack_all = _ag_direct(_pack_partials(small, r["loss"], r["g_conv_w"]), "ag_small")
    res, loss = _adam_small(pack_all, w, m, v)
    loss = loss[0, 0]
    cw_parts = pack_all[:, 0, _CW_OFF:].reshape(N_DEV, CONV_K, XBC_W)
    cw_mine = lax.dynamic_slice_in_dim(cw_parts, me * cw_n, cw_n, axis=2)
    res["conv_w"] = [a[None] for a in _adam(cw_mine, conv_w[0], m_conv_w[0], v_conv_w[0], "adam_conv_w")]

    dmod_piece = lax.dynamic_slice_in_dim(pack_all[:, 0, :3 * D_MODEL], me * ada_n, ada_n, axis=1)
    g_ada = _gw_ada(c_all, dmod_piece)
    res["w_ada"] = [a[None] for a in _adam(g_ada[None], w_ada[0], m_w_ada[0], v_w_ada[0], "adam_w_ada")]

    cat = lambda d: jnp.concatenate([d["w_attn_proj"][0], d["w_ssm_proj"][0], d["w_out"][0]], axis=0)
    rows_res = _adam(_exchange_wait(started["rows"], g_ada, False, "rs_rows_wait"), cat(w), cat(m), cat(v), "adam_w_rows")
    res["w_in"] = [a.T[None] for a in _adam(_exchange_wait(started["in"], rows_res[0], False, "rs_in_wait"),
                                            w_in[0].T, m_w_in[0].T, v_w_in[0].T, "adam_w_in")]
    res["w_attn_proj"] = [a[None, :r_ap] for a in rows_res]
    res["w_ssm_proj"] = [a[None, r_ap:r_ap + r_sp] for a in rows_res]
    res["w_out"] = [a[None, r_ap + r_sp:] for a in rows_res]

    outs = [loss, r["grad_x"][None]]
    for j in range(4):
        outs += [res[name][j] for name in WEIGHTS]
    return tuple(outs)
```

```python
import functools
import math

import numpy as np
import jax
import jax.numpy as jnp
from jax import lax
from jax.experimental import pallas as pl
from jax.experimental.pallas import tpu as pltpu

F32 = jnp.float32
BF = jnp.bfloat16
HI = lax.Precision.HIGHEST

D_MODEL = 1024
ATTN_HEADS = 16
KV_HEADS = 4
GRP = ATTN_HEADS // KV_HEADS
HEAD_DIM = 64
ATTN_W = ATTN_HEADS * HEAD_DIM
KV_W = KV_HEADS * HEAD_DIM
BLOCK = 128
REL_BUCKETS = 32
REL_MAX_DIST = 128
SSM_W = 2048
SSM_P = 64
SSM_HEADS = 32
SSM_G = 4
SSM_R = 8
SSM_N = 128
CONV_K = 4
XBC_W = SSM_W + 2 * SSM_G * SSM_N
SEG_W = (ATTN_W, KV_W, KV_W, ATTN_W, SSM_W, XBC_W, SSM_HEADS, D_MODEL, D_MODEL)
SEG_OFF = tuple(int(v) for v in np.cumsum((0,) + SEG_W))
IN_W = SEG_OFF[-1]
GATE_SEGS = (3, 4, 7, 8)
EPS = 1e-6
N_DEV = 8
ADAM_LR, ADAM_B1, ADAM_B2, ADAM_EPS, ADAM_WD, ADAM_STEP = 0.001, 0.9, 0.999, 1e-08, 0.01, 10
VMEM_LIMIT = 60 * 1024 * 1024
MESH = pl.DeviceIdType.MESH
ANY = pl.BlockSpec(memory_space=pl.ANY)


def _dot(a, b, precision=None):
    return jnp.dot(a, b, preferred_element_type=F32, precision=precision)


def _dot_nt(a, b, precision=None):
    return lax.dot_general(a, b, (((1,), (1,)), ((), ())), preferred_element_type=F32, precision=precision)


def _dot_tn(a, b, precision=None):
    return lax.dot_general(a, b, (((0,), (0,)), ((), ())), preferred_element_type=F32, precision=precision)


def _bf(a):
    return a.astype(BF)


def _sig(a):
    return 0.5 * jnp.tanh(0.5 * a) + 0.5


def _params(**kw):
    return pltpu.CompilerParams(vmem_limit_bytes=VMEM_LIMIT, **kw)


def _full(shape):
    nd = len(shape)
    return pl.BlockSpec(shape, lambda i: (0,) * nd)


def _rows(tm, w):
    return pl.BlockSpec((tm, w), lambda i: (i, 0))


def _inproj(x, norm_w, scale, shift, w_t, tm=256):
    s = x.shape[0]

    def body(x_ref, nw_ref, sc_ref, sh_ref, w_hbm, *rest):
        outs, h_ref, w_vm, sem = rest[:9], rest[9], rest[10], rest[11]
        first = pl.program_id(0) == 0
        cps = [pltpu.make_async_copy(w_hbm.at[SEG_OFF[j]:SEG_OFF[j + 1], :], w_vm.at[SEG_OFF[j]:SEG_OFF[j + 1], :], sem.at[j])
               for j in range(9)]

        def tile(waiting):
            xv = x_ref[...]
            r = lax.rsqrt(jnp.mean(xv * xv, axis=-1, keepdims=True) + EPS)
            h = xv * r * (nw_ref[...] * (1.0 + sc_ref[...])) + sh_ref[...]
            hb = _bf(h)
            h_ref[...] = hb
            for j in range(9):
                if waiting:
                    cps[j].wait()
                outs[j][...] = _dot_nt(hb, w_vm[SEG_OFF[j]:SEG_OFF[j + 1], :]).astype(outs[j].dtype)

        @pl.when(first)
        def _():
            for cp in cps:
                cp.start()
            tile(True)

        @pl.when(jnp.logical_not(first))
        def _():
            tile(False)

    vec = _full((1, D_MODEL))
    return pl.pallas_call(
        body, name="inproj", grid=(s // tm,),
        in_specs=[_rows(tm, D_MODEL), vec, vec, vec, ANY],
        out_specs=[_rows(tm, w) for w in SEG_W] + [_rows(tm, D_MODEL)],
        out_shape=[jax.ShapeDtypeStruct((s, w), BF if j in GATE_SEGS else F32) for j, w in enumerate(SEG_W)]
                  + [jax.ShapeDtypeStruct((s, D_MODEL), BF)],
        scratch_shapes=[pltpu.VMEM((IN_W, D_MODEL), BF), pltpu.SemaphoreType.DMA((9,))],
        compiler_params=_params(dimension_semantics=("arbitrary",)),
    )(x, norm_w, scale, shift, w_t)


def _bucket_onehot_t():
    qi = jnp.arange(BLOCK)[:, None]
    kj = jnp.arange(2 * BLOCK)[None, :]
    dist = qi + BLOCK - kj
    n = jnp.maximum(dist, 0)
    max_exact = REL_BUCKETS // 2
    nf = jnp.maximum(n, 1).astype(F32)
    large = max_exact + (jnp.log(nf / max_exact) / math.log(REL_MAX_DIST / max_exact)
                         * (REL_BUCKETS - max_exact)).astype(jnp.int32)
    large = jnp.minimum(large, REL_BUCKETS - 1)
    bucket = jnp.where(n < max_exact, n, large).reshape(1, BLOCK * 2 * BLOCK)
    return (bucket == jnp.arange(REL_BUCKETS)[:, None]).astype(F32)


def _bias_dense(rel_bias_t, oh_t):
    def body(rb_ref, oh_ref, o_ref):
        o_ref[...] = _dot(rb_ref[...], oh_ref[...], HI)

    return pl.pallas_call(
        body, name="bias_dense", out_shape=jax.ShapeDtypeStruct((ATTN_HEADS, BLOCK * 2 * BLOCK), F32),
        compiler_params=_params(),
    )(rel_bias_t, oh_t)


def _bias_grad(ds_sum, oh_t):
    def body(ds_ref, oh_ref, o_ref):
        o_ref[...] = _dot_nt(ds_ref[...], oh_ref[...], HI)

    return pl.pallas_call(
        body, name="bias_grad", out_shape=jax.ShapeDtypeStruct((ATTN_HEADS, REL_BUCKETS), F32),
        compiler_params=_params(),
    )(ds_sum, oh_t)


def _group_sum(a, e):
    hi = _bf(a)
    return _dot(hi, e) + _dot(_bf(a - hi.astype(F32)), e)


def _group_bcast(a, e3t):
    hi = _bf(a)
    r1 = a - hi.astype(F32)
    mid = _bf(r1)
    return _dot(jnp.concatenate([hi, mid, _bf(r1 - mid.astype(F32))], axis=1), e3t)


def _membership(width, group, ngroups):
    e = (jnp.arange(width)[:, None] // group == jnp.arange(ngroups)[None, :]).astype(BF)
    return e, jnp.tile(e.T, (3, 1))


def _fold(width, group):
    return (jnp.arange(width)[:, None] % group == jnp.arange(group)[None, :]).astype(BF)


def _heads_norm(t, w_x, e, e3t):
    r = lax.rsqrt(_group_sum(t * t, e) * (1.0 / HEAD_DIM) + EPS)
    r_x = _group_bcast(r, e3t)
    return t * r_x * w_x, r_x


def _heads_norm_bwd(t, r_x, w_x, d, e, e3t):
    wd = d * w_x
    corr = _group_bcast(_group_sum(t * wd, e) * (1.0 / HEAD_DIM), e3t)
    return r_x * wd - t * (r_x * r_x * r_x) * corr, jnp.sum(d * t * r_x, axis=0, keepdims=True)


def _stack_heads(a, hk):
    return jnp.concatenate([a[:, (hk * GRP + g) * HEAD_DIM:(hk * GRP + g + 1) * HEAD_DIM] for g in range(GRP)], axis=0)


def _stack_cols(a, hk):
    return jnp.concatenate([a[:, hk * GRP + g:hk * GRP + g + 1] for g in range(GRP)], axis=0)


def _masked_bias(bias):
    qi = jnp.arange(BLOCK)[:, None]
    kj = jnp.arange(2 * BLOCK)[None, :]
    cur_ok = jnp.logical_and(kj >= BLOCK, kj - BLOCK <= qi)
    both_ok = jnp.logical_or(jnp.logical_and(kj < BLOCK, kj > qi), cur_ok)
    return jnp.stack([jnp.where(cur_ok, bias, -1e30), jnp.where(both_ok, bias, -1e30)])


def _attn_consts(qnw, knw):
    eq, eq3t = _membership(ATTN_W, HEAD_DIM, ATTN_HEADS)
    ek, ek3t = _membership(KV_W, HEAD_DIM, ATTN_HEADS)
    return (jnp.tile(qnw, (1, ATTN_HEADS)), jnp.tile(knw, (1, KV_HEADS)), eq, eq3t, ek, ek3t)


def _attn_fwd(q, k, v, bias, sinks, consts):
    s = q.shape[0]
    nb = s // BLOCK
    gq = GRP * BLOCK
    bias_t = bias.reshape(2, KV_HEADS, GRP, BLOCK, 2 * BLOCK).transpose(0, 1, 4, 2, 3).reshape(2, KV_HEADS, 2 * BLOCK, gq)
    sink_rows = jnp.repeat(sinks.reshape(KV_HEADS, GRP), BLOCK, axis=1).reshape(KV_HEADS, 1, gq)
    eye = jnp.eye(BLOCK, dtype=BF)

    def body(q_ref, kp_ref, kc_ref, vp_ref, vc_ref, b_ref, bt_ref, sk_ref, skr_ref, eye_ref,
             qw_ref, kw_ref, eq_ref, eq3_ref, ek_ref, ek3_ref, o_ref, lse_ref):
        qn = _bf(_heads_norm(q_ref[...], qw_ref[...], eq_ref[...], eq3_ref[...])[0] * (HEAD_DIM ** -0.5))
        kn = _bf(_heads_norm(jnp.concatenate([kp_ref[...], kc_ref[...]], axis=0), kw_ref[...], ek_ref[...], ek3_ref[...])[0])
        vv = _bf(jnp.concatenate([vp_ref[...], vc_ref[...]], axis=0))
        ones = jnp.ones((2 * BLOCK, HEAD_DIM), BF)
        lses = []
        kss = [slice(hk * HEAD_DIM, (hk + 1) * HEAD_DIM) for hk in range(KV_HEADS)]
        qgs = [_stack_heads(qn, hk) for hk in range(KV_HEADS)]
        sc_ts = [_dot_nt(kn[:, kss[hk]], qgs[hk]) + bt_ref[0, hk] for hk in range(KV_HEADS)]
        m8s = [_bf(jnp.broadcast_to(jnp.maximum(jnp.max(sc_ts[hk], axis=0, keepdims=True), skr_ref[hk]), (8, gq)))
               for hk in range(KV_HEADS)]
        ms = [jnp.concatenate([_dot_nt(eye_ref[...], m8[:, g * BLOCK:(g + 1) * BLOCK])[:, 0:1] for g in range(GRP)], axis=0)
              for m8 in m8s]
        scs = [_dot_nt(qgs[hk], kn[:, kss[hk]]) + b_ref[0, hk * GRP:(hk + 1) * GRP].reshape(gq, 2 * BLOCK)
               for hk in range(KV_HEADS)]
        ps = [_bf(jnp.exp(scs[hk] - ms[hk])) for hk in range(KV_HEADS)]
        pvs = [_dot(ps[hk], jnp.concatenate([vv[:, kss[hk]], ones], axis=1)) for hk in range(KV_HEADS)]
        for hk in range(KV_HEADS):
            m, pv = ms[hk], pvs[hk]
            sink = jnp.concatenate([jnp.full((BLOCK, 1), sk_ref[0, hk * GRP + g], F32) for g in range(GRP)], axis=0)
            den = pv[:, HEAD_DIM:HEAD_DIM + 1] + jnp.exp(sink - m)
            out = pv[:, :HEAD_DIM] * (1.0 / den)
            lse = m + jnp.log(den)
            for g in range(GRP):
                h = hk * GRP + g
                o_ref[:, h * HEAD_DIM:(h + 1) * HEAD_DIM] = out[g * BLOCK:(g + 1) * BLOCK]
                lses.append(lse[g * BLOCK:(g + 1) * BLOCK])
        lse_ref[...] = jnp.concatenate(lses, axis=1)

    cur = lambda w: pl.BlockSpec((BLOCK, w), lambda i: (i, 0))
    prev = lambda w: pl.BlockSpec((BLOCK, w), lambda i: (jnp.maximum(i - 1, 0), 0))
    whole = lambda a: pl.BlockSpec(a.shape, lambda i: (0,) * a.ndim)
    first_or_not = lambda a: pl.BlockSpec((1,) + a.shape[1:], lambda i: (jnp.minimum(i, 1),) + (0,) * (a.ndim - 1))
    return pl.pallas_call(
        body, name="attn_fwd", grid=(nb,),
        in_specs=[cur(ATTN_W), prev(KV_W), cur(KV_W), prev(KV_W), cur(KV_W), first_or_not(bias), first_or_not(bias_t),
                  pl.BlockSpec(memory_space=pltpu.SMEM), whole(sink_rows), whole(eye)] + [_full(c.shape) for c in consts],
        out_specs=[cur(ATTN_W), cur(ATTN_HEADS)],
        out_shape=[jax.ShapeDtypeStruct((s, ATTN_W), F32), jax.ShapeDtypeStruct((s, ATTN_HEADS), F32)],
        compiler_params=_params(dimension_semantics=("arbitrary",)),
    )(q, k, k, v, v, bias, bias_t, sinks, sink_rows, eye, *consts)


def _conv_taps(xbc, tail):
    ext = jnp.concatenate([tail, xbc], axis=0)
    return [pltpu.roll(ext, CONV_K - 1 - j, axis=0)[8:8 + BLOCK] if j < CONV_K - 1 else xbc for j in range(CONV_K)]


def _softplus(u):
    return jnp.maximum(u, 0.0) + jnp.log(1.0 + jnp.exp(-jnp.abs(u)))


def _tril():
    r = lax.broadcasted_iota(jnp.int32, (BLOCK, BLOCK), 0)
    c = lax.broadcasted_iota(jnp.int32, (BLOCK, BLOCK), 1)
    return r >= c


def _triu():
    r = lax.broadcasted_iota(jnp.int32, (BLOCK, BLOCK), 0)
    c = lax.broadcasted_iota(jnp.int32, (BLOCK, BLOCK), 1)
    return r <= c


def _exact_left(m01, a):
    hi = _bf(a)
    r1 = a - hi.astype(F32)
    mid = _bf(r1)
    return _dot(m01, hi) + _dot(m01, mid) + _dot(m01, _bf(r1 - mid.astype(F32)))


def _ssd_common(conv, dtr, dtb_ref, alog_ref, e3_ref):
    sg = _sig(conv)
    xact = conv * sg
    u = dtr + dtb_ref[...]
    dt = _softplus(u)
    a = -jnp.exp(alog_ref[...])
    trilb = _tril()
    acum = _exact_left(trilb.astype(BF), dt * a)
    both = _group_bcast(jnp.concatenate([dt, acum], axis=0), e3_ref[...])
    dt_x, acum_x = both[:BLOCK], both[BLOCK:]
    return sg, xact, u, dt, a, trilb, acum, dt_x, acum_x


SSD_CH = 2


def _ssd_fwd(xbc, dt_raw, conv_w, conv_b, dt_bias, a_log, dsk_x, e3t):
    s = xbc.shape[0]
    nc = s // BLOCK
    ch = SSD_CH if nc % SSD_CH == 0 else 1
    rows = ch * BLOCK

    def body(x_ref, tail_ref, dtr_ref, cw_ref, cb_ref, dtb_ref, alog_ref, dsk_ref, e3_ref,
             y_ref, hp_ref, conv_ref, hst, yd_s, yoff_s):
        i = pl.program_id(0)

        @pl.when(i == 0)
        def _():
            hst[...] = jnp.zeros_like(hst)

        for j in range(ch):
            rs = slice(j * BLOCK, (j + 1) * BLOCK)
            tail = jnp.where(i > 0, tail_ref[...], 0.0) if j == 0 else x_ref[j * BLOCK - 8:j * BLOCK, :]
            taps = _conv_taps(x_ref[rs, :], tail)
            conv = cb_ref[...] + sum(taps[t] * cw_ref[t:t + 1, :] for t in range(CONV_K))
            conv_ref[rs, :] = conv
            _, xact, _, _, _, trilb, acum, dt_x, acum_x = _ssd_common(conv, dtr_ref[rs, :], dtb_ref, alog_ref, e3_ref)
            xs = xact[:, :SSM_W]
            acum_t = acum.T
            ea_x = jnp.exp(acum_x)
            last_x = acum_x[BLOCK - 1:BLOCK, :]
            xdt = xs * dt_x
            xw = xdt * jnp.exp(last_x - acum_x)
            cd_x = jnp.exp(last_x)
            hprev = hst[...]
            hp_ref[j] = hprev
            sls = [slice(g * SSM_R * SSM_P, (g + 1) * SSM_R * SSM_P) for g in range(SSM_G)]
            bgs = [_bf(xact[:, SSM_W + g * SSM_N:SSM_W + (g + 1) * SSM_N]) for g in range(SSM_G)]
            cgs = [_bf(xact[:, SSM_W + SSM_G * SSM_N + g * SSM_N:SSM_W + SSM_G * SSM_N + (g + 1) * SSM_N])
                   for g in range(SSM_G)]
            xdt_b, xw_b, hprev_b = _bf(xdt), _bf(xw), _bf(hprev)
            cbs = [_dot_nt(cgs[g], bgs[g]) for g in range(SSM_G)]
            for g in range(SSM_G):
                sl = sls[g]
                yoff_s[:, sl] = _dot(cgs[g], hprev_b[:, sl]) * ea_x[:, sl]
                hst[:, sl] = hprev[:, sl] * cd_x[:, sl] + _dot_tn(bgs[g], xw_b[:, sl])
            for g in range(SSM_G):
                hss = [slice((g * SSM_R + r) * SSM_P, (g * SSM_R + r + 1) * SSM_P) for r in range(SSM_R)]
                mms = [_bf(cbs[g] * jnp.exp(jnp.where(trilb, acum[:, g * SSM_R + r:g * SSM_R + r + 1]
                                                      - acum_t[g * SSM_R + r:g * SSM_R + r + 1, :], -1e30)))
                       for r in range(SSM_R)]
                for r in range(SSM_R):
                    yd_s[:, hss[r]] = _dot(mms[r], xdt_b[:, hss[r]])
            y_ref[rs, :] = yd_s[...] + yoff_s[...] + dsk_ref[...] * xs

    blk = lambda w: pl.BlockSpec((rows, w), lambda i: (i, 0))
    return pl.pallas_call(
        body, name="ssd_fwd", grid=(nc // ch,),
        in_specs=[blk(XBC_W), pl.BlockSpec((8, XBC_W), lambda i: (jnp.maximum(i * (rows // 8) - 1, 0), 0)),
                  blk(SSM_HEADS), _full((CONV_K, XBC_W)), _full((1, XBC_W)), _full((1, SSM_HEADS)),
                  _full((1, SSM_HEADS)), _full((1, SSM_W)), _full((3 * SSM_HEADS, SSM_W))],
        out_specs=[blk(SSM_W), pl.BlockSpec((ch, SSM_N, SSM_W), lambda i: (i, 0, 0)), blk(XBC_W)],
        out_shape=[jax.ShapeDtypeStruct((s, SSM_W), F32), jax.ShapeDtypeStruct((nc, SSM_N, SSM_W), F32),
                   jax.ShapeDtypeStruct((s, XBC_W), F32)],
        scratch_shapes=[pltpu.VMEM((SSM_N, SSM_W), F32), pltpu.VMEM((BLOCK, SSM_W), F32), pltpu.VMEM((BLOCK, SSM_W), F32)],
        compiler_params=_params(dimension_semantics=("arbitrary",)),
    )(xbc, xbc, dt_raw, conv_w, conv_b, dt_bias, a_log, dsk_x, e3t)


def _dsilu(z, sg):
    return sg * (1.0 + z * (1.0 - sg))


def _mid(x, tgt, o_att, za, ypre, zm, ga, gb, gate, ssm_nw, rows_all, tm=256):
    s = x.shape[0]
    gw = SSM_W // SSM_G

    r_ap, r_sp = ATTN_W // N_DEV, SSM_W // N_DEV

    def body(x_ref, t_ref, o_ref, za_ref, yp_ref, zm_ref, ga_ref, gb_ref, gate_ref, nw_ref, rows_h,
             dout_ref, do_ref, dza_ref, dyp_ref, dzm_ref, dga_ref, dgb_ref,
             yag_ref, dya_ref, yn_ref, dyb_ref, mg_ref, dob_ref, gnw_ref, dgate_ref, loss_ref,
             wap_v, wsp_v, wout_v, sem):
        i = pl.program_id(0)

        @pl.when(i == 0)
        def _():
            cps = []
            for d in range(N_DEV):
                for j, (dst, r0, rn) in enumerate(((wap_v, 0, r_ap), (wsp_v, r_ap, r_sp), (wout_v, r_ap + r_sp, r_ap))):
                    cps.append(pltpu.make_async_copy(rows_h.at[d, r0:r0 + rn, :], dst.at[d * rn:(d + 1) * rn, :], sem.at[j]))
            for cp in cps:
                cp.start()
            gnw_ref[...] = jnp.zeros_like(gnw_ref)
            dgate_ref[...] = jnp.zeros_like(dgate_ref)
            loss_ref[...] = jnp.zeros_like(loss_ref)
            for cp in cps:
                cp.wait()

        gate = gate_ref[...]
        nw = nw_ref[...]
        o_att = o_ref[...]
        z_a = za_ref[...].astype(F32)
        s_a = _sig(z_a)
        silu_a = z_a * s_a
        yag = _bf(o_att * silu_a)
        yag_ref[...] = yag
        ypre = yp_ref[...]
        z_m = zm_ref[...].astype(F32)
        s_m = _sig(z_m)
        silu_m = z_m * s_m
        yg = ypre * silu_m
        rinv = jnp.concatenate(
            [jnp.broadcast_to(lax.rsqrt(jnp.mean(yg[:, g * gw:(g + 1) * gw] ** 2, axis=-1, keepdims=True) + EPS), (tm, gw))
             for g in range(SSM_G)], axis=1)
        ynr = yg * rinv
        yn = _bf(ynr * nw)
        yn_ref[...] = yn
        y_a = _dot(yag, wap_v[...])
        y_b = _dot(yn, wsp_v[...])
        g_a = _sig(ga_ref[...].astype(F32))
        g_b = _sig(gb_ref[...].astype(F32))
        merged = _bf(g_a * y_a + g_b * y_b)
        mg_ref[...] = merged
        o = _dot(merged, wout_v[...])
        diff = x_ref[...] + gate * o - t_ref[...]
        loss_ref[...] += (0.5 / D_MODEL) * jnp.sum(diff * diff, axis=(0, 1), keepdims=True)
        dout = diff * (1.0 / D_MODEL)
        dout_ref[...] = dout
        dgate_ref[...] += jnp.sum(dout * o, axis=0, keepdims=True)
        d_o = _bf(dout * gate)
        dob_ref[...] = d_o
        dmerged = _dot_nt(d_o, wout_v[...])
        dy_af = dmerged * g_a
        dy_bf = dmerged * g_b
        dy_a = _bf(dy_af)
        dy_b = _bf(dy_bf)
        dya_ref[...] = dy_a
        dyb_ref[...] = dy_b
        dyag = _dot_nt(dy_a, wap_v[...])
        dyn = _dot_nt(dy_b, wsp_v[...])
        dga_ref[...] = _bf(dy_af * y_a * (1.0 - g_a))
        dgb_ref[...] = _bf(dy_bf * y_b * (1.0 - g_b))
        do_ref[...] = dyag * silu_a
        dza_ref[...] = _bf(dyag * o_att * _dsilu(z_a, s_a))
        gnw_ref[...] += jnp.sum(dyn * ynr, axis=0, keepdims=True)
        dynw = dyn * nw
        corr = jnp.concatenate(
            [jnp.broadcast_to(jnp.mean((dynw * ynr)[:, g * gw:(g + 1) * gw], axis=-1, keepdims=True), (tm, gw))
             for g in range(SSM_G)], axis=1)
        dyg = rinv * (dynw - ynr * corr)
        dyp_ref[...] = dyg * silu_m
        dzm_ref[...] = _bf(dyg * ypre * _dsilu(z_m, s_m))

    r1, r2 = _rows(tm, D_MODEL), _rows(tm, SSM_W)
    sd = jax.ShapeDtypeStruct
    return pl.pallas_call(
        body, name="mid", grid=(s // tm,),
        in_specs=[r1, r1, r1, r1, r2, r2, r1, r1, _full((1, D_MODEL)), _full((1, SSM_W)), ANY],
        out_specs=[r1, r1, r1, r2, r2, r1, r1, r1, r1, r2, r1, r1, r1,
                   _full((1, SSM_W)), _full((1, D_MODEL)), _full((1, 1))],
        out_shape=[sd((s, D_MODEL), F32), sd((s, ATTN_W), F32), sd((s, ATTN_W), BF), sd((s, SSM_W), F32),
                   sd((s, SSM_W), BF), sd((s, D_MODEL), BF), sd((s, D_MODEL), BF),
                   sd((s, ATTN_W), BF), sd((s, D_MODEL), BF), sd((s, SSM_W), BF), sd((s, D_MODEL), BF),
                   sd((s, D_MODEL), BF), sd((s, D_MODEL), BF),
                   sd((1, SSM_W), F32), sd((1, D_MODEL), F32), sd((1, 1), F32)],
        scratch_shapes=[pltpu.VMEM((ATTN_W, D_MODEL), BF), pltpu.VMEM((SSM_W, D_MODEL), BF), pltpu.VMEM((D_MODEL, D_MODEL), BF),
                        pltpu.SemaphoreType.DMA((3,))],
        compiler_params=_params(dimension_semantics=("arbitrary",)),
    )(x, tgt, o_att, za, ypre, zm, ga, gb, gate, ssm_nw, rows_all)


def _attn_bwd(q, k, v, bias, sinks, consts, o_att, lse, d_o):
    s = q.shape[0]
    nb = s // BLOCK
    folds = (_fold(ATTN_W, HEAD_DIM), _fold(KV_W, HEAD_DIM))

    def body(q_ref, kp_ref, kc_ref, vp_ref, vc_ref, b_ref, skv_ref, qw_ref, kw_ref, eq_ref, eq3_ref, ek_ref, ek3_ref,
             fq_ref, fk_ref, o_ref, lse_ref, do_ref,
             dq_ref, dk_ref, dv_ref, dss_ref, gqw_ref, gkw_ref, gsk_ref, ckn, cv, dqn_s, dkn_s, dv_s, gq_x, gk_x):
        i = pl.program_id(0)
        kw, ek, ek3 = kw_ref[...], ek_ref[...], ek3_ref[...]

        @pl.when(i == 0)
        def _():
            for ref in (ckn, cv, dss_ref, gq_x, gk_x, gsk_ref):
                ref[...] = jnp.zeros_like(ref)

        @pl.when(i < nb)
        def _():
            qw, eq, eq3 = qw_ref[...], eq_ref[...], eq3_ref[...]
            qf = q_ref[...]
            qnf, rq_x = _heads_norm(qf, qw, eq, eq3)
            qn = _bf(qnf * (HEAD_DIM ** -0.5))
            kf = jnp.concatenate([kp_ref[...], kc_ref[...]], axis=0)
            knf, rk_x = _heads_norm(kf, kw, ek, ek3)
            kn = _bf(knf)
            vv = _bf(jnp.concatenate([vp_ref[...], vc_ref[...]], axis=0))
            d_of = do_ref[...]
            d_ob = _bf(d_of)
            lse_all = lse_ref[...]
            delta = _group_sum(d_of * o_ref[...], eq)
            gsk_ref[...] += jnp.sum(-jnp.exp(skv_ref[...] - lse_all) * delta, axis=0, keepdims=True)
            kss = [slice(hk * HEAD_DIM, (hk + 1) * HEAD_DIM) for hk in range(KV_HEADS)]
            qgs = [_stack_heads(qn, hk) for hk in range(KV_HEADS)]
            d_ogs = [_stack_heads(d_ob, hk) for hk in range(KV_HEADS)]
            scs = [_dot_nt(qgs[hk], kn[:, kss[hk]]) + b_ref[0, hk * GRP:(hk + 1) * GRP].reshape(GRP * BLOCK, 2 * BLOCK)
                   for hk in range(KV_HEADS)]
            dps = [_dot_nt(d_ogs[hk], vv[:, kss[hk]]) for hk in range(KV_HEADS)]
            ps = [jnp.exp(scs[hk] - _stack_cols(lse_all, hk)) for hk in range(KV_HEADS)]
            dss = [ps[hk] * (dps[hk] - _stack_cols(delta, hk)) for hk in range(KV_HEADS)]
            pbs = [_bf(p) for p in ps]
            dsbs = [_bf(ds) for ds in dss]
            for hk in range(KV_HEADS):
                dss_ref[hk * GRP:(hk + 1) * GRP] += dss[hk].reshape(GRP, BLOCK, 2 * BLOCK)
            for hk in range(KV_HEADS):
                dv_s[:, kss[hk]] = _dot_tn(pbs[hk], d_ogs[hk])
                dkn_s[:, kss[hk]] = _dot_tn(dsbs[hk], qgs[hk])
            dqns = [_dot(dsbs[hk], kn[:, kss[hk]]) * (HEAD_DIM ** -0.5) for hk in range(KV_HEADS)]
            for hk in range(KV_HEADS):
                for g in range(GRP):
                    h = hk * GRP + g
                    dqn_s[:, h * HEAD_DIM:(h + 1) * HEAD_DIM] = dqns[hk][g * BLOCK:(g + 1) * BLOCK]
            dq, gq = _heads_norm_bwd(qf, rq_x, qw, dqn_s[...], eq, eq3)
            dq_ref[...] = _bf(dq)
            gq_x[...] += gq
            dk, gk = _heads_norm_bwd(kf[:BLOCK], rk_x[:BLOCK], kw, ckn[...] + dkn_s[0:BLOCK, :], ek, ek3)
            dk_ref[...] = _bf(dk)
            gk_x[...] += gk
            dv_ref[...] = _bf(cv[...] + dv_s[0:BLOCK, :])
            ckn[...] = dkn_s[BLOCK:2 * BLOCK, :]
            cv[...] = dv_s[BLOCK:2 * BLOCK, :]

        @pl.when(i == nb)
        def _():
            kc = kc_ref[...]
            dk, gk = _heads_norm_bwd(kc, _heads_norm(kc, kw, ek, ek3)[1], kw, ckn[...], ek, ek3)
            dk_ref[...] = _bf(dk)
            dv_ref[...] = _bf(cv[...])
            gqw_ref[...] = _group_sum(jnp.broadcast_to(gq_x[...], (8, ATTN_W)), fq_ref[...])[0:1]
            gkw_ref[...] = _group_sum(jnp.broadcast_to(gk_x[...] + gk, (8, KV_W)), fk_ref[...])[0:1]

    last = nb - 1
    cur = lambda w: pl.BlockSpec((BLOCK, w), lambda i: (jnp.minimum(i, last), 0))
    prev = lambda w: pl.BlockSpec((BLOCK, w), lambda i: (jnp.maximum(jnp.minimum(i, last) - 1, 0), 0))
    late = lambda w: pl.BlockSpec((BLOCK, w), lambda i: (jnp.maximum(i - 1, 0), 0))
    sd = jax.ShapeDtypeStruct
    return pl.pallas_call(
        body, name="attn_bwd", grid=(nb + 1,),
        in_specs=[cur(ATTN_W), prev(KV_W), cur(KV_W), prev(KV_W), cur(KV_W),
                  pl.BlockSpec((1, ATTN_HEADS, BLOCK, 2 * BLOCK), lambda i: (jnp.minimum(i, 1), 0, 0, 0)),
                  _full((1, ATTN_HEADS))]
                 + [_full(c.shape) for c in consts + folds] + [cur(ATTN_W), cur(ATTN_HEADS), cur(ATTN_W)],
        out_specs=[cur(ATTN_W), late(KV_W), late(KV_W),
                   pl.BlockSpec((ATTN_HEADS, BLOCK, 2 * BLOCK), lambda i: (0, 0, 0)),
                   _full((1, HEAD_DIM)), _full((1, HEAD_DIM)), _full((1, ATTN_HEADS))],
        out_shape=[sd((s, ATTN_W), BF), sd((s, KV_W), BF), sd((s, KV_W), BF),
                   sd((ATTN_HEADS, BLOCK, 2 * BLOCK), F32), sd((1, HEAD_DIM), F32), sd((1, HEAD_DIM), F32),
                   sd((1, ATTN_HEADS), F32)],
        scratch_shapes=[pltpu.VMEM((BLOCK, KV_W), F32), pltpu.VMEM((BLOCK, KV_W), F32),
                        pltpu.VMEM((BLOCK, ATTN_W), F32), pltpu.VMEM((2 * BLOCK, KV_W), F32),
                        pltpu.VMEM((2 * BLOCK, KV_W), F32), pltpu.VMEM((1, ATTN_W), F32), pltpu.VMEM((1, KV_W), F32)],
        compiler_params=_params(dimension_semantics=("arbitrary",)),
    )(q, k, k, v, v, bias, sinks, *consts, *folds, o_att, lse, d_o)


def _ssd_bwd(xbc, conv_all, dt_raw, conv_w, dt_bias, a_log, dsk_x, e_mat, e3t, hprev_all, dy_all):
    s = xbc.shape[0]
    nc = s // BLOCK
    ch = SSD_CH if nc % SSD_CH == 0 else 1
    rows = ch * BLOCK
    nsteps = nc // ch
    gw = SSM_R * SSM_P
    b0, c0 = SSM_W, SSM_W + SSM_G * SSM_N

    def body(x_ref, conv_ref, dtr_ref, cw_ref, dtb_ref, alog_ref, dsk_ref, e_ref, e3_ref, hp_ref, dy_ref,
             dx_ref, ddt_ref, gcw_ref, gcb_ref, gdtb_ref, galog_ref, gdsk_ref,
             dh, nhead, gdskx, dxdt_s, dbc_s, dxd_s):
        def chunk_bwd(j):
            rs = slice(j * BLOCK, (j + 1) * BLOCK)
            conv = conv_ref[rs, :]
            sg, xact, u, dt, a, trilb, acum, dt_x, acum_x = _ssd_common(conv, dtr_ref[rs, :], dtb_ref, alog_ref, e3_ref)
            xs = xact[:, :SSM_W]
            acum_t = acum.T
            ea_x = jnp.exp(acum_x)
            last_x = acum_x[BLOCK - 1:BLOCK, :]
            dte_x = jnp.exp(last_x - acum_x)
            cd_x = jnp.exp(last_x)
            xdt = xs * dt_x
            xw = xdt * dte_x
            hprev = hp_ref[j]
            dhn = dh[...]
            dy = dy_ref[rs, :]
            gdskx[...] += jnp.sum(dy * xs, axis=0, keepdims=True)
            dyea = dy * ea_x
            lane = lax.broadcasted_iota(jnp.int32, (BLOCK, SSM_HEADS), 1)
            dacum = jnp.zeros((BLOCK, SSM_HEADS), F32)
            dacc_x, dlast_x = [], []
            sls = [slice(g * gw, (g + 1) * gw) for g in range(SSM_G)]
            bgs = [_bf(xact[:, b0 + g * SSM_N:b0 + (g + 1) * SSM_N]) for g in range(SSM_G)]
            cgs = [_bf(xact[:, c0 + g * SSM_N:c0 + (g + 1) * SSM_N]) for g in range(SSM_G)]
            hpgs = [_bf(hprev[:, sl]) for sl in sls]
            dhgs = [_bf(dhn[:, sl]) for sl in sls]
            dyeags = [_bf(dyea[:, sl]) for sl in sls]
            xwgs = [_bf(xw[:, sl]) for sl in sls]
            cbs = [_dot_nt(cgs[g], bgs[g]) for g in range(SSM_G)]
            gmats = [_dot(cgs[g], hpgs[g]) for g in range(SSM_G)]
            dxws = [_dot(bgs[g], dhgs[g]) for g in range(SSM_G)]
            dcgs = [_dot_nt(dyeags[g], hpgs[g]) for g in range(SSM_G)]
            dbgs = [_dot_nt(xwgs[g], dhgs[g]) for g in range(SSM_G)]
            for g in range(SSM_G):
                sl = sls[g]
                dh[:, sl] = dhn[:, sl] * cd_x[:, sl] + _dot_tn(cgs[g], dyeags[g])
                dxdt_s[:, sl] = dxws[g] * dte_x[:, sl]
                dacc_x.append(dy[:, sl] * gmats[g] * ea_x[:, sl] - dxws[g] * xw[:, sl])
                dlast_x.append(jnp.sum(dxws[g] * xw[:, sl], axis=0, keepdims=True)
                               + jnp.sum(dhn[:, sl] * hprev[:, sl], axis=0, keepdims=True) * cd_x[:, sl])
            for g in range(SSM_G):
                bg, cg, cb, dbg, dcg = bgs[g], cgs[g], cbs[g], dbgs[g], dcgs[g]
                hss = [slice((g * SSM_R + r) * SSM_P, (g * SSM_R + r + 1) * SSM_P) for r in range(SSM_R)]
                lms = [jnp.exp(jnp.where(trilb, acum[:, g * SSM_R + r:g * SSM_R + r + 1]
                                         - acum_t[g * SSM_R + r:g * SSM_R + r + 1, :], -1e30)) for r in range(SSM_R)]
                mms = [cb * lm for lm in lms]
                dyhs = [_bf(dy[:, hs]) for hs in hss]
                dms = [_dot_nt(dyhs[r], _bf(xdt[:, hss[r]])) for r in range(SSM_R)]
                for r in range(SSM_R):
                    dxd_s[:, hss[r]] = _dot_tn(_bf(mms[r]), dyhs[r])
                dcb = sum(dms[r] * lms[r] for r in range(SSM_R))
                wms = [dms[r] * mms[r] for r in range(SSM_R)]
                antis = [wm - wm.T for wm in wms]
                for r in range(SSM_R):
                    dacum = dacum + _group_sum(antis[r], (lane == g * SSM_R + r).astype(BF))
                dcbb = _bf(dcb)
                dbc_s[:, g * SSM_N:(g + 1) * SSM_N] = dbg + _dot_tn(dcbb, cg)
                dbc_s[:, SSM_G * SSM_N + g * SSM_N:SSM_G * SSM_N + (g + 1) * SSM_N] = dcg + _dot(dcbb, bg)
            dxdt = dxdt_s[...] + dxd_s[...]
            dxs = dy * dsk_ref[...] + dxdt * dt_x
            red = _group_sum(jnp.concatenate(
                [dxdt * xs, jnp.concatenate(dacc_x, axis=1),
                 jnp.broadcast_to(jnp.concatenate(dlast_x, axis=1), (8, SSM_W))], axis=0), e_ref[...])
            row = lax.broadcasted_iota(jnp.int32, (BLOCK, SSM_HEADS), 0)
            dacum = dacum + red[BLOCK:2 * BLOCK] + jnp.where(row == BLOCK - 1, red[2 * BLOCK:2 * BLOCK + 1], 0.0)
            ddta = _exact_left(_triu().astype(BF), dacum)
            ddt = red[:BLOCK] + ddta * a
            galog_ref[...] += jnp.sum(ddta * dt, axis=0, keepdims=True) * a
            du = ddt * _sig(u)
            ddt_ref[rs, :] = _bf(du)
            gdtb_ref[...] += jnp.sum(du, axis=0, keepdims=True)
            dconv = jnp.concatenate([dxs, dbc_s[...]], axis=1) * _dsilu(conv, sg)
            gcb_ref[...] += jnp.sum(dconv, axis=0, keepdims=True)
            ext2 = jnp.concatenate([dconv, nhead[...]], axis=0)
            ahead = [pltpu.roll(ext2, BLOCK + 8 - (CONV_K - 1 - j), axis=0)[0:BLOCK] if j < CONV_K - 1 else dconv
                     for j in range(CONV_K)]
            dx_ref[rs, :] = _bf(sum(ahead[j] * cw_ref[j:j + 1, :] for j in range(CONV_K)))
            xraw = x_ref[rs, :]
            gcw_ref[...] += jnp.concatenate([jnp.sum(ahead[j] * xraw, axis=0, keepdims=True) for j in range(CONV_K)], axis=0)
            nhead[...] = dconv[0:8]

        i = pl.program_id(0)

        @pl.when(i == 0)
        def _():
            for ref in (dh, nhead, gdskx, gcw_ref, gcb_ref, gdtb_ref, galog_ref, gdsk_ref):
                ref[...] = jnp.zeros_like(ref)

        for j in reversed(range(ch)):
            chunk_bwd(j)

        @pl.when(i == nsteps - 1)
        def _():
            gdsk_ref[...] = _group_sum(jnp.broadcast_to(gdskx[...], (8, SSM_W)), e_ref[...])[0:1]

    chunk = lambda w: pl.BlockSpec((rows, w), lambda i: (nsteps - 1 - i, 0))
    sd = jax.ShapeDtypeStruct
    return pl.pallas_call(
        body, name="ssd_bwd", grid=(nsteps,),
        in_specs=[chunk(XBC_W), chunk(XBC_W),
                  chunk(SSM_HEADS), _full((CONV_K, XBC_W)), _full((1, SSM_HEADS)),
                  _full((1, SSM_HEADS)), _full((1, SSM_W)), _full((SSM_W, SSM_HEADS)), _full((3 * SSM_HEADS, SSM_W)),
                  pl.BlockSpec((ch, SSM_N, SSM_W), lambda i: (nsteps - 1 - i, 0, 0)), chunk(SSM_W)],
        out_specs=[chunk(XBC_W), chunk(SSM_HEADS), _full((CONV_K, XBC_W)), _full((1, XBC_W)),
                   _full((1, SSM_HEADS)), _full((1, SSM_HEADS)), _full((1, SSM_HEADS))],
        out_shape=[sd((s, XBC_W), BF), sd((s, SSM_HEADS), BF), sd((CONV_K, XBC_W), F32), sd((1, XBC_W), F32),
                   sd((1, SSM_HEADS), F32), sd((1, SSM_HEADS), F32), sd((1, SSM_HEADS), F32)],
        scratch_shapes=[pltpu.VMEM((SSM_N, SSM_W), F32), pltpu.VMEM((8, XBC_W), F32),
                        pltpu.VMEM((1, SSM_W), F32), pltpu.VMEM((BLOCK, SSM_W), F32),
                        pltpu.VMEM((BLOCK, 2 * SSM_G * SSM_N), F32), pltpu.VMEM((BLOCK, SSM_W), F32)],
        compiler_params=_params(dimension_semantics=("arbitrary",)),
    )(xbc, conv_all, dt_raw, conv_w, dt_bias, a_log, dsk_x, e_mat, e3t, hprev_all, dy_all)


def _dh(x, dout, norm_w, scale, dsegs, w_t, tm=256):
    s = x.shape[0]

    def body(x_ref, dout_ref, nw_ref, sc_ref, *rest):
        d_refs, w_hbm = rest[:9], rest[9]
        gx_ref, dshift_ref, dscale_ref, gnw_ref = rest[10:14]
        w_vm, sem = rest[14], rest[15]
        first = pl.program_id(0) == 0
        cps = [pltpu.make_async_copy(w_hbm.at[SEG_OFF[j]:SEG_OFF[j + 1], :], w_vm.at[SEG_OFF[j]:SEG_OFF[j + 1], :], sem.at[j])
               for j in range(9)]

        def tile(waiting):
            dh = None
            for j in range(9):
                if waiting:
                    cps[j].wait()
                part = _dot(d_refs[j][...], w_vm[SEG_OFF[j]:SEG_OFF[j + 1], :])
                dh = part if dh is None else dh + part
            xv = x_ref[...]
            r = lax.rsqrt(jnp.mean(xv * xv, axis=-1, keepdims=True) + EPS)
            xn = xv * r
            nw = nw_ref[...]
            sc1 = 1.0 + sc_ref[...]
            dshift_ref[...] += jnp.sum(dh, axis=0, keepdims=True)
            dhxn = jnp.sum(dh * xn, axis=0, keepdims=True)
            dscale_ref[...] += dhxn * nw
            gnw_ref[...] += dhxn * sc1
            dxn = dh * (nw * sc1)
            gx_ref[...] = dout_ref[...] + r * (dxn - xn * jnp.mean(xn * dxn, axis=-1, keepdims=True))

        @pl.when(first)
        def _():
            for cp in cps:
                cp.start()
            for ref in (dshift_ref, dscale_ref, gnw_ref):
                ref[...] = jnp.zeros_like(ref)
            tile(True)

        @pl.when(jnp.logical_not(first))
        def _():
            tile(False)

    vec = _full((1, D_MODEL))
    sd = jax.ShapeDtypeStruct
    return pl.pallas_call(
        body, name="dh", grid=(s // tm,),
        in_specs=[_rows(tm, D_MODEL), _rows(tm, D_MODEL), vec, vec] + [_rows(tm, w) for w in SEG_W] + [ANY],
        out_specs=[_rows(tm, D_MODEL), vec, vec, vec],
        out_shape=[sd((s, D_MODEL), F32), sd((1, D_MODEL), F32), sd((1, D_MODEL), F32), sd((1, D_MODEL), F32)],
        scratch_shapes=[pltpu.VMEM((IN_W, D_MODEL), BF), pltpu.SemaphoreType.DMA((9,))],
        compiler_params=_params(dimension_semantics=("arbitrary",)),
    )(x, dout, norm_w, scale, *dsegs, w_t)


def _gw_seg(h, dseg, name, tm=1024):
    s, w = dseg.shape
    tn = min(w, 1024)
    tm = min(tm, s)
    nm = s // tm

    def body(h_ref, d_ref, o_ref, acc):
        m = pl.program_id(1)

        @pl.when(m == 0)
        def _():
            acc[...] = jnp.zeros_like(acc)

        acc[...] += _dot_tn(d_ref[...], h_ref[...])

        @pl.when(m == nm - 1)
        def _():
            o_ref[...] = _bf(acc[...])

    return pl.pallas_call(
        body, name=name, grid=(w // tn, nm),
        in_specs=[pl.BlockSpec((tm, D_MODEL), lambda n, m: (m, 0)), pl.BlockSpec((tm, tn), lambda n, m: (m, n))],
        out_specs=pl.BlockSpec((tn, D_MODEL), lambda n, m: (n, 0)),
        out_shape=jax.ShapeDtypeStruct((w, D_MODEL), BF),
        scratch_shapes=[pltpu.VMEM((tn, D_MODEL), F32)],
        compiler_params=_params(dimension_semantics=("arbitrary", "arbitrary")),
    )(h, dseg)


def _gw_in(h, dsegs):
    return [_gw_seg(h, d, "gw_in_%d" % j) for j, d in enumerate(dsegs)]


def _local_step(x, tgt, shift, scale, gate, w_t, rows_fn, norm_w, qnw, knw, rel_bias, sinks,
                conv_w, conv_b, dt_bias, a_log, d_skip, ssm_nw, after_mid=None, after_gw=None):
    oh_t = _bucket_onehot_t()
    bias = _masked_bias(_bias_dense(rel_bias.T, oh_t).reshape(ATTN_HEADS, BLOCK, 2 * BLOCK))
    *segs, h = _inproj(x, norm_w, scale, shift, w_t)
    q, k, v, za, zm, xbc, dtr, ga, gb = segs
    consts = _attn_consts(qnw, knw)
    o_att, lse = _attn_fwd(q, k, v, bias, sinks, consts)
    e_mat, e3t = _membership(SSM_W, SSM_P, SSM_HEADS)
    dsk_x = jnp.repeat(d_skip, SSM_P, axis=1)
    ypre, hprev, conv = _ssd_fwd(xbc, dtr, conv_w, conv_b, dt_bias, a_log, dsk_x, e3t)
    (dout, d_o, dza, dyp, dzm, dga, dgb, yag, dy_a, yn, dy_b, merged, dob, g_ssm_nw, dgate, loss) = _mid(
        x, tgt, o_att, za, ypre, zm, ga, gb, gate, ssm_nw, rows_fn(ypre))
    g_wap = _gw_seg(dy_a, yag, "gw_attn_proj")
    g_wsp = _gw_seg(dy_b, yn, "gw_ssm_proj")
    g_wout = _gw_seg(dob, merged, "gw_out")
    zero = after_mid(g_wap, g_wsp, g_wout) if after_mid is not None else 0.0
    dq, dk, dv, dss, g_qnw, g_knw, g_sinks = _attn_bwd(q, k, v, bias, sinks + zero, consts, o_att, lse, d_o)
    g_rel = _bias_grad(dss.reshape(ATTN_HEADS, BLOCK * 2 * BLOCK), oh_t).T
    dxbc, ddt, g_cw, g_cb, g_dtb, g_alog, g_dsk = _ssd_bwd(
        xbc, conv, dtr, conv_w, dt_bias, a_log, dsk_x, e_mat, e3t, hprev, dyp)
    dsegs = (dq, dk, dv, dza, dzm, dxbc, ddt, dga, dgb)
    g_ws = _gw_in(h, dsegs)
    zero = after_gw(g_ws) if after_gw is not None else 0.0
    gx, dshift, dscale, g_nw = _dh(x, dout, norm_w + zero, scale, dsegs, w_t)
    return dict(loss=loss, grad_x=gx, dmod=jnp.concatenate([dshift, dscale, dgate], axis=1), g_ws=g_ws,
                g_wap=g_wap, g_wsp=g_wsp, g_wout=g_wout, g_norm_w=g_nw, g_qnw=g_qnw, g_knw=g_knw, g_rel=g_rel,
                g_sinks=g_sinks, g_conv_w=g_cw, g_conv_b=g_cb, g_dt_bias=g_dtb, g_a_log=g_alog, g_d_skip=g_dsk,
                g_ssm_nw=g_ssm_nw)


def _me():
    return lax.axis_index("x"), lax.axis_index("y"), lax.axis_index("c")


def _flip(v, bit):
    return 1 - v if bit else v


def _ag_direct(v, name):
    def body(v_ref, out_ref, send_sems, recv_sems, local_sem):
        x, y, c = _me()
        me = 4 * x + 2 * y + c
        mine = pltpu.make_async_copy(v_ref, out_ref.at[me], local_sem)
        mine.start()
        peers = [(_flip(x, k >> 2 & 1), _flip(y, k >> 1 & 1), _flip(c, k & 1)) for k in range(1, N_DEV)]
        sends = [pltpu.make_async_remote_copy(
            src_ref=v_ref, dst_ref=out_ref.at[me], send_sem=send_sems.at[j], recv_sem=recv_sems.at[j],
            device_id=p, device_id_type=MESH) for j, p in enumerate(peers)]
        for cp in sends:
            cp.start()
        for j, (px, py, pc) in enumerate(peers):
            pltpu.make_async_remote_copy(
                src_ref=v_ref, dst_ref=out_ref.at[4 * px + 2 * py + pc], send_sem=send_sems.at[j],
                recv_sem=recv_sems.at[j], device_id=(px, py, pc), device_id_type=MESH).wait_recv()
        for cp in sends:
            cp.wait_send()
        mine.wait()

    vm = pl.BlockSpec(memory_space=pltpu.VMEM)
    return pl.pallas_call(
        body, name=name, out_shape=jax.ShapeDtypeStruct((N_DEV,) + v.shape, v.dtype),
        in_specs=[vm], out_specs=vm,
        scratch_shapes=[pltpu.SemaphoreType.DMA((N_DEV - 1,)), pltpu.SemaphoreType.DMA((N_DEV - 1,)),
                        pltpu.SemaphoreType.DMA],
        compiler_params=_params(),
    )(v)


def _ag_two_level(v, name):
    def body(v_ref, out_ref, token, send_sems, recv_sems, local_sem):
        token[...] = jnp.zeros_like(token)
        x, y, c = _me()
        me, sibling = (x, y, c), (x, y, 1 - c)
        chips = [(1 - x, y), (x, 1 - y), (1 - x, 1 - y)]

        def slot(px, py, pc):
            return out_ref.at[4 * px + 2 * py + pc]

        def copy(k, block, to, src=None):
            return pltpu.make_async_remote_copy(
                src_ref=slot(*block) if src is None else src, dst_ref=slot(*block),
                send_sem=send_sems.at[k], recv_sem=recv_sems.at[k], device_id=to, device_id_type=MESH)

        mine = pltpu.make_async_copy(v_ref, slot(*me), local_sem)
        mine.start()
        first = [copy(0, me, sibling, src=v_ref)]
        first += [copy(1 + j, me, (*chip, c), src=v_ref) for j, chip in enumerate(chips)]
        for cp in first:
            cp.start()
        passed = [copy(4 + j, (*chip, c), sibling) for j, chip in enumerate(chips)]
        for j, chip in enumerate(chips):
            copy(1 + j, (*chip, c), me).wait_recv()
            passed[j].start()
        copy(0, sibling, me).wait_recv()
        for j, chip in enumerate(chips):
            copy(4 + j, (*chip, 1 - c), me).wait_recv()
        for cp in first + passed:
            cp.wait_send()
        mine.wait()

    out, token = pl.pallas_call(
        body, name=name,
        out_shape=(jax.ShapeDtypeStruct((N_DEV,) + v.shape, v.dtype), jax.ShapeDtypeStruct((8, 128), v.dtype)),
        in_specs=[ANY], out_specs=(ANY, pl.BlockSpec(memory_space=pltpu.VMEM)),
        scratch_shapes=[pltpu.SemaphoreType.DMA((7,)), pltpu.SemaphoreType.DMA((7,)), pltpu.SemaphoreType.DMA],
        compiler_params=_params(),
    )(v)
    return out, token[0:1, 0:1]


def _rs_sibling(g, name):
    def body(g_ref, out_ref, send_sems, recv_sems):
        x, y, c = _me()
        cps = [pltpu.make_async_remote_copy(
            src_ref=g_ref.at[2 * ch + 1 - c], dst_ref=out_ref.at[ch], send_sem=send_sems.at[ch],
            recv_sem=recv_sems.at[ch], device_id=(x, y, 1 - c), device_id_type=MESH) for ch in range(4)]
        for cp in cps:
            cp.start()
        for cp in cps:
            cp.wait()

    return pl.pallas_call(
        body, name=name, out_shape=jax.ShapeDtypeStruct((4,) + g.shape[1:], g.dtype),
        in_specs=[ANY], out_specs=ANY,
        scratch_shapes=[pltpu.SemaphoreType.DMA((4,)), pltpu.SemaphoreType.DMA((4,))],
        compiler_params=_params(),
    )(g)


def _add_sibling(g, got, name):
    _, r, n = g.shape
    tr = min(r, 256)

    def body(c_ref, a_ref, b_ref, o_ref):
        o_ref[...] = a_ref[...] + b_ref[...]

    grid_spec = pltpu.PrefetchScalarGridSpec(
        num_scalar_prefetch=1, grid=(4, r // tr),
        in_specs=[pl.BlockSpec((1, tr, n), lambda ch, i, c_ref: (2 * ch + c_ref[0], i, 0)),
                  pl.BlockSpec((1, tr, n), lambda ch, i, c_ref: (ch, i, 0))],
        out_specs=pl.BlockSpec((1, tr, n), lambda ch, i, c_ref: (ch, i, 0)))
    return pl.pallas_call(
        body, name=name, grid_spec=grid_spec, out_shape=jax.ShapeDtypeStruct((4, r, n), g.dtype),
        compiler_params=_params(dimension_semantics=("arbitrary", "arbitrary")),
    )(lax.axis_index("c").reshape(1).astype(jnp.int32), g, got)


def _rs_chips(p, name):
    def body(p_ref, out_ref, send_sems, recv_sems, local_sem):
        x, y, c = _me()
        my_chip = 2 * x + y
        mine = pltpu.make_async_copy(p_ref.at[my_chip], out_ref.at[my_chip], local_sem)
        mine.start()
        chips = [(1 - x, y), (x, 1 - y), (1 - x, 1 - y)]
        sends = [pltpu.make_async_remote_copy(
            src_ref=p_ref.at[2 * px + py], dst_ref=out_ref.at[my_chip], send_sem=send_sems.at[j],
            recv_sem=recv_sems.at[j], device_id=(px, py, c), device_id_type=MESH) for j, (px, py) in enumerate(chips)]
        for cp in sends:
            cp.start()
        for j, (px, py) in enumerate(chips):
            pltpu.make_async_remote_copy(
                src_ref=p_ref.at[my_chip], dst_ref=out_ref.at[2 * px + py], send_sem=send_sems.at[j],
                recv_sem=recv_sems.at[j], device_id=(px, py, c), device_id_type=MESH).wait_recv()
        for cp in sends:
            cp.wait_send()
        mine.wait()

    return pl.pallas_call(
        body, name=name, out_shape=jax.ShapeDtypeStruct(p.shape, p.dtype),
        in_specs=[ANY], out_specs=ANY,
        scratch_shapes=[pltpu.SemaphoreType.DMA((3,)), pltpu.SemaphoreType.DMA((3,)), pltpu.SemaphoreType.DMA],
        compiler_params=_params(),
    )(p)


HBM = pl.BlockSpec(memory_space=pltpu.HBM)
SEM = pl.BlockSpec(memory_space=pltpu.SEMAPHORE)
EFFECT = pltpu.SideEffectType.DATAFLOW_SIDE_EFFECTING


def _peers(x, y, c):
    return [(_flip(x, k >> 2 & 1), _flip(y, k >> 1 & 1), _flip(c, k & 1)) for k in range(1, N_DEV)]


def _exchange_start(src, land, gather, name):
    def body(src_ref, land_ref, send_sems, recv_sems, src_thru, land_thru, token):
        x, y, c = _me()
        me = 4 * x + 2 * y + c
        for j, (px, py, pc) in enumerate(_peers(x, y, c)):
            pltpu.make_async_remote_copy(
                src_ref=src_ref if gather else src_ref.at[4 * px + 2 * py + pc], dst_ref=land_ref.at[me],
                send_sem=send_sems.at[j], recv_sem=recv_sems.at[j], device_id=(px, py, pc), device_id_type=MESH).start()
        token[...] = jnp.zeros_like(token)

    sems = pltpu.SemaphoreType.DMA((N_DEV - 1,))
    out = pl.pallas_call(
        body, name=name,
        out_shape=(sems, sems, pltpu.HBM(src.shape, src.dtype), pltpu.HBM(land.shape, land.dtype),
                   jax.ShapeDtypeStruct((8, 128), F32)),
        in_specs=(HBM, HBM), out_specs=(SEM, SEM, HBM, HBM, pl.BlockSpec(memory_space=pltpu.VMEM)),
        input_output_aliases={0: 2, 1: 3},
        compiler_params=pltpu.CompilerParams(has_side_effects=EFFECT),
    )(pltpu.with_memory_space_constraint(src, pltpu.HBM), pltpu.with_memory_space_constraint(land, pltpu.HBM))
    return out[:4], out[4][0, 0]


def _exchange_wait(started, after, gather, name):
    send_sems, recv_sems, src_thru, land_thru = started

    def body(src_ref, land_ref, send_sems, recv_sems, after_ref, src_dead, got_ref):
        x, y, c = _me()
        for j, (px, py, pc) in enumerate(_peers(x, y, c)):
            pid = 4 * px + 2 * py + pc
            cp = pltpu.make_async_remote_copy(
                src_ref=src_ref if gather else src_ref.at[pid], dst_ref=land_ref.at[pid],
                send_sem=send_sems.at[j], recv_sem=recv_sems.at[j], device_id=(px, py, pc), device_id_type=MESH)
            cp.wait_send()
            cp.wait_recv()

    return pl.pallas_call(
        body, name=name,
        out_shape=(pltpu.HBM(src_thru.shape, src_thru.dtype), pltpu.HBM(land_thru.shape, land_thru.dtype)),
        in_specs=(HBM, HBM, SEM, SEM, ANY), out_specs=(HBM, HBM), input_output_aliases={0: 0, 1: 1},
        compiler_params=pltpu.CompilerParams(has_side_effects=EFFECT),
    )(src_thru, land_thru, send_sems, recv_sems, after)[1]


def _reduce_scatter(g, name):
    got = _rs_sibling(g, name + "_sib")
    return _rs_chips(_add_sibling(g, got, name + "_add"), name + "_chips")


def _silu(a):
    return a * _sig(a)


def _mod_piece(c_all, w_ada, b_piece):
    def body(c_ref, w_ref, b_ref, o_ref):
        o_ref[...] = _dot(_bf(_silu(c_ref[...])), _bf(w_ref[...])) + b_ref[...]

    return pl.pallas_call(
        body, name="mod_piece", out_shape=jax.ShapeDtypeStruct((c_all.shape[0], w_ada.shape[1]), F32),
        compiler_params=_params(),
    )(c_all, w_ada, b_piece)


def _gw_ada(c_all, dmod_piece):
    def body(c_ref, d_ref, o_ref):
        o_ref[...] = _dot_tn(_bf(_silu(c_ref[...])), _bf(d_ref[...]))

    return pl.pallas_call(
        body, name="gw_ada", out_shape=jax.ShapeDtypeStruct((c_all.shape[1], dmod_piece.shape[1]), F32),
        compiler_params=_params(),
    )(c_all, dmod_piece)


def _adam(parts, w, m, v, name):
    k, r, n = parts.shape
    if r <= 256 or r % 256 == 0:
        tr, tn = min(r, 256), n
    else:
        tr, tn = r, 256
    assert r % tr == 0 and n % tn == 0

    def body(p_ref, w_ref, m_ref, v_ref, g_ref, d_ref, nm_ref, nv_ref):
        g = p_ref[0].astype(F32)
        for j in range(1, k):
            g = g + p_ref[j].astype(F32)
        g_ref[...] = g
        d_ref[...], nm_ref[...], nv_ref[...] = _adam_math(g, w_ref[...], m_ref[...], v_ref[...])

    blk = pl.BlockSpec((tr, tn), lambda i, j: (i, j))
    return pl.pallas_call(
        body, name=name, grid=(r // tr, n // tn),
        in_specs=[pl.BlockSpec((k, tr, tn), lambda i, j: (0, i, j)), blk, blk, blk],
        out_specs=[blk, blk, blk, blk],
        out_shape=[jax.ShapeDtypeStruct((r, n), F32)] * 4,
        compiler_params=_params(dimension_semantics=("arbitrary", "arbitrary")),
    )(parts, w, m, v)


def _adam_math(g, w, m, v):
    m_new = ADAM_B1 * m + (1.0 - ADAM_B1) * g
    v_new = ADAM_B2 * v + (1.0 - ADAM_B2) * jnp.square(g)
    m_hat = m_new / (1.0 - ADAM_B1 ** ADAM_STEP)
    v_hat = v_new / (1.0 - ADAM_B2 ** ADAM_STEP)
    return -ADAM_LR * (m_hat / (jnp.sqrt(v_hat) + ADAM_EPS) + ADAM_WD * w), m_new, v_new


_SMALL = (("b_ada", 3 * D_MODEL), ("norm_w", D_MODEL), ("q_norm_w", HEAD_DIM), ("k_norm_w", HEAD_DIM),
          ("rel_bias", REL_BUCKETS * ATTN_HEADS), ("sinks", ATTN_HEADS), ("conv_b", XBC_W), ("dt_bias", SSM_HEADS),
          ("a_log", SSM_HEADS), ("d_skip", SSM_HEADS), ("ssm_norm_w", SSM_W))
_SLOT = tuple(-(-n // 128) * 128 for _, n in _SMALL)
_SLOT_OFF = tuple(int(o) for o in np.cumsum((0,) + _SLOT))
_LOSS_OFF = _SLOT_OFF[-1]
_CW_OFF = _LOSS_OFF + 128
_PACK_N = _CW_OFF + CONV_K * XBC_W


def _pack_partials(small, loss, g_conv_w):
    parts = []
    for (name, n), slot in zip(_SMALL, _SLOT):
        parts.append(small[name].reshape(1, n))
        if slot > n:
            parts.append(jnp.zeros((1, slot - n), F32))
    parts += [loss.reshape(1, 1), jnp.zeros((1, 127), F32), g_conv_w.reshape(1, CONV_K * XBC_W)]
    return jnp.concatenate(parts, axis=1)


def _adam_small(pack_all, w, m, v):
    names = [name for name, _ in _SMALL]

    def body(p_ref, *rest):
        ins, outs = rest[:3 * len(names)], rest[3 * len(names):]

        def total(off, n):
            g = p_ref[0, :, off:off + n]
            for d in range(1, N_DEV):
                g = g + p_ref[d, :, off:off + n]
            return g

        for j, (name, n) in enumerate(_SMALL):
            g = total(_SLOT_OFF[j], n)
            delta, m_new, v_new = _adam_math(g, ins[3 * j][...], ins[3 * j + 1][...], ins[3 * j + 2][...])
            outs[4 * j][...] = g
            outs[4 * j + 1][...] = delta
            outs[4 * j + 2][...] = m_new
            outs[4 * j + 3][...] = v_new
        outs[-1][...] = total(_LOSS_OFF, 1)

    flat = []
    for name, n in _SMALL:
        flat += [w[name].reshape(1, n), m[name].reshape(1, n), v[name].reshape(1, n)]
    out_shape = [jax.ShapeDtypeStruct((1, n), F32) for _, n in _SMALL for _ in range(4)] + [jax.ShapeDtypeStruct((1, 1), F32)]
    out = pl.pallas_call(body, name="adam_small", out_shape=out_shape, compiler_params=_params())(pack_all, *flat)
    res = {name: [out[4 * j + t].reshape(w[name].shape) for t in range(4)] for j, name in enumerate(names)}
    return res, out[-1]


WEIGHTS = ("w_ada", "b_ada", "norm_w", "w_in", "q_norm_w", "k_norm_w", "rel_bias", "sinks", "conv_w", "conv_b",
           "dt_bias", "a_log", "d_skip", "ssm_norm_w", "w_attn_proj", "w_ssm_proj", "w_out")


def kernel(x, c, w_ada, b_ada, norm_w, w_in, q_norm_w, k_norm_w, rel_bias, sinks, conv_w, conv_b, dt_bias, a_log, d_skip, ssm_norm_w, w_attn_proj, w_ssm_proj, w_out, loss_target, m_w_ada, m_b_ada, m_norm_w, m_w_in, m_q_norm_w, m_k_norm_w, m_rel_bias, m_sinks, m_conv_w, m_conv_b, m_dt_bias, m_a_log, m_d_skip, m_ssm_norm_w, m_w_attn_proj, m_w_ssm_proj, m_w_out, v_w_ada, v_b_ada, v_norm_w, v_w_in, v_q_norm_w, v_k_norm_w, v_rel_bias, v_sinks, v_conv_w, v_conv_b, v_dt_bias, v_a_log, v_d_skip, v_ssm_norm_w, v_w_attn_proj, v_w_ssm_proj, v_w_out):
    w = dict(w_ada=w_ada, b_ada=b_ada, norm_w=norm_w, w_in=w_in, q_norm_w=q_norm_w, k_norm_w=k_norm_w,
             rel_bias=rel_bias, sinks=sinks, conv_w=conv_w, conv_b=conv_b, dt_bias=dt_bias, a_log=a_log,
             d_skip=d_skip, ssm_norm_w=ssm_norm_w, w_attn_proj=w_attn_proj, w_ssm_proj=w_ssm_proj, w_out=w_out)
    m = dict(w_ada=m_w_ada, b_ada=m_b_ada, norm_w=m_norm_w, w_in=m_w_in, q_norm_w=m_q_norm_w, k_norm_w=m_k_norm_w,
             rel_bias=m_rel_bias, sinks=m_sinks, conv_w=m_conv_w, conv_b=m_conv_b, dt_bias=m_dt_bias, a_log=m_a_log,
             d_skip=m_d_skip, ssm_norm_w=m_ssm_norm_w, w_attn_proj=m_w_attn_proj, w_ssm_proj=m_w_ssm_proj, w_out=m_w_out)
    v = dict(w_ada=v_w_ada, b_ada=v_b_ada, norm_w=v_norm_w, w_in=v_w_in, q_norm_w=v_q_norm_w, k_norm_w=v_k_norm_w,
             rel_bias=v_rel_bias, sinks=v_sinks, conv_w=v_conv_w, conv_b=v_conv_b, dt_bias=v_dt_bias, a_log=v_a_log,
             d_skip=v_d_skip, ssm_norm_w=v_ssm_norm_w, w_attn_proj=v_w_attn_proj, w_ssm_proj=v_w_ssm_proj, w_out=v_w_out)
    me = 4 * lax.axis_index("x") + 2 * lax.axis_index("y") + lax.axis_index("c")
    ada_n = w_ada.shape[2]
    in_n = w_in.shape[2]
    cw_n = conv_w.shape[2]

    first = _ag_direct(jnp.concatenate([c, conv_w[0].reshape(1, CONV_K * cw_n)], axis=1), "ag_c")[:, 0]
    c_all = first[:, :D_MODEL]
    conv_w_full = first[:, D_MODEL:].reshape(N_DEV, CONV_K, cw_n).transpose(1, 0, 2).reshape(CONV_K, XBC_W)
    b_piece = lax.dynamic_slice_in_dim(b_ada, me * ada_n, ada_n, axis=1)
    mod_all = _ag_direct(_mod_piece(c_all, w_ada[0], b_piece), "ag_mod")
    mod = lax.dynamic_index_in_dim(mod_all, me, axis=1, keepdims=False).reshape(1, 3 * D_MODEL)
    shift, scale, gate = mod[:, :D_MODEL], mod[:, D_MODEL:2 * D_MODEL], mod[:, 2 * D_MODEL:]

    w_t, zero = _ag_two_level(w_in[0].T.astype(BF), "ag_w_in")
    w_t = w_t.reshape(N_DEV * in_n, D_MODEL)

    def with_mine(blocks, mine):
        return lax.dynamic_update_index_in_dim(lax.empty(blocks, mine.dtype), mine, me, axis=0)

    rows = jnp.concatenate([w_attn_proj[0], w_ssm_proj[0], w_out[0]], axis=0).astype(BF) + zero
    r_ap, r_sp = w_attn_proj.shape[1], w_ssm_proj.shape[1]
    rows_started, zero = _exchange_start(rows, with_mine((N_DEV,) + rows.shape, rows), True, "ag_rows_start")

    def rows_fn(after):
        return _exchange_wait(rows_started, after, True, "ag_rows_wait")

    started = {}

    def send_blocks(key, g, name):
        started[key], zero = _exchange_start(
            g, with_mine(g.shape, lax.dynamic_index_in_dim(g, me, axis=0, keepdims=False)), False, name)
        return zero

    def after_mid(g_wap, g_wsp, g_wout):
        return send_blocks("rows", jnp.concatenate(
            [g_wap.reshape(N_DEV, r_ap, D_MODEL), g_wsp.reshape(N_DEV, r_sp, D_MODEL),
             g_wout.reshape(N_DEV, r_ap, D_MODEL)], axis=1), "rs_rows_start")

    def after_gw(g_ws):
        return send_blocks("in", jnp.concatenate(g_ws, axis=0).reshape(N_DEV, in_n, D_MODEL), "rs_in_start")

    r = _local_step(x[0], loss_target[0], shift, scale + zero, gate, w_t, rows_fn, norm_w, q_norm_w, k_norm_w,
                    rel_bias, sinks, conv_w_full, conv_b, dt_bias, a_log, d_skip, ssm_norm_w, after_mid, after_gw)

    small = dict(b_ada=r["dmod"], norm_w=r["g_norm_w"], q_norm_w=r["g_qnw"], k_norm_w=r["g_knw"], rel_bias=r["g_rel"],
                 sinks=r["g_sinks"], conv_b=r["g_conv_b"], dt_bias=r["g_dt_bias"], a_log=r["g_a_log"],
                 d_skip=r["g_d_skip"], ssm_norm_w=r["g_ssm_nw"])
    pack_all = _ag_direct(_pack_partials(small, r["loss"], r["g_conv_w"]), "ag_small")
    res, loss = _adam_small(pack_all, w, m, v)
    loss = loss[0, 0]
    cw_parts = pack_all[:, 0, _CW_OFF:].reshape(N_DEV, CONV_K, XBC_W)
    cw_mine = lax.dynamic_slice_in_dim(cw_parts, me * cw_n, cw_n, axis=2)
    res["conv_w"] = [a[None] for a in _adam(cw_mine, conv_w[0], m_conv_w[0], v_conv_w[0], "adam_conv_w")]

    dmod_piece = lax.dynamic_slice_in_dim(pack_all[:, 0, :3 * D_MODEL], me * ada_n, ada_n, axis=1)
    g_ada = _gw_ada(c_all, dmod_piece)
    res["w_ada"] = [a[None] for a in _adam(g_ada[None], w_ada[0], m_w_ada[0], v_w_ada[0], "adam_w_ada")]

    cat = lambda d: jnp.concatenate([d["w_attn_proj"][0], d["w_ssm_proj"][0], d["w_out"][0]], axis=0)
    rows_res = _adam(_exchange_wait(started["rows"], g_ada, False, "rs_rows_wait"), cat(w), cat(m), cat(v), "adam_w_rows")
    res["w_in"] = [a.T[None] for a in _adam(_exchange_wait(started["in"], rows_res[0], False, "rs_in_wait"),
                                            w_in[0].T, m_w_in[0].T, v_w_in[0].T, "adam_w_in")]
    res["w_attn_proj"] = [a[None, :r_ap] for a in rows_res]
    res["w_ssm_proj"] = [a[None, r_ap:r_ap + r_sp] for a in rows_res]
    res["w_out"] = [a[None, r_ap + r_sp:] for a in rows_res]

    outs = [loss, r["grad_x"][None]]
    for j in range(4):
        outs += [res[name][j] for name in WEIGHTS]
    return tuple(outs)
```

```python
import functools
import math

import numpy as np
import jax
import jax.numpy as jnp
from jax import lax
from jax.experimental import pallas as pl
from jax.experimental.pallas import tpu as pltpu

F32 = jnp.float32
BF = jnp.bfloat16
HI = lax.Precision.HIGHEST

D_MODEL = 1024
ATTN_HEADS = 16
KV_HEADS = 4
GRP = ATTN_HEADS // KV_HEADS
HEAD_DIM = 64
ATTN_W = ATTN_HEADS * HEAD_DIM
KV_W = KV_HEADS * HEAD_DIM
BLOCK = 128
REL_BUCKETS = 32
REL_MAX_DIST = 128
SSM_W = 2048
SSM_P = 64
SSM_HEADS = 32
SSM_G = 4
SSM_R = 8
SSM_N = 128
CONV_K = 4
XBC_W = SSM_W + 2 * SSM_G * SSM_N
SEG_W = (ATTN_W, KV_W, KV_W, ATTN_W, SSM_W, XBC_W, SSM_HEADS, D_MODEL, D_MODEL)
SEG_OFF = tuple(int(v) for v in np.cumsum((0,) + SEG_W))
IN_W = SEG_OFF[-1]
GATE_SEGS = (3, 4, 7, 8)
EPS = 1e-6
N_DEV = 8
ADAM_LR, ADAM_B1, ADAM_B2, ADAM_EPS, ADAM_WD, ADAM_STEP = 0.001, 0.9, 0.999, 1e-08, 0.01, 10
VMEM_LIMIT = 60 * 1024 * 1024
MESH = pl.DeviceIdType.MESH
ANY = pl.BlockSpec(memory_space=pl.ANY)


def _dot(a, b, precision=None):
    return jnp.dot(a, b, preferred_element_type=F32, precision=precision)


def _dot_nt(a, b, precision=None):
    return lax.dot_general(a, b, (((1,), (1,)), ((), ())), preferred_element_type=F32, precision=precision)


def _dot_tn(a, b, precision=None):
    return lax.dot_general(a, b, (((0,), (0,)), ((), ())), preferred_element_type=F32, precision=precision)


def _bf(a):
    return a.astype(BF)


def _sig(a):
    return 0.5 * jnp.tanh(0.5 * a) + 0.5


def _params(**kw):
    return pltpu.CompilerParams(vmem_limit_bytes=VMEM_LIMIT, **kw)


def _full(shape):
    nd = len(shape)
    return pl.BlockSpec(shape, lambda i: (0,) * nd)


def _rows(tm, w):
    return pl.BlockSpec((tm, w), lambda i: (i, 0))


def _inproj(x, norm_w, scale, shift, w_t, tm=256):
    s = x.shape[0]

    def body(x_ref, nw_ref, sc_ref, sh_ref, w_hbm, *rest):
        outs, h_ref, w_vm, sem = rest[:9], rest[9], rest[10], rest[11]
        first = pl.program_id(0) == 0
        cps = [pltpu.make_async_copy(w_hbm.at[SEG_OFF[j]:SEG_OFF[j + 1], :], w_vm.at[SEG_OFF[j]:SEG_OFF[j + 1], :], sem.at[j])
               for j in range(9)]

        def tile(waiting):
            xv = x_ref[...]
            r = lax.rsqrt(jnp.mean(xv * xv, axis=-1, keepdims=True) + EPS)
            h = xv * r * (nw_ref[...] * (1.0 + sc_ref[...])) + sh_ref[...]
            hb = _bf(h)
            h_ref[...] = hb
            for j in range(9):
                if waiting:
                    cps[j].wait()
                outs[j][...] = _dot_nt(hb, w_vm[SEG_OFF[j]:SEG_OFF[j + 1], :]).astype(outs[j].dtype)

        @pl.when(first)
        def _():
            for cp in cps:
                cp.start()
            tile(True)

        @pl.when(jnp.logical_not(first))
        def _():
            tile(False)

    vec = _full((1, D_MODEL))
    return pl.pallas_call(
        body, name="inproj", grid=(s // tm,),
        in_specs=[_rows(tm, D_MODEL), vec, vec, vec, ANY],
        out_specs=[_rows(tm, w) for w in SEG_W] + [_rows(tm, D_MODEL)],
        out_shape=[jax.ShapeDtypeStruct((s, w), BF if j in GATE_SEGS else F32) for j, w in enumerate(SEG_W)]
                  + [jax.ShapeDtypeStruct((s, D_MODEL), BF)],
        scratch_shapes=[pltpu.VMEM((IN_W, D_MODEL), BF), pltpu.SemaphoreType.DMA((9,))],
        compiler_params=_params(dimension_semantics=("arbitrary",)),
    )(x, norm_w, scale, shift, w_t)


def _bucket_onehot_t():
    qi = jnp.arange(BLOCK)[:, None]
    kj = jnp.arange(2 * BLOCK)[None, :]
    dist = qi + BLOCK - kj
    n = jnp.maximum(dist, 0)
    max_exact = REL_BUCKETS // 2
    nf = jnp.maximum(n, 1).astype(F32)
    large = max_exact + (jnp.log(nf / max_exact) / math.log(REL_MAX_DIST / max_exact)
                         * (REL_BUCKETS - max_exact)).astype(jnp.int32)
    large = jnp.minimum(large, REL_BUCKETS - 1)
    bucket = jnp.where(n < max_exact, n, large).reshape(1, BLOCK * 2 * BLOCK)
    return (bucket == jnp.arange(REL_BUCKETS)[:, None]).astype(F32)


def _bias_dense(rel_bias_t, oh_t):
    def body(rb_ref, oh_ref, o_ref):
        o_ref[...] = _dot(rb_ref[...], oh_ref[...], HI)

    return pl.pallas_call(
        body, name="bias_dense", out_shape=jax.ShapeDtypeStruct((ATTN_HEADS, BLOCK * 2 * BLOCK), F32),
        compiler_params=_params(),
    )(rel_bias_t, oh_t)


def _bias_grad(ds_sum, oh_t):
    def body(ds_ref, oh_ref, o_ref):
        o_ref[...] = _dot_nt(ds_ref[...], oh_ref[...], HI)

    return pl.pallas_call(
        body, name="bias_grad", out_shape=jax.ShapeDtypeStruct((ATTN_HEADS, REL_BUCKETS), F32),
        compiler_params=_params(),
    )(ds_sum, oh_t)


def _group_sum(a, e):
    hi = _bf(a)
    return _dot(hi, e) + _dot(_bf(a - hi.astype(F32)), e)


def _group_bcast(a, e3t):
    hi = _bf(a)
    r1 = a - hi.astype(F32)
    mid = _bf(r1)
    return _dot(jnp.concatenate([hi, mid, _bf(r1 - mid.astype(F32))], axis=1), e3t)


def _membership(width, group, ngroups):
    e = (jnp.arange(width)[:, None] // group == jnp.arange(ngroups)[None, :]).astype(BF)
    return e, jnp.tile(e.T, (3, 1))


def _fold(width, group):
    return (jnp.arange(width)[:, None] % group == jnp.arange(group)[None, :]).astype(BF)


def _heads_norm(t, w_x, e, e3t):
    r = lax.rsqrt(_group_sum(t * t, e) * (1.0 / HEAD_DIM) + EPS)
    r_x = _group_bcast(r, e3t)
    return t * r_x * w_x, r_x


def _heads_norm_bwd(t, r_x, w_x, d, e, e3t):
    wd = d * w_x
    corr = _group_bcast(_group_sum(t * wd, e) * (1.0 / HEAD_DIM), e3t)
    return r_x * wd - t * (r_x * r_x * r_x) * corr, jnp.sum(d * t * r_x, axis=0, keepdims=True)


def _stack_heads(a, hk):
    return jnp.concatenate([a[:, (hk * GRP + g) * HEAD_DIM:(hk * GRP + g + 1) * HEAD_DIM] for g in range(GRP)], axis=0)


def _stack_cols(a, hk):
    return jnp.concatenate([a[:, hk * GRP + g:hk * GRP + g + 1] for g in range(GRP)], axis=0)


def _masked_bias(bias):
    qi = jnp.arange(BLOCK)[:, None]
    kj = jnp.arange(2 * BLOCK)[None, :]
    cur_ok = jnp.logical_and(kj >= BLOCK, kj - BLOCK <= qi)
    both_ok = jnp.logical_or(jnp.logical_and(kj < BLOCK, kj > qi), cur_ok)
    return jnp.stack([jnp.where(cur_ok, bias, -1e30), jnp.where(both_ok, bias, -1e30)])


def _attn_consts(qnw, knw):
    eq, eq3t = _membership(ATTN_W, HEAD_DIM, ATTN_HEADS)
    ek, ek3t = _membership(KV_W, HEAD_DIM, ATTN_HEADS)
    return (jnp.tile(qnw, (1, ATTN_HEADS)), jnp.tile(knw, (1, KV_HEADS)), eq, eq3t, ek, ek3t)


def _attn_fwd(q, k, v, bias, sinks, consts):
    s = q.shape[0]
    nb = s // BLOCK
    gq = GRP * BLOCK
    bias_t = bias.reshape(2, KV_HEADS, GRP, BLOCK, 2 * BLOCK).transpose(0, 1, 4, 2, 3).reshape(2, KV_HEADS, 2 * BLOCK, gq)
    sink_rows = jnp.repeat(sinks.reshape(KV_HEADS, GRP), BLOCK, axis=1).reshape(KV_HEADS, 1, gq)
    eye = jnp.eye(BLOCK, dtype=BF)

    def body(q_ref, kp_ref, kc_ref, vp_ref, vc_ref, b_ref, bt_ref, sk_ref, skr_ref, eye_ref,
             qw_ref, kw_ref, eq_ref, eq3_ref, ek_ref, ek3_ref, o_ref, lse_ref):
        qn = _bf(_heads_norm(q_ref[...], qw_ref[...], eq_ref[...], eq3_ref[...])[0] * (HEAD_DIM ** -0.5))
        kn = _bf(_heads_norm(jnp.concatenate([kp_ref[...], kc_ref[...]], axis=0), kw_ref[...], ek_ref[...], ek3_ref[...])[0])
        vv = _bf(jnp.concatenate([vp_ref[...], vc_ref[...]], axis=0))
        ones = jnp.ones((2 * BLOCK, HEAD_DIM), BF)
        lses = []
        kss = [slice(hk * HEAD_DIM, (hk + 1) * HEAD_DIM) for hk in range(KV_HEADS)]
        qgs = [_stack_heads(qn, hk) for hk in range(KV_HEADS)]
        sc_ts = [_dot_nt(kn[:, kss[hk]], qgs[hk]) + bt_ref[0, hk] for hk in range(KV_HEADS)]
        m8s = [_bf(jnp.broadcast_to(jnp.maximum(jnp.max(sc_ts[hk], axis=0, keepdims=True), skr_ref[hk]), (8, gq)))
               for hk in range(KV_HEADS)]
        ms = [jnp.concatenate([_dot_nt(eye_ref[...], m8[:, g * BLOCK:(g + 1) * BLOCK])[:, 0:1] for g in range(GRP)], axis=0)
              for m8 in m8s]
        scs = [_dot_nt(qgs[hk], kn[:, kss[hk]]) + b_ref[0, hk * GRP:(hk + 1) * GRP].reshape(gq, 2 * BLOCK)
               for hk in range(KV_HEADS)]
        ps = [_bf(jnp.exp(scs[hk] - ms[hk])) for hk in range(KV_HEADS)]
        pvs = [_dot(ps[hk], jnp.concatenate([vv[:, kss[hk]], ones], axis=1)) for hk in range(KV_HEADS)]
        for hk in range(KV_HEADS):
            m, pv = ms[hk], pvs[hk]
            sink = jnp.concatenate([jnp.full((BLOCK, 1), sk_ref[0, hk * GRP + g], F32) for g in range(GRP)], axis=0)
            den = pv[:, HEAD_DIM:HEAD_DIM + 1] + jnp.exp(sink - m)
            out = pv[:, :HEAD_DIM] * (1.0 / den)
            lse = m + jnp.log(den)
            for g in range(GRP):
                h = hk * GRP + g
                o_ref[:, h * HEAD_DIM:(h + 1) * HEAD_DIM] = out[g * BLOCK:(g + 1) * BLOCK]
                lses.append(lse[g * BLOCK:(g + 1) * BLOCK])
        lse_ref[...] = jnp.concatenate(lses, axis=1)

    cur = lambda w: pl.BlockSpec((BLOCK, w), lambda i: (i, 0))
    prev = lambda w: pl.BlockSpec((BLOCK, w), lambda i: (jnp.maximum(i - 1, 0), 0))
    whole = lambda a: pl.BlockSpec(a.shape, lambda i: (0,) * a.ndim)
    first_or_not = lambda a: pl.BlockSpec((1,) + a.shape[1:], lambda i: (jnp.minimum(i, 1),) + (0,) * (a.ndim - 1))
    return pl.pallas_call(
        body, name="attn_fwd", grid=(nb,),
        in_specs=[cur(ATTN_W), prev(KV_W), cur(KV_W), prev(KV_W), cur(KV_W), first_or_not(bias), first_or_not(bias_t),
                  pl.BlockSpec(memory_space=pltpu.SMEM), whole(sink_rows), whole(eye)] + [_full(c.shape) for c in consts],
        out_specs=[cur(ATTN_W), cur(ATTN_HEADS)],
        out_shape=[jax.ShapeDtypeStruct((s, ATTN_W), F32), jax.ShapeDtypeStruct((s, ATTN_HEADS), F32)],
        compiler_params=_params(dimension_semantics=("arbitrary",)),
    )(q, k, k, v, v, bias, bias_t, sinks, sink_rows, eye, *consts)


def _conv_taps(xbc, tail):
    ext = jnp.concatenate([tail, xbc], axis=0)
    return [pltpu.roll(ext, CONV_K - 1 - j, axis=0)[8:8 + BLOCK] if j < CONV_K - 1 else xbc for j in range(CONV_K)]


def _softplus(u):
    return jnp.maximum(u, 0.0) + jnp.log(1.0 + jnp.exp(-jnp.abs(u)))


def _tril():
    r = lax.broadcasted_iota(jnp.int32, (BLOCK, BLOCK), 0)
    c = lax.broadcasted_iota(jnp.int32, (BLOCK, BLOCK), 1)
    return r >= c


def _triu():
    r = lax.broadcasted_iota(jnp.int32, (BLOCK, BLOCK), 0)
    c = lax.broadcasted_iota(jnp.int32, (BLOCK, BLOCK), 1)
    return r <= c


def _exact_left(m01, a):
    hi = _bf(a)
    r1 = a - hi.astype(F32)
    mid = _bf(r1)
    return _dot(m01, hi) + _dot(m01, mid) + _dot(m01, _bf(r1 - mid.astype(F32)))


def _ssd_common(conv, dtr, dtb_ref, alog_ref, e3_ref):
    sg = _sig(conv)
    xact = conv * sg
    u = dtr + dtb_ref[...]
    dt = _softplus(u)
    a = -jnp.exp(alog_ref[...])
    trilb = _tril()
    acum = _exact_left(trilb.astype(BF), dt * a)
    both = _group_bcast(jnp.concatenate([dt, acum], axis=0), e3_ref[...])
    dt_x, acum_x = both[:BLOCK], both[BLOCK:]
    return sg, xact, u, dt, a, trilb, acum, dt_x, acum_x


SSD_CH = 2


def _ssd_fwd(xbc, dt_raw, conv_w, conv_b, dt_bias, a_log, dsk_x, e3t):
    s = xbc.shape[0]
    nc = s // BLOCK
    ch = SSD_CH if nc % SSD_CH == 0 else 1
    rows = ch * BLOCK

    def body(x_ref, tail_ref, dtr_ref, cw_ref, cb_ref, dtb_ref, alog_ref, dsk_ref, e3_ref,
             y_ref, hp_ref, conv_ref, hst, yd_s, yoff_s):
        i = pl.program_id(0)

        @pl.when(i == 0)
        def _():
            hst[...] = jnp.zeros_like(hst)

        for j in range(ch):
            rs = slice(j * BLOCK, (j + 1) * BLOCK)
            tail = jnp.where(i > 0, tail_ref[...], 0.0) if j == 0 else x_ref[j * BLOCK - 8:j * BLOCK, :]
            taps = _conv_taps(x_ref[rs, :], tail)
            conv = cb_ref[...] + sum(taps[t] * cw_ref[t:t + 1, :] for t in range(CONV_K))
            conv_ref[rs, :] = conv
            _, xact, _, _, _, trilb, acum, dt_x, acum_x = _ssd_common(conv, dtr_ref[rs, :], dtb_ref, alog_ref, e3_ref)
            xs = xact[:, :SSM_W]
            acum_t = acum.T
            ea_x = jnp.exp(acum_x)
            last_x = acum_x[BLOCK - 1:BLOCK, :]
            xdt = xs * dt_x
            xw = xdt * jnp.exp(last_x - acum_x)
            cd_x = jnp.exp(last_x)
            hprev = hst[...]
            hp_ref[j] = hprev
            sls = [slice(g * SSM_R * SSM_P, (g + 1) * SSM_R * SSM_P) for g in range(SSM_G)]
            bgs = [_bf(xact[:, SSM_W + g * SSM_N:SSM_W + (g + 1) * SSM_N]) for g in range(SSM_G)]
            cgs = [_bf(xact[:, SSM_W + SSM_G * SSM_N + g * SSM_N:SSM_W + SSM_G * SSM_N + (g + 1) * SSM_N])
                   for g in range(SSM_G)]
            xdt_b, xw_b, hprev_b = _bf(xdt), _bf(xw), _bf(hprev)
            low_half = lax.broadcasted_iota(jnp.int32, (BLOCK, 2 * SSM_P), 1) < SSM_P
            cbs = [_dot_nt(cgs[g], bgs[g]) for g in range(SSM_G)]
            for g in range(SSM_G):
                sl = sls[g]
                yoff_s[:, sl] = _dot(cgs[g], hprev_b[:, sl]) * ea_x[:, sl]
                hst[:, sl] = hprev[:, sl] * cd_x[:, sl] + _dot_tn(bgs[g], xw_b[:, sl])
            for g in range(SSM_G):
                hss = [slice((g * SSM_R + r) * SSM_P, (g * SSM_R + r + 1) * SSM_P) for r in range(SSM_R)]
                mms = [_bf(cbs[g] * jnp.exp(jnp.where(trilb, acum[:, g * SSM_R + r:g * SSM_R + r + 1]
                                                      - acum_t[g * SSM_R + r:g * SSM_R + r + 1, :], -1e30)))
                       for r in range(SSM_R)]
                for r in range(0, SSM_R, 2):
                    pair = slice(hss[r].start, hss[r + 1].stop)
                    xp = xdt_b[:, pair]
                    rhs = jnp.concatenate([jnp.where(low_half, xp, 0), jnp.where(low_half, 0, xp)], axis=0)
                    yd_s[:, pair] = _dot(jnp.concatenate([mms[r], mms[r + 1]], axis=1), rhs)
            y_ref[rs, :] = yd_s[...] + yoff_s[...] + dsk_ref[...] * xs

    blk = lambda w: pl.BlockSpec((rows, w), lambda i: (i, 0))
    return pl.pallas_call(
        body, name="ssd_fwd", grid=(nc // ch,),
        in_specs=[blk(XBC_W), pl.BlockSpec((8, XBC_W), lambda i: (jnp.maximum(i * (rows // 8) - 1, 0), 0)),
                  blk(SSM_HEADS), _full((CONV_K, XBC_W)), _full((1, XBC_W)), _full((1, SSM_HEADS)),
                  _full((1, SSM_HEADS)), _full((1, SSM_W)), _full((3 * SSM_HEADS, SSM_W))],
        out_specs=[blk(SSM_W), pl.BlockSpec((ch, SSM_N, SSM_W), lambda i: (i, 0, 0)), blk(XBC_W)],
        out_shape=[jax.ShapeDtypeStruct((s, SSM_W), F32), jax.ShapeDtypeStruct((nc, SSM_N, SSM_W), F32),
                   jax.ShapeDtypeStruct((s, XBC_W), F32)],
        scratch_shapes=[pltpu.VMEM((SSM_N, SSM_W), F32), pltpu.VMEM((BLOCK, SSM_W), F32), pltpu.VMEM((BLOCK, SSM_W), F32)],
        compiler_params=_params(dimension_semantics=("arbitrary",)),
    )(xbc, xbc, dt_raw, conv_w, conv_b, dt_bias, a_log, dsk_x, e3t)


def _dsilu(z, sg):
    return sg * (1.0 + z * (1.0 - sg))


def _mid(x, tgt, o_att, za, ypre, zm, ga, gb, gate, ssm_nw, rows_all, tm=256):
    s = x.shape[0]
    gw = SSM_W // SSM_G

    r_ap, r_sp = ATTN_W // N_DEV, SSM_W // N_DEV

    def body(x_ref, t_ref, o_ref, za_ref, yp_ref, zm_ref, ga_ref, gb_ref, gate_ref, nw_ref, rows_h,
             dout_ref, do_ref, dza_ref, dyp_ref, dzm_ref, dga_ref, dgb_ref,
             yag_ref, dya_ref, yn_ref, dyb_ref, mg_ref, dob_ref, gnw_ref, dgate_ref, loss_ref,
             wap_v, wsp_v, wout_v, sem):
        i = pl.program_id(0)

        @pl.when(i == 0)
        def _():
            cps = []
            for d in range(N_DEV):
                for j, (dst, r0, rn) in enumerate(((wap_v, 0, r_ap), (wsp_v, r_ap, r_sp), (wout_v, r_ap + r_sp, r_ap))):
                    cps.append(pltpu.make_async_copy(rows_h.at[d, r0:r0 + rn, :], dst.at[d * rn:(d + 1) * rn, :], sem.at[j]))
            for cp in cps:
                cp.start()
            gnw_ref[...] = jnp.zeros_like(gnw_ref)
            dgate_ref[...] = jnp.zeros_like(dgate_ref)
            loss_ref[...] = jnp.zeros_like(loss_ref)
            for cp in cps:
                cp.wait()

        gate = gate_ref[...]
        nw = nw_ref[...]
        o_att = o_ref[...]
        z_a = za_ref[...].astype(F32)
        s_a = _sig(z_a)
        silu_a = z_a * s_a
        yag = _bf(o_att * silu_a)
        yag_ref[...] = yag
        ypre = yp_ref[...]
        z_m = zm_ref[...].astype(F32)
        s_m = _sig(z_m)
        silu_m = z_m * s_m
        yg = ypre * silu_m
        rinv = jnp.concatenate(
            [jnp.broadcast_to(lax.rsqrt(jnp.mean(yg[:, g * gw:(g + 1) * gw] ** 2, axis=-1, keepdims=True) + EPS), (tm, gw))
             for g in range(SSM_G)], axis=1)
        ynr = yg * rinv
        yn = _bf(ynr * nw)
        yn_ref[...] = yn
        y_a = _dot(yag, wap_v[...])
        y_b = _dot(yn, wsp_v[...])
        g_a = _sig(ga_ref[...].astype(F32))
        g_b = _sig(gb_ref[...].astype(F32))
        merged = _bf(g_a * y_a + g_b * y_b)
        mg_ref[...] = merged
        o = _dot(merged, wout_v[...])
        diff = x_ref[...] + gate * o - t_ref[...]
        loss_ref[...] += (0.5 / D_MODEL) * jnp.sum(diff * diff, axis=(0, 1), keepdims=True)
        dout = diff * (1.0 / D_MODEL)
        dout_ref[...] = dout
        dgate_ref[...] += jnp.sum(dout * o, axis=0, keepdims=True)
        d_o = _bf(dout * gate)
        dob_ref[...] = d_o
        dmerged = _dot_nt(d_o, wout_v[...])
        dy_af = dmerged * g_a
        dy_bf = dmerged * g_b
        dy_a = _bf(dy_af)
        dy_b = _bf(dy_bf)
        dya_ref[...] = dy_a
        dyb_ref[...] = dy_b
        dyag = _dot_nt(dy_a, wap_v[...])
        dyn = _dot_nt(dy_b, wsp_v[...])
        dga_ref[...] = _bf(dy_af * y_a * (1.0 - g_a))
        dgb_ref[...] = _bf(dy_bf * y_b * (1.0 - g_b))
        do_ref[...] = dyag * silu_a
        dza_ref[...] = _bf(dyag * o_att * _dsilu(z_a, s_a))
        gnw_ref[...] += jnp.sum(dyn * ynr, axis=0, keepdims=True)
        dynw = dyn * nw
        corr = jnp.concatenate(
            [jnp.broadcast_to(jnp.mean((dynw * ynr)[:, g * gw:(g + 1) * gw], axis=-1, keepdims=True), (tm, gw))
             for g in range(SSM_G)], axis=1)
        dyg = rinv * (dynw - ynr * corr)
        dyp_ref[...] = dyg * silu_m
        dzm_ref[...] = _bf(dyg * ypre * _dsilu(z_m, s_m))

    r1, r2 = _rows(tm, D_MODEL), _rows(tm, SSM_W)
    sd = jax.ShapeDtypeStruct
    return pl.pallas_call(
        body, name="mid", grid=(s // tm,),
        in_specs=[r1, r1, r1, r1, r2, r2, r1, r1, _full((1, D_MODEL)), _full((1, SSM_W)), ANY],
        out_specs=[r1, r1, r1, r2, r2, r1, r1, r1, r1, r2, r1, r1, r1,
                   _full((1, SSM_W)), _full((1, D_MODEL)), _full((1, 1))],
        out_shape=[sd((s, D_MODEL), F32), sd((s, ATTN_W), F32), sd((s, ATTN_W), BF), sd((s, SSM_W), F32),
                   sd((s, SSM_W), BF), sd((s, D_MODEL), BF), sd((s, D_MODEL), BF),
                   sd((s, ATTN_W), BF), sd((s, D_MODEL), BF), sd((s, SSM_W), BF), sd((s, D_MODEL), BF),
                   sd((s, D_MODEL), BF), sd((s, D_MODEL), BF),
                   sd((1, SSM_W), F32), sd((1, D_MODEL), F32), sd((1, 1), F32)],
        scratch_shapes=[pltpu.VMEM((ATTN_W, D_MODEL), BF), pltpu.VMEM((SSM_W, D_MODEL), BF), pltpu.VMEM((D_MODEL, D_MODEL), BF),
                        pltpu.SemaphoreType.DMA((3,))],
        compiler_params=_params(dimension_semantics=("arbitrary",)),
    )(x, tgt, o_att, za, ypre, zm, ga, gb, gate, ssm_nw, rows_all)


def _attn_bwd(q, k, v, bias, sinks, consts, o_att, lse, d_o):
    s = q.shape[0]
    nb = s // BLOCK
    folds = (_fold(ATTN_W, HEAD_DIM), _fold(KV_W, HEAD_DIM))

    def body(q_ref, kp_ref, kc_ref, vp_ref, vc_ref, b_ref, skv_ref, qw_ref, kw_ref, eq_ref, eq3_ref, ek_ref, ek3_ref,
             fq_ref, fk_ref, o_ref, lse_ref, do_ref,
             dq_ref, dk_ref, dv_ref, dss_ref, gqw_ref, gkw_ref, gsk_ref, ckn, cv, dqn_s, dkn_s, dv_s, gq_x, gk_x):
        i = pl.program_id(0)
        kw, ek, ek3 = kw_ref[...], ek_ref[...], ek3_ref[...]

        @pl.when(i == 0)
        def _():
            for ref in (ckn, cv, dss_ref, gq_x, gk_x, gsk_ref):
                ref[...] = jnp.zeros_like(ref)

        @pl.when(i < nb)
        def _():
            qw, eq, eq3 = qw_ref[...], eq_ref[...], eq3_ref[...]
            qf = q_ref[...]
            qnf, rq_x = _heads_norm(qf, qw, eq, eq3)
            qn = _bf(qnf * (HEAD_DIM ** -0.5))
            kf = jnp.concatenate([kp_ref[...], kc_ref[...]], axis=0)
            knf, rk_x = _heads_norm(kf, kw, ek, ek3)
            kn = _bf(knf)
            vv = _bf(jnp.concatenate([vp_ref[...], vc_ref[...]], axis=0))
            d_of = do_ref[...]
            d_ob = _bf(d_of)
            lse_all = lse_ref[...]
            delta = _group_sum(d_of * o_ref[...], eq)
            gsk_ref[...] += jnp.sum(-jnp.exp(skv_ref[...] - lse_all) * delta, axis=0, keepdims=True)
            kss = [slice(hk * HEAD_DIM, (hk + 1) * HEAD_DIM) for hk in range(KV_HEADS)]
            qgs = [_stack_heads(qn, hk) for hk in range(KV_HEADS)]
            d_ogs = [_stack_heads(d_ob, hk) for hk in range(KV_HEADS)]
            scs = [_dot_nt(qgs[hk], kn[:, kss[hk]]) + b_ref[0, hk * GRP:(hk + 1) * GRP].reshape(GRP * BLOCK, 2 * BLOCK)
                   for hk in range(KV_HEADS)]
            dps = [_dot_nt(d_ogs[hk], vv[:, kss[hk]]) for hk in range(KV_HEADS)]
            ps = [jnp.exp(scs[hk] - _stack_cols(lse_all, hk)) for hk in range(KV_HEADS)]
            dss = [ps[hk] * (dps[hk] - _stack_cols(delta, hk)) for hk in range(KV_HEADS)]
            pbs = [_bf(p) for p in ps]
            dsbs = [_bf(ds) for ds in dss]
            for hk in range(KV_HEADS):
                dss_ref[hk * GRP:(hk + 1) * GRP] += dss[hk].reshape(GRP, BLOCK, 2 * BLOCK)
            for hk in range(KV_HEADS):
                dv_s[:, kss[hk]] = _dot_tn(pbs[hk], d_ogs[hk])
                dkn_s[:, kss[hk]] = _dot_tn(dsbs[hk], qgs[hk])
            dqns = [_dot(dsbs[hk], kn[:, kss[hk]]) * (HEAD_DIM ** -0.5) for hk in range(KV_HEADS)]
            for hk in range(KV_HEADS):
                for g in range(GRP):
                    h = hk * GRP + g
                    dqn_s[:, h * HEAD_DIM:(h + 1) * HEAD_DIM] = dqns[hk][g * BLOCK:(g + 1) * BLOCK]
            dq, gq = _heads_norm_bwd(qf, rq_x, qw, dqn_s[...], eq, eq3)
            dq_ref[...] = _bf(dq)
            gq_x[...] += gq
            dk, gk = _heads_norm_bwd(kf[:BLOCK], rk_x[:BLOCK], kw, ckn[...] + dkn_s[0:BLOCK, :], ek, ek3)
            dk_ref[...] = _bf(dk)
            gk_x[...] += gk
            dv_ref[...] = _bf(cv[...] + dv_s[0:BLOCK, :])
            ckn[...] = dkn_s[BLOCK:2 * BLOCK, :]
            cv[...] = dv_s[BLOCK:2 * BLOCK, :]

        @pl.when(i == nb)
        def _():
            kc = kc_ref[...]
            dk, gk = _heads_norm_bwd(kc, _heads_norm(kc, kw, ek, ek3)[1], kw, ckn[...], ek, ek3)
            dk_ref[...] = _bf(dk)
            dv_ref[...] = _bf(cv[...])
            gqw_ref[...] = _group_sum(jnp.broadcast_to(gq_x[...], (8, ATTN_W)), fq_ref[...])[0:1]
            gkw_ref[...] = _group_sum(jnp.broadcast_to(gk_x[...] + gk, (8, KV_W)), fk_ref[...])[0:1]

    last = nb - 1
    cur = lambda w: pl.BlockSpec((BLOCK, w), lambda i: (jnp.minimum(i, last), 0))
    prev = lambda w: pl.BlockSpec((BLOCK, w), lambda i: (jnp.maximum(jnp.minimum(i, last) - 1, 0), 0))
    late = lambda w: pl.BlockSpec((BLOCK, w), lambda i: (jnp.maximum(i - 1, 0), 0))
    sd = jax.ShapeDtypeStruct
    return pl.pallas_call(
        body, name="attn_bwd", grid=(nb + 1,),
        in_specs=[cur(ATTN_W), prev(KV_W), cur(KV_W), prev(KV_W), cur(KV_W),
                  pl.BlockSpec((1, ATTN_HEADS, BLOCK, 2 * BLOCK), lambda i: (jnp.minimum(i, 1), 0, 0, 0)),
                  _full((1, ATTN_HEADS))]
                 + [_full(c.shape) for c in consts + folds] + [cur(ATTN_W), cur(ATTN_HEADS), cur(ATTN_W)],
        out_specs=[cur(ATTN_W), late(KV_W), late(KV_W),
                   pl.BlockSpec((ATTN_HEADS, BLOCK, 2 * BLOCK), lambda i: (0, 0, 0)),
                   _full((1, HEAD_DIM)), _full((1, HEAD_DIM)), _full((1, ATTN_HEADS))],
        out_shape=[sd((s, ATTN_W), BF), sd((s, KV_W), BF), sd((s, KV_W), BF),
                   sd((ATTN_HEADS, BLOCK, 2 * BLOCK), F32), sd((1, HEAD_DIM), F32), sd((1, HEAD_DIM), F32),
                   sd((1, ATTN_HEADS), F32)],
        scratch_shapes=[pltpu.VMEM((BLOCK, KV_W), F32), pltpu.VMEM((BLOCK, KV_W), F32),
                        pltpu.VMEM((BLOCK, ATTN_W), F32), pltpu.VMEM((2 * BLOCK, KV_W), F32),
                        pltpu.VMEM((2 * BLOCK, KV_W), F32), pltpu.VMEM((1, ATTN_W), F32), pltpu.VMEM((1, KV_W), F32)],
        compiler_params=_params(dimension_semantics=("arbitrary",)),
    )(q, k, k, v, v, bias, sinks, *consts, *folds, o_att, lse, d_o)


def _ssd_bwd(xbc, conv_all, dt_raw, conv_w, dt_bias, a_log, dsk_x, e_mat, e3t, hprev_all, dy_all):
    s = xbc.shape[0]
    nc = s // BLOCK
    ch = 1
    rows = ch * BLOCK
    nsteps = nc // ch
    gw = SSM_R * SSM_P
    b0, c0 = SSM_W, SSM_W + SSM_G * SSM_N

    def body(x_ref, conv_ref, dtr_ref, cw_ref, dtb_ref, alog_ref, dsk_ref, e_ref, e3_ref, hp_ref, dy_ref,
             dx_ref, ddt_ref, gcw_ref, gcb_ref, gdtb_ref, galog_ref, gdsk_ref,
             dh, nhead, gdskx, dxdt_s, dbc_s, dxd_s):
        def chunk_bwd(j):
            rs = slice(j * BLOCK, (j + 1) * BLOCK)
            conv = conv_ref[rs, :]
            sg, xact, u, dt, a, trilb, acum, dt_x, acum_x = _ssd_common(conv, dtr_ref[rs, :], dtb_ref, alog_ref, e3_ref)
            xs = xact[:, :SSM_W]
            acum_t = acum.T
            ea_x = jnp.exp(acum_x)
            last_x = acum_x[BLOCK - 1:BLOCK, :]
            dte_x = jnp.exp(last_x - acum_x)
            cd_x = jnp.exp(last_x)
            xdt = xs * dt_x
            xw = xdt * dte_x
            hprev = hp_ref[j]
            dhn = dh[...]
            dy = dy_ref[rs, :]
            gdskx[...] += jnp.sum(dy * xs, axis=0, keepdims=True)
            dyea = dy * ea_x
            lane = lax.broadcasted_iota(jnp.int32, (BLOCK, SSM_HEADS), 1)
            dacum = jnp.zeros((BLOCK, SSM_HEADS), F32)
            dacc_x, dlast_x = [], []
            sls = [slice(g * gw, (g + 1) * gw) for g in range(SSM_G)]
            bgs = [_bf(xact[:, b0 + g * SSM_N:b0 + (g + 1) * SSM_N]) for g in range(SSM_G)]
            cgs = [_bf(xact[:, c0 + g * SSM_N:c0 + (g + 1) * SSM_N]) for g in range(SSM_G)]
            hpgs = [_bf(hprev[:, sl]) for sl in sls]
            dhgs = [_bf(dhn[:, sl]) for sl in sls]
            dyeags = [_bf(dyea[:, sl]) for sl in sls]
            xwgs = [_bf(xw[:, sl]) for sl in sls]
            xdt_b, dy_b = _bf(xdt), _bf(dy)
            low_half = lax.broadcasted_iota(jnp.int32, (BLOCK, 2 * SSM_P), 1) < SSM_P
            cbs = [_dot_nt(cgs[g], bgs[g]) for g in range(SSM_G)]
            gmats = [_dot(cgs[g], hpgs[g]) for g in range(SSM_G)]
            dxws = [_dot(bgs[g], dhgs[g]) for g in range(SSM_G)]
            dcgs = [_dot_nt(dyeags[g], hpgs[g]) for g in range(SSM_G)]
            dbgs = [_dot_nt(xwgs[g], dhgs[g]) for g in range(SSM_G)]
            for g in range(SSM_G):
                sl = sls[g]
                dh[:, sl] = dhn[:, sl] * cd_x[:, sl] + _dot_tn(cgs[g], dyeags[g])
                dxdt_s[:, sl] = dxws[g] * dte_x[:, sl]
                dacc_x.append(dy[:, sl] * gmats[g] * ea_x[:, sl] - dxws[g] * xw[:, sl])
                dlast_x.append(jnp.sum(dxws[g] * xw[:, sl], axis=0, keepdims=True)
                               + jnp.sum(dhn[:, sl] * hprev[:, sl], axis=0, keepdims=True) * cd_x[:, sl])
            for g in range(SSM_G):
                bg, cg, cb, dbg, dcg = bgs[g], cgs[g], cbs[g], dbgs[g], dcgs[g]
                hss = [slice((g * SSM_R + r) * SSM_P, (g * SSM_R + r + 1) * SSM_P) for r in range(SSM_R)]
                lms = [jnp.exp(jnp.where(trilb, acum[:, g * SSM_R + r:g * SSM_R + r + 1]
                                         - acum_t[g * SSM_R + r:g * SSM_R + r + 1, :], -1e30)) for r in range(SSM_R)]
                mms = [cb * lm for lm in lms]
                mmbs = [_bf(mm) for mm in mms]
                dms = []
                for r in range(0, SSM_R, 2):
                    pair = slice(hss[r].start, hss[r + 1].stop)
                    xp, dyp = xdt_b[:, pair], dy_b[:, pair]
                    dmp = _dot_nt(dyp, jnp.concatenate([jnp.where(low_half, xp, 0), jnp.where(low_half, 0, xp)], axis=0))
                    dms += [dmp[:, :BLOCK], dmp[:, BLOCK:]]
                    dxd_s[:, pair] = _dot_tn(jnp.concatenate([mmbs[r], mmbs[r + 1]], axis=0),
                                             jnp.concatenate([jnp.where(low_half, dyp, 0), jnp.where(low_half, 0, dyp)], axis=0))
                dcb = sum(dms[r] * lms[r] for r in range(SSM_R))
                wms = [dms[r] * mms[r] for r in range(SSM_R)]
                antis = [wm - wm.T for wm in wms]
                for r in range(SSM_R):
                    dacum = dacum + _group_sum(antis[r], (lane == g * SSM_R + r).astype(BF))
                dcbb = _bf(dcb)
                dbc_s[:, g * SSM_N:(g + 1) * SSM_N] = dbg + _dot_tn(dcbb, cg)
                dbc_s[:, SSM_G * SSM_N + g * SSM_N:SSM_G * SSM_N + (g + 1) * SSM_N] = dcg + _dot(dcbb, bg)
            dxdt = dxdt_s[...] + dxd_s[...]
            dxs = dy * dsk_ref[...] + dxdt * dt_x
            red = _group_sum(jnp.concatenate(
                [dxdt * xs, jnp.concatenate(dacc_x, axis=1),
                 jnp.broadcast_to(jnp.concatenate(dlast_x, axis=1), (8, SSM_W))], axis=0), e_ref[...])
            row = lax.broadcasted_iota(jnp.int32, (BLOCK, SSM_HEADS), 0)
            dacum = dacum + red[BLOCK:2 * BLOCK] + jnp.where(row == BLOCK - 1, red[2 * BLOCK:2 * BLOCK + 1], 0.0)
            ddta = _exact_left(_triu().astype(BF), dacum)
            ddt = red[:BLOCK] + ddta * a
            galog_ref[...] += jnp.sum(ddta * dt, axis=0, keepdims=True) * a
            du = ddt * _sig(u)
            ddt_ref[rs, :] = _bf(du)
            gdtb_ref[...] += jnp.sum(du, axis=0, keepdims=True)
            dconv = jnp.concatenate([dxs, dbc_s[...]], axis=1) * _dsilu(conv, sg)
            gcb_ref[...] += jnp.sum(dconv, axis=0, keepdims=True)
            ext2 = jnp.concatenate([dconv, nhead[...]], axis=0)
            ahead = [pltpu.roll(ext2, BLOCK + 8 - (CONV_K - 1 - j), axis=0)[0:BLOCK] if j < CONV_K - 1 else dconv
                     for j in range(CONV_K)]
            dx_ref[rs, :] = _bf(sum(ahead[j] * cw_ref[j:j + 1, :] for j in range(CONV_K)))
            xraw = x_ref[rs, :]
            gcw_ref[...] += jnp.concatenate([jnp.sum(ahead[j] * xraw, axis=0, keepdims=True) for j in range(CONV_K)], axis=0)
            nhead[...] = dconv[0:8]

        i = pl.program_id(0)

        @pl.when(i == 0)
        def _():
            for ref in (dh, nhead, gdskx, gcw_ref, gcb_ref, gdtb_ref, galog_ref, gdsk_ref):
                ref[...] = jnp.zeros_like(ref)

        for j in reversed(range(ch)):
            chunk_bwd(j)

        @pl.when(i == nsteps - 1)
        def _():
            gdsk_ref[...] = _group_sum(jnp.broadcast_to(gdskx[...], (8, SSM_W)), e_ref[...])[0:1]

    chunk = lambda w: pl.BlockSpec((rows, w), lambda i: (nsteps - 1 - i, 0))
    sd = jax.ShapeDtypeStruct
    return pl.pallas_call(
        body, name="ssd_bwd", grid=(nsteps,),
        in_specs=[chunk(XBC_W), chunk(XBC_W),
                  chunk(SSM_HEADS), _full((CONV_K, XBC_W)), _full((1, SSM_HEADS)),
                  _full((1, SSM_HEADS)), _full((1, SSM_W)), _full((SSM_W, SSM_HEADS)), _full((3 * SSM_HEADS, SSM_W)),
                  pl.BlockSpec((ch, SSM_N, SSM_W), lambda i: (nsteps - 1 - i, 0, 0)), chunk(SSM_W)],
        out_specs=[chunk(XBC_W), chunk(SSM_HEADS), _full((CONV_K, XBC_W)), _full((1, XBC_W)),
                   _full((1, SSM_HEADS)), _full((1, SSM_HEADS)), _full((1, SSM_HEADS))],
        out_shape=[sd((s, XBC_W), BF), sd((s, SSM_HEADS), BF), sd((CONV_K, XBC_W), F32), sd((1, XBC_W), F32),
                   sd((1, SSM_HEADS), F32), sd((1, SSM_HEADS), F32), sd((1, SSM_HEADS), F32)],
        scratch_shapes=[pltpu.VMEM((SSM_N, SSM_W), F32), pltpu.VMEM((8, XBC_W), F32),
                        pltpu.VMEM((1, SSM_W), F32), pltpu.VMEM((BLOCK, SSM_W), F32),
                        pltpu.VMEM((BLOCK, 2 * SSM_G * SSM_N), F32), pltpu.VMEM((BLOCK, SSM_W), F32)],
        compiler_params=_params(dimension_semantics=("arbitrary",)),
    )(xbc, conv_all, dt_raw, conv_w, dt_bias, a_log, dsk_x, e_mat, e3t, hprev_all, dy_all)


def _dh(x, dout, norm_w, scale, dsegs, w_t, tm=256):
    s = x.shape[0]

    def body(x_ref, dout_ref, nw_ref, sc_ref, *rest):
        d_refs, w_hbm = rest[:9], rest[9]
        gx_ref, dshift_ref, dscale_ref, gnw_ref = rest[10:14]
        w_vm, sem = rest[14], rest[15]
        first = pl.program_id(0) == 0
        cps = [pltpu.make_async_copy(w_hbm.at[SEG_OFF[j]:SEG_OFF[j + 1], :], w_vm.at[SEG_OFF[j]:SEG_OFF[j + 1], :], sem.at[j])
               for j in range(9)]

        def tile(waiting):
            dh = None
            for j in range(9):
                if waiting:
                    cps[j].wait()
                part = _dot(d_refs[j][...], w_vm[SEG_OFF[j]:SEG_OFF[j + 1], :])
                dh = part if dh is None else dh + part
            xv = x_ref[...]
            r = lax.rsqrt(jnp.mean(xv * xv, axis=-1, keepdims=True) + EPS)
            xn = xv * r
            nw = nw_ref[...]
            sc1 = 1.0 + sc_ref[...]
            dshift_ref[...] += jnp.sum(dh, axis=0, keepdims=True)
            dhxn = jnp.sum(dh * xn, axis=0, keepdims=True)
            dscale_ref[...] += dhxn * nw
            gnw_ref[...] += dhxn * sc1
            dxn = dh * (nw * sc1)
            gx_ref[...] = dout_ref[...] + r * (dxn - xn * jnp.mean(xn * dxn, axis=-1, keepdims=True))

        @pl.when(first)
        def _():
            for cp in cps:
                cp.start()
            for ref in (dshift_ref, dscale_ref, gnw_ref):
                ref[...] = jnp.zeros_like(ref)
            tile(True)

        @pl.when(jnp.logical_not(first))
        def _():
            tile(False)

    vec = _full((1, D_MODEL))
    sd = jax.ShapeDtypeStruct
    return pl.pallas_call(
        body, name="dh", grid=(s // tm,),
        in_specs=[_rows(tm, D_MODEL), _rows(tm, D_MODEL), vec, vec] + [_rows(tm, w) for w in SEG_W] + [ANY],
        out_specs=[_rows(tm, D_MODEL), vec, vec, vec],
        out_shape=[sd((s, D_MODEL), F32), sd((1, D_MODEL), F32), sd((1, D_MODEL), F32), sd((1, D_MODEL), F32)],
        scratch_shapes=[pltpu.VMEM((IN_W, D_MODEL), BF), pltpu.SemaphoreType.DMA((9,))],
        compiler_params=_params(dimension_semantics=("arbitrary",)),
    )(x, dout, norm_w, scale, *dsegs, w_t)


def _gw_seg(h, dseg, name, tm=1024):
    s, w = dseg.shape
    tn = min(w, 1024)
    tm = min(tm, s)
    nm = s // tm

    def body(h_ref, d_ref, o_ref, acc):
        m = pl.program_id(1)

        @pl.when(m == 0)
        def _():
            acc[...] = jnp.zeros_like(acc)

        acc[...] += _dot_tn(d_ref[...], h_ref[...])

        @pl.when(m == nm - 1)
        def _():
            o_ref[...] = _bf(acc[...])

    return pl.pallas_call(
        body, name=name, grid=(w // tn, nm),
        in_specs=[pl.BlockSpec((tm, D_MODEL), lambda n, m: (m, 0)), pl.BlockSpec((tm, tn), lambda n, m: (m, n))],
        out_specs=pl.BlockSpec((tn, D_MODEL), lambda n, m: (n, 0)),
        out_shape=jax.ShapeDtypeStruct((w, D_MODEL), BF),
        scratch_shapes=[pltpu.VMEM((tn, D_MODEL), F32)],
        compiler_params=_params(dimension_semantics=("arbitrary", "arbitrary")),
    )(h, dseg)


def _gw_in(h, dsegs):
    return [_gw_seg(h, d, "gw_in_%d" % j) for j, d in enumerate(dsegs)]


def _local_step(x, tgt, shift, scale, gate, w_t, rows_fn, norm_w, qnw, knw, rel_bias, sinks,
                conv_w, conv_b, dt_bias, a_log, d_skip, ssm_nw, after_mid=None, after_gw=None):
    oh_t = _bucket_onehot_t()
    bias = _masked_bias(_bias_dense(rel_bias.T, oh_t).reshape(ATTN_HEADS, BLOCK, 2 * BLOCK))
    *segs, h = _inproj(x, norm_w, scale, shift, w_t)
    q, k, v, za, zm, xbc, dtr, ga, gb = segs
    consts = _attn_consts(qnw, knw)
    o_att, lse = _attn_fwd(q, k, v, bias, sinks, consts)
    e_mat, e3t = _membership(SSM_W, SSM_P, SSM_HEADS)
    dsk_x = jnp.repeat(d_skip, SSM_P, axis=1)
    ypre, hprev, conv = _ssd_fwd(xbc, dtr, conv_w, conv_b, dt_bias, a_log, dsk_x, e3t)
    (dout, d_o, dza, dyp, dzm, dga, dgb, yag, dy_a, yn, dy_b, merged, dob, g_ssm_nw, dgate, loss) = _mid(
        x, tgt, o_att, za, ypre, zm, ga, gb, gate, ssm_nw, rows_fn(ypre))
    g_wap = _gw_seg(dy_a, yag, "gw_attn_proj")
    g_wsp = _gw_seg(dy_b, yn, "gw_ssm_proj")
    g_wout = _gw_seg(dob, merged, "gw_out")
    zero = after_mid(g_wap, g_wsp, g_wout) if after_mid is not None else 0.0
    dq, dk, dv, dss, g_qnw, g_knw, g_sinks = _attn_bwd(q, k, v, bias, sinks + zero, consts, o_att, lse, d_o)
    g_rel = _bias_grad(dss.reshape(ATTN_HEADS, BLOCK * 2 * BLOCK), oh_t).T
    dxbc, ddt, g_cw, g_cb, g_dtb, g_alog, g_dsk = _ssd_bwd(
        xbc, conv, dtr, conv_w, dt_bias, a_log, dsk_x, e_mat, e3t, hprev, dyp)
    dsegs = (dq, dk, dv, dza, dzm, dxbc, ddt, dga, dgb)
    g_ws = _gw_in(h, dsegs)
    zero = after_gw(g_ws) if after_gw is not None else 0.0
    gx, dshift, dscale, g_nw = _dh(x, dout, norm_w + zero, scale, dsegs, w_t)
    return dict(loss=loss, grad_x=gx, dmod=jnp.concatenate([dshift, dscale, dgate], axis=1), g_ws=g_ws,
                g_wap=g_wap, g_wsp=g_wsp, g_wout=g_wout, g_norm_w=g_nw, g_qnw=g_qnw, g_knw=g_knw, g_rel=g_rel,
                g_sinks=g_sinks, g_conv_w=g_cw, g_conv_b=g_cb, g_dt_bias=g_dtb, g_a_log=g_alog, g_d_skip=g_dsk,
                g_ssm_nw=g_ssm_nw)


def _me():
    return lax.axis_index("x"), lax.axis_index("y"), lax.axis_index("c")


def _flip(v, bit):
    return 1 - v if bit else v


def _ag_direct(v, name):
    def body(v_ref, out_ref, send_sems, recv_sems, local_sem):
        x, y, c = _me()
        me = 4 * x + 2 * y + c
        mine = pltpu.make_async_copy(v_ref, out_ref.at[me], local_sem)
        mine.start()
        peers = [(_flip(x, k >> 2 & 1), _flip(y, k >> 1 & 1), _flip(c, k & 1)) for k in range(1, N_DEV)]
        sends = [pltpu.make_async_remote_copy(
            src_ref=v_ref, dst_ref=out_ref.at[me], send_sem=send_sems.at[j], recv_sem=recv_sems.at[j],
            device_id=p, device_id_type=MESH) for j, p in enumerate(peers)]
        for cp in sends:
            cp.start()
        for j, (px, py, pc) in enumerate(peers):
            pltpu.make_async_remote_copy(
                src_ref=v_ref, dst_ref=out_ref.at[4 * px + 2 * py + pc], send_sem=send_sems.at[j],
                recv_sem=recv_sems.at[j], device_id=(px, py, pc), device_id_type=MESH).wait_recv()
        for cp in sends:
            cp.wait_send()
        mine.wait()

    vm = pl.BlockSpec(memory_space=pltpu.VMEM)
    return pl.pallas_call(
        body, name=name, out_shape=jax.ShapeDtypeStruct((N_DEV,) + v.shape, v.dtype),
        in_specs=[vm], out_specs=vm,
        scratch_shapes=[pltpu.SemaphoreType.DMA((N_DEV - 1,)), pltpu.SemaphoreType.DMA((N_DEV - 1,)),
                        pltpu.SemaphoreType.DMA],
        compiler_params=_params(),
    )(v)


def _ag_two_level(v, name):
    def body(v_ref, out_ref, token, send_sems, recv_sems, local_sem):
        token[...] = jnp.zeros_like(token)
        x, y, c = _me()
        me, sibling = (x, y, c), (x, y, 1 - c)
        chips = [(1 - x, y), (x, 1 - y), (1 - x, 1 - y)]

        def slot(px, py, pc):
            return out_ref.at[4 * px + 2 * py + pc]

        def copy(k, block, to, src=None):
            return pltpu.make_async_remote_copy(
                src_ref=slot(*block) if src is None else src, dst_ref=slot(*block),
                send_sem=send_sems.at[k], recv_sem=recv_sems.at[k], device_id=to, device_id_type=MESH)

        mine = pltpu.make_async_copy(v_ref, slot(*me), local_sem)
        mine.start()
        first = [copy(0, me, sibling, src=v_ref)]
        first += [copy(1 + j, me, (*chip, c), src=v_ref) for j, chip in enumerate(chips)]
        for cp in first:
            cp.start()
        passed = [copy(4 + j, (*chip, c), sibling) for j, chip in enumerate(chips)]
        for j, chip in enumerate(chips):
            copy(1 + j, (*chip, c), me).wait_recv()
            passed[j].start()
        copy(0, sibling, me).wait_recv()
        for j, chip in enumerate(chips):
            copy(4 + j, (*chip, 1 - c), me).wait_recv()
        for cp in first + passed:
            cp.wait_send()
        mine.wait()

    out, token = pl.pallas_call(
        body, name=name,
        out_shape=(jax.ShapeDtypeStruct((N_DEV,) + v.shape, v.dtype), jax.ShapeDtypeStruct((8, 128), v.dtype)),
        in_specs=[ANY], out_specs=(ANY, pl.BlockSpec(memory_space=pltpu.VMEM)),
        scratch_shapes=[pltpu.SemaphoreType.DMA((7,)), pltpu.SemaphoreType.DMA((7,)), pltpu.SemaphoreType.DMA],
        compiler_params=_params(),
    )(v)
    return out, token[0:1, 0:1]


def _rs_sibling(g, name):
    def body(g_ref, out_ref, send_sems, recv_sems):
        x, y, c = _me()
        cps = [pltpu.make_async_remote_copy(
            src_ref=g_ref.at[2 * ch + 1 - c], dst_ref=out_ref.at[ch], send_sem=send_sems.at[ch],
            recv_sem=recv_sems.at[ch], device_id=(x, y, 1 - c), device_id_type=MESH) for ch in range(4)]
        for cp in cps:
            cp.start()
        for cp in cps:
            cp.wait()

    return pl.pallas_call(
        body, name=name, out_shape=jax.ShapeDtypeStruct((4,) + g.shape[1:], g.dtype),
        in_specs=[ANY], out_specs=ANY,
        scratch_shapes=[pltpu.SemaphoreType.DMA((4,)), pltpu.SemaphoreType.DMA((4,))],
        compiler_params=_params(),
    )(g)


def _add_sibling(g, got, name):
    _, r, n = g.shape
    tr = min(r, 256)

    def body(c_ref, a_ref, b_ref, o_ref):
        o_ref[...] = a_ref[...] + b_ref[...]

    grid_spec = pltpu.PrefetchScalarGridSpec(
        num_scalar_prefetch=1, grid=(4, r // tr),
        in_specs=[pl.BlockSpec((1, tr, n), lambda ch, i, c_ref: (2 * ch + c_ref[0], i, 0)),
                  pl.BlockSpec((1, tr, n), lambda ch, i, c_ref: (ch, i, 0))],
        out_specs=pl.BlockSpec((1, tr, n), lambda ch, i, c_ref: (ch, i, 0)))
    return pl.pallas_call(
        body, name=name, grid_spec=grid_spec, out_shape=jax.ShapeDtypeStruct((4, r, n), g.dtype),
        compiler_params=_params(dimension_semantics=("arbitrary", "arbitrary")),
    )(lax.axis_index("c").reshape(1).astype(jnp.int32), g, got)


def _rs_chips(p, name):
    def body(p_ref, out_ref, send_sems, recv_sems, local_sem):
        x, y, c = _me()
        my_chip = 2 * x + y
        mine = pltpu.make_async_copy(p_ref.at[my_chip], out_ref.at[my_chip], local_sem)
        mine.start()
        chips = [(1 - x, y), (x, 1 - y), (1 - x, 1 - y)]
        sends = [pltpu.make_async_remote_copy(
            src_ref=p_ref.at[2 * px + py], dst_ref=out_ref.at[my_chip], send_sem=send_sems.at[j],
            recv_sem=recv_sems.at[j], device_id=(px, py, c), device_id_type=MESH) for j, (px, py) in enumerate(chips)]
        for cp in sends:
            cp.start()
        for j, (px, py) in enumerate(chips):
            pltpu.make_async_remote_copy(
                src_ref=p_ref.at[my_chip], dst_ref=out_ref.at[2 * px + py], send_sem=send_sems.at[j],
                recv_sem=recv_sems.at[j], device_id=(px, py, c), device_id_type=MESH).wait_recv()
        for cp in sends:
            cp.wait_send()
        mine.wait()

    return pl.pallas_call(
        body, name=name, out_shape=jax.ShapeDtypeStruct(p.shape, p.dtype),
        in_specs=[ANY], out_specs=ANY,
        scratch_shapes=[pltpu.SemaphoreType.DMA((3,)), pltpu.SemaphoreType.DMA((3,)), pltpu.SemaphoreType.DMA],
        compiler_params=_params(),
    )(p)


HBM = pl.BlockSpec(memory_space=pltpu.HBM)
SEM = pl.BlockSpec(memory_space=pltpu.SEMAPHORE)
EFFECT = pltpu.SideEffectType.DATAFLOW_SIDE_EFFECTING


def _peers(x, y, c):
    return [(_flip(x, k >> 2 & 1), _flip(y, k >> 1 & 1), _flip(c, k & 1)) for k in range(1, N_DEV)]


def _exchange_start(src, land, gather, name):
    def body(src_ref, land_ref, send_sems, recv_sems, src_thru, land_thru, token):
        x, y, c = _me()
        me = 4 * x + 2 * y + c
        for j, (px, py, pc) in enumerate(_peers(x, y, c)):
            pltpu.make_async_remote_copy(
                src_ref=src_ref if gather else src_ref.at[4 * px + 2 * py + pc], dst_ref=land_ref.at[me],
                send_sem=send_sems.at[j], recv_sem=recv_sems.at[j], device_id=(px, py, pc), device_id_type=MESH).start()
        token[...] = jnp.zeros_like(token)

    sems = pltpu.SemaphoreType.DMA((N_DEV - 1,))
    out = pl.pallas_call(
        body, name=name,
        out_shape=(sems, sems, pltpu.HBM(src.shape, src.dtype), pltpu.HBM(land.shape, land.dtype),
                   jax.ShapeDtypeStruct((8, 128), F32)),
        in_specs=(HBM, HBM), out_specs=(SEM, SEM, HBM, HBM, pl.BlockSpec(memory_space=pltpu.VMEM)),
        input_output_aliases={0: 2, 1: 3},
        compiler_params=pltpu.CompilerParams(has_side_effects=EFFECT),
    )(pltpu.with_memory_space_constraint(src, pltpu.HBM), pltpu.with_memory_space_constraint(land, pltpu.HBM))
    return out[:4], out[4][0, 0]


def _exchange_wait(started, after, gather, name):
    send_sems, recv_sems, src_thru, land_thru = started

    def body(src_ref, land_ref, send_sems, recv_sems, after_ref, src_dead, got_ref):
        x, y, c = _me()
        for j, (px, py, pc) in enumerate(_peers(x, y, c)):
            pid = 4 * px + 2 * py + pc
            cp = pltpu.make_async_remote_copy(
                src_ref=src_ref if gather else src_ref.at[pid], dst_ref=land_ref.at[pid],
                send_sem=send_sems.at[j], recv_sem=recv_sems.at[j], device_id=(px, py, pc), device_id_type=MESH)
            cp.wait_send()
            cp.wait_recv()

    return pl.pallas_call(
        body, name=name,
        out_shape=(pltpu.HBM(src_thru.shape, src_thru.dtype), pltpu.HBM(land_thru.shape, land_thru.dtype)),
        in_specs=(HBM, HBM, SEM, SEM, ANY), out_specs=(HBM, HBM), input_output_aliases={0: 0, 1: 1},
        compiler_params=pltpu.CompilerParams(has_side_effects=EFFECT),
    )(src_thru, land_thru, send_sems, recv_sems, after)[1]


def _reduce_scatter(g, name):
    got = _rs_sibling(g, name + "_sib")
    return _rs_chips(_add_sibling(g, got, name + "_add"), name + "_chips")


def _silu(a):
    return a * _sig(a)


def _mod_piece(c_all, w_ada, b_piece):
    def body(c_ref, w_ref, b_ref, o_ref):
        o_ref[...] = _dot(_bf(_silu(c_ref[...])), _bf(w_ref[...])) + b_ref[...]

    return pl.pallas_call(
        body, name="mod_piece", out_shape=jax.ShapeDtypeStruct((c_all.shape[0], w_ada.shape[1]), F32),
        compiler_params=_params(),
    )(c_all, w_ada, b_piece)


def _gw_ada(c_all, dmod_piece):
    def body(c_ref, d_ref, o_ref):
        o_ref[...] = _dot_tn(_bf(_silu(c_ref[...])), _bf(d_ref[...]))

    return pl.pallas_call(
        body, name="gw_ada", out_shape=jax.ShapeDtypeStruct((c_all.shape[1], dmod_piece.shape[1]), F32),
        compiler_params=_params(),
    )(c_all, dmod_piece)


def _adam(parts, w, m, v, name):
    k, r, n = parts.shape
    if r <= 256 or r % 256 == 0:
        tr, tn = min(r, 256), n
    else:
        tr, tn = r, 256
    assert r % tr == 0 and n % tn == 0

    def body(p_ref, w_ref, m_ref, v_ref, g_ref, d_ref, nm_ref, nv_ref):
        g = p_ref[0].astype(F32)
        for j in range(1, k):
            g = g + p_ref[j].astype(F32)
        g_ref[...] = g
        d_ref[...], nm_ref[...], nv_ref[...] = _adam_math(g, w_ref[...], m_ref[...], v_ref[...])

    blk = pl.BlockSpec((tr, tn), lambda i, j: (i, j))
    return pl.pallas_call(
        body, name=name, grid=(r // tr, n // tn),
        in_specs=[pl.BlockSpec((k, tr, tn), lambda i, j: (0, i, j)), blk, blk, blk],
        out_specs=[blk, blk, blk, blk],
        out_shape=[jax.ShapeDtypeStruct((r, n), F32)] * 4,
        compiler_params=_params(dimension_semantics=("arbitrary", "arbitrary")),
    )(parts, w, m, v)


def _adam_math(g, w, m, v):
    m_new = ADAM_B1 * m + (1.0 - ADAM_B1) * g
    v_new = ADAM_B2 * v + (1.0 - ADAM_B2) * jnp.square(g)
    m_hat = m_new / (1.0 - ADAM_B1 ** ADAM_STEP)
    v_hat = v_new / (1.0 - ADAM_B2 ** ADAM_STEP)
    return -ADAM_LR * (m_hat / (jnp.sqrt(v_hat) + ADAM_EPS) + ADAM_WD * w), m_new, v_new


_SMALL = (("b_ada", 3 * D_MODEL), ("norm_w", D_MODEL), ("q_norm_w", HEAD_DIM), ("k_norm_w", HEAD_DIM),
          ("rel_bias", REL_BUCKETS * ATTN_HEADS), ("sinks", ATTN_HEADS), ("conv_b", XBC_W), ("dt_bias", SSM_HEADS),
          ("a_log", SSM_HEADS), ("d_skip", SSM_HEADS), ("ssm_norm_w", SSM_W))
_SLOT = tuple(-(-n // 128) * 128 for _, n in _SMALL)
_SLOT_OFF = tuple(int(o) for o in np.cumsum((0,) + _SLOT))
_LOSS_OFF = _SLOT_OFF[-1]
_CW_OFF = _LOSS_OFF + 128
_PACK_N = _CW_OFF + CONV_K * XBC_W


def _pack_partials(small, loss, g_conv_w):
    parts = []
    for (name, n), slot in zip(_SMALL, _SLOT):
        parts.append(small[name].reshape(1, n))
        if slot > n:
            parts.append(jnp.zeros((1, slot - n), F32))
    parts += [loss.reshape(1, 1), jnp.zeros((1, 127), F32), g_conv_w.reshape(1, CONV_K * XBC_W)]
    return jnp.concatenate(parts, axis=1)


def _adam_small(pack_all, w, m, v):
    names = [name for name, _ in _SMALL]

    def body(p_ref, *rest):
        ins, outs = rest[:3 * len(names)], rest[3 * len(names):]

        def total(off, n):
            g = p_ref[0, :, off:off + n]
            for d in range(1, N_DEV):
                g = g + p_ref[d, :, off:off + n]
            return g

        for j, (name, n) in enumerate(_SMALL):
            g = total(_SLOT_OFF[j], n)
            delta, m_new, v_new = _adam_math(g, ins[3 * j][...], ins[3 * j + 1][...], ins[3 * j + 2][...])
            outs[4 * j][...] = g
            outs[4 * j + 1][...] = delta
            outs[4 * j + 2][...] = m_new
            outs[4 * j + 3][...] = v_new
        outs[-1][...] = total(_LOSS_OFF, 1)

    flat = []
    for name, n in _SMALL:
        flat += [w[name].reshape(1, n), m[name].reshape(1, n), v[name].reshape(1, n)]
    out_shape = [jax.ShapeDtypeStruct((1, n), F32) for _, n in _SMALL for _ in range(4)] + [jax.ShapeDtypeStruct((1, 1), F32)]
    out = pl.pallas_call(body, name="adam_small", out_shape=out_shape, compiler_params=_params())(pack_all, *flat)
    res = {name: [out[4 * j + t].reshape(w[name].shape) for t in range(4)] for j, name in enumerate(names)}
    return res, out[-1]


WEIGHTS = ("w_ada", "b_ada", "norm_w", "w_in", "q_norm_w", "k_norm_w", "rel_bias", "sinks", "conv_w", "conv_b",
           "dt_bias", "a_log", "d_skip", "ssm_norm_w", "w_attn_proj", "w_ssm_proj", "w_out")


def kernel(x, c, w_ada, b_ada, norm_w, w_in, q_norm_w, k_norm_w, rel_bias, sinks, conv_w, conv_b, dt_bias, a_log, d_skip, ssm_norm_w, w_attn_proj, w_ssm_proj, w_out, loss_target, m_w_ada, m_b_ada, m_norm_w, m_w_in, m_q_norm_w, m_k_norm_w, m_rel_bias, m_sinks, m_conv_w, m_conv_b, m_dt_bias, m_a_log, m_d_skip, m_ssm_norm_w, m_w_attn_proj, m_w_ssm_proj, m_w_out, v_w_ada, v_b_ada, v_norm_w, v_w_in, v_q_norm_w, v_k_norm_w, v_rel_bias, v_sinks, v_conv_w, v_conv_b, v_dt_bias, v_a_log, v_d_skip, v_ssm_norm_w, v_w_attn_proj, v_w_ssm_proj, v_w_out):
    w = dict(w_ada=w_ada, b_ada=b_ada, norm_w=norm_w, w_in=w_in, q_norm_w=q_norm_w, k_norm_w=k_norm_w,
             rel_bias=rel_bias, sinks=sinks, conv_w=conv_w, conv_b=conv_b, dt_bias=dt_bias, a_log=a_log,
             d_skip=d_skip, ssm_norm_w=ssm_norm_w, w_attn_proj=w_attn_proj, w_ssm_proj=w_ssm_proj, w_out=w_out)
    m = dict(w_ada=m_w_ada, b_ada=m_b_ada, norm_w=m_norm_w, w_in=m_w_in, q_norm_w=m_q_norm_w, k_norm_w=m_k_norm_w,
             rel_bias=m_rel_bias, sinks=m_sinks, conv_w=m_conv_w, conv_b=m_conv_b, dt_bias=m_dt_bias, a_log=m_a_log,
             d_skip=m_d_skip, ssm_norm_w=m_ssm_norm_w, w_attn_proj=m_w_attn_proj, w_ssm_proj=m_w_ssm_proj, w_out=m_w_out)
    v = dict(w_ada=v_w_ada, b_ada=v_b_ada, norm_w=v_norm_w, w_in=v_w_in, q_norm_w=v_q_norm_w, k_norm_w=v_k_norm_w,
             rel_bias=v_rel_bias, sinks=v_sinks, conv_w=v_conv_w, conv_b=v_conv_b, dt_bias=v_dt_bias, a_log=v_a_log,
             d_skip=v_d_skip, ssm_norm_w=v_ssm_norm_w, w_attn_proj=v_w_attn_proj, w_ssm_proj=v_w_ssm_proj, w_out=v_w_out)
    me = 4 * lax.axis_index("x") + 2 * lax.axis_index("y") + lax.axis_index("c")
    ada_n = w_ada.shape[2]
    in_n = w_in.shape[2]
    cw_n = conv_w.shape[2]

    first = _ag_direct(jnp.concatenate([c, conv_w[0].reshape(1, CONV_K * cw_n)], axis=1), "ag_c")[:, 0]
    c_all = first[:, :D_MODEL]
    conv_w_full = first[:, D_MODEL:].reshape(N_DEV, CONV_K, cw_n).transpose(1, 0, 2).reshape(CONV_K, XBC_W)
    b_piece = lax.dynamic_slice_in_dim(b_ada, me * ada_n, ada_n, axis=1)
    mod_all = _ag_direct(_mod_piece(c_all, w_ada[0], b_piece), "ag_mod")
    mod = lax.dynamic_index_in_dim(mod_all, me, axis=1, keepdims=False).reshape(1, 3 * D_MODEL)
    shift, scale, gate = mod[:, :D_MODEL], mod[:, D_MODEL:2 * D_MODEL], mod[:, 2 * D_MODEL:]

    w_t, zero = _ag_two_level(w_in[0].T.astype(BF), "ag_w_in")
    w_t = w_t.reshape(N_DEV * in_n, D_MODEL)

    def with_mine(blocks, mine):
        return lax.dynamic_update_index_in_dim(lax.empty(blocks, mine.dtype), mine, me, axis=0)

    rows = jnp.concatenate([w_attn_proj[0], w_ssm_proj[0], w_out[0]], axis=0).astype(BF) + zero
    r_ap, r_sp = w_attn_proj.shape[1], w_ssm_proj.shape[1]
    rows_started, zero = _exchange_start(rows, with_mine((N_DEV,) + rows.shape, rows), True, "ag_rows_start")

    def rows_fn(after):
        return _exchange_wait(rows_started, after, True, "ag_rows_wait")

    started = {}

    def send_blocks(key, g, name):
        started[key], zero = _exchange_start(
            g, with_mine(g.shape, lax.dynamic_index_in_dim(g, me, axis=0, keepdims=False)), False, name)
        return zero

    def after_mid(g_wap, g_wsp, g_wout):
        return send_blocks("rows", jnp.concatenate(
            [g_wap.reshape(N_DEV, r_ap, D_MODEL), g_wsp.reshape(N_DEV, r_sp, D_MODEL),
             g_wout.reshape(N_DEV, r_ap, D_MODEL)], axis=1), "rs_rows_start")

    def after_gw(g_ws):
        return send_blocks("in", jnp.concatenate(g_ws, axis=0).reshape(N_DEV, in_n, D_MODEL), "rs_in_start")

    r = _local_step(x[0], loss_target[0], shift, scale + zero, gate, w_t, rows_fn, norm_w, q_norm_w, k_norm_w,
                    rel_bias, sinks, conv_w_full, conv_b, dt_bias, a_log, d_skip, ssm_norm_w, after_mid, after_gw)

    small = dict(b_ada=r["dmod"], norm_w=r["g_norm_w"], q_norm_w=r["g_qnw"], k_norm_w=r["g_knw"], rel_bias=r["g_rel"],
                 sinks=r["g_sinks"], conv_b=r["g_conv_b"], dt_bias=r["g_dt_bias"], a_log=r["g_a_log"],
                 d_skip=r["g_d_skip"], ssm_norm_w=r["g_ssm_nw"])
    pack_all = _ag_direct(_pack_partials(small, r["loss"], r["g_conv_w"]), "ag_small")
    res, loss = _adam_small(pack_all, w, m, v)
    loss = loss[0, 0]
    cw_parts = pack_all[:, 0, _CW_OFF:].reshape(N_DEV, CONV_K, XBC_W)
    cw_mine = lax.dynamic_slice_in_dim(cw_parts, me * cw_n, cw_n, axis=2)
    res["conv_w"] = [a[None] for a in _adam(cw_mine, conv_w[0], m_conv_w[0], v_conv_w[0], "adam_conv_w")]

    dmod_piece = lax.dynamic_slice_in_dim(pack_all[:, 0, :3 * D_MODEL], me * ada_n, ada_n, axis=1)
    g_ada = _gw_ada(c_all, dmod_piece)
    res["w_ada"] = [a[None] for a in _adam(g_ada[None], w_ada[0], m_w_ada[0], v_w_ada[0], "adam_w_ada")]

    cat = lambda d: jnp.concatenate([d["w_attn_proj"][0], d["w_ssm_proj"][0], d["w_out"][0]], axis=0)
    rows_res = _adam(_exchange_wait(started["rows"], g_ada, False, "rs_rows_wait"), cat(w), cat(m), cat(v), "adam_w_rows")
    res["w_in"] = [a.T[None] for a in _adam(_exchange_wait(started["in"], rows_res[0], False, "rs_in_wait"),
                                            w_in[0].T, m_w_in[0].T, v_w_in[0].T, "adam_w_in")]
    res["w_attn_proj"] = [a[None, :r_ap] for a in rows_res]
    res["w_ssm_proj"] = [a[None, r_ap:r_ap + r_sp] for a in rows_res]
    res["w_out"] = [a[None, r_ap + r_sp:] for a in rows_res]

    outs = [loss, r["grad_x"][None]]
    for j in range(4):
        outs += [res[name][j] for name in WEIGHTS]
    return tuple(outs)
```

```python
import functools
import math

import numpy as np
import jax
import jax.numpy as jnp
from jax import lax
from jax.experimental import pallas as pl
from jax.experimental.pallas import tpu as pltpu

F32 = jnp.float32
BF = jnp.bfloat16
HI = lax.Precision.HIGHEST

D_MODEL = 1024
ATTN_HEADS = 16
KV_HEADS = 4
GRP = ATTN_HEADS // KV_HEADS
HEAD_DIM = 64
ATTN_W = ATTN_HEADS * HEAD_DIM
KV_W = KV_HEADS * HEAD_DIM
BLOCK = 128
REL_BUCKETS = 32
REL_MAX_DIST = 128
SSM_W = 2048
SSM_P = 64
SSM_HEADS = 32
SSM_G = 4
SSM_R = 8
SSM_N = 128
CONV_K = 4
XBC_W = SSM_W + 2 * SSM_G * SSM_N
SEG_W = (ATTN_W, KV_W, KV_W, ATTN_W, SSM_W, XBC_W, SSM_HEADS, D_MODEL, D_MODEL)
SEG_OFF = tuple(int(v) for v in np.cumsum((0,) + SEG_W))
IN_W = SEG_OFF[-1]
GATE_SEGS = (3, 4, 7, 8)
EPS = 1e-6
N_DEV = 8
ADAM_LR, ADAM_B1, ADAM_B2, ADAM_EPS, ADAM_WD, ADAM_STEP = 0.001, 0.9, 0.999, 1e-08, 0.01, 10
VMEM_LIMIT = 60 * 1024 * 1024
MESH = pl.DeviceIdType.MESH
ANY = pl.BlockSpec(memory_space=pl.ANY)


def _dot(a, b, precision=None):
    return jnp.dot(a, b, preferred_element_type=F32, precision=precision)


def _dot_nt(a, b, precision=None):
    return lax.dot_general(a, b, (((1,), (1,)), ((), ())), preferred_element_type=F32, precision=precision)


def _dot_tn(a, b, precision=None):
    return lax.dot_general(a, b, (((0,), (0,)), ((), ())), preferred_element_type=F32, precision=precision)


def _bf(a):
    return a.astype(BF)


def _sig(a):
    return 0.5 * jnp.tanh(0.5 * a) + 0.5


def _params(**kw):
    return pltpu.CompilerParams(vmem_limit_bytes=VMEM_LIMIT, **kw)


def _full(shape):
    nd = len(shape)
    return pl.BlockSpec(shape, lambda i: (0,) * nd)


def _rows(tm, w):
    return pl.BlockSpec((tm, w), lambda i: (i, 0))


def _inproj(x, norm_w, scale, shift, w_t, tm=256):
    s = x.shape[0]

    def body(x_ref, nw_ref, sc_ref, sh_ref, w_hbm, *rest):
        outs, h_ref, w_vm, sem = rest[:9], rest[9], rest[10], rest[11]
        first = pl.program_id(0) == 0
        cps = [pltpu.make_async_copy(w_hbm.at[SEG_OFF[j]:SEG_OFF[j + 1], :], w_vm.at[SEG_OFF[j]:SEG_OFF[j + 1], :], sem.at[j])
               for j in range(9)]

        def tile(waiting):
            xv = x_ref[...]
            r = lax.rsqrt(jnp.mean(xv * xv, axis=-1, keepdims=True) + EPS)
            h = xv * r * (nw_ref[...] * (1.0 + sc_ref[...])) + sh_ref[...]
            hb = _bf(h)
            h_ref[...] = hb
            for j in range(9):
                if waiting:
                    cps[j].wait()
                outs[j][...] = _dot_nt(hb, w_vm[SEG_OFF[j]:SEG_OFF[j + 1], :]).astype(outs[j].dtype)

        @pl.when(first)
        def _():
            for cp in cps:
                cp.start()
            tile(True)

        @pl.when(jnp.logical_not(first))
        def _():
            tile(False)

    vec = _full((1, D_MODEL))
    return pl.pallas_call(
        body, name="inproj", grid=(s // tm,),
        in_specs=[_rows(tm, D_MODEL), vec, vec, vec, ANY],
        out_specs=[_rows(tm, w) for w in SEG_W] + [_rows(tm, D_MODEL)],
        out_shape=[jax.ShapeDtypeStruct((s, w), BF if j in GATE_SEGS else F32) for j, w in enumerate(SEG_W)]
                  + [jax.ShapeDtypeStruct((s, D_MODEL), BF)],
        scratch_shapes=[pltpu.VMEM((IN_W, D_MODEL), BF), pltpu.SemaphoreType.DMA((9,))],
        compiler_params=_params(dimension_semantics=("arbitrary",)),
    )(x, norm_w, scale, shift, w_t)


def _bucket_onehot_t():
    qi = jnp.arange(BLOCK)[:, None]
    kj = jnp.arange(2 * BLOCK)[None, :]
    dist = qi + BLOCK - kj
    n = jnp.maximum(dist, 0)
    max_exact = REL_BUCKETS // 2
    nf = jnp.maximum(n, 1).astype(F32)
    large = max_exact + (jnp.log(nf / max_exact) / math.log(REL_MAX_DIST / max_exact)
                         * (REL_BUCKETS - max_exact)).astype(jnp.int32)
    large = jnp.minimum(large, REL_BUCKETS - 1)
    bucket = jnp.where(n < max_exact, n, large).reshape(1, BLOCK * 2 * BLOCK)
    return (bucket == jnp.arange(REL_BUCKETS)[:, None]).astype(F32)


def _bias_dense(rel_bias_t, oh_t):
    def body(rb_ref, oh_ref, o_ref):
        o_ref[...] = _dot(rb_ref[...], oh_ref[...], HI)

    return pl.pallas_call(
        body, name="bias_dense", out_shape=jax.ShapeDtypeStruct((ATTN_HEADS, BLOCK * 2 * BLOCK), F32),
        compiler_params=_params(),
    )(rel_bias_t, oh_t)


def _bias_grad(ds_sum, oh_t):
    def body(ds_ref, oh_ref, o_ref):
        o_ref[...] = _dot_nt(ds_ref[...], oh_ref[...], HI)

    return pl.pallas_call(
        body, name="bias_grad", out_shape=jax.ShapeDtypeStruct((ATTN_HEADS, REL_BUCKETS), F32),
        compiler_params=_params(),
    )(ds_sum, oh_t)


def _group_sum(a, e):
    hi = _bf(a)
    return _dot(hi, e) + _dot(_bf(a - hi.astype(F32)), e)


def _group_bcast(a, e3t):
    hi = _bf(a)
    r1 = a - hi.astype(F32)
    mid = _bf(r1)
    return _dot(jnp.concatenate([hi, mid, _bf(r1 - mid.astype(F32))], axis=1), e3t)


def _membership(width, group, ngroups):
    e = (jnp.arange(width)[:, None] // group == jnp.arange(ngroups)[None, :]).astype(BF)
    return e, jnp.tile(e.T, (3, 1))


def _fold(width, group):
    return (jnp.arange(width)[:, None] % group == jnp.arange(group)[None, :]).astype(BF)


def _heads_norm(t, w_x, e, e3t):
    r = lax.rsqrt(_group_sum(t * t, e) * (1.0 / HEAD_DIM) + EPS)
    r_x = _group_bcast(r, e3t)
    return t * r_x * w_x, r_x


def _heads_norm_bwd(t, r_x, w_x, d, e, e3t):
    wd = d * w_x
    corr = _group_bcast(_group_sum(t * wd, e) * (1.0 / HEAD_DIM), e3t)
    return r_x * wd - t * (r_x * r_x * r_x) * corr, jnp.sum(d * t * r_x, axis=0, keepdims=True)


def _stack_heads(a, hk):
    return jnp.concatenate([a[:, (hk * GRP + g) * HEAD_DIM:(hk * GRP + g + 1) * HEAD_DIM] for g in range(GRP)], axis=0)


def _stack_cols(a, hk):
    return jnp.concatenate([a[:, hk * GRP + g:hk * GRP + g + 1] for g in range(GRP)], axis=0)


def _masked_bias(bias):
    qi = jnp.arange(BLOCK)[:, None]
    kj = jnp.arange(2 * BLOCK)[None, :]
    cur_ok = jnp.logical_and(kj >= BLOCK, kj - BLOCK <= qi)
    both_ok = jnp.logical_or(jnp.logical_and(kj < BLOCK, kj > qi), cur_ok)
    return jnp.stack([jnp.where(cur_ok, bias, -1e30), jnp.where(both_ok, bias, -1e30)])


def _attn_consts(qnw, knw):
    eq, eq3t = _membership(ATTN_W, HEAD_DIM, ATTN_HEADS)
    ek, ek3t = _membership(KV_W, HEAD_DIM, ATTN_HEADS)
    return (jnp.tile(qnw, (1, ATTN_HEADS)), jnp.tile(knw, (1, KV_HEADS)), eq, eq3t, ek, ek3t)


def _attn_fwd(q, k, v, bias, sinks, consts):
    s = q.shape[0]
    nb = s // BLOCK
    gq = GRP * BLOCK
    bias_t = bias.reshape(2, KV_HEADS, GRP, BLOCK, 2 * BLOCK).transpose(0, 1, 4, 2, 3).reshape(2, KV_HEADS, 2 * BLOCK, gq)
    sink_rows = jnp.repeat(sinks.reshape(KV_HEADS, GRP), BLOCK, axis=1).reshape(KV_HEADS, 1, gq)
    eye = jnp.eye(BLOCK, dtype=BF)

    def body(q_ref, kp_ref, kc_ref, vp_ref, vc_ref, b_ref, bt_ref, sk_ref, skr_ref, eye_ref,
             qw_ref, kw_ref, eq_ref, eq3_ref, ek_ref, ek3_ref, o_ref, lse_ref):
        qn = _bf(_heads_norm(q_ref[...], qw_ref[...], eq_ref[...], eq3_ref[...])[0] * (HEAD_DIM ** -0.5))
        kn = _bf(_heads_norm(jnp.concatenate([kp_ref[...], kc_ref[...]], axis=0), kw_ref[...], ek_ref[...], ek3_ref[...])[0])
        vv = _bf(jnp.concatenate([vp_ref[...], vc_ref[...]], axis=0))
        ones = jnp.ones((2 * BLOCK, HEAD_DIM), BF)
        lses = []
        kss = [slice(hk * HEAD_DIM, (hk + 1) * HEAD_DIM) for hk in range(KV_HEADS)]
        qgs = [_stack_heads(qn, hk) for hk in range(KV_HEADS)]
        sc_ts = [_dot_nt(kn[:, kss[hk]], qgs[hk]) + bt_ref[0, hk] for hk in range(KV_HEADS)]
        m8s = [_bf(jnp.broadcast_to(jnp.maximum(jnp.max(sc_ts[hk], axis=0, keepdims=True), skr_ref[hk]), (8, gq)))
               for hk in range(KV_HEADS)]
        ms = [jnp.concatenate([_dot_nt(eye_ref[...], m8[:, g * BLOCK:(g + 1) * BLOCK])[:, 0:1] for g in range(GRP)], axis=0)
              for m8 in m8s]
        scs = [_dot_nt(qgs[hk], kn[:, kss[hk]]) + b_ref[0, hk * GRP:(hk + 1) * GRP].reshape(gq, 2 * BLOCK)
               for hk in range(KV_HEADS)]
        ps = [_bf(jnp.exp(scs[hk] - ms[hk])) for hk in range(KV_HEADS)]
        pvs = [_dot(ps[hk], jnp.concatenate([vv[:, kss[hk]], ones], axis=1)) for hk in range(KV_HEADS)]
        for hk in range(KV_HEADS):
            m, pv = ms[hk], pvs[hk]
            sink = jnp.concatenate([jnp.full((BLOCK, 1), sk_ref[0, hk * GRP + g], F32) for g in range(GRP)], axis=0)
            den = pv[:, HEAD_DIM:HEAD_DIM + 1] + jnp.exp(sink - m)
            out = pv[:, :HEAD_DIM] * (1.0 / den)
            lse = m + jnp.log(den)
            for g in range(GRP):
                h = hk * GRP + g
                o_ref[:, h * HEAD_DIM:(h + 1) * HEAD_DIM] = out[g * BLOCK:(g + 1) * BLOCK]
                lses.append(lse[g * BLOCK:(g + 1) * BLOCK])
        lse_ref[...] = jnp.concatenate(lses, axis=1)

    cur = lambda w: pl.BlockSpec((BLOCK, w), lambda i: (i, 0))
    prev = lambda w: pl.BlockSpec((BLOCK, w), lambda i: (jnp.maximum(i - 1, 0), 0))
    whole = lambda a: pl.BlockSpec(a.shape, lambda i: (0,) * a.ndim)
    first_or_not = lambda a: pl.BlockSpec((1,) + a.shape[1:], lambda i: (jnp.minimum(i, 1),) + (0,) * (a.ndim - 1))
    return pl.pallas_call(
        body, name="attn_fwd", grid=(nb,),
        in_specs=[cur(ATTN_W), prev(KV_W), cur(KV_W), prev(KV_W), cur(KV_W), first_or_not(bias), first_or_not(bias_t),
                  pl.BlockSpec(memory_space=pltpu.SMEM), whole(sink_rows), whole(eye)] + [_full(c.shape) for c in consts],
        out_specs=[cur(ATTN_W), cur(ATTN_HEADS)],
        out_shape=[jax.ShapeDtypeStruct((s, ATTN_W), F32), jax.ShapeDtypeStruct((s, ATTN_HEADS), F32)],
        compiler_params=_params(dimension_semantics=("arbitrary",)),
    )(q, k, k, v, v, bias, bias_t, sinks, sink_rows, eye, *consts)


def _conv_taps(xbc, tail):
    ext = jnp.concatenate([tail, xbc], axis=0)
    return [pltpu.roll(ext, CONV_K - 1 - j, axis=0)[8:8 + BLOCK] if j < CONV_K - 1 else xbc for j in range(CONV_K)]


def _softplus(u):
    return jnp.maximum(u, 0.0) + jnp.log(1.0 + jnp.exp(-jnp.abs(u)))


def _tril():
    r = lax.broadcasted_iota(jnp.int32, (BLOCK, BLOCK), 0)
    c = lax.broadcasted_iota(jnp.int32, (BLOCK, BLOCK), 1)
    return r >= c


def _triu():
    r = lax.broadcasted_iota(jnp.int32, (BLOCK, BLOCK), 0)
    c = lax.broadcasted_iota(jnp.int32, (BLOCK, BLOCK), 1)
    return r <= c


def _exact_left(m01, a):
    hi = _bf(a)
    r1 = a - hi.astype(F32)
    mid = _bf(r1)
    return _dot(m01, hi) + _dot(m01, mid) + _dot(m01, _bf(r1 - mid.astype(F32)))


def _ssd_common(conv, dtr, dtb_ref, alog_ref, e3_ref):
    sg = _sig(conv)
    xact = conv * sg
    u = dtr + dtb_ref[...]
    dt = _softplus(u)
    a = -jnp.exp(alog_ref[...])
    trilb = _tril()
    acum = _exact_left(trilb.astype(BF), dt * a)
    both = _group_bcast(jnp.concatenate([dt, acum], axis=0), e3_ref[...])
    dt_x, acum_x = both[:BLOCK], both[BLOCK:]
    return sg, xact, u, dt, a, trilb, acum, dt_x, acum_x


SSD_CH = 2


def _ssd_fwd(xbc, dt_raw, conv_w, conv_b, dt_bias, a_log, dsk_x, e3t):
    s = xbc.shape[0]
    nc = s // BLOCK
    ch = SSD_CH if nc % SSD_CH == 0 else 1
    rows = ch * BLOCK

    def body(x_ref, tail_ref, dtr_ref, cw_ref, cb_ref, dtb_ref, alog_ref, dsk_ref, e3_ref,
             y_ref, hp_ref, conv_ref, hst, yd_s, yoff_s):
        i = pl.program_id(0)

        @pl.when(i == 0)
        def _():
            hst[...] = jnp.zeros_like(hst)

        for j in range(ch):
            rs = slice(j * BLOCK, (j + 1) * BLOCK)
            tail = jnp.where(i > 0, tail_ref[...], 0.0) if j == 0 else x_ref[j * BLOCK - 8:j * BLOCK, :]
            taps = _conv_taps(x_ref[rs, :], tail)
            conv = cb_ref[...] + sum(taps[t] * cw_ref[t:t + 1, :] for t in range(CONV_K))
            conv_ref[rs, :] = conv
            _, xact, _, _, _, trilb, acum, dt_x, acum_x = _ssd_common(conv, dtr_ref[rs, :], dtb_ref, alog_ref, e3_ref)
            xs = xact[:, :SSM_W]
            acum_t = acum.T
            ea_x = jnp.exp(acum_x)
            last_x = acum_x[BLOCK - 1:BLOCK, :]
            xdt = xs * dt_x
            xw = xdt * jnp.exp(last_x - acum_x)
            cd_x = jnp.exp(last_x)
            hprev = hst[...]
            hp_ref[j] = hprev
            sls = [slice(g * SSM_R * SSM_P, (g + 1) * SSM_R * SSM_P) for g in range(SSM_G)]
            bgs = [_bf(xact[:, SSM_W + g * SSM_N:SSM_W + (g + 1) * SSM_N]) for g in range(SSM_G)]
            cgs = [_bf(xact[:, SSM_W + SSM_G * SSM_N + g * SSM_N:SSM_W + SSM_G * SSM_N + (g + 1) * SSM_N])
                   for g in range(SSM_G)]
            xdt_b, xw_b, hprev_b = _bf(xdt), _bf(xw), _bf(hprev)
            low_half = lax.broadcasted_iota(jnp.int32, (BLOCK, 2 * SSM_P), 1) < SSM_P
            cbs = [_dot_nt(cgs[g], bgs[g]) for g in range(SSM_G)]
            for g in range(SSM_G):
                sl = sls[g]
                yoff_s[:, sl] = _dot(cgs[g], hprev_b[:, sl]) * ea_x[:, sl]
                hst[:, sl] = hprev[:, sl] * cd_x[:, sl] + _dot_tn(bgs[g], xw_b[:, sl])
            for g in range(SSM_G):
                hss = [slice((g * SSM_R + r) * SSM_P, (g * SSM_R + r + 1) * SSM_P) for r in range(SSM_R)]
                mms = [_bf(cbs[g] * jnp.exp(jnp.where(trilb, acum[:, g * SSM_R + r:g * SSM_R + r + 1]
                                                      - acum_t[g * SSM_R + r:g * SSM_R + r + 1, :], -1e30)))
                       for r in range(SSM_R)]
                for r in range(0, SSM_R, 2):
                    pair = slice(hss[r].start, hss[r + 1].stop)
                    xp = xdt_b[:, pair]
                    rhs = jnp.concatenate([jnp.where(low_half, xp, 0), jnp.where(low_half, 0, xp)], axis=0)
                    yd_s[:, pair] = _dot(jnp.concatenate([mms[r], mms[r + 1]], axis=1), rhs)
            y_ref[rs, :] = yd_s[...] + yoff_s[...] + dsk_ref[...] * xs

    blk = lambda w: pl.BlockSpec((rows, w), lambda i: (i, 0))
    return pl.pallas_call(
        body, name="ssd_fwd", grid=(nc // ch,),
        in_specs=[blk(XBC_W), pl.BlockSpec((8, XBC_W), lambda i: (jnp.maximum(i * (rows // 8) - 1, 0), 0)),
                  blk(SSM_HEADS), _full((CONV_K, XBC_W)), _full((1, XBC_W)), _full((1, SSM_HEADS)),
                  _full((1, SSM_HEADS)), _full((1, SSM_W)), _full((3 * SSM_HEADS, SSM_W))],
        out_specs=[blk(SSM_W), pl.BlockSpec((ch, SSM_N, SSM_W), lambda i: (i, 0, 0)), blk(XBC_W)],
        out_shape=[jax.ShapeDtypeStruct((s, SSM_W), F32), jax.ShapeDtypeStruct((nc, SSM_N, SSM_W), F32),
                   jax.ShapeDtypeStruct((s, XBC_W), F32)],
        scratch_shapes=[pltpu.VMEM((SSM_N, SSM_W), F32), pltpu.VMEM((BLOCK, SSM_W), F32), pltpu.VMEM((BLOCK, SSM_W), F32)],
        compiler_params=_params(dimension_semantics=("arbitrary",)),
    )(xbc, xbc, dt_raw, conv_w, conv_b, dt_bias, a_log, dsk_x, e3t)


def _dsilu(z, sg):
    return sg * (1.0 + z * (1.0 - sg))


def _mid(x, tgt, o_att, za, ypre, zm, ga, gb, gate, ssm_nw, rows_all, eq, tm=256):
    s = x.shape[0]
    gw = SSM_W // SSM_G

    r_ap, r_sp = ATTN_W // N_DEV, SSM_W // N_DEV

    def body(x_ref, t_ref, o_ref, za_ref, yp_ref, zm_ref, ga_ref, gb_ref, gate_ref, nw_ref, rows_h, eq_ref,
             dout_ref, do_ref, delta_ref, dza_ref, dyp_ref, dzm_ref, dga_ref, dgb_ref,
             yag_ref, dya_ref, yn_ref, dyb_ref, mg_ref, dob_ref, gnw_ref, dgate_ref, loss_ref,
             wap_v, wsp_v, wout_v, sem):
        i = pl.program_id(0)

        @pl.when(i == 0)
        def _():
            cps = []
            for d in range(N_DEV):
                for j, (dst, r0, rn) in enumerate(((wap_v, 0, r_ap), (wsp_v, r_ap, r_sp), (wout_v, r_ap + r_sp, r_ap))):
                    cps.append(pltpu.make_async_copy(rows_h.at[d, r0:r0 + rn, :], dst.at[d * rn:(d + 1) * rn, :], sem.at[j]))
            for cp in cps:
                cp.start()
            gnw_ref[...] = jnp.zeros_like(gnw_ref)
            dgate_ref[...] = jnp.zeros_like(dgate_ref)
            loss_ref[...] = jnp.zeros_like(loss_ref)
            for cp in cps:
                cp.wait()

        gate = gate_ref[...]
        nw = nw_ref[...]
        o_att = o_ref[...]
        z_a = za_ref[...].astype(F32)
        s_a = _sig(z_a)
        silu_a = z_a * s_a
        yag = _bf(o_att * silu_a)
        yag_ref[...] = yag
        ypre = yp_ref[...]
        z_m = zm_ref[...].astype(F32)
        s_m = _sig(z_m)
        silu_m = z_m * s_m
        yg = ypre * silu_m
        rinv = jnp.concatenate(
            [jnp.broadcast_to(lax.rsqrt(jnp.mean(yg[:, g * gw:(g + 1) * gw] ** 2, axis=-1, keepdims=True) + EPS), (tm, gw))
             for g in range(SSM_G)], axis=1)
        ynr = yg * rinv
        yn = _bf(ynr * nw)
        yn_ref[...] = yn
        y_a = _dot(yag, wap_v[...])
        y_b = _dot(yn, wsp_v[...])
        g_a = _sig(ga_ref[...].astype(F32))
        g_b = _sig(gb_ref[...].astype(F32))
        merged = _bf(g_a * y_a + g_b * y_b)
        mg_ref[...] = merged
        o = _dot(merged, wout_v[...])
        diff = x_ref[...] + gate * o - t_ref[...]
        loss_ref[...] += (0.5 / D_MODEL) * jnp.sum(diff * diff, axis=(0, 1), keepdims=True)
        dout = diff * (1.0 / D_MODEL)
        dout_ref[...] = dout
        dgate_ref[...] += jnp.sum(dout * o, axis=0, keepdims=True)
        d_o = _bf(dout * gate)
        dob_ref[...] = d_o
        dmerged = _dot_nt(d_o, wout_v[...])
        dy_af = dmerged * g_a
        dy_bf = dmerged * g_b
        dy_a = _bf(dy_af)
        dy_b = _bf(dy_bf)
        dya_ref[...] = dy_a
        dyb_ref[...] = dy_b
        dyag = _dot_nt(dy_a, wap_v[...])
        dyn = _dot_nt(dy_b, wsp_v[...])
        dga_ref[...] = _bf(dy_af * y_a * (1.0 - g_a))
        dgb_ref[...] = _bf(dy_bf * y_b * (1.0 - g_b))
        d_att = dyag * silu_a
        do_ref[...] = _bf(d_att)
        delta_ref[...] = _group_sum(d_att * o_att, eq_ref[...])
        dza_ref[...] = _bf(dyag * o_att * _dsilu(z_a, s_a))
        gnw_ref[...] += jnp.sum(dyn * ynr, axis=0, keepdims=True)
        dynw = dyn * nw
        corr = jnp.concatenate(
            [jnp.broadcast_to(jnp.mean((dynw * ynr)[:, g * gw:(g + 1) * gw], axis=-1, keepdims=True), (tm, gw))
             for g in range(SSM_G)], axis=1)
        dyg = rinv * (dynw - ynr * corr)
        dyp_ref[...] = _bf(dyg * silu_m)
        dzm_ref[...] = _bf(dyg * ypre * _dsilu(z_m, s_m))

    r1, r2 = _rows(tm, D_MODEL), _rows(tm, SSM_W)
    sd = jax.ShapeDtypeStruct
    return pl.pallas_call(
        body, name="mid", grid=(s // tm,),
        in_specs=[r1, r1, r1, r1, r2, r2, r1, r1, _full((1, D_MODEL)), _full((1, SSM_W)), ANY, _full(eq.shape)],
        out_specs=[r1, r1, _rows(tm, ATTN_HEADS), r1, r2, r2, r1, r1, r1, r1, r2, r1, r1, r1,
                   _full((1, SSM_W)), _full((1, D_MODEL)), _full((1, 1))],
        out_shape=[sd((s, D_MODEL), F32), sd((s, ATTN_W), BF), sd((s, ATTN_HEADS), F32), sd((s, ATTN_W), BF),
                   sd((s, SSM_W), BF), sd((s, SSM_W), BF), sd((s, D_MODEL), BF), sd((s, D_MODEL), BF),
                   sd((s, ATTN_W), BF), sd((s, D_MODEL), BF), sd((s, SSM_W), BF), sd((s, D_MODEL), BF),
                   sd((s, D_MODEL), BF), sd((s, D_MODEL), BF),
                   sd((1, SSM_W), F32), sd((1, D_MODEL), F32), sd((1, 1), F32)],
        scratch_shapes=[pltpu.VMEM((ATTN_W, D_MODEL), BF), pltpu.VMEM((SSM_W, D_MODEL), BF), pltpu.VMEM((D_MODEL, D_MODEL), BF),
                        pltpu.SemaphoreType.DMA((3,))],
        compiler_params=_params(dimension_semantics=("arbitrary",)),
    )(x, tgt, o_att, za, ypre, zm, ga, gb, gate, ssm_nw, rows_all, eq)


def _attn_bwd(q, k, v, bias, sinks, consts, delta, lse, d_o):
    s = q.shape[0]
    nb = s // BLOCK
    folds = (_fold(ATTN_W, HEAD_DIM), _fold(KV_W, HEAD_DIM))

    def body(q_ref, kp_ref, kc_ref, vp_ref, vc_ref, b_ref, skv_ref, qw_ref, kw_ref, eq_ref, eq3_ref, ek_ref, ek3_ref,
             fq_ref, fk_ref, delta_ref, lse_ref, do_ref,
             dq_ref, dk_ref, dv_ref, dss_ref, gqw_ref, gkw_ref, gsk_ref, ckn, cv, dqn_s, dkn_s, dv_s, gq_x, gk_x):
        i = pl.program_id(0)
        kw, ek, ek3 = kw_ref[...], ek_ref[...], ek3_ref[...]

        @pl.when(i == 0)
        def _():
            for ref in (ckn, cv, dss_ref, gq_x, gk_x, gsk_ref):
                ref[...] = jnp.zeros_like(ref)

        @pl.when(i < nb)
        def _():
            qw, eq, eq3 = qw_ref[...], eq_ref[...], eq3_ref[...]
            qf = q_ref[...]
            qnf, rq_x = _heads_norm(qf, qw, eq, eq3)
            qn = _bf(qnf * (HEAD_DIM ** -0.5))
            kf = jnp.concatenate([kp_ref[...], kc_ref[...]], axis=0)
            knf, rk_x = _heads_norm(kf, kw, ek, ek3)
            kn = _bf(knf)
            vv = _bf(jnp.concatenate([vp_ref[...], vc_ref[...]], axis=0))
            d_ob = do_ref[...]
            lse_all = lse_ref[...]
            delta = delta_ref[...]
            gsk_ref[...] += jnp.sum(-jnp.exp(skv_ref[...] - lse_all) * delta, axis=0, keepdims=True)
            kss = [slice(hk * HEAD_DIM, (hk + 1) * HEAD_DIM) for hk in range(KV_HEADS)]
            qgs = [_stack_heads(qn, hk) for hk in range(KV_HEADS)]
            d_ogs = [_stack_heads(d_ob, hk) for hk in range(KV_HEADS)]
            scs = [_dot_nt(qgs[hk], kn[:, kss[hk]]) + b_ref[0, hk * GRP:(hk + 1) * GRP].reshape(GRP * BLOCK, 2 * BLOCK)
                   for hk in range(KV_HEADS)]
            dps = [_dot_nt(d_ogs[hk], vv[:, kss[hk]]) for hk in range(KV_HEADS)]
            ps = [jnp.exp(scs[hk] - _stack_cols(lse_all, hk)) for hk in range(KV_HEADS)]
            dss = [ps[hk] * (dps[hk] - _stack_cols(delta, hk)) for hk in range(KV_HEADS)]
            pbs = [_bf(p) for p in ps]
            dsbs = [_bf(ds) for ds in dss]
            for hk in range(KV_HEADS):
                dss_ref[hk * GRP:(hk + 1) * GRP] += dss[hk].reshape(GRP, BLOCK, 2 * BLOCK)
            for hk in range(KV_HEADS):
                dv_s[:, kss[hk]] = _dot_tn(pbs[hk], d_ogs[hk])
                dkn_s[:, kss[hk]] = _dot_tn(dsbs[hk], qgs[hk])
            dqns = [_dot(dsbs[hk], kn[:, kss[hk]]) * (HEAD_DIM ** -0.5) for hk in range(KV_HEADS)]
            for hk in range(KV_HEADS):
                for g in range(GRP):
                    h = hk * GRP + g
                    dqn_s[:, h * HEAD_DIM:(h + 1) * HEAD_DIM] = dqns[hk][g * BLOCK:(g + 1) * BLOCK]
            dq, gq = _heads_norm_bwd(qf, rq_x, qw, dqn_s[...], eq, eq3)
            dq_ref[...] = _bf(dq)
            gq_x[...] += gq
            dk, gk = _heads_norm_bwd(kf[:BLOCK], rk_x[:BLOCK], kw, ckn[...] + dkn_s[0:BLOCK, :], ek, ek3)
            dk_ref[...] = _bf(dk)
            gk_x[...] += gk
            dv_ref[...] = _bf(cv[...] + dv_s[0:BLOCK, :])
            ckn[...] = dkn_s[BLOCK:2 * BLOCK, :]
            cv[...] = dv_s[BLOCK:2 * BLOCK, :]

        @pl.when(i == nb)
        def _():
            kc = kc_ref[...]
            dk, gk = _heads_norm_bwd(kc, _heads_norm(kc, kw, ek, ek3)[1], kw, ckn[...], ek, ek3)
            dk_ref[...] = _bf(dk)
            dv_ref[...] = _bf(cv[...])
            gqw_ref[...] = _group_sum(jnp.broadcast_to(gq_x[...], (8, ATTN_W)), fq_ref[...])[0:1]
            gkw_ref[...] = _group_sum(jnp.broadcast_to(gk_x[...] + gk, (8, KV_W)), fk_ref[...])[0:1]

    last = nb - 1
    cur = lambda w: pl.BlockSpec((BLOCK, w), lambda i: (jnp.minimum(i, last), 0))
    prev = lambda w: pl.BlockSpec((BLOCK, w), lambda i: (jnp.maximum(jnp.minimum(i, last) - 1, 0), 0))
    late = lambda w: pl.BlockSpec((BLOCK, w), lambda i: (jnp.maximum(i - 1, 0), 0))
    sd = jax.ShapeDtypeStruct
    return pl.pallas_call(
        body, name="attn_bwd", grid=(nb + 1,),
        in_specs=[cur(ATTN_W), prev(KV_W), cur(KV_W), prev(KV_W), cur(KV_W),
                  pl.BlockSpec((1, ATTN_HEADS, BLOCK, 2 * BLOCK), lambda i: (jnp.minimum(i, 1), 0, 0, 0)),
                  _full((1, ATTN_HEADS))]
                 + [_full(c.shape) for c in consts + folds] + [cur(ATTN_HEADS), cur(ATTN_HEADS), cur(ATTN_W)],
        out_specs=[cur(ATTN_W), late(KV_W), late(KV_W),
                   pl.BlockSpec((ATTN_HEADS, BLOCK, 2 * BLOCK), lambda i: (0, 0, 0)),
                   _full((1, HEAD_DIM)), _full((1, HEAD_DIM)), _full((1, ATTN_HEADS))],
        out_shape=[sd((s, ATTN_W), BF), sd((s, KV_W), BF), sd((s, KV_W), BF),
                   sd((ATTN_HEADS, BLOCK, 2 * BLOCK), F32), sd((1, HEAD_DIM), F32), sd((1, HEAD_DIM), F32),
                   sd((1, ATTN_HEADS), F32)],
        scratch_shapes=[pltpu.VMEM((BLOCK, KV_W), F32), pltpu.VMEM((BLOCK, KV_W), F32),
                        pltpu.VMEM((BLOCK, ATTN_W), F32), pltpu.VMEM((2 * BLOCK, KV_W), F32),
                        pltpu.VMEM((2 * BLOCK, KV_W), F32), pltpu.VMEM((1, ATTN_W), F32), pltpu.VMEM((1, KV_W), F32)],
        compiler_params=_params(dimension_semantics=("arbitrary",)),
    )(q, k, k, v, v, bias, sinks, *consts, *folds, delta, lse, d_o)


def _ssd_bwd(xbc, conv_all, dt_raw, conv_w, dt_bias, a_log, dsk_x, e_mat, e3t, hprev_all, dy_all):
    s = xbc.shape[0]
    nc = s // BLOCK
    ch = 1
    rows = ch * BLOCK
    nsteps = nc // ch
    gw = SSM_R * SSM_P
    b0, c0 = SSM_W, SSM_W + SSM_G * SSM_N

    def body(x_ref, conv_ref, dtr_ref, cw_ref, dtb_ref, alog_ref, dsk_ref, e_ref, e3_ref, hp_ref, dy_ref,
             dx_ref, ddt_ref, gcw_ref, gcb_ref, gdtb_ref, galog_ref, gdsk_ref,
             dh, nhead, gdskx, dxdt_s, dbc_s, dxd_s):
        def chunk_bwd(j):
            rs = slice(j * BLOCK, (j + 1) * BLOCK)
            conv = conv_ref[rs, :]
            sg, xact, u, dt, a, trilb, acum, dt_x, acum_x = _ssd_common(conv, dtr_ref[rs, :], dtb_ref, alog_ref, e3_ref)
            xs = xact[:, :SSM_W]
            acum_t = acum.T
            ea_x = jnp.exp(acum_x)
            last_x = acum_x[BLOCK - 1:BLOCK, :]
            dte_x = jnp.exp(last_x - acum_x)
            cd_x = jnp.exp(last_x)
            xdt = xs * dt_x
            xw = xdt * dte_x
            hprev = hp_ref[j]
            dhn = dh[...]
            dy = dy_ref[rs, :].astype(F32)
            gdskx[...] += jnp.sum(dy * xs, axis=0, keepdims=True)
            dyea = dy * ea_x
            lane = lax.broadcasted_iota(jnp.int32, (BLOCK, SSM_HEADS), 1)
            dacum = jnp.zeros((BLOCK, SSM_HEADS), F32)
            dacc_x, dlast_x = [], []
            sls = [slice(g * gw, (g + 1) * gw) for g in range(SSM_G)]
            bgs = [_bf(xact[:, b0 + g * SSM_N:b0 + (g + 1) * SSM_N]) for g in range(SSM_G)]
            cgs = [_bf(xact[:, c0 + g * SSM_N:c0 + (g + 1) * SSM_N]) for g in range(SSM_G)]
            hpgs = [_bf(hprev[:, sl]) for sl in sls]
            dhgs = [_bf(dhn[:, sl]) for sl in sls]
            dyeags = [_bf(dyea[:, sl]) for sl in sls]
            xwgs = [_bf(xw[:, sl]) for sl in sls]
            xdt_b, dy_b = _bf(xdt), _bf(dy)
            low_half = lax.broadcasted_iota(jnp.int32, (BLOCK, 2 * SSM_P), 1) < SSM_P
            cbs = [_dot_nt(cgs[g], bgs[g]) for g in range(SSM_G)]
            gmats = [_dot(cgs[g], hpgs[g]) for g in range(SSM_G)]
            dxws = [_dot(bgs[g], dhgs[g]) for g in range(SSM_G)]
            dcgs = [_dot_nt(dyeags[g], hpgs[g]) for g in range(SSM_G)]
            dbgs = [_dot_nt(xwgs[g], dhgs[g]) for g in range(SSM_G)]
            for g in range(SSM_G):
                sl = sls[g]
                dh[:, sl] = dhn[:, sl] * cd_x[:, sl] + _dot_tn(cgs[g], dyeags[g])
                dxdt_s[:, sl] = dxws[g] * dte_x[:, sl]
                dacc_x.append(dy[:, sl] * gmats[g] * ea_x[:, sl] - dxws[g] * xw[:, sl])
                dlast_x.append(jnp.sum(dxws[g] * xw[:, sl], axis=0, keepdims=True)
                               + jnp.sum(dhn[:, sl] * hprev[:, sl], axis=0, keepdims=True) * cd_x[:, sl])
            for g in range(SSM_G):
                bg, cg, cb, dbg, dcg = bgs[g], cgs[g], cbs[g], dbgs[g], dcgs[g]
                hss = [slice((g * SSM_R + r) * SSM_P, (g * SSM_R + r + 1) * SSM_P) for r in range(SSM_R)]
                lms = [jnp.exp(jnp.where(trilb, acum[:, g * SSM_R + r:g * SSM_R + r + 1]
                                         - acum_t[g * SSM_R + r:g * SSM_R + r + 1, :], -1e30)) for r in range(SSM_R)]
                mms = [cb * lm for lm in lms]
                mmbs = [_bf(mm) for mm in mms]
                dms = []
                for r in range(0, SSM_R, 2):
                    pair = slice(hss[r].start, hss[r + 1].stop)
                    xp, dyp = xdt_b[:, pair], dy_b[:, pair]
                    dmp = _dot_nt(dyp, jnp.concatenate([jnp.where(low_half, xp, 0), jnp.where(low_half, 0, xp)], axis=0))
                    dms += [dmp[:, :BLOCK], dmp[:, BLOCK:]]
                    dxd_s[:, pair] = _dot_tn(jnp.concatenate([mmbs[r], mmbs[r + 1]], axis=0),
                                             jnp.concatenate([jnp.where(low_half, dyp, 0), jnp.where(low_half, 0, dyp)], axis=0))
                dcb = sum(dms[r] * lms[r] for r in range(SSM_R))
                wms = [dms[r] * mms[r] for r in range(SSM_R)]
                antis = [wm - wm.T for wm in wms]
                for r in range(SSM_R):
                    dacum = dacum + _group_sum(antis[r], (lane == g * SSM_R + r).astype(BF))
                dcbb = _bf(dcb)
                dbc_s[:, g * SSM_N:(g + 1) * SSM_N] = dbg + _dot_tn(dcbb, cg)
                dbc_s[:, SSM_G * SSM_N + g * SSM_N:SSM_G * SSM_N + (g + 1) * SSM_N] = dcg + _dot(dcbb, bg)
            dxdt = dxdt_s[...] + dxd_s[...]
            dxs = dy * dsk_ref[...] + dxdt * dt_x
            red = _group_sum(jnp.concatenate(
                [dxdt * xs, jnp.concatenate(dacc_x, axis=1),
                 jnp.broadcast_to(jnp.concatenate(dlast_x, axis=1), (8, SSM_W))], axis=0), e_ref[...])
            row = lax.broadcasted_iota(jnp.int32, (BLOCK, SSM_HEADS), 0)
            dacum = dacum + red[BLOCK:2 * BLOCK] + jnp.where(row == BLOCK - 1, red[2 * BLOCK:2 * BLOCK + 1], 0.0)
            ddta = _exact_left(_triu().astype(BF), dacum)
            ddt = red[:BLOCK] + ddta * a
            galog_ref[...] += jnp.sum(ddta * dt, axis=0, keepdims=True) * a
            du = ddt * _sig(u)
            ddt_ref[rs, :] = _bf(du)
            gdtb_ref[...] += jnp.sum(du, axis=0, keepdims=True)
            dconv = jnp.concatenate([dxs, dbc_s[...]], axis=1) * _dsilu(conv, sg)
            gcb_ref[...] += jnp.sum(dconv, axis=0, keepdims=True)
            ext2 = jnp.concatenate([dconv, nhead[...]], axis=0)
            ahead = [pltpu.roll(ext2, BLOCK + 8 - (CONV_K - 1 - j), axis=0)[0:BLOCK] if j < CONV_K - 1 else dconv
                     for j in range(CONV_K)]
            dx_ref[rs, :] = _bf(sum(ahead[j] * cw_ref[j:j + 1, :] for j in range(CONV_K)))
            xraw = x_ref[rs, :]
            gcw_ref[...] += jnp.concatenate([jnp.sum(ahead[j] * xraw, axis=0, keepdims=True) for j in range(CONV_K)], axis=0)
            nhead[...] = dconv[0:8]

        i = pl.program_id(0)

        @pl.when(i == 0)
        def _():
            for ref in (dh, nhead, gdskx, gcw_ref, gcb_ref, gdtb_ref, galog_ref, gdsk_ref):
                ref[...] = jnp.zeros_like(ref)

        for j in reversed(range(ch)):
            chunk_bwd(j)

        @pl.when(i == nsteps - 1)
        def _():
            gdsk_ref[...] = _group_sum(jnp.broadcast_to(gdskx[...], (8, SSM_W)), e_ref[...])[0:1]

    chunk = lambda w: pl.BlockSpec((rows, w), lambda i: (nsteps - 1 - i, 0))
    sd = jax.ShapeDtypeStruct
    return pl.pallas_call(
        body, name="ssd_bwd", grid=(nsteps,),
        in_specs=[chunk(XBC_W), chunk(XBC_W),
                  chunk(SSM_HEADS), _full((CONV_K, XBC_W)), _full((1, SSM_HEADS)),
                  _full((1, SSM_HEADS)), _full((1, SSM_W)), _full((SSM_W, SSM_HEADS)), _full((3 * SSM_HEADS, SSM_W)),
                  pl.BlockSpec((ch, SSM_N, SSM_W), lambda i: (nsteps - 1 - i, 0, 0)), chunk(SSM_W)],
        out_specs=[chunk(XBC_W), chunk(SSM_HEADS), _full((CONV_K, XBC_W)), _full((1, XBC_W)),
                   _full((1, SSM_HEADS)), _full((1, SSM_HEADS)), _full((1, SSM_HEADS))],
        out_shape=[sd((s, XBC_W), BF), sd((s, SSM_HEADS), BF), sd((CONV_K, XBC_W), F32), sd((1, XBC_W), F32),
                   sd((1, SSM_HEADS), F32), sd((1, SSM_HEADS), F32), sd((1, SSM_HEADS), F32)],
        scratch_shapes=[pltpu.VMEM((SSM_N, SSM_W), F32), pltpu.VMEM((8, XBC_W), F32),
                        pltpu.VMEM((1, SSM_W), F32), pltpu.VMEM((BLOCK, SSM_W), F32),
                        pltpu.VMEM((BLOCK, 2 * SSM_G * SSM_N), F32), pltpu.VMEM((BLOCK, SSM_W), F32)],
        compiler_params=_params(dimension_semantics=("arbitrary",)),
    )(xbc, conv_all, dt_raw, conv_w, dt_bias, a_log, dsk_x, e_mat, e3t, hprev_all, dy_all)


def _dh(x, dout, norm_w, scale, dsegs, w_t, tm=256):
    s = x.shape[0]

    def body(x_ref, dout_ref, nw_ref, sc_ref, *rest):
        d_refs, w_hbm = rest[:9], rest[9]
        gx_ref, dshift_ref, dscale_ref, gnw_ref = rest[10:14]
        w_vm, sem = rest[14], rest[15]
        first = pl.program_id(0) == 0
        cps = [pltpu.make_async_copy(w_hbm.at[SEG_OFF[j]:SEG_OFF[j + 1], :], w_vm.at[SEG_OFF[j]:SEG_OFF[j + 1], :], sem.at[j])
               for j in range(9)]

        def tile(waiting):
            dh = None
            for j in range(9):
                if waiting:
                    cps[j].wait()
                part = _dot(d_refs[j][...], w_vm[SEG_OFF[j]:SEG_OFF[j + 1], :])
                dh = part if dh is None else dh + part
            xv = x_ref[...]
            r = lax.rsqrt(jnp.mean(xv * xv, axis=-1, keepdims=True) + EPS)
            xn = xv * r
            nw = nw_ref[...]
            sc1 = 1.0 + sc_ref[...]
            dshift_ref[...] += jnp.sum(dh, axis=0, keepdims=True)
            dhxn = jnp.sum(dh * xn, axis=0, keepdims=True)
            dscale_ref[...] += dhxn * nw
            gnw_ref[...] += dhxn * sc1
            dxn = dh * (nw * sc1)
            gx_ref[...] = dout_ref[...] + r * (dxn - xn * jnp.mean(xn * dxn, axis=-1, keepdims=True))

        @pl.when(first)
        def _():
            for cp in cps:
                cp.start()
            for ref in (dshift_ref, dscale_ref, gnw_ref):
                ref[...] = jnp.zeros_like(ref)
            tile(True)

        @pl.when(jnp.logical_not(first))
        def _():
            tile(False)

    vec = _full((1, D_MODEL))
    sd = jax.ShapeDtypeStruct
    return pl.pallas_call(
        body, name="dh", grid=(s // tm,),
        in_specs=[_rows(tm, D_MODEL), _rows(tm, D_MODEL), vec, vec] + [_rows(tm, w) for w in SEG_W] + [ANY],
        out_specs=[_rows(tm, D_MODEL), vec, vec, vec],
        out_shape=[sd((s, D_MODEL), F32), sd((1, D_MODEL), F32), sd((1, D_MODEL), F32), sd((1, D_MODEL), F32)],
        scratch_shapes=[pltpu.VMEM((IN_W, D_MODEL), BF), pltpu.SemaphoreType.DMA((9,))],
        compiler_params=_params(dimension_semantics=("arbitrary",)),
    )(x, dout, norm_w, scale, *dsegs, w_t)


def _gw_seg(h, dseg, name, tm=1024):
    s, w = dseg.shape
    tn = min(w, 1024)
    tm = min(tm, s)
    nm = s // tm

    def body(h_ref, d_ref, o_ref, acc):
        m = pl.program_id(1)

        @pl.when(m == 0)
        def _():
            acc[...] = jnp.zeros_like(acc)

        acc[...] += _dot_tn(d_ref[...], h_ref[...])

        @pl.when(m == nm - 1)
        def _():
            o_ref[...] = _bf(acc[...])

    return pl.pallas_call(
        body, name=name, grid=(w // tn, nm),
        in_specs=[pl.BlockSpec((tm, D_MODEL), lambda n, m: (m, 0)), pl.BlockSpec((tm, tn), lambda n, m: (m, n))],
        out_specs=pl.BlockSpec((tn, D_MODEL), lambda n, m: (n, 0)),
        out_shape=jax.ShapeDtypeStruct((w, D_MODEL), BF),
        scratch_shapes=[pltpu.VMEM((tn, D_MODEL), F32)],
        compiler_params=_params(dimension_semantics=("arbitrary", "arbitrary")),
    )(h, dseg)


def _gw_in(h, dsegs):
    return [_gw_seg(h, d, "gw_in_%d" % j) for j, d in enumerate(dsegs)]


def _local_step(x, tgt, shift, scale, gate, w_t, rows_fn, norm_w, qnw, knw, rel_bias, sinks,
                conv_w, conv_b, dt_bias, a_log, d_skip, ssm_nw, after_mid=None, after_gw=None):
    oh_t = _bucket_onehot_t()
    bias = _masked_bias(_bias_dense(rel_bias.T, oh_t).reshape(ATTN_HEADS, BLOCK, 2 * BLOCK))
    *segs, h = _inproj(x, norm_w, scale, shift, w_t)
    q, k, v, za, zm, xbc, dtr, ga, gb = segs
    consts = _attn_consts(qnw, knw)
    o_att, lse = _attn_fwd(q, k, v, bias, sinks, consts)
    e_mat, e3t = _membership(SSM_W, SSM_P, SSM_HEADS)
    dsk_x = jnp.repeat(d_skip, SSM_P, axis=1)
    ypre, hprev, conv = _ssd_fwd(xbc, dtr, conv_w, conv_b, dt_bias, a_log, dsk_x, e3t)
    (dout, d_o, delta, dza, dyp, dzm, dga, dgb, yag, dy_a, yn, dy_b, merged, dob, g_ssm_nw, dgate, loss) = _mid(
        x, tgt, o_att, za, ypre, zm, ga, gb, gate, ssm_nw, rows_fn(ypre), consts[2])
    g_wap = _gw_seg(dy_a, yag, "gw_attn_proj")
    g_wsp = _gw_seg(dy_b, yn, "gw_ssm_proj")
    g_wout = _gw_seg(dob, merged, "gw_out")
    zero = after_mid(g_wap, g_wsp, g_wout) if after_mid is not None else 0.0
    dq, dk, dv, dss, g_qnw, g_knw, g_sinks = _attn_bwd(q, k, v, bias, sinks + zero, consts, delta, lse, d_o)
    g_rel = _bias_grad(dss.reshape(ATTN_HEADS, BLOCK * 2 * BLOCK), oh_t).T
    dxbc, ddt, g_cw, g_cb, g_dtb, g_alog, g_dsk = _ssd_bwd(
        xbc, conv, dtr, conv_w, dt_bias, a_log, dsk_x, e_mat, e3t, hprev, dyp)
    dsegs = (dq, dk, dv, dza, dzm, dxbc, ddt, dga, dgb)
    g_ws = _gw_in(h, dsegs)
    zero = after_gw(g_ws) if after_gw is not None else 0.0
    gx, dshift, dscale, g_nw = _dh(x, dout, norm_w + zero, scale, dsegs, w_t)
    return dict(loss=loss, grad_x=gx, dmod=jnp.concatenate([dshift, dscale, dgate], axis=1), g_ws=g_ws,
                g_wap=g_wap, g_wsp=g_wsp, g_wout=g_wout, g_norm_w=g_nw, g_qnw=g_qnw, g_knw=g_knw, g_rel=g_rel,
                g_sinks=g_sinks, g_conv_w=g_cw, g_conv_b=g_cb, g_dt_bias=g_dtb, g_a_log=g_alog, g_d_skip=g_dsk,
                g_ssm_nw=g_ssm_nw)


def _me():
    return lax.axis_index("x"), lax.axis_index("y"), lax.axis_index("c")


def _flip(v, bit):
    return 1 - v if bit else v


def _ag_direct(v, name):
    def body(v_ref, out_ref, send_sems, recv_sems, local_sem):
        x, y, c = _me()
        me = 4 * x + 2 * y + c
        mine = pltpu.make_async_copy(v_ref, out_ref.at[me], local_sem)
        mine.start()
        peers = [(_flip(x, k >> 2 & 1), _flip(y, k >> 1 & 1), _flip(c, k & 1)) for k in range(1, N_DEV)]
        sends = [pltpu.make_async_remote_copy(
            src_ref=v_ref, dst_ref=out_ref.at[me], send_sem=send_sems.at[j], recv_sem=recv_sems.at[j],
            device_id=p, device_id_type=MESH) for j, p in enumerate(peers)]
        for cp in sends:
            cp.start()
        for j, (px, py, pc) in enumerate(peers):
            pltpu.make_async_remote_copy(
                src_ref=v_ref, dst_ref=out_ref.at[4 * px + 2 * py + pc], send_sem=send_sems.at[j],
                recv_sem=recv_sems.at[j], device_id=(px, py, pc), device_id_type=MESH).wait_recv()
        for cp in sends:
            cp.wait_send()
        mine.wait()

    vm = pl.BlockSpec(memory_space=pltpu.VMEM)
    return pl.pallas_call(
        body, name=name, out_shape=jax.ShapeDtypeStruct((N_DEV,) + v.shape, v.dtype),
        in_specs=[vm], out_specs=vm,
        scratch_shapes=[pltpu.SemaphoreType.DMA((N_DEV - 1,)), pltpu.SemaphoreType.DMA((N_DEV - 1,)),
                        pltpu.SemaphoreType.DMA],
        compiler_params=_params(),
    )(v)


def _ag_two_level(v, name):
    def body(v_ref, out_ref, token, send_sems, recv_sems, local_sem):
        token[...] = jnp.zeros_like(token)
        x, y, c = _me()
        me, sibling = (x, y, c), (x, y, 1 - c)
        chips = [(1 - x, y), (x, 1 - y), (1 - x, 1 - y)]

        def slot(px, py, pc):
            return out_ref.at[4 * px + 2 * py + pc]

        def copy(k, block, to, src=None):
            return pltpu.make_async_remote_copy(
                src_ref=slot(*block) if src is None else src, dst_ref=slot(*block),
                send_sem=send_sems.at[k], recv_sem=recv_sems.at[k], device_id=to, device_id_type=MESH)

        mine = pltpu.make_async_copy(v_ref, slot(*me), local_sem)
        mine.start()
        first = [copy(0, me, sibling, src=v_ref)]
        first += [copy(1 + j, me, (*chip, c), src=v_ref) for j, chip in enumerate(chips)]
        for cp in first:
            cp.start()
        passed = [copy(4 + j, (*chip, c), sibling) for j, chip in enumerate(chips)]
        for j, chip in enumerate(chips):
            copy(1 + j, (*chip, c), me).wait_recv()
            passed[j].start()
        copy(0, sibling, me).wait_recv()
        for j, chip in enumerate(chips):
            copy(4 + j, (*chip, 1 - c), me).wait_recv()
        for cp in first + passed:
            cp.wait_send()
        mine.wait()

    out, token = pl.pallas_call(
        body, name=name,
        out_shape=(jax.ShapeDtypeStruct((N_DEV,) + v.shape, v.dtype), jax.ShapeDtypeStruct((8, 128), v.dtype)),
        in_specs=[ANY], out_specs=(ANY, pl.BlockSpec(memory_space=pltpu.VMEM)),
        scratch_shapes=[pltpu.SemaphoreType.DMA((7,)), pltpu.SemaphoreType.DMA((7,)), pltpu.SemaphoreType.DMA],
        compiler_params=_params(),
    )(v)
    return out, token[0:1, 0:1]


def _rs_sibling(g, name):
    def body(g_ref, out_ref, send_sems, recv_sems):
        x, y, c = _me()
        cps = [pltpu.make_async_remote_copy(
            src_ref=g_ref.at[2 * ch + 1 - c], dst_ref=out_ref.at[ch], send_sem=send_sems.at[ch],
            recv_sem=recv_sems.at[ch], device_id=(x, y, 1 - c), device_id_type=MESH) for ch in range(4)]
        for cp in cps:
            cp.start()
        for cp in cps:
            cp.wait()

    return pl.pallas_call(
        body, name=name, out_shape=jax.ShapeDtypeStruct((4,) + g.shape[1:], g.dtype),
        in_specs=[ANY], out_specs=ANY,
        scratch_shapes=[pltpu.SemaphoreType.DMA((4,)), pltpu.SemaphoreType.DMA((4,))],
        compiler_params=_params(),
    )(g)


def _add_sibling(g, got, name):
    _, r, n = g.shape
    tr = min(r, 256)

    def body(c_ref, a_ref, b_ref, o_ref):
        o_ref[...] = a_ref[...] + b_ref[...]

    grid_spec = pltpu.PrefetchScalarGridSpec(
        num_scalar_prefetch=1, grid=(4, r // tr),
        in_specs=[pl.BlockSpec((1, tr, n), lambda ch, i, c_ref: (2 * ch + c_ref[0], i, 0)),
                  pl.BlockSpec((1, tr, n), lambda ch, i, c_ref: (ch, i, 0))],
        out_specs=pl.BlockSpec((1, tr, n), lambda ch, i, c_ref: (ch, i, 0)))
    return pl.pallas_call(
        body, name=name, grid_spec=grid_spec, out_shape=jax.ShapeDtypeStruct((4, r, n), g.dtype),
        compiler_params=_params(dimension_semantics=("arbitrary", "arbitrary")),
    )(lax.axis_index("c").reshape(1).astype(jnp.int32), g, got)


def _rs_chips(p, name):
    def body(p_ref, out_ref, send_sems, recv_sems, local_sem):
        x, y, c = _me()
        my_chip = 2 * x + y
        mine = pltpu.make_async_copy(p_ref.at[my_chip], out_ref.at[my_chip], local_sem)
        mine.start()
        chips = [(1 - x, y), (x, 1 - y), (1 - x, 1 - y)]
        sends = [pltpu.make_async_remote_copy(
            src_ref=p_ref.at[2 * px + py], dst_ref=out_ref.at[my_chip], send_sem=send_sems.at[j],
            recv_sem=recv_sems.at[j], device_id=(px, py, c), device_id_type=MESH) for j, (px, py) in enumerate(chips)]
        for cp in sends:
            cp.start()
        for j, (px, py) in enumerate(chips):
            pltpu.make_async_remote_copy(
                src_ref=p_ref.at[my_chip], dst_ref=out_ref.at[2 * px + py], send_sem=send_sems.at[j],
                recv_sem=recv_sems.at[j], device_id=(px, py, c), device_id_type=MESH).wait_recv()
        for cp in sends:
            cp.wait_send()
        mine.wait()

    return pl.pallas_call(
        body, name=name, out_shape=jax.ShapeDtypeStruct(p.shape, p.dtype),
        in_specs=[ANY], out_specs=ANY,
        scratch_shapes=[pltpu.SemaphoreType.DMA((3,)), pltpu.SemaphoreType.DMA((3,)), pltpu.SemaphoreType.DMA],
        compiler_params=_params(),
    )(p)


HBM = pl.BlockSpec(memory_space=pltpu.HBM)
SEM = pl.BlockSpec(memory_space=pltpu.SEMAPHORE)
EFFECT = pltpu.SideEffectType.DATAFLOW_SIDE_EFFECTING


def _peers(x, y, c):
    return [(_flip(x, k >> 2 & 1), _flip(y, k >> 1 & 1), _flip(c, k & 1)) for k in range(1, N_DEV)]


def _exchange_start(src, land, gather, name):
    def body(src_ref, land_ref, send_sems, recv_sems, src_thru, land_thru, token):
        x, y, c = _me()
        me = 4 * x + 2 * y + c
        for j, (px, py, pc) in enumerate(_peers(x, y, c)):
            pltpu.make_async_remote_copy(
                src_ref=src_ref if gather else src_ref.at[4 * px + 2 * py + pc], dst_ref=land_ref.at[me],
                send_sem=send_sems.at[j], recv_sem=recv_sems.at[j], device_id=(px, py, pc), device_id_type=MESH).start()
        token[...] = jnp.zeros_like(token)

    sems = pltpu.SemaphoreType.DMA((N_DEV - 1,))
    out = pl.pallas_call(
        body, name=name,
        out_shape=(sems, sems, pltpu.HBM(src.shape, src.dtype), pltpu.HBM(land.shape, land.dtype),
                   jax.ShapeDtypeStruct((8, 128), F32)),
        in_specs=(HBM, HBM), out_specs=(SEM, SEM, HBM, HBM, pl.BlockSpec(memory_space=pltpu.VMEM)),
        input_output_aliases={0: 2, 1: 3},
        compiler_params=pltpu.CompilerParams(has_side_effects=EFFECT),
    )(pltpu.with_memory_space_constraint(src, pltpu.HBM), pltpu.with_memory_space_constraint(land, pltpu.HBM))
    return out[:4], out[4][0, 0]


def _exchange_wait(started, after, gather, name):
    send_sems, recv_sems, src_thru, land_thru = started

    def body(src_ref, land_ref, send_sems, recv_sems, after_ref, src_dead, got_ref):
        x, y, c = _me()
        for j, (px, py, pc) in enumerate(_peers(x, y, c)):
            pid = 4 * px + 2 * py + pc
            cp = pltpu.make_async_remote_copy(
                src_ref=src_ref if gather else src_ref.at[pid], dst_ref=land_ref.at[pid],
                send_sem=send_sems.at[j], recv_sem=recv_sems.at[j], device_id=(px, py, pc), device_id_type=MESH)
            cp.wait_send()
            cp.wait_recv()

    return pl.pallas_call(
        body, name=name,
        out_shape=(pltpu.HBM(src_thru.shape, src_thru.dtype), pltpu.HBM(land_thru.shape, land_thru.dtype)),
        in_specs=(HBM, HBM, SEM, SEM, ANY), out_specs=(HBM, HBM), input_output_aliases={0: 0, 1: 1},
        compiler_params=pltpu.CompilerParams(has_side_effects=EFFECT),
    )(src_thru, land_thru, send_sems, recv_sems, after)[1]


def _reduce_scatter(g, name):
    got = _rs_sibling(g, name + "_sib")
    return _rs_chips(_add_sibling(g, got, name + "_add"), name + "_chips")


def _silu(a):
    return a * _sig(a)


def _mod_piece(c_all, w_ada, b_piece):
    def body(c_ref, w_ref, b_ref, o_ref):
        o_ref[...] = _dot(_bf(_silu(c_ref[...])), _bf(w_ref[...])) + b_ref[...]

    return pl.pallas_call(
        body, name="mod_piece", out_shape=jax.ShapeDtypeStruct((c_all.shape[0], w_ada.shape[1]), F32),
        compiler_params=_params(),
    )(c_all, w_ada, b_piece)


def _gw_ada(c_all, dmod_piece):
    def body(c_ref, d_ref, o_ref):
        o_ref[...] = _dot_tn(_bf(_silu(c_ref[...])), _bf(d_ref[...]))

    return pl.pallas_call(
        body, name="gw_ada", out_shape=jax.ShapeDtypeStruct((c_all.shape[1], dmod_piece.shape[1]), F32),
        compiler_params=_params(),
    )(c_all, dmod_piece)


def _adam(parts, w, m, v, name):
    k, r, n = parts.shape
    if r <= 256 or r % 256 == 0:
        tr, tn = min(r, 256), n
    else:
        tr, tn = r, 256
    assert r % tr == 0 and n % tn == 0

    def body(p_ref, w_ref, m_ref, v_ref, g_ref, d_ref, nm_ref, nv_ref):
        g = p_ref[0].astype(F32)
        for j in range(1, k):
            g = g + p_ref[j].astype(F32)
        g_ref[...] = g
        d_ref[...], nm_ref[...], nv_ref[...] = _adam_math(g, w_ref[...], m_ref[...], v_ref[...])

    blk = pl.BlockSpec((tr, tn), lambda i, j: (i, j))
    return pl.pallas_call(
        body, name=name, grid=(r // tr, n // tn),
        in_specs=[pl.BlockSpec((k, tr, tn), lambda i, j: (0, i, j)), blk, blk, blk],
        out_specs=[blk, blk, blk, blk],
        out_shape=[jax.ShapeDtypeStruct((r, n), F32)] * 4,
        compiler_params=_params(dimension_semantics=("arbitrary", "arbitrary")),
    )(parts, w, m, v)


def _adam_math(g, w, m, v):
    m_new = ADAM_B1 * m + (1.0 - ADAM_B1) * g
    v_new = ADAM_B2 * v + (1.0 - ADAM_B2) * jnp.square(g)
    m_hat = m_new / (1.0 - ADAM_B1 ** ADAM_STEP)
    v_hat = v_new / (1.0 - ADAM_B2 ** ADAM_STEP)
    return -ADAM_LR * (m_hat / (jnp.sqrt(v_hat) + ADAM_EPS) + ADAM_WD * w), m_new, v_new


_SMALL = (("b_ada", 3 * D_MODEL), ("norm_w", D_MODEL), ("q_norm_w", HEAD_DIM), ("k_norm_w", HEAD_DIM),
          ("rel_bias", REL_BUCKETS * ATTN_HEADS), ("sinks", ATTN_HEADS), ("conv_b", XBC_W), ("dt_bias", SSM_HEADS),
          ("a_log", SSM_HEADS), ("d_skip", SSM_HEADS), ("ssm_norm_w", SSM_W))
_SLOT = tuple(-(-n // 128) * 128 for _, n in _SMALL)
_SLOT_OFF = tuple(int(o) for o in np.cumsum((0,) + _SLOT))
_LOSS_OFF = _SLOT_OFF[-1]
_CW_OFF = _LOSS_OFF + 128
_PACK_N = _CW_OFF + CONV_K * XBC_W


def _pack_partials(small, loss, g_conv_w):
    parts = []
    for (name, n), slot in zip(_SMALL, _SLOT):
        parts.append(small[name].reshape(1, n))
        if slot > n:
            parts.append(jnp.zeros((1, slot - n), F32))
    parts += [loss.reshape(1, 1), jnp.zeros((1, 127), F32), g_conv_w.reshape(1, CONV_K * XBC_W)]
    return jnp.concatenate(parts, axis=1)


def _adam_small(pack_all, w, m, v):
    names = [name for name, _ in _SMALL]

    def body(p_ref, *rest):
        ins, outs = rest[:3 * len(names)], rest[3 * len(names):]

        def total(off, n):
            g = p_ref[0, :, off:off + n]
            for d in range(1, N_DEV):
                g = g + p_ref[d, :, off:off + n]
            return g

        for j, (name, n) in enumerate(_SMALL):
            g = total(_SLOT_OFF[j], n)
            delta, m_new, v_new = _adam_math(g, ins[3 * j][...], ins[3 * j + 1][...], ins[3 * j + 2][...])
            outs[4 * j][...] = g
            outs[4 * j + 1][...] = delta
            outs[4 * j + 2][...] = m_new
            outs[4 * j + 3][...] = v_new
        outs[-1][...] = total(_LOSS_OFF, 1)

    flat = []
    for name, n in _SMALL:
        flat += [w[name].reshape(1, n), m[name].reshape(1, n), v[name].reshape(1, n)]
    out_shape = [jax.ShapeDtypeStruct((1, n), F32) for _, n in _SMALL for _ in range(4)] + [jax.ShapeDtypeStruct((1, 1), F32)]
    out = pl.pallas_call(body, name="adam_small", out_shape=out_shape, compiler_params=_params())(pack_all, *flat)
    res = {name: [out[4 * j + t].reshape(w[name].shape) for t in range(4)] for j, name in enumerate(names)}
    return res, out[-1]


WEIGHTS = ("w_ada", "b_ada", "norm_w", "w_in", "q_norm_w", "k_norm_w", "rel_bias", "sinks", "conv_w", "conv_b",
           "dt_bias", "a_log", "d_skip", "ssm_norm_w", "w_attn_proj", "w_ssm_proj", "w_out")


def kernel(x, c, w_ada, b_ada, norm_w, w_in, q_norm_w, k_norm_w, rel_bias, sinks, conv_w, conv_b, dt_bias, a_log, d_skip, ssm_norm_w, w_attn_proj, w_ssm_proj, w_out, loss_target, m_w_ada, m_b_ada, m_norm_w, m_w_in, m_q_norm_w, m_k_norm_w, m_rel_bias, m_sinks, m_conv_w, m_conv_b, m_dt_bias, m_a_log, m_d_skip, m_ssm_norm_w, m_w_attn_proj, m_w_ssm_proj, m_w_out, v_w_ada, v_b_ada, v_norm_w, v_w_in, v_q_norm_w, v_k_norm_w, v_rel_bias, v_sinks, v_conv_w, v_conv_b, v_dt_bias, v_a_log, v_d_skip, v_ssm_norm_w, v_w_attn_proj, v_w_ssm_proj, v_w_out):
    w = dict(w_ada=w_ada, b_ada=b_ada, norm_w=norm_w, w_in=w_in, q_norm_w=q_norm_w, k_norm_w=k_norm_w,
             rel_bias=rel_bias, sinks=sinks, conv_w=conv_w, conv_b=conv_b, dt_bias=dt_bias, a_log=a_log,
             d_skip=d_skip, ssm_norm_w=ssm_norm_w, w_attn_proj=w_attn_proj, w_ssm_proj=w_ssm_proj, w_out=w_out)
    m = dict(w_ada=m_w_ada, b_ada=m_b_ada, norm_w=m_norm_w, w_in=m_w_in, q_norm_w=m_q_norm_w, k_norm_w=m_k_norm_w,
             rel_bias=m_rel_bias, sinks=m_sinks, conv_w=m_conv_w, conv_b=m_conv_b, dt_bias=m_dt_bias, a_log=m_a_log,
             d_skip=m_d_skip, ssm_norm_w=m_ssm_norm_w, w_attn_proj=m_w_attn_proj, w_ssm_proj=m_w_ssm_proj, w_out=m_w_out)
    v = dict(w_ada=v_w_ada, b_ada=v_b_ada, norm_w=v_norm_w, w_in=v_w_in, q_norm_w=v_q_norm_w, k_norm_w=v_k_norm_w,
             rel_bias=v_rel_bias, sinks=v_sinks, conv_w=v_conv_w, conv_b=v_conv_b, dt_bias=v_dt_bias, a_log=v_a_log,
             d_skip=v_d_skip, ssm_norm_w=v_ssm_norm_w, w_attn_proj=v_w_attn_proj, w_ssm_proj=v_w_ssm_proj, w_out=v_w_out)
    me = 4 * lax.axis_index("x") + 2 * lax.axis_index("y") + lax.axis_index("c")
    ada_n = w_ada.shape[2]
    in_n = w_in.shape[2]
    cw_n = conv_w.shape[2]

    first = _ag_direct(jnp.concatenate([c, conv_w[0].reshape(1, CONV_K * cw_n)], axis=1), "ag_c")[:, 0]
    c_all = first[:, :D_MODEL]
    conv_w_full = first[:, D_MODEL:].reshape(N_DEV, CONV_K, cw_n).transpose(1, 0, 2).reshape(CONV_K, XBC_W)
    b_piece = lax.dynamic_slice_in_dim(b_ada, me * ada_n, ada_n, axis=1)
    mod_all = _ag_direct(_mod_piece(c_all, w_ada[0], b_piece), "ag_mod")
    mod = lax.dynamic_index_in_dim(mod_all, me, axis=1, keepdims=False).reshape(1, 3 * D_MODEL)
    shift, scale, gate = mod[:, :D_MODEL], mod[:, D_MODEL:2 * D_MODEL], mod[:, 2 * D_MODEL:]

    w_t, zero = _ag_two_level(w_in[0].T.astype(BF), "ag_w_in")
    w_t = w_t.reshape(N_DEV * in_n, D_MODEL)

    def with_mine(blocks, mine):
        return lax.dynamic_update_index_in_dim(lax.empty(blocks, mine.dtype), mine, me, axis=0)

    rows = jnp.concatenate([w_attn_proj[0], w_ssm_proj[0], w_out[0]], axis=0).astype(BF) + zero
    r_ap, r_sp = w_attn_proj.shape[1], w_ssm_proj.shape[1]
    rows_started, zero = _exchange_start(rows, with_mine((N_DEV,) + rows.shape, rows), True, "ag_rows_start")

    def rows_fn(after):
        return _exchange_wait(rows_started, after, True, "ag_rows_wait")

    started = {}

    def send_blocks(key, g, name):
        started[key], zero = _exchange_start(
            g, with_mine(g.shape, lax.dynamic_index_in_dim(g, me, axis=0, keepdims=False)), False, name)
        return zero

    def after_mid(g_wap, g_wsp, g_wout):
        return send_blocks("rows", jnp.concatenate(
            [g_wap.reshape(N_DEV, r_ap, D_MODEL), g_wsp.reshape(N_DEV, r_sp, D_MODEL),
             g_wout.reshape(N_DEV, r_ap, D_MODEL)], axis=1), "rs_rows_start")

    def after_gw(g_ws):
        return send_blocks("in", jnp.concatenate(g_ws, axis=0).reshape(N_DEV, in_n, D_MODEL), "rs_in_start")

    r = _local_step(x[0], loss_target[0], shift, scale + zero, gate, w_t, rows_fn, norm_w, q_norm_w, k_norm_w,
                    rel_bias, sinks, conv_w_full, conv_b, dt_bias, a_log, d_skip, ssm_norm_w, after_mid, after_gw)

    small = dict(b_ada=r["dmod"], norm_w=r["g_norm_w"], q_norm_w=r["g_qnw"], k_norm_w=r["g_knw"], rel_bias=r["g_rel"],
                 sinks=r["g_sinks"], conv_b=r["g_conv_b"], dt_bias=r["g_dt_bias"], a_log=r["g_a_log"],
                 d_skip=r["g_d_skip"], ssm_norm_w=r["g_ssm_nw"])
    pack_all = _ag_direct(_pack_partials(small, r["loss"], r["g_conv_w"]), "ag_small")
    res, loss = _adam_small(pack_all, w, m, v)
    loss = loss[0, 0]
    cw_parts = pack_all[:, 0, _CW_OFF:].reshape(N_DEV, CONV_K, XBC_W)
    cw_mine = lax.dynamic_slice_in_dim(cw_parts, me * cw_n, cw_n, axis=2)
    res["conv_w"] = [a[None] for a in _adam(cw_mine, conv_w[0], m_conv_w[0], v_conv_w[0], "adam_conv_w")]

    dmod_piece = lax.dynamic_slice_in_dim(pack_all[:, 0, :3 * D_MODEL], me * ada_n, ada_n, axis=1)
    g_ada = _gw_ada(c_all, dmod_piece)
    res["w_ada"] = [a[None] for a in _adam(g_ada[None], w_ada[0], m_w_ada[0], v_w_ada[0], "adam_w_ada")]

    cat = lambda d: jnp.concatenate([d["w_attn_proj"][0], d["w_ssm_proj"][0], d["w_out"][0]], axis=0)
    rows_res = _adam(_exchange_wait(started["rows"], g_ada, False, "rs_rows_wait"), cat(w), cat(m), cat(v), "adam_w_rows")
    res["w_in"] = [a.T[None] for a in _adam(_exchange_wait(started["in"], rows_res[0], False, "rs_in_wait"),
                                            w_in[0].T, m_w_in[0].T, v_w_in[0].T, "adam_w_in")]
    res["w_attn_proj"] = [a[None, :r_ap] for a in rows_res]
    res["w_ssm_proj"] = [a[None, r_ap:r_ap + r_sp] for a in rows_res]
    res["w_out"] = [a[None, r_ap + r_sp:] for a in rows_res]

    outs = [loss, r["grad_x"][None]]
    for j in range(4):
        outs += [res[name][j] for name in WEIGHTS]
    return tuple(outs)
```

```python
import math

import numpy as np
import jax
import jax.numpy as jnp
from jax import lax
from jax.experimental import pallas as pl
from jax.experimental.pallas import tpu as pltpu

F32 = jnp.float32
BF = jnp.bfloat16
HI = lax.Precision.HIGHEST

D_MODEL = 1024
ATTN_HEADS = 16
KV_HEADS = 4
GRP = ATTN_HEADS // KV_HEADS
HEAD_DIM = 64
ATTN_W = ATTN_HEADS * HEAD_DIM
KV_W = KV_HEADS * HEAD_DIM
BLOCK = 128
REL_BUCKETS = 32
REL_MAX_DIST = 128
SSM_W = 2048
SSM_P = 64
SSM_HEADS = 32
SSM_G = 4
SSM_R = 8
SSM_N = 128
CONV_K = 4
XBC_W = SSM_W + 2 * SSM_G * SSM_N
SEG_W = (ATTN_W, KV_W, KV_W, ATTN_W, SSM_W, XBC_W, SSM_HEADS, D_MODEL, D_MODEL)
SEG_OFF = tuple(int(v) for v in np.cumsum((0,) + SEG_W))
IN_W = SEG_OFF[-1]
GATE_SEGS = (3, 4, 7, 8)
EPS = 1e-6
N_DEV = 8
ADAM_LR, ADAM_B1, ADAM_B2, ADAM_EPS, ADAM_WD, ADAM_STEP = 0.001, 0.9, 0.999, 1e-08, 0.01, 10
VMEM_LIMIT = 60 * 1024 * 1024
MESH = pl.DeviceIdType.MESH
ANY = pl.BlockSpec(memory_space=pl.ANY)


def _dot(a, b, precision=None):
    return jnp.dot(a, b, preferred_element_type=F32, precision=precision)


def _dot_nt(a, b, precision=None):
    return lax.dot_general(a, b, (((1,), (1,)), ((), ())), preferred_element_type=F32, precision=precision)


def _dot_tn(a, b, precision=None):
    return lax.dot_general(a, b, (((0,), (0,)), ((), ())), preferred_element_type=F32, precision=precision)


def _bf(a):
    return a.astype(BF)


def _sig(a):
    return 0.5 * jnp.tanh(0.5 * a) + 0.5


def _params(**kw):
    return pltpu.CompilerParams(vmem_limit_bytes=VMEM_LIMIT, **kw)


def _full(shape):
    nd = len(shape)
    return pl.BlockSpec(shape, lambda i: (0,) * nd)


def _rows(tm, w):
    return pl.BlockSpec((tm, w), lambda i: (i, 0))


def _inproj(x, norm_w, scale, shift, w_t, tm=256):
    s = x.shape[0]

    def body(x_ref, nw_ref, sc_ref, sh_ref, w_hbm, *rest):
        outs, h_ref, w_vm, sem = rest[:9], rest[9], rest[10], rest[11]
        first = pl.program_id(0) == 0
        cps = [pltpu.make_async_copy(w_hbm.at[SEG_OFF[j]:SEG_OFF[j + 1], :], w_vm.at[SEG_OFF[j]:SEG_OFF[j + 1], :], sem.at[j])
               for j in range(9)]

        def tile(waiting):
            xv = x_ref[...]
            r = lax.rsqrt(jnp.mean(xv * xv, axis=-1, keepdims=True) + EPS)
            h = xv * r * (nw_ref[...] * (1.0 + sc_ref[...])) + sh_ref[...]
            hb = _bf(h)
            h_ref[...] = hb
            for j in range(9):
                if waiting:
                    cps[j].wait()
                outs[j][...] = _dot_nt(hb, w_vm[SEG_OFF[j]:SEG_OFF[j + 1], :]).astype(outs[j].dtype)

        @pl.when(first)
        def _():
            for cp in cps:
                cp.start()
            tile(True)

        @pl.when(jnp.logical_not(first))
        def _():
            tile(False)

    vec = _full((1, D_MODEL))
    return pl.pallas_call(
        body, name="inproj", grid=(s // tm,),
        in_specs=[_rows(tm, D_MODEL), vec, vec, vec, ANY],
        out_specs=[_rows(tm, w) for w in SEG_W] + [_rows(tm, D_MODEL)],
        out_shape=[jax.ShapeDtypeStruct((s, w), BF if j in GATE_SEGS else F32) for j, w in enumerate(SEG_W)]
                  + [jax.ShapeDtypeStruct((s, D_MODEL), BF)],
        scratch_shapes=[pltpu.VMEM((IN_W, D_MODEL), BF), pltpu.SemaphoreType.DMA((9,))],
        compiler_params=_params(dimension_semantics=("arbitrary",)),
    )(x, norm_w, scale, shift, w_t)


def _bucket_onehot_t():
    qi = jnp.arange(BLOCK)[:, None]
    kj = jnp.arange(2 * BLOCK)[None, :]
    dist = qi + BLOCK - kj
    n = jnp.maximum(dist, 0)
    max_exact = REL_BUCKETS // 2
    nf = jnp.maximum(n, 1).astype(F32)
    large = max_exact + (jnp.log(nf / max_exact) / math.log(REL_MAX_DIST / max_exact)
                         * (REL_BUCKETS - max_exact)).astype(jnp.int32)
    large = jnp.minimum(large, REL_BUCKETS - 1)
    bucket = jnp.where(n < max_exact, n, large).reshape(1, BLOCK * 2 * BLOCK)
    return (bucket == jnp.arange(REL_BUCKETS)[:, None]).astype(F32)


def _bias_dense(rel_bias_t, oh_t):
    def body(rb_ref, oh_ref, o_ref):
        o_ref[...] = _dot(rb_ref[...], oh_ref[...], HI)

    return pl.pallas_call(
        body, name="bias_dense", out_shape=jax.ShapeDtypeStruct((ATTN_HEADS, BLOCK * 2 * BLOCK), F32),
        compiler_params=_params(),
    )(rel_bias_t, oh_t)


def _bias_grad(ds_sum, oh_t):
    def body(ds_ref, oh_ref, o_ref):
        o_ref[...] = _dot_nt(ds_ref[...], oh_ref[...], HI)

    return pl.pallas_call(
        body, name="bias_grad", out_shape=jax.ShapeDtypeStruct((ATTN_HEADS, REL_BUCKETS), F32),
        compiler_params=_params(),
    )(ds_sum, oh_t)


def _group_sum(a, e):
    hi = _bf(a)
    return _dot(hi, e) + _dot(_bf(a - hi.astype(F32)), e)


def _group_bcast(a, e3t):
    hi = _bf(a)
    r1 = a - hi.astype(F32)
    mid = _bf(r1)
    return _dot(jnp.concatenate([hi, mid, _bf(r1 - mid.astype(F32))], axis=1), e3t)


def _membership(width, group, ngroups):
    e = (jnp.arange(width)[:, None] // group == jnp.arange(ngroups)[None, :]).astype(BF)
    return e, jnp.tile(e.T, (3, 1))


def _fold(width, group):
    return (jnp.arange(width)[:, None] % group == jnp.arange(group)[None, :]).astype(BF)


def _heads_norm(t, w_x, e, e3t):
    r = lax.rsqrt(_group_sum(t * t, e) * (1.0 / HEAD_DIM) + EPS)
    r_x = _group_bcast(r, e3t)
    return t * r_x * w_x, r_x


def _heads_norm_bwd(t, r_x, w_x, d, e, e3t):
    wd = d * w_x
    corr = _group_bcast(_group_sum(t * wd, e) * (1.0 / HEAD_DIM), e3t)
    return r_x * wd - t * (r_x * r_x * r_x) * corr, jnp.sum(d * t * r_x, axis=0, keepdims=True)


def _stack_heads(a, hk):
    return jnp.concatenate([a[:, (hk * GRP + g) * HEAD_DIM:(hk * GRP + g + 1) * HEAD_DIM] for g in range(GRP)], axis=0)


def _stack_cols(a, hk):
    return jnp.concatenate([a[:, hk * GRP + g:hk * GRP + g + 1] for g in range(GRP)], axis=0)


def _masked_bias(bias):
    qi = jnp.arange(BLOCK)[:, None]
    kj = jnp.arange(2 * BLOCK)[None, :]
    cur_ok = jnp.logical_and(kj >= BLOCK, kj - BLOCK <= qi)
    both_ok = jnp.logical_or(jnp.logical_and(kj < BLOCK, kj > qi), cur_ok)
    return jnp.stack([jnp.where(cur_ok, bias, -1e30), jnp.where(both_ok, bias, -1e30)])


def _attn_consts(qnw, knw):
    eq, eq3t = _membership(ATTN_W, HEAD_DIM, ATTN_HEADS)
    ek, ek3t = _membership(KV_W, HEAD_DIM, ATTN_HEADS)
    return (jnp.tile(qnw, (1, ATTN_HEADS)), jnp.tile(knw, (1, KV_HEADS)), eq, eq3t, ek, ek3t)


def _attn_fwd(q, k, v, bias, sinks, consts):
    s = q.shape[0]
    nb = s // BLOCK
    gq = GRP * BLOCK
    bias_t = bias.reshape(2, KV_HEADS, GRP, BLOCK, 2 * BLOCK).transpose(0, 1, 4, 2, 3).reshape(2, KV_HEADS, 2 * BLOCK, gq)
    sink_rows = jnp.repeat(sinks.reshape(KV_HEADS, GRP), BLOCK, axis=1).reshape(KV_HEADS, 1, gq)
    eye = jnp.eye(BLOCK, dtype=BF)

    def body(q_ref, kp_ref, kc_ref, vp_ref, vc_ref, b_ref, bt_ref, sk_ref, skr_ref, eye_ref,
             qw_ref, kw_ref, eq_ref, eq3_ref, ek_ref, ek3_ref, o_ref, lse_ref):
        qn = _bf(_heads_norm(q_ref[...], qw_ref[...], eq_ref[...], eq3_ref[...])[0] * (HEAD_DIM ** -0.5))
        kn = _bf(_heads_norm(jnp.concatenate([kp_ref[...], kc_ref[...]], axis=0), kw_ref[...], ek_ref[...], ek3_ref[...])[0])
        vv = _bf(jnp.concatenate([vp_ref[...], vc_ref[...]], axis=0))
        ones = jnp.ones((2 * BLOCK, HEAD_DIM), BF)
        lses = []
        kss = [slice(hk * HEAD_DIM, (hk + 1) * HEAD_DIM) for hk in range(KV_HEADS)]
        qgs = [_stack_heads(qn, hk) for hk in range(KV_HEADS)]
        sc_ts = [_dot_nt(kn[:, kss[hk]], qgs[hk]) + bt_ref[0, hk] for hk in range(KV_HEADS)]
        m_rows = [jnp.maximum(jnp.max(sc_ts[hk], axis=0, keepdims=True), skr_ref[hk]) for hk in range(KV_HEADS)]
        m8s = [_bf(jnp.broadcast_to(m + jnp.abs(m) * (2.0 ** -7), (8, gq))) for m in m_rows]
        ms = [jnp.concatenate([_dot_nt(eye_ref[...], m8[:, g * BLOCK:(g + 1) * BLOCK])[:, 0:1] for g in range(GRP)], axis=0)
              for m8 in m8s]
        scs = [_dot_nt(qgs[hk], kn[:, kss[hk]]) + b_ref[0, hk * GRP:(hk + 1) * GRP].reshape(gq, 2 * BLOCK)
               for hk in range(KV_HEADS)]
        ps = [_bf(jnp.exp(scs[hk] - ms[hk])) for hk in range(KV_HEADS)]
        pvs = [_dot(ps[hk], jnp.concatenate([vv[:, kss[hk]], ones], axis=1)) for hk in range(KV_HEADS)]
        for hk in range(KV_HEADS):
            m, pv = ms[hk], pvs[hk]
            sink = jnp.concatenate([jnp.full((BLOCK, 1), sk_ref[0, hk * GRP + g], F32) for g in range(GRP)], axis=0)
            den = pv[:, HEAD_DIM:HEAD_DIM + 1] + jnp.exp(sink - m)
            out = pv[:, :HEAD_DIM] * (1.0 / den)
            lse = m + jnp.log(den)
            for g in range(GRP):
                h = hk * GRP + g
                o_ref[:, h * HEAD_DIM:(h + 1) * HEAD_DIM] = out[g * BLOCK:(g + 1) * BLOCK]
                lses.append(lse[g * BLOCK:(g + 1) * BLOCK])
        lse_ref[...] = jnp.concatenate(lses, axis=1)

    cur = lambda w: pl.BlockSpec((BLOCK, w), lambda i: (i, 0))
    prev = lambda w: pl.BlockSpec((BLOCK, w), lambda i: (jnp.maximum(i - 1, 0), 0))
    whole = lambda a: pl.BlockSpec(a.shape, lambda i: (0,) * a.ndim)
    first_or_not = lambda a: pl.BlockSpec((1,) + a.shape[1:], lambda i: (jnp.minimum(i, 1),) + (0,) * (a.ndim - 1))
    return pl.pallas_call(
        body, name="attn_fwd", grid=(nb,),
        in_specs=[cur(ATTN_W), prev(KV_W), cur(KV_W), prev(KV_W), cur(KV_W), first_or_not(bias), first_or_not(bias_t),
                  pl.BlockSpec(memory_space=pltpu.SMEM), whole(sink_rows), whole(eye)] + [_full(c.shape) for c in consts],
        out_specs=[cur(ATTN_W), cur(ATTN_HEADS)],
        out_shape=[jax.ShapeDtypeStruct((s, ATTN_W), F32), jax.ShapeDtypeStruct((s, ATTN_HEADS), F32)],
        compiler_params=_params(dimension_semantics=("arbitrary",)),
    )(q, k, k, v, v, bias, bias_t, sinks, sink_rows, eye, *consts)


def _conv_taps(xbc, tail):
    ext = jnp.concatenate([tail, xbc], axis=0)
    return [pltpu.roll(ext, CONV_K - 1 - j, axis=0)[8:8 + BLOCK] if j < CONV_K - 1 else xbc for j in range(CONV_K)]


def _softplus(u):
    return jnp.maximum(u, 0.0) + jnp.log(1.0 + jnp.exp(-jnp.abs(u)))


def _tril():
    r = lax.broadcasted_iota(jnp.int32, (BLOCK, BLOCK), 0)
    c = lax.broadcasted_iota(jnp.int32, (BLOCK, BLOCK), 1)
    return r >= c


def _triu():
    r = lax.broadcasted_iota(jnp.int32, (BLOCK, BLOCK), 0)
    c = lax.broadcasted_iota(jnp.int32, (BLOCK, BLOCK), 1)
    return r <= c


def _exact_left(m01, a):
    hi = _bf(a)
    r1 = a - hi.astype(F32)
    mid = _bf(r1)
    return _dot(m01, hi) + _dot(m01, mid) + _dot(m01, _bf(r1 - mid.astype(F32)))


def _ssd_common(conv, dtr, dtb_ref, alog_ref, e3_ref):
    sg = _sig(conv)
    xact = conv * sg
    u = dtr + dtb_ref[...]
    dt = _softplus(u)
    a = -jnp.exp(alog_ref[...])
    trilb = _tril()
    acum = _exact_left(trilb.astype(BF), dt * a)
    both = _group_bcast(jnp.concatenate([dt, acum], axis=0), e3_ref[...])
    dt_x, acum_x = both[:BLOCK], both[BLOCK:]
    return sg, xact, u, dt, a, trilb, acum, dt_x, acum_x


SSD_CH = 2


def _ssd_fwd(xbc, dt_raw, conv_w, conv_b, dt_bias, a_log, dsk_x, e3t):
    s = xbc.shape[0]
    nc = s // BLOCK
    ch = SSD_CH if nc % SSD_CH == 0 else 1
    rows = ch * BLOCK

    def body(x_ref, tail_ref, dtr_ref, cw_ref, cb_ref, dtb_ref, alog_ref, dsk_ref, e3_ref,
             y_ref, hp_ref, conv_ref, hst, yd_s, yoff_s):
        i = pl.program_id(0)

        @pl.when(i == 0)
        def _():
            hst[...] = jnp.zeros_like(hst)

        for j in range(ch):
            rs = slice(j * BLOCK, (j + 1) * BLOCK)
            tail = jnp.where(i > 0, tail_ref[...], 0.0) if j == 0 else x_ref[j * BLOCK - 8:j * BLOCK, :]
            taps = _conv_taps(x_ref[rs, :], tail)
            conv = cb_ref[...] + sum(taps[t] * cw_ref[t:t + 1, :] for t in range(CONV_K))
            conv_ref[rs, :] = conv
            _, xact, _, _, _, trilb, acum, dt_x, acum_x = _ssd_common(conv, dtr_ref[rs, :], dtb_ref, alog_ref, e3_ref)
            xs = xact[:, :SSM_W]
            acum_t = acum.T
            ea_x = jnp.exp(acum_x)
            last_x = acum_x[BLOCK - 1:BLOCK, :]
            xdt = xs * dt_x
            xw = xdt * jnp.exp(last_x - acum_x)
            cd_x = jnp.exp(last_x)
            hprev = hst[...]
            hp_ref[j] = hprev
            sls = [slice(g * SSM_R * SSM_P, (g + 1) * SSM_R * SSM_P) for g in range(SSM_G)]
            bgs = [_bf(xact[:, SSM_W + g * SSM_N:SSM_W + (g + 1) * SSM_N]) for g in range(SSM_G)]
            cgs = [_bf(xact[:, SSM_W + SSM_G * SSM_N + g * SSM_N:SSM_W + SSM_G * SSM_N + (g + 1) * SSM_N])
                   for g in range(SSM_G)]
            xdt_b, xw_b, hprev_b = _bf(xdt), _bf(xw), _bf(hprev)
            low_half = lax.broadcasted_iota(jnp.int32, (BLOCK, 2 * SSM_P), 1) < SSM_P
            cbs = [_dot_nt(cgs[g], bgs[g]) for g in range(SSM_G)]
            for g in range(SSM_G):
                sl = sls[g]
                yoff_s[:, sl] = _dot(cgs[g], hprev_b[:, sl]) * ea_x[:, sl]
                hst[:, sl] = hprev[:, sl] * cd_x[:, sl] + _dot_tn(bgs[g], xw_b[:, sl])
            for g in range(SSM_G):
                hss = [slice((g * SSM_R + r) * SSM_P, (g * SSM_R + r + 1) * SSM_P) for r in range(SSM_R)]
                mms = [_bf(cbs[g] * jnp.exp(jnp.where(trilb, acum[:, g * SSM_R + r:g * SSM_R + r + 1]
                                                      - acum_t[g * SSM_R + r:g * SSM_R + r + 1, :], -1e30)))
                       for r in range(SSM_R)]
                for r in range(0, SSM_R, 2):
                    pair = slice(hss[r].start, hss[r + 1].stop)
                    xp = xdt_b[:, pair]
                    rhs = jnp.concatenate([jnp.where(low_half, xp, 0), jnp.where(low_half, 0, xp)], axis=0)
                    yd_s[:, pair] = _dot(jnp.concatenate([mms[r], mms[r + 1]], axis=1), rhs)
            y_ref[rs, :] = yd_s[...] + yoff_s[...] + dsk_ref[...] * xs

    blk = lambda w: pl.BlockSpec((rows, w), lambda i: (i, 0))
    return pl.pallas_call(
        body, name="ssd_fwd", grid=(nc // ch,),
        in_specs=[blk(XBC_W), pl.BlockSpec((8, XBC_W), lambda i: (jnp.maximum(i * (rows // 8) - 1, 0), 0)),
                  blk(SSM_HEADS), _full((CONV_K, XBC_W)), _full((1, XBC_W)), _full((1, SSM_HEADS)),
                  _full((1, SSM_HEADS)), _full((1, SSM_W)), _full((3 * SSM_HEADS, SSM_W))],
        out_specs=[blk(SSM_W), pl.BlockSpec((ch, SSM_N, SSM_W), lambda i: (i, 0, 0)), blk(XBC_W)],
        out_shape=[jax.ShapeDtypeStruct((s, SSM_W), F32), jax.ShapeDtypeStruct((nc, SSM_N, SSM_W), F32),
                   jax.ShapeDtypeStruct((s, XBC_W), F32)],
        scratch_shapes=[pltpu.VMEM((SSM_N, SSM_W), F32), pltpu.VMEM((BLOCK, SSM_W), F32), pltpu.VMEM((BLOCK, SSM_W), F32)],
        compiler_params=_params(dimension_semantics=("arbitrary",)),
    )(xbc, xbc, dt_raw, conv_w, conv_b, dt_bias, a_log, dsk_x, e3t)


def _dsilu(z, sg):
    return sg * (1.0 + z * (1.0 - sg))


def _mid(x, tgt, o_att, za, ypre, zm, ga, gb, gate, ssm_nw, rows_all, tm=256):
    s = x.shape[0]
    gw = SSM_W // SSM_G

    r_ap, r_sp = ATTN_W // N_DEV, SSM_W // N_DEV

    def body(x_ref, t_ref, o_ref, za_ref, yp_ref, zm_ref, ga_ref, gb_ref, gate_ref, nw_ref, rows_h,
             dout_ref, do_ref, dza_ref, dyp_ref, dzm_ref, dga_ref, dgb_ref,
             yag_ref, dya_ref, yn_ref, dyb_ref, mg_ref, dob_ref, gnw_ref, dgate_ref, loss_ref,
             wap_v, wsp_v, wout_v, sem):
        i = pl.program_id(0)

        @pl.when(i == 0)
        def _():
            cps = []
            for d in range(N_DEV):
                for j, (dst, r0, rn) in enumerate(((wap_v, 0, r_ap), (wsp_v, r_ap, r_sp), (wout_v, r_ap + r_sp, r_ap))):
                    cps.append(pltpu.make_async_copy(rows_h.at[d, r0:r0 + rn, :], dst.at[d * rn:(d + 1) * rn, :], sem.at[j]))
            for cp in cps:
                cp.start()
            gnw_ref[...] = jnp.zeros_like(gnw_ref)
            dgate_ref[...] = jnp.zeros_like(dgate_ref)
            loss_ref[...] = jnp.zeros_like(loss_ref)
            for cp in cps:
                cp.wait()

        gate = gate_ref[...]
        nw = nw_ref[...]
        o_att = o_ref[...]
        z_a = za_ref[...].astype(F32)
        s_a = _sig(z_a)
        silu_a = z_a * s_a
        yag = _bf(o_att * silu_a)
        yag_ref[...] = yag
        ypre = yp_ref[...]
        z_m = zm_ref[...].astype(F32)
        s_m = _sig(z_m)
        silu_m = z_m * s_m
        yg = ypre * silu_m
        rinv = jnp.concatenate(
            [jnp.broadcast_to(lax.rsqrt(jnp.mean(yg[:, g * gw:(g + 1) * gw] ** 2, axis=-1, keepdims=True) + EPS), (tm, gw))
             for g in range(SSM_G)], axis=1)
        ynr = yg * rinv
        yn = _bf(ynr * nw)
        yn_ref[...] = yn
        y_a = _dot(yag, wap_v[...])
        y_b = _dot(yn, wsp_v[...])
        g_a = _sig(ga_ref[...].astype(F32))
        g_b = _sig(gb_ref[...].astype(F32))
        merged = _bf(g_a * y_a + g_b * y_b)
        mg_ref[...] = merged
        o = _dot(merged, wout_v[...])
        diff = x_ref[...] + gate * o - t_ref[...]
        loss_ref[...] += (0.5 / D_MODEL) * jnp.sum(diff * diff, axis=(0, 1), keepdims=True)
        dout = diff * (1.0 / D_MODEL)
        dout_ref[...] = dout
        dgate_ref[...] += jnp.sum(dout * o, axis=0, keepdims=True)
        d_o = _bf(dout * gate)
        dob_ref[...] = d_o
        dmerged = _dot_nt(d_o, wout_v[...])
        dy_af = dmerged * g_a
        dy_bf = dmerged * g_b
        dy_a = _bf(dy_af)
        dy_b = _bf(dy_bf)
        dya_ref[...] = dy_a
        dyb_ref[...] = dy_b
        dyag = _dot_nt(dy_a, wap_v[...])
        dyn = _dot_nt(dy_b, wsp_v[...])
        dga_ref[...] = _bf(dy_af * y_a * (1.0 - g_a))
        dgb_ref[...] = _bf(dy_bf * y_b * (1.0 - g_b))
        do_ref[...] = dyag * silu_a
        dza_ref[...] = _bf(dyag * o_att * _dsilu(z_a, s_a))
        gnw_ref[...] += jnp.sum(dyn * ynr, axis=0, keepdims=True)
        dynw = dyn * nw
        corr = jnp.concatenate(
            [jnp.broadcast_to(jnp.mean((dynw * ynr)[:, g * gw:(g + 1) * gw], axis=-1, keepdims=True), (tm, gw))
             for g in range(SSM_G)], axis=1)
        dyg = rinv * (dynw - ynr * corr)
        dyp_ref[...] = dyg * silu_m
        dzm_ref[...] = _bf(dyg * ypre * _dsilu(z_m, s_m))

    r1, r2 = _rows(tm, D_MODEL), _rows(tm, SSM_W)
    sd = jax.ShapeDtypeStruct
    return pl.pallas_call(
        body, name="mid", grid=(s // tm,),
        in_specs=[r1, r1, r1, r1, r2, r2, r1, r1, _full((1, D_MODEL)), _full((1, SSM_W)), ANY],
        out_specs=[r1, r1, r1, r2, r2, r1, r1, r1, r1, r2, r1, r1, r1,
                   _full((1, SSM_W)), _full((1, D_MODEL)), _full((1, 1))],
        out_shape=[sd((s, D_MODEL), F32), sd((s, ATTN_W), F32), sd((s, ATTN_W), BF), sd((s, SSM_W), F32),
                   sd((s, SSM_W), BF), sd((s, D_MODEL), BF), sd((s, D_MODEL), BF),
                   sd((s, ATTN_W), BF), sd((s, D_MODEL), BF), sd((s, SSM_W), BF), sd((s, D_MODEL), BF),
                   sd((s, D_MODEL), BF), sd((s, D_MODEL), BF),
                   sd((1, SSM_W), F32), sd((1, D_MODEL), F32), sd((1, 1), F32)],
        scratch_shapes=[pltpu.VMEM((ATTN_W, D_MODEL), BF), pltpu.VMEM((SSM_W, D_MODEL), BF), pltpu.VMEM((D_MODEL, D_MODEL), BF),
                        pltpu.SemaphoreType.DMA((3,))],
        compiler_params=_params(dimension_semantics=("arbitrary",)),
    )(x, tgt, o_att, za, ypre, zm, ga, gb, gate, ssm_nw, rows_all)


def _attn_bwd(q, k, v, bias, sinks, consts, o_att, lse, d_o):
    s = q.shape[0]
    nb = s // BLOCK
    folds = (_fold(ATTN_W, HEAD_DIM), _fold(KV_W, HEAD_DIM))

    def body(q_ref, kp_ref, kc_ref, vp_ref, vc_ref, b_ref, skv_ref, qw_ref, kw_ref, eq_ref, eq3_ref, ek_ref, ek3_ref,
             fq_ref, fk_ref, o_ref, lse_ref, do_ref,
             dq_ref, dk_ref, dv_ref, dss_ref, gqw_ref, gkw_ref, gsk_ref, ckn, cv, dqn_s, dkn_s, dv_s, gq_x, gk_x):
        i = pl.program_id(0)
        kw, ek, ek3 = kw_ref[...], ek_ref[...], ek3_ref[...]

        @pl.when(i == 0)
        def _():
            for ref in (ckn, cv, dss_ref, gq_x, gk_x, gsk_ref):
                ref[...] = jnp.zeros_like(ref)

        @pl.when(i < nb)
        def _():
            qw, eq, eq3 = qw_ref[...], eq_ref[...], eq3_ref[...]
            qf = q_ref[...]
            qnf, rq_x = _heads_norm(qf, qw, eq, eq3)
            qn = _bf(qnf * (HEAD_DIM ** -0.5))
            kf = jnp.concatenate([kp_ref[...], kc_ref[...]], axis=0)
            knf, rk_x = _heads_norm(kf, kw, ek, ek3)
            kn = _bf(knf)
            vv = _bf(jnp.concatenate([vp_ref[...], vc_ref[...]], axis=0))
            d_of = do_ref[...]
            d_ob = _bf(d_of)
            lse_all = lse_ref[...]
            delta = _group_sum(d_of * o_ref[...], eq)
            gsk_ref[...] += jnp.sum(-jnp.exp(skv_ref[...] - lse_all) * delta, axis=0, keepdims=True)
            kss = [slice(hk * HEAD_DIM, (hk + 1) * HEAD_DIM) for hk in range(KV_HEADS)]
            qgs = [_stack_heads(qn, hk) for hk in range(KV_HEADS)]
            d_ogs = [_stack_heads(d_ob, hk) for hk in range(KV_HEADS)]
            scs = [_dot_nt(qgs[hk], kn[:, kss[hk]]) + b_ref[0, hk * GRP:(hk + 1) * GRP].reshape(GRP * BLOCK, 2 * BLOCK)
                   for hk in range(KV_HEADS)]
            dps = [_dot_nt(d_ogs[hk], vv[:, kss[hk]]) for hk in range(KV_HEADS)]
            ps = [jnp.exp(scs[hk] - _stack_cols(lse_all, hk)) for hk in range(KV_HEADS)]
            dss = [ps[hk] * (dps[hk] - _stack_cols(delta, hk)) for hk in range(KV_HEADS)]
            pbs = [_bf(p) for p in ps]
            dsbs = [_bf(ds) for ds in dss]
            for hk in range(KV_HEADS):
                dss_ref[hk * GRP:(hk + 1) * GRP] += dss[hk].reshape(GRP, BLOCK, 2 * BLOCK)
            for hk in range(KV_HEADS):
                dv_s[:, kss[hk]] = _dot_tn(pbs[hk], d_ogs[hk])
                dkn_s[:, kss[hk]] = _dot_tn(dsbs[hk], qgs[hk])
            dqns = [_dot(dsbs[hk], kn[:, kss[hk]]) * (HEAD_DIM ** -0.5) for hk in range(KV_HEADS)]
            for hk in range(KV_HEADS):
                for g in range(GRP):
                    h = hk * GRP + g
                    dqn_s[:, h * HEAD_DIM:(h + 1) * HEAD_DIM] = dqns[hk][g * BLOCK:(g + 1) * BLOCK]
            dq, gq = _heads_norm_bwd(qf, rq_x, qw, dqn_s[...], eq, eq3)
            dq_ref[...] = _bf(dq)
            gq_x[...] += gq
            dk, gk = _heads_norm_bwd(kf[:BLOCK], rk_x[:BLOCK], kw, ckn[...] + dkn_s[0:BLOCK, :], ek, ek3)
            dk_ref[...] = _bf(dk)
            gk_x[...] += gk
            dv_ref[...] = _bf(cv[...] + dv_s[0:BLOCK, :])
            ckn[...] = dkn_s[BLOCK:2 * BLOCK, :]
            cv[...] = dv_s[BLOCK:2 * BLOCK, :]

        @pl.when(i == nb)
        def _():
            kc = kc_ref[...]
            dk, gk = _heads_norm_bwd(kc, _heads_norm(kc, kw, ek, ek3)[1], kw, ckn[...], ek, ek3)
            dk_ref[...] = _bf(dk)
            dv_ref[...] = _bf(cv[...])
            gqw_ref[...] = _group_sum(jnp.broadcast_to(gq_x[...], (8, ATTN_W)), fq_ref[...])[0:1]
            gkw_ref[...] = _group_sum(jnp.broadcast_to(gk_x[...] + gk, (8, KV_W)), fk_ref[...])[0:1]

    last = nb - 1
    cur = lambda w: pl.BlockSpec((BLOCK, w), lambda i: (jnp.minimum(i, last), 0))
    prev = lambda w: pl.BlockSpec((BLOCK, w), lambda i: (jnp.maximum(jnp.minimum(i, last) - 1, 0), 0))
    late = lambda w: pl.BlockSpec((BLOCK, w), lambda i: (jnp.maximum(i - 1, 0), 0))
    sd = jax.ShapeDtypeStruct
    return pl.pallas_call(
        body, name="attn_bwd", grid=(nb + 1,),
        in_specs=[cur(ATTN_W), prev(KV_W), cur(KV_W), prev(KV_W), cur(KV_W),
                  pl.BlockSpec((1, ATTN_HEADS, BLOCK, 2 * BLOCK), lambda i: (jnp.minimum(i, 1), 0, 0, 0)),
                  _full((1, ATTN_HEADS))]
                 + [_full(c.shape) for c in consts + folds] + [cur(ATTN_W), cur(ATTN_HEADS), cur(ATTN_W)],
        out_specs=[cur(ATTN_W), late(KV_W), late(KV_W),
                   pl.BlockSpec((ATTN_HEADS, BLOCK, 2 * BLOCK), lambda i: (0, 0, 0)),
                   _full((1, HEAD_DIM)), _full((1, HEAD_DIM)), _full((1, ATTN_HEADS))],
        out_shape=[sd((s, ATTN_W), BF), sd((s, KV_W), BF), sd((s, KV_W), BF),
                   sd((ATTN_HEADS, BLOCK, 2 * BLOCK), F32), sd((1, HEAD_DIM), F32), sd((1, HEAD_DIM), F32),
                   sd((1, ATTN_HEADS), F32)],
        scratch_shapes=[pltpu.VMEM((BLOCK, KV_W), F32), pltpu.VMEM((BLOCK, KV_W), F32),
                        pltpu.VMEM((BLOCK, ATTN_W), F32), pltpu.VMEM((2 * BLOCK, KV_W), F32),
                        pltpu.VMEM((2 * BLOCK, KV_W), F32), pltpu.VMEM((1, ATTN_W), F32), pltpu.VMEM((1, KV_W), F32)],
        compiler_params=_params(dimension_semantics=("arbitrary",)),
    )(q, k, k, v, v, bias, sinks, *consts, *folds, o_att, lse, d_o)


def _ssd_bwd(xbc, conv_all, dt_raw, conv_w, dt_bias, a_log, dsk_x, e_mat, e3t, hprev_all, dy_all):
    s = xbc.shape[0]
    nc = s // BLOCK
    ch = 1
    rows = ch * BLOCK
    nsteps = nc // ch
    gw = SSM_R * SSM_P
    b0, c0 = SSM_W, SSM_W + SSM_G * SSM_N

    def body(x_ref, conv_ref, dtr_ref, cw_ref, dtb_ref, alog_ref, dsk_ref, e_ref, e3_ref, hp_ref, dy_ref,
             dx_ref, ddt_ref, gcw_ref, gcb_ref, gdtb_ref, galog_ref, gdsk_ref,
             dh, nhead, gdskx, dxdt_s, dbc_s, dxd_s):
        def chunk_bwd(j):
            rs = slice(j * BLOCK, (j + 1) * BLOCK)
            conv = conv_ref[rs, :]
            sg, xact, u, dt, a, trilb, acum, dt_x, acum_x = _ssd_common(conv, dtr_ref[rs, :], dtb_ref, alog_ref, e3_ref)
            xs = xact[:, :SSM_W]
            acum_t = acum.T
            ea_x = jnp.exp(acum_x)
            last_x = acum_x[BLOCK - 1:BLOCK, :]
            dte_x = jnp.exp(last_x - acum_x)
            cd_x = jnp.exp(last_x)
            xdt = xs * dt_x
            xw = xdt * dte_x
            hprev = hp_ref[j]
            dhn = dh[...]
            dy = dy_ref[rs, :]
            gdskx[...] += jnp.sum(dy * xs, axis=0, keepdims=True)
            dyea = dy * ea_x
            lane = lax.broadcasted_iota(jnp.int32, (BLOCK, SSM_HEADS), 1)
            dacum = jnp.zeros((BLOCK, SSM_HEADS), F32)
            dacc_x, dlast_x = [], []
            sls = [slice(g * gw, (g + 1) * gw) for g in range(SSM_G)]
            bgs = [_bf(xact[:, b0 + g * SSM_N:b0 + (g + 1) * SSM_N]) for g in range(SSM_G)]
            cgs = [_bf(xact[:, c0 + g * SSM_N:c0 + (g + 1) * SSM_N]) for g in range(SSM_G)]
            hpgs = [_bf(hprev[:, sl]) for sl in sls]
            dhgs = [_bf(dhn[:, sl]) for sl in sls]
            dyeags = [_bf(dyea[:, sl]) for sl in sls]
            xwgs = [_bf(xw[:, sl]) for sl in sls]
            xdt_b, dy_b = _bf(xdt), _bf(dy)
            low_half = lax.broadcasted_iota(jnp.int32, (BLOCK, 2 * SSM_P), 1) < SSM_P
            cbs = [_dot_nt(cgs[g], bgs[g]) for g in range(SSM_G)]
            gmats = [_dot(cgs[g], hpgs[g]) for g in range(SSM_G)]
            dxws = [_dot(bgs[g], dhgs[g]) for g in range(SSM_G)]
            dcgs = [_dot_nt(dyeags[g], hpgs[g]) for g in range(SSM_G)]
            dbgs = [_dot_nt(xwgs[g], dhgs[g]) for g in range(SSM_G)]
            for g in range(SSM_G):
                sl = sls[g]
                dh[:, sl] = dhn[:, sl] * cd_x[:, sl] + _dot_tn(cgs[g], dyeags[g])
                dxdt_s[:, sl] = dxws[g] * dte_x[:, sl]
                dacc_x.append(dy[:, sl] * gmats[g] * ea_x[:, sl] - dxws[g] * xw[:, sl])
                dlast_x.append(jnp.sum(dxws[g] * xw[:, sl], axis=0, keepdims=True)
                               + jnp.sum(dhn[:, sl] * hprev[:, sl], axis=0, keepdims=True) * cd_x[:, sl])
            for g in range(SSM_G):
                bg, cg, cb, dbg, dcg = bgs[g], cgs[g], cbs[g], dbgs[g], dcgs[g]
                hss = [slice((g * SSM_R + r) * SSM_P, (g * SSM_R + r + 1) * SSM_P) for r in range(SSM_R)]
                lms = [jnp.exp(jnp.where(trilb, acum[:, g * SSM_R + r:g * SSM_R + r + 1]
                                         - acum_t[g * SSM_R + r:g * SSM_R + r + 1, :], -1e30)) for r in range(SSM_R)]
                mms = [cb * lm for lm in lms]
                mmbs = [_bf(mm) for mm in mms]
                dms = []
                for r in range(0, SSM_R, 2):
                    pair = slice(hss[r].start, hss[r + 1].stop)
                    xp, dyp = xdt_b[:, pair], dy_b[:, pair]
                    dmp = _dot_nt(dyp, jnp.concatenate([jnp.where(low_half, xp, 0), jnp.where(low_half, 0, xp)], axis=0))
                    dms += [dmp[:, :BLOCK], dmp[:, BLOCK:]]
                    dxd_s[:, pair] = _dot_tn(jnp.concatenate([mmbs[r], mmbs[r + 1]], axis=0),
                                             jnp.concatenate([jnp.where(low_half, dyp, 0), jnp.where(low_half, 0, dyp)], axis=0))
                dcb = sum(dms[r] * lms[r] for r in range(SSM_R))
                wms = [dms[r] * mms[r] for r in range(SSM_R)]
                antis = [wm - wm.T for wm in wms]
                for r in range(SSM_R):
                    dacum = dacum + _group_sum(antis[r], (lane == g * SSM_R + r).astype(BF))
                dcbb = _bf(dcb)
                dbc_s[:, g * SSM_N:(g + 1) * SSM_N] = dbg + _dot_tn(dcbb, cg)
                dbc_s[:, SSM_G * SSM_N + g * SSM_N:SSM_G * SSM_N + (g + 1) * SSM_N] = dcg + _dot(dcbb, bg)
            dxdt = dxdt_s[...] + dxd_s[...]
            dxs = dy * dsk_ref[...] + dxdt * dt_x
            red = _group_sum(jnp.concatenate(
                [dxdt * xs, jnp.concatenate(dacc_x, axis=1),
                 jnp.broadcast_to(jnp.concatenate(dlast_x, axis=1), (8, SSM_W))], axis=0), e_ref[...])
            row = lax.broadcasted_iota(jnp.int32, (BLOCK, SSM_HEADS), 0)
            dacum = dacum + red[BLOCK:2 * BLOCK] + jnp.where(row == BLOCK - 1, red[2 * BLOCK:2 * BLOCK + 1], 0.0)
            ddta = _exact_left(_triu().astype(BF), dacum)
            ddt = red[:BLOCK] + ddta * a
            galog_ref[...] += jnp.sum(ddta * dt, axis=0, keepdims=True) * a
            du = ddt * _sig(u)
            ddt_ref[rs, :] = _bf(du)
            gdtb_ref[...] += jnp.sum(du, axis=0, keepdims=True)
            dconv = jnp.concatenate([dxs, dbc_s[...]], axis=1) * _dsilu(conv, sg)
            gcb_ref[...] += jnp.sum(dconv, axis=0, keepdims=True)
            ext2 = jnp.concatenate([dconv, nhead[...]], axis=0)
            ahead = [pltpu.roll(ext2, BLOCK + 8 - (CONV_K - 1 - j), axis=0)[0:BLOCK] if j < CONV_K - 1 else dconv
                     for j in range(CONV_K)]
            dx_ref[rs, :] = _bf(sum(ahead[j] * cw_ref[j:j + 1, :] for j in range(CONV_K)))
            xraw = x_ref[rs, :]
            gcw_ref[...] += jnp.concatenate([jnp.sum(ahead[j] * xraw, axis=0, keepdims=True) for j in range(CONV_K)], axis=0)
            nhead[...] = dconv[0:8]

        i = pl.program_id(0)

        @pl.when(i == 0)
        def _():
            for ref in (dh, nhead, gdskx, gcw_ref, gcb_ref, gdtb_ref, galog_ref, gdsk_ref):
                ref[...] = jnp.zeros_like(ref)

        for j in reversed(range(ch)):
            chunk_bwd(j)

        @pl.when(i == nsteps - 1)
        def _():
            gdsk_ref[...] = _group_sum(jnp.broadcast_to(gdskx[...], (8, SSM_W)), e_ref[...])[0:1]

    chunk = lambda w: pl.BlockSpec((rows, w), lambda i: (nsteps - 1 - i, 0))
    sd = jax.ShapeDtypeStruct
    return pl.pallas_call(
        body, name="ssd_bwd", grid=(nsteps,),
        in_specs=[chunk(XBC_W), chunk(XBC_W),
                  chunk(SSM_HEADS), _full((CONV_K, XBC_W)), _full((1, SSM_HEADS)),
                  _full((1, SSM_HEADS)), _full((1, SSM_W)), _full((SSM_W, SSM_HEADS)), _full((3 * SSM_HEADS, SSM_W)),
                  pl.BlockSpec((ch, SSM_N, SSM_W), lambda i: (nsteps - 1 - i, 0, 0)), chunk(SSM_W)],
        out_specs=[chunk(XBC_W), chunk(SSM_HEADS), _full((CONV_K, XBC_W)), _full((1, XBC_W)),
                   _full((1, SSM_HEADS)), _full((1, SSM_HEADS)), _full((1, SSM_HEADS))],
        out_shape=[sd((s, XBC_W), BF), sd((s, SSM_HEADS), BF), sd((CONV_K, XBC_W), F32), sd((1, XBC_W), F32),
                   sd((1, SSM_HEADS), F32), sd((1, SSM_HEADS), F32), sd((1, SSM_HEADS), F32)],
        scratch_shapes=[pltpu.VMEM((SSM_N, SSM_W), F32), pltpu.VMEM((8, XBC_W), F32),
                        pltpu.VMEM((1, SSM_W), F32), pltpu.VMEM((BLOCK, SSM_W), F32),
                        pltpu.VMEM((BLOCK, 2 * SSM_G * SSM_N), F32), pltpu.VMEM((BLOCK, SSM_W), F32)],
        compiler_params=_params(dimension_semantics=("arbitrary",)),
    )(xbc, conv_all, dt_raw, conv_w, dt_bias, a_log, dsk_x, e_mat, e3t, hprev_all, dy_all)


def _dh(x, dout, norm_w, scale, dsegs, w_t, tm=256):
    s = x.shape[0]

    def body(x_ref, dout_ref, nw_ref, sc_ref, *rest):
        d_refs, w_hbm = rest[:9], rest[9]
        gx_ref, dshift_ref, dscale_ref, gnw_ref = rest[10:14]
        w_vm, sem = rest[14], rest[15]
        first = pl.program_id(0) == 0
        cps = [pltpu.make_async_copy(w_hbm.at[SEG_OFF[j]:SEG_OFF[j + 1], :], w_vm.at[SEG_OFF[j]:SEG_OFF[j + 1], :], sem.at[j])
               for j in range(9)]

        def tile(waiting):
            dh = None
            for j in range(9):
                if waiting:
                    cps[j].wait()
                part = _dot(d_refs[j][...], w_vm[SEG_OFF[j]:SEG_OFF[j + 1], :])
                dh = part if dh is None else dh + part
            xv = x_ref[...]
            r = lax.rsqrt(jnp.mean(xv * xv, axis=-1, keepdims=True) + EPS)
            xn = xv * r
            nw = nw_ref[...]
            sc1 = 1.0 + sc_ref[...]
            dshift_ref[...] += jnp.sum(dh, axis=0, keepdims=True)
            dhxn = jnp.sum(dh * xn, axis=0, keepdims=True)
            dscale_ref[...] += dhxn * nw
            gnw_ref[...] += dhxn * sc1
            dxn = dh * (nw * sc1)
            gx_ref[...] = dout_ref[...] + r * (dxn - xn * jnp.mean(xn * dxn, axis=-1, keepdims=True))

        @pl.when(first)
        def _():
            for cp in cps:
                cp.start()
            for ref in (dshift_ref, dscale_ref, gnw_ref):
                ref[...] = jnp.zeros_like(ref)
            tile(True)

        @pl.when(jnp.logical_not(first))
        def _():
            tile(False)

    vec = _full((1, D_MODEL))
    sd = jax.ShapeDtypeStruct
    return pl.pallas_call(
        body, name="dh", grid=(s // tm,),
        in_specs=[_rows(tm, D_MODEL), _rows(tm, D_MODEL), vec, vec] + [_rows(tm, w) for w in SEG_W] + [ANY],
        out_specs=[_rows(tm, D_MODEL), vec, vec, vec],
        out_shape=[sd((s, D_MODEL), F32), sd((1, D_MODEL), F32), sd((1, D_MODEL), F32), sd((1, D_MODEL), F32)],
        scratch_shapes=[pltpu.VMEM((IN_W, D_MODEL), BF), pltpu.SemaphoreType.DMA((9,))],
        compiler_params=_params(dimension_semantics=("arbitrary",)),
    )(x, dout, norm_w, scale, *dsegs, w_t)


def _gw_seg(h, dseg, name, tm=1024):
    s, w = dseg.shape
    tn = min(w, 1024)
    tm = min(tm, s)
    nm = s // tm

    def body(h_ref, d_ref, o_ref, acc):
        m = pl.program_id(1)

        @pl.when(m == 0)
        def _():
            acc[...] = jnp.zeros_like(acc)

        acc[...] += _dot_tn(d_ref[...], h_ref[...])

        @pl.when(m == nm - 1)
        def _():
            o_ref[...] = _bf(acc[...])

    return pl.pallas_call(
        body, name=name, grid=(w // tn, nm),
        in_specs=[pl.BlockSpec((tm, D_MODEL), lambda n, m: (m, 0)), pl.BlockSpec((tm, tn), lambda n, m: (m, n))],
        out_specs=pl.BlockSpec((tn, D_MODEL), lambda n, m: (n, 0)),
        out_shape=jax.ShapeDtypeStruct((w, D_MODEL), BF),
        scratch_shapes=[pltpu.VMEM((tn, D_MODEL), F32)],
        compiler_params=_params(dimension_semantics=("arbitrary", "arbitrary")),
    )(h, dseg)


def _gw_in(h, dsegs):
    return [_gw_seg(h, d, "gw_in_%d" % j) for j, d in enumerate(dsegs)]


def _local_step(x, tgt, shift, scale, gate, w_t, rows_fn, norm_w, qnw, knw, rel_bias, sinks,
                conv_w, conv_b, dt_bias, a_log, d_skip, ssm_nw, after_mid=None, after_gw=None):
    oh_t = _bucket_onehot_t()
    bias = _masked_bias(_bias_dense(rel_bias.T, oh_t).reshape(ATTN_HEADS, BLOCK, 2 * BLOCK))
    *segs, h = _inproj(x, norm_w, scale, shift, w_t)
    q, k, v, za, zm, xbc, dtr, ga, gb = segs
    consts = _attn_consts(qnw, knw)
    o_att, lse = _attn_fwd(q, k, v, bias, sinks, consts)
    e_mat, e3t = _membership(SSM_W, SSM_P, SSM_HEADS)
    dsk_x = jnp.repeat(d_skip, SSM_P, axis=1)
    ypre, hprev, conv = _ssd_fwd(xbc, dtr, conv_w, conv_b, dt_bias, a_log, dsk_x, e3t)
    (dout, d_o, dza, dyp, dzm, dga, dgb, yag, dy_a, yn, dy_b, merged, dob, g_ssm_nw, dgate, loss) = _mid(
        x, tgt, o_att, za, ypre, zm, ga, gb, gate, ssm_nw, rows_fn(ypre))
    g_wap = _gw_seg(dy_a, yag, "gw_attn_proj")
    g_wsp = _gw_seg(dy_b, yn, "gw_ssm_proj")
    g_wout = _gw_seg(dob, merged, "gw_out")
    zero = after_mid(g_wap, g_wsp, g_wout) if after_mid is not None else 0.0
    dq, dk, dv, dss, g_qnw, g_knw, g_sinks = _attn_bwd(q, k, v, bias, sinks + zero, consts, o_att, lse, d_o)
    g_rel = _bias_grad(dss.reshape(ATTN_HEADS, BLOCK * 2 * BLOCK), oh_t).T
    dxbc, ddt, g_cw, g_cb, g_dtb, g_alog, g_dsk = _ssd_bwd(
        xbc, conv, dtr, conv_w, dt_bias, a_log, dsk_x, e_mat, e3t, hprev, dyp)
    dsegs = (dq, dk, dv, dza, dzm, dxbc, ddt, dga, dgb)
    g_ws = _gw_in(h, dsegs)
    zero = after_gw(g_ws) if after_gw is not None else 0.0
    gx, dshift, dscale, g_nw = _dh(x, dout, norm_w + zero, scale, dsegs, w_t)
    return dict(loss=loss, grad_x=gx, dmod=jnp.concatenate([dshift, dscale, dgate], axis=1), g_ws=g_ws,
                g_wap=g_wap, g_wsp=g_wsp, g_wout=g_wout, g_norm_w=g_nw, g_qnw=g_qnw, g_knw=g_knw, g_rel=g_rel,
                g_sinks=g_sinks, g_conv_w=g_cw, g_conv_b=g_cb, g_dt_bias=g_dtb, g_a_log=g_alog, g_d_skip=g_dsk,
                g_ssm_nw=g_ssm_nw)


def _me():
    return lax.axis_index("x"), lax.axis_index("y"), lax.axis_index("c")


def _flip(v, bit):
    return 1 - v if bit else v


def _ag_direct(v, name):
    def body(v_ref, out_ref, send_sems, recv_sems, local_sem):
        x, y, c = _me()
        me = 4 * x + 2 * y + c
        mine = pltpu.make_async_copy(v_ref, out_ref.at[me], local_sem)
        mine.start()
        peers = [(_flip(x, k >> 2 & 1), _flip(y, k >> 1 & 1), _flip(c, k & 1)) for k in range(1, N_DEV)]
        sends = [pltpu.make_async_remote_copy(
            src_ref=v_ref, dst_ref=out_ref.at[me], send_sem=send_sems.at[j], recv_sem=recv_sems.at[j],
            device_id=p, device_id_type=MESH) for j, p in enumerate(peers)]
        for cp in sends:
            cp.start()
        for j, (px, py, pc) in enumerate(peers):
            pltpu.make_async_remote_copy(
                src_ref=v_ref, dst_ref=out_ref.at[4 * px + 2 * py + pc], send_sem=send_sems.at[j],
                recv_sem=recv_sems.at[j], device_id=(px, py, pc), device_id_type=MESH).wait_recv()
        for cp in sends:
            cp.wait_send()
        mine.wait()

    vm = pl.BlockSpec(memory_space=pltpu.VMEM)
    return pl.pallas_call(
        body, name=name, out_shape=jax.ShapeDtypeStruct((N_DEV,) + v.shape, v.dtype),
        in_specs=[vm], out_specs=vm,
        scratch_shapes=[pltpu.SemaphoreType.DMA((N_DEV - 1,)), pltpu.SemaphoreType.DMA((N_DEV - 1,)),
                        pltpu.SemaphoreType.DMA],
        compiler_params=_params(),
    )(v)


def _ag_two_level(v, name):
    def body(v_ref, out_ref, token, send_sems, recv_sems, local_sem):
        token[...] = jnp.zeros_like(token)
        x, y, c = _me()
        me, sibling = (x, y, c), (x, y, 1 - c)
        chips = [(1 - x, y), (x, 1 - y), (1 - x, 1 - y)]

        def slot(px, py, pc):
            return out_ref.at[4 * px + 2 * py + pc]

        def copy(k, block, to, src=None):
            return pltpu.make_async_remote_copy(
                src_ref=slot(*block) if src is None else src, dst_ref=slot(*block),
                send_sem=send_sems.at[k], recv_sem=recv_sems.at[k], device_id=to, device_id_type=MESH)

        mine = pltpu.make_async_copy(v_ref, slot(*me), local_sem)
        mine.start()
        first = [copy(0, me, sibling, src=v_ref)]
        first += [copy(1 + j, me, (*chip, c), src=v_ref) for j, chip in enumerate(chips)]
        for cp in first:
            cp.start()
        passed = [copy(4 + j, (*chip, c), sibling) for j, chip in enumerate(chips)]
        for j, chip in enumerate(chips):
            copy(1 + j, (*chip, c), me).wait_recv()
            passed[j].start()
        copy(0, sibling, me).wait_recv()
        for j, chip in enumerate(chips):
            copy(4 + j, (*chip, 1 - c), me).wait_recv()
        for cp in first + passed:
            cp.wait_send()
        mine.wait()

    out, token = pl.pallas_call(
        body, name=name,
        out_shape=(jax.ShapeDtypeStruct((N_DEV,) + v.shape, v.dtype), jax.ShapeDtypeStruct((8, 128), v.dtype)),
        in_specs=[ANY], out_specs=(ANY, pl.BlockSpec(memory_space=pltpu.VMEM)),
        scratch_shapes=[pltpu.SemaphoreType.DMA((7,)), pltpu.SemaphoreType.DMA((7,)), pltpu.SemaphoreType.DMA],
        compiler_params=_params(),
    )(v)
    return out, token[0:1, 0:1]


HBM = pl.BlockSpec(memory_space=pltpu.HBM)
SEM = pl.BlockSpec(memory_space=pltpu.SEMAPHORE)
EFFECT = pltpu.SideEffectType.DATAFLOW_SIDE_EFFECTING


def _peers(x, y, c):
    return [(_flip(x, k >> 2 & 1), _flip(y, k >> 1 & 1), _flip(c, k & 1)) for k in range(1, N_DEV)]


def _exchange_start(src, land, gather, name):
    def body(src_ref, land_ref, send_sems, recv_sems, src_thru, land_thru, token):
        x, y, c = _me()
        me = 4 * x + 2 * y + c
        for j, (px, py, pc) in enumerate(_peers(x, y, c)):
            pltpu.make_async_remote_copy(
                src_ref=src_ref if gather else src_ref.at[4 * px + 2 * py + pc], dst_ref=land_ref.at[me],
                send_sem=send_sems.at[j], recv_sem=recv_sems.at[j], device_id=(px, py, pc), device_id_type=MESH).start()
        token[...] = jnp.zeros_like(token)

    sems = pltpu.SemaphoreType.DMA((N_DEV - 1,))
    out = pl.pallas_call(
        body, name=name,
        out_shape=(sems, sems, pltpu.HBM(src.shape, src.dtype), pltpu.HBM(land.shape, land.dtype),
                   jax.ShapeDtypeStruct((8, 128), F32)),
        in_specs=(HBM, HBM), out_specs=(SEM, SEM, HBM, HBM, pl.BlockSpec(memory_space=pltpu.VMEM)),
        input_output_aliases={0: 2, 1: 3},
        compiler_params=pltpu.CompilerParams(has_side_effects=EFFECT),
    )(pltpu.with_memory_space_constraint(src, pltpu.HBM), pltpu.with_memory_space_constraint(land, pltpu.HBM))
    return out[:4], out[4][0, 0]


def _exchange_wait(started, after, gather, name):
    send_sems, recv_sems, src_thru, land_thru = started

    def body(src_ref, land_ref, send_sems, recv_sems, after_ref, src_dead, got_ref):
        x, y, c = _me()
        for j, (px, py, pc) in enumerate(_peers(x, y, c)):
            pid = 4 * px + 2 * py + pc
            cp = pltpu.make_async_remote_copy(
                src_ref=src_ref if gather else src_ref.at[pid], dst_ref=land_ref.at[pid],
                send_sem=send_sems.at[j], recv_sem=recv_sems.at[j], device_id=(px, py, pc), device_id_type=MESH)
            cp.wait_send()
            cp.wait_recv()

    return pl.pallas_call(
        body, name=name,
        out_shape=(pltpu.HBM(src_thru.shape, src_thru.dtype), pltpu.HBM(land_thru.shape, land_thru.dtype)),
        in_specs=(HBM, HBM, SEM, SEM, ANY), out_specs=(HBM, HBM), input_output_aliases={0: 0, 1: 1},
        compiler_params=pltpu.CompilerParams(has_side_effects=EFFECT),
    )(src_thru, land_thru, send_sems, recv_sems, after)[1]


def _silu(a):
    return a * _sig(a)


def _mod_piece(c_all, w_ada, b_piece):
    def body(c_ref, w_ref, b_ref, o_ref):
        o_ref[...] = _dot(_bf(_silu(c_ref[...])), _bf(w_ref[...])) + b_ref[...]

    return pl.pallas_call(
        body, name="mod_piece", out_shape=jax.ShapeDtypeStruct((c_all.shape[0], w_ada.shape[1]), F32),
        compiler_params=_params(),
    )(c_all, w_ada, b_piece)


def _gw_ada(c_all, dmod_piece):
    def body(c_ref, d_ref, o_ref):
        o_ref[...] = _dot_tn(_bf(_silu(c_ref[...])), _bf(d_ref[...]))

    return pl.pallas_call(
        body, name="gw_ada", out_shape=jax.ShapeDtypeStruct((c_all.shape[1], dmod_piece.shape[1]), F32),
        compiler_params=_params(),
    )(c_all, dmod_piece)


def _adam(parts, w, m, v, name):
    k, r, n = parts.shape
    if r <= 256 or r % 256 == 0:
        tr, tn = min(r, 256), n
    else:
        tr, tn = r, 256
    assert r % tr == 0 and n % tn == 0

    def body(p_ref, w_ref, m_ref, v_ref, g_ref, d_ref, nm_ref, nv_ref):
        g = p_ref[0].astype(F32)
        for j in range(1, k):
            g = g + p_ref[j].astype(F32)
        g_ref[...] = g
        d_ref[...], nm_ref[...], nv_ref[...] = _adam_math(g, w_ref[...], m_ref[...], v_ref[...])

    blk = pl.BlockSpec((tr, tn), lambda i, j: (i, j))
    return pl.pallas_call(
        body, name=name, grid=(r // tr, n // tn),
        in_specs=[pl.BlockSpec((k, tr, tn), lambda i, j: (0, i, j)), blk, blk, blk],
        out_specs=[blk, blk, blk, blk],
        out_shape=[jax.ShapeDtypeStruct((r, n), F32)] * 4,
        compiler_params=_params(dimension_semantics=("arbitrary", "arbitrary")),
    )(parts, w, m, v)


def _adam_math(g, w, m, v):
    m_new = ADAM_B1 * m + (1.0 - ADAM_B1) * g
    v_new = ADAM_B2 * v + (1.0 - ADAM_B2) * jnp.square(g)
    m_hat = m_new / (1.0 - ADAM_B1 ** ADAM_STEP)
    v_hat = v_new / (1.0 - ADAM_B2 ** ADAM_STEP)
    return -ADAM_LR * (m_hat / (jnp.sqrt(v_hat) + ADAM_EPS) + ADAM_WD * w), m_new, v_new


_SMALL = (("b_ada", 3 * D_MODEL), ("norm_w", D_MODEL), ("q_norm_w", HEAD_DIM), ("k_norm_w", HEAD_DIM),
          ("rel_bias", REL_BUCKETS * ATTN_HEADS), ("sinks", ATTN_HEADS), ("conv_b", XBC_W), ("dt_bias", SSM_HEADS),
          ("a_log", SSM_HEADS), ("d_skip", SSM_HEADS), ("ssm_norm_w", SSM_W))
_SLOT = tuple(-(-n // 128) * 128 for _, n in _SMALL)
_SLOT_OFF = tuple(int(o) for o in np.cumsum((0,) + _SLOT))
_LOSS_OFF = _SLOT_OFF[-1]
_CW_OFF = _LOSS_OFF + 128
_PACK_N = _CW_OFF + CONV_K * XBC_W


def _pack_partials(small, loss, g_conv_w):
    parts = []
    for (name, n), slot in zip(_SMALL, _SLOT):
        parts.append(small[name].reshape(1, n))
        if slot > n:
            parts.append(jnp.zeros((1, slot - n), F32))
    parts += [loss.reshape(1, 1), jnp.zeros((1, 127), F32), g_conv_w.reshape(1, CONV_K * XBC_W)]
    return jnp.concatenate(parts, axis=1)


def _adam_small(pack_all, w, m, v):
    names = [name for name, _ in _SMALL]

    def body(p_ref, *rest):
        ins, outs = rest[:3 * len(names)], rest[3 * len(names):]

        def total(off, n):
            g = p_ref[0, :, off:off + n]
            for d in range(1, N_DEV):
                g = g + p_ref[d, :, off:off + n]
            return g

        for j, (name, n) in enumerate(_SMALL):
            g = total(_SLOT_OFF[j], n)
            delta, m_new, v_new = _adam_math(g, ins[3 * j][...], ins[3 * j + 1][...], ins[3 * j + 2][...])
            outs[4 * j][...] = g
            outs[4 * j + 1][...] = delta
            outs[4 * j + 2][...] = m_new
            outs[4 * j + 3][...] = v_new
        outs[-1][...] = total(_LOSS_OFF, 1)

    flat = []
    for name, n in _SMALL:
        flat += [w[name].reshape(1, n), m[name].reshape(1, n), v[name].reshape(1, n)]
    out_shape = [jax.ShapeDtypeStruct((1, n), F32) for _, n in _SMALL for _ in range(4)] + [jax.ShapeDtypeStruct((1, 1), F32)]
    out = pl.pallas_call(body, name="adam_small", out_shape=out_shape, compiler_params=_params())(pack_all, *flat)
    res = {name: [out[4 * j + t].reshape(w[name].shape) for t in range(4)] for j, name in enumerate(names)}
    return res, out[-1]


WEIGHTS = ("w_ada", "b_ada", "norm_w", "w_in", "q_norm_w", "k_norm_w", "rel_bias", "sinks", "conv_w", "conv_b",
           "dt_bias", "a_log", "d_skip", "ssm_norm_w", "w_attn_proj", "w_ssm_proj", "w_out")


def kernel(x, c, w_ada, b_ada, norm_w, w_in, q_norm_w, k_norm_w, rel_bias, sinks, conv_w, conv_b, dt_bias, a_log, d_skip, ssm_norm_w, w_attn_proj, w_ssm_proj, w_out, loss_target, m_w_ada, m_b_ada, m_norm_w, m_w_in, m_q_norm_w, m_k_norm_w, m_rel_bias, m_sinks, m_conv_w, m_conv_b, m_dt_bias, m_a_log, m_d_skip, m_ssm_norm_w, m_w_attn_proj, m_w_ssm_proj, m_w_out, v_w_ada, v_b_ada, v_norm_w, v_w_in, v_q_norm_w, v_k_norm_w, v_rel_bias, v_sinks, v_conv_w, v_conv_b, v_dt_bias, v_a_log, v_d_skip, v_ssm_norm_w, v_w_attn_proj, v_w_ssm_proj, v_w_out):
    w = dict(w_ada=w_ada, b_ada=b_ada, norm_w=norm_w, w_in=w_in, q_norm_w=q_norm_w, k_norm_w=k_norm_w,
             rel_bias=rel_bias, sinks=sinks, conv_w=conv_w, conv_b=conv_b, dt_bias=dt_bias, a_log=a_log,
             d_skip=d_skip, ssm_norm_w=ssm_norm_w, w_attn_proj=w_attn_proj, w_ssm_proj=w_ssm_proj, w_out=w_out)
    m = dict(w_ada=m_w_ada, b_ada=m_b_ada, norm_w=m_norm_w, w_in=m_w_in, q_norm_w=m_q_norm_w, k_norm_w=m_k_norm_w,
             rel_bias=m_rel_bias, sinks=m_sinks, conv_w=m_conv_w, conv_b=m_conv_b, dt_bias=m_dt_bias, a_log=m_a_log,
             d_skip=m_d_skip, ssm_norm_w=m_ssm_norm_w, w_attn_proj=m_w_attn_proj, w_ssm_proj=m_w_ssm_proj, w_out=m_w_out)
    v = dict(w_ada=v_w_ada, b_ada=v_b_ada, norm_w=v_norm_w, w_in=v_w_in, q_norm_w=v_q_norm_w, k_norm_w=v_k_norm_w,
             rel_bias=v_rel_bias, sinks=v_sinks, conv_w=v_conv_w, conv_b=v_conv_b, dt_bias=v_dt_bias, a_log=v_a_log,
             d_skip=v_d_skip, ssm_norm_w=v_ssm_norm_w, w_attn_proj=v_w_attn_proj, w_ssm_proj=v_w_ssm_proj, w_out=v_w_out)
    me = 4 * lax.axis_index("x") + 2 * lax.axis_index("y") + lax.axis_index("c")
    ada_n = w_ada.shape[2]
    in_n = w_in.shape[2]
    cw_n = conv_w.shape[2]

    first = _ag_direct(jnp.concatenate([c, conv_w[0].reshape(1, CONV_K * cw_n)], axis=1), "ag_c")[:, 0]
    c_all = first[:, :D_MODEL]
    conv_w_full = first[:, D_MODEL:].reshape(N_DEV, CONV_K, cw_n).transpose(1, 0, 2).reshape(CONV_K, XBC_W)
    b_piece = lax.dynamic_slice_in_dim(b_ada, me * ada_n, ada_n, axis=1)
    mod_all = _ag_direct(_mod_piece(c_all, w_ada[0], b_piece), "ag_mod")
    mod = lax.dynamic_index_in_dim(mod_all, me, axis=1, keepdims=False).reshape(1, 3 * D_MODEL)
    shift, scale, gate = mod[:, :D_MODEL], mod[:, D_MODEL:2 * D_MODEL], mod[:, 2 * D_MODEL:]

    w_t, zero = _ag_two_level(w_in[0].T.astype(BF), "ag_w_in")
    w_t = w_t.reshape(N_DEV * in_n, D_MODEL)

    def with_mine(blocks, mine):
        return lax.dynamic_update_index_in_dim(lax.empty(blocks, mine.dtype), mine, me, axis=0)

    rows = jnp.concatenate([w_attn_proj[0], w_ssm_proj[0], w_out[0]], axis=0).astype(BF) + zero
    r_ap, r_sp = w_attn_proj.shape[1], w_ssm_proj.shape[1]
    rows_started, zero = _exchange_start(rows, with_mine((N_DEV,) + rows.shape, rows), True, "ag_rows_start")

    def rows_fn(after):
        return _exchange_wait(rows_started, after, True, "ag_rows_wait")

    started = {}

    def send_blocks(key, g, name):
        started[key], zero = _exchange_start(
            g, with_mine(g.shape, lax.dynamic_index_in_dim(g, me, axis=0, keepdims=False)), False, name)
        return zero

    def after_mid(g_wap, g_wsp, g_wout):
        return send_blocks("rows", jnp.concatenate(
            [g_wap.reshape(N_DEV, r_ap, D_MODEL), g_wsp.reshape(N_DEV, r_sp, D_MODEL),
             g_wout.reshape(N_DEV, r_ap, D_MODEL)], axis=1), "rs_rows_start")

    def after_gw(g_ws):
        return send_blocks("in", jnp.concatenate(g_ws, axis=0).reshape(N_DEV, in_n, D_MODEL), "rs_in_start")

    r = _local_step(x[0], loss_target[0], shift, scale + zero, gate, w_t, rows_fn, norm_w, q_norm_w, k_norm_w,
                    rel_bias, sinks, conv_w_full, conv_b, dt_bias, a_log, d_skip, ssm_norm_w, after_mid, after_gw)

    small = dict(b_ada=r["dmod"], norm_w=r["g_norm_w"], q_norm_w=r["g_qnw"], k_norm_w=r["g_knw"], rel_bias=r["g_rel"],
                 sinks=r["g_sinks"], conv_b=r["g_conv_b"], dt_bias=r["g_dt_bias"], a_log=r["g_a_log"],
                 d_skip=r["g_d_skip"], ssm_norm_w=r["g_ssm_nw"])
    pack_all = _ag_direct(_pack_partials(small, r["loss"], r["g_conv_w"]), "ag_small")
    res, loss = _adam_small(pack_all, w, m, v)
    loss = loss[0, 0]
    cw_parts = pack_all[:, 0, _CW_OFF:].reshape(N_DEV, CONV_K, XBC_W)
    cw_mine = lax.dynamic_slice_in_dim(cw_parts, me * cw_n, cw_n, axis=2)
    res["conv_w"] = [a[None] for a in _adam(cw_mine, conv_w[0], m_conv_w[0], v_conv_w[0], "adam_conv_w")]

    dmod_piece = lax.dynamic_slice_in_dim(pack_all[:, 0, :3 * D_MODEL], me * ada_n, ada_n, axis=1)
    g_ada = _gw_ada(c_all, dmod_piece)
    res["w_ada"] = [a[None] for a in _adam(g_ada[None], w_ada[0], m_w_ada[0], v_w_ada[0], "adam_w_ada")]

    cat = lambda d: jnp.concatenate([d["w_attn_proj"][0], d["w_ssm_proj"][0], d["w_out"][0]], axis=0)
    rows_res = _adam(_exchange_wait(started["rows"], g_ada, False, "rs_rows_wait"), cat(w), cat(m), cat(v), "adam_w_rows")
    res["w_in"] = [a.T[None] for a in _adam(_exchange_wait(started["in"], rows_res[0], False, "rs_in_wait"),
                                            w_in[0].T, m_w_in[0].T, v_w_in[0].T, "adam_w_in")]
    res["w_attn_proj"] = [a[None, :r_ap] for a in rows_res]
    res["w_ssm_proj"] = [a[None, r_ap:r_ap + r_sp] for a in rows_res]
    res["w_out"] = [a[None, r_ap + r_sp:] for a in rows_res]

    outs = [loss, r["grad_x"][None]]
    for j in range(4):
        outs += [res[name][j] for name in WEIGHTS]
    return tuple(outs)
```

```python
import math

import numpy as np
import jax
import jax.numpy as jnp
from jax import lax
from jax.experimental import pallas as pl
from jax.experimental.pallas import tpu as pltpu

F32 = jnp.float32
BF = jnp.bfloat16
HI = lax.Precision.HIGHEST

D_MODEL = 1024
ATTN_HEADS = 16
KV_HEADS = 4
GRP = ATTN_HEADS // KV_HEADS
HEAD_DIM = 64
ATTN_W = ATTN_HEADS * HEAD_DIM
KV_W = KV_HEADS * HEAD_DIM
BLOCK = 128
REL_BUCKETS = 32
REL_MAX_DIST = 128
SSM_W = 2048
SSM_P = 64
SSM_HEADS = 32
SSM_G = 4
SSM_R = 8
SSM_N = 128
CONV_K = 4
XBC_W = SSM_W + 2 * SSM_G * SSM_N
SEG_W = (ATTN_W, 2 * KV_W, ATTN_W + SSM_W, XBC_W, SSM_HEADS, 2 * D_MODEL)
NSEG = len(SEG_W)
SEG_OFF = tuple(int(v) for v in np.cumsum((0,) + SEG_W))
IN_W = SEG_OFF[-1]
GATE_SEGS = (2, 5)
EPS = 1e-6
N_DEV = 8
ADAM_LR, ADAM_B1, ADAM_B2, ADAM_EPS, ADAM_WD, ADAM_STEP = 0.001, 0.9, 0.999, 1e-08, 0.01, 10
VMEM_LIMIT = 60 * 1024 * 1024
MESH = pl.DeviceIdType.MESH
ANY = pl.BlockSpec(memory_space=pl.ANY)


def _dot(a, b, precision=None):
    return jnp.dot(a, b, preferred_element_type=F32, precision=precision)


def _dot_nt(a, b, precision=None):
    return lax.dot_general(a, b, (((1,), (1,)), ((), ())), preferred_element_type=F32, precision=precision)


def _dot_tn(a, b, precision=None):
    return lax.dot_general(a, b, (((0,), (0,)), ((), ())), preferred_element_type=F32, precision=precision)


def _bf(a):
    return a.astype(BF)


def _sig(a):
    return 0.5 * jnp.tanh(0.5 * a) + 0.5


def _params(**kw):
    return pltpu.CompilerParams(vmem_limit_bytes=VMEM_LIMIT, **kw)


def _full(shape):
    nd = len(shape)
    return pl.BlockSpec(shape, lambda i: (0,) * nd)


def _rows(tm, w):
    return pl.BlockSpec((tm, w), lambda i: (i, 0))


def _inproj(x, norm_w, scale, shift, w_t, tm=256):
    s = x.shape[0]

    def body(x_ref, nw_ref, sc_ref, sh_ref, w_hbm, *rest):
        outs, h_ref, w_vm, sem = rest[:NSEG], rest[NSEG], rest[NSEG + 1], rest[NSEG + 2]
        first = pl.program_id(0) == 0
        cps = [pltpu.make_async_copy(w_hbm.at[SEG_OFF[j]:SEG_OFF[j + 1], :], w_vm.at[SEG_OFF[j]:SEG_OFF[j + 1], :], sem.at[j])
               for j in range(NSEG)]

        def tile(waiting):
            xv = x_ref[...]
            r = lax.rsqrt(jnp.mean(xv * xv, axis=-1, keepdims=True) + EPS)
            h = xv * r * (nw_ref[...] * (1.0 + sc_ref[...])) + sh_ref[...]
            hb = _bf(h)
            h_ref[...] = hb
            for j in range(NSEG):
                if waiting:
                    cps[j].wait()
                outs[j][...] = _dot_nt(hb, w_vm[SEG_OFF[j]:SEG_OFF[j + 1], :]).astype(outs[j].dtype)

        @pl.when(first)
        def _():
            for cp in cps:
                cp.start()
            tile(True)

        @pl.when(jnp.logical_not(first))
        def _():
            tile(False)

    vec = _full((1, D_MODEL))
    return pl.pallas_call(
        body, name="inproj", grid=(s // tm,),
        in_specs=[_rows(tm, D_MODEL), vec, vec, vec, ANY],
        out_specs=[_rows(tm, w) for w in SEG_W] + [_rows(tm, D_MODEL)],
        out_shape=[jax.ShapeDtypeStruct((s, w), BF if j in GATE_SEGS else F32) for j, w in enumerate(SEG_W)]
                  + [jax.ShapeDtypeStruct((s, D_MODEL), BF)],
        scratch_shapes=[pltpu.VMEM((IN_W, D_MODEL), BF), pltpu.SemaphoreType.DMA((NSEG,))],
        compiler_params=_params(dimension_semantics=("arbitrary",)),
    )(x, norm_w, scale, shift, w_t)


def _bucket_onehot_t():
    qi = jnp.arange(BLOCK)[:, None]
    kj = jnp.arange(2 * BLOCK)[None, :]
    dist = qi + BLOCK - kj
    n = jnp.maximum(dist, 0)
    max_exact = REL_BUCKETS // 2
    nf = jnp.maximum(n, 1).astype(F32)
    large = max_exact + (jnp.log(nf / max_exact) / math.log(REL_MAX_DIST / max_exact)
                         * (REL_BUCKETS - max_exact)).astype(jnp.int32)
    large = jnp.minimum(large, REL_BUCKETS - 1)
    bucket = jnp.where(n < max_exact, n, large).reshape(1, BLOCK * 2 * BLOCK)
    return (bucket == jnp.arange(REL_BUCKETS)[:, None]).astype(F32)


def _bias_dense(rel_bias_t, oh_t):
    def body(rb_ref, oh_ref, o_ref):
        o_ref[...] = _dot(rb_ref[...], oh_ref[...], HI)

    return pl.pallas_call(
        body, name="bias_dense", out_shape=jax.ShapeDtypeStruct((ATTN_HEADS, BLOCK * 2 * BLOCK), F32),
        compiler_params=_params(),
    )(rel_bias_t, oh_t)


def _bias_grad(ds_sum, oh_t):
    def body(ds_ref, oh_ref, o_ref):
        o_ref[...] = _dot_nt(ds_ref[...], oh_ref[...], HI)

    return pl.pallas_call(
        body, name="bias_grad", out_shape=jax.ShapeDtypeStruct((ATTN_HEADS, REL_BUCKETS), F32),
        compiler_params=_params(),
    )(ds_sum, oh_t)


def _group_sum(a, e):
    hi = _bf(a)
    return _dot(hi, e) + _dot(_bf(a - hi.astype(F32)), e)


def _group_bcast(a, e3t):
    hi = _bf(a)
    r1 = a - hi.astype(F32)
    mid = _bf(r1)
    return _dot(jnp.concatenate([hi, mid, _bf(r1 - mid.astype(F32))], axis=1), e3t)


def _membership(width, group, ngroups):
    e = (jnp.arange(width)[:, None] // group == jnp.arange(ngroups)[None, :]).astype(BF)
    return e, jnp.tile(e.T, (3, 1))


def _fold(width, group):
    return (jnp.arange(width)[:, None] % group == jnp.arange(group)[None, :]).astype(BF)


def _heads_norm(t, w_x, e, e3t):
    r = lax.rsqrt(_group_sum(t * t, e) * (1.0 / HEAD_DIM) + EPS)
    r_x = _group_bcast(r, e3t)
    return t * r_x * w_x, r_x


def _heads_norm_bwd(t, r_x, w_x, d, e, e3t):
    wd = d * w_x
    corr = _group_bcast(_group_sum(t * wd, e) * (1.0 / HEAD_DIM), e3t)
    return r_x * wd - t * (r_x * r_x * r_x) * corr, jnp.sum(d * t * r_x, axis=0, keepdims=True)


def _stack_heads(a, hk):
    return jnp.concatenate([a[:, (hk * GRP + g) * HEAD_DIM:(hk * GRP + g + 1) * HEAD_DIM] for g in range(GRP)], axis=0)


def _stack_cols(a, hk):
    return jnp.concatenate([a[:, hk * GRP + g:hk * GRP + g + 1] for g in range(GRP)], axis=0)


def _masked_bias(bias):
    qi = jnp.arange(BLOCK)[:, None]
    kj = jnp.arange(2 * BLOCK)[None, :]
    cur_ok = jnp.logical_and(kj >= BLOCK, kj - BLOCK <= qi)
    both_ok = jnp.logical_or(jnp.logical_and(kj < BLOCK, kj > qi), cur_ok)
    return jnp.stack([jnp.where(cur_ok, bias, -1e30), jnp.where(both_ok, bias, -1e30)])


def _attn_consts(qnw, knw):
    eq, eq3t = _membership(ATTN_W, HEAD_DIM, ATTN_HEADS)
    ek, ek3t = _membership(KV_W, HEAD_DIM, ATTN_HEADS)
    return (jnp.tile(qnw, (1, ATTN_HEADS)), jnp.tile(knw, (1, KV_HEADS)), eq, eq3t, ek, ek3t)


def _attn_fwd(q, kv, bias, sinks, consts):
    s = q.shape[0]
    nb = s // BLOCK
    gq = GRP * BLOCK
    bias_t = bias.reshape(2, KV_HEADS, GRP, BLOCK, 2 * BLOCK).transpose(0, 1, 4, 2, 3).reshape(2, KV_HEADS, 2 * BLOCK, gq)
    sink_rows = jnp.repeat(sinks.reshape(KV_HEADS, GRP), BLOCK, axis=1).reshape(KV_HEADS, 1, gq)
    eye = jnp.eye(BLOCK, dtype=BF)

    def body(q_ref, kp_ref, kc_ref, vp_ref, vc_ref, b_ref, bt_ref, sk_ref, skr_ref, eye_ref,
             qw_ref, kw_ref, eq_ref, eq3_ref, ek_ref, ek3_ref, o_ref, lse_ref):
        qn = _bf(_heads_norm(q_ref[...], qw_ref[...], eq_ref[...], eq3_ref[...])[0] * (HEAD_DIM ** -0.5))
        kn = _bf(_heads_norm(jnp.concatenate([kp_ref[...], kc_ref[...]], axis=0), kw_ref[...], ek_ref[...], ek3_ref[...])[0])
        vv = _bf(jnp.concatenate([vp_ref[...], vc_ref[...]], axis=0))
        ones = jnp.ones((2 * BLOCK, HEAD_DIM), BF)
        lses = []
        kss = [slice(hk * HEAD_DIM, (hk + 1) * HEAD_DIM) for hk in range(KV_HEADS)]
        qgs = [_stack_heads(qn, hk) for hk in range(KV_HEADS)]
        sc_ts = [_dot_nt(kn[:, kss[hk]], qgs[hk]) + bt_ref[0, hk] for hk in range(KV_HEADS)]
        m_rows = [jnp.maximum(jnp.max(sc_ts[hk], axis=0, keepdims=True), skr_ref[hk]) for hk in range(KV_HEADS)]
        m8s = [_bf(jnp.broadcast_to(m + jnp.abs(m) * (2.0 ** -7), (8, gq))) for m in m_rows]
        ms = [jnp.concatenate([_dot_nt(eye_ref[...], m8[:, g * BLOCK:(g + 1) * BLOCK])[:, 0:1] for g in range(GRP)], axis=0)
              for m8 in m8s]
        scs = [_dot_nt(qgs[hk], kn[:, kss[hk]]) + b_ref[0, hk * GRP:(hk + 1) * GRP].reshape(gq, 2 * BLOCK)
               for hk in range(KV_HEADS)]
        ps = [_bf(jnp.exp(scs[hk] - ms[hk])) for hk in range(KV_HEADS)]
        pvs = [_dot(ps[hk], jnp.concatenate([vv[:, kss[hk]], ones], axis=1)) for hk in range(KV_HEADS)]
        for hk in range(KV_HEADS):
            m, pv = ms[hk], pvs[hk]
            sink = jnp.concatenate([jnp.full((BLOCK, 1), sk_ref[0, hk * GRP + g], F32) for g in range(GRP)], axis=0)
            den = pv[:, HEAD_DIM:HEAD_DIM + 1] + jnp.exp(sink - m)
            out = pv[:, :HEAD_DIM] * (1.0 / den)
            lse = m + jnp.log(den)
            for g in range(GRP):
                h = hk * GRP + g
                o_ref[:, h * HEAD_DIM:(h + 1) * HEAD_DIM] = out[g * BLOCK:(g + 1) * BLOCK]
                lses.append(lse[g * BLOCK:(g + 1) * BLOCK])
        lse_ref[...] = jnp.concatenate(lses, axis=1)

    cur = lambda w, col=0: pl.BlockSpec((BLOCK, w), lambda i: (i, col))
    prev = lambda w, col=0: pl.BlockSpec((BLOCK, w), lambda i: (jnp.maximum(i - 1, 0), col))
    whole = lambda a: pl.BlockSpec(a.shape, lambda i: (0,) * a.ndim)
    first_or_not = lambda a: pl.BlockSpec((1,) + a.shape[1:], lambda i: (jnp.minimum(i, 1),) + (0,) * (a.ndim - 1))
    return pl.pallas_call(
        body, name="attn_fwd", grid=(nb,),
        in_specs=[cur(ATTN_W), prev(KV_W, 0), cur(KV_W, 0), prev(KV_W, 1), cur(KV_W, 1),
                  first_or_not(bias), first_or_not(bias_t),
                  pl.BlockSpec(memory_space=pltpu.SMEM), whole(sink_rows), whole(eye)] + [_full(c.shape) for c in consts],
        out_specs=[cur(ATTN_W), cur(ATTN_HEADS)],
        out_shape=[jax.ShapeDtypeStruct((s, ATTN_W), F32), jax.ShapeDtypeStruct((s, ATTN_HEADS), F32)],
        compiler_params=_params(dimension_semantics=("arbitrary",)),
    )(q, kv, kv, kv, kv, bias, bias_t, sinks, sink_rows, eye, *consts)


def _conv_taps(xbc, tail):
    ext = jnp.concatenate([tail, xbc], axis=0)
    return [pltpu.roll(ext, CONV_K - 1 - j, axis=0)[8:8 + BLOCK] if j < CONV_K - 1 else xbc for j in range(CONV_K)]


def _softplus(u):
    return jnp.maximum(u, 0.0) + jnp.log(1.0 + jnp.exp(-jnp.abs(u)))


def _tril():
    r = lax.broadcasted_iota(jnp.int32, (BLOCK, BLOCK), 0)
    c = lax.broadcasted_iota(jnp.int32, (BLOCK, BLOCK), 1)
    return r >= c


def _triu():
    r = lax.broadcasted_iota(jnp.int32, (BLOCK, BLOCK), 0)
    c = lax.broadcasted_iota(jnp.int32, (BLOCK, BLOCK), 1)
    return r <= c


def _exact_left(m01, a):
    hi = _bf(a)
    r1 = a - hi.astype(F32)
    mid = _bf(r1)
    return _dot(m01, hi) + _dot(m01, mid) + _dot(m01, _bf(r1 - mid.astype(F32)))


def _ssd_common(conv, dtr, dtb_ref, alog_ref, e3_ref):
    sg = _sig(conv)
    xact = conv * sg
    u = dtr + dtb_ref[...]
    dt = _softplus(u)
    a = -jnp.exp(alog_ref[...])
    trilb = _tril()
    acum = _exact_left(trilb.astype(BF), dt * a)
    both = _group_bcast(jnp.concatenate([dt, acum], axis=0), e3_ref[...])
    dt_x, acum_x = both[:BLOCK], both[BLOCK:]
    return sg, xact, u, dt, a, trilb, acum, dt_x, acum_x


SSD_CH = 2


def _ssd_fwd(xbc, dt_raw, conv_w, conv_b, dt_bias, a_log, dsk_x, e3t):
    s = xbc.shape[0]
    nc = s // BLOCK
    ch = SSD_CH if nc % SSD_CH == 0 else 1
    rows = ch * BLOCK

    def body(x_ref, tail_ref, dtr_ref, cw_ref, cb_ref, dtb_ref, alog_ref, dsk_ref, e3_ref,
             y_ref, hp_ref, conv_ref, hst, yd_s, yoff_s):
        i = pl.program_id(0)

        @pl.when(i == 0)
        def _():
            hst[...] = jnp.zeros_like(hst)

        for j in range(ch):
            rs = slice(j * BLOCK, (j + 1) * BLOCK)
            tail = jnp.where(i > 0, tail_ref[...], 0.0) if j == 0 else x_ref[j * BLOCK - 8:j * BLOCK, :]
            taps = _conv_taps(x_ref[rs, :], tail)
            conv = cb_ref[...] + sum(taps[t] * cw_ref[t:t + 1, :] for t in range(CONV_K))
            conv_ref[rs, :] = conv
            _, xact, _, _, _, trilb, acum, dt_x, acum_x = _ssd_common(conv, dtr_ref[rs, :], dtb_ref, alog_ref, e3_ref)
            xs = xact[:, :SSM_W]
            acum_t = acum.T
            ea_x = jnp.exp(acum_x)
            last_x = acum_x[BLOCK - 1:BLOCK, :]
            xdt = xs * dt_x
            xw = xdt * jnp.exp(last_x - acum_x)
            cd_x = jnp.exp(last_x)
            hprev = hst[...]
            hp_ref[j] = hprev
            sls = [slice(g * SSM_R * SSM_P, (g + 1) * SSM_R * SSM_P) for g in range(SSM_G)]
            bgs = [_bf(xact[:, SSM_W + g * SSM_N:SSM_W + (g + 1) * SSM_N]) for g in range(SSM_G)]
            cgs = [_bf(xact[:, SSM_W + SSM_G * SSM_N + g * SSM_N:SSM_W + SSM_G * SSM_N + (g + 1) * SSM_N])
                   for g in range(SSM_G)]
            xdt_b, xw_b, hprev_b = _bf(xdt), _bf(xw), _bf(hprev)
            low_half = lax.broadcasted_iota(jnp.int32, (BLOCK, 2 * SSM_P), 1) < SSM_P
            cbs = [_dot_nt(cgs[g], bgs[g]) for g in range(SSM_G)]
            for g in range(SSM_G):
                sl = sls[g]
                yoff_s[:, sl] = _dot(cgs[g], hprev_b[:, sl]) * ea_x[:, sl]
                hst[:, sl] = hprev[:, sl] * cd_x[:, sl] + _dot_tn(bgs[g], xw_b[:, sl])
            for g in range(SSM_G):
                hss = [slice((g * SSM_R + r) * SSM_P, (g * SSM_R + r + 1) * SSM_P) for r in range(SSM_R)]
                mms = [_bf(cbs[g] * jnp.exp(jnp.where(trilb, acum[:, g * SSM_R + r:g * SSM_R + r + 1]
                                                      - acum_t[g * SSM_R + r:g * SSM_R + r + 1, :], -1e30)))
                       for r in range(SSM_R)]
                for r in range(0, SSM_R, 2):
                    pair = slice(hss[r].start, hss[r + 1].stop)
                    xp = xdt_b[:, pair]
                    rhs = jnp.concatenate([jnp.where(low_half, xp, 0), jnp.where(low_half, 0, xp)], axis=0)
                    yd_s[:, pair] = _dot(jnp.concatenate([mms[r], mms[r + 1]], axis=1), rhs)
            y_ref[rs, :] = yd_s[...] + yoff_s[...] + dsk_ref[...] * xs

    blk = lambda w: pl.BlockSpec((rows, w), lambda i: (i, 0))
    return pl.pallas_call(
        body, name="ssd_fwd", grid=(nc // ch,),
        in_specs=[blk(XBC_W), pl.BlockSpec((8, XBC_W), lambda i: (jnp.maximum(i * (rows // 8) - 1, 0), 0)),
                  blk(SSM_HEADS), _full((CONV_K, XBC_W)), _full((1, XBC_W)), _full((1, SSM_HEADS)),
                  _full((1, SSM_HEADS)), _full((1, SSM_W)), _full((3 * SSM_HEADS, SSM_W))],
        out_specs=[blk(SSM_W), pl.BlockSpec((ch, SSM_N, SSM_W), lambda i: (i, 0, 0)), blk(XBC_W)],
        out_shape=[jax.ShapeDtypeStruct((s, SSM_W), F32), jax.ShapeDtypeStruct((nc, SSM_N, SSM_W), F32),
                   jax.ShapeDtypeStruct((s, XBC_W), F32)],
        scratch_shapes=[pltpu.VMEM((SSM_N, SSM_W), F32), pltpu.VMEM((BLOCK, SSM_W), F32), pltpu.VMEM((BLOCK, SSM_W), F32)],
        compiler_params=_params(dimension_semantics=("arbitrary",)),
    )(xbc, xbc, dt_raw, conv_w, conv_b, dt_bias, a_log, dsk_x, e3t)


def _dsilu(z, sg):
    return sg * (1.0 + z * (1.0 - sg))


def _mid(x, tgt, o_att, zam, ypre, gab, gate, ssm_nw, rows_all, tm=256):
    s = x.shape[0]
    gw = SSM_W // SSM_G

    r_ap, r_sp = ATTN_W // N_DEV, SSM_W // N_DEV

    def body(x_ref, t_ref, o_ref, zam_ref, yp_ref, gab_ref, gate_ref, nw_ref, rows_h,
             dout_ref, do_ref, dzam_ref, dyp_ref, dgab_ref,
             yag_ref, dya_ref, yn_ref, dyb_ref, mg_ref, dob_ref, gnw_ref, dgate_ref, loss_ref,
             wap_v, wsp_v, wout_v, sem):
        i = pl.program_id(0)

        @pl.when(i == 0)
        def _():
            cps = []
            for d in range(N_DEV):
                for j, (dst, r0, rn) in enumerate(((wap_v, 0, r_ap), (wsp_v, r_ap, r_sp), (wout_v, r_ap + r_sp, r_ap))):
                    cps.append(pltpu.make_async_copy(rows_h.at[d, r0:r0 + rn, :], dst.at[d * rn:(d + 1) * rn, :], sem.at[j]))
            for cp in cps:
                cp.start()
            gnw_ref[...] = jnp.zeros_like(gnw_ref)
            dgate_ref[...] = jnp.zeros_like(dgate_ref)
            loss_ref[...] = jnp.zeros_like(loss_ref)
            for cp in cps:
                cp.wait()

        gate = gate_ref[...]
        nw = nw_ref[...]
        o_att = o_ref[...]
        z_a = zam_ref[:, :ATTN_W].astype(F32)
        s_a = _sig(z_a)
        silu_a = z_a * s_a
        yag = _bf(o_att * silu_a)
        yag_ref[...] = yag
        ypre = yp_ref[...]
        z_m = zam_ref[:, ATTN_W:].astype(F32)
        s_m = _sig(z_m)
        silu_m = z_m * s_m
        yg = ypre * silu_m
        rinv = jnp.concatenate(
            [jnp.broadcast_to(lax.rsqrt(jnp.mean(yg[:, g * gw:(g + 1) * gw] ** 2, axis=-1, keepdims=True) + EPS), (tm, gw))
             for g in range(SSM_G)], axis=1)
        ynr = yg * rinv
        yn = _bf(ynr * nw)
        yn_ref[...] = yn
        y_a = _dot(yag, wap_v[...])
        y_b = _dot(yn, wsp_v[...])
        g_a = _sig(gab_ref[:, :D_MODEL].astype(F32))
        g_b = _sig(gab_ref[:, D_MODEL:].astype(F32))
        merged = _bf(g_a * y_a + g_b * y_b)
        mg_ref[...] = merged
        o = _dot(merged, wout_v[...])
        diff = x_ref[...] + gate * o - t_ref[...]
        loss_ref[...] += (0.5 / D_MODEL) * jnp.sum(diff * diff, axis=(0, 1), keepdims=True)
        dout = diff * (1.0 / D_MODEL)
        dout_ref[...] = dout
        dgate_ref[...] += jnp.sum(dout * o, axis=0, keepdims=True)
        d_o = _bf(dout * gate)
        dob_ref[...] = d_o
        dmerged = _dot_nt(d_o, wout_v[...])
        dy_af = dmerged * g_a
        dy_bf = dmerged * g_b
        dy_a = _bf(dy_af)
        dy_b = _bf(dy_bf)
        dya_ref[...] = dy_a
        dyb_ref[...] = dy_b
        dyag = _dot_nt(dy_a, wap_v[...])
        dyn = _dot_nt(dy_b, wsp_v[...])
        dgab_ref[:, :D_MODEL] = _bf(dy_af * y_a * (1.0 - g_a))
        dgab_ref[:, D_MODEL:] = _bf(dy_bf * y_b * (1.0 - g_b))
        do_ref[...] = dyag * silu_a
        dzam_ref[:, :ATTN_W] = _bf(dyag * o_att * _dsilu(z_a, s_a))
        gnw_ref[...] += jnp.sum(dyn * ynr, axis=0, keepdims=True)
        dynw = dyn * nw
        corr = jnp.concatenate(
            [jnp.broadcast_to(jnp.mean((dynw * ynr)[:, g * gw:(g + 1) * gw], axis=-1, keepdims=True), (tm, gw))
             for g in range(SSM_G)], axis=1)
        dyg = rinv * (dynw - ynr * corr)
        dyp_ref[...] = dyg * silu_m
        dzam_ref[:, ATTN_W:] = _bf(dyg * ypre * _dsilu(z_m, s_m))

    r1, r2, r3 = _rows(tm, D_MODEL), _rows(tm, SSM_W), _rows(tm, ATTN_W + SSM_W)
    sd = jax.ShapeDtypeStruct
    return pl.pallas_call(
        body, name="mid", grid=(s // tm,),
        in_specs=[r1, r1, r1, r3, r2, r2, _full((1, D_MODEL)), _full((1, SSM_W)), ANY],
        out_specs=[r1, r1, r3, r2, r2, r1, r1, r2, r1, r1, r1,
                   _full((1, SSM_W)), _full((1, D_MODEL)), _full((1, 1))],
        out_shape=[sd((s, D_MODEL), F32), sd((s, ATTN_W), F32), sd((s, ATTN_W + SSM_W), BF), sd((s, SSM_W), F32),
                   sd((s, 2 * D_MODEL), BF),
                   sd((s, ATTN_W), BF), sd((s, D_MODEL), BF), sd((s, SSM_W), BF), sd((s, D_MODEL), BF),
                   sd((s, D_MODEL), BF), sd((s, D_MODEL), BF),
                   sd((1, SSM_W), F32), sd((1, D_MODEL), F32), sd((1, 1), F32)],
        scratch_shapes=[pltpu.VMEM((ATTN_W, D_MODEL), BF), pltpu.VMEM((SSM_W, D_MODEL), BF), pltpu.VMEM((D_MODEL, D_MODEL), BF),
                        pltpu.SemaphoreType.DMA((3,))],
        compiler_params=_params(dimension_semantics=("arbitrary",)),
    )(x, tgt, o_att, zam, ypre, gab, gate, ssm_nw, rows_all)


def _attn_bwd(q, kv, bias, sinks, consts, o_att, lse, d_o):
    s = q.shape[0]
    nb = s // BLOCK
    folds = (_fold(ATTN_W, HEAD_DIM), _fold(KV_W, HEAD_DIM))

    def body(q_ref, kp_ref, kc_ref, vp_ref, vc_ref, b_ref, skv_ref, qw_ref, kw_ref, eq_ref, eq3_ref, ek_ref, ek3_ref,
             fq_ref, fk_ref, o_ref, lse_ref, do_ref,
             dq_ref, dkv_ref, dss_ref, gqw_ref, gkw_ref, gsk_ref, ckn, cv, dqn_s, dkn_s, dv_s, gq_x, gk_x):
        i = pl.program_id(0)
        kw, ek, ek3 = kw_ref[...], ek_ref[...], ek3_ref[...]

        @pl.when(i == 0)
        def _():
            for ref in (ckn, cv, dss_ref, gq_x, gk_x, gsk_ref):
                ref[...] = jnp.zeros_like(ref)

        @pl.when(i < nb)
        def _():
            qw, eq, eq3 = qw_ref[...], eq_ref[...], eq3_ref[...]
            qf = q_ref[...]
            qnf, rq_x = _heads_norm(qf, qw, eq, eq3)
            qn = _bf(qnf * (HEAD_DIM ** -0.5))
            kf = jnp.concatenate([kp_ref[...], kc_ref[...]], axis=0)
            knf, rk_x = _heads_norm(kf, kw, ek, ek3)
            kn = _bf(knf)
            vv = _bf(jnp.concatenate([vp_ref[...], vc_ref[...]], axis=0))
            d_of = do_ref[...]
            d_ob = _bf(d_of)
            lse_all = lse_ref[...]
            delta = _group_sum(d_of * o_ref[...], eq)
            gsk_ref[...] += jnp.sum(-jnp.exp(skv_ref[...] - lse_all) * delta, axis=0, keepdims=True)
            kss = [slice(hk * HEAD_DIM, (hk + 1) * HEAD_DIM) for hk in range(KV_HEADS)]
            qgs = [_stack_heads(qn, hk) for hk in range(KV_HEADS)]
            d_ogs = [_stack_heads(d_ob, hk) for hk in range(KV_HEADS)]
            scs = [_dot_nt(qgs[hk], kn[:, kss[hk]]) + b_ref[0, hk * GRP:(hk + 1) * GRP].reshape(GRP * BLOCK, 2 * BLOCK)
                   for hk in range(KV_HEADS)]
            dps = [_dot_nt(d_ogs[hk], vv[:, kss[hk]]) for hk in range(KV_HEADS)]
            ps = [jnp.exp(scs[hk] - _stack_cols(lse_all, hk)) for hk in range(KV_HEADS)]
            dss = [ps[hk] * (dps[hk] - _stack_cols(delta, hk)) for hk in range(KV_HEADS)]
            pbs = [_bf(p) for p in ps]
            dsbs = [_bf(ds) for ds in dss]
            for hk in range(KV_HEADS):
                dss_ref[hk * GRP:(hk + 1) * GRP] += dss[hk].reshape(GRP, BLOCK, 2 * BLOCK)
            for hk in range(KV_HEADS):
                dv_s[:, kss[hk]] = _dot_tn(pbs[hk], d_ogs[hk])
                dkn_s[:, kss[hk]] = _dot_tn(dsbs[hk], qgs[hk])
            dqns = [_dot(dsbs[hk], kn[:, kss[hk]]) * (HEAD_DIM ** -0.5) for hk in range(KV_HEADS)]
            for hk in range(KV_HEADS):
                for g in range(GRP):
                    h = hk * GRP + g
                    dqn_s[:, h * HEAD_DIM:(h + 1) * HEAD_DIM] = dqns[hk][g * BLOCK:(g + 1) * BLOCK]
            dq, gq = _heads_norm_bwd(qf, rq_x, qw, dqn_s[...], eq, eq3)
            dq_ref[...] = _bf(dq)
            gq_x[...] += gq
            dk, gk = _heads_norm_bwd(kf[:BLOCK], rk_x[:BLOCK], kw, ckn[...] + dkn_s[0:BLOCK, :], ek, ek3)
            dkv_ref[:, :KV_W] = _bf(dk)
            gk_x[...] += gk
            dkv_ref[:, KV_W:] = _bf(cv[...] + dv_s[0:BLOCK, :])
            ckn[...] = dkn_s[BLOCK:2 * BLOCK, :]
            cv[...] = dv_s[BLOCK:2 * BLOCK, :]

        @pl.when(i == nb)
        def _():
            kc = kc_ref[...]
            dk, gk = _heads_norm_bwd(kc, _heads_norm(kc, kw, ek, ek3)[1], kw, ckn[...], ek, ek3)
            dkv_ref[:, :KV_W] = _bf(dk)
            dkv_ref[:, KV_W:] = _bf(cv[...])
            gqw_ref[...] = _group_sum(jnp.broadcast_to(gq_x[...], (8, ATTN_W)), fq_ref[...])[0:1]
            gkw_ref[...] = _group_sum(jnp.broadcast_to(gk_x[...] + gk, (8, KV_W)), fk_ref[...])[0:1]

    last = nb - 1
    cur = lambda w, col=0: pl.BlockSpec((BLOCK, w), lambda i: (jnp.minimum(i, last), col))
    prev = lambda w, col=0: pl.BlockSpec((BLOCK, w), lambda i: (jnp.maximum(jnp.minimum(i, last) - 1, 0), col))
    late = lambda w: pl.BlockSpec((BLOCK, w), lambda i: (jnp.maximum(i - 1, 0), 0))
    sd = jax.ShapeDtypeStruct
    return pl.pallas_call(
        body, name="attn_bwd", grid=(nb + 1,),
        in_specs=[cur(ATTN_W), prev(KV_W, 0), cur(KV_W, 0), prev(KV_W, 1), cur(KV_W, 1),
                  pl.BlockSpec((1, ATTN_HEADS, BLOCK, 2 * BLOCK), lambda i: (jnp.minimum(i, 1), 0, 0, 0)),
                  _full((1, ATTN_HEADS))]
                 + [_full(c.shape) for c in consts + folds] + [cur(ATTN_W), cur(ATTN_HEADS), cur(ATTN_W)],
        out_specs=[cur(ATTN_W), late(2 * KV_W),
                   pl.BlockSpec((ATTN_HEADS, BLOCK, 2 * BLOCK), lambda i: (0, 0, 0)),
                   _full((1, HEAD_DIM)), _full((1, HEAD_DIM)), _full((1, ATTN_HEADS))],
        out_shape=[sd((s, ATTN_W), BF), sd((s, 2 * KV_W), BF),
                   sd((ATTN_HEADS, BLOCK, 2 * BLOCK), F32), sd((1, HEAD_DIM), F32), sd((1, HEAD_DIM), F32),
                   sd((1, ATTN_HEADS), F32)],
        scratch_shapes=[pltpu.VMEM((BLOCK, KV_W), F32), pltpu.VMEM((BLOCK, KV_W), F32),
                        pltpu.VMEM((BLOCK, ATTN_W), F32), pltpu.VMEM((2 * BLOCK, KV_W), F32),
                        pltpu.VMEM((2 * BLOCK, KV_W), F32), pltpu.VMEM((1, ATTN_W), F32), pltpu.VMEM((1, KV_W), F32)],
        compiler_params=_params(dimension_semantics=("arbitrary",)),
    )(q, kv, kv, kv, kv, bias, sinks, *consts, *folds, o_att, lse, d_o)


def _ssd_bwd(xbc, conv_all, dt_raw, conv_w, dt_bias, a_log, dsk_x, e_mat, e3t, hprev_all, dy_all):
    s = xbc.shape[0]
    nc = s // BLOCK
    ch = 1
    rows = ch * BLOCK
    nsteps = nc // ch
    gw = SSM_R * SSM_P
    b0, c0 = SSM_W, SSM_W + SSM_G * SSM_N

    def body(x_ref, conv_ref, dtr_ref, cw_ref, dtb_ref, alog_ref, dsk_ref, e_ref, e3_ref, hp_ref, dy_ref,
             dx_ref, ddt_ref, gcw_ref, gcb_ref, gdtb_ref, galog_ref, gdsk_ref,
             dh, nhead, gdskx, dxdt_s, dbc_s, dxd_s):
        def chunk_bwd(j):
            rs = slice(j * BLOCK, (j + 1) * BLOCK)
            conv = conv_ref[rs, :]
            sg, xact, u, dt, a, trilb, acum, dt_x, acum_x = _ssd_common(conv, dtr_ref[rs, :], dtb_ref, alog_ref, e3_ref)
            xs = xact[:, :SSM_W]
            acum_t = acum.T
            ea_x = jnp.exp(acum_x)
            last_x = acum_x[BLOCK - 1:BLOCK, :]
            dte_x = jnp.exp(last_x - acum_x)
            cd_x = jnp.exp(last_x)
            xdt = xs * dt_x
            xw = xdt * dte_x
            hprev = hp_ref[j]
            dhn = dh[...]
            dy = dy_ref[rs, :]
            gdskx[...] += jnp.sum(dy * xs, axis=0, keepdims=True)
            dyea = dy * ea_x
            lane = lax.broadcasted_iota(jnp.int32, (BLOCK, SSM_HEADS), 1)
            dacum = jnp.zeros((BLOCK, SSM_HEADS), F32)
            dacc_x, dlast_x = [], []
            sls = [slice(g * gw, (g + 1) * gw) for g in range(SSM_G)]
            bgs = [_bf(xact[:, b0 + g * SSM_N:b0 + (g + 1) * SSM_N]) for g in range(SSM_G)]
            cgs = [_bf(xact[:, c0 + g * SSM_N:c0 + (g + 1) * SSM_N]) for g in range(SSM_G)]
            hpgs = [_bf(hprev[:, sl]) for sl in sls]
            dhgs = [_bf(dhn[:, sl]) for sl in sls]
            dyeags = [_bf(dyea[:, sl]) for sl in sls]
            xwgs = [_bf(xw[:, sl]) for sl in sls]
            xdt_b, dy_b = _bf(xdt), _bf(dy)
            low_half = lax.broadcasted_iota(jnp.int32, (BLOCK, 2 * SSM_P), 1) < SSM_P
            cbs = [_dot_nt(cgs[g], bgs[g]) for g in range(SSM_G)]
            gmats = [_dot(cgs[g], hpgs[g]) for g in range(SSM_G)]
            dxws = [_dot(bgs[g], dhgs[g]) for g in range(SSM_G)]
            dcgs = [_dot_nt(dyeags[g], hpgs[g]) for g in range(SSM_G)]
            dbgs = [_dot_nt(xwgs[g], dhgs[g]) for g in range(SSM_G)]
            for g in range(SSM_G):
                sl = sls[g]
                dh[:, sl] = dhn[:, sl] * cd_x[:, sl] + _dot_tn(cgs[g], dyeags[g])
                dxdt_s[:, sl] = dxws[g] * dte_x[:, sl]
                dacc_x.append(dy[:, sl] * gmats[g] * ea_x[:, sl] - dxws[g] * xw[:, sl])
                dlast_x.append(jnp.sum(dxws[g] * xw[:, sl], axis=0, keepdims=True)
                               + jnp.sum(dhn[:, sl] * hprev[:, sl], axis=0, keepdims=True) * cd_x[:, sl])
            for g in range(SSM_G):
                bg, cg, cb, dbg, dcg = bgs[g], cgs[g], cbs[g], dbgs[g], dcgs[g]
                hss = [slice((g * SSM_R + r) * SSM_P, (g * SSM_R + r + 1) * SSM_P) for r in range(SSM_R)]
                lms = [jnp.exp(jnp.where(trilb, acum[:, g * SSM_R + r:g * SSM_R + r + 1]
                                         - acum_t[g * SSM_R + r:g * SSM_R + r + 1, :], -1e30)) for r in range(SSM_R)]
                mms = [cb * lm for lm in lms]
                mmbs = [_bf(mm) for mm in mms]
                dms = []
                for r in range(0, SSM_R, 2):
                    pair = slice(hss[r].start, hss[r + 1].stop)
                    xp, dyp = xdt_b[:, pair], dy_b[:, pair]
                    dmp = _dot_nt(dyp, jnp.concatenate([jnp.where(low_half, xp, 0), jnp.where(low_half, 0, xp)], axis=0))
                    dms += [dmp[:, :BLOCK], dmp[:, BLOCK:]]
                    dxd_s[:, pair] = _dot_tn(jnp.concatenate([mmbs[r], mmbs[r + 1]], axis=0),
                                             jnp.concatenate([jnp.where(low_half, dyp, 0), jnp.where(low_half, 0, dyp)], axis=0))
                dcb = sum(dms[r] * lms[r] for r in range(SSM_R))
                wms = [dms[r] * mms[r] for r in range(SSM_R)]
                antis = [wm - wm.T for wm in wms]
                for r in range(SSM_R):
                    dacum = dacum + _group_sum(antis[r], (lane == g * SSM_R + r).astype(BF))
                dcbb = _bf(dcb)
                dbc_s[:, g * SSM_N:(g + 1) * SSM_N] = dbg + _dot_tn(dcbb, cg)
                dbc_s[:, SSM_G * SSM_N + g * SSM_N:SSM_G * SSM_N + (g + 1) * SSM_N] = dcg + _dot(dcbb, bg)
            dxdt = dxdt_s[...] + dxd_s[...]
            dxs = dy * dsk_ref[...] + dxdt * dt_x
            red = _group_sum(jnp.concatenate(
                [dxdt * xs, jnp.concatenate(dacc_x, axis=1),
                 jnp.broadcast_to(jnp.concatenate(dlast_x, axis=1), (8, SSM_W))], axis=0), e_ref[...])
            row = lax.broadcasted_iota(jnp.int32, (BLOCK, SSM_HEADS), 0)
            dacum = dacum + red[BLOCK:2 * BLOCK] + jnp.where(row == BLOCK - 1, red[2 * BLOCK:2 * BLOCK + 1], 0.0)
            ddta = _exact_left(_triu().astype(BF), dacum)
            ddt = red[:BLOCK] + ddta * a
            galog_ref[...] += jnp.sum(ddta * dt, axis=0, keepdims=True) * a
            du = ddt * _sig(u)
            ddt_ref[rs, :] = _bf(du)
            gdtb_ref[...] += jnp.sum(du, axis=0, keepdims=True)
            dconv = jnp.concatenate([dxs, dbc_s[...]], axis=1) * _dsilu(conv, sg)
            gcb_ref[...] += jnp.sum(dconv, axis=0, keepdims=True)
            ext2 = jnp.concatenate([dconv, nhead[...]], axis=0)
            ahead = [pltpu.roll(ext2, BLOCK + 8 - (CONV_K - 1 - j), axis=0)[0:BLOCK] if j < CONV_K - 1 else dconv
                     for j in range(CONV_K)]
            dx_ref[rs, :] = _bf(sum(ahead[j] * cw_ref[j:j + 1, :] for j in range(CONV_K)))
            xraw = x_ref[rs, :]
            gcw_ref[...] += jnp.concatenate([jnp.sum(ahead[j] * xraw, axis=0, keepdims=True) for j in range(CONV_K)], axis=0)
            nhead[...] = dconv[0:8]

        i = pl.program_id(0)

        @pl.when(i == 0)
        def _():
            for ref in (dh, nhead, gdskx, gcw_ref, gcb_ref, gdtb_ref, galog_ref, gdsk_ref):
                ref[...] = jnp.zeros_like(ref)

        for j in reversed(range(ch)):
            chunk_bwd(j)

        @pl.when(i == nsteps - 1)
        def _():
            gdsk_ref[...] = _group_sum(jnp.broadcast_to(gdskx[...], (8, SSM_W)), e_ref[...])[0:1]

    chunk = lambda w: pl.BlockSpec((rows, w), lambda i: (nsteps - 1 - i, 0))
    sd = jax.ShapeDtypeStruct
    return pl.pallas_call(
        body, name="ssd_bwd", grid=(nsteps,),
        in_specs=[chunk(XBC_W), chunk(XBC_W),
                  chunk(SSM_HEADS), _full((CONV_K, XBC_W)), _full((1, SSM_HEADS)),
                  _full((1, SSM_HEADS)), _full((1, SSM_W)), _full((SSM_W, SSM_HEADS)), _full((3 * SSM_HEADS, SSM_W)),
                  pl.BlockSpec((ch, SSM_N, SSM_W), lambda i: (nsteps - 1 - i, 0, 0)), chunk(SSM_W)],
        out_specs=[chunk(XBC_W), chunk(SSM_HEADS), _full((CONV_K, XBC_W)), _full((1, XBC_W)),
                   _full((1, SSM_HEADS)), _full((1, SSM_HEADS)), _full((1, SSM_HEADS))],
        out_shape=[sd((s, XBC_W), BF), sd((s, SSM_HEADS), BF), sd((CONV_K, XBC_W), F32), sd((1, XBC_W), F32),
                   sd((1, SSM_HEADS), F32), sd((1, SSM_HEADS), F32), sd((1, SSM_HEADS), F32)],
        scratch_shapes=[pltpu.VMEM((SSM_N, SSM_W), F32), pltpu.VMEM((8, XBC_W), F32),
                        pltpu.VMEM((1, SSM_W), F32), pltpu.VMEM((BLOCK, SSM_W), F32),
                        pltpu.VMEM((BLOCK, 2 * SSM_G * SSM_N), F32), pltpu.VMEM((BLOCK, SSM_W), F32)],
        compiler_params=_params(dimension_semantics=("arbitrary",)),
    )(xbc, conv_all, dt_raw, conv_w, dt_bias, a_log, dsk_x, e_mat, e3t, hprev_all, dy_all)


def _dh(x, dout, norm_w, scale, dsegs, w_t, tm=256):
    s = x.shape[0]

    def body(x_ref, dout_ref, nw_ref, sc_ref, *rest):
        d_refs, w_hbm = rest[:NSEG], rest[NSEG]
        gx_ref, dshift_ref, dscale_ref, gnw_ref = rest[NSEG + 1:NSEG + 5]
        w_vm, sem = rest[NSEG + 5], rest[NSEG + 6]
        first = pl.program_id(0) == 0
        cps = [pltpu.make_async_copy(w_hbm.at[SEG_OFF[j]:SEG_OFF[j + 1], :], w_vm.at[SEG_OFF[j]:SEG_OFF[j + 1], :], sem.at[j])
               for j in range(NSEG)]

        def tile(waiting):
            dh = None
            for j in range(NSEG):
                if waiting:
                    cps[j].wait()
                part = _dot(d_refs[j][...], w_vm[SEG_OFF[j]:SEG_OFF[j + 1], :])
                dh = part if dh is None else dh + part
            xv = x_ref[...]
            r = lax.rsqrt(jnp.mean(xv * xv, axis=-1, keepdims=True) + EPS)
            xn = xv * r
            nw = nw_ref[...]
            sc1 = 1.0 + sc_ref[...]
            dshift_ref[...] += jnp.sum(dh, axis=0, keepdims=True)
            dhxn = jnp.sum(dh * xn, axis=0, keepdims=True)
            dscale_ref[...] += dhxn * nw
            gnw_ref[...] += dhxn * sc1
            dxn = dh * (nw * sc1)
            gx_ref[...] = dout_ref[...] + r * (dxn - xn * jnp.mean(xn * dxn, axis=-1, keepdims=True))

        @pl.when(first)
        def _():
            for cp in cps:
                cp.start()
            for ref in (dshift_ref, dscale_ref, gnw_ref):
                ref[...] = jnp.zeros_like(ref)
            tile(True)

        @pl.when(jnp.logical_not(first))
        def _():
            tile(False)

    vec = _full((1, D_MODEL))
    sd = jax.ShapeDtypeStruct
    return pl.pallas_call(
        body, name="dh", grid=(s // tm,),
        in_specs=[_rows(tm, D_MODEL), _rows(tm, D_MODEL), vec, vec] + [_rows(tm, w) for w in SEG_W] + [ANY],
        out_specs=[_rows(tm, D_MODEL), vec, vec, vec],
        out_shape=[sd((s, D_MODEL), F32), sd((1, D_MODEL), F32), sd((1, D_MODEL), F32), sd((1, D_MODEL), F32)],
        scratch_shapes=[pltpu.VMEM((IN_W, D_MODEL), BF), pltpu.SemaphoreType.DMA((NSEG,))],
        compiler_params=_params(dimension_semantics=("arbitrary",)),
    )(x, dout, norm_w, scale, *dsegs, w_t)


def _gw_seg(h, dseg, name, tm=1024):
    s, w = dseg.shape
    tn = min(w, 1024)
    tm = min(tm, s)
    nm = s // tm

    def body(h_ref, d_ref, o_ref, acc):
        m = pl.program_id(1)

        @pl.when(m == 0)
        def _():
            acc[...] = jnp.zeros_like(acc)

        acc[...] += _dot_tn(d_ref[...], h_ref[...])

        @pl.when(m == nm - 1)
        def _():
            o_ref[...] = _bf(acc[...])

    return pl.pallas_call(
        body, name=name, grid=(w // tn, nm),
        in_specs=[pl.BlockSpec((tm, D_MODEL), lambda n, m: (m, 0)), pl.BlockSpec((tm, tn), lambda n, m: (m, n))],
        out_specs=pl.BlockSpec((tn, D_MODEL), lambda n, m: (n, 0)),
        out_shape=jax.ShapeDtypeStruct((w, D_MODEL), BF),
        scratch_shapes=[pltpu.VMEM((tn, D_MODEL), F32)],
        compiler_params=_params(dimension_semantics=("arbitrary", "arbitrary")),
    )(h, dseg)


def _gw_in(h, dsegs):
    return [_gw_seg(h, d, "gw_in_%d" % j) for j, d in enumerate(dsegs)]


def _local_step(x, tgt, shift, scale, gate, w_t, rows_fn, norm_w, qnw, knw, rel_bias, sinks,
                conv_w, conv_b, dt_bias, a_log, d_skip, ssm_nw, after_mid=None, after_gw=None):
    oh_t = _bucket_onehot_t()
    bias = _masked_bias(_bias_dense(rel_bias.T, oh_t).reshape(ATTN_HEADS, BLOCK, 2 * BLOCK))
    *segs, h = _inproj(x, norm_w, scale, shift, w_t)
    q, kv, zam, xbc, dtr, gab = segs
    consts = _attn_consts(qnw, knw)
    o_att, lse = _attn_fwd(q, kv, bias, sinks, consts)
    e_mat, e3t = _membership(SSM_W, SSM_P, SSM_HEADS)
    dsk_x = jnp.repeat(d_skip, SSM_P, axis=1)
    ypre, hprev, conv = _ssd_fwd(xbc, dtr, conv_w, conv_b, dt_bias, a_log, dsk_x, e3t)
    (dout, d_o, dzam, dyp, dgab, yag, dy_a, yn, dy_b, merged, dob, g_ssm_nw, dgate, loss) = _mid(
        x, tgt, o_att, zam, ypre, gab, gate, ssm_nw, rows_fn(ypre))
    g_wap = _gw_seg(dy_a, yag, "gw_attn_proj")
    g_wsp = _gw_seg(dy_b, yn, "gw_ssm_proj")
    g_wout = _gw_seg(dob, merged, "gw_out")
    zero = after_mid(g_wap, g_wsp, g_wout) if after_mid is not None else 0.0
    dq, dkv, dss, g_qnw, g_knw, g_sinks = _attn_bwd(q, kv, bias, sinks + zero, consts, o_att, lse, d_o)
    g_rel = _bias_grad(dss.reshape(ATTN_HEADS, BLOCK * 2 * BLOCK), oh_t).T
    dxbc, ddt, g_cw, g_cb, g_dtb, g_alog, g_dsk = _ssd_bwd(
        xbc, conv, dtr, conv_w, dt_bias, a_log, dsk_x, e_mat, e3t, hprev, dyp)
    dsegs = (dq, dkv, dzam, dxbc, ddt, dgab)
    g_ws = _gw_in(h, dsegs)
    zero = after_gw(g_ws) if after_gw is not None else 0.0
    gx, dshift, dscale, g_nw = _dh(x, dout, norm_w + zero, scale, dsegs, w_t)
    return dict(loss=loss, grad_x=gx, dmod=jnp.concatenate([dshift, dscale, dgate], axis=1), g_ws=g_ws,
                g_wap=g_wap, g_wsp=g_wsp, g_wout=g_wout, g_norm_w=g_nw, g_qnw=g_qnw, g_knw=g_knw, g_rel=g_rel,
                g_sinks=g_sinks, g_conv_w=g_cw, g_conv_b=g_cb, g_dt_bias=g_dtb, g_a_log=g_alog, g_d_skip=g_dsk,
                g_ssm_nw=g_ssm_nw)


def _me():
    return lax.axis_index("x"), lax.axis_index("y"), lax.axis_index("c")


def _flip(v, bit):
    return 1 - v if bit else v


def _ag_direct(v, name):
    def body(v_ref, out_ref, send_sems, recv_sems, local_sem):
        x, y, c = _me()
        me = 4 * x + 2 * y + c
        mine = pltpu.make_async_copy(v_ref, out_ref.at[me], local_sem)
        mine.start()
        peers = [(_flip(x, k >> 2 & 1), _flip(y, k >> 1 & 1), _flip(c, k & 1)) for k in range(1, N_DEV)]
        sends = [pltpu.make_async_remote_copy(
            src_ref=v_ref, dst_ref=out_ref.at[me], send_sem=send_sems.at[j], recv_sem=recv_sems.at[j],
            device_id=p, device_id_type=MESH) for j, p in enumerate(peers)]
        for cp in sends:
            cp.start()
        for j, (px, py, pc) in enumerate(peers):
            pltpu.make_async_remote_copy(
                src_ref=v_ref, dst_ref=out_ref.at[4 * px + 2 * py + pc], send_sem=send_sems.at[j],
                recv_sem=recv_sems.at[j], device_id=(px, py, pc), device_id_type=MESH).wait_recv()
        for cp in sends:
            cp.wait_send()
        mine.wait()

    vm = pl.BlockSpec(memory_space=pltpu.VMEM)
    return pl.pallas_call(
        body, name=name, out_shape=jax.ShapeDtypeStruct((N_DEV,) + v.shape, v.dtype),
        in_specs=[vm], out_specs=vm,
        scratch_shapes=[pltpu.SemaphoreType.DMA((N_DEV - 1,)), pltpu.SemaphoreType.DMA((N_DEV - 1,)),
                        pltpu.SemaphoreType.DMA],
        compiler_params=_params(),
    )(v)


def _ag_two_level(v, name):
    def body(v_ref, out_ref, token, send_sems, recv_sems, local_sem):
        token[...] = jnp.zeros_like(token)
        x, y, c = _me()
        me, sibling = (x, y, c), (x, y, 1 - c)
        chips = [(1 - x, y), (x, 1 - y), (1 - x, 1 - y)]

        def slot(px, py, pc):
            return out_ref.at[4 * px + 2 * py + pc]

        def copy(k, block, to, src=None):
            return pltpu.make_async_remote_copy(
                src_ref=slot(*block) if src is None else src, dst_ref=slot(*block),
                send_sem=send_sems.at[k], recv_sem=recv_sems.at[k], device_id=to, device_id_type=MESH)

        mine = pltpu.make_async_copy(v_ref, slot(*me), local_sem)
        mine.start()
        first = [copy(0, me, sibling, src=v_ref)]
        first += [copy(1 + j, me, (*chip, c), src=v_ref) for j, chip in enumerate(chips)]
        for cp in first:
            cp.start()
        passed = [copy(4 + j, (*chip, c), sibling) for j, chip in enumerate(chips)]
        for j, chip in enumerate(chips):
            copy(1 + j, (*chip, c), me).wait_recv()
            passed[j].start()
        copy(0, sibling, me).wait_recv()
        for j, chip in enumerate(chips):
            copy(4 + j, (*chip, 1 - c), me).wait_recv()
        for cp in first + passed:
            cp.wait_send()
        mine.wait()

    out, token = pl.pallas_call(
        body, name=name,
        out_shape=(jax.ShapeDtypeStruct((N_DEV,) + v.shape, v.dtype), jax.ShapeDtypeStruct((8, 128), v.dtype)),
        in_specs=[ANY], out_specs=(ANY, pl.BlockSpec(memory_space=pltpu.VMEM)),
        scratch_shapes=[pltpu.SemaphoreType.DMA((7,)), pltpu.SemaphoreType.DMA((7,)), pltpu.SemaphoreType.DMA],
        compiler_params=_params(),
    )(v)
    return out, token[0:1, 0:1]


HBM = pl.BlockSpec(memory_space=pltpu.HBM)
SEM = pl.BlockSpec(memory_space=pltpu.SEMAPHORE)
EFFECT = pltpu.SideEffectType.DATAFLOW_SIDE_EFFECTING


def _peers(x, y, c):
    return [(_flip(x, k >> 2 & 1), _flip(y, k >> 1 & 1), _flip(c, k & 1)) for k in range(1, N_DEV)]


def _exchange_start(src, land, gather, name):
    def body(src_ref, land_ref, send_sems, recv_sems, src_thru, land_thru, token):
        x, y, c = _me()
        me = 4 * x + 2 * y + c
        for j, (px, py, pc) in enumerate(_peers(x, y, c)):
            pltpu.make_async_remote_copy(
                src_ref=src_ref if gather else src_ref.at[4 * px + 2 * py + pc], dst_ref=land_ref.at[me],
                send_sem=send_sems.at[j], recv_sem=recv_sems.at[j], device_id=(px, py, pc), device_id_type=MESH).start()
        token[...] = jnp.zeros_like(token)

    sems = pltpu.SemaphoreType.DMA((N_DEV - 1,))
    out = pl.pallas_call(
        body, name=name,
        out_shape=(sems, sems, pltpu.HBM(src.shape, src.dtype), pltpu.HBM(land.shape, land.dtype),
                   jax.ShapeDtypeStruct((8, 128), F32)),
        in_specs=(HBM, HBM), out_specs=(SEM, SEM, HBM, HBM, pl.BlockSpec(memory_space=pltpu.VMEM)),
        input_output_aliases={0: 2, 1: 3},
        compiler_params=pltpu.CompilerParams(has_side_effects=EFFECT),
    )(pltpu.with_memory_space_constraint(src, pltpu.HBM), pltpu.with_memory_space_constraint(land, pltpu.HBM))
    return out[:4], out[4][0, 0]


def _exchange_wait(started, after, gather, name):
    send_sems, recv_sems, src_thru, land_thru = started

    def body(src_ref, land_ref, send_sems, recv_sems, after_ref, src_dead, got_ref):
        x, y, c = _me()
        for j, (px, py, pc) in enumerate(_peers(x, y, c)):
            pid = 4 * px + 2 * py + pc
            cp = pltpu.make_async_remote_copy(
                src_ref=src_ref if gather else src_ref.at[pid], dst_ref=land_ref.at[pid],
                send_sem=send_sems.at[j], recv_sem=recv_sems.at[j], device_id=(px, py, pc), device_id_type=MESH)
            cp.wait_send()
            cp.wait_recv()

    return pl.pallas_call(
        body, name=name,
        out_shape=(pltpu.HBM(src_thru.shape, src_thru.dtype), pltpu.HBM(land_thru.shape, land_thru.dtype)),
        in_specs=(HBM, HBM, SEM, SEM, ANY), out_specs=(HBM, HBM), input_output_aliases={0: 0, 1: 1},
        compiler_params=pltpu.CompilerParams(has_side_effects=EFFECT),
    )(src_thru, land_thru, send_sems, recv_sems, after)[1]


def _silu(a):
    return a * _sig(a)


def _mod_piece(c_all, w_ada, b_piece):
    def body(c_ref, w_ref, b_ref, o_ref):
        o_ref[...] = _dot(_bf(_silu(c_ref[...])), _bf(w_ref[...])) + b_ref[...]

    return pl.pallas_call(
        body, name="mod_piece", out_shape=jax.ShapeDtypeStruct((c_all.shape[0], w_ada.shape[1]), F32),
        compiler_params=_params(),
    )(c_all, w_ada, b_piece)


def _gw_ada(c_all, dmod_piece):
    def body(c_ref, d_ref, o_ref):
        o_ref[...] = _dot_tn(_bf(_silu(c_ref[...])), _bf(d_ref[...]))

    return pl.pallas_call(
        body, name="gw_ada", out_shape=jax.ShapeDtypeStruct((c_all.shape[1], dmod_piece.shape[1]), F32),
        compiler_params=_params(),
    )(c_all, dmod_piece)


def _adam(parts, w, m, v, name):
    k, r, n = parts.shape
    if r <= 256 or r % 256 == 0:
        tr, tn = min(r, 256), n
    else:
        tr, tn = r, 256
    assert r % tr == 0 and n % tn == 0

    def body(p_ref, w_ref, m_ref, v_ref, g_ref, d_ref, nm_ref, nv_ref):
        g = p_ref[0].astype(F32)
        for j in range(1, k):
            g = g + p_ref[j].astype(F32)
        g_ref[...] = g
        d_ref[...], nm_ref[...], nv_ref[...] = _adam_math(g, w_ref[...], m_ref[...], v_ref[...])

    blk = pl.BlockSpec((tr, tn), lambda i, j: (i, j))
    return pl.pallas_call(
        body, name=name, grid=(r // tr, n // tn),
        in_specs=[pl.BlockSpec((k, tr, tn), lambda i, j: (0, i, j)), blk, blk, blk],
        out_specs=[blk, blk, blk, blk],
        out_shape=[jax.ShapeDtypeStruct((r, n), F32)] * 4,
        compiler_params=_params(dimension_semantics=("arbitrary", "arbitrary")),
    )(parts, w, m, v)


def _adam_math(g, w, m, v):
    m_new = ADAM_B1 * m + (1.0 - ADAM_B1) * g
    v_new = ADAM_B2 * v + (1.0 - ADAM_B2) * jnp.square(g)
    m_hat = m_new / (1.0 - ADAM_B1 ** ADAM_STEP)
    v_hat = v_new / (1.0 - ADAM_B2 ** ADAM_STEP)
    return -ADAM_LR * (m_hat / (jnp.sqrt(v_hat) + ADAM_EPS) + ADAM_WD * w), m_new, v_new


_SMALL = (("b_ada", 3 * D_MODEL), ("norm_w", D_MODEL), ("q_norm_w", HEAD_DIM), ("k_norm_w", HEAD_DIM),
          ("rel_bias", REL_BUCKETS * ATTN_HEADS), ("sinks", ATTN_HEADS), ("conv_b", XBC_W), ("dt_bias", SSM_HEADS),
          ("a_log", SSM_HEADS), ("d_skip", SSM_HEADS), ("ssm_norm_w", SSM_W))
_SLOT = tuple(-(-n // 128) * 128 for _, n in _SMALL)
_SLOT_OFF = tuple(int(o) for o in np.cumsum((0,) + _SLOT))
_LOSS_OFF = _SLOT_OFF[-1]
_CW_OFF = _LOSS_OFF + 128
_PACK_N = _CW_OFF + CONV_K * XBC_W


def _pack_partials(small, loss, g_conv_w):
    parts = []
    for (name, n), slot in zip(_SMALL, _SLOT):
        parts.append(small[name].reshape(1, n))
        if slot > n:
            parts.append(jnp.zeros((1, slot - n), F32))
    parts += [loss.reshape(1, 1), jnp.zeros((1, 127), F32), g_conv_w.reshape(1, CONV_K * XBC_W)]
    return jnp.concatenate(parts, axis=1)


def _adam_small(pack_all, w, m, v):
    names = [name for name, _ in _SMALL]

    def body(p_ref, *rest):
        ins, outs = rest[:3 * len(names)], rest[3 * len(names):]

        def total(off, n):
            g = p_ref[0, :, off:off + n]
            for d in range(1, N_DEV):
                g = g + p_ref[d, :, off:off + n]
            return g

        for j, (name, n) in enumerate(_SMALL):
            g = total(_SLOT_OFF[j], n)
            delta, m_new, v_new = _adam_math(g, ins[3 * j][...], ins[3 * j + 1][...], ins[3 * j + 2][...])
            outs[4 * j][...] = g
            outs[4 * j + 1][...] = delta
            outs[4 * j + 2][...] = m_new
            outs[4 * j + 3][...] = v_new
        outs[-1][...] = total(_LOSS_OFF, 1)

    flat = []
    for name, n in _SMALL:
        flat += [w[name].reshape(1, n), m[name].reshape(1, n), v[name].reshape(1, n)]
    out_shape = [jax.ShapeDtypeStruct((1, n), F32) for _, n in _SMALL for _ in range(4)] + [jax.ShapeDtypeStruct((1, 1), F32)]
    out = pl.pallas_call(body, name="adam_small", out_shape=out_shape, compiler_params=_params())(pack_all, *flat)
    res = {name: [out[4 * j + t].reshape(w[name].shape) for t in range(4)] for j, name in enumerate(names)}
    return res, out[-1]


WEIGHTS = ("w_ada", "b_ada", "norm_w", "w_in", "q_norm_w", "k_norm_w", "rel_bias", "sinks", "conv_w", "conv_b",
           "dt_bias", "a_log", "d_skip", "ssm_norm_w", "w_attn_proj", "w_ssm_proj", "w_out")


def kernel(x, c, w_ada, b_ada, norm_w, w_in, q_norm_w, k_norm_w, rel_bias, sinks, conv_w, conv_b, dt_bias, a_log, d_skip, ssm_norm_w, w_attn_proj, w_ssm_proj, w_out, loss_target, m_w_ada, m_b_ada, m_norm_w, m_w_in, m_q_norm_w, m_k_norm_w, m_rel_bias, m_sinks, m_conv_w, m_conv_b, m_dt_bias, m_a_log, m_d_skip, m_ssm_norm_w, m_w_attn_proj, m_w_ssm_proj, m_w_out, v_w_ada, v_b_ada, v_norm_w, v_w_in, v_q_norm_w, v_k_norm_w, v_rel_bias, v_sinks, v_conv_w, v_conv_b, v_dt_bias, v_a_log, v_d_skip, v_ssm_norm_w, v_w_attn_proj, v_w_ssm_proj, v_w_out):
    w = dict(w_ada=w_ada, b_ada=b_ada, norm_w=norm_w, w_in=w_in, q_norm_w=q_norm_w, k_norm_w=k_norm_w,
             rel_bias=rel_bias, sinks=sinks, conv_w=conv_w, conv_b=conv_b, dt_bias=dt_bias, a_log=a_log,
             d_skip=d_skip, ssm_norm_w=ssm_norm_w, w_attn_proj=w_attn_proj, w_ssm_proj=w_ssm_proj, w_out=w_out)
    m = dict(w_ada=m_w_ada, b_ada=m_b_ada, norm_w=m_norm_w, w_in=m_w_in, q_norm_w=m_q_norm_w, k_norm_w=m_k_norm_w,
             rel_bias=m_rel_bias, sinks=m_sinks, conv_w=m_conv_w, conv_b=m_conv_b, dt_bias=m_dt_bias, a_log=m_a_log,
             d_skip=m_d_skip, ssm_norm_w=m_ssm_norm_w, w_attn_proj=m_w_attn_proj, w_ssm_proj=m_w_ssm_proj, w_out=m_w_out)
    v = dict(w_ada=v_w_ada, b_ada=v_b_ada, norm_w=v_norm_w, w_in=v_w_in, q_norm_w=v_q_norm_w, k_norm_w=v_k_norm_w,
             rel_bias=v_rel_bias, sinks=v_sinks, conv_w=v_conv_w, conv_b=v_conv_b, dt_bias=v_dt_bias, a_log=v_a_log,
             d_skip=v_d_skip, ssm_norm_w=v_ssm_norm_w, w_attn_proj=v_w_attn_proj, w_ssm_proj=v_w_ssm_proj, w_out=v_w_out)
    me = 4 * lax.axis_index("x") + 2 * lax.axis_index("y") + lax.axis_index("c")
    ada_n = w_ada.shape[2]
    in_n = w_in.shape[2]
    cw_n = conv_w.shape[2]

    first = _ag_direct(jnp.concatenate([c, conv_w[0].reshape(1, CONV_K * cw_n)], axis=1), "ag_c")[:, 0]
    c_all = first[:, :D_MODEL]
    conv_w_full = first[:, D_MODEL:].reshape(N_DEV, CONV_K, cw_n).transpose(1, 0, 2).reshape(CONV_K, XBC_W)
    b_piece = lax.dynamic_slice_in_dim(b_ada, me * ada_n, ada_n, axis=1)
    mod_all = _ag_direct(_mod_piece(c_all, w_ada[0], b_piece), "ag_mod")
    mod = lax.dynamic_index_in_dim(mod_all, me, axis=1, keepdims=False).reshape(1, 3 * D_MODEL)
    shift, scale, gate = mod[:, :D_MODEL], mod[:, D_MODEL:2 * D_MODEL], mod[:, 2 * D_MODEL:]

    w_t, zero = _ag_two_level(w_in[0].T.astype(BF), "ag_w_in")
    w_t = w_t.reshape(N_DEV * in_n, D_MODEL)

    def with_mine(blocks, mine):
        return lax.dynamic_update_index_in_dim(lax.empty(blocks, mine.dtype), mine, me, axis=0)

    rows = jnp.concatenate([w_attn_proj[0], w_ssm_proj[0], w_out[0]], axis=0).astype(BF) + zero
    r_ap, r_sp = w_attn_proj.shape[1], w_ssm_proj.shape[1]
    rows_started, zero = _exchange_start(rows, with_mine((N_DEV,) + rows.shape, rows), True, "ag_rows_start")

    def rows_fn(after):
        return _exchange_wait(rows_started, after, True, "ag_rows_wait")

    started = {}

    def send_blocks(key, g, name):
        started[key], zero = _exchange_start(
            g, with_mine(g.shape, lax.dynamic_index_in_dim(g, me, axis=0, keepdims=False)), False, name)
        return zero

    def after_mid(g_wap, g_wsp, g_wout):
        return send_blocks("rows", jnp.concatenate(
            [g_wap.reshape(N_DEV, r_ap, D_MODEL), g_wsp.reshape(N_DEV, r_sp, D_MODEL),
             g_wout.reshape(N_DEV, r_ap, D_MODEL)], axis=1), "rs_rows_start")

    def after_gw(g_ws):
        return send_blocks("in", jnp.concatenate(g_ws, axis=0).reshape(N_DEV, in_n, D_MODEL), "rs_in_start")

    r = _local_step(x[0], loss_target[0], shift, scale + zero, gate, w_t, rows_fn, norm_w, q_norm_w, k_norm_w,
                    rel_bias, sinks, conv_w_full, conv_b, dt_bias, a_log, d_skip, ssm_norm_w, after_mid, after_gw)

    small = dict(b_ada=r["dmod"], norm_w=r["g_norm_w"], q_norm_w=r["g_qnw"], k_norm_w=r["g_knw"], rel_bias=r["g_rel"],
                 sinks=r["g_sinks"], conv_b=r["g_conv_b"], dt_bias=r["g_dt_bias"], a_log=r["g_a_log"],
                 d_skip=r["g_d_skip"], ssm_norm_w=r["g_ssm_nw"])
    pack_all = _ag_direct(_pack_partials(small, r["loss"], r["g_conv_w"]), "ag_small")
    res, loss = _adam_small(pack_all, w, m, v)
    loss = loss[0, 0]
    cw_parts = pack_all[:, 0, _CW_OFF:].reshape(N_DEV, CONV_K, XBC_W)
    cw_mine = lax.dynamic_slice_in_dim(cw_parts, me * cw_n, cw_n, axis=2)
    res["conv_w"] = [a[None] for a in _adam(cw_mine, conv_w[0], m_conv_w[0], v_conv_w[0], "adam_conv_w")]

    dmod_piece = lax.dynamic_slice_in_dim(pack_all[:, 0, :3 * D_MODEL], me * ada_n, ada_n, axis=1)
    g_ada = _gw_ada(c_all, dmod_piece)
    res["w_ada"] = [a[None] for a in _adam(g_ada[None], w_ada[0], m_w_ada[0], v_w_ada[0], "adam_w_ada")]

    cat = lambda d: jnp.concatenate([d["w_attn_proj"][0], d["w_ssm_proj"][0], d["w_out"][0]], axis=0)
    rows_res = _adam(_exchange_wait(started["rows"], g_ada, False, "rs_rows_wait"), cat(w), cat(m), cat(v), "adam_w_rows")
    res["w_in"] = [a.T[None] for a in _adam(_exchange_wait(started["in"], rows_res[0], False, "rs_in_wait"),
                                            w_in[0].T, m_w_in[0].T, v_w_in[0].T, "adam_w_in")]
    res["w_attn_proj"] = [a[None, :r_ap] for a in rows_res]
    res["w_ssm_proj"] = [a[None, r_ap:r_ap + r_sp] for a in rows_res]
    res["w_out"] = [a[None, r_ap + r_sp:] for a in rows_res]

    outs = [loss, r["grad_x"][None]]
    for j in range(4):
        outs += [res[name][j] for name in WEIGHTS]
    return tuple(outs)
```

```python
import math

import numpy as np
import jax
import jax.numpy as jnp
from jax import lax
from jax.experimental import pallas as pl
from jax.experimental.pallas import tpu as pltpu

F32 = jnp.float32
BF = jnp.bfloat16
HI = lax.Precision.HIGHEST

D_MODEL = 1024
ATTN_HEADS = 16
KV_HEADS = 4
GRP = ATTN_HEADS // KV_HEADS
HEAD_DIM = 64
ATTN_W = ATTN_HEADS * HEAD_DIM
KV_W = KV_HEADS * HEAD_DIM
BLOCK = 128
REL_BUCKETS = 32
REL_MAX_DIST = 128
SSM_W = 2048
SSM_P = 64
SSM_HEADS = 32
SSM_G = 4
SSM_R = 8
SSM_N = 128
CONV_K = 4
XBC_W = SSM_W + 2 * SSM_G * SSM_N
SEG_W = (ATTN_W, 2 * KV_W, ATTN_W + SSM_W, XBC_W, SSM_HEADS, 2 * D_MODEL)
NSEG = len(SEG_W)
SEG_OFF = tuple(int(v) for v in np.cumsum((0,) + SEG_W))
IN_W = SEG_OFF[-1]
GATE_SEGS = (2, 5)
EPS = 1e-6
N_DEV = 8
ADAM_LR, ADAM_B1, ADAM_B2, ADAM_EPS, ADAM_WD, ADAM_STEP = 0.001, 0.9, 0.999, 1e-08, 0.01, 10
VMEM_LIMIT = 60 * 1024 * 1024
MESH = pl.DeviceIdType.MESH
ANY = pl.BlockSpec(memory_space=pl.ANY)


def _dot(a, b, precision=None):
    return jnp.dot(a, b, preferred_element_type=F32, precision=precision)


def _dot_nt(a, b, precision=None):
    return lax.dot_general(a, b, (((1,), (1,)), ((), ())), preferred_element_type=F32, precision=precision)


def _dot_tn(a, b, precision=None):
    return lax.dot_general(a, b, (((0,), (0,)), ((), ())), preferred_element_type=F32, precision=precision)


def _bf(a):
    return a.astype(BF)


def _sig(a):
    return 0.5 * jnp.tanh(0.5 * a) + 0.5


def _params(**kw):
    return pltpu.CompilerParams(vmem_limit_bytes=VMEM_LIMIT, **kw)


def _full(shape):
    nd = len(shape)
    return pl.BlockSpec(shape, lambda i: (0,) * nd)


def _rows(tm, w):
    return pl.BlockSpec((tm, w), lambda i: (i, 0))


def _inproj(x, norm_w, scale, shift, w_t, tm=256):
    s = x.shape[0]

    def body(x_ref, nw_ref, sc_ref, sh_ref, w_hbm, *rest):
        outs, h_ref, w_vm, sem = rest[:NSEG], rest[NSEG], rest[NSEG + 1], rest[NSEG + 2]
        first = pl.program_id(0) == 0
        cps = [pltpu.make_async_copy(w_hbm.at[SEG_OFF[j]:SEG_OFF[j + 1], :], w_vm.at[SEG_OFF[j]:SEG_OFF[j + 1], :], sem.at[j])
               for j in range(NSEG)]

        def tile(waiting):
            xv = x_ref[...]
            r = lax.rsqrt(jnp.mean(xv * xv, axis=-1, keepdims=True) + EPS)
            h = xv * r * (nw_ref[...] * (1.0 + sc_ref[...])) + sh_ref[...]
            hb = _bf(h)
            h_ref[...] = hb
            for j in range(NSEG):
                if waiting:
                    cps[j].wait()
                outs[j][...] = _dot_nt(hb, w_vm[SEG_OFF[j]:SEG_OFF[j + 1], :]).astype(outs[j].dtype)

        @pl.when(first)
        def _():
            for cp in cps:
                cp.start()
            tile(True)

        @pl.when(jnp.logical_not(first))
        def _():
            tile(False)

    vec = _full((1, D_MODEL))
    return pl.pallas_call(
        body, name="inproj", grid=(s // tm,),
        in_specs=[_rows(tm, D_MODEL), vec, vec, vec, ANY],
        out_specs=[_rows(tm, w) for w in SEG_W] + [_rows(tm, D_MODEL)],
        out_shape=[jax.ShapeDtypeStruct((s, w), BF if j in GATE_SEGS else F32) for j, w in enumerate(SEG_W)]
                  + [jax.ShapeDtypeStruct((s, D_MODEL), BF)],
        scratch_shapes=[pltpu.VMEM((IN_W, D_MODEL), BF), pltpu.SemaphoreType.DMA((NSEG,))],
        compiler_params=_params(dimension_semantics=("arbitrary",)),
    )(x, norm_w, scale, shift, w_t)


def _bucket_onehot_t():
    qi = jnp.arange(BLOCK)[:, None]
    kj = jnp.arange(2 * BLOCK)[None, :]
    dist = qi + BLOCK - kj
    n = jnp.maximum(dist, 0)
    max_exact = REL_BUCKETS // 2
    nf = jnp.maximum(n, 1).astype(F32)
    large = max_exact + (jnp.log(nf / max_exact) / math.log(REL_MAX_DIST / max_exact)
                         * (REL_BUCKETS - max_exact)).astype(jnp.int32)
    large = jnp.minimum(large, REL_BUCKETS - 1)
    bucket = jnp.where(n < max_exact, n, large).reshape(1, BLOCK * 2 * BLOCK)
    return (bucket == jnp.arange(REL_BUCKETS)[:, None]).astype(F32)


def _bias_dense(rel_bias_t, oh_t):
    def body(rb_ref, oh_ref, o_ref):
        o_ref[...] = _dot(rb_ref[...], oh_ref[...], HI)

    return pl.pallas_call(
        body, name="bias_dense", out_shape=jax.ShapeDtypeStruct((ATTN_HEADS, BLOCK * 2 * BLOCK), F32),
        compiler_params=_params(),
    )(rel_bias_t, oh_t)


def _bias_grad(ds_sum, oh_t):
    def body(ds_ref, oh_ref, o_ref):
        o_ref[...] = _dot_nt(ds_ref[...], oh_ref[...], HI)

    return pl.pallas_call(
        body, name="bias_grad", out_shape=jax.ShapeDtypeStruct((ATTN_HEADS, REL_BUCKETS), F32),
        compiler_params=_params(),
    )(ds_sum, oh_t)


def _group_sum(a, e):
    hi = _bf(a)
    return _dot(hi, e) + _dot(_bf(a - hi.astype(F32)), e)


def _group_bcast(a, e3t):
    hi = _bf(a)
    r1 = a - hi.astype(F32)
    mid = _bf(r1)
    return _dot(jnp.concatenate([hi, mid, _bf(r1 - mid.astype(F32))], axis=1), e3t)


def _membership(width, group, ngroups):
    e = (jnp.arange(width)[:, None] // group == jnp.arange(ngroups)[None, :]).astype(BF)
    return e, jnp.tile(e.T, (3, 1))


def _fold(width, group):
    return (jnp.arange(width)[:, None] % group == jnp.arange(group)[None, :]).astype(BF)


def _heads_norm(t, w_x, e, e3t):
    r = lax.rsqrt(_group_sum(t * t, e) * (1.0 / HEAD_DIM) + EPS)
    r_x = _group_bcast(r, e3t)
    return t * r_x * w_x, r_x


def _heads_norm_bwd(t, r_x, w_x, d, e, e3t):
    wd = d * w_x
    corr = _group_bcast(_group_sum(t * wd, e) * (1.0 / HEAD_DIM), e3t)
    return r_x * wd - t * (r_x * r_x * r_x) * corr, jnp.sum(d * t * r_x, axis=0, keepdims=True)


def _stack_heads(a, hk):
    return jnp.concatenate([a[:, (hk * GRP + g) * HEAD_DIM:(hk * GRP + g + 1) * HEAD_DIM] for g in range(GRP)], axis=0)


def _stack_cols(a, hk):
    return jnp.concatenate([a[:, hk * GRP + g:hk * GRP + g + 1] for g in range(GRP)], axis=0)


def _masked_bias(bias):
    qi = jnp.arange(BLOCK)[:, None]
    kj = jnp.arange(2 * BLOCK)[None, :]
    cur_ok = jnp.logical_and(kj >= BLOCK, kj - BLOCK <= qi)
    both_ok = jnp.logical_or(jnp.logical_and(kj < BLOCK, kj > qi), cur_ok)
    return jnp.stack([jnp.where(cur_ok, bias, -1e30), jnp.where(both_ok, bias, -1e30)])


def _attn_consts(qnw, knw):
    eq, eq3t = _membership(ATTN_W, HEAD_DIM, ATTN_HEADS)
    ek, ek3t = _membership(KV_W, HEAD_DIM, ATTN_HEADS)
    return (jnp.tile(qnw, (1, ATTN_HEADS)), jnp.tile(knw, (1, KV_HEADS)), eq, eq3t, ek, ek3t)


def _attn_fwd(q, kv, bias, sinks, consts):
    s = q.shape[0]
    nb = s // BLOCK
    gq = GRP * BLOCK
    bias_t = bias.reshape(2, KV_HEADS, GRP, BLOCK, 2 * BLOCK).transpose(0, 1, 4, 2, 3).reshape(2, KV_HEADS, 2 * BLOCK, gq)
    sink_rows = jnp.repeat(sinks.reshape(KV_HEADS, GRP), BLOCK, axis=1).reshape(KV_HEADS, 1, gq)
    eye = jnp.eye(BLOCK, dtype=BF)

    def body(q_ref, kp_ref, kc_ref, vp_ref, vc_ref, b_ref, bt_ref, sk_ref, skr_ref, eye_ref,
             qw_ref, kw_ref, eq_ref, eq3_ref, ek_ref, ek3_ref, o_ref, lse_ref):
        qn = _bf(_heads_norm(q_ref[...], qw_ref[...], eq_ref[...], eq3_ref[...])[0] * (HEAD_DIM ** -0.5))
        kn = _bf(_heads_norm(jnp.concatenate([kp_ref[...], kc_ref[...]], axis=0), kw_ref[...], ek_ref[...], ek3_ref[...])[0])
        vv = _bf(jnp.concatenate([vp_ref[...], vc_ref[...]], axis=0))
        ones = jnp.ones((2 * BLOCK, HEAD_DIM), BF)
        lses = []
        kss = [slice(hk * HEAD_DIM, (hk + 1) * HEAD_DIM) for hk in range(KV_HEADS)]
        qgs = [_stack_heads(qn, hk) for hk in range(KV_HEADS)]
        sc_ts = [_dot_nt(kn[:, kss[hk]], qgs[hk]) + bt_ref[0, hk] for hk in range(KV_HEADS)]
        m_rows = [jnp.maximum(jnp.max(sc_ts[hk], axis=0, keepdims=True), skr_ref[hk]) for hk in range(KV_HEADS)]
        m8s = [_bf(jnp.broadcast_to(m + jnp.abs(m) * (2.0 ** -7), (8, gq))) for m in m_rows]
        ms = [jnp.concatenate([_dot_nt(eye_ref[...], m8[:, g * BLOCK:(g + 1) * BLOCK])[:, 0:1] for g in range(GRP)], axis=0)
              for m8 in m8s]
        scs = [_dot_nt(qgs[hk], kn[:, kss[hk]]) + b_ref[0, hk * GRP:(hk + 1) * GRP].reshape(gq, 2 * BLOCK)
               for hk in range(KV_HEADS)]
        ps = [_bf(jnp.exp(scs[hk] - ms[hk])) for hk in range(KV_HEADS)]
        pvs = [_dot(ps[hk], jnp.concatenate([vv[:, kss[hk]], ones], axis=1)) for hk in range(KV_HEADS)]
        for hk in range(KV_HEADS):
            m, pv = ms[hk], pvs[hk]
            sink = jnp.concatenate([jnp.full((BLOCK, 1), sk_ref[0, hk * GRP + g], F32) for g in range(GRP)], axis=0)
            den = pv[:, HEAD_DIM:HEAD_DIM + 1] + jnp.exp(sink - m)
            out = pv[:, :HEAD_DIM] * (1.0 / den)
            lse = m + jnp.log(den)
            for g in range(GRP):
                h = hk * GRP + g
                o_ref[:, h * HEAD_DIM:(h + 1) * HEAD_DIM] = out[g * BLOCK:(g + 1) * BLOCK]
                lses.append(lse[g * BLOCK:(g + 1) * BLOCK])
        lse_ref[...] = jnp.concatenate(lses, axis=1)

    cur = lambda w, col=0: pl.BlockSpec((BLOCK, w), lambda i: (i, col))
    prev = lambda w, col=0: pl.BlockSpec((BLOCK, w), lambda i: (jnp.maximum(i - 1, 0), col))
    whole = lambda a: pl.BlockSpec(a.shape, lambda i: (0,) * a.ndim)
    first_or_not = lambda a: pl.BlockSpec((1,) + a.shape[1:], lambda i: (jnp.minimum(i, 1),) + (0,) * (a.ndim - 1))
    return pl.pallas_call(
        body, name="attn_fwd", grid=(nb,),
        in_specs=[cur(ATTN_W), prev(KV_W, 0), cur(KV_W, 0), prev(KV_W, 1), cur(KV_W, 1),
                  first_or_not(bias), first_or_not(bias_t),
                  pl.BlockSpec(memory_space=pltpu.SMEM), whole(sink_rows), whole(eye)] + [_full(c.shape) for c in consts],
        out_specs=[cur(ATTN_W), cur(ATTN_HEADS)],
        out_shape=[jax.ShapeDtypeStruct((s, ATTN_W), F32), jax.ShapeDtypeStruct((s, ATTN_HEADS), F32)],
        compiler_params=_params(dimension_semantics=("arbitrary",)),
    )(q, kv, kv, kv, kv, bias, bias_t, sinks, sink_rows, eye, *consts)


def _conv_taps(xbc, tail):
    ext = jnp.concatenate([tail, xbc], axis=0)
    return [pltpu.roll(ext, CONV_K - 1 - j, axis=0)[8:8 + BLOCK] if j < CONV_K - 1 else xbc for j in range(CONV_K)]


def _softplus(u):
    return jnp.maximum(u, 0.0) + jnp.log(1.0 + jnp.exp(-jnp.abs(u)))


def _tril():
    r = lax.broadcasted_iota(jnp.int32, (BLOCK, BLOCK), 0)
    c = lax.broadcasted_iota(jnp.int32, (BLOCK, BLOCK), 1)
    return r >= c


def _triu():
    r = lax.broadcasted_iota(jnp.int32, (BLOCK, BLOCK), 0)
    c = lax.broadcasted_iota(jnp.int32, (BLOCK, BLOCK), 1)
    return r <= c


def _exact_left(m01, a):
    hi = _bf(a)
    r1 = a - hi.astype(F32)
    mid = _bf(r1)
    return _dot(m01, hi) + _dot(m01, mid) + _dot(m01, _bf(r1 - mid.astype(F32)))


def _ssd_common(conv, dtr, dtb_ref, alog_ref, e3_ref):
    sg = _sig(conv)
    xact = conv * sg
    u = dtr + dtb_ref[...]
    dt = _softplus(u)
    a = -jnp.exp(alog_ref[...])
    trilb = _tril()
    acum = _exact_left(trilb.astype(BF), dt * a) * math.log2(math.e)
    both = _group_bcast(jnp.concatenate([dt, acum], axis=0), e3_ref[...])
    dt_x, acum_x = both[:BLOCK], both[BLOCK:]
    return sg, xact, u, dt, a, trilb, acum, dt_x, acum_x


SSD_CH = 2


def _ssd_fwd(xbc, dt_raw, conv_w, conv_b, dt_bias, a_log, dsk_x, e3t):
    s = xbc.shape[0]
    nc = s // BLOCK
    ch = SSD_CH if nc % SSD_CH == 0 else 1
    rows = ch * BLOCK

    def body(x_ref, tail_ref, dtr_ref, cw_ref, cb_ref, dtb_ref, alog_ref, dsk_ref, e3_ref,
             y_ref, hp_ref, conv_ref, hst, yd_s, yoff_s):
        i = pl.program_id(0)

        @pl.when(i == 0)
        def _():
            hst[...] = jnp.zeros_like(hst)

        for j in range(ch):
            rs = slice(j * BLOCK, (j + 1) * BLOCK)
            tail = jnp.where(i > 0, tail_ref[...], 0.0) if j == 0 else x_ref[j * BLOCK - 8:j * BLOCK, :]
            taps = _conv_taps(x_ref[rs, :], tail)
            conv = cb_ref[...] + sum(taps[t] * cw_ref[t:t + 1, :] for t in range(CONV_K))
            conv_ref[rs, :] = conv
            _, xact, _, _, _, trilb, acum, dt_x, acum_x = _ssd_common(conv, dtr_ref[rs, :], dtb_ref, alog_ref, e3_ref)
            xs = xact[:, :SSM_W]
            acum_t = acum.T
            ea_x = jnp.exp2(acum_x)
            last_x = acum_x[BLOCK - 1:BLOCK, :]
            xdt = xs * dt_x
            xw = xdt * jnp.exp2(last_x - acum_x)
            cd_x = jnp.exp2(last_x)
            hprev = hst[...]
            hp_ref[j] = hprev
            sls = [slice(g * SSM_R * SSM_P, (g + 1) * SSM_R * SSM_P) for g in range(SSM_G)]
            bgs = [_bf(xact[:, SSM_W + g * SSM_N:SSM_W + (g + 1) * SSM_N]) for g in range(SSM_G)]
            cgs = [_bf(xact[:, SSM_W + SSM_G * SSM_N + g * SSM_N:SSM_W + SSM_G * SSM_N + (g + 1) * SSM_N])
                   for g in range(SSM_G)]
            xdt_b, xw_b, hprev_b = _bf(xdt), _bf(xw), _bf(hprev)
            low_half = lax.broadcasted_iota(jnp.int32, (BLOCK, 2 * SSM_P), 1) < SSM_P
            cbs = [_dot_nt(cgs[g], bgs[g]) for g in range(SSM_G)]
            for g in range(SSM_G):
                sl = sls[g]
                yoff_s[:, sl] = _dot(cgs[g], hprev_b[:, sl]) * ea_x[:, sl]
                hst[:, sl] = hprev[:, sl] * cd_x[:, sl] + _dot_tn(bgs[g], xw_b[:, sl])
            for g in range(SSM_G):
                hss = [slice((g * SSM_R + r) * SSM_P, (g * SSM_R + r + 1) * SSM_P) for r in range(SSM_R)]
                mms = [_bf(cbs[g] * jnp.exp2(jnp.where(trilb, acum[:, g * SSM_R + r:g * SSM_R + r + 1]
                                                      - acum_t[g * SSM_R + r:g * SSM_R + r + 1, :], -1e30)))
                       for r in range(SSM_R)]
                for r in range(0, SSM_R, 2):
                    pair = slice(hss[r].start, hss[r + 1].stop)
                    xp = xdt_b[:, pair]
                    rhs = jnp.concatenate([jnp.where(low_half, xp, 0), jnp.where(low_half, 0, xp)], axis=0)
                    yd_s[:, pair] = _dot(jnp.concatenate([mms[r], mms[r + 1]], axis=1), rhs)
            y_ref[rs, :] = yd_s[...] + yoff_s[...] + dsk_ref[...] * xs

    blk = lambda w: pl.BlockSpec((rows, w), lambda i: (i, 0))
    return pl.pallas_call(
        body, name="ssd_fwd", grid=(nc // ch,),
        in_specs=[blk(XBC_W), pl.BlockSpec((8, XBC_W), lambda i: (jnp.maximum(i * (rows // 8) - 1, 0), 0)),
                  blk(SSM_HEADS), _full((CONV_K, XBC_W)), _full((1, XBC_W)), _full((1, SSM_HEADS)),
                  _full((1, SSM_HEADS)), _full((1, SSM_W)), _full((3 * SSM_HEADS, SSM_W))],
        out_specs=[blk(SSM_W), pl.BlockSpec((ch, SSM_N, SSM_W), lambda i: (i, 0, 0)), blk(XBC_W)],
        out_shape=[jax.ShapeDtypeStruct((s, SSM_W), F32), jax.ShapeDtypeStruct((nc, SSM_N, SSM_W), F32),
                   jax.ShapeDtypeStruct((s, XBC_W), F32)],
        scratch_shapes=[pltpu.VMEM((SSM_N, SSM_W), F32), pltpu.VMEM((BLOCK, SSM_W), F32), pltpu.VMEM((BLOCK, SSM_W), F32)],
        compiler_params=_params(dimension_semantics=("arbitrary",)),
    )(xbc, xbc, dt_raw, conv_w, conv_b, dt_bias, a_log, dsk_x, e3t)


def _dsilu(z, sg, silu):
    return sg * (1.0 + (z - silu))


def _mid(x, tgt, o_att, zam, ypre, gab, gate, ssm_nw, rows_all, tm=256):
    s = x.shape[0]
    gw = SSM_W // SSM_G

    r_ap, r_sp = ATTN_W // N_DEV, SSM_W // N_DEV

    def body(x_ref, t_ref, o_ref, zam_ref, yp_ref, gab_ref, gate_ref, nw_ref, rows_h,
             dout_ref, do_ref, dzam_ref, dyp_ref, dgab_ref,
             yag_ref, dya_ref, yn_ref, dyb_ref, mg_ref, dob_ref, gnw_ref, dgate_ref, loss_ref,
             wap_v, wsp_v, wout_v, sem):
        i = pl.program_id(0)

        @pl.when(i == 0)
        def _():
            cps = []
            for d in range(N_DEV):
                for j, (dst, r0, rn) in enumerate(((wap_v, 0, r_ap), (wsp_v, r_ap, r_sp), (wout_v, r_ap + r_sp, r_ap))):
                    cps.append(pltpu.make_async_copy(rows_h.at[d, r0:r0 + rn, :], dst.at[d * rn:(d + 1) * rn, :], sem.at[j]))
            for cp in cps:
                cp.start()
            gnw_ref[...] = jnp.zeros_like(gnw_ref)
            dgate_ref[...] = jnp.zeros_like(dgate_ref)
            loss_ref[...] = jnp.zeros_like(loss_ref)
            for cp in cps:
                cp.wait()

        gate = gate_ref[...]
        nw = nw_ref[...]
        o_att = o_ref[...]
        z_a = zam_ref[:, :ATTN_W].astype(F32)
        s_a = _sig(z_a)
        silu_a = z_a * s_a
        yag = _bf(o_att * silu_a)
        yag_ref[...] = yag
        ypre = yp_ref[...]
        z_m = zam_ref[:, ATTN_W:].astype(F32)
        s_m = _sig(z_m)
        silu_m = z_m * s_m
        yg = ypre * silu_m
        rinv = jnp.concatenate(
            [jnp.broadcast_to(lax.rsqrt(jnp.mean(yg[:, g * gw:(g + 1) * gw] ** 2, axis=-1, keepdims=True) + EPS), (tm, gw))
             for g in range(SSM_G)], axis=1)
        ynr = yg * rinv
        yn = _bf(ynr * nw)
        yn_ref[...] = yn
        y_a = _dot(yag, wap_v[...])
        y_b = _dot(yn, wsp_v[...])
        g_a = _sig(gab_ref[:, :D_MODEL].astype(F32))
        g_b = _sig(gab_ref[:, D_MODEL:].astype(F32))
        merged = _bf(g_a * y_a + g_b * y_b)
        mg_ref[...] = merged
        o = _dot(merged, wout_v[...])
        diff = x_ref[...] + gate * o - t_ref[...]
        loss_ref[...] += (0.5 / D_MODEL) * jnp.sum(diff * diff, axis=(0, 1), keepdims=True)
        dout = diff * (1.0 / D_MODEL)
        dout_ref[...] = dout
        dgate_ref[...] += jnp.sum(dout * o, axis=0, keepdims=True)
        d_o = _bf(dout * gate)
        dob_ref[...] = d_o
        dmerged = _dot_nt(d_o, wout_v[...])
        dy_af = dmerged * g_a
        dy_bf = dmerged * g_b
        dy_a = _bf(dy_af)
        dy_b = _bf(dy_bf)
        dya_ref[...] = dy_a
        dyb_ref[...] = dy_b
        dyag = _dot_nt(dy_a, wap_v[...])
        dyn = _dot_nt(dy_b, wsp_v[...])
        dgab_ref[:, :D_MODEL] = _bf(dy_af * y_a * (1.0 - g_a))
        dgab_ref[:, D_MODEL:] = _bf(dy_bf * y_b * (1.0 - g_b))
        do_ref[...] = dyag * silu_a
        dzam_ref[:, :ATTN_W] = _bf(dyag * o_att * _dsilu(z_a, s_a, silu_a))
        gnw_ref[...] += jnp.sum(dyn * ynr, axis=0, keepdims=True)
        dynw = dyn * nw
        corr = jnp.concatenate(
            [jnp.broadcast_to(jnp.mean((dynw * ynr)[:, g * gw:(g + 1) * gw], axis=-1, keepdims=True), (tm, gw))
             for g in range(SSM_G)], axis=1)
        dyg = rinv * (dynw - ynr * corr)
        dyp_ref[...] = dyg * silu_m
        dzam_ref[:, ATTN_W:] = _bf(dyg * ypre * _dsilu(z_m, s_m, silu_m))

    r1, r2, r3 = _rows(tm, D_MODEL), _rows(tm, SSM_W), _rows(tm, ATTN_W + SSM_W)
    sd = jax.ShapeDtypeStruct
    return pl.pallas_call(
        body, name="mid", grid=(s // tm,),
        in_specs=[r1, r1, r1, r3, r2, r2, _full((1, D_MODEL)), _full((1, SSM_W)), ANY],
        out_specs=[r1, r1, r3, r2, r2, r1, r1, r2, r1, r1, r1,
                   _full((1, SSM_W)), _full((1, D_MODEL)), _full((1, 1))],
        out_shape=[sd((s, D_MODEL), F32), sd((s, ATTN_W), F32), sd((s, ATTN_W + SSM_W), BF), sd((s, SSM_W), F32),
                   sd((s, 2 * D_MODEL), BF),
                   sd((s, ATTN_W), BF), sd((s, D_MODEL), BF), sd((s, SSM_W), BF), sd((s, D_MODEL), BF),
                   sd((s, D_MODEL), BF), sd((s, D_MODEL), BF),
                   sd((1, SSM_W), F32), sd((1, D_MODEL), F32), sd((1, 1), F32)],
        scratch_shapes=[pltpu.VMEM((ATTN_W, D_MODEL), BF), pltpu.VMEM((SSM_W, D_MODEL), BF), pltpu.VMEM((D_MODEL, D_MODEL), BF),
                        pltpu.SemaphoreType.DMA((3,))],
        compiler_params=_params(dimension_semantics=("arbitrary",)),
    )(x, tgt, o_att, zam, ypre, gab, gate, ssm_nw, rows_all)


def _attn_bwd(q, kv, bias, sinks, consts, o_att, lse, d_o):
    s = q.shape[0]
    nb = s // BLOCK
    folds = (_fold(ATTN_W, HEAD_DIM), _fold(KV_W, HEAD_DIM))

    def body(q_ref, kp_ref, kc_ref, vp_ref, vc_ref, b_ref, skv_ref, qw_ref, kw_ref, eq_ref, eq3_ref, ek_ref, ek3_ref,
             fq_ref, fk_ref, o_ref, lse_ref, do_ref,
             dq_ref, dkv_ref, dss_ref, gqw_ref, gkw_ref, gsk_ref, ckn, cv, dqn_s, dkn_s, dv_s, gq_x, gk_x):
        i = pl.program_id(0)
        kw, ek, ek3 = kw_ref[...], ek_ref[...], ek3_ref[...]

        @pl.when(i == 0)
        def _():
            for ref in (ckn, cv, dss_ref, gq_x, gk_x, gsk_ref):
                ref[...] = jnp.zeros_like(ref)

        @pl.when(i < nb)
        def _():
            qw, eq, eq3 = qw_ref[...], eq_ref[...], eq3_ref[...]
            qf = q_ref[...]
            qnf, rq_x = _heads_norm(qf, qw, eq, eq3)
            qn = _bf(qnf * (HEAD_DIM ** -0.5))
            kf = jnp.concatenate([kp_ref[...], kc_ref[...]], axis=0)
            knf, rk_x = _heads_norm(kf, kw, ek, ek3)
            kn = _bf(knf)
            vv = _bf(jnp.concatenate([vp_ref[...], vc_ref[...]], axis=0))
            d_of = do_ref[...]
            d_ob = _bf(d_of)
            lse_all = lse_ref[...]
            delta = _group_sum(d_of * o_ref[...], eq)
            gsk_ref[...] += jnp.sum(-jnp.exp(skv_ref[...] - lse_all) * delta, axis=0, keepdims=True)
            kss = [slice(hk * HEAD_DIM, (hk + 1) * HEAD_DIM) for hk in range(KV_HEADS)]
            qgs = [_stack_heads(qn, hk) for hk in range(KV_HEADS)]
            d_ogs = [_stack_heads(d_ob, hk) for hk in range(KV_HEADS)]
            scs = [_dot_nt(qgs[hk], kn[:, kss[hk]]) + b_ref[0, hk * GRP:(hk + 1) * GRP].reshape(GRP * BLOCK, 2 * BLOCK)
                   for hk in range(KV_HEADS)]
            dps = [_dot_nt(d_ogs[hk], vv[:, kss[hk]]) for hk in range(KV_HEADS)]
            ps = [jnp.exp(scs[hk] - _stack_cols(lse_all, hk)) for hk in range(KV_HEADS)]
            dss = [ps[hk] * (dps[hk] - _stack_cols(delta, hk)) for hk in range(KV_HEADS)]
            pbs = [_bf(p) for p in ps]
            dsbs = [_bf(ds) for ds in dss]
            for hk in range(KV_HEADS):
                dss_ref[hk * GRP:(hk + 1) * GRP] += dss[hk].reshape(GRP, BLOCK, 2 * BLOCK)
            for hk in range(KV_HEADS):
                dv_s[:, kss[hk]] = _dot_tn(pbs[hk], d_ogs[hk])
                dkn_s[:, kss[hk]] = _dot_tn(dsbs[hk], qgs[hk])
            dqns = [_dot(dsbs[hk], kn[:, kss[hk]]) * (HEAD_DIM ** -0.5) for hk in range(KV_HEADS)]
            for hk in range(KV_HEADS):
                for g in range(GRP):
                    h = hk * GRP + g
                    dqn_s[:, h * HEAD_DIM:(h + 1) * HEAD_DIM] = dqns[hk][g * BLOCK:(g + 1) * BLOCK]
            dq, gq = _heads_norm_bwd(qf, rq_x, qw, dqn_s[...], eq, eq3)
            dq_ref[...] = _bf(dq)
            gq_x[...] += gq
            dk, gk = _heads_norm_bwd(kf[:BLOCK], rk_x[:BLOCK], kw, ckn[...] + dkn_s[0:BLOCK, :], ek, ek3)
            dkv_ref[:, :KV_W] = _bf(dk)
            gk_x[...] += gk
            dkv_ref[:, KV_W:] = _bf(cv[...] + dv_s[0:BLOCK, :])
            ckn[...] = dkn_s[BLOCK:2 * BLOCK, :]
            cv[...] = dv_s[BLOCK:2 * BLOCK, :]

        @pl.when(i == nb)
        def _():
            kc = kc_ref[...]
            dk, gk = _heads_norm_bwd(kc, _heads_norm(kc, kw, ek, ek3)[1], kw, ckn[...], ek, ek3)
            dkv_ref[:, :KV_W] = _bf(dk)
            dkv_ref[:, KV_W:] = _bf(cv[...])
            gqw_ref[...] = _group_sum(jnp.broadcast_to(gq_x[...], (8, ATTN_W)), fq_ref[...])[0:1]
            gkw_ref[...] = _group_sum(jnp.broadcast_to(gk_x[...] + gk, (8, KV_W)), fk_ref[...])[0:1]

    last = nb - 1
    cur = lambda w, col=0: pl.BlockSpec((BLOCK, w), lambda i: (jnp.minimum(i, last), col))
    prev = lambda w, col=0: pl.BlockSpec((BLOCK, w), lambda i: (jnp.maximum(jnp.minimum(i, last) - 1, 0), col))
    late = lambda w: pl.BlockSpec((BLOCK, w), lambda i: (jnp.maximum(i - 1, 0), 0))
    sd = jax.ShapeDtypeStruct
    return pl.pallas_call(
        body, name="attn_bwd", grid=(nb + 1,),
        in_specs=[cur(ATTN_W), prev(KV_W, 0), cur(KV_W, 0), prev(KV_W, 1), cur(KV_W, 1),
                  pl.BlockSpec((1, ATTN_HEADS, BLOCK, 2 * BLOCK), lambda i: (jnp.minimum(i, 1), 0, 0, 0)),
                  _full((1, ATTN_HEADS))]
                 + [_full(c.shape) for c in consts + folds] + [cur(ATTN_W), cur(ATTN_HEADS), cur(ATTN_W)],
        out_specs=[cur(ATTN_W), late(2 * KV_W),
                   pl.BlockSpec((ATTN_HEADS, BLOCK, 2 * BLOCK), lambda i: (0, 0, 0)),
                   _full((1, HEAD_DIM)), _full((1, HEAD_DIM)), _full((1, ATTN_HEADS))],
        out_shape=[sd((s, ATTN_W), BF), sd((s, 2 * KV_W), BF),
                   sd((ATTN_HEADS, BLOCK, 2 * BLOCK), F32), sd((1, HEAD_DIM), F32), sd((1, HEAD_DIM), F32),
                   sd((1, ATTN_HEADS), F32)],
        scratch_shapes=[pltpu.VMEM((BLOCK, KV_W), F32), pltpu.VMEM((BLOCK, KV_W), F32),
                        pltpu.VMEM((BLOCK, ATTN_W), F32), pltpu.VMEM((2 * BLOCK, KV_W), F32),
                        pltpu.VMEM((2 * BLOCK, KV_W), F32), pltpu.VMEM((1, ATTN_W), F32), pltpu.VMEM((1, KV_W), F32)],
        compiler_params=_params(dimension_semantics=("arbitrary",)),
    )(q, kv, kv, kv, kv, bias, sinks, *consts, *folds, o_att, lse, d_o)


def _ssd_bwd(xbc, conv_all, dt_raw, conv_w, dt_bias, a_log, dsk_x, e_mat, e3t, hprev_all, dy_all):
    s = xbc.shape[0]
    nc = s // BLOCK
    ch = 1
    rows = ch * BLOCK
    nsteps = nc // ch
    gw = SSM_R * SSM_P
    b0, c0 = SSM_W, SSM_W + SSM_G * SSM_N

    def body(x_ref, conv_ref, dtr_ref, cw_ref, dtb_ref, alog_ref, dsk_ref, e_ref, e3_ref, hp_ref, dy_ref,
             dx_ref, ddt_ref, gcw_ref, gcb_ref, gdtb_ref, galog_ref, gdsk_ref,
             dh, nhead, gdskx, dxdt_s, dbc_s, dxd_s):
        def chunk_bwd(j):
            rs = slice(j * BLOCK, (j + 1) * BLOCK)
            conv = conv_ref[rs, :]
            sg, xact, u, dt, a, trilb, acum, dt_x, acum_x = _ssd_common(conv, dtr_ref[rs, :], dtb_ref, alog_ref, e3_ref)
            xs = xact[:, :SSM_W]
            acum_t = acum.T
            ea_x = jnp.exp2(acum_x)
            last_x = acum_x[BLOCK - 1:BLOCK, :]
            dte_x = jnp.exp2(last_x - acum_x)
            cd_x = jnp.exp2(last_x)
            xdt = xs * dt_x
            xw = xdt * dte_x
            hprev = hp_ref[j]
            dhn = dh[...]
            dy = dy_ref[rs, :]
            gdskx[...] += jnp.sum(dy * xs, axis=0, keepdims=True)
            dyea = dy * ea_x
            lane = lax.broadcasted_iota(jnp.int32, (BLOCK, SSM_HEADS), 1)
            dacum = jnp.zeros((BLOCK, SSM_HEADS), F32)
            dacc_x, dlast_x = [], []
            sls = [slice(g * gw, (g + 1) * gw) for g in range(SSM_G)]
            bgs = [_bf(xact[:, b0 + g * SSM_N:b0 + (g + 1) * SSM_N]) for g in range(SSM_G)]
            cgs = [_bf(xact[:, c0 + g * SSM_N:c0 + (g + 1) * SSM_N]) for g in range(SSM_G)]
            hpgs = [_bf(hprev[:, sl]) for sl in sls]
            dhgs = [_bf(dhn[:, sl]) for sl in sls]
            dyeags = [_bf(dyea[:, sl]) for sl in sls]
            xwgs = [_bf(xw[:, sl]) for sl in sls]
            xdt_b, dy_b = _bf(xdt), _bf(dy)
            low_half = lax.broadcasted_iota(jnp.int32, (BLOCK, 2 * SSM_P), 1) < SSM_P
            cbs = [_dot_nt(cgs[g], bgs[g]) for g in range(SSM_G)]
            gmats = [_dot(cgs[g], hpgs[g]) for g in range(SSM_G)]
            dxws = [_dot(bgs[g], dhgs[g]) for g in range(SSM_G)]
            dcgs = [_dot_nt(dyeags[g], hpgs[g]) for g in range(SSM_G)]
            dbgs = [_dot_nt(xwgs[g], dhgs[g]) for g in range(SSM_G)]
            for g in range(SSM_G):
                sl = sls[g]
                dh[:, sl] = dhn[:, sl] * cd_x[:, sl] + _dot_tn(cgs[g], dyeags[g])
                dxdt_s[:, sl] = dxws[g] * dte_x[:, sl]
                dacc_x.append(dy[:, sl] * gmats[g] * ea_x[:, sl] - dxws[g] * xw[:, sl])
                dlast_x.append(jnp.sum(dxws[g] * xw[:, sl], axis=0, keepdims=True)
                               + jnp.sum(dhn[:, sl] * hprev[:, sl], axis=0, keepdims=True) * cd_x[:, sl])
            for g in range(SSM_G):
                bg, cg, cb, dbg, dcg = bgs[g], cgs[g], cbs[g], dbgs[g], dcgs[g]
                hss = [slice((g * SSM_R + r) * SSM_P, (g * SSM_R + r + 1) * SSM_P) for r in range(SSM_R)]
                lms = [jnp.exp2(jnp.where(trilb, acum[:, g * SSM_R + r:g * SSM_R + r + 1]
                                         - acum_t[g * SSM_R + r:g * SSM_R + r + 1, :], -1e30)) for r in range(SSM_R)]
                mms = [cb * lm for lm in lms]
                mmbs = [_bf(mm) for mm in mms]
                dms = []
                for r in range(0, SSM_R, 2):
                    pair = slice(hss[r].start, hss[r + 1].stop)
                    xp, dyp = xdt_b[:, pair], dy_b[:, pair]
                    dmp = _dot_nt(dyp, jnp.concatenate([jnp.where(low_half, xp, 0), jnp.where(low_half, 0, xp)], axis=0))
                    dms += [dmp[:, :BLOCK], dmp[:, BLOCK:]]
                    dxd_s[:, pair] = _dot_tn(jnp.concatenate([mmbs[r], mmbs[r + 1]], axis=0),
                                             jnp.concatenate([jnp.where(low_half, dyp, 0), jnp.where(low_half, 0, dyp)], axis=0))
                dcb = sum(dms[r] * lms[r] for r in range(SSM_R))
                wms = [dms[r] * mms[r] for r in range(SSM_R)]
                antis = [wm - wm.T for wm in wms]
                for r in range(SSM_R):
                    dacum = dacum + _group_sum(antis[r], (lane == g * SSM_R + r).astype(BF))
                dcbb = _bf(dcb)
                dbc_s[:, g * SSM_N:(g + 1) * SSM_N] = dbg + _dot_tn(dcbb, cg)
                dbc_s[:, SSM_G * SSM_N + g * SSM_N:SSM_G * SSM_N + (g + 1) * SSM_N] = dcg + _dot(dcbb, bg)
            dxdt = dxdt_s[...] + dxd_s[...]
            dxs = dy * dsk_ref[...] + dxdt * dt_x
            red = _group_sum(jnp.concatenate(
                [dxdt * xs, jnp.concatenate(dacc_x, axis=1),
                 jnp.broadcast_to(jnp.concatenate(dlast_x, axis=1), (8, SSM_W))], axis=0), e_ref[...])
            row = lax.broadcasted_iota(jnp.int32, (BLOCK, SSM_HEADS), 0)
            dacum = dacum + red[BLOCK:2 * BLOCK] + jnp.where(row == BLOCK - 1, red[2 * BLOCK:2 * BLOCK + 1], 0.0)
            ddta = _exact_left(_triu().astype(BF), dacum)
            ddt = red[:BLOCK] + ddta * a
            galog_ref[...] += jnp.sum(ddta * dt, axis=0, keepdims=True) * a
            du = ddt * _sig(u)
            ddt_ref[rs, :] = _bf(du)
            gdtb_ref[...] += jnp.sum(du, axis=0, keepdims=True)
            dconv = jnp.concatenate([dxs, dbc_s[...]], axis=1) * _dsilu(conv, sg, xact)
            gcb_ref[...] += jnp.sum(dconv, axis=0, keepdims=True)
            ext2 = jnp.concatenate([dconv, nhead[...]], axis=0)
            ahead = [pltpu.roll(ext2, BLOCK + 8 - (CONV_K - 1 - j), axis=0)[0:BLOCK] if j < CONV_K - 1 else dconv
                     for j in range(CONV_K)]
            dx_ref[rs, :] = _bf(sum(ahead[j] * cw_ref[j:j + 1, :] for j in range(CONV_K)))
            xraw = x_ref[rs, :]
            gcw_ref[...] += jnp.concatenate([jnp.sum(ahead[j] * xraw, axis=0, keepdims=True) for j in range(CONV_K)], axis=0)
            nhead[...] = dconv[0:8]

        i = pl.program_id(0)

        @pl.when(i == 0)
        def _():
            for ref in (dh, nhead, gdskx, gcw_ref, gcb_ref, gdtb_ref, galog_ref, gdsk_ref):
                ref[...] = jnp.zeros_like(ref)

        for j in reversed(range(ch)):
            chunk_bwd(j)

        @pl.when(i == nsteps - 1)
        def _():
            gdsk_ref[...] = _group_sum(jnp.broadcast_to(gdskx[...], (8, SSM_W)), e_ref[...])[0:1]

    chunk = lambda w: pl.BlockSpec((rows, w), lambda i: (nsteps - 1 - i, 0))
    sd = jax.ShapeDtypeStruct
    return pl.pallas_call(
        body, name="ssd_bwd", grid=(nsteps,),
        in_specs=[chunk(XBC_W), chunk(XBC_W),
                  chunk(SSM_HEADS), _full((CONV_K, XBC_W)), _full((1, SSM_HEADS)),
                  _full((1, SSM_HEADS)), _full((1, SSM_W)), _full((SSM_W, SSM_HEADS)), _full((3 * SSM_HEADS, SSM_W)),
                  pl.BlockSpec((ch, SSM_N, SSM_W), lambda i: (nsteps - 1 - i, 0, 0)), chunk(SSM_W)],
        out_specs=[chunk(XBC_W), chunk(SSM_HEADS), _full((CONV_K, XBC_W)), _full((1, XBC_W)),
                   _full((1, SSM_HEADS)), _full((1, SSM_HEADS)), _full((1, SSM_HEADS))],
        out_shape=[sd((s, XBC_W), BF), sd((s, SSM_HEADS), BF), sd((CONV_K, XBC_W), F32), sd((1, XBC_W), F32),
                   sd((1, SSM_HEADS), F32), sd((1, SSM_HEADS), F32), sd((1, SSM_HEADS), F32)],
        scratch_shapes=[pltpu.VMEM((SSM_N, SSM_W), F32), pltpu.VMEM((8, XBC_W), F32),
                        pltpu.VMEM((1, SSM_W), F32), pltpu.VMEM((BLOCK, SSM_W), F32),
                        pltpu.VMEM((BLOCK, 2 * SSM_G * SSM_N), F32), pltpu.VMEM((BLOCK, SSM_W), F32)],
        compiler_params=_params(dimension_semantics=("arbitrary",)),
    )(xbc, conv_all, dt_raw, conv_w, dt_bias, a_log, dsk_x, e_mat, e3t, hprev_all, dy_all)


def _dh(x, dout, norm_w, scale, dsegs, w_t, tm=256):
    s = x.shape[0]

    def body(x_ref, dout_ref, nw_ref, sc_ref, *rest):
        d_refs, w_hbm = rest[:NSEG], rest[NSEG]
        gx_ref, dshift_ref, dscale_ref, gnw_ref = rest[NSEG + 1:NSEG + 5]
        w_vm, sem = rest[NSEG + 5], rest[NSEG + 6]
        first = pl.program_id(0) == 0
        cps = [pltpu.make_async_copy(w_hbm.at[SEG_OFF[j]:SEG_OFF[j + 1], :], w_vm.at[SEG_OFF[j]:SEG_OFF[j + 1], :], sem.at[j])
               for j in range(NSEG)]

        def tile(waiting):
            dh = None
            for j in range(NSEG):
                if waiting:
                    cps[j].wait()
                part = _dot(d_refs[j][...], w_vm[SEG_OFF[j]:SEG_OFF[j + 1], :])
                dh = part if dh is None else dh + part
            xv = x_ref[...]
            r = lax.rsqrt(jnp.mean(xv * xv, axis=-1, keepdims=True) + EPS)
            xn = xv * r
            nw = nw_ref[...]
            sc1 = 1.0 + sc_ref[...]
            dshift_ref[...] += jnp.sum(dh, axis=0, keepdims=True)
            dhxn = jnp.sum(dh * xn, axis=0, keepdims=True)
            dscale_ref[...] += dhxn * nw
            gnw_ref[...] += dhxn * sc1
            dxn = dh * (nw * sc1)
            gx_ref[...] = dout_ref[...] + r * (dxn - xn * jnp.mean(xn * dxn, axis=-1, keepdims=True))

        @pl.when(first)
        def _():
            for cp in cps:
                cp.start()
            for ref in (dshift_ref, dscale_ref, gnw_ref):
                ref[...] = jnp.zeros_like(ref)
            tile(True)

        @pl.when(jnp.logical_not(first))
        def _():
            tile(False)

    vec = _full((1, D_MODEL))
    sd = jax.ShapeDtypeStruct
    return pl.pallas_call(
        body, name="dh", grid=(s // tm,),
        in_specs=[_rows(tm, D_MODEL), _rows(tm, D_MODEL), vec, vec] + [_rows(tm, w) for w in SEG_W] + [ANY],
        out_specs=[_rows(tm, D_MODEL), vec, vec, vec],
        out_shape=[sd((s, D_MODEL), F32), sd((1, D_MODEL), F32), sd((1, D_MODEL), F32), sd((1, D_MODEL), F32)],
        scratch_shapes=[pltpu.VMEM((IN_W, D_MODEL), BF), pltpu.SemaphoreType.DMA((NSEG,))],
        compiler_params=_params(dimension_semantics=("arbitrary",)),
    )(x, dout, norm_w, scale, *dsegs, w_t)


def _gw_seg(h, dseg, name, tm=1024):
    s, w = dseg.shape
    tn = min(w, 1024)
    tm = min(tm, s)
    nm = s // tm

    def body(h_ref, d_ref, o_ref, acc):
        m = pl.program_id(1)

        @pl.when(m == 0)
        def _():
            acc[...] = jnp.zeros_like(acc)

        acc[...] += _dot_tn(d_ref[...], h_ref[...])

        @pl.when(m == nm - 1)
        def _():
            o_ref[...] = _bf(acc[...])

    return pl.pallas_call(
        body, name=name, grid=(w // tn, nm),
        in_specs=[pl.BlockSpec((tm, D_MODEL), lambda n, m: (m, 0)), pl.BlockSpec((tm, tn), lambda n, m: (m, n))],
        out_specs=pl.BlockSpec((tn, D_MODEL), lambda n, m: (n, 0)),
        out_shape=jax.ShapeDtypeStruct((w, D_MODEL), BF),
        scratch_shapes=[pltpu.VMEM((tn, D_MODEL), F32)],
        compiler_params=_params(dimension_semantics=("arbitrary", "arbitrary")),
    )(h, dseg)


def _gw_in(h, dsegs):
    return [_gw_seg(h, d, "gw_in_%d" % j) for j, d in enumerate(dsegs)]


def _local_step(x, tgt, shift, scale, gate, w_t, rows_fn, norm_w, qnw, knw, rel_bias, sinks,
                conv_w, conv_b, dt_bias, a_log, d_skip, ssm_nw, after_mid=None, after_gw=None):
    oh_t = _bucket_onehot_t()
    bias = _masked_bias(_bias_dense(rel_bias.T, oh_t).reshape(ATTN_HEADS, BLOCK, 2 * BLOCK))
    *segs, h = _inproj(x, norm_w, scale, shift, w_t)
    q, kv, zam, xbc, dtr, gab = segs
    consts = _attn_consts(qnw, knw)
    o_att, lse = _attn_fwd(q, kv, bias, sinks, consts)
    e_mat, e3t = _membership(SSM_W, SSM_P, SSM_HEADS)
    dsk_x = jnp.repeat(d_skip, SSM_P, axis=1)
    ypre, hprev, conv = _ssd_fwd(xbc, dtr, conv_w, conv_b, dt_bias, a_log, dsk_x, e3t)
    (dout, d_o, dzam, dyp, dgab, yag, dy_a, yn, dy_b, merged, dob, g_ssm_nw, dgate, loss) = _mid(
        x, tgt, o_att, zam, ypre, gab, gate, ssm_nw, rows_fn(ypre))
    g_wap = _gw_seg(dy_a, yag, "gw_attn_proj")
    g_wsp = _gw_seg(dy_b, yn, "gw_ssm_proj")
    g_wout = _gw_seg(dob, merged, "gw_out")
    zero = after_mid(g_wap, g_wsp, g_wout) if after_mid is not None else 0.0
    dq, dkv, dss, g_qnw, g_knw, g_sinks = _attn_bwd(q, kv, bias, sinks + zero, consts, o_att, lse, d_o)
    g_rel = _bias_grad(dss.reshape(ATTN_HEADS, BLOCK * 2 * BLOCK), oh_t).T
    dxbc, ddt, g_cw, g_cb, g_dtb, g_alog, g_dsk = _ssd_bwd(
        xbc, conv, dtr, conv_w, dt_bias, a_log, dsk_x, e_mat, e3t, hprev, dyp)
    dsegs = (dq, dkv, dzam, dxbc, ddt, dgab)
    g_ws = _gw_in(h, dsegs)
    zero = after_gw(g_ws) if after_gw is not None else 0.0
    gx, dshift, dscale, g_nw = _dh(x, dout, norm_w + zero, scale, dsegs, w_t)
    return dict(loss=loss, grad_x=gx, dmod=jnp.concatenate([dshift, dscale, dgate], axis=1), g_ws=g_ws,
                g_wap=g_wap, g_wsp=g_wsp, g_wout=g_wout, g_norm_w=g_nw, g_qnw=g_qnw, g_knw=g_knw, g_rel=g_rel,
                g_sinks=g_sinks, g_conv_w=g_cw, g_conv_b=g_cb, g_dt_bias=g_dtb, g_a_log=g_alog, g_d_skip=g_dsk,
                g_ssm_nw=g_ssm_nw)


def _me():
    return lax.axis_index("x"), lax.axis_index("y"), lax.axis_index("c")


def _flip(v, bit):
    return 1 - v if bit else v


def _ag_direct(v, name):
    def body(v_ref, out_ref, send_sems, recv_sems, local_sem):
        x, y, c = _me()
        me = 4 * x + 2 * y + c
        mine = pltpu.make_async_copy(v_ref, out_ref.at[me], local_sem)
        mine.start()
        peers = [(_flip(x, k >> 2 & 1), _flip(y, k >> 1 & 1), _flip(c, k & 1)) for k in range(1, N_DEV)]
        sends = [pltpu.make_async_remote_copy(
            src_ref=v_ref, dst_ref=out_ref.at[me], send_sem=send_sems.at[j], recv_sem=recv_sems.at[j],
            device_id=p, device_id_type=MESH) for j, p in enumerate(peers)]
        for cp in sends:
            cp.start()
        for j, (px, py, pc) in enumerate(peers):
            pltpu.make_async_remote_copy(
                src_ref=v_ref, dst_ref=out_ref.at[4 * px + 2 * py + pc], send_sem=send_sems.at[j],
                recv_sem=recv_sems.at[j], device_id=(px, py, pc), device_id_type=MESH).wait_recv()
        for cp in sends:
            cp.wait_send()
        mine.wait()

    vm = pl.BlockSpec(memory_space=pltpu.VMEM)
    return pl.pallas_call(
        body, name=name, out_shape=jax.ShapeDtypeStruct((N_DEV,) + v.shape, v.dtype),
        in_specs=[vm], out_specs=vm,
        scratch_shapes=[pltpu.SemaphoreType.DMA((N_DEV - 1,)), pltpu.SemaphoreType.DMA((N_DEV - 1,)),
                        pltpu.SemaphoreType.DMA],
        compiler_params=_params(),
    )(v)


def _ag_two_level(v, name):
    def body(v_ref, out_ref, token, send_sems, recv_sems, local_sem):
        token[...] = jnp.zeros_like(token)
        x, y, c = _me()
        me, sibling = (x, y, c), (x, y, 1 - c)
        chips = [(1 - x, y), (x, 1 - y), (1 - x, 1 - y)]

        def slot(px, py, pc):
            return out_ref.at[4 * px + 2 * py + pc]

        def copy(k, block, to, src=None):
            return pltpu.make_async_remote_copy(
                src_ref=slot(*block) if src is None else src, dst_ref=slot(*block),
                send_sem=send_sems.at[k], recv_sem=recv_sems.at[k], device_id=to, device_id_type=MESH)

        mine = pltpu.make_async_copy(v_ref, slot(*me), local_sem)
        mine.start()
        first = [copy(0, me, sibling, src=v_ref)]
        first += [copy(1 + j, me, (*chip, c), src=v_ref) for j, chip in enumerate(chips)]
        for cp in first:
            cp.start()
        passed = [copy(4 + j, (*chip, c), sibling) for j, chip in enumerate(chips)]
        for j, chip in enumerate(chips):
            copy(1 + j, (*chip, c), me).wait_recv()
            passed[j].start()
        copy(0, sibling, me).wait_recv()
        for j, chip in enumerate(chips):
            copy(4 + j, (*chip, 1 - c), me).wait_recv()
        for cp in first + passed:
            cp.wait_send()
        mine.wait()

    out, token = pl.pallas_call(
        body, name=name,
        out_shape=(jax.ShapeDtypeStruct((N_DEV,) + v.shape, v.dtype), jax.ShapeDtypeStruct((8, 128), v.dtype)),
        in_specs=[ANY], out_specs=(ANY, pl.BlockSpec(memory_space=pltpu.VMEM)),
        scratch_shapes=[pltpu.SemaphoreType.DMA((7,)), pltpu.SemaphoreType.DMA((7,)), pltpu.SemaphoreType.DMA],
        compiler_params=_params(),
    )(v)
    return out, token[0:1, 0:1]


HBM = pl.BlockSpec(memory_space=pltpu.HBM)
SEM = pl.BlockSpec(memory_space=pltpu.SEMAPHORE)
EFFECT = pltpu.SideEffectType.DATAFLOW_SIDE_EFFECTING


def _peers(x, y, c):
    return [(_flip(x, k >> 2 & 1), _flip(y, k >> 1 & 1), _flip(c, k & 1)) for k in range(1, N_DEV)]


def _exchange_start(src, land, gather, name):
    def body(src_ref, land_ref, send_sems, recv_sems, src_thru, land_thru, token):
        x, y, c = _me()
        me = 4 * x + 2 * y + c
        for j, (px, py, pc) in enumerate(_peers(x, y, c)):
            pltpu.make_async_remote_copy(
                src_ref=src_ref if gather else src_ref.at[4 * px + 2 * py + pc], dst_ref=land_ref.at[me],
                send_sem=send_sems.at[j], recv_sem=recv_sems.at[j], device_id=(px, py, pc), device_id_type=MESH).start()
        token[...] = jnp.zeros_like(token)

    sems = pltpu.SemaphoreType.DMA((N_DEV - 1,))
    out = pl.pallas_call(
        body, name=name,
        out_shape=(sems, sems, pltpu.HBM(src.shape, src.dtype), pltpu.HBM(land.shape, land.dtype),
                   jax.ShapeDtypeStruct((8, 128), F32)),
        in_specs=(HBM, HBM), out_specs=(SEM, SEM, HBM, HBM, pl.BlockSpec(memory_space=pltpu.VMEM)),
        input_output_aliases={0: 2, 1: 3},
        compiler_params=pltpu.CompilerParams(has_side_effects=EFFECT),
    )(pltpu.with_memory_space_constraint(src, pltpu.HBM), pltpu.with_memory_space_constraint(land, pltpu.HBM))
    return out[:4], out[4][0, 0]


def _exchange_wait(started, after, gather, name):
    send_sems, recv_sems, src_thru, land_thru = started

    def body(src_ref, land_ref, send_sems, recv_sems, after_ref, src_dead, got_ref):
        x, y, c = _me()
        for j, (px, py, pc) in enumerate(_peers(x, y, c)):
            pid = 4 * px + 2 * py + pc
            cp = pltpu.make_async_remote_copy(
                src_ref=src_ref if gather else src_ref.at[pid], dst_ref=land_ref.at[pid],
                send_sem=send_sems.at[j], recv_sem=recv_sems.at[j], device_id=(px, py, pc), device_id_type=MESH)
            cp.wait_send()
            cp.wait_recv()

    return pl.pallas_call(
        body, name=name,
        out_shape=(pltpu.HBM(src_thru.shape, src_thru.dtype), pltpu.HBM(land_thru.shape, land_thru.dtype)),
        in_specs=(HBM, HBM, SEM, SEM, ANY), out_specs=(HBM, HBM), input_output_aliases={0: 0, 1: 1},
        compiler_params=pltpu.CompilerParams(has_side_effects=EFFECT),
    )(src_thru, land_thru, send_sems, recv_sems, after)[1]


def _silu(a):
    return a * _sig(a)


def _mod_piece(c_all, w_ada, b_piece):
    def body(c_ref, w_ref, b_ref, o_ref):
        o_ref[...] = _dot(_bf(_silu(c_ref[...])), _bf(w_ref[...])) + b_ref[...]

    return pl.pallas_call(
        body, name="mod_piece", out_shape=jax.ShapeDtypeStruct((c_all.shape[0], w_ada.shape[1]), F32),
        compiler_params=_params(),
    )(c_all, w_ada, b_piece)


def _gw_ada(c_all, dmod_piece):
    def body(c_ref, d_ref, o_ref):
        o_ref[...] = _dot_tn(_bf(_silu(c_ref[...])), _bf(d_ref[...]))

    return pl.pallas_call(
        body, name="gw_ada", out_shape=jax.ShapeDtypeStruct((c_all.shape[1], dmod_piece.shape[1]), F32),
        compiler_params=_params(),
    )(c_all, dmod_piece)


def _adam(parts, w, m, v, name):
    k, r, n = parts.shape
    if r <= 256 or r % 256 == 0:
        tr, tn = min(r, 256), n
    else:
        tr, tn = r, 256
    assert r % tr == 0 and n % tn == 0

    def body(p_ref, w_ref, m_ref, v_ref, g_ref, d_ref, nm_ref, nv_ref):
        g = p_ref[0].astype(F32)
        for j in range(1, k):
            g = g + p_ref[j].astype(F32)
        g_ref[...] = g
        d_ref[...], nm_ref[...], nv_ref[...] = _adam_math(g, w_ref[...], m_ref[...], v_ref[...])

    blk = pl.BlockSpec((tr, tn), lambda i, j: (i, j))
    return pl.pallas_call(
        body, name=name, grid=(r // tr, n // tn),
        in_specs=[pl.BlockSpec((k, tr, tn), lambda i, j: (0, i, j)), blk, blk, blk],
        out_specs=[blk, blk, blk, blk],
        out_shape=[jax.ShapeDtypeStruct((r, n), F32)] * 4,
        compiler_params=_params(dimension_semantics=("arbitrary", "arbitrary")),
    )(parts, w, m, v)


def _adam_math(g, w, m, v):
    m_new = ADAM_B1 * m + (1.0 - ADAM_B1) * g
    v_new = ADAM_B2 * v + (1.0 - ADAM_B2) * jnp.square(g)
    m_hat = m_new / (1.0 - ADAM_B1 ** ADAM_STEP)
    v_hat = v_new / (1.0 - ADAM_B2 ** ADAM_STEP)
    return -ADAM_LR * (m_hat / (jnp.sqrt(v_hat) + ADAM_EPS) + ADAM_WD * w), m_new, v_new


_SMALL = (("b_ada", 3 * D_MODEL), ("norm_w", D_MODEL), ("q_norm_w", HEAD_DIM), ("k_norm_w", HEAD_DIM),
          ("rel_bias", REL_BUCKETS * ATTN_HEADS), ("sinks", ATTN_HEADS), ("conv_b", XBC_W), ("dt_bias", SSM_HEADS),
          ("a_log", SSM_HEADS), ("d_skip", SSM_HEADS), ("ssm_norm_w", SSM_W))
_SLOT = tuple(-(-n // 128) * 128 for _, n in _SMALL)
_SLOT_OFF = tuple(int(o) for o in np.cumsum((0,) + _SLOT))
_LOSS_OFF = _SLOT_OFF[-1]
_CW_OFF = _LOSS_OFF + 128
_PACK_N = _CW_OFF + CONV_K * XBC_W


def _pack_partials(small, loss, g_conv_w):
    parts = []
    for (name, n), slot in zip(_SMALL, _SLOT):
        parts.append(small[name].reshape(1, n))
        if slot > n:
            parts.append(jnp.zeros((1, slot - n), F32))
    parts += [loss.reshape(1, 1), jnp.zeros((1, 127), F32), g_conv_w.reshape(1, CONV_K * XBC_W)]
    return jnp.concatenate(parts, axis=1)


def _adam_small(pack_all, w, m, v):
    names = [name for name, _ in _SMALL]

    def body(p_ref, *rest):
        ins, outs = rest[:3 * len(names)], rest[3 * len(names):]

        def total(off, n):
            g = p_ref[0, :, off:off + n]
            for d in range(1, N_DEV):
                g = g + p_ref[d, :, off:off + n]
            return g

        for j, (name, n) in enumerate(_SMALL):
            g = total(_SLOT_OFF[j], n)
            delta, m_new, v_new = _adam_math(g, ins[3 * j][...], ins[3 * j + 1][...], ins[3 * j + 2][...])
            outs[4 * j][...] = g
            outs[4 * j + 1][...] = delta
            outs[4 * j + 2][...] = m_new
            outs[4 * j + 3][...] = v_new
        outs[-1][...] = total(_LOSS_OFF, 1)

    flat = []
    for name, n in _SMALL:
        flat += [w[name].reshape(1, n), m[name].reshape(1, n), v[name].reshape(1, n)]
    out_shape = [jax.ShapeDtypeStruct((1, n), F32) for _, n in _SMALL for _ in range(4)] + [jax.ShapeDtypeStruct((1, 1), F32)]
    out = pl.pallas_call(body, name="adam_small", out_shape=out_shape, compiler_params=_params())(pack_all, *flat)
    res = {name: [out[4 * j + t].reshape(w[name].shape) for t in range(4)] for j, name in enumerate(names)}
    return res, out[-1]


WEIGHTS = ("w_ada", "b_ada", "norm_w", "w_in", "q_norm_w", "k_norm_w", "rel_bias", "sinks", "conv_w", "conv_b",
           "dt_bias", "a_log", "d_skip", "ssm_norm_w", "w_attn_proj", "w_ssm_proj", "w_out")


def kernel(x, c, w_ada, b_ada, norm_w, w_in, q_norm_w, k_norm_w, rel_bias, sinks, conv_w, conv_b, dt_bias, a_log, d_skip, ssm_norm_w, w_attn_proj, w_ssm_proj, w_out, loss_target, m_w_ada, m_b_ada, m_norm_w, m_w_in, m_q_norm_w, m_k_norm_w, m_rel_bias, m_sinks, m_conv_w, m_conv_b, m_dt_bias, m_a_log, m_d_skip, m_ssm_norm_w, m_w_attn_proj, m_w_ssm_proj, m_w_out, v_w_ada, v_b_ada, v_norm_w, v_w_in, v_q_norm_w, v_k_norm_w, v_rel_bias, v_sinks, v_conv_w, v_conv_b, v_dt_bias, v_a_log, v_d_skip, v_ssm_norm_w, v_w_attn_proj, v_w_ssm_proj, v_w_out):
    w = dict(w_ada=w_ada, b_ada=b_ada, norm_w=norm_w, w_in=w_in, q_norm_w=q_norm_w, k_norm_w=k_norm_w,
             rel_bias=rel_bias, sinks=sinks, conv_w=conv_w, conv_b=conv_b, dt_bias=dt_bias, a_log=a_log,
             d_skip=d_skip, ssm_norm_w=ssm_norm_w, w_attn_proj=w_attn_proj, w_ssm_proj=w_ssm_proj, w_out=w_out)
    m = dict(w_ada=m_w_ada, b_ada=m_b_ada, norm_w=m_norm_w, w_in=m_w_in, q_norm_w=m_q_norm_w, k_norm_w=m_k_norm_w,
             rel_bias=m_rel_bias, sinks=m_sinks, conv_w=m_conv_w, conv_b=m_conv_b, dt_bias=m_dt_bias, a_log=m_a_log,
             d_skip=m_d_skip, ssm_norm_w=m_ssm_norm_w, w_attn_proj=m_w_attn_proj, w_ssm_proj=m_w_ssm_proj, w_out=m_w_out)
    v = dict(w_ada=v_w_ada, b_ada=v_b_ada, norm_w=v_norm_w, w_in=v_w_in, q_norm_w=v_q_norm_w, k_norm_w=v_k_norm_w,
             rel_bias=v_rel_bias, sinks=v_sinks, conv_w=v_conv_w, conv_b=v_conv_b, dt_bias=v_dt_bias, a_log=v_a_log,
             d_skip=v_d_skip, ssm_norm_w=v_ssm_norm_w, w_attn_proj=v_w_attn_proj, w_ssm_proj=v_w_ssm_proj, w_out=v_w_out)
    me = 4 * lax.axis_index("x") + 2 * lax.axis_index("y") + lax.axis_index("c")
    ada_n = w_ada.shape[2]
    in_n = w_in.shape[2]
    cw_n = conv_w.shape[2]

    first = _ag_direct(jnp.concatenate([c, conv_w[0].reshape(1, CONV_K * cw_n)], axis=1), "ag_c")[:, 0]
    c_all = first[:, :D_MODEL]
    conv_w_full = first[:, D_MODEL:].reshape(N_DEV, CONV_K, cw_n).transpose(1, 0, 2).reshape(CONV_K, XBC_W)
    b_piece = lax.dynamic_slice_in_dim(b_ada, me * ada_n, ada_n, axis=1)
    mod_all = _ag_direct(_mod_piece(c_all, w_ada[0], b_piece), "ag_mod")
    mod = lax.dynamic_index_in_dim(mod_all, me, axis=1, keepdims=False).reshape(1, 3 * D_MODEL)
    shift, scale, gate = mod[:, :D_MODEL], mod[:, D_MODEL:2 * D_MODEL], mod[:, 2 * D_MODEL:]

    w_t, zero = _ag_two_level(w_in[0].T.astype(BF), "ag_w_in")
    w_t = w_t.reshape(N_DEV * in_n, D_MODEL)

    def with_mine(blocks, mine):
        return lax.dynamic_update_index_in_dim(lax.empty(blocks, mine.dtype), mine, me, axis=0)

    rows = jnp.concatenate([w_attn_proj[0], w_ssm_proj[0], w_out[0]], axis=0).astype(BF) + zero
    r_ap, r_sp = w_attn_proj.shape[1], w_ssm_proj.shape[1]
    rows_started, zero = _exchange_start(rows, with_mine((N_DEV,) + rows.shape, rows), True, "ag_rows_start")

    def rows_fn(after):
        return _exchange_wait(rows_started, after, True, "ag_rows_wait")

    started = {}

    def send_blocks(key, g, name):
        started[key], zero = _exchange_start(
            g, with_mine(g.shape, lax.dynamic_index_in_dim(g, me, axis=0, keepdims=False)), False, name)
        return zero

    def after_mid(g_wap, g_wsp, g_wout):
        return send_blocks("rows", jnp.concatenate(
            [g_wap.reshape(N_DEV, r_ap, D_MODEL), g_wsp.reshape(N_DEV, r_sp, D_MODEL),
             g_wout.reshape(N_DEV, r_ap, D_MODEL)], axis=1), "rs_rows_start")

    def after_gw(g_ws):
        return send_blocks("in", jnp.concatenate(g_ws, axis=0).reshape(N_DEV, in_n, D_MODEL), "rs_in_start")

    r = _local_step(x[0], loss_target[0], shift, scale + zero, gate, w_t, rows_fn, norm_w, q_norm_w, k_norm_w,
                    rel_bias, sinks, conv_w_full, conv_b, dt_bias, a_log, d_skip, ssm_norm_w, after_mid, after_gw)

    small = dict(b_ada=r["dmod"], norm_w=r["g_norm_w"], q_norm_w=r["g_qnw"], k_norm_w=r["g_knw"], rel_bias=r["g_rel"],
                 sinks=r["g_sinks"], conv_b=r["g_conv_b"], dt_bias=r["g_dt_bias"], a_log=r["g_a_log"],
                 d_skip=r["g_d_skip"], ssm_norm_w=r["g_ssm_nw"])
    pack_all = _ag_direct(_pack_partials(small, r["loss"], r["g_conv_w"]), "ag_small")
    res, loss = _adam_small(pack_all, w, m, v)
    loss = loss[0, 0]
    cw_parts = pack_all[:, 0, _CW_OFF:].reshape(N_DEV, CONV_K, XBC_W)
    cw_mine = lax.dynamic_slice_in_dim(cw_parts, me * cw_n, cw_n, axis=2)
    res["conv_w"] = [a[None] for a in _adam(cw_mine, conv_w[0], m_conv_w[0], v_conv_w[0], "adam_conv_w")]

    dmod_piece = lax.dynamic_slice_in_dim(pack_all[:, 0, :3 * D_MODEL], me * ada_n, ada_n, axis=1)
    g_ada = _gw_ada(c_all, dmod_piece)
    res["w_ada"] = [a[None] for a in _adam(g_ada[None], w_ada[0], m_w_ada[0], v_w_ada[0], "adam_w_ada")]

    cat = lambda d: jnp.concatenate([d["w_attn_proj"][0], d["w_ssm_proj"][0], d["w_out"][0]], axis=0)
    rows_res = _adam(_exchange_wait(started["rows"], g_ada, False, "rs_rows_wait"), cat(w), cat(m), cat(v), "adam_w_rows")
    res["w_in"] = [a.T[None] for a in _adam(_exchange_wait(started["in"], rows_res[0], False, "rs_in_wait"),
                                            w_in[0].T, m_w_in[0].T, v_w_in[0].T, "adam_w_in")]
    res["w_attn_proj"] = [a[None, :r_ap] for a in rows_res]
    res["w_ssm_proj"] = [a[None, r_ap:r_ap + r_sp] for a in rows_res]
    res["w_out"] = [a[None, r_ap + r_sp:] for a in rows_res]

    outs = [loss, r["grad_x"][None]]
    for j in range(4):
        outs += [res[name][j] for name in WEIGHTS]
    return tuple(outs)
```

```python
import math

import numpy as np
import jax
import jax.numpy as jnp
from jax import lax
from jax.experimental import pallas as pl
from jax.experimental.pallas import tpu as pltpu

F32 = jnp.float32
BF = jnp.bfloat16
HI = lax.Precision.HIGHEST

D_MODEL = 1024
ATTN_HEADS = 16
KV_HEADS = 4
GRP = ATTN_HEADS // KV_HEADS
HEAD_DIM = 64
ATTN_W = ATTN_HEADS * HEAD_DIM
KV_W = KV_HEADS * HEAD_DIM
BLOCK = 128
REL_BUCKETS = 32
REL_MAX_DIST = 128
SSM_W = 2048
SSM_P = 64
SSM_HEADS = 32
SSM_G = 4
SSM_R = 8
SSM_N = 128
CONV_K = 4
XBC_W = SSM_W + 2 * SSM_G * SSM_N
SEG_W = (ATTN_W, 2 * KV_W, ATTN_W + SSM_W, XBC_W, SSM_HEADS, 2 * D_MODEL)
NSEG = len(SEG_W)
SEG_OFF = tuple(int(v) for v in np.cumsum((0,) + SEG_W))
IN_W = SEG_OFF[-1]
GATE_SEGS = (2, 5)
EPS = 1e-6
N_DEV = 8
ADAM_LR, ADAM_B1, ADAM_B2, ADAM_EPS, ADAM_WD, ADAM_STEP = 0.001, 0.9, 0.999, 1e-08, 0.01, 10
VMEM_LIMIT = 60 * 1024 * 1024
MESH = pl.DeviceIdType.MESH
ANY = pl.BlockSpec(memory_space=pl.ANY)


def _dot(a, b, precision=None):
    return jnp.dot(a, b, preferred_element_type=F32, precision=precision)


def _dot_nt(a, b, precision=None):
    return lax.dot_general(a, b, (((1,), (1,)), ((), ())), preferred_element_type=F32, precision=precision)


def _dot_tn(a, b, precision=None):
    return lax.dot_general(a, b, (((0,), (0,)), ((), ())), preferred_element_type=F32, precision=precision)


def _bf(a):
    return a.astype(BF)


def _sig(a):
    return 0.5 * jnp.tanh(0.5 * a) + 0.5


def _params(**kw):
    return pltpu.CompilerParams(vmem_limit_bytes=VMEM_LIMIT, **kw)


def _full(shape):
    nd = len(shape)
    return pl.BlockSpec(shape, lambda i: (0,) * nd)


def _rows(tm, w):
    return pl.BlockSpec((tm, w), lambda i: (i, 0))


def _inproj(x, norm_w, scale, shift, w_t, tm=256):
    s = x.shape[0]

    def body(x_ref, nw_ref, sc_ref, sh_ref, w_hbm, *rest):
        outs, h_ref, w_vm, sem = rest[:NSEG], rest[NSEG], rest[NSEG + 1], rest[NSEG + 2]
        first = pl.program_id(0) == 0
        cps = [pltpu.make_async_copy(w_hbm.at[SEG_OFF[j]:SEG_OFF[j + 1], :], w_vm.at[SEG_OFF[j]:SEG_OFF[j + 1], :], sem.at[j])
               for j in range(NSEG)]

        def tile(waiting):
            xv = x_ref[...]
            r = lax.rsqrt(jnp.mean(xv * xv, axis=-1, keepdims=True) + EPS)
            h = xv * r * (nw_ref[...] * (1.0 + sc_ref[...])) + sh_ref[...]
            hb = _bf(h)
            h_ref[...] = hb
            for j in range(NSEG):
                if waiting:
                    cps[j].wait()
                outs[j][...] = _dot_nt(hb, w_vm[SEG_OFF[j]:SEG_OFF[j + 1], :]).astype(outs[j].dtype)

        @pl.when(first)
        def _():
            for cp in cps:
                cp.start()
            tile(True)

        @pl.when(jnp.logical_not(first))
        def _():
            tile(False)

    vec = _full((1, D_MODEL))
    return pl.pallas_call(
        body, name="inproj", grid=(s // tm,),
        in_specs=[_rows(tm, D_MODEL), vec, vec, vec, ANY],
        out_specs=[_rows(tm, w) for w in SEG_W] + [_rows(tm, D_MODEL)],
        out_shape=[jax.ShapeDtypeStruct((s, w), BF if j in GATE_SEGS else F32) for j, w in enumerate(SEG_W)]
                  + [jax.ShapeDtypeStruct((s, D_MODEL), BF)],
        scratch_shapes=[pltpu.VMEM((IN_W, D_MODEL), BF), pltpu.SemaphoreType.DMA((NSEG,))],
        compiler_params=_params(dimension_semantics=("arbitrary",)),
    )(x, norm_w, scale, shift, w_t)


def _bucket_onehot_t():
    qi = jnp.arange(BLOCK)[:, None]
    kj = jnp.arange(2 * BLOCK)[None, :]
    dist = qi + BLOCK - kj
    n = jnp.maximum(dist, 0)
    max_exact = REL_BUCKETS // 2
    nf = jnp.maximum(n, 1).astype(F32)
    large = max_exact + (jnp.log(nf / max_exact) / math.log(REL_MAX_DIST / max_exact)
                         * (REL_BUCKETS - max_exact)).astype(jnp.int32)
    large = jnp.minimum(large, REL_BUCKETS - 1)
    bucket = jnp.where(n < max_exact, n, large).reshape(1, BLOCK * 2 * BLOCK)
    return (bucket == jnp.arange(REL_BUCKETS)[:, None]).astype(F32)


def _bias_dense(rel_bias_t, oh_t):
    def body(rb_ref, oh_ref, o_ref):
        o_ref[...] = _dot(rb_ref[...], oh_ref[...], HI)

    return pl.pallas_call(
        body, name="bias_dense", out_shape=jax.ShapeDtypeStruct((ATTN_HEADS, BLOCK * 2 * BLOCK), F32),
        compiler_params=_params(),
    )(rel_bias_t, oh_t)


def _bias_grad(ds_sum, oh_t):
    def body(ds_ref, oh_ref, o_ref):
        o_ref[...] = _dot_nt(ds_ref[...], oh_ref[...], HI)

    return pl.pallas_call(
        body, name="bias_grad", out_shape=jax.ShapeDtypeStruct((ATTN_HEADS, REL_BUCKETS), F32),
        compiler_params=_params(),
    )(ds_sum, oh_t)


def _group_sum(a, e):
    hi = _bf(a)
    return _dot(hi, e) + _dot(_bf(a - hi.astype(F32)), e)


def _group_bcast(a, e3t):
    hi = _bf(a)
    r1 = a - hi.astype(F32)
    mid = _bf(r1)
    return _dot(jnp.concatenate([hi, mid, _bf(r1 - mid.astype(F32))], axis=1), e3t)


def _membership(width, group, ngroups):
    e = (jnp.arange(width)[:, None] // group == jnp.arange(ngroups)[None, :]).astype(BF)
    return e, jnp.tile(e.T, (3, 1))


def _fold(width, group):
    return (jnp.arange(width)[:, None] % group == jnp.arange(group)[None, :]).astype(BF)


def _heads_norm(t, w_x, e, e3t):
    r = lax.rsqrt(_dot(_bf(t * t), e) * (1.0 / HEAD_DIM) + EPS)
    r_x = _group_bcast(r, e3t)
    return t * r_x * w_x, r_x


def _heads_norm_bwd(t, r_x, w_x, d, e, e3t):
    wd = d * w_x
    corr = _group_bcast(_group_sum(t * wd, e) * (1.0 / HEAD_DIM), e3t)
    return r_x * wd - t * (r_x * r_x * r_x) * corr, jnp.sum(d * t * r_x, axis=0, keepdims=True)


def _stack_heads(a, hk):
    return jnp.concatenate([a[:, (hk * GRP + g) * HEAD_DIM:(hk * GRP + g + 1) * HEAD_DIM] for g in range(GRP)], axis=0)


def _stack_cols(a, hk):
    return jnp.concatenate([a[:, hk * GRP + g:hk * GRP + g + 1] for g in range(GRP)], axis=0)


def _masked_bias(bias):
    qi = jnp.arange(BLOCK)[:, None]
    kj = jnp.arange(2 * BLOCK)[None, :]
    cur_ok = jnp.logical_and(kj >= BLOCK, kj - BLOCK <= qi)
    both_ok = jnp.logical_or(jnp.logical_and(kj < BLOCK, kj > qi), cur_ok)
    return jnp.stack([jnp.where(cur_ok, bias, -1e30), jnp.where(both_ok, bias, -1e30)])


def _attn_consts(qnw, knw):
    eq, eq3t = _membership(ATTN_W, HEAD_DIM, ATTN_HEADS)
    ek, ek3t = _membership(KV_W, HEAD_DIM, ATTN_HEADS)
    return (jnp.tile(qnw, (1, ATTN_HEADS)), jnp.tile(knw, (1, KV_HEADS)), eq, eq3t, ek, ek3t)


def _attn_fwd(q, kv, bias, sinks, consts):
    s = q.shape[0]
    nb = s // BLOCK
    gq = GRP * BLOCK
    bias_t = bias.reshape(2, KV_HEADS, GRP, BLOCK, 2 * BLOCK).transpose(0, 1, 4, 2, 3).reshape(2, KV_HEADS, 2 * BLOCK, gq)
    sink_rows = jnp.repeat(sinks.reshape(KV_HEADS, GRP), BLOCK, axis=1).reshape(KV_HEADS, 1, gq)
    eye = jnp.eye(BLOCK, dtype=BF)

    def body(q_ref, kp_ref, kc_ref, vp_ref, vc_ref, b_ref, bt_ref, sk_ref, skr_ref, eye_ref,
             qw_ref, kw_ref, eq_ref, eq3_ref, ek_ref, ek3_ref, o_ref, lse_ref):
        qn = _bf(_heads_norm(q_ref[...], qw_ref[...], eq_ref[...], eq3_ref[...])[0] * (HEAD_DIM ** -0.5))
        kn = _bf(_heads_norm(jnp.concatenate([kp_ref[...], kc_ref[...]], axis=0), kw_ref[...], ek_ref[...], ek3_ref[...])[0])
        vv = _bf(jnp.concatenate([vp_ref[...], vc_ref[...]], axis=0))
        ones = jnp.ones((2 * BLOCK, HEAD_DIM), BF)
        lses = []
        kss = [slice(hk * HEAD_DIM, (hk + 1) * HEAD_DIM) for hk in range(KV_HEADS)]
        qgs = [_stack_heads(qn, hk) for hk in range(KV_HEADS)]
        sc_ts = [_dot_nt(kn[:, kss[hk]], qgs[hk]) + bt_ref[0, hk] for hk in range(KV_HEADS)]
        m_rows = [jnp.maximum(jnp.max(sc_ts[hk], axis=0, keepdims=True), skr_ref[hk]) for hk in range(KV_HEADS)]
        m8s = [_bf(jnp.broadcast_to(m + jnp.abs(m) * (2.0 ** -7), (8, gq))) for m in m_rows]
        ms = [jnp.concatenate([_dot_nt(eye_ref[...], m8[:, g * BLOCK:(g + 1) * BLOCK])[:, 0:1] for g in range(GRP)], axis=0)
              for m8 in m8s]
        scs = [_dot_nt(qgs[hk], kn[:, kss[hk]]) + b_ref[0, hk * GRP:(hk + 1) * GRP].reshape(gq, 2 * BLOCK)
               for hk in range(KV_HEADS)]
        ps = [_bf(jnp.exp(scs[hk] - ms[hk])) for hk in range(KV_HEADS)]
        pvs = [_dot(ps[hk], jnp.concatenate([vv[:, kss[hk]], ones], axis=1)) for hk in range(KV_HEADS)]
        for hk in range(KV_HEADS):
            m, pv = ms[hk], pvs[hk]
            sink = jnp.concatenate([jnp.full((BLOCK, 1), sk_ref[0, hk * GRP + g], F32) for g in range(GRP)], axis=0)
            den = pv[:, HEAD_DIM:HEAD_DIM + 1] + jnp.exp(sink - m)
            out = pv[:, :HEAD_DIM] * (1.0 / den)
            lse = m + jnp.log(den)
            for g in range(GRP):
                h = hk * GRP + g
                o_ref[:, h * HEAD_DIM:(h + 1) * HEAD_DIM] = out[g * BLOCK:(g + 1) * BLOCK]
                lses.append(lse[g * BLOCK:(g + 1) * BLOCK])
        lse_ref[...] = jnp.concatenate(lses, axis=1)

    cur = lambda w, col=0: pl.BlockSpec((BLOCK, w), lambda i: (i, col))
    prev = lambda w, col=0: pl.BlockSpec((BLOCK, w), lambda i: (jnp.maximum(i - 1, 0), col))
    whole = lambda a: pl.BlockSpec(a.shape, lambda i: (0,) * a.ndim)
    first_or_not = lambda a: pl.BlockSpec((1,) + a.shape[1:], lambda i: (jnp.minimum(i, 1),) + (0,) * (a.ndim - 1))
    return pl.pallas_call(
        body, name="attn_fwd", grid=(nb,),
        in_specs=[cur(ATTN_W), prev(KV_W, 0), cur(KV_W, 0), prev(KV_W, 1), cur(KV_W, 1),
                  first_or_not(bias), first_or_not(bias_t),
                  pl.BlockSpec(memory_space=pltpu.SMEM), whole(sink_rows), whole(eye)] + [_full(c.shape) for c in consts],
        out_specs=[cur(ATTN_W), cur(ATTN_HEADS)],
        out_shape=[jax.ShapeDtypeStruct((s, ATTN_W), F32), jax.ShapeDtypeStruct((s, ATTN_HEADS), F32)],
        compiler_params=_params(dimension_semantics=("arbitrary",)),
    )(q, kv, kv, kv, kv, bias, bias_t, sinks, sink_rows, eye, *consts)


def _conv_taps(xbc, tail):
    ext = jnp.concatenate([tail, xbc], axis=0)
    return [pltpu.roll(ext, CONV_K - 1 - j, axis=0)[8:8 + BLOCK] if j < CONV_K - 1 else xbc for j in range(CONV_K)]


def _softplus(u):
    return jnp.maximum(u, 0.0) + jnp.log(1.0 + jnp.exp(-jnp.abs(u)))


def _tril():
    r = lax.broadcasted_iota(jnp.int32, (BLOCK, BLOCK), 0)
    c = lax.broadcasted_iota(jnp.int32, (BLOCK, BLOCK), 1)
    return r >= c


def _triu():
    r = lax.broadcasted_iota(jnp.int32, (BLOCK, BLOCK), 0)
    c = lax.broadcasted_iota(jnp.int32, (BLOCK, BLOCK), 1)
    return r <= c


def _exact_left(m01, a):
    hi = _bf(a)
    r1 = a - hi.astype(F32)
    mid = _bf(r1)
    return _dot(m01, hi) + _dot(m01, mid) + _dot(m01, _bf(r1 - mid.astype(F32)))


def _ssd_common(conv, dtr, dtb_ref, alog_ref, e3_ref):
    sg = _sig(conv)
    xact = conv * sg
    u = dtr + dtb_ref[...]
    dt = _softplus(u)
    a = -jnp.exp(alog_ref[...])
    trilb = _tril()
    acum = _exact_left(trilb.astype(BF), dt * a) * math.log2(math.e)
    both = _group_bcast(jnp.concatenate([dt, acum], axis=0), e3_ref[...])
    dt_x, acum_x = both[:BLOCK], both[BLOCK:]
    return sg, xact, u, dt, a, trilb, acum, dt_x, acum_x


SSD_CH = 2


def _ssd_fwd(xbc, dt_raw, conv_w, conv_b, dt_bias, a_log, dsk_x, e3t):
    s = xbc.shape[0]
    nc = s // BLOCK
    ch = SSD_CH if nc % SSD_CH == 0 else 1
    rows = ch * BLOCK

    def body(x_ref, tail_ref, dtr_ref, cw_ref, cb_ref, dtb_ref, alog_ref, dsk_ref, e3_ref,
             y_ref, hp_ref, conv_ref, hst, yd_s, yoff_s):
        i = pl.program_id(0)

        @pl.when(i == 0)
        def _():
            hst[...] = jnp.zeros_like(hst)

        for j in range(ch):
            rs = slice(j * BLOCK, (j + 1) * BLOCK)
            tail = jnp.where(i > 0, tail_ref[...], 0.0) if j == 0 else x_ref[j * BLOCK - 8:j * BLOCK, :]
            taps = _conv_taps(x_ref[rs, :], tail)
            conv = cb_ref[...] + sum(taps[t] * cw_ref[t:t + 1, :] for t in range(CONV_K))
            conv_ref[rs, :] = conv
            _, xact, _, _, _, trilb, acum, dt_x, acum_x = _ssd_common(conv, dtr_ref[rs, :], dtb_ref, alog_ref, e3_ref)
            xs = xact[:, :SSM_W]
            acum_t = acum.T
            ea_x = jnp.exp2(acum_x)
            last_x = acum_x[BLOCK - 1:BLOCK, :]
            xdt = xs * dt_x
            xw = xdt * jnp.exp2(last_x - acum_x)
            cd_x = jnp.exp2(last_x)
            hprev = hst[...]
            hp_ref[j] = hprev
            sls = [slice(g * SSM_R * SSM_P, (g + 1) * SSM_R * SSM_P) for g in range(SSM_G)]
            bgs = [_bf(xact[:, SSM_W + g * SSM_N:SSM_W + (g + 1) * SSM_N]) for g in range(SSM_G)]
            cgs = [_bf(xact[:, SSM_W + SSM_G * SSM_N + g * SSM_N:SSM_W + SSM_G * SSM_N + (g + 1) * SSM_N])
                   for g in range(SSM_G)]
            xdt_b, xw_b, hprev_b = _bf(xdt), _bf(xw), _bf(hprev)
            low_half = lax.broadcasted_iota(jnp.int32, (BLOCK, 2 * SSM_P), 1) < SSM_P
            cbs = [_dot_nt(cgs[g], bgs[g]) for g in range(SSM_G)]
            for g in range(SSM_G):
                sl = sls[g]
                yoff_s[:, sl] = _dot(cgs[g], hprev_b[:, sl]) * ea_x[:, sl]
                hst[:, sl] = hprev[:, sl] * cd_x[:, sl] + _dot_tn(bgs[g], xw_b[:, sl])
            for g in range(SSM_G):
                hss = [slice((g * SSM_R + r) * SSM_P, (g * SSM_R + r + 1) * SSM_P) for r in range(SSM_R)]
                mms = [_bf(cbs[g] * jnp.exp2(jnp.where(trilb, acum[:, g * SSM_R + r:g * SSM_R + r + 1]
                                                      - acum_t[g * SSM_R + r:g * SSM_R + r + 1, :], -1e30)))
                       for r in range(SSM_R)]
                for r in range(0, SSM_R, 2):
                    pair = slice(hss[r].start, hss[r + 1].stop)
                    xp = xdt_b[:, pair]
                    rhs = jnp.concatenate([jnp.where(low_half, xp, 0), jnp.where(low_half, 0, xp)], axis=0)
                    yd_s[:, pair] = _dot(jnp.concatenate([mms[r], mms[r + 1]], axis=1), rhs)
            y_ref[rs, :] = yd_s[...] + yoff_s[...] + dsk_ref[...] * xs

    blk = lambda w: pl.BlockSpec((rows, w), lambda i: (i, 0))
    return pl.pallas_call(
        body, name="ssd_fwd", grid=(nc // ch,),
        in_specs=[blk(XBC_W), pl.BlockSpec((8, XBC_W), lambda i: (jnp.maximum(i * (rows // 8) - 1, 0), 0)),
                  blk(SSM_HEADS), _full((CONV_K, XBC_W)), _full((1, XBC_W)), _full((1, SSM_HEADS)),
                  _full((1, SSM_HEADS)), _full((1, SSM_W)), _full((3 * SSM_HEADS, SSM_W))],
        out_specs=[blk(SSM_W), pl.BlockSpec((ch, SSM_N, SSM_W), lambda i: (i, 0, 0)), blk(XBC_W)],
        out_shape=[jax.ShapeDtypeStruct((s, SSM_W), F32), jax.ShapeDtypeStruct((nc, SSM_N, SSM_W), F32),
                   jax.ShapeDtypeStruct((s, XBC_W), F32)],
        scratch_shapes=[pltpu.VMEM((SSM_N, SSM_W), F32), pltpu.VMEM((BLOCK, SSM_W), F32), pltpu.VMEM((BLOCK, SSM_W), F32)],
        compiler_params=_params(dimension_semantics=("arbitrary",)),
    )(xbc, xbc, dt_raw, conv_w, conv_b, dt_bias, a_log, dsk_x, e3t)


def _dsilu(z, sg, silu):
    return sg * (1.0 + (z - silu))


def _mid(x, tgt, o_att, zam, ypre, gab, gate, ssm_nw, rows_all, tm=256):
    s = x.shape[0]
    gw = SSM_W // SSM_G

    r_ap, r_sp = ATTN_W // N_DEV, SSM_W // N_DEV

    def body(x_ref, t_ref, o_ref, zam_ref, yp_ref, gab_ref, gate_ref, nw_ref, rows_h,
             dout_ref, do_ref, dzam_ref, dyp_ref, dgab_ref,
             yag_ref, dya_ref, yn_ref, dyb_ref, mg_ref, dob_ref, gnw_ref, dgate_ref, loss_ref,
             wap_v, wsp_v, wout_v, sem):
        i = pl.program_id(0)

        @pl.when(i == 0)
        def _():
            cps = []
            for d in range(N_DEV):
                for j, (dst, r0, rn) in enumerate(((wap_v, 0, r_ap), (wsp_v, r_ap, r_sp), (wout_v, r_ap + r_sp, r_ap))):
                    cps.append(pltpu.make_async_copy(rows_h.at[d, r0:r0 + rn, :], dst.at[d * rn:(d + 1) * rn, :], sem.at[j]))
            for cp in cps:
                cp.start()
            gnw_ref[...] = jnp.zeros_like(gnw_ref)
            dgate_ref[...] = jnp.zeros_like(dgate_ref)
            loss_ref[...] = jnp.zeros_like(loss_ref)
            for cp in cps:
                cp.wait()

        gate = gate_ref[...]
        nw = nw_ref[...]
        o_att = o_ref[...]
        z_a = zam_ref[:, :ATTN_W].astype(F32)
        s_a = _sig(z_a)
        silu_a = z_a * s_a
        yag = _bf(o_att * silu_a)
        yag_ref[...] = yag
        ypre = yp_ref[...]
        z_m = zam_ref[:, ATTN_W:].astype(F32)
        s_m = _sig(z_m)
        silu_m = z_m * s_m
        yg = ypre * silu_m
        rinv = jnp.concatenate(
            [jnp.broadcast_to(lax.rsqrt(jnp.mean(yg[:, g * gw:(g + 1) * gw] ** 2, axis=-1, keepdims=True) + EPS), (tm, gw))
             for g in range(SSM_G)], axis=1)
        ynr = yg * rinv
        yn = _bf(ynr * nw)
        yn_ref[...] = yn
        y_a = _dot(yag, wap_v[...])
        y_b = _dot(yn, wsp_v[...])
        g_a = _sig(gab_ref[:, :D_MODEL].astype(F32))
        g_b = _sig(gab_ref[:, D_MODEL:].astype(F32))
        merged = _bf(g_a * y_a + g_b * y_b)
        mg_ref[...] = merged
        o = _dot(merged, wout_v[...])
        diff = x_ref[...] + gate * o - t_ref[...]
        loss_ref[...] += (0.5 / D_MODEL) * jnp.sum(diff * diff, axis=(0, 1), keepdims=True)
        dout = diff * (1.0 / D_MODEL)
        dout_ref[...] = dout
        dgate_ref[...] += jnp.sum(dout * o, axis=0, keepdims=True)
        d_o = _bf(dout * gate)
        dob_ref[...] = d_o
        dmerged = _dot_nt(d_o, wout_v[...])
        dy_af = dmerged * g_a
        dy_bf = dmerged * g_b
        dy_a = _bf(dy_af)
        dy_b = _bf(dy_bf)
        dya_ref[...] = dy_a
        dyb_ref[...] = dy_b
        dyag = _dot_nt(dy_a, wap_v[...])
        dyn = _dot_nt(dy_b, wsp_v[...])
        dgab_ref[:, :D_MODEL] = _bf(dy_af * y_a * (1.0 - g_a))
        dgab_ref[:, D_MODEL:] = _bf(dy_bf * y_b * (1.0 - g_b))
        do_ref[...] = dyag * silu_a
        dzam_ref[:, :ATTN_W] = _bf(dyag * o_att * _dsilu(z_a, s_a, silu_a))
        gnw_ref[...] += jnp.sum(dyn * ynr, axis=0, keepdims=True)
        dynw = dyn * nw
        corr = jnp.concatenate(
            [jnp.broadcast_to(jnp.mean((dynw * ynr)[:, g * gw:(g + 1) * gw], axis=-1, keepdims=True), (tm, gw))
             for g in range(SSM_G)], axis=1)
        dyg = rinv * (dynw - ynr * corr)
        dyp_ref[...] = dyg * silu_m
        dzam_ref[:, ATTN_W:] = _bf(dyg * ypre * _dsilu(z_m, s_m, silu_m))

    r1, r2, r3 = _rows(tm, D_MODEL), _rows(tm, SSM_W), _rows(tm, ATTN_W + SSM_W)
    sd = jax.ShapeDtypeStruct
    return pl.pallas_call(
        body, name="mid", grid=(s // tm,),
        in_specs=[r1, r1, r1, r3, r2, r2, _full((1, D_MODEL)), _full((1, SSM_W)), ANY],
        out_specs=[r1, r1, r3, r2, r2, r1, r1, r2, r1, r1, r1,
                   _full((1, SSM_W)), _full((1, D_MODEL)), _full((1, 1))],
        out_shape=[sd((s, D_MODEL), F32), sd((s, ATTN_W), F32), sd((s, ATTN_W + SSM_W), BF), sd((s, SSM_W), F32),
                   sd((s, 2 * D_MODEL), BF),
                   sd((s, ATTN_W), BF), sd((s, D_MODEL), BF), sd((s, SSM_W), BF), sd((s, D_MODEL), BF),
                   sd((s, D_MODEL), BF), sd((s, D_MODEL), BF),
                   sd((1, SSM_W), F32), sd((1, D_MODEL), F32), sd((1, 1), F32)],
        scratch_shapes=[pltpu.VMEM((ATTN_W, D_MODEL), BF), pltpu.VMEM((SSM_W, D_MODEL), BF), pltpu.VMEM((D_MODEL, D_MODEL), BF),
                        pltpu.SemaphoreType.DMA((3,))],
        compiler_params=_params(dimension_semantics=("arbitrary",)),
    )(x, tgt, o_att, zam, ypre, gab, gate, ssm_nw, rows_all)


def _attn_bwd(q, kv, bias, sinks, consts, o_att, lse, d_o):
    s = q.shape[0]
    nb = s // BLOCK
    folds = (_fold(ATTN_W, HEAD_DIM), _fold(KV_W, HEAD_DIM))

    def body(q_ref, kp_ref, kc_ref, vp_ref, vc_ref, b_ref, skv_ref, qw_ref, kw_ref, eq_ref, eq3_ref, ek_ref, ek3_ref,
             fq_ref, fk_ref, o_ref, lse_ref, do_ref,
             dq_ref, dkv_ref, dss_ref, gqw_ref, gkw_ref, gsk_ref, ckn, cv, dqn_s, dkn_s, dv_s, gq_x, gk_x):
        i = pl.program_id(0)
        kw, ek, ek3 = kw_ref[...], ek_ref[...], ek3_ref[...]

        @pl.when(i == 0)
        def _():
            for ref in (ckn, cv, dss_ref, gq_x, gk_x, gsk_ref):
                ref[...] = jnp.zeros_like(ref)

        @pl.when(i < nb)
        def _():
            qw, eq, eq3 = qw_ref[...], eq_ref[...], eq3_ref[...]
            qf = q_ref[...]
            qnf, rq_x = _heads_norm(qf, qw, eq, eq3)
            qn = _bf(qnf * (HEAD_DIM ** -0.5))
            kf = jnp.concatenate([kp_ref[...], kc_ref[...]], axis=0)
            knf, rk_x = _heads_norm(kf, kw, ek, ek3)
            kn = _bf(knf)
            vv = _bf(jnp.concatenate([vp_ref[...], vc_ref[...]], axis=0))
            d_of = do_ref[...]
            d_ob = _bf(d_of)
            lse_all = lse_ref[...]
            delta = _group_sum(d_of * o_ref[...], eq)
            gsk_ref[...] += jnp.sum(-jnp.exp(skv_ref[...] - lse_all) * delta, axis=0, keepdims=True)
            kss = [slice(hk * HEAD_DIM, (hk + 1) * HEAD_DIM) for hk in range(KV_HEADS)]
            qgs = [_stack_heads(qn, hk) for hk in range(KV_HEADS)]
            d_ogs = [_stack_heads(d_ob, hk) for hk in range(KV_HEADS)]
            scs = [_dot_nt(qgs[hk], kn[:, kss[hk]]) + b_ref[0, hk * GRP:(hk + 1) * GRP].reshape(GRP * BLOCK, 2 * BLOCK)
                   for hk in range(KV_HEADS)]
            dps = [_dot_nt(d_ogs[hk], vv[:, kss[hk]]) for hk in range(KV_HEADS)]
            ps = [jnp.exp(scs[hk] - _stack_cols(lse_all, hk)) for hk in range(KV_HEADS)]
            dss = [ps[hk] * (dps[hk] - _stack_cols(delta, hk)) for hk in range(KV_HEADS)]
            pbs = [_bf(p) for p in ps]
            dsbs = [_bf(ds) for ds in dss]
            for hk in range(KV_HEADS):
                dss_ref[hk * GRP:(hk + 1) * GRP] += dss[hk].reshape(GRP, BLOCK, 2 * BLOCK)
            for hk in range(KV_HEADS):
                dv_s[:, kss[hk]] = _dot_tn(pbs[hk], d_ogs[hk])
                dkn_s[:, kss[hk]] = _dot_tn(dsbs[hk], qgs[hk])
            dqns = [_dot(dsbs[hk], kn[:, kss[hk]]) * (HEAD_DIM ** -0.5) for hk in range(KV_HEADS)]
            for hk in range(KV_HEADS):
                for g in range(GRP):
                    h = hk * GRP + g
                    dqn_s[:, h * HEAD_DIM:(h + 1) * HEAD_DIM] = dqns[hk][g * BLOCK:(g + 1) * BLOCK]
            dq, gq = _heads_norm_bwd(qf, rq_x, qw, dqn_s[...], eq, eq3)
            dq_ref[...] = _bf(dq)
            gq_x[...] += gq
            dk, gk = _heads_norm_bwd(kf[:BLOCK], rk_x[:BLOCK], kw, ckn[...] + dkn_s[0:BLOCK, :], ek, ek3)
            dkv_ref[:, :KV_W] = _bf(dk)
            gk_x[...] += gk
            dkv_ref[:, KV_W:] = _bf(cv[...] + dv_s[0:BLOCK, :])
            ckn[...] = dkn_s[BLOCK:2 * BLOCK, :]
            cv[...] = dv_s[BLOCK:2 * BLOCK, :]

        @pl.when(i == nb)
        def _():
            kc = kc_ref[...]
            dk, gk = _heads_norm_bwd(kc, _heads_norm(kc, kw, ek, ek3)[1], kw, ckn[...], ek, ek3)
            dkv_ref[:, :KV_W] = _bf(dk)
            dkv_ref[:, KV_W:] = _bf(cv[...])
            gqw_ref[...] = _group_sum(jnp.broadcast_to(gq_x[...], (8, ATTN_W)), fq_ref[...])[0:1]
            gkw_ref[...] = _group_sum(jnp.broadcast_to(gk_x[...] + gk, (8, KV_W)), fk_ref[...])[0:1]

    last = nb - 1
    cur = lambda w, col=0: pl.BlockSpec((BLOCK, w), lambda i: (jnp.minimum(i, last), col))
    prev = lambda w, col=0: pl.BlockSpec((BLOCK, w), lambda i: (jnp.maximum(jnp.minimum(i, last) - 1, 0), col))
    late = lambda w: pl.BlockSpec((BLOCK, w), lambda i: (jnp.maximum(i - 1, 0), 0))
    sd = jax.ShapeDtypeStruct
    return pl.pallas_call(
        body, name="attn_bwd", grid=(nb + 1,),
        in_specs=[cur(ATTN_W), prev(KV_W, 0), cur(KV_W, 0), prev(KV_W, 1), cur(KV_W, 1),
                  pl.BlockSpec((1, ATTN_HEADS, BLOCK, 2 * BLOCK), lambda i: (jnp.minimum(i, 1), 0, 0, 0)),
                  _full((1, ATTN_HEADS))]
                 + [_full(c.shape) for c in consts + folds] + [cur(ATTN_W), cur(ATTN_HEADS), cur(ATTN_W)],
        out_specs=[cur(ATTN_W), late(2 * KV_W),
                   pl.BlockSpec((ATTN_HEADS, BLOCK, 2 * BLOCK), lambda i: (0, 0, 0)),
                   _full((1, HEAD_DIM)), _full((1, HEAD_DIM)), _full((1, ATTN_HEADS))],
        out_shape=[sd((s, ATTN_W), BF), sd((s, 2 * KV_W), BF),
                   sd((ATTN_HEADS, BLOCK, 2 * BLOCK), F32), sd((1, HEAD_DIM), F32), sd((1, HEAD_DIM), F32),
                   sd((1, ATTN_HEADS), F32)],
        scratch_shapes=[pltpu.VMEM((BLOCK, KV_W), F32), pltpu.VMEM((BLOCK, KV_W), F32),
                        pltpu.VMEM((BLOCK, ATTN_W), F32), pltpu.VMEM((2 * BLOCK, KV_W), F32),
                        pltpu.VMEM((2 * BLOCK, KV_W), F32), pltpu.VMEM((1, ATTN_W), F32), pltpu.VMEM((1, KV_W), F32)],
        compiler_params=_params(dimension_semantics=("arbitrary",)),
    )(q, kv, kv, kv, kv, bias, sinks, *consts, *folds, o_att, lse, d_o)


def _ssd_bwd(xbc, conv_all, dt_raw, conv_w, dt_bias, a_log, dsk_x, e_mat, e3t, hprev_all, dy_all):
    s = xbc.shape[0]
    nc = s // BLOCK
    ch = 1
    rows = ch * BLOCK
    nsteps = nc // ch
    gw = SSM_R * SSM_P
    b0, c0 = SSM_W, SSM_W + SSM_G * SSM_N

    def body(x_ref, conv_ref, dtr_ref, cw_ref, dtb_ref, alog_ref, dsk_ref, e_ref, e3_ref, hp_ref, dy_ref,
             dx_ref, ddt_ref, gcw_ref, gcb_ref, gdtb_ref, galog_ref, gdsk_ref,
             dh, nhead, gdskx, dxdt_s, dbc_s, dxd_s):
        def chunk_bwd(j):
            rs = slice(j * BLOCK, (j + 1) * BLOCK)
            conv = conv_ref[rs, :]
            sg, xact, u, dt, a, trilb, acum, dt_x, acum_x = _ssd_common(conv, dtr_ref[rs, :], dtb_ref, alog_ref, e3_ref)
            xs = xact[:, :SSM_W]
            acum_t = acum.T
            ea_x = jnp.exp2(acum_x)
            last_x = acum_x[BLOCK - 1:BLOCK, :]
            dte_x = jnp.exp2(last_x - acum_x)
            cd_x = jnp.exp2(last_x)
            xdt = xs * dt_x
            xw = xdt * dte_x
            hprev = hp_ref[j]
            dhn = dh[...]
            dy = dy_ref[rs, :]
            gdskx[...] += jnp.sum(dy * xs, axis=0, keepdims=True)
            dyea = dy * ea_x
            lane = lax.broadcasted_iota(jnp.int32, (BLOCK, SSM_HEADS), 1)
            dacum = jnp.zeros((BLOCK, SSM_HEADS), F32)
            dacc_x, dlast_x = [], []
            sls = [slice(g * gw, (g + 1) * gw) for g in range(SSM_G)]
            bgs = [_bf(xact[:, b0 + g * SSM_N:b0 + (g + 1) * SSM_N]) for g in range(SSM_G)]
            cgs = [_bf(xact[:, c0 + g * SSM_N:c0 + (g + 1) * SSM_N]) for g in range(SSM_G)]
            hpgs = [_bf(hprev[:, sl]) for sl in sls]
            dhgs = [_bf(dhn[:, sl]) for sl in sls]
            dyeags = [_bf(dyea[:, sl]) for sl in sls]
            xwgs = [_bf(xw[:, sl]) for sl in sls]
            xdt_b, dy_b = _bf(xdt), _bf(dy)
            low_half = lax.broadcasted_iota(jnp.int32, (BLOCK, 2 * SSM_P), 1) < SSM_P
            cbs = [_dot_nt(cgs[g], bgs[g]) for g in range(SSM_G)]
            gmats = [_dot(cgs[g], hpgs[g]) for g in range(SSM_G)]
            dxws = [_dot(bgs[g], dhgs[g]) for g in range(SSM_G)]
            dcgs = [_dot_nt(dyeags[g], hpgs[g]) for g in range(SSM_G)]
            dbgs = [_dot_nt(xwgs[g], dhgs[g]) for g in range(SSM_G)]
            for g in range(SSM_G):
                sl = sls[g]
                dh[:, sl] = dhn[:, sl] * cd_x[:, sl] + _dot_tn(cgs[g], dyeags[g])
                dxdt_s[:, sl] = dxws[g] * dte_x[:, sl]
                dacc_x.append(dy[:, sl] * gmats[g] * ea_x[:, sl] - dxws[g] * xw[:, sl])
                dlast_x.append(jnp.sum(dxws[g] * xw[:, sl], axis=0, keepdims=True)
                               + jnp.sum(dhn[:, sl] * hprev[:, sl], axis=0, keepdims=True) * cd_x[:, sl])
            for g in range(SSM_G):
                bg, cg, cb, dbg, dcg = bgs[g], cgs[g], cbs[g], dbgs[g], dcgs[g]
                hss = [slice((g * SSM_R + r) * SSM_P, (g * SSM_R + r + 1) * SSM_P) for r in range(SSM_R)]
                lms = [jnp.exp2(jnp.where(trilb, acum[:, g * SSM_R + r:g * SSM_R + r + 1]
                                         - acum_t[g * SSM_R + r:g * SSM_R + r + 1, :], -1e30)) for r in range(SSM_R)]
                mms = [cb * lm for lm in lms]
                mmbs = [_bf(mm) for mm in mms]
                dms = []
                for r in range(0, SSM_R, 2):
                    pair = slice(hss[r].start, hss[r + 1].stop)
                    xp, dyp = xdt_b[:, pair], dy_b[:, pair]
                    dmp = _dot_nt(dyp, jnp.concatenate([jnp.where(low_half, xp, 0), jnp.where(low_half, 0, xp)], axis=0))
                    dms += [dmp[:, :BLOCK], dmp[:, BLOCK:]]
                    dxd_s[:, pair] = _dot_tn(jnp.concatenate([mmbs[r], mmbs[r + 1]], axis=0),
                                             jnp.concatenate([jnp.where(low_half, dyp, 0), jnp.where(low_half, 0, dyp)], axis=0))
                dcb = sum(dms[r] * lms[r] for r in range(SSM_R))
                wms = [dms[r] * mms[r] for r in range(SSM_R)]
                antis = [_bf(wm - wm.T) for wm in wms]
                for r in range(SSM_R):
                    dacum = dacum + _dot(antis[r], (lane == g * SSM_R + r).astype(BF))
                dcbb = _bf(dcb)
                dbc_s[:, g * SSM_N:(g + 1) * SSM_N] = dbg + _dot_tn(dcbb, cg)
                dbc_s[:, SSM_G * SSM_N + g * SSM_N:SSM_G * SSM_N + (g + 1) * SSM_N] = dcg + _dot(dcbb, bg)
            dxdt = dxdt_s[...] + dxd_s[...]
            dxs = dy * dsk_ref[...] + dxdt * dt_x
            red = _group_sum(jnp.concatenate(
                [dxdt * xs, jnp.concatenate(dacc_x, axis=1),
                 jnp.broadcast_to(jnp.concatenate(dlast_x, axis=1), (8, SSM_W))], axis=0), e_ref[...])
            row = lax.broadcasted_iota(jnp.int32, (BLOCK, SSM_HEADS), 0)
            dacum = dacum + red[BLOCK:2 * BLOCK] + jnp.where(row == BLOCK - 1, red[2 * BLOCK:2 * BLOCK + 1], 0.0)
            ddta = _exact_left(_triu().astype(BF), dacum)
            ddt = red[:BLOCK] + ddta * a
            galog_ref[...] += jnp.sum(ddta * dt, axis=0, keepdims=True) * a
            du = ddt * _sig(u)
            ddt_ref[rs, :] = _bf(du)
            gdtb_ref[...] += jnp.sum(du, axis=0, keepdims=True)
            dconv = jnp.concatenate([dxs, dbc_s[...]], axis=1) * _dsilu(conv, sg, xact)
            gcb_ref[...] += jnp.sum(dconv, axis=0, keepdims=True)
            ext2 = jnp.concatenate([dconv, nhead[...]], axis=0)
            ahead = [pltpu.roll(ext2, BLOCK + 8 - (CONV_K - 1 - j), axis=0)[0:BLOCK] if j < CONV_K - 1 else dconv
                     for j in range(CONV_K)]
            dx_ref[rs, :] = _bf(sum(ahead[j] * cw_ref[j:j + 1, :] for j in range(CONV_K)))
            xraw = x_ref[rs, :]
            gcw_ref[...] += jnp.concatenate([jnp.sum(ahead[j] * xraw, axis=0, keepdims=True) for j in range(CONV_K)], axis=0)
            nhead[...] = dconv[0:8]

        i = pl.program_id(0)

        @pl.when(i == 0)
        def _():
            for ref in (dh, nhead, gdskx, gcw_ref, gcb_ref, gdtb_ref, galog_ref, gdsk_ref):
                ref[...] = jnp.zeros_like(ref)

        for j in reversed(range(ch)):
            chunk_bwd(j)

        @pl.when(i == nsteps - 1)
        def _():
            gdsk_ref[...] = _group_sum(jnp.broadcast_to(gdskx[...], (8, SSM_W)), e_ref[...])[0:1]

    chunk = lambda w: pl.BlockSpec((rows, w), lambda i: (nsteps - 1 - i, 0))
    sd = jax.ShapeDtypeStruct
    return pl.pallas_call(
        body, name="ssd_bwd", grid=(nsteps,),
        in_specs=[chunk(XBC_W), chunk(XBC_W),
                  chunk(SSM_HEADS), _full((CONV_K, XBC_W)), _full((1, SSM_HEADS)),
                  _full((1, SSM_HEADS)), _full((1, SSM_W)), _full((SSM_W, SSM_HEADS)), _full((3 * SSM_HEADS, SSM_W)),
                  pl.BlockSpec((ch, SSM_N, SSM_W), lambda i: (nsteps - 1 - i, 0, 0)), chunk(SSM_W)],
        out_specs=[chunk(XBC_W), chunk(SSM_HEADS), _full((CONV_K, XBC_W)), _full((1, XBC_W)),
                   _full((1, SSM_HEADS)), _full((1, SSM_HEADS)), _full((1, SSM_HEADS))],
        out_shape=[sd((s, XBC_W), BF), sd((s, SSM_HEADS), BF), sd((CONV_K, XBC_W), F32), sd((1, XBC_W), F32),
                   sd((1, SSM_HEADS), F32), sd((1, SSM_HEADS), F32), sd((1, SSM_HEADS), F32)],
        scratch_shapes=[pltpu.VMEM((SSM_N, SSM_W), F32), pltpu.VMEM((8, XBC_W), F32),
                        pltpu.VMEM((1, SSM_W), F32), pltpu.VMEM((BLOCK, SSM_W), F32),
                        pltpu.VMEM((BLOCK, 2 * SSM_G * SSM_N), F32), pltpu.VMEM((BLOCK, SSM_W), F32)],
        compiler_params=_params(dimension_semantics=("arbitrary",)),
    )(xbc, conv_all, dt_raw, conv_w, dt_bias, a_log, dsk_x, e_mat, e3t, hprev_all, dy_all)


def _dh(x, dout, norm_w, scale, dsegs, w_t, tm=256):
    s = x.shape[0]

    def body(x_ref, dout_ref, nw_ref, sc_ref, *rest):
        d_refs, w_hbm = rest[:NSEG], rest[NSEG]
        gx_ref, dshift_ref, dscale_ref, gnw_ref = rest[NSEG + 1:NSEG + 5]
        w_vm, sem = rest[NSEG + 5], rest[NSEG + 6]
        first = pl.program_id(0) == 0
        cps = [pltpu.make_async_copy(w_hbm.at[SEG_OFF[j]:SEG_OFF[j + 1], :], w_vm.at[SEG_OFF[j]:SEG_OFF[j + 1], :], sem.at[j])
               for j in range(NSEG)]

        def tile(waiting):
            dh = None
            for j in range(NSEG):
                if waiting:
                    cps[j].wait()
                part = _dot(d_refs[j][...], w_vm[SEG_OFF[j]:SEG_OFF[j + 1], :])
                dh = part if dh is None else dh + part
            xv = x_ref[...]
            r = lax.rsqrt(jnp.mean(xv * xv, axis=-1, keepdims=True) + EPS)
            xn = xv * r
            nw = nw_ref[...]
            sc1 = 1.0 + sc_ref[...]
            dshift_ref[...] += jnp.sum(dh, axis=0, keepdims=True)
            dhxn = jnp.sum(dh * xn, axis=0, keepdims=True)
            dscale_ref[...] += dhxn * nw
            gnw_ref[...] += dhxn * sc1
            dxn = dh * (nw * sc1)
            gx_ref[...] = dout_ref[...] + r * (dxn - xn * jnp.mean(xn * dxn, axis=-1, keepdims=True))

        @pl.when(first)
        def _():
            for cp in cps:
                cp.start()
            for ref in (dshift_ref, dscale_ref, gnw_ref):
                ref[...] = jnp.zeros_like(ref)
            tile(True)

        @pl.when(jnp.logical_not(first))
        def _():
            tile(False)

    vec = _full((1, D_MODEL))
    sd = jax.ShapeDtypeStruct
    return pl.pallas_call(
        body, name="dh", grid=(s // tm,),
        in_specs=[_rows(tm, D_MODEL), _rows(tm, D_MODEL), vec, vec] + [_rows(tm, w) for w in SEG_W] + [ANY],
        out_specs=[_rows(tm, D_MODEL), vec, vec, vec],
        out_shape=[sd((s, D_MODEL), F32), sd((1, D_MODEL), F32), sd((1, D_MODEL), F32), sd((1, D_MODEL), F32)],
        scratch_shapes=[pltpu.VMEM((IN_W, D_MODEL), BF), pltpu.SemaphoreType.DMA((NSEG,))],
        compiler_params=_params(dimension_semantics=("arbitrary",)),
    )(x, dout, norm_w, scale, *dsegs, w_t)


def _gw_seg(h, dseg, name, tm=1024):
    s, w = dseg.shape
    tn = min(w, 1024)
    tm = min(tm, s)
    nm = s // tm

    def body(h_ref, d_ref, o_ref, acc):
        m = pl.program_id(1)

        @pl.when(m == 0)
        def _():
            acc[...] = jnp.zeros_like(acc)

        acc[...] += _dot_tn(d_ref[...], h_ref[...])

        @pl.when(m == nm - 1)
        def _():
            o_ref[...] = _bf(acc[...])

    return pl.pallas_call(
        body, name=name, grid=(w // tn, nm),
        in_specs=[pl.BlockSpec((tm, D_MODEL), lambda n, m: (m, 0)), pl.BlockSpec((tm, tn), lambda n, m: (m, n))],
        out_specs=pl.BlockSpec((tn, D_MODEL), lambda n, m: (n, 0)),
        out_shape=jax.ShapeDtypeStruct((w, D_MODEL), BF),
        scratch_shapes=[pltpu.VMEM((tn, D_MODEL), F32)],
        compiler_params=_params(dimension_semantics=("arbitrary", "arbitrary")),
    )(h, dseg)


def _gw_in(h, dsegs):
    return [_gw_seg(h, d, "gw_in_%d" % j) for j, d in enumerate(dsegs)]


def _local_step(x, tgt, shift, scale, gate, w_t, rows_fn, norm_w, qnw, knw, rel_bias, sinks,
                conv_w, conv_b, dt_bias, a_log, d_skip, ssm_nw, after_mid=None, after_gw=None):
    oh_t = _bucket_onehot_t()
    bias = _masked_bias(_bias_dense(rel_bias.T, oh_t).reshape(ATTN_HEADS, BLOCK, 2 * BLOCK))
    *segs, h = _inproj(x, norm_w, scale, shift, w_t)
    q, kv, zam, xbc, dtr, gab = segs
    consts = _attn_consts(qnw, knw)
    o_att, lse = _attn_fwd(q, kv, bias, sinks, consts)
    e_mat, e3t = _membership(SSM_W, SSM_P, SSM_HEADS)
    dsk_x = jnp.repeat(d_skip, SSM_P, axis=1)
    ypre, hprev, conv = _ssd_fwd(xbc, dtr, conv_w, conv_b, dt_bias, a_log, dsk_x, e3t)
    (dout, d_o, dzam, dyp, dgab, yag, dy_a, yn, dy_b, merged, dob, g_ssm_nw, dgate, loss) = _mid(
        x, tgt, o_att, zam, ypre, gab, gate, ssm_nw, rows_fn(ypre))
    g_wap = _gw_seg(dy_a, yag, "gw_attn_proj")
    g_wsp = _gw_seg(dy_b, yn, "gw_ssm_proj")
    g_wout = _gw_seg(dob, merged, "gw_out")
    zero = after_mid(g_wap, g_wsp, g_wout) if after_mid is not None else 0.0
    dq, dkv, dss, g_qnw, g_knw, g_sinks = _attn_bwd(q, kv, bias, sinks + zero, consts, o_att, lse, d_o)
    g_rel = _bias_grad(dss.reshape(ATTN_HEADS, BLOCK * 2 * BLOCK), oh_t).T
    dxbc, ddt, g_cw, g_cb, g_dtb, g_alog, g_dsk = _ssd_bwd(
        xbc, conv, dtr, conv_w, dt_bias, a_log, dsk_x, e_mat, e3t, hprev, dyp)
    dsegs = (dq, dkv, dzam, dxbc, ddt, dgab)
    g_ws = _gw_in(h, dsegs)
    zero = after_gw(g_ws) if after_gw is not None else 0.0
    gx, dshift, dscale, g_nw = _dh(x, dout, norm_w + zero, scale, dsegs, w_t)
    return dict(loss=loss, grad_x=gx, dmod=jnp.concatenate([dshift, dscale, dgate], axis=1), g_ws=g_ws,
                g_wap=g_wap, g_wsp=g_wsp, g_wout=g_wout, g_norm_w=g_nw, g_qnw=g_qnw, g_knw=g_knw, g_rel=g_rel,
                g_sinks=g_sinks, g_conv_w=g_cw, g_conv_b=g_cb, g_dt_bias=g_dtb, g_a_log=g_alog, g_d_skip=g_dsk,
                g_ssm_nw=g_ssm_nw)


def _me():
    return lax.axis_index("x"), lax.axis_index("y"), lax.axis_index("c")


def _flip(v, bit):
    return 1 - v if bit else v


def _ag_direct(v, name):
    def body(v_ref, out_ref, send_sems, recv_sems, local_sem):
        x, y, c = _me()
        me = 4 * x + 2 * y + c
        mine = pltpu.make_async_copy(v_ref, out_ref.at[me], local_sem)
        mine.start()
        peers = [(_flip(x, k >> 2 & 1), _flip(y, k >> 1 & 1), _flip(c, k & 1)) for k in range(1, N_DEV)]
        sends = [pltpu.make_async_remote_copy(
            src_ref=v_ref, dst_ref=out_ref.at[me], send_sem=send_sems.at[j], recv_sem=recv_sems.at[j],
            device_id=p, device_id_type=MESH) for j, p in enumerate(peers)]
        for cp in sends:
            cp.start()
        for j, (px, py, pc) in enumerate(peers):
            pltpu.make_async_remote_copy(
                src_ref=v_ref, dst_ref=out_ref.at[4 * px + 2 * py + pc], send_sem=send_sems.at[j],
                recv_sem=recv_sems.at[j], device_id=(px, py, pc), device_id_type=MESH).wait_recv()
        for cp in sends:
            cp.wait_send()
        mine.wait()

    vm = pl.BlockSpec(memory_space=pltpu.VMEM)
    return pl.pallas_call(
        body, name=name, out_shape=jax.ShapeDtypeStruct((N_DEV,) + v.shape, v.dtype),
        in_specs=[vm], out_specs=vm,
        scratch_shapes=[pltpu.SemaphoreType.DMA((N_DEV - 1,)), pltpu.SemaphoreType.DMA((N_DEV - 1,)),
                        pltpu.SemaphoreType.DMA],
        compiler_params=_params(),
    )(v)


def _ag_two_level(v, name):
    def body(v_ref, out_ref, token, send_sems, recv_sems, local_sem):
        token[...] = jnp.zeros_like(token)
        x, y, c = _me()
        me, sibling = (x, y, c), (x, y, 1 - c)
        chips = [(1 - x, y), (x, 1 - y), (1 - x, 1 - y)]

        def slot(px, py, pc):
            return out_ref.at[4 * px + 2 * py + pc]

        def copy(k, block, to, src=None):
            return pltpu.make_async_remote_copy(
                src_ref=slot(*block) if src is None else src, dst_ref=slot(*block),
                send_sem=send_sems.at[k], recv_sem=recv_sems.at[k], device_id=to, device_id_type=MESH)

        mine = pltpu.make_async_copy(v_ref, slot(*me), local_sem)
        mine.start()
        first = [copy(0, me, sibling, src=v_ref)]
        first += [copy(1 + j, me, (*chip, c), src=v_ref) for j, chip in enumerate(chips)]
        for cp in first:
            cp.start()
        passed = [copy(4 + j, (*chip, c), sibling) for j, chip in enumerate(chips)]
        for j, chip in enumerate(chips):
            copy(1 + j, (*chip, c), me).wait_recv()
            passed[j].start()
        copy(0, sibling, me).wait_recv()
        for j, chip in enumerate(chips):
            copy(4 + j, (*chip, 1 - c), me).wait_recv()
        for cp in first + passed:
            cp.wait_send()
        mine.wait()

    out, token = pl.pallas_call(
        body, name=name,
        out_shape=(jax.ShapeDtypeStruct((N_DEV,) + v.shape, v.dtype), jax.ShapeDtypeStruct((8, 128), v.dtype)),
        in_specs=[ANY], out_specs=(ANY, pl.BlockSpec(memory_space=pltpu.VMEM)),
        scratch_shapes=[pltpu.SemaphoreType.DMA((7,)), pltpu.SemaphoreType.DMA((7,)), pltpu.SemaphoreType.DMA],
        compiler_params=_params(),
    )(v)
    return out, token[0:1, 0:1]


HBM = pl.BlockSpec(memory_space=pltpu.HBM)
SEM = pl.BlockSpec(memory_space=pltpu.SEMAPHORE)
EFFECT = pltpu.SideEffectType.DATAFLOW_SIDE_EFFECTING


def _peers(x, y, c):
    return [(_flip(x, k >> 2 & 1), _flip(y, k >> 1 & 1), _flip(c, k & 1)) for k in range(1, N_DEV)]


def _exchange_start(src, land, gather, name):
    def body(src_ref, land_ref, send_sems, recv_sems, src_thru, land_thru, token):
        x, y, c = _me()
        me = 4 * x + 2 * y + c
        for j, (px, py, pc) in enumerate(_peers(x, y, c)):
            pltpu.make_async_remote_copy(
                src_ref=src_ref if gather else src_ref.at[4 * px + 2 * py + pc], dst_ref=land_ref.at[me],
                send_sem=send_sems.at[j], recv_sem=recv_sems.at[j], device_id=(px, py, pc), device_id_type=MESH).start()
        token[...] = jnp.zeros_like(token)

    sems = pltpu.SemaphoreType.DMA((N_DEV - 1,))
    out = pl.pallas_call(
        body, name=name,
        out_shape=(sems, sems, pltpu.HBM(src.shape, src.dtype), pltpu.HBM(land.shape, land.dtype),
                   jax.ShapeDtypeStruct((8, 128), F32)),
        in_specs=(HBM, HBM), out_specs=(SEM, SEM, HBM, HBM, pl.BlockSpec(memory_space=pltpu.VMEM)),
        input_output_aliases={0: 2, 1: 3},
        compiler_params=pltpu.CompilerParams(has_side_effects=EFFECT),
    )(pltpu.with_memory_space_constraint(src, pltpu.HBM), pltpu.with_memory_space_constraint(land, pltpu.HBM))
    return out[:4], out[4][0, 0]


def _exchange_wait(started, after, gather, name):
    send_sems, recv_sems, src_thru, land_thru = started

    def body(src_ref, land_ref, send_sems, recv_sems, after_ref, src_dead, got_ref):
        x, y, c = _me()
        for j, (px, py, pc) in enumerate(_peers(x, y, c)):
            pid = 4 * px + 2 * py + pc
            cp = pltpu.make_async_remote_copy(
                src_ref=src_ref if gather else src_ref.at[pid], dst_ref=land_ref.at[pid],
                send_sem=send_sems.at[j], recv_sem=recv_sems.at[j], device_id=(px, py, pc), device_id_type=MESH)
            cp.wait_send()
            cp.wait_recv()

    return pl.pallas_call(
        body, name=name,
        out_shape=(pltpu.HBM(src_thru.shape, src_thru.dtype), pltpu.HBM(land_thru.shape, land_thru.dtype)),
        in_specs=(HBM, HBM, SEM, SEM, ANY), out_specs=(HBM, HBM), input_output_aliases={0: 0, 1: 1},
        compiler_params=pltpu.CompilerParams(has_side_effects=EFFECT),
    )(src_thru, land_thru, send_sems, recv_sems, after)[1]


def _silu(a):
    return a * _sig(a)


def _mod_piece(c_all, w_ada, b_piece):
    def body(c_ref, w_ref, b_ref, o_ref):
        o_ref[...] = _dot(_bf(_silu(c_ref[...])), _bf(w_ref[...])) + b_ref[...]

    return pl.pallas_call(
        body, name="mod_piece", out_shape=jax.ShapeDtypeStruct((c_all.shape[0], w_ada.shape[1]), F32),
        compiler_params=_params(),
    )(c_all, w_ada, b_piece)


def _gw_ada(c_all, dmod_piece):
    def body(c_ref, d_ref, o_ref):
        o_ref[...] = _dot_tn(_bf(_silu(c_ref[...])), _bf(d_ref[...]))

    return pl.pallas_call(
        body, name="gw_ada", out_shape=jax.ShapeDtypeStruct((c_all.shape[1], dmod_piece.shape[1]), F32),
        compiler_params=_params(),
    )(c_all, dmod_piece)


def _adam(parts, w, m, v, name):
    k, r, n = parts.shape
    if r <= 256 or r % 256 == 0:
        tr, tn = min(r, 256), n
    else:
        tr, tn = r, 256
    assert r % tr == 0 and n % tn == 0

    def body(p_ref, w_ref, m_ref, v_ref, g_ref, d_ref, nm_ref, nv_ref):
        g = p_ref[0].astype(F32)
        for j in range(1, k):
            g = g + p_ref[j].astype(F32)
        g_ref[...] = g
        d_ref[...], nm_ref[...], nv_ref[...] = _adam_math(g, w_ref[...], m_ref[...], v_ref[...])

    blk = pl.BlockSpec((tr, tn), lambda i, j: (i, j))
    return pl.pallas_call(
        body, name=name, grid=(r // tr, n // tn),
        in_specs=[pl.BlockSpec((k, tr, tn), lambda i, j: (0, i, j)), blk, blk, blk],
        out_specs=[blk, blk, blk, blk],
        out_shape=[jax.ShapeDtypeStruct((r, n), F32)] * 4,
        compiler_params=_params(dimension_semantics=("arbitrary", "arbitrary")),
    )(parts, w, m, v)


def _adam_math(g, w, m, v):
    m_new = ADAM_B1 * m + (1.0 - ADAM_B1) * g
    v_new = ADAM_B2 * v + (1.0 - ADAM_B2) * jnp.square(g)
    m_hat = m_new / (1.0 - ADAM_B1 ** ADAM_STEP)
    v_hat = v_new / (1.0 - ADAM_B2 ** ADAM_STEP)
    return -ADAM_LR * (m_hat / (jnp.sqrt(v_hat) + ADAM_EPS) + ADAM_WD * w), m_new, v_new


_SMALL = (("b_ada", 3 * D_MODEL), ("norm_w", D_MODEL), ("q_norm_w", HEAD_DIM), ("k_norm_w", HEAD_DIM),
          ("rel_bias", REL_BUCKETS * ATTN_HEADS), ("sinks", ATTN_HEADS), ("conv_b", XBC_W), ("dt_bias", SSM_HEADS),
          ("a_log", SSM_HEADS), ("d_skip", SSM_HEADS), ("ssm_norm_w", SSM_W))
_SLOT = tuple(-(-n // 128) * 128 for _, n in _SMALL)
_SLOT_OFF = tuple(int(o) for o in np.cumsum((0,) + _SLOT))
_LOSS_OFF = _SLOT_OFF[-1]
_CW_OFF = _LOSS_OFF + 128
_PACK_N = _CW_OFF + CONV_K * XBC_W


def _pack_partials(small, loss, g_conv_w):
    parts = []
    for (name, n), slot in zip(_SMALL, _SLOT):
        parts.append(small[name].reshape(1, n))
        if slot > n:
            parts.append(jnp.zeros((1, slot - n), F32))
    parts += [loss.reshape(1, 1), jnp.zeros((1, 127), F32), g_conv_w.reshape(1, CONV_K * XBC_W)]
    return jnp.concatenate(parts, axis=1)


def _adam_small(pack_all, w, m, v):
    names = [name for name, _ in _SMALL]

    def body(p_ref, *rest):
        ins, outs = rest[:3 * len(names)], rest[3 * len(names):]

        def total(off, n):
            g = p_ref[0, :, off:off + n]
            for d in range(1, N_DEV):
                g = g + p_ref[d, :, off:off + n]
            return g

        for j, (name, n) in enumerate(_SMALL):
            g = total(_SLOT_OFF[j], n)
            delta, m_new, v_new = _adam_math(g, ins[3 * j][...], ins[3 * j + 1][...], ins[3 * j + 2][...])
            outs[4 * j][...] = g
            outs[4 * j + 1][...] = delta
            outs[4 * j + 2][...] = m_new
            outs[4 * j + 3][...] = v_new
        outs[-1][...] = total(_LOSS_OFF, 1)

    flat = []
    for name, n in _SMALL:
        flat += [w[name].reshape(1, n), m[name].reshape(1, n), v[name].reshape(1, n)]
    out_shape = [jax.ShapeDtypeStruct((1, n), F32) for _, n in _SMALL for _ in range(4)] + [jax.ShapeDtypeStruct((1, 1), F32)]
    out = pl.pallas_call(body, name="adam_small", out_shape=out_shape, compiler_params=_params())(pack_all, *flat)
    res = {name: [out[4 * j + t].reshape(w[name].shape) for t in range(4)] for j, name in enumerate(names)}
    return res, out[-1]


WEIGHTS = ("w_ada", "b_ada", "norm_w", "w_in", "q_norm_w", "k_norm_w", "rel_bias", "sinks", "conv_w", "conv_b",
           "dt_bias", "a_log", "d_skip", "ssm_norm_w", "w_attn_proj", "w_ssm_proj", "w_out")


def kernel(x, c, w_ada, b_ada, norm_w, w_in, q_norm_w, k_norm_w, rel_bias, sinks, conv_w, conv_b, dt_bias, a_log, d_skip, ssm_norm_w, w_attn_proj, w_ssm_proj, w_out, loss_target, m_w_ada, m_b_ada, m_norm_w, m_w_in, m_q_norm_w, m_k_norm_w, m_rel_bias, m_sinks, m_conv_w, m_conv_b, m_dt_bias, m_a_log, m_d_skip, m_ssm_norm_w, m_w_attn_proj, m_w_ssm_proj, m_w_out, v_w_ada, v_b_ada, v_norm_w, v_w_in, v_q_norm_w, v_k_norm_w, v_rel_bias, v_sinks, v_conv_w, v_conv_b, v_dt_bias, v_a_log, v_d_skip, v_ssm_norm_w, v_w_attn_proj, v_w_ssm_proj, v_w_out):
    w = dict(w_ada=w_ada, b_ada=b_ada, norm_w=norm_w, w_in=w_in, q_norm_w=q_norm_w, k_norm_w=k_norm_w,
             rel_bias=rel_bias, sinks=sinks, conv_w=conv_w, conv_b=conv_b, dt_bias=dt_bias, a_log=a_log,
             d_skip=d_skip, ssm_norm_w=ssm_norm_w, w_attn_proj=w_attn_proj, w_ssm_proj=w_ssm_proj, w_out=w_out)
    m = dict(w_ada=m_w_ada, b_ada=m_b_ada, norm_w=m_norm_w, w_in=m_w_in, q_norm_w=m_q_norm_w, k_norm_w=m_k_norm_w,
             rel_bias=m_rel_bias, sinks=m_sinks, conv_w=m_conv_w, conv_b=m_conv_b, dt_bias=m_dt_bias, a_log=m_a_log,
             d_skip=m_d_skip, ssm_norm_w=m_ssm_norm_w, w_attn_proj=m_w_attn_proj, w_ssm_proj=m_w_ssm_proj, w_out=m_w_out)
    v = dict(w_ada=v_w_ada, b_ada=v_b_ada, norm_w=v_norm_w, w_in=v_w_in, q_norm_w=v_q_norm_w, k_norm_w=v_k_norm_w,
             rel_bias=v_rel_bias, sinks=v_sinks, conv_w=v_conv_w, conv_b=v_conv_b, dt_bias=v_dt_bias, a_log=v_a_log,
             d_skip=v_d_skip, ssm_norm_w=v_ssm_norm_w, w_attn_proj=v_w_attn_proj, w_ssm_proj=v_w_ssm_proj, w_out=v_w_out)
    me = 4 * lax.axis_index("x") + 2 * lax.axis_index("y") + lax.axis_index("c")
    ada_n = w_ada.shape[2]
    in_n = w_in.shape[2]
    cw_n = conv_w.shape[2]

    first = _ag_direct(jnp.concatenate([c, conv_w[0].reshape(1, CONV_K * cw_n)], axis=1), "ag_c")[:, 0]
    c_all = first[:, :D_MODEL]
    conv_w_full = first[:, D_MODEL:].reshape(N_DEV, CONV_K, cw_n).transpose(1, 0, 2).reshape(CONV_K, XBC_W)
    b_piece = lax.dynamic_slice_in_dim(b_ada, me * ada_n, ada_n, axis=1)
    mod_all = _ag_direct(_mod_piece(c_all, w_ada[0], b_piece), "ag_mod")
    mod = lax.dynamic_index_in_dim(mod_all, me, axis=1, keepdims=False).reshape(1, 3 * D_MODEL)
    shift, scale, gate = mod[:, :D_MODEL], mod[:, D_MODEL:2 * D_MODEL], mod[:, 2 * D_MODEL:]

    w_t, zero = _ag_two_level(w_in[0].T.astype(BF), "ag_w_in")
    w_t = w_t.reshape(N_DEV * in_n, D_MODEL)

    def with_mine(blocks, mine):
        return lax.dynamic_update_index_in_dim(lax.empty(blocks, mine.dtype), mine, me, axis=0)

    rows = jnp.concatenate([w_attn_proj[0], w_ssm_proj[0], w_out[0]], axis=0).astype(BF) + zero
    r_ap, r_sp = w_attn_proj.shape[1], w_ssm_proj.shape[1]
    rows_started, zero = _exchange_start(rows, with_mine((N_DEV,) + rows.shape, rows), True, "ag_rows_start")

    def rows_fn(after):
        return _exchange_wait(rows_started, after, True, "ag_rows_wait")

    started = {}

    def send_blocks(key, g, name):
        started[key], zero = _exchange_start(
            g, with_mine(g.shape, lax.dynamic_index_in_dim(g, me, axis=0, keepdims=False)), False, name)
        return zero

    def after_mid(g_wap, g_wsp, g_wout):
        return send_blocks("rows", jnp.concatenate(
            [g_wap.reshape(N_DEV, r_ap, D_MODEL), g_wsp.reshape(N_DEV, r_sp, D_MODEL),
             g_wout.reshape(N_DEV, r_ap, D_MODEL)], axis=1), "rs_rows_start")

    def after_gw(g_ws):
        return send_blocks("in", jnp.concatenate(g_ws, axis=0).reshape(N_DEV, in_n, D_MODEL), "rs_in_start")

    r = _local_step(x[0], loss_target[0], shift, scale + zero, gate, w_t, rows_fn, norm_w, q_norm_w, k_norm_w,
                    rel_bias, sinks, conv_w_full, conv_b, dt_bias, a_log, d_skip, ssm_norm_w, after_mid, after_gw)

    small = dict(b_ada=r["dmod"], norm_w=r["g_norm_w"], q_norm_w=r["g_qnw"], k_norm_w=r["g_knw"], rel_bias=r["g_rel"],
                 sinks=r["g_sinks"], conv_b=r["g_conv_b"], dt_bias=r["g_dt_bias"], a_log=r["g_a_log"],
                 d_skip=r["g_d_skip"], ssm_norm_w=r["g_ssm_nw"])
    pack_all = _ag_direct(_pack_partials(small, r["loss"], r["g_conv_w"]), "ag_small")
    res, loss = _adam_small(pack_all, w, m, v)
    loss = loss[0, 0]
    cw_parts = pack_all[:, 0, _CW_OFF:].reshape(N_DEV, CONV_K, XBC_W)
    cw_mine = lax.dynamic_slice_in_dim(cw_parts, me * cw_n, cw_n, axis=2)
    res["conv_w"] = [a[None] for a in _adam(cw_mine, conv_w[0], m_conv_w[0], v_conv_w[0], "adam_conv_w")]

    dmod_piece = lax.dynamic_slice_in_dim(pack_all[:, 0, :3 * D_MODEL], me * ada_n, ada_n, axis=1)
    g_ada = _gw_ada(c_all, dmod_piece)
    res["w_ada"] = [a[None] for a in _adam(g_ada[None], w_ada[0], m_w_ada[0], v_w_ada[0], "adam_w_ada")]

    cat = lambda d: jnp.concatenate([d["w_attn_proj"][0], d["w_ssm_proj"][0], d["w_out"][0]], axis=0)
    rows_res = _adam(_exchange_wait(started["rows"], g_ada, False, "rs_rows_wait"), cat(w), cat(m), cat(v), "adam_w_rows")
    res["w_in"] = [a.T[None] for a in _adam(_exchange_wait(started["in"], rows_res[0], False, "rs_in_wait"),
                                            w_in[0].T, m_w_in[0].T, v_w_in[0].T, "adam_w_in")]
    res["w_attn_proj"] = [a[None, :r_ap] for a in rows_res]
    res["w_ssm_proj"] = [a[None, r_ap:r_ap + r_sp] for a in rows_res]
    res["w_out"] = [a[None, r_ap + r_sp:] for a in rows_res]

    outs = [loss, r["grad_x"][None]]
    for j in range(4):
        outs += [res[name][j] for name in WEIGHTS]
    return tuple(outs)
```

```python
import math

import numpy as np
import jax
import jax.numpy as jnp
from jax import lax
from jax.experimental import pallas as pl
from jax.experimental.pallas import tpu as pltpu

F32 = jnp.float32
BF = jnp.bfloat16
HI = lax.Precision.HIGHEST

D_MODEL = 1024
ATTN_HEADS = 16
KV_HEADS = 4
GRP = ATTN_HEADS // KV_HEADS
HEAD_DIM = 64
ATTN_W = ATTN_HEADS * HEAD_DIM
KV_W = KV_HEADS * HEAD_DIM
BLOCK = 128
REL_BUCKETS = 32
REL_MAX_DIST = 128
SSM_W = 2048
SSM_P = 64
SSM_HEADS = 32
SSM_G = 4
SSM_R = 8
SSM_N = 128
CONV_K = 4
XBC_W = SSM_W + 2 * SSM_G * SSM_N
SEG_W = (ATTN_W, 2 * KV_W, ATTN_W + SSM_W, XBC_W, SSM_HEADS, 2 * D_MODEL)
NSEG = len(SEG_W)
SEG_OFF = tuple(int(v) for v in np.cumsum((0,) + SEG_W))
IN_W = SEG_OFF[-1]
GATE_SEGS = (2, 5)
EPS = 1e-6
N_DEV = 8
ADAM_LR, ADAM_B1, ADAM_B2, ADAM_EPS, ADAM_WD, ADAM_STEP = 0.001, 0.9, 0.999, 1e-08, 0.01, 10
VMEM_LIMIT = 60 * 1024 * 1024
MESH = pl.DeviceIdType.MESH
ANY = pl.BlockSpec(memory_space=pl.ANY)


def _dot(a, b, precision=None):
    return jnp.dot(a, b, preferred_element_type=F32, precision=precision)


def _dot_nt(a, b, precision=None):
    return lax.dot_general(a, b, (((1,), (1,)), ((), ())), preferred_element_type=F32, precision=precision)


def _dot_tn(a, b, precision=None):
    return lax.dot_general(a, b, (((0,), (0,)), ((), ())), preferred_element_type=F32, precision=precision)


def _bf(a):
    return a.astype(BF)


def _sig(a):
    return 0.5 * jnp.tanh(0.5 * a) + 0.5


def _params(**kw):
    return pltpu.CompilerParams(vmem_limit_bytes=VMEM_LIMIT, **kw)


def _full(shape):
    nd = len(shape)
    return pl.BlockSpec(shape, lambda i: (0,) * nd)


def _rows(tm, w):
    return pl.BlockSpec((tm, w), lambda i: (i, 0))


def _inproj(x, norm_w, scale, shift, w_t, tm=256):
    s = x.shape[0]

    def body(x_ref, nw_ref, sc_ref, sh_ref, w_hbm, *rest):
        outs, h_ref, w_vm, sem = rest[:NSEG], rest[NSEG], rest[NSEG + 1], rest[NSEG + 2]
        first = pl.program_id(0) == 0
        cps = [pltpu.make_async_copy(w_hbm.at[SEG_OFF[j]:SEG_OFF[j + 1], :], w_vm.at[SEG_OFF[j]:SEG_OFF[j + 1], :], sem.at[j])
               for j in range(NSEG)]

        def tile(waiting):
            xv = x_ref[...]
            r = lax.rsqrt(jnp.mean(xv * xv, axis=-1, keepdims=True) + EPS)
            h = xv * r * (nw_ref[...] * (1.0 + sc_ref[...])) + sh_ref[...]
            hb = _bf(h)
            h_ref[...] = hb
            for j in range(NSEG):
                if waiting:
                    cps[j].wait()
                outs[j][...] = _dot_nt(hb, w_vm[SEG_OFF[j]:SEG_OFF[j + 1], :]).astype(outs[j].dtype)

        @pl.when(first)
        def _():
            for cp in cps:
                cp.start()
            tile(True)

        @pl.when(jnp.logical_not(first))
        def _():
            tile(False)

    vec = _full((1, D_MODEL))
    return pl.pallas_call(
        body, name="inproj", grid=(s // tm,),
        in_specs=[_rows(tm, D_MODEL), vec, vec, vec, ANY],
        out_specs=[_rows(tm, w) for w in SEG_W] + [_rows(tm, D_MODEL)],
        out_shape=[jax.ShapeDtypeStruct((s, w), BF if j in GATE_SEGS else F32) for j, w in enumerate(SEG_W)]
                  + [jax.ShapeDtypeStruct((s, D_MODEL), BF)],
        scratch_shapes=[pltpu.VMEM((IN_W, D_MODEL), BF), pltpu.SemaphoreType.DMA((NSEG,))],
        compiler_params=_params(dimension_semantics=("arbitrary",)),
    )(x, norm_w, scale, shift, w_t)


def _bucket_onehot_t():
    qi = jnp.arange(BLOCK)[:, None]
    kj = jnp.arange(2 * BLOCK)[None, :]
    dist = qi + BLOCK - kj
    n = jnp.maximum(dist, 0)
    max_exact = REL_BUCKETS // 2
    nf = jnp.maximum(n, 1).astype(F32)
    large = max_exact + (jnp.log(nf / max_exact) / math.log(REL_MAX_DIST / max_exact)
                         * (REL_BUCKETS - max_exact)).astype(jnp.int32)
    large = jnp.minimum(large, REL_BUCKETS - 1)
    bucket = jnp.where(n < max_exact, n, large).reshape(1, BLOCK * 2 * BLOCK)
    return (bucket == jnp.arange(REL_BUCKETS)[:, None]).astype(F32)


def _bias_dense(rel_bias_t, oh_t):
    def body(rb_ref, oh_ref, o_ref):
        o_ref[...] = _dot(rb_ref[...], oh_ref[...], HI)

    return pl.pallas_call(
        body, name="bias_dense", out_shape=jax.ShapeDtypeStruct((ATTN_HEADS, BLOCK * 2 * BLOCK), F32),
        compiler_params=_params(),
    )(rel_bias_t, oh_t)


def _bias_grad(ds_sum, oh_t):
    def body(ds_ref, oh_ref, o_ref):
        o_ref[...] = _dot_nt(ds_ref[...], oh_ref[...], HI)

    return pl.pallas_call(
        body, name="bias_grad", out_shape=jax.ShapeDtypeStruct((ATTN_HEADS, REL_BUCKETS), F32),
        compiler_params=_params(),
    )(ds_sum, oh_t)


def _group_sum(a, e):
    hi = _bf(a)
    return _dot(hi, e) + _dot(_bf(a - hi.astype(F32)), e)


def _group_bcast(a, e3t):
    hi = _bf(a)
    r1 = a - hi.astype(F32)
    mid = _bf(r1)
    return _dot(jnp.concatenate([hi, mid, _bf(r1 - mid.astype(F32))], axis=1), e3t)


def _membership(width, group, ngroups):
    e = (jnp.arange(width)[:, None] // group == jnp.arange(ngroups)[None, :]).astype(BF)
    return e, jnp.tile(e.T, (3, 1))


def _fold(width, group):
    return (jnp.arange(width)[:, None] % group == jnp.arange(group)[None, :]).astype(BF)


def _heads_norm(t, w_x, e, e3t):
    r = lax.rsqrt(_dot(_bf(t * t), e) * (1.0 / HEAD_DIM) + EPS)
    r_x = _group_bcast(r, e3t)
    return t * r_x * w_x, r_x


def _heads_norm_bwd(t, r_x, w_x, d, e, e3t):
    wd = d * w_x
    corr = _group_bcast(_dot(_bf(t * wd), e) * (1.0 / HEAD_DIM), e3t)
    return r_x * wd - t * (r_x * r_x * r_x) * corr, jnp.sum(d * t * r_x, axis=0, keepdims=True)


def _stack_heads(a, hk):
    return jnp.concatenate([a[:, (hk * GRP + g) * HEAD_DIM:(hk * GRP + g + 1) * HEAD_DIM] for g in range(GRP)], axis=0)


def _stack_cols(a, hk):
    return jnp.concatenate([a[:, hk * GRP + g:hk * GRP + g + 1] for g in range(GRP)], axis=0)


def _masked_bias(bias):
    qi = jnp.arange(BLOCK)[:, None]
    kj = jnp.arange(2 * BLOCK)[None, :]
    cur_ok = jnp.logical_and(kj >= BLOCK, kj - BLOCK <= qi)
    both_ok = jnp.logical_or(jnp.logical_and(kj < BLOCK, kj > qi), cur_ok)
    return jnp.stack([jnp.where(cur_ok, bias, -1e30), jnp.where(both_ok, bias, -1e30)])


def _attn_consts(qnw, knw):
    eq, eq3t = _membership(ATTN_W, HEAD_DIM, ATTN_HEADS)
    ek, ek3t = _membership(KV_W, HEAD_DIM, ATTN_HEADS)
    return (jnp.tile(qnw, (1, ATTN_HEADS)), jnp.tile(knw, (1, KV_HEADS)), eq, eq3t, ek, ek3t)


def _attn_fwd(q, kv, bias, sinks, consts):
    s = q.shape[0]
    nb = s // BLOCK
    gq = GRP * BLOCK
    bias_t = bias.reshape(2, KV_HEADS, GRP, BLOCK, 2 * BLOCK).transpose(0, 1, 4, 2, 3).reshape(2, KV_HEADS, 2 * BLOCK, gq)
    sink_rows = jnp.repeat(sinks.reshape(KV_HEADS, GRP), BLOCK, axis=1).reshape(KV_HEADS, 1, gq)
    eye = jnp.eye(BLOCK, dtype=BF)

    def body(q_ref, kp_ref, kc_ref, vp_ref, vc_ref, b_ref, bt_ref, sk_ref, skr_ref, eye_ref,
             qw_ref, kw_ref, eq_ref, eq3_ref, ek_ref, ek3_ref, o_ref, lse_ref):
        qn = _bf(_heads_norm(q_ref[...], qw_ref[...], eq_ref[...], eq3_ref[...])[0] * (HEAD_DIM ** -0.5))
        kn = _bf(_heads_norm(jnp.concatenate([kp_ref[...], kc_ref[...]], axis=0), kw_ref[...], ek_ref[...], ek3_ref[...])[0])
        vv = _bf(jnp.concatenate([vp_ref[...], vc_ref[...]], axis=0))
        ones = jnp.ones((2 * BLOCK, HEAD_DIM), BF)
        lses = []
        kss = [slice(hk * HEAD_DIM, (hk + 1) * HEAD_DIM) for hk in range(KV_HEADS)]
        qgs = [_stack_heads(qn, hk) for hk in range(KV_HEADS)]
        sc_ts = [_dot_nt(kn[:, kss[hk]], qgs[hk]) + bt_ref[0, hk] for hk in range(KV_HEADS)]
        m_rows = [jnp.maximum(jnp.max(sc_ts[hk], axis=0, keepdims=True), skr_ref[hk]) for hk in range(KV_HEADS)]
        m8s = [_bf(jnp.broadcast_to(m + jnp.abs(m) * (2.0 ** -7), (8, gq))) for m in m_rows]
        ms = [jnp.concatenate([_dot_nt(eye_ref[...], m8[:, g * BLOCK:(g + 1) * BLOCK])[:, 0:1] for g in range(GRP)], axis=0)
              for m8 in m8s]
        scs = [_dot_nt(qgs[hk], kn[:, kss[hk]]) + b_ref[0, hk * GRP:(hk + 1) * GRP].reshape(gq, 2 * BLOCK)
               for hk in range(KV_HEADS)]
        ps = [_bf(jnp.exp(scs[hk] - ms[hk])) for hk in range(KV_HEADS)]
        pvs = [_dot(ps[hk], jnp.concatenate([vv[:, kss[hk]], ones], axis=1)) for hk in range(KV_HEADS)]
        for hk in range(KV_HEADS):
            m, pv = ms[hk], pvs[hk]
            sink = jnp.concatenate([jnp.full((BLOCK, 1), sk_ref[0, hk * GRP + g], F32) for g in range(GRP)], axis=0)
            den = pv[:, HEAD_DIM:HEAD_DIM + 1] + jnp.exp(sink - m)
            out = pv[:, :HEAD_DIM] * (1.0 / den)
            lse = m + jnp.log(den)
            for g in range(GRP):
                h = hk * GRP + g
                o_ref[:, h * HEAD_DIM:(h + 1) * HEAD_DIM] = out[g * BLOCK:(g + 1) * BLOCK]
                lses.append(lse[g * BLOCK:(g + 1) * BLOCK])
        lse_ref[...] = jnp.concatenate(lses, axis=1)

    cur = lambda w, col=0: pl.BlockSpec((BLOCK, w), lambda i: (i, col))
    prev = lambda w, col=0: pl.BlockSpec((BLOCK, w), lambda i: (jnp.maximum(i - 1, 0), col))
    whole = lambda a: pl.BlockSpec(a.shape, lambda i: (0,) * a.ndim)
    first_or_not = lambda a: pl.BlockSpec((1,) + a.shape[1:], lambda i: (jnp.minimum(i, 1),) + (0,) * (a.ndim - 1))
    return pl.pallas_call(
        body, name="attn_fwd", grid=(nb,),
        in_specs=[cur(ATTN_W), prev(KV_W, 0), cur(KV_W, 0), prev(KV_W, 1), cur(KV_W, 1),
                  first_or_not(bias), first_or_not(bias_t),
                  pl.BlockSpec(memory_space=pltpu.SMEM), whole(sink_rows), whole(eye)] + [_full(c.shape) for c in consts],
        out_specs=[cur(ATTN_W), cur(ATTN_HEADS)],
        out_shape=[jax.ShapeDtypeStruct((s, ATTN_W), F32), jax.ShapeDtypeStruct((s, ATTN_HEADS), F32)],
        compiler_params=_params(dimension_semantics=("arbitrary",)),
    )(q, kv, kv, kv, kv, bias, bias_t, sinks, sink_rows, eye, *consts)


def _conv_taps(xbc, tail):
    ext = jnp.concatenate([tail, xbc], axis=0)
    return [pltpu.roll(ext, CONV_K - 1 - j, axis=0)[8:8 + BLOCK] if j < CONV_K - 1 else xbc for j in range(CONV_K)]


def _softplus(u):
    return jnp.maximum(u, 0.0) + jnp.log(1.0 + jnp.exp(-jnp.abs(u)))


def _tril():
    r = lax.broadcasted_iota(jnp.int32, (BLOCK, BLOCK), 0)
    c = lax.broadcasted_iota(jnp.int32, (BLOCK, BLOCK), 1)
    return r >= c


def _triu():
    r = lax.broadcasted_iota(jnp.int32, (BLOCK, BLOCK), 0)
    c = lax.broadcasted_iota(jnp.int32, (BLOCK, BLOCK), 1)
    return r <= c


def _exact_left(m01, a):
    hi = _bf(a)
    r1 = a - hi.astype(F32)
    mid = _bf(r1)
    return _dot(m01, hi) + _dot(m01, mid) + _dot(m01, _bf(r1 - mid.astype(F32)))


def _ssd_common(conv, dtr, dtb_ref, alog_ref, e3_ref):
    sg = _sig(conv)
    xact = conv * sg
    u = dtr + dtb_ref[...]
    dt = _softplus(u)
    a = -jnp.exp(alog_ref[...])
    trilb = _tril()
    acum = _exact_left(trilb.astype(BF), dt * a) * math.log2(math.e)
    both = _group_bcast(jnp.concatenate([dt, acum], axis=0), e3_ref[...])
    dt_x, acum_x = both[:BLOCK], both[BLOCK:]
    return sg, xact, u, dt, a, trilb, acum, dt_x, acum_x


SSD_CH = 2


def _ssd_fwd(xbc, dt_raw, conv_w, conv_b, dt_bias, a_log, dsk_x, e3t):
    s = xbc.shape[0]
    nc = s // BLOCK
    ch = SSD_CH if nc % SSD_CH == 0 else 1
    rows = ch * BLOCK

    def body(x_ref, tail_ref, dtr_ref, cw_ref, cb_ref, dtb_ref, alog_ref, dsk_ref, e3_ref,
             y_ref, hp_ref, conv_ref, hst, yd_s, yoff_s):
        i = pl.program_id(0)

        @pl.when(i == 0)
        def _():
            hst[...] = jnp.zeros_like(hst)

        for j in range(ch):
            rs = slice(j * BLOCK, (j + 1) * BLOCK)
            tail = jnp.where(i > 0, tail_ref[...], 0.0) if j == 0 else x_ref[j * BLOCK - 8:j * BLOCK, :]
            taps = _conv_taps(x_ref[rs, :], tail)
            conv = cb_ref[...] + sum(taps[t] * cw_ref[t:t + 1, :] for t in range(CONV_K))
            conv_ref[rs, :] = conv
            _, xact, _, _, _, trilb, acum, dt_x, acum_x = _ssd_common(conv, dtr_ref[rs, :], dtb_ref, alog_ref, e3_ref)
            xs = xact[:, :SSM_W]
            acum_t = acum.T
            ea_x = jnp.exp2(acum_x)
            last_x = acum_x[BLOCK - 1:BLOCK, :]
            xdt = xs * dt_x
            xw = xdt * jnp.exp2(last_x - acum_x)
            cd_x = jnp.exp2(last_x)
            hprev = hst[...]
            hp_ref[j] = hprev
            sls = [slice(g * SSM_R * SSM_P, (g + 1) * SSM_R * SSM_P) for g in range(SSM_G)]
            bgs = [_bf(xact[:, SSM_W + g * SSM_N:SSM_W + (g + 1) * SSM_N]) for g in range(SSM_G)]
            cgs = [_bf(xact[:, SSM_W + SSM_G * SSM_N + g * SSM_N:SSM_W + SSM_G * SSM_N + (g + 1) * SSM_N])
                   for g in range(SSM_G)]
            xdt_b, xw_b, hprev_b = _bf(xdt), _bf(xw), _bf(hprev)
            low_half = lax.broadcasted_iota(jnp.int32, (BLOCK, 2 * SSM_P), 1) < SSM_P
            cbs = [_dot_nt(cgs[g], bgs[g]) for g in range(SSM_G)]
            for g in range(SSM_G):
                sl = sls[g]
                yoff_s[:, sl] = _dot(cgs[g], hprev_b[:, sl]) * ea_x[:, sl]
                hst[:, sl] = hprev[:, sl] * cd_x[:, sl] + _dot_tn(bgs[g], xw_b[:, sl])
            for g in range(SSM_G):
                hss = [slice((g * SSM_R + r) * SSM_P, (g * SSM_R + r + 1) * SSM_P) for r in range(SSM_R)]
                mms = [_bf(cbs[g] * jnp.exp2(jnp.where(trilb, acum[:, g * SSM_R + r:g * SSM_R + r + 1]
                                                      - acum_t[g * SSM_R + r:g * SSM_R + r + 1, :], -1e30)))
                       for r in range(SSM_R)]
                for r in range(0, SSM_R, 2):
                    pair = slice(hss[r].start, hss[r + 1].stop)
                    xp = xdt_b[:, pair]
                    rhs = jnp.concatenate([jnp.where(low_half, xp, 0), jnp.where(low_half, 0, xp)], axis=0)
                    yd_s[:, pair] = _dot(jnp.concatenate([mms[r], mms[r + 1]], axis=1), rhs)
            y_ref[rs, :] = yd_s[...] + yoff_s[...] + dsk_ref[...] * xs

    blk = lambda w: pl.BlockSpec((rows, w), lambda i: (i, 0))
    return pl.pallas_call(
        body, name="ssd_fwd", grid=(nc // ch,),
        in_specs=[blk(XBC_W), pl.BlockSpec((8, XBC_W), lambda i: (jnp.maximum(i * (rows // 8) - 1, 0), 0)),
                  blk(SSM_HEADS), _full((CONV_K, XBC_W)), _full((1, XBC_W)), _full((1, SSM_HEADS)),
                  _full((1, SSM_HEADS)), _full((1, SSM_W)), _full((3 * SSM_HEADS, SSM_W))],
        out_specs=[blk(SSM_W), pl.BlockSpec((ch, SSM_N, SSM_W), lambda i: (i, 0, 0)), blk(XBC_W)],
        out_shape=[jax.ShapeDtypeStruct((s, SSM_W), F32), jax.ShapeDtypeStruct((nc, SSM_N, SSM_W), F32),
                   jax.ShapeDtypeStruct((s, XBC_W), F32)],
        scratch_shapes=[pltpu.VMEM((SSM_N, SSM_W), F32), pltpu.VMEM((BLOCK, SSM_W), F32), pltpu.VMEM((BLOCK, SSM_W), F32)],
        compiler_params=_params(dimension_semantics=("arbitrary",)),
    )(xbc, xbc, dt_raw, conv_w, conv_b, dt_bias, a_log, dsk_x, e3t)


def _dsilu(z, sg, silu):
    return sg * (1.0 + (z - silu))


def _mid(x, tgt, o_att, zam, ypre, gab, gate, ssm_nw, rows_all, tm=256):
    s = x.shape[0]
    gw = SSM_W // SSM_G

    r_ap, r_sp = ATTN_W // N_DEV, SSM_W // N_DEV

    def body(x_ref, t_ref, o_ref, zam_ref, yp_ref, gab_ref, gate_ref, nw_ref, rows_h,
             dout_ref, do_ref, dzam_ref, dyp_ref, dgab_ref,
             yag_ref, dya_ref, yn_ref, dyb_ref, mg_ref, dob_ref, gnw_ref, dgate_ref, loss_ref,
             wap_v, wsp_v, wout_v, sem):
        i = pl.program_id(0)

        @pl.when(i == 0)
        def _():
            cps = []
            for d in range(N_DEV):
                for j, (dst, r0, rn) in enumerate(((wap_v, 0, r_ap), (wsp_v, r_ap, r_sp), (wout_v, r_ap + r_sp, r_ap))):
                    cps.append(pltpu.make_async_copy(rows_h.at[d, r0:r0 + rn, :], dst.at[d * rn:(d + 1) * rn, :], sem.at[j]))
            for cp in cps:
                cp.start()
            gnw_ref[...] = jnp.zeros_like(gnw_ref)
            dgate_ref[...] = jnp.zeros_like(dgate_ref)
            loss_ref[...] = jnp.zeros_like(loss_ref)
            for cp in cps:
                cp.wait()

        gate = gate_ref[...]
        nw = nw_ref[...]
        o_att = o_ref[...]
        z_a = zam_ref[:, :ATTN_W].astype(F32)
        s_a = _sig(z_a)
        silu_a = z_a * s_a
        yag = _bf(o_att * silu_a)
        yag_ref[...] = yag
        ypre = yp_ref[...]
        z_m = zam_ref[:, ATTN_W:].astype(F32)
        s_m = _sig(z_m)
        silu_m = z_m * s_m
        yg = ypre * silu_m
        rinv = jnp.concatenate(
            [jnp.broadcast_to(lax.rsqrt(jnp.mean(yg[:, g * gw:(g + 1) * gw] ** 2, axis=-1, keepdims=True) + EPS), (tm, gw))
             for g in range(SSM_G)], axis=1)
        ynr = yg * rinv
        yn = _bf(ynr * nw)
        yn_ref[...] = yn
        y_a = _dot(yag, wap_v[...])
        y_b = _dot(yn, wsp_v[...])
        g_a = _sig(gab_ref[:, :D_MODEL].astype(F32))
        g_b = _sig(gab_ref[:, D_MODEL:].astype(F32))
        merged = _bf(g_a * y_a + g_b * y_b)
        mg_ref[...] = merged
        o = _dot(merged, wout_v[...])
        diff = x_ref[...] + gate * o - t_ref[...]
        loss_ref[...] += (0.5 / D_MODEL) * jnp.sum(diff * diff, axis=(0, 1), keepdims=True)
        dout = diff * (1.0 / D_MODEL)
        dout_ref[...] = dout
        dgate_ref[...] += jnp.sum(dout * o, axis=0, keepdims=True)
        d_o = _bf(dout * gate)
        dob_ref[...] = d_o
        dmerged = _dot_nt(d_o, wout_v[...])
        dy_af = dmerged * g_a
        dy_bf = dmerged * g_b
        dy_a = _bf(dy_af)
        dy_b = _bf(dy_bf)
        dya_ref[...] = dy_a
        dyb_ref[...] = dy_b
        dyag = _dot_nt(dy_a, wap_v[...])
        dyn = _dot_nt(dy_b, wsp_v[...])
        dgab_ref[:, :D_MODEL] = _bf(dy_af * y_a * (1.0 - g_a))
        dgab_ref[:, D_MODEL:] = _bf(dy_bf * y_b * (1.0 - g_b))
        do_ref[...] = dyag * silu_a
        dzam_ref[:, :ATTN_W] = _bf(dyag * o_att * _dsilu(z_a, s_a, silu_a))
        gnw_ref[...] += jnp.sum(dyn * ynr, axis=0, keepdims=True)
        dynw = dyn * nw
        corr = jnp.concatenate(
            [jnp.broadcast_to(jnp.mean((dynw * ynr)[:, g * gw:(g + 1) * gw], axis=-1, keepdims=True), (tm, gw))
             for g in range(SSM_G)], axis=1)
        dyg = rinv * (dynw - ynr * corr)
        dyp_ref[...] = dyg * silu_m
        dzam_ref[:, ATTN_W:] = _bf(dyg * ypre * _dsilu(z_m, s_m, silu_m))

    r1, r2, r3 = _rows(tm, D_MODEL), _rows(tm, SSM_W), _rows(tm, ATTN_W + SSM_W)
    sd = jax.ShapeDtypeStruct
    return pl.pallas_call(
        body, name="mid", grid=(s // tm,),
        in_specs=[r1, r1, r1, r3, r2, r2, _full((1, D_MODEL)), _full((1, SSM_W)), ANY],
        out_specs=[r1, r1, r3, r2, r2, r1, r1, r2, r1, r1, r1,
                   _full((1, SSM_W)), _full((1, D_MODEL)), _full((1, 1))],
        out_shape=[sd((s, D_MODEL), F32), sd((s, ATTN_W), F32), sd((s, ATTN_W + SSM_W), BF), sd((s, SSM_W), F32),
                   sd((s, 2 * D_MODEL), BF),
                   sd((s, ATTN_W), BF), sd((s, D_MODEL), BF), sd((s, SSM_W), BF), sd((s, D_MODEL), BF),
                   sd((s, D_MODEL), BF), sd((s, D_MODEL), BF),
                   sd((1, SSM_W), F32), sd((1, D_MODEL), F32), sd((1, 1), F32)],
        scratch_shapes=[pltpu.VMEM((ATTN_W, D_MODEL), BF), pltpu.VMEM((SSM_W, D_MODEL), BF), pltpu.VMEM((D_MODEL, D_MODEL), BF),
                        pltpu.SemaphoreType.DMA((3,))],
        compiler_params=_params(dimension_semantics=("arbitrary",)),
    )(x, tgt, o_att, zam, ypre, gab, gate, ssm_nw, rows_all)


def _attn_bwd(q, kv, bias, sinks, consts, o_att, lse, d_o):
    s = q.shape[0]
    nb = s // BLOCK
    folds = (_fold(ATTN_W, HEAD_DIM), _fold(KV_W, HEAD_DIM))

    def body(q_ref, kp_ref, kc_ref, vp_ref, vc_ref, b_ref, skv_ref, qw_ref, kw_ref, eq_ref, eq3_ref, ek_ref, ek3_ref,
             fq_ref, fk_ref, o_ref, lse_ref, do_ref,
             dq_ref, dkv_ref, dss_ref, gqw_ref, gkw_ref, gsk_ref, ckn, cv, dqn_s, dkn_s, dv_s, gq_x, gk_x):
        i = pl.program_id(0)
        kw, ek, ek3 = kw_ref[...], ek_ref[...], ek3_ref[...]

        @pl.when(i == 0)
        def _():
            for ref in (ckn, cv, dss_ref, gq_x, gk_x, gsk_ref):
                ref[...] = jnp.zeros_like(ref)

        @pl.when(i < nb)
        def _():
            qw, eq, eq3 = qw_ref[...], eq_ref[...], eq3_ref[...]
            qf = q_ref[...]
            qnf, rq_x = _heads_norm(qf, qw, eq, eq3)
            qn = _bf(qnf * (HEAD_DIM ** -0.5))
            kf = jnp.concatenate([kp_ref[...], kc_ref[...]], axis=0)
            knf, rk_x = _heads_norm(kf, kw, ek, ek3)
            kn = _bf(knf)
            vv = _bf(jnp.concatenate([vp_ref[...], vc_ref[...]], axis=0))
            d_of = do_ref[...]
            d_ob = _bf(d_of)
            lse_all = lse_ref[...]
            delta = _dot(_bf(d_of * o_ref[...]), eq)
            gsk_ref[...] += jnp.sum(-jnp.exp(skv_ref[...] - lse_all) * delta, axis=0, keepdims=True)
            kss = [slice(hk * HEAD_DIM, (hk + 1) * HEAD_DIM) for hk in range(KV_HEADS)]
            qgs = [_stack_heads(qn, hk) for hk in range(KV_HEADS)]
            d_ogs = [_stack_heads(d_ob, hk) for hk in range(KV_HEADS)]
            scs = [_dot_nt(qgs[hk], kn[:, kss[hk]]) + b_ref[0, hk * GRP:(hk + 1) * GRP].reshape(GRP * BLOCK, 2 * BLOCK)
                   for hk in range(KV_HEADS)]
            dps = [_dot_nt(d_ogs[hk], vv[:, kss[hk]]) for hk in range(KV_HEADS)]
            ps = [jnp.exp(scs[hk] - _stack_cols(lse_all, hk)) for hk in range(KV_HEADS)]
            dss = [ps[hk] * (dps[hk] - _stack_cols(delta, hk)) for hk in range(KV_HEADS)]
            pbs = [_bf(p) for p in ps]
            dsbs = [_bf(ds) for ds in dss]
            for hk in range(KV_HEADS):
                dss_ref[hk * GRP:(hk + 1) * GRP] += dss[hk].reshape(GRP, BLOCK, 2 * BLOCK)
            for hk in range(KV_HEADS):
                dv_s[:, kss[hk]] = _dot_tn(pbs[hk], d_ogs[hk])
                dkn_s[:, kss[hk]] = _dot_tn(dsbs[hk], qgs[hk])
            dqns = [_dot(dsbs[hk], kn[:, kss[hk]]) * (HEAD_DIM ** -0.5) for hk in range(KV_HEADS)]
            for hk in range(KV_HEADS):
                for g in range(GRP):
                    h = hk * GRP + g
                    dqn_s[:, h * HEAD_DIM:(h + 1) * HEAD_DIM] = dqns[hk][g * BLOCK:(g + 1) * BLOCK]
            dq, gq = _heads_norm_bwd(qf, rq_x, qw, dqn_s[...], eq, eq3)
            dq_ref[...] = _bf(dq)
            gq_x[...] += gq
            dk, gk = _heads_norm_bwd(kf[:BLOCK], rk_x[:BLOCK], kw, ckn[...] + dkn_s[0:BLOCK, :], ek, ek3)
            dkv_ref[:, :KV_W] = _bf(dk)
            gk_x[...] += gk
            dkv_ref[:, KV_W:] = _bf(cv[...] + dv_s[0:BLOCK, :])
            ckn[...] = dkn_s[BLOCK:2 * BLOCK, :]
            cv[...] = dv_s[BLOCK:2 * BLOCK, :]

        @pl.when(i == nb)
        def _():
            kc = kc_ref[...]
            dk, gk = _heads_norm_bwd(kc, _heads_norm(kc, kw, ek, ek3)[1], kw, ckn[...], ek, ek3)
            dkv_ref[:, :KV_W] = _bf(dk)
            dkv_ref[:, KV_W:] = _bf(cv[...])
            gqw_ref[...] = _group_sum(jnp.broadcast_to(gq_x[...], (8, ATTN_W)), fq_ref[...])[0:1]
            gkw_ref[...] = _group_sum(jnp.broadcast_to(gk_x[...] + gk, (8, KV_W)), fk_ref[...])[0:1]

    last = nb - 1
    cur = lambda w, col=0: pl.BlockSpec((BLOCK, w), lambda i: (jnp.minimum(i, last), col))
    prev = lambda w, col=0: pl.BlockSpec((BLOCK, w), lambda i: (jnp.maximum(jnp.minimum(i, last) - 1, 0), col))
    late = lambda w: pl.BlockSpec((BLOCK, w), lambda i: (jnp.maximum(i - 1, 0), 0))
    sd = jax.ShapeDtypeStruct
    return pl.pallas_call(
        body, name="attn_bwd", grid=(nb + 1,),
        in_specs=[cur(ATTN_W), prev(KV_W, 0), cur(KV_W, 0), prev(KV_W, 1), cur(KV_W, 1),
                  pl.BlockSpec((1, ATTN_HEADS, BLOCK, 2 * BLOCK), lambda i: (jnp.minimum(i, 1), 0, 0, 0)),
                  _full((1, ATTN_HEADS))]
                 + [_full(c.shape) for c in consts + folds] + [cur(ATTN_W), cur(ATTN_HEADS), cur(ATTN_W)],
        out_specs=[cur(ATTN_W), late(2 * KV_W),
                   pl.BlockSpec((ATTN_HEADS, BLOCK, 2 * BLOCK), lambda i: (0, 0, 0)),
                   _full((1, HEAD_DIM)), _full((1, HEAD_DIM)), _full((1, ATTN_HEADS))],
        out_shape=[sd((s, ATTN_W), BF), sd((s, 2 * KV_W), BF),
                   sd((ATTN_HEADS, BLOCK, 2 * BLOCK), F32), sd((1, HEAD_DIM), F32), sd((1, HEAD_DIM), F32),
                   sd((1, ATTN_HEADS), F32)],
        scratch_shapes=[pltpu.VMEM((BLOCK, KV_W), F32), pltpu.VMEM((BLOCK, KV_W), F32),
                        pltpu.VMEM((BLOCK, ATTN_W), F32), pltpu.VMEM((2 * BLOCK, KV_W), F32),
                        pltpu.VMEM((2 * BLOCK, KV_W), F32), pltpu.VMEM((1, ATTN_W), F32), pltpu.VMEM((1, KV_W), F32)],
        compiler_params=_params(dimension_semantics=("arbitrary",)),
    )(q, kv, kv, kv, kv, bias, sinks, *consts, *folds, o_att, lse, d_o)


def _ssd_bwd(xbc, conv_all, dt_raw, conv_w, dt_bias, a_log, dsk_x, e_mat, e3t, hprev_all, dy_all):
    s = xbc.shape[0]
    nc = s // BLOCK
    ch = 1
    rows = ch * BLOCK
    nsteps = nc // ch
    gw = SSM_R * SSM_P
    b0, c0 = SSM_W, SSM_W + SSM_G * SSM_N

    def body(x_ref, conv_ref, dtr_ref, cw_ref, dtb_ref, alog_ref, dsk_ref, e_ref, e3_ref, hp_ref, dy_ref,
             dx_ref, ddt_ref, gcw_ref, gcb_ref, gdtb_ref, galog_ref, gdsk_ref,
             dh, nhead, gdskx, dxdt_s, dbc_s, dxd_s):
        def chunk_bwd(j):
            rs = slice(j * BLOCK, (j + 1) * BLOCK)
            conv = conv_ref[rs, :]
            sg, xact, u, dt, a, trilb, acum, dt_x, acum_x = _ssd_common(conv, dtr_ref[rs, :], dtb_ref, alog_ref, e3_ref)
            xs = xact[:, :SSM_W]
            acum_t = acum.T
            ea_x = jnp.exp2(acum_x)
            last_x = acum_x[BLOCK - 1:BLOCK, :]
            dte_x = jnp.exp2(last_x - acum_x)
            cd_x = jnp.exp2(last_x)
            xdt = xs * dt_x
            xw = xdt * dte_x
            hprev = hp_ref[j]
            dhn = dh[...]
            dy = dy_ref[rs, :]
            gdskx[...] += jnp.sum(dy * xs, axis=0, keepdims=True)
            dyea = dy * ea_x
            lane = lax.broadcasted_iota(jnp.int32, (BLOCK, SSM_HEADS), 1)
            dacum = jnp.zeros((BLOCK, SSM_HEADS), F32)
            dacc_x, dlast_x = [], []
            sls = [slice(g * gw, (g + 1) * gw) for g in range(SSM_G)]
            bgs = [_bf(xact[:, b0 + g * SSM_N:b0 + (g + 1) * SSM_N]) for g in range(SSM_G)]
            cgs = [_bf(xact[:, c0 + g * SSM_N:c0 + (g + 1) * SSM_N]) for g in range(SSM_G)]
            hpgs = [_bf(hprev[:, sl]) for sl in sls]
            dhgs = [_bf(dhn[:, sl]) for sl in sls]
            dyeags = [_bf(dyea[:, sl]) for sl in sls]
            xwgs = [_bf(xw[:, sl]) for sl in sls]
            xdt_b, dy_b = _bf(xdt), _bf(dy)
            low_half = lax.broadcasted_iota(jnp.int32, (BLOCK, 2 * SSM_P), 1) < SSM_P
            cbs = [_dot_nt(cgs[g], bgs[g]) for g in range(SSM_G)]
            gmats = [_dot(cgs[g], hpgs[g]) for g in range(SSM_G)]
            dxws = [_dot(bgs[g], dhgs[g]) for g in range(SSM_G)]
            dcgs = [_dot_nt(dyeags[g], hpgs[g]) for g in range(SSM_G)]
            dbgs = [_dot_nt(xwgs[g], dhgs[g]) for g in range(SSM_G)]
            for g in range(SSM_G):
                sl = sls[g]
                dh[:, sl] = dhn[:, sl] * cd_x[:, sl] + _dot_tn(cgs[g], dyeags[g])
                dxdt_s[:, sl] = dxws[g] * dte_x[:, sl]
                dacc_x.append(dy[:, sl] * gmats[g] * ea_x[:, sl] - dxws[g] * xw[:, sl])
                dlast_x.append(jnp.sum(dxws[g] * xw[:, sl], axis=0, keepdims=True)
                               + jnp.sum(dhn[:, sl] * hprev[:, sl], axis=0, keepdims=True) * cd_x[:, sl])
            for g in range(SSM_G):
                bg, cg, cb, dbg, dcg = bgs[g], cgs[g], cbs[g], dbgs[g], dcgs[g]
                hss = [slice((g * SSM_R + r) * SSM_P, (g * SSM_R + r + 1) * SSM_P) for r in range(SSM_R)]
                lms = [jnp.exp2(jnp.where(trilb, acum[:, g * SSM_R + r:g * SSM_R + r + 1]
                                         - acum_t[g * SSM_R + r:g * SSM_R + r + 1, :], -1e30)) for r in range(SSM_R)]
                mms = [cb * lm for lm in lms]
                mmbs = [_bf(mm) for mm in mms]
                dms = []
                for r in range(0, SSM_R, 2):
                    pair = slice(hss[r].start, hss[r + 1].stop)
                    xp, dyp = xdt_b[:, pair], dy_b[:, pair]
                    dmp = _dot_nt(dyp, jnp.concatenate([jnp.where(low_half, xp, 0), jnp.where(low_half, 0, xp)], axis=0))
                    dms += [dmp[:, :BLOCK], dmp[:, BLOCK:]]
                    dxd_s[:, pair] = _dot_tn(jnp.concatenate([mmbs[r], mmbs[r + 1]], axis=0),
                                             jnp.concatenate([jnp.where(low_half, dyp, 0), jnp.where(low_half, 0, dyp)], axis=0))
                dcb = sum(dms[r] * lms[r] for r in range(SSM_R))
                wms = [dms[r] * mms[r] for r in range(SSM_R)]
                antis = [_bf(wm - wm.T) for wm in wms]
                for r in range(SSM_R):
                    dacum = dacum + _dot(antis[r], (lane == g * SSM_R + r).astype(BF))
                dcbb = _bf(dcb)
                dbc_s[:, g * SSM_N:(g + 1) * SSM_N] = dbg + _dot_tn(dcbb, cg)
                dbc_s[:, SSM_G * SSM_N + g * SSM_N:SSM_G * SSM_N + (g + 1) * SSM_N] = dcg + _dot(dcbb, bg)
            dxdt = dxdt_s[...] + dxd_s[...]
            dxs = dy * dsk_ref[...] + dxdt * dt_x
            red = _group_sum(jnp.concatenate(
                [dxdt * xs, jnp.concatenate(dacc_x, axis=1),
                 jnp.broadcast_to(jnp.concatenate(dlast_x, axis=1), (8, SSM_W))], axis=0), e_ref[...])
            row = lax.broadcasted_iota(jnp.int32, (BLOCK, SSM_HEADS), 0)
            dacum = dacum + red[BLOCK:2 * BLOCK] + jnp.where(row == BLOCK - 1, red[2 * BLOCK:2 * BLOCK + 1], 0.0)
            ddta = _exact_left(_triu().astype(BF), dacum)
            ddt = red[:BLOCK] + ddta * a
            galog_ref[...] += jnp.sum(ddta * dt, axis=0, keepdims=True) * a
            du = ddt * _sig(u)
            ddt_ref[rs, :] = _bf(du)
            gdtb_ref[...] += jnp.sum(du, axis=0, keepdims=True)
            dconv = jnp.concatenate([dxs, dbc_s[...]], axis=1) * _dsilu(conv, sg, xact)
            gcb_ref[...] += jnp.sum(dconv, axis=0, keepdims=True)
            ext2 = jnp.concatenate([dconv, nhead[...]], axis=0)
            ahead = [pltpu.roll(ext2, BLOCK + 8 - (CONV_K - 1 - j), axis=0)[0:BLOCK] if j < CONV_K - 1 else dconv
                     for j in range(CONV_K)]
            dx_ref[rs, :] = _bf(sum(ahead[j] * cw_ref[j:j + 1, :] for j in range(CONV_K)))
            xraw = x_ref[rs, :]
            gcw_ref[...] += jnp.concatenate([jnp.sum(ahead[j] * xraw, axis=0, keepdims=True) for j in range(CONV_K)], axis=0)
            nhead[...] = dconv[0:8]

        i = pl.program_id(0)

        @pl.when(i == 0)
        def _():
            for ref in (dh, nhead, gdskx, gcw_ref, gcb_ref, gdtb_ref, galog_ref, gdsk_ref):
                ref[...] = jnp.zeros_like(ref)

        for j in reversed(range(ch)):
            chunk_bwd(j)

        @pl.when(i == nsteps - 1)
        def _():
            gdsk_ref[...] = _group_sum(jnp.broadcast_to(gdskx[...], (8, SSM_W)), e_ref[...])[0:1]

    chunk = lambda w: pl.BlockSpec((rows, w), lambda i: (nsteps - 1 - i, 0))
    sd = jax.ShapeDtypeStruct
    return pl.pallas_call(
        body, name="ssd_bwd", grid=(nsteps,),
        in_specs=[chunk(XBC_W), chunk(XBC_W),
                  chunk(SSM_HEADS), _full((CONV_K, XBC_W)), _full((1, SSM_HEADS)),
                  _full((1, SSM_HEADS)), _full((1, SSM_W)), _full((SSM_W, SSM_HEADS)), _full((3 * SSM_HEADS, SSM_W)),
                  pl.BlockSpec((ch, SSM_N, SSM_W), lambda i: (nsteps - 1 - i, 0, 0)), chunk(SSM_W)],
        out_specs=[chunk(XBC_W), chunk(SSM_HEADS), _full((CONV_K, XBC_W)), _full((1, XBC_W)),
                   _full((1, SSM_HEADS)), _full((1, SSM_HEADS)), _full((1, SSM_HEADS))],
        out_shape=[sd((s, XBC_W), BF), sd((s, SSM_HEADS), BF), sd((CONV_K, XBC_W), F32), sd((1, XBC_W), F32),
                   sd((1, SSM_HEADS), F32), sd((1, SSM_HEADS), F32), sd((1, SSM_HEADS), F32)],
        scratch_shapes=[pltpu.VMEM((SSM_N, SSM_W), F32), pltpu.VMEM((8, XBC_W), F32),
                        pltpu.VMEM((1, SSM_W), F32), pltpu.VMEM((BLOCK, SSM_W), F32),
                        pltpu.VMEM((BLOCK, 2 * SSM_G * SSM_N), F32), pltpu.VMEM((BLOCK, SSM_W), F32)],
        compiler_params=_params(dimension_semantics=("arbitrary",)),
    )(xbc, conv_all, dt_raw, conv_w, dt_bias, a_log, dsk_x, e_mat, e3t, hprev_all, dy_all)


def _dh(x, dout, norm_w, scale, dsegs, w_t, tm=256):
    s = x.shape[0]

    def body(x_ref, dout_ref, nw_ref, sc_ref, *rest):
        d_refs, w_hbm = rest[:NSEG], rest[NSEG]
        gx_ref, dshift_ref, dscale_ref, gnw_ref = rest[NSEG + 1:NSEG + 5]
        w_vm, sem = rest[NSEG + 5], rest[NSEG + 6]
        first = pl.program_id(0) == 0
        cps = [pltpu.make_async_copy(w_hbm.at[SEG_OFF[j]:SEG_OFF[j + 1], :], w_vm.at[SEG_OFF[j]:SEG_OFF[j + 1], :], sem.at[j])
               for j in range(NSEG)]

        def tile(waiting):
            dh = None
            for j in range(NSEG):
                if waiting:
                    cps[j].wait()
                part = _dot(d_refs[j][...], w_vm[SEG_OFF[j]:SEG_OFF[j + 1], :])
                dh = part if dh is None else dh + part
            xv = x_ref[...]
            r = lax.rsqrt(jnp.mean(xv * xv, axis=-1, keepdims=True) + EPS)
            xn = xv * r
            nw = nw_ref[...]
            sc1 = 1.0 + sc_ref[...]
            dshift_ref[...] += jnp.sum(dh, axis=0, keepdims=True)
            dhxn = jnp.sum(dh * xn, axis=0, keepdims=True)
            dscale_ref[...] += dhxn * nw
            gnw_ref[...] += dhxn * sc1
            dxn = dh * (nw * sc1)
            gx_ref[...] = dout_ref[...] + r * (dxn - xn * jnp.mean(xn * dxn, axis=-1, keepdims=True))

        @pl.when(first)
        def _():
            for cp in cps:
                cp.start()
            for ref in (dshift_ref, dscale_ref, gnw_ref):
                ref[...] = jnp.zeros_like(ref)
            tile(True)

        @pl.when(jnp.logical_not(first))
        def _():
            tile(False)

    vec = _full((1, D_MODEL))
    sd = jax.ShapeDtypeStruct
    return pl.pallas_call(
        body, name="dh", grid=(s // tm,),
        in_specs=[_rows(tm, D_MODEL), _rows(tm, D_MODEL), vec, vec] + [_rows(tm, w) for w in SEG_W] + [ANY],
        out_specs=[_rows(tm, D_MODEL), vec, vec, vec],
        out_shape=[sd((s, D_MODEL), F32), sd((1, D_MODEL), F32), sd((1, D_MODEL), F32), sd((1, D_MODEL), F32)],
        scratch_shapes=[pltpu.VMEM((IN_W, D_MODEL), BF), pltpu.SemaphoreType.DMA((NSEG,))],
        compiler_params=_params(dimension_semantics=("arbitrary",)),
    )(x, dout, norm_w, scale, *dsegs, w_t)


def _gw_seg(h, dseg, name, tm=1024):
    s, w = dseg.shape
    tn = min(w, 1024)
    tm = min(tm, s)
    nm = s // tm

    def body(h_ref, d_ref, o_ref, acc):
        m = pl.program_id(1)

        @pl.when(m == 0)
        def _():
            acc[...] = jnp.zeros_like(acc)

        acc[...] += _dot_tn(d_ref[...], h_ref[...])

        @pl.when(m == nm - 1)
        def _():
            o_ref[...] = _bf(acc[...])

    return pl.pallas_call(
        body, name=name, grid=(w // tn, nm),
        in_specs=[pl.BlockSpec((tm, D_MODEL), lambda n, m: (m, 0)), pl.BlockSpec((tm, tn), lambda n, m: (m, n))],
        out_specs=pl.BlockSpec((tn, D_MODEL), lambda n, m: (n, 0)),
        out_shape=jax.ShapeDtypeStruct((w, D_MODEL), BF),
        scratch_shapes=[pltpu.VMEM((tn, D_MODEL), F32)],
        compiler_params=_params(dimension_semantics=("arbitrary", "arbitrary")),
    )(h, dseg)


def _gw_in(h, dsegs):
    return [_gw_seg(h, d, "gw_in_%d" % j) for j, d in enumerate(dsegs)]


def _local_step(x, tgt, shift, scale, gate, w_t, rows_fn, norm_w, qnw, knw, rel_bias, sinks,
                conv_w, conv_b, dt_bias, a_log, d_skip, ssm_nw, after_mid=None, after_gw=None):
    oh_t = _bucket_onehot_t()
    bias = _masked_bias(_bias_dense(rel_bias.T, oh_t).reshape(ATTN_HEADS, BLOCK, 2 * BLOCK))
    *segs, h = _inproj(x, norm_w, scale, shift, w_t)
    q, kv, zam, xbc, dtr, gab = segs
    consts = _attn_consts(qnw, knw)
    o_att, lse = _attn_fwd(q, kv, bias, sinks, consts)
    e_mat, e3t = _membership(SSM_W, SSM_P, SSM_HEADS)
    dsk_x = jnp.repeat(d_skip, SSM_P, axis=1)
    ypre, hprev, conv = _ssd_fwd(xbc, dtr, conv_w, conv_b, dt_bias, a_log, dsk_x, e3t)
    (dout, d_o, dzam, dyp, dgab, yag, dy_a, yn, dy_b, merged, dob, g_ssm_nw, dgate, loss) = _mid(
        x, tgt, o_att, zam, ypre, gab, gate, ssm_nw, rows_fn(ypre))
    g_wap = _gw_seg(dy_a, yag, "gw_attn_proj")
    g_wsp = _gw_seg(dy_b, yn, "gw_ssm_proj")
    g_wout = _gw_seg(dob, merged, "gw_out")
    zero = after_mid(g_wap, g_wsp, g_wout) if after_mid is not None else 0.0
    dq, dkv, dss, g_qnw, g_knw, g_sinks = _attn_bwd(q, kv, bias, sinks + zero, consts, o_att, lse, d_o)
    g_rel = _bias_grad(dss.reshape(ATTN_HEADS, BLOCK * 2 * BLOCK), oh_t).T
    dxbc, ddt, g_cw, g_cb, g_dtb, g_alog, g_dsk = _ssd_bwd(
        xbc, conv, dtr, conv_w, dt_bias, a_log, dsk_x, e_mat, e3t, hprev, dyp)
    dsegs = (dq, dkv, dzam, dxbc, ddt, dgab)
    g_ws = _gw_in(h, dsegs)
    zero = after_gw(g_ws) if after_gw is not None else 0.0
    gx, dshift, dscale, g_nw = _dh(x, dout, norm_w + zero, scale, dsegs, w_t)
    return dict(loss=loss, grad_x=gx, dmod=jnp.concatenate([dshift, dscale, dgate], axis=1), g_ws=g_ws,
                g_wap=g_wap, g_wsp=g_wsp, g_wout=g_wout, g_norm_w=g_nw, g_qnw=g_qnw, g_knw=g_knw, g_rel=g_rel,
                g_sinks=g_sinks, g_conv_w=g_cw, g_conv_b=g_cb, g_dt_bias=g_dtb, g_a_log=g_alog, g_d_skip=g_dsk,
                g_ssm_nw=g_ssm_nw)


def _me():
    return lax.axis_index("x"), lax.axis_index("y"), lax.axis_index("c")


def _flip(v, bit):
    return 1 - v if bit else v


def _ag_direct(v, name):
    def body(v_ref, out_ref, send_sems, recv_sems, local_sem):
        x, y, c = _me()
        me = 4 * x + 2 * y + c
        mine = pltpu.make_async_copy(v_ref, out_ref.at[me], local_sem)
        mine.start()
        peers = [(_flip(x, k >> 2 & 1), _flip(y, k >> 1 & 1), _flip(c, k & 1)) for k in range(1, N_DEV)]
        sends = [pltpu.make_async_remote_copy(
            src_ref=v_ref, dst_ref=out_ref.at[me], send_sem=send_sems.at[j], recv_sem=recv_sems.at[j],
            device_id=p, device_id_type=MESH) for j, p in enumerate(peers)]
        for cp in sends:
            cp.start()
        for j, (px, py, pc) in enumerate(peers):
            pltpu.make_async_remote_copy(
                src_ref=v_ref, dst_ref=out_ref.at[4 * px + 2 * py + pc], send_sem=send_sems.at[j],
                recv_sem=recv_sems.at[j], device_id=(px, py, pc), device_id_type=MESH).wait_recv()
        for cp in sends:
            cp.wait_send()
        mine.wait()

    vm = pl.BlockSpec(memory_space=pltpu.VMEM)
    return pl.pallas_call(
        body, name=name, out_shape=jax.ShapeDtypeStruct((N_DEV,) + v.shape, v.dtype),
        in_specs=[vm], out_specs=vm,
        scratch_shapes=[pltpu.SemaphoreType.DMA((N_DEV - 1,)), pltpu.SemaphoreType.DMA((N_DEV - 1,)),
                        pltpu.SemaphoreType.DMA],
        compiler_params=_params(),
    )(v)


def _ag_two_level(v, name):
    def body(v_ref, out_ref, token, send_sems, recv_sems, local_sem):
        token[...] = jnp.zeros_like(token)
        x, y, c = _me()
        me, sibling = (x, y, c), (x, y, 1 - c)
        chips = [(1 - x, y), (x, 1 - y), (1 - x, 1 - y)]

        def slot(px, py, pc):
            return out_ref.at[4 * px + 2 * py + pc]

        def copy(k, block, to, src=None):
            return pltpu.make_async_remote_copy(
                src_ref=slot(*block) if src is None else src, dst_ref=slot(*block),
                send_sem=send_sems.at[k], recv_sem=recv_sems.at[k], device_id=to, device_id_type=MESH)

        mine = pltpu.make_async_copy(v_ref, slot(*me), local_sem)
        mine.start()
        first = [copy(0, me, sibling, src=v_ref)]
        first += [copy(1 + j, me, (*chip, c), src=v_ref) for j, chip in enumerate(chips)]
        for cp in first:
            cp.start()
        passed = [copy(4 + j, (*chip, c), sibling) for j, chip in enumerate(chips)]
        for j, chip in enumerate(chips):
            copy(1 + j, (*chip, c), me).wait_recv()
            passed[j].start()
        copy(0, sibling, me).wait_recv()
        for j, chip in enumerate(chips):
            copy(4 + j, (*chip, 1 - c), me).wait_recv()
        for cp in first + passed:
            cp.wait_send()
        mine.wait()

    out, token = pl.pallas_call(
        body, name=name,
        out_shape=(jax.ShapeDtypeStruct((N_DEV,) + v.shape, v.dtype), jax.ShapeDtypeStruct((8, 128), v.dtype)),
        in_specs=[ANY], out_specs=(ANY, pl.BlockSpec(memory_space=pltpu.VMEM)),
        scratch_shapes=[pltpu.SemaphoreType.DMA((7,)), pltpu.SemaphoreType.DMA((7,)), pltpu.SemaphoreType.DMA],
        compiler_params=_params(),
    )(v)
    return out, token[0:1, 0:1]


HBM = pl.BlockSpec(memory_space=pltpu.HBM)
SEM = pl.BlockSpec(memory_space=pltpu.SEMAPHORE)
EFFECT = pltpu.SideEffectType.DATAFLOW_SIDE_EFFECTING


def _peers(x, y, c):
    return [(_flip(x, k >> 2 & 1), _flip(y, k >> 1 & 1), _flip(c, k & 1)) for k in range(1, N_DEV)]


def _exchange_start(src, land, gather, name):
    def body(src_ref, land_ref, send_sems, recv_sems, src_thru, land_thru, token):
        x, y, c = _me()
        me = 4 * x + 2 * y + c
        for j, (px, py, pc) in enumerate(_peers(x, y, c)):
            pltpu.make_async_remote_copy(
                src_ref=src_ref if gather else src_ref.at[4 * px + 2 * py + pc], dst_ref=land_ref.at[me],
                send_sem=send_sems.at[j], recv_sem=recv_sems.at[j], device_id=(px, py, pc), device_id_type=MESH).start()
        token[...] = jnp.zeros_like(token)

    sems = pltpu.SemaphoreType.DMA((N_DEV - 1,))
    out = pl.pallas_call(
        body, name=name,
        out_shape=(sems, sems, pltpu.HBM(src.shape, src.dtype), pltpu.HBM(land.shape, land.dtype),
                   jax.ShapeDtypeStruct((8, 128), F32)),
        in_specs=(HBM, HBM), out_specs=(SEM, SEM, HBM, HBM, pl.BlockSpec(memory_space=pltpu.VMEM)),
        input_output_aliases={0: 2, 1: 3},
        compiler_params=pltpu.CompilerParams(has_side_effects=EFFECT),
    )(pltpu.with_memory_space_constraint(src, pltpu.HBM), pltpu.with_memory_space_constraint(land, pltpu.HBM))
    return out[:4], out[4][0, 0]


def _exchange_wait(started, after, gather, name):
    send_sems, recv_sems, src_thru, land_thru = started

    def body(src_ref, land_ref, send_sems, recv_sems, after_ref, src_dead, got_ref):
        x, y, c = _me()
        for j, (px, py, pc) in enumerate(_peers(x, y, c)):
            pid = 4 * px + 2 * py + pc
            cp = pltpu.make_async_remote_copy(
                src_ref=src_ref if gather else src_ref.at[pid], dst_ref=land_ref.at[pid],
                send_sem=send_sems.at[j], recv_sem=recv_sems.at[j], device_id=(px, py, pc), device_id_type=MESH)
            cp.wait_send()
            cp.wait_recv()

    return pl.pallas_call(
        body, name=name,
        out_shape=(pltpu.HBM(src_thru.shape, src_thru.dtype), pltpu.HBM(land_thru.shape, land_thru.dtype)),
        in_specs=(HBM, HBM, SEM, SEM, ANY), out_specs=(HBM, HBM), input_output_aliases={0: 0, 1: 1},
        compiler_params=pltpu.CompilerParams(has_side_effects=EFFECT),
    )(src_thru, land_thru, send_sems, recv_sems, after)[1]


def _silu(a):
    return a * _sig(a)


def _mod_piece(c_all, w_ada, b_piece):
    def body(c_ref, w_ref, b_ref, o_ref):
        o_ref[...] = _dot(_bf(_silu(c_ref[...])), _bf(w_ref[...])) + b_ref[...]

    return pl.pallas_call(
        body, name="mod_piece", out_shape=jax.ShapeDtypeStruct((c_all.shape[0], w_ada.shape[1]), F32),
        compiler_params=_params(),
    )(c_all, w_ada, b_piece)


def _gw_ada(c_all, dmod_piece):
    def body(c_ref, d_ref, o_ref):
        o_ref[...] = _dot_tn(_bf(_silu(c_ref[...])), _bf(d_ref[...]))

    return pl.pallas_call(
        body, name="gw_ada", out_shape=jax.ShapeDtypeStruct((c_all.shape[1], dmod_piece.shape[1]), F32),
        compiler_params=_params(),
    )(c_all, dmod_piece)


def _adam(parts, w, m, v, name):
    k, r, n = parts.shape
    if r <= 256 or r % 256 == 0:
        tr, tn = min(r, 256), n
    else:
        tr, tn = r, 256
    assert r % tr == 0 and n % tn == 0

    def body(p_ref, w_ref, m_ref, v_ref, g_ref, d_ref, nm_ref, nv_ref):
        g = p_ref[0].astype(F32)
        for j in range(1, k):
            g = g + p_ref[j].astype(F32)
        g_ref[...] = g
        d_ref[...], nm_ref[...], nv_ref[...] = _adam_math(g, w_ref[...], m_ref[...], v_ref[...])

    blk = pl.BlockSpec((tr, tn), lambda i, j: (i, j))
    return pl.pallas_call(
        body, name=name, grid=(r // tr, n // tn),
        in_specs=[pl.BlockSpec((k, tr, tn), lambda i, j: (0, i, j)), blk, blk, blk],
        out_specs=[blk, blk, blk, blk],
        out_shape=[jax.ShapeDtypeStruct((r, n), F32)] * 4,
        compiler_params=_params(dimension_semantics=("arbitrary", "arbitrary")),
    )(parts, w, m, v)


def _adam_math(g, w, m, v):
    m_new = ADAM_B1 * m + (1.0 - ADAM_B1) * g
    v_new = ADAM_B2 * v + (1.0 - ADAM_B2) * jnp.square(g)
    m_hat = m_new / (1.0 - ADAM_B1 ** ADAM_STEP)
    v_hat = v_new / (1.0 - ADAM_B2 ** ADAM_STEP)
    return -ADAM_LR * (m_hat / (jnp.sqrt(v_hat) + ADAM_EPS) + ADAM_WD * w), m_new, v_new


_SMALL = (("b_ada", 3 * D_MODEL), ("norm_w", D_MODEL), ("q_norm_w", HEAD_DIM), ("k_norm_w", HEAD_DIM),
          ("rel_bias", REL_BUCKETS * ATTN_HEADS), ("sinks", ATTN_HEADS), ("conv_b", XBC_W), ("dt_bias", SSM_HEADS),
          ("a_log", SSM_HEADS), ("d_skip", SSM_HEADS), ("ssm_norm_w", SSM_W))
_SLOT = tuple(-(-n // 128) * 128 for _, n in _SMALL)
_SLOT_OFF = tuple(int(o) for o in np.cumsum((0,) + _SLOT))
_LOSS_OFF = _SLOT_OFF[-1]
_CW_OFF = _LOSS_OFF + 128
_PACK_N = _CW_OFF + CONV_K * XBC_W


def _pack_partials(small, loss, g_conv_w):
    parts = []
    for (name, n), slot in zip(_SMALL, _SLOT):
        parts.append(small[name].reshape(1, n))
        if slot > n:
            parts.append(jnp.zeros((1, slot - n), F32))
    parts += [loss.reshape(1, 1), jnp.zeros((1, 127), F32), g_conv_w.reshape(1, CONV_K * XBC_W)]
    return jnp.concatenate(parts, axis=1)


def _adam_small(pack_all, w, m, v):
    names = [name for name, _ in _SMALL]

    def body(p_ref, *rest):
        ins, outs = rest[:3 * len(names)], rest[3 * len(names):]

        def total(off, n):
            g = p_ref[0, :, off:off + n]
            for d in range(1, N_DEV):
                g = g + p_ref[d, :, off:off + n]
            return g

        for j, (name, n) in enumerate(_SMALL):
            g = total(_SLOT_OFF[j], n)
            delta, m_new, v_new = _adam_math(g, ins[3 * j][...], ins[3 * j + 1][...], ins[3 * j + 2][...])
            outs[4 * j][...] = g
            outs[4 * j + 1][...] = delta
            outs[4 * j + 2][...] = m_new
            outs[4 * j + 3][...] = v_new
        outs[-1][...] = total(_LOSS_OFF, 1)

    flat = []
    for name, n in _SMALL:
        flat += [w[name].reshape(1, n), m[name].reshape(1, n), v[name].reshape(1, n)]
    out_shape = [jax.ShapeDtypeStruct((1, n), F32) for _, n in _SMALL for _ in range(4)] + [jax.ShapeDtypeStruct((1, 1), F32)]
    out = pl.pallas_call(body, name="adam_small", out_shape=out_shape, compiler_params=_params())(pack_all, *flat)
    res = {name: [out[4 * j + t].reshape(w[name].shape) for t in range(4)] for j, name in enumerate(names)}
    return res, out[-1]


WEIGHTS = ("w_ada", "b_ada", "norm_w", "w_in", "q_norm_w", "k_norm_w", "rel_bias", "sinks", "conv_w", "conv_b",
           "dt_bias", "a_log", "d_skip", "ssm_norm_w", "w_attn_proj", "w_ssm_proj", "w_out")


def kernel(x, c, w_ada, b_ada, norm_w, w_in, q_norm_w, k_norm_w, rel_bias, sinks, conv_w, conv_b, dt_bias, a_log, d_skip, ssm_norm_w, w_attn_proj, w_ssm_proj, w_out, loss_target, m_w_ada, m_b_ada, m_norm_w, m_w_in, m_q_norm_w, m_k_norm_w, m_rel_bias, m_sinks, m_conv_w, m_conv_b, m_dt_bias, m_a_log, m_d_skip, m_ssm_norm_w, m_w_attn_proj, m_w_ssm_proj, m_w_out, v_w_ada, v_b_ada, v_norm_w, v_w_in, v_q_norm_w, v_k_norm_w, v_rel_bias, v_sinks, v_conv_w, v_conv_b, v_dt_bias, v_a_log, v_d_skip, v_ssm_norm_w, v_w_attn_proj, v_w_ssm_proj, v_w_out):
    w = dict(w_ada=w_ada, b_ada=b_ada, norm_w=norm_w, w_in=w_in, q_norm_w=q_norm_w, k_norm_w=k_norm_w,
             rel_bias=rel_bias, sinks=sinks, conv_w=conv_w, conv_b=conv_b, dt_bias=dt_bias, a_log=a_log,
             d_skip=d_skip, ssm_norm_w=ssm_norm_w, w_attn_proj=w_attn_proj, w_ssm_proj=w_ssm_proj, w_out=w_out)
    m = dict(w_ada=m_w_ada, b_ada=m_b_ada, norm_w=m_norm_w, w_in=m_w_in, q_norm_w=m_q_norm_w, k_norm_w=m_k_norm_w,
             rel_bias=m_rel_bias, sinks=m_sinks, conv_w=m_conv_w, conv_b=m_conv_b, dt_bias=m_dt_bias, a_log=m_a_log,
             d_skip=m_d_skip, ssm_norm_w=m_ssm_norm_w, w_attn_proj=m_w_attn_proj, w_ssm_proj=m_w_ssm_proj, w_out=m_w_out)
    v = dict(w_ada=v_w_ada, b_ada=v_b_ada, norm_w=v_norm_w, w_in=v_w_in, q_norm_w=v_q_norm_w, k_norm_w=v_k_norm_w,
             rel_bias=v_rel_bias, sinks=v_sinks, conv_w=v_conv_w, conv_b=v_conv_b, dt_bias=v_dt_bias, a_log=v_a_log,
             d_skip=v_d_skip, ssm_norm_w=v_ssm_norm_w, w_attn_proj=v_w_attn_proj, w_ssm_proj=v_w_ssm_proj, w_out=v_w_out)
    me = 4 * lax.axis_index("x") + 2 * lax.axis_index("y") + lax.axis_index("c")
    ada_n = w_ada.shape[2]
    in_n = w_in.shape[2]
    cw_n = conv_w.shape[2]

    first = _ag_direct(jnp.concatenate([c, conv_w[0].reshape(1, CONV_K * cw_n)], axis=1), "ag_c")[:, 0]
    c_all = first[:, :D_MODEL]
    conv_w_full = first[:, D_MODEL:].reshape(N_DEV, CONV_K, cw_n).transpose(1, 0, 2).reshape(CONV_K, XBC_W)
    b_piece = lax.dynamic_slice_in_dim(b_ada, me * ada_n, ada_n, axis=1)
    mod_all = _ag_direct(_mod_piece(c_all, w_ada[0], b_piece), "ag_mod")
    mod = lax.dynamic_index_in_dim(mod_all, me, axis=1, keepdims=False).reshape(1, 3 * D_MODEL)
    shift, scale, gate = mod[:, :D_MODEL], mod[:, D_MODEL:2 * D_MODEL], mod[:, 2 * D_MODEL:]

    w_t, zero = _ag_two_level(w_in[0].T.astype(BF), "ag_w_in")
    w_t = w_t.reshape(N_DEV * in_n, D_MODEL)

    def with_mine(blocks, mine):
        return lax.dynamic_update_index_in_dim(lax.empty(blocks, mine.dtype), mine, me, axis=0)

    rows = jnp.concatenate([w_attn_proj[0], w_ssm_proj[0], w_out[0]], axis=0).astype(BF) + zero
    r_ap, r_sp = w_attn_proj.shape[1], w_ssm_proj.shape[1]
    rows_started, zero = _exchange_start(rows, with_mine((N_DEV,) + rows.shape, rows), True, "ag_rows_start")

    def rows_fn(after):
        return _exchange_wait(rows_started, after, True, "ag_rows_wait")

    started = {}

    def send_blocks(key, g, name):
        started[key], zero = _exchange_start(
            g, with_mine(g.shape, lax.dynamic_index_in_dim(g, me, axis=0, keepdims=False)), False, name)
        return zero

    def after_mid(g_wap, g_wsp, g_wout):
        return send_blocks("rows", jnp.concatenate(
            [g_wap.reshape(N_DEV, r_ap, D_MODEL), g_wsp.reshape(N_DEV, r_sp, D_MODEL),
             g_wout.reshape(N_DEV, r_ap, D_MODEL)], axis=1), "rs_rows_start")

    def after_gw(g_ws):
        return send_blocks("in", jnp.concatenate(g_ws, axis=0).reshape(N_DEV, in_n, D_MODEL), "rs_in_start")

    r = _local_step(x[0], loss_target[0], shift, scale + zero, gate, w_t, rows_fn, norm_w, q_norm_w, k_norm_w,
                    rel_bias, sinks, conv_w_full, conv_b, dt_bias, a_log, d_skip, ssm_norm_w, after_mid, after_gw)

    small = dict(b_ada=r["dmod"], norm_w=r["g_norm_w"], q_norm_w=r["g_qnw"], k_norm_w=r["g_knw"], rel_bias=r["g_rel"],
                 sinks=r["g_sinks"], conv_b=r["g_conv_b"], dt_bias=r["g_dt_bias"], a_log=r["g_a_log"],
                 d_skip=r["g_d_skip"], ssm_norm_w=r["g_ssm_nw"])
    pack_all = _ag_direct(_pack_partials(small, r["loss"], r["g_conv_w"]), "ag_small")
    res, loss = _adam_small(pack_all, w, m, v)
    loss = loss[0, 0]
    cw_parts = pack_all[:, 0, _CW_OFF:].reshape(N_DEV, CONV_K, XBC_W)
    cw_mine = lax.dynamic_slice_in_dim(cw_parts, me * cw_n, cw_n, axis=2)
    res["conv_w"] = [a[None] for a in _adam(cw_mine, conv_w[0], m_conv_w[0], v_conv_w[0], "adam_conv_w")]

    dmod_piece = lax.dynamic_slice_in_dim(pack_all[:, 0, :3 * D_MODEL], me * ada_n, ada_n, axis=1)
    g_ada = _gw_ada(c_all, dmod_piece)
    res["w_ada"] = [a[None] for a in _adam(g_ada[None], w_ada[0], m_w_ada[0], v_w_ada[0], "adam_w_ada")]

    cat = lambda d: jnp.concatenate([d["w_attn_proj"][0], d["w_ssm_proj"][0], d["w_out"][0]], axis=0)
    rows_res = _adam(_exchange_wait(started["rows"], g_ada, False, "rs_rows_wait"), cat(w), cat(m), cat(v), "adam_w_rows")
    res["w_in"] = [a.T[None] for a in _adam(_exchange_wait(started["in"], rows_res[0], False, "rs_in_wait"),
                                            w_in[0].T, m_w_in[0].T, v_w_in[0].T, "adam_w_in")]
    res["w_attn_proj"] = [a[None, :r_ap] for a in rows_res]
    res["w_ssm_proj"] = [a[None, r_ap:r_ap + r_sp] for a in rows_res]
    res["w_out"] = [a[None, r_ap + r_sp:] for a in rows_res]

    outs = [loss, r["grad_x"][None]]
    for j in range(4):
        outs += [res[name][j] for name in WEIGHTS]
    return tuple(outs)
```

```python
import math

import numpy as np
import jax
import jax.numpy as jnp
from jax import lax
from jax.experimental import pallas as pl
from jax.experimental.pallas import tpu as pltpu

F32 = jnp.float32
BF = jnp.bfloat16
HI = lax.Precision.HIGHEST

D_MODEL = 1024
ATTN_HEADS = 16
KV_HEADS = 4
GRP = ATTN_HEADS // KV_HEADS
HEAD_DIM = 64
ATTN_W = ATTN_HEADS * HEAD_DIM
KV_W = KV_HEADS * HEAD_DIM
BLOCK = 128
REL_BUCKETS = 32
REL_MAX_DIST = 128
SSM_W = 2048
SSM_P = 64
SSM_HEADS = 32
SSM_G = 4
SSM_R = 8
SSM_N = 128
CONV_K = 4
XBC_W = SSM_W + 2 * SSM_G * SSM_N
SEG_W = (ATTN_W, 2 * KV_W, ATTN_W + SSM_W, XBC_W, SSM_HEADS, 2 * D_MODEL)
NSEG = len(SEG_W)
SEG_OFF = tuple(int(v) for v in np.cumsum((0,) + SEG_W))
IN_W = SEG_OFF[-1]
GATE_SEGS = (2, 5)
EPS = 1e-6
N_DEV = 8
ADAM_LR, ADAM_B1, ADAM_B2, ADAM_EPS, ADAM_WD, ADAM_STEP = 0.001, 0.9, 0.999, 1e-08, 0.01, 10
VMEM_LIMIT = 60 * 1024 * 1024
MESH = pl.DeviceIdType.MESH
ANY = pl.BlockSpec(memory_space=pl.ANY)


def _dot(a, b, precision=None):
    return jnp.dot(a, b, preferred_element_type=F32, precision=precision)


def _dot_nt(a, b, precision=None):
    return lax.dot_general(a, b, (((1,), (1,)), ((), ())), preferred_element_type=F32, precision=precision)


def _dot_tn(a, b, precision=None):
    return lax.dot_general(a, b, (((0,), (0,)), ((), ())), preferred_element_type=F32, precision=precision)


def _bf(a):
    return a.astype(BF)


def _sig(a):
    return 0.5 * jnp.tanh(0.5 * a) + 0.5


def _params(**kw):
    return pltpu.CompilerParams(vmem_limit_bytes=VMEM_LIMIT, **kw)


def _full(shape):
    nd = len(shape)
    return pl.BlockSpec(shape, lambda i: (0,) * nd)


def _rows(tm, w):
    return pl.BlockSpec((tm, w), lambda i: (i, 0))


def _inproj(x, norm_w, scale, shift, w_t, tm=256):
    s = x.shape[0]

    def body(x_ref, nw_ref, sc_ref, sh_ref, w_hbm, *rest):
        outs, h_ref, w_vm, sem = rest[:NSEG], rest[NSEG], rest[NSEG + 1], rest[NSEG + 2]
        first = pl.program_id(0) == 0
        cps = [pltpu.make_async_copy(w_hbm.at[SEG_OFF[j]:SEG_OFF[j + 1], :], w_vm.at[SEG_OFF[j]:SEG_OFF[j + 1], :], sem.at[j])
               for j in range(NSEG)]

        def tile(waiting):
            xv = x_ref[...]
            r = lax.rsqrt(jnp.mean(xv * xv, axis=-1, keepdims=True) + EPS)
            h = xv * r * (nw_ref[...] * (1.0 + sc_ref[...])) + sh_ref[...]
            hb = _bf(h)
            h_ref[...] = hb
            for j in range(NSEG):
                if waiting:
                    cps[j].wait()
                outs[j][...] = _dot_nt(hb, w_vm[SEG_OFF[j]:SEG_OFF[j + 1], :]).astype(outs[j].dtype)

        @pl.when(first)
        def _():
            for cp in cps:
                cp.start()
            tile(True)

        @pl.when(jnp.logical_not(first))
        def _():
            tile(False)

    vec = _full((1, D_MODEL))
    return pl.pallas_call(
        body, name="inproj", grid=(s // tm,),
        in_specs=[_rows(tm, D_MODEL), vec, vec, vec, ANY],
        out_specs=[_rows(tm, w) for w in SEG_W] + [_rows(tm, D_MODEL)],
        out_shape=[jax.ShapeDtypeStruct((s, w), BF if j in GATE_SEGS else F32) for j, w in enumerate(SEG_W)]
                  + [jax.ShapeDtypeStruct((s, D_MODEL), BF)],
        scratch_shapes=[pltpu.VMEM((IN_W, D_MODEL), BF), pltpu.SemaphoreType.DMA((NSEG,))],
        compiler_params=_params(dimension_semantics=("arbitrary",)),
    )(x, norm_w, scale, shift, w_t)


def _bucket_onehot_t():
    qi = jnp.arange(BLOCK)[:, None]
    kj = jnp.arange(2 * BLOCK)[None, :]
    dist = qi + BLOCK - kj
    n = jnp.maximum(dist, 0)
    max_exact = REL_BUCKETS // 2
    nf = jnp.maximum(n, 1).astype(F32)
    large = max_exact + (jnp.log(nf / max_exact) / math.log(REL_MAX_DIST / max_exact)
                         * (REL_BUCKETS - max_exact)).astype(jnp.int32)
    large = jnp.minimum(large, REL_BUCKETS - 1)
    bucket = jnp.where(n < max_exact, n, large).reshape(1, BLOCK * 2 * BLOCK)
    return (bucket == jnp.arange(REL_BUCKETS)[:, None]).astype(F32)


def _bias_dense(rel_bias_t, oh_t):
    def body(rb_ref, oh_ref, o_ref):
        o_ref[...] = _dot(rb_ref[...], oh_ref[...], HI)

    return pl.pallas_call(
        body, name="bias_dense", out_shape=jax.ShapeDtypeStruct((ATTN_HEADS, BLOCK * 2 * BLOCK), F32),
        compiler_params=_params(),
    )(rel_bias_t, oh_t)


def _bias_grad(ds_sum, oh_t):
    def body(ds_ref, oh_ref, o_ref):
        o_ref[...] = _dot_nt(ds_ref[...], oh_ref[...], HI)

    return pl.pallas_call(
        body, name="bias_grad", out_shape=jax.ShapeDtypeStruct((ATTN_HEADS, REL_BUCKETS), F32),
        compiler_params=_params(),
    )(ds_sum, oh_t)


def _group_sum(a, e):
    hi = _bf(a)
    return _dot(hi, e) + _dot(_bf(a - hi.astype(F32)), e)


def _group_bcast(a, e3t):
    hi = _bf(a)
    r1 = a - hi.astype(F32)
    mid = _bf(r1)
    return _dot(jnp.concatenate([hi, mid, _bf(r1 - mid.astype(F32))], axis=1), e3t)


def _membership(width, group, ngroups):
    e = (jnp.arange(width)[:, None] // group == jnp.arange(ngroups)[None, :]).astype(BF)
    return e, jnp.tile(e.T, (3, 1))


def _fold(width, group):
    return (jnp.arange(width)[:, None] % group == jnp.arange(group)[None, :]).astype(BF)


def _heads_norm(t, w_x, e, e3t):
    r = lax.rsqrt(_dot(_bf(t * t), e) * (1.0 / HEAD_DIM) + EPS)
    r_x = _group_bcast(r, e3t)
    return t * r_x * w_x, r_x


def _heads_norm_bwd(t, r_x, w_x, d, e, e3t):
    wd = d * w_x
    corr = _group_bcast(_dot(_bf(t * wd), e) * (1.0 / HEAD_DIM), e3t)
    return r_x * wd - t * (r_x * r_x * r_x) * corr, jnp.sum(d * t * r_x, axis=0, keepdims=True)


def _stack_heads(a, hk):
    return jnp.concatenate([a[:, (hk * GRP + g) * HEAD_DIM:(hk * GRP + g + 1) * HEAD_DIM] for g in range(GRP)], axis=0)


def _stack_cols(a, hk):
    return jnp.concatenate([a[:, hk * GRP + g:hk * GRP + g + 1] for g in range(GRP)], axis=0)


def _masked_bias(bias):
    qi = jnp.arange(BLOCK)[:, None]
    kj = jnp.arange(2 * BLOCK)[None, :]
    cur_ok = jnp.logical_and(kj >= BLOCK, kj - BLOCK <= qi)
    both_ok = jnp.logical_or(jnp.logical_and(kj < BLOCK, kj > qi), cur_ok)
    return jnp.stack([jnp.where(cur_ok, bias, -1e30), jnp.where(both_ok, bias, -1e30)])


def _attn_consts(qnw, knw):
    eq, eq3t = _membership(ATTN_W, HEAD_DIM, ATTN_HEADS)
    ek, ek3t = _membership(KV_W, HEAD_DIM, ATTN_HEADS)
    return (jnp.tile(qnw, (1, ATTN_HEADS)), jnp.tile(knw, (1, KV_HEADS)), eq, eq3t, ek, ek3t)


def _attn_fwd(q, kv, bias, sinks, consts):
    s = q.shape[0]
    nb = s // BLOCK
    gq = GRP * BLOCK
    bias_t = bias.reshape(2, KV_HEADS, GRP, BLOCK, 2 * BLOCK).transpose(0, 1, 4, 2, 3).reshape(2, KV_HEADS, 2 * BLOCK, gq)
    sink_rows = jnp.repeat(sinks.reshape(KV_HEADS, GRP), BLOCK, axis=1).reshape(KV_HEADS, 1, gq)
    eye = jnp.eye(BLOCK, dtype=BF)

    def body(q_ref, kp_ref, kc_ref, vp_ref, vc_ref, b_ref, bt_ref, sk_ref, skr_ref, eye_ref,
             qw_ref, kw_ref, eq_ref, eq3_ref, ek_ref, ek3_ref, o_ref, lse_ref):
        qn = _bf(_heads_norm(q_ref[...], qw_ref[...], eq_ref[...], eq3_ref[...])[0] * (HEAD_DIM ** -0.5))
        kn = _bf(_heads_norm(jnp.concatenate([kp_ref[...], kc_ref[...]], axis=0), kw_ref[...], ek_ref[...], ek3_ref[...])[0])
        vv = _bf(jnp.concatenate([vp_ref[...], vc_ref[...]], axis=0))
        ones = jnp.ones((2 * BLOCK, HEAD_DIM), BF)
        lses = []
        kss = [slice(hk * HEAD_DIM, (hk + 1) * HEAD_DIM) for hk in range(KV_HEADS)]
        qgs = [_stack_heads(qn, hk) for hk in range(KV_HEADS)]
        sc_ts = [_dot_nt(kn[:, kss[hk]], qgs[hk]) + bt_ref[0, hk] for hk in range(KV_HEADS)]
        m_rows = [jnp.maximum(jnp.max(sc_ts[hk], axis=0, keepdims=True), skr_ref[hk]) for hk in range(KV_HEADS)]
        m8s = [_bf(jnp.broadcast_to(m + jnp.abs(m) * (2.0 ** -7), (8, gq))) for m in m_rows]
        ms = [jnp.concatenate([_dot_nt(eye_ref[...], m8[:, g * BLOCK:(g + 1) * BLOCK])[:, 0:1] for g in range(GRP)], axis=0)
              for m8 in m8s]
        scs = [_dot_nt(qgs[hk], kn[:, kss[hk]]) + b_ref[0, hk * GRP:(hk + 1) * GRP].reshape(gq, 2 * BLOCK)
               for hk in range(KV_HEADS)]
        ps = [_bf(jnp.exp(scs[hk] - ms[hk])) for hk in range(KV_HEADS)]
        pvs = [_dot(ps[hk], jnp.concatenate([vv[:, kss[hk]], ones], axis=1)) for hk in range(KV_HEADS)]
        for hk in range(KV_HEADS):
            m, pv = ms[hk], pvs[hk]
            sink = jnp.concatenate([jnp.full((BLOCK, 1), sk_ref[0, hk * GRP + g], F32) for g in range(GRP)], axis=0)
            den = pv[:, HEAD_DIM:HEAD_DIM + 1] + jnp.exp(sink - m)
            out = pv[:, :HEAD_DIM] * (1.0 / den)
            lse = m + jnp.log(den)
            for g in range(GRP):
                h = hk * GRP + g
                o_ref[:, h * HEAD_DIM:(h + 1) * HEAD_DIM] = out[g * BLOCK:(g + 1) * BLOCK]
                lses.append(lse[g * BLOCK:(g + 1) * BLOCK])
        lse_ref[...] = jnp.concatenate(lses, axis=1)

    cur = lambda w, col=0: pl.BlockSpec((BLOCK, w), lambda i: (i, col))
    prev = lambda w, col=0: pl.BlockSpec((BLOCK, w), lambda i: (jnp.maximum(i - 1, 0), col))
    whole = lambda a: pl.BlockSpec(a.shape, lambda i: (0,) * a.ndim)
    first_or_not = lambda a: pl.BlockSpec((1,) + a.shape[1:], lambda i: (jnp.minimum(i, 1),) + (0,) * (a.ndim - 1))
    return pl.pallas_call(
        body, name="attn_fwd", grid=(nb,),
        in_specs=[cur(ATTN_W), prev(KV_W, 0), cur(KV_W, 0), prev(KV_W, 1), cur(KV_W, 1),
                  first_or_not(bias), first_or_not(bias_t),
                  pl.BlockSpec(memory_space=pltpu.SMEM), whole(sink_rows), whole(eye)] + [_full(c.shape) for c in consts],
        out_specs=[cur(ATTN_W), cur(ATTN_HEADS)],
        out_shape=[jax.ShapeDtypeStruct((s, ATTN_W), F32), jax.ShapeDtypeStruct((s, ATTN_HEADS), F32)],
        compiler_params=_params(dimension_semantics=("arbitrary",)),
    )(q, kv, kv, kv, kv, bias, bias_t, sinks, sink_rows, eye, *consts)


def _conv_taps(xbc, tail):
    ext = jnp.concatenate([tail, xbc], axis=0)
    return [pltpu.roll(ext, CONV_K - 1 - j, axis=0)[8:8 + BLOCK] if j < CONV_K - 1 else xbc for j in range(CONV_K)]


def _softplus(u):
    return jnp.maximum(u, 0.0) + jnp.log(1.0 + jnp.exp(-jnp.abs(u)))


def _tril():
    r = lax.broadcasted_iota(jnp.int32, (BLOCK, BLOCK), 0)
    c = lax.broadcasted_iota(jnp.int32, (BLOCK, BLOCK), 1)
    return r >= c


def _triu():
    r = lax.broadcasted_iota(jnp.int32, (BLOCK, BLOCK), 0)
    c = lax.broadcasted_iota(jnp.int32, (BLOCK, BLOCK), 1)
    return r <= c


def _exact_left(m01, a):
    hi = _bf(a)
    r1 = a - hi.astype(F32)
    mid = _bf(r1)
    return _dot(m01, hi) + _dot(m01, mid) + _dot(m01, _bf(r1 - mid.astype(F32)))


def _ssd_common(conv, dtr, dtb_ref, alog_ref, e3_ref):
    sg = _sig(conv)
    xact = conv * sg
    u = dtr + dtb_ref[...]
    dt = _softplus(u)
    a = -jnp.exp(alog_ref[...])
    trilb = _tril()
    acum = _exact_left(trilb.astype(BF), dt * a) * math.log2(math.e)
    both = _group_bcast(jnp.concatenate([dt, acum], axis=0), e3_ref[...])
    dt_x, acum_x = both[:BLOCK], both[BLOCK:]
    return sg, xact, u, dt, a, trilb, acum, dt_x, acum_x


SSD_CH = 4


def _ssd_fwd(xbc, dt_raw, conv_w, conv_b, dt_bias, a_log, dsk_x, e3t):
    s = xbc.shape[0]
    nc = s // BLOCK
    ch = SSD_CH if nc % SSD_CH == 0 else 1
    rows = ch * BLOCK

    def body(x_ref, tail_ref, dtr_ref, cw_ref, cb_ref, dtb_ref, alog_ref, dsk_ref, e3_ref,
             y_ref, hp_ref, conv_ref, hst, yd_s, yoff_s):
        i = pl.program_id(0)

        @pl.when(i == 0)
        def _():
            hst[...] = jnp.zeros_like(hst)

        for j in range(ch):
            rs = slice(j * BLOCK, (j + 1) * BLOCK)
            tail = jnp.where(i > 0, tail_ref[...], 0.0) if j == 0 else x_ref[j * BLOCK - 8:j * BLOCK, :]
            taps = _conv_taps(x_ref[rs, :], tail)
            conv = cb_ref[...] + sum(taps[t] * cw_ref[t:t + 1, :] for t in range(CONV_K))
            conv_ref[rs, :] = conv
            _, xact, _, _, _, trilb, acum, dt_x, acum_x = _ssd_common(conv, dtr_ref[rs, :], dtb_ref, alog_ref, e3_ref)
            xs = xact[:, :SSM_W]
            acum_t = acum.T
            ea_x = jnp.exp2(acum_x)
            last_x = acum_x[BLOCK - 1:BLOCK, :]
            xdt = xs * dt_x
            xw = xdt * jnp.exp2(last_x - acum_x)
            cd_x = jnp.exp2(last_x)
            hprev = hst[...]
            hp_ref[j] = hprev
            sls = [slice(g * SSM_R * SSM_P, (g + 1) * SSM_R * SSM_P) for g in range(SSM_G)]
            bgs = [_bf(xact[:, SSM_W + g * SSM_N:SSM_W + (g + 1) * SSM_N]) for g in range(SSM_G)]
            cgs = [_bf(xact[:, SSM_W + SSM_G * SSM_N + g * SSM_N:SSM_W + SSM_G * SSM_N + (g + 1) * SSM_N])
                   for g in range(SSM_G)]
            xdt_b, xw_b, hprev_b = _bf(xdt), _bf(xw), _bf(hprev)
            low_half = lax.broadcasted_iota(jnp.int32, (BLOCK, 2 * SSM_P), 1) < SSM_P
            cbs = [_dot_nt(cgs[g], bgs[g]) for g in range(SSM_G)]
            for g in range(SSM_G):
                sl = sls[g]
                yoff_s[:, sl] = _dot(cgs[g], hprev_b[:, sl]) * ea_x[:, sl]
                hst[:, sl] = hprev[:, sl] * cd_x[:, sl] + _dot_tn(bgs[g], xw_b[:, sl])
            for g in range(SSM_G):
                hss = [slice((g * SSM_R + r) * SSM_P, (g * SSM_R + r + 1) * SSM_P) for r in range(SSM_R)]
                mms = [_bf(cbs[g] * jnp.exp2(jnp.where(trilb, acum[:, g * SSM_R + r:g * SSM_R + r + 1]
                                                      - acum_t[g * SSM_R + r:g * SSM_R + r + 1, :], -1e30)))
                       for r in range(SSM_R)]
                for r in range(0, SSM_R, 2):
                    pair = slice(hss[r].start, hss[r + 1].stop)
                    xp = xdt_b[:, pair]
                    rhs = jnp.concatenate([jnp.where(low_half, xp, 0), jnp.where(low_half, 0, xp)], axis=0)
                    yd_s[:, pair] = _dot(jnp.concatenate([mms[r], mms[r + 1]], axis=1), rhs)
            y_ref[rs, :] = yd_s[...] + yoff_s[...] + dsk_ref[...] * xs

    blk = lambda w: pl.BlockSpec((rows, w), lambda i: (i, 0))
    return pl.pallas_call(
        body, name="ssd_fwd", grid=(nc // ch,),
        in_specs=[blk(XBC_W), pl.BlockSpec((8, XBC_W), lambda i: (jnp.maximum(i * (rows // 8) - 1, 0), 0)),
                  blk(SSM_HEADS), _full((CONV_K, XBC_W)), _full((1, XBC_W)), _full((1, SSM_HEADS)),
                  _full((1, SSM_HEADS)), _full((1, SSM_W)), _full((3 * SSM_HEADS, SSM_W))],
        out_specs=[blk(SSM_W), pl.BlockSpec((ch, SSM_N, SSM_W), lambda i: (i, 0, 0)), blk(XBC_W)],
        out_shape=[jax.ShapeDtypeStruct((s, SSM_W), F32), jax.ShapeDtypeStruct((nc, SSM_N, SSM_W), F32),
                   jax.ShapeDtypeStruct((s, XBC_W), F32)],
        scratch_shapes=[pltpu.VMEM((SSM_N, SSM_W), F32), pltpu.VMEM((BLOCK, SSM_W), F32), pltpu.VMEM((BLOCK, SSM_W), F32)],
        compiler_params=_params(dimension_semantics=("arbitrary",)),
    )(xbc, xbc, dt_raw, conv_w, conv_b, dt_bias, a_log, dsk_x, e3t)


def _dsilu(z, sg, silu):
    return sg * (1.0 + (z - silu))


def _mid(x, tgt, o_att, zam, ypre, gab, gate, ssm_nw, rows_all, tm=256):
    s = x.shape[0]
    gw = SSM_W // SSM_G

    r_ap, r_sp = ATTN_W // N_DEV, SSM_W // N_DEV

    def body(x_ref, t_ref, o_ref, zam_ref, yp_ref, gab_ref, gate_ref, nw_ref, rows_h,
             dout_ref, do_ref, dzam_ref, dyp_ref, dgab_ref,
             yag_ref, dya_ref, yn_ref, dyb_ref, mg_ref, dob_ref, gnw_ref, dgate_ref, loss_ref,
             wap_v, wsp_v, wout_v, sem):
        i = pl.program_id(0)

        @pl.when(i == 0)
        def _():
            cps = []
            for d in range(N_DEV):
                for j, (dst, r0, rn) in enumerate(((wap_v, 0, r_ap), (wsp_v, r_ap, r_sp), (wout_v, r_ap + r_sp, r_ap))):
                    cps.append(pltpu.make_async_copy(rows_h.at[d, r0:r0 + rn, :], dst.at[d * rn:(d + 1) * rn, :], sem.at[j]))
            for cp in cps:
                cp.start()
            gnw_ref[...] = jnp.zeros_like(gnw_ref)
            dgate_ref[...] = jnp.zeros_like(dgate_ref)
            loss_ref[...] = jnp.zeros_like(loss_ref)
            for cp in cps:
                cp.wait()

        gate = gate_ref[...]
        nw = nw_ref[...]
        o_att = o_ref[...]
        z_a = zam_ref[:, :ATTN_W].astype(F32)
        s_a = _sig(z_a)
        silu_a = z_a * s_a
        yag = _bf(o_att * silu_a)
        yag_ref[...] = yag
        ypre = yp_ref[...]
        z_m = zam_ref[:, ATTN_W:].astype(F32)
        s_m = _sig(z_m)
        silu_m = z_m * s_m
        yg = ypre * silu_m
        rinv = jnp.concatenate(
            [jnp.broadcast_to(lax.rsqrt(jnp.mean(yg[:, g * gw:(g + 1) * gw] ** 2, axis=-1, keepdims=True) + EPS), (tm, gw))
             for g in range(SSM_G)], axis=1)
        ynr = yg * rinv
        yn = _bf(ynr * nw)
        yn_ref[...] = yn
        y_a = _dot(yag, wap_v[...])
        y_b = _dot(yn, wsp_v[...])
        g_a = _sig(gab_ref[:, :D_MODEL].astype(F32))
        g_b = _sig(gab_ref[:, D_MODEL:].astype(F32))
        merged = _bf(g_a * y_a + g_b * y_b)
        mg_ref[...] = merged
        o = _dot(merged, wout_v[...])
        diff = x_ref[...] + gate * o - t_ref[...]
        loss_ref[...] += (0.5 / D_MODEL) * jnp.sum(diff * diff, axis=(0, 1), keepdims=True)
        dout = diff * (1.0 / D_MODEL)
        dout_ref[...] = dout
        dgate_ref[...] += jnp.sum(dout * o, axis=0, keepdims=True)
        d_o = _bf(dout * gate)
        dob_ref[...] = d_o
        dmerged = _dot_nt(d_o, wout_v[...])
        dy_af = dmerged * g_a
        dy_bf = dmerged * g_b
        dy_a = _bf(dy_af)
        dy_b = _bf(dy_bf)
        dya_ref[...] = dy_a
        dyb_ref[...] = dy_b
        dyag = _dot_nt(dy_a, wap_v[...])
        dyn = _dot_nt(dy_b, wsp_v[...])
        dgab_ref[:, :D_MODEL] = _bf(dy_af * y_a * (1.0 - g_a))
        dgab_ref[:, D_MODEL:] = _bf(dy_bf * y_b * (1.0 - g_b))
        do_ref[...] = dyag * silu_a
        dzam_ref[:, :ATTN_W] = _bf(dyag * o_att * _dsilu(z_a, s_a, silu_a))
        gnw_ref[...] += jnp.sum(dyn * ynr, axis=0, keepdims=True)
        dynw = dyn * nw
        corr = jnp.concatenate(
            [jnp.broadcast_to(jnp.mean((dynw * ynr)[:, g * gw:(g + 1) * gw], axis=-1, keepdims=True), (tm, gw))
             for g in range(SSM_G)], axis=1)
        dyg = rinv * (dynw - ynr * corr)
        dyp_ref[...] = dyg * silu_m
        dzam_ref[:, ATTN_W:] = _bf(dyg * ypre * _dsilu(z_m, s_m, silu_m))

    r1, r2, r3 = _rows(tm, D_MODEL), _rows(tm, SSM_W), _rows(tm, ATTN_W + SSM_W)
    sd = jax.ShapeDtypeStruct
    return pl.pallas_call(
        body, name="mid", grid=(s // tm,),
        in_specs=[r1, r1, r1, r3, r2, r2, _full((1, D_MODEL)), _full((1, SSM_W)), ANY],
        out_specs=[r1, r1, r3, r2, r2, r1, r1, r2, r1, r1, r1,
                   _full((1, SSM_W)), _full((1, D_MODEL)), _full((1, 1))],
        out_shape=[sd((s, D_MODEL), F32), sd((s, ATTN_W), F32), sd((s, ATTN_W + SSM_W), BF), sd((s, SSM_W), F32),
                   sd((s, 2 * D_MODEL), BF),
                   sd((s, ATTN_W), BF), sd((s, D_MODEL), BF), sd((s, SSM_W), BF), sd((s, D_MODEL), BF),
                   sd((s, D_MODEL), BF), sd((s, D_MODEL), BF),
                   sd((1, SSM_W), F32), sd((1, D_MODEL), F32), sd((1, 1), F32)],
        scratch_shapes=[pltpu.VMEM((ATTN_W, D_MODEL), BF), pltpu.VMEM((SSM_W, D_MODEL), BF), pltpu.VMEM((D_MODEL, D_MODEL), BF),
                        pltpu.SemaphoreType.DMA((3,))],
        compiler_params=_params(dimension_semantics=("arbitrary",)),
    )(x, tgt, o_att, zam, ypre, gab, gate, ssm_nw, rows_all)


def _attn_bwd(q, kv, bias, sinks, consts, o_att, lse, d_o):
    s = q.shape[0]
    nb = s // BLOCK
    folds = (_fold(ATTN_W, HEAD_DIM), _fold(KV_W, HEAD_DIM))

    def body(q_ref, kp_ref, kc_ref, vp_ref, vc_ref, b_ref, skv_ref, qw_ref, kw_ref, eq_ref, eq3_ref, ek_ref, ek3_ref,
             fq_ref, fk_ref, o_ref, lse_ref, do_ref,
             dq_ref, dkv_ref, dss_ref, gqw_ref, gkw_ref, gsk_ref, ckn, cv, dqn_s, dkn_s, dv_s, gq_x, gk_x):
        i = pl.program_id(0)
        kw, ek, ek3 = kw_ref[...], ek_ref[...], ek3_ref[...]

        @pl.when(i == 0)
        def _():
            for ref in (ckn, cv, dss_ref, gq_x, gk_x, gsk_ref):
                ref[...] = jnp.zeros_like(ref)

        @pl.when(i < nb)
        def _():
            qw, eq, eq3 = qw_ref[...], eq_ref[...], eq3_ref[...]
            qf = q_ref[...]
            qnf, rq_x = _heads_norm(qf, qw, eq, eq3)
            qn = _bf(qnf * (HEAD_DIM ** -0.5))
            kf = jnp.concatenate([kp_ref[...], kc_ref[...]], axis=0)
            knf, rk_x = _heads_norm(kf, kw, ek, ek3)
            kn = _bf(knf)
            vv = _bf(jnp.concatenate([vp_ref[...], vc_ref[...]], axis=0))
            d_of = do_ref[...]
            d_ob = _bf(d_of)
            lse_all = lse_ref[...]
            delta = _dot(_bf(d_of * o_ref[...]), eq)
            gsk_ref[...] += jnp.sum(-jnp.exp(skv_ref[...] - lse_all) * delta, axis=0, keepdims=True)
            kss = [slice(hk * HEAD_DIM, (hk + 1) * HEAD_DIM) for hk in range(KV_HEADS)]
            qgs = [_stack_heads(qn, hk) for hk in range(KV_HEADS)]
            d_ogs = [_stack_heads(d_ob, hk) for hk in range(KV_HEADS)]
            scs = [_dot_nt(qgs[hk], kn[:, kss[hk]]) + b_ref[0, hk * GRP:(hk + 1) * GRP].reshape(GRP * BLOCK, 2 * BLOCK)
                   for hk in range(KV_HEADS)]
            dps = [_dot_nt(d_ogs[hk], vv[:, kss[hk]]) for hk in range(KV_HEADS)]
            ps = [jnp.exp(scs[hk] - _stack_cols(lse_all, hk)) for hk in range(KV_HEADS)]
            dss = [ps[hk] * (dps[hk] - _stack_cols(delta, hk)) for hk in range(KV_HEADS)]
            pbs = [_bf(p) for p in ps]
            dsbs = [_bf(ds) for ds in dss]
            for hk in range(KV_HEADS):
                dss_ref[hk * GRP:(hk + 1) * GRP] += dss[hk].reshape(GRP, BLOCK, 2 * BLOCK)
            for hk in range(KV_HEADS):
                dv_s[:, kss[hk]] = _dot_tn(pbs[hk], d_ogs[hk])
                dkn_s[:, kss[hk]] = _dot_tn(dsbs[hk], qgs[hk])
            dqns = [_dot(dsbs[hk], kn[:, kss[hk]]) * (HEAD_DIM ** -0.5) for hk in range(KV_HEADS)]
            for hk in range(KV_HEADS):
                for g in range(GRP):
                    h = hk * GRP + g
                    dqn_s[:, h * HEAD_DIM:(h + 1) * HEAD_DIM] = dqns[hk][g * BLOCK:(g + 1) * BLOCK]
            dq, gq = _heads_norm_bwd(qf, rq_x, qw, dqn_s[...], eq, eq3)
            dq_ref[...] = _bf(dq)
            gq_x[...] += gq
            dk, gk = _heads_norm_bwd(kf[:BLOCK], rk_x[:BLOCK], kw, ckn[...] + dkn_s[0:BLOCK, :], ek, ek3)
            dkv_ref[:, :KV_W] = _bf(dk)
            gk_x[...] += gk
            dkv_ref[:, KV_W:] = _bf(cv[...] + dv_s[0:BLOCK, :])
            ckn[...] = dkn_s[BLOCK:2 * BLOCK, :]
            cv[...] = dv_s[BLOCK:2 * BLOCK, :]

        @pl.when(i == nb)
        def _():
            kc = kc_ref[...]
            dk, gk = _heads_norm_bwd(kc, _heads_norm(kc, kw, ek, ek3)[1], kw, ckn[...], ek, ek3)
            dkv_ref[:, :KV_W] = _bf(dk)
            dkv_ref[:, KV_W:] = _bf(cv[...])
            gqw_ref[...] = _group_sum(jnp.broadcast_to(gq_x[...], (8, ATTN_W)), fq_ref[...])[0:1]
            gkw_ref[...] = _group_sum(jnp.broadcast_to(gk_x[...] + gk, (8, KV_W)), fk_ref[...])[0:1]

    last = nb - 1
    cur = lambda w, col=0: pl.BlockSpec((BLOCK, w), lambda i: (jnp.minimum(i, last), col))
    prev = lambda w, col=0: pl.BlockSpec((BLOCK, w), lambda i: (jnp.maximum(jnp.minimum(i, last) - 1, 0), col))
    late = lambda w: pl.BlockSpec((BLOCK, w), lambda i: (jnp.maximum(i - 1, 0), 0))
    sd = jax.ShapeDtypeStruct
    return pl.pallas_call(
        body, name="attn_bwd", grid=(nb + 1,),
        in_specs=[cur(ATTN_W), prev(KV_W, 0), cur(KV_W, 0), prev(KV_W, 1), cur(KV_W, 1),
                  pl.BlockSpec((1, ATTN_HEADS, BLOCK, 2 * BLOCK), lambda i: (jnp.minimum(i, 1), 0, 0, 0)),
                  _full((1, ATTN_HEADS))]
                 + [_full(c.shape) for c in consts + folds] + [cur(ATTN_W), cur(ATTN_HEADS), cur(ATTN_W)],
        out_specs=[cur(ATTN_W), late(2 * KV_W),
                   pl.BlockSpec((ATTN_HEADS, BLOCK, 2 * BLOCK), lambda i: (0, 0, 0)),
                   _full((1, HEAD_DIM)), _full((1, HEAD_DIM)), _full((1, ATTN_HEADS))],
        out_shape=[sd((s, ATTN_W), BF), sd((s, 2 * KV_W), BF),
                   sd((ATTN_HEADS, BLOCK, 2 * BLOCK), F32), sd((1, HEAD_DIM), F32), sd((1, HEAD_DIM), F32),
                   sd((1, ATTN_HEADS), F32)],
        scratch_shapes=[pltpu.VMEM((BLOCK, KV_W), F32), pltpu.VMEM((BLOCK, KV_W), F32),
                        pltpu.VMEM((BLOCK, ATTN_W), F32), pltpu.VMEM((2 * BLOCK, KV_W), F32),
                        pltpu.VMEM((2 * BLOCK, KV_W), F32), pltpu.VMEM((1, ATTN_W), F32), pltpu.VMEM((1, KV_W), F32)],
        compiler_params=_params(dimension_semantics=("arbitrary",)),
    )(q, kv, kv, kv, kv, bias, sinks, *consts, *folds, o_att, lse, d_o)


def _ssd_bwd(xbc, conv_all, dt_raw, conv_w, dt_bias, a_log, dsk_x, e_mat, e3t, hprev_all, dy_all):
    s = xbc.shape[0]
    nc = s // BLOCK
    ch = 1
    rows = ch * BLOCK
    nsteps = nc // ch
    gw = SSM_R * SSM_P
    b0, c0 = SSM_W, SSM_W + SSM_G * SSM_N

    def body(x_ref, conv_ref, dtr_ref, cw_ref, dtb_ref, alog_ref, dsk_ref, e_ref, e3_ref, hp_ref, dy_ref,
             dx_ref, ddt_ref, gcw_ref, gcb_ref, gdtb_ref, galog_ref, gdsk_ref,
             dh, nhead, gdskx, dxdt_s, dbc_s, dxd_s):
        def chunk_bwd(j):
            rs = slice(j * BLOCK, (j + 1) * BLOCK)
            conv = conv_ref[rs, :]
            sg, xact, u, dt, a, trilb, acum, dt_x, acum_x = _ssd_common(conv, dtr_ref[rs, :], dtb_ref, alog_ref, e3_ref)
            xs = xact[:, :SSM_W]
            acum_t = acum.T
            ea_x = jnp.exp2(acum_x)
            last_x = acum_x[BLOCK - 1:BLOCK, :]
            dte_x = jnp.exp2(last_x - acum_x)
            cd_x = jnp.exp2(last_x)
            xdt = xs * dt_x
            xw = xdt * dte_x
            hprev = hp_ref[j]
            dhn = dh[...]
            dy = dy_ref[rs, :]
            gdskx[...] += jnp.sum(dy * xs, axis=0, keepdims=True)
            dyea = dy * ea_x
            lane = lax.broadcasted_iota(jnp.int32, (BLOCK, SSM_HEADS), 1)
            dacum = jnp.zeros((BLOCK, SSM_HEADS), F32)
            dacc_x, dlast_x = [], []
            sls = [slice(g * gw, (g + 1) * gw) for g in range(SSM_G)]
            bgs = [_bf(xact[:, b0 + g * SSM_N:b0 + (g + 1) * SSM_N]) for g in range(SSM_G)]
            cgs = [_bf(xact[:, c0 + g * SSM_N:c0 + (g + 1) * SSM_N]) for g in range(SSM_G)]
            hpgs = [_bf(hprev[:, sl]) for sl in sls]
            dhgs = [_bf(dhn[:, sl]) for sl in sls]
            dyeags = [_bf(dyea[:, sl]) for sl in sls]
            xwgs = [_bf(xw[:, sl]) for sl in sls]
            xdt_b, dy_b = _bf(xdt), _bf(dy)
            low_half = lax.broadcasted_iota(jnp.int32, (BLOCK, 2 * SSM_P), 1) < SSM_P
            cbs = [_dot_nt(cgs[g], bgs[g]) for g in range(SSM_G)]
            gmats = [_dot(cgs[g], hpgs[g]) for g in range(SSM_G)]
            dxws = [_dot(bgs[g], dhgs[g]) for g in range(SSM_G)]
            dcgs = [_dot_nt(dyeags[g], hpgs[g]) for g in range(SSM_G)]
            dbgs = [_dot_nt(xwgs[g], dhgs[g]) for g in range(SSM_G)]
            for g in range(SSM_G):
                sl = sls[g]
                dh[:, sl] = dhn[:, sl] * cd_x[:, sl] + _dot_tn(cgs[g], dyeags[g])
                dxdt_s[:, sl] = dxws[g] * dte_x[:, sl]
                dacc_x.append(dy[:, sl] * gmats[g] * ea_x[:, sl] - dxws[g] * xw[:, sl])
                dlast_x.append(jnp.sum(dxws[g] * xw[:, sl], axis=0, keepdims=True)
                               + jnp.sum(dhn[:, sl] * hprev[:, sl], axis=0, keepdims=True) * cd_x[:, sl])
            for g in range(SSM_G):
                bg, cg, cb, dbg, dcg = bgs[g], cgs[g], cbs[g], dbgs[g], dcgs[g]
                hss = [slice((g * SSM_R + r) * SSM_P, (g * SSM_R + r + 1) * SSM_P) for r in range(SSM_R)]
                lms = [jnp.exp2(jnp.where(trilb, acum[:, g * SSM_R + r:g * SSM_R + r + 1]
                                         - acum_t[g * SSM_R + r:g * SSM_R + r + 1, :], -1e30)) for r in range(SSM_R)]
                mms = [cb * lm for lm in lms]
                mmbs = [_bf(mm) for mm in mms]
                dms = []
                for r in range(0, SSM_R, 2):
                    pair = slice(hss[r].start, hss[r + 1].stop)
                    xp, dyp = xdt_b[:, pair], dy_b[:, pair]
                    dmp = _dot_nt(dyp, jnp.concatenate([jnp.where(low_half, xp, 0), jnp.where(low_half, 0, xp)], axis=0))
                    dms += [dmp[:, :BLOCK], dmp[:, BLOCK:]]
                    dxd_s[:, pair] = _dot_tn(jnp.concatenate([mmbs[r], mmbs[r + 1]], axis=0),
                                             jnp.concatenate([jnp.where(low_half, dyp, 0), jnp.where(low_half, 0, dyp)], axis=0))
                dcb = sum(dms[r] * lms[r] for r in range(SSM_R))
                wms = [dms[r] * mms[r] for r in range(SSM_R)]
                antis = [_bf(wm - wm.T) for wm in wms]
                for r in range(SSM_R):
                    dacum = dacum + _dot(antis[r], (lane == g * SSM_R + r).astype(BF))
                dcbb = _bf(dcb)
                dbc_s[:, g * SSM_N:(g + 1) * SSM_N] = dbg + _dot_tn(dcbb, cg)
                dbc_s[:, SSM_G * SSM_N + g * SSM_N:SSM_G * SSM_N + (g + 1) * SSM_N] = dcg + _dot(dcbb, bg)
            dxdt = dxdt_s[...] + dxd_s[...]
            dxs = dy * dsk_ref[...] + dxdt * dt_x
            red = _group_sum(jnp.concatenate(
                [dxdt * xs, jnp.concatenate(dacc_x, axis=1),
                 jnp.broadcast_to(jnp.concatenate(dlast_x, axis=1), (8, SSM_W))], axis=0), e_ref[...])
            row = lax.broadcasted_iota(jnp.int32, (BLOCK, SSM_HEADS), 0)
            dacum = dacum + red[BLOCK:2 * BLOCK] + jnp.where(row == BLOCK - 1, red[2 * BLOCK:2 * BLOCK + 1], 0.0)
            ddta = _exact_left(_triu().astype(BF), dacum)
            ddt = red[:BLOCK] + ddta * a
            galog_ref[...] += jnp.sum(ddta * dt, axis=0, keepdims=True) * a
            du = ddt * _sig(u)
            ddt_ref[rs, :] = _bf(du)
            gdtb_ref[...] += jnp.sum(du, axis=0, keepdims=True)
            dconv = jnp.concatenate([dxs, dbc_s[...]], axis=1) * _dsilu(conv, sg, xact)
            gcb_ref[...] += jnp.sum(dconv, axis=0, keepdims=True)
            ext2 = jnp.concatenate([dconv, nhead[...]], axis=0)
            ahead = [pltpu.roll(ext2, BLOCK + 8 - (CONV_K - 1 - j), axis=0)[0:BLOCK] if j < CONV_K - 1 else dconv
                     for j in range(CONV_K)]
            dx_ref[rs, :] = _bf(sum(ahead[j] * cw_ref[j:j + 1, :] for j in range(CONV_K)))
            xraw = x_ref[rs, :]
            gcw_ref[...] += jnp.concatenate([jnp.sum(ahead[j] * xraw, axis=0, keepdims=True) for j in range(CONV_K)], axis=0)
            nhead[...] = dconv[0:8]

        i = pl.program_id(0)

        @pl.when(i == 0)
        def _():
            for ref in (dh, nhead, gdskx, gcw_ref, gcb_ref, gdtb_ref, galog_ref, gdsk_ref):
                ref[...] = jnp.zeros_like(ref)

        for j in reversed(range(ch)):
            chunk_bwd(j)

        @pl.when(i == nsteps - 1)
        def _():
            gdsk_ref[...] = _group_sum(jnp.broadcast_to(gdskx[...], (8, SSM_W)), e_ref[...])[0:1]

    chunk = lambda w: pl.BlockSpec((rows, w), lambda i: (nsteps - 1 - i, 0))
    sd = jax.ShapeDtypeStruct
    return pl.pallas_call(
        body, name="ssd_bwd", grid=(nsteps,),
        in_specs=[chunk(XBC_W), chunk(XBC_W),
                  chunk(SSM_HEADS), _full((CONV_K, XBC_W)), _full((1, SSM_HEADS)),
                  _full((1, SSM_HEADS)), _full((1, SSM_W)), _full((SSM_W, SSM_HEADS)), _full((3 * SSM_HEADS, SSM_W)),
                  pl.BlockSpec((ch, SSM_N, SSM_W), lambda i: (nsteps - 1 - i, 0, 0)), chunk(SSM_W)],
        out_specs=[chunk(XBC_W), chunk(SSM_HEADS), _full((CONV_K, XBC_W)), _full((1, XBC_W)),
                   _full((1, SSM_HEADS)), _full((1, SSM_HEADS)), _full((1, SSM_HEADS))],
        out_shape=[sd((s, XBC_W), BF), sd((s, SSM_HEADS), BF), sd((CONV_K, XBC_W), F32), sd((1, XBC_W), F32),
                   sd((1, SSM_HEADS), F32), sd((1, SSM_HEADS), F32), sd((1, SSM_HEADS), F32)],
        scratch_shapes=[pltpu.VMEM((SSM_N, SSM_W), F32), pltpu.VMEM((8, XBC_W), F32),
                        pltpu.VMEM((1, SSM_W), F32), pltpu.VMEM((BLOCK, SSM_W), F32),
                        pltpu.VMEM((BLOCK, 2 * SSM_G * SSM_N), F32), pltpu.VMEM((BLOCK, SSM_W), F32)],
        compiler_params=_params(dimension_semantics=("arbitrary",)),
    )(xbc, conv_all, dt_raw, conv_w, dt_bias, a_log, dsk_x, e_mat, e3t, hprev_all, dy_all)


def _dh(x, dout, norm_w, scale, dsegs, w_t, tm=256):
    s = x.shape[0]

    def body(x_ref, dout_ref, nw_ref, sc_ref, *rest):
        d_refs, w_hbm = rest[:NSEG], rest[NSEG]
        gx_ref, dshift_ref, dscale_ref, gnw_ref = rest[NSEG + 1:NSEG + 5]
        w_vm, sem = rest[NSEG + 5], rest[NSEG + 6]
        first = pl.program_id(0) == 0
        cps = [pltpu.make_async_copy(w_hbm.at[SEG_OFF[j]:SEG_OFF[j + 1], :], w_vm.at[SEG_OFF[j]:SEG_OFF[j + 1], :], sem.at[j])
               for j in range(NSEG)]

        def tile(waiting):
            dh = None
            for j in range(NSEG):
                if waiting:
                    cps[j].wait()
                part = _dot(d_refs[j][...], w_vm[SEG_OFF[j]:SEG_OFF[j + 1], :])
                dh = part if dh is None else dh + part
            xv = x_ref[...]
            r = lax.rsqrt(jnp.mean(xv * xv, axis=-1, keepdims=True) + EPS)
            xn = xv * r
            nw = nw_ref[...]
            sc1 = 1.0 + sc_ref[...]
            dshift_ref[...] += jnp.sum(dh, axis=0, keepdims=True)
            dhxn = jnp.sum(dh * xn, axis=0, keepdims=True)
            dscale_ref[...] += dhxn * nw
            gnw_ref[...] += dhxn * sc1
            dxn = dh * (nw * sc1)
            gx_ref[...] = dout_ref[...] + r * (dxn - xn * jnp.mean(xn * dxn, axis=-1, keepdims=True))

        @pl.when(first)
        def _():
            for cp in cps:
                cp.start()
            for ref in (dshift_ref, dscale_ref, gnw_ref):
                ref[...] = jnp.zeros_like(ref)
            tile(True)

        @pl.when(jnp.logical_not(first))
        def _():
            tile(False)

    vec = _full((1, D_MODEL))
    sd = jax.ShapeDtypeStruct
    return pl.pallas_call(
        body, name="dh", grid=(s // tm,),
        in_specs=[_rows(tm, D_MODEL), _rows(tm, D_MODEL), vec, vec] + [_rows(tm, w) for w in SEG_W] + [ANY],
        out_specs=[_rows(tm, D_MODEL), vec, vec, vec],
        out_shape=[sd((s, D_MODEL), F32), sd((1, D_MODEL), F32), sd((1, D_MODEL), F32), sd((1, D_MODEL), F32)],
        scratch_shapes=[pltpu.VMEM((IN_W, D_MODEL), BF), pltpu.SemaphoreType.DMA((NSEG,))],
        compiler_params=_params(dimension_semantics=("arbitrary",)),
    )(x, dout, norm_w, scale, *dsegs, w_t)


def _gw_seg(h, dseg, name, tm=1024):
    s, w = dseg.shape
    tn = min(w, 1024)
    tm = min(tm, s)
    nm = s // tm

    def body(h_ref, d_ref, o_ref, acc):
        m = pl.program_id(1)

        @pl.when(m == 0)
        def _():
            acc[...] = jnp.zeros_like(acc)

        acc[...] += _dot_tn(d_ref[...], h_ref[...])

        @pl.when(m == nm - 1)
        def _():
            o_ref[...] = _bf(acc[...])

    return pl.pallas_call(
        body, name=name, grid=(w // tn, nm),
        in_specs=[pl.BlockSpec((tm, D_MODEL), lambda n, m: (m, 0)), pl.BlockSpec((tm, tn), lambda n, m: (m, n))],
        out_specs=pl.BlockSpec((tn, D_MODEL), lambda n, m: (n, 0)),
        out_shape=jax.ShapeDtypeStruct((w, D_MODEL), BF),
        scratch_shapes=[pltpu.VMEM((tn, D_MODEL), F32)],
        compiler_params=_params(dimension_semantics=("arbitrary", "arbitrary")),
    )(h, dseg)


def _gw_in(h, dsegs):
    return [_gw_seg(h, d, "gw_in_%d" % j) for j, d in enumerate(dsegs)]


def _local_step(x, tgt, shift, scale, gate, w_t, rows_fn, norm_w, qnw, knw, rel_bias, sinks,
                conv_w, conv_b, dt_bias, a_log, d_skip, ssm_nw, after_mid=None, after_gw=None):
    oh_t = _bucket_onehot_t()
    bias = _masked_bias(_bias_dense(rel_bias.T, oh_t).reshape(ATTN_HEADS, BLOCK, 2 * BLOCK))
    *segs, h = _inproj(x, norm_w, scale, shift, w_t)
    q, kv, zam, xbc, dtr, gab = segs
    consts = _attn_consts(qnw, knw)
    o_att, lse = _attn_fwd(q, kv, bias, sinks, consts)
    e_mat, e3t = _membership(SSM_W, SSM_P, SSM_HEADS)
    dsk_x = jnp.repeat(d_skip, SSM_P, axis=1)
    ypre, hprev, conv = _ssd_fwd(xbc, dtr, conv_w, conv_b, dt_bias, a_log, dsk_x, e3t)
    (dout, d_o, dzam, dyp, dgab, yag, dy_a, yn, dy_b, merged, dob, g_ssm_nw, dgate, loss) = _mid(
        x, tgt, o_att, zam, ypre, gab, gate, ssm_nw, rows_fn(ypre))
    g_wap = _gw_seg(dy_a, yag, "gw_attn_proj")
    g_wsp = _gw_seg(dy_b, yn, "gw_ssm_proj")
    g_wout = _gw_seg(dob, merged, "gw_out")
    zero = after_mid(g_wap, g_wsp, g_wout) if after_mid is not None else 0.0
    dq, dkv, dss, g_qnw, g_knw, g_sinks = _attn_bwd(q, kv, bias, sinks + zero, consts, o_att, lse, d_o)
    g_rel = _bias_grad(dss.reshape(ATTN_HEADS, BLOCK * 2 * BLOCK), oh_t).T
    dxbc, ddt, g_cw, g_cb, g_dtb, g_alog, g_dsk = _ssd_bwd(
        xbc, conv, dtr, conv_w, dt_bias, a_log, dsk_x, e_mat, e3t, hprev, dyp)
    dsegs = (dq, dkv, dzam, dxbc, ddt, dgab)
    g_ws = _gw_in(h, dsegs)
    zero = after_gw(g_ws) if after_gw is not None else 0.0
    gx, dshift, dscale, g_nw = _dh(x, dout, norm_w + zero, scale, dsegs, w_t)
    return dict(loss=loss, grad_x=gx, dmod=jnp.concatenate([dshift, dscale, dgate], axis=1), g_ws=g_ws,
                g_wap=g_wap, g_wsp=g_wsp, g_wout=g_wout, g_norm_w=g_nw, g_qnw=g_qnw, g_knw=g_knw, g_rel=g_rel,
                g_sinks=g_sinks, g_conv_w=g_cw, g_conv_b=g_cb, g_dt_bias=g_dtb, g_a_log=g_alog, g_d_skip=g_dsk,
                g_ssm_nw=g_ssm_nw)


def _me():
    return lax.axis_index("x"), lax.axis_index("y"), lax.axis_index("c")


def _flip(v, bit):
    return 1 - v if bit else v


def _ag_direct(v, name):
    def body(v_ref, out_ref, send_sems, recv_sems, local_sem):
        x, y, c = _me()
        me = 4 * x + 2 * y + c
        mine = pltpu.make_async_copy(v_ref, out_ref.at[me], local_sem)
        mine.start()
        peers = [(_flip(x, k >> 2 & 1), _flip(y, k >> 1 & 1), _flip(c, k & 1)) for k in range(1, N_DEV)]
        sends = [pltpu.make_async_remote_copy(
            src_ref=v_ref, dst_ref=out_ref.at[me], send_sem=send_sems.at[j], recv_sem=recv_sems.at[j],
            device_id=p, device_id_type=MESH) for j, p in enumerate(peers)]
        for cp in sends:
            cp.start()
        for j, (px, py, pc) in enumerate(peers):
            pltpu.make_async_remote_copy(
                src_ref=v_ref, dst_ref=out_ref.at[4 * px + 2 * py + pc], send_sem=send_sems.at[j],
                recv_sem=recv_sems.at[j], device_id=(px, py, pc), device_id_type=MESH).wait_recv()
        for cp in sends:
            cp.wait_send()
        mine.wait()

    vm = pl.BlockSpec(memory_space=pltpu.VMEM)
    return pl.pallas_call(
        body, name=name, out_shape=jax.ShapeDtypeStruct((N_DEV,) + v.shape, v.dtype),
        in_specs=[vm], out_specs=vm,
        scratch_shapes=[pltpu.SemaphoreType.DMA((N_DEV - 1,)), pltpu.SemaphoreType.DMA((N_DEV - 1,)),
                        pltpu.SemaphoreType.DMA],
        compiler_params=_params(),
    )(v)


def _ag_two_level(v, name):
    def body(v_ref, out_ref, token, send_sems, recv_sems, local_sem):
        token[...] = jnp.zeros_like(token)
        x, y, c = _me()
        me, sibling = (x, y, c), (x, y, 1 - c)
        chips = [(1 - x, y), (x, 1 - y), (1 - x, 1 - y)]

        def slot(px, py, pc):
            return out_ref.at[4 * px + 2 * py + pc]

        def copy(k, block, to, src=None):
            return pltpu.make_async_remote_copy(
                src_ref=slot(*block) if src is None else src, dst_ref=slot(*block),
                send_sem=send_sems.at[k], recv_sem=recv_sems.at[k], device_id=to, device_id_type=MESH)

        mine = pltpu.make_async_copy(v_ref, slot(*me), local_sem)
        mine.start()
        first = [copy(0, me, sibling, src=v_ref)]
        first += [copy(1 + j, me, (*chip, c), src=v_ref) for j, chip in enumerate(chips)]
        for cp in first:
            cp.start()
        passed = [copy(4 + j, (*chip, c), sibling) for j, chip in enumerate(chips)]
        for j, chip in enumerate(chips):
            copy(1 + j, (*chip, c), me).wait_recv()
            passed[j].start()
        copy(0, sibling, me).wait_recv()
        for j, chip in enumerate(chips):
            copy(4 + j, (*chip, 1 - c), me).wait_recv()
        for cp in first + passed:
            cp.wait_send()
        mine.wait()

    out, token = pl.pallas_call(
        body, name=name,
        out_shape=(jax.ShapeDtypeStruct((N_DEV,) + v.shape, v.dtype), jax.ShapeDtypeStruct((8, 128), v.dtype)),
        in_specs=[ANY], out_specs=(ANY, pl.BlockSpec(memory_space=pltpu.VMEM)),
        scratch_shapes=[pltpu.SemaphoreType.DMA((7,)), pltpu.SemaphoreType.DMA((7,)), pltpu.SemaphoreType.DMA],
        compiler_params=_params(),
    )(v)
    return out, token[0:1, 0:1]


HBM = pl.BlockSpec(memory_space=pltpu.HBM)
SEM = pl.BlockSpec(memory_space=pltpu.SEMAPHORE)
EFFECT = pltpu.SideEffectType.DATAFLOW_SIDE_EFFECTING


def _peers(x, y, c):
    return [(_flip(x, k >> 2 & 1), _flip(y, k >> 1 & 1), _flip(c, k & 1)) for k in range(1, N_DEV)]


def _exchange_start(src, land, gather, name):
    def body(src_ref, land_ref, send_sems, recv_sems, src_thru, land_thru, token):
        x, y, c = _me()
        me = 4 * x + 2 * y + c
        for j, (px, py, pc) in enumerate(_peers(x, y, c)):
            pltpu.make_async_remote_copy(
                src_ref=src_ref if gather else src_ref.at[4 * px + 2 * py + pc], dst_ref=land_ref.at[me],
                send_sem=send_sems.at[j], recv_sem=recv_sems.at[j], device_id=(px, py, pc), device_id_type=MESH).start()
        token[...] = jnp.zeros_like(token)

    sems = pltpu.SemaphoreType.DMA((N_DEV - 1,))
    out = pl.pallas_call(
        body, name=name,
        out_shape=(sems, sems, pltpu.HBM(src.shape, src.dtype), pltpu.HBM(land.shape, land.dtype),
                   jax.ShapeDtypeStruct((8, 128), F32)),
        in_specs=(HBM, HBM), out_specs=(SEM, SEM, HBM, HBM, pl.BlockSpec(memory_space=pltpu.VMEM)),
        input_output_aliases={0: 2, 1: 3},
        compiler_params=pltpu.CompilerParams(has_side_effects=EFFECT),
    )(pltpu.with_memory_space_constraint(src, pltpu.HBM), pltpu.with_memory_space_constraint(land, pltpu.HBM))
    return out[:4], out[4][0, 0]


def _exchange_wait(started, after, gather, name):
    send_sems, recv_sems, src_thru, land_thru = started

    def body(src_ref, land_ref, send_sems, recv_sems, after_ref, src_dead, got_ref):
        x, y, c = _me()
        for j, (px, py, pc) in enumerate(_peers(x, y, c)):
            pid = 4 * px + 2 * py + pc
            cp = pltpu.make_async_remote_copy(
                src_ref=src_ref if gather else src_ref.at[pid], dst_ref=land_ref.at[pid],
                send_sem=send_sems.at[j], recv_sem=recv_sems.at[j], device_id=(px, py, pc), device_id_type=MESH)
            cp.wait_send()
            cp.wait_recv()

    return pl.pallas_call(
        body, name=name,
        out_shape=(pltpu.HBM(src_thru.shape, src_thru.dtype), pltpu.HBM(land_thru.shape, land_thru.dtype)),
        in_specs=(HBM, HBM, SEM, SEM, ANY), out_specs=(HBM, HBM), input_output_aliases={0: 0, 1: 1},
        compiler_params=pltpu.CompilerParams(has_side_effects=EFFECT),
    )(src_thru, land_thru, send_sems, recv_sems, after)[1]


def _silu(a):
    return a * _sig(a)


def _mod_piece(c_all, w_ada, b_piece):
    def body(c_ref, w_ref, b_ref, o_ref):
        o_ref[...] = _dot(_bf(_silu(c_ref[...])), _bf(w_ref[...])) + b_ref[...]

    return pl.pallas_call(
        body, name="mod_piece", out_shape=jax.ShapeDtypeStruct((c_all.shape[0], w_ada.shape[1]), F32),
        compiler_params=_params(),
    )(c_all, w_ada, b_piece)


def _gw_ada(c_all, dmod_piece):
    def body(c_ref, d_ref, o_ref):
        o_ref[...] = _dot_tn(_bf(_silu(c_ref[...])), _bf(d_ref[...]))

    return pl.pallas_call(
        body, name="gw_ada", out_shape=jax.ShapeDtypeStruct((c_all.shape[1], dmod_piece.shape[1]), F32),
        compiler_params=_params(),
    )(c_all, dmod_piece)


def _adam(parts, w, m, v, name):
    k, r, n = parts.shape
    if r <= 256 or r % 256 == 0:
        tr, tn = min(r, 256), n
    else:
        tr, tn = r, 256
    assert r % tr == 0 and n % tn == 0

    def body(p_ref, w_ref, m_ref, v_ref, g_ref, d_ref, nm_ref, nv_ref):
        g = p_ref[0].astype(F32)
        for j in range(1, k):
            g = g + p_ref[j].astype(F32)
        g_ref[...] = g
        d_ref[...], nm_ref[...], nv_ref[...] = _adam_math(g, w_ref[...], m_ref[...], v_ref[...])

    blk = pl.BlockSpec((tr, tn), lambda i, j: (i, j))
    return pl.pallas_call(
        body, name=name, grid=(r // tr, n // tn),
        in_specs=[pl.BlockSpec((k, tr, tn), lambda i, j: (0, i, j)), blk, blk, blk],
        out_specs=[blk, blk, blk, blk],
        out_shape=[jax.ShapeDtypeStruct((r, n), F32)] * 4,
        compiler_params=_params(dimension_semantics=("arbitrary", "arbitrary")),
    )(parts, w, m, v)


def _adam_math(g, w, m, v):
    m_new = ADAM_B1 * m + (1.0 - ADAM_B1) * g
    v_new = ADAM_B2 * v + (1.0 - ADAM_B2) * jnp.square(g)
    m_hat = m_new / (1.0 - ADAM_B1 ** ADAM_STEP)
    v_hat = v_new / (1.0 - ADAM_B2 ** ADAM_STEP)
    return -ADAM_LR * (m_hat / (jnp.sqrt(v_hat) + ADAM_EPS) + ADAM_WD * w), m_new, v_new


_SMALL = (("b_ada", 3 * D_MODEL), ("norm_w", D_MODEL), ("q_norm_w", HEAD_DIM), ("k_norm_w", HEAD_DIM),
          ("rel_bias", REL_BUCKETS * ATTN_HEADS), ("sinks", ATTN_HEADS), ("conv_b", XBC_W), ("dt_bias", SSM_HEADS),
          ("a_log", SSM_HEADS), ("d_skip", SSM_HEADS), ("ssm_norm_w", SSM_W))
_SLOT = tuple(-(-n // 128) * 128 for _, n in _SMALL)
_SLOT_OFF = tuple(int(o) for o in np.cumsum((0,) + _SLOT))
_LOSS_OFF = _SLOT_OFF[-1]
_CW_OFF = _LOSS_OFF + 128
_PACK_N = _CW_OFF + CONV_K * XBC_W


def _pack_partials(small, loss, g_conv_w):
    parts = []
    for (name, n), slot in zip(_SMALL, _SLOT):
        parts.append(small[name].reshape(1, n))
        if slot > n:
            parts.append(jnp.zeros((1, slot - n), F32))
    parts += [loss.reshape(1, 1), jnp.zeros((1, 127), F32), g_conv_w.reshape(1, CONV_K * XBC_W)]
    return jnp.concatenate(parts, axis=1)


def _adam_small(pack_all, w, m, v):
    names = [name for name, _ in _SMALL]

    def body(p_ref, *rest):
        ins, outs = rest[:3 * len(names)], rest[3 * len(names):]

        def total(off, n):
            g = p_ref[0, :, off:off + n]
            for d in range(1, N_DEV):
                g = g + p_ref[d, :, off:off + n]
            return g

        for j, (name, n) in enumerate(_SMALL):
            g = total(_SLOT_OFF[j], n)
            delta, m_new, v_new = _adam_math(g, ins[3 * j][...], ins[3 * j + 1][...], ins[3 * j + 2][...])
            outs[4 * j][...] = g
            outs[4 * j + 1][...] = delta
            outs[4 * j + 2][...] = m_new
            outs[4 * j + 3][...] = v_new
        outs[-1][...] = total(_LOSS_OFF, 1)

    flat = []
    for name, n in _SMALL:
        flat += [w[name].reshape(1, n), m[name].reshape(1, n), v[name].reshape(1, n)]
    out_shape = [jax.ShapeDtypeStruct((1, n), F32) for _, n in _SMALL for _ in range(4)] + [jax.ShapeDtypeStruct((1, 1), F32)]
    out = pl.pallas_call(body, name="adam_small", out_shape=out_shape, compiler_params=_params())(pack_all, *flat)
    res = {name: [out[4 * j + t].reshape(w[name].shape) for t in range(4)] for j, name in enumerate(names)}
    return res, out[-1]


WEIGHTS = ("w_ada", "b_ada", "norm_w", "w_in", "q_norm_w", "k_norm_w", "rel_bias", "sinks", "conv_w", "conv_b",
           "dt_bias", "a_log", "d_skip", "ssm_norm_w", "w_attn_proj", "w_ssm_proj", "w_out")


def kernel(x, c, w_ada, b_ada, norm_w, w_in, q_norm_w, k_norm_w, rel_bias, sinks, conv_w, conv_b, dt_bias, a_log, d_skip, ssm_norm_w, w_attn_proj, w_ssm_proj, w_out, loss_target, m_w_ada, m_b_ada, m_norm_w, m_w_in, m_q_norm_w, m_k_norm_w, m_rel_bias, m_sinks, m_conv_w, m_conv_b, m_dt_bias, m_a_log, m_d_skip, m_ssm_norm_w, m_w_attn_proj, m_w_ssm_proj, m_w_out, v_w_ada, v_b_ada, v_norm_w, v_w_in, v_q_norm_w, v_k_norm_w, v_rel_bias, v_sinks, v_conv_w, v_conv_b, v_dt_bias, v_a_log, v_d_skip, v_ssm_norm_w, v_w_attn_proj, v_w_ssm_proj, v_w_out):
    w = dict(w_ada=w_ada, b_ada=b_ada, norm_w=norm_w, w_in=w_in, q_norm_w=q_norm_w, k_norm_w=k_norm_w,
             rel_bias=rel_bias, sinks=sinks, conv_w=conv_w, conv_b=conv_b, dt_bias=dt_bias, a_log=a_log,
             d_skip=d_skip, ssm_norm_w=ssm_norm_w, w_attn_proj=w_attn_proj, w_ssm_proj=w_ssm_proj, w_out=w_out)
    m = dict(w_ada=m_w_ada, b_ada=m_b_ada, norm_w=m_norm_w, w_in=m_w_in, q_norm_w=m_q_norm_w, k_norm_w=m_k_norm_w,
             rel_bias=m_rel_bias, sinks=m_sinks, conv_w=m_conv_w, conv_b=m_conv_b, dt_bias=m_dt_bias, a_log=m_a_log,
             d_skip=m_d_skip, ssm_norm_w=m_ssm_norm_w, w_attn_proj=m_w_attn_proj, w_ssm_proj=m_w_ssm_proj, w_out=m_w_out)
    v = dict(w_ada=v_w_ada, b_ada=v_b_ada, norm_w=v_norm_w, w_in=v_w_in, q_norm_w=v_q_norm_w, k_norm_w=v_k_norm_w,
             rel_bias=v_rel_bias, sinks=v_sinks, conv_w=v_conv_w, conv_b=v_conv_b, dt_bias=v_dt_bias, a_log=v_a_log,
             d_skip=v_d_skip, ssm_norm_w=v_ssm_norm_w, w_attn_proj=v_w_attn_proj, w_ssm_proj=v_w_ssm_proj, w_out=v_w_out)
    me = 4 * lax.axis_index("x") + 2 * lax.axis_index("y") + lax.axis_index("c")
    ada_n = w_ada.shape[2]
    in_n = w_in.shape[2]
    cw_n = conv_w.shape[2]

    first = _ag_direct(jnp.concatenate([c, conv_w[0].reshape(1, CONV_K * cw_n)], axis=1), "ag_c")[:, 0]
    c_all = first[:, :D_MODEL]
    conv_w_full = first[:, D_MODEL:].reshape(N_DEV, CONV_K, cw_n).transpose(1, 0, 2).reshape(CONV_K, XBC_W)
    b_piece = lax.dynamic_slice_in_dim(b_ada, me * ada_n, ada_n, axis=1)
    mod_all = _ag_direct(_mod_piece(c_all, w_ada[0], b_piece), "ag_mod")
    mod = lax.dynamic_index_in_dim(mod_all, me, axis=1, keepdims=False).reshape(1, 3 * D_MODEL)
    shift, scale, gate = mod[:, :D_MODEL], mod[:, D_MODEL:2 * D_MODEL], mod[:, 2 * D_MODEL:]

    w_t, zero = _ag_two_level(w_in[0].T.astype(BF), "ag_w_in")
    w_t = w_t.reshape(N_DEV * in_n, D_MODEL)

    def with_mine(blocks, mine):
        return lax.dynamic_update_index_in_dim(lax.empty(blocks, mine.dtype), mine, me, axis=0)

    rows = jnp.concatenate([w_attn_proj[0], w_ssm_proj[0], w_out[0]], axis=0).astype(BF) + zero
    r_ap, r_sp = w_attn_proj.shape[1], w_ssm_proj.shape[1]
    rows_started, zero = _exchange_start(rows, with_mine((N_DEV,) + rows.shape, rows), True, "ag_rows_start")

    def rows_fn(after):
        return _exchange_wait(rows_started, after, True, "ag_rows_wait")

    started = {}

    def send_blocks(key, g, name):
        started[key], zero = _exchange_start(
            g, with_mine(g.shape, lax.dynamic_index_in_dim(g, me, axis=0, keepdims=False)), False, name)
        return zero

    def after_mid(g_wap, g_wsp, g_wout):
        return send_blocks("rows", jnp.concatenate(
            [g_wap.reshape(N_DEV, r_ap, D_MODEL), g_wsp.reshape(N_DEV, r_sp, D_MODEL),
             g_wout.reshape(N_DEV, r_ap, D_MODEL)], axis=1), "rs_rows_start")

    def after_gw(g_ws):
        return send_blocks("in", jnp.concatenate(g_ws, axis=0).reshape(N_DEV, in_n, D_MODEL), "rs_in_start")

    r = _local_step(x[0], loss_target[0], shift, scale + zero, gate, w_t, rows_fn, norm_w, q_norm_w, k_norm_w,
                    rel_bias, sinks, conv_w_full, conv_b, dt_bias, a_log, d_skip, ssm_norm_w, after_mid, after_gw)

    small = dict(b_ada=r["dmod"], norm_w=r["g_norm_w"], q_norm_w=r["g_qnw"], k_norm_w=r["g_knw"], rel_bias=r["g_rel"],
                 sinks=r["g_sinks"], conv_b=r["g_conv_b"], dt_bias=r["g_dt_bias"], a_log=r["g_a_log"],
                 d_skip=r["g_d_skip"], ssm_norm_w=r["g_ssm_nw"])
    pack_all = _ag_direct(_pack_partials(small, r["loss"], r["g_conv_w"]), "ag_small")
    res, loss = _adam_small(pack_all, w, m, v)
    loss = loss[0, 0]
    cw_parts = pack_all[:, 0, _CW_OFF:].reshape(N_DEV, CONV_K, XBC_W)
    cw_mine = lax.dynamic_slice_in_dim(cw_parts, me * cw_n, cw_n, axis=2)
    res["conv_w"] = [a[None] for a in _adam(cw_mine, conv_w[0], m_conv_w[0], v_conv_w[0], "adam_conv_w")]

    dmod_piece = lax.dynamic_slice_in_dim(pack_all[:, 0, :3 * D_MODEL], me * ada_n, ada_n, axis=1)
    g_ada = _gw_ada(c_all, dmod_piece)
    res["w_ada"] = [a[None] for a in _adam(g_ada[None], w_ada[0], m_w_ada[0], v_w_ada[0], "adam_w_ada")]

    cat = lambda d: jnp.concatenate([d["w_attn_proj"][0], d["w_ssm_proj"][0], d["w_out"][0]], axis=0)
    rows_res = _adam(_exchange_wait(started["rows"], g_ada, False, "rs_rows_wait"), cat(w), cat(m), cat(v), "adam_w_rows")
    res["w_in"] = [a.T[None] for a in _adam(_exchange_wait(started["in"], rows_res[0], False, "rs_in_wait"),
                                            w_in[0].T, m_w_in[0].T, v_w_in[0].T, "adam_w_in")]
    res["w_attn_proj"] = [a[None, :r_ap] for a in rows_res]
    res["w_ssm_proj"] = [a[None, r_ap:r_ap + r_sp] for a in rows_res]
    res["w_out"] = [a[None, r_ap + r_sp:] for a in rows_res]

    outs = [loss, r["grad_x"][None]]
    for j in range(4):
        outs += [res[name][j] for name in WEIGHTS]
    return tuple(outs)
```

```python
import math

import numpy as np
import jax
import jax.numpy as jnp
from jax import lax
from jax.experimental import pallas as pl
from jax.experimental.pallas import tpu as pltpu

F32 = jnp.float32
BF = jnp.bfloat16
HI = lax.Precision.HIGHEST

D_MODEL = 1024
ATTN_HEADS = 16
KV_HEADS = 4
GRP = ATTN_HEADS // KV_HEADS
HEAD_DIM = 64
ATTN_W = ATTN_HEADS * HEAD_DIM
KV_W = KV_HEADS * HEAD_DIM
BLOCK = 128
REL_BUCKETS = 32
REL_MAX_DIST = 128
SSM_W = 2048
SSM_P = 64
SSM_HEADS = 32
SSM_G = 4
SSM_R = 8
SSM_N = 128
CONV_K = 4
XBC_W = SSM_W + 2 * SSM_G * SSM_N
SEG_W = (ATTN_W, 2 * KV_W, ATTN_W + SSM_W, XBC_W, SSM_HEADS, 2 * D_MODEL)
NSEG = len(SEG_W)
SEG_OFF = tuple(int(v) for v in np.cumsum((0,) + SEG_W))
IN_W = SEG_OFF[-1]
GATE_SEGS = (2, 5)
EPS = 1e-6
N_DEV = 8
ADAM_LR, ADAM_B1, ADAM_B2, ADAM_EPS, ADAM_WD, ADAM_STEP = 0.001, 0.9, 0.999, 1e-08, 0.01, 10
VMEM_LIMIT = 60 * 1024 * 1024
MESH = pl.DeviceIdType.MESH
ANY = pl.BlockSpec(memory_space=pl.ANY)


def _dot(a, b, precision=None):
    return jnp.dot(a, b, preferred_element_type=F32, precision=precision)


def _dot_nt(a, b, precision=None):
    return lax.dot_general(a, b, (((1,), (1,)), ((), ())), preferred_element_type=F32, precision=precision)


def _dot_tn(a, b, precision=None):
    return lax.dot_general(a, b, (((0,), (0,)), ((), ())), preferred_element_type=F32, precision=precision)


def _bf(a):
    return a.astype(BF)


def _sig(a):
    return 0.5 * jnp.tanh(0.5 * a) + 0.5


def _params(**kw):
    return pltpu.CompilerParams(vmem_limit_bytes=VMEM_LIMIT, **kw)


def _full(shape):
    nd = len(shape)
    return pl.BlockSpec(shape, lambda i: (0,) * nd)


def _rows(tm, w):
    return pl.BlockSpec((tm, w), lambda i: (i, 0))


def _inproj(x, norm_w, scale, shift, w_t, tm=256):
    s = x.shape[0]

    def body(x_ref, nw_ref, sc_ref, sh_ref, w_hbm, *rest):
        outs, h_ref, w_vm, sem = rest[:NSEG], rest[NSEG], rest[NSEG + 1], rest[NSEG + 2]
        first = pl.program_id(0) == 0
        cps = [pltpu.make_async_copy(w_hbm.at[SEG_OFF[j]:SEG_OFF[j + 1], :], w_vm.at[SEG_OFF[j]:SEG_OFF[j + 1], :], sem.at[j])
               for j in range(NSEG)]

        def tile(waiting):
            xv = x_ref[...]
            r = lax.rsqrt(jnp.mean(xv * xv, axis=-1, keepdims=True) + EPS)
            h = xv * r * (nw_ref[...] * (1.0 + sc_ref[...])) + sh_ref[...]
            hb = _bf(h)
            h_ref[...] = hb
            for j in range(NSEG):
                if waiting:
                    cps[j].wait()
                outs[j][...] = _dot_nt(hb, w_vm[SEG_OFF[j]:SEG_OFF[j + 1], :]).astype(outs[j].dtype)

        @pl.when(first)
        def _():
            for cp in cps:
                cp.start()
            tile(True)

        @pl.when(jnp.logical_not(first))
        def _():
            tile(False)

    vec = _full((1, D_MODEL))
    return pl.pallas_call(
        body, name="inproj", grid=(s // tm,),
        in_specs=[_rows(tm, D_MODEL), vec, vec, vec, ANY],
        out_specs=[_rows(tm, w) for w in SEG_W] + [_rows(tm, D_MODEL)],
        out_shape=[jax.ShapeDtypeStruct((s, w), BF if j in GATE_SEGS else F32) for j, w in enumerate(SEG_W)]
                  + [jax.ShapeDtypeStruct((s, D_MODEL), BF)],
        scratch_shapes=[pltpu.VMEM((IN_W, D_MODEL), BF), pltpu.SemaphoreType.DMA((NSEG,))],
        compiler_params=_params(dimension_semantics=("arbitrary",)),
    )(x, norm_w, scale, shift, w_t)


def _bucket_onehot_t():
    qi = jnp.arange(BLOCK)[:, None]
    kj = jnp.arange(2 * BLOCK)[None, :]
    dist = qi + BLOCK - kj
    n = jnp.maximum(dist, 0)
    max_exact = REL_BUCKETS // 2
    nf = jnp.maximum(n, 1).astype(F32)
    large = max_exact + (jnp.log(nf / max_exact) / math.log(REL_MAX_DIST / max_exact)
                         * (REL_BUCKETS - max_exact)).astype(jnp.int32)
    large = jnp.minimum(large, REL_BUCKETS - 1)
    bucket = jnp.where(n < max_exact, n, large).reshape(1, BLOCK * 2 * BLOCK)
    return (bucket == jnp.arange(REL_BUCKETS)[:, None]).astype(F32)


def _bias_dense(rel_bias_t, oh_t):
    def body(rb_ref, oh_ref, o_ref):
        o_ref[...] = _dot(rb_ref[...], oh_ref[...], HI)

    return pl.pallas_call(
        body, name="bias_dense", out_shape=jax.ShapeDtypeStruct((ATTN_HEADS, BLOCK * 2 * BLOCK), F32),
        compiler_params=_params(),
    )(rel_bias_t, oh_t)


def _bias_grad(ds_sum, oh_t):
    def body(ds_ref, oh_ref, o_ref):
        o_ref[...] = _dot_nt(ds_ref[...], oh_ref[...], HI)

    return pl.pallas_call(
        body, name="bias_grad", out_shape=jax.ShapeDtypeStruct((ATTN_HEADS, REL_BUCKETS), F32),
        compiler_params=_params(),
    )(ds_sum, oh_t)


def _group_sum(a, e):
    hi = _bf(a)
    return _dot(hi, e) + _dot(_bf(a - hi.astype(F32)), e)


def _group_bcast(a, e3t):
    hi = _bf(a)
    r1 = a - hi.astype(F32)
    mid = _bf(r1)
    return _dot(jnp.concatenate([hi, mid, _bf(r1 - mid.astype(F32))], axis=1), e3t)


def _membership(width, group, ngroups):
    e = (jnp.arange(width)[:, None] // group == jnp.arange(ngroups)[None, :]).astype(BF)
    return e, jnp.tile(e.T, (3, 1))


def _fold(width, group):
    return (jnp.arange(width)[:, None] % group == jnp.arange(group)[None, :]).astype(BF)


def _heads_norm(t, w_x, e, e3t):
    r = lax.rsqrt(_dot(_bf(t * t), e) * (1.0 / HEAD_DIM) + EPS)
    r_x = _group_bcast(r, e3t)
    return t * r_x * w_x, r_x


def _heads_norm_bwd(t, r_x, w_x, d, e, e3t):
    wd = d * w_x
    corr = _group_bcast(_dot(_bf(t * wd), e) * (1.0 / HEAD_DIM), e3t)
    return r_x * wd - t * (r_x * r_x * r_x) * corr, jnp.sum(d * t * r_x, axis=0, keepdims=True)


def _stack_heads(a, hk):
    return jnp.concatenate([a[:, (hk * GRP + g) * HEAD_DIM:(hk * GRP + g + 1) * HEAD_DIM] for g in range(GRP)], axis=0)


def _stack_cols(a, hk):
    return jnp.concatenate([a[:, hk * GRP + g:hk * GRP + g + 1] for g in range(GRP)], axis=0)


def _masked_bias(bias):
    qi = jnp.arange(BLOCK)[:, None]
    kj = jnp.arange(2 * BLOCK)[None, :]
    cur_ok = jnp.logical_and(kj >= BLOCK, kj - BLOCK <= qi)
    both_ok = jnp.logical_or(jnp.logical_and(kj < BLOCK, kj > qi), cur_ok)
    return jnp.stack([jnp.where(cur_ok, bias, -1e30), jnp.where(both_ok, bias, -1e30)])


def _attn_consts(qnw, knw):
    eq, eq3t = _membership(ATTN_W, HEAD_DIM, ATTN_HEADS)
    ek, ek3t = _membership(KV_W, HEAD_DIM, ATTN_HEADS)
    return (jnp.tile(qnw, (1, ATTN_HEADS)), jnp.tile(knw, (1, KV_HEADS)), eq, eq3t, ek, ek3t)


def _attn_fwd(q, kv, bias, sinks, consts):
    s = q.shape[0]
    nb = s // BLOCK
    gq = GRP * BLOCK
    bias_t = bias.reshape(2, KV_HEADS, GRP, BLOCK, 2 * BLOCK).transpose(0, 1, 4, 2, 3).reshape(2, KV_HEADS, 2 * BLOCK, gq)
    sink_rows = jnp.repeat(sinks.reshape(KV_HEADS, GRP), BLOCK, axis=1).reshape(KV_HEADS, 1, gq)
    eye = jnp.eye(BLOCK, dtype=BF)

    def body(q_ref, kp_ref, kc_ref, vp_ref, vc_ref, b_ref, bt_ref, sk_ref, skr_ref, eye_ref,
             qw_ref, kw_ref, eq_ref, eq3_ref, ek_ref, ek3_ref, o_ref, lse_ref):
        qn = _bf(_heads_norm(q_ref[...], qw_ref[...], eq_ref[...], eq3_ref[...])[0] * (HEAD_DIM ** -0.5))
        kn = _bf(_heads_norm(jnp.concatenate([kp_ref[...], kc_ref[...]], axis=0), kw_ref[...], ek_ref[...], ek3_ref[...])[0])
        vv = _bf(jnp.concatenate([vp_ref[...], vc_ref[...]], axis=0))
        ones = jnp.ones((2 * BLOCK, HEAD_DIM), BF)
        lses = []
        kss = [slice(hk * HEAD_DIM, (hk + 1) * HEAD_DIM) for hk in range(KV_HEADS)]
        qgs = [_stack_heads(qn, hk) for hk in range(KV_HEADS)]
        sc_ts = [_dot_nt(kn[:, kss[hk]], qgs[hk]) + bt_ref[0, hk] for hk in range(KV_HEADS)]
        m_rows = [jnp.maximum(jnp.max(sc_ts[hk], axis=0, keepdims=True), skr_ref[hk]) for hk in range(KV_HEADS)]
        m8s = [_bf(jnp.broadcast_to(m + jnp.abs(m) * (2.0 ** -7), (8, gq))) for m in m_rows]
        ms = [jnp.concatenate([_dot_nt(eye_ref[...], m8[:, g * BLOCK:(g + 1) * BLOCK])[:, 0:1] for g in range(GRP)], axis=0)
              for m8 in m8s]
        scs = [_dot_nt(qgs[hk], kn[:, kss[hk]]) + b_ref[0, hk * GRP:(hk + 1) * GRP].reshape(gq, 2 * BLOCK)
               for hk in range(KV_HEADS)]
        ps = [_bf(jnp.exp(scs[hk] - ms[hk])) for hk in range(KV_HEADS)]
        pvs = [_dot(ps[hk], jnp.concatenate([vv[:, kss[hk]], ones], axis=1)) for hk in range(KV_HEADS)]
        for hk in range(KV_HEADS):
            m, pv = ms[hk], pvs[hk]
            sink = jnp.concatenate([jnp.full((BLOCK, 1), sk_ref[0, hk * GRP + g], F32) for g in range(GRP)], axis=0)
            den = pv[:, HEAD_DIM:HEAD_DIM + 1] + jnp.exp(sink - m)
            out = pv[:, :HEAD_DIM] * (1.0 / den)
            lse = m + jnp.log(den)
            for g in range(GRP):
                h = hk * GRP + g
                o_ref[:, h * HEAD_DIM:(h + 1) * HEAD_DIM] = out[g * BLOCK:(g + 1) * BLOCK]
                lses.append(lse[g * BLOCK:(g + 1) * BLOCK])
        lse_ref[...] = jnp.concatenate(lses, axis=1)

    cur = lambda w, col=0: pl.BlockSpec((BLOCK, w), lambda i: (i, col))
    prev = lambda w, col=0: pl.BlockSpec((BLOCK, w), lambda i: (jnp.maximum(i - 1, 0), col))
    whole = lambda a: pl.BlockSpec(a.shape, lambda i: (0,) * a.ndim)
    first_or_not = lambda a: pl.BlockSpec((1,) + a.shape[1:], lambda i: (jnp.minimum(i, 1),) + (0,) * (a.ndim - 1))
    return pl.pallas_call(
        body, name="attn_fwd", grid=(nb,),
        in_specs=[cur(ATTN_W), prev(KV_W, 0), cur(KV_W, 0), prev(KV_W, 1), cur(KV_W, 1),
                  first_or_not(bias), first_or_not(bias_t),
                  pl.BlockSpec(memory_space=pltpu.SMEM), whole(sink_rows), whole(eye)] + [_full(c.shape) for c in consts],
        out_specs=[cur(ATTN_W), cur(ATTN_HEADS)],
        out_shape=[jax.ShapeDtypeStruct((s, ATTN_W), F32), jax.ShapeDtypeStruct((s, ATTN_HEADS), F32)],
        compiler_params=_params(dimension_semantics=("arbitrary",)),
    )(q, kv, kv, kv, kv, bias, bias_t, sinks, sink_rows, eye, *consts)


def _conv_taps(xbc, tail):
    ext = jnp.concatenate([tail, xbc], axis=0)
    return [pltpu.roll(ext, CONV_K - 1 - j, axis=0)[8:8 + BLOCK] if j < CONV_K - 1 else xbc for j in range(CONV_K)]


def _softplus(u):
    return jnp.maximum(u, 0.0) + jnp.log(1.0 + jnp.exp(-jnp.abs(u)))


def _tril():
    r = lax.broadcasted_iota(jnp.int32, (BLOCK, BLOCK), 0)
    c = lax.broadcasted_iota(jnp.int32, (BLOCK, BLOCK), 1)
    return r >= c


def _triu():
    r = lax.broadcasted_iota(jnp.int32, (BLOCK, BLOCK), 0)
    c = lax.broadcasted_iota(jnp.int32, (BLOCK, BLOCK), 1)
    return r <= c


def _exact_left(m01, a):
    hi = _bf(a)
    r1 = a - hi.astype(F32)
    mid = _bf(r1)
    return _dot(m01, hi) + _dot(m01, mid) + _dot(m01, _bf(r1 - mid.astype(F32)))


def _ssd_common(conv, dtr, dtb_ref, alog_ref, e3_ref):
    sg = _sig(conv)
    xact = conv * sg
    u = dtr + dtb_ref[...]
    dt = _softplus(u)
    a = -jnp.exp(alog_ref[...])
    trilb = _tril()
    acum = _exact_left(trilb.astype(BF), dt * a) * math.log2(math.e)
    both = _group_bcast(jnp.concatenate([dt, acum], axis=0), e3_ref[...])
    dt_x, acum_x = both[:BLOCK], both[BLOCK:]
    return sg, xact, u, dt, a, trilb, acum, dt_x, acum_x


SSD_CH = 2


def _ssd_fwd(xbc, dt_raw, conv_w, conv_b, dt_bias, a_log, dsk_x, e3t):
    s = xbc.shape[0]
    nc = s // BLOCK
    ch = SSD_CH if nc % SSD_CH == 0 else 1
    rows = ch * BLOCK

    def body(x_ref, tail_ref, dtr_ref, cw_ref, cb_ref, dtb_ref, alog_ref, dsk_ref, e3_ref,
             y_ref, hp_ref, conv_ref, hst, yd_s, yoff_s):
        i = pl.program_id(0)

        @pl.when(i == 0)
        def _():
            hst[...] = jnp.zeros_like(hst)

        for j in range(ch):
            rs = slice(j * BLOCK, (j + 1) * BLOCK)
            tail = jnp.where(i > 0, tail_ref[...], 0.0) if j == 0 else x_ref[j * BLOCK - 8:j * BLOCK, :]
            taps = _conv_taps(x_ref[rs, :], tail)
            conv = cb_ref[...] + sum(taps[t] * cw_ref[t:t + 1, :] for t in range(CONV_K))
            conv_ref[rs, :] = conv
            _, xact, _, _, _, trilb, acum, dt_x, acum_x = _ssd_common(conv, dtr_ref[rs, :], dtb_ref, alog_ref, e3_ref)
            xs = xact[:, :SSM_W]
            acum_t = acum.T
            ea_x = jnp.exp2(acum_x)
            last_x = acum_x[BLOCK - 1:BLOCK, :]
            xdt = xs * dt_x
            xw = xdt * jnp.exp2(last_x - acum_x)
            cd_x = jnp.exp2(last_x)
            hprev = hst[...]
            hp_ref[j] = hprev
            sls = [slice(g * SSM_R * SSM_P, (g + 1) * SSM_R * SSM_P) for g in range(SSM_G)]
            bgs = [_bf(xact[:, SSM_W + g * SSM_N:SSM_W + (g + 1) * SSM_N]) for g in range(SSM_G)]
            cgs = [_bf(xact[:, SSM_W + SSM_G * SSM_N + g * SSM_N:SSM_W + SSM_G * SSM_N + (g + 1) * SSM_N])
                   for g in range(SSM_G)]
            xdt_b, xw_b, hprev_b = _bf(xdt), _bf(xw), _bf(hprev)
            low_half = lax.broadcasted_iota(jnp.int32, (BLOCK, 2 * SSM_P), 1) < SSM_P
            cbs = [_dot_nt(cgs[g], bgs[g]) for g in range(SSM_G)]
            for g in range(SSM_G):
                sl = sls[g]
                yoff_s[:, sl] = _dot(cgs[g], hprev_b[:, sl]) * ea_x[:, sl]
                hst[:, sl] = hprev[:, sl] * cd_x[:, sl] + _dot_tn(bgs[g], xw_b[:, sl])
            for g in range(SSM_G):
                hss = [slice((g * SSM_R + r) * SSM_P, (g * SSM_R + r + 1) * SSM_P) for r in range(SSM_R)]
                mms = [_bf(cbs[g] * jnp.exp2(jnp.where(trilb, acum[:, g * SSM_R + r:g * SSM_R + r + 1]
                                                      - acum_t[g * SSM_R + r:g * SSM_R + r + 1, :], -1e30)))
                       for r in range(SSM_R)]
                for r in range(0, SSM_R, 2):
                    pair = slice(hss[r].start, hss[r + 1].stop)
                    xp = xdt_b[:, pair]
                    rhs = jnp.concatenate([jnp.where(low_half, xp, 0), jnp.where(low_half, 0, xp)], axis=0)
                    yd_s[:, pair] = _dot(jnp.concatenate([mms[r], mms[r + 1]], axis=1), rhs)
            y_ref[rs, :] = yd_s[...] + yoff_s[...] + dsk_ref[...] * xs

    blk = lambda w: pl.BlockSpec((rows, w), lambda i: (i, 0))
    return pl.pallas_call(
        body, name="ssd_fwd", grid=(nc // ch,),
        in_specs=[blk(XBC_W), pl.BlockSpec((8, XBC_W), lambda i: (jnp.maximum(i * (rows // 8) - 1, 0), 0)),
                  blk(SSM_HEADS), _full((CONV_K, XBC_W)), _full((1, XBC_W)), _full((1, SSM_HEADS)),
                  _full((1, SSM_HEADS)), _full((1, SSM_W)), _full((3 * SSM_HEADS, SSM_W))],
        out_specs=[blk(SSM_W), pl.BlockSpec((ch, SSM_N, SSM_W), lambda i: (i, 0, 0)), blk(XBC_W)],
        out_shape=[jax.ShapeDtypeStruct((s, SSM_W), F32), jax.ShapeDtypeStruct((nc, SSM_N, SSM_W), F32),
                   jax.ShapeDtypeStruct((s, XBC_W), F32)],
        scratch_shapes=[pltpu.VMEM((SSM_N, SSM_W), F32), pltpu.VMEM((BLOCK, SSM_W), F32), pltpu.VMEM((BLOCK, SSM_W), F32)],
        compiler_params=_params(dimension_semantics=("arbitrary",)),
    )(xbc, xbc, dt_raw, conv_w, conv_b, dt_bias, a_log, dsk_x, e3t)


def _dsilu(z, sg, silu):
    return sg * (1.0 + (z - silu))


def _mid(x, tgt, o_att, zam, ypre, gab, gate, ssm_nw, rows_all, tm=256):
    s = x.shape[0]
    gw = SSM_W // SSM_G

    r_ap, r_sp = ATTN_W // N_DEV, SSM_W // N_DEV

    def body(x_ref, t_ref, o_ref, zam_ref, yp_ref, gab_ref, gate_ref, nw_ref, rows_h,
             dout_ref, do_ref, dzam_ref, dyp_ref, dgab_ref,
             yag_ref, dya_ref, yn_ref, dyb_ref, mg_ref, dob_ref, gnw_ref, dgate_ref, loss_ref,
             wap_v, wsp_v, wout_v, sem):
        i = pl.program_id(0)

        @pl.when(i == 0)
        def _():
            cps = []
            for d in range(N_DEV):
                for j, (dst, r0, rn) in enumerate(((wap_v, 0, r_ap), (wsp_v, r_ap, r_sp), (wout_v, r_ap + r_sp, r_ap))):
                    cps.append(pltpu.make_async_copy(rows_h.at[d, r0:r0 + rn, :], dst.at[d * rn:(d + 1) * rn, :], sem.at[j]))
            for cp in cps:
                cp.start()
            gnw_ref[...] = jnp.zeros_like(gnw_ref)
            dgate_ref[...] = jnp.zeros_like(dgate_ref)
            loss_ref[...] = jnp.zeros_like(loss_ref)
            for cp in cps:
                cp.wait()

        gate = gate_ref[...]
        nw = nw_ref[...]
        o_att = o_ref[...]
        z_a = zam_ref[:, :ATTN_W].astype(F32)
        s_a = _sig(z_a)
        silu_a = z_a * s_a
        yag = _bf(o_att * silu_a)
        yag_ref[...] = yag
        ypre = yp_ref[...]
        z_m = zam_ref[:, ATTN_W:].astype(F32)
        s_m = _sig(z_m)
        silu_m = z_m * s_m
        yg = ypre * silu_m
        rinv = jnp.concatenate(
            [jnp.broadcast_to(lax.rsqrt(jnp.mean(yg[:, g * gw:(g + 1) * gw] ** 2, axis=-1, keepdims=True) + EPS), (tm, gw))
             for g in range(SSM_G)], axis=1)
        ynr = yg * rinv
        yn = _bf(ynr * nw)
        yn_ref[...] = yn
        y_a = _dot(yag, wap_v[...])
        y_b = _dot(yn, wsp_v[...])
        g_a = _sig(gab_ref[:, :D_MODEL].astype(F32))
        g_b = _sig(gab_ref[:, D_MODEL:].astype(F32))
        merged = _bf(g_a * y_a + g_b * y_b)
        mg_ref[...] = merged
        o = _dot(merged, wout_v[...])
        diff = x_ref[...] + gate * o - t_ref[...]
        loss_ref[...] += (0.5 / D_MODEL) * jnp.sum(diff * diff, axis=(0, 1), keepdims=True)
        dout = diff * (1.0 / D_MODEL)
        dout_ref[...] = dout
        dgate_ref[...] += jnp.sum(dout * o, axis=0, keepdims=True)
        d_o = _bf(dout * gate)
        dob_ref[...] = d_o
        dmerged = _dot_nt(d_o, wout_v[...])
        dy_af = dmerged * g_a
        dy_bf = dmerged * g_b
        dy_a = _bf(dy_af)
        dy_b = _bf(dy_bf)
        dya_ref[...] = dy_a
        dyb_ref[...] = dy_b
        dyag = _dot_nt(dy_a, wap_v[...])
        dyn = _dot_nt(dy_b, wsp_v[...])
        dgab_ref[:, :D_MODEL] = _bf(dy_af * y_a * (1.0 - g_a))
        dgab_ref[:, D_MODEL:] = _bf(dy_bf * y_b * (1.0 - g_b))
        do_ref[...] = dyag * silu_a
        dzam_ref[:, :ATTN_W] = _bf(dyag * o_att * _dsilu(z_a, s_a, silu_a))
        gnw_ref[...] += jnp.sum(dyn * ynr, axis=0, keepdims=True)
        dynw = dyn * nw
        corr = jnp.concatenate(
            [jnp.broadcast_to(jnp.mean((dynw * ynr)[:, g * gw:(g + 1) * gw], axis=-1, keepdims=True), (tm, gw))
             for g in range(SSM_G)], axis=1)
        dyg = rinv * (dynw - ynr * corr)
        dyp_ref[...] = dyg * silu_m
        dzam_ref[:, ATTN_W:] = _bf(dyg * ypre * _dsilu(z_m, s_m, silu_m))

    r1, r2, r3 = _rows(tm, D_MODEL), _rows(tm, SSM_W), _rows(tm, ATTN_W + SSM_W)
    sd = jax.ShapeDtypeStruct
    return pl.pallas_call(
        body, name="mid", grid=(s // tm,),
        in_specs=[r1, r1, r1, r3, r2, r2, _full((1, D_MODEL)), _full((1, SSM_W)), ANY],
        out_specs=[r1, r1, r3, r2, r2, r1, r1, r2, r1, r1, r1,
                   _full((1, SSM_W)), _full((1, D_MODEL)), _full((1, 1))],
        out_shape=[sd((s, D_MODEL), F32), sd((s, ATTN_W), F32), sd((s, ATTN_W + SSM_W), BF), sd((s, SSM_W), F32),
                   sd((s, 2 * D_MODEL), BF),
                   sd((s, ATTN_W), BF), sd((s, D_MODEL), BF), sd((s, SSM_W), BF), sd((s, D_MODEL), BF),
                   sd((s, D_MODEL), BF), sd((s, D_MODEL), BF),
                   sd((1, SSM_W), F32), sd((1, D_MODEL), F32), sd((1, 1), F32)],
        scratch_shapes=[pltpu.VMEM((ATTN_W, D_MODEL), BF), pltpu.VMEM((SSM_W, D_MODEL), BF), pltpu.VMEM((D_MODEL, D_MODEL), BF),
                        pltpu.SemaphoreType.DMA((3,))],
        compiler_params=_params(dimension_semantics=("arbitrary",)),
    )(x, tgt, o_att, zam, ypre, gab, gate, ssm_nw, rows_all)


def _attn_bwd(q, kv, bias, sinks, consts, o_att, lse, d_o):
    s = q.shape[0]
    nb = s // BLOCK
    folds = (_fold(ATTN_W, HEAD_DIM), _fold(KV_W, HEAD_DIM))

    def body(q_ref, kp_ref, kc_ref, vp_ref, vc_ref, b_ref, skv_ref, qw_ref, kw_ref, eq_ref, eq3_ref, ek_ref, ek3_ref,
             fq_ref, fk_ref, o_ref, lse_ref, do_ref,
             dq_ref, dkv_ref, dss_ref, gqw_ref, gkw_ref, gsk_ref, ckn, cv, dqn_s, dkn_s, dv_s, gq_x, gk_x):
        i = pl.program_id(0)
        kw, ek, ek3 = kw_ref[...], ek_ref[...], ek3_ref[...]

        @pl.when(i == 0)
        def _():
            for ref in (ckn, cv, dss_ref, gq_x, gk_x, gsk_ref):
                ref[...] = jnp.zeros_like(ref)

        @pl.when(i < nb)
        def _():
            qw, eq, eq3 = qw_ref[...], eq_ref[...], eq3_ref[...]
            qf = q_ref[...]
            qnf, rq_x = _heads_norm(qf, qw, eq, eq3)
            qn = _bf(qnf * (HEAD_DIM ** -0.5))
            kf = jnp.concatenate([kp_ref[...], kc_ref[...]], axis=0)
            knf, rk_x = _heads_norm(kf, kw, ek, ek3)
            kn = _bf(knf)
            vv = _bf(jnp.concatenate([vp_ref[...], vc_ref[...]], axis=0))
            d_of = do_ref[...]
            d_ob = _bf(d_of)
            lse_all = lse_ref[...]
            delta = _dot(_bf(d_of * o_ref[...]), eq)
            gsk_ref[...] += jnp.sum(-jnp.exp(skv_ref[...] - lse_all) * delta, axis=0, keepdims=True)
            kss = [slice(hk * HEAD_DIM, (hk + 1) * HEAD_DIM) for hk in range(KV_HEADS)]
            qgs = [_stack_heads(qn, hk) for hk in range(KV_HEADS)]
            d_ogs = [_stack_heads(d_ob, hk) for hk in range(KV_HEADS)]
            scs = [_dot_nt(qgs[hk], kn[:, kss[hk]]) + b_ref[0, hk * GRP:(hk + 1) * GRP].reshape(GRP * BLOCK, 2 * BLOCK)
                   for hk in range(KV_HEADS)]
            dps = [_dot_nt(d_ogs[hk], vv[:, kss[hk]]) for hk in range(KV_HEADS)]
            ps = [jnp.exp(scs[hk] - _stack_cols(lse_all, hk)) for hk in range(KV_HEADS)]
            dss = [ps[hk] * (dps[hk] - _stack_cols(delta, hk)) for hk in range(KV_HEADS)]
            pbs = [_bf(p) for p in ps]
            dsbs = [_bf(ds) for ds in dss]
            for hk in range(KV_HEADS):
                dss_ref[hk * GRP:(hk + 1) * GRP] += dss[hk].reshape(GRP, BLOCK, 2 * BLOCK)
            for hk in range(KV_HEADS):
                dv_s[:, kss[hk]] = _dot_tn(pbs[hk], d_ogs[hk])
                dkn_s[:, kss[hk]] = _dot_tn(dsbs[hk], qgs[hk])
            dqns = [_dot(dsbs[hk], kn[:, kss[hk]]) * (HEAD_DIM ** -0.5) for hk in range(KV_HEADS)]
            for hk in range(KV_HEADS):
                for g in range(GRP):
                    h = hk * GRP + g
                    dqn_s[:, h * HEAD_DIM:(h + 1) * HEAD_DIM] = dqns[hk][g * BLOCK:(g + 1) * BLOCK]
            dq, gq = _heads_norm_bwd(qf, rq_x, qw, dqn_s[...], eq, eq3)
            dq_ref[...] = _bf(dq)
            gq_x[...] += gq
            dk, gk = _heads_norm_bwd(kf[:BLOCK], rk_x[:BLOCK], kw, ckn[...] + dkn_s[0:BLOCK, :], ek, ek3)
            dkv_ref[:, :KV_W] = _bf(dk)
            gk_x[...] += gk
            dkv_ref[:, KV_W:] = _bf(cv[...] + dv_s[0:BLOCK, :])
            ckn[...] = dkn_s[BLOCK:2 * BLOCK, :]
            cv[...] = dv_s[BLOCK:2 * BLOCK, :]

        @pl.when(i == nb)
        def _():
            kc = kc_ref[...]
            dk, gk = _heads_norm_bwd(kc, _heads_norm(kc, kw, ek, ek3)[1], kw, ckn[...], ek, ek3)
            dkv_ref[:, :KV_W] = _bf(dk)
            dkv_ref[:, KV_W:] = _bf(cv[...])
            gqw_ref[...] = _group_sum(jnp.broadcast_to(gq_x[...], (8, ATTN_W)), fq_ref[...])[0:1]
            gkw_ref[...] = _group_sum(jnp.broadcast_to(gk_x[...] + gk, (8, KV_W)), fk_ref[...])[0:1]

    last = nb - 1
    cur = lambda w, col=0: pl.BlockSpec((BLOCK, w), lambda i: (jnp.minimum(i, last), col))
    prev = lambda w, col=0: pl.BlockSpec((BLOCK, w), lambda i: (jnp.maximum(jnp.minimum(i, last) - 1, 0), col))
    late = lambda w: pl.BlockSpec((BLOCK, w), lambda i: (jnp.maximum(i - 1, 0), 0))
    sd = jax.ShapeDtypeStruct
    return pl.pallas_call(
        body, name="attn_bwd", grid=(nb + 1,),
        in_specs=[cur(ATTN_W), prev(KV_W, 0), cur(KV_W, 0), prev(KV_W, 1), cur(KV_W, 1),
                  pl.BlockSpec((1, ATTN_HEADS, BLOCK, 2 * BLOCK), lambda i: (jnp.minimum(i, 1), 0, 0, 0)),
                  _full((1, ATTN_HEADS))]
                 + [_full(c.shape) for c in consts + folds] + [cur(ATTN_W), cur(ATTN_HEADS), cur(ATTN_W)],
        out_specs=[cur(ATTN_W), late(2 * KV_W),
                   pl.BlockSpec((ATTN_HEADS, BLOCK, 2 * BLOCK), lambda i: (0, 0, 0)),
                   _full((1, HEAD_DIM)), _full((1, HEAD_DIM)), _full((1, ATTN_HEADS))],
        out_shape=[sd((s, ATTN_W), BF), sd((s, 2 * KV_W), BF),
                   sd((ATTN_HEADS, BLOCK, 2 * BLOCK), F32), sd((1, HEAD_DIM), F32), sd((1, HEAD_DIM), F32),
                   sd((1, ATTN_HEADS), F32)],
        scratch_shapes=[pltpu.VMEM((BLOCK, KV_W), F32), pltpu.VMEM((BLOCK, KV_W), F32),
                        pltpu.VMEM((BLOCK, ATTN_W), F32), pltpu.VMEM((2 * BLOCK, KV_W), F32),
                        pltpu.VMEM((2 * BLOCK, KV_W), F32), pltpu.VMEM((1, ATTN_W), F32), pltpu.VMEM((1, KV_W), F32)],
        compiler_params=_params(dimension_semantics=("arbitrary",)),
    )(q, kv, kv, kv, kv, bias, sinks, *consts, *folds, o_att, lse, d_o)


def _ssd_bwd(xbc, conv_all, dt_raw, conv_w, dt_bias, a_log, dsk_x, e_mat, e3t, hprev_all, dy_all):
    s = xbc.shape[0]
    nc = s // BLOCK
    ch = 1
    rows = ch * BLOCK
    nsteps = nc // ch
    gw = SSM_R * SSM_P
    b0, c0 = SSM_W, SSM_W + SSM_G * SSM_N

    def body(x_ref, conv_ref, dtr_ref, cw_ref, dtb_ref, alog_ref, dsk_ref, e_ref, e3_ref, hp_ref, dy_ref,
             dx_ref, ddt_ref, gcw_ref, gcb_ref, gdtb_ref, galog_ref, gdsk_ref,
             dh, nhead, gdskx, dxdt_s, dbc_s, dxd_s):
        def chunk_bwd(j):
            rs = slice(j * BLOCK, (j + 1) * BLOCK)
            conv = conv_ref[rs, :]
            sg, xact, u, dt, a, trilb, acum, dt_x, acum_x = _ssd_common(conv, dtr_ref[rs, :], dtb_ref, alog_ref, e3_ref)
            xs = xact[:, :SSM_W]
            acum_t = acum.T
            ea_x = jnp.exp2(acum_x)
            last_x = acum_x[BLOCK - 1:BLOCK, :]
            dte_x = jnp.exp2(last_x - acum_x)
            cd_x = jnp.exp2(last_x)
            xdt = xs * dt_x
            xw = xdt * dte_x
            hprev = hp_ref[j]
            dhn = dh[...]
            dy = dy_ref[rs, :]
            gdskx[...] += jnp.sum(dy * xs, axis=0, keepdims=True)
            dyea = dy * ea_x
            lane = lax.broadcasted_iota(jnp.int32, (BLOCK, SSM_HEADS), 1)
            dacum = jnp.zeros((BLOCK, SSM_HEADS), F32)
            dacc_x, dlast_x = [], []
            sls = [slice(g * gw, (g + 1) * gw) for g in range(SSM_G)]
            bgs = [_bf(xact[:, b0 + g * SSM_N:b0 + (g + 1) * SSM_N]) for g in range(SSM_G)]
            cgs = [_bf(xact[:, c0 + g * SSM_N:c0 + (g + 1) * SSM_N]) for g in range(SSM_G)]
            hpgs = [_bf(hprev[:, sl]) for sl in sls]
            dhgs = [_bf(dhn[:, sl]) for sl in sls]
            dyeags = [_bf(dyea[:, sl]) for sl in sls]
            xwgs = [_bf(xw[:, sl]) for sl in sls]
            xdt_b, dy_b = _bf(xdt), _bf(dy)
            low_half = lax.broadcasted_iota(jnp.int32, (BLOCK, 2 * SSM_P), 1) < SSM_P
            cbs = [_dot_nt(cgs[g], bgs[g]) for g in range(SSM_G)]
            gmats = [_dot(cgs[g], hpgs[g]) for g in range(SSM_G)]
            dxws = [_dot(bgs[g], dhgs[g]) for g in range(SSM_G)]
            dcgs = [_dot_nt(dyeags[g], hpgs[g]) for g in range(SSM_G)]
            dbgs = [_dot_nt(xwgs[g], dhgs[g]) for g in range(SSM_G)]
            for g in range(SSM_G):
                sl = sls[g]
                dh[:, sl] = dhn[:, sl] * cd_x[:, sl] + _dot_tn(cgs[g], dyeags[g])
                dxdt_s[:, sl] = dxws[g] * dte_x[:, sl]
                dacc_x.append(dy[:, sl] * gmats[g] * ea_x[:, sl] - dxws[g] * xw[:, sl])
                dlast_x.append(jnp.sum(dxws[g] * xw[:, sl], axis=0, keepdims=True)
                               + jnp.sum(dhn[:, sl] * hprev[:, sl], axis=0, keepdims=True) * cd_x[:, sl])
            for g in range(SSM_G):
                bg, cg, cb, dbg, dcg = bgs[g], cgs[g], cbs[g], dbgs[g], dcgs[g]
                hss = [slice((g * SSM_R + r) * SSM_P, (g * SSM_R + r + 1) * SSM_P) for r in range(SSM_R)]
                lms = [jnp.exp2(jnp.where(trilb, acum[:, g * SSM_R + r:g * SSM_R + r + 1]
                                         - acum_t[g * SSM_R + r:g * SSM_R + r + 1, :], -1e30)) for r in range(SSM_R)]
                mms = [cb * lm for lm in lms]
                mmbs = [_bf(mm) for mm in mms]
                dms = []
                for r in range(0, SSM_R, 2):
                    pair = slice(hss[r].start, hss[r + 1].stop)
                    xp, dyp = xdt_b[:, pair], dy_b[:, pair]
                    dmp = _dot_nt(dyp, jnp.concatenate([jnp.where(low_half, xp, 0), jnp.where(low_half, 0, xp)], axis=0))
                    dms += [dmp[:, :BLOCK], dmp[:, BLOCK:]]
                    dxd_s[:, pair] = _dot_tn(jnp.concatenate([mmbs[r], mmbs[r + 1]], axis=0),
                                             jnp.concatenate([jnp.where(low_half, dyp, 0), jnp.where(low_half, 0, dyp)], axis=0))
                dcb = sum(dms[r] * lms[r] for r in range(SSM_R))
                wms = [dms[r] * mms[r] for r in range(SSM_R)]
                antis = [_bf(wm - wm.T) for wm in wms]
                for r in range(SSM_R):
                    dacum = dacum + _dot(antis[r], (lane == g * SSM_R + r).astype(BF))
                dcbb = _bf(dcb)
                dbc_s[:, g * SSM_N:(g + 1) * SSM_N] = dbg + _dot_tn(dcbb, cg)
                dbc_s[:, SSM_G * SSM_N + g * SSM_N:SSM_G * SSM_N + (g + 1) * SSM_N] = dcg + _dot(dcbb, bg)
            dxdt = dxdt_s[...] + dxd_s[...]
            dxs = dy * dsk_ref[...] + dxdt * dt_x
            red = _group_sum(jnp.concatenate(
                [dxdt * xs, jnp.concatenate(dacc_x, axis=1),
                 jnp.broadcast_to(jnp.concatenate(dlast_x, axis=1), (8, SSM_W))], axis=0), e_ref[...])
            row = lax.broadcasted_iota(jnp.int32, (BLOCK, SSM_HEADS), 0)
            dacum = dacum + red[BLOCK:2 * BLOCK] + jnp.where(row == BLOCK - 1, red[2 * BLOCK:2 * BLOCK + 1], 0.0)
            ddta = _exact_left(_triu().astype(BF), dacum)
            ddt = red[:BLOCK] + ddta * a
            galog_ref[...] += jnp.sum(ddta * dt, axis=0, keepdims=True) * a
            du = ddt * _sig(u)
            ddt_ref[rs, :] = _bf(du)
            gdtb_ref[...] += jnp.sum(du, axis=0, keepdims=True)
            dconv = jnp.concatenate([dxs, dbc_s[...]], axis=1) * _dsilu(conv, sg, xact)
            gcb_ref[...] += jnp.sum(dconv, axis=0, keepdims=True)
            ext2 = jnp.concatenate([dconv, nhead[...]], axis=0)
            ahead = [pltpu.roll(ext2, BLOCK + 8 - (CONV_K - 1 - j), axis=0)[0:BLOCK] if j < CONV_K - 1 else dconv
                     for j in range(CONV_K)]
            dx_ref[rs, :] = _bf(sum(ahead[j] * cw_ref[j:j + 1, :] for j in range(CONV_K)))
            xraw = x_ref[rs, :]
            gcw_ref[...] += jnp.concatenate([jnp.sum(ahead[j] * xraw, axis=0, keepdims=True) for j in range(CONV_K)], axis=0)
            nhead[...] = dconv[0:8]

        i = pl.program_id(0)

        @pl.when(i == 0)
        def _():
            for ref in (dh, nhead, gdskx, gcw_ref, gcb_ref, gdtb_ref, galog_ref, gdsk_ref):
                ref[...] = jnp.zeros_like(ref)

        for j in reversed(range(ch)):
            chunk_bwd(j)

        @pl.when(i == nsteps - 1)
        def _():
            gdsk_ref[...] = _group_sum(jnp.broadcast_to(gdskx[...], (8, SSM_W)), e_ref[...])[0:1]

    chunk = lambda w: pl.BlockSpec((rows, w), lambda i: (nsteps - 1 - i, 0))
    sd = jax.ShapeDtypeStruct
    return pl.pallas_call(
        body, name="ssd_bwd", grid=(nsteps,),
        in_specs=[chunk(XBC_W), chunk(XBC_W),
                  chunk(SSM_HEADS), _full((CONV_K, XBC_W)), _full((1, SSM_HEADS)),
                  _full((1, SSM_HEADS)), _full((1, SSM_W)), _full((SSM_W, SSM_HEADS)), _full((3 * SSM_HEADS, SSM_W)),
                  pl.BlockSpec((ch, SSM_N, SSM_W), lambda i: (nsteps - 1 - i, 0, 0)), chunk(SSM_W)],
        out_specs=[chunk(XBC_W), chunk(SSM_HEADS), _full((CONV_K, XBC_W)), _full((1, XBC_W)),
                   _full((1, SSM_HEADS)), _full((1, SSM_HEADS)), _full((1, SSM_HEADS))],
        out_shape=[sd((s, XBC_W), BF), sd((s, SSM_HEADS), BF), sd((CONV_K, XBC_W), F32), sd((1, XBC_W), F32),
                   sd((1, SSM_HEADS), F32), sd((1, SSM_HEADS), F32), sd((1, SSM_HEADS), F32)],
        scratch_shapes=[pltpu.VMEM((SSM_N, SSM_W), F32), pltpu.VMEM((8, XBC_W), F32),
                        pltpu.VMEM((1, SSM_W), F32), pltpu.VMEM((BLOCK, SSM_W), F32),
                        pltpu.VMEM((BLOCK, 2 * SSM_G * SSM_N), F32), pltpu.VMEM((BLOCK, SSM_W), F32)],
        compiler_params=_params(dimension_semantics=("arbitrary",)),
    )(xbc, conv_all, dt_raw, conv_w, dt_bias, a_log, dsk_x, e_mat, e3t, hprev_all, dy_all)


def _dh(x, dout, norm_w, scale, dsegs, w_t, tm=256):
    s = x.shape[0]

    def body(x_ref, dout_ref, nw_ref, sc_ref, *rest):
        d_refs, w_hbm = rest[:NSEG], rest[NSEG]
        gx_ref, dshift_ref, dscale_ref, gnw_ref = rest[NSEG + 1:NSEG + 5]
        w_vm, sem = rest[NSEG + 5], rest[NSEG + 6]
        first = pl.program_id(0) == 0
        cps = [pltpu.make_async_copy(w_hbm.at[SEG_OFF[j]:SEG_OFF[j + 1], :], w_vm.at[SEG_OFF[j]:SEG_OFF[j + 1], :], sem.at[j])
               for j in range(NSEG)]

        def tile(waiting):
            dh = None
            for j in range(NSEG):
                if waiting:
                    cps[j].wait()
                part = _dot(d_refs[j][...], w_vm[SEG_OFF[j]:SEG_OFF[j + 1], :])
                dh = part if dh is None else dh + part
            xv = x_ref[...]
            r = lax.rsqrt(jnp.mean(xv * xv, axis=-1, keepdims=True) + EPS)
            xn = xv * r
            nw = nw_ref[...]
            sc1 = 1.0 + sc_ref[...]
            dshift_ref[...] += jnp.sum(dh, axis=0, keepdims=True)
            dhxn = jnp.sum(dh * xn, axis=0, keepdims=True)
            dscale_ref[...] += dhxn * nw
            gnw_ref[...] += dhxn * sc1
            dxn = dh * (nw * sc1)
            gx_ref[...] = dout_ref[...] + r * (dxn - xn * jnp.mean(xn * dxn, axis=-1, keepdims=True))

        @pl.when(first)
        def _():
            for cp in cps:
                cp.start()
            for ref in (dshift_ref, dscale_ref, gnw_ref):
                ref[...] = jnp.zeros_like(ref)
            tile(True)

        @pl.when(jnp.logical_not(first))
        def _():
            tile(False)

    vec = _full((1, D_MODEL))
    sd = jax.ShapeDtypeStruct
    return pl.pallas_call(
        body, name="dh", grid=(s // tm,),
        in_specs=[_rows(tm, D_MODEL), _rows(tm, D_MODEL), vec, vec] + [_rows(tm, w) for w in SEG_W] + [ANY],
        out_specs=[_rows(tm, D_MODEL), vec, vec, vec],
        out_shape=[sd((s, D_MODEL), F32), sd((1, D_MODEL), F32), sd((1, D_MODEL), F32), sd((1, D_MODEL), F32)],
        scratch_shapes=[pltpu.VMEM((IN_W, D_MODEL), BF), pltpu.SemaphoreType.DMA((NSEG,))],
        compiler_params=_params(dimension_semantics=("arbitrary",)),
    )(x, dout, norm_w, scale, *dsegs, w_t)


def _gw_seg(h, dseg, name, tm=1024):
    s, w = dseg.shape
    tn = min(w, 1024)
    tm = min(tm, s)
    nm = s // tm

    def body(h_ref, d_ref, o_ref, acc):
        m = pl.program_id(1)

        @pl.when(m == 0)
        def _():
            acc[...] = jnp.zeros_like(acc)

        acc[...] += _dot_tn(d_ref[...], h_ref[...])

        @pl.when(m == nm - 1)
        def _():
            o_ref[...] = _bf(acc[...])

    return pl.pallas_call(
        body, name=name, grid=(w // tn, nm),
        in_specs=[pl.BlockSpec((tm, D_MODEL), lambda n, m: (m, 0)), pl.BlockSpec((tm, tn), lambda n, m: (m, n))],
        out_specs=pl.BlockSpec((tn, D_MODEL), lambda n, m: (n, 0)),
        out_shape=jax.ShapeDtypeStruct((w, D_MODEL), BF),
        scratch_shapes=[pltpu.VMEM((tn, D_MODEL), F32)],
        compiler_params=_params(dimension_semantics=("arbitrary", "arbitrary")),
    )(h, dseg)


def _gw_in(h, dsegs):
    return [_gw_seg(h, d, "gw_in_%d" % j) for j, d in enumerate(dsegs)]


def _local_step(x, tgt, shift, scale, gate, w_t, rows_fn, norm_w, qnw, knw, rel_bias, sinks,
                conv_w, conv_b, dt_bias, a_log, d_skip, ssm_nw, after_mid=None, after_gw=None):
    oh_t = _bucket_onehot_t()
    bias = _masked_bias(_bias_dense(rel_bias.T, oh_t).reshape(ATTN_HEADS, BLOCK, 2 * BLOCK))
    *segs, h = _inproj(x, norm_w, scale, shift, w_t)
    q, kv, zam, xbc, dtr, gab = segs
    consts = _attn_consts(qnw, knw)
    o_att, lse = _attn_fwd(q, kv, bias, sinks, consts)
    e_mat, e3t = _membership(SSM_W, SSM_P, SSM_HEADS)
    dsk_x = jnp.repeat(d_skip, SSM_P, axis=1)
    ypre, hprev, conv = _ssd_fwd(xbc, dtr, conv_w, conv_b, dt_bias, a_log, dsk_x, e3t)
    (dout, d_o, dzam, dyp, dgab, yag, dy_a, yn, dy_b, merged, dob, g_ssm_nw, dgate, loss) = _mid(
        x, tgt, o_att, zam, ypre, gab, gate, ssm_nw, rows_fn(ypre))
    g_wap = _gw_seg(dy_a, yag, "gw_attn_proj")
    g_wsp = _gw_seg(dy_b, yn, "gw_ssm_proj")
    g_wout = _gw_seg(dob, merged, "gw_out")
    zero = after_mid(g_wap, g_wsp, g_wout) if after_mid is not None else 0.0
    dq, dkv, dss, g_qnw, g_knw, g_sinks = _attn_bwd(q, kv, bias, sinks + zero, consts, o_att, lse, d_o)
    g_rel = _bias_grad(dss.reshape(ATTN_HEADS, BLOCK * 2 * BLOCK), oh_t).T
    dxbc, ddt, g_cw, g_cb, g_dtb, g_alog, g_dsk = _ssd_bwd(
        xbc, conv, dtr, conv_w, dt_bias, a_log, dsk_x, e_mat, e3t, hprev, dyp)
    dsegs = (dq, dkv, dzam, dxbc, ddt, dgab)
    g_ws = _gw_in(h, dsegs)
    zero = after_gw(g_ws) if after_gw is not None else 0.0
    gx, dshift, dscale, g_nw = _dh(x, dout, norm_w + zero, scale, dsegs, w_t)
    return dict(loss=loss, grad_x=gx, dmod=jnp.concatenate([dshift, dscale, dgate], axis=1), g_ws=g_ws,
                g_wap=g_wap, g_wsp=g_wsp, g_wout=g_wout, g_norm_w=g_nw, g_qnw=g_qnw, g_knw=g_knw, g_rel=g_rel,
                g_sinks=g_sinks, g_conv_w=g_cw, g_conv_b=g_cb, g_dt_bias=g_dtb, g_a_log=g_alog, g_d_skip=g_dsk,
                g_ssm_nw=g_ssm_nw)


def _me():
    return lax.axis_index("x"), lax.axis_index("y"), lax.axis_index("c")


def _flip(v, bit):
    return 1 - v if bit else v


def _ag_direct(v, name):
    def body(v_ref, out_ref, send_sems, recv_sems, local_sem):
        x, y, c = _me()
        me = 4 * x + 2 * y + c
        mine = pltpu.make_async_copy(v_ref, out_ref.at[me], local_sem)
        mine.start()
        peers = [(_flip(x, k >> 2 & 1), _flip(y, k >> 1 & 1), _flip(c, k & 1)) for k in range(1, N_DEV)]
        sends = [pltpu.make_async_remote_copy(
            src_ref=v_ref, dst_ref=out_ref.at[me], send_sem=send_sems.at[j], recv_sem=recv_sems.at[j],
            device_id=p, device_id_type=MESH) for j, p in enumerate(peers)]
        for cp in sends:
            cp.start()
        for j, (px, py, pc) in enumerate(peers):
            pltpu.make_async_remote_copy(
                src_ref=v_ref, dst_ref=out_ref.at[4 * px + 2 * py + pc], send_sem=send_sems.at[j],
                recv_sem=recv_sems.at[j], device_id=(px, py, pc), device_id_type=MESH).wait_recv()
        for cp in sends:
            cp.wait_send()
        mine.wait()

    vm = pl.BlockSpec(memory_space=pltpu.VMEM)
    return pl.pallas_call(
        body, name=name, out_shape=jax.ShapeDtypeStruct((N_DEV,) + v.shape, v.dtype),
        in_specs=[vm], out_specs=vm,
        scratch_shapes=[pltpu.SemaphoreType.DMA((N_DEV - 1,)), pltpu.SemaphoreType.DMA((N_DEV - 1,)),
                        pltpu.SemaphoreType.DMA],
        compiler_params=_params(),
    )(v)


def _gather_mod(v, w_ada, b_piece):
    ncols = w_ada.shape[1]

    def body(v_ref, w_ref, b_ref, rows_ref, mods_ref, piece, send_sems, recv_sems, local_sems):
        x, y, c = _me()
        me = 4 * x + 2 * y + c
        peers = _peers(x, y, c)

        def exchange(src, dst, rnd):
            mine = pltpu.make_async_copy(src, dst.at[me], local_sems.at[rnd])
            mine.start()
            sends = [pltpu.make_async_remote_copy(
                src_ref=src, dst_ref=dst.at[me], send_sem=send_sems.at[rnd, j], recv_sem=recv_sems.at[rnd, j],
                device_id=p, device_id_type=MESH) for j, p in enumerate(peers)]
            for cp in sends:
                cp.start()
            for j, (px, py, pc) in enumerate(peers):
                pltpu.make_async_remote_copy(
                    src_ref=src, dst_ref=dst.at[4 * px + 2 * py + pc], send_sem=send_sems.at[rnd, j],
                    recv_sem=recv_sems.at[rnd, j], device_id=(px, py, pc), device_id_type=MESH).wait_recv()
            for cp in sends:
                cp.wait_send()
            mine.wait()

        exchange(v_ref, rows_ref, 0)
        c_all = rows_ref[:, 0, :D_MODEL]
        piece[...] = _dot(_bf(_silu(c_all)), _bf(w_ref[...])) + b_ref[...]
        exchange(piece, mods_ref, 1)

    vm = pl.BlockSpec(memory_space=pltpu.VMEM)
    return pl.pallas_call(
        body, name="gather_mod",
        out_shape=(jax.ShapeDtypeStruct((N_DEV,) + v.shape, F32), jax.ShapeDtypeStruct((N_DEV, N_DEV, ncols), F32)),
        in_specs=[vm, vm, vm], out_specs=(vm, vm),
        scratch_shapes=[pltpu.VMEM((N_DEV, ncols), F32), pltpu.SemaphoreType.DMA((2, N_DEV - 1)),
                        pltpu.SemaphoreType.DMA((2, N_DEV - 1)), pltpu.SemaphoreType.DMA((2,))],
        compiler_params=_params(),
    )(v, w_ada, b_piece)


def _ag_two_level(v, name):
    def body(v_ref, out_ref, token, send_sems, recv_sems, local_sem):
        token[...] = jnp.zeros_like(token)
        x, y, c = _me()
        me, sibling = (x, y, c), (x, y, 1 - c)
        chips = [(1 - x, y), (x, 1 - y), (1 - x, 1 - y)]

        def slot(px, py, pc):
            return out_ref.at[4 * px + 2 * py + pc]

        def copy(k, block, to, src=None):
            return pltpu.make_async_remote_copy(
                src_ref=slot(*block) if src is None else src, dst_ref=slot(*block),
                send_sem=send_sems.at[k], recv_sem=recv_sems.at[k], device_id=to, device_id_type=MESH)

        mine = pltpu.make_async_copy(v_ref, slot(*me), local_sem)
        mine.start()
        first = [copy(0, me, sibling, src=v_ref)]
        first += [copy(1 + j, me, (*chip, c), src=v_ref) for j, chip in enumerate(chips)]
        for cp in first:
            cp.start()
        passed = [copy(4 + j, (*chip, c), sibling) for j, chip in enumerate(chips)]
        for j, chip in enumerate(chips):
            copy(1 + j, (*chip, c), me).wait_recv()
            passed[j].start()
        copy(0, sibling, me).wait_recv()
        for j, chip in enumerate(chips):
            copy(4 + j, (*chip, 1 - c), me).wait_recv()
        for cp in first + passed:
            cp.wait_send()
        mine.wait()

    out, token = pl.pallas_call(
        body, name=name,
        out_shape=(jax.ShapeDtypeStruct((N_DEV,) + v.shape, v.dtype), jax.ShapeDtypeStruct((8, 128), v.dtype)),
        in_specs=[ANY], out_specs=(ANY, pl.BlockSpec(memory_space=pltpu.VMEM)),
        scratch_shapes=[pltpu.SemaphoreType.DMA((7,)), pltpu.SemaphoreType.DMA((7,)), pltpu.SemaphoreType.DMA],
        compiler_params=_params(),
    )(v)
    return out, token[0:1, 0:1]


HBM = pl.BlockSpec(memory_space=pltpu.HBM)
SEM = pl.BlockSpec(memory_space=pltpu.SEMAPHORE)
EFFECT = pltpu.SideEffectType.DATAFLOW_SIDE_EFFECTING


def _peers(x, y, c):
    return [(_flip(x, k >> 2 & 1), _flip(y, k >> 1 & 1), _flip(c, k & 1)) for k in range(1, N_DEV)]


def _exchange_start(src, land, gather, name):
    def body(src_ref, land_ref, send_sems, recv_sems, src_thru, land_thru, token):
        x, y, c = _me()
        me = 4 * x + 2 * y + c
        for j, (px, py, pc) in enumerate(_peers(x, y, c)):
            pltpu.make_async_remote_copy(
                src_ref=src_ref if gather else src_ref.at[4 * px + 2 * py + pc], dst_ref=land_ref.at[me],
                send_sem=send_sems.at[j], recv_sem=recv_sems.at[j], device_id=(px, py, pc), device_id_type=MESH).start()
        token[...] = jnp.zeros_like(token)

    sems = pltpu.SemaphoreType.DMA((N_DEV - 1,))
    out = pl.pallas_call(
        body, name=name,
        out_shape=(sems, sems, pltpu.HBM(src.shape, src.dtype), pltpu.HBM(land.shape, land.dtype),
                   jax.ShapeDtypeStruct((8, 128), F32)),
        in_specs=(HBM, HBM), out_specs=(SEM, SEM, HBM, HBM, pl.BlockSpec(memory_space=pltpu.VMEM)),
        input_output_aliases={0: 2, 1: 3},
        compiler_params=pltpu.CompilerParams(has_side_effects=EFFECT),
    )(pltpu.with_memory_space_constraint(src, pltpu.HBM), pltpu.with_memory_space_constraint(land, pltpu.HBM))
    return out[:4], out[4][0, 0]


def _exchange_wait(started, after, gather, name):
    send_sems, recv_sems, src_thru, land_thru = started

    def body(src_ref, land_ref, send_sems, recv_sems, after_ref, src_dead, got_ref):
        x, y, c = _me()
        for j, (px, py, pc) in enumerate(_peers(x, y, c)):
            pid = 4 * px + 2 * py + pc
            cp = pltpu.make_async_remote_copy(
                src_ref=src_ref if gather else src_ref.at[pid], dst_ref=land_ref.at[pid],
                send_sem=send_sems.at[j], recv_sem=recv_sems.at[j], device_id=(px, py, pc), device_id_type=MESH)
            cp.wait_send()
            cp.wait_recv()

    return pl.pallas_call(
        body, name=name,
        out_shape=(pltpu.HBM(src_thru.shape, src_thru.dtype), pltpu.HBM(land_thru.shape, land_thru.dtype)),
        in_specs=(HBM, HBM, SEM, SEM, ANY), out_specs=(HBM, HBM), input_output_aliases={0: 0, 1: 1},
        compiler_params=pltpu.CompilerParams(has_side_effects=EFFECT),
    )(src_thru, land_thru, send_sems, recv_sems, after)[1]


def _silu(a):
    return a * _sig(a)


def _gw_ada(c_all, dmod_piece):
    def body(c_ref, d_ref, o_ref):
        o_ref[...] = _dot_tn(_bf(_silu(c_ref[...])), _bf(d_ref[...]))

    return pl.pallas_call(
        body, name="gw_ada", out_shape=jax.ShapeDtypeStruct((c_all.shape[1], dmod_piece.shape[1]), F32),
        compiler_params=_params(),
    )(c_all, dmod_piece)


def _adam(parts, w, m, v, name):
    k, r, n = parts.shape
    if r <= 256 or r % 256 == 0:
        tr, tn = min(r, 256), n
    else:
        tr, tn = r, 256
    assert r % tr == 0 and n % tn == 0

    def body(p_ref, w_ref, m_ref, v_ref, g_ref, d_ref, nm_ref, nv_ref):
        g = p_ref[0].astype(F32)
        for j in range(1, k):
            g = g + p_ref[j].astype(F32)
        g_ref[...] = g
        d_ref[...], nm_ref[...], nv_ref[...] = _adam_math(g, w_ref[...], m_ref[...], v_ref[...])

    blk = pl.BlockSpec((tr, tn), lambda i, j: (i, j))
    return pl.pallas_call(
        body, name=name, grid=(r // tr, n // tn),
        in_specs=[pl.BlockSpec((k, tr, tn), lambda i, j: (0, i, j)), blk, blk, blk],
        out_specs=[blk, blk, blk, blk],
        out_shape=[jax.ShapeDtypeStruct((r, n), F32)] * 4,
        compiler_params=_params(dimension_semantics=("arbitrary", "arbitrary")),
    )(parts, w, m, v)


def _adam_math(g, w, m, v):
    m_new = ADAM_B1 * m + (1.0 - ADAM_B1) * g
    v_new = ADAM_B2 * v + (1.0 - ADAM_B2) * jnp.square(g)
    m_hat = m_new / (1.0 - ADAM_B1 ** ADAM_STEP)
    v_hat = v_new / (1.0 - ADAM_B2 ** ADAM_STEP)
    return -ADAM_LR * (m_hat / (jnp.sqrt(v_hat) + ADAM_EPS) + ADAM_WD * w), m_new, v_new


_SMALL = (("b_ada", 3 * D_MODEL), ("norm_w", D_MODEL), ("q_norm_w", HEAD_DIM), ("k_norm_w", HEAD_DIM),
          ("rel_bias", REL_BUCKETS * ATTN_HEADS), ("sinks", ATTN_HEADS), ("conv_b", XBC_W), ("dt_bias", SSM_HEADS),
          ("a_log", SSM_HEADS), ("d_skip", SSM_HEADS), ("ssm_norm_w", SSM_W))
_SLOT = tuple(-(-n // 128) * 128 for _, n in _SMALL)
_SLOT_OFF = tuple(int(o) for o in np.cumsum((0,) + _SLOT))
_LOSS_OFF = _SLOT_OFF[-1]
_CW_OFF = _LOSS_OFF + 128
_PACK_N = _CW_OFF + CONV_K * XBC_W


def _pack_partials(small, loss, g_conv_w):
    parts = []
    for (name, n), slot in zip(_SMALL, _SLOT):
        parts.append(small[name].reshape(1, n))
        if slot > n:
            parts.append(jnp.zeros((1, slot - n), F32))
    parts += [loss.reshape(1, 1), jnp.zeros((1, 127), F32), g_conv_w.reshape(1, CONV_K * XBC_W)]
    return jnp.concatenate(parts, axis=1)


def _adam_small(pack_all, w, m, v):
    names = [name for name, _ in _SMALL]

    def body(p_ref, *rest):
        ins, outs = rest[:3 * len(names)], rest[3 * len(names):]

        def total(off, n):
            g = p_ref[0, :, off:off + n]
            for d in range(1, N_DEV):
                g = g + p_ref[d, :, off:off + n]
            return g

        for j, (name, n) in enumerate(_SMALL):
            g = total(_SLOT_OFF[j], n)
            delta, m_new, v_new = _adam_math(g, ins[3 * j][...], ins[3 * j + 1][...], ins[3 * j + 2][...])
            outs[4 * j][...] = g
            outs[4 * j + 1][...] = delta
            outs[4 * j + 2][...] = m_new
            outs[4 * j + 3][...] = v_new
        outs[-1][...] = total(_LOSS_OFF, 1)

    flat = []
    for name, n in _SMALL:
        flat += [w[name].reshape(1, n), m[name].reshape(1, n), v[name].reshape(1, n)]
    out_shape = [jax.ShapeDtypeStruct((1, n), F32) for _, n in _SMALL for _ in range(4)] + [jax.ShapeDtypeStruct((1, 1), F32)]
    out = pl.pallas_call(body, name="adam_small", out_shape=out_shape, compiler_params=_params())(pack_all, *flat)
    res = {name: [out[4 * j + t].reshape(w[name].shape) for t in range(4)] for j, name in enumerate(names)}
    return res, out[-1]


WEIGHTS = ("w_ada", "b_ada", "norm_w", "w_in", "q_norm_w", "k_norm_w", "rel_bias", "sinks", "conv_w", "conv_b",
           "dt_bias", "a_log", "d_skip", "ssm_norm_w", "w_attn_proj", "w_ssm_proj", "w_out")


def kernel(x, c, w_ada, b_ada, norm_w, w_in, q_norm_w, k_norm_w, rel_bias, sinks, conv_w, conv_b, dt_bias, a_log, d_skip, ssm_norm_w, w_attn_proj, w_ssm_proj, w_out, loss_target, m_w_ada, m_b_ada, m_norm_w, m_w_in, m_q_norm_w, m_k_norm_w, m_rel_bias, m_sinks, m_conv_w, m_conv_b, m_dt_bias, m_a_log, m_d_skip, m_ssm_norm_w, m_w_attn_proj, m_w_ssm_proj, m_w_out, v_w_ada, v_b_ada, v_norm_w, v_w_in, v_q_norm_w, v_k_norm_w, v_rel_bias, v_sinks, v_conv_w, v_conv_b, v_dt_bias, v_a_log, v_d_skip, v_ssm_norm_w, v_w_attn_proj, v_w_ssm_proj, v_w_out):
    w = dict(w_ada=w_ada, b_ada=b_ada, norm_w=norm_w, w_in=w_in, q_norm_w=q_norm_w, k_norm_w=k_norm_w,
             rel_bias=rel_bias, sinks=sinks, conv_w=conv_w, conv_b=conv_b, dt_bias=dt_bias, a_log=a_log,
             d_skip=d_skip, ssm_norm_w=ssm_norm_w, w_attn_proj=w_attn_proj, w_ssm_proj=w_ssm_proj, w_out=w_out)
    m = dict(w_ada=m_w_ada, b_ada=m_b_ada, norm_w=m_norm_w, w_in=m_w_in, q_norm_w=m_q_norm_w, k_norm_w=m_k_norm_w,
             rel_bias=m_rel_bias, sinks=m_sinks, conv_w=m_conv_w, conv_b=m_conv_b, dt_bias=m_dt_bias, a_log=m_a_log,
             d_skip=m_d_skip, ssm_norm_w=m_ssm_norm_w, w_attn_proj=m_w_attn_proj, w_ssm_proj=m_w_ssm_proj, w_out=m_w_out)
    v = dict(w_ada=v_w_ada, b_ada=v_b_ada, norm_w=v_norm_w, w_in=v_w_in, q_norm_w=v_q_norm_w, k_norm_w=v_k_norm_w,
             rel_bias=v_rel_bias, sinks=v_sinks, conv_w=v_conv_w, conv_b=v_conv_b, dt_bias=v_dt_bias, a_log=v_a_log,
             d_skip=v_d_skip, ssm_norm_w=v_ssm_norm_w, w_attn_proj=v_w_attn_proj, w_ssm_proj=v_w_ssm_proj, w_out=v_w_out)
    me = 4 * lax.axis_index("x") + 2 * lax.axis_index("y") + lax.axis_index("c")
    ada_n = w_ada.shape[2]
    in_n = w_in.shape[2]
    cw_n = conv_w.shape[2]

    b_piece = lax.dynamic_slice_in_dim(b_ada, me * ada_n, ada_n, axis=1)
    first, mod_all = _gather_mod(jnp.concatenate([c, conv_w[0].reshape(1, CONV_K * cw_n)], axis=1), w_ada[0], b_piece)
    first = first[:, 0]
    c_all = first[:, :D_MODEL]
    conv_w_full = first[:, D_MODEL:].reshape(N_DEV, CONV_K, cw_n).transpose(1, 0, 2).reshape(CONV_K, XBC_W)
    mod = lax.dynamic_index_in_dim(mod_all, me, axis=1, keepdims=False).reshape(1, 3 * D_MODEL)
    shift, scale, gate = mod[:, :D_MODEL], mod[:, D_MODEL:2 * D_MODEL], mod[:, 2 * D_MODEL:]

    w_t, zero = _ag_two_level(w_in[0].T.astype(BF), "ag_w_in")
    w_t = w_t.reshape(N_DEV * in_n, D_MODEL)

    def with_mine(blocks, mine):
        return lax.dynamic_update_index_in_dim(lax.empty(blocks, mine.dtype), mine, me, axis=0)

    rows = jnp.concatenate([w_attn_proj[0], w_ssm_proj[0], w_out[0]], axis=0).astype(BF) + zero
    r_ap, r_sp = w_attn_proj.shape[1], w_ssm_proj.shape[1]
    rows_started, zero = _exchange_start(rows, with_mine((N_DEV,) + rows.shape, rows), True, "ag_rows_start")

    def rows_fn(after):
        return _exchange_wait(rows_started, after, True, "ag_rows_wait")

    started = {}

    def send_blocks(key, g, name):
        started[key], zero = _exchange_start(
            g, with_mine(g.shape, lax.dynamic_index_in_dim(g, me, axis=0, keepdims=False)), False, name)
        return zero

    def after_mid(g_wap, g_wsp, g_wout):
        return send_blocks("rows", jnp.concatenate(
            [g_wap.reshape(N_DEV, r_ap, D_MODEL), g_wsp.reshape(N_DEV, r_sp, D_MODEL),
             g_wout.reshape(N_DEV, r_ap, D_MODEL)], axis=1), "rs_rows_start")

    def after_gw(g_ws):
        return send_blocks("in", jnp.concatenate(g_ws, axis=0).reshape(N_DEV, in_n, D_MODEL), "rs_in_start")

    r = _local_step(x[0], loss_target[0], shift, scale + zero, gate, w_t, rows_fn, norm_w, q_norm_w, k_norm_w,
                    rel_bias, sinks, conv_w_full, conv_b, dt_bias, a_log, d_skip, ssm_norm_w, after_mid, after_gw)

    small = dict(b_ada=r["dmod"], norm_w=r["g_norm_w"], q_norm_w=r["g_qnw"], k_norm_w=r["g_knw"], rel_bias=r["g_rel"],
                 sinks=r["g_sinks"], conv_b=r["g_conv_b"], dt_bias=r["g_dt_bias"], a_log=r["g_a_log"],
                 d_skip=r["g_d_skip"], ssm_norm_w=r["g_ssm_nw"])
    pack_all = _ag_direct(_pack_partials(small, r["loss"], r["g_conv_w"]), "ag_small")
    res, loss = _adam_small(pack_all, w, m, v)
    loss = loss[0, 0]
    cw_parts = pack_all[:, 0, _CW_OFF:].reshape(N_DEV, CONV_K, XBC_W)
    cw_mine = lax.dynamic_slice_in_dim(cw_parts, me * cw_n, cw_n, axis=2)
    res["conv_w"] = [a[None] for a in _adam(cw_mine, conv_w[0], m_conv_w[0], v_conv_w[0], "adam_conv_w")]

    dmod_piece = lax.dynamic_slice_in_dim(pack_all[:, 0, :3 * D_MODEL], me * ada_n, ada_n, axis=1)
    g_ada = _gw_ada(c_all, dmod_piece)
    res["w_ada"] = [a[None] for a in _adam(g_ada[None], w_ada[0], m_w_ada[0], v_w_ada[0], "adam_w_ada")]

    cat = lambda d: jnp.concatenate([d["w_attn_proj"][0], d["w_ssm_proj"][0], d["w_out"][0]], axis=0)
    rows_res = _adam(_exchange_wait(started["rows"], g_ada, False, "rs_rows_wait"), cat(w), cat(m), cat(v), "adam_w_rows")
    res["w_in"] = [a.T[None] for a in _adam(_exchange_wait(started["in"], rows_res[0], False, "rs_in_wait"),
                                            w_in[0].T, m_w_in[0].T, v_w_in[0].T, "adam_w_in")]
    res["w_attn_proj"] = [a[None, :r_ap] for a in rows_res]
    res["w_ssm_proj"] = [a[None, r_ap:r_ap + r_sp] for a in rows_res]
    res["w_out"] = [a[None, r_ap + r_sp:] for a in rows_res]

    outs = [loss, r["grad_x"][None]]
    for j in range(4):
        outs += [res[name][j] for name in WEIGHTS]
    return tuple(outs)
```

```python
import math

import numpy as np
import jax
import jax.numpy as jnp
from jax import lax
from jax.experimental import pallas as pl
from jax.experimental.pallas import tpu as pltpu

F32 = jnp.float32
BF = jnp.bfloat16
HI = lax.Precision.HIGHEST

D_MODEL = 1024
ATTN_HEADS = 16
KV_HEADS = 4
GRP = ATTN_HEADS // KV_HEADS
HEAD_DIM = 64
ATTN_W = ATTN_HEADS * HEAD_DIM
KV_W = KV_HEADS * HEAD_DIM
BLOCK = 128
REL_BUCKETS = 32
REL_MAX_DIST = 128
SSM_W = 2048
SSM_P = 64
SSM_HEADS = 32
SSM_G = 4
SSM_R = 8
SSM_N = 128
CONV_K = 4
XBC_W = SSM_W + 2 * SSM_G * SSM_N
SEG_W = (ATTN_W, 2 * KV_W, ATTN_W + SSM_W, XBC_W, SSM_HEADS, 2 * D_MODEL)
NSEG = len(SEG_W)
SEG_OFF = tuple(int(v) for v in np.cumsum((0,) + SEG_W))
IN_W = SEG_OFF[-1]
GATE_SEGS = (2, 5)
EPS = 1e-6
N_DEV = 8
ADAM_LR, ADAM_B1, ADAM_B2, ADAM_EPS, ADAM_WD, ADAM_STEP = 0.001, 0.9, 0.999, 1e-08, 0.01, 10
VMEM_LIMIT = 60 * 1024 * 1024
MESH = pl.DeviceIdType.MESH
ANY = pl.BlockSpec(memory_space=pl.ANY)


def _dot(a, b, precision=None):
    return jnp.dot(a, b, preferred_element_type=F32, precision=precision)


def _dot_nt(a, b, precision=None):
    return lax.dot_general(a, b, (((1,), (1,)), ((), ())), preferred_element_type=F32, precision=precision)


def _dot_tn(a, b, precision=None):
    return lax.dot_general(a, b, (((0,), (0,)), ((), ())), preferred_element_type=F32, precision=precision)


def _bf(a):
    return a.astype(BF)


def _sig(a):
    return 0.5 * jnp.tanh(0.5 * a) + 0.5


def _params(**kw):
    return pltpu.CompilerParams(vmem_limit_bytes=VMEM_LIMIT, **kw)


def _full(shape):
    nd = len(shape)
    return pl.BlockSpec(shape, lambda i: (0,) * nd)


def _rows(tm, w):
    return pl.BlockSpec((tm, w), lambda i: (i, 0))


def _inproj(x, norm_w, scale, shift, w_t, tm=256):
    s = x.shape[0]

    def body(x_ref, nw_ref, sc_ref, sh_ref, w_hbm, *rest):
        outs, h_ref, w_vm, sem = rest[:NSEG], rest[NSEG], rest[NSEG + 1], rest[NSEG + 2]
        first = pl.program_id(0) == 0
        cps = [pltpu.make_async_copy(w_hbm.at[SEG_OFF[j]:SEG_OFF[j + 1], :], w_vm.at[SEG_OFF[j]:SEG_OFF[j + 1], :], sem.at[j])
               for j in range(NSEG)]

        def tile(waiting):
            xv = x_ref[...]
            r = lax.rsqrt(jnp.mean(xv * xv, axis=-1, keepdims=True) + EPS)
            h = xv * r * (nw_ref[...] * (1.0 + sc_ref[...])) + sh_ref[...]
            hb = _bf(h)
            h_ref[...] = hb
            for j in range(NSEG):
                if waiting:
                    cps[j].wait()
                outs[j][...] = _dot_nt(hb, w_vm[SEG_OFF[j]:SEG_OFF[j + 1], :]).astype(outs[j].dtype)

        @pl.when(first)
        def _():
            for cp in cps:
                cp.start()
            tile(True)

        @pl.when(jnp.logical_not(first))
        def _():
            tile(False)

    vec = _full((1, D_MODEL))
    return pl.pallas_call(
        body, name="inproj", grid=(s // tm,),
        in_specs=[_rows(tm, D_MODEL), vec, vec, vec, ANY],
        out_specs=[_rows(tm, w) for w in SEG_W] + [_rows(tm, D_MODEL)],
        out_shape=[jax.ShapeDtypeStruct((s, w), BF if j in GATE_SEGS else F32) for j, w in enumerate(SEG_W)]
                  + [jax.ShapeDtypeStruct((s, D_MODEL), BF)],
        scratch_shapes=[pltpu.VMEM((IN_W, D_MODEL), BF), pltpu.SemaphoreType.DMA((NSEG,))],
        compiler_params=_params(dimension_semantics=("arbitrary",)),
    )(x, norm_w, scale, shift, w_t)


def _bucket_onehot_t():
    qi = jnp.arange(BLOCK)[:, None]
    kj = jnp.arange(2 * BLOCK)[None, :]
    dist = qi + BLOCK - kj
    n = jnp.maximum(dist, 0)
    max_exact = REL_BUCKETS // 2
    nf = jnp.maximum(n, 1).astype(F32)
    large = max_exact + (jnp.log(nf / max_exact) / math.log(REL_MAX_DIST / max_exact)
                         * (REL_BUCKETS - max_exact)).astype(jnp.int32)
    large = jnp.minimum(large, REL_BUCKETS - 1)
    bucket = jnp.where(n < max_exact, n, large).reshape(1, BLOCK * 2 * BLOCK)
    return (bucket == jnp.arange(REL_BUCKETS)[:, None]).astype(F32)


def _bias_dense(rel_bias_t, oh_t):
    def body(rb_ref, oh_ref, o_ref):
        o_ref[...] = _dot(rb_ref[...], oh_ref[...], HI)

    return pl.pallas_call(
        body, name="bias_dense", out_shape=jax.ShapeDtypeStruct((ATTN_HEADS, BLOCK * 2 * BLOCK), F32),
        compiler_params=_params(),
    )(rel_bias_t, oh_t)


def _bias_grad(ds_sum, oh_t):
    def body(ds_ref, oh_ref, o_ref):
        o_ref[...] = _dot_nt(ds_ref[...], oh_ref[...], HI)

    return pl.pallas_call(
        body, name="bias_grad", out_shape=jax.ShapeDtypeStruct((ATTN_HEADS, REL_BUCKETS), F32),
        compiler_params=_params(),
    )(ds_sum, oh_t)


def _group_sum(a, e):
    hi = _bf(a)
    return _dot(hi, e) + _dot(_bf(a - hi.astype(F32)), e)


def _group_bcast(a, e3t):
    hi = _bf(a)
    r1 = a - hi.astype(F32)
    mid = _bf(r1)
    return _dot(jnp.concatenate([hi, mid, _bf(r1 - mid.astype(F32))], axis=1), e3t)


def _membership(width, group, ngroups):
    e = (jnp.arange(width)[:, None] // group == jnp.arange(ngroups)[None, :]).astype(BF)
    return e, jnp.tile(e.T, (3, 1))


def _fold(width, group):
    return (jnp.arange(width)[:, None] % group == jnp.arange(group)[None, :]).astype(BF)


def _heads_norm(t, w_x, e, e3t):
    r = lax.rsqrt(_dot(_bf(t * t), e) * (1.0 / HEAD_DIM) + EPS)
    r_x = _group_bcast(r, e3t)
    return t * r_x * w_x, r_x


def _heads_norm_bwd(t, r_x, w_x, d, e, e3t):
    wd = d * w_x
    corr = _group_bcast(_dot(_bf(t * wd), e) * (1.0 / HEAD_DIM), e3t)
    return r_x * wd - t * (r_x * r_x * r_x) * corr, jnp.sum(d * t * r_x, axis=0, keepdims=True)


def _stack_heads(a, hk):
    return jnp.concatenate([a[:, (hk * GRP + g) * HEAD_DIM:(hk * GRP + g + 1) * HEAD_DIM] for g in range(GRP)], axis=0)


def _stack_cols(a, hk):
    return jnp.concatenate([a[:, hk * GRP + g:hk * GRP + g + 1] for g in range(GRP)], axis=0)


def _masked_bias(bias):
    qi = jnp.arange(BLOCK)[:, None]
    kj = jnp.arange(2 * BLOCK)[None, :]
    cur_ok = jnp.logical_and(kj >= BLOCK, kj - BLOCK <= qi)
    both_ok = jnp.logical_or(jnp.logical_and(kj < BLOCK, kj > qi), cur_ok)
    return jnp.stack([jnp.where(cur_ok, bias, -1e30), jnp.where(both_ok, bias, -1e30)])


def _attn_consts(qnw, knw):
    eq, eq3t = _membership(ATTN_W, HEAD_DIM, ATTN_HEADS)
    ek, ek3t = _membership(KV_W, HEAD_DIM, ATTN_HEADS)
    return (jnp.tile(qnw, (1, ATTN_HEADS)), jnp.tile(knw, (1, KV_HEADS)), eq, eq3t, ek, ek3t)


def _attn_fwd(q, kv, bias, sinks, consts):
    s = q.shape[0]
    nb = s // BLOCK
    gq = GRP * BLOCK
    bias_t = bias.reshape(2, KV_HEADS, GRP, BLOCK, 2 * BLOCK).transpose(0, 1, 4, 2, 3).reshape(2, KV_HEADS, 2 * BLOCK, gq)
    sink_rows = jnp.repeat(sinks.reshape(KV_HEADS, GRP), BLOCK, axis=1).reshape(KV_HEADS, 1, gq)
    eye = jnp.eye(BLOCK, dtype=BF)

    def body(q_ref, kp_ref, kc_ref, vp_ref, vc_ref, b_ref, bt_ref, sk_ref, skr_ref, eye_ref,
             qw_ref, kw_ref, eq_ref, eq3_ref, ek_ref, ek3_ref, o_ref, lse_ref):
        qn = _bf(_heads_norm(q_ref[...], qw_ref[...], eq_ref[...], eq3_ref[...])[0] * (HEAD_DIM ** -0.5))
        kn = _bf(_heads_norm(jnp.concatenate([kp_ref[...], kc_ref[...]], axis=0), kw_ref[...], ek_ref[...], ek3_ref[...])[0])
        vv = _bf(jnp.concatenate([vp_ref[...], vc_ref[...]], axis=0))
        ones = jnp.ones((2 * BLOCK, HEAD_DIM), BF)
        lses = []
        kss = [slice(hk * HEAD_DIM, (hk + 1) * HEAD_DIM) for hk in range(KV_HEADS)]
        qgs = [_stack_heads(qn, hk) for hk in range(KV_HEADS)]
        sc_ts = [_dot_nt(kn[:, kss[hk]], qgs[hk]) + bt_ref[0, hk] for hk in range(KV_HEADS)]
        m_rows = [jnp.maximum(jnp.max(sc_ts[hk], axis=0, keepdims=True), skr_ref[hk]) for hk in range(KV_HEADS)]
        m8s = [_bf(jnp.broadcast_to(m + jnp.abs(m) * (2.0 ** -7), (8, gq))) for m in m_rows]
        ms = [jnp.concatenate([_dot_nt(eye_ref[...], m8[:, g * BLOCK:(g + 1) * BLOCK])[:, 0:1] for g in range(GRP)], axis=0)
              for m8 in m8s]
        scs = [_dot_nt(qgs[hk], kn[:, kss[hk]]) + b_ref[0, hk * GRP:(hk + 1) * GRP].reshape(gq, 2 * BLOCK)
               for hk in range(KV_HEADS)]
        ps = [_bf(jnp.exp(scs[hk] - ms[hk])) for hk in range(KV_HEADS)]
        pvs = [_dot(ps[hk], jnp.concatenate([vv[:, kss[hk]], ones], axis=1)) for hk in range(KV_HEADS)]
        for hk in range(KV_HEADS):
            m, pv = ms[hk], pvs[hk]
            sink = jnp.concatenate([jnp.full((BLOCK, 1), sk_ref[0, hk * GRP + g], F32) for g in range(GRP)], axis=0)
            den = pv[:, HEAD_DIM:HEAD_DIM + 1] + jnp.exp(sink - m)
            out = pv[:, :HEAD_DIM] * (1.0 / den)
            lse = m + jnp.log(den)
            for g in range(GRP):
                h = hk * GRP + g
                o_ref[:, h * HEAD_DIM:(h + 1) * HEAD_DIM] = out[g * BLOCK:(g + 1) * BLOCK]
                lses.append(lse[g * BLOCK:(g + 1) * BLOCK])
        lse_ref[...] = jnp.concatenate(lses, axis=1)

    cur = lambda w, col=0: pl.BlockSpec((BLOCK, w), lambda i: (i, col))
    prev = lambda w, col=0: pl.BlockSpec((BLOCK, w), lambda i: (jnp.maximum(i - 1, 0), col))
    whole = lambda a: pl.BlockSpec(a.shape, lambda i: (0,) * a.ndim)
    first_or_not = lambda a: pl.BlockSpec((1,) + a.shape[1:], lambda i: (jnp.minimum(i, 1),) + (0,) * (a.ndim - 1))
    return pl.pallas_call(
        body, name="attn_fwd", grid=(nb,),
        in_specs=[cur(ATTN_W), prev(KV_W, 0), cur(KV_W, 0), prev(KV_W, 1), cur(KV_W, 1),
                  first_or_not(bias), first_or_not(bias_t),
                  pl.BlockSpec(memory_space=pltpu.SMEM), whole(sink_rows), whole(eye)] + [_full(c.shape) for c in consts],
        out_specs=[cur(ATTN_W), cur(ATTN_HEADS)],
        out_shape=[jax.ShapeDtypeStruct((s, ATTN_W), F32), jax.ShapeDtypeStruct((s, ATTN_HEADS), F32)],
        compiler_params=_params(dimension_semantics=("arbitrary",)),
    )(q, kv, kv, kv, kv, bias, bias_t, sinks, sink_rows, eye, *consts)


def _conv_taps(xbc, tail):
    ext = jnp.concatenate([tail, xbc], axis=0)
    return [pltpu.roll(ext, CONV_K - 1 - j, axis=0)[8:8 + BLOCK] if j < CONV_K - 1 else xbc for j in range(CONV_K)]


def _softplus(u):
    return jnp.maximum(u, 0.0) + jnp.log(1.0 + jnp.exp(-jnp.abs(u)))


def _tril():
    r = lax.broadcasted_iota(jnp.int32, (BLOCK, BLOCK), 0)
    c = lax.broadcasted_iota(jnp.int32, (BLOCK, BLOCK), 1)
    return r >= c


def _triu():
    r = lax.broadcasted_iota(jnp.int32, (BLOCK, BLOCK), 0)
    c = lax.broadcasted_iota(jnp.int32, (BLOCK, BLOCK), 1)
    return r <= c


def _exact_left(m01, a):
    hi = _bf(a)
    r1 = a - hi.astype(F32)
    mid = _bf(r1)
    return _dot(m01, hi) + _dot(m01, mid) + _dot(m01, _bf(r1 - mid.astype(F32)))


def _ssd_common(conv, dtr, dtb_ref, alog_ref, e3_ref):
    sg = _sig(conv)
    xact = conv * sg
    u = dtr + dtb_ref[...]
    dt = _softplus(u)
    a = -jnp.exp(alog_ref[...])
    trilb = _tril()
    acum = _exact_left(trilb.astype(BF), dt * a) * math.log2(math.e)
    both = _group_bcast(jnp.concatenate([dt, acum], axis=0), e3_ref[...])
    dt_x, acum_x = both[:BLOCK], both[BLOCK:]
    return sg, xact, u, dt, a, trilb, acum, dt_x, acum_x


SSD_CH = 2


def _ssd_fwd(xbc, dt_raw, conv_w, conv_b, dt_bias, a_log, dsk_x, e3t):
    s = xbc.shape[0]
    nc = s // BLOCK
    ch = SSD_CH if nc % SSD_CH == 0 else 1
    rows = ch * BLOCK

    def body(x_ref, tail_ref, dtr_ref, cw_ref, cb_ref, dtb_ref, alog_ref, dsk_ref, e3_ref,
             y_ref, hp_ref, conv_ref, hst, yd_s, yoff_s):
        i = pl.program_id(0)

        @pl.when(i == 0)
        def _():
            hst[...] = jnp.zeros_like(hst)

        for j in range(ch):
            rs = slice(j * BLOCK, (j + 1) * BLOCK)
            tail = jnp.where(i > 0, tail_ref[...], 0.0) if j == 0 else x_ref[j * BLOCK - 8:j * BLOCK, :]
            taps = _conv_taps(x_ref[rs, :], tail)
            conv = cb_ref[...] + sum(taps[t] * cw_ref[t:t + 1, :] for t in range(CONV_K))
            conv_ref[rs, :] = conv
            _, xact, _, _, _, trilb, acum, dt_x, acum_x = _ssd_common(conv, dtr_ref[rs, :], dtb_ref, alog_ref, e3_ref)
            xs = xact[:, :SSM_W]
            acum_t = acum.T
            ea_x = jnp.exp2(acum_x)
            last_x = acum_x[BLOCK - 1:BLOCK, :]
            xdt = xs * dt_x
            xw = xdt * jnp.exp2(last_x - acum_x)
            cd_x = jnp.exp2(last_x)
            hprev = hst[...]
            hp_ref[j] = hprev
            sls = [slice(g * SSM_R * SSM_P, (g + 1) * SSM_R * SSM_P) for g in range(SSM_G)]
            bgs = [_bf(xact[:, SSM_W + g * SSM_N:SSM_W + (g + 1) * SSM_N]) for g in range(SSM_G)]
            cgs = [_bf(xact[:, SSM_W + SSM_G * SSM_N + g * SSM_N:SSM_W + SSM_G * SSM_N + (g + 1) * SSM_N])
                   for g in range(SSM_G)]
            xdt_b, xw_b, hprev_b = _bf(xdt), _bf(xw), _bf(hprev)
            low_half = lax.broadcasted_iota(jnp.int32, (BLOCK, 2 * SSM_P), 1) < SSM_P
            cbs = [_dot_nt(cgs[g], bgs[g]) for g in range(SSM_G)]
            for g in range(SSM_G):
                sl = sls[g]
                yoff_s[:, sl] = _dot(cgs[g], hprev_b[:, sl]) * ea_x[:, sl]
                hst[:, sl] = hprev[:, sl] * cd_x[:, sl] + _dot_tn(bgs[g], xw_b[:, sl])
            for g in range(SSM_G):
                hss = [slice((g * SSM_R + r) * SSM_P, (g * SSM_R + r + 1) * SSM_P) for r in range(SSM_R)]
                mms = [_bf(cbs[g] * jnp.exp2(jnp.where(trilb, acum[:, g * SSM_R + r:g * SSM_R + r + 1]
                                                      - acum_t[g * SSM_R + r:g * SSM_R + r + 1, :], -1e30)))
                       for r in range(SSM_R)]
                for r in range(0, SSM_R, 2):
                    pair = slice(hss[r].start, hss[r + 1].stop)
                    xp = xdt_b[:, pair]
                    rhs = jnp.concatenate([jnp.where(low_half, xp, 0), jnp.where(low_half, 0, xp)], axis=0)
                    yd_s[:, pair] = _dot(jnp.concatenate([mms[r], mms[r + 1]], axis=1), rhs)
            y_ref[rs, :] = yd_s[...] + yoff_s[...] + dsk_ref[...] * xs

    blk = lambda w: pl.BlockSpec((rows, w), lambda i: (i, 0))
    return pl.pallas_call(
        body, name="ssd_fwd", grid=(nc // ch,),
        in_specs=[blk(XBC_W), pl.BlockSpec((8, XBC_W), lambda i: (jnp.maximum(i * (rows // 8) - 1, 0), 0)),
                  blk(SSM_HEADS), _full((CONV_K, XBC_W)), _full((1, XBC_W)), _full((1, SSM_HEADS)),
                  _full((1, SSM_HEADS)), _full((1, SSM_W)), _full((3 * SSM_HEADS, SSM_W))],
        out_specs=[blk(SSM_W), pl.BlockSpec((ch, SSM_N, SSM_W), lambda i: (i, 0, 0)), blk(XBC_W)],
        out_shape=[jax.ShapeDtypeStruct((s, SSM_W), F32), jax.ShapeDtypeStruct((nc, SSM_N, SSM_W), F32),
                   jax.ShapeDtypeStruct((s, XBC_W), F32)],
        scratch_shapes=[pltpu.VMEM((SSM_N, SSM_W), F32), pltpu.VMEM((BLOCK, SSM_W), F32), pltpu.VMEM((BLOCK, SSM_W), F32)],
        compiler_params=_params(dimension_semantics=("arbitrary",)),
    )(xbc, xbc, dt_raw, conv_w, conv_b, dt_bias, a_log, dsk_x, e3t)


def _dsilu(z, sg, silu):
    return sg * (1.0 + (z - silu))


def _mid(x, tgt, o_att, zam, ypre, gab, gate, ssm_nw, rows_all, tm=256):
    s = x.shape[0]
    gw = SSM_W // SSM_G

    r_ap, r_sp = ATTN_W // N_DEV, SSM_W // N_DEV

    def body(x_ref, t_ref, o_ref, zam_ref, yp_ref, gab_ref, gate_ref, nw_ref, rows_h,
             dout_ref, do_ref, dzam_ref, dyp_ref, dgab_ref,
             yag_ref, dya_ref, yn_ref, dyb_ref, mg_ref, dob_ref, gnw_ref, dgate_ref, loss_ref,
             wap_v, wsp_v, wout_v, sem):
        i = pl.program_id(0)

        @pl.when(i == 0)
        def _():
            cps = []
            for d in range(N_DEV):
                for j, (dst, r0, rn) in enumerate(((wap_v, 0, r_ap), (wsp_v, r_ap, r_sp), (wout_v, r_ap + r_sp, r_ap))):
                    cps.append(pltpu.make_async_copy(rows_h.at[d, r0:r0 + rn, :], dst.at[d * rn:(d + 1) * rn, :], sem.at[j]))
            for cp in cps:
                cp.start()
            gnw_ref[...] = jnp.zeros_like(gnw_ref)
            dgate_ref[...] = jnp.zeros_like(dgate_ref)
            loss_ref[...] = jnp.zeros_like(loss_ref)
            for cp in cps:
                cp.wait()

        gate = gate_ref[...]
        nw = nw_ref[...]
        o_att = o_ref[...]
        z_a = zam_ref[:, :ATTN_W].astype(F32)
        s_a = _sig(z_a)
        silu_a = z_a * s_a
        yag = _bf(o_att * silu_a)
        yag_ref[...] = yag
        ypre = yp_ref[...]
        z_m = zam_ref[:, ATTN_W:].astype(F32)
        s_m = _sig(z_m)
        silu_m = z_m * s_m
        yg = ypre * silu_m
        rinv = jnp.concatenate(
            [jnp.broadcast_to(lax.rsqrt(jnp.mean(yg[:, g * gw:(g + 1) * gw] ** 2, axis=-1, keepdims=True) + EPS), (tm, gw))
             for g in range(SSM_G)], axis=1)
        ynr = yg * rinv
        yn = _bf(ynr * nw)
        yn_ref[...] = yn
        y_a = _dot(yag, wap_v[...])
        y_b = _dot(yn, wsp_v[...])
        g_a = _sig(gab_ref[:, :D_MODEL].astype(F32))
        g_b = _sig(gab_ref[:, D_MODEL:].astype(F32))
        merged = _bf(g_a * y_a + g_b * y_b)
        mg_ref[...] = merged
        o = _dot(merged, wout_v[...])
        diff = x_ref[...] + gate * o - t_ref[...]
        loss_ref[...] += (0.5 / D_MODEL) * jnp.sum(diff * diff, axis=(0, 1), keepdims=True)
        dout = diff * (1.0 / D_MODEL)
        dout_ref[...] = dout
        dgate_ref[...] += jnp.sum(dout * o, axis=0, keepdims=True)
        d_o = _bf(dout * gate)
        dob_ref[...] = d_o
        dmerged = _dot_nt(d_o, wout_v[...])
        dy_af = dmerged * g_a
        dy_bf = dmerged * g_b
        dy_a = _bf(dy_af)
        dy_b = _bf(dy_bf)
        dya_ref[...] = dy_a
        dyb_ref[...] = dy_b
        dyag = _dot_nt(dy_a, wap_v[...])
        dyn = _dot_nt(dy_b, wsp_v[...])
        dgab_ref[:, :D_MODEL] = _bf(dy_af * y_a * (1.0 - g_a))
        dgab_ref[:, D_MODEL:] = _bf(dy_bf * y_b * (1.0 - g_b))
        do_ref[...] = dyag * silu_a
        dzam_ref[:, :ATTN_W] = _bf(dyag * o_att * _dsilu(z_a, s_a, silu_a))
        gnw_ref[...] += jnp.sum(dyn * ynr, axis=0, keepdims=True)
        dynw = dyn * nw
        corr = jnp.concatenate(
            [jnp.broadcast_to(jnp.mean((dynw * ynr)[:, g * gw:(g + 1) * gw], axis=-1, keepdims=True), (tm, gw))
             for g in range(SSM_G)], axis=1)
        dyg = rinv * (dynw - ynr * corr)
        dyp_ref[...] = dyg * silu_m
        dzam_ref[:, ATTN_W:] = _bf(dyg * ypre * _dsilu(z_m, s_m, silu_m))

    r1, r2, r3 = _rows(tm, D_MODEL), _rows(tm, SSM_W), _rows(tm, ATTN_W + SSM_W)
    sd = jax.ShapeDtypeStruct
    return pl.pallas_call(
        body, name="mid", grid=(s // tm,),
        in_specs=[r1, r1, r1, r3, r2, r2, _full((1, D_MODEL)), _full((1, SSM_W)), ANY],
        out_specs=[r1, r1, r3, r2, r2, r1, r1, r2, r1, r1, r1,
                   _full((1, SSM_W)), _full((1, D_MODEL)), _full((1, 1))],
        out_shape=[sd((s, D_MODEL), F32), sd((s, ATTN_W), F32), sd((s, ATTN_W + SSM_W), BF), sd((s, SSM_W), F32),
                   sd((s, 2 * D_MODEL), BF),
                   sd((s, ATTN_W), BF), sd((s, D_MODEL), BF), sd((s, SSM_W), BF), sd((s, D_MODEL), BF),
                   sd((s, D_MODEL), BF), sd((s, D_MODEL), BF),
                   sd((1, SSM_W), F32), sd((1, D_MODEL), F32), sd((1, 1), F32)],
        scratch_shapes=[pltpu.VMEM((ATTN_W, D_MODEL), BF), pltpu.VMEM((SSM_W, D_MODEL), BF), pltpu.VMEM((D_MODEL, D_MODEL), BF),
                        pltpu.SemaphoreType.DMA((3,))],
        compiler_params=_params(dimension_semantics=("arbitrary",)),
    )(x, tgt, o_att, zam, ypre, gab, gate, ssm_nw, rows_all)


def _attn_bwd(q, kv, bias, sinks, consts, o_att, lse, d_o):
    s = q.shape[0]
    nb = s // BLOCK
    folds = (_fold(ATTN_W, HEAD_DIM), _fold(KV_W, HEAD_DIM))

    def body(q_ref, kp_ref, kc_ref, vp_ref, vc_ref, b_ref, skv_ref, qw_ref, kw_ref, eq_ref, eq3_ref, ek_ref, ek3_ref,
             fq_ref, fk_ref, o_ref, lse_ref, do_ref,
             dq_ref, dkv_ref, dss_ref, gqw_ref, gkw_ref, gsk_ref, ckn, cv, dqn_s, dkn_s, dv_s, gq_x, gk_x):
        i = pl.program_id(0)
        kw, ek, ek3 = kw_ref[...], ek_ref[...], ek3_ref[...]

        @pl.when(i == 0)
        def _():
            for ref in (ckn, cv, dss_ref, gq_x, gk_x, gsk_ref):
                ref[...] = jnp.zeros_like(ref)

        @pl.when(i < nb)
        def _():
            qw, eq, eq3 = qw_ref[...], eq_ref[...], eq3_ref[...]
            qf = q_ref[...]
            qnf, rq_x = _heads_norm(qf, qw, eq, eq3)
            qn = _bf(qnf * (HEAD_DIM ** -0.5))
            kf = jnp.concatenate([kp_ref[...], kc_ref[...]], axis=0)
            knf, rk_x = _heads_norm(kf, kw, ek, ek3)
            kn = _bf(knf)
            vv = _bf(jnp.concatenate([vp_ref[...], vc_ref[...]], axis=0))
            d_of = do_ref[...]
            d_ob = _bf(d_of)
            lse_all = lse_ref[...]
            delta = _dot(_bf(d_of * o_ref[...]), eq)
            gsk_ref[...] += jnp.sum(-jnp.exp(skv_ref[...] - lse_all) * delta, axis=0, keepdims=True)
            kss = [slice(hk * HEAD_DIM, (hk + 1) * HEAD_DIM) for hk in range(KV_HEADS)]
            qgs = [_stack_heads(qn, hk) for hk in range(KV_HEADS)]
            d_ogs = [_stack_heads(d_ob, hk) for hk in range(KV_HEADS)]
            scs = [_dot_nt(qgs[hk], kn[:, kss[hk]]) + b_ref[0, hk * GRP:(hk + 1) * GRP].reshape(GRP * BLOCK, 2 * BLOCK)
                   for hk in range(KV_HEADS)]
            dps = [_dot_nt(d_ogs[hk], vv[:, kss[hk]]) for hk in range(KV_HEADS)]
            ps = [jnp.exp(scs[hk] - _stack_cols(lse_all, hk)) for hk in range(KV_HEADS)]
            dss = [ps[hk] * (dps[hk] - _stack_cols(delta, hk)) for hk in range(KV_HEADS)]
            pbs = [_bf(p) for p in ps]
            dsbs = [_bf(ds) for ds in dss]
            for hk in range(KV_HEADS):
                dss_ref[hk * GRP:(hk + 1) * GRP] += dss[hk].reshape(GRP, BLOCK, 2 * BLOCK)
            for hk in range(KV_HEADS):
                dv_s[:, kss[hk]] = _dot_tn(pbs[hk], d_ogs[hk])
                dkn_s[:, kss[hk]] = _dot_tn(dsbs[hk], qgs[hk])
            dqns = [_dot(dsbs[hk], kn[:, kss[hk]]) * (HEAD_DIM ** -0.5) for hk in range(KV_HEADS)]
            for hk in range(KV_HEADS):
                for g in range(GRP):
                    h = hk * GRP + g
                    dqn_s[:, h * HEAD_DIM:(h + 1) * HEAD_DIM] = dqns[hk][g * BLOCK:(g + 1) * BLOCK]
            dq, gq = _heads_norm_bwd(qf, rq_x, qw, dqn_s[...], eq, eq3)
            dq_ref[...] = _bf(dq)
            gq_x[...] += gq
            dk, gk = _heads_norm_bwd(kf[:BLOCK], rk_x[:BLOCK], kw, ckn[...] + dkn_s[0:BLOCK, :], ek, ek3)
            dkv_ref[:, :KV_W] = _bf(dk)
            gk_x[...] += gk
            dkv_ref[:, KV_W:] = _bf(cv[...] + dv_s[0:BLOCK, :])
            ckn[...] = dkn_s[BLOCK:2 * BLOCK, :]
            cv[...] = dv_s[BLOCK:2 * BLOCK, :]

        @pl.when(i == nb)
        def _():
            kc = kc_ref[...]
            dk, gk = _heads_norm_bwd(kc, _heads_norm(kc, kw, ek, ek3)[1], kw, ckn[...], ek, ek3)
            dkv_ref[:, :KV_W] = _bf(dk)
            dkv_ref[:, KV_W:] = _bf(cv[...])
            gqw_ref[...] = _group_sum(jnp.broadcast_to(gq_x[...], (8, ATTN_W)), fq_ref[...])[0:1]
            gkw_ref[...] = _group_sum(jnp.broadcast_to(gk_x[...] + gk, (8, KV_W)), fk_ref[...])[0:1]

    last = nb - 1
    cur = lambda w, col=0: pl.BlockSpec((BLOCK, w), lambda i: (jnp.minimum(i, last), col))
    prev = lambda w, col=0: pl.BlockSpec((BLOCK, w), lambda i: (jnp.maximum(jnp.minimum(i, last) - 1, 0), col))
    late = lambda w: pl.BlockSpec((BLOCK, w), lambda i: (jnp.maximum(i - 1, 0), 0))
    sd = jax.ShapeDtypeStruct
    return pl.pallas_call(
        body, name="attn_bwd", grid=(nb + 1,),
        in_specs=[cur(ATTN_W), prev(KV_W, 0), cur(KV_W, 0), prev(KV_W, 1), cur(KV_W, 1),
                  pl.BlockSpec((1, ATTN_HEADS, BLOCK, 2 * BLOCK), lambda i: (jnp.minimum(i, 1), 0, 0, 0)),
                  _full((1, ATTN_HEADS))]
                 + [_full(c.shape) for c in consts + folds] + [cur(ATTN_W), cur(ATTN_HEADS), cur(ATTN_W)],
        out_specs=[cur(ATTN_W), late(2 * KV_W),
                   pl.BlockSpec((ATTN_HEADS, BLOCK, 2 * BLOCK), lambda i: (0, 0, 0)),
                   _full((1, HEAD_DIM)), _full((1, HEAD_DIM)), _full((1, ATTN_HEADS))],
        out_shape=[sd((s, ATTN_W), BF), sd((s, 2 * KV_W), BF),
                   sd((ATTN_HEADS, BLOCK, 2 * BLOCK), F32), sd((1, HEAD_DIM), F32), sd((1, HEAD_DIM), F32),
                   sd((1, ATTN_HEADS), F32)],
        scratch_shapes=[pltpu.VMEM((BLOCK, KV_W), F32), pltpu.VMEM((BLOCK, KV_W), F32),
                        pltpu.VMEM((BLOCK, ATTN_W), F32), pltpu.VMEM((2 * BLOCK, KV_W), F32),
                        pltpu.VMEM((2 * BLOCK, KV_W), F32), pltpu.VMEM((1, ATTN_W), F32), pltpu.VMEM((1, KV_W), F32)],
        compiler_params=_params(dimension_semantics=("arbitrary",)),
    )(q, kv, kv, kv, kv, bias, sinks, *consts, *folds, o_att, lse, d_o)


def _ssd_bwd(xbc, conv_all, dt_raw, conv_w, dt_bias, a_log, dsk_x, e_mat, e3t, hprev_all, dy_all):
    s = xbc.shape[0]
    nc = s // BLOCK
    ch = 1
    rows = ch * BLOCK
    nsteps = nc // ch
    gw = SSM_R * SSM_P
    b0, c0 = SSM_W, SSM_W + SSM_G * SSM_N

    def body(x_ref, conv_ref, dtr_ref, cw_ref, dtb_ref, alog_ref, dsk_ref, e_ref, e3_ref, hp_ref, dy_ref,
             dx_ref, ddt_ref, gcw_ref, gcb_ref, gdtb_ref, galog_ref, gdsk_ref,
             dh, nhead, gdskx, dxdt_s, dbc_s, dxd_s):
        def chunk_bwd(j):
            rs = slice(j * BLOCK, (j + 1) * BLOCK)
            conv = conv_ref[rs, :]
            sg, xact, u, dt, a, trilb, acum, dt_x, acum_x = _ssd_common(conv, dtr_ref[rs, :], dtb_ref, alog_ref, e3_ref)
            xs = xact[:, :SSM_W]
            acum_t = acum.T
            ea_x = jnp.exp2(acum_x)
            last_x = acum_x[BLOCK - 1:BLOCK, :]
            dte_x = jnp.exp2(last_x - acum_x)
            cd_x = jnp.exp2(last_x)
            xdt = xs * dt_x
            xw = xdt * dte_x
            hprev = hp_ref[j]
            dhn = dh[...]
            dy = dy_ref[rs, :]
            gdskx[...] += jnp.sum(dy * xs, axis=0, keepdims=True)
            dyea = dy * ea_x
            lane = lax.broadcasted_iota(jnp.int32, (BLOCK, SSM_HEADS), 1)
            dacum = jnp.zeros((BLOCK, SSM_HEADS), F32)
            dacc_x, dlast_x = [], []
            sls = [slice(g * gw, (g + 1) * gw) for g in range(SSM_G)]
            bgs = [_bf(xact[:, b0 + g * SSM_N:b0 + (g + 1) * SSM_N]) for g in range(SSM_G)]
            cgs = [_bf(xact[:, c0 + g * SSM_N:c0 + (g + 1) * SSM_N]) for g in range(SSM_G)]
            hpgs = [_bf(hprev[:, sl]) for sl in sls]
            dhgs = [_bf(dhn[:, sl]) for sl in sls]
            dyeags = [_bf(dyea[:, sl]) for sl in sls]
            xwgs = [_bf(xw[:, sl]) for sl in sls]
            xdt_b, dy_b = _bf(xdt), _bf(dy)
            low_half = lax.broadcasted_iota(jnp.int32, (BLOCK, 2 * SSM_P), 1) < SSM_P
            cbs = [_dot_nt(cgs[g], bgs[g]) for g in range(SSM_G)]
            gmats = [_dot(cgs[g], hpgs[g]) for g in range(SSM_G)]
            dxws = [_dot(bgs[g], dhgs[g]) for g in range(SSM_G)]
            dcgs = [_dot_nt(dyeags[g], hpgs[g]) for g in range(SSM_G)]
            dbgs = [_dot_nt(xwgs[g], dhgs[g]) for g in range(SSM_G)]
            for g in range(SSM_G):
                sl = sls[g]
                dh[:, sl] = dhn[:, sl] * cd_x[:, sl] + _dot_tn(cgs[g], dyeags[g])
                dxdt_s[:, sl] = dxws[g] * dte_x[:, sl]
                dacc_x.append(dy[:, sl] * gmats[g] * ea_x[:, sl] - dxws[g] * xw[:, sl])
                dlast_x.append(jnp.sum(dxws[g] * xw[:, sl], axis=0, keepdims=True)
                               + jnp.sum(dhn[:, sl] * hprev[:, sl], axis=0, keepdims=True) * cd_x[:, sl])
            for g in range(SSM_G):
                bg, cg, cb, dbg, dcg = bgs[g], cgs[g], cbs[g], dbgs[g], dcgs[g]
                hss = [slice((g * SSM_R + r) * SSM_P, (g * SSM_R + r + 1) * SSM_P) for r in range(SSM_R)]
                lms = [jnp.exp2(jnp.where(trilb, acum[:, g * SSM_R + r:g * SSM_R + r + 1]
                                         - acum_t[g * SSM_R + r:g * SSM_R + r + 1, :], -1e30)) for r in range(SSM_R)]
                mms = [cb * lm for lm in lms]
                mmbs = [_bf(mm) for mm in mms]
                dms = []
                for r in range(0, SSM_R, 2):
                    pair = slice(hss[r].start, hss[r + 1].stop)
                    xp, dyp = xdt_b[:, pair], dy_b[:, pair]
                    dmp = _dot_nt(dyp, jnp.concatenate([jnp.where(low_half, xp, 0), jnp.where(low_half, 0, xp)], axis=0))
                    dms += [dmp[:, :BLOCK], dmp[:, BLOCK:]]
                    dxd_s[:, pair] = _dot_tn(jnp.concatenate([mmbs[r], mmbs[r + 1]], axis=0),
                                             jnp.concatenate([jnp.where(low_half, dyp, 0), jnp.where(low_half, 0, dyp)], axis=0))
                dcb = sum(dms[r] * lms[r] for r in range(SSM_R))
                wms = [dms[r] * mms[r] for r in range(SSM_R)]
                antis = [_bf(wm - wm.T) for wm in wms]
                for r in range(SSM_R):
                    dacum = dacum + _dot(antis[r], (lane == g * SSM_R + r).astype(BF))
                dcbb = _bf(dcb)
                dbc_s[:, g * SSM_N:(g + 1) * SSM_N] = dbg + _dot_tn(dcbb, cg)
                dbc_s[:, SSM_G * SSM_N + g * SSM_N:SSM_G * SSM_N + (g + 1) * SSM_N] = dcg + _dot(dcbb, bg)
            dxdt = dxdt_s[...] + dxd_s[...]
            dxs = dy * dsk_ref[...] + dxdt * dt_x
            red = _group_sum(jnp.concatenate(
                [dxdt * xs, jnp.concatenate(dacc_x, axis=1),
                 jnp.broadcast_to(jnp.concatenate(dlast_x, axis=1), (8, SSM_W))], axis=0), e_ref[...])
            row = lax.broadcasted_iota(jnp.int32, (BLOCK, SSM_HEADS), 0)
            dacum = dacum + red[BLOCK:2 * BLOCK] + jnp.where(row == BLOCK - 1, red[2 * BLOCK:2 * BLOCK + 1], 0.0)
            ddta = _exact_left(_triu().astype(BF), dacum)
            ddt = red[:BLOCK] + ddta * a
            galog_ref[...] += jnp.sum(ddta * dt, axis=0, keepdims=True) * a
            du = ddt * _sig(u)
            ddt_ref[rs, :] = _bf(du)
            gdtb_ref[...] += jnp.sum(du, axis=0, keepdims=True)
            dconv = jnp.concatenate([dxs, dbc_s[...]], axis=1) * _dsilu(conv, sg, xact)
            gcb_ref[...] += jnp.sum(dconv, axis=0, keepdims=True)
            ext2 = jnp.concatenate([dconv, nhead[...]], axis=0)
            ahead = [pltpu.roll(ext2, BLOCK + 8 - (CONV_K - 1 - j), axis=0)[0:BLOCK] if j < CONV_K - 1 else dconv
                     for j in range(CONV_K)]
            dx_ref[rs, :] = _bf(sum(ahead[j] * cw_ref[j:j + 1, :] for j in range(CONV_K)))
            xraw = x_ref[rs, :]
            gcw_ref[...] += jnp.concatenate([jnp.sum(ahead[j] * xraw, axis=0, keepdims=True) for j in range(CONV_K)], axis=0)
            nhead[...] = dconv[0:8]

        i = pl.program_id(0)

        @pl.when(i == 0)
        def _():
            for ref in (dh, nhead, gdskx, gcw_ref, gcb_ref, gdtb_ref, galog_ref, gdsk_ref):
                ref[...] = jnp.zeros_like(ref)

        for j in reversed(range(ch)):
            chunk_bwd(j)

        @pl.when(i == nsteps - 1)
        def _():
            gdsk_ref[...] = _group_sum(jnp.broadcast_to(gdskx[...], (8, SSM_W)), e_ref[...])[0:1]

    chunk = lambda w: pl.BlockSpec((rows, w), lambda i: (nsteps - 1 - i, 0))
    sd = jax.ShapeDtypeStruct
    return pl.pallas_call(
        body, name="ssd_bwd", grid=(nsteps,),
        in_specs=[chunk(XBC_W), chunk(XBC_W),
                  chunk(SSM_HEADS), _full((CONV_K, XBC_W)), _full((1, SSM_HEADS)),
                  _full((1, SSM_HEADS)), _full((1, SSM_W)), _full((SSM_W, SSM_HEADS)), _full((3 * SSM_HEADS, SSM_W)),
                  pl.BlockSpec((ch, SSM_N, SSM_W), lambda i: (nsteps - 1 - i, 0, 0)), chunk(SSM_W)],
        out_specs=[chunk(XBC_W), chunk(SSM_HEADS), _full((CONV_K, XBC_W)), _full((1, XBC_W)),
                   _full((1, SSM_HEADS)), _full((1, SSM_HEADS)), _full((1, SSM_HEADS))],
        out_shape=[sd((s, XBC_W), BF), sd((s, SSM_HEADS), BF), sd((CONV_K, XBC_W), F32), sd((1, XBC_W), F32),
                   sd((1, SSM_HEADS), F32), sd((1, SSM_HEADS), F32), sd((1, SSM_HEADS), F32)],
        scratch_shapes=[pltpu.VMEM((SSM_N, SSM_W), F32), pltpu.VMEM((8, XBC_W), F32),
                        pltpu.VMEM((1, SSM_W), F32), pltpu.VMEM((BLOCK, SSM_W), F32),
                        pltpu.VMEM((BLOCK, 2 * SSM_G * SSM_N), F32), pltpu.VMEM((BLOCK, SSM_W), F32)],
        compiler_params=_params(dimension_semantics=("arbitrary",)),
    )(xbc, conv_all, dt_raw, conv_w, dt_bias, a_log, dsk_x, e_mat, e3t, hprev_all, dy_all)


def _dh(x, dout, norm_w, scale, dsegs, w_t, tm=256):
    s = x.shape[0]

    def body(x_ref, dout_ref, nw_ref, sc_ref, *rest):
        d_refs, w_hbm = rest[:NSEG], rest[NSEG]
        gx_ref, dshift_ref, dscale_ref, gnw_ref = rest[NSEG + 1:NSEG + 5]
        w_vm, sem = rest[NSEG + 5], rest[NSEG + 6]
        first = pl.program_id(0) == 0
        cps = [pltpu.make_async_copy(w_hbm.at[SEG_OFF[j]:SEG_OFF[j + 1], :], w_vm.at[SEG_OFF[j]:SEG_OFF[j + 1], :], sem.at[j])
               for j in range(NSEG)]

        def tile(waiting):
            dh = None
            for j in range(NSEG):
                if waiting:
                    cps[j].wait()
                part = _dot(d_refs[j][...], w_vm[SEG_OFF[j]:SEG_OFF[j + 1], :])
                dh = part if dh is None else dh + part
            xv = x_ref[...]
            r = lax.rsqrt(jnp.mean(xv * xv, axis=-1, keepdims=True) + EPS)
            xn = xv * r
            nw = nw_ref[...]
            sc1 = 1.0 + sc_ref[...]
            dshift_ref[...] += jnp.sum(dh, axis=0, keepdims=True)
            dhxn = jnp.sum(dh * xn, axis=0, keepdims=True)
            dscale_ref[...] += dhxn * nw
            gnw_ref[...] += dhxn * sc1
            dxn = dh * (nw * sc1)
            gx_ref[...] = dout_ref[...] + r * (dxn - xn * jnp.mean(xn * dxn, axis=-1, keepdims=True))

        @pl.when(first)
        def _():
            for cp in cps:
                cp.start()
            for ref in (dshift_ref, dscale_ref, gnw_ref):
                ref[...] = jnp.zeros_like(ref)
            tile(True)

        @pl.when(jnp.logical_not(first))
        def _():
            tile(False)

    vec = _full((1, D_MODEL))
    sd = jax.ShapeDtypeStruct
    return pl.pallas_call(
        body, name="dh", grid=(s // tm,),
        in_specs=[_rows(tm, D_MODEL), _rows(tm, D_MODEL), vec, vec] + [_rows(tm, w) for w in SEG_W] + [ANY],
        out_specs=[_rows(tm, D_MODEL), vec, vec, vec],
        out_shape=[sd((s, D_MODEL), F32), sd((1, D_MODEL), F32), sd((1, D_MODEL), F32), sd((1, D_MODEL), F32)],
        scratch_shapes=[pltpu.VMEM((IN_W, D_MODEL), BF), pltpu.SemaphoreType.DMA((NSEG,))],
        compiler_params=_params(dimension_semantics=("arbitrary",)),
    )(x, dout, norm_w, scale, *dsegs, w_t)


def _gw_seg(h, dseg, name, tm=1024):
    s, w = dseg.shape
    tn = min(w, 1024)
    tm = min(tm, s)
    nm = s // tm

    def body(h_ref, d_ref, o_ref, acc):
        m = pl.program_id(1)

        @pl.when(m == 0)
        def _():
            acc[...] = jnp.zeros_like(acc)

        acc[...] += _dot_tn(d_ref[...], h_ref[...])

        @pl.when(m == nm - 1)
        def _():
            o_ref[...] = _bf(acc[...])

    return pl.pallas_call(
        body, name=name, grid=(w // tn, nm),
        in_specs=[pl.BlockSpec((tm, D_MODEL), lambda n, m: (m, 0)), pl.BlockSpec((tm, tn), lambda n, m: (m, n))],
        out_specs=pl.BlockSpec((tn, D_MODEL), lambda n, m: (n, 0)),
        out_shape=jax.ShapeDtypeStruct((w, D_MODEL), BF),
        scratch_shapes=[pltpu.VMEM((tn, D_MODEL), F32)],
        compiler_params=_params(dimension_semantics=("arbitrary", "arbitrary")),
    )(h, dseg)


def _gw_in(h, dsegs):
    return [_gw_seg(h, d, "gw_in_%d" % j) for j, d in enumerate(dsegs)]


def _local_step(x, tgt, shift, scale, gate, w_t, rows_fn, norm_w, qnw, knw, rel_bias, sinks,
                conv_w, conv_b, dt_bias, a_log, d_skip, ssm_nw, after_mid=None, after_gw=None):
    oh_t = _bucket_onehot_t()
    bias = _masked_bias(_bias_dense(rel_bias.T, oh_t).reshape(ATTN_HEADS, BLOCK, 2 * BLOCK))
    *segs, h = _inproj(x, norm_w, scale, shift, w_t)
    q, kv, zam, xbc, dtr, gab = segs
    consts = _attn_consts(qnw, knw)
    o_att, lse = _attn_fwd(q, kv, bias, sinks, consts)
    e_mat, e3t = _membership(SSM_W, SSM_P, SSM_HEADS)
    dsk_x = jnp.repeat(d_skip, SSM_P, axis=1)
    ypre, hprev, conv = _ssd_fwd(xbc, dtr, conv_w, conv_b, dt_bias, a_log, dsk_x, e3t)
    (dout, d_o, dzam, dyp, dgab, yag, dy_a, yn, dy_b, merged, dob, g_ssm_nw, dgate, loss) = _mid(
        x, tgt, o_att, zam, ypre, gab, gate, ssm_nw, rows_fn(ypre))
    g_wap = _gw_seg(dy_a, yag, "gw_attn_proj")
    g_wsp = _gw_seg(dy_b, yn, "gw_ssm_proj")
    g_wout = _gw_seg(dob, merged, "gw_out")
    zero = after_mid(g_wap, g_wsp, g_wout) if after_mid is not None else 0.0
    dq, dkv, dss, g_qnw, g_knw, g_sinks = _attn_bwd(q, kv, bias, sinks + zero, consts, o_att, lse, d_o)
    g_rel = _bias_grad(dss.reshape(ATTN_HEADS, BLOCK * 2 * BLOCK), oh_t).T
    dxbc, ddt, g_cw, g_cb, g_dtb, g_alog, g_dsk = _ssd_bwd(
        xbc, conv, dtr, conv_w, dt_bias, a_log, dsk_x, e_mat, e3t, hprev, dyp)
    dsegs = (dq, dkv, dzam, dxbc, ddt, dgab)
    g_ws = _gw_in(h, dsegs)
    zero = after_gw(g_ws) if after_gw is not None else 0.0
    gx, dshift, dscale, g_nw = _dh(x, dout, norm_w + zero, scale, dsegs, w_t)
    return dict(loss=loss, grad_x=gx, dmod=jnp.concatenate([dshift, dscale, dgate], axis=1), g_ws=g_ws,
                g_wap=g_wap, g_wsp=g_wsp, g_wout=g_wout, g_norm_w=g_nw, g_qnw=g_qnw, g_knw=g_knw, g_rel=g_rel,
                g_sinks=g_sinks, g_conv_w=g_cw, g_conv_b=g_cb, g_dt_bias=g_dtb, g_a_log=g_alog, g_d_skip=g_dsk,
                g_ssm_nw=g_ssm_nw)


def _me():
    return lax.axis_index("x"), lax.axis_index("y"), lax.axis_index("c")


def _flip(v, bit):
    return 1 - v if bit else v


def _ag_direct(v, name):
    def body(v_ref, out_ref, send_sems, recv_sems, local_sem):
        x, y, c = _me()
        me = 4 * x + 2 * y + c
        mine = pltpu.make_async_copy(v_ref, out_ref.at[me], local_sem)
        mine.start()
        peers = [(_flip(x, k >> 2 & 1), _flip(y, k >> 1 & 1), _flip(c, k & 1)) for k in range(1, N_DEV)]
        sends = [pltpu.make_async_remote_copy(
            src_ref=v_ref, dst_ref=out_ref.at[me], send_sem=send_sems.at[j], recv_sem=recv_sems.at[j],
            device_id=p, device_id_type=MESH) for j, p in enumerate(peers)]
        for cp in sends:
            cp.start()
        for j, (px, py, pc) in enumerate(peers):
            pltpu.make_async_remote_copy(
                src_ref=v_ref, dst_ref=out_ref.at[4 * px + 2 * py + pc], send_sem=send_sems.at[j],
                recv_sem=recv_sems.at[j], device_id=(px, py, pc), device_id_type=MESH).wait_recv()
        for cp in sends:
            cp.wait_send()
        mine.wait()

    vm = pl.BlockSpec(memory_space=pltpu.VMEM)
    return pl.pallas_call(
        body, name=name, out_shape=jax.ShapeDtypeStruct((N_DEV,) + v.shape, v.dtype),
        in_specs=[vm], out_specs=vm,
        scratch_shapes=[pltpu.SemaphoreType.DMA((N_DEV - 1,)), pltpu.SemaphoreType.DMA((N_DEV - 1,)),
                        pltpu.SemaphoreType.DMA],
        compiler_params=_params(),
    )(v)


def _gather_mod(v, w_ada, b_piece):
    ncols = w_ada.shape[1]

    def body(v_ref, w_ref, b_ref, rows_ref, mods_ref, piece, send_sems, recv_sems, local_sems):
        x, y, c = _me()
        me = 4 * x + 2 * y + c
        peers = _peers(x, y, c)

        def exchange(src, dst, rnd):
            mine = pltpu.make_async_copy(src, dst.at[me], local_sems.at[rnd])
            mine.start()
            sends = [pltpu.make_async_remote_copy(
                src_ref=src, dst_ref=dst.at[me], send_sem=send_sems.at[rnd, j], recv_sem=recv_sems.at[rnd, j],
                device_id=p, device_id_type=MESH) for j, p in enumerate(peers)]
            for cp in sends:
                cp.start()
            for j, (px, py, pc) in enumerate(peers):
                pltpu.make_async_remote_copy(
                    src_ref=src, dst_ref=dst.at[4 * px + 2 * py + pc], send_sem=send_sems.at[rnd, j],
                    recv_sem=recv_sems.at[rnd, j], device_id=(px, py, pc), device_id_type=MESH).wait_recv()
            for cp in sends:
                cp.wait_send()
            mine.wait()

        exchange(v_ref, rows_ref, 0)
        c_all = rows_ref[:, 0, :D_MODEL]
        piece[...] = _dot(_bf(_silu(c_all)), _bf(w_ref[...])) + b_ref[...]
        exchange(piece, mods_ref, 1)

    vm = pl.BlockSpec(memory_space=pltpu.VMEM)
    return pl.pallas_call(
        body, name="gather_mod",
        out_shape=(jax.ShapeDtypeStruct((N_DEV,) + v.shape, F32), jax.ShapeDtypeStruct((N_DEV, N_DEV, ncols), F32)),
        in_specs=[vm, vm, vm], out_specs=(vm, vm),
        scratch_shapes=[pltpu.VMEM((N_DEV, ncols), F32), pltpu.SemaphoreType.DMA((2, N_DEV - 1)),
                        pltpu.SemaphoreType.DMA((2, N_DEV - 1)), pltpu.SemaphoreType.DMA((2,))],
        compiler_params=_params(),
    )(v, w_ada, b_piece)


def _ag_two_level(v, name, chunks=1):
    rows = v.shape[0] // chunks
    assert rows * chunks == v.shape[0] and rows % 8 == 0

    def body(v_ref, out_ref, token, send_sems, recv_sems, local_sem):
        token[...] = jnp.zeros_like(token)
        x, y, c = _me()
        me, sibling = (x, y, c), (x, y, 1 - c)
        chips = [(1 - x, y), (x, 1 - y), (1 - x, 1 - y)]

        def piece(ref, k):
            return ref.at[pl.ds(k * rows, rows), :]

        def slot(px, py, pc):
            return out_ref.at[4 * px + 2 * py + pc]

        def copy(n, k, block, to, src=None):
            return pltpu.make_async_remote_copy(
                src_ref=piece(slot(*block) if src is None else src, k), dst_ref=piece(slot(*block), k),
                send_sem=send_sems.at[n * chunks + k], recv_sem=recv_sems.at[n * chunks + k],
                device_id=to, device_id_type=MESH)

        mine = pltpu.make_async_copy(v_ref, slot(*me), local_sem)
        mine.start()
        first = [copy(0, k, me, sibling, src=v_ref) for k in range(chunks)]
        first += [copy(1 + j, k, me, (*chip, c), src=v_ref) for k in range(chunks) for j, chip in enumerate(chips)]
        for cp in first:
            cp.start()
        passed = []
        for k in range(chunks):
            for j, chip in enumerate(chips):
                copy(1 + j, k, (*chip, c), me).wait_recv()
                passed.append(copy(4 + j, k, (*chip, c), sibling))
                passed[-1].start()
        for k in range(chunks):
            copy(0, k, sibling, me).wait_recv()
            for j, chip in enumerate(chips):
                copy(4 + j, k, (*chip, 1 - c), me).wait_recv()
        for cp in first + passed:
            cp.wait_send()
        mine.wait()

    out, token = pl.pallas_call(
        body, name=name,
        out_shape=(jax.ShapeDtypeStruct((N_DEV,) + v.shape, v.dtype), jax.ShapeDtypeStruct((8, 128), v.dtype)),
        in_specs=[ANY], out_specs=(ANY, pl.BlockSpec(memory_space=pltpu.VMEM)),
        scratch_shapes=[pltpu.SemaphoreType.DMA((7 * chunks,)), pltpu.SemaphoreType.DMA((7 * chunks,)),
                        pltpu.SemaphoreType.DMA],
        compiler_params=_params(),
    )(v)
    return out, token[0:1, 0:1]


HBM = pl.BlockSpec(memory_space=pltpu.HBM)
SEM = pl.BlockSpec(memory_space=pltpu.SEMAPHORE)
EFFECT = pltpu.SideEffectType.DATAFLOW_SIDE_EFFECTING


def _peers(x, y, c):
    return [(_flip(x, k >> 2 & 1), _flip(y, k >> 1 & 1), _flip(c, k & 1)) for k in range(1, N_DEV)]


def _exchange_start(src, land, gather, name):
    def body(src_ref, land_ref, send_sems, recv_sems, src_thru, land_thru, token):
        x, y, c = _me()
        me = 4 * x + 2 * y + c
        for j, (px, py, pc) in enumerate(_peers(x, y, c)):
            pltpu.make_async_remote_copy(
                src_ref=src_ref if gather else src_ref.at[4 * px + 2 * py + pc], dst_ref=land_ref.at[me],
                send_sem=send_sems.at[j], recv_sem=recv_sems.at[j], device_id=(px, py, pc), device_id_type=MESH).start()
        token[...] = jnp.zeros_like(token)

    sems = pltpu.SemaphoreType.DMA((N_DEV - 1,))
    out = pl.pallas_call(
        body, name=name,
        out_shape=(sems, sems, pltpu.HBM(src.shape, src.dtype), pltpu.HBM(land.shape, land.dtype),
                   jax.ShapeDtypeStruct((8, 128), F32)),
        in_specs=(HBM, HBM), out_specs=(SEM, SEM, HBM, HBM, pl.BlockSpec(memory_space=pltpu.VMEM)),
        input_output_aliases={0: 2, 1: 3},
        compiler_params=pltpu.CompilerParams(has_side_effects=EFFECT),
    )(pltpu.with_memory_space_constraint(src, pltpu.HBM), pltpu.with_memory_space_constraint(land, pltpu.HBM))
    return out[:4], out[4][0, 0]


def _exchange_wait(started, after, gather, name):
    send_sems, recv_sems, src_thru, land_thru = started

    def body(src_ref, land_ref, send_sems, recv_sems, after_ref, src_dead, got_ref):
        x, y, c = _me()
        for j, (px, py, pc) in enumerate(_peers(x, y, c)):
            pid = 4 * px + 2 * py + pc
            cp = pltpu.make_async_remote_copy(
                src_ref=src_ref if gather else src_ref.at[pid], dst_ref=land_ref.at[pid],
                send_sem=send_sems.at[j], recv_sem=recv_sems.at[j], device_id=(px, py, pc), device_id_type=MESH)
            cp.wait_send()
            cp.wait_recv()

    return pl.pallas_call(
        body, name=name,
        out_shape=(pltpu.HBM(src_thru.shape, src_thru.dtype), pltpu.HBM(land_thru.shape, land_thru.dtype)),
        in_specs=(HBM, HBM, SEM, SEM, ANY), out_specs=(HBM, HBM), input_output_aliases={0: 0, 1: 1},
        compiler_params=pltpu.CompilerParams(has_side_effects=EFFECT),
    )(src_thru, land_thru, send_sems, recv_sems, after)[1]


def _silu(a):
    return a * _sig(a)


def _gw_ada(c_all, dmod_piece):
    def body(c_ref, d_ref, o_ref):
        o_ref[...] = _dot_tn(_bf(_silu(c_ref[...])), _bf(d_ref[...]))

    return pl.pallas_call(
        body, name="gw_ada", out_shape=jax.ShapeDtypeStruct((c_all.shape[1], dmod_piece.shape[1]), F32),
        compiler_params=_params(),
    )(c_all, dmod_piece)


def _adam(parts, w, m, v, name):
    k, r, n = parts.shape
    if r <= 256 or r % 256 == 0:
        tr, tn = min(r, 256), n
    else:
        tr, tn = r, 256
    assert r % tr == 0 and n % tn == 0

    def body(p_ref, w_ref, m_ref, v_ref, g_ref, d_ref, nm_ref, nv_ref):
        g = p_ref[0].astype(F32)
        for j in range(1, k):
            g = g + p_ref[j].astype(F32)
        g_ref[...] = g
        d_ref[...], nm_ref[...], nv_ref[...] = _adam_math(g, w_ref[...], m_ref[...], v_ref[...])

    blk = pl.BlockSpec((tr, tn), lambda i, j: (i, j))
    return pl.pallas_call(
        body, name=name, grid=(r // tr, n // tn),
        in_specs=[pl.BlockSpec((k, tr, tn), lambda i, j: (0, i, j)), blk, blk, blk],
        out_specs=[blk, blk, blk, blk],
        out_shape=[jax.ShapeDtypeStruct((r, n), F32)] * 4,
        compiler_params=_params(dimension_semantics=("arbitrary", "arbitrary")),
    )(parts, w, m, v)


def _adam_math(g, w, m, v):
    m_new = ADAM_B1 * m + (1.0 - ADAM_B1) * g
    v_new = ADAM_B2 * v + (1.0 - ADAM_B2) * jnp.square(g)
    m_hat = m_new / (1.0 - ADAM_B1 ** ADAM_STEP)
    v_hat = v_new / (1.0 - ADAM_B2 ** ADAM_STEP)
    return -ADAM_LR * (m_hat / (jnp.sqrt(v_hat) + ADAM_EPS) + ADAM_WD * w), m_new, v_new


_SMALL = (("b_ada", 3 * D_MODEL), ("norm_w", D_MODEL), ("q_norm_w", HEAD_DIM), ("k_norm_w", HEAD_DIM),
          ("rel_bias", REL_BUCKETS * ATTN_HEADS), ("sinks", ATTN_HEADS), ("conv_b", XBC_W), ("dt_bias", SSM_HEADS),
          ("a_log", SSM_HEADS), ("d_skip", SSM_HEADS), ("ssm_norm_w", SSM_W))
_SLOT = tuple(-(-n // 128) * 128 for _, n in _SMALL)
_SLOT_OFF = tuple(int(o) for o in np.cumsum((0,) + _SLOT))
_LOSS_OFF = _SLOT_OFF[-1]
_CW_OFF = _LOSS_OFF + 128
_PACK_N = _CW_OFF + CONV_K * XBC_W


def _pack_partials(small, loss, g_conv_w):
    parts = []
    for (name, n), slot in zip(_SMALL, _SLOT):
        parts.append(small[name].reshape(1, n))
        if slot > n:
            parts.append(jnp.zeros((1, slot - n), F32))
    parts += [loss.reshape(1, 1), jnp.zeros((1, 127), F32), g_conv_w.reshape(1, CONV_K * XBC_W)]
    return jnp.concatenate(parts, axis=1)


def _adam_small(pack_all, w, m, v):
    names = [name for name, _ in _SMALL]

    def body(p_ref, *rest):
        ins, outs = rest[:3 * len(names)], rest[3 * len(names):]

        def total(off, n):
            g = p_ref[0, :, off:off + n]
            for d in range(1, N_DEV):
                g = g + p_ref[d, :, off:off + n]
            return g

        for j, (name, n) in enumerate(_SMALL):
            g = total(_SLOT_OFF[j], n)
            delta, m_new, v_new = _adam_math(g, ins[3 * j][...], ins[3 * j + 1][...], ins[3 * j + 2][...])
            outs[4 * j][...] = g
            outs[4 * j + 1][...] = delta
            outs[4 * j + 2][...] = m_new
            outs[4 * j + 3][...] = v_new
        outs[-1][...] = total(_LOSS_OFF, 1)

    flat = []
    for name, n in _SMALL:
        flat += [w[name].reshape(1, n), m[name].reshape(1, n), v[name].reshape(1, n)]
    out_shape = [jax.ShapeDtypeStruct((1, n), F32) for _, n in _SMALL for _ in range(4)] + [jax.ShapeDtypeStruct((1, 1), F32)]
    out = pl.pallas_call(body, name="adam_small", out_shape=out_shape, compiler_params=_params())(pack_all, *flat)
    res = {name: [out[4 * j + t].reshape(w[name].shape) for t in range(4)] for j, name in enumerate(names)}
    return res, out[-1]


WEIGHTS = ("w_ada", "b_ada", "norm_w", "w_in", "q_norm_w", "k_norm_w", "rel_bias", "sinks", "conv_w", "conv_b",
           "dt_bias", "a_log", "d_skip", "ssm_norm_w", "w_attn_proj", "w_ssm_proj", "w_out")


def kernel(x, c, w_ada, b_ada, norm_w, w_in, q_norm_w, k_norm_w, rel_bias, sinks, conv_w, conv_b, dt_bias, a_log, d_skip, ssm_norm_w, w_attn_proj, w_ssm_proj, w_out, loss_target, m_w_ada, m_b_ada, m_norm_w, m_w_in, m_q_norm_w, m_k_norm_w, m_rel_bias, m_sinks, m_conv_w, m_conv_b, m_dt_bias, m_a_log, m_d_skip, m_ssm_norm_w, m_w_attn_proj, m_w_ssm_proj, m_w_out, v_w_ada, v_b_ada, v_norm_w, v_w_in, v_q_norm_w, v_k_norm_w, v_rel_bias, v_sinks, v_conv_w, v_conv_b, v_dt_bias, v_a_log, v_d_skip, v_ssm_norm_w, v_w_attn_proj, v_w_ssm_proj, v_w_out):
    w = dict(w_ada=w_ada, b_ada=b_ada, norm_w=norm_w, w_in=w_in, q_norm_w=q_norm_w, k_norm_w=k_norm_w,
             rel_bias=rel_bias, sinks=sinks, conv_w=conv_w, conv_b=conv_b, dt_bias=dt_bias, a_log=a_log,
             d_skip=d_skip, ssm_norm_w=ssm_norm_w, w_attn_proj=w_attn_proj, w_ssm_proj=w_ssm_proj, w_out=w_out)
    m = dict(w_ada=m_w_ada, b_ada=m_b_ada, norm_w=m_norm_w, w_in=m_w_in, q_norm_w=m_q_norm_w, k_norm_w=m_k_norm_w,
             rel_bias=m_rel_bias, sinks=m_sinks, conv_w=m_conv_w, conv_b=m_conv_b, dt_bias=m_dt_bias, a_log=m_a_log,
             d_skip=m_d_skip, ssm_norm_w=m_ssm_norm_w, w_attn_proj=m_w_attn_proj, w_ssm_proj=m_w_ssm_proj, w_out=m_w_out)
    v = dict(w_ada=v_w_ada, b_ada=v_b_ada, norm_w=v_norm_w, w_in=v_w_in, q_norm_w=v_q_norm_w, k_norm_w=v_k_norm_w,
             rel_bias=v_rel_bias, sinks=v_sinks, conv_w=v_conv_w, conv_b=v_conv_b, dt_bias=v_dt_bias, a_log=v_a_log,
             d_skip=v_d_skip, ssm_norm_w=v_ssm_norm_w, w_attn_proj=v_w_attn_proj, w_ssm_proj=v_w_ssm_proj, w_out=v_w_out)
    me = 4 * lax.axis_index("x") + 2 * lax.axis_index("y") + lax.axis_index("c")
    ada_n = w_ada.shape[2]
    in_n = w_in.shape[2]
    cw_n = conv_w.shape[2]

    b_piece = lax.dynamic_slice_in_dim(b_ada, me * ada_n, ada_n, axis=1)
    first, mod_all = _gather_mod(jnp.concatenate([c, conv_w[0].reshape(1, CONV_K * cw_n)], axis=1), w_ada[0], b_piece)
    first = first[:, 0]
    c_all = first[:, :D_MODEL]
    conv_w_full = first[:, D_MODEL:].reshape(N_DEV, CONV_K, cw_n).transpose(1, 0, 2).reshape(CONV_K, XBC_W)
    mod = lax.dynamic_index_in_dim(mod_all, me, axis=1, keepdims=False).reshape(1, 3 * D_MODEL)
    shift, scale, gate = mod[:, :D_MODEL], mod[:, D_MODEL:2 * D_MODEL], mod[:, 2 * D_MODEL:]

    pad = -in_n % 24
    w_t, zero = _ag_two_level(jnp.pad(w_in[0].T.astype(BF), ((0, pad), (0, 0))), "ag_w_in", chunks=3)
    w_t = w_t[:, :in_n].reshape(N_DEV * in_n, D_MODEL)

    def with_mine(blocks, mine):
        return lax.dynamic_update_index_in_dim(lax.empty(blocks, mine.dtype), mine, me, axis=0)

    rows = jnp.concatenate([w_attn_proj[0], w_ssm_proj[0], w_out[0]], axis=0).astype(BF) + zero
    r_ap, r_sp = w_attn_proj.shape[1], w_ssm_proj.shape[1]
    rows_started, zero = _exchange_start(rows, with_mine((N_DEV,) + rows.shape, rows), True, "ag_rows_start")

    def rows_fn(after):
        return _exchange_wait(rows_started, after, True, "ag_rows_wait")

    started = {}

    def send_blocks(key, g, name):
        started[key], zero = _exchange_start(
            g, with_mine(g.shape, lax.dynamic_index_in_dim(g, me, axis=0, keepdims=False)), False, name)
        return zero

    def after_mid(g_wap, g_wsp, g_wout):
        return send_blocks("rows", jnp.concatenate(
            [g_wap.reshape(N_DEV, r_ap, D_MODEL), g_wsp.reshape(N_DEV, r_sp, D_MODEL),
             g_wout.reshape(N_DEV, r_ap, D_MODEL)], axis=1), "rs_rows_start")

    def after_gw(g_ws):
        return send_blocks("in", jnp.concatenate(g_ws, axis=0).reshape(N_DEV, in_n, D_MODEL), "rs_in_start")

    r = _local_step(x[0], loss_target[0], shift, scale + zero, gate, w_t, rows_fn, norm_w, q_norm_w, k_norm_w,
                    rel_bias, sinks, conv_w_full, conv_b, dt_bias, a_log, d_skip, ssm_norm_w, after_mid, after_gw)

    small = dict(b_ada=r["dmod"], norm_w=r["g_norm_w"], q_norm_w=r["g_qnw"], k_norm_w=r["g_knw"], rel_bias=r["g_rel"],
                 sinks=r["g_sinks"], conv_b=r["g_conv_b"], dt_bias=r["g_dt_bias"], a_log=r["g_a_log"],
                 d_skip=r["g_d_skip"], ssm_norm_w=r["g_ssm_nw"])
    pack_all = _ag_direct(_pack_partials(small, r["loss"], r["g_conv_w"]), "ag_small")
    res, loss = _adam_small(pack_all, w, m, v)
    loss = loss[0, 0]
    cw_parts = pack_all[:, 0, _CW_OFF:].reshape(N_DEV, CONV_K, XBC_W)
    cw_mine = lax.dynamic_slice_in_dim(cw_parts, me * cw_n, cw_n, axis=2)
    res["conv_w"] = [a[None] for a in _adam(cw_mine, conv_w[0], m_conv_w[0], v_conv_w[0], "adam_conv_w")]

    dmod_piece = lax.dynamic_slice_in_dim(pack_all[:, 0, :3 * D_MODEL], me * ada_n, ada_n, axis=1)
    g_ada = _gw_ada(c_all, dmod_piece)
    res["w_ada"] = [a[None] for a in _adam(g_ada[None], w_ada[0], m_w_ada[0], v_w_ada[0], "adam_w_ada")]

    cat = lambda d: jnp.concatenate([d["w_attn_proj"][0], d["w_ssm_proj"][0], d["w_out"][0]], axis=0)
    rows_res = _adam(_exchange_wait(started["rows"], g_ada, False, "rs_rows_wait"), cat(w), cat(m), cat(v), "adam_w_rows")
    res["w_in"] = [a.T[None] for a in _adam(_exchange_wait(started["in"], rows_res[0], False, "rs_in_wait"),
                                            w_in[0].T, m_w_in[0].T, v_w_in[0].T, "adam_w_in")]
    res["w_attn_proj"] = [a[None, :r_ap] for a in rows_res]
    res["w_ssm_proj"] = [a[None, r_ap:r_ap + r_sp] for a in rows_res]
    res["w_out"] = [a[None, r_ap + r_sp:] for a in rows_res]

    outs = [loss, r["grad_x"][None]]
    for j in range(4):
        outs += [res[name][j] for name in WEIGHTS]
    return tuple(outs)
```

```python
import math

import numpy as np
import jax
import jax.numpy as jnp
from jax import lax
from jax.experimental import pallas as pl
from jax.experimental.pallas import tpu as pltpu

F32 = jnp.float32
BF = jnp.bfloat16
HI = lax.Precision.HIGHEST

D_MODEL = 1024
ATTN_HEADS = 16
KV_HEADS = 4
GRP = ATTN_HEADS // KV_HEADS
HEAD_DIM = 64
ATTN_W = ATTN_HEADS * HEAD_DIM
KV_W = KV_HEADS * HEAD_DIM
BLOCK = 128
REL_BUCKETS = 32
REL_MAX_DIST = 128
SSM_W = 2048
SSM_P = 64
SSM_HEADS = 32
SSM_G = 4
SSM_R = 8
SSM_N = 128
CONV_K = 4
XBC_W = SSM_W + 2 * SSM_G * SSM_N
SEG_W = (ATTN_W, 2 * KV_W, ATTN_W + SSM_W, XBC_W, SSM_HEADS, 2 * D_MODEL)
NSEG = len(SEG_W)
SEG_OFF = tuple(int(v) for v in np.cumsum((0,) + SEG_W))
IN_W = SEG_OFF[-1]
GATE_SEGS = (2, 5)
EPS = 1e-6
N_DEV = 8
ADAM_LR, ADAM_B1, ADAM_B2, ADAM_EPS, ADAM_WD, ADAM_STEP = 0.001, 0.9, 0.999, 1e-08, 0.01, 10
VMEM_LIMIT = 60 * 1024 * 1024
MESH = pl.DeviceIdType.MESH
ANY = pl.BlockSpec(memory_space=pl.ANY)


def _dot(a, b, precision=None):
    return jnp.dot(a, b, preferred_element_type=F32, precision=precision)


def _dot_nt(a, b, precision=None):
    return lax.dot_general(a, b, (((1,), (1,)), ((), ())), preferred_element_type=F32, precision=precision)


def _dot_tn(a, b, precision=None):
    return lax.dot_general(a, b, (((0,), (0,)), ((), ())), preferred_element_type=F32, precision=precision)


def _bf(a):
    return a.astype(BF)


def _sig(a):
    return 0.5 * jnp.tanh(0.5 * a) + 0.5


def _params(**kw):
    return pltpu.CompilerParams(vmem_limit_bytes=VMEM_LIMIT, **kw)


def _full(shape):
    nd = len(shape)
    return pl.BlockSpec(shape, lambda i: (0,) * nd)


def _rows(tm, w):
    return pl.BlockSpec((tm, w), lambda i: (i, 0))


def _inproj(x, norm_w, scale, shift, w_t, tm=256):
    s = x.shape[0]

    def body(x_ref, nw_ref, sc_ref, sh_ref, w_hbm, *rest):
        outs, h_ref, w_vm, sem = rest[:NSEG], rest[NSEG], rest[NSEG + 1], rest[NSEG + 2]
        first = pl.program_id(0) == 0
        cps = [pltpu.make_async_copy(w_hbm.at[SEG_OFF[j]:SEG_OFF[j + 1], :], w_vm.at[SEG_OFF[j]:SEG_OFF[j + 1], :], sem.at[j])
               for j in range(NSEG)]

        def tile(waiting):
            xv = x_ref[...]
            r = lax.rsqrt(jnp.mean(xv * xv, axis=-1, keepdims=True) + EPS)
            h = xv * r * (nw_ref[...] * (1.0 + sc_ref[...])) + sh_ref[...]
            hb = _bf(h)
            h_ref[...] = hb
            for j in range(NSEG):
                if waiting:
                    cps[j].wait()
                outs[j][...] = _dot_nt(hb, w_vm[SEG_OFF[j]:SEG_OFF[j + 1], :]).astype(outs[j].dtype)

        @pl.when(first)
        def _():
            for cp in cps:
                cp.start()
            tile(True)

        @pl.when(jnp.logical_not(first))
        def _():
            tile(False)

    vec = _full((1, D_MODEL))
    return pl.pallas_call(
        body, name="inproj", grid=(s // tm,),
        in_specs=[_rows(tm, D_MODEL), vec, vec, vec, ANY],
        out_specs=[_rows(tm, w) for w in SEG_W] + [_rows(tm, D_MODEL)],
        out_shape=[jax.ShapeDtypeStruct((s, w), BF if j in GATE_SEGS else F32) for j, w in enumerate(SEG_W)]
                  + [jax.ShapeDtypeStruct((s, D_MODEL), BF)],
        scratch_shapes=[pltpu.VMEM((IN_W, D_MODEL), BF), pltpu.SemaphoreType.DMA((NSEG,))],
        compiler_params=_params(dimension_semantics=("arbitrary",)),
    )(x, norm_w, scale, shift, w_t)


def _bucket_onehot_t():
    qi = jnp.arange(BLOCK)[:, None]
    kj = jnp.arange(2 * BLOCK)[None, :]
    dist = qi + BLOCK - kj
    n = jnp.maximum(dist, 0)
    max_exact = REL_BUCKETS // 2
    nf = jnp.maximum(n, 1).astype(F32)
    large = max_exact + (jnp.log(nf / max_exact) / math.log(REL_MAX_DIST / max_exact)
                         * (REL_BUCKETS - max_exact)).astype(jnp.int32)
    large = jnp.minimum(large, REL_BUCKETS - 1)
    bucket = jnp.where(n < max_exact, n, large).reshape(1, BLOCK * 2 * BLOCK)
    return (bucket == jnp.arange(REL_BUCKETS)[:, None]).astype(F32)


def _bias_dense(rel_bias_t, oh_t):
    def body(rb_ref, oh_ref, o_ref):
        o_ref[...] = _dot(rb_ref[...], oh_ref[...], HI)

    return pl.pallas_call(
        body, name="bias_dense", out_shape=jax.ShapeDtypeStruct((ATTN_HEADS, BLOCK * 2 * BLOCK), F32),
        compiler_params=_params(),
    )(rel_bias_t, oh_t)


def _bias_grad(ds_sum, oh_t):
    def body(ds_ref, oh_ref, o_ref):
        o_ref[...] = _dot_nt(ds_ref[...], oh_ref[...], HI)

    return pl.pallas_call(
        body, name="bias_grad", out_shape=jax.ShapeDtypeStruct((ATTN_HEADS, REL_BUCKETS), F32),
        compiler_params=_params(),
    )(ds_sum, oh_t)


def _group_sum(a, e):
    hi = _bf(a)
    return _dot(hi, e) + _dot(_bf(a - hi.astype(F32)), e)


def _group_bcast(a, e3t):
    hi = _bf(a)
    r1 = a - hi.astype(F32)
    mid = _bf(r1)
    return _dot(jnp.concatenate([hi, mid, _bf(r1 - mid.astype(F32))], axis=1), e3t)


def _membership(width, group, ngroups):
    e = (jnp.arange(width)[:, None] // group == jnp.arange(ngroups)[None, :]).astype(BF)
    return e, jnp.tile(e.T, (3, 1))


def _fold(width, group):
    return (jnp.arange(width)[:, None] % group == jnp.arange(group)[None, :]).astype(BF)


def _heads_norm(t, w_x, e, e3t):
    r = lax.rsqrt(_dot(_bf(t * t), e) * (1.0 / HEAD_DIM) + EPS)
    r_x = _group_bcast(r, e3t)
    return t * r_x * w_x, r_x


def _heads_norm_bwd(t, r_x, w_x, d, e, e3t):
    wd = d * w_x
    corr = _group_bcast(_dot(_bf(t * wd), e) * (1.0 / HEAD_DIM), e3t)
    return r_x * wd - t * (r_x * r_x * r_x) * corr, jnp.sum(d * t * r_x, axis=0, keepdims=True)


def _stack_heads(a, hk):
    return jnp.concatenate([a[:, (hk * GRP + g) * HEAD_DIM:(hk * GRP + g + 1) * HEAD_DIM] for g in range(GRP)], axis=0)


def _stack_cols(a, hk):
    return jnp.concatenate([a[:, hk * GRP + g:hk * GRP + g + 1] for g in range(GRP)], axis=0)


def _masked_bias(bias):
    qi = jnp.arange(BLOCK)[:, None]
    kj = jnp.arange(2 * BLOCK)[None, :]
    cur_ok = jnp.logical_and(kj >= BLOCK, kj - BLOCK <= qi)
    both_ok = jnp.logical_or(jnp.logical_and(kj < BLOCK, kj > qi), cur_ok)
    return jnp.stack([jnp.where(cur_ok, bias, -1e30), jnp.where(both_ok, bias, -1e30)])


def _attn_consts(qnw, knw):
    eq, eq3t = _membership(ATTN_W, HEAD_DIM, ATTN_HEADS)
    ek, ek3t = _membership(KV_W, HEAD_DIM, ATTN_HEADS)
    return (jnp.tile(qnw, (1, ATTN_HEADS)), jnp.tile(knw, (1, KV_HEADS)), eq, eq3t, ek, ek3t)


def _attn_fwd(q, kv, bias, sinks, consts):
    s = q.shape[0]
    nb = s // BLOCK
    gq = GRP * BLOCK
    bias_t = bias.reshape(2, KV_HEADS, GRP, BLOCK, 2 * BLOCK).transpose(0, 1, 4, 2, 3).reshape(2, KV_HEADS, 2 * BLOCK, gq)
    sink_rows = jnp.repeat(sinks.reshape(KV_HEADS, GRP), BLOCK, axis=1).reshape(KV_HEADS, 1, gq)
    eye = jnp.eye(BLOCK, dtype=BF)

    def body(q_ref, kp_ref, kc_ref, vp_ref, vc_ref, b_ref, bt_ref, sk_ref, skr_ref, eye_ref,
             qw_ref, kw_ref, eq_ref, eq3_ref, ek_ref, ek3_ref, o_ref, lse_ref):
        qn = _bf(_heads_norm(q_ref[...], qw_ref[...], eq_ref[...], eq3_ref[...])[0] * (HEAD_DIM ** -0.5))
        kn = _bf(_heads_norm(jnp.concatenate([kp_ref[...], kc_ref[...]], axis=0), kw_ref[...], ek_ref[...], ek3_ref[...])[0])
        vv = _bf(jnp.concatenate([vp_ref[...], vc_ref[...]], axis=0))
        ones = jnp.ones((2 * BLOCK, HEAD_DIM), BF)
        lses = []
        kss = [slice(hk * HEAD_DIM, (hk + 1) * HEAD_DIM) for hk in range(KV_HEADS)]
        qgs = [_stack_heads(qn, hk) for hk in range(KV_HEADS)]
        sc_ts = [_dot_nt(kn[:, kss[hk]], qgs[hk]) + bt_ref[0, hk] for hk in range(KV_HEADS)]
        m_rows = [jnp.maximum(jnp.max(sc_ts[hk], axis=0, keepdims=True), skr_ref[hk]) for hk in range(KV_HEADS)]
        m8s = [_bf(jnp.broadcast_to(m + jnp.abs(m) * (2.0 ** -7), (8, gq))) for m in m_rows]
        ms = [jnp.concatenate([_dot_nt(eye_ref[...], m8[:, g * BLOCK:(g + 1) * BLOCK])[:, 0:1] for g in range(GRP)], axis=0)
              for m8 in m8s]
        scs = [_dot_nt(qgs[hk], kn[:, kss[hk]]) + b_ref[0, hk * GRP:(hk + 1) * GRP].reshape(gq, 2 * BLOCK)
               for hk in range(KV_HEADS)]
        ps = [_bf(jnp.exp(scs[hk] - ms[hk])) for hk in range(KV_HEADS)]
        pvs = [_dot(ps[hk], jnp.concatenate([vv[:, kss[hk]], ones], axis=1)) for hk in range(KV_HEADS)]
        for hk in range(KV_HEADS):
            m, pv = ms[hk], pvs[hk]
            sink = jnp.concatenate([jnp.full((BLOCK, 1), sk_ref[0, hk * GRP + g], F32) for g in range(GRP)], axis=0)
            den = pv[:, HEAD_DIM:HEAD_DIM + 1] + jnp.exp(sink - m)
            out = pv[:, :HEAD_DIM] * (1.0 / den)
            lse = m + jnp.log(den)
            for g in range(GRP):
                h = hk * GRP + g
                o_ref[:, h * HEAD_DIM:(h + 1) * HEAD_DIM] = out[g * BLOCK:(g + 1) * BLOCK]
                lses.append(lse[g * BLOCK:(g + 1) * BLOCK])
        lse_ref[...] = jnp.concatenate(lses, axis=1)

    cur = lambda w, col=0: pl.BlockSpec((BLOCK, w), lambda i: (i, col))
    prev = lambda w, col=0: pl.BlockSpec((BLOCK, w), lambda i: (jnp.maximum(i - 1, 0), col))
    whole = lambda a: pl.BlockSpec(a.shape, lambda i: (0,) * a.ndim)
    first_or_not = lambda a: pl.BlockSpec((1,) + a.shape[1:], lambda i: (jnp.minimum(i, 1),) + (0,) * (a.ndim - 1))
    return pl.pallas_call(
        body, name="attn_fwd", grid=(nb,),
        in_specs=[cur(ATTN_W), prev(KV_W, 0), cur(KV_W, 0), prev(KV_W, 1), cur(KV_W, 1),
                  first_or_not(bias), first_or_not(bias_t),
                  pl.BlockSpec(memory_space=pltpu.SMEM), whole(sink_rows), whole(eye)] + [_full(c.shape) for c in consts],
        out_specs=[cur(ATTN_W), cur(ATTN_HEADS)],
        out_shape=[jax.ShapeDtypeStruct((s, ATTN_W), F32), jax.ShapeDtypeStruct((s, ATTN_HEADS), F32)],
        compiler_params=_params(dimension_semantics=("arbitrary",)),
    )(q, kv, kv, kv, kv, bias, bias_t, sinks, sink_rows, eye, *consts)


def _conv_taps(xbc, tail):
    ext = jnp.concatenate([tail, xbc], axis=0)
    return [pltpu.roll(ext, CONV_K - 1 - j, axis=0)[8:8 + BLOCK] if j < CONV_K - 1 else xbc for j in range(CONV_K)]


def _softplus(u):
    return jnp.maximum(u, 0.0) + jnp.log(1.0 + jnp.exp(-jnp.abs(u)))


def _tril():
    r = lax.broadcasted_iota(jnp.int32, (BLOCK, BLOCK), 0)
    c = lax.broadcasted_iota(jnp.int32, (BLOCK, BLOCK), 1)
    return r >= c


def _triu():
    r = lax.broadcasted_iota(jnp.int32, (BLOCK, BLOCK), 0)
    c = lax.broadcasted_iota(jnp.int32, (BLOCK, BLOCK), 1)
    return r <= c


def _exact_left(m01, a):
    hi = _bf(a)
    r1 = a - hi.astype(F32)
    mid = _bf(r1)
    return _dot(m01, hi) + _dot(m01, mid) + _dot(m01, _bf(r1 - mid.astype(F32)))


def _ssd_common(conv, dtr, dtb_ref, alog_ref, e3_ref):
    sg = _sig(conv)
    xact = conv * sg
    u = dtr + dtb_ref[...]
    dt = _softplus(u)
    a = -jnp.exp(alog_ref[...])
    trilb = _tril()
    acum = _exact_left(trilb.astype(BF), dt * a) * math.log2(math.e)
    both = _group_bcast(jnp.concatenate([dt, acum], axis=0), e3_ref[...])
    dt_x, acum_x = both[:BLOCK], both[BLOCK:]
    return sg, xact, u, dt, a, trilb, acum, dt_x, acum_x


SSD_CH = 2


def _ssd_fwd(xbc, dt_raw, conv_w, conv_b, dt_bias, a_log, dsk_x, e3t):
    s = xbc.shape[0]
    nc = s // BLOCK
    ch = SSD_CH if nc % SSD_CH == 0 else 1
    rows = ch * BLOCK

    def body(x_ref, tail_ref, dtr_ref, cw_ref, cb_ref, dtb_ref, alog_ref, dsk_ref, e3_ref,
             y_ref, hp_ref, conv_ref, hst, yd_s, yoff_s):
        i = pl.program_id(0)

        @pl.when(i == 0)
        def _():
            hst[...] = jnp.zeros_like(hst)

        for j in range(ch):
            rs = slice(j * BLOCK, (j + 1) * BLOCK)
            tail = jnp.where(i > 0, tail_ref[...], 0.0) if j == 0 else x_ref[j * BLOCK - 8:j * BLOCK, :]
            taps = _conv_taps(x_ref[rs, :], tail)
            conv = cb_ref[...] + sum(taps[t] * cw_ref[t:t + 1, :] for t in range(CONV_K))
            conv_ref[rs, :] = conv
            _, xact, _, _, _, trilb, acum, dt_x, acum_x = _ssd_common(conv, dtr_ref[rs, :], dtb_ref, alog_ref, e3_ref)
            xs = xact[:, :SSM_W]
            acum_t = acum.T
            ea_x = jnp.exp2(acum_x)
            last_x = acum_x[BLOCK - 1:BLOCK, :]
            xdt = xs * dt_x
            xw = xdt * jnp.exp2(last_x - acum_x)
            cd_x = jnp.exp2(last_x)
            hprev = hst[...]
            hp_ref[j] = hprev
            sls = [slice(g * SSM_R * SSM_P, (g + 1) * SSM_R * SSM_P) for g in range(SSM_G)]
            bgs = [_bf(xact[:, SSM_W + g * SSM_N:SSM_W + (g + 1) * SSM_N]) for g in range(SSM_G)]
            cgs = [_bf(xact[:, SSM_W + SSM_G * SSM_N + g * SSM_N:SSM_W + SSM_G * SSM_N + (g + 1) * SSM_N])
                   for g in range(SSM_G)]
            xdt_b, xw_b, hprev_b = _bf(xdt), _bf(xw), _bf(hprev)
            low_half = lax.broadcasted_iota(jnp.int32, (BLOCK, 2 * SSM_P), 1) < SSM_P
            cbs = [_dot_nt(cgs[g], bgs[g]) for g in range(SSM_G)]
            for g in range(SSM_G):
                sl = sls[g]
                yoff_s[:, sl] = _dot(cgs[g], hprev_b[:, sl]) * ea_x[:, sl]
                hst[:, sl] = hprev[:, sl] * cd_x[:, sl] + _dot_tn(bgs[g], xw_b[:, sl])
            for g in range(SSM_G):
                hss = [slice((g * SSM_R + r) * SSM_P, (g * SSM_R + r + 1) * SSM_P) for r in range(SSM_R)]
                mms = [_bf(cbs[g] * jnp.exp2(jnp.where(trilb, acum[:, g * SSM_R + r:g * SSM_R + r + 1]
                                                      - acum_t[g * SSM_R + r:g * SSM_R + r + 1, :], -1e30)))
                       for r in range(SSM_R)]
                for r in range(0, SSM_R, 2):
                    pair = slice(hss[r].start, hss[r + 1].stop)
                    xp = xdt_b[:, pair]
                    rhs = jnp.concatenate([jnp.where(low_half, xp, 0), jnp.where(low_half, 0, xp)], axis=0)
                    yd_s[:, pair] = _dot(jnp.concatenate([mms[r], mms[r + 1]], axis=1), rhs)
            y_ref[rs, :] = yd_s[...] + yoff_s[...] + dsk_ref[...] * xs

    blk = lambda w: pl.BlockSpec((rows, w), lambda i: (i, 0))
    return pl.pallas_call(
        body, name="ssd_fwd", grid=(nc // ch,),
        in_specs=[blk(XBC_W), pl.BlockSpec((8, XBC_W), lambda i: (jnp.maximum(i * (rows // 8) - 1, 0), 0)),
                  blk(SSM_HEADS), _full((CONV_K, XBC_W)), _full((1, XBC_W)), _full((1, SSM_HEADS)),
                  _full((1, SSM_HEADS)), _full((1, SSM_W)), _full((3 * SSM_HEADS, SSM_W))],
        out_specs=[blk(SSM_W), pl.BlockSpec((ch, SSM_N, SSM_W), lambda i: (i, 0, 0)), blk(XBC_W)],
        out_shape=[jax.ShapeDtypeStruct((s, SSM_W), F32), jax.ShapeDtypeStruct((nc, SSM_N, SSM_W), F32),
                   jax.ShapeDtypeStruct((s, XBC_W), F32)],
        scratch_shapes=[pltpu.VMEM((SSM_N, SSM_W), F32), pltpu.VMEM((BLOCK, SSM_W), F32), pltpu.VMEM((BLOCK, SSM_W), F32)],
        compiler_params=_params(dimension_semantics=("arbitrary",)),
    )(xbc, xbc, dt_raw, conv_w, conv_b, dt_bias, a_log, dsk_x, e3t)


def _dsilu(z, sg, silu):
    return sg * (1.0 + (z - silu))


def _mid(x, tgt, o_att, zam, ypre, gab, gate, ssm_nw, rows_all, tm=256):
    s = x.shape[0]
    gw = SSM_W // SSM_G

    r_ap, r_sp = ATTN_W // N_DEV, SSM_W // N_DEV

    def body(x_ref, t_ref, o_ref, zam_ref, yp_ref, gab_ref, gate_ref, nw_ref, rows_h,
             dout_ref, do_ref, dzam_ref, dyp_ref, dgab_ref,
             yag_ref, dya_ref, yn_ref, dyb_ref, mg_ref, dob_ref, gnw_ref, dgate_ref, loss_ref,
             wap_v, wsp_v, wout_v, sem):
        i = pl.program_id(0)

        @pl.when(i == 0)
        def _():
            cps = []
            for d in range(N_DEV):
                for j, (dst, r0, rn) in enumerate(((wap_v, 0, r_ap), (wsp_v, r_ap, r_sp), (wout_v, r_ap + r_sp, r_ap))):
                    cps.append(pltpu.make_async_copy(rows_h.at[d, r0:r0 + rn, :], dst.at[d * rn:(d + 1) * rn, :], sem.at[j]))
            for cp in cps:
                cp.start()
            gnw_ref[...] = jnp.zeros_like(gnw_ref)
            dgate_ref[...] = jnp.zeros_like(dgate_ref)
            loss_ref[...] = jnp.zeros_like(loss_ref)
            for cp in cps:
                cp.wait()

        gate = gate_ref[...]
        nw = nw_ref[...]
        o_att = o_ref[...]
        z_a = zam_ref[:, :ATTN_W].astype(F32)
        s_a = _sig(z_a)
        silu_a = z_a * s_a
        yag = _bf(o_att * silu_a)
        yag_ref[...] = yag
        ypre = yp_ref[...]
        z_m = zam_ref[:, ATTN_W:].astype(F32)
        s_m = _sig(z_m)
        silu_m = z_m * s_m
        yg = ypre * silu_m
        rinv = jnp.concatenate(
            [jnp.broadcast_to(lax.rsqrt(jnp.mean(yg[:, g * gw:(g + 1) * gw] ** 2, axis=-1, keepdims=True) + EPS), (tm, gw))
             for g in range(SSM_G)], axis=1)
        ynr = yg * rinv
        yn = _bf(ynr * nw)
        yn_ref[...] = yn
        y_a = _dot(yag, wap_v[...])
        y_b = _dot(yn, wsp_v[...])
        g_a = _sig(gab_ref[:, :D_MODEL].astype(F32))
        g_b = _sig(gab_ref[:, D_MODEL:].astype(F32))
        merged = _bf(g_a * y_a + g_b * y_b)
        mg_ref[...] = merged
        o = _dot(merged, wout_v[...])
        diff = x_ref[...] + gate * o - t_ref[...]
        loss_ref[...] += (0.5 / D_MODEL) * jnp.sum(diff * diff, axis=(0, 1), keepdims=True)
        dout = diff * (1.0 / D_MODEL)
        dout_ref[...] = dout
        dgate_ref[...] += jnp.sum(dout * o, axis=0, keepdims=True)
        d_o = _bf(dout * gate)
        dob_ref[...] = d_o
        dmerged = _dot_nt(d_o, wout_v[...])
        dy_af = dmerged * g_a
        dy_bf = dmerged * g_b
        dy_a = _bf(dy_af)
        dy_b = _bf(dy_bf)
        dya_ref[...] = dy_a
        dyb_ref[...] = dy_b
        dyag = _dot_nt(dy_a, wap_v[...])
        dyn = _dot_nt(dy_b, wsp_v[...])
        dgab_ref[:, :D_MODEL] = _bf(dy_af * y_a * (1.0 - g_a))
        dgab_ref[:, D_MODEL:] = _bf(dy_bf * y_b * (1.0 - g_b))
        do_ref[...] = dyag * silu_a
        dzam_ref[:, :ATTN_W] = _bf(dyag * o_att * _dsilu(z_a, s_a, silu_a))
        gnw_ref[...] += jnp.sum(dyn * ynr, axis=0, keepdims=True)
        dynw = dyn * nw
        corr = jnp.concatenate(
            [jnp.broadcast_to(jnp.mean((dynw * ynr)[:, g * gw:(g + 1) * gw], axis=-1, keepdims=True), (tm, gw))
             for g in range(SSM_G)], axis=1)
        dyg = rinv * (dynw - ynr * corr)
        dyp_ref[...] = dyg * silu_m
        dzam_ref[:, ATTN_W:] = _bf(dyg * ypre * _dsilu(z_m, s_m, silu_m))

    r1, r2, r3 = _rows(tm, D_MODEL), _rows(tm, SSM_W), _rows(tm, ATTN_W + SSM_W)
    sd = jax.ShapeDtypeStruct
    return pl.pallas_call(
        body, name="mid", grid=(s // tm,),
        in_specs=[r1, r1, r1, r3, r2, r2, _full((1, D_MODEL)), _full((1, SSM_W)), ANY],
        out_specs=[r1, r1, r3, r2, r2, r1, r1, r2, r1, r1, r1,
                   _full((1, SSM_W)), _full((1, D_MODEL)), _full((1, 1))],
        out_shape=[sd((s, D_MODEL), F32), sd((s, ATTN_W), F32), sd((s, ATTN_W + SSM_W), BF), sd((s, SSM_W), F32),
                   sd((s, 2 * D_MODEL), BF),
                   sd((s, ATTN_W), BF), sd((s, D_MODEL), BF), sd((s, SSM_W), BF), sd((s, D_MODEL), BF),
                   sd((s, D_MODEL), BF), sd((s, D_MODEL), BF),
                   sd((1, SSM_W), F32), sd((1, D_MODEL), F32), sd((1, 1), F32)],
        scratch_shapes=[pltpu.VMEM((ATTN_W, D_MODEL), BF), pltpu.VMEM((SSM_W, D_MODEL), BF), pltpu.VMEM((D_MODEL, D_MODEL), BF),
                        pltpu.SemaphoreType.DMA((3,))],
        compiler_params=_params(dimension_semantics=("arbitrary",)),
    )(x, tgt, o_att, zam, ypre, gab, gate, ssm_nw, rows_all)


def _attn_bwd(q, kv, bias, sinks, consts, o_att, lse, d_o):
    s = q.shape[0]
    nb = s // BLOCK
    folds = (_fold(ATTN_W, HEAD_DIM), _fold(KV_W, HEAD_DIM))

    def body(q_ref, kp_ref, kc_ref, vp_ref, vc_ref, b_ref, skv_ref, qw_ref, kw_ref, eq_ref, eq3_ref, ek_ref, ek3_ref,
             fq_ref, fk_ref, o_ref, lse_ref, do_ref,
             dq_ref, dkv_ref, dss_ref, gqw_ref, gkw_ref, gsk_ref, ckn, cv, dqn_s, dkn_s, dv_s, gq_x, gk_x):
        i = pl.program_id(0)
        kw, ek, ek3 = kw_ref[...], ek_ref[...], ek3_ref[...]

        @pl.when(i == 0)
        def _():
            for ref in (ckn, cv, dss_ref, gq_x, gk_x, gsk_ref):
                ref[...] = jnp.zeros_like(ref)

        @pl.when(i < nb)
        def _():
            qw, eq, eq3 = qw_ref[...], eq_ref[...], eq3_ref[...]
            qf = q_ref[...]
            qnf, rq_x = _heads_norm(qf, qw, eq, eq3)
            qn = _bf(qnf * (HEAD_DIM ** -0.5))
            kf = jnp.concatenate([kp_ref[...], kc_ref[...]], axis=0)
            knf, rk_x = _heads_norm(kf, kw, ek, ek3)
            kn = _bf(knf)
            vv = _bf(jnp.concatenate([vp_ref[...], vc_ref[...]], axis=0))
            d_of = do_ref[...]
            d_ob = _bf(d_of)
            lse_all = lse_ref[...]
            delta = _dot(_bf(d_of * o_ref[...]), eq)
            gsk_ref[...] += jnp.sum(-jnp.exp(skv_ref[...] - lse_all) * delta, axis=0, keepdims=True)
            kss = [slice(hk * HEAD_DIM, (hk + 1) * HEAD_DIM) for hk in range(KV_HEADS)]
            qgs = [_stack_heads(qn, hk) for hk in range(KV_HEADS)]
            d_ogs = [_stack_heads(d_ob, hk) for hk in range(KV_HEADS)]
            scs = [_dot_nt(qgs[hk], kn[:, kss[hk]]) + b_ref[0, hk * GRP:(hk + 1) * GRP].reshape(GRP * BLOCK, 2 * BLOCK)
                   for hk in range(KV_HEADS)]
            dps = [_dot_nt(d_ogs[hk], vv[:, kss[hk]]) for hk in range(KV_HEADS)]
            ps = [jnp.exp(scs[hk] - _stack_cols(lse_all, hk)) for hk in range(KV_HEADS)]
            dss = [ps[hk] * (dps[hk] - _stack_cols(delta, hk)) for hk in range(KV_HEADS)]
            pbs = [_bf(p) for p in ps]
            dsbs = [_bf(ds) for ds in dss]
            for hk in range(KV_HEADS):
                dss_ref[hk * GRP:(hk + 1) * GRP] += dss[hk].reshape(GRP, BLOCK, 2 * BLOCK)
            for hk in range(KV_HEADS):
                dv_s[:, kss[hk]] = _dot_tn(pbs[hk], d_ogs[hk])
                dkn_s[:, kss[hk]] = _dot_tn(dsbs[hk], qgs[hk])
            dqns = [_dot(dsbs[hk], kn[:, kss[hk]]) * (HEAD_DIM ** -0.5) for hk in range(KV_HEADS)]
            for hk in range(KV_HEADS):
                for g in range(GRP):
                    h = hk * GRP + g
                    dqn_s[:, h * HEAD_DIM:(h + 1) * HEAD_DIM] = dqns[hk][g * BLOCK:(g + 1) * BLOCK]
            dq, gq = _heads_norm_bwd(qf, rq_x, qw, dqn_s[...], eq, eq3)
            dq_ref[...] = _bf(dq)
            gq_x[...] += gq
            dk, gk = _heads_norm_bwd(kf[:BLOCK], rk_x[:BLOCK], kw, ckn[...] + dkn_s[0:BLOCK, :], ek, ek3)
            dkv_ref[:, :KV_W] = _bf(dk)
            gk_x[...] += gk
            dkv_ref[:, KV_W:] = _bf(cv[...] + dv_s[0:BLOCK, :])
            ckn[...] = dkn_s[BLOCK:2 * BLOCK, :]
            cv[...] = dv_s[BLOCK:2 * BLOCK, :]

        @pl.when(i == nb)
        def _():
            kc = kc_ref[...]
            dk, gk = _heads_norm_bwd(kc, _heads_norm(kc, kw, ek, ek3)[1], kw, ckn[...], ek, ek3)
            dkv_ref[:, :KV_W] = _bf(dk)
            dkv_ref[:, KV_W:] = _bf(cv[...])
            gqw_ref[...] = _group_sum(jnp.broadcast_to(gq_x[...], (8, ATTN_W)), fq_ref[...])[0:1]
            gkw_ref[...] = _group_sum(jnp.broadcast_to(gk_x[...] + gk, (8, KV_W)), fk_ref[...])[0:1]

    last = nb - 1
    cur = lambda w, col=0: pl.BlockSpec((BLOCK, w), lambda i: (jnp.minimum(i, last), col))
    prev = lambda w, col=0: pl.BlockSpec((BLOCK, w), lambda i: (jnp.maximum(jnp.minimum(i, last) - 1, 0), col))
    late = lambda w: pl.BlockSpec((BLOCK, w), lambda i: (jnp.maximum(i - 1, 0), 0))
    sd = jax.ShapeDtypeStruct
    return pl.pallas_call(
        body, name="attn_bwd", grid=(nb + 1,),
        in_specs=[cur(ATTN_W), prev(KV_W, 0), cur(KV_W, 0), prev(KV_W, 1), cur(KV_W, 1),
                  pl.BlockSpec((1, ATTN_HEADS, BLOCK, 2 * BLOCK), lambda i: (jnp.minimum(i, 1), 0, 0, 0)),
                  _full((1, ATTN_HEADS))]
                 + [_full(c.shape) for c in consts + folds] + [cur(ATTN_W), cur(ATTN_HEADS), cur(ATTN_W)],
        out_specs=[cur(ATTN_W), late(2 * KV_W),
                   pl.BlockSpec((ATTN_HEADS, BLOCK, 2 * BLOCK), lambda i: (0, 0, 0)),
                   _full((1, HEAD_DIM)), _full((1, HEAD_DIM)), _full((1, ATTN_HEADS))],
        out_shape=[sd((s, ATTN_W), BF), sd((s, 2 * KV_W), BF),
                   sd((ATTN_HEADS, BLOCK, 2 * BLOCK), F32), sd((1, HEAD_DIM), F32), sd((1, HEAD_DIM), F32),
                   sd((1, ATTN_HEADS), F32)],
        scratch_shapes=[pltpu.VMEM((BLOCK, KV_W), F32), pltpu.VMEM((BLOCK, KV_W), F32),
                        pltpu.VMEM((BLOCK, ATTN_W), F32), pltpu.VMEM((2 * BLOCK, KV_W), F32),
                        pltpu.VMEM((2 * BLOCK, KV_W), F32), pltpu.VMEM((1, ATTN_W), F32), pltpu.VMEM((1, KV_W), F32)],
        compiler_params=_params(dimension_semantics=("arbitrary",)),
    )(q, kv, kv, kv, kv, bias, sinks, *consts, *folds, o_att, lse, d_o)


def _ssd_bwd(xbc, conv_all, dt_raw, conv_w, dt_bias, a_log, dsk_x, e_mat, e3t, hprev_all, dy_all):
    s = xbc.shape[0]
    nc = s // BLOCK
    ch = 1
    rows = ch * BLOCK
    nsteps = nc // ch
    gw = SSM_R * SSM_P
    b0, c0 = SSM_W, SSM_W + SSM_G * SSM_N

    def body(x_ref, conv_ref, dtr_ref, cw_ref, dtb_ref, alog_ref, dsk_ref, e_ref, e3_ref, hp_ref, dy_ref,
             dx_ref, ddt_ref, gcw_ref, gcb_ref, gdtb_ref, galog_ref, gdsk_ref,
             dh, nhead, gdskx, dxdt_s, dbc_s, dxd_s):
        def chunk_bwd(j):
            rs = slice(j * BLOCK, (j + 1) * BLOCK)
            conv = conv_ref[rs, :]
            sg, xact, u, dt, a, trilb, acum, dt_x, acum_x = _ssd_common(conv, dtr_ref[rs, :], dtb_ref, alog_ref, e3_ref)
            xs = xact[:, :SSM_W]
            acum_t = acum.T
            ea_x = jnp.exp2(acum_x)
            last_x = acum_x[BLOCK - 1:BLOCK, :]
            dte_x = jnp.exp2(last_x - acum_x)
            cd_x = jnp.exp2(last_x)
            xdt = xs * dt_x
            xw = xdt * dte_x
            hprev = hp_ref[j]
            dhn = dh[...]
            dy = dy_ref[rs, :]
            gdskx[...] += jnp.sum(dy * xs, axis=0, keepdims=True)
            dyea = dy * ea_x
            lane = lax.broadcasted_iota(jnp.int32, (BLOCK, SSM_HEADS), 1)
            dacum = jnp.zeros((BLOCK, SSM_HEADS), F32)
            dacc_x, dlast_x = [], []
            sls = [slice(g * gw, (g + 1) * gw) for g in range(SSM_G)]
            bgs = [_bf(xact[:, b0 + g * SSM_N:b0 + (g + 1) * SSM_N]) for g in range(SSM_G)]
            cgs = [_bf(xact[:, c0 + g * SSM_N:c0 + (g + 1) * SSM_N]) for g in range(SSM_G)]
            hpgs = [_bf(hprev[:, sl]) for sl in sls]
            dhgs = [_bf(dhn[:, sl]) for sl in sls]
            dyeags = [_bf(dyea[:, sl]) for sl in sls]
            xwgs = [_bf(xw[:, sl]) for sl in sls]
            xdt_b, dy_b = _bf(xdt), _bf(dy)
            low_half = lax.broadcasted_iota(jnp.int32, (BLOCK, 2 * SSM_P), 1) < SSM_P
            cbs = [_dot_nt(cgs[g], bgs[g]) for g in range(SSM_G)]
            gmats = [_dot(cgs[g], hpgs[g]) for g in range(SSM_G)]
            dxws = [_dot(bgs[g], dhgs[g]) for g in range(SSM_G)]
            dcgs = [_dot_nt(dyeags[g], hpgs[g]) for g in range(SSM_G)]
            dbgs = [_dot_nt(xwgs[g], dhgs[g]) for g in range(SSM_G)]
            for g in range(SSM_G):
                sl = sls[g]
                dh[:, sl] = dhn[:, sl] * cd_x[:, sl] + _dot_tn(cgs[g], dyeags[g])
                dxdt_s[:, sl] = dxws[g] * dte_x[:, sl]
                dacc_x.append(dy[:, sl] * gmats[g] * ea_x[:, sl] - dxws[g] * xw[:, sl])
                dlast_x.append(jnp.sum(dxws[g] * xw[:, sl], axis=0, keepdims=True)
                               + jnp.sum(dhn[:, sl] * hprev[:, sl], axis=0, keepdims=True) * cd_x[:, sl])
            for g in range(SSM_G):
                bg, cg, cb, dbg, dcg = bgs[g], cgs[g], cbs[g], dbgs[g], dcgs[g]
                hss = [slice((g * SSM_R + r) * SSM_P, (g * SSM_R + r + 1) * SSM_P) for r in range(SSM_R)]
                lms = [jnp.exp2(jnp.where(trilb, acum[:, g * SSM_R + r:g * SSM_R + r + 1]
                                         - acum_t[g * SSM_R + r:g * SSM_R + r + 1, :], -1e30)) for r in range(SSM_R)]
                mms = [cb * lm for lm in lms]
                mmbs = [_bf(mm) for mm in mms]
                dms = []
                for r in range(0, SSM_R, 2):
                    pair = slice(hss[r].start, hss[r + 1].stop)
                    xp, dyp = xdt_b[:, pair], dy_b[:, pair]
                    dmp = _dot_nt(dyp, jnp.concatenate([jnp.where(low_half, xp, 0), jnp.where(low_half, 0, xp)], axis=0))
                    dms += [dmp[:, :BLOCK], dmp[:, BLOCK:]]
                    dxd_s[:, pair] = _dot_tn(jnp.concatenate([mmbs[r], mmbs[r + 1]], axis=0),
                                             jnp.concatenate([jnp.where(low_half, dyp, 0), jnp.where(low_half, 0, dyp)], axis=0))
                dcb = sum(dms[r] * lms[r] for r in range(SSM_R))
                wms = [dms[r] * mms[r] for r in range(SSM_R)]
                antis = [_bf(wm - wm.T) for wm in wms]
                for r in range(SSM_R):
                    dacum = dacum + _dot(antis[r], (lane == g * SSM_R + r).astype(BF))
                dcbb = _bf(dcb)
                dbc_s[:, g * SSM_N:(g + 1) * SSM_N] = dbg + _dot_tn(dcbb, cg)
                dbc_s[:, SSM_G * SSM_N + g * SSM_N:SSM_G * SSM_N + (g + 1) * SSM_N] = dcg + _dot(dcbb, bg)
            dxdt = dxdt_s[...] + dxd_s[...]
            dxs = dy * dsk_ref[...] + dxdt * dt_x
            red = _group_sum(jnp.concatenate(
                [dxdt * xs, jnp.concatenate(dacc_x, axis=1),
                 jnp.broadcast_to(jnp.concatenate(dlast_x, axis=1), (8, SSM_W))], axis=0), e_ref[...])
            row = lax.broadcasted_iota(jnp.int32, (BLOCK, SSM_HEADS), 0)
            dacum = dacum + red[BLOCK:2 * BLOCK] + jnp.where(row == BLOCK - 1, red[2 * BLOCK:2 * BLOCK + 1], 0.0)
            ddta = _exact_left(_triu().astype(BF), dacum)
            ddt = red[:BLOCK] + ddta * a
            galog_ref[...] += jnp.sum(ddta * dt, axis=0, keepdims=True) * a
            du = ddt * _sig(u)
            ddt_ref[rs, :] = _bf(du)
            gdtb_ref[...] += jnp.sum(du, axis=0, keepdims=True)
            dconv = jnp.concatenate([dxs, dbc_s[...]], axis=1) * _dsilu(conv, sg, xact)
            gcb_ref[...] += jnp.sum(dconv, axis=0, keepdims=True)
            ext2 = jnp.concatenate([dconv, nhead[...]], axis=0)
            ahead = [pltpu.roll(ext2, BLOCK + 8 - (CONV_K - 1 - j), axis=0)[0:BLOCK] if j < CONV_K - 1 else dconv
                     for j in range(CONV_K)]
            dx_ref[rs, :] = _bf(sum(ahead[j] * cw_ref[j:j + 1, :] for j in range(CONV_K)))
            xraw = x_ref[rs, :]
            gcw_ref[...] += jnp.concatenate([jnp.sum(ahead[j] * xraw, axis=0, keepdims=True) for j in range(CONV_K)], axis=0)
            nhead[...] = dconv[0:8]

        i = pl.program_id(0)

        @pl.when(i == 0)
        def _():
            for ref in (dh, nhead, gdskx, gcw_ref, gcb_ref, gdtb_ref, galog_ref, gdsk_ref):
                ref[...] = jnp.zeros_like(ref)

        for j in reversed(range(ch)):
            chunk_bwd(j)

        @pl.when(i == nsteps - 1)
        def _():
            gdsk_ref[...] = _group_sum(jnp.broadcast_to(gdskx[...], (8, SSM_W)), e_ref[...])[0:1]

    chunk = lambda w: pl.BlockSpec((rows, w), lambda i: (nsteps - 1 - i, 0))
    sd = jax.ShapeDtypeStruct
    return pl.pallas_call(
        body, name="ssd_bwd", grid=(nsteps,),
        in_specs=[chunk(XBC_W), chunk(XBC_W),
                  chunk(SSM_HEADS), _full((CONV_K, XBC_W)), _full((1, SSM_HEADS)),
                  _full((1, SSM_HEADS)), _full((1, SSM_W)), _full((SSM_W, SSM_HEADS)), _full((3 * SSM_HEADS, SSM_W)),
                  pl.BlockSpec((ch, SSM_N, SSM_W), lambda i: (nsteps - 1 - i, 0, 0)), chunk(SSM_W)],
        out_specs=[chunk(XBC_W), chunk(SSM_HEADS), _full((CONV_K, XBC_W)), _full((1, XBC_W)),
                   _full((1, SSM_HEADS)), _full((1, SSM_HEADS)), _full((1, SSM_HEADS))],
        out_shape=[sd((s, XBC_W), BF), sd((s, SSM_HEADS), BF), sd((CONV_K, XBC_W), F32), sd((1, XBC_W), F32),
                   sd((1, SSM_HEADS), F32), sd((1, SSM_HEADS), F32), sd((1, SSM_HEADS), F32)],
        scratch_shapes=[pltpu.VMEM((SSM_N, SSM_W), F32), pltpu.VMEM((8, XBC_W), F32),
                        pltpu.VMEM((1, SSM_W), F32), pltpu.VMEM((BLOCK, SSM_W), F32),
                        pltpu.VMEM((BLOCK, 2 * SSM_G * SSM_N), F32), pltpu.VMEM((BLOCK, SSM_W), F32)],
        compiler_params=_params(dimension_semantics=("arbitrary",)),
    )(xbc, conv_all, dt_raw, conv_w, dt_bias, a_log, dsk_x, e_mat, e3t, hprev_all, dy_all)


def _dh(x, dout, norm_w, scale, dsegs, w_t, tm=256):
    s = x.shape[0]

    def body(x_ref, dout_ref, nw_ref, sc_ref, *rest):
        d_refs, w_hbm = rest[:NSEG], rest[NSEG]
        gx_ref, dshift_ref, dscale_ref, gnw_ref = rest[NSEG + 1:NSEG + 5]
        w_vm, sem = rest[NSEG + 5], rest[NSEG + 6]
        first = pl.program_id(0) == 0
        cps = [pltpu.make_async_copy(w_hbm.at[SEG_OFF[j]:SEG_OFF[j + 1], :], w_vm.at[SEG_OFF[j]:SEG_OFF[j + 1], :], sem.at[j])
               for j in range(NSEG)]

        def tile(waiting):
            dh = None
            for j in range(NSEG):
                if waiting:
                    cps[j].wait()
                part = _dot(d_refs[j][...], w_vm[SEG_OFF[j]:SEG_OFF[j + 1], :])
                dh = part if dh is None else dh + part
            xv = x_ref[...]
            r = lax.rsqrt(jnp.mean(xv * xv, axis=-1, keepdims=True) + EPS)
            xn = xv * r
            nw = nw_ref[...]
            sc1 = 1.0 + sc_ref[...]
            dshift_ref[...] += jnp.sum(dh, axis=0, keepdims=True)
            dhxn = jnp.sum(dh * xn, axis=0, keepdims=True)
            dscale_ref[...] += dhxn * nw
            gnw_ref[...] += dhxn * sc1
            dxn = dh * (nw * sc1)
            gx_ref[...] = dout_ref[...] + r * (dxn - xn * jnp.mean(xn * dxn, axis=-1, keepdims=True))

        @pl.when(first)
        def _():
            for cp in cps:
                cp.start()
            for ref in (dshift_ref, dscale_ref, gnw_ref):
                ref[...] = jnp.zeros_like(ref)
            tile(True)

        @pl.when(jnp.logical_not(first))
        def _():
            tile(False)

    vec = _full((1, D_MODEL))
    sd = jax.ShapeDtypeStruct
    return pl.pallas_call(
        body, name="dh", grid=(s // tm,),
        in_specs=[_rows(tm, D_MODEL), _rows(tm, D_MODEL), vec, vec] + [_rows(tm, w) for w in SEG_W] + [ANY],
        out_specs=[_rows(tm, D_MODEL), vec, vec, vec],
        out_shape=[sd((s, D_MODEL), F32), sd((1, D_MODEL), F32), sd((1, D_MODEL), F32), sd((1, D_MODEL), F32)],
        scratch_shapes=[pltpu.VMEM((IN_W, D_MODEL), BF), pltpu.SemaphoreType.DMA((NSEG,))],
        compiler_params=_params(dimension_semantics=("arbitrary",)),
    )(x, dout, norm_w, scale, *dsegs, w_t)


def _gw_seg(h, dseg, name, tm=1024):
    s, w = dseg.shape
    tn = min(w, 1024)
    tm = min(tm, s)
    nm = s // tm

    def body(h_ref, d_ref, o_ref, acc):
        m = pl.program_id(1)

        @pl.when(m == 0)
        def _():
            acc[...] = jnp.zeros_like(acc)

        acc[...] += _dot_tn(d_ref[...], h_ref[...])

        @pl.when(m == nm - 1)
        def _():
            o_ref[...] = _bf(acc[...])

    return pl.pallas_call(
        body, name=name, grid=(w // tn, nm),
        in_specs=[pl.BlockSpec((tm, D_MODEL), lambda n, m: (m, 0)), pl.BlockSpec((tm, tn), lambda n, m: (m, n))],
        out_specs=pl.BlockSpec((tn, D_MODEL), lambda n, m: (n, 0)),
        out_shape=jax.ShapeDtypeStruct((w, D_MODEL), BF),
        scratch_shapes=[pltpu.VMEM((tn, D_MODEL), F32)],
        compiler_params=_params(dimension_semantics=("arbitrary", "arbitrary")),
    )(h, dseg)


def _gw_in(h, dsegs):
    return [_gw_seg(h, d, "gw_in_%d" % j) for j, d in enumerate(dsegs)]


def _local_step(x, tgt, shift, scale, gate, w_t, rows_fn, norm_w, qnw, knw, rel_bias, sinks,
                conv_w, conv_b, dt_bias, a_log, d_skip, ssm_nw, after_mid=None, after_gw=None):
    oh_t = _bucket_onehot_t()
    bias = _masked_bias(_bias_dense(rel_bias.T, oh_t).reshape(ATTN_HEADS, BLOCK, 2 * BLOCK))
    *segs, h = _inproj(x, norm_w, scale, shift, w_t)
    q, kv, zam, xbc, dtr, gab = segs
    consts = _attn_consts(qnw, knw)
    o_att, lse = _attn_fwd(q, kv, bias, sinks, consts)
    e_mat, e3t = _membership(SSM_W, SSM_P, SSM_HEADS)
    dsk_x = jnp.repeat(d_skip, SSM_P, axis=1)
    ypre, hprev, conv = _ssd_fwd(xbc, dtr, conv_w, conv_b, dt_bias, a_log, dsk_x, e3t)
    (dout, d_o, dzam, dyp, dgab, yag, dy_a, yn, dy_b, merged, dob, g_ssm_nw, dgate, loss) = _mid(
        x, tgt, o_att, zam, ypre, gab, gate, ssm_nw, rows_fn(ypre))
    g_wap = _gw_seg(dy_a, yag, "gw_attn_proj")
    g_wsp = _gw_seg(dy_b, yn, "gw_ssm_proj")
    g_wout = _gw_seg(dob, merged, "gw_out")
    zero = after_mid(g_wap, g_wsp, g_wout) if after_mid is not None else 0.0
    dq, dkv, dss, g_qnw, g_knw, g_sinks = _attn_bwd(q, kv, bias, sinks + zero, consts, o_att, lse, d_o)
    g_rel = _bias_grad(dss.reshape(ATTN_HEADS, BLOCK * 2 * BLOCK), oh_t).T
    dxbc, ddt, g_cw, g_cb, g_dtb, g_alog, g_dsk = _ssd_bwd(
        xbc, conv, dtr, conv_w, dt_bias, a_log, dsk_x, e_mat, e3t, hprev, dyp)
    dsegs = (dq, dkv, dzam, dxbc, ddt, dgab)
    g_ws = _gw_in(h, dsegs)
    zero = after_gw(g_ws) if after_gw is not None else 0.0
    gx, dshift, dscale, g_nw = _dh(x, dout, norm_w + zero, scale, dsegs, w_t)
    return dict(loss=loss, grad_x=gx, dmod=jnp.concatenate([dshift, dscale, dgate], axis=1), g_ws=g_ws,
                g_wap=g_wap, g_wsp=g_wsp, g_wout=g_wout, g_norm_w=g_nw, g_qnw=g_qnw, g_knw=g_knw, g_rel=g_rel,
                g_sinks=g_sinks, g_conv_w=g_cw, g_conv_b=g_cb, g_dt_bias=g_dtb, g_a_log=g_alog, g_d_skip=g_dsk,
                g_ssm_nw=g_ssm_nw)


def _me():
    return lax.axis_index("x"), lax.axis_index("y"), lax.axis_index("c")


def _flip(v, bit):
    return 1 - v if bit else v


def _ag_direct(v, name):
    def body(v_ref, out_ref, send_sems, recv_sems, local_sem):
        x, y, c = _me()
        me = 4 * x + 2 * y + c
        mine = pltpu.make_async_copy(v_ref, out_ref.at[me], local_sem)
        mine.start()
        peers = [(_flip(x, k >> 2 & 1), _flip(y, k >> 1 & 1), _flip(c, k & 1)) for k in range(1, N_DEV)]
        sends = [pltpu.make_async_remote_copy(
            src_ref=v_ref, dst_ref=out_ref.at[me], send_sem=send_sems.at[j], recv_sem=recv_sems.at[j],
            device_id=p, device_id_type=MESH) for j, p in enumerate(peers)]
        for cp in sends:
            cp.start()
        for j, (px, py, pc) in enumerate(peers):
            pltpu.make_async_remote_copy(
                src_ref=v_ref, dst_ref=out_ref.at[4 * px + 2 * py + pc], send_sem=send_sems.at[j],
                recv_sem=recv_sems.at[j], device_id=(px, py, pc), device_id_type=MESH).wait_recv()
        for cp in sends:
            cp.wait_send()
        mine.wait()

    vm = pl.BlockSpec(memory_space=pltpu.VMEM)
    return pl.pallas_call(
        body, name=name, out_shape=jax.ShapeDtypeStruct((N_DEV,) + v.shape, v.dtype),
        in_specs=[vm], out_specs=vm,
        scratch_shapes=[pltpu.SemaphoreType.DMA((N_DEV - 1,)), pltpu.SemaphoreType.DMA((N_DEV - 1,)),
                        pltpu.SemaphoreType.DMA],
        compiler_params=_params(),
    )(v)


def _gather_mod(v, w_ada, b_piece):
    ncols = w_ada.shape[1]

    def body(v_ref, w_ref, b_ref, rows_ref, mods_ref, piece, send_sems, recv_sems, local_sems):
        x, y, c = _me()
        me = 4 * x + 2 * y + c
        peers = _peers(x, y, c)

        def exchange(src, dst, rnd):
            mine = pltpu.make_async_copy(src, dst.at[me], local_sems.at[rnd])
            mine.start()
            sends = [pltpu.make_async_remote_copy(
                src_ref=src, dst_ref=dst.at[me], send_sem=send_sems.at[rnd, j], recv_sem=recv_sems.at[rnd, j],
                device_id=p, device_id_type=MESH) for j, p in enumerate(peers)]
            for cp in sends:
                cp.start()
            for j, (px, py, pc) in enumerate(peers):
                pltpu.make_async_remote_copy(
                    src_ref=src, dst_ref=dst.at[4 * px + 2 * py + pc], send_sem=send_sems.at[rnd, j],
                    recv_sem=recv_sems.at[rnd, j], device_id=(px, py, pc), device_id_type=MESH).wait_recv()
            for cp in sends:
                cp.wait_send()
            mine.wait()

        exchange(v_ref, rows_ref, 0)
        c_all = rows_ref[:, 0, :D_MODEL]
        piece[...] = _dot(_bf(_silu(c_all)), _bf(w_ref[...])) + b_ref[...]
        exchange(piece, mods_ref, 1)

    vm = pl.BlockSpec(memory_space=pltpu.VMEM)
    return pl.pallas_call(
        body, name="gather_mod",
        out_shape=(jax.ShapeDtypeStruct((N_DEV,) + v.shape, F32), jax.ShapeDtypeStruct((N_DEV, N_DEV, ncols), F32)),
        in_specs=[vm, vm, vm], out_specs=(vm, vm),
        scratch_shapes=[pltpu.VMEM((N_DEV, ncols), F32), pltpu.SemaphoreType.DMA((2, N_DEV - 1)),
                        pltpu.SemaphoreType.DMA((2, N_DEV - 1)), pltpu.SemaphoreType.DMA((2,))],
        compiler_params=_params(),
    )(v, w_ada, b_piece)


def _ag_relayed(v, name, chunks=1):
    rows = v.shape[0] // chunks
    assert rows * chunks == v.shape[0] and rows % 8 == 0

    def body(v_ref, out_ref, token, send_sems, recv_sems, local_sem):
        token[...] = jnp.zeros_like(token)
        x, y, c = _me()
        flip_x, flip_y = 1 - x, 1 - y
        ax, ay = c * x + (1 - c) * flip_x, c * flip_y + (1 - c) * y
        bx, by = c * flip_x + (1 - c) * x, c * y + (1 - c) * flip_y
        me, sib = (x, y, c), (x, y, 1 - c)
        a, b, dg = (ax, ay, c), (bx, by, c), (flip_x, flip_y, c)
        sa, sb, sdg = (bx, by, 1 - c), (ax, ay, 1 - c), (flip_x, flip_y, 1 - c)

        def piece(ref, k):
            return ref.at[pl.ds(k * rows, rows), :]

        def slot(px, py, pc):
            return out_ref.at[4 * px + 2 * py + pc]

        def copy(n, k, block, to, src=None):
            return pltpu.make_async_remote_copy(
                src_ref=piece(slot(*block) if src is None else src, k), dst_ref=piece(slot(*block), k),
                send_sem=send_sems.at[n * chunks + k], recv_sem=recv_sems.at[n * chunks + k],
                device_id=to, device_id_type=MESH)

        mine = pltpu.make_async_copy(v_ref, slot(*me), local_sem)
        mine.start()
        started = [copy(n, k, me, to, src=v_ref) for k in range(chunks) for n, to in ((1, a), (2, b), (0, sib))]
        for cp in started:
            cp.start()

        def arrived(n, k, block, then):
            copy(n, k, block, me).wait_recv()
            for n2, to in then:
                started.append(copy(n2, k, block, to))
                started[-1].start()

        for k in range(chunks):
            arrived(1, k, a, ((3, b), (4, sib)))
            arrived(2, k, b, ((5, sib),))
        for k in range(chunks):
            arrived(3, k, dg, ((6, sib),))
        for k in range(chunks):
            for n, block in ((0, sib), (4, sa), (5, sb), (6, sdg)):
                copy(n, k, block, me).wait_recv()
        for cp in started:
            cp.wait_send()
        mine.wait()

    out, token = pl.pallas_call(
        body, name=name,
        out_shape=(jax.ShapeDtypeStruct((N_DEV,) + v.shape, v.dtype), jax.ShapeDtypeStruct((8, 128), v.dtype)),
        in_specs=[ANY], out_specs=(ANY, pl.BlockSpec(memory_space=pltpu.VMEM)),
        scratch_shapes=[pltpu.SemaphoreType.DMA((7 * chunks,)), pltpu.SemaphoreType.DMA((7 * chunks,)),
                        pltpu.SemaphoreType.DMA],
        compiler_params=_params(),
    )(v)
    return out, token[0:1, 0:1]


HBM = pl.BlockSpec(memory_space=pltpu.HBM)
SEM = pl.BlockSpec(memory_space=pltpu.SEMAPHORE)
EFFECT = pltpu.SideEffectType.DATAFLOW_SIDE_EFFECTING


def _peers(x, y, c):
    return [(_flip(x, k >> 2 & 1), _flip(y, k >> 1 & 1), _flip(c, k & 1)) for k in range(1, N_DEV)]


def _exchange_start(src, land, gather, name):
    def body(src_ref, land_ref, send_sems, recv_sems, src_thru, land_thru, token):
        x, y, c = _me()
        me = 4 * x + 2 * y + c
        for j, (px, py, pc) in enumerate(_peers(x, y, c)):
            pltpu.make_async_remote_copy(
                src_ref=src_ref if gather else src_ref.at[4 * px + 2 * py + pc], dst_ref=land_ref.at[me],
                send_sem=send_sems.at[j], recv_sem=recv_sems.at[j], device_id=(px, py, pc), device_id_type=MESH).start()
        token[...] = jnp.zeros_like(token)

    sems = pltpu.SemaphoreType.DMA((N_DEV - 1,))
    out = pl.pallas_call(
        body, name=name,
        out_shape=(sems, sems, pltpu.HBM(src.shape, src.dtype), pltpu.HBM(land.shape, land.dtype),
                   jax.ShapeDtypeStruct((8, 128), F32)),
        in_specs=(HBM, HBM), out_specs=(SEM, SEM, HBM, HBM, pl.BlockSpec(memory_space=pltpu.VMEM)),
        input_output_aliases={0: 2, 1: 3},
        compiler_params=pltpu.CompilerParams(has_side_effects=EFFECT),
    )(pltpu.with_memory_space_constraint(src, pltpu.HBM), pltpu.with_memory_space_constraint(land, pltpu.HBM))
    return out[:4], out[4][0, 0]


def _exchange_wait(started, after, gather, name):
    send_sems, recv_sems, src_thru, land_thru = started

    def body(src_ref, land_ref, send_sems, recv_sems, after_ref, src_dead, got_ref):
        x, y, c = _me()
        for j, (px, py, pc) in enumerate(_peers(x, y, c)):
            pid = 4 * px + 2 * py + pc
            cp = pltpu.make_async_remote_copy(
                src_ref=src_ref if gather else src_ref.at[pid], dst_ref=land_ref.at[pid],
                send_sem=send_sems.at[j], recv_sem=recv_sems.at[j], device_id=(px, py, pc), device_id_type=MESH)
            cp.wait_send()
            cp.wait_recv()

    return pl.pallas_call(
        body, name=name,
        out_shape=(pltpu.HBM(src_thru.shape, src_thru.dtype), pltpu.HBM(land_thru.shape, land_thru.dtype)),
        in_specs=(HBM, HBM, SEM, SEM, ANY), out_specs=(HBM, HBM), input_output_aliases={0: 0, 1: 1},
        compiler_params=pltpu.CompilerParams(has_side_effects=EFFECT),
    )(src_thru, land_thru, send_sems, recv_sems, after)[1]


def _silu(a):
    return a * _sig(a)


def _gw_ada(c_all, dmod_piece):
    def body(c_ref, d_ref, o_ref):
        o_ref[...] = _dot_tn(_bf(_silu(c_ref[...])), _bf(d_ref[...]))

    return pl.pallas_call(
        body, name="gw_ada", out_shape=jax.ShapeDtypeStruct((c_all.shape[1], dmod_piece.shape[1]), F32),
        compiler_params=_params(),
    )(c_all, dmod_piece)


def _adam(parts, w, m, v, name):
    k, r, n = parts.shape
    if r <= 256 or r % 256 == 0:
        tr, tn = min(r, 256), n
    else:
        tr, tn = r, 256
    assert r % tr == 0 and n % tn == 0

    def body(p_ref, w_ref, m_ref, v_ref, g_ref, d_ref, nm_ref, nv_ref):
        g = p_ref[0].astype(F32)
        for j in range(1, k):
            g = g + p_ref[j].astype(F32)
        g_ref[...] = g
        d_ref[...], nm_ref[...], nv_ref[...] = _adam_math(g, w_ref[...], m_ref[...], v_ref[...])

    blk = pl.BlockSpec((tr, tn), lambda i, j: (i, j))
    return pl.pallas_call(
        body, name=name, grid=(r // tr, n // tn),
        in_specs=[pl.BlockSpec((k, tr, tn), lambda i, j: (0, i, j)), blk, blk, blk],
        out_specs=[blk, blk, blk, blk],
        out_shape=[jax.ShapeDtypeStruct((r, n), F32)] * 4,
        compiler_params=_params(dimension_semantics=("arbitrary", "arbitrary")),
    )(parts, w, m, v)


def _adam_math(g, w, m, v):
    m_new = ADAM_B1 * m + (1.0 - ADAM_B1) * g
    v_new = ADAM_B2 * v + (1.0 - ADAM_B2) * jnp.square(g)
    m_hat = m_new / (1.0 - ADAM_B1 ** ADAM_STEP)
    v_hat = v_new / (1.0 - ADAM_B2 ** ADAM_STEP)
    return -ADAM_LR * (m_hat / (jnp.sqrt(v_hat) + ADAM_EPS) + ADAM_WD * w), m_new, v_new


_SMALL = (("b_ada", 3 * D_MODEL), ("norm_w", D_MODEL), ("q_norm_w", HEAD_DIM), ("k_norm_w", HEAD_DIM),
          ("rel_bias", REL_BUCKETS * ATTN_HEADS), ("sinks", ATTN_HEADS), ("conv_b", XBC_W), ("dt_bias", SSM_HEADS),
          ("a_log", SSM_HEADS), ("d_skip", SSM_HEADS), ("ssm_norm_w", SSM_W))
_SLOT = tuple(-(-n // 128) * 128 for _, n in _SMALL)
_SLOT_OFF = tuple(int(o) for o in np.cumsum((0,) + _SLOT))
_LOSS_OFF = _SLOT_OFF[-1]
_CW_OFF = _LOSS_OFF + 128
_PACK_N = _CW_OFF + CONV_K * XBC_W


def _pack_partials(small, loss, g_conv_w):
    parts = []
    for (name, n), slot in zip(_SMALL, _SLOT):
        parts.append(small[name].reshape(1, n))
        if slot > n:
            parts.append(jnp.zeros((1, slot - n), F32))
    parts += [loss.reshape(1, 1), jnp.zeros((1, 127), F32), g_conv_w.reshape(1, CONV_K * XBC_W)]
    return jnp.concatenate(parts, axis=1)


def _adam_small(pack_all, w, m, v):
    names = [name for name, _ in _SMALL]

    def body(p_ref, *rest):
        ins, outs = rest[:3 * len(names)], rest[3 * len(names):]

        def total(off, n):
            g = p_ref[0, :, off:off + n]
            for d in range(1, N_DEV):
                g = g + p_ref[d, :, off:off + n]
            return g

        for j, (name, n) in enumerate(_SMALL):
            g = total(_SLOT_OFF[j], n)
            delta, m_new, v_new = _adam_math(g, ins[3 * j][...], ins[3 * j + 1][...], ins[3 * j + 2][...])
            outs[4 * j][...] = g
            outs[4 * j + 1][...] = delta
            outs[4 * j + 2][...] = m_new
            outs[4 * j + 3][...] = v_new
        outs[-1][...] = total(_LOSS_OFF, 1)

    flat = []
    for name, n in _SMALL:
        flat += [w[name].reshape(1, n), m[name].reshape(1, n), v[name].reshape(1, n)]
    out_shape = [jax.ShapeDtypeStruct((1, n), F32) for _, n in _SMALL for _ in range(4)] + [jax.ShapeDtypeStruct((1, 1), F32)]
    out = pl.pallas_call(body, name="adam_small", out_shape=out_shape, compiler_params=_params())(pack_all, *flat)
    res = {name: [out[4 * j + t].reshape(w[name].shape) for t in range(4)] for j, name in enumerate(names)}
    return res, out[-1]


WEIGHTS = ("w_ada", "b_ada", "norm_w", "w_in", "q_norm_w", "k_norm_w", "rel_bias", "sinks", "conv_w", "conv_b",
           "dt_bias", "a_log", "d_skip", "ssm_norm_w", "w_attn_proj", "w_ssm_proj", "w_out")


def kernel(x, c, w_ada, b_ada, norm_w, w_in, q_norm_w, k_norm_w, rel_bias, sinks, conv_w, conv_b, dt_bias, a_log, d_skip, ssm_norm_w, w_attn_proj, w_ssm_proj, w_out, loss_target, m_w_ada, m_b_ada, m_norm_w, m_w_in, m_q_norm_w, m_k_norm_w, m_rel_bias, m_sinks, m_conv_w, m_conv_b, m_dt_bias, m_a_log, m_d_skip, m_ssm_norm_w, m_w_attn_proj, m_w_ssm_proj, m_w_out, v_w_ada, v_b_ada, v_norm_w, v_w_in, v_q_norm_w, v_k_norm_w, v_rel_bias, v_sinks, v_conv_w, v_conv_b, v_dt_bias, v_a_log, v_d_skip, v_ssm_norm_w, v_w_attn_proj, v_w_ssm_proj, v_w_out):
    w = dict(w_ada=w_ada, b_ada=b_ada, norm_w=norm_w, w_in=w_in, q_norm_w=q_norm_w, k_norm_w=k_norm_w,
             rel_bias=rel_bias, sinks=sinks, conv_w=conv_w, conv_b=conv_b, dt_bias=dt_bias, a_log=a_log,
             d_skip=d_skip, ssm_norm_w=ssm_norm_w, w_attn_proj=w_attn_proj, w_ssm_proj=w_ssm_proj, w_out=w_out)
    m = dict(w_ada=m_w_ada, b_ada=m_b_ada, norm_w=m_norm_w, w_in=m_w_in, q_norm_w=m_q_norm_w, k_norm_w=m_k_norm_w,
             rel_bias=m_rel_bias, sinks=m_sinks, conv_w=m_conv_w, conv_b=m_conv_b, dt_bias=m_dt_bias, a_log=m_a_log,
             d_skip=m_d_skip, ssm_norm_w=m_ssm_norm_w, w_attn_proj=m_w_attn_proj, w_ssm_proj=m_w_ssm_proj, w_out=m_w_out)
    v = dict(w_ada=v_w_ada, b_ada=v_b_ada, norm_w=v_norm_w, w_in=v_w_in, q_norm_w=v_q_norm_w, k_norm_w=v_k_norm_w,
             rel_bias=v_rel_bias, sinks=v_sinks, conv_w=v_conv_w, conv_b=v_conv_b, dt_bias=v_dt_bias, a_log=v_a_log,
             d_skip=v_d_skip, ssm_norm_w=v_ssm_norm_w, w_attn_proj=v_w_attn_proj, w_ssm_proj=v_w_ssm_proj, w_out=v_w_out)
    me = 4 * lax.axis_index("x") + 2 * lax.axis_index("y") + lax.axis_index("c")
    ada_n = w_ada.shape[2]
    in_n = w_in.shape[2]
    cw_n = conv_w.shape[2]

    b_piece = lax.dynamic_slice_in_dim(b_ada, me * ada_n, ada_n, axis=1)
    first, mod_all = _gather_mod(jnp.concatenate([c, conv_w[0].reshape(1, CONV_K * cw_n)], axis=1), w_ada[0], b_piece)
    first = first[:, 0]
    c_all = first[:, :D_MODEL]
    conv_w_full = first[:, D_MODEL:].reshape(N_DEV, CONV_K, cw_n).transpose(1, 0, 2).reshape(CONV_K, XBC_W)
    mod = lax.dynamic_index_in_dim(mod_all, me, axis=1, keepdims=False).reshape(1, 3 * D_MODEL)
    shift, scale, gate = mod[:, :D_MODEL], mod[:, D_MODEL:2 * D_MODEL], mod[:, 2 * D_MODEL:]

    pad = -in_n % 24
    w_t, zero = _ag_relayed(jnp.pad(w_in[0].T.astype(BF), ((0, pad), (0, 0))), "ag_w_in", chunks=3)
    w_t = w_t[:, :in_n].reshape(N_DEV * in_n, D_MODEL)

    def with_mine(blocks, mine):
        return lax.dynamic_update_index_in_dim(lax.empty(blocks, mine.dtype), mine, me, axis=0)

    rows = jnp.concatenate([w_attn_proj[0], w_ssm_proj[0], w_out[0]], axis=0).astype(BF) + zero
    r_ap, r_sp = w_attn_proj.shape[1], w_ssm_proj.shape[1]
    rows_started, zero = _exchange_start(rows, with_mine((N_DEV,) + rows.shape, rows), True, "ag_rows_start")

    def rows_fn(after):
        return _exchange_wait(rows_started, after, True, "ag_rows_wait")

    started = {}

    def send_blocks(key, g, name):
        started[key], zero = _exchange_start(
            g, with_mine(g.shape, lax.dynamic_index_in_dim(g, me, axis=0, keepdims=False)), False, name)
        return zero

    def after_mid(g_wap, g_wsp, g_wout):
        return send_blocks("rows", jnp.concatenate(
            [g_wap.reshape(N_DEV, r_ap, D_MODEL), g_wsp.reshape(N_DEV, r_sp, D_MODEL),
             g_wout.reshape(N_DEV, r_ap, D_MODEL)], axis=1), "rs_rows_start")

    def after_gw(g_ws):
        return send_blocks("in", jnp.concatenate(g_ws, axis=0).reshape(N_DEV, in_n, D_MODEL), "rs_in_start")

    r = _local_step(x[0], loss_target[0], shift, scale + zero, gate, w_t, rows_fn, norm_w, q_norm_w, k_norm_w,
                    rel_bias, sinks, conv_w_full, conv_b, dt_bias, a_log, d_skip, ssm_norm_w, after_mid, after_gw)

    small = dict(b_ada=r["dmod"], norm_w=r["g_norm_w"], q_norm_w=r["g_qnw"], k_norm_w=r["g_knw"], rel_bias=r["g_rel"],
                 sinks=r["g_sinks"], conv_b=r["g_conv_b"], dt_bias=r["g_dt_bias"], a_log=r["g_a_log"],
                 d_skip=r["g_d_skip"], ssm_norm_w=r["g_ssm_nw"])
    pack_all = _ag_direct(_pack_partials(small, r["loss"], r["g_conv_w"]), "ag_small")
    res, loss = _adam_small(pack_all, w, m, v)
    loss = loss[0, 0]
    cw_parts = pack_all[:, 0, _CW_OFF:].reshape(N_DEV, CONV_K, XBC_W)
    cw_mine = lax.dynamic_slice_in_dim(cw_parts, me * cw_n, cw_n, axis=2)
    res["conv_w"] = [a[None] for a in _adam(cw_mine, conv_w[0], m_conv_w[0], v_conv_w[0], "adam_conv_w")]

    dmod_piece = lax.dynamic_slice_in_dim(pack_all[:, 0, :3 * D_MODEL], me * ada_n, ada_n, axis=1)
    g_ada = _gw_ada(c_all, dmod_piece)
    res["w_ada"] = [a[None] for a in _adam(g_ada[None], w_ada[0], m_w_ada[0], v_w_ada[0], "adam_w_ada")]

    cat = lambda d: jnp.concatenate([d["w_attn_proj"][0], d["w_ssm_proj"][0], d["w_out"][0]], axis=0)
    rows_res = _adam(_exchange_wait(started["rows"], g_ada, False, "rs_rows_wait"), cat(w), cat(m), cat(v), "adam_w_rows")
    res["w_in"] = [a.T[None] for a in _adam(_exchange_wait(started["in"], rows_res[0], False, "rs_in_wait"),
                                            w_in[0].T, m_w_in[0].T, v_w_in[0].T, "adam_w_in")]
    res["w_attn_proj"] = [a[None, :r_ap] for a in rows_res]
    res["w_ssm_proj"] = [a[None, r_ap:r_ap + r_sp] for a in rows_res]
    res["w_out"] = [a[None, r_ap + r_sp:] for a in rows_res]

    outs = [loss, r["grad_x"][None]]
    for j in range(4):
        outs += [res[name][j] for name in WEIGHTS]
    return tuple(outs)
```

```python
import math

import numpy as np
import jax
import jax.numpy as jnp
from jax import lax
from jax.experimental import pallas as pl
from jax.experimental.pallas import tpu as pltpu

F32 = jnp.float32
BF = jnp.bfloat16
HI = lax.Precision.HIGHEST

D_MODEL = 1024
ATTN_HEADS = 16
KV_HEADS = 4
GRP = ATTN_HEADS // KV_HEADS
HEAD_DIM = 64
ATTN_W = ATTN_HEADS * HEAD_DIM
KV_W = KV_HEADS * HEAD_DIM
BLOCK = 128
REL_BUCKETS = 32
REL_MAX_DIST = 128
SSM_W = 2048
SSM_P = 64
SSM_HEADS = 32
SSM_G = 4
SSM_R = 8
SSM_N = 128
CONV_K = 4
XBC_W = SSM_W + 2 * SSM_G * SSM_N
SEG_W = (ATTN_W, 2 * KV_W, ATTN_W + SSM_W, XBC_W, SSM_HEADS, 2 * D_MODEL)
NSEG = len(SEG_W)
SEG_OFF = tuple(int(v) for v in np.cumsum((0,) + SEG_W))
IN_W = SEG_OFF[-1]
GATE_SEGS = (2, 5)
EPS = 1e-6
N_DEV = 8
AG_PIECES = 9
ADAM_LR, ADAM_B1, ADAM_B2, ADAM_EPS, ADAM_WD, ADAM_STEP = 0.001, 0.9, 0.999, 1e-08, 0.01, 10
VMEM_LIMIT = 60 * 1024 * 1024
MESH = pl.DeviceIdType.MESH
ANY = pl.BlockSpec(memory_space=pl.ANY)


def _dot(a, b, precision=None):
    return jnp.dot(a, b, preferred_element_type=F32, precision=precision)


def _dot_nt(a, b, precision=None):
    return lax.dot_general(a, b, (((1,), (1,)), ((), ())), preferred_element_type=F32, precision=precision)


def _dot_tn(a, b, precision=None):
    return lax.dot_general(a, b, (((0,), (0,)), ((), ())), preferred_element_type=F32, precision=precision)


def _bf(a):
    return a.astype(BF)


def _sig(a):
    return 0.5 * jnp.tanh(0.5 * a) + 0.5


def _params(**kw):
    return pltpu.CompilerParams(vmem_limit_bytes=VMEM_LIMIT, **kw)


def _full(shape):
    nd = len(shape)
    return pl.BlockSpec(shape, lambda i: (0,) * nd)


def _rows(tm, w):
    return pl.BlockSpec((tm, w), lambda i: (i, 0))


def _inproj(x, norm_w, scale, shift, w_t, tm=256):
    s = x.shape[0]

    def body(x_ref, nw_ref, sc_ref, sh_ref, w_hbm, *rest):
        outs, h_ref, w_vm, sem = rest[:NSEG], rest[NSEG], rest[NSEG + 1], rest[NSEG + 2]
        first = pl.program_id(0) == 0
        cps = [pltpu.make_async_copy(w_hbm.at[SEG_OFF[j]:SEG_OFF[j + 1], :], w_vm.at[SEG_OFF[j]:SEG_OFF[j + 1], :], sem.at[j])
               for j in range(NSEG)]

        def tile(waiting):
            xv = x_ref[...]
            r = lax.rsqrt(jnp.mean(xv * xv, axis=-1, keepdims=True) + EPS)
            h = xv * r * (nw_ref[...] * (1.0 + sc_ref[...])) + sh_ref[...]
            hb = _bf(h)
            h_ref[...] = hb
            for j in range(NSEG):
                if waiting:
                    cps[j].wait()
                outs[j][...] = _dot_nt(hb, w_vm[SEG_OFF[j]:SEG_OFF[j + 1], :]).astype(outs[j].dtype)

        @pl.when(first)
        def _():
            for cp in cps:
                cp.start()
            tile(True)

        @pl.when(jnp.logical_not(first))
        def _():
            tile(False)

    vec = _full((1, D_MODEL))
    return pl.pallas_call(
        body, name="inproj", grid=(s // tm,),
        in_specs=[_rows(tm, D_MODEL), vec, vec, vec, ANY],
        out_specs=[_rows(tm, w) for w in SEG_W] + [_rows(tm, D_MODEL)],
        out_shape=[jax.ShapeDtypeStruct((s, w), BF if j in GATE_SEGS else F32) for j, w in enumerate(SEG_W)]
                  + [jax.ShapeDtypeStruct((s, D_MODEL), BF)],
        scratch_shapes=[pltpu.VMEM((IN_W, D_MODEL), BF), pltpu.SemaphoreType.DMA((NSEG,))],
        compiler_params=_params(dimension_semantics=("arbitrary",)),
    )(x, norm_w, scale, shift, w_t)


def _bucket_onehot_t():
    qi = jnp.arange(BLOCK)[:, None]
    kj = jnp.arange(2 * BLOCK)[None, :]
    dist = qi + BLOCK - kj
    n = jnp.maximum(dist, 0)
    max_exact = REL_BUCKETS // 2
    nf = jnp.maximum(n, 1).astype(F32)
    large = max_exact + (jnp.log(nf / max_exact) / math.log(REL_MAX_DIST / max_exact)
                         * (REL_BUCKETS - max_exact)).astype(jnp.int32)
    large = jnp.minimum(large, REL_BUCKETS - 1)
    bucket = jnp.where(n < max_exact, n, large).reshape(1, BLOCK * 2 * BLOCK)
    return (bucket == jnp.arange(REL_BUCKETS)[:, None]).astype(F32)


def _bias_dense(rel_bias_t, oh_t):
    def body(rb_ref, oh_ref, o_ref):
        o_ref[...] = _dot(rb_ref[...], oh_ref[...], HI)

    return pl.pallas_call(
        body, name="bias_dense", out_shape=jax.ShapeDtypeStruct((ATTN_HEADS, BLOCK * 2 * BLOCK), F32),
        compiler_params=_params(),
    )(rel_bias_t, oh_t)


def _bias_grad(ds_sum, oh_t):
    def body(ds_ref, oh_ref, o_ref):
        o_ref[...] = _dot_nt(ds_ref[...], oh_ref[...], HI)

    return pl.pallas_call(
        body, name="bias_grad", out_shape=jax.ShapeDtypeStruct((ATTN_HEADS, REL_BUCKETS), F32),
        compiler_params=_params(),
    )(ds_sum, oh_t)


def _group_sum(a, e):
    hi = _bf(a)
    return _dot(hi, e) + _dot(_bf(a - hi.astype(F32)), e)


def _group_bcast(a, e3t):
    hi = _bf(a)
    r1 = a - hi.astype(F32)
    mid = _bf(r1)
    return _dot(jnp.concatenate([hi, mid, _bf(r1 - mid.astype(F32))], axis=1), e3t)


def _membership(width, group, ngroups):
    e = (jnp.arange(width)[:, None] // group == jnp.arange(ngroups)[None, :]).astype(BF)
    return e, jnp.tile(e.T, (3, 1))


def _fold(width, group):
    return (jnp.arange(width)[:, None] % group == jnp.arange(group)[None, :]).astype(BF)


def _heads_norm(t, w_x, e, e3t):
    r = lax.rsqrt(_dot(_bf(t * t), e) * (1.0 / HEAD_DIM) + EPS)
    r_x = _group_bcast(r, e3t)
    return t * r_x * w_x, r_x


def _heads_norm_bwd(t, r_x, w_x, d, e, e3t):
    wd = d * w_x
    corr = _group_bcast(_dot(_bf(t * wd), e) * (1.0 / HEAD_DIM), e3t)
    return r_x * wd - t * (r_x * r_x * r_x) * corr, jnp.sum(d * t * r_x, axis=0, keepdims=True)


def _stack_heads(a, hk):
    return jnp.concatenate([a[:, (hk * GRP + g) * HEAD_DIM:(hk * GRP + g + 1) * HEAD_DIM] for g in range(GRP)], axis=0)


def _stack_cols(a, hk):
    return jnp.concatenate([a[:, hk * GRP + g:hk * GRP + g + 1] for g in range(GRP)], axis=0)


def _masked_bias(bias):
    qi = jnp.arange(BLOCK)[:, None]
    kj = jnp.arange(2 * BLOCK)[None, :]
    cur_ok = jnp.logical_and(kj >= BLOCK, kj - BLOCK <= qi)
    both_ok = jnp.logical_or(jnp.logical_and(kj < BLOCK, kj > qi), cur_ok)
    return jnp.stack([jnp.where(cur_ok, bias, -1e30), jnp.where(both_ok, bias, -1e30)])


def _attn_consts(qnw, knw):
    eq, eq3t = _membership(ATTN_W, HEAD_DIM, ATTN_HEADS)
    ek, ek3t = _membership(KV_W, HEAD_DIM, ATTN_HEADS)
    return (jnp.tile(qnw, (1, ATTN_HEADS)), jnp.tile(knw, (1, KV_HEADS)), eq, eq3t, ek, ek3t)


def _attn_fwd(q, kv, bias, sinks, consts):
    s = q.shape[0]
    nb = s // BLOCK
    gq = GRP * BLOCK
    bias_t = bias.reshape(2, KV_HEADS, GRP, BLOCK, 2 * BLOCK).transpose(0, 1, 4, 2, 3).reshape(2, KV_HEADS, 2 * BLOCK, gq)
    sink_rows = jnp.repeat(sinks.reshape(KV_HEADS, GRP), BLOCK, axis=1).reshape(KV_HEADS, 1, gq)
    eye = jnp.eye(BLOCK, dtype=BF)

    def body(q_ref, kp_ref, kc_ref, vp_ref, vc_ref, b_ref, bt_ref, sk_ref, skr_ref, eye_ref,
             qw_ref, kw_ref, eq_ref, eq3_ref, ek_ref, ek3_ref, o_ref, lse_ref):
        qn = _bf(_heads_norm(q_ref[...], qw_ref[...], eq_ref[...], eq3_ref[...])[0] * (HEAD_DIM ** -0.5))
        kn = _bf(_heads_norm(jnp.concatenate([kp_ref[...], kc_ref[...]], axis=0), kw_ref[...], ek_ref[...], ek3_ref[...])[0])
        vv = _bf(jnp.concatenate([vp_ref[...], vc_ref[...]], axis=0))
        ones = jnp.ones((2 * BLOCK, HEAD_DIM), BF)
        lses = []
        kss = [slice(hk * HEAD_DIM, (hk + 1) * HEAD_DIM) for hk in range(KV_HEADS)]
        qgs = [_stack_heads(qn, hk) for hk in range(KV_HEADS)]
        sc_ts = [_dot_nt(kn[:, kss[hk]], qgs[hk]) + bt_ref[0, hk] for hk in range(KV_HEADS)]
        m_rows = [jnp.maximum(jnp.max(sc_ts[hk], axis=0, keepdims=True), skr_ref[hk]) for hk in range(KV_HEADS)]
        m8s = [_bf(jnp.broadcast_to(m + jnp.abs(m) * (2.0 ** -7), (8, gq))) for m in m_rows]
        ms = [jnp.concatenate([_dot_nt(eye_ref[...], m8[:, g * BLOCK:(g + 1) * BLOCK])[:, 0:1] for g in range(GRP)], axis=0)
              for m8 in m8s]
        scs = [_dot_nt(qgs[hk], kn[:, kss[hk]]) + b_ref[0, hk * GRP:(hk + 1) * GRP].reshape(gq, 2 * BLOCK)
               for hk in range(KV_HEADS)]
        ps = [_bf(jnp.exp(scs[hk] - ms[hk])) for hk in range(KV_HEADS)]
        pvs = [_dot(ps[hk], jnp.concatenate([vv[:, kss[hk]], ones], axis=1)) for hk in range(KV_HEADS)]
        for hk in range(KV_HEADS):
            m, pv = ms[hk], pvs[hk]
            sink = jnp.concatenate([jnp.full((BLOCK, 1), sk_ref[0, hk * GRP + g], F32) for g in range(GRP)], axis=0)
            den = pv[:, HEAD_DIM:HEAD_DIM + 1] + jnp.exp(sink - m)
            out = pv[:, :HEAD_DIM] * (1.0 / den)
            lse = m + jnp.log(den)
            for g in range(GRP):
                h = hk * GRP + g
                o_ref[:, h * HEAD_DIM:(h + 1) * HEAD_DIM] = out[g * BLOCK:(g + 1) * BLOCK]
                lses.append(lse[g * BLOCK:(g + 1) * BLOCK])
        lse_ref[...] = jnp.concatenate(lses, axis=1)

    cur = lambda w, col=0: pl.BlockSpec((BLOCK, w), lambda i: (i, col))
    prev = lambda w, col=0: pl.BlockSpec((BLOCK, w), lambda i: (jnp.maximum(i - 1, 0), col))
    whole = lambda a: pl.BlockSpec(a.shape, lambda i: (0,) * a.ndim)
    first_or_not = lambda a: pl.BlockSpec((1,) + a.shape[1:], lambda i: (jnp.minimum(i, 1),) + (0,) * (a.ndim - 1))
    return pl.pallas_call(
        body, name="attn_fwd", grid=(nb,),
        in_specs=[cur(ATTN_W), prev(KV_W, 0), cur(KV_W, 0), prev(KV_W, 1), cur(KV_W, 1),
                  first_or_not(bias), first_or_not(bias_t),
                  pl.BlockSpec(memory_space=pltpu.SMEM), whole(sink_rows), whole(eye)] + [_full(c.shape) for c in consts],
        out_specs=[cur(ATTN_W), cur(ATTN_HEADS)],
        out_shape=[jax.ShapeDtypeStruct((s, ATTN_W), F32), jax.ShapeDtypeStruct((s, ATTN_HEADS), F32)],
        compiler_params=_params(dimension_semantics=("arbitrary",)),
    )(q, kv, kv, kv, kv, bias, bias_t, sinks, sink_rows, eye, *consts)


def _conv_taps(xbc, tail):
    ext = jnp.concatenate([tail, xbc], axis=0)
    return [pltpu.roll(ext, CONV_K - 1 - j, axis=0)[8:8 + BLOCK] if j < CONV_K - 1 else xbc for j in range(CONV_K)]


def _softplus(u):
    return jnp.maximum(u, 0.0) + jnp.log(1.0 + jnp.exp(-jnp.abs(u)))


def _tril():
    r = lax.broadcasted_iota(jnp.int32, (BLOCK, BLOCK), 0)
    c = lax.broadcasted_iota(jnp.int32, (BLOCK, BLOCK), 1)
    return r >= c


def _triu():
    r = lax.broadcasted_iota(jnp.int32, (BLOCK, BLOCK), 0)
    c = lax.broadcasted_iota(jnp.int32, (BLOCK, BLOCK), 1)
    return r <= c


def _exact_left(m01, a):
    hi = _bf(a)
    r1 = a - hi.astype(F32)
    mid = _bf(r1)
    return _dot(m01, hi) + _dot(m01, mid) + _dot(m01, _bf(r1 - mid.astype(F32)))


def _ssd_common(conv, dtr, dtb_ref, alog_ref, e3_ref):
    sg = _sig(conv)
    xact = conv * sg
    u = dtr + dtb_ref[...]
    dt = _softplus(u)
    a = -jnp.exp(alog_ref[...])
    trilb = _tril()
    acum = _exact_left(trilb.astype(BF), dt * a) * math.log2(math.e)
    both = _group_bcast(jnp.concatenate([dt, acum], axis=0), e3_ref[...])
    dt_x, acum_x = both[:BLOCK], both[BLOCK:]
    return sg, xact, u, dt, a, trilb, acum, dt_x, acum_x


SSD_CH = 2


def _ssd_fwd(xbc, dt_raw, conv_w, conv_b, dt_bias, a_log, dsk_x, e3t):
    s = xbc.shape[0]
    nc = s // BLOCK
    ch = SSD_CH if nc % SSD_CH == 0 else 1
    rows = ch * BLOCK

    def body(x_ref, tail_ref, dtr_ref, cw_ref, cb_ref, dtb_ref, alog_ref, dsk_ref, e3_ref,
             y_ref, hp_ref, conv_ref, hst, yd_s, yoff_s):
        i = pl.program_id(0)

        @pl.when(i == 0)
        def _():
            hst[...] = jnp.zeros_like(hst)

        for j in range(ch):
            rs = slice(j * BLOCK, (j + 1) * BLOCK)
            tail = jnp.where(i > 0, tail_ref[...], 0.0) if j == 0 else x_ref[j * BLOCK - 8:j * BLOCK, :]
            taps = _conv_taps(x_ref[rs, :], tail)
            conv = cb_ref[...] + sum(taps[t] * cw_ref[t:t + 1, :] for t in range(CONV_K))
            conv_ref[rs, :] = conv
            _, xact, _, _, _, trilb, acum, dt_x, acum_x = _ssd_common(conv, dtr_ref[rs, :], dtb_ref, alog_ref, e3_ref)
            xs = xact[:, :SSM_W]
            acum_t = acum.T
            ea_x = jnp.exp2(acum_x)
            last_x = acum_x[BLOCK - 1:BLOCK, :]
            xdt = xs * dt_x
            xw = xdt * jnp.exp2(last_x - acum_x)
            cd_x = jnp.exp2(last_x)
            hprev = hst[...]
            hp_ref[j] = hprev
            sls = [slice(g * SSM_R * SSM_P, (g + 1) * SSM_R * SSM_P) for g in range(SSM_G)]
            bgs = [_bf(xact[:, SSM_W + g * SSM_N:SSM_W + (g + 1) * SSM_N]) for g in range(SSM_G)]
            cgs = [_bf(xact[:, SSM_W + SSM_G * SSM_N + g * SSM_N:SSM_W + SSM_G * SSM_N + (g + 1) * SSM_N])
                   for g in range(SSM_G)]
            xdt_b, xw_b, hprev_b = _bf(xdt), _bf(xw), _bf(hprev)
            low_half = lax.broadcasted_iota(jnp.int32, (BLOCK, 2 * SSM_P), 1) < SSM_P
            cbs = [_dot_nt(cgs[g], bgs[g]) for g in range(SSM_G)]
            for g in range(SSM_G):
                sl = sls[g]
                yoff_s[:, sl] = _dot(cgs[g], hprev_b[:, sl]) * ea_x[:, sl]
                hst[:, sl] = hprev[:, sl] * cd_x[:, sl] + _dot_tn(bgs[g], xw_b[:, sl])
            for g in range(SSM_G):
                hss = [slice((g * SSM_R + r) * SSM_P, (g * SSM_R + r + 1) * SSM_P) for r in range(SSM_R)]
                mms = [_bf(cbs[g] * jnp.exp2(jnp.where(trilb, acum[:, g * SSM_R + r:g * SSM_R + r + 1]
                                                      - acum_t[g * SSM_R + r:g * SSM_R + r + 1, :], -1e30)))
                       for r in range(SSM_R)]
                for r in range(0, SSM_R, 2):
                    pair = slice(hss[r].start, hss[r + 1].stop)
                    xp = xdt_b[:, pair]
                    rhs = jnp.concatenate([jnp.where(low_half, xp, 0), jnp.where(low_half, 0, xp)], axis=0)
                    yd_s[:, pair] = _dot(jnp.concatenate([mms[r], mms[r + 1]], axis=1), rhs)
            y_ref[rs, :] = yd_s[...] + yoff_s[...] + dsk_ref[...] * xs

    blk = lambda w: pl.BlockSpec((rows, w), lambda i: (i, 0))
    return pl.pallas_call(
        body, name="ssd_fwd", grid=(nc // ch,),
        in_specs=[blk(XBC_W), pl.BlockSpec((8, XBC_W), lambda i: (jnp.maximum(i * (rows // 8) - 1, 0), 0)),
                  blk(SSM_HEADS), _full((CONV_K, XBC_W)), _full((1, XBC_W)), _full((1, SSM_HEADS)),
                  _full((1, SSM_HEADS)), _full((1, SSM_W)), _full((3 * SSM_HEADS, SSM_W))],
        out_specs=[blk(SSM_W), pl.BlockSpec((ch, SSM_N, SSM_W), lambda i: (i, 0, 0)), blk(XBC_W)],
        out_shape=[jax.ShapeDtypeStruct((s, SSM_W), F32), jax.ShapeDtypeStruct((nc, SSM_N, SSM_W), F32),
                   jax.ShapeDtypeStruct((s, XBC_W), F32)],
        scratch_shapes=[pltpu.VMEM((SSM_N, SSM_W), F32), pltpu.VMEM((BLOCK, SSM_W), F32), pltpu.VMEM((BLOCK, SSM_W), F32)],
        compiler_params=_params(dimension_semantics=("arbitrary",)),
    )(xbc, xbc, dt_raw, conv_w, conv_b, dt_bias, a_log, dsk_x, e3t)


def _dsilu(z, sg, silu):
    return sg * (1.0 + (z - silu))


def _mid(x, tgt, o_att, zam, ypre, gab, gate, ssm_nw, rows_all, tm=256):
    s = x.shape[0]
    gw = SSM_W // SSM_G

    r_ap, r_sp = ATTN_W // N_DEV, SSM_W // N_DEV

    def body(x_ref, t_ref, o_ref, zam_ref, yp_ref, gab_ref, gate_ref, nw_ref, rows_h,
             dout_ref, do_ref, dzam_ref, dyp_ref, dgab_ref,
             yag_ref, dya_ref, yn_ref, dyb_ref, mg_ref, dob_ref, gnw_ref, dgate_ref, loss_ref,
             wap_v, wsp_v, wout_v, sem):
        i = pl.program_id(0)

        @pl.when(i == 0)
        def _():
            cps = []
            for d in range(N_DEV):
                for j, (dst, r0, rn) in enumerate(((wap_v, 0, r_ap), (wsp_v, r_ap, r_sp), (wout_v, r_ap + r_sp, r_ap))):
                    cps.append(pltpu.make_async_copy(rows_h.at[d, r0:r0 + rn, :], dst.at[d * rn:(d + 1) * rn, :], sem.at[j]))
            for cp in cps:
                cp.start()
            gnw_ref[...] = jnp.zeros_like(gnw_ref)
            dgate_ref[...] = jnp.zeros_like(dgate_ref)
            loss_ref[...] = jnp.zeros_like(loss_ref)
            for cp in cps:
                cp.wait()

        gate = gate_ref[...]
        nw = nw_ref[...]
        o_att = o_ref[...]
        z_a = zam_ref[:, :ATTN_W].astype(F32)
        s_a = _sig(z_a)
        silu_a = z_a * s_a
        yag = _bf(o_att * silu_a)
        yag_ref[...] = yag
        ypre = yp_ref[...]
        z_m = zam_ref[:, ATTN_W:].astype(F32)
        s_m = _sig(z_m)
        silu_m = z_m * s_m
        yg = ypre * silu_m
        rinv = jnp.concatenate(
            [jnp.broadcast_to(lax.rsqrt(jnp.mean(yg[:, g * gw:(g + 1) * gw] ** 2, axis=-1, keepdims=True) + EPS), (tm, gw))
             for g in range(SSM_G)], axis=1)
        ynr = yg * rinv
        yn = _bf(ynr * nw)
        yn_ref[...] = yn
        y_a = _dot(yag, wap_v[...])
        y_b = _dot(yn, wsp_v[...])
        g_a = _sig(gab_ref[:, :D_MODEL].astype(F32))
        g_b = _sig(gab_ref[:, D_MODEL:].astype(F32))
        merged = _bf(g_a * y_a + g_b * y_b)
        mg_ref[...] = merged
        o = _dot(merged, wout_v[...])
        diff = x_ref[...] + gate * o - t_ref[...]
        loss_ref[...] += (0.5 / D_MODEL) * jnp.sum(diff * diff, axis=(0, 1), keepdims=True)
        dout = diff * (1.0 / D_MODEL)
        dout_ref[...] = dout
        dgate_ref[...] += jnp.sum(dout * o, axis=0, keepdims=True)
        d_o = _bf(dout * gate)
        dob_ref[...] = d_o
        dmerged = _dot_nt(d_o, wout_v[...])
        dy_af = dmerged * g_a
        dy_bf = dmerged * g_b
        dy_a = _bf(dy_af)
        dy_b = _bf(dy_bf)
        dya_ref[...] = dy_a
        dyb_ref[...] = dy_b
        dyag = _dot_nt(dy_a, wap_v[...])
        dyn = _dot_nt(dy_b, wsp_v[...])
        dgab_ref[:, :D_MODEL] = _bf(dy_af * y_a * (1.0 - g_a))
        dgab_ref[:, D_MODEL:] = _bf(dy_bf * y_b * (1.0 - g_b))
        do_ref[...] = dyag * silu_a
        dzam_ref[:, :ATTN_W] = _bf(dyag * o_att * _dsilu(z_a, s_a, silu_a))
        gnw_ref[...] += jnp.sum(dyn * ynr, axis=0, keepdims=True)
        dynw = dyn * nw
        corr = jnp.concatenate(
            [jnp.broadcast_to(jnp.mean((dynw * ynr)[:, g * gw:(g + 1) * gw], axis=-1, keepdims=True), (tm, gw))
             for g in range(SSM_G)], axis=1)
        dyg = rinv * (dynw - ynr * corr)
        dyp_ref[...] = dyg * silu_m
        dzam_ref[:, ATTN_W:] = _bf(dyg * ypre * _dsilu(z_m, s_m, silu_m))

    r1, r2, r3 = _rows(tm, D_MODEL), _rows(tm, SSM_W), _rows(tm, ATTN_W + SSM_W)
    sd = jax.ShapeDtypeStruct
    return pl.pallas_call(
        body, name="mid", grid=(s // tm,),
        in_specs=[r1, r1, r1, r3, r2, r2, _full((1, D_MODEL)), _full((1, SSM_W)), ANY],
        out_specs=[r1, r1, r3, r2, r2, r1, r1, r2, r1, r1, r1,
                   _full((1, SSM_W)), _full((1, D_MODEL)), _full((1, 1))],
        out_shape=[sd((s, D_MODEL), F32), sd((s, ATTN_W), F32), sd((s, ATTN_W + SSM_W), BF), sd((s, SSM_W), F32),
                   sd((s, 2 * D_MODEL), BF),
                   sd((s, ATTN_W), BF), sd((s, D_MODEL), BF), sd((s, SSM_W), BF), sd((s, D_MODEL), BF),
                   sd((s, D_MODEL), BF), sd((s, D_MODEL), BF),
                   sd((1, SSM_W), F32), sd((1, D_MODEL), F32), sd((1, 1), F32)],
        scratch_shapes=[pltpu.VMEM((ATTN_W, D_MODEL), BF), pltpu.VMEM((SSM_W, D_MODEL), BF), pltpu.VMEM((D_MODEL, D_MODEL), BF),
                        pltpu.SemaphoreType.DMA((3,))],
        compiler_params=_params(dimension_semantics=("arbitrary",)),
    )(x, tgt, o_att, zam, ypre, gab, gate, ssm_nw, rows_all)


def _attn_bwd(q, kv, bias, sinks, consts, o_att, lse, d_o):
    s = q.shape[0]
    nb = s // BLOCK
    folds = (_fold(ATTN_W, HEAD_DIM), _fold(KV_W, HEAD_DIM))

    def body(q_ref, kp_ref, kc_ref, vp_ref, vc_ref, b_ref, skv_ref, qw_ref, kw_ref, eq_ref, eq3_ref, ek_ref, ek3_ref,
             fq_ref, fk_ref, o_ref, lse_ref, do_ref,
             dq_ref, dkv_ref, dss_ref, gqw_ref, gkw_ref, gsk_ref, ckn, cv, dqn_s, dkn_s, dv_s, gq_x, gk_x):
        i = pl.program_id(0)
        kw, ek, ek3 = kw_ref[...], ek_ref[...], ek3_ref[...]

        @pl.when(i == 0)
        def _():
            for ref in (ckn, cv, dss_ref, gq_x, gk_x, gsk_ref):
                ref[...] = jnp.zeros_like(ref)

        @pl.when(i < nb)
        def _():
            qw, eq, eq3 = qw_ref[...], eq_ref[...], eq3_ref[...]
            qf = q_ref[...]
            qnf, rq_x = _heads_norm(qf, qw, eq, eq3)
            qn = _bf(qnf * (HEAD_DIM ** -0.5))
            kf = jnp.concatenate([kp_ref[...], kc_ref[...]], axis=0)
            knf, rk_x = _heads_norm(kf, kw, ek, ek3)
            kn = _bf(knf)
            vv = _bf(jnp.concatenate([vp_ref[...], vc_ref[...]], axis=0))
            d_of = do_ref[...]
            d_ob = _bf(d_of)
            lse_all = lse_ref[...]
            delta = _dot(_bf(d_of * o_ref[...]), eq)
            gsk_ref[...] += jnp.sum(-jnp.exp(skv_ref[...] - lse_all) * delta, axis=0, keepdims=True)
            kss = [slice(hk * HEAD_DIM, (hk + 1) * HEAD_DIM) for hk in range(KV_HEADS)]
            qgs = [_stack_heads(qn, hk) for hk in range(KV_HEADS)]
            d_ogs = [_stack_heads(d_ob, hk) for hk in range(KV_HEADS)]
            scs = [_dot_nt(qgs[hk], kn[:, kss[hk]]) + b_ref[0, hk * GRP:(hk + 1) * GRP].reshape(GRP * BLOCK, 2 * BLOCK)
                   for hk in range(KV_HEADS)]
            dps = [_dot_nt(d_ogs[hk], vv[:, kss[hk]]) for hk in range(KV_HEADS)]
            ps = [jnp.exp(scs[hk] - _stack_cols(lse_all, hk)) for hk in range(KV_HEADS)]
            dss = [ps[hk] * (dps[hk] - _stack_cols(delta, hk)) for hk in range(KV_HEADS)]
            pbs = [_bf(p) for p in ps]
            dsbs = [_bf(ds) for ds in dss]
            for hk in range(KV_HEADS):
                dss_ref[hk * GRP:(hk + 1) * GRP] += dss[hk].reshape(GRP, BLOCK, 2 * BLOCK)
            for hk in range(KV_HEADS):
                dv_s[:, kss[hk]] = _dot_tn(pbs[hk], d_ogs[hk])
                dkn_s[:, kss[hk]] = _dot_tn(dsbs[hk], qgs[hk])
            dqns = [_dot(dsbs[hk], kn[:, kss[hk]]) * (HEAD_DIM ** -0.5) for hk in range(KV_HEADS)]
            for hk in range(KV_HEADS):
                for g in range(GRP):
                    h = hk * GRP + g
                    dqn_s[:, h * HEAD_DIM:(h + 1) * HEAD_DIM] = dqns[hk][g * BLOCK:(g + 1) * BLOCK]
            dq, gq = _heads_norm_bwd(qf, rq_x, qw, dqn_s[...], eq, eq3)
            dq_ref[...] = _bf(dq)
            gq_x[...] += gq
            dk, gk = _heads_norm_bwd(kf[:BLOCK], rk_x[:BLOCK], kw, ckn[...] + dkn_s[0:BLOCK, :], ek, ek3)
            dkv_ref[:, :KV_W] = _bf(dk)
            gk_x[...] += gk
            dkv_ref[:, KV_W:] = _bf(cv[...] + dv_s[0:BLOCK, :])
            ckn[...] = dkn_s[BLOCK:2 * BLOCK, :]
            cv[...] = dv_s[BLOCK:2 * BLOCK, :]

        @pl.when(i == nb)
        def _():
            kc = kc_ref[...]
            dk, gk = _heads_norm_bwd(kc, _heads_norm(kc, kw, ek, ek3)[1], kw, ckn[...], ek, ek3)
            dkv_ref[:, :KV_W] = _bf(dk)
            dkv_ref[:, KV_W:] = _bf(cv[...])
            gqw_ref[...] = _group_sum(jnp.broadcast_to(gq_x[...], (8, ATTN_W)), fq_ref[...])[0:1]
            gkw_ref[...] = _group_sum(jnp.broadcast_to(gk_x[...] + gk, (8, KV_W)), fk_ref[...])[0:1]

    last = nb - 1
    cur = lambda w, col=0: pl.BlockSpec((BLOCK, w), lambda i: (jnp.minimum(i, last), col))
    prev = lambda w, col=0: pl.BlockSpec((BLOCK, w), lambda i: (jnp.maximum(jnp.minimum(i, last) - 1, 0), col))
    late = lambda w: pl.BlockSpec((BLOCK, w), lambda i: (jnp.maximum(i - 1, 0), 0))
    sd = jax.ShapeDtypeStruct
    return pl.pallas_call(
        body, name="attn_bwd", grid=(nb + 1,),
        in_specs=[cur(ATTN_W), prev(KV_W, 0), cur(KV_W, 0), prev(KV_W, 1), cur(KV_W, 1),
                  pl.BlockSpec((1, ATTN_HEADS, BLOCK, 2 * BLOCK), lambda i: (jnp.minimum(i, 1), 0, 0, 0)),
                  _full((1, ATTN_HEADS))]
                 + [_full(c.shape) for c in consts + folds] + [cur(ATTN_W), cur(ATTN_HEADS), cur(ATTN_W)],
        out_specs=[cur(ATTN_W), late(2 * KV_W),
                   pl.BlockSpec((ATTN_HEADS, BLOCK, 2 * BLOCK), lambda i: (0, 0, 0)),
                   _full((1, HEAD_DIM)), _full((1, HEAD_DIM)), _full((1, ATTN_HEADS))],
        out_shape=[sd((s, ATTN_W), BF), sd((s, 2 * KV_W), BF),
                   sd((ATTN_HEADS, BLOCK, 2 * BLOCK), F32), sd((1, HEAD_DIM), F32), sd((1, HEAD_DIM), F32),
                   sd((1, ATTN_HEADS), F32)],
        scratch_shapes=[pltpu.VMEM((BLOCK, KV_W), F32), pltpu.VMEM((BLOCK, KV_W), F32),
                        pltpu.VMEM((BLOCK, ATTN_W), F32), pltpu.VMEM((2 * BLOCK, KV_W), F32),
                        pltpu.VMEM((2 * BLOCK, KV_W), F32), pltpu.VMEM((1, ATTN_W), F32), pltpu.VMEM((1, KV_W), F32)],
        compiler_params=_params(dimension_semantics=("arbitrary",)),
    )(q, kv, kv, kv, kv, bias, sinks, *consts, *folds, o_att, lse, d_o)


def _ssd_bwd(xbc, conv_all, dt_raw, conv_w, dt_bias, a_log, dsk_x, e_mat, e3t, hprev_all, dy_all):
    s = xbc.shape[0]
    nc = s // BLOCK
    ch = 1
    rows = ch * BLOCK
    nsteps = nc // ch
    gw = SSM_R * SSM_P
    b0, c0 = SSM_W, SSM_W + SSM_G * SSM_N

    def body(x_ref, conv_ref, dtr_ref, cw_ref, dtb_ref, alog_ref, dsk_ref, e_ref, e3_ref, hp_ref, dy_ref,
             dx_ref, ddt_ref, gcw_ref, gcb_ref, gdtb_ref, galog_ref, gdsk_ref,
             dh, nhead, gdskx, dxdt_s, dbc_s, dxd_s):
        def chunk_bwd(j):
            rs = slice(j * BLOCK, (j + 1) * BLOCK)
            conv = conv_ref[rs, :]
            sg, xact, u, dt, a, trilb, acum, dt_x, acum_x = _ssd_common(conv, dtr_ref[rs, :], dtb_ref, alog_ref, e3_ref)
            xs = xact[:, :SSM_W]
            acum_t = acum.T
            ea_x = jnp.exp2(acum_x)
            last_x = acum_x[BLOCK - 1:BLOCK, :]
            dte_x = jnp.exp2(last_x - acum_x)
            cd_x = jnp.exp2(last_x)
            xdt = xs * dt_x
            xw = xdt * dte_x
            hprev = hp_ref[j]
            dhn = dh[...]
            dy = dy_ref[rs, :]
            gdskx[...] += jnp.sum(dy * xs, axis=0, keepdims=True)
            dyea = dy * ea_x
            lane = lax.broadcasted_iota(jnp.int32, (BLOCK, SSM_HEADS), 1)
            dacum = jnp.zeros((BLOCK, SSM_HEADS), F32)
            dacc_x, dlast_x = [], []
            sls = [slice(g * gw, (g + 1) * gw) for g in range(SSM_G)]
            bgs = [_bf(xact[:, b0 + g * SSM_N:b0 + (g + 1) * SSM_N]) for g in range(SSM_G)]
            cgs = [_bf(xact[:, c0 + g * SSM_N:c0 + (g + 1) * SSM_N]) for g in range(SSM_G)]
            hpgs = [_bf(hprev[:, sl]) for sl in sls]
            dhgs = [_bf(dhn[:, sl]) for sl in sls]
            dyeags = [_bf(dyea[:, sl]) for sl in sls]
            xwgs = [_bf(xw[:, sl]) for sl in sls]
            xdt_b, dy_b = _bf(xdt), _bf(dy)
            low_half = lax.broadcasted_iota(jnp.int32, (BLOCK, 2 * SSM_P), 1) < SSM_P
            cbs = [_dot_nt(cgs[g], bgs[g]) for g in range(SSM_G)]
            gmats = [_dot(cgs[g], hpgs[g]) for g in range(SSM_G)]
            dxws = [_dot(bgs[g], dhgs[g]) for g in range(SSM_G)]
            dcgs = [_dot_nt(dyeags[g], hpgs[g]) for g in range(SSM_G)]
            dbgs = [_dot_nt(xwgs[g], dhgs[g]) for g in range(SSM_G)]
            for g in range(SSM_G):
                sl = sls[g]
                dh[:, sl] = dhn[:, sl] * cd_x[:, sl] + _dot_tn(cgs[g], dyeags[g])
                dxdt_s[:, sl] = dxws[g] * dte_x[:, sl]
                dacc_x.append(dy[:, sl] * gmats[g] * ea_x[:, sl] - dxws[g] * xw[:, sl])
                dlast_x.append(jnp.sum(dxws[g] * xw[:, sl], axis=0, keepdims=True)
                               + jnp.sum(dhn[:, sl] * hprev[:, sl], axis=0, keepdims=True) * cd_x[:, sl])
            for g in range(SSM_G):
                bg, cg, cb, dbg, dcg = bgs[g], cgs[g], cbs[g], dbgs[g], dcgs[g]
                hss = [slice((g * SSM_R + r) * SSM_P, (g * SSM_R + r + 1) * SSM_P) for r in range(SSM_R)]
                lms = [jnp.exp2(jnp.where(trilb, acum[:, g * SSM_R + r:g * SSM_R + r + 1]
                                         - acum_t[g * SSM_R + r:g * SSM_R + r + 1, :], -1e30)) for r in range(SSM_R)]
                mms = [cb * lm for lm in lms]
                mmbs = [_bf(mm) for mm in mms]
                dms = []
                for r in range(0, SSM_R, 2):
                    pair = slice(hss[r].start, hss[r + 1].stop)
                    xp, dyp = xdt_b[:, pair], dy_b[:, pair]
                    dmp = _dot_nt(dyp, jnp.concatenate([jnp.where(low_half, xp, 0), jnp.where(low_half, 0, xp)], axis=0))
                    dms += [dmp[:, :BLOCK], dmp[:, BLOCK:]]
                    dxd_s[:, pair] = _dot_tn(jnp.concatenate([mmbs[r], mmbs[r + 1]], axis=0),
                                             jnp.concatenate([jnp.where(low_half, dyp, 0), jnp.where(low_half, 0, dyp)], axis=0))
                dcb = sum(dms[r] * lms[r] for r in range(SSM_R))
                wms = [dms[r] * mms[r] for r in range(SSM_R)]
                antis = [_bf(wm - wm.T) for wm in wms]
                for r in range(SSM_R):
                    dacum = dacum + _dot(antis[r], (lane == g * SSM_R + r).astype(BF))
                dcbb = _bf(dcb)
                dbc_s[:, g * SSM_N:(g + 1) * SSM_N] = dbg + _dot_tn(dcbb, cg)
                dbc_s[:, SSM_G * SSM_N + g * SSM_N:SSM_G * SSM_N + (g + 1) * SSM_N] = dcg + _dot(dcbb, bg)
            dxdt = dxdt_s[...] + dxd_s[...]
            dxs = dy * dsk_ref[...] + dxdt * dt_x
            red = _group_sum(jnp.concatenate(
                [dxdt * xs, jnp.concatenate(dacc_x, axis=1),
                 jnp.broadcast_to(jnp.concatenate(dlast_x, axis=1), (8, SSM_W))], axis=0), e_ref[...])
            row = lax.broadcasted_iota(jnp.int32, (BLOCK, SSM_HEADS), 0)
            dacum = dacum + red[BLOCK:2 * BLOCK] + jnp.where(row == BLOCK - 1, red[2 * BLOCK:2 * BLOCK + 1], 0.0)
            ddta = _exact_left(_triu().astype(BF), dacum)
            ddt = red[:BLOCK] + ddta * a
            galog_ref[...] += jnp.sum(ddta * dt, axis=0, keepdims=True) * a
            du = ddt * _sig(u)
            ddt_ref[rs, :] = _bf(du)
            gdtb_ref[...] += jnp.sum(du, axis=0, keepdims=True)
            dconv = jnp.concatenate([dxs, dbc_s[...]], axis=1) * _dsilu(conv, sg, xact)
            gcb_ref[...] += jnp.sum(dconv, axis=0, keepdims=True)
            ext2 = jnp.concatenate([dconv, nhead[...]], axis=0)
            ahead = [pltpu.roll(ext2, BLOCK + 8 - (CONV_K - 1 - j), axis=0)[0:BLOCK] if j < CONV_K - 1 else dconv
                     for j in range(CONV_K)]
            dx_ref[rs, :] = _bf(sum(ahead[j] * cw_ref[j:j + 1, :] for j in range(CONV_K)))
            xraw = x_ref[rs, :]
            gcw_ref[...] += jnp.concatenate([jnp.sum(ahead[j] * xraw, axis=0, keepdims=True) for j in range(CONV_K)], axis=0)
            nhead[...] = dconv[0:8]

        i = pl.program_id(0)

        @pl.when(i == 0)
        def _():
            for ref in (dh, nhead, gdskx, gcw_ref, gcb_ref, gdtb_ref, galog_ref, gdsk_ref):
                ref[...] = jnp.zeros_like(ref)

        for j in reversed(range(ch)):
            chunk_bwd(j)

        @pl.when(i == nsteps - 1)
        def _():
            gdsk_ref[...] = _group_sum(jnp.broadcast_to(gdskx[...], (8, SSM_W)), e_ref[...])[0:1]

    chunk = lambda w: pl.BlockSpec((rows, w), lambda i: (nsteps - 1 - i, 0))
    sd = jax.ShapeDtypeStruct
    return pl.pallas_call(
        body, name="ssd_bwd", grid=(nsteps,),
        in_specs=[chunk(XBC_W), chunk(XBC_W),
                  chunk(SSM_HEADS), _full((CONV_K, XBC_W)), _full((1, SSM_HEADS)),
                  _full((1, SSM_HEADS)), _full((1, SSM_W)), _full((SSM_W, SSM_HEADS)), _full((3 * SSM_HEADS, SSM_W)),
                  pl.BlockSpec((ch, SSM_N, SSM_W), lambda i: (nsteps - 1 - i, 0, 0)), chunk(SSM_W)],
        out_specs=[chunk(XBC_W), chunk(SSM_HEADS), _full((CONV_K, XBC_W)), _full((1, XBC_W)),
                   _full((1, SSM_HEADS)), _full((1, SSM_HEADS)), _full((1, SSM_HEADS))],
        out_shape=[sd((s, XBC_W), BF), sd((s, SSM_HEADS), BF), sd((CONV_K, XBC_W), F32), sd((1, XBC_W), F32),
                   sd((1, SSM_HEADS), F32), sd((1, SSM_HEADS), F32), sd((1, SSM_HEADS), F32)],
        scratch_shapes=[pltpu.VMEM((SSM_N, SSM_W), F32), pltpu.VMEM((8, XBC_W), F32),
                        pltpu.VMEM((1, SSM_W), F32), pltpu.VMEM((BLOCK, SSM_W), F32),
                        pltpu.VMEM((BLOCK, 2 * SSM_G * SSM_N), F32), pltpu.VMEM((BLOCK, SSM_W), F32)],
        compiler_params=_params(dimension_semantics=("arbitrary",)),
    )(xbc, conv_all, dt_raw, conv_w, dt_bias, a_log, dsk_x, e_mat, e3t, hprev_all, dy_all)


def _dh(x, dout, norm_w, scale, dsegs, w_t, tm=256):
    s = x.shape[0]

    def body(x_ref, dout_ref, nw_ref, sc_ref, *rest):
        d_refs, w_hbm = rest[:NSEG], rest[NSEG]
        gx_ref, dshift_ref, dscale_ref, gnw_ref = rest[NSEG + 1:NSEG + 5]
        w_vm, sem = rest[NSEG + 5], rest[NSEG + 6]
        first = pl.program_id(0) == 0
        cps = [pltpu.make_async_copy(w_hbm.at[SEG_OFF[j]:SEG_OFF[j + 1], :], w_vm.at[SEG_OFF[j]:SEG_OFF[j + 1], :], sem.at[j])
               for j in range(NSEG)]

        def tile(waiting):
            dh = None
            for j in range(NSEG):
                if waiting:
                    cps[j].wait()
                part = _dot(d_refs[j][...], w_vm[SEG_OFF[j]:SEG_OFF[j + 1], :])
                dh = part if dh is None else dh + part
            xv = x_ref[...]
            r = lax.rsqrt(jnp.mean(xv * xv, axis=-1, keepdims=True) + EPS)
            xn = xv * r
            nw = nw_ref[...]
            sc1 = 1.0 + sc_ref[...]
            dshift_ref[...] += jnp.sum(dh, axis=0, keepdims=True)
            dhxn = jnp.sum(dh * xn, axis=0, keepdims=True)
            dscale_ref[...] += dhxn * nw
            gnw_ref[...] += dhxn * sc1
            dxn = dh * (nw * sc1)
            gx_ref[...] = dout_ref[...] + r * (dxn - xn * jnp.mean(xn * dxn, axis=-1, keepdims=True))

        @pl.when(first)
        def _():
            for cp in cps:
                cp.start()
            for ref in (dshift_ref, dscale_ref, gnw_ref):
                ref[...] = jnp.zeros_like(ref)
            tile(True)

        @pl.when(jnp.logical_not(first))
        def _():
            tile(False)

    vec = _full((1, D_MODEL))
    sd = jax.ShapeDtypeStruct
    return pl.pallas_call(
        body, name="dh", grid=(s // tm,),
        in_specs=[_rows(tm, D_MODEL), _rows(tm, D_MODEL), vec, vec] + [_rows(tm, w) for w in SEG_W] + [ANY],
        out_specs=[_rows(tm, D_MODEL), vec, vec, vec],
        out_shape=[sd((s, D_MODEL), F32), sd((1, D_MODEL), F32), sd((1, D_MODEL), F32), sd((1, D_MODEL), F32)],
        scratch_shapes=[pltpu.VMEM((IN_W, D_MODEL), BF), pltpu.SemaphoreType.DMA((NSEG,))],
        compiler_params=_params(dimension_semantics=("arbitrary",)),
    )(x, dout, norm_w, scale, *dsegs, w_t)


def _gw_seg(h, dseg, name, tm=1024):
    s, w = dseg.shape
    tn = min(w, 1024)
    tm = min(tm, s)
    nm = s // tm

    def body(h_ref, d_ref, o_ref, acc):
        m = pl.program_id(1)

        @pl.when(m == 0)
        def _():
            acc[...] = jnp.zeros_like(acc)

        acc[...] += _dot_tn(d_ref[...], h_ref[...])

        @pl.when(m == nm - 1)
        def _():
            o_ref[...] = _bf(acc[...])

    return pl.pallas_call(
        body, name=name, grid=(w // tn, nm),
        in_specs=[pl.BlockSpec((tm, D_MODEL), lambda n, m: (m, 0)), pl.BlockSpec((tm, tn), lambda n, m: (m, n))],
        out_specs=pl.BlockSpec((tn, D_MODEL), lambda n, m: (n, 0)),
        out_shape=jax.ShapeDtypeStruct((w, D_MODEL), BF),
        scratch_shapes=[pltpu.VMEM((tn, D_MODEL), F32)],
        compiler_params=_params(dimension_semantics=("arbitrary", "arbitrary")),
    )(h, dseg)


def _gw_in(h, dsegs):
    return [_gw_seg(h, d, "gw_in_%d" % j) for j, d in enumerate(dsegs)]


def _local_step(x, tgt, shift, scale, gate, w_t, rows_fn, norm_w, qnw, knw, rel_bias, sinks,
                conv_w, conv_b, dt_bias, a_log, d_skip, ssm_nw, after_mid=None, after_gw=None):
    oh_t = _bucket_onehot_t()
    bias = _masked_bias(_bias_dense(rel_bias.T, oh_t).reshape(ATTN_HEADS, BLOCK, 2 * BLOCK))
    *segs, h = _inproj(x, norm_w, scale, shift, w_t)
    q, kv, zam, xbc, dtr, gab = segs
    consts = _attn_consts(qnw, knw)
    o_att, lse = _attn_fwd(q, kv, bias, sinks, consts)
    e_mat, e3t = _membership(SSM_W, SSM_P, SSM_HEADS)
    dsk_x = jnp.repeat(d_skip, SSM_P, axis=1)
    ypre, hprev, conv = _ssd_fwd(xbc, dtr, conv_w, conv_b, dt_bias, a_log, dsk_x, e3t)
    (dout, d_o, dzam, dyp, dgab, yag, dy_a, yn, dy_b, merged, dob, g_ssm_nw, dgate, loss) = _mid(
        x, tgt, o_att, zam, ypre, gab, gate, ssm_nw, rows_fn(ypre))
    g_wap = _gw_seg(dy_a, yag, "gw_attn_proj")
    g_wsp = _gw_seg(dy_b, yn, "gw_ssm_proj")
    g_wout = _gw_seg(dob, merged, "gw_out")
    zero = after_mid(g_wap, g_wsp, g_wout) if after_mid is not None else 0.0
    dq, dkv, dss, g_qnw, g_knw, g_sinks = _attn_bwd(q, kv, bias, sinks + zero, consts, o_att, lse, d_o)
    g_rel = _bias_grad(dss.reshape(ATTN_HEADS, BLOCK * 2 * BLOCK), oh_t).T
    dxbc, ddt, g_cw, g_cb, g_dtb, g_alog, g_dsk = _ssd_bwd(
        xbc, conv, dtr, conv_w, dt_bias, a_log, dsk_x, e_mat, e3t, hprev, dyp)
    dsegs = (dq, dkv, dzam, dxbc, ddt, dgab)
    g_ws = _gw_in(h, dsegs)
    zero = after_gw(g_ws) if after_gw is not None else 0.0
    gx, dshift, dscale, g_nw = _dh(x, dout, norm_w + zero, scale, dsegs, w_t)
    return dict(loss=loss, grad_x=gx, dmod=jnp.concatenate([dshift, dscale, dgate], axis=1), g_ws=g_ws,
                g_wap=g_wap, g_wsp=g_wsp, g_wout=g_wout, g_norm_w=g_nw, g_qnw=g_qnw, g_knw=g_knw, g_rel=g_rel,
                g_sinks=g_sinks, g_conv_w=g_cw, g_conv_b=g_cb, g_dt_bias=g_dtb, g_a_log=g_alog, g_d_skip=g_dsk,
                g_ssm_nw=g_ssm_nw)


def _me():
    return lax.axis_index("x"), lax.axis_index("y"), lax.axis_index("c")


def _flip(v, bit):
    return 1 - v if bit else v


def _ag_direct(v, name):
    def body(v_ref, out_ref, send_sems, recv_sems, local_sem):
        x, y, c = _me()
        me = 4 * x + 2 * y + c
        mine = pltpu.make_async_copy(v_ref, out_ref.at[me], local_sem)
        mine.start()
        peers = [(_flip(x, k >> 2 & 1), _flip(y, k >> 1 & 1), _flip(c, k & 1)) for k in range(1, N_DEV)]
        sends = [pltpu.make_async_remote_copy(
            src_ref=v_ref, dst_ref=out_ref.at[me], send_sem=send_sems.at[j], recv_sem=recv_sems.at[j],
            device_id=p, device_id_type=MESH) for j, p in enumerate(peers)]
        for cp in sends:
            cp.start()
        for j, (px, py, pc) in enumerate(peers):
            pltpu.make_async_remote_copy(
                src_ref=v_ref, dst_ref=out_ref.at[4 * px + 2 * py + pc], send_sem=send_sems.at[j],
                recv_sem=recv_sems.at[j], device_id=(px, py, pc), device_id_type=MESH).wait_recv()
        for cp in sends:
            cp.wait_send()
        mine.wait()

    vm = pl.BlockSpec(memory_space=pltpu.VMEM)
    return pl.pallas_call(
        body, name=name, out_shape=jax.ShapeDtypeStruct((N_DEV,) + v.shape, v.dtype),
        in_specs=[vm], out_specs=vm,
        scratch_shapes=[pltpu.SemaphoreType.DMA((N_DEV - 1,)), pltpu.SemaphoreType.DMA((N_DEV - 1,)),
                        pltpu.SemaphoreType.DMA],
        compiler_params=_params(),
    )(v)


def _gather_mod(v, w_ada, b_piece):
    ncols = w_ada.shape[1]

    def body(v_ref, w_ref, b_ref, rows_ref, mods_ref, piece, send_sems, recv_sems, local_sems):
        x, y, c = _me()
        me = 4 * x + 2 * y + c
        peers = _peers(x, y, c)

        def exchange(src, dst, rnd):
            mine = pltpu.make_async_copy(src, dst.at[me], local_sems.at[rnd])
            mine.start()
            sends = [pltpu.make_async_remote_copy(
                src_ref=src, dst_ref=dst.at[me], send_sem=send_sems.at[rnd, j], recv_sem=recv_sems.at[rnd, j],
                device_id=p, device_id_type=MESH) for j, p in enumerate(peers)]
            for cp in sends:
                cp.start()
            for j, (px, py, pc) in enumerate(peers):
                pltpu.make_async_remote_copy(
                    src_ref=src, dst_ref=dst.at[4 * px + 2 * py + pc], send_sem=send_sems.at[rnd, j],
                    recv_sem=recv_sems.at[rnd, j], device_id=(px, py, pc), device_id_type=MESH).wait_recv()
            for cp in sends:
                cp.wait_send()
            mine.wait()

        exchange(v_ref, rows_ref, 0)
        c_all = rows_ref[:, 0, :D_MODEL]
        piece[...] = _dot(_bf(_silu(c_all)), _bf(w_ref[...])) + b_ref[...]
        exchange(piece, mods_ref, 1)

    vm = pl.BlockSpec(memory_space=pltpu.VMEM)
    return pl.pallas_call(
        body, name="gather_mod",
        out_shape=(jax.ShapeDtypeStruct((N_DEV,) + v.shape, F32), jax.ShapeDtypeStruct((N_DEV, N_DEV, ncols), F32)),
        in_specs=[vm, vm, vm], out_specs=(vm, vm),
        scratch_shapes=[pltpu.VMEM((N_DEV, ncols), F32), pltpu.SemaphoreType.DMA((2, N_DEV - 1)),
                        pltpu.SemaphoreType.DMA((2, N_DEV - 1)), pltpu.SemaphoreType.DMA((2,))],
        compiler_params=_params(),
    )(v, w_ada, b_piece)


def _ag_relayed(v, name, chunks=1):
    rows = v.shape[0] // chunks
    assert rows * chunks == v.shape[0] and rows % 8 == 0

    def body(v_ref, out_ref, token, send_sems, recv_sems, local_sem):
        token[...] = jnp.zeros_like(token)
        x, y, c = _me()
        flip_x, flip_y = 1 - x, 1 - y
        ax, ay = c * x + (1 - c) * flip_x, c * flip_y + (1 - c) * y
        bx, by = c * flip_x + (1 - c) * x, c * y + (1 - c) * flip_y
        me, sib = (x, y, c), (x, y, 1 - c)
        a, b, dg = (ax, ay, c), (bx, by, c), (flip_x, flip_y, c)
        sa, sb, sdg = (bx, by, 1 - c), (ax, ay, 1 - c), (flip_x, flip_y, 1 - c)

        def piece(ref, k):
            return ref.at[pl.ds(k * rows, rows), :]

        def slot(px, py, pc):
            return out_ref.at[4 * px + 2 * py + pc]

        def copy(n, k, block, to, src=None):
            return pltpu.make_async_remote_copy(
                src_ref=piece(slot(*block) if src is None else src, k), dst_ref=piece(slot(*block), k),
                send_sem=send_sems.at[n * chunks + k], recv_sem=recv_sems.at[n * chunks + k],
                device_id=to, device_id_type=MESH)

        mine = pltpu.make_async_copy(v_ref, slot(*me), local_sem)
        mine.start()
        started = [copy(n, k, me, to, src=v_ref) for k in range(chunks) for n, to in ((1, a), (2, b), (0, sib))]
        for cp in started:
            cp.start()

        def arrived(n, k, block, then):
            copy(n, k, block, me).wait_recv()
            for n2, to in then:
                started.append(copy(n2, k, block, to))
                started[-1].start()

        for k in range(chunks):
            arrived(1, k, a, ((3, b), (4, sib)))
            arrived(2, k, b, ((5, sib),))
        for k in range(chunks):
            arrived(3, k, dg, ((6, sib),))
        for k in range(chunks):
            for n, block in ((0, sib), (4, sa), (5, sb), (6, sdg)):
                copy(n, k, block, me).wait_recv()
        for cp in started:
            cp.wait_send()
        mine.wait()

    out, token = pl.pallas_call(
        body, name=name,
        out_shape=(jax.ShapeDtypeStruct((N_DEV,) + v.shape, v.dtype), jax.ShapeDtypeStruct((8, 128), v.dtype)),
        in_specs=[ANY], out_specs=(ANY, pl.BlockSpec(memory_space=pltpu.VMEM)),
        scratch_shapes=[pltpu.SemaphoreType.DMA((7 * chunks,)), pltpu.SemaphoreType.DMA((7 * chunks,)),
                        pltpu.SemaphoreType.DMA],
        compiler_params=_params(),
    )(v)
    return out, token[0:1, 0:1]


HBM = pl.BlockSpec(memory_space=pltpu.HBM)
SEM = pl.BlockSpec(memory_space=pltpu.SEMAPHORE)
EFFECT = pltpu.SideEffectType.DATAFLOW_SIDE_EFFECTING


def _peers(x, y, c):
    return [(_flip(x, k >> 2 & 1), _flip(y, k >> 1 & 1), _flip(c, k & 1)) for k in range(1, N_DEV)]


def _exchange_start(src, land, gather, name):
    def body(src_ref, land_ref, send_sems, recv_sems, src_thru, land_thru, token):
        x, y, c = _me()
        me = 4 * x + 2 * y + c
        for j, (px, py, pc) in enumerate(_peers(x, y, c)):
            pltpu.make_async_remote_copy(
                src_ref=src_ref if gather else src_ref.at[4 * px + 2 * py + pc], dst_ref=land_ref.at[me],
                send_sem=send_sems.at[j], recv_sem=recv_sems.at[j], device_id=(px, py, pc), device_id_type=MESH).start()
        token[...] = jnp.zeros_like(token)

    sems = pltpu.SemaphoreType.DMA((N_DEV - 1,))
    out = pl.pallas_call(
        body, name=name,
        out_shape=(sems, sems, pltpu.HBM(src.shape, src.dtype), pltpu.HBM(land.shape, land.dtype),
                   jax.ShapeDtypeStruct((8, 128), F32)),
        in_specs=(HBM, HBM), out_specs=(SEM, SEM, HBM, HBM, pl.BlockSpec(memory_space=pltpu.VMEM)),
        input_output_aliases={0: 2, 1: 3},
        compiler_params=pltpu.CompilerParams(has_side_effects=EFFECT),
    )(pltpu.with_memory_space_constraint(src, pltpu.HBM), pltpu.with_memory_space_constraint(land, pltpu.HBM))
    return out[:4], out[4][0, 0]


def _exchange_wait(started, after, gather, name):
    send_sems, recv_sems, src_thru, land_thru = started

    def body(src_ref, land_ref, send_sems, recv_sems, after_ref, src_dead, got_ref):
        x, y, c = _me()
        for j, (px, py, pc) in enumerate(_peers(x, y, c)):
            pid = 4 * px + 2 * py + pc
            cp = pltpu.make_async_remote_copy(
                src_ref=src_ref if gather else src_ref.at[pid], dst_ref=land_ref.at[pid],
                send_sem=send_sems.at[j], recv_sem=recv_sems.at[j], device_id=(px, py, pc), device_id_type=MESH)
            cp.wait_send()
            cp.wait_recv()

    return pl.pallas_call(
        body, name=name,
        out_shape=(pltpu.HBM(src_thru.shape, src_thru.dtype), pltpu.HBM(land_thru.shape, land_thru.dtype)),
        in_specs=(HBM, HBM, SEM, SEM, ANY), out_specs=(HBM, HBM), input_output_aliases={0: 0, 1: 1},
        compiler_params=pltpu.CompilerParams(has_side_effects=EFFECT),
    )(src_thru, land_thru, send_sems, recv_sems, after)[1]


def _silu(a):
    return a * _sig(a)


def _gw_ada(c_all, dmod_piece):
    def body(c_ref, d_ref, o_ref):
        o_ref[...] = _dot_tn(_bf(_silu(c_ref[...])), _bf(d_ref[...]))

    return pl.pallas_call(
        body, name="gw_ada", out_shape=jax.ShapeDtypeStruct((c_all.shape[1], dmod_piece.shape[1]), F32),
        compiler_params=_params(),
    )(c_all, dmod_piece)


def _adam(parts, w, m, v, name):
    k, r, n = parts.shape
    if r <= 256 or r % 256 == 0:
        tr, tn = min(r, 256), n
    else:
        tr, tn = r, 256
    assert r % tr == 0 and n % tn == 0

    def body(p_ref, w_ref, m_ref, v_ref, g_ref, d_ref, nm_ref, nv_ref):
        g = p_ref[0].astype(F32)
        for j in range(1, k):
            g = g + p_ref[j].astype(F32)
        g_ref[...] = g
        d_ref[...], nm_ref[...], nv_ref[...] = _adam_math(g, w_ref[...], m_ref[...], v_ref[...])

    blk = pl.BlockSpec((tr, tn), lambda i, j: (i, j))
    return pl.pallas_call(
        body, name=name, grid=(r // tr, n // tn),
        in_specs=[pl.BlockSpec((k, tr, tn), lambda i, j: (0, i, j)), blk, blk, blk],
        out_specs=[blk, blk, blk, blk],
        out_shape=[jax.ShapeDtypeStruct((r, n), F32)] * 4,
        compiler_params=_params(dimension_semantics=("arbitrary", "arbitrary")),
    )(parts, w, m, v)


def _adam_math(g, w, m, v):
    m_new = ADAM_B1 * m + (1.0 - ADAM_B1) * g
    v_new = ADAM_B2 * v + (1.0 - ADAM_B2) * jnp.square(g)
    m_hat = m_new / (1.0 - ADAM_B1 ** ADAM_STEP)
    v_hat = v_new / (1.0 - ADAM_B2 ** ADAM_STEP)
    return -ADAM_LR * (m_hat / (jnp.sqrt(v_hat) + ADAM_EPS) + ADAM_WD * w), m_new, v_new


_SMALL = (("b_ada", 3 * D_MODEL), ("norm_w", D_MODEL), ("q_norm_w", HEAD_DIM), ("k_norm_w", HEAD_DIM),
          ("rel_bias", REL_BUCKETS * ATTN_HEADS), ("sinks", ATTN_HEADS), ("conv_b", XBC_W), ("dt_bias", SSM_HEADS),
          ("a_log", SSM_HEADS), ("d_skip", SSM_HEADS), ("ssm_norm_w", SSM_W))
_SLOT = tuple(-(-n // 128) * 128 for _, n in _SMALL)
_SLOT_OFF = tuple(int(o) for o in np.cumsum((0,) + _SLOT))
_LOSS_OFF = _SLOT_OFF[-1]
_CW_OFF = _LOSS_OFF + 128
_PACK_N = _CW_OFF + CONV_K * XBC_W


def _pack_partials(small, loss, g_conv_w):
    parts = []
    for (name, n), slot in zip(_SMALL, _SLOT):
        parts.append(small[name].reshape(1, n))
        if slot > n:
            parts.append(jnp.zeros((1, slot - n), F32))
    parts += [loss.reshape(1, 1), jnp.zeros((1, 127), F32), g_conv_w.reshape(1, CONV_K * XBC_W)]
    return jnp.concatenate(parts, axis=1)


def _adam_small(pack_all, w, m, v):
    names = [name for name, _ in _SMALL]

    def body(p_ref, *rest):
        ins, outs = rest[:3 * len(names)], rest[3 * len(names):]

        def total(off, n):
            g = p_ref[0, :, off:off + n]
            for d in range(1, N_DEV):
                g = g + p_ref[d, :, off:off + n]
            return g

        for j, (name, n) in enumerate(_SMALL):
            g = total(_SLOT_OFF[j], n)
            delta, m_new, v_new = _adam_math(g, ins[3 * j][...], ins[3 * j + 1][...], ins[3 * j + 2][...])
            outs[4 * j][...] = g
            outs[4 * j + 1][...] = delta
            outs[4 * j + 2][...] = m_new
            outs[4 * j + 3][...] = v_new
        outs[-1][...] = total(_LOSS_OFF, 1)

    flat = []
    for name, n in _SMALL:
        flat += [w[name].reshape(1, n), m[name].reshape(1, n), v[name].reshape(1, n)]
    out_shape = [jax.ShapeDtypeStruct((1, n), F32) for _, n in _SMALL for _ in range(4)] + [jax.ShapeDtypeStruct((1, 1), F32)]
    out = pl.pallas_call(body, name="adam_small", out_shape=out_shape, compiler_params=_params())(pack_all, *flat)
    res = {name: [out[4 * j + t].reshape(w[name].shape) for t in range(4)] for j, name in enumerate(names)}
    return res, out[-1]


WEIGHTS = ("w_ada", "b_ada", "norm_w", "w_in", "q_norm_w", "k_norm_w", "rel_bias", "sinks", "conv_w", "conv_b",
           "dt_bias", "a_log", "d_skip", "ssm_norm_w", "w_attn_proj", "w_ssm_proj", "w_out")


def kernel(x, c, w_ada, b_ada, norm_w, w_in, q_norm_w, k_norm_w, rel_bias, sinks, conv_w, conv_b, dt_bias, a_log, d_skip, ssm_norm_w, w_attn_proj, w_ssm_proj, w_out, loss_target, m_w_ada, m_b_ada, m_norm_w, m_w_in, m_q_norm_w, m_k_norm_w, m_rel_bias, m_sinks, m_conv_w, m_conv_b, m_dt_bias, m_a_log, m_d_skip, m_ssm_norm_w, m_w_attn_proj, m_w_ssm_proj, m_w_out, v_w_ada, v_b_ada, v_norm_w, v_w_in, v_q_norm_w, v_k_norm_w, v_rel_bias, v_sinks, v_conv_w, v_conv_b, v_dt_bias, v_a_log, v_d_skip, v_ssm_norm_w, v_w_attn_proj, v_w_ssm_proj, v_w_out):
    w = dict(w_ada=w_ada, b_ada=b_ada, norm_w=norm_w, w_in=w_in, q_norm_w=q_norm_w, k_norm_w=k_norm_w,
             rel_bias=rel_bias, sinks=sinks, conv_w=conv_w, conv_b=conv_b, dt_bias=dt_bias, a_log=a_log,
             d_skip=d_skip, ssm_norm_w=ssm_norm_w, w_attn_proj=w_attn_proj, w_ssm_proj=w_ssm_proj, w_out=w_out)
    m = dict(w_ada=m_w_ada, b_ada=m_b_ada, norm_w=m_norm_w, w_in=m_w_in, q_norm_w=m_q_norm_w, k_norm_w=m_k_norm_w,
             rel_bias=m_rel_bias, sinks=m_sinks, conv_w=m_conv_w, conv_b=m_conv_b, dt_bias=m_dt_bias, a_log=m_a_log,
             d_skip=m_d_skip, ssm_norm_w=m_ssm_norm_w, w_attn_proj=m_w_attn_proj, w_ssm_proj=m_w_ssm_proj, w_out=m_w_out)
    v = dict(w_ada=v_w_ada, b_ada=v_b_ada, norm_w=v_norm_w, w_in=v_w_in, q_norm_w=v_q_norm_w, k_norm_w=v_k_norm_w,
             rel_bias=v_rel_bias, sinks=v_sinks, conv_w=v_conv_w, conv_b=v_conv_b, dt_bias=v_dt_bias, a_log=v_a_log,
             d_skip=v_d_skip, ssm_norm_w=v_ssm_norm_w, w_attn_proj=v_w_attn_proj, w_ssm_proj=v_w_ssm_proj, w_out=v_w_out)
    me = 4 * lax.axis_index("x") + 2 * lax.axis_index("y") + lax.axis_index("c")
    ada_n = w_ada.shape[2]
    in_n = w_in.shape[2]
    cw_n = conv_w.shape[2]

    b_piece = lax.dynamic_slice_in_dim(b_ada, me * ada_n, ada_n, axis=1)
    first, mod_all = _gather_mod(jnp.concatenate([c, conv_w[0].reshape(1, CONV_K * cw_n)], axis=1), w_ada[0], b_piece)
    first = first[:, 0]
    c_all = first[:, :D_MODEL]
    conv_w_full = first[:, D_MODEL:].reshape(N_DEV, CONV_K, cw_n).transpose(1, 0, 2).reshape(CONV_K, XBC_W)
    mod = lax.dynamic_index_in_dim(mod_all, me, axis=1, keepdims=False).reshape(1, 3 * D_MODEL)
    shift, scale, gate = mod[:, :D_MODEL], mod[:, D_MODEL:2 * D_MODEL], mod[:, 2 * D_MODEL:]

    pad = -in_n % (8 * AG_PIECES)
    w_t, zero = _ag_relayed(jnp.pad(w_in[0].T.astype(BF), ((0, pad), (0, 0))), "ag_w_in", chunks=AG_PIECES)
    w_t = w_t[:, :in_n].reshape(N_DEV * in_n, D_MODEL)

    def with_mine(blocks, mine):
        return lax.dynamic_update_index_in_dim(lax.empty(blocks, mine.dtype), mine, me, axis=0)

    rows = jnp.concatenate([w_attn_proj[0], w_ssm_proj[0], w_out[0]], axis=0).astype(BF) + zero
    r_ap, r_sp = w_attn_proj.shape[1], w_ssm_proj.shape[1]
    rows_started, zero = _exchange_start(rows, with_mine((N_DEV,) + rows.shape, rows), True, "ag_rows_start")

    def rows_fn(after):
        return _exchange_wait(rows_started, after, True, "ag_rows_wait")

    started = {}

    def send_blocks(key, g, name):
        started[key], zero = _exchange_start(
            g, with_mine(g.shape, lax.dynamic_index_in_dim(g, me, axis=0, keepdims=False)), False, name)
        return zero

    def after_mid(g_wap, g_wsp, g_wout):
        return send_blocks("rows", jnp.concatenate(
            [g_wap.reshape(N_DEV, r_ap, D_MODEL), g_wsp.reshape(N_DEV, r_sp, D_MODEL),
             g_wout.reshape(N_DEV, r_ap, D_MODEL)], axis=1), "rs_rows_start")

    def after_gw(g_ws):
        return send_blocks("in", jnp.concatenate(g_ws, axis=0).reshape(N_DEV, in_n, D_MODEL), "rs_in_start")

    r = _local_step(x[0], loss_target[0], shift, scale + zero, gate, w_t, rows_fn, norm_w, q_norm_w, k_norm_w,
                    rel_bias, sinks, conv_w_full, conv_b, dt_bias, a_log, d_skip, ssm_norm_w, after_mid, after_gw)

    small = dict(b_ada=r["dmod"], norm_w=r["g_norm_w"], q_norm_w=r["g_qnw"], k_norm_w=r["g_knw"], rel_bias=r["g_rel"],
                 sinks=r["g_sinks"], conv_b=r["g_conv_b"], dt_bias=r["g_dt_bias"], a_log=r["g_a_log"],
                 d_skip=r["g_d_skip"], ssm_norm_w=r["g_ssm_nw"])
    pack_all = _ag_direct(_pack_partials(small, r["loss"], r["g_conv_w"]), "ag_small")
    res, loss = _adam_small(pack_all, w, m, v)
    loss = loss[0, 0]
    cw_parts = pack_all[:, 0, _CW_OFF:].reshape(N_DEV, CONV_K, XBC_W)
    cw_mine = lax.dynamic_slice_in_dim(cw_parts, me * cw_n, cw_n, axis=2)
    res["conv_w"] = [a[None] for a in _adam(cw_mine, conv_w[0], m_conv_w[0], v_conv_w[0], "adam_conv_w")]

    dmod_piece = lax.dynamic_slice_in_dim(pack_all[:, 0, :3 * D_MODEL], me * ada_n, ada_n, axis=1)
    g_ada = _gw_ada(c_all, dmod_piece)
    res["w_ada"] = [a[None] for a in _adam(g_ada[None], w_ada[0], m_w_ada[0], v_w_ada[0], "adam_w_ada")]

    cat = lambda d: jnp.concatenate([d["w_attn_proj"][0], d["w_ssm_proj"][0], d["w_out"][0]], axis=0)
    rows_res = _adam(_exchange_wait(started["rows"], g_ada, False, "rs_rows_wait"), cat(w), cat(m), cat(v), "adam_w_rows")
    res["w_in"] = [a.T[None] for a in _adam(_exchange_wait(started["in"], rows_res[0], False, "rs_in_wait"),
                                            w_in[0].T, m_w_in[0].T, v_w_in[0].T, "adam_w_in")]
    res["w_attn_proj"] = [a[None, :r_ap] for a in rows_res]
    res["w_ssm_proj"] = [a[None, r_ap:r_ap + r_sp] for a in rows_res]
    res["w_out"] = [a[None, r_ap + r_sp:] for a in rows_res]

    outs = [loss, r["grad_x"][None]]
    for j in range(4):
        outs += [res[name][j] for name in WEIGHTS]
    return tuple(outs)
```

```python
import math

import numpy as np
import jax
import jax.numpy as jnp
from jax import lax
from jax.experimental import pallas as pl
from jax.experimental.pallas import tpu as pltpu

F32 = jnp.float32
BF = jnp.bfloat16
HI = lax.Precision.HIGHEST

D_MODEL = 1024
ATTN_HEADS = 16
KV_HEADS = 4
GRP = ATTN_HEADS // KV_HEADS
HEAD_DIM = 64
ATTN_W = ATTN_HEADS * HEAD_DIM
KV_W = KV_HEADS * HEAD_DIM
BLOCK = 128
REL_BUCKETS = 32
REL_MAX_DIST = 128
SSM_W = 2048
SSM_P = 64
SSM_HEADS = 32
SSM_G = 4
SSM_R = 8
SSM_N = 128
CONV_K = 4
XBC_W = SSM_W + 2 * SSM_G * SSM_N
SEG_W = (ATTN_W, 2 * KV_W, ATTN_W + SSM_W, XBC_W, SSM_HEADS, 2 * D_MODEL)
NSEG = len(SEG_W)
SEG_OFF = tuple(int(v) for v in np.cumsum((0,) + SEG_W))
IN_W = SEG_OFF[-1]
GATE_SEGS = (2, 5)
EPS = 1e-6
N_DEV = 8
ADAM_LR, ADAM_B1, ADAM_B2, ADAM_EPS, ADAM_WD, ADAM_STEP = 0.001, 0.9, 0.999, 1e-08, 0.01, 10
VMEM_LIMIT = 60 * 1024 * 1024
MESH = pl.DeviceIdType.MESH
ANY = pl.BlockSpec(memory_space=pl.ANY)


def _dot(a, b, precision=None):
    return jnp.dot(a, b, preferred_element_type=F32, precision=precision)


def _dot_nt(a, b, precision=None):
    return lax.dot_general(a, b, (((1,), (1,)), ((), ())), preferred_element_type=F32, precision=precision)


def _dot_tn(a, b, precision=None):
    return lax.dot_general(a, b, (((0,), (0,)), ((), ())), preferred_element_type=F32, precision=precision)


def _bf(a):
    return a.astype(BF)


def _sig(a):
    return 0.5 * jnp.tanh(0.5 * a) + 0.5


def _params(**kw):
    return pltpu.CompilerParams(vmem_limit_bytes=VMEM_LIMIT, **kw)


def _full(shape):
    nd = len(shape)
    return pl.BlockSpec(shape, lambda i: (0,) * nd)


def _rows(tm, w):
    return pl.BlockSpec((tm, w), lambda i: (i, 0))


def _inproj(x, norm_w, scale, shift, w_t, tm=256):
    s = x.shape[0]

    def body(x_ref, nw_ref, sc_ref, sh_ref, w_hbm, *rest):
        outs, h_ref, w_vm, sem = rest[:NSEG], rest[NSEG], rest[NSEG + 1], rest[NSEG + 2]
        first = pl.program_id(0) == 0
        cps = [pltpu.make_async_copy(w_hbm.at[SEG_OFF[j]:SEG_OFF[j + 1], :], w_vm.at[SEG_OFF[j]:SEG_OFF[j + 1], :], sem.at[j])
               for j in range(NSEG)]

        def tile(waiting):
            xv = x_ref[...]
            r = lax.rsqrt(jnp.mean(xv * xv, axis=-1, keepdims=True) + EPS)
            h = xv * r * (nw_ref[...] * (1.0 + sc_ref[...])) + sh_ref[...]
            hb = _bf(h)
            h_ref[...] = hb
            for j in range(NSEG):
                if waiting:
                    cps[j].wait()
                outs[j][...] = _dot_nt(hb, w_vm[SEG_OFF[j]:SEG_OFF[j + 1], :]).astype(outs[j].dtype)

        @pl.when(first)
        def _():
            for cp in cps:
                cp.start()
            tile(True)

        @pl.when(jnp.logical_not(first))
        def _():
            tile(False)

    vec = _full((1, D_MODEL))
    return pl.pallas_call(
        body, name="inproj", grid=(s // tm,),
        in_specs=[_rows(tm, D_MODEL), vec, vec, vec, ANY],
        out_specs=[_rows(tm, w) for w in SEG_W] + [_rows(tm, D_MODEL)],
        out_shape=[jax.ShapeDtypeStruct((s, w), BF if j in GATE_SEGS else F32) for j, w in enumerate(SEG_W)]
                  + [jax.ShapeDtypeStruct((s, D_MODEL), BF)],
        scratch_shapes=[pltpu.VMEM((IN_W, D_MODEL), BF), pltpu.SemaphoreType.DMA((NSEG,))],
        compiler_params=_params(dimension_semantics=("arbitrary",)),
    )(x, norm_w, scale, shift, w_t)


def _bucket_onehot_t():
    qi = jnp.arange(BLOCK)[:, None]
    kj = jnp.arange(2 * BLOCK)[None, :]
    dist = qi + BLOCK - kj
    n = jnp.maximum(dist, 0)
    max_exact = REL_BUCKETS // 2
    nf = jnp.maximum(n, 1).astype(F32)
    large = max_exact + (jnp.log(nf / max_exact) / math.log(REL_MAX_DIST / max_exact)
                         * (REL_BUCKETS - max_exact)).astype(jnp.int32)
    large = jnp.minimum(large, REL_BUCKETS - 1)
    bucket = jnp.where(n < max_exact, n, large).reshape(1, BLOCK * 2 * BLOCK)
    return (bucket == jnp.arange(REL_BUCKETS)[:, None]).astype(F32)


def _bias_dense(rel_bias_t, oh_t):
    def body(rb_ref, oh_ref, o_ref):
        o_ref[...] = _dot(rb_ref[...], oh_ref[...], HI)

    return pl.pallas_call(
        body, name="bias_dense", out_shape=jax.ShapeDtypeStruct((ATTN_HEADS, BLOCK * 2 * BLOCK), F32),
        compiler_params=_params(),
    )(rel_bias_t, oh_t)


def _bias_grad(ds_sum, oh_t):
    def body(ds_ref, oh_ref, o_ref):
        o_ref[...] = _dot_nt(ds_ref[...], oh_ref[...], HI)

    return pl.pallas_call(
        body, name="bias_grad", out_shape=jax.ShapeDtypeStruct((ATTN_HEADS, REL_BUCKETS), F32),
        compiler_params=_params(),
    )(ds_sum, oh_t)


def _group_sum(a, e):
    hi = _bf(a)
    return _dot(hi, e) + _dot(_bf(a - hi.astype(F32)), e)


def _group_bcast(a, e3t):
    hi = _bf(a)
    r1 = a - hi.astype(F32)
    mid = _bf(r1)
    return _dot(jnp.concatenate([hi, mid, _bf(r1 - mid.astype(F32))], axis=1), e3t)


def _membership(width, group, ngroups):
    e = (jnp.arange(width)[:, None] // group == jnp.arange(ngroups)[None, :]).astype(BF)
    return e, jnp.tile(e.T, (3, 1))


def _fold(width, group):
    return (jnp.arange(width)[:, None] % group == jnp.arange(group)[None, :]).astype(BF)


def _heads_norm(t, w_x, e, e3t):
    r = lax.rsqrt(_dot(_bf(t * t), e) * (1.0 / HEAD_DIM) + EPS)
    r_x = _group_bcast(r, e3t)
    return t * r_x * w_x, r_x


def _heads_norm_bwd(t, r_x, w_x, d, e, e3t):
    wd = d * w_x
    corr = _group_bcast(_dot(_bf(t * wd), e) * (1.0 / HEAD_DIM), e3t)
    return r_x * wd - t * (r_x * r_x * r_x) * corr, jnp.sum(d * t * r_x, axis=0, keepdims=True)


def _stack_heads(a, hk):
    return jnp.concatenate([a[:, (hk * GRP + g) * HEAD_DIM:(hk * GRP + g + 1) * HEAD_DIM] for g in range(GRP)], axis=0)


def _stack_cols(a, hk):
    return jnp.concatenate([a[:, hk * GRP + g:hk * GRP + g + 1] for g in range(GRP)], axis=0)


def _masked_bias(bias):
    qi = jnp.arange(BLOCK)[:, None]
    kj = jnp.arange(2 * BLOCK)[None, :]
    cur_ok = jnp.logical_and(kj >= BLOCK, kj - BLOCK <= qi)
    both_ok = jnp.logical_or(jnp.logical_and(kj < BLOCK, kj > qi), cur_ok)
    return jnp.stack([jnp.where(cur_ok, bias, -1e30), jnp.where(both_ok, bias, -1e30)])


def _attn_consts(qnw, knw):
    eq, eq3t = _membership(ATTN_W, HEAD_DIM, ATTN_HEADS)
    ek, ek3t = _membership(KV_W, HEAD_DIM, ATTN_HEADS)
    return (jnp.tile(qnw, (1, ATTN_HEADS)), jnp.tile(knw, (1, KV_HEADS)), eq, eq3t, ek, ek3t)


def _attn_fwd(q, kv, bias, sinks, consts):
    s = q.shape[0]
    nb = s // BLOCK
    gq = GRP * BLOCK
    bias_t = bias.reshape(2, KV_HEADS, GRP, BLOCK, 2 * BLOCK).transpose(0, 1, 4, 2, 3).reshape(2, KV_HEADS, 2 * BLOCK, gq)
    sink_rows = jnp.repeat(sinks.reshape(KV_HEADS, GRP), BLOCK, axis=1).reshape(KV_HEADS, 1, gq)
    eye = jnp.eye(BLOCK, dtype=BF)

    def body(q_ref, kp_ref, kc_ref, vp_ref, vc_ref, b_ref, bt_ref, sk_ref, skr_ref, eye_ref,
             qw_ref, kw_ref, eq_ref, eq3_ref, ek_ref, ek3_ref, o_ref, lse_ref):
        qn = _bf(_heads_norm(q_ref[...], qw_ref[...], eq_ref[...], eq3_ref[...])[0] * (HEAD_DIM ** -0.5))
        kn = _bf(_heads_norm(jnp.concatenate([kp_ref[...], kc_ref[...]], axis=0), kw_ref[...], ek_ref[...], ek3_ref[...])[0])
        vv = _bf(jnp.concatenate([vp_ref[...], vc_ref[...]], axis=0))
        ones = jnp.ones((2 * BLOCK, HEAD_DIM), BF)
        kss = [slice(hk * HEAD_DIM, (hk + 1) * HEAD_DIM) for hk in range(KV_HEADS)]
        qgs = [_stack_heads(qn, hk) for hk in range(KV_HEADS)]
        sc_ts = [_dot_nt(kn[:, kss[hk]], qgs[hk]) + bt_ref[0, hk] for hk in range(KV_HEADS)]
        m_rows = [jnp.maximum(jnp.max(sc_ts[hk], axis=0, keepdims=True), skr_ref[hk]) for hk in range(KV_HEADS)]
        m_hq = _bf(jnp.concatenate([(m + jnp.abs(m) * (2.0 ** -7))[:, g * BLOCK:(g + 1) * BLOCK]
                                    for m in m_rows for g in range(GRP)], axis=0))
        m16 = _dot_nt(eye_ref[...], m_hq)
        ms = [_stack_cols(m16, hk) for hk in range(KV_HEADS)]
        scs = [_dot_nt(qgs[hk], kn[:, kss[hk]]) + b_ref[0, hk * GRP:(hk + 1) * GRP].reshape(gq, 2 * BLOCK)
               for hk in range(KV_HEADS)]
        ps = [_bf(jnp.exp(scs[hk] - ms[hk])) for hk in range(KV_HEADS)]
        pvs = [_dot(ps[hk], jnp.concatenate([vv[:, kss[hk]], ones], axis=1)) for hk in range(KV_HEADS)]
        den16 = jnp.concatenate([pvs[hk][g * BLOCK:(g + 1) * BLOCK, HEAD_DIM:HEAD_DIM + 1]
                                 for hk in range(KV_HEADS) for g in range(GRP)], axis=1)
        den16 = den16 + jnp.exp(sk_ref[...] - m16)
        lse_ref[...] = m16 + jnp.log(den16)
        inv16 = 1.0 / den16
        for hk in range(KV_HEADS):
            for g in range(GRP):
                h = hk * GRP + g
                o_ref[:, h * HEAD_DIM:(h + 1) * HEAD_DIM] = (pvs[hk][g * BLOCK:(g + 1) * BLOCK, :HEAD_DIM]
                                                             * inv16[:, h:h + 1])

    cur = lambda w, col=0: pl.BlockSpec((BLOCK, w), lambda i: (i, col))
    prev = lambda w, col=0: pl.BlockSpec((BLOCK, w), lambda i: (jnp.maximum(i - 1, 0), col))
    whole = lambda a: pl.BlockSpec(a.shape, lambda i: (0,) * a.ndim)
    first_or_not = lambda a: pl.BlockSpec((1,) + a.shape[1:], lambda i: (jnp.minimum(i, 1),) + (0,) * (a.ndim - 1))
    return pl.pallas_call(
        body, name="attn_fwd", grid=(nb,),
        in_specs=[cur(ATTN_W), prev(KV_W, 0), cur(KV_W, 0), prev(KV_W, 1), cur(KV_W, 1),
                  first_or_not(bias), first_or_not(bias_t),
                  whole(sinks), whole(sink_rows), whole(eye)] + [_full(c.shape) for c in consts],
        out_specs=[cur(ATTN_W), cur(ATTN_HEADS)],
        out_shape=[jax.ShapeDtypeStruct((s, ATTN_W), F32), jax.ShapeDtypeStruct((s, ATTN_HEADS), F32)],
        compiler_params=_params(dimension_semantics=("arbitrary",)),
    )(q, kv, kv, kv, kv, bias, bias_t, sinks, sink_rows, eye, *consts)


def _conv_taps(xbc, tail):
    ext = jnp.concatenate([tail, xbc], axis=0)
    return [pltpu.roll(ext, CONV_K - 1 - j, axis=0)[8:8 + BLOCK] if j < CONV_K - 1 else xbc for j in range(CONV_K)]


def _softplus(u):
    return jnp.maximum(u, 0.0) + jnp.log(1.0 + jnp.exp(-jnp.abs(u)))


def _tril():
    r = lax.broadcasted_iota(jnp.int32, (BLOCK, BLOCK), 0)
    c = lax.broadcasted_iota(jnp.int32, (BLOCK, BLOCK), 1)
    return r >= c


def _triu():
    r = lax.broadcasted_iota(jnp.int32, (BLOCK, BLOCK), 0)
    c = lax.broadcasted_iota(jnp.int32, (BLOCK, BLOCK), 1)
    return r <= c


def _exact_left(m01, a):
    hi = _bf(a)
    r1 = a - hi.astype(F32)
    mid = _bf(r1)
    return _dot(m01, hi) + _dot(m01, mid) + _dot(m01, _bf(r1 - mid.astype(F32)))


def _ssd_common(conv, dtr, dtb_ref, alog_ref, e3_ref):
    sg = _sig(conv)
    xact = conv * sg
    u = dtr + dtb_ref[...]
    dt = _softplus(u)
    a = -jnp.exp(alog_ref[...])
    trilb = _tril()
    acum = _exact_left(trilb.astype(BF), dt * a) * math.log2(math.e)
    both = _group_bcast(jnp.concatenate([dt, acum], axis=0), e3_ref[...])
    dt_x, acum_x = both[:BLOCK], both[BLOCK:]
    return sg, xact, u, dt, a, trilb, acum, dt_x, acum_x


SSD_CH = 2


def _ssd_fwd(xbc, dt_raw, conv_w, conv_b, dt_bias, a_log, dsk_x, e3t):
    s = xbc.shape[0]
    nc = s // BLOCK
    ch = SSD_CH if nc % SSD_CH == 0 else 1
    rows = ch * BLOCK

    def body(x_ref, tail_ref, dtr_ref, cw_ref, cb_ref, dtb_ref, alog_ref, dsk_ref, e3_ref,
             y_ref, hp_ref, conv_ref, hst, yd_s, yoff_s):
        i = pl.program_id(0)

        @pl.when(i == 0)
        def _():
            hst[...] = jnp.zeros_like(hst)

        for j in range(ch):
            rs = slice(j * BLOCK, (j + 1) * BLOCK)
            tail = jnp.where(i > 0, tail_ref[...], 0.0) if j == 0 else x_ref[j * BLOCK - 8:j * BLOCK, :]
            taps = _conv_taps(x_ref[rs, :], tail)
            conv = cb_ref[...] + sum(taps[t] * cw_ref[t:t + 1, :] for t in range(CONV_K))
            conv_ref[rs, :] = conv
            _, xact, _, _, _, trilb, acum, dt_x, acum_x = _ssd_common(conv, dtr_ref[rs, :], dtb_ref, alog_ref, e3_ref)
            xs = xact[:, :SSM_W]
            acum_t = acum.T
            ea_x = jnp.exp2(acum_x)
            last_x = acum_x[BLOCK - 1:BLOCK, :]
            xdt = xs * dt_x
            xw = xdt * jnp.exp2(last_x - acum_x)
            cd_x = jnp.exp2(last_x)
            hprev = hst[...]
            hp_ref[j] = hprev
            sls = [slice(g * SSM_R * SSM_P, (g + 1) * SSM_R * SSM_P) for g in range(SSM_G)]
            bgs = [_bf(xact[:, SSM_W + g * SSM_N:SSM_W + (g + 1) * SSM_N]) for g in range(SSM_G)]
            cgs = [_bf(xact[:, SSM_W + SSM_G * SSM_N + g * SSM_N:SSM_W + SSM_G * SSM_N + (g + 1) * SSM_N])
                   for g in range(SSM_G)]
            xdt_b, xw_b, hprev_b = _bf(xdt), _bf(xw), _bf(hprev)
            low_half = lax.broadcasted_iota(jnp.int32, (BLOCK, 2 * SSM_P), 1) < SSM_P
            cbs = [_dot_nt(cgs[g], bgs[g]) for g in range(SSM_G)]
            for g in range(SSM_G):
                sl = sls[g]
                yoff_s[:, sl] = _dot(cgs[g], hprev_b[:, sl]) * ea_x[:, sl]
                hst[:, sl] = hprev[:, sl] * cd_x[:, sl] + _dot_tn(bgs[g], xw_b[:, sl])
            for g in range(SSM_G):
                hss = [slice((g * SSM_R + r) * SSM_P, (g * SSM_R + r + 1) * SSM_P) for r in range(SSM_R)]
                mms = [_bf(cbs[g] * jnp.exp2(jnp.where(trilb, acum[:, g * SSM_R + r:g * SSM_R + r + 1]
                                                      - acum_t[g * SSM_R + r:g * SSM_R + r + 1, :], -1e30)))
                       for r in range(SSM_R)]
                for r in range(0, SSM_R, 2):
                    pair = slice(hss[r].start, hss[r + 1].stop)
                    xp = xdt_b[:, pair]
                    rhs = jnp.concatenate([jnp.where(low_half, xp, 0), jnp.where(low_half, 0, xp)], axis=0)
                    yd_s[:, pair] = _dot(jnp.concatenate([mms[r], mms[r + 1]], axis=1), rhs)
            y_ref[rs, :] = yd_s[...] + yoff_s[...] + dsk_ref[...] * xs

    blk = lambda w: pl.BlockSpec((rows, w), lambda i: (i, 0))
    return pl.pallas_call(
        body, name="ssd_fwd", grid=(nc // ch,),
        in_specs=[blk(XBC_W), pl.BlockSpec((8, XBC_W), lambda i: (jnp.maximum(i * (rows // 8) - 1, 0), 0)),
                  blk(SSM_HEADS), _full((CONV_K, XBC_W)), _full((1, XBC_W)), _full((1, SSM_HEADS)),
                  _full((1, SSM_HEADS)), _full((1, SSM_W)), _full((3 * SSM_HEADS, SSM_W))],
        out_specs=[blk(SSM_W), pl.BlockSpec((ch, SSM_N, SSM_W), lambda i: (i, 0, 0)), blk(XBC_W)],
        out_shape=[jax.ShapeDtypeStruct((s, SSM_W), F32), jax.ShapeDtypeStruct((nc, SSM_N, SSM_W), F32),
                   jax.ShapeDtypeStruct((s, XBC_W), F32)],
        scratch_shapes=[pltpu.VMEM((SSM_N, SSM_W), F32), pltpu.VMEM((BLOCK, SSM_W), F32), pltpu.VMEM((BLOCK, SSM_W), F32)],
        compiler_params=_params(dimension_semantics=("arbitrary",)),
    )(xbc, xbc, dt_raw, conv_w, conv_b, dt_bias, a_log, dsk_x, e3t)


def _dsilu(z, sg, silu):
    return sg * (1.0 + (z - silu))


def _mid(x, tgt, o_att, zam, ypre, gab, gate, ssm_nw, rows_all, tm=256):
    s = x.shape[0]
    gw = SSM_W // SSM_G

    r_ap, r_sp = ATTN_W // N_DEV, SSM_W // N_DEV

    def body(x_ref, t_ref, o_ref, zam_ref, yp_ref, gab_ref, gate_ref, nw_ref, rows_h,
             dout_ref, do_ref, dzam_ref, dyp_ref, dgab_ref,
             yag_ref, dya_ref, yn_ref, dyb_ref, mg_ref, dob_ref, gnw_ref, dgate_ref, loss_ref,
             wap_v, wsp_v, wout_v, sem):
        i = pl.program_id(0)

        @pl.when(i == 0)
        def _():
            cps = []
            for d in range(N_DEV):
                for j, (dst, r0, rn) in enumerate(((wap_v, 0, r_ap), (wsp_v, r_ap, r_sp), (wout_v, r_ap + r_sp, r_ap))):
                    cps.append(pltpu.make_async_copy(rows_h.at[d, r0:r0 + rn, :], dst.at[d * rn:(d + 1) * rn, :], sem.at[j]))
            for cp in cps:
                cp.start()
            gnw_ref[...] = jnp.zeros_like(gnw_ref)
            dgate_ref[...] = jnp.zeros_like(dgate_ref)
            loss_ref[...] = jnp.zeros_like(loss_ref)
            for cp in cps:
                cp.wait()

        gate = gate_ref[...]
        nw = nw_ref[...]
        o_att = o_ref[...]
        z_a = zam_ref[:, :ATTN_W].astype(F32)
        s_a = _sig(z_a)
        silu_a = z_a * s_a
        yag = _bf(o_att * silu_a)
        yag_ref[...] = yag
        ypre = yp_ref[...]
        z_m = zam_ref[:, ATTN_W:].astype(F32)
        s_m = _sig(z_m)
        silu_m = z_m * s_m
        yg = ypre * silu_m
        rinv = jnp.concatenate(
            [jnp.broadcast_to(lax.rsqrt(jnp.mean(yg[:, g * gw:(g + 1) * gw] ** 2, axis=-1, keepdims=True) + EPS), (tm, gw))
             for g in range(SSM_G)], axis=1)
        ynr = yg * rinv
        yn = _bf(ynr * nw)
        yn_ref[...] = yn
        y_a = _dot(yag, wap_v[...])
        y_b = _dot(yn, wsp_v[...])
        g_a = _sig(gab_ref[:, :D_MODEL].astype(F32))
        g_b = _sig(gab_ref[:, D_MODEL:].astype(F32))
        merged = _bf(g_a * y_a + g_b * y_b)
        mg_ref[...] = merged
        o = _dot(merged, wout_v[...])
        diff = x_ref[...] + gate * o - t_ref[...]
        loss_ref[...] += (0.5 / D_MODEL) * jnp.sum(diff * diff, axis=(0, 1), keepdims=True)
        dout = diff * (1.0 / D_MODEL)
        dout_ref[...] = dout
        dgate_ref[...] += jnp.sum(dout * o, axis=0, keepdims=True)
        d_o = _bf(dout * gate)
        dob_ref[...] = d_o
        dmerged = _dot_nt(d_o, wout_v[...])
        dy_af = dmerged * g_a
        dy_bf = dmerged * g_b
        dy_a = _bf(dy_af)
        dy_b = _bf(dy_bf)
        dya_ref[...] = dy_a
        dyb_ref[...] = dy_b
        dyag = _dot_nt(dy_a, wap_v[...])
        dyn = _dot_nt(dy_b, wsp_v[...])
        dgab_ref[:, :D_MODEL] = _bf(dy_af * y_a * (1.0 - g_a))
        dgab_ref[:, D_MODEL:] = _bf(dy_bf * y_b * (1.0 - g_b))
        do_ref[...] = dyag * silu_a
        dzam_ref[:, :ATTN_W] = _bf(dyag * o_att * _dsilu(z_a, s_a, silu_a))
        gnw_ref[...] += jnp.sum(dyn * ynr, axis=0, keepdims=True)
        dynw = dyn * nw
        corr = jnp.concatenate(
            [jnp.broadcast_to(jnp.mean((dynw * ynr)[:, g * gw:(g + 1) * gw], axis=-1, keepdims=True), (tm, gw))
             for g in range(SSM_G)], axis=1)
        dyg = rinv * (dynw - ynr * corr)
        dyp_ref[...] = dyg * silu_m
        dzam_ref[:, ATTN_W:] = _bf(dyg * ypre * _dsilu(z_m, s_m, silu_m))

    r1, r2, r3 = _rows(tm, D_MODEL), _rows(tm, SSM_W), _rows(tm, ATTN_W + SSM_W)
    sd = jax.ShapeDtypeStruct
    return pl.pallas_call(
        body, name="mid", grid=(s // tm,),
        in_specs=[r1, r1, r1, r3, r2, r2, _full((1, D_MODEL)), _full((1, SSM_W)), ANY],
        out_specs=[r1, r1, r3, r2, r2, r1, r1, r2, r1, r1, r1,
                   _full((1, SSM_W)), _full((1, D_MODEL)), _full((1, 1))],
        out_shape=[sd((s, D_MODEL), F32), sd((s, ATTN_W), F32), sd((s, ATTN_W + SSM_W), BF), sd((s, SSM_W), F32),
                   sd((s, 2 * D_MODEL), BF),
                   sd((s, ATTN_W), BF), sd((s, D_MODEL), BF), sd((s, SSM_W), BF), sd((s, D_MODEL), BF),
                   sd((s, D_MODEL), BF), sd((s, D_MODEL), BF),
                   sd((1, SSM_W), F32), sd((1, D_MODEL), F32), sd((1, 1), F32)],
        scratch_shapes=[pltpu.VMEM((ATTN_W, D_MODEL), BF), pltpu.VMEM((SSM_W, D_MODEL), BF), pltpu.VMEM((D_MODEL, D_MODEL), BF),
                        pltpu.SemaphoreType.DMA((3,))],
        compiler_params=_params(dimension_semantics=("arbitrary",)),
    )(x, tgt, o_att, zam, ypre, gab, gate, ssm_nw, rows_all)


def _attn_bwd(q, kv, bias, sinks, consts, o_att, lse, d_o):
    s = q.shape[0]
    nb = s // BLOCK
    folds = (_fold(ATTN_W, HEAD_DIM), _fold(KV_W, HEAD_DIM))

    def body(q_ref, kp_ref, kc_ref, vp_ref, vc_ref, b_ref, skv_ref, qw_ref, kw_ref, eq_ref, eq3_ref, ek_ref, ek3_ref,
             fq_ref, fk_ref, o_ref, lse_ref, do_ref,
             dq_ref, dkv_ref, dss_ref, gqw_ref, gkw_ref, gsk_ref, ckn, cv, dqn_s, dkn_s, dv_s, gq_x, gk_x):
        i = pl.program_id(0)
        kw, ek, ek3 = kw_ref[...], ek_ref[...], ek3_ref[...]

        @pl.when(i == 0)
        def _():
            for ref in (ckn, cv, dss_ref, gq_x, gk_x, gsk_ref):
                ref[...] = jnp.zeros_like(ref)

        @pl.when(i < nb)
        def _():
            qw, eq, eq3 = qw_ref[...], eq_ref[...], eq3_ref[...]
            qf = q_ref[...]
            qnf, rq_x = _heads_norm(qf, qw, eq, eq3)
            qn = _bf(qnf * (HEAD_DIM ** -0.5))
            kf = jnp.concatenate([kp_ref[...], kc_ref[...]], axis=0)
            knf, rk_x = _heads_norm(kf, kw, ek, ek3)
            kn = _bf(knf)
            vv = _bf(jnp.concatenate([vp_ref[...], vc_ref[...]], axis=0))
            d_of = do_ref[...]
            d_ob = _bf(d_of)
            lse_all = lse_ref[...]
            delta = _dot(_bf(d_of * o_ref[...]), eq)
            gsk_ref[...] += jnp.sum(-jnp.exp(skv_ref[...] - lse_all) * delta, axis=0, keepdims=True)
            kss = [slice(hk * HEAD_DIM, (hk + 1) * HEAD_DIM) for hk in range(KV_HEADS)]
            qgs = [_stack_heads(qn, hk) for hk in range(KV_HEADS)]
            d_ogs = [_stack_heads(d_ob, hk) for hk in range(KV_HEADS)]
            scs = [_dot_nt(qgs[hk], kn[:, kss[hk]]) + b_ref[0, hk * GRP:(hk + 1) * GRP].reshape(GRP * BLOCK, 2 * BLOCK)
                   for hk in range(KV_HEADS)]
            dps = [_dot_nt(d_ogs[hk], vv[:, kss[hk]]) for hk in range(KV_HEADS)]
            ps = [jnp.exp(scs[hk] - _stack_cols(lse_all, hk)) for hk in range(KV_HEADS)]
            dss = [ps[hk] * (dps[hk] - _stack_cols(delta, hk)) for hk in range(KV_HEADS)]
            pbs = [_bf(p) for p in ps]
            dsbs = [_bf(ds) for ds in dss]
            for hk in range(KV_HEADS):
                dss_ref[hk * GRP:(hk + 1) * GRP] += dss[hk].reshape(GRP, BLOCK, 2 * BLOCK)
            for hk in range(KV_HEADS):
                dv_s[:, kss[hk]] = _dot_tn(pbs[hk], d_ogs[hk])
                dkn_s[:, kss[hk]] = _dot_tn(dsbs[hk], qgs[hk])
            dqns = [_dot(dsbs[hk], kn[:, kss[hk]]) * (HEAD_DIM ** -0.5) for hk in range(KV_HEADS)]
            for hk in range(KV_HEADS):
                for g in range(GRP):
                    h = hk * GRP + g
                    dqn_s[:, h * HEAD_DIM:(h + 1) * HEAD_DIM] = dqns[hk][g * BLOCK:(g + 1) * BLOCK]
            dq, gq = _heads_norm_bwd(qf, rq_x, qw, dqn_s[...], eq, eq3)
            dq_ref[...] = _bf(dq)
            gq_x[...] += gq
            dk, gk = _heads_norm_bwd(kf[:BLOCK], rk_x[:BLOCK], kw, ckn[...] + dkn_s[0:BLOCK, :], ek, ek3)
            dkv_ref[:, :KV_W] = _bf(dk)
            gk_x[...] += gk
            dkv_ref[:, KV_W:] = _bf(cv[...] + dv_s[0:BLOCK, :])
            ckn[...] = dkn_s[BLOCK:2 * BLOCK, :]
            cv[...] = dv_s[BLOCK:2 * BLOCK, :]

        @pl.when(i == nb)
        def _():
            kc = kc_ref[...]
            dk, gk = _heads_norm_bwd(kc, _heads_norm(kc, kw, ek, ek3)[1], kw, ckn[...], ek, ek3)
            dkv_ref[:, :KV_W] = _bf(dk)
            dkv_ref[:, KV_W:] = _bf(cv[...])
            gqw_ref[...] = _group_sum(jnp.broadcast_to(gq_x[...], (8, ATTN_W)), fq_ref[...])[0:1]
            gkw_ref[...] = _group_sum(jnp.broadcast_to(gk_x[...] + gk, (8, KV_W)), fk_ref[...])[0:1]

    last = nb - 1
    cur = lambda w, col=0: pl.BlockSpec((BLOCK, w), lambda i: (jnp.minimum(i, last), col))
    prev = lambda w, col=0: pl.BlockSpec((BLOCK, w), lambda i: (jnp.maximum(jnp.minimum(i, last) - 1, 0), col))
    late = lambda w: pl.BlockSpec((BLOCK, w), lambda i: (jnp.maximum(i - 1, 0), 0))
    sd = jax.ShapeDtypeStruct
    return pl.pallas_call(
        body, name="attn_bwd", grid=(nb + 1,),
        in_specs=[cur(ATTN_W), prev(KV_W, 0), cur(KV_W, 0), prev(KV_W, 1), cur(KV_W, 1),
                  pl.BlockSpec((1, ATTN_HEADS, BLOCK, 2 * BLOCK), lambda i: (jnp.minimum(i, 1), 0, 0, 0)),
                  _full((1, ATTN_HEADS))]
                 + [_full(c.shape) for c in consts + folds] + [cur(ATTN_W), cur(ATTN_HEADS), cur(ATTN_W)],
        out_specs=[cur(ATTN_W), late(2 * KV_W),
                   pl.BlockSpec((ATTN_HEADS, BLOCK, 2 * BLOCK), lambda i: (0, 0, 0)),
                   _full((1, HEAD_DIM)), _full((1, HEAD_DIM)), _full((1, ATTN_HEADS))],
        out_shape=[sd((s, ATTN_W), BF), sd((s, 2 * KV_W), BF),
                   sd((ATTN_HEADS, BLOCK, 2 * BLOCK), F32), sd((1, HEAD_DIM), F32), sd((1, HEAD_DIM), F32),
                   sd((1, ATTN_HEADS), F32)],
        scratch_shapes=[pltpu.VMEM((BLOCK, KV_W), F32), pltpu.VMEM((BLOCK, KV_W), F32),
                        pltpu.VMEM((BLOCK, ATTN_W), F32), pltpu.VMEM((2 * BLOCK, KV_W), F32),
                        pltpu.VMEM((2 * BLOCK, KV_W), F32), pltpu.VMEM((1, ATTN_W), F32), pltpu.VMEM((1, KV_W), F32)],
        compiler_params=_params(dimension_semantics=("arbitrary",)),
    )(q, kv, kv, kv, kv, bias, sinks, *consts, *folds, o_att, lse, d_o)


def _ssd_bwd(xbc, conv_all, dt_raw, conv_w, dt_bias, a_log, dsk_x, e_mat, e3t, hprev_all, dy_all):
    s = xbc.shape[0]
    nc = s // BLOCK
    ch = 1
    rows = ch * BLOCK
    nsteps = nc // ch
    gw = SSM_R * SSM_P
    b0, c0 = SSM_W, SSM_W + SSM_G * SSM_N

    def body(x_ref, conv_ref, dtr_ref, cw_ref, dtb_ref, alog_ref, dsk_ref, e_ref, e3_ref, hp_ref, dy_ref,
             dx_ref, ddt_ref, gcw_ref, gcb_ref, gdtb_ref, galog_ref, gdsk_ref,
             dh, nhead, gdskx, dxdt_s, dbc_s, dxd_s):
        def chunk_bwd(j):
            rs = slice(j * BLOCK, (j + 1) * BLOCK)
            conv = conv_ref[rs, :]
            sg, xact, u, dt, a, trilb, acum, dt_x, acum_x = _ssd_common(conv, dtr_ref[rs, :], dtb_ref, alog_ref, e3_ref)
            xs = xact[:, :SSM_W]
            acum_t = acum.T
            ea_x = jnp.exp2(acum_x)
            last_x = acum_x[BLOCK - 1:BLOCK, :]
            dte_x = jnp.exp2(last_x - acum_x)
            cd_x = jnp.exp2(last_x)
            xdt = xs * dt_x
            xw = xdt * dte_x
            hprev = hp_ref[j]
            dhn = dh[...]
            dy = dy_ref[rs, :]
            gdskx[...] += jnp.sum(dy * xs, axis=0, keepdims=True)
            dyea = dy * ea_x
            lane = lax.broadcasted_iota(jnp.int32, (BLOCK, SSM_HEADS), 1)
            dacum = jnp.zeros((BLOCK, SSM_HEADS), F32)
            dacc_x, dlast_x = [], []
            sls = [slice(g * gw, (g + 1) * gw) for g in range(SSM_G)]
            bgs = [_bf(xact[:, b0 + g * SSM_N:b0 + (g + 1) * SSM_N]) for g in range(SSM_G)]
            cgs = [_bf(xact[:, c0 + g * SSM_N:c0 + (g + 1) * SSM_N]) for g in range(SSM_G)]
            hpgs = [_bf(hprev[:, sl]) for sl in sls]
            dhgs = [_bf(dhn[:, sl]) for sl in sls]
            dyeags = [_bf(dyea[:, sl]) for sl in sls]
            xwgs = [_bf(xw[:, sl]) for sl in sls]
            xdt_b, dy_b = _bf(xdt), _bf(dy)
            low_half = lax.broadcasted_iota(jnp.int32, (BLOCK, 2 * SSM_P), 1) < SSM_P
            cbs = [_dot_nt(cgs[g], bgs[g]) for g in range(SSM_G)]
            gmats = [_dot(cgs[g], hpgs[g]) for g in range(SSM_G)]
            dxws = [_dot(bgs[g], dhgs[g]) for g in range(SSM_G)]
            dcgs = [_dot_nt(dyeags[g], hpgs[g]) for g in range(SSM_G)]
            dbgs = [_dot_nt(xwgs[g], dhgs[g]) for g in range(SSM_G)]
            for g in range(SSM_G):
                sl = sls[g]
                dh[:, sl] = dhn[:, sl] * cd_x[:, sl] + _dot_tn(cgs[g], dyeags[g])
                dxdt_s[:, sl] = dxws[g] * dte_x[:, sl]
                dacc_x.append(dy[:, sl] * gmats[g] * ea_x[:, sl] - dxws[g] * xw[:, sl])
                dlast_x.append(jnp.sum(dxws[g] * xw[:, sl], axis=0, keepdims=True)
                               + jnp.sum(dhn[:, sl] * hprev[:, sl], axis=0, keepdims=True) * cd_x[:, sl])
            for g in range(SSM_G):
                bg, cg, cb, dbg, dcg = bgs[g], cgs[g], cbs[g], dbgs[g], dcgs[g]
                hss = [slice((g * SSM_R + r) * SSM_P, (g * SSM_R + r + 1) * SSM_P) for r in range(SSM_R)]
                lms = [jnp.exp2(jnp.where(trilb, acum[:, g * SSM_R + r:g * SSM_R + r + 1]
                                         - acum_t[g * SSM_R + r:g * SSM_R + r + 1, :], -1e30)) for r in range(SSM_R)]
                mms = [cb * lm for lm in lms]
                mmbs = [_bf(mm) for mm in mms]
                dms = []
                for r in range(0, SSM_R, 2):
                    pair = slice(hss[r].start, hss[r + 1].stop)
                    xp, dyp = xdt_b[:, pair], dy_b[:, pair]
                    dmp = _dot_nt(dyp, jnp.concatenate([jnp.where(low_half, xp, 0), jnp.where(low_half, 0, xp)], axis=0))
                    dms += [dmp[:, :BLOCK], dmp[:, BLOCK:]]
                    dxd_s[:, pair] = _dot_tn(jnp.concatenate([mmbs[r], mmbs[r + 1]], axis=0),
                                             jnp.concatenate([jnp.where(low_half, dyp, 0), jnp.where(low_half, 0, dyp)], axis=0))
                dcb = sum(dms[r] * lms[r] for r in range(SSM_R))
                wms = [dms[r] * mms[r] for r in range(SSM_R)]
                antis = [_bf(wm - wm.T) for wm in wms]
                for r in range(SSM_R):
                    dacum = dacum + _dot(antis[r], (lane == g * SSM_R + r).astype(BF))
                dcbb = _bf(dcb)
                dbc_s[:, g * SSM_N:(g + 1) * SSM_N] = dbg + _dot_tn(dcbb, cg)
                dbc_s[:, SSM_G * SSM_N + g * SSM_N:SSM_G * SSM_N + (g + 1) * SSM_N] = dcg + _dot(dcbb, bg)
            dxdt = dxdt_s[...] + dxd_s[...]
            dxs = dy * dsk_ref[...] + dxdt * dt_x
            red = _group_sum(jnp.concatenate(
                [dxdt * xs, jnp.concatenate(dacc_x, axis=1),
                 jnp.broadcast_to(jnp.concatenate(dlast_x, axis=1), (8, SSM_W))], axis=0), e_ref[...])
            row = lax.broadcasted_iota(jnp.int32, (BLOCK, SSM_HEADS), 0)
            dacum = dacum + red[BLOCK:2 * BLOCK] + jnp.where(row == BLOCK - 1, red[2 * BLOCK:2 * BLOCK + 1], 0.0)
            ddta = _exact_left(_triu().astype(BF), dacum)
            ddt = red[:BLOCK] + ddta * a
            galog_ref[...] += jnp.sum(ddta * dt, axis=0, keepdims=True) * a
            du = ddt * _sig(u)
            ddt_ref[rs, :] = _bf(du)
            gdtb_ref[...] += jnp.sum(du, axis=0, keepdims=True)
            dconv = jnp.concatenate([dxs, dbc_s[...]], axis=1) * _dsilu(conv, sg, xact)
            gcb_ref[...] += jnp.sum(dconv, axis=0, keepdims=True)
            ext2 = jnp.concatenate([dconv, nhead[...]], axis=0)
            ahead = [pltpu.roll(ext2, BLOCK + 8 - (CONV_K - 1 - j), axis=0)[0:BLOCK] if j < CONV_K - 1 else dconv
                     for j in range(CONV_K)]
            dx_ref[rs, :] = _bf(sum(ahead[j] * cw_ref[j:j + 1, :] for j in range(CONV_K)))
            xraw = x_ref[rs, :]
            gcw_ref[...] += jnp.concatenate([jnp.sum(ahead[j] * xraw, axis=0, keepdims=True) for j in range(CONV_K)], axis=0)
            nhead[...] = dconv[0:8]

        i = pl.program_id(0)

        @pl.when(i == 0)
        def _():
            for ref in (dh, nhead, gdskx, gcw_ref, gcb_ref, gdtb_ref, galog_ref, gdsk_ref):
                ref[...] = jnp.zeros_like(ref)

        for j in reversed(range(ch)):
            chunk_bwd(j)

        @pl.when(i == nsteps - 1)
        def _():
            gdsk_ref[...] = _group_sum(jnp.broadcast_to(gdskx[...], (8, SSM_W)), e_ref[...])[0:1]

    chunk = lambda w: pl.BlockSpec((rows, w), lambda i: (nsteps - 1 - i, 0))
    sd = jax.ShapeDtypeStruct
    return pl.pallas_call(
        body, name="ssd_bwd", grid=(nsteps,),
        in_specs=[chunk(XBC_W), chunk(XBC_W),
                  chunk(SSM_HEADS), _full((CONV_K, XBC_W)), _full((1, SSM_HEADS)),
                  _full((1, SSM_HEADS)), _full((1, SSM_W)), _full((SSM_W, SSM_HEADS)), _full((3 * SSM_HEADS, SSM_W)),
                  pl.BlockSpec((ch, SSM_N, SSM_W), lambda i: (nsteps - 1 - i, 0, 0)), chunk(SSM_W)],
        out_specs=[chunk(XBC_W), chunk(SSM_HEADS), _full((CONV_K, XBC_W)), _full((1, XBC_W)),
                   _full((1, SSM_HEADS)), _full((1, SSM_HEADS)), _full((1, SSM_HEADS))],
        out_shape=[sd((s, XBC_W), BF), sd((s, SSM_HEADS), BF), sd((CONV_K, XBC_W), F32), sd((1, XBC_W), F32),
                   sd((1, SSM_HEADS), F32), sd((1, SSM_HEADS), F32), sd((1, SSM_HEADS), F32)],
        scratch_shapes=[pltpu.VMEM((SSM_N, SSM_W), F32), pltpu.VMEM((8, XBC_W), F32),
                        pltpu.VMEM((1, SSM_W), F32), pltpu.VMEM((BLOCK, SSM_W), F32),
                        pltpu.VMEM((BLOCK, 2 * SSM_G * SSM_N), F32), pltpu.VMEM((BLOCK, SSM_W), F32)],
        compiler_params=_params(dimension_semantics=("arbitrary",)),
    )(xbc, conv_all, dt_raw, conv_w, dt_bias, a_log, dsk_x, e_mat, e3t, hprev_all, dy_all)


def _dh(x, dout, norm_w, scale, dsegs, w_t, tm=256):
    s = x.shape[0]

    def body(x_ref, dout_ref, nw_ref, sc_ref, *rest):
        d_refs, w_hbm = rest[:NSEG], rest[NSEG]
        gx_ref, dshift_ref, dscale_ref, gnw_ref = rest[NSEG + 1:NSEG + 5]
        w_vm, sem = rest[NSEG + 5], rest[NSEG + 6]
        first = pl.program_id(0) == 0
        cps = [pltpu.make_async_copy(w_hbm.at[SEG_OFF[j]:SEG_OFF[j + 1], :], w_vm.at[SEG_OFF[j]:SEG_OFF[j + 1], :], sem.at[j])
               for j in range(NSEG)]

        def tile(waiting):
            dh = None
            for j in range(NSEG):
                if waiting:
                    cps[j].wait()
                part = _dot(d_refs[j][...], w_vm[SEG_OFF[j]:SEG_OFF[j + 1], :])
                dh = part if dh is None else dh + part
            xv = x_ref[...]
            r = lax.rsqrt(jnp.mean(xv * xv, axis=-1, keepdims=True) + EPS)
            xn = xv * r
            nw = nw_ref[...]
            sc1 = 1.0 + sc_ref[...]
            dshift_ref[...] += jnp.sum(dh, axis=0, keepdims=True)
            dhxn = jnp.sum(dh * xn, axis=0, keepdims=True)
            dscale_ref[...] += dhxn * nw
            gnw_ref[...] += dhxn * sc1
            dxn = dh * (nw * sc1)
            gx_ref[...] = dout_ref[...] + r * (dxn - xn * jnp.mean(xn * dxn, axis=-1, keepdims=True))

        @pl.when(first)
        def _():
            for cp in cps:
                cp.start()
            for ref in (dshift_ref, dscale_ref, gnw_ref):
                ref[...] = jnp.zeros_like(ref)
            tile(True)

        @pl.when(jnp.logical_not(first))
        def _():
            tile(False)

    vec = _full((1, D_MODEL))
    sd = jax.ShapeDtypeStruct
    return pl.pallas_call(
        body, name="dh", grid=(s // tm,),
        in_specs=[_rows(tm, D_MODEL), _rows(tm, D_MODEL), vec, vec] + [_rows(tm, w) for w in SEG_W] + [ANY],
        out_specs=[_rows(tm, D_MODEL), vec, vec, vec],
        out_shape=[sd((s, D_MODEL), F32), sd((1, D_MODEL), F32), sd((1, D_MODEL), F32), sd((1, D_MODEL), F32)],
        scratch_shapes=[pltpu.VMEM((IN_W, D_MODEL), BF), pltpu.SemaphoreType.DMA((NSEG,))],
        compiler_params=_params(dimension_semantics=("arbitrary",)),
    )(x, dout, norm_w, scale, *dsegs, w_t)


def _gw_seg(h, dseg, name, tm=1024):
    s, w = dseg.shape
    tn = min(w, 1024)
    tm = min(tm, s)
    nm = s // tm

    def body(h_ref, d_ref, o_ref, acc):
        m = pl.program_id(1)

        @pl.when(m == 0)
        def _():
            acc[...] = jnp.zeros_like(acc)

        acc[...] += _dot_tn(d_ref[...], h_ref[...])

        @pl.when(m == nm - 1)
        def _():
            o_ref[...] = _bf(acc[...])

    return pl.pallas_call(
        body, name=name, grid=(w // tn, nm),
        in_specs=[pl.BlockSpec((tm, D_MODEL), lambda n, m: (m, 0)), pl.BlockSpec((tm, tn), lambda n, m: (m, n))],
        out_specs=pl.BlockSpec((tn, D_MODEL), lambda n, m: (n, 0)),
        out_shape=jax.ShapeDtypeStruct((w, D_MODEL), BF),
        scratch_shapes=[pltpu.VMEM((tn, D_MODEL), F32)],
        compiler_params=_params(dimension_semantics=("arbitrary", "arbitrary")),
    )(h, dseg)


def _gw_in(h, dsegs):
    return [_gw_seg(h, d, "gw_in_%d" % j) for j, d in enumerate(dsegs)]


def _local_step(x, tgt, shift, scale, gate, w_t, rows_fn, norm_w, qnw, knw, rel_bias, sinks,
                conv_w, conv_b, dt_bias, a_log, d_skip, ssm_nw, after_mid=None, after_gw=None):
    oh_t = _bucket_onehot_t()
    bias = _masked_bias(_bias_dense(rel_bias.T, oh_t).reshape(ATTN_HEADS, BLOCK, 2 * BLOCK))
    *segs, h = _inproj(x, norm_w, scale, shift, w_t)
    q, kv, zam, xbc, dtr, gab = segs
    consts = _attn_consts(qnw, knw)
    o_att, lse = _attn_fwd(q, kv, bias, sinks, consts)
    e_mat, e3t = _membership(SSM_W, SSM_P, SSM_HEADS)
    dsk_x = jnp.repeat(d_skip, SSM_P, axis=1)
    ypre, hprev, conv = _ssd_fwd(xbc, dtr, conv_w, conv_b, dt_bias, a_log, dsk_x, e3t)
    (dout, d_o, dzam, dyp, dgab, yag, dy_a, yn, dy_b, merged, dob, g_ssm_nw, dgate, loss) = _mid(
        x, tgt, o_att, zam, ypre, gab, gate, ssm_nw, rows_fn(ypre))
    g_wap = _gw_seg(dy_a, yag, "gw_attn_proj")
    g_wsp = _gw_seg(dy_b, yn, "gw_ssm_proj")
    g_wout = _gw_seg(dob, merged, "gw_out")
    zero = after_mid(g_wap, g_wsp, g_wout) if after_mid is not None else 0.0
    dq, dkv, dss, g_qnw, g_knw, g_sinks = _attn_bwd(q, kv, bias, sinks + zero, consts, o_att, lse, d_o)
    g_rel = _bias_grad(dss.reshape(ATTN_HEADS, BLOCK * 2 * BLOCK), oh_t).T
    dxbc, ddt, g_cw, g_cb, g_dtb, g_alog, g_dsk = _ssd_bwd(
        xbc, conv, dtr, conv_w, dt_bias, a_log, dsk_x, e_mat, e3t, hprev, dyp)
    dsegs = (dq, dkv, dzam, dxbc, ddt, dgab)
    g_ws = _gw_in(h, dsegs)
    zero = after_gw(g_ws) if after_gw is not None else 0.0
    gx, dshift, dscale, g_nw = _dh(x, dout, norm_w + zero, scale, dsegs, w_t)
    return dict(loss=loss, grad_x=gx, dmod=jnp.concatenate([dshift, dscale, dgate], axis=1), g_ws=g_ws,
                g_wap=g_wap, g_wsp=g_wsp, g_wout=g_wout, g_norm_w=g_nw, g_qnw=g_qnw, g_knw=g_knw, g_rel=g_rel,
                g_sinks=g_sinks, g_conv_w=g_cw, g_conv_b=g_cb, g_dt_bias=g_dtb, g_a_log=g_alog, g_d_skip=g_dsk,
                g_ssm_nw=g_ssm_nw)


def _me():
    return lax.axis_index("x"), lax.axis_index("y"), lax.axis_index("c")


def _flip(v, bit):
    return 1 - v if bit else v


def _ag_direct(v, name):
    def body(v_ref, out_ref, send_sems, recv_sems, local_sem):
        x, y, c = _me()
        me = 4 * x + 2 * y + c
        mine = pltpu.make_async_copy(v_ref, out_ref.at[me], local_sem)
        mine.start()
        peers = [(_flip(x, k >> 2 & 1), _flip(y, k >> 1 & 1), _flip(c, k & 1)) for k in range(1, N_DEV)]
        sends = [pltpu.make_async_remote_copy(
            src_ref=v_ref, dst_ref=out_ref.at[me], send_sem=send_sems.at[j], recv_sem=recv_sems.at[j],
            device_id=p, device_id_type=MESH) for j, p in enumerate(peers)]
        for cp in sends:
            cp.start()
        for j, (px, py, pc) in enumerate(peers):
            pltpu.make_async_remote_copy(
                src_ref=v_ref, dst_ref=out_ref.at[4 * px + 2 * py + pc], send_sem=send_sems.at[j],
                recv_sem=recv_sems.at[j], device_id=(px, py, pc), device_id_type=MESH).wait_recv()
        for cp in sends:
            cp.wait_send()
        mine.wait()

    vm = pl.BlockSpec(memory_space=pltpu.VMEM)
    return pl.pallas_call(
        body, name=name, out_shape=jax.ShapeDtypeStruct((N_DEV,) + v.shape, v.dtype),
        in_specs=[vm], out_specs=vm,
        scratch_shapes=[pltpu.SemaphoreType.DMA((N_DEV - 1,)), pltpu.SemaphoreType.DMA((N_DEV - 1,)),
                        pltpu.SemaphoreType.DMA],
        compiler_params=_params(),
    )(v)


def _gather_mod(v, w_ada, b_piece):
    ncols = w_ada.shape[1]

    def body(v_ref, w_ref, b_ref, rows_ref, mods_ref, piece, send_sems, recv_sems, local_sems):
        x, y, c = _me()
        me = 4 * x + 2 * y + c
        peers = _peers(x, y, c)

        def exchange(src, dst, rnd):
            mine = pltpu.make_async_copy(src, dst.at[me], local_sems.at[rnd])
            mine.start()
            sends = [pltpu.make_async_remote_copy(
                src_ref=src, dst_ref=dst.at[me], send_sem=send_sems.at[rnd, j], recv_sem=recv_sems.at[rnd, j],
                device_id=p, device_id_type=MESH) for j, p in enumerate(peers)]
            for cp in sends:
                cp.start()
            for j, (px, py, pc) in enumerate(peers):
                pltpu.make_async_remote_copy(
                    src_ref=src, dst_ref=dst.at[4 * px + 2 * py + pc], send_sem=send_sems.at[rnd, j],
                    recv_sem=recv_sems.at[rnd, j], device_id=(px, py, pc), device_id_type=MESH).wait_recv()
            for cp in sends:
                cp.wait_send()
            mine.wait()

        exchange(v_ref, rows_ref, 0)
        c_all = rows_ref[:, 0, :D_MODEL]
        piece[...] = _dot(_bf(_silu(c_all)), _bf(w_ref[...])) + b_ref[...]
        exchange(piece, mods_ref, 1)

    vm = pl.BlockSpec(memory_space=pltpu.VMEM)
    return pl.pallas_call(
        body, name="gather_mod",
        out_shape=(jax.ShapeDtypeStruct((N_DEV,) + v.shape, F32), jax.ShapeDtypeStruct((N_DEV, N_DEV, ncols), F32)),
        in_specs=[vm, vm, vm], out_specs=(vm, vm),
        scratch_shapes=[pltpu.VMEM((N_DEV, ncols), F32), pltpu.SemaphoreType.DMA((2, N_DEV - 1)),
                        pltpu.SemaphoreType.DMA((2, N_DEV - 1)), pltpu.SemaphoreType.DMA((2,))],
        compiler_params=_params(),
    )(v, w_ada, b_piece)


def _ag_relayed(v, name, chunks=1):
    rows = v.shape[0] // chunks
    assert rows * chunks == v.shape[0] and rows % 8 == 0

    def body(v_ref, out_ref, token, send_sems, recv_sems, local_sem):
        token[...] = jnp.zeros_like(token)
        x, y, c = _me()
        flip_x, flip_y = 1 - x, 1 - y
        ax, ay = c * x + (1 - c) * flip_x, c * flip_y + (1 - c) * y
        bx, by = c * flip_x + (1 - c) * x, c * y + (1 - c) * flip_y
        me, sib = (x, y, c), (x, y, 1 - c)
        a, b, dg = (ax, ay, c), (bx, by, c), (flip_x, flip_y, c)
        sa, sb, sdg = (bx, by, 1 - c), (ax, ay, 1 - c), (flip_x, flip_y, 1 - c)

        def piece(ref, k):
            return ref.at[pl.ds(k * rows, rows), :]

        def slot(px, py, pc):
            return out_ref.at[4 * px + 2 * py + pc]

        def copy(n, k, block, to, src=None):
            return pltpu.make_async_remote_copy(
                src_ref=piece(slot(*block) if src is None else src, k), dst_ref=piece(slot(*block), k),
                send_sem=send_sems.at[n * chunks + k], recv_sem=recv_sems.at[n * chunks + k],
                device_id=to, device_id_type=MESH)

        mine = pltpu.make_async_copy(v_ref, slot(*me), local_sem)
        mine.start()
        started = [copy(n, k, me, to, src=v_ref) for k in range(chunks) for n, to in ((1, a), (2, b), (0, sib))]
        for cp in started:
            cp.start()

        def arrived(n, k, block, then):
            copy(n, k, block, me).wait_recv()
            for n2, to in then:
                started.append(copy(n2, k, block, to))
                started[-1].start()

        for k in range(chunks):
            arrived(1, k, a, ((3, b), (4, sib)))
            arrived(2, k, b, ((5, sib),))
        for k in range(chunks):
            arrived(3, k, dg, ((6, sib),))
        for k in range(chunks):
            for n, block in ((0, sib), (4, sa), (5, sb), (6, sdg)):
                copy(n, k, block, me).wait_recv()
        for cp in started:
            cp.wait_send()
        mine.wait()

    out, token = pl.pallas_call(
        body, name=name,
        out_shape=(jax.ShapeDtypeStruct((N_DEV,) + v.shape, v.dtype), jax.ShapeDtypeStruct((8, 128), v.dtype)),
        in_specs=[ANY], out_specs=(ANY, pl.BlockSpec(memory_space=pltpu.VMEM)),
        scratch_shapes=[pltpu.SemaphoreType.DMA((7 * chunks,)), pltpu.SemaphoreType.DMA((7 * chunks,)),
                        pltpu.SemaphoreType.DMA],
        compiler_params=_params(),
    )(v)
    return out, token[0:1, 0:1]


HBM = pl.BlockSpec(memory_space=pltpu.HBM)
SEM = pl.BlockSpec(memory_space=pltpu.SEMAPHORE)
EFFECT = pltpu.SideEffectType.DATAFLOW_SIDE_EFFECTING


def _peers(x, y, c):
    return [(_flip(x, k >> 2 & 1), _flip(y, k >> 1 & 1), _flip(c, k & 1)) for k in range(1, N_DEV)]


def _exchange_start(src, land, gather, name):
    def body(src_ref, land_ref, send_sems, recv_sems, src_thru, land_thru, token):
        x, y, c = _me()
        me = 4 * x + 2 * y + c
        for j, (px, py, pc) in enumerate(_peers(x, y, c)):
            pltpu.make_async_remote_copy(
                src_ref=src_ref if gather else src_ref.at[4 * px + 2 * py + pc], dst_ref=land_ref.at[me],
                send_sem=send_sems.at[j], recv_sem=recv_sems.at[j], device_id=(px, py, pc), device_id_type=MESH).start()
        token[...] = jnp.zeros_like(token)

    sems = pltpu.SemaphoreType.DMA((N_DEV - 1,))
    out = pl.pallas_call(
        body, name=name,
        out_shape=(sems, sems, pltpu.HBM(src.shape, src.dtype), pltpu.HBM(land.shape, land.dtype),
                   jax.ShapeDtypeStruct((8, 128), F32)),
        in_specs=(HBM, HBM), out_specs=(SEM, SEM, HBM, HBM, pl.BlockSpec(memory_space=pltpu.VMEM)),
        input_output_aliases={0: 2, 1: 3},
        compiler_params=pltpu.CompilerParams(has_side_effects=EFFECT),
    )(pltpu.with_memory_space_constraint(src, pltpu.HBM), pltpu.with_memory_space_constraint(land, pltpu.HBM))
    return out[:4], out[4][0, 0]


def _exchange_wait(started, after, gather, name):
    send_sems, recv_sems, src_thru, land_thru = started

    def body(src_ref, land_ref, send_sems, recv_sems, after_ref, src_dead, got_ref):
        x, y, c = _me()
        for j, (px, py, pc) in enumerate(_peers(x, y, c)):
            pid = 4 * px + 2 * py + pc
            cp = pltpu.make_async_remote_copy(
                src_ref=src_ref if gather else src_ref.at[pid], dst_ref=land_ref.at[pid],
                send_sem=send_sems.at[j], recv_sem=recv_sems.at[j], device_id=(px, py, pc), device_id_type=MESH)
            cp.wait_send()
            cp.wait_recv()

    return pl.pallas_call(
        body, name=name,
        out_shape=(pltpu.HBM(src_thru.shape, src_thru.dtype), pltpu.HBM(land_thru.shape, land_thru.dtype)),
        in_specs=(HBM, HBM, SEM, SEM, ANY), out_specs=(HBM, HBM), input_output_aliases={0: 0, 1: 1},
        compiler_params=pltpu.CompilerParams(has_side_effects=EFFECT),
    )(src_thru, land_thru, send_sems, recv_sems, after)[1]


def _silu(a):
    return a * _sig(a)


def _gw_ada(c_all, dmod_piece):
    def body(c_ref, d_ref, o_ref):
        o_ref[...] = _dot_tn(_bf(_silu(c_ref[...])), _bf(d_ref[...]))

    return pl.pallas_call(
        body, name="gw_ada", out_shape=jax.ShapeDtypeStruct((c_all.shape[1], dmod_piece.shape[1]), F32),
        compiler_params=_params(),
    )(c_all, dmod_piece)


def _adam(parts, w, m, v, name):
    k, r, n = parts.shape
    if r <= 256 or r % 256 == 0:
        tr, tn = min(r, 256), n
    else:
        tr, tn = r, 256
    assert r % tr == 0 and n % tn == 0

    def body(p_ref, w_ref, m_ref, v_ref, g_ref, d_ref, nm_ref, nv_ref):
        g = p_ref[0].astype(F32)
        for j in range(1, k):
            g = g + p_ref[j].astype(F32)
        g_ref[...] = g
        d_ref[...], nm_ref[...], nv_ref[...] = _adam_math(g, w_ref[...], m_ref[...], v_ref[...])

    blk = pl.BlockSpec((tr, tn), lambda i, j: (i, j))
    return pl.pallas_call(
        body, name=name, grid=(r // tr, n // tn),
        in_specs=[pl.BlockSpec((k, tr, tn), lambda i, j: (0, i, j)), blk, blk, blk],
        out_specs=[blk, blk, blk, blk],
        out_shape=[jax.ShapeDtypeStruct((r, n), F32)] * 4,
        compiler_params=_params(dimension_semantics=("arbitrary", "arbitrary")),
    )(parts, w, m, v)


def _adam_math(g, w, m, v):
    m_new = ADAM_B1 * m + (1.0 - ADAM_B1) * g
    v_new = ADAM_B2 * v + (1.0 - ADAM_B2) * jnp.square(g)
    m_hat = m_new / (1.0 - ADAM_B1 ** ADAM_STEP)
    v_hat = v_new / (1.0 - ADAM_B2 ** ADAM_STEP)
    return -ADAM_LR * (m_hat / (jnp.sqrt(v_hat) + ADAM_EPS) + ADAM_WD * w), m_new, v_new


_SMALL = (("b_ada", 3 * D_MODEL), ("norm_w", D_MODEL), ("q_norm_w", HEAD_DIM), ("k_norm_w", HEAD_DIM),
          ("rel_bias", REL_BUCKETS * ATTN_HEADS), ("sinks", ATTN_HEADS), ("conv_b", XBC_W), ("dt_bias", SSM_HEADS),
          ("a_log", SSM_HEADS), ("d_skip", SSM_HEADS), ("ssm_norm_w", SSM_W))
_SLOT = tuple(-(-n // 128) * 128 for _, n in _SMALL)
_SLOT_OFF = tuple(int(o) for o in np.cumsum((0,) + _SLOT))
_LOSS_OFF = _SLOT_OFF[-1]
_CW_OFF = _LOSS_OFF + 128
_PACK_N = _CW_OFF + CONV_K * XBC_W


def _pack_partials(small, loss, g_conv_w):
    parts = []
    for (name, n), slot in zip(_SMALL, _SLOT):
        parts.append(small[name].reshape(1, n))
        if slot > n:
            parts.append(jnp.zeros((1, slot - n), F32))
    parts += [loss.reshape(1, 1), jnp.zeros((1, 127), F32), g_conv_w.reshape(1, CONV_K * XBC_W)]
    return jnp.concatenate(parts, axis=1)


def _adam_small(pack_all, w, m, v):
    names = [name for name, _ in _SMALL]

    def body(p_ref, *rest):
        ins, outs = rest[:3 * len(names)], rest[3 * len(names):]

        def total(off, n):
            g = p_ref[0, :, off:off + n]
            for d in range(1, N_DEV):
                g = g + p_ref[d, :, off:off + n]
            return g

        for j, (name, n) in enumerate(_SMALL):
            g = total(_SLOT_OFF[j], n)
            delta, m_new, v_new = _adam_math(g, ins[3 * j][...], ins[3 * j + 1][...], ins[3 * j + 2][...])
            outs[4 * j][...] = g
            outs[4 * j + 1][...] = delta
            outs[4 * j + 2][...] = m_new
            outs[4 * j + 3][...] = v_new
        outs[-1][...] = total(_LOSS_OFF, 1)

    flat = []
    for name, n in _SMALL:
        flat += [w[name].reshape(1, n), m[name].reshape(1, n), v[name].reshape(1, n)]
    out_shape = [jax.ShapeDtypeStruct((1, n), F32) for _, n in _SMALL for _ in range(4)] + [jax.ShapeDtypeStruct((1, 1), F32)]
    out = pl.pallas_call(body, name="adam_small", out_shape=out_shape, compiler_params=_params())(pack_all, *flat)
    res = {name: [out[4 * j + t].reshape(w[name].shape) for t in range(4)] for j, name in enumerate(names)}
    return res, out[-1]


WEIGHTS = ("w_ada", "b_ada", "norm_w", "w_in", "q_norm_w", "k_norm_w", "rel_bias", "sinks", "conv_w", "conv_b",
           "dt_bias", "a_log", "d_skip", "ssm_norm_w", "w_attn_proj", "w_ssm_proj", "w_out")


def kernel(x, c, w_ada, b_ada, norm_w, w_in, q_norm_w, k_norm_w, rel_bias, sinks, conv_w, conv_b, dt_bias, a_log, d_skip, ssm_norm_w, w_attn_proj, w_ssm_proj, w_out, loss_target, m_w_ada, m_b_ada, m_norm_w, m_w_in, m_q_norm_w, m_k_norm_w, m_rel_bias, m_sinks, m_conv_w, m_conv_b, m_dt_bias, m_a_log, m_d_skip, m_ssm_norm_w, m_w_attn_proj, m_w_ssm_proj, m_w_out, v_w_ada, v_b_ada, v_norm_w, v_w_in, v_q_norm_w, v_k_norm_w, v_rel_bias, v_sinks, v_conv_w, v_conv_b, v_dt_bias, v_a_log, v_d_skip, v_ssm_norm_w, v_w_attn_proj, v_w_ssm_proj, v_w_out):
    w = dict(w_ada=w_ada, b_ada=b_ada, norm_w=norm_w, w_in=w_in, q_norm_w=q_norm_w, k_norm_w=k_norm_w,
             rel_bias=rel_bias, sinks=sinks, conv_w=conv_w, conv_b=conv_b, dt_bias=dt_bias, a_log=a_log,
             d_skip=d_skip, ssm_norm_w=ssm_norm_w, w_attn_proj=w_attn_proj, w_ssm_proj=w_ssm_proj, w_out=w_out)
    m = dict(w_ada=m_w_ada, b_ada=m_b_ada, norm_w=m_norm_w, w_in=m_w_in, q_norm_w=m_q_norm_w, k_norm_w=m_k_norm_w,
             rel_bias=m_rel_bias, sinks=m_sinks, conv_w=m_conv_w, conv_b=m_conv_b, dt_bias=m_dt_bias, a_log=m_a_log,
             d_skip=m_d_skip, ssm_norm_w=m_ssm_norm_w, w_attn_proj=m_w_attn_proj, w_ssm_proj=m_w_ssm_proj, w_out=m_w_out)
    v = dict(w_ada=v_w_ada, b_ada=v_b_ada, norm_w=v_norm_w, w_in=v_w_in, q_norm_w=v_q_norm_w, k_norm_w=v_k_norm_w,
             rel_bias=v_rel_bias, sinks=v_sinks, conv_w=v_conv_w, conv_b=v_conv_b, dt_bias=v_dt_bias, a_log=v_a_log,
             d_skip=v_d_skip, ssm_norm_w=v_ssm_norm_w, w_attn_proj=v_w_attn_proj, w_ssm_proj=v_w_ssm_proj, w_out=v_w_out)
    me = 4 * lax.axis_index("x") + 2 * lax.axis_index("y") + lax.axis_index("c")
    ada_n = w_ada.shape[2]
    in_n = w_in.shape[2]
    cw_n = conv_w.shape[2]

    b_piece = lax.dynamic_slice_in_dim(b_ada, me * ada_n, ada_n, axis=1)
    first, mod_all = _gather_mod(jnp.concatenate([c, conv_w[0].reshape(1, CONV_K * cw_n)], axis=1), w_ada[0], b_piece)
    first = first[:, 0]
    c_all = first[:, :D_MODEL]
    conv_w_full = first[:, D_MODEL:].reshape(N_DEV, CONV_K, cw_n).transpose(1, 0, 2).reshape(CONV_K, XBC_W)
    mod = lax.dynamic_index_in_dim(mod_all, me, axis=1, keepdims=False).reshape(1, 3 * D_MODEL)
    shift, scale, gate = mod[:, :D_MODEL], mod[:, D_MODEL:2 * D_MODEL], mod[:, 2 * D_MODEL:]

    pad = -in_n % 24
    w_t, zero = _ag_relayed(jnp.pad(w_in[0].T.astype(BF), ((0, pad), (0, 0))), "ag_w_in", chunks=3)
    w_t = w_t[:, :in_n].reshape(N_DEV * in_n, D_MODEL)

    def with_mine(blocks, mine):
        return lax.dynamic_update_index_in_dim(lax.empty(blocks, mine.dtype), mine, me, axis=0)

    rows = jnp.concatenate([w_attn_proj[0], w_ssm_proj[0], w_out[0]], axis=0).astype(BF) + zero
    r_ap, r_sp = w_attn_proj.shape[1], w_ssm_proj.shape[1]
    rows_started, zero = _exchange_start(rows, with_mine((N_DEV,) + rows.shape, rows), True, "ag_rows_start")

    def rows_fn(after):
        return _exchange_wait(rows_started, after, True, "ag_rows_wait")

    started = {}

    def send_blocks(key, g, name):
        started[key], zero = _exchange_start(
            g, with_mine(g.shape, lax.dynamic_index_in_dim(g, me, axis=0, keepdims=False)), False, name)
        return zero

    def after_mid(g_wap, g_wsp, g_wout):
        return send_blocks("rows", jnp.concatenate(
            [g_wap.reshape(N_DEV, r_ap, D_MODEL), g_wsp.reshape(N_DEV, r_sp, D_MODEL),
             g_wout.reshape(N_DEV, r_ap, D_MODEL)], axis=1), "rs_rows_start")

    def after_gw(g_ws):
        return send_blocks("in", jnp.concatenate(g_ws, axis=0).reshape(N_DEV, in_n, D_MODEL), "rs_in_start")

    r = _local_step(x[0], loss_target[0], shift, scale + zero, gate, w_t, rows_fn, norm_w, q_norm_w, k_norm_w,
                    rel_bias, sinks, conv_w_full, conv_b, dt_bias, a_log, d_skip, ssm_norm_w, after_mid, after_gw)

    small = dict(b_ada=r["dmod"], norm_w=r["g_norm_w"], q_norm_w=r["g_qnw"], k_norm_w=r["g_knw"], rel_bias=r["g_rel"],
                 sinks=r["g_sinks"], conv_b=r["g_conv_b"], dt_bias=r["g_dt_bias"], a_log=r["g_a_log"],
                 d_skip=r["g_d_skip"], ssm_norm_w=r["g_ssm_nw"])
    pack_all = _ag_direct(_pack_partials(small, r["loss"], r["g_conv_w"]), "ag_small")
    res, loss = _adam_small(pack_all, w, m, v)
    loss = loss[0, 0]
    cw_parts = pack_all[:, 0, _CW_OFF:].reshape(N_DEV, CONV_K, XBC_W)
    cw_mine = lax.dynamic_slice_in_dim(cw_parts, me * cw_n, cw_n, axis=2)
    res["conv_w"] = [a[None] for a in _adam(cw_mine, conv_w[0], m_conv_w[0], v_conv_w[0], "adam_conv_w")]

    dmod_piece = lax.dynamic_slice_in_dim(pack_all[:, 0, :3 * D_MODEL], me * ada_n, ada_n, axis=1)
    g_ada = _gw_ada(c_all, dmod_piece)
    res["w_ada"] = [a[None] for a in _adam(g_ada[None], w_ada[0], m_w_ada[0], v_w_ada[0], "adam_w_ada")]

    cat = lambda d: jnp.concatenate([d["w_attn_proj"][0], d["w_ssm_proj"][0], d["w_out"][0]], axis=0)
    rows_res = _adam(_exchange_wait(started["rows"], g_ada, False, "rs_rows_wait"), cat(w), cat(m), cat(v), "adam_w_rows")
    res["w_in"] = [a.T[None] for a in _adam(_exchange_wait(started["in"], rows_res[0], False, "rs_in_wait"),
                                            w_in[0].T, m_w_in[0].T, v_w_in[0].T, "adam_w_in")]
    res["w_attn_proj"] = [a[None, :r_ap] for a in rows_res]
    res["w_ssm_proj"] = [a[None, r_ap:r_ap + r_sp] for a in rows_res]
    res["w_out"] = [a[None, r_ap + r_sp:] for a in rows_res]

    outs = [loss, r["grad_x"][None]]
    for j in range(4):
        outs += [res[name][j] for name in WEIGHTS]
    return tuple(outs)
```

```python
import math

import numpy as np
import jax
import jax.numpy as jnp
from jax import lax
from jax.experimental import pallas as pl
from jax.experimental.pallas import tpu as pltpu

F32 = jnp.float32
BF = jnp.bfloat16
HI = lax.Precision.HIGHEST

D_MODEL = 1024
ATTN_HEADS = 16
KV_HEADS = 4
GRP = ATTN_HEADS // KV_HEADS
HEAD_DIM = 64
ATTN_W = ATTN_HEADS * HEAD_DIM
KV_W = KV_HEADS * HEAD_DIM
BLOCK = 128
REL_BUCKETS = 32
REL_MAX_DIST = 128
SSM_W = 2048
SSM_P = 64
SSM_HEADS = 32
SSM_G = 4
SSM_R = 8
SSM_N = 128
CONV_K = 4
XBC_W = SSM_W + 2 * SSM_G * SSM_N
SEG_W = (ATTN_W, 2 * KV_W, ATTN_W + SSM_W, XBC_W, SSM_HEADS, 2 * D_MODEL)
NSEG = len(SEG_W)
SEG_OFF = tuple(int(v) for v in np.cumsum((0,) + SEG_W))
IN_W = SEG_OFF[-1]
GATE_SEGS = (2, 5)
EPS = 1e-6
N_DEV = 8
ADAM_LR, ADAM_B1, ADAM_B2, ADAM_EPS, ADAM_WD, ADAM_STEP = 0.001, 0.9, 0.999, 1e-08, 0.01, 10
VMEM_LIMIT = 60 * 1024 * 1024
MESH = pl.DeviceIdType.MESH
ANY = pl.BlockSpec(memory_space=pl.ANY)


def _dot(a, b, precision=None):
    return jnp.dot(a, b, preferred_element_type=F32, precision=precision)


def _dot_nt(a, b, precision=None):
    return lax.dot_general(a, b, (((1,), (1,)), ((), ())), preferred_element_type=F32, precision=precision)


def _dot_tn(a, b, precision=None):
    return lax.dot_general(a, b, (((0,), (0,)), ((), ())), preferred_element_type=F32, precision=precision)


def _bf(a):
    return a.astype(BF)


def _sig(a):
    return 0.5 * jnp.tanh(0.5 * a) + 0.5


def _params(**kw):
    return pltpu.CompilerParams(vmem_limit_bytes=VMEM_LIMIT, **kw)


def _full(shape):
    nd = len(shape)
    return pl.BlockSpec(shape, lambda i: (0,) * nd)


def _rows(tm, w):
    return pl.BlockSpec((tm, w), lambda i: (i, 0))


def _inproj(x, norm_w, scale, shift, w_t, tm=256):
    s = x.shape[0]

    def body(x_ref, nw_ref, sc_ref, sh_ref, w_hbm, *rest):
        outs, h_ref, w_vm, sem = rest[:NSEG], rest[NSEG], rest[NSEG + 1], rest[NSEG + 2]
        first = pl.program_id(0) == 0
        cps = [pltpu.make_async_copy(w_hbm.at[SEG_OFF[j]:SEG_OFF[j + 1], :], w_vm.at[SEG_OFF[j]:SEG_OFF[j + 1], :], sem.at[j])
               for j in range(NSEG)]

        def tile(waiting):
            xv = x_ref[...]
            r = lax.rsqrt(jnp.mean(xv * xv, axis=-1, keepdims=True) + EPS)
            h = xv * r * (nw_ref[...] * (1.0 + sc_ref[...])) + sh_ref[...]
            hb = _bf(h)
            h_ref[...] = hb
            for j in range(NSEG):
                if waiting:
                    cps[j].wait()
                outs[j][...] = _dot_nt(hb, w_vm[SEG_OFF[j]:SEG_OFF[j + 1], :]).astype(outs[j].dtype)

        @pl.when(first)
        def _():
            for cp in cps:
                cp.start()
            tile(True)

        @pl.when(jnp.logical_not(first))
        def _():
            tile(False)

    vec = _full((1, D_MODEL))
    return pl.pallas_call(
        body, name="inproj", grid=(s // tm,),
        in_specs=[_rows(tm, D_MODEL), vec, vec, vec, ANY],
        out_specs=[_rows(tm, w) for w in SEG_W] + [_rows(tm, D_MODEL)],
        out_shape=[jax.ShapeDtypeStruct((s, w), BF if j in GATE_SEGS else F32) for j, w in enumerate(SEG_W)]
                  + [jax.ShapeDtypeStruct((s, D_MODEL), BF)],
        scratch_shapes=[pltpu.VMEM((IN_W, D_MODEL), BF), pltpu.SemaphoreType.DMA((NSEG,))],
        compiler_params=_params(dimension_semantics=("arbitrary",)),
    )(x, norm_w, scale, shift, w_t)


def _bucket_onehot_t():
    qi = jnp.arange(BLOCK)[:, None]
    kj = jnp.arange(2 * BLOCK)[None, :]
    dist = qi + BLOCK - kj
    n = jnp.maximum(dist, 0)
    max_exact = REL_BUCKETS // 2
    nf = jnp.maximum(n, 1).astype(F32)
    large = max_exact + (jnp.log(nf / max_exact) / math.log(REL_MAX_DIST / max_exact)
                         * (REL_BUCKETS - max_exact)).astype(jnp.int32)
    large = jnp.minimum(large, REL_BUCKETS - 1)
    bucket = jnp.where(n < max_exact, n, large).reshape(1, BLOCK * 2 * BLOCK)
    return (bucket == jnp.arange(REL_BUCKETS)[:, None]).astype(F32)


def _bias_dense(rel_bias_t, oh_t):
    def body(rb_ref, oh_ref, o_ref):
        o_ref[...] = _dot(rb_ref[...], oh_ref[...], HI)

    return pl.pallas_call(
        body, name="bias_dense", out_shape=jax.ShapeDtypeStruct((ATTN_HEADS, BLOCK * 2 * BLOCK), F32),
        compiler_params=_params(),
    )(rel_bias_t, oh_t)


def _bias_grad(ds_sum, oh_t):
    def body(ds_ref, oh_ref, o_ref):
        o_ref[...] = _dot_nt(ds_ref[...], oh_ref[...], HI)

    return pl.pallas_call(
        body, name="bias_grad", out_shape=jax.ShapeDtypeStruct((ATTN_HEADS, REL_BUCKETS), F32),
        compiler_params=_params(),
    )(ds_sum, oh_t)


def _group_sum(a, e):
    hi = _bf(a)
    return _dot(hi, e) + _dot(_bf(a - hi.astype(F32)), e)


def _group_bcast(a, e3t):
    hi = _bf(a)
    r1 = a - hi.astype(F32)
    mid = _bf(r1)
    return _dot(jnp.concatenate([hi, mid, _bf(r1 - mid.astype(F32))], axis=1), e3t)


def _membership(width, group, ngroups):
    e = (jnp.arange(width)[:, None] // group == jnp.arange(ngroups)[None, :]).astype(BF)
    return e, jnp.tile(e.T, (3, 1))


def _fold(width, group):
    return (jnp.arange(width)[:, None] % group == jnp.arange(group)[None, :]).astype(BF)


def _heads_norm(t, w_x, e, e3t):
    r = lax.rsqrt(_dot(_bf(t * t), e) * (1.0 / HEAD_DIM) + EPS)
    r_x = _group_bcast(r, e3t)
    return t * r_x * w_x, r_x


def _heads_norm_bwd(t, r_x, w_x, d, e, e3t):
    wd = d * w_x
    corr = _group_bcast(_dot(_bf(t * wd), e) * (1.0 / HEAD_DIM), e3t)
    return r_x * wd - t * (r_x * r_x * r_x) * corr, jnp.sum(d * t * r_x, axis=0, keepdims=True)


def _stack_heads(a, hk):
    return jnp.concatenate([a[:, (hk * GRP + g) * HEAD_DIM:(hk * GRP + g + 1) * HEAD_DIM] for g in range(GRP)], axis=0)


def _stack_cols(a, hk):
    return jnp.concatenate([a[:, hk * GRP + g:hk * GRP + g + 1] for g in range(GRP)], axis=0)


def _masked_bias(bias):
    qi = jnp.arange(BLOCK)[:, None]
    kj = jnp.arange(2 * BLOCK)[None, :]
    cur_ok = jnp.logical_and(kj >= BLOCK, kj - BLOCK <= qi)
    both_ok = jnp.logical_or(jnp.logical_and(kj < BLOCK, kj > qi), cur_ok)
    return jnp.stack([jnp.where(cur_ok, bias, -1e30), jnp.where(both_ok, bias, -1e30)])


def _attn_consts(qnw, knw):
    eq, eq3t = _membership(ATTN_W, HEAD_DIM, ATTN_HEADS)
    ek, ek3t = _membership(KV_W, HEAD_DIM, ATTN_HEADS)
    return (jnp.tile(qnw, (1, ATTN_HEADS)), jnp.tile(knw, (1, KV_HEADS)), eq, eq3t, ek, ek3t)


def _attn_fwd(q, kv, bias, sinks, consts):
    s = q.shape[0]
    nb = s // BLOCK
    gq = GRP * BLOCK
    bias_t = bias.reshape(2, KV_HEADS, GRP, BLOCK, 2 * BLOCK).transpose(0, 1, 4, 2, 3).reshape(2, KV_HEADS, 2 * BLOCK, gq)
    sink_rows = jnp.repeat(sinks.reshape(KV_HEADS, GRP), BLOCK, axis=1).reshape(KV_HEADS, 1, gq)
    eye = jnp.eye(BLOCK, dtype=BF)

    def body(q_ref, kp_ref, kc_ref, vp_ref, vc_ref, b_ref, bt_ref, sk_ref, skr_ref, eye_ref,
             qw_ref, kw_ref, eq_ref, eq3_ref, ek_ref, ek3_ref, o_ref, lse_ref):
        qn = _bf(_heads_norm(q_ref[...], qw_ref[...], eq_ref[...], eq3_ref[...])[0] * (HEAD_DIM ** -0.5))
        kn = _bf(_heads_norm(jnp.concatenate([kp_ref[...], kc_ref[...]], axis=0), kw_ref[...], ek_ref[...], ek3_ref[...])[0])
        vv = _bf(jnp.concatenate([vp_ref[...], vc_ref[...]], axis=0))
        ones = jnp.ones((2 * BLOCK, HEAD_DIM), BF)
        kss = [slice(hk * HEAD_DIM, (hk + 1) * HEAD_DIM) for hk in range(KV_HEADS)]
        qgs = [_stack_heads(qn, hk) for hk in range(KV_HEADS)]
        sc_ts = [_dot_nt(kn[:, kss[hk]], qgs[hk]) + bt_ref[0, hk] for hk in range(KV_HEADS)]
        m_rows = [jnp.maximum(jnp.max(sc_ts[hk], axis=0, keepdims=True), skr_ref[hk]) for hk in range(KV_HEADS)]
        m_hq = _bf(jnp.concatenate([(m + jnp.abs(m) * (2.0 ** -7))[:, g * BLOCK:(g + 1) * BLOCK]
                                    for m in m_rows for g in range(GRP)], axis=0))
        m16 = _dot_nt(eye_ref[...], m_hq)
        ms = [_stack_cols(m16, hk) for hk in range(KV_HEADS)]
        scs = [_dot_nt(qgs[hk], kn[:, kss[hk]]) + b_ref[0, hk * GRP:(hk + 1) * GRP].reshape(gq, 2 * BLOCK)
               for hk in range(KV_HEADS)]
        ps = [_bf(jnp.exp(scs[hk] - ms[hk])) for hk in range(KV_HEADS)]
        pvs = [_dot(ps[hk], jnp.concatenate([vv[:, kss[hk]], ones], axis=1)) for hk in range(KV_HEADS)]
        den16 = jnp.concatenate([pvs[hk][g * BLOCK:(g + 1) * BLOCK, HEAD_DIM:HEAD_DIM + 1]
                                 for hk in range(KV_HEADS) for g in range(GRP)], axis=1)
        den16 = den16 + jnp.exp(sk_ref[...] - m16)
        lse_ref[...] = m16 + jnp.log(den16)
        inv16 = 1.0 / den16
        for hk in range(KV_HEADS):
            for g in range(GRP):
                h = hk * GRP + g
                o_ref[:, h * HEAD_DIM:(h + 1) * HEAD_DIM] = (pvs[hk][g * BLOCK:(g + 1) * BLOCK, :HEAD_DIM]
                                                             * inv16[:, h:h + 1])

    cur = lambda w, col=0: pl.BlockSpec((BLOCK, w), lambda i: (i, col))
    prev = lambda w, col=0: pl.BlockSpec((BLOCK, w), lambda i: (jnp.maximum(i - 1, 0), col))
    whole = lambda a: pl.BlockSpec(a.shape, lambda i: (0,) * a.ndim)
    first_or_not = lambda a: pl.BlockSpec((1,) + a.shape[1:], lambda i: (jnp.minimum(i, 1),) + (0,) * (a.ndim - 1))
    return pl.pallas_call(
        body, name="attn_fwd", grid=(nb,),
        in_specs=[cur(ATTN_W), prev(KV_W, 0), cur(KV_W, 0), prev(KV_W, 1), cur(KV_W, 1),
                  first_or_not(bias), first_or_not(bias_t),
                  whole(sinks), whole(sink_rows), whole(eye)] + [_full(c.shape) for c in consts],
        out_specs=[cur(ATTN_W), cur(ATTN_HEADS)],
        out_shape=[jax.ShapeDtypeStruct((s, ATTN_W), F32), jax.ShapeDtypeStruct((s, ATTN_HEADS), F32)],
        compiler_params=_params(dimension_semantics=("arbitrary",)),
    )(q, kv, kv, kv, kv, bias, bias_t, sinks, sink_rows, eye, *consts)


def _conv_taps(xbc, tail):
    ext = jnp.concatenate([tail, xbc], axis=0)
    return [pltpu.roll(ext, CONV_K - 1 - j, axis=0)[8:8 + BLOCK] if j < CONV_K - 1 else xbc for j in range(CONV_K)]


def _softplus(u):
    return jnp.maximum(u, 0.0) + jnp.log(1.0 + jnp.exp(-jnp.abs(u)))


def _tril():
    r = lax.broadcasted_iota(jnp.int32, (BLOCK, BLOCK), 0)
    c = lax.broadcasted_iota(jnp.int32, (BLOCK, BLOCK), 1)
    return r >= c


def _triu():
    r = lax.broadcasted_iota(jnp.int32, (BLOCK, BLOCK), 0)
    c = lax.broadcasted_iota(jnp.int32, (BLOCK, BLOCK), 1)
    return r <= c


def _exact_left(m01, a):
    hi = _bf(a)
    r1 = a - hi.astype(F32)
    mid = _bf(r1)
    return _dot(m01, hi) + _dot(m01, mid) + _dot(m01, _bf(r1 - mid.astype(F32)))


def _ssd_common(conv, dtr, dtb_ref, alog_ref, e3_ref):
    sg = _sig(conv)
    xact = conv * sg
    u = dtr + dtb_ref[...]
    dt = _softplus(u)
    a = -jnp.exp(alog_ref[...])
    trilb = _tril()
    acum = _exact_left(trilb.astype(BF), dt * a) * math.log2(math.e)
    both = _group_bcast(jnp.concatenate([dt, acum], axis=0), e3_ref[...])
    dt_x, acum_x = both[:BLOCK], both[BLOCK:]
    return sg, xact, u, dt, a, trilb, acum, dt_x, acum_x


SSD_CH = 2


def _ssd_fwd(xbc, dt_raw, conv_w, conv_b, dt_bias, a_log, dsk_x, e3t):
    s = xbc.shape[0]
    nc = s // BLOCK
    ch = SSD_CH if nc % SSD_CH == 0 else 1
    rows = ch * BLOCK

    def body(x_ref, tail_ref, dtr_ref, cw_ref, cb_ref, dtb_ref, alog_ref, dsk_ref, e3_ref,
             y_ref, hp_ref, conv_ref, hst, yd_s, yoff_s):
        i = pl.program_id(0)

        @pl.when(i == 0)
        def _():
            hst[...] = jnp.zeros_like(hst)

        for j in range(ch):
            rs = slice(j * BLOCK, (j + 1) * BLOCK)
            tail = jnp.where(i > 0, tail_ref[...], 0.0) if j == 0 else x_ref[j * BLOCK - 8:j * BLOCK, :]
            taps = _conv_taps(x_ref[rs, :], tail)
            conv = cb_ref[...] + sum(taps[t] * cw_ref[t:t + 1, :] for t in range(CONV_K))
            conv_ref[rs, :] = conv
            _, xact, _, _, _, trilb, acum, dt_x, acum_x = _ssd_common(conv, dtr_ref[rs, :], dtb_ref, alog_ref, e3_ref)
            xs = xact[:, :SSM_W]
            acum_t = acum.T
            ea_x = jnp.exp2(acum_x)
            last_x = acum_x[BLOCK - 1:BLOCK, :]
            xdt = xs * dt_x
            xw = xdt * jnp.exp2(last_x - acum_x)
            cd_x = jnp.exp2(last_x)
            hprev = hst[...]
            hp_ref[j] = hprev
            sls = [slice(g * SSM_R * SSM_P, (g + 1) * SSM_R * SSM_P) for g in range(SSM_G)]
            bgs = [_bf(xact[:, SSM_W + g * SSM_N:SSM_W + (g + 1) * SSM_N]) for g in range(SSM_G)]
            cgs = [_bf(xact[:, SSM_W + SSM_G * SSM_N + g * SSM_N:SSM_W + SSM_G * SSM_N + (g + 1) * SSM_N])
                   for g in range(SSM_G)]
            xdt_b, xw_b, hprev_b = _bf(xdt), _bf(xw), _bf(hprev)
            low_half = lax.broadcasted_iota(jnp.int32, (BLOCK, 2 * SSM_P), 1) < SSM_P
            cbs = [_dot_nt(cgs[g], bgs[g]) for g in range(SSM_G)]
            for g in range(SSM_G):
                sl = sls[g]
                yoff_s[:, sl] = _dot(cgs[g], hprev_b[:, sl]) * ea_x[:, sl]
                hst[:, sl] = hprev[:, sl] * cd_x[:, sl] + _dot_tn(bgs[g], xw_b[:, sl])
            for g in range(SSM_G):
                hss = [slice((g * SSM_R + r) * SSM_P, (g * SSM_R + r + 1) * SSM_P) for r in range(SSM_R)]
                mms = [_bf(cbs[g] * jnp.exp2(jnp.where(trilb, acum[:, g * SSM_R + r:g * SSM_R + r + 1]
                                                      - acum_t[g * SSM_R + r:g * SSM_R + r + 1, :], -1e30)))
                       for r in range(SSM_R)]
                for r in range(0, SSM_R, 2):
                    pair = slice(hss[r].start, hss[r + 1].stop)
                    xp = xdt_b[:, pair]
                    rhs = jnp.concatenate([jnp.where(low_half, xp, 0), jnp.where(low_half, 0, xp)], axis=0)
                    yd_s[:, pair] = _dot(jnp.concatenate([mms[r], mms[r + 1]], axis=1), rhs)
            y_ref[rs, :] = yd_s[...] + yoff_s[...] + dsk_ref[...] * xs

    blk = lambda w: pl.BlockSpec((rows, w), lambda i: (i, 0))
    return pl.pallas_call(
        body, name="ssd_fwd", grid=(nc // ch,),
        in_specs=[blk(XBC_W), pl.BlockSpec((8, XBC_W), lambda i: (jnp.maximum(i * (rows // 8) - 1, 0), 0)),
                  blk(SSM_HEADS), _full((CONV_K, XBC_W)), _full((1, XBC_W)), _full((1, SSM_HEADS)),
                  _full((1, SSM_HEADS)), _full((1, SSM_W)), _full((3 * SSM_HEADS, SSM_W))],
        out_specs=[blk(SSM_W), pl.BlockSpec((ch, SSM_N, SSM_W), lambda i: (i, 0, 0)), blk(XBC_W)],
        out_shape=[jax.ShapeDtypeStruct((s, SSM_W), F32), jax.ShapeDtypeStruct((nc, SSM_N, SSM_W), F32),
                   jax.ShapeDtypeStruct((s, XBC_W), F32)],
        scratch_shapes=[pltpu.VMEM((SSM_N, SSM_W), F32), pltpu.VMEM((BLOCK, SSM_W), F32), pltpu.VMEM((BLOCK, SSM_W), F32)],
        compiler_params=_params(dimension_semantics=("arbitrary",)),
    )(xbc, xbc, dt_raw, conv_w, conv_b, dt_bias, a_log, dsk_x, e3t)


def _dsilu(z, sg, silu):
    return sg * (1.0 + (z - silu))


def _mid(x, tgt, o_att, zam, ypre, gab, gate, ssm_nw, rows_all, tm=256):
    s = x.shape[0]
    gw = SSM_W // SSM_G

    r_ap, r_sp = ATTN_W // N_DEV, SSM_W // N_DEV

    def body(x_ref, t_ref, o_ref, zam_ref, yp_ref, gab_ref, gate_ref, nw_ref, rows_h,
             dout_ref, do_ref, dzam_ref, dyp_ref, dgab_ref,
             yag_ref, dya_ref, yn_ref, dyb_ref, mg_ref, dob_ref, gnw_ref, dgate_ref, loss_ref,
             wap_v, wsp_v, wout_v, sem):
        i = pl.program_id(0)

        @pl.when(i == 0)
        def _():
            cps = []
            for d in range(N_DEV):
                for j, (dst, r0, rn) in enumerate(((wap_v, 0, r_ap), (wsp_v, r_ap, r_sp), (wout_v, r_ap + r_sp, r_ap))):
                    cps.append(pltpu.make_async_copy(rows_h.at[d, r0:r0 + rn, :], dst.at[d * rn:(d + 1) * rn, :], sem.at[j]))
            for cp in cps:
                cp.start()
            gnw_ref[...] = jnp.zeros_like(gnw_ref)
            dgate_ref[...] = jnp.zeros_like(dgate_ref)
            loss_ref[...] = jnp.zeros_like(loss_ref)
            for cp in cps:
                cp.wait()

        gate = gate_ref[...]
        nw = nw_ref[...]
        o_att = o_ref[...]
        z_a = zam_ref[:, :ATTN_W].astype(F32)
        s_a = _sig(z_a)
        silu_a = z_a * s_a
        yag = _bf(o_att * silu_a)
        yag_ref[...] = yag
        ypre = yp_ref[...]
        z_m = zam_ref[:, ATTN_W:].astype(F32)
        s_m = _sig(z_m)
        silu_m = z_m * s_m
        yg = ypre * silu_m
        rinv = jnp.concatenate(
            [jnp.broadcast_to(lax.rsqrt(jnp.mean(yg[:, g * gw:(g + 1) * gw] ** 2, axis=-1, keepdims=True) + EPS), (tm, gw))
             for g in range(SSM_G)], axis=1)
        ynr = yg * rinv
        yn = _bf(ynr * nw)
        yn_ref[...] = yn
        y_a = _dot(yag, wap_v[...])
        y_b = _dot(yn, wsp_v[...])
        g_a = _sig(gab_ref[:, :D_MODEL].astype(F32))
        g_b = _sig(gab_ref[:, D_MODEL:].astype(F32))
        merged = _bf(g_a * y_a + g_b * y_b)
        mg_ref[...] = merged
        o = _dot(merged, wout_v[...])
        diff = x_ref[...] + gate * o - t_ref[...]
        loss_ref[...] += (0.5 / D_MODEL) * jnp.sum(diff * diff, axis=(0, 1), keepdims=True)
        dout = diff * (1.0 / D_MODEL)
        dout_ref[...] = dout
        dgate_ref[...] += jnp.sum(dout * o, axis=0, keepdims=True)
        d_o = _bf(dout * gate)
        dob_ref[...] = d_o
        dmerged = _dot_nt(d_o, wout_v[...])
        dy_af = dmerged * g_a
        dy_bf = dmerged * g_b
        dy_a = _bf(dy_af)
        dy_b = _bf(dy_bf)
        dya_ref[...] = dy_a
        dyb_ref[...] = dy_b
        dyag = _dot_nt(dy_a, wap_v[...])
        dyn = _dot_nt(dy_b, wsp_v[...])
        dgab_ref[:, :D_MODEL] = _bf(dy_af * y_a * (1.0 - g_a))
        dgab_ref[:, D_MODEL:] = _bf(dy_bf * y_b * (1.0 - g_b))
        do_ref[...] = dyag * silu_a
        dzam_ref[:, :ATTN_W] = _bf(dyag * o_att * _dsilu(z_a, s_a, silu_a))
        gnw_ref[...] += jnp.sum(dyn * ynr, axis=0, keepdims=True)
        dynw = dyn * nw
        corr = jnp.concatenate(
            [jnp.broadcast_to(jnp.mean((dynw * ynr)[:, g * gw:(g + 1) * gw], axis=-1, keepdims=True), (tm, gw))
             for g in range(SSM_G)], axis=1)
        dyg = rinv * (dynw - ynr * corr)
        dyp_ref[...] = dyg * silu_m
        dzam_ref[:, ATTN_W:] = _bf(dyg * ypre * _dsilu(z_m, s_m, silu_m))

    r1, r2, r3 = _rows(tm, D_MODEL), _rows(tm, SSM_W), _rows(tm, ATTN_W + SSM_W)
    sd = jax.ShapeDtypeStruct
    return pl.pallas_call(
        body, name="mid", grid=(s // tm,),
        in_specs=[r1, r1, r1, r3, r2, r2, _full((1, D_MODEL)), _full((1, SSM_W)), ANY],
        out_specs=[r1, r1, r3, r2, r2, r1, r1, r2, r1, r1, r1,
                   _full((1, SSM_W)), _full((1, D_MODEL)), _full((1, 1))],
        out_shape=[sd((s, D_MODEL), F32), sd((s, ATTN_W), F32), sd((s, ATTN_W + SSM_W), BF), sd((s, SSM_W), F32),
                   sd((s, 2 * D_MODEL), BF),
                   sd((s, ATTN_W), BF), sd((s, D_MODEL), BF), sd((s, SSM_W), BF), sd((s, D_MODEL), BF),
                   sd((s, D_MODEL), BF), sd((s, D_MODEL), BF),
                   sd((1, SSM_W), F32), sd((1, D_MODEL), F32), sd((1, 1), F32)],
        scratch_shapes=[pltpu.VMEM((ATTN_W, D_MODEL), BF), pltpu.VMEM((SSM_W, D_MODEL), BF), pltpu.VMEM((D_MODEL, D_MODEL), BF),
                        pltpu.SemaphoreType.DMA((3,))],
        compiler_params=_params(dimension_semantics=("arbitrary",)),
    )(x, tgt, o_att, zam, ypre, gab, gate, ssm_nw, rows_all)


def _attn_bwd(q, kv, bias, sinks, consts, o_att, lse, d_o):
    s = q.shape[0]
    nb = s // BLOCK
    folds = (_fold(ATTN_W, HEAD_DIM), _fold(KV_W, HEAD_DIM))

    def body(q_ref, kp_ref, kc_ref, vp_ref, vc_ref, b_ref, skv_ref, qw_ref, kw_ref, eq_ref, eq3_ref, ek_ref, ek3_ref,
             fq_ref, fk_ref, o_ref, lse_ref, do_ref,
             dq_ref, dkv_ref, dss_ref, gqw_ref, gkw_ref, gsk_ref, ckn, cv, dqn_s, dkn_s, dv_s, gq_x, gk_x):
        i = pl.program_id(0)
        kw, ek, ek3 = kw_ref[...], ek_ref[...], ek3_ref[...]

        @pl.when(i == 0)
        def _():
            for ref in (ckn, cv, dss_ref, gq_x, gk_x, gsk_ref):
                ref[...] = jnp.zeros_like(ref)

        @pl.when(i < nb)
        def _():
            qw, eq, eq3 = qw_ref[...], eq_ref[...], eq3_ref[...]
            qf = q_ref[...]
            qnf, rq_x = _heads_norm(qf, qw, eq, eq3)
            qn = _bf(qnf * (HEAD_DIM ** -0.5))
            kf = jnp.concatenate([kp_ref[...], kc_ref[...]], axis=0)
            knf, rk_x = _heads_norm(kf, kw, ek, ek3)
            kn = _bf(knf)
            vv = _bf(jnp.concatenate([vp_ref[...], vc_ref[...]], axis=0))
            d_of = do_ref[...]
            d_ob = _bf(d_of)
            lse_all = lse_ref[...]
            delta = _dot(_bf(d_of * o_ref[...]), eq)
            gsk_ref[...] += jnp.sum(-jnp.exp(skv_ref[...] - lse_all) * delta, axis=0, keepdims=True)
            kss = [slice(hk * HEAD_DIM, (hk + 1) * HEAD_DIM) for hk in range(KV_HEADS)]
            qgs = [_stack_heads(qn, hk) for hk in range(KV_HEADS)]
            d_ogs = [_stack_heads(d_ob, hk) for hk in range(KV_HEADS)]
            scs = [_dot_nt(qgs[hk], kn[:, kss[hk]]) + b_ref[0, hk * GRP:(hk + 1) * GRP].reshape(GRP * BLOCK, 2 * BLOCK)
                   for hk in range(KV_HEADS)]
            dps = [_dot_nt(d_ogs[hk], vv[:, kss[hk]]) for hk in range(KV_HEADS)]
            ps = [jnp.exp(scs[hk] - _stack_cols(lse_all, hk)) for hk in range(KV_HEADS)]
            dss = [ps[hk] * (dps[hk] - _stack_cols(delta, hk)) for hk in range(KV_HEADS)]
            pbs = [_bf(p) for p in ps]
            dsbs = [_bf(ds) for ds in dss]
            for hk in range(KV_HEADS):
                dss_ref[hk * GRP:(hk + 1) * GRP] += dss[hk].reshape(GRP, BLOCK, 2 * BLOCK)
            for hk in range(KV_HEADS):
                dv_s[:, kss[hk]] = _dot_tn(pbs[hk], d_ogs[hk])
                dkn_s[:, kss[hk]] = _dot_tn(dsbs[hk], qgs[hk])
            dqns = [_dot(dsbs[hk], kn[:, kss[hk]]) * (HEAD_DIM ** -0.5) for hk in range(KV_HEADS)]
            for hk in range(KV_HEADS):
                for g in range(GRP):
                    h = hk * GRP + g
                    dqn_s[:, h * HEAD_DIM:(h + 1) * HEAD_DIM] = dqns[hk][g * BLOCK:(g + 1) * BLOCK]
            dq, gq = _heads_norm_bwd(qf, rq_x, qw, dqn_s[...], eq, eq3)
            dq_ref[...] = _bf(dq)
            gq_x[...] += gq
            dk, gk = _heads_norm_bwd(kf[:BLOCK], rk_x[:BLOCK], kw, ckn[...] + dkn_s[0:BLOCK, :], ek, ek3)
            dkv_ref[:, :KV_W] = _bf(dk)
            gk_x[...] += gk
            dkv_ref[:, KV_W:] = _bf(cv[...] + dv_s[0:BLOCK, :])
            ckn[...] = dkn_s[BLOCK:2 * BLOCK, :]
            cv[...] = dv_s[BLOCK:2 * BLOCK, :]

        @pl.when(i == nb)
        def _():
            kc = kc_ref[...]
            dk, gk = _heads_norm_bwd(kc, _heads_norm(kc, kw, ek, ek3)[1], kw, ckn[...], ek, ek3)
            dkv_ref[:, :KV_W] = _bf(dk)
            dkv_ref[:, KV_W:] = _bf(cv[...])
            gqw_ref[...] = _group_sum(jnp.broadcast_to(gq_x[...], (8, ATTN_W)), fq_ref[...])[0:1]
            gkw_ref[...] = _group_sum(jnp.broadcast_to(gk_x[...] + gk, (8, KV_W)), fk_ref[...])[0:1]

    last = nb - 1
    cur = lambda w, col=0: pl.BlockSpec((BLOCK, w), lambda i: (jnp.minimum(i, last), col))
    prev = lambda w, col=0: pl.BlockSpec((BLOCK, w), lambda i: (jnp.maximum(jnp.minimum(i, last) - 1, 0), col))
    late = lambda w: pl.BlockSpec((BLOCK, w), lambda i: (jnp.maximum(i - 1, 0), 0))
    sd = jax.ShapeDtypeStruct
    return pl.pallas_call(
        body, name="attn_bwd", grid=(nb + 1,),
        in_specs=[cur(ATTN_W), prev(KV_W, 0), cur(KV_W, 0), prev(KV_W, 1), cur(KV_W, 1),
                  pl.BlockSpec((1, ATTN_HEADS, BLOCK, 2 * BLOCK), lambda i: (jnp.minimum(i, 1), 0, 0, 0)),
                  _full((1, ATTN_HEADS))]
                 + [_full(c.shape) for c in consts + folds] + [cur(ATTN_W), cur(ATTN_HEADS), cur(ATTN_W)],
        out_specs=[cur(ATTN_W), late(2 * KV_W),
                   pl.BlockSpec((ATTN_HEADS, BLOCK, 2 * BLOCK), lambda i: (0, 0, 0)),
                   _full((1, HEAD_DIM)), _full((1, HEAD_DIM)), _full((1, ATTN_HEADS))],
        out_shape=[sd((s, ATTN_W), BF), sd((s, 2 * KV_W), BF),
                   sd((ATTN_HEADS, BLOCK, 2 * BLOCK), F32), sd((1, HEAD_DIM), F32), sd((1, HEAD_DIM), F32),
                   sd((1, ATTN_HEADS), F32)],
        scratch_shapes=[pltpu.VMEM((BLOCK, KV_W), F32), pltpu.VMEM((BLOCK, KV_W), F32),
                        pltpu.VMEM((BLOCK, ATTN_W), F32), pltpu.VMEM((2 * BLOCK, KV_W), F32),
                        pltpu.VMEM((2 * BLOCK, KV_W), F32), pltpu.VMEM((1, ATTN_W), F32), pltpu.VMEM((1, KV_W), F32)],
        compiler_params=_params(dimension_semantics=("arbitrary",)),
    )(q, kv, kv, kv, kv, bias, sinks, *consts, *folds, o_att, lse, d_o)


def _ssd_bwd(xbc, conv_all, dt_raw, conv_w, dt_bias, a_log, dsk_x, e_mat, e3t, hprev_all, dy_all):
    s = xbc.shape[0]
    nc = s // BLOCK
    ch = 1
    rows = ch * BLOCK
    nsteps = nc // ch
    gw = SSM_R * SSM_P
    b0, c0 = SSM_W, SSM_W + SSM_G * SSM_N

    def body(x_ref, conv_ref, dtr_ref, cw_ref, dtb_ref, alog_ref, dsk_ref, e_ref, e3_ref, hp_ref, dy_ref,
             dx_ref, ddt_ref, gcw_ref, gcb_ref, gdtb_ref, galog_ref, gdsk_ref,
             dh, nhead, gdskx, dxdt_s, dbc_s, dxd_s):
        def chunk_bwd(j):
            rs = slice(j * BLOCK, (j + 1) * BLOCK)
            conv = conv_ref[rs, :]
            sg, xact, u, dt, a, trilb, acum, dt_x, acum_x = _ssd_common(conv, dtr_ref[rs, :], dtb_ref, alog_ref, e3_ref)
            xs = xact[:, :SSM_W]
            acum_t = acum.T
            ea_x = jnp.exp2(acum_x)
            last_x = acum_x[BLOCK - 1:BLOCK, :]
            dte_x = jnp.exp2(last_x - acum_x)
            cd_x = jnp.exp2(last_x)
            xdt = xs * dt_x
            xw = xdt * dte_x
            hprev = hp_ref[j]
            dhn = dh[...]
            dy = dy_ref[rs, :]
            gdskx[...] += jnp.sum(dy * xs, axis=0, keepdims=True)
            dyea = dy * ea_x
            lane = lax.broadcasted_iota(jnp.int32, (BLOCK, SSM_HEADS), 1)
            dacum = jnp.zeros((BLOCK, SSM_HEADS), F32)
            dacc_x, dlast_x = [], []
            sls = [slice(g * gw, (g + 1) * gw) for g in range(SSM_G)]
            bgs = [_bf(xact[:, b0 + g * SSM_N:b0 + (g + 1) * SSM_N]) for g in range(SSM_G)]
            cgs = [_bf(xact[:, c0 + g * SSM_N:c0 + (g + 1) * SSM_N]) for g in range(SSM_G)]
            hpgs = [_bf(hprev[:, sl]) for sl in sls]
            dhgs = [_bf(dhn[:, sl]) for sl in sls]
            dyeags = [_bf(dyea[:, sl]) for sl in sls]
            xwgs = [_bf(xw[:, sl]) for sl in sls]
            xdt_b, dy_b = _bf(xdt), _bf(dy)
            low_half = lax.broadcasted_iota(jnp.int32, (BLOCK, 2 * SSM_P), 1) < SSM_P
            cbs = [_dot_nt(cgs[g], bgs[g]) for g in range(SSM_G)]
            gmats = [_dot(cgs[g], hpgs[g]) for g in range(SSM_G)]
            dxws = [_dot(bgs[g], dhgs[g]) for g in range(SSM_G)]
            dcgs = [_dot_nt(dyeags[g], hpgs[g]) for g in range(SSM_G)]
            dbgs = [_dot_nt(xwgs[g], dhgs[g]) for g in range(SSM_G)]
            for g in range(SSM_G):
                sl = sls[g]
                dh[:, sl] = dhn[:, sl] * cd_x[:, sl] + _dot_tn(cgs[g], dyeags[g])
                dxdt_s[:, sl] = dxws[g] * dte_x[:, sl]
                dacc_x.append(dy[:, sl] * gmats[g] * ea_x[:, sl] - dxws[g] * xw[:, sl])
                dlast_x.append(jnp.sum(dxws[g] * xw[:, sl], axis=0, keepdims=True)
                               + jnp.sum(dhn[:, sl] * hprev[:, sl], axis=0, keepdims=True) * cd_x[:, sl])
            for g in range(SSM_G):
                bg, cg, cb, dbg, dcg = bgs[g], cgs[g], cbs[g], dbgs[g], dcgs[g]
                hss = [slice((g * SSM_R + r) * SSM_P, (g * SSM_R + r + 1) * SSM_P) for r in range(SSM_R)]
                lms = [jnp.exp2(jnp.where(trilb, acum[:, g * SSM_R + r:g * SSM_R + r + 1]
                                         - acum_t[g * SSM_R + r:g * SSM_R + r + 1, :], -1e30)) for r in range(SSM_R)]
                mms = [cb * lm for lm in lms]
                mmbs = [_bf(mm) for mm in mms]
                dms = []
                for r in range(0, SSM_R, 2):
                    pair = slice(hss[r].start, hss[r + 1].stop)
                    xp, dyp = xdt_b[:, pair], dy_b[:, pair]
                    dmp = _dot_nt(dyp, jnp.concatenate([jnp.where(low_half, xp, 0), jnp.where(low_half, 0, xp)], axis=0))
                    dms += [dmp[:, :BLOCK], dmp[:, BLOCK:]]
                    dxd_s[:, pair] = _dot_tn(jnp.concatenate([mmbs[r], mmbs[r + 1]], axis=0),
                                             jnp.concatenate([jnp.where(low_half, dyp, 0), jnp.where(low_half, 0, dyp)], axis=0))
                dcb = sum(dms[r] * lms[r] for r in range(SSM_R))
                wms = [dms[r] * mms[r] for r in range(SSM_R)]
                antis = [_bf(wm - wm.T) for wm in wms]
                for r in range(SSM_R):
                    dacum = dacum + _dot(antis[r], (lane == g * SSM_R + r).astype(BF))
                dcbb = _bf(dcb)
                dbc_s[:, g * SSM_N:(g + 1) * SSM_N] = dbg + _dot_tn(dcbb, cg)
                dbc_s[:, SSM_G * SSM_N + g * SSM_N:SSM_G * SSM_N + (g + 1) * SSM_N] = dcg + _dot(dcbb, bg)
            dxdt = dxdt_s[...] + dxd_s[...]
            dxs = dy * dsk_ref[...] + dxdt * dt_x
            red = _group_sum(jnp.concatenate(
                [dxdt * xs, jnp.concatenate(dacc_x, axis=1),
                 jnp.broadcast_to(jnp.concatenate(dlast_x, axis=1), (8, SSM_W))], axis=0), e_ref[...])
            row = lax.broadcasted_iota(jnp.int32, (BLOCK, SSM_HEADS), 0)
            dacum = dacum + red[BLOCK:2 * BLOCK] + jnp.where(row == BLOCK - 1, red[2 * BLOCK:2 * BLOCK + 1], 0.0)
            ddta = _exact_left(_triu().astype(BF), dacum)
            ddt = red[:BLOCK] + ddta * a
            galog_ref[...] += jnp.sum(ddta * dt, axis=0, keepdims=True) * a
            du = ddt * _sig(u)
            ddt_ref[rs, :] = _bf(du)
            gdtb_ref[...] += jnp.sum(du, axis=0, keepdims=True)
            dconv = jnp.concatenate([dxs, dbc_s[...]], axis=1) * _dsilu(conv, sg, xact)
            gcb_ref[...] += jnp.sum(dconv, axis=0, keepdims=True)
            ext2 = jnp.concatenate([dconv, nhead[...]], axis=0)
            ahead = [pltpu.roll(ext2, BLOCK + 8 - (CONV_K - 1 - j), axis=0)[0:BLOCK] if j < CONV_K - 1 else dconv
                     for j in range(CONV_K)]
            dx_ref[rs, :] = _bf(sum(ahead[j] * cw_ref[j:j + 1, :] for j in range(CONV_K)))
            xraw = x_ref[rs, :]
            gcw_ref[...] += jnp.concatenate([jnp.sum(ahead[j] * xraw, axis=0, keepdims=True) for j in range(CONV_K)], axis=0)
            nhead[...] = dconv[0:8]

        i = pl.program_id(0)

        @pl.when(i == 0)
        def _():
            for ref in (dh, nhead, gdskx, gcw_ref, gcb_ref, gdtb_ref, galog_ref, gdsk_ref):
                ref[...] = jnp.zeros_like(ref)

        for j in reversed(range(ch)):
            chunk_bwd(j)

        @pl.when(i == nsteps - 1)
        def _():
            gdsk_ref[...] = _group_sum(jnp.broadcast_to(gdskx[...], (8, SSM_W)), e_ref[...])[0:1]

    chunk = lambda w: pl.BlockSpec((rows, w), lambda i: (nsteps - 1 - i, 0))
    sd = jax.ShapeDtypeStruct
    return pl.pallas_call(
        body, name="ssd_bwd", grid=(nsteps,),
        in_specs=[chunk(XBC_W), chunk(XBC_W),
                  chunk(SSM_HEADS), _full((CONV_K, XBC_W)), _full((1, SSM_HEADS)),
                  _full((1, SSM_HEADS)), _full((1, SSM_W)), _full((SSM_W, SSM_HEADS)), _full((3 * SSM_HEADS, SSM_W)),
                  pl.BlockSpec((ch, SSM_N, SSM_W), lambda i: (nsteps - 1 - i, 0, 0)), chunk(SSM_W)],
        out_specs=[chunk(XBC_W), chunk(SSM_HEADS), _full((CONV_K, XBC_W)), _full((1, XBC_W)),
                   _full((1, SSM_HEADS)), _full((1, SSM_HEADS)), _full((1, SSM_HEADS))],
        out_shape=[sd((s, XBC_W), BF), sd((s, SSM_HEADS), BF), sd((CONV_K, XBC_W), F32), sd((1, XBC_W), F32),
                   sd((1, SSM_HEADS), F32), sd((1, SSM_HEADS), F32), sd((1, SSM_HEADS), F32)],
        scratch_shapes=[pltpu.VMEM((SSM_N, SSM_W), F32), pltpu.VMEM((8, XBC_W), F32),
                        pltpu.VMEM((1, SSM_W), F32), pltpu.VMEM((BLOCK, SSM_W), F32),
                        pltpu.VMEM((BLOCK, 2 * SSM_G * SSM_N), F32), pltpu.VMEM((BLOCK, SSM_W), F32)],
        compiler_params=_params(dimension_semantics=("arbitrary",)),
    )(xbc, conv_all, dt_raw, conv_w, dt_bias, a_log, dsk_x, e_mat, e3t, hprev_all, dy_all)


def _dh(x, dout, norm_w, scale, dsegs, w_t, tm=256):
    s = x.shape[0]

    def body(x_ref, dout_ref, nw_ref, sc_ref, *rest):
        d_refs, w_hbm = rest[:NSEG], rest[NSEG]
        gx_ref, dshift_ref, dscale_ref, gnw_ref = rest[NSEG + 1:NSEG + 5]
        w_vm, sem = rest[NSEG + 5], rest[NSEG + 6]
        first = pl.program_id(0) == 0
        order = sorted(range(NSEG), key=lambda j: SEG_W[j] % 128 != 0)
        dst_off = np.cumsum([0] + [SEG_W[j] for j in order])
        cps = [pltpu.make_async_copy(w_hbm.at[SEG_OFF[j]:SEG_OFF[j + 1], :],
                                     w_vm.at[int(dst_off[n]):int(dst_off[n + 1]), :], sem.at[n]) for n, j in enumerate(order)]

        def tile(waiting):
            if waiting:
                for cp in cps:
                    cp.wait()
            dh = _dot(jnp.concatenate([d_refs[j][...] for j in order], axis=1), w_vm[...])
            xv = x_ref[...]
            r = lax.rsqrt(jnp.mean(xv * xv, axis=-1, keepdims=True) + EPS)
            xn = xv * r
            nw = nw_ref[...]
            sc1 = 1.0 + sc_ref[...]
            dshift_ref[...] += jnp.sum(dh, axis=0, keepdims=True)
            dhxn = jnp.sum(dh * xn, axis=0, keepdims=True)
            dscale_ref[...] += dhxn * nw
            gnw_ref[...] += dhxn * sc1
            dxn = dh * (nw * sc1)
            gx_ref[...] = dout_ref[...] + r * (dxn - xn * jnp.mean(xn * dxn, axis=-1, keepdims=True))

        @pl.when(first)
        def _():
            for cp in cps:
                cp.start()
            for ref in (dshift_ref, dscale_ref, gnw_ref):
                ref[...] = jnp.zeros_like(ref)
            tile(True)

        @pl.when(jnp.logical_not(first))
        def _():
            tile(False)

    vec = _full((1, D_MODEL))
    sd = jax.ShapeDtypeStruct
    return pl.pallas_call(
        body, name="dh", grid=(s // tm,),
        in_specs=[_rows(tm, D_MODEL), _rows(tm, D_MODEL), vec, vec] + [_rows(tm, w) for w in SEG_W] + [ANY],
        out_specs=[_rows(tm, D_MODEL), vec, vec, vec],
        out_shape=[sd((s, D_MODEL), F32), sd((1, D_MODEL), F32), sd((1, D_MODEL), F32), sd((1, D_MODEL), F32)],
        scratch_shapes=[pltpu.VMEM((IN_W, D_MODEL), BF), pltpu.SemaphoreType.DMA((NSEG,))],
        compiler_params=_params(dimension_semantics=("arbitrary",)),
    )(x, dout, norm_w, scale, *dsegs, w_t)


def _gw_seg(h, dseg, name, tm=1024):
    s, w = dseg.shape
    tn = min(w, 1024)
    tm = min(tm, s)
    nm = s // tm

    def body(h_ref, d_ref, o_ref, acc):
        m = pl.program_id(1)

        @pl.when(m == 0)
        def _():
            acc[...] = jnp.zeros_like(acc)

        acc[...] += _dot_tn(d_ref[...], h_ref[...])

        @pl.when(m == nm - 1)
        def _():
            o_ref[...] = _bf(acc[...])

    return pl.pallas_call(
        body, name=name, grid=(w // tn, nm),
        in_specs=[pl.BlockSpec((tm, D_MODEL), lambda n, m: (m, 0)), pl.BlockSpec((tm, tn), lambda n, m: (m, n))],
        out_specs=pl.BlockSpec((tn, D_MODEL), lambda n, m: (n, 0)),
        out_shape=jax.ShapeDtypeStruct((w, D_MODEL), BF),
        scratch_shapes=[pltpu.VMEM((tn, D_MODEL), F32)],
        compiler_params=_params(dimension_semantics=("arbitrary", "arbitrary")),
    )(h, dseg)


def _gw_in(h, dsegs):
    return [_gw_seg(h, d, "gw_in_%d" % j) for j, d in enumerate(dsegs)]


def _local_step(x, tgt, shift, scale, gate, w_t, rows_fn, norm_w, qnw, knw, rel_bias, sinks,
                conv_w, conv_b, dt_bias, a_log, d_skip, ssm_nw, after_mid=None, after_gw=None):
    oh_t = _bucket_onehot_t()
    bias = _masked_bias(_bias_dense(rel_bias.T, oh_t).reshape(ATTN_HEADS, BLOCK, 2 * BLOCK))
    *segs, h = _inproj(x, norm_w, scale, shift, w_t)
    q, kv, zam, xbc, dtr, gab = segs
    consts = _attn_consts(qnw, knw)
    o_att, lse = _attn_fwd(q, kv, bias, sinks, consts)
    e_mat, e3t = _membership(SSM_W, SSM_P, SSM_HEADS)
    dsk_x = jnp.repeat(d_skip, SSM_P, axis=1)
    ypre, hprev, conv = _ssd_fwd(xbc, dtr, conv_w, conv_b, dt_bias, a_log, dsk_x, e3t)
    (dout, d_o, dzam, dyp, dgab, yag, dy_a, yn, dy_b, merged, dob, g_ssm_nw, dgate, loss) = _mid(
        x, tgt, o_att, zam, ypre, gab, gate, ssm_nw, rows_fn(ypre))
    g_wap = _gw_seg(dy_a, yag, "gw_attn_proj")
    g_wsp = _gw_seg(dy_b, yn, "gw_ssm_proj")
    g_wout = _gw_seg(dob, merged, "gw_out")
    zero = after_mid(g_wap, g_wsp, g_wout) if after_mid is not None else 0.0
    dq, dkv, dss, g_qnw, g_knw, g_sinks = _attn_bwd(q, kv, bias, sinks + zero, consts, o_att, lse, d_o)
    g_rel = _bias_grad(dss.reshape(ATTN_HEADS, BLOCK * 2 * BLOCK), oh_t).T
    dxbc, ddt, g_cw, g_cb, g_dtb, g_alog, g_dsk = _ssd_bwd(
        xbc, conv, dtr, conv_w, dt_bias, a_log, dsk_x, e_mat, e3t, hprev, dyp)
    dsegs = (dq, dkv, dzam, dxbc, ddt, dgab)
    g_ws = _gw_in(h, dsegs)
    zero = after_gw(g_ws) if after_gw is not None else 0.0
    gx, dshift, dscale, g_nw = _dh(x, dout, norm_w + zero, scale, dsegs, w_t)
    return dict(loss=loss, grad_x=gx, dmod=jnp.concatenate([dshift, dscale, dgate], axis=1), g_ws=g_ws,
                g_wap=g_wap, g_wsp=g_wsp, g_wout=g_wout, g_norm_w=g_nw, g_qnw=g_qnw, g_knw=g_knw, g_rel=g_rel,
                g_sinks=g_sinks, g_conv_w=g_cw, g_conv_b=g_cb, g_dt_bias=g_dtb, g_a_log=g_alog, g_d_skip=g_dsk,
                g_ssm_nw=g_ssm_nw)


def _me():
    return lax.axis_index("x"), lax.axis_index("y"), lax.axis_index("c")


def _flip(v, bit):
    return 1 - v if bit else v


def _ag_direct(v, name):
    def body(v_ref, out_ref, send_sems, recv_sems, local_sem):
        x, y, c = _me()
        me = 4 * x + 2 * y + c
        mine = pltpu.make_async_copy(v_ref, out_ref.at[me], local_sem)
        mine.start()
        peers = [(_flip(x, k >> 2 & 1), _flip(y, k >> 1 & 1), _flip(c, k & 1)) for k in range(1, N_DEV)]
        sends = [pltpu.make_async_remote_copy(
            src_ref=v_ref, dst_ref=out_ref.at[me], send_sem=send_sems.at[j], recv_sem=recv_sems.at[j],
            device_id=p, device_id_type=MESH) for j, p in enumerate(peers)]
        for cp in sends:
            cp.start()
        for j, (px, py, pc) in enumerate(peers):
            pltpu.make_async_remote_copy(
                src_ref=v_ref, dst_ref=out_ref.at[4 * px + 2 * py + pc], send_sem=send_sems.at[j],
                recv_sem=recv_sems.at[j], device_id=(px, py, pc), device_id_type=MESH).wait_recv()
        for cp in sends:
            cp.wait_send()
        mine.wait()

    vm = pl.BlockSpec(memory_space=pltpu.VMEM)
    return pl.pallas_call(
        body, name=name, out_shape=jax.ShapeDtypeStruct((N_DEV,) + v.shape, v.dtype),
        in_specs=[vm], out_specs=vm,
        scratch_shapes=[pltpu.SemaphoreType.DMA((N_DEV - 1,)), pltpu.SemaphoreType.DMA((N_DEV - 1,)),
                        pltpu.SemaphoreType.DMA],
        compiler_params=_params(),
    )(v)


def _gather_mod(v, w_ada, b_piece):
    ncols = w_ada.shape[1]

    def body(v_ref, w_ref, b_ref, rows_ref, mods_ref, piece, send_sems, recv_sems, local_sems):
        x, y, c = _me()
        me = 4 * x + 2 * y + c
        peers = _peers(x, y, c)

        def exchange(src, dst, rnd):
            mine = pltpu.make_async_copy(src, dst.at[me], local_sems.at[rnd])
            mine.start()
            sends = [pltpu.make_async_remote_copy(
                src_ref=src, dst_ref=dst.at[me], send_sem=send_sems.at[rnd, j], recv_sem=recv_sems.at[rnd, j],
                device_id=p, device_id_type=MESH) for j, p in enumerate(peers)]
            for cp in sends:
                cp.start()
            for j, (px, py, pc) in enumerate(peers):
                pltpu.make_async_remote_copy(
                    src_ref=src, dst_ref=dst.at[4 * px + 2 * py + pc], send_sem=send_sems.at[rnd, j],
                    recv_sem=recv_sems.at[rnd, j], device_id=(px, py, pc), device_id_type=MESH).wait_recv()
            for cp in sends:
                cp.wait_send()
            mine.wait()

        exchange(v_ref, rows_ref, 0)
        c_all = rows_ref[:, 0, :D_MODEL]
        piece[...] = _dot(_bf(_silu(c_all)), _bf(w_ref[...])) + b_ref[...]
        exchange(piece, mods_ref, 1)

    vm = pl.BlockSpec(memory_space=pltpu.VMEM)
    return pl.pallas_call(
        body, name="gather_mod",
        out_shape=(jax.ShapeDtypeStruct((N_DEV,) + v.shape, F32), jax.ShapeDtypeStruct((N_DEV, N_DEV, ncols), F32)),
        in_specs=[vm, vm, vm], out_specs=(vm, vm),
        scratch_shapes=[pltpu.VMEM((N_DEV, ncols), F32), pltpu.SemaphoreType.DMA((2, N_DEV - 1)),
                        pltpu.SemaphoreType.DMA((2, N_DEV - 1)), pltpu.SemaphoreType.DMA((2,))],
        compiler_params=_params(),
    )(v, w_ada, b_piece)


def _ag_relayed(v, name, chunks=1):
    rows = v.shape[0] // chunks
    assert rows * chunks == v.shape[0] and rows % 8 == 0

    def body(v_ref, out_ref, token, send_sems, recv_sems, local_sem):
        token[...] = jnp.zeros_like(token)
        x, y, c = _me()
        flip_x, flip_y = 1 - x, 1 - y
        ax, ay = c * x + (1 - c) * flip_x, c * flip_y + (1 - c) * y
        bx, by = c * flip_x + (1 - c) * x, c * y + (1 - c) * flip_y
        me, sib = (x, y, c), (x, y, 1 - c)
        a, b, dg = (ax, ay, c), (bx, by, c), (flip_x, flip_y, c)
        sa, sb, sdg = (bx, by, 1 - c), (ax, ay, 1 - c), (flip_x, flip_y, 1 - c)

        def piece(ref, k):
            return ref.at[pl.ds(k * rows, rows), :]

        def slot(px, py, pc):
            return out_ref.at[4 * px + 2 * py + pc]

        def copy(n, k, block, to, src=None):
            return pltpu.make_async_remote_copy(
                src_ref=piece(slot(*block) if src is None else src, k), dst_ref=piece(slot(*block), k),
                send_sem=send_sems.at[n * chunks + k], recv_sem=recv_sems.at[n * chunks + k],
                device_id=to, device_id_type=MESH)

        mine = pltpu.make_async_copy(v_ref, slot(*me), local_sem)
        mine.start()
        started = [copy(n, k, me, to, src=v_ref) for k in range(chunks) for n, to in ((1, a), (2, b), (0, sib))]
        for cp in started:
            cp.start()

        def arrived(n, k, block, then):
            copy(n, k, block, me).wait_recv()
            for n2, to in then:
                started.append(copy(n2, k, block, to))
                started[-1].start()

        for k in range(chunks):
            arrived(1, k, a, ((3, b), (4, sib)))
            arrived(2, k, b, ((5, sib),))
        for k in range(chunks):
            arrived(3, k, dg, ((6, sib),))
        for k in range(chunks):
            for n, block in ((0, sib), (4, sa), (5, sb), (6, sdg)):
                copy(n, k, block, me).wait_recv()
        for cp in started:
            cp.wait_send()
        mine.wait()

    out, token = pl.pallas_call(
        body, name=name,
        out_shape=(jax.ShapeDtypeStruct((N_DEV,) + v.shape, v.dtype), jax.ShapeDtypeStruct((8, 128), v.dtype)),
        in_specs=[ANY], out_specs=(ANY, pl.BlockSpec(memory_space=pltpu.VMEM)),
        scratch_shapes=[pltpu.SemaphoreType.DMA((7 * chunks,)), pltpu.SemaphoreType.DMA((7 * chunks,)),
                        pltpu.SemaphoreType.DMA],
        compiler_params=_params(),
    )(v)
    return out, token[0:1, 0:1]


HBM = pl.BlockSpec(memory_space=pltpu.HBM)
SEM = pl.BlockSpec(memory_space=pltpu.SEMAPHORE)
EFFECT = pltpu.SideEffectType.DATAFLOW_SIDE_EFFECTING


def _peers(x, y, c):
    return [(_flip(x, k >> 2 & 1), _flip(y, k >> 1 & 1), _flip(c, k & 1)) for k in range(1, N_DEV)]


def _exchange_start(src, land, gather, name):
    def body(src_ref, land_ref, send_sems, recv_sems, src_thru, land_thru, token):
        x, y, c = _me()
        me = 4 * x + 2 * y + c
        for j, (px, py, pc) in enumerate(_peers(x, y, c)):
            pltpu.make_async_remote_copy(
                src_ref=src_ref if gather else src_ref.at[4 * px + 2 * py + pc], dst_ref=land_ref.at[me],
                send_sem=send_sems.at[j], recv_sem=recv_sems.at[j], device_id=(px, py, pc), device_id_type=MESH).start()
        token[...] = jnp.zeros_like(token)

    sems = pltpu.SemaphoreType.DMA((N_DEV - 1,))
    out = pl.pallas_call(
        body, name=name,
        out_shape=(sems, sems, pltpu.HBM(src.shape, src.dtype), pltpu.HBM(land.shape, land.dtype),
                   jax.ShapeDtypeStruct((8, 128), F32)),
        in_specs=(HBM, HBM), out_specs=(SEM, SEM, HBM, HBM, pl.BlockSpec(memory_space=pltpu.VMEM)),
        input_output_aliases={0: 2, 1: 3},
        compiler_params=pltpu.CompilerParams(has_side_effects=EFFECT),
    )(pltpu.with_memory_space_constraint(src, pltpu.HBM), pltpu.with_memory_space_constraint(land, pltpu.HBM))
    return out[:4], out[4][0, 0]


def _exchange_wait(started, after, gather, name):
    send_sems, recv_sems, src_thru, land_thru = started

    def body(src_ref, land_ref, send_sems, recv_sems, after_ref, src_dead, got_ref):
        x, y, c = _me()
        for j, (px, py, pc) in enumerate(_peers(x, y, c)):
            pid = 4 * px + 2 * py + pc
            cp = pltpu.make_async_remote_copy(
                src_ref=src_ref if gather else src_ref.at[pid], dst_ref=land_ref.at[pid],
                send_sem=send_sems.at[j], recv_sem=recv_sems.at[j], device_id=(px, py, pc), device_id_type=MESH)
            cp.wait_send()
            cp.wait_recv()

    return pl.pallas_call(
        body, name=name,
        out_shape=(pltpu.HBM(src_thru.shape, src_thru.dtype), pltpu.HBM(land_thru.shape, land_thru.dtype)),
        in_specs=(HBM, HBM, SEM, SEM, ANY), out_specs=(HBM, HBM), input_output_aliases={0: 0, 1: 1},
        compiler_params=pltpu.CompilerParams(has_side_effects=EFFECT),
    )(src_thru, land_thru, send_sems, recv_sems, after)[1]


def _silu(a):
    return a * _sig(a)


def _gw_ada(c_all, dmod_piece):
    def body(c_ref, d_ref, o_ref):
        o_ref[...] = _dot_tn(_bf(_silu(c_ref[...])), _bf(d_ref[...]))

    return pl.pallas_call(
        body, name="gw_ada", out_shape=jax.ShapeDtypeStruct((c_all.shape[1], dmod_piece.shape[1]), F32),
        compiler_params=_params(),
    )(c_all, dmod_piece)


def _adam(parts, w, m, v, name):
    k, r, n = parts.shape
    if r <= 256 or r % 256 == 0:
        tr, tn = min(r, 256), n
    else:
        tr, tn = r, 256
    assert r % tr == 0 and n % tn == 0

    def body(p_ref, w_ref, m_ref, v_ref, g_ref, d_ref, nm_ref, nv_ref):
        g = p_ref[0].astype(F32)
        for j in range(1, k):
            g = g + p_ref[j].astype(F32)
        g_ref[...] = g
        d_ref[...], nm_ref[...], nv_ref[...] = _adam_math(g, w_ref[...], m_ref[...], v_ref[...])

    blk = pl.BlockSpec((tr, tn), lambda i, j: (i, j))
    return pl.pallas_call(
        body, name=name, grid=(r // tr, n // tn),
        in_specs=[pl.BlockSpec((k, tr, tn), lambda i, j: (0, i, j)), blk, blk, blk],
        out_specs=[blk, blk, blk, blk],
        out_shape=[jax.ShapeDtypeStruct((r, n), F32)] * 4,
        compiler_params=_params(dimension_semantics=("arbitrary", "arbitrary")),
    )(parts, w, m, v)


def _adam_math(g, w, m, v):
    m_new = ADAM_B1 * m + (1.0 - ADAM_B1) * g
    v_new = ADAM_B2 * v + (1.0 - ADAM_B2) * jnp.square(g)
    m_hat = m_new / (1.0 - ADAM_B1 ** ADAM_STEP)
    v_hat = v_new / (1.0 - ADAM_B2 ** ADAM_STEP)
    return -ADAM_LR * (m_hat / (jnp.sqrt(v_hat) + ADAM_EPS) + ADAM_WD * w), m_new, v_new


_SMALL = (("b_ada", 3 * D_MODEL), ("norm_w", D_MODEL), ("q_norm_w", HEAD_DIM), ("k_norm_w", HEAD_DIM),
          ("rel_bias", REL_BUCKETS * ATTN_HEADS), ("sinks", ATTN_HEADS), ("conv_b", XBC_W), ("dt_bias", SSM_HEADS),
          ("a_log", SSM_HEADS), ("d_skip", SSM_HEADS), ("ssm_norm_w", SSM_W))
_SLOT = tuple(-(-n // 128) * 128 for _, n in _SMALL)
_SLOT_OFF = tuple(int(o) for o in np.cumsum((0,) + _SLOT))
_LOSS_OFF = _SLOT_OFF[-1]
_CW_OFF = _LOSS_OFF + 128
_PACK_N = _CW_OFF + CONV_K * XBC_W


def _pack_partials(small, loss, g_conv_w):
    parts = []
    for (name, n), slot in zip(_SMALL, _SLOT):
        parts.append(small[name].reshape(1, n))
        if slot > n:
            parts.append(jnp.zeros((1, slot - n), F32))
    parts += [loss.reshape(1, 1), jnp.zeros((1, 127), F32), g_conv_w.reshape(1, CONV_K * XBC_W)]
    return jnp.concatenate(parts, axis=1)


def _adam_small(pack_all, w, m, v):
    names = [name for name, _ in _SMALL]

    def body(p_ref, *rest):
        ins, outs = rest[:3 * len(names)], rest[3 * len(names):]

        def total(off, n):
            g = p_ref[0, :, off:off + n]
            for d in range(1, N_DEV):
                g = g + p_ref[d, :, off:off + n]
            return g

        for j, (name, n) in enumerate(_SMALL):
            g = total(_SLOT_OFF[j], n)
            delta, m_new, v_new = _adam_math(g, ins[3 * j][...], ins[3 * j + 1][...], ins[3 * j + 2][...])
            outs[4 * j][...] = g
            outs[4 * j + 1][...] = delta
            outs[4 * j + 2][...] = m_new
            outs[4 * j + 3][...] = v_new
        outs[-1][...] = total(_LOSS_OFF, 1)

    flat = []
    for name, n in _SMALL:
        flat += [w[name].reshape(1, n), m[name].reshape(1, n), v[name].reshape(1, n)]
    out_shape = [jax.ShapeDtypeStruct((1, n), F32) for _, n in _SMALL for _ in range(4)] + [jax.ShapeDtypeStruct((1, 1), F32)]
    out = pl.pallas_call(body, name="adam_small", out_shape=out_shape, compiler_params=_params())(pack_all, *flat)
    res = {name: [out[4 * j + t].reshape(w[name].shape) for t in range(4)] for j, name in enumerate(names)}
    return res, out[-1]


WEIGHTS = ("w_ada", "b_ada", "norm_w", "w_in", "q_norm_w", "k_norm_w", "rel_bias", "sinks", "conv_w", "conv_b",
           "dt_bias", "a_log", "d_skip", "ssm_norm_w", "w_attn_proj", "w_ssm_proj", "w_out")


def kernel(x, c, w_ada, b_ada, norm_w, w_in, q_norm_w, k_norm_w, rel_bias, sinks, conv_w, conv_b, dt_bias, a_log, d_skip, ssm_norm_w, w_attn_proj, w_ssm_proj, w_out, loss_target, m_w_ada, m_b_ada, m_norm_w, m_w_in, m_q_norm_w, m_k_norm_w, m_rel_bias, m_sinks, m_conv_w, m_conv_b, m_dt_bias, m_a_log, m_d_skip, m_ssm_norm_w, m_w_attn_proj, m_w_ssm_proj, m_w_out, v_w_ada, v_b_ada, v_norm_w, v_w_in, v_q_norm_w, v_k_norm_w, v_rel_bias, v_sinks, v_conv_w, v_conv_b, v_dt_bias, v_a_log, v_d_skip, v_ssm_norm_w, v_w_attn_proj, v_w_ssm_proj, v_w_out):
    w = dict(w_ada=w_ada, b_ada=b_ada, norm_w=norm_w, w_in=w_in, q_norm_w=q_norm_w, k_norm_w=k_norm_w,
             rel_bias=rel_bias, sinks=sinks, conv_w=conv_w, conv_b=conv_b, dt_bias=dt_bias, a_log=a_log,
             d_skip=d_skip, ssm_norm_w=ssm_norm_w, w_attn_proj=w_attn_proj, w_ssm_proj=w_ssm_proj, w_out=w_out)
    m = dict(w_ada=m_w_ada, b_ada=m_b_ada, norm_w=m_norm_w, w_in=m_w_in, q_norm_w=m_q_norm_w, k_norm_w=m_k_norm_w,
             rel_bias=m_rel_bias, sinks=m_sinks, conv_w=m_conv_w, conv_b=m_conv_b, dt_bias=m_dt_bias, a_log=m_a_log,
             d_skip=m_d_skip, ssm_norm_w=m_ssm_norm_w, w_attn_proj=m_w_attn_proj, w_ssm_proj=m_w_ssm_proj, w_out=m_w_out)
    v = dict(w_ada=v_w_ada, b_ada=v_b_ada, norm_w=v_norm_w, w_in=v_w_in, q_norm_w=v_q_norm_w, k_norm_w=v_k_norm_w,
             rel_bias=v_rel_bias, sinks=v_sinks, conv_w=v_conv_w, conv_b=v_conv_b, dt_bias=v_dt_bias, a_log=v_a_log,
             d_skip=v_d_skip, ssm_norm_w=v_ssm_norm_w, w_attn_proj=v_w_attn_proj, w_ssm_proj=v_w_ssm_proj, w_out=v_w_out)
    me = 4 * lax.axis_index("x") + 2 * lax.axis_index("y") + lax.axis_index("c")
    ada_n = w_ada.shape[2]
    in_n = w_in.shape[2]
    cw_n = conv_w.shape[2]

    b_piece = lax.dynamic_slice_in_dim(b_ada, me * ada_n, ada_n, axis=1)
    first, mod_all = _gather_mod(jnp.concatenate([c, conv_w[0].reshape(1, CONV_K * cw_n)], axis=1), w_ada[0], b_piece)
    first = first[:, 0]
    c_all = first[:, :D_MODEL]
    conv_w_full = first[:, D_MODEL:].reshape(N_DEV, CONV_K, cw_n).transpose(1, 0, 2).reshape(CONV_K, XBC_W)
    mod = lax.dynamic_index_in_dim(mod_all, me, axis=1, keepdims=False).reshape(1, 3 * D_MODEL)
    shift, scale, gate = mod[:, :D_MODEL], mod[:, D_MODEL:2 * D_MODEL], mod[:, 2 * D_MODEL:]

    pad = -in_n % 24
    w_t, zero = _ag_relayed(jnp.pad(w_in[0].T.astype(BF), ((0, pad), (0, 0))), "ag_w_in", chunks=3)
    w_t = w_t[:, :in_n].reshape(N_DEV * in_n, D_MODEL)

    def with_mine(blocks, mine):
        return lax.dynamic_update_index_in_dim(lax.empty(blocks, mine.dtype), mine, me, axis=0)

    rows = jnp.concatenate([w_attn_proj[0], w_ssm_proj[0], w_out[0]], axis=0).astype(BF) + zero
    r_ap, r_sp = w_attn_proj.shape[1], w_ssm_proj.shape[1]
    rows_started, zero = _exchange_start(rows, with_mine((N_DEV,) + rows.shape, rows), True, "ag_rows_start")

    def rows_fn(after):
        return _exchange_wait(rows_started, after, True, "ag_rows_wait")

    started = {}

    def send_blocks(key, g, name):
        started[key], zero = _exchange_start(
            g, with_mine(g.shape, lax.dynamic_index_in_dim(g, me, axis=0, keepdims=False)), False, name)
        return zero

    def after_mid(g_wap, g_wsp, g_wout):
        return send_blocks("rows", jnp.concatenate(
            [g_wap.reshape(N_DEV, r_ap, D_MODEL), g_wsp.reshape(N_DEV, r_sp, D_MODEL),
             g_wout.reshape(N_DEV, r_ap, D_MODEL)], axis=1), "rs_rows_start")

    def after_gw(g_ws):
        return send_blocks("in", jnp.concatenate(g_ws, axis=0).reshape(N_DEV, in_n, D_MODEL), "rs_in_start")

    r = _local_step(x[0], loss_target[0], shift, scale + zero, gate, w_t, rows_fn, norm_w, q_norm_w, k_norm_w,
                    rel_bias, sinks, conv_w_full, conv_b, dt_bias, a_log, d_skip, ssm_norm_w, after_mid, after_gw)

    small = dict(b_ada=r["dmod"], norm_w=r["g_norm_w"], q_norm_w=r["g_qnw"], k_norm_w=r["g_knw"], rel_bias=r["g_rel"],
                 sinks=r["g_sinks"], conv_b=r["g_conv_b"], dt_bias=r["g_dt_bias"], a_log=r["g_a_log"],
                 d_skip=r["g_d_skip"], ssm_norm_w=r["g_ssm_nw"])
    pack_all = _ag_direct(_pack_partials(small, r["loss"], r["g_conv_w"]), "ag_small")
    res, loss = _adam_small(pack_all, w, m, v)
    loss = loss[0, 0]
    cw_parts = pack_all[:, 0, _CW_OFF:].reshape(N_DEV, CONV_K, XBC_W)
    cw_mine = lax.dynamic_slice_in_dim(cw_parts, me * cw_n, cw_n, axis=2)
    res["conv_w"] = [a[None] for a in _adam(cw_mine, conv_w[0], m_conv_w[0], v_conv_w[0], "adam_conv_w")]

    dmod_piece = lax.dynamic_slice_in_dim(pack_all[:, 0, :3 * D_MODEL], me * ada_n, ada_n, axis=1)
    g_ada = _gw_ada(c_all, dmod_piece)
    res["w_ada"] = [a[None] for a in _adam(g_ada[None], w_ada[0], m_w_ada[0], v_w_ada[0], "adam_w_ada")]

    cat = lambda d: jnp.concatenate([d["w_attn_proj"][0], d["w_ssm_proj"][0], d["w_out"][0]], axis=0)
    rows_res = _adam(_exchange_wait(started["rows"], g_ada, False, "rs_rows_wait"), cat(w), cat(m), cat(v), "adam_w_rows")
    res["w_in"] = [a.T[None] for a in _adam(_exchange_wait(started["in"], rows_res[0], False, "rs_in_wait"),
                                            w_in[0].T, m_w_in[0].T, v_w_in[0].T, "adam_w_in")]
    res["w_attn_proj"] = [a[None, :r_ap] for a in rows_res]
    res["w_ssm_proj"] = [a[None, r_ap:r_ap + r_sp] for a in rows_res]
    res["w_out"] = [a[None, r_ap + r_sp:] for a in rows_res]

    outs = [loss, r["grad_x"][None]]
    for j in range(4):
        outs += [res[name][j] for name in WEIGHTS]
    return tuple(outs)
```

```python
import math

import numpy as np
import jax
import jax.numpy as jnp
from jax import lax
from jax.experimental import pallas as pl
from jax.experimental.pallas import tpu as pltpu

F32 = jnp.float32
BF = jnp.bfloat16
HI = lax.Precision.HIGHEST

D_MODEL = 1024
ATTN_HEADS = 16
KV_HEADS = 4
GRP = ATTN_HEADS // KV_HEADS
HEAD_DIM = 64
ATTN_W = ATTN_HEADS * HEAD_DIM
KV_W = KV_HEADS * HEAD_DIM
BLOCK = 128
REL_BUCKETS = 32
REL_MAX_DIST = 128
SSM_W = 2048
SSM_P = 64
SSM_HEADS = 32
SSM_G = 4
SSM_R = 8
SSM_N = 128
CONV_K = 4
XBC_W = SSM_W + 2 * SSM_G * SSM_N
SEG_W = (ATTN_W, 2 * KV_W, ATTN_W + SSM_W, XBC_W, SSM_HEADS, 2 * D_MODEL)
NSEG = len(SEG_W)
SEG_OFF = tuple(int(v) for v in np.cumsum((0,) + SEG_W))
IN_W = SEG_OFF[-1]
GATE_SEGS = (2, 5)
EPS = 1e-6
N_DEV = 8
ADAM_LR, ADAM_B1, ADAM_B2, ADAM_EPS, ADAM_WD, ADAM_STEP = 0.001, 0.9, 0.999, 1e-08, 0.01, 10
VMEM_LIMIT = 60 * 1024 * 1024
MESH = pl.DeviceIdType.MESH
ANY = pl.BlockSpec(memory_space=pl.ANY)


def _dot(a, b, precision=None):
    return jnp.dot(a, b, preferred_element_type=F32, precision=precision)


def _dot_nt(a, b, precision=None):
    return lax.dot_general(a, b, (((1,), (1,)), ((), ())), preferred_element_type=F32, precision=precision)


def _dot_tn(a, b, precision=None):
    return lax.dot_general(a, b, (((0,), (0,)), ((), ())), preferred_element_type=F32, precision=precision)


def _bf(a):
    return a.astype(BF)


def _sig(a):
    return 0.5 * jnp.tanh(0.5 * a) + 0.5


def _params(**kw):
    return pltpu.CompilerParams(vmem_limit_bytes=VMEM_LIMIT, **kw)


def _full(shape):
    nd = len(shape)
    return pl.BlockSpec(shape, lambda i: (0,) * nd)


def _rows(tm, w):
    return pl.BlockSpec((tm, w), lambda i: (i, 0))


def _inproj(x, norm_w, scale, shift, w_t, tm=256):
    s = x.shape[0]

    def body(x_ref, nw_ref, sc_ref, sh_ref, w_hbm, *rest):
        outs, h_ref, w_vm, sem = rest[:NSEG], rest[NSEG], rest[NSEG + 1], rest[NSEG + 2]
        first = pl.program_id(0) == 0
        cps = [pltpu.make_async_copy(w_hbm.at[SEG_OFF[j]:SEG_OFF[j + 1], :], w_vm.at[SEG_OFF[j]:SEG_OFF[j + 1], :], sem.at[j])
               for j in range(NSEG)]

        def tile(waiting):
            xv = x_ref[...]
            r = lax.rsqrt(jnp.mean(xv * xv, axis=-1, keepdims=True) + EPS)
            h = xv * r * (nw_ref[...] * (1.0 + sc_ref[...])) + sh_ref[...]
            hb = _bf(h)
            h_ref[...] = hb
            for j in range(NSEG):
                if waiting:
                    cps[j].wait()
                outs[j][...] = _dot_nt(hb, w_vm[SEG_OFF[j]:SEG_OFF[j + 1], :]).astype(outs[j].dtype)

        @pl.when(first)
        def _():
            for cp in cps:
                cp.start()
            tile(True)

        @pl.when(jnp.logical_not(first))
        def _():
            tile(False)

    vec = _full((1, D_MODEL))
    return pl.pallas_call(
        body, name="inproj", grid=(s // tm,),
        in_specs=[_rows(tm, D_MODEL), vec, vec, vec, ANY],
        out_specs=[_rows(tm, w) for w in SEG_W] + [_rows(tm, D_MODEL)],
        out_shape=[jax.ShapeDtypeStruct((s, w), BF if j in GATE_SEGS else F32) for j, w in enumerate(SEG_W)]
                  + [jax.ShapeDtypeStruct((s, D_MODEL), BF)],
        scratch_shapes=[pltpu.VMEM((IN_W, D_MODEL), BF), pltpu.SemaphoreType.DMA((NSEG,))],
        compiler_params=_params(dimension_semantics=("arbitrary",)),
    )(x, norm_w, scale, shift, w_t)


def _bucket_onehot_t():
    qi = jnp.arange(BLOCK)[:, None]
    kj = jnp.arange(2 * BLOCK)[None, :]
    dist = qi + BLOCK - kj
    n = jnp.maximum(dist, 0)
    max_exact = REL_BUCKETS // 2
    nf = jnp.maximum(n, 1).astype(F32)
    large = max_exact + (jnp.log(nf / max_exact) / math.log(REL_MAX_DIST / max_exact)
                         * (REL_BUCKETS - max_exact)).astype(jnp.int32)
    large = jnp.minimum(large, REL_BUCKETS - 1)
    bucket = jnp.where(n < max_exact, n, large).reshape(1, BLOCK * 2 * BLOCK)
    return (bucket == jnp.arange(REL_BUCKETS)[:, None]).astype(F32)


def _bias_dense(rel_bias_t, oh_t):
    def body(rb_ref, oh_ref, o_ref):
        o_ref[...] = _dot(rb_ref[...], oh_ref[...], HI)

    return pl.pallas_call(
        body, name="bias_dense", out_shape=jax.ShapeDtypeStruct((ATTN_HEADS, BLOCK * 2 * BLOCK), F32),
        compiler_params=_params(),
    )(rel_bias_t, oh_t)


def _bias_grad(ds_sum, oh_t):
    def body(ds_ref, oh_ref, o_ref):
        o_ref[...] = _dot_nt(ds_ref[...], oh_ref[...], HI)

    return pl.pallas_call(
        body, name="bias_grad", out_shape=jax.ShapeDtypeStruct((ATTN_HEADS, REL_BUCKETS), F32),
        compiler_params=_params(),
    )(ds_sum, oh_t)


def _group_sum(a, e):
    hi = _bf(a)
    return _dot(hi, e) + _dot(_bf(a - hi.astype(F32)), e)


def _group_bcast(a, e3t):
    hi = _bf(a)
    r1 = a - hi.astype(F32)
    mid = _bf(r1)
    return _dot(jnp.concatenate([hi, mid, _bf(r1 - mid.astype(F32))], axis=1), e3t)


def _membership(width, group, ngroups):
    e = (jnp.arange(width)[:, None] // group == jnp.arange(ngroups)[None, :]).astype(BF)
    return e, jnp.tile(e.T, (3, 1))


def _fold(width, group):
    return (jnp.arange(width)[:, None] % group == jnp.arange(group)[None, :]).astype(BF)


def _heads_norm(t, w_x, e, e3t):
    r = lax.rsqrt(_dot(_bf(t * t), e) * (1.0 / HEAD_DIM) + EPS)
    r_x = _group_bcast(r, e3t)
    return t * r_x * w_x, r_x


def _heads_norm_bwd(t, r_x, w_x, d, e, e3t):
    wd = d * w_x
    corr = _group_bcast(_dot(_bf(t * wd), e) * (1.0 / HEAD_DIM), e3t)
    return r_x * wd - t * (r_x * r_x * r_x) * corr, jnp.sum(d * t * r_x, axis=0, keepdims=True)


def _stack_heads(a, hk):
    return jnp.concatenate([a[:, (hk * GRP + g) * HEAD_DIM:(hk * GRP + g + 1) * HEAD_DIM] for g in range(GRP)], axis=0)


def _stack_cols(a, hk):
    return jnp.concatenate([a[:, hk * GRP + g:hk * GRP + g + 1] for g in range(GRP)], axis=0)


def _masked_bias(bias):
    qi = jnp.arange(BLOCK)[:, None]
    kj = jnp.arange(2 * BLOCK)[None, :]
    cur_ok = jnp.logical_and(kj >= BLOCK, kj - BLOCK <= qi)
    both_ok = jnp.logical_or(jnp.logical_and(kj < BLOCK, kj > qi), cur_ok)
    return jnp.stack([jnp.where(cur_ok, bias, -1e30), jnp.where(both_ok, bias, -1e30)])


def _attn_consts(qnw, knw):
    eq, eq3t = _membership(ATTN_W, HEAD_DIM, ATTN_HEADS)
    ek, ek3t = _membership(KV_W, HEAD_DIM, ATTN_HEADS)
    return (jnp.tile(qnw, (1, ATTN_HEADS)), jnp.tile(knw, (1, KV_HEADS)), eq, eq3t, ek, ek3t)


def _attn_fwd(q, kv, bias, sinks, consts):
    s = q.shape[0]
    nb = s // BLOCK
    gq = GRP * BLOCK
    bias_t = bias.reshape(2, KV_HEADS, GRP, BLOCK, 2 * BLOCK).transpose(0, 1, 4, 2, 3).reshape(2, KV_HEADS, 2 * BLOCK, gq)
    sink_rows = jnp.repeat(sinks.reshape(KV_HEADS, GRP), BLOCK, axis=1).reshape(KV_HEADS, 1, gq)
    eye = jnp.eye(BLOCK, dtype=BF)

    def body(q_ref, kp_ref, kc_ref, vp_ref, vc_ref, b_ref, bt_ref, sk_ref, skr_ref, eye_ref,
             qw_ref, kw_ref, eq_ref, eq3_ref, ek_ref, ek3_ref, o_ref, lse_ref):
        qn = _bf(_heads_norm(q_ref[...], qw_ref[...], eq_ref[...], eq3_ref[...])[0] * (HEAD_DIM ** -0.5))
        kn = _bf(_heads_norm(jnp.concatenate([kp_ref[...], kc_ref[...]], axis=0), kw_ref[...], ek_ref[...], ek3_ref[...])[0])
        vv = _bf(jnp.concatenate([vp_ref[...], vc_ref[...]], axis=0))
        ones = jnp.ones((2 * BLOCK, HEAD_DIM), BF)
        kss = [slice(hk * HEAD_DIM, (hk + 1) * HEAD_DIM) for hk in range(KV_HEADS)]
        qgs = [_stack_heads(qn, hk) for hk in range(KV_HEADS)]
        sc_ts = [_dot_nt(kn[:, kss[hk]], qgs[hk]) + bt_ref[0, hk] for hk in range(KV_HEADS)]
        m_rows = [jnp.maximum(jnp.max(sc_ts[hk], axis=0, keepdims=True), skr_ref[hk]) for hk in range(KV_HEADS)]
        m_hq = _bf(jnp.concatenate([(m + jnp.abs(m) * (2.0 ** -7))[:, g * BLOCK:(g + 1) * BLOCK]
                                    for m in m_rows for g in range(GRP)], axis=0))
        m16 = _dot_nt(eye_ref[...], m_hq)
        ms = [_stack_cols(m16, hk) for hk in range(KV_HEADS)]
        scs = [_dot_nt(qgs[hk], kn[:, kss[hk]]) + b_ref[0, hk * GRP:(hk + 1) * GRP].reshape(gq, 2 * BLOCK)
               for hk in range(KV_HEADS)]
        ps = [_bf(jnp.exp(scs[hk] - ms[hk])) for hk in range(KV_HEADS)]
        pvs = [_dot(ps[hk], jnp.concatenate([vv[:, kss[hk]], ones], axis=1)) for hk in range(KV_HEADS)]
        den16 = jnp.concatenate([pvs[hk][g * BLOCK:(g + 1) * BLOCK, HEAD_DIM:HEAD_DIM + 1]
                                 for hk in range(KV_HEADS) for g in range(GRP)], axis=1)
        den16 = den16 + jnp.exp(sk_ref[...] - m16)
        lse_ref[...] = m16 + jnp.log(den16)
        inv16 = 1.0 / den16
        for hk in range(KV_HEADS):
            for g in range(GRP):
                h = hk * GRP + g
                o_ref[:, h * HEAD_DIM:(h + 1) * HEAD_DIM] = (pvs[hk][g * BLOCK:(g + 1) * BLOCK, :HEAD_DIM]
                                                             * inv16[:, h:h + 1])

    cur = lambda w, col=0: pl.BlockSpec((BLOCK, w), lambda i: (i, col))
    prev = lambda w, col=0: pl.BlockSpec((BLOCK, w), lambda i: (jnp.maximum(i - 1, 0), col))
    whole = lambda a: pl.BlockSpec(a.shape, lambda i: (0,) * a.ndim)
    first_or_not = lambda a: pl.BlockSpec((1,) + a.shape[1:], lambda i: (jnp.minimum(i, 1),) + (0,) * (a.ndim - 1))
    return pl.pallas_call(
        body, name="attn_fwd", grid=(nb,),
        in_specs=[cur(ATTN_W), prev(KV_W, 0), cur(KV_W, 0), prev(KV_W, 1), cur(KV_W, 1),
                  first_or_not(bias), first_or_not(bias_t),
                  whole(sinks), whole(sink_rows), whole(eye)] + [_full(c.shape) for c in consts],
        out_specs=[cur(ATTN_W), cur(ATTN_HEADS)],
        out_shape=[jax.ShapeDtypeStruct((s, ATTN_W), F32), jax.ShapeDtypeStruct((s, ATTN_HEADS), F32)],
        compiler_params=_params(dimension_semantics=("arbitrary",)),
    )(q, kv, kv, kv, kv, bias, bias_t, sinks, sink_rows, eye, *consts)


def _conv_taps(xbc, tail):
    ext = jnp.concatenate([tail, xbc], axis=0)
    return [pltpu.roll(ext, CONV_K - 1 - j, axis=0)[8:8 + BLOCK] if j < CONV_K - 1 else xbc for j in range(CONV_K)]


def _softplus(u):
    return jnp.maximum(u, 0.0) + jnp.log(1.0 + jnp.exp(-jnp.abs(u)))


def _tril():
    r = lax.broadcasted_iota(jnp.int32, (BLOCK, BLOCK), 0)
    c = lax.broadcasted_iota(jnp.int32, (BLOCK, BLOCK), 1)
    return r >= c


def _triu():
    r = lax.broadcasted_iota(jnp.int32, (BLOCK, BLOCK), 0)
    c = lax.broadcasted_iota(jnp.int32, (BLOCK, BLOCK), 1)
    return r <= c


def _exact_left(m01, a):
    hi = _bf(a)
    r1 = a - hi.astype(F32)
    mid = _bf(r1)
    return _dot(m01, hi) + _dot(m01, mid) + _dot(m01, _bf(r1 - mid.astype(F32)))


def _ssd_common(conv, dtr, dtb_ref, alog_ref, e3_ref):
    sg = _sig(conv)
    xact = conv * sg
    u = dtr + dtb_ref[...]
    dt = _softplus(u)
    a = -jnp.exp(alog_ref[...])
    trilb = _tril()
    acum = _exact_left(trilb.astype(BF), dt * a) * math.log2(math.e)
    both = _group_bcast(jnp.concatenate([dt, acum], axis=0), e3_ref[...])
    dt_x, acum_x = both[:BLOCK], both[BLOCK:]
    return sg, xact, u, dt, a, trilb, acum, dt_x, acum_x


SSD_CH = 2


def _ssd_fwd(xbc, dt_raw, conv_w, conv_b, dt_bias, a_log, dsk_x, e3t):
    s = xbc.shape[0]
    nc = s // BLOCK
    ch = SSD_CH if nc % SSD_CH == 0 else 1
    rows = ch * BLOCK

    def body(x_ref, tail_ref, dtr_ref, cw_ref, cb_ref, dtb_ref, alog_ref, dsk_ref, e3_ref,
             y_ref, hp_ref, conv_ref, hst, yd_s, yoff_s):
        i = pl.program_id(0)

        @pl.when(i == 0)
        def _():
            hst[...] = jnp.zeros_like(hst)

        for j in range(ch):
            rs = slice(j * BLOCK, (j + 1) * BLOCK)
            tail = jnp.where(i > 0, tail_ref[...], 0.0) if j == 0 else x_ref[j * BLOCK - 8:j * BLOCK, :]
            taps = _conv_taps(x_ref[rs, :], tail)
            conv = cb_ref[...] + sum(taps[t] * cw_ref[t:t + 1, :] for t in range(CONV_K))
            conv_ref[rs, :] = conv
            _, xact, _, _, _, trilb, acum, dt_x, acum_x = _ssd_common(conv, dtr_ref[rs, :], dtb_ref, alog_ref, e3_ref)
            xs = xact[:, :SSM_W]
            acum_t = acum.T
            ea_x = jnp.exp2(acum_x)
            last_x = acum_x[BLOCK - 1:BLOCK, :]
            xdt = xs * dt_x
            xw = xdt * jnp.exp2(last_x - acum_x)
            cd_x = jnp.exp2(last_x)
            hprev = hst[...]
            hp_ref[j] = hprev
            sls = [slice(g * SSM_R * SSM_P, (g + 1) * SSM_R * SSM_P) for g in range(SSM_G)]
            bgs = [_bf(xact[:, SSM_W + g * SSM_N:SSM_W + (g + 1) * SSM_N]) for g in range(SSM_G)]
            cgs = [_bf(xact[:, SSM_W + SSM_G * SSM_N + g * SSM_N:SSM_W + SSM_G * SSM_N + (g + 1) * SSM_N])
                   for g in range(SSM_G)]
            xdt_b, xw_b, hprev_b = _bf(xdt), _bf(xw), _bf(hprev)
            low_half = lax.broadcasted_iota(jnp.int32, (BLOCK, 2 * SSM_P), 1) < SSM_P
            cbs = [_dot_nt(cgs[g], bgs[g]) for g in range(SSM_G)]
            for g in range(SSM_G):
                sl = sls[g]
                yoff_s[:, sl] = _dot(cgs[g], hprev_b[:, sl]) * ea_x[:, sl]
                hst[:, sl] = hprev[:, sl] * cd_x[:, sl] + _dot_tn(bgs[g], xw_b[:, sl])
            for g in range(SSM_G):
                hss = [slice((g * SSM_R + r) * SSM_P, (g * SSM_R + r + 1) * SSM_P) for r in range(SSM_R)]
                mms = [_bf(cbs[g] * jnp.exp2(jnp.where(trilb, acum[:, g * SSM_R + r:g * SSM_R + r + 1]
                                                      - acum_t[g * SSM_R + r:g * SSM_R + r + 1, :], -1e30)))
                       for r in range(SSM_R)]
                for r in range(0, SSM_R, 2):
                    pair = slice(hss[r].start, hss[r + 1].stop)
                    xp = xdt_b[:, pair]
                    rhs = jnp.concatenate([jnp.where(low_half, xp, 0), jnp.where(low_half, 0, xp)], axis=0)
                    yd_s[:, pair] = _dot(jnp.concatenate([mms[r], mms[r + 1]], axis=1), rhs)
            y_ref[rs, :] = yd_s[...] + yoff_s[...] + dsk_ref[...] * xs

    blk = lambda w: pl.BlockSpec((rows, w), lambda i: (i, 0))
    return pl.pallas_call(
        body, name="ssd_fwd", grid=(nc // ch,),
        in_specs=[blk(XBC_W), pl.BlockSpec((8, XBC_W), lambda i: (jnp.maximum(i * (rows // 8) - 1, 0), 0)),
                  blk(SSM_HEADS), _full((CONV_K, XBC_W)), _full((1, XBC_W)), _full((1, SSM_HEADS)),
                  _full((1, SSM_HEADS)), _full((1, SSM_W)), _full((3 * SSM_HEADS, SSM_W))],
        out_specs=[blk(SSM_W), pl.BlockSpec((ch, SSM_N, SSM_W), lambda i: (i, 0, 0)), blk(XBC_W)],
        out_shape=[jax.ShapeDtypeStruct((s, SSM_W), F32), jax.ShapeDtypeStruct((nc, SSM_N, SSM_W), F32),
                   jax.ShapeDtypeStruct((s, XBC_W), F32)],
        scratch_shapes=[pltpu.VMEM((SSM_N, SSM_W), F32), pltpu.VMEM((BLOCK, SSM_W), F32), pltpu.VMEM((BLOCK, SSM_W), F32)],
        compiler_params=_params(dimension_semantics=("arbitrary",)),
    )(xbc, xbc, dt_raw, conv_w, conv_b, dt_bias, a_log, dsk_x, e3t)


def _dsilu(z, sg, silu):
    return sg * (1.0 + (z - silu))


def _mid(x, tgt, o_att, zam, ypre, gab, gate, ssm_nw, rows_all, tm=256):
    s = x.shape[0]
    gw = SSM_W // SSM_G

    r_ap, r_sp = ATTN_W // N_DEV, SSM_W // N_DEV

    def body(x_ref, t_ref, o_ref, zam_ref, yp_ref, gab_ref, gate_ref, nw_ref, rows_h,
             dout_ref, do_ref, dzam_ref, dyp_ref, dgab_ref,
             yag_ref, dya_ref, yn_ref, dyb_ref, mg_ref, dob_ref, gnw_ref, dgate_ref, loss_ref,
             wap_v, wsp_v, wout_v, sem):
        i = pl.program_id(0)

        @pl.when(i == 0)
        def _():
            cps = []
            for d in range(N_DEV):
                for j, (dst, r0, rn) in enumerate(((wap_v, 0, r_ap), (wsp_v, r_ap, r_sp), (wout_v, r_ap + r_sp, r_ap))):
                    cps.append(pltpu.make_async_copy(rows_h.at[d, r0:r0 + rn, :], dst.at[d * rn:(d + 1) * rn, :], sem.at[j]))
            for cp in cps:
                cp.start()
            gnw_ref[...] = jnp.zeros_like(gnw_ref)
            dgate_ref[...] = jnp.zeros_like(dgate_ref)
            loss_ref[...] = jnp.zeros_like(loss_ref)
            for cp in cps:
                cp.wait()

        gate = gate_ref[...]
        nw = nw_ref[...]
        o_att = o_ref[...]
        z_a = zam_ref[:, :ATTN_W].astype(F32)
        s_a = _sig(z_a)
        silu_a = z_a * s_a
        yag = _bf(o_att * silu_a)
        yag_ref[...] = yag
        ypre = yp_ref[...]
        z_m = zam_ref[:, ATTN_W:].astype(F32)
        s_m = _sig(z_m)
        silu_m = z_m * s_m
        yg = ypre * silu_m
        rinv = jnp.concatenate(
            [jnp.broadcast_to(lax.rsqrt(jnp.mean(yg[:, g * gw:(g + 1) * gw] ** 2, axis=-1, keepdims=True) + EPS), (tm, gw))
             for g in range(SSM_G)], axis=1)
        ynr = yg * rinv
        yn = _bf(ynr * nw)
        yn_ref[...] = yn
        y_a = _dot(yag, wap_v[...])
        y_b = _dot(yn, wsp_v[...])
        g_a = _sig(gab_ref[:, :D_MODEL].astype(F32))
        g_b = _sig(gab_ref[:, D_MODEL:].astype(F32))
        merged = _bf(g_a * y_a + g_b * y_b)
        mg_ref[...] = merged
        o = _dot(merged, wout_v[...])
        diff = x_ref[...] + gate * o - t_ref[...]
        loss_ref[...] += (0.5 / D_MODEL) * jnp.sum(diff * diff, axis=(0, 1), keepdims=True)
        dout = diff * (1.0 / D_MODEL)
        dout_ref[...] = dout
        dgate_ref[...] += jnp.sum(dout * o, axis=0, keepdims=True)
        d_o = _bf(dout * gate)
        dob_ref[...] = d_o
        dmerged = _dot_nt(d_o, wout_v[...])
        dy_af = dmerged * g_a
        dy_bf = dmerged * g_b
        dy_a = _bf(dy_af)
        dy_b = _bf(dy_bf)
        dya_ref[...] = dy_a
        dyb_ref[...] = dy_b
        dyag = _dot_nt(dy_a, wap_v[...])
        dyn = _dot_nt(dy_b, wsp_v[...])
        dgab_ref[:, :D_MODEL] = _bf(dy_af * y_a * (1.0 - g_a))
        dgab_ref[:, D_MODEL:] = _bf(dy_bf * y_b * (1.0 - g_b))
        do_ref[...] = dyag * silu_a
        dzam_ref[:, :ATTN_W] = _bf(dyag * o_att * _dsilu(z_a, s_a, silu_a))
        gnw_ref[...] += jnp.sum(dyn * ynr, axis=0, keepdims=True)
        dynw = dyn * nw
        corr = jnp.concatenate(
            [jnp.broadcast_to(jnp.mean((dynw * ynr)[:, g * gw:(g + 1) * gw], axis=-1, keepdims=True), (tm, gw))
             for g in range(SSM_G)], axis=1)
        dyg = rinv * (dynw - ynr * corr)
        dyp_ref[...] = dyg * silu_m
        dzam_ref[:, ATTN_W:] = _bf(dyg * ypre * _dsilu(z_m, s_m, silu_m))

    r1, r2, r3 = _rows(tm, D_MODEL), _rows(tm, SSM_W), _rows(tm, ATTN_W + SSM_W)
    sd = jax.ShapeDtypeStruct
    return pl.pallas_call(
        body, name="mid", grid=(s // tm,),
        in_specs=[r1, r1, r1, r3, r2, r2, _full((1, D_MODEL)), _full((1, SSM_W)), ANY],
        out_specs=[r1, r1, r3, r2, r2, r1, r1, r2, r1, r1, r1,
                   _full((1, SSM_W)), _full((1, D_MODEL)), _full((1, 1))],
        out_shape=[sd((s, D_MODEL), F32), sd((s, ATTN_W), F32), sd((s, ATTN_W + SSM_W), BF), sd((s, SSM_W), F32),
                   sd((s, 2 * D_MODEL), BF),
                   sd((s, ATTN_W), BF), sd((s, D_MODEL), BF), sd((s, SSM_W), BF), sd((s, D_MODEL), BF),
                   sd((s, D_MODEL), BF), sd((s, D_MODEL), BF),
                   sd((1, SSM_W), F32), sd((1, D_MODEL), F32), sd((1, 1), F32)],
        scratch_shapes=[pltpu.VMEM((ATTN_W, D_MODEL), BF), pltpu.VMEM((SSM_W, D_MODEL), BF), pltpu.VMEM((D_MODEL, D_MODEL), BF),
                        pltpu.SemaphoreType.DMA((3,))],
        compiler_params=_params(dimension_semantics=("arbitrary",)),
    )(x, tgt, o_att, zam, ypre, gab, gate, ssm_nw, rows_all)


def _attn_bwd(q, kv, bias, sinks, consts, o_att, lse, d_o):
    s = q.shape[0]
    nb = s // BLOCK
    folds = (_fold(ATTN_W, HEAD_DIM), _fold(KV_W, HEAD_DIM))

    def body(q_ref, kp_ref, kc_ref, vp_ref, vc_ref, b_ref, skv_ref, qw_ref, kw_ref, eq_ref, eq3_ref, ek_ref, ek3_ref,
             fq_ref, fk_ref, o_ref, lse_ref, do_ref,
             dq_ref, dkv_ref, dss_ref, gqw_ref, gkw_ref, gsk_ref, ckn, cv, dqn_s, dkn_s, dv_s, gq_x, gk_x):
        i = pl.program_id(0)
        kw, ek, ek3 = kw_ref[...], ek_ref[...], ek3_ref[...]

        @pl.when(i == 0)
        def _():
            for ref in (ckn, cv, dss_ref, gq_x, gk_x, gsk_ref):
                ref[...] = jnp.zeros_like(ref)

        @pl.when(i < nb)
        def _():
            qw, eq, eq3 = qw_ref[...], eq_ref[...], eq3_ref[...]
            qf = q_ref[...]
            qnf, rq_x = _heads_norm(qf, qw, eq, eq3)
            qn = _bf(qnf * (HEAD_DIM ** -0.5))
            kf = jnp.concatenate([kp_ref[...], kc_ref[...]], axis=0)
            knf, rk_x = _heads_norm(kf, kw, ek, ek3)
            kn = _bf(knf)
            vv = _bf(jnp.concatenate([vp_ref[...], vc_ref[...]], axis=0))
            d_of = do_ref[...]
            d_ob = _bf(d_of)
            lse_all = lse_ref[...]
            delta = _dot(_bf(d_of * o_ref[...]), eq)
            gsk_ref[...] += jnp.sum(-jnp.exp(skv_ref[...] - lse_all) * delta, axis=0, keepdims=True)
            kss = [slice(hk * HEAD_DIM, (hk + 1) * HEAD_DIM) for hk in range(KV_HEADS)]
            qgs = [_stack_heads(qn, hk) for hk in range(KV_HEADS)]
            d_ogs = [_stack_heads(d_ob, hk) for hk in range(KV_HEADS)]
            scs = [_dot_nt(qgs[hk], kn[:, kss[hk]]) + b_ref[0, hk * GRP:(hk + 1) * GRP].reshape(GRP * BLOCK, 2 * BLOCK)
                   for hk in range(KV_HEADS)]
            dps = [_dot_nt(d_ogs[hk], vv[:, kss[hk]]) for hk in range(KV_HEADS)]
            ps = [jnp.exp(scs[hk] - _stack_cols(lse_all, hk)) for hk in range(KV_HEADS)]
            dss = [ps[hk] * (dps[hk] - _stack_cols(delta, hk)) for hk in range(KV_HEADS)]
            pbs = [_bf(p) for p in ps]
            dsbs = [_bf(ds) for ds in dss]
            for hk in range(KV_HEADS):
                dss_ref[hk * GRP:(hk + 1) * GRP] += dss[hk].reshape(GRP, BLOCK, 2 * BLOCK)
            for hk in range(KV_HEADS):
                dv_s[:, kss[hk]] = _dot_tn(pbs[hk], d_ogs[hk])
                dkn_s[:, kss[hk]] = _dot_tn(dsbs[hk], qgs[hk])
            dqns = [_dot(dsbs[hk], kn[:, kss[hk]]) * (HEAD_DIM ** -0.5) for hk in range(KV_HEADS)]
            for hk in range(KV_HEADS):
                for g in range(GRP):
                    h = hk * GRP + g
                    dqn_s[:, h * HEAD_DIM:(h + 1) * HEAD_DIM] = dqns[hk][g * BLOCK:(g + 1) * BLOCK]
            dq, gq = _heads_norm_bwd(qf, rq_x, qw, dqn_s[...], eq, eq3)
            dq_ref[...] = _bf(dq)
            gq_x[...] += gq
            dk, gk = _heads_norm_bwd(kf[:BLOCK], rk_x[:BLOCK], kw, ckn[...] + dkn_s[0:BLOCK, :], ek, ek3)
            dkv_ref[:, :KV_W] = _bf(dk)
            gk_x[...] += gk
            dkv_ref[:, KV_W:] = _bf(cv[...] + dv_s[0:BLOCK, :])
            ckn[...] = dkn_s[BLOCK:2 * BLOCK, :]
            cv[...] = dv_s[BLOCK:2 * BLOCK, :]

        @pl.when(i == nb)
        def _():
            kc = kc_ref[...]
            dk, gk = _heads_norm_bwd(kc, _heads_norm(kc, kw, ek, ek3)[1], kw, ckn[...], ek, ek3)
            dkv_ref[:, :KV_W] = _bf(dk)
            dkv_ref[:, KV_W:] = _bf(cv[...])
            gqw_ref[...] = _group_sum(jnp.broadcast_to(gq_x[...], (8, ATTN_W)), fq_ref[...])[0:1]
            gkw_ref[...] = _group_sum(jnp.broadcast_to(gk_x[...] + gk, (8, KV_W)), fk_ref[...])[0:1]

    last = nb - 1
    cur = lambda w, col=0: pl.BlockSpec((BLOCK, w), lambda i: (jnp.minimum(i, last), col))
    prev = lambda w, col=0: pl.BlockSpec((BLOCK, w), lambda i: (jnp.maximum(jnp.minimum(i, last) - 1, 0), col))
    late = lambda w: pl.BlockSpec((BLOCK, w), lambda i: (jnp.maximum(i - 1, 0), 0))
    sd = jax.ShapeDtypeStruct
    return pl.pallas_call(
        body, name="attn_bwd", grid=(nb + 1,),
        in_specs=[cur(ATTN_W), prev(KV_W, 0), cur(KV_W, 0), prev(KV_W, 1), cur(KV_W, 1),
                  pl.BlockSpec((1, ATTN_HEADS, BLOCK, 2 * BLOCK), lambda i: (jnp.minimum(i, 1), 0, 0, 0)),
                  _full((1, ATTN_HEADS))]
                 + [_full(c.shape) for c in consts + folds] + [cur(ATTN_W), cur(ATTN_HEADS), cur(ATTN_W)],
        out_specs=[cur(ATTN_W), late(2 * KV_W),
                   pl.BlockSpec((ATTN_HEADS, BLOCK, 2 * BLOCK), lambda i: (0, 0, 0)),
                   _full((1, HEAD_DIM)), _full((1, HEAD_DIM)), _full((1, ATTN_HEADS))],
        out_shape=[sd((s, ATTN_W), BF), sd((s, 2 * KV_W), BF),
                   sd((ATTN_HEADS, BLOCK, 2 * BLOCK), F32), sd((1, HEAD_DIM), F32), sd((1, HEAD_DIM), F32),
                   sd((1, ATTN_HEADS), F32)],
        scratch_shapes=[pltpu.VMEM((BLOCK, KV_W), F32), pltpu.VMEM((BLOCK, KV_W), F32),
                        pltpu.VMEM((BLOCK, ATTN_W), F32), pltpu.VMEM((2 * BLOCK, KV_W), F32),
                        pltpu.VMEM((2 * BLOCK, KV_W), F32), pltpu.VMEM((1, ATTN_W), F32), pltpu.VMEM((1, KV_W), F32)],
        compiler_params=_params(dimension_semantics=("arbitrary",)),
    )(q, kv, kv, kv, kv, bias, sinks, *consts, *folds, o_att, lse, d_o)


def _ssd_bwd(xbc, conv_all, dt_raw, conv_w, dt_bias, a_log, dsk_x, e_mat, e3t, hprev_all, dy_all):
    s = xbc.shape[0]
    nc = s // BLOCK
    ch = 1
    rows = ch * BLOCK
    nsteps = nc // ch
    gw = SSM_R * SSM_P
    b0, c0 = SSM_W, SSM_W + SSM_G * SSM_N

    def body(x_ref, conv_ref, dtr_ref, cw_ref, dtb_ref, alog_ref, dsk_ref, e_ref, e3_ref, hp_ref, dy_ref,
             dx_ref, ddt_ref, gcw_ref, gcb_ref, gdtb_ref, galog_ref, gdsk_ref,
             dh, nhead, gdskx, dxdt_s, dbc_s, dxd_s):
        def chunk_bwd(j):
            rs = slice(j * BLOCK, (j + 1) * BLOCK)
            conv = conv_ref[rs, :]
            sg, xact, u, dt, a, trilb, acum, dt_x, acum_x = _ssd_common(conv, dtr_ref[rs, :], dtb_ref, alog_ref, e3_ref)
            xs = xact[:, :SSM_W]
            acum_t = acum.T
            ea_x = jnp.exp2(acum_x)
            last_x = acum_x[BLOCK - 1:BLOCK, :]
            dte_x = jnp.exp2(last_x - acum_x)
            cd_x = jnp.exp2(last_x)
            xdt = xs * dt_x
            xw = xdt * dte_x
            hprev = hp_ref[j]
            dhn = dh[...]
            dy = dy_ref[rs, :]
            gdskx[...] += jnp.sum(dy * xs, axis=0, keepdims=True)
            dyea = dy * ea_x
            lane = lax.broadcasted_iota(jnp.int32, (BLOCK, SSM_HEADS), 1)
            dacum = jnp.zeros((BLOCK, SSM_HEADS), F32)
            dacc_x, dlast_x = [], []
            sls = [slice(g * gw, (g + 1) * gw) for g in range(SSM_G)]
            bgs = [_bf(xact[:, b0 + g * SSM_N:b0 + (g + 1) * SSM_N]) for g in range(SSM_G)]
            cgs = [_bf(xact[:, c0 + g * SSM_N:c0 + (g + 1) * SSM_N]) for g in range(SSM_G)]
            hpgs = [_bf(hprev[:, sl]) for sl in sls]
            dhgs = [_bf(dhn[:, sl]) for sl in sls]
            dyeags = [_bf(dyea[:, sl]) for sl in sls]
            xwgs = [_bf(xw[:, sl]) for sl in sls]
            xdt_b, dy_b = _bf(xdt), _bf(dy)
            low_half = lax.broadcasted_iota(jnp.int32, (BLOCK, 2 * SSM_P), 1) < SSM_P
            cbs = [_dot_nt(cgs[g], bgs[g]) for g in range(SSM_G)]
            gmats = [_dot(cgs[g], hpgs[g]) for g in range(SSM_G)]
            dxws = [_dot(bgs[g], dhgs[g]) for g in range(SSM_G)]
            dcgs = [_dot_nt(dyeags[g], hpgs[g]) for g in range(SSM_G)]
            dbgs = [_dot_nt(xwgs[g], dhgs[g]) for g in range(SSM_G)]
            for g in range(SSM_G):
                sl = sls[g]
                dh[:, sl] = dhn[:, sl] * cd_x[:, sl] + _dot_tn(cgs[g], dyeags[g])
                dxdt_s[:, sl] = dxws[g] * dte_x[:, sl]
                dacc_x.append(dy[:, sl] * gmats[g] * ea_x[:, sl] - dxws[g] * xw[:, sl])
                dlast_x.append(jnp.sum(dxws[g] * xw[:, sl], axis=0, keepdims=True)
                               + jnp.sum(dhn[:, sl] * hprev[:, sl], axis=0, keepdims=True) * cd_x[:, sl])
            for g in range(SSM_G):
                bg, cg, cb, dbg, dcg = bgs[g], cgs[g], cbs[g], dbgs[g], dcgs[g]
                hss = [slice((g * SSM_R + r) * SSM_P, (g * SSM_R + r + 1) * SSM_P) for r in range(SSM_R)]
                lms = [jnp.exp2(jnp.where(trilb, acum[:, g * SSM_R + r:g * SSM_R + r + 1]
                                         - acum_t[g * SSM_R + r:g * SSM_R + r + 1, :], -1e30)) for r in range(SSM_R)]
                mms = [cb * lm for lm in lms]
                mmbs = [_bf(mm) for mm in mms]
                dms = []
                for r in range(0, SSM_R, 2):
                    pair = slice(hss[r].start, hss[r + 1].stop)
                    xp, dyp = xdt_b[:, pair], dy_b[:, pair]
                    dmp = _dot_nt(dyp, jnp.concatenate([jnp.where(low_half, xp, 0), jnp.where(low_half, 0, xp)], axis=0))
                    dms += [dmp[:, :BLOCK], dmp[:, BLOCK:]]
                    dxd_s[:, pair] = _dot_tn(jnp.concatenate([mmbs[r], mmbs[r + 1]], axis=0),
                                             jnp.concatenate([jnp.where(low_half, dyp, 0), jnp.where(low_half, 0, dyp)], axis=0))
                dcb = sum(dms[r] * lms[r] for r in range(SSM_R))
                wms = [dms[r] * mms[r] for r in range(SSM_R)]
                antis = [_bf(wm - wm.T) for wm in wms]
                for r in range(SSM_R):
                    dacum = dacum + _dot(antis[r], (lane == g * SSM_R + r).astype(BF))
                dcbb = _bf(dcb)
                dbc_s[:, g * SSM_N:(g + 1) * SSM_N] = dbg + _dot_tn(dcbb, cg)
                dbc_s[:, SSM_G * SSM_N + g * SSM_N:SSM_G * SSM_N + (g + 1) * SSM_N] = dcg + _dot(dcbb, bg)
            dxdt = dxdt_s[...] + dxd_s[...]
            dxs = dy * dsk_ref[...] + dxdt * dt_x
            red = _group_sum(jnp.concatenate(
                [dxdt * xs, jnp.concatenate(dacc_x, axis=1),
                 jnp.broadcast_to(jnp.concatenate(dlast_x, axis=1), (8, SSM_W))], axis=0), e_ref[...])
            row = lax.broadcasted_iota(jnp.int32, (BLOCK, SSM_HEADS), 0)
            dacum = dacum + red[BLOCK:2 * BLOCK] + jnp.where(row == BLOCK - 1, red[2 * BLOCK:2 * BLOCK + 1], 0.0)
            ddta = _exact_left(_triu().astype(BF), dacum)
            ddt = red[:BLOCK] + ddta * a
            galog_ref[...] += jnp.sum(ddta * dt, axis=0, keepdims=True) * a
            du = ddt * _sig(u)
            ddt_ref[rs, :] = _bf(du)
            gdtb_ref[...] += jnp.sum(du, axis=0, keepdims=True)
            dconv = jnp.concatenate([dxs, dbc_s[...]], axis=1) * _dsilu(conv, sg, xact)
            gcb_ref[...] += jnp.sum(dconv, axis=0, keepdims=True)
            ext2 = jnp.concatenate([dconv, nhead[...]], axis=0)
            ahead = [pltpu.roll(ext2, BLOCK + 8 - (CONV_K - 1 - j), axis=0)[0:BLOCK] if j < CONV_K - 1 else dconv
                     for j in range(CONV_K)]
            dx_ref[rs, :] = _bf(sum(ahead[j] * cw_ref[j:j + 1, :] for j in range(CONV_K)))
            xraw = x_ref[rs, :]
            gcw_ref[...] += jnp.concatenate([jnp.sum(ahead[j] * xraw, axis=0, keepdims=True) for j in range(CONV_K)], axis=0)
            nhead[...] = dconv[0:8]

        i = pl.program_id(0)

        @pl.when(i == 0)
        def _():
            for ref in (dh, nhead, gdskx, gcw_ref, gcb_ref, gdtb_ref, galog_ref, gdsk_ref):
                ref[...] = jnp.zeros_like(ref)

        for j in reversed(range(ch)):
            chunk_bwd(j)

        @pl.when(i == nsteps - 1)
        def _():
            gdsk_ref[...] = _group_sum(jnp.broadcast_to(gdskx[...], (8, SSM_W)), e_ref[...])[0:1]

    chunk = lambda w: pl.BlockSpec((rows, w), lambda i: (nsteps - 1 - i, 0))
    sd = jax.ShapeDtypeStruct
    return pl.pallas_call(
        body, name="ssd_bwd", grid=(nsteps,),
        in_specs=[chunk(XBC_W), chunk(XBC_W),
                  chunk(SSM_HEADS), _full((CONV_K, XBC_W)), _full((1, SSM_HEADS)),
                  _full((1, SSM_HEADS)), _full((1, SSM_W)), _full((SSM_W, SSM_HEADS)), _full((3 * SSM_HEADS, SSM_W)),
                  pl.BlockSpec((ch, SSM_N, SSM_W), lambda i: (nsteps - 1 - i, 0, 0)), chunk(SSM_W)],
        out_specs=[chunk(XBC_W), chunk(SSM_HEADS), _full((CONV_K, XBC_W)), _full((1, XBC_W)),
                   _full((1, SSM_HEADS)), _full((1, SSM_HEADS)), _full((1, SSM_HEADS))],
        out_shape=[sd((s, XBC_W), BF), sd((s, SSM_HEADS), BF), sd((CONV_K, XBC_W), F32), sd((1, XBC_W), F32),
                   sd((1, SSM_HEADS), F32), sd((1, SSM_HEADS), F32), sd((1, SSM_HEADS), F32)],
        scratch_shapes=[pltpu.VMEM((SSM_N, SSM_W), F32), pltpu.VMEM((8, XBC_W), F32),
                        pltpu.VMEM((1, SSM_W), F32), pltpu.VMEM((BLOCK, SSM_W), F32),
                        pltpu.VMEM((BLOCK, 2 * SSM_G * SSM_N), F32), pltpu.VMEM((BLOCK, SSM_W), F32)],
        compiler_params=_params(dimension_semantics=("arbitrary",)),
    )(xbc, conv_all, dt_raw, conv_w, dt_bias, a_log, dsk_x, e_mat, e3t, hprev_all, dy_all)


def _dh(x, dout, norm_w, scale, dsegs, w_t, tm=256):
    s = x.shape[0]

    def body(x_ref, dout_ref, nw_ref, sc_ref, *rest):
        d_refs, w_hbm = rest[:NSEG], rest[NSEG]
        gx_ref, dshift_ref, dscale_ref, gnw_ref = rest[NSEG + 1:NSEG + 5]
        w_vm, sem = rest[NSEG + 5], rest[NSEG + 6]
        first = pl.program_id(0) == 0
        cps = [pltpu.make_async_copy(w_hbm.at[SEG_OFF[j]:SEG_OFF[j + 1], :], w_vm.at[SEG_OFF[j]:SEG_OFF[j + 1], :], sem.at[j])
               for j in range(NSEG)]

        def tile(waiting):
            dh = None
            for j in range(NSEG):
                if waiting:
                    cps[j].wait()
                part = _dot(d_refs[j][...], w_vm[SEG_OFF[j]:SEG_OFF[j + 1], :])
                dh = part if dh is None else dh + part
            xv = x_ref[...]
            r = lax.rsqrt(jnp.mean(xv * xv, axis=-1, keepdims=True) + EPS)
            xn = xv * r
            nw = nw_ref[...]
            sc1 = 1.0 + sc_ref[...]
            dshift_ref[...] += jnp.sum(dh, axis=0, keepdims=True)
            dhxn = jnp.sum(dh * xn, axis=0, keepdims=True)
            dscale_ref[...] += dhxn * nw
            gnw_ref[...] += dhxn * sc1
            dxn = dh * (nw * sc1)
            gx_ref[...] = dout_ref[...] + r * (dxn - xn * jnp.mean(xn * dxn, axis=-1, keepdims=True))

        @pl.when(first)
        def _():
            for cp in cps:
                cp.start()
            for ref in (dshift_ref, dscale_ref, gnw_ref):
                ref[...] = jnp.zeros_like(ref)
            tile(True)

        @pl.when(jnp.logical_not(first))
        def _():
            tile(False)

    vec = _full((1, D_MODEL))
    sd = jax.ShapeDtypeStruct
    return pl.pallas_call(
        body, name="dh", grid=(s // tm,),
        in_specs=[_rows(tm, D_MODEL), _rows(tm, D_MODEL), vec, vec] + [_rows(tm, w) for w in SEG_W] + [ANY],
        out_specs=[_rows(tm, D_MODEL), vec, vec, vec],
        out_shape=[sd((s, D_MODEL), F32), sd((1, D_MODEL), F32), sd((1, D_MODEL), F32), sd((1, D_MODEL), F32)],
        scratch_shapes=[pltpu.VMEM((IN_W, D_MODEL), BF), pltpu.SemaphoreType.DMA((NSEG,))],
        compiler_params=_params(dimension_semantics=("arbitrary",)),
    )(x, dout, norm_w, scale, *dsegs, w_t)


def _gw_seg(h, dseg, name, tm=1024):
    s, w = dseg.shape
    tn = w if w <= 2048 else w // 2
    tm = min(tm, s)
    nm = s // tm

    def body(h_ref, d_ref, o_ref, acc):
        m = pl.program_id(1)

        @pl.when(m == 0)
        def _():
            acc[...] = jnp.zeros_like(acc)

        acc[...] += _dot_tn(d_ref[...], h_ref[...])

        @pl.when(m == nm - 1)
        def _():
            o_ref[...] = _bf(acc[...])

    return pl.pallas_call(
        body, name=name, grid=(w // tn, nm),
        in_specs=[pl.BlockSpec((tm, D_MODEL), lambda n, m: (m, 0)), pl.BlockSpec((tm, tn), lambda n, m: (m, n))],
        out_specs=pl.BlockSpec((tn, D_MODEL), lambda n, m: (n, 0)),
        out_shape=jax.ShapeDtypeStruct((w, D_MODEL), BF),
        scratch_shapes=[pltpu.VMEM((tn, D_MODEL), F32)],
        compiler_params=_params(dimension_semantics=("arbitrary", "arbitrary")),
    )(h, dseg)


def _gw_in(h, dsegs):
    return [_gw_seg(h, d, "gw_in_%d" % j) for j, d in enumerate(dsegs)]


def _local_step(x, tgt, shift, scale, gate, w_t, rows_fn, norm_w, qnw, knw, rel_bias, sinks,
                conv_w, conv_b, dt_bias, a_log, d_skip, ssm_nw, after_mid=None, after_gw=None):
    oh_t = _bucket_onehot_t()
    bias = _masked_bias(_bias_dense(rel_bias.T, oh_t).reshape(ATTN_HEADS, BLOCK, 2 * BLOCK))
    *segs, h = _inproj(x, norm_w, scale, shift, w_t)
    q, kv, zam, xbc, dtr, gab = segs
    consts = _attn_consts(qnw, knw)
    o_att, lse = _attn_fwd(q, kv, bias, sinks, consts)
    e_mat, e3t = _membership(SSM_W, SSM_P, SSM_HEADS)
    dsk_x = jnp.repeat(d_skip, SSM_P, axis=1)
    ypre, hprev, conv = _ssd_fwd(xbc, dtr, conv_w, conv_b, dt_bias, a_log, dsk_x, e3t)
    (dout, d_o, dzam, dyp, dgab, yag, dy_a, yn, dy_b, merged, dob, g_ssm_nw, dgate, loss) = _mid(
        x, tgt, o_att, zam, ypre, gab, gate, ssm_nw, rows_fn(ypre))
    g_wap = _gw_seg(dy_a, yag, "gw_attn_proj")
    g_wsp = _gw_seg(dy_b, yn, "gw_ssm_proj")
    g_wout = _gw_seg(dob, merged, "gw_out")
    zero = after_mid(g_wap, g_wsp, g_wout) if after_mid is not None else 0.0
    dq, dkv, dss, g_qnw, g_knw, g_sinks = _attn_bwd(q, kv, bias, sinks + zero, consts, o_att, lse, d_o)
    g_rel = _bias_grad(dss.reshape(ATTN_HEADS, BLOCK * 2 * BLOCK), oh_t).T
    dxbc, ddt, g_cw, g_cb, g_dtb, g_alog, g_dsk = _ssd_bwd(
        xbc, conv, dtr, conv_w, dt_bias, a_log, dsk_x, e_mat, e3t, hprev, dyp)
    dsegs = (dq, dkv, dzam, dxbc, ddt, dgab)
    g_ws = _gw_in(h, dsegs)
    zero = after_gw(g_ws) if after_gw is not None else 0.0
    gx, dshift, dscale, g_nw = _dh(x, dout, norm_w + zero, scale, dsegs, w_t)
    return dict(loss=loss, grad_x=gx, dmod=jnp.concatenate([dshift, dscale, dgate], axis=1), g_ws=g_ws,
                g_wap=g_wap, g_wsp=g_wsp, g_wout=g_wout, g_norm_w=g_nw, g_qnw=g_qnw, g_knw=g_knw, g_rel=g_rel,
                g_sinks=g_sinks, g_conv_w=g_cw, g_conv_b=g_cb, g_dt_bias=g_dtb, g_a_log=g_alog, g_d_skip=g_dsk,
                g_ssm_nw=g_ssm_nw)


def _me():
    return lax.axis_index("x"), lax.axis_index("y"), lax.axis_index("c")


def _flip(v, bit):
    return 1 - v if bit else v


def _ag_direct(v, name):
    def body(v_ref, out_ref, send_sems, recv_sems, local_sem):
        x, y, c = _me()
        me = 4 * x + 2 * y + c
        mine = pltpu.make_async_copy(v_ref, out_ref.at[me], local_sem)
        mine.start()
        peers = [(_flip(x, k >> 2 & 1), _flip(y, k >> 1 & 1), _flip(c, k & 1)) for k in range(1, N_DEV)]
        sends = [pltpu.make_async_remote_copy(
            src_ref=v_ref, dst_ref=out_ref.at[me], send_sem=send_sems.at[j], recv_sem=recv_sems.at[j],
            device_id=p, device_id_type=MESH) for j, p in enumerate(peers)]
        for cp in sends:
            cp.start()
        for j, (px, py, pc) in enumerate(peers):
            pltpu.make_async_remote_copy(
                src_ref=v_ref, dst_ref=out_ref.at[4 * px + 2 * py + pc], send_sem=send_sems.at[j],
                recv_sem=recv_sems.at[j], device_id=(px, py, pc), device_id_type=MESH).wait_recv()
        for cp in sends:
            cp.wait_send()
        mine.wait()

    vm = pl.BlockSpec(memory_space=pltpu.VMEM)
    return pl.pallas_call(
        body, name=name, out_shape=jax.ShapeDtypeStruct((N_DEV,) + v.shape, v.dtype),
        in_specs=[vm], out_specs=vm,
        scratch_shapes=[pltpu.SemaphoreType.DMA((N_DEV - 1,)), pltpu.SemaphoreType.DMA((N_DEV - 1,)),
                        pltpu.SemaphoreType.DMA],
        compiler_params=_params(),
    )(v)


def _gather_mod(v, w_ada, b_piece):
    ncols = w_ada.shape[1]

    def body(v_ref, w_ref, b_ref, rows_ref, mods_ref, piece, send_sems, recv_sems, local_sems):
        x, y, c = _me()
        me = 4 * x + 2 * y + c
        peers = _peers(x, y, c)

        def exchange(src, dst, rnd):
            mine = pltpu.make_async_copy(src, dst.at[me], local_sems.at[rnd])
            mine.start()
            sends = [pltpu.make_async_remote_copy(
                src_ref=src, dst_ref=dst.at[me], send_sem=send_sems.at[rnd, j], recv_sem=recv_sems.at[rnd, j],
                device_id=p, device_id_type=MESH) for j, p in enumerate(peers)]
            for cp in sends:
                cp.start()
            for j, (px, py, pc) in enumerate(peers):
                pltpu.make_async_remote_copy(
                    src_ref=src, dst_ref=dst.at[4 * px + 2 * py + pc], send_sem=send_sems.at[rnd, j],
                    recv_sem=recv_sems.at[rnd, j], device_id=(px, py, pc), device_id_type=MESH).wait_recv()
            for cp in sends:
                cp.wait_send()
            mine.wait()

        exchange(v_ref, rows_ref, 0)
        c_all = rows_ref[:, 0, :D_MODEL]
        piece[...] = _dot(_bf(_silu(c_all)), _bf(w_ref[...])) + b_ref[...]
        exchange(piece, mods_ref, 1)

    vm = pl.BlockSpec(memory_space=pltpu.VMEM)
    return pl.pallas_call(
        body, name="gather_mod",
        out_shape=(jax.ShapeDtypeStruct((N_DEV,) + v.shape, F32), jax.ShapeDtypeStruct((N_DEV, N_DEV, ncols), F32)),
        in_specs=[vm, vm, vm], out_specs=(vm, vm),
        scratch_shapes=[pltpu.VMEM((N_DEV, ncols), F32), pltpu.SemaphoreType.DMA((2, N_DEV - 1)),
                        pltpu.SemaphoreType.DMA((2, N_DEV - 1)), pltpu.SemaphoreType.DMA((2,))],
        compiler_params=_params(),
    )(v, w_ada, b_piece)


def _ag_relayed(v, name, chunks=1):
    rows = v.shape[0] // chunks
    assert rows * chunks == v.shape[0] and rows % 8 == 0

    def body(v_ref, out_ref, token, send_sems, recv_sems, local_sem):
        token[...] = jnp.zeros_like(token)
        x, y, c = _me()
        flip_x, flip_y = 1 - x, 1 - y
        ax, ay = c * x + (1 - c) * flip_x, c * flip_y + (1 - c) * y
        bx, by = c * flip_x + (1 - c) * x, c * y + (1 - c) * flip_y
        me, sib = (x, y, c), (x, y, 1 - c)
        a, b, dg = (ax, ay, c), (bx, by, c), (flip_x, flip_y, c)
        sa, sb, sdg = (bx, by, 1 - c), (ax, ay, 1 - c), (flip_x, flip_y, 1 - c)

        def piece(ref, k):
            return ref.at[pl.ds(k * rows, rows), :]

        def slot(px, py, pc):
            return out_ref.at[4 * px + 2 * py + pc]

        def copy(n, k, block, to, src=None):
            return pltpu.make_async_remote_copy(
                src_ref=piece(slot(*block) if src is None else src, k), dst_ref=piece(slot(*block), k),
                send_sem=send_sems.at[n * chunks + k], recv_sem=recv_sems.at[n * chunks + k],
                device_id=to, device_id_type=MESH)

        mine = pltpu.make_async_copy(v_ref, slot(*me), local_sem)
        mine.start()
        started = [copy(n, k, me, to, src=v_ref) for k in range(chunks) for n, to in ((1, a), (2, b), (0, sib))]
        for cp in started:
            cp.start()

        def arrived(n, k, block, then):
            copy(n, k, block, me).wait_recv()
            for n2, to in then:
                started.append(copy(n2, k, block, to))
                started[-1].start()

        for k in range(chunks):
            arrived(1, k, a, ((3, b), (4, sib)))
            arrived(2, k, b, ((5, sib),))
        for k in range(chunks):
            arrived(3, k, dg, ((6, sib),))
        for k in range(chunks):
            for n, block in ((0, sib), (4, sa), (5, sb), (6, sdg)):
                copy(n, k, block, me).wait_recv()
        for cp in started:
            cp.wait_send()
        mine.wait()

    out, token = pl.pallas_call(
        body, name=name,
        out_shape=(jax.ShapeDtypeStruct((N_DEV,) + v.shape, v.dtype), jax.ShapeDtypeStruct((8, 128), v.dtype)),
        in_specs=[ANY], out_specs=(ANY, pl.BlockSpec(memory_space=pltpu.VMEM)),
        scratch_shapes=[pltpu.SemaphoreType.DMA((7 * chunks,)), pltpu.SemaphoreType.DMA((7 * chunks,)),
                        pltpu.SemaphoreType.DMA],
        compiler_params=_params(),
    )(v)
    return out, token[0:1, 0:1]


HBM = pl.BlockSpec(memory_space=pltpu.HBM)
SEM = pl.BlockSpec(memory_space=pltpu.SEMAPHORE)
EFFECT = pltpu.SideEffectType.DATAFLOW_SIDE_EFFECTING


def _peers(x, y, c):
    return [(_flip(x, k >> 2 & 1), _flip(y, k >> 1 & 1), _flip(c, k & 1)) for k in range(1, N_DEV)]


def _exchange_start(src, land, gather, name):
    def body(src_ref, land_ref, send_sems, recv_sems, src_thru, land_thru, token):
        x, y, c = _me()
        me = 4 * x + 2 * y + c
        for j, (px, py, pc) in enumerate(_peers(x, y, c)):
            pltpu.make_async_remote_copy(
                src_ref=src_ref if gather else src_ref.at[4 * px + 2 * py + pc], dst_ref=land_ref.at[me],
                send_sem=send_sems.at[j], recv_sem=recv_sems.at[j], device_id=(px, py, pc), device_id_type=MESH).start()
        token[...] = jnp.zeros_like(token)

    sems = pltpu.SemaphoreType.DMA((N_DEV - 1,))
    out = pl.pallas_call(
        body, name=name,
        out_shape=(sems, sems, pltpu.HBM(src.shape, src.dtype), pltpu.HBM(land.shape, land.dtype),
                   jax.ShapeDtypeStruct((8, 128), F32)),
        in_specs=(HBM, HBM), out_specs=(SEM, SEM, HBM, HBM, pl.BlockSpec(memory_space=pltpu.VMEM)),
        input_output_aliases={0: 2, 1: 3},
        compiler_params=pltpu.CompilerParams(has_side_effects=EFFECT),
    )(pltpu.with_memory_space_constraint(src, pltpu.HBM), pltpu.with_memory_space_constraint(land, pltpu.HBM))
    return out[:4], out[4][0, 0]


def _exchange_wait(started, after, gather, name):
    send_sems, recv_sems, src_thru, land_thru = started

    def body(src_ref, land_ref, send_sems, recv_sems, after_ref, src_dead, got_ref):
        x, y, c = _me()
        for j, (px, py, pc) in enumerate(_peers(x, y, c)):
            pid = 4 * px + 2 * py + pc
            cp = pltpu.make_async_remote_copy(
                src_ref=src_ref if gather else src_ref.at[pid], dst_ref=land_ref.at[pid],
                send_sem=send_sems.at[j], recv_sem=recv_sems.at[j], device_id=(px, py, pc), device_id_type=MESH)
            cp.wait_send()
            cp.wait_recv()

    return pl.pallas_call(
        body, name=name,
        out_shape=(pltpu.HBM(src_thru.shape, src_thru.dtype), pltpu.HBM(land_thru.shape, land_thru.dtype)),
        in_specs=(HBM, HBM, SEM, SEM, ANY), out_specs=(HBM, HBM), input_output_aliases={0: 0, 1: 1},
        compiler_params=pltpu.CompilerParams(has_side_effects=EFFECT),
    )(src_thru, land_thru, send_sems, recv_sems, after)[1]


def _silu(a):
    return a * _sig(a)


def _gw_ada(c_all, dmod_piece):
    def body(c_ref, d_ref, o_ref):
        o_ref[...] = _dot_tn(_bf(_silu(c_ref[...])), _bf(d_ref[...]))

    return pl.pallas_call(
        body, name="gw_ada", out_shape=jax.ShapeDtypeStruct((c_all.shape[1], dmod_piece.shape[1]), F32),
        compiler_params=_params(),
    )(c_all, dmod_piece)


def _adam(parts, w, m, v, name):
    k, r, n = parts.shape
    if r <= 256 or r % 256 == 0:
        tr, tn = min(r, 256), n
    else:
        tr, tn = r, 256
    assert r % tr == 0 and n % tn == 0

    def body(p_ref, w_ref, m_ref, v_ref, g_ref, d_ref, nm_ref, nv_ref):
        g = p_ref[0].astype(F32)
        for j in range(1, k):
            g = g + p_ref[j].astype(F32)
        g_ref[...] = g
        d_ref[...], nm_ref[...], nv_ref[...] = _adam_math(g, w_ref[...], m_ref[...], v_ref[...])

    blk = pl.BlockSpec((tr, tn), lambda i, j: (i, j))
    return pl.pallas_call(
        body, name=name, grid=(r // tr, n // tn),
        in_specs=[pl.BlockSpec((k, tr, tn), lambda i, j: (0, i, j)), blk, blk, blk],
        out_specs=[blk, blk, blk, blk],
        out_shape=[jax.ShapeDtypeStruct((r, n), F32)] * 4,
        compiler_params=_params(dimension_semantics=("arbitrary", "arbitrary")),
    )(parts, w, m, v)


def _adam_math(g, w, m, v):
    m_new = ADAM_B1 * m + (1.0 - ADAM_B1) * g
    v_new = ADAM_B2 * v + (1.0 - ADAM_B2) * jnp.square(g)
    m_hat = m_new / (1.0 - ADAM_B1 ** ADAM_STEP)
    v_hat = v_new / (1.0 - ADAM_B2 ** ADAM_STEP)
    return -ADAM_LR * (m_hat / (jnp.sqrt(v_hat) + ADAM_EPS) + ADAM_WD * w), m_new, v_new


_SMALL = (("b_ada", 3 * D_MODEL), ("norm_w", D_MODEL), ("q_norm_w", HEAD_DIM), ("k_norm_w", HEAD_DIM),
          ("rel_bias", REL_BUCKETS * ATTN_HEADS), ("sinks", ATTN_HEADS), ("conv_b", XBC_W), ("dt_bias", SSM_HEADS),
          ("a_log", SSM_HEADS), ("d_skip", SSM_HEADS), ("ssm_norm_w", SSM_W))
_SLOT = tuple(-(-n // 128) * 128 for _, n in _SMALL)
_SLOT_OFF = tuple(int(o) for o in np.cumsum((0,) + _SLOT))
_LOSS_OFF = _SLOT_OFF[-1]
_CW_OFF = _LOSS_OFF + 128
_PACK_N = _CW_OFF + CONV_K * XBC_W


def _pack_partials(small, loss, g_conv_w):
    parts = []
    for (name, n), slot in zip(_SMALL, _SLOT):
        parts.append(small[name].reshape(1, n))
        if slot > n:
            parts.append(jnp.zeros((1, slot - n), F32))
    parts += [loss.reshape(1, 1), jnp.zeros((1, 127), F32), g_conv_w.reshape(1, CONV_K * XBC_W)]
    return jnp.concatenate(parts, axis=1)


def _adam_small(pack_all, w, m, v):
    names = [name for name, _ in _SMALL]

    def body(p_ref, *rest):
        ins, outs = rest[:3 * len(names)], rest[3 * len(names):]

        def total(off, n):
            g = p_ref[0, :, off:off + n]
            for d in range(1, N_DEV):
                g = g + p_ref[d, :, off:off + n]
            return g

        for j, (name, n) in enumerate(_SMALL):
            g = total(_SLOT_OFF[j], n)
            delta, m_new, v_new = _adam_math(g, ins[3 * j][...], ins[3 * j + 1][...], ins[3 * j + 2][...])
            outs[4 * j][...] = g
            outs[4 * j + 1][...] = delta
            outs[4 * j + 2][...] = m_new
            outs[4 * j + 3][...] = v_new
        outs[-1][...] = total(_LOSS_OFF, 1)

    flat = []
    for name, n in _SMALL:
        flat += [w[name].reshape(1, n), m[name].reshape(1, n), v[name].reshape(1, n)]
    out_shape = [jax.ShapeDtypeStruct((1, n), F32) for _, n in _SMALL for _ in range(4)] + [jax.ShapeDtypeStruct((1, 1), F32)]
    out = pl.pallas_call(body, name="adam_small", out_shape=out_shape, compiler_params=_params())(pack_all, *flat)
    res = {name: [out[4 * j + t].reshape(w[name].shape) for t in range(4)] for j, name in enumerate(names)}
    return res, out[-1]


WEIGHTS = ("w_ada", "b_ada", "norm_w", "w_in", "q_norm_w", "k_norm_w", "rel_bias", "sinks", "conv_w", "conv_b",
           "dt_bias", "a_log", "d_skip", "ssm_norm_w", "w_attn_proj", "w_ssm_proj", "w_out")


def kernel(x, c, w_ada, b_ada, norm_w, w_in, q_norm_w, k_norm_w, rel_bias, sinks, conv_w, conv_b, dt_bias, a_log, d_skip, ssm_norm_w, w_attn_proj, w_ssm_proj, w_out, loss_target, m_w_ada, m_b_ada, m_norm_w, m_w_in, m_q_norm_w, m_k_norm_w, m_rel_bias, m_sinks, m_conv_w, m_conv_b, m_dt_bias, m_a_log, m_d_skip, m_ssm_norm_w, m_w_attn_proj, m_w_ssm_proj, m_w_out, v_w_ada, v_b_ada, v_norm_w, v_w_in, v_q_norm_w, v_k_norm_w, v_rel_bias, v_sinks, v_conv_w, v_conv_b, v_dt_bias, v_a_log, v_d_skip, v_ssm_norm_w, v_w_attn_proj, v_w_ssm_proj, v_w_out):
    w = dict(w_ada=w_ada, b_ada=b_ada, norm_w=norm_w, w_in=w_in, q_norm_w=q_norm_w, k_norm_w=k_norm_w,
             rel_bias=rel_bias, sinks=sinks, conv_w=conv_w, conv_b=conv_b, dt_bias=dt_bias, a_log=a_log,
             d_skip=d_skip, ssm_norm_w=ssm_norm_w, w_attn_proj=w_attn_proj, w_ssm_proj=w_ssm_proj, w_out=w_out)
    m = dict(w_ada=m_w_ada, b_ada=m_b_ada, norm_w=m_norm_w, w_in=m_w_in, q_norm_w=m_q_norm_w, k_norm_w=m_k_norm_w,
             rel_bias=m_rel_bias, sinks=m_sinks, conv_w=m_conv_w, conv_b=m_conv_b, dt_bias=m_dt_bias, a_log=m_a_log,
             d_skip=m_d_skip, ssm_norm_w=m_ssm_norm_w, w_attn_proj=m_w_attn_proj, w_ssm_proj=m_w_ssm_proj, w_out=m_w_out)
    v = dict(w_ada=v_w_ada, b_ada=v_b_ada, norm_w=v_norm_w, w_in=v_w_in, q_norm_w=v_q_norm_w, k_norm_w=v_k_norm_w,
             rel_bias=v_rel_bias, sinks=v_sinks, conv_w=v_conv_w, conv_b=v_conv_b, dt_bias=v_dt_bias, a_log=v_a_log,
             d_skip=v_d_skip, ssm_norm_w=v_ssm_norm_w, w_attn_proj=v_w_attn_proj, w_ssm_proj=v_w_ssm_proj, w_out=v_w_out)
    me = 4 * lax.axis_index("x") + 2 * lax.axis_index("y") + lax.axis_index("c")
    ada_n = w_ada.shape[2]
    in_n = w_in.shape[2]
    cw_n = conv_w.shape[2]

    b_piece = lax.dynamic_slice_in_dim(b_ada, me * ada_n, ada_n, axis=1)
    first, mod_all = _gather_mod(jnp.concatenate([c, conv_w[0].reshape(1, CONV_K * cw_n)], axis=1), w_ada[0], b_piece)
    first = first[:, 0]
    c_all = first[:, :D_MODEL]
    conv_w_full = first[:, D_MODEL:].reshape(N_DEV, CONV_K, cw_n).transpose(1, 0, 2).reshape(CONV_K, XBC_W)
    mod = lax.dynamic_index_in_dim(mod_all, me, axis=1, keepdims=False).reshape(1, 3 * D_MODEL)
    shift, scale, gate = mod[:, :D_MODEL], mod[:, D_MODEL:2 * D_MODEL], mod[:, 2 * D_MODEL:]

    pad = -in_n % 24
    w_t, zero = _ag_relayed(jnp.pad(w_in[0].T.astype(BF), ((0, pad), (0, 0))), "ag_w_in", chunks=3)
    w_t = w_t[:, :in_n].reshape(N_DEV * in_n, D_MODEL)

    def with_mine(blocks, mine):
        return lax.dynamic_update_index_in_dim(lax.empty(blocks, mine.dtype), mine, me, axis=0)

    rows = jnp.concatenate([w_attn_proj[0], w_ssm_proj[0], w_out[0]], axis=0).astype(BF) + zero
    r_ap, r_sp = w_attn_proj.shape[1], w_ssm_proj.shape[1]
    rows_started, zero = _exchange_start(rows, with_mine((N_DEV,) + rows.shape, rows), True, "ag_rows_start")

    def rows_fn(after):
        return _exchange_wait(rows_started, after, True, "ag_rows_wait")

    started = {}

    def send_blocks(key, g, name):
        started[key], zero = _exchange_start(
            g, with_mine(g.shape, lax.dynamic_index_in_dim(g, me, axis=0, keepdims=False)), False, name)
        return zero

    def after_mid(g_wap, g_wsp, g_wout):
        return send_blocks("rows", jnp.concatenate(
            [g_wap.reshape(N_DEV, r_ap, D_MODEL), g_wsp.reshape(N_DEV, r_sp, D_MODEL),
             g_wout.reshape(N_DEV, r_ap, D_MODEL)], axis=1), "rs_rows_start")

    def after_gw(g_ws):
        return send_blocks("in", jnp.concatenate(g_ws, axis=0).reshape(N_DEV, in_n, D_MODEL), "rs_in_start")

    r = _local_step(x[0], loss_target[0], shift, scale + zero, gate, w_t, rows_fn, norm_w, q_norm_w, k_norm_w,
                    rel_bias, sinks, conv_w_full, conv_b, dt_bias, a_log, d_skip, ssm_norm_w, after_mid, after_gw)

    small = dict(b_ada=r["dmod"], norm_w=r["g_norm_w"], q_norm_w=r["g_qnw"], k_norm_w=r["g_knw"], rel_bias=r["g_rel"],
                 sinks=r["g_sinks"], conv_b=r["g_conv_b"], dt_bias=r["g_dt_bias"], a_log=r["g_a_log"],
                 d_skip=r["g_d_skip"], ssm_norm_w=r["g_ssm_nw"])
    pack_all = _ag_direct(_pack_partials(small, r["loss"], r["g_conv_w"]), "ag_small")
    res, loss = _adam_small(pack_all, w, m, v)
    loss = loss[0, 0]
    cw_parts = pack_all[:, 0, _CW_OFF:].reshape(N_DEV, CONV_K, XBC_W)
    cw_mine = lax.dynamic_slice_in_dim(cw_parts, me * cw_n, cw_n, axis=2)
    res["conv_w"] = [a[None] for a in _adam(cw_mine, conv_w[0], m_conv_w[0], v_conv_w[0], "adam_conv_w")]

    dmod_piece = lax.dynamic_slice_in_dim(pack_all[:, 0, :3 * D_MODEL], me * ada_n, ada_n, axis=1)
    g_ada = _gw_ada(c_all, dmod_piece)
    res["w_ada"] = [a[None] for a in _adam(g_ada[None], w_ada[0], m_w_ada[0], v_w_ada[0], "adam_w_ada")]

    cat = lambda d: jnp.concatenate([d["w_attn_proj"][0], d["w_ssm_proj"][0], d["w_out"][0]], axis=0)
    rows_res = _adam(_exchange_wait(started["rows"], g_ada, False, "rs_rows_wait"), cat(w), cat(m), cat(v), "adam_w_rows")
    res["w_in"] = [a.T[None] for a in _adam(_exchange_wait(started["in"], rows_res[0], False, "rs_in_wait"),
                                            w_in[0].T, m_w_in[0].T, v_w_in[0].T, "adam_w_in")]
    res["w_attn_proj"] = [a[None, :r_ap] for a in rows_res]
    res["w_ssm_proj"] = [a[None, r_ap:r_ap + r_sp] for a in rows_res]
    res["w_out"] = [a[None, r_ap + r_sp:] for a in rows_res]

    outs = [loss, r["grad_x"][None]]
    for j in range(4):
        outs += [res[name][j] for name in WEIGHTS]
    return tuple(outs)
```

```python
import math

import numpy as np
import jax
import jax.numpy as jnp
from jax import lax
from jax.experimental import pallas as pl
from jax.experimental.pallas import tpu as pltpu

F32 = jnp.float32
BF = jnp.bfloat16
HI = lax.Precision.HIGHEST

D_MODEL = 1024
ATTN_HEADS = 16
KV_HEADS = 4
GRP = ATTN_HEADS // KV_HEADS
HEAD_DIM = 64
ATTN_W = ATTN_HEADS * HEAD_DIM
KV_W = KV_HEADS * HEAD_DIM
BLOCK = 128
REL_BUCKETS = 32
REL_MAX_DIST = 128
SSM_W = 2048
SSM_P = 64
SSM_HEADS = 32
SSM_G = 4
SSM_R = 8
SSM_N = 128
CONV_K = 4
XBC_W = SSM_W + 2 * SSM_G * SSM_N
SEG_W = (ATTN_W, 2 * KV_W, ATTN_W + SSM_W, XBC_W, SSM_HEADS, 2 * D_MODEL)
NSEG = len(SEG_W)
SEG_OFF = tuple(int(v) for v in np.cumsum((0,) + SEG_W))
IN_W = SEG_OFF[-1]
GATE_SEGS = (2, 5)
EPS = 1e-6
N_DEV = 8
ADAM_LR, ADAM_B1, ADAM_B2, ADAM_EPS, ADAM_WD, ADAM_STEP = 0.001, 0.9, 0.999, 1e-08, 0.01, 10
VMEM_LIMIT = 60 * 1024 * 1024
MESH = pl.DeviceIdType.MESH
ANY = pl.BlockSpec(memory_space=pl.ANY)


def _dot(a, b, precision=None):
    return jnp.dot(a, b, preferred_element_type=F32, precision=precision)


def _dot_nt(a, b, precision=None):
    return lax.dot_general(a, b, (((1,), (1,)), ((), ())), preferred_element_type=F32, precision=precision)


def _dot_tn(a, b, precision=None):
    return lax.dot_general(a, b, (((0,), (0,)), ((), ())), preferred_element_type=F32, precision=precision)


def _bf(a):
    return a.astype(BF)


def _sig(a):
    return 0.5 * jnp.tanh(0.5 * a) + 0.5


def _params(**kw):
    return pltpu.CompilerParams(vmem_limit_bytes=VMEM_LIMIT, **kw)


def _full(shape):
    nd = len(shape)
    return pl.BlockSpec(shape, lambda i: (0,) * nd)


def _rows(tm, w):
    return pl.BlockSpec((tm, w), lambda i: (i, 0))


def _inproj(x, norm_w, scale, shift, w_t, tm=256):
    s = x.shape[0]

    def body(x_ref, nw_ref, sc_ref, sh_ref, w_hbm, *rest):
        outs, h_ref, w_vm, sem = rest[:NSEG], rest[NSEG], rest[NSEG + 1], rest[NSEG + 2]
        first = pl.program_id(0) == 0
        cps = [pltpu.make_async_copy(w_hbm.at[SEG_OFF[j]:SEG_OFF[j + 1], :], w_vm.at[SEG_OFF[j]:SEG_OFF[j + 1], :], sem.at[j])
               for j in range(NSEG)]

        def tile(waiting):
            xv = x_ref[...]
            r = lax.rsqrt(jnp.mean(xv * xv, axis=-1, keepdims=True) + EPS)
            h = xv * r * (nw_ref[...] * (1.0 + sc_ref[...])) + sh_ref[...]
            hb = _bf(h)
            h_ref[...] = hb
            for j in range(NSEG):
                if waiting:
                    cps[j].wait()
                outs[j][...] = _dot_nt(hb, w_vm[SEG_OFF[j]:SEG_OFF[j + 1], :]).astype(outs[j].dtype)

        @pl.when(first)
        def _():
            for cp in cps:
                cp.start()
            tile(True)

        @pl.when(jnp.logical_not(first))
        def _():
            tile(False)

    vec = _full((1, D_MODEL))
    return pl.pallas_call(
        body, name="inproj", grid=(s // tm,),
        in_specs=[_rows(tm, D_MODEL), vec, vec, vec, ANY],
        out_specs=[_rows(tm, w) for w in SEG_W] + [_rows(tm, D_MODEL)],
        out_shape=[jax.ShapeDtypeStruct((s, w), BF if j in GATE_SEGS else F32) for j, w in enumerate(SEG_W)]
                  + [jax.ShapeDtypeStruct((s, D_MODEL), BF)],
        scratch_shapes=[pltpu.VMEM((IN_W, D_MODEL), BF), pltpu.SemaphoreType.DMA((NSEG,))],
        compiler_params=_params(dimension_semantics=("arbitrary",)),
    )(x, norm_w, scale, shift, w_t)


def _bucket_onehot_t():
    qi = jnp.arange(BLOCK)[:, None]
    kj = jnp.arange(2 * BLOCK)[None, :]
    dist = qi + BLOCK - kj
    n = jnp.maximum(dist, 0)
    max_exact = REL_BUCKETS // 2
    nf = jnp.maximum(n, 1).astype(F32)
    large = max_exact + (jnp.log(nf / max_exact) / math.log(REL_MAX_DIST / max_exact)
                         * (REL_BUCKETS - max_exact)).astype(jnp.int32)
    large = jnp.minimum(large, REL_BUCKETS - 1)
    bucket = jnp.where(n < max_exact, n, large).reshape(1, BLOCK * 2 * BLOCK)
    return (bucket == jnp.arange(REL_BUCKETS)[:, None]).astype(F32)


def _bias_dense(rel_bias_t, oh_t):
    def body(rb_ref, oh_ref, o_ref):
        o_ref[...] = _dot(rb_ref[...], oh_ref[...], HI)

    return pl.pallas_call(
        body, name="bias_dense", out_shape=jax.ShapeDtypeStruct((ATTN_HEADS, BLOCK * 2 * BLOCK), F32),
        compiler_params=_params(),
    )(rel_bias_t, oh_t)


def _bias_grad(ds_sum, oh_t):
    def body(ds_ref, oh_ref, o_ref):
        o_ref[...] = _dot_nt(ds_ref[...], oh_ref[...], HI)

    return pl.pallas_call(
        body, name="bias_grad", out_shape=jax.ShapeDtypeStruct((ATTN_HEADS, REL_BUCKETS), F32),
        compiler_params=_params(),
    )(ds_sum, oh_t)


def _group_sum(a, e):
    hi = _bf(a)
    return _dot(hi, e) + _dot(_bf(a - hi.astype(F32)), e)


def _group_bcast(a, e3t):
    hi = _bf(a)
    r1 = a - hi.astype(F32)
    mid = _bf(r1)
    return _dot(jnp.concatenate([hi, mid, _bf(r1 - mid.astype(F32))], axis=1), e3t)


def _membership(width, group, ngroups):
    e = (jnp.arange(width)[:, None] // group == jnp.arange(ngroups)[None, :]).astype(BF)
    return e, jnp.tile(e.T, (3, 1))


def _fold(width, group):
    return (jnp.arange(width)[:, None] % group == jnp.arange(group)[None, :]).astype(BF)


def _heads_norm(t, w_x, e, e3t):
    r = lax.rsqrt(_dot(_bf(t * t), e) * (1.0 / HEAD_DIM) + EPS)
    r_x = _group_bcast(r, e3t)
    return t * r_x * w_x, r_x


def _heads_norm_bwd(t, r_x, w_x, d, e, e3t):
    wd = d * w_x
    corr = _group_bcast(_dot(_bf(t * wd), e) * (1.0 / HEAD_DIM), e3t)
    return r_x * wd - t * (r_x * r_x * r_x) * corr, jnp.sum(d * t * r_x, axis=0, keepdims=True)


def _stack_heads(a, hk):
    return jnp.concatenate([a[:, (hk * GRP + g) * HEAD_DIM:(hk * GRP + g + 1) * HEAD_DIM] for g in range(GRP)], axis=0)


def _stack_cols(a, hk):
    return jnp.concatenate([a[:, hk * GRP + g:hk * GRP + g + 1] for g in range(GRP)], axis=0)


def _masked_bias(bias):
    qi = jnp.arange(BLOCK)[:, None]
    kj = jnp.arange(2 * BLOCK)[None, :]
    cur_ok = jnp.logical_and(kj >= BLOCK, kj - BLOCK <= qi)
    both_ok = jnp.logical_or(jnp.logical_and(kj < BLOCK, kj > qi), cur_ok)
    return jnp.stack([jnp.where(cur_ok, bias, -1e30), jnp.where(both_ok, bias, -1e30)])


def _attn_consts(qnw, knw):
    eq, eq3t = _membership(ATTN_W, HEAD_DIM, ATTN_HEADS)
    ek, ek3t = _membership(KV_W, HEAD_DIM, ATTN_HEADS)
    return (jnp.tile(qnw, (1, ATTN_HEADS)), jnp.tile(knw, (1, KV_HEADS)), eq, eq3t, ek, ek3t)


def _attn_fwd(q, kv, bias, sinks, consts):
    s = q.shape[0]
    nb = s // BLOCK
    gq = GRP * BLOCK
    bias_t = bias.reshape(2, KV_HEADS, GRP, BLOCK, 2 * BLOCK).transpose(0, 1, 4, 2, 3).reshape(2, KV_HEADS, 2 * BLOCK, gq)
    sink_rows = jnp.repeat(sinks.reshape(KV_HEADS, GRP), BLOCK, axis=1).reshape(KV_HEADS, 1, gq)
    eye = jnp.eye(BLOCK, dtype=BF)

    def body(q_ref, kp_ref, kc_ref, vp_ref, vc_ref, b_ref, bt_ref, sk_ref, skr_ref, eye_ref,
             qw_ref, kw_ref, eq_ref, eq3_ref, ek_ref, ek3_ref, o_ref, lse_ref):
        qn = _bf(_heads_norm(q_ref[...], qw_ref[...], eq_ref[...], eq3_ref[...])[0] * (HEAD_DIM ** -0.5))
        kn = _bf(_heads_norm(jnp.concatenate([kp_ref[...], kc_ref[...]], axis=0), kw_ref[...], ek_ref[...], ek3_ref[...])[0])
        vv = _bf(jnp.concatenate([vp_ref[...], vc_ref[...]], axis=0))
        ones = jnp.ones((2 * BLOCK, HEAD_DIM), BF)
        kss = [slice(hk * HEAD_DIM, (hk + 1) * HEAD_DIM) for hk in range(KV_HEADS)]
        qgs = [_stack_heads(qn, hk) for hk in range(KV_HEADS)]
        sc_ts = [_dot_nt(kn[:, kss[hk]], qgs[hk]) + bt_ref[0, hk] for hk in range(KV_HEADS)]
        m_rows = [jnp.maximum(jnp.max(sc_ts[hk], axis=0, keepdims=True), skr_ref[hk]) for hk in range(KV_HEADS)]
        m_hq = _bf(jnp.concatenate([(m + jnp.abs(m) * (2.0 ** -7))[:, g * BLOCK:(g + 1) * BLOCK]
                                    for m in m_rows for g in range(GRP)], axis=0))
        m16 = _dot_nt(eye_ref[...], m_hq)
        ms = [_stack_cols(m16, hk) for hk in range(KV_HEADS)]
        scs = [_dot_nt(qgs[hk], kn[:, kss[hk]]) + b_ref[0, hk * GRP:(hk + 1) * GRP].reshape(gq, 2 * BLOCK)
               for hk in range(KV_HEADS)]
        ps = [_bf(jnp.exp(scs[hk] - ms[hk])) for hk in range(KV_HEADS)]
        pvs = [_dot(ps[hk], jnp.concatenate([vv[:, kss[hk]], ones], axis=1)) for hk in range(KV_HEADS)]
        den16 = jnp.concatenate([pvs[hk][g * BLOCK:(g + 1) * BLOCK, HEAD_DIM:HEAD_DIM + 1]
                                 for hk in range(KV_HEADS) for g in range(GRP)], axis=1)
        den16 = den16 + jnp.exp(sk_ref[...] - m16)
        lse_ref[...] = m16 + jnp.log(den16)
        inv16 = 1.0 / den16
        for hk in range(KV_HEADS):
            for g in range(GRP):
                h = hk * GRP + g
                o_ref[:, h * HEAD_DIM:(h + 1) * HEAD_DIM] = (pvs[hk][g * BLOCK:(g + 1) * BLOCK, :HEAD_DIM]
                                                             * inv16[:, h:h + 1])

    cur = lambda w, col=0: pl.BlockSpec((BLOCK, w), lambda i: (i, col))
    prev = lambda w, col=0: pl.BlockSpec((BLOCK, w), lambda i: (jnp.maximum(i - 1, 0), col))
    whole = lambda a: pl.BlockSpec(a.shape, lambda i: (0,) * a.ndim)
    first_or_not = lambda a: pl.BlockSpec((1,) + a.shape[1:], lambda i: (jnp.minimum(i, 1),) + (0,) * (a.ndim - 1))
    return pl.pallas_call(
        body, name="attn_fwd", grid=(nb,),
        in_specs=[cur(ATTN_W), prev(KV_W, 0), cur(KV_W, 0), prev(KV_W, 1), cur(KV_W, 1),
                  first_or_not(bias), first_or_not(bias_t),
                  whole(sinks), whole(sink_rows), whole(eye)] + [_full(c.shape) for c in consts],
        out_specs=[cur(ATTN_W), cur(ATTN_HEADS)],
        out_shape=[jax.ShapeDtypeStruct((s, ATTN_W), F32), jax.ShapeDtypeStruct((s, ATTN_HEADS), F32)],
        compiler_params=_params(dimension_semantics=("arbitrary",)),
    )(q, kv, kv, kv, kv, bias, bias_t, sinks, sink_rows, eye, *consts)


def _conv_taps(xbc, tail):
    ext = jnp.concatenate([tail, xbc], axis=0)
    return [pltpu.roll(ext, CONV_K - 1 - j, axis=0)[8:8 + BLOCK] if j < CONV_K - 1 else xbc for j in range(CONV_K)]


def _softplus(u):
    return jnp.maximum(u, 0.0) + jnp.log(1.0 + jnp.exp(-jnp.abs(u)))


def _tril():
    r = lax.broadcasted_iota(jnp.int32, (BLOCK, BLOCK), 0)
    c = lax.broadcasted_iota(jnp.int32, (BLOCK, BLOCK), 1)
    return r >= c


def _triu():
    r = lax.broadcasted_iota(jnp.int32, (BLOCK, BLOCK), 0)
    c = lax.broadcasted_iota(jnp.int32, (BLOCK, BLOCK), 1)
    return r <= c


def _exact_left(m01, a):
    hi = _bf(a)
    r1 = a - hi.astype(F32)
    mid = _bf(r1)
    return _dot(m01, hi) + _dot(m01, mid) + _dot(m01, _bf(r1 - mid.astype(F32)))


def _ssd_common(conv, dtr, dtb_ref, alog_ref, e3_ref):
    sg = _sig(conv)
    xact = conv * sg
    u = dtr + dtb_ref[...]
    dt = _softplus(u)
    a = -jnp.exp(alog_ref[...])
    trilb = _tril()
    acum = _exact_left(trilb.astype(BF), dt * a) * math.log2(math.e)
    both = _group_bcast(jnp.concatenate([dt, acum], axis=0), e3_ref[...])
    dt_x, acum_x = both[:BLOCK], both[BLOCK:]
    return sg, xact, u, dt, a, trilb, acum, dt_x, acum_x


SSD_CH = 2


def _ssd_fwd(xbc, dt_raw, conv_w, conv_b, dt_bias, a_log, dsk_x, e3t):
    s = xbc.shape[0]
    nc = s // BLOCK
    ch = SSD_CH if nc % SSD_CH == 0 else 1
    rows = ch * BLOCK

    def body(x_ref, tail_ref, dtr_ref, cw_ref, cb_ref, dtb_ref, alog_ref, dsk_ref, e3_ref,
             y_ref, hp_ref, conv_ref, hst, yd_s, yoff_s):
        i = pl.program_id(0)

        @pl.when(i == 0)
        def _():
            hst[...] = jnp.zeros_like(hst)

        for j in range(ch):
            rs = slice(j * BLOCK, (j + 1) * BLOCK)
            tail = jnp.where(i > 0, tail_ref[...], 0.0) if j == 0 else x_ref[j * BLOCK - 8:j * BLOCK, :]
            taps = _conv_taps(x_ref[rs, :], tail)
            conv = cb_ref[...] + sum(taps[t] * cw_ref[t:t + 1, :] for t in range(CONV_K))
            conv_ref[rs, :] = conv
            _, xact, _, _, _, trilb, acum, dt_x, acum_x = _ssd_common(conv, dtr_ref[rs, :], dtb_ref, alog_ref, e3_ref)
            xs = xact[:, :SSM_W]
            acum_t = acum.T
            ea_x = jnp.exp2(acum_x)
            last_x = acum_x[BLOCK - 1:BLOCK, :]
            xdt = xs * dt_x
            xw = xdt * jnp.exp2(last_x - acum_x)
            cd_x = jnp.exp2(last_x)
            hprev = hst[...]
            hp_ref[j] = hprev
            sls = [slice(g * SSM_R * SSM_P, (g + 1) * SSM_R * SSM_P) for g in range(SSM_G)]
            bgs = [_bf(xact[:, SSM_W + g * SSM_N:SSM_W + (g + 1) * SSM_N]) for g in range(SSM_G)]
            cgs = [_bf(xact[:, SSM_W + SSM_G * SSM_N + g * SSM_N:SSM_W + SSM_G * SSM_N + (g + 1) * SSM_N])
                   for g in range(SSM_G)]
            xdt_b, xw_b, hprev_b = _bf(xdt), _bf(xw), _bf(hprev)
            low_half = lax.broadcasted_iota(jnp.int32, (BLOCK, 2 * SSM_P), 1) < SSM_P
            cbs = [_dot_nt(cgs[g], bgs[g]) for g in range(SSM_G)]
            for g in range(SSM_G):
                sl = sls[g]
                yoff_s[:, sl] = _dot(cgs[g], hprev_b[:, sl]) * ea_x[:, sl]
                hst[:, sl] = hprev[:, sl] * cd_x[:, sl] + _dot_tn(bgs[g], xw_b[:, sl])
            for g in range(SSM_G):
                hss = [slice((g * SSM_R + r) * SSM_P, (g * SSM_R + r + 1) * SSM_P) for r in range(SSM_R)]
                mms = [_bf(cbs[g] * jnp.exp2(jnp.where(trilb, acum[:, g * SSM_R + r:g * SSM_R + r + 1]
                                                      - acum_t[g * SSM_R + r:g * SSM_R + r + 1, :], -1e30)))
                       for r in range(SSM_R)]
                for r in range(0, SSM_R, 2):
                    pair = slice(hss[r].start, hss[r + 1].stop)
                    xp = xdt_b[:, pair]
                    rhs = jnp.concatenate([jnp.where(low_half, xp, 0), jnp.where(low_half, 0, xp)], axis=0)
                    yd_s[:, pair] = _dot(jnp.concatenate([mms[r], mms[r + 1]], axis=1), rhs)
            y_ref[rs, :] = yd_s[...] + yoff_s[...] + dsk_ref[...] * xs

    blk = lambda w: pl.BlockSpec((rows, w), lambda i: (i, 0))
    return pl.pallas_call(
        body, name="ssd_fwd", grid=(nc // ch,),
        in_specs=[blk(XBC_W), pl.BlockSpec((8, XBC_W), lambda i: (jnp.maximum(i * (rows // 8) - 1, 0), 0)),
                  blk(SSM_HEADS), _full((CONV_K, XBC_W)), _full((1, XBC_W)), _full((1, SSM_HEADS)),
                  _full((1, SSM_HEADS)), _full((1, SSM_W)), _full((3 * SSM_HEADS, SSM_W))],
        out_specs=[blk(SSM_W), pl.BlockSpec((ch, SSM_N, SSM_W), lambda i: (i, 0, 0)), blk(XBC_W)],
        out_shape=[jax.ShapeDtypeStruct((s, SSM_W), F32), jax.ShapeDtypeStruct((nc, SSM_N, SSM_W), F32),
                   jax.ShapeDtypeStruct((s, XBC_W), F32)],
        scratch_shapes=[pltpu.VMEM((SSM_N, SSM_W), F32), pltpu.VMEM((BLOCK, SSM_W), F32), pltpu.VMEM((BLOCK, SSM_W), F32)],
        compiler_params=_params(dimension_semantics=("arbitrary",)),
    )(xbc, xbc, dt_raw, conv_w, conv_b, dt_bias, a_log, dsk_x, e3t)


def _dsilu(z, sg, silu):
    return sg * (1.0 + (z - silu))


def _mid(x, tgt, o_att, zam, ypre, gab, gate, ssm_nw, rows_all, tm=256):
    s = x.shape[0]
    gw = SSM_W // SSM_G

    r_ap, r_sp = ATTN_W // N_DEV, SSM_W // N_DEV

    def body(x_ref, t_ref, o_ref, zam_ref, yp_ref, gab_ref, gate_ref, nw_ref, rows_h,
             dout_ref, do_ref, dzam_ref, dyp_ref, dgab_ref,
             yag_ref, dya_ref, yn_ref, dyb_ref, mg_ref, dob_ref, gnw_ref, dgate_ref, loss_ref,
             wap_v, wsp_v, wout_v, sem):
        i = pl.program_id(0)

        @pl.when(i == 0)
        def _():
            cps = []
            for d in range(N_DEV):
                for j, (dst, r0, rn) in enumerate(((wap_v, 0, r_ap), (wsp_v, r_ap, r_sp), (wout_v, r_ap + r_sp, r_ap))):
                    cps.append(pltpu.make_async_copy(rows_h.at[d, r0:r0 + rn, :], dst.at[d * rn:(d + 1) * rn, :], sem.at[j]))
            for cp in cps:
                cp.start()
            gnw_ref[...] = jnp.zeros_like(gnw_ref)
            dgate_ref[...] = jnp.zeros_like(dgate_ref)
            loss_ref[...] = jnp.zeros_like(loss_ref)
            for cp in cps:
                cp.wait()

        gate = gate_ref[...]
        nw = nw_ref[...]
        o_att = o_ref[...]
        z_a = zam_ref[:, :ATTN_W].astype(F32)
        s_a = _sig(z_a)
        silu_a = z_a * s_a
        yag = _bf(o_att * silu_a)
        yag_ref[...] = yag
        ypre = yp_ref[...]
        z_m = zam_ref[:, ATTN_W:].astype(F32)
        s_m = _sig(z_m)
        silu_m = z_m * s_m
        yg = ypre * silu_m
        rinv = jnp.concatenate(
            [jnp.broadcast_to(lax.rsqrt(jnp.mean(yg[:, g * gw:(g + 1) * gw] ** 2, axis=-1, keepdims=True) + EPS), (tm, gw))
             for g in range(SSM_G)], axis=1)
        ynr = yg * rinv
        yn = _bf(ynr * nw)
        yn_ref[...] = yn
        y_a = _dot(yag, wap_v[...])
        y_b = _dot(yn, wsp_v[...])
        g_a = _sig(gab_ref[:, :D_MODEL].astype(F32))
        g_b = _sig(gab_ref[:, D_MODEL:].astype(F32))
        merged = _bf(g_a * y_a + g_b * y_b)
        mg_ref[...] = merged
        o = _dot(merged, wout_v[...])
        diff = x_ref[...] + gate * o - t_ref[...]
        loss_ref[...] += (0.5 / D_MODEL) * jnp.sum(diff * diff, axis=(0, 1), keepdims=True)
        dout = diff * (1.0 / D_MODEL)
        dout_ref[...] = dout
        dgate_ref[...] += jnp.sum(dout * o, axis=0, keepdims=True)
        d_o = _bf(dout * gate)
        dob_ref[...] = d_o
        dmerged = _dot_nt(d_o, wout_v[...])
        dy_af = dmerged * g_a
        dy_bf = dmerged * g_b
        dy_a = _bf(dy_af)
        dy_b = _bf(dy_bf)
        dya_ref[...] = dy_a
        dyb_ref[...] = dy_b
        dyag = _dot_nt(dy_a, wap_v[...])
        dyn = _dot_nt(dy_b, wsp_v[...])
        dgab_ref[:, :D_MODEL] = _bf(dy_af * y_a * (1.0 - g_a))
        dgab_ref[:, D_MODEL:] = _bf(dy_bf * y_b * (1.0 - g_b))
        do_ref[...] = dyag * silu_a
        dzam_ref[:, :ATTN_W] = _bf(dyag * o_att * _dsilu(z_a, s_a, silu_a))
        gnw_ref[...] += jnp.sum(dyn * ynr, axis=0, keepdims=True)
        dynw = dyn * nw
        corr = jnp.concatenate(
            [jnp.broadcast_to(jnp.mean((dynw * ynr)[:, g * gw:(g + 1) * gw], axis=-1, keepdims=True), (tm, gw))
             for g in range(SSM_G)], axis=1)
        dyg = rinv * (dynw - ynr * corr)
        dyp_ref[...] = dyg * silu_m
        dzam_ref[:, ATTN_W:] = _bf(dyg * ypre * _dsilu(z_m, s_m, silu_m))

    r1, r2, r3 = _rows(tm, D_MODEL), _rows(tm, SSM_W), _rows(tm, ATTN_W + SSM_W)
    sd = jax.ShapeDtypeStruct
    return pl.pallas_call(
        body, name="mid", grid=(s // tm,),
        in_specs=[r1, r1, r1, r3, r2, r2, _full((1, D_MODEL)), _full((1, SSM_W)), ANY],
        out_specs=[r1, r1, r3, r2, r2, r1, r1, r2, r1, r1, r1,
                   _full((1, SSM_W)), _full((1, D_MODEL)), _full((1, 1))],
        out_shape=[sd((s, D_MODEL), F32), sd((s, ATTN_W), F32), sd((s, ATTN_W + SSM_W), BF), sd((s, SSM_W), F32),
                   sd((s, 2 * D_MODEL), BF),
                   sd((s, ATTN_W), BF), sd((s, D_MODEL), BF), sd((s, SSM_W), BF), sd((s, D_MODEL), BF),
                   sd((s, D_MODEL), BF), sd((s, D_MODEL), BF),
                   sd((1, SSM_W), F32), sd((1, D_MODEL), F32), sd((1, 1), F32)],
        scratch_shapes=[pltpu.VMEM((ATTN_W, D_MODEL), BF), pltpu.VMEM((SSM_W, D_MODEL), BF), pltpu.VMEM((D_MODEL, D_MODEL), BF),
                        pltpu.SemaphoreType.DMA((3,))],
        compiler_params=_params(dimension_semantics=("arbitrary",)),
    )(x, tgt, o_att, zam, ypre, gab, gate, ssm_nw, rows_all)


def _attn_bwd(q, kv, bias, sinks, consts, o_att, lse, d_o):
    s = q.shape[0]
    nb = s // BLOCK
    folds = (_fold(ATTN_W, HEAD_DIM), _fold(KV_W, HEAD_DIM))

    def body(q_ref, kp_ref, kc_ref, vp_ref, vc_ref, b_ref, skv_ref, qw_ref, kw_ref, eq_ref, eq3_ref, ek_ref, ek3_ref,
             fq_ref, fk_ref, o_ref, lse_ref, do_ref,
             dq_ref, dkv_ref, dss_ref, gqw_ref, gkw_ref, gsk_ref, ckn, cv, dqn_s, dkn_s, dv_s, gq_x, gk_x):
        i = pl.program_id(0)
        kw, ek, ek3 = kw_ref[...], ek_ref[...], ek3_ref[...]

        @pl.when(i == 0)
        def _():
            for ref in (ckn, cv, dss_ref, gq_x, gk_x, gsk_ref):
                ref[...] = jnp.zeros_like(ref)

        @pl.when(i < nb)
        def _():
            qw, eq, eq3 = qw_ref[...], eq_ref[...], eq3_ref[...]
            qf = q_ref[...]
            qnf, rq_x = _heads_norm(qf, qw, eq, eq3)
            qn = _bf(qnf * (HEAD_DIM ** -0.5))
            kf = jnp.concatenate([kp_ref[...], kc_ref[...]], axis=0)
            knf, rk_x = _heads_norm(kf, kw, ek, ek3)
            kn = _bf(knf)
            vv = _bf(jnp.concatenate([vp_ref[...], vc_ref[...]], axis=0))
            d_of = do_ref[...]
            d_ob = _bf(d_of)
            lse_all = lse_ref[...]
            delta = _dot(_bf(d_of * o_ref[...]), eq)
            gsk_ref[...] += jnp.sum(-jnp.exp(skv_ref[...] - lse_all) * delta, axis=0, keepdims=True)
            kss = [slice(hk * HEAD_DIM, (hk + 1) * HEAD_DIM) for hk in range(KV_HEADS)]
            qgs = [_stack_heads(qn, hk) for hk in range(KV_HEADS)]
            d_ogs = [_stack_heads(d_ob, hk) for hk in range(KV_HEADS)]
            scs = [_dot_nt(qgs[hk], kn[:, kss[hk]]) + b_ref[0, hk * GRP:(hk + 1) * GRP].reshape(GRP * BLOCK, 2 * BLOCK)
                   for hk in range(KV_HEADS)]
            dps = [_dot_nt(d_ogs[hk], vv[:, kss[hk]]) for hk in range(KV_HEADS)]
            ps = [jnp.exp(scs[hk] - _stack_cols(lse_all, hk)) for hk in range(KV_HEADS)]
            dss = [ps[hk] * (dps[hk] - _stack_cols(delta, hk)) for hk in range(KV_HEADS)]
            pbs = [_bf(p) for p in ps]
            dsbs = [_bf(ds) for ds in dss]
            for hk in range(KV_HEADS):
                dss_ref[hk * GRP:(hk + 1) * GRP] += dss[hk].reshape(GRP, BLOCK, 2 * BLOCK)
            for hk in range(KV_HEADS):
                dv_s[:, kss[hk]] = _dot_tn(pbs[hk], d_ogs[hk])
                dkn_s[:, kss[hk]] = _dot_tn(dsbs[hk], qgs[hk])
            dqns = [_dot(dsbs[hk], kn[:, kss[hk]]) * (HEAD_DIM ** -0.5) for hk in range(KV_HEADS)]
            for hk in range(KV_HEADS):
                for g in range(GRP):
                    h = hk * GRP + g
                    dqn_s[:, h * HEAD_DIM:(h + 1) * HEAD_DIM] = dqns[hk][g * BLOCK:(g + 1) * BLOCK]
            dq, gq = _heads_norm_bwd(qf, rq_x, qw, dqn_s[...], eq, eq3)
            dq_ref[...] = _bf(dq)
            gq_x[...] += gq
            dk, gk = _heads_norm_bwd(kf[:BLOCK], rk_x[:BLOCK], kw, ckn[...] + dkn_s[0:BLOCK, :], ek, ek3)
            dkv_ref[:, :KV_W] = _bf(dk)
            gk_x[...] += gk
            dkv_ref[:, KV_W:] = _bf(cv[...] + dv_s[0:BLOCK, :])
            ckn[...] = dkn_s[BLOCK:2 * BLOCK, :]
            cv[...] = dv_s[BLOCK:2 * BLOCK, :]

        @pl.when(i == nb)
        def _():
            kc = kc_ref[...]
            dk, gk = _heads_norm_bwd(kc, _heads_norm(kc, kw, ek, ek3)[1], kw, ckn[...], ek, ek3)
            dkv_ref[:, :KV_W] = _bf(dk)
            dkv_ref[:, KV_W:] = _bf(cv[...])
            gqw_ref[...] = _group_sum(jnp.broadcast_to(gq_x[...], (8, ATTN_W)), fq_ref[...])[0:1]
            gkw_ref[...] = _group_sum(jnp.broadcast_to(gk_x[...] + gk, (8, KV_W)), fk_ref[...])[0:1]

    last = nb - 1
    cur = lambda w, col=0: pl.BlockSpec((BLOCK, w), lambda i: (jnp.minimum(i, last), col))
    prev = lambda w, col=0: pl.BlockSpec((BLOCK, w), lambda i: (jnp.maximum(jnp.minimum(i, last) - 1, 0), col))
    late = lambda w: pl.BlockSpec((BLOCK, w), lambda i: (jnp.maximum(i - 1, 0), 0))
    sd = jax.ShapeDtypeStruct
    return pl.pallas_call(
        body, name="attn_bwd", grid=(nb + 1,),
        in_specs=[cur(ATTN_W), prev(KV_W, 0), cur(KV_W, 0), prev(KV_W, 1), cur(KV_W, 1),
                  pl.BlockSpec((1, ATTN_HEADS, BLOCK, 2 * BLOCK), lambda i: (jnp.minimum(i, 1), 0, 0, 0)),
                  _full((1, ATTN_HEADS))]
                 + [_full(c.shape) for c in consts + folds] + [cur(ATTN_W), cur(ATTN_HEADS), cur(ATTN_W)],
        out_specs=[cur(ATTN_W), late(2 * KV_W),
                   pl.BlockSpec((ATTN_HEADS, BLOCK, 2 * BLOCK), lambda i: (0, 0, 0)),
                   _full((1, HEAD_DIM)), _full((1, HEAD_DIM)), _full((1, ATTN_HEADS))],
        out_shape=[sd((s, ATTN_W), BF), sd((s, 2 * KV_W), BF),
                   sd((ATTN_HEADS, BLOCK, 2 * BLOCK), F32), sd((1, HEAD_DIM), F32), sd((1, HEAD_DIM), F32),
                   sd((1, ATTN_HEADS), F32)],
        scratch_shapes=[pltpu.VMEM((BLOCK, KV_W), F32), pltpu.VMEM((BLOCK, KV_W), F32),
                        pltpu.VMEM((BLOCK, ATTN_W), F32), pltpu.VMEM((2 * BLOCK, KV_W), F32),
                        pltpu.VMEM((2 * BLOCK, KV_W), F32), pltpu.VMEM((1, ATTN_W), F32), pltpu.VMEM((1, KV_W), F32)],
        compiler_params=_params(dimension_semantics=("arbitrary",)),
    )(q, kv, kv, kv, kv, bias, sinks, *consts, *folds, o_att, lse, d_o)


def _ssd_bwd(xbc, conv_all, dt_raw, conv_w, dt_bias, a_log, dsk_x, e_mat, e3t, hprev_all, dy_all):
    s = xbc.shape[0]
    nc = s // BLOCK
    ch = 1
    rows = ch * BLOCK
    nsteps = nc // ch
    gw = SSM_R * SSM_P
    b0, c0 = SSM_W, SSM_W + SSM_G * SSM_N

    def body(x_ref, conv_ref, dtr_ref, cw_ref, dtb_ref, alog_ref, dsk_ref, e_ref, e3_ref, hp_ref, dy_ref,
             dx_ref, ddt_ref, gcw_ref, gcb_ref, gdtb_ref, galog_ref, gdsk_ref,
             dh, nhead, gdskx, dxdt_s, dbc_s, dxd_s):
        def chunk_bwd(j):
            rs = slice(j * BLOCK, (j + 1) * BLOCK)
            conv = conv_ref[rs, :]
            sg, xact, u, dt, a, trilb, acum, dt_x, acum_x = _ssd_common(conv, dtr_ref[rs, :], dtb_ref, alog_ref, e3_ref)
            xs = xact[:, :SSM_W]
            acum_t = acum.T
            ea_x = jnp.exp2(acum_x)
            last_x = acum_x[BLOCK - 1:BLOCK, :]
            dte_x = jnp.exp2(last_x - acum_x)
            cd_x = jnp.exp2(last_x)
            xdt = xs * dt_x
            xw = xdt * dte_x
            hprev = hp_ref[j]
            dhn = dh[...]
            dy = dy_ref[rs, :]
            gdskx[...] += jnp.sum(dy * xs, axis=0, keepdims=True)
            dyea = dy * ea_x
            lane = lax.broadcasted_iota(jnp.int32, (BLOCK, SSM_HEADS), 1)
            dacum = jnp.zeros((BLOCK, SSM_HEADS), F32)
            dacc_x, dlast_x = [], []
            sls = [slice(g * gw, (g + 1) * gw) for g in range(SSM_G)]
            bgs = [_bf(xact[:, b0 + g * SSM_N:b0 + (g + 1) * SSM_N]) for g in range(SSM_G)]
            cgs = [_bf(xact[:, c0 + g * SSM_N:c0 + (g + 1) * SSM_N]) for g in range(SSM_G)]
            hpgs = [_bf(hprev[:, sl]) for sl in sls]
            dhgs = [_bf(dhn[:, sl]) for sl in sls]
            dyeags = [_bf(dyea[:, sl]) for sl in sls]
            xwgs = [_bf(xw[:, sl]) for sl in sls]
            xdt_b, dy_b = _bf(xdt), _bf(dy)
            low_half = lax.broadcasted_iota(jnp.int32, (BLOCK, 2 * SSM_P), 1) < SSM_P
            cbs = [_dot_nt(cgs[g], bgs[g]) for g in range(SSM_G)]
            gmats = [_dot(cgs[g], hpgs[g]) for g in range(SSM_G)]
            dxws = [_dot(bgs[g], dhgs[g]) for g in range(SSM_G)]
            dcgs = [_dot_nt(dyeags[g], hpgs[g]) for g in range(SSM_G)]
            dbgs = [_dot_nt(xwgs[g], dhgs[g]) for g in range(SSM_G)]
            for g in range(SSM_G):
                sl = sls[g]
                dh[:, sl] = dhn[:, sl] * cd_x[:, sl] + _dot_tn(cgs[g], dyeags[g])
                dxdt_s[:, sl] = dxws[g] * dte_x[:, sl]
                dacc_x.append(dy[:, sl] * gmats[g] * ea_x[:, sl] - dxws[g] * xw[:, sl])
                dlast_x.append(jnp.sum(dxws[g] * xw[:, sl], axis=0, keepdims=True)
                               + jnp.sum(dhn[:, sl] * hprev[:, sl], axis=0, keepdims=True) * cd_x[:, sl])
            for g in range(SSM_G):
                bg, cg, cb, dbg, dcg = bgs[g], cgs[g], cbs[g], dbgs[g], dcgs[g]
                hss = [slice((g * SSM_R + r) * SSM_P, (g * SSM_R + r + 1) * SSM_P) for r in range(SSM_R)]
                lms = [jnp.exp2(jnp.where(trilb, acum[:, g * SSM_R + r:g * SSM_R + r + 1]
                                         - acum_t[g * SSM_R + r:g * SSM_R + r + 1, :], -1e30)) for r in range(SSM_R)]
                mms = [cb * lm for lm in lms]
                mmbs = [_bf(mm) for mm in mms]
                dms = []
                for r in range(0, SSM_R, 2):
                    pair = slice(hss[r].start, hss[r + 1].stop)
                    xp, dyp = xdt_b[:, pair], dy_b[:, pair]
                    dmp = _dot_nt(dyp, jnp.concatenate([jnp.where(low_half, xp, 0), jnp.where(low_half, 0, xp)], axis=0))
                    dms += [dmp[:, :BLOCK], dmp[:, BLOCK:]]
                    dxd_s[:, pair] = _dot_tn(jnp.concatenate([mmbs[r], mmbs[r + 1]], axis=0),
                                             jnp.concatenate([jnp.where(low_half, dyp, 0), jnp.where(low_half, 0, dyp)], axis=0))
                dcb = sum(dms[r] * lms[r] for r in range(SSM_R))
                wms = [dms[r] * mms[r] for r in range(SSM_R)]
                antis = [_bf(wm - wm.T) for wm in wms]
                for r in range(SSM_R):
                    dacum = dacum + _dot(antis[r], (lane == g * SSM_R + r).astype(BF))
                dcbb = _bf(dcb)
                dbc_s[:, g * SSM_N:(g + 1) * SSM_N] = dbg + _dot_tn(dcbb, cg)
                dbc_s[:, SSM_G * SSM_N + g * SSM_N:SSM_G * SSM_N + (g + 1) * SSM_N] = dcg + _dot(dcbb, bg)
            dxdt = dxdt_s[...] + dxd_s[...]
            dxs = dy * dsk_ref[...] + dxdt * dt_x
            red = _group_sum(jnp.concatenate(
                [dxdt * xs, jnp.concatenate(dacc_x, axis=1),
                 jnp.broadcast_to(jnp.concatenate(dlast_x, axis=1), (8, SSM_W))], axis=0), e_ref[...])
            row = lax.broadcasted_iota(jnp.int32, (BLOCK, SSM_HEADS), 0)
            dacum = dacum + red[BLOCK:2 * BLOCK] + jnp.where(row == BLOCK - 1, red[2 * BLOCK:2 * BLOCK + 1], 0.0)
            ddta = _exact_left(_triu().astype(BF), dacum)
            ddt = red[:BLOCK] + ddta * a
            galog_ref[...] += jnp.sum(ddta * dt, axis=0, keepdims=True) * a
            du = ddt * _sig(u)
            ddt_ref[rs, :] = _bf(du)
            gdtb_ref[...] += jnp.sum(du, axis=0, keepdims=True)
            dconv = jnp.concatenate([dxs, dbc_s[...]], axis=1) * _dsilu(conv, sg, xact)
            gcb_ref[...] += jnp.sum(dconv, axis=0, keepdims=True)
            ext2 = jnp.concatenate([dconv, nhead[...]], axis=0)
            ahead = [pltpu.roll(ext2, BLOCK + 8 - (CONV_K - 1 - j), axis=0)[0:BLOCK] if j < CONV_K - 1 else dconv
                     for j in range(CONV_K)]
            dx_ref[rs, :] = _bf(sum(ahead[j] * cw_ref[j:j + 1, :] for j in range(CONV_K)))
            xraw = x_ref[rs, :]
            gcw_ref[...] += jnp.concatenate([jnp.sum(ahead[j] * xraw, axis=0, keepdims=True) for j in range(CONV_K)], axis=0)
            nhead[...] = dconv[0:8]

        i = pl.program_id(0)

        @pl.when(i == 0)
        def _():
            for ref in (dh, nhead, gdskx, gcw_ref, gcb_ref, gdtb_ref, galog_ref, gdsk_ref):
                ref[...] = jnp.zeros_like(ref)

        for j in reversed(range(ch)):
            chunk_bwd(j)

        @pl.when(i == nsteps - 1)
        def _():
            gdsk_ref[...] = _group_sum(jnp.broadcast_to(gdskx[...], (8, SSM_W)), e_ref[...])[0:1]

    chunk = lambda w: pl.BlockSpec((rows, w), lambda i: (nsteps - 1 - i, 0))
    sd = jax.ShapeDtypeStruct
    return pl.pallas_call(
        body, name="ssd_bwd", grid=(nsteps,),
        in_specs=[chunk(XBC_W), chunk(XBC_W),
                  chunk(SSM_HEADS), _full((CONV_K, XBC_W)), _full((1, SSM_HEADS)),
                  _full((1, SSM_HEADS)), _full((1, SSM_W)), _full((SSM_W, SSM_HEADS)), _full((3 * SSM_HEADS, SSM_W)),
                  pl.BlockSpec((ch, SSM_N, SSM_W), lambda i: (nsteps - 1 - i, 0, 0)), chunk(SSM_W)],
        out_specs=[chunk(XBC_W), chunk(SSM_HEADS), _full((CONV_K, XBC_W)), _full((1, XBC_W)),
                   _full((1, SSM_HEADS)), _full((1, SSM_HEADS)), _full((1, SSM_HEADS))],
        out_shape=[sd((s, XBC_W), BF), sd((s, SSM_HEADS), BF), sd((CONV_K, XBC_W), F32), sd((1, XBC_W), F32),
                   sd((1, SSM_HEADS), F32), sd((1, SSM_HEADS), F32), sd((1, SSM_HEADS), F32)],
        scratch_shapes=[pltpu.VMEM((SSM_N, SSM_W), F32), pltpu.VMEM((8, XBC_W), F32),
                        pltpu.VMEM((1, SSM_W), F32), pltpu.VMEM((BLOCK, SSM_W), F32),
                        pltpu.VMEM((BLOCK, 2 * SSM_G * SSM_N), F32), pltpu.VMEM((BLOCK, SSM_W), F32)],
        compiler_params=_params(dimension_semantics=("arbitrary",)),
    )(xbc, conv_all, dt_raw, conv_w, dt_bias, a_log, dsk_x, e_mat, e3t, hprev_all, dy_all)


def _dh(x, dout, norm_w, scale, dsegs, w_t, tm=256):
    s = x.shape[0]

    def body(x_ref, dout_ref, nw_ref, sc_ref, *rest):
        d_refs, w_hbm = rest[:NSEG], rest[NSEG]
        gx_ref, dshift_ref, dscale_ref, gnw_ref = rest[NSEG + 1:NSEG + 5]
        w_vm, sem = rest[NSEG + 5], rest[NSEG + 6]
        first = pl.program_id(0) == 0
        cps = [pltpu.make_async_copy(w_hbm.at[SEG_OFF[j]:SEG_OFF[j + 1], :], w_vm.at[SEG_OFF[j]:SEG_OFF[j + 1], :], sem.at[j])
               for j in range(NSEG)]

        def tile(waiting):
            dh = None
            for j in range(NSEG):
                if waiting:
                    cps[j].wait()
                part = _dot(d_refs[j][...], w_vm[SEG_OFF[j]:SEG_OFF[j + 1], :])
                dh = part if dh is None else dh + part
            xv = x_ref[...]
            r = lax.rsqrt(jnp.mean(xv * xv, axis=-1, keepdims=True) + EPS)
            xn = xv * r
            nw = nw_ref[...]
            sc1 = 1.0 + sc_ref[...]
            dshift_ref[...] += jnp.sum(dh, axis=0, keepdims=True)
            dhxn = jnp.sum(dh * xn, axis=0, keepdims=True)
            dscale_ref[...] += dhxn * nw
            gnw_ref[...] += dhxn * sc1
            dxn = dh * (nw * sc1)
            gx_ref[...] = dout_ref[...] + r * (dxn - xn * jnp.mean(xn * dxn, axis=-1, keepdims=True))

        @pl.when(first)
        def _():
            for cp in cps:
                cp.start()
            for ref in (dshift_ref, dscale_ref, gnw_ref):
                ref[...] = jnp.zeros_like(ref)
            tile(True)

        @pl.when(jnp.logical_not(first))
        def _():
            tile(False)

    vec = _full((1, D_MODEL))
    sd = jax.ShapeDtypeStruct
    return pl.pallas_call(
        body, name="dh", grid=(s // tm,),
        in_specs=[_rows(tm, D_MODEL), _rows(tm, D_MODEL), vec, vec] + [_rows(tm, w) for w in SEG_W] + [ANY],
        out_specs=[_rows(tm, D_MODEL), vec, vec, vec],
        out_shape=[sd((s, D_MODEL), F32), sd((1, D_MODEL), F32), sd((1, D_MODEL), F32), sd((1, D_MODEL), F32)],
        scratch_shapes=[pltpu.VMEM((IN_W, D_MODEL), BF), pltpu.SemaphoreType.DMA((NSEG,))],
        compiler_params=_params(dimension_semantics=("arbitrary",)),
    )(x, dout, norm_w, scale, *dsegs, w_t)


def _gw_seg(h, dseg, name, tm=1024):
    s, w = dseg.shape
    tn = w
    tm = min(tm, s)
    nm = s // tm

    def body(h_ref, d_ref, o_ref, acc):
        m = pl.program_id(1)

        @pl.when(m == 0)
        def _():
            acc[...] = jnp.zeros_like(acc)

        acc[...] += _dot_tn(d_ref[...], h_ref[...])

        @pl.when(m == nm - 1)
        def _():
            o_ref[...] = _bf(acc[...])

    return pl.pallas_call(
        body, name=name, grid=(w // tn, nm),
        in_specs=[pl.BlockSpec((tm, D_MODEL), lambda n, m: (m, 0)), pl.BlockSpec((tm, tn), lambda n, m: (m, n))],
        out_specs=pl.BlockSpec((tn, D_MODEL), lambda n, m: (n, 0)),
        out_shape=jax.ShapeDtypeStruct((w, D_MODEL), BF),
        scratch_shapes=[pltpu.VMEM((tn, D_MODEL), F32)],
        compiler_params=_params(dimension_semantics=("arbitrary", "arbitrary")),
    )(h, dseg)


def _gw_in(h, dsegs):
    return [_gw_seg(h, d, "gw_in_%d" % j) for j, d in enumerate(dsegs)]


def _local_step(x, tgt, shift, scale, gate, w_t, rows_fn, norm_w, qnw, knw, rel_bias, sinks,
                conv_w, conv_b, dt_bias, a_log, d_skip, ssm_nw, after_mid=None, after_gw=None):
    oh_t = _bucket_onehot_t()
    bias = _masked_bias(_bias_dense(rel_bias.T, oh_t).reshape(ATTN_HEADS, BLOCK, 2 * BLOCK))
    *segs, h = _inproj(x, norm_w, scale, shift, w_t)
    q, kv, zam, xbc, dtr, gab = segs
    consts = _attn_consts(qnw, knw)
    o_att, lse = _attn_fwd(q, kv, bias, sinks, consts)
    e_mat, e3t = _membership(SSM_W, SSM_P, SSM_HEADS)
    dsk_x = jnp.repeat(d_skip, SSM_P, axis=1)
    ypre, hprev, conv = _ssd_fwd(xbc, dtr, conv_w, conv_b, dt_bias, a_log, dsk_x, e3t)
    (dout, d_o, dzam, dyp, dgab, yag, dy_a, yn, dy_b, merged, dob, g_ssm_nw, dgate, loss) = _mid(
        x, tgt, o_att, zam, ypre, gab, gate, ssm_nw, rows_fn(ypre))
    g_wap = _gw_seg(dy_a, yag, "gw_attn_proj")
    g_wsp = _gw_seg(dy_b, yn, "gw_ssm_proj")
    g_wout = _gw_seg(dob, merged, "gw_out")
    zero = after_mid(g_wap, g_wsp, g_wout) if after_mid is not None else 0.0
    dq, dkv, dss, g_qnw, g_knw, g_sinks = _attn_bwd(q, kv, bias, sinks + zero, consts, o_att, lse, d_o)
    g_rel = _bias_grad(dss.reshape(ATTN_HEADS, BLOCK * 2 * BLOCK), oh_t).T
    dxbc, ddt, g_cw, g_cb, g_dtb, g_alog, g_dsk = _ssd_bwd(
        xbc, conv, dtr, conv_w, dt_bias, a_log, dsk_x, e_mat, e3t, hprev, dyp)
    dsegs = (dq, dkv, dzam, dxbc, ddt, dgab)
    g_ws = _gw_in(h, dsegs)
    zero = after_gw(g_ws) if after_gw is not None else 0.0
    gx, dshift, dscale, g_nw = _dh(x, dout, norm_w + zero, scale, dsegs, w_t)
    return dict(loss=loss, grad_x=gx, dmod=jnp.concatenate([dshift, dscale, dgate], axis=1), g_ws=g_ws,
                g_wap=g_wap, g_wsp=g_wsp, g_wout=g_wout, g_norm_w=g_nw, g_qnw=g_qnw, g_knw=g_knw, g_rel=g_rel,
                g_sinks=g_sinks, g_conv_w=g_cw, g_conv_b=g_cb, g_dt_bias=g_dtb, g_a_log=g_alog, g_d_skip=g_dsk,
                g_ssm_nw=g_ssm_nw)


def _me():
    return lax.axis_index("x"), lax.axis_index("y"), lax.axis_index("c")


def _flip(v, bit):
    return 1 - v if bit else v


def _ag_direct(v, name):
    def body(v_ref, out_ref, send_sems, recv_sems, local_sem):
        x, y, c = _me()
        me = 4 * x + 2 * y + c
        mine = pltpu.make_async_copy(v_ref, out_ref.at[me], local_sem)
        mine.start()
        peers = [(_flip(x, k >> 2 & 1), _flip(y, k >> 1 & 1), _flip(c, k & 1)) for k in range(1, N_DEV)]
        sends = [pltpu.make_async_remote_copy(
            src_ref=v_ref, dst_ref=out_ref.at[me], send_sem=send_sems.at[j], recv_sem=recv_sems.at[j],
            device_id=p, device_id_type=MESH) for j, p in enumerate(peers)]
        for cp in sends:
            cp.start()
        for j, (px, py, pc) in enumerate(peers):
            pltpu.make_async_remote_copy(
                src_ref=v_ref, dst_ref=out_ref.at[4 * px + 2 * py + pc], send_sem=send_sems.at[j],
                recv_sem=recv_sems.at[j], device_id=(px, py, pc), device_id_type=MESH).wait_recv()
        for cp in sends:
            cp.wait_send()
        mine.wait()

    vm = pl.BlockSpec(memory_space=pltpu.VMEM)
    return pl.pallas_call(
        body, name=name, out_shape=jax.ShapeDtypeStruct((N_DEV,) + v.shape, v.dtype),
        in_specs=[vm], out_specs=vm,
        scratch_shapes=[pltpu.SemaphoreType.DMA((N_DEV - 1,)), pltpu.SemaphoreType.DMA((N_DEV - 1,)),
                        pltpu.SemaphoreType.DMA],
        compiler_params=_params(),
    )(v)


def _gather_mod(v, w_ada, b_piece):
    ncols = w_ada.shape[1]

    def body(v_ref, w_ref, b_ref, rows_ref, mods_ref, piece, send_sems, recv_sems, local_sems):
        x, y, c = _me()
        me = 4 * x + 2 * y + c
        peers = _peers(x, y, c)

        def exchange(src, dst, rnd):
            mine = pltpu.make_async_copy(src, dst.at[me], local_sems.at[rnd])
            mine.start()
            sends = [pltpu.make_async_remote_copy(
                src_ref=src, dst_ref=dst.at[me], send_sem=send_sems.at[rnd, j], recv_sem=recv_sems.at[rnd, j],
                device_id=p, device_id_type=MESH) for j, p in enumerate(peers)]
            for cp in sends:
                cp.start()
            for j, (px, py, pc) in enumerate(peers):
                pltpu.make_async_remote_copy(
                    src_ref=src, dst_ref=dst.at[4 * px + 2 * py + pc], send_sem=send_sems.at[rnd, j],
                    recv_sem=recv_sems.at[rnd, j], device_id=(px, py, pc), device_id_type=MESH).wait_recv()
            for cp in sends:
                cp.wait_send()
            mine.wait()

        exchange(v_ref, rows_ref, 0)
        c_all = rows_ref[:, 0, :D_MODEL]
        piece[...] = _dot(_bf(_silu(c_all)), _bf(w_ref[...])) + b_ref[...]
        exchange(piece, mods_ref, 1)

    vm = pl.BlockSpec(memory_space=pltpu.VMEM)
    return pl.pallas_call(
        body, name="gather_mod",
        out_shape=(jax.ShapeDtypeStruct((N_DEV,) + v.shape, F32), jax.ShapeDtypeStruct((N_DEV, N_DEV, ncols), F32)),
        in_specs=[vm, vm, vm], out_specs=(vm, vm),
        scratch_shapes=[pltpu.VMEM((N_DEV, ncols), F32), pltpu.SemaphoreType.DMA((2, N_DEV - 1)),
                        pltpu.SemaphoreType.DMA((2, N_DEV - 1)), pltpu.SemaphoreType.DMA((2,))],
        compiler_params=_params(),
    )(v, w_ada, b_piece)


def _ag_relayed(v, name, chunks=1):
    rows = v.shape[0] // chunks
    assert rows * chunks == v.shape[0] and rows % 8 == 0

    def body(v_ref, out_ref, token, send_sems, recv_sems, local_sem):
        token[...] = jnp.zeros_like(token)
        x, y, c = _me()
        flip_x, flip_y = 1 - x, 1 - y
        ax, ay = c * x + (1 - c) * flip_x, c * flip_y + (1 - c) * y
        bx, by = c * flip_x + (1 - c) * x, c * y + (1 - c) * flip_y
        me, sib = (x, y, c), (x, y, 1 - c)
        a, b, dg = (ax, ay, c), (bx, by, c), (flip_x, flip_y, c)
        sa, sb, sdg = (bx, by, 1 - c), (ax, ay, 1 - c), (flip_x, flip_y, 1 - c)

        def piece(ref, k):
            return ref.at[pl.ds(k * rows, rows), :]

        def slot(px, py, pc):
            return out_ref.at[4 * px + 2 * py + pc]

        def copy(n, k, block, to, src=None):
            return pltpu.make_async_remote_copy(
                src_ref=piece(slot(*block) if src is None else src, k), dst_ref=piece(slot(*block), k),
                send_sem=send_sems.at[n * chunks + k], recv_sem=recv_sems.at[n * chunks + k],
                device_id=to, device_id_type=MESH)

        mine = pltpu.make_async_copy(v_ref, slot(*me), local_sem)
        mine.start()
        started = [copy(n, k, me, to, src=v_ref) for k in range(chunks) for n, to in ((1, a), (2, b), (0, sib))]
        for cp in started:
            cp.start()

        def arrived(n, k, block, then):
            copy(n, k, block, me).wait_recv()
            for n2, to in then:
                started.append(copy(n2, k, block, to))
                started[-1].start()

        for k in range(chunks):
            arrived(1, k, a, ((3, b), (4, sib)))
            arrived(2, k, b, ((5, sib),))
        for k in range(chunks):
            arrived(3, k, dg, ((6, sib),))
        for k in range(chunks):
            for n, block in ((0, sib), (4, sa), (5, sb), (6, sdg)):
                copy(n, k, block, me).wait_recv()
        for cp in started:
            cp.wait_send()
        mine.wait()

    out, token = pl.pallas_call(
        body, name=name,
        out_shape=(jax.ShapeDtypeStruct((N_DEV,) + v.shape, v.dtype), jax.ShapeDtypeStruct((8, 128), v.dtype)),
        in_specs=[ANY], out_specs=(ANY, pl.BlockSpec(memory_space=pltpu.VMEM)),
        scratch_shapes=[pltpu.SemaphoreType.DMA((7 * chunks,)), pltpu.SemaphoreType.DMA((7 * chunks,)),
                        pltpu.SemaphoreType.DMA],
        compiler_params=_params(),
    )(v)
    return out, token[0:1, 0:1]


HBM = pl.BlockSpec(memory_space=pltpu.HBM)
SEM = pl.BlockSpec(memory_space=pltpu.SEMAPHORE)
EFFECT = pltpu.SideEffectType.DATAFLOW_SIDE_EFFECTING


def _peers(x, y, c):
    return [(_flip(x, k >> 2 & 1), _flip(y, k >> 1 & 1), _flip(c, k & 1)) for k in range(1, N_DEV)]


def _exchange_start(src, land, gather, name):
    def body(src_ref, land_ref, send_sems, recv_sems, src_thru, land_thru, token):
        x, y, c = _me()
        me = 4 * x + 2 * y + c
        for j, (px, py, pc) in enumerate(_peers(x, y, c)):
            pltpu.make_async_remote_copy(
                src_ref=src_ref if gather else src_ref.at[4 * px + 2 * py + pc], dst_ref=land_ref.at[me],
                send_sem=send_sems.at[j], recv_sem=recv_sems.at[j], device_id=(px, py, pc), device_id_type=MESH).start()
        token[...] = jnp.zeros_like(token)

    sems = pltpu.SemaphoreType.DMA((N_DEV - 1,))
    out = pl.pallas_call(
        body, name=name,
        out_shape=(sems, sems, pltpu.HBM(src.shape, src.dtype), pltpu.HBM(land.shape, land.dtype),
                   jax.ShapeDtypeStruct((8, 128), F32)),
        in_specs=(HBM, HBM), out_specs=(SEM, SEM, HBM, HBM, pl.BlockSpec(memory_space=pltpu.VMEM)),
        input_output_aliases={0: 2, 1: 3},
        compiler_params=pltpu.CompilerParams(has_side_effects=EFFECT),
    )(pltpu.with_memory_space_constraint(src, pltpu.HBM), pltpu.with_memory_space_constraint(land, pltpu.HBM))
    return out[:4], out[4][0, 0]


def _exchange_wait(started, after, gather, name):
    send_sems, recv_sems, src_thru, land_thru = started

    def body(src_ref, land_ref, send_sems, recv_sems, after_ref, src_dead, got_ref):
        x, y, c = _me()
        for j, (px, py, pc) in enumerate(_peers(x, y, c)):
            pid = 4 * px + 2 * py + pc
            cp = pltpu.make_async_remote_copy(
                src_ref=src_ref if gather else src_ref.at[pid], dst_ref=land_ref.at[pid],
                send_sem=send_sems.at[j], recv_sem=recv_sems.at[j], device_id=(px, py, pc), device_id_type=MESH)
            cp.wait_send()
            cp.wait_recv()

    return pl.pallas_call(
        body, name=name,
        out_shape=(pltpu.HBM(src_thru.shape, src_thru.dtype), pltpu.HBM(land_thru.shape, land_thru.dtype)),
        in_specs=(HBM, HBM, SEM, SEM, ANY), out_specs=(HBM, HBM), input_output_aliases={0: 0, 1: 1},
        compiler_params=pltpu.CompilerParams(has_side_effects=EFFECT),
    )(src_thru, land_thru, send_sems, recv_sems, after)[1]


def _silu(a):
    return a * _sig(a)


def _gw_ada(c_all, dmod_piece):
    def body(c_ref, d_ref, o_ref):
        o_ref[...] = _dot_tn(_bf(_silu(c_ref[...])), _bf(d_ref[...]))

    return pl.pallas_call(
        body, name="gw_ada", out_shape=jax.ShapeDtypeStruct((c_all.shape[1], dmod_piece.shape[1]), F32),
        compiler_params=_params(),
    )(c_all, dmod_piece)


def _adam(parts, w, m, v, name):
    k, r, n = parts.shape
    if r <= 256 or r % 256 == 0:
        tr, tn = min(r, 256), n
    else:
        tr, tn = r, 256
    assert r % tr == 0 and n % tn == 0

    def body(p_ref, w_ref, m_ref, v_ref, g_ref, d_ref, nm_ref, nv_ref):
        g = p_ref[0].astype(F32)
        for j in range(1, k):
            g = g + p_ref[j].astype(F32)
        g_ref[...] = g
        d_ref[...], nm_ref[...], nv_ref[...] = _adam_math(g, w_ref[...], m_ref[...], v_ref[...])

    blk = pl.BlockSpec((tr, tn), lambda i, j: (i, j))
    return pl.pallas_call(
        body, name=name, grid=(r // tr, n // tn),
        in_specs=[pl.BlockSpec((k, tr, tn), lambda i, j: (0, i, j)), blk, blk, blk],
        out_specs=[blk, blk, blk, blk],
        out_shape=[jax.ShapeDtypeStruct((r, n), F32)] * 4,
        compiler_params=_params(dimension_semantics=("arbitrary", "arbitrary")),
    )(parts, w, m, v)


def _adam_math(g, w, m, v):
    m_new = ADAM_B1 * m + (1.0 - ADAM_B1) * g
    v_new = ADAM_B2 * v + (1.0 - ADAM_B2) * jnp.square(g)
    m_hat = m_new / (1.0 - ADAM_B1 ** ADAM_STEP)
    v_hat = v_new / (1.0 - ADAM_B2 ** ADAM_STEP)
    return -ADAM_LR * (m_hat / (jnp.sqrt(v_hat) + ADAM_EPS) + ADAM_WD * w), m_new, v_new


_SMALL = (("b_ada", 3 * D_MODEL), ("norm_w", D_MODEL), ("q_norm_w", HEAD_DIM), ("k_norm_w", HEAD_DIM),
          ("rel_bias", REL_BUCKETS * ATTN_HEADS), ("sinks", ATTN_HEADS), ("conv_b", XBC_W), ("dt_bias", SSM_HEADS),
          ("a_log", SSM_HEADS), ("d_skip", SSM_HEADS), ("ssm_norm_w", SSM_W))
_SLOT = tuple(-(-n // 128) * 128 for _, n in _SMALL)
_SLOT_OFF = tuple(int(o) for o in np.cumsum((0,) + _SLOT))
_LOSS_OFF = _SLOT_OFF[-1]
_CW_OFF = _LOSS_OFF + 128
_PACK_N = _CW_OFF + CONV_K * XBC_W


def _pack_partials(small, loss, g_conv_w):
    parts = []
    for (name, n), slot in zip(_SMALL, _SLOT):
        parts.append(small[name].reshape(1, n))
        if slot > n:
            parts.append(jnp.zeros((1, slot - n), F32))
    parts += [loss.reshape(1, 1), jnp.zeros((1, 127), F32), g_conv_w.reshape(1, CONV_K * XBC_W)]
    return jnp.concatenate(parts, axis=1)


def _adam_small(pack_all, w, m, v):
    names = [name for name, _ in _SMALL]

    def body(p_ref, *rest):
        ins, outs = rest[:3 * len(names)], rest[3 * len(names):]

        def total(off, n):
            g = p_ref[0, :, off:off + n]
            for d in range(1, N_DEV):
                g = g + p_ref[d, :, off:off + n]
            return g

        for j, (name, n) in enumerate(_SMALL):
            g = total(_SLOT_OFF[j], n)
            delta, m_new, v_new = _adam_math(g, ins[3 * j][...], ins[3 * j + 1][...], ins[3 * j + 2][...])
            outs[4 * j][...] = g
            outs[4 * j + 1][...] = delta
            outs[4 * j + 2][...] = m_new
            outs[4 * j + 3][...] = v_new
        outs[-1][...] = total(_LOSS_OFF, 1)

    flat = []
    for name, n in _SMALL:
        flat += [w[name].reshape(1, n), m[name].reshape(1, n), v[name].reshape(1, n)]
    out_shape = [jax.ShapeDtypeStruct((1, n), F32) for _, n in _SMALL for _ in range(4)] + [jax.ShapeDtypeStruct((1, 1), F32)]
    out = pl.pallas_call(body, name="adam_small", out_shape=out_shape, compiler_params=_params())(pack_all, *flat)
    res = {name: [out[4 * j + t].reshape(w[name].shape) for t in range(4)] for j, name in enumerate(names)}
    return res, out[-1]


WEIGHTS = ("w_ada", "b_ada", "norm_w", "w_in", "q_norm_w", "k_norm_w", "rel_bias", "sinks", "conv_w", "conv_b",
           "dt_bias", "a_log", "d_skip", "ssm_norm_w", "w_attn_proj", "w_ssm_proj", "w_out")


def kernel(x, c, w_ada, b_ada, norm_w, w_in, q_norm_w, k_norm_w, rel_bias, sinks, conv_w, conv_b, dt_bias, a_log, d_skip, ssm_norm_w, w_attn_proj, w_ssm_proj, w_out, loss_target, m_w_ada, m_b_ada, m_norm_w, m_w_in, m_q_norm_w, m_k_norm_w, m_rel_bias, m_sinks, m_conv_w, m_conv_b, m_dt_bias, m_a_log, m_d_skip, m_ssm_norm_w, m_w_attn_proj, m_w_ssm_proj, m_w_out, v_w_ada, v_b_ada, v_norm_w, v_w_in, v_q_norm_w, v_k_norm_w, v_rel_bias, v_sinks, v_conv_w, v_conv_b, v_dt_bias, v_a_log, v_d_skip, v_ssm_norm_w, v_w_attn_proj, v_w_ssm_proj, v_w_out):
    w = dict(w_ada=w_ada, b_ada=b_ada, norm_w=norm_w, w_in=w_in, q_norm_w=q_norm_w, k_norm_w=k_norm_w,
             rel_bias=rel_bias, sinks=sinks, conv_w=conv_w, conv_b=conv_b, dt_bias=dt_bias, a_log=a_log,
             d_skip=d_skip, ssm_norm_w=ssm_norm_w, w_attn_proj=w_attn_proj, w_ssm_proj=w_ssm_proj, w_out=w_out)
    m = dict(w_ada=m_w_ada, b_ada=m_b_ada, norm_w=m_norm_w, w_in=m_w_in, q_norm_w=m_q_norm_w, k_norm_w=m_k_norm_w,
             rel_bias=m_rel_bias, sinks=m_sinks, conv_w=m_conv_w, conv_b=m_conv_b, dt_bias=m_dt_bias, a_log=m_a_log,
             d_skip=m_d_skip, ssm_norm_w=m_ssm_norm_w, w_attn_proj=m_w_attn_proj, w_ssm_proj=m_w_ssm_proj, w_out=m_w_out)
    v = dict(w_ada=v_w_ada, b_ada=v_b_ada, norm_w=v_norm_w, w_in=v_w_in, q_norm_w=v_q_norm_w, k_norm_w=v_k_norm_w,
             rel_bias=v_rel_bias, sinks=v_sinks, conv_w=v_conv_w, conv_b=v_conv_b, dt_bias=v_dt_bias, a_log=v_a_log,
             d_skip=v_d_skip, ssm_norm_w=v_ssm_norm_w, w_attn_proj=v_w_attn_proj, w_ssm_proj=v_w_ssm_proj, w_out=v_w_out)
    me = 4 * lax.axis_index("x") + 2 * lax.axis_index("y") + lax.axis_index("c")
    ada_n = w_ada.shape[2]
    in_n = w_in.shape[2]
    cw_n = conv_w.shape[2]

    b_piece = lax.dynamic_slice_in_dim(b_ada, me * ada_n, ada_n, axis=1)
    first, mod_all = _gather_mod(jnp.concatenate([c, conv_w[0].reshape(1, CONV_K * cw_n)], axis=1), w_ada[0], b_piece)
    first = first[:, 0]
    c_all = first[:, :D_MODEL]
    conv_w_full = first[:, D_MODEL:].reshape(N_DEV, CONV_K, cw_n).transpose(1, 0, 2).reshape(CONV_K, XBC_W)
    mod = lax.dynamic_index_in_dim(mod_all, me, axis=1, keepdims=False).reshape(1, 3 * D_MODEL)
    shift, scale, gate = mod[:, :D_MODEL], mod[:, D_MODEL:2 * D_MODEL], mod[:, 2 * D_MODEL:]

    pad = -in_n % 24
    w_t, zero = _ag_relayed(jnp.pad(w_in[0].T.astype(BF), ((0, pad), (0, 0))), "ag_w_in", chunks=3)
    w_t = w_t[:, :in_n].reshape(N_DEV * in_n, D_MODEL)

    def with_mine(blocks, mine):
        return lax.dynamic_update_index_in_dim(lax.empty(blocks, mine.dtype), mine, me, axis=0)

    rows = jnp.concatenate([w_attn_proj[0], w_ssm_proj[0], w_out[0]], axis=0).astype(BF) + zero
    r_ap, r_sp = w_attn_proj.shape[1], w_ssm_proj.shape[1]
    rows_started, zero = _exchange_start(rows, with_mine((N_DEV,) + rows.shape, rows), True, "ag_rows_start")

    def rows_fn(after):
        return _exchange_wait(rows_started, after, True, "ag_rows_wait")

    started = {}

    def send_blocks(key, g, name):
        started[key], zero = _exchange_start(
            g, with_mine(g.shape, lax.dynamic_index_in_dim(g, me, axis=0, keepdims=False)), False, name)
        return zero

    def after_mid(g_wap, g_wsp, g_wout):
        return send_blocks("rows", jnp.concatenate(
            [g_wap.reshape(N_DEV, r_ap, D_MODEL), g_wsp.reshape(N_DEV, r_sp, D_MODEL),
             g_wout.reshape(N_DEV, r_ap, D_MODEL)], axis=1), "rs_rows_start")

    def after_gw(g_ws):
        return send_blocks("in", jnp.concatenate(g_ws, axis=0).reshape(N_DEV, in_n, D_MODEL), "rs_in_start")

    r = _local_step(x[0], loss_target[0], shift, scale + zero, gate, w_t, rows_fn, norm_w, q_norm_w, k_norm_w,
                    rel_bias, sinks, conv_w_full, conv_b, dt_bias, a_log, d_skip, ssm_norm_w, after_mid, after_gw)

    small = dict(b_ada=r["dmod"], norm_w=r["g_norm_w"], q_norm_w=r["g_qnw"], k_norm_w=r["g_knw"], rel_bias=r["g_rel"],
                 sinks=r["g_sinks"], conv_b=r["g_conv_b"], dt_bias=r["g_dt_bias"], a_log=r["g_a_log"],
                 d_skip=r["g_d_skip"], ssm_norm_w=r["g_ssm_nw"])
    pack_all = _ag_direct(_pack_partials(small, r["loss"], r["g_conv_w"]), "ag_small")
    res, loss = _adam_small(pack_all, w, m, v)
    loss = loss[0, 0]
    cw_parts = pack_all[:, 0, _CW_OFF:].reshape(N_DEV, CONV_K, XBC_W)
    cw_mine = lax.dynamic_slice_in_dim(cw_parts, me * cw_n, cw_n, axis=2)
    res["conv_w"] = [a[None] for a in _adam(cw_mine, conv_w[0], m_conv_w[0], v_conv_w[0], "adam_conv_w")]

    dmod_piece = lax.dynamic_slice_in_dim(pack_all[:, 0, :3 * D_MODEL], me * ada_n, ada_n, axis=1)
    g_ada = _gw_ada(c_all, dmod_piece)
    res["w_ada"] = [a[None] for a in _adam(g_ada[None], w_ada[0], m_w_ada[0], v_w_ada[0], "adam_w_ada")]

    cat = lambda d: jnp.concatenate([d["w_attn_proj"][0], d["w_ssm_proj"][0], d["w_out"][0]], axis=0)
    rows_res = _adam(_exchange_wait(started["rows"], g_ada, False, "rs_rows_wait"), cat(w), cat(m), cat(v), "adam_w_rows")
    res["w_in"] = [a.T[None] for a in _adam(_exchange_wait(started["in"], rows_res[0], False, "rs_in_wait"),
                                            w_in[0].T, m_w_in[0].T, v_w_in[0].T, "adam_w_in")]
    res["w_attn_proj"] = [a[None, :r_ap] for a in rows_res]
    res["w_ssm_proj"] = [a[None, r_ap:r_ap + r_sp] for a in rows_res]
    res["w_out"] = [a[None, r_ap + r_sp:] for a in rows_res]

    outs = [loss, r["grad_x"][None]]
    for j in range(4):
        outs += [res[name][j] for name in WEIGHTS]
    return tuple(outs)
```

```python
import math

import numpy as np
import jax
import jax.numpy as jnp
from jax import lax
from jax.experimental import pallas as pl
from jax.experimental.pallas import tpu as pltpu

F32 = jnp.float32
BF = jnp.bfloat16
HI = lax.Precision.HIGHEST

D_MODEL = 1024
ATTN_HEADS = 16
KV_HEADS = 4
GRP = ATTN_HEADS // KV_HEADS
HEAD_DIM = 64
ATTN_W = ATTN_HEADS * HEAD_DIM
KV_W = KV_HEADS * HEAD_DIM
BLOCK = 128
REL_BUCKETS = 32
REL_MAX_DIST = 128
SSM_W = 2048
SSM_P = 64
SSM_HEADS = 32
SSM_G = 4
SSM_R = 8
SSM_N = 128
CONV_K = 4
XBC_W = SSM_W + 2 * SSM_G * SSM_N
SEG_W = (ATTN_W, 2 * KV_W, ATTN_W + SSM_W, XBC_W, SSM_HEADS, 2 * D_MODEL)
NSEG = len(SEG_W)
SEG_OFF = tuple(int(v) for v in np.cumsum((0,) + SEG_W))
IN_W = SEG_OFF[-1]
GATE_SEGS = (2, 5)
EPS = 1e-6
N_DEV = 8
ADAM_LR, ADAM_B1, ADAM_B2, ADAM_EPS, ADAM_WD, ADAM_STEP = 0.001, 0.9, 0.999, 1e-08, 0.01, 10
VMEM_LIMIT = 60 * 1024 * 1024
MESH = pl.DeviceIdType.MESH
ANY = pl.BlockSpec(memory_space=pl.ANY)


def _dot(a, b, precision=None):
    return jnp.dot(a, b, preferred_element_type=F32, precision=precision)


def _dot_nt(a, b, precision=None):
    return lax.dot_general(a, b, (((1,), (1,)), ((), ())), preferred_element_type=F32, precision=precision)


def _dot_tn(a, b, precision=None):
    return lax.dot_general(a, b, (((0,), (0,)), ((), ())), preferred_element_type=F32, precision=precision)


def _bf(a):
    return a.astype(BF)


def _sig(a):
    return 0.5 * jnp.tanh(0.5 * a) + 0.5


def _params(**kw):
    return pltpu.CompilerParams(vmem_limit_bytes=VMEM_LIMIT, **kw)


def _full(shape):
    nd = len(shape)
    return pl.BlockSpec(shape, lambda i: (0,) * nd)


def _rows(tm, w):
    return pl.BlockSpec((tm, w), lambda i: (i, 0))


def _inproj(x, norm_w, scale, shift, w_t, tm=256):
    s = x.shape[0]

    def body(x_ref, nw_ref, sc_ref, sh_ref, w_hbm, *rest):
        outs, h_ref, w_vm, sem = rest[:NSEG], rest[NSEG], rest[NSEG + 1], rest[NSEG + 2]
        first = pl.program_id(0) == 0
        cps = [pltpu.make_async_copy(w_hbm.at[SEG_OFF[j]:SEG_OFF[j + 1], :], w_vm.at[SEG_OFF[j]:SEG_OFF[j + 1], :], sem.at[j])
               for j in range(NSEG)]

        def tile(waiting):
            xv = x_ref[...]
            r = lax.rsqrt(jnp.mean(xv * xv, axis=-1, keepdims=True) + EPS)
            h = xv * r * (nw_ref[...] * (1.0 + sc_ref[...])) + sh_ref[...]
            hb = _bf(h)
            h_ref[...] = hb
            for j in range(NSEG):
                if waiting:
                    cps[j].wait()
                outs[j][...] = _dot_nt(hb, w_vm[SEG_OFF[j]:SEG_OFF[j + 1], :]).astype(outs[j].dtype)

        @pl.when(first)
        def _():
            for cp in cps:
                cp.start()
            tile(True)

        @pl.when(jnp.logical_not(first))
        def _():
            tile(False)

    vec = _full((1, D_MODEL))
    return pl.pallas_call(
        body, name="inproj", grid=(s // tm,),
        in_specs=[_rows(tm, D_MODEL), vec, vec, vec, ANY],
        out_specs=[_rows(tm, w) for w in SEG_W] + [_rows(tm, D_MODEL)],
        out_shape=[jax.ShapeDtypeStruct((s, w), BF if j in GATE_SEGS else F32) for j, w in enumerate(SEG_W)]
                  + [jax.ShapeDtypeStruct((s, D_MODEL), BF)],
        scratch_shapes=[pltpu.VMEM((IN_W, D_MODEL), BF), pltpu.SemaphoreType.DMA((NSEG,))],
        compiler_params=_params(dimension_semantics=("arbitrary",)),
    )(x, norm_w, scale, shift, w_t)


def _bucket_onehot_t():
    qi = jnp.arange(BLOCK)[:, None]
    kj = jnp.arange(2 * BLOCK)[None, :]
    dist = qi + BLOCK - kj
    n = jnp.maximum(dist, 0)
    max_exact = REL_BUCKETS // 2
    nf = jnp.maximum(n, 1).astype(F32)
    large = max_exact + (jnp.log(nf / max_exact) / math.log(REL_MAX_DIST / max_exact)
                         * (REL_BUCKETS - max_exact)).astype(jnp.int32)
    large = jnp.minimum(large, REL_BUCKETS - 1)
    bucket = jnp.where(n < max_exact, n, large).reshape(1, BLOCK * 2 * BLOCK)
    return (bucket == jnp.arange(REL_BUCKETS)[:, None]).astype(F32)


def _bias_dense(rel_bias_t, oh_t):
    def body(rb_ref, oh_ref, o_ref):
        o_ref[...] = _dot(rb_ref[...], oh_ref[...], HI)

    return pl.pallas_call(
        body, name="bias_dense", out_shape=jax.ShapeDtypeStruct((ATTN_HEADS, BLOCK * 2 * BLOCK), F32),
        compiler_params=_params(),
    )(rel_bias_t, oh_t)


def _bias_grad(ds_sum, oh_t):
    def body(ds_ref, oh_ref, o_ref):
        o_ref[...] = _dot_nt(ds_ref[...], oh_ref[...], HI)

    return pl.pallas_call(
        body, name="bias_grad", out_shape=jax.ShapeDtypeStruct((ATTN_HEADS, REL_BUCKETS), F32),
        compiler_params=_params(),
    )(ds_sum, oh_t)


def _group_sum(a, e):
    hi = _bf(a)
    return _dot(hi, e) + _dot(_bf(a - hi.astype(F32)), e)


def _group_bcast(a, e3t):
    hi = _bf(a)
    r1 = a - hi.astype(F32)
    mid = _bf(r1)
    return _dot(jnp.concatenate([hi, mid, _bf(r1 - mid.astype(F32))], axis=1), e3t)


def _membership(width, group, ngroups):
    e = (jnp.arange(width)[:, None] // group == jnp.arange(ngroups)[None, :]).astype(BF)
    return e, jnp.tile(e.T, (3, 1))


def _fold(width, group):
    return (jnp.arange(width)[:, None] % group == jnp.arange(group)[None, :]).astype(BF)


def _heads_norm(t, w_x, e, e3t):
    r = lax.rsqrt(_dot(_bf(t * t), e) * (1.0 / HEAD_DIM) + EPS)
    r_x = _group_bcast(r, e3t)
    return t * r_x * w_x, r_x


def _heads_norm_bwd(t, r_x, w_x, d, e, e3t):
    wd = d * w_x
    corr = _group_bcast(_dot(_bf(t * wd), e) * (1.0 / HEAD_DIM), e3t)
    return r_x * wd - t * (r_x * r_x * r_x) * corr, jnp.sum(d * t * r_x, axis=0, keepdims=True)


def _stack_heads(a, hk):
    return jnp.concatenate([a[:, (hk * GRP + g) * HEAD_DIM:(hk * GRP + g + 1) * HEAD_DIM] for g in range(GRP)], axis=0)


def _stack_cols(a, hk):
    return jnp.concatenate([a[:, hk * GRP + g:hk * GRP + g + 1] for g in range(GRP)], axis=0)


def _masked_bias(bias):
    qi = jnp.arange(BLOCK)[:, None]
    kj = jnp.arange(2 * BLOCK)[None, :]
    cur_ok = jnp.logical_and(kj >= BLOCK, kj - BLOCK <= qi)
    both_ok = jnp.logical_or(jnp.logical_and(kj < BLOCK, kj > qi), cur_ok)
    return jnp.stack([jnp.where(cur_ok, bias, -1e30), jnp.where(both_ok, bias, -1e30)])


def _attn_consts(qnw, knw):
    eq, eq3t = _membership(ATTN_W, HEAD_DIM, ATTN_HEADS)
    ek, ek3t = _membership(KV_W, HEAD_DIM, ATTN_HEADS)
    return (jnp.tile(qnw, (1, ATTN_HEADS)), jnp.tile(knw, (1, KV_HEADS)), eq, eq3t, ek, ek3t)


def _attn_fwd(q, kv, bias, sinks, consts):
    s = q.shape[0]
    nb = s // BLOCK
    gq = GRP * BLOCK
    bias_t = bias.reshape(2, KV_HEADS, GRP, BLOCK, 2 * BLOCK).transpose(0, 1, 4, 2, 3).reshape(2, KV_HEADS, 2 * BLOCK, gq)
    sink_rows = jnp.repeat(sinks.reshape(KV_HEADS, GRP), BLOCK, axis=1).reshape(KV_HEADS, 1, gq)
    eye = jnp.eye(BLOCK, dtype=BF)

    def body(q_ref, kp_ref, kc_ref, vp_ref, vc_ref, b_ref, bt_ref, sk_ref, skr_ref, eye_ref,
             qw_ref, kw_ref, eq_ref, eq3_ref, ek_ref, ek3_ref, o_ref, lse_ref):
        qn = _bf(_heads_norm(q_ref[...], qw_ref[...], eq_ref[...], eq3_ref[...])[0] * (HEAD_DIM ** -0.5))
        kn = _bf(_heads_norm(jnp.concatenate([kp_ref[...], kc_ref[...]], axis=0), kw_ref[...], ek_ref[...], ek3_ref[...])[0])
        vv = _bf(jnp.concatenate([vp_ref[...], vc_ref[...]], axis=0))
        ones = jnp.ones((2 * BLOCK, HEAD_DIM), BF)
        kss = [slice(hk * HEAD_DIM, (hk + 1) * HEAD_DIM) for hk in range(KV_HEADS)]
        qgs = [_stack_heads(qn, hk) for hk in range(KV_HEADS)]
        sc_ts = [_dot_nt(kn[:, kss[hk]], qgs[hk]) + bt_ref[0, hk] for hk in range(KV_HEADS)]
        m_rows = [jnp.maximum(jnp.max(sc_ts[hk], axis=0, keepdims=True), skr_ref[hk]) for hk in range(KV_HEADS)]
        m_hq = _bf(jnp.concatenate([(m + jnp.abs(m) * (2.0 ** -7))[:, g * BLOCK:(g + 1) * BLOCK]
                                    for m in m_rows for g in range(GRP)], axis=0))
        m16 = _dot_nt(eye_ref[...], m_hq)
        ms = [_stack_cols(m16, hk) for hk in range(KV_HEADS)]
        scs = [_dot_nt(qgs[hk], kn[:, kss[hk]]) + b_ref[0, hk * GRP:(hk + 1) * GRP].reshape(gq, 2 * BLOCK)
               for hk in range(KV_HEADS)]
        ps = [_bf(jnp.exp(scs[hk] - ms[hk])) for hk in range(KV_HEADS)]
        pvs = [_dot(ps[hk], jnp.concatenate([vv[:, kss[hk]], ones], axis=1)) for hk in range(KV_HEADS)]
        den16 = jnp.concatenate([pvs[hk][g * BLOCK:(g + 1) * BLOCK, HEAD_DIM:HEAD_DIM + 1]
                                 for hk in range(KV_HEADS) for g in range(GRP)], axis=1)
        den16 = den16 + jnp.exp(sk_ref[...] - m16)
        lse_ref[...] = m16 + jnp.log(den16)
        inv16 = 1.0 / den16
        for hk in range(KV_HEADS):
            for g in range(GRP):
                h = hk * GRP + g
                o_ref[:, h * HEAD_DIM:(h + 1) * HEAD_DIM] = (pvs[hk][g * BLOCK:(g + 1) * BLOCK, :HEAD_DIM]
                                                             * inv16[:, h:h + 1])

    cur = lambda w, col=0: pl.BlockSpec((BLOCK, w), lambda i: (i, col))
    prev = lambda w, col=0: pl.BlockSpec((BLOCK, w), lambda i: (jnp.maximum(i - 1, 0), col))
    whole = lambda a: pl.BlockSpec(a.shape, lambda i: (0,) * a.ndim)
    first_or_not = lambda a: pl.BlockSpec((1,) + a.shape[1:], lambda i: (jnp.minimum(i, 1),) + (0,) * (a.ndim - 1))
    return pl.pallas_call(
        body, name="attn_fwd", grid=(nb,),
        in_specs=[cur(ATTN_W), prev(KV_W, 0), cur(KV_W, 0), prev(KV_W, 1), cur(KV_W, 1),
                  first_or_not(bias), first_or_not(bias_t),
                  whole(sinks), whole(sink_rows), whole(eye)] + [_full(c.shape) for c in consts],
        out_specs=[cur(ATTN_W), cur(ATTN_HEADS)],
        out_shape=[jax.ShapeDtypeStruct((s, ATTN_W), F32), jax.ShapeDtypeStruct((s, ATTN_HEADS), F32)],
        compiler_params=_params(dimension_semantics=("arbitrary",)),
    )(q, kv, kv, kv, kv, bias, bias_t, sinks, sink_rows, eye, *consts)


def _conv_taps(xbc, tail):
    ext = jnp.concatenate([tail, xbc], axis=0)
    return [pltpu.roll(ext, CONV_K - 1 - j, axis=0)[8:8 + BLOCK] if j < CONV_K - 1 else xbc for j in range(CONV_K)]


def _softplus(u):
    return jnp.maximum(u, 0.0) + jnp.log(1.0 + jnp.exp(-jnp.abs(u)))


def _tril():
    r = lax.broadcasted_iota(jnp.int32, (BLOCK, BLOCK), 0)
    c = lax.broadcasted_iota(jnp.int32, (BLOCK, BLOCK), 1)
    return r >= c


def _triu():
    r = lax.broadcasted_iota(jnp.int32, (BLOCK, BLOCK), 0)
    c = lax.broadcasted_iota(jnp.int32, (BLOCK, BLOCK), 1)
    return r <= c


def _exact_left(m01, a):
    hi = _bf(a)
    r1 = a - hi.astype(F32)
    mid = _bf(r1)
    return _dot(m01, hi) + _dot(m01, mid) + _dot(m01, _bf(r1 - mid.astype(F32)))


def _ssd_common(conv, dtr, dtb_ref, alog_ref, e3_ref):
    sg = _sig(conv)
    xact = conv * sg
    u = dtr + dtb_ref[...]
    dt = _softplus(u)
    a = -jnp.exp(alog_ref[...])
    trilb = _tril()
    acum = _exact_left(trilb.astype(BF), dt * a) * math.log2(math.e)
    both = _group_bcast(jnp.concatenate([dt, acum], axis=0), e3_ref[...])
    dt_x, acum_x = both[:BLOCK], both[BLOCK:]
    return sg, xact, u, dt, a, trilb, acum, dt_x, acum_x


SSD_CH = 2


def _ssd_fwd(xbc, dt_raw, conv_w, conv_b, dt_bias, a_log, dsk_x, e3t):
    s = xbc.shape[0]
    nc = s // BLOCK
    ch = SSD_CH if nc % SSD_CH == 0 else 1
    rows = ch * BLOCK

    def body(x_ref, tail_ref, dtr_ref, cw_ref, cb_ref, dtb_ref, alog_ref, dsk_ref, e3_ref,
             y_ref, hp_ref, conv_ref, hst, yd_s, yoff_s):
        i = pl.program_id(0)

        @pl.when(i == 0)
        def _():
            hst[...] = jnp.zeros_like(hst)

        for j in range(ch):
            rs = slice(j * BLOCK, (j + 1) * BLOCK)
            tail = jnp.where(i > 0, tail_ref[...], 0.0) if j == 0 else x_ref[j * BLOCK - 8:j * BLOCK, :]
            taps = _conv_taps(x_ref[rs, :], tail)
            conv = cb_ref[...] + sum(taps[t] * cw_ref[t:t + 1, :] for t in range(CONV_K))
            conv_ref[rs, :] = conv
            _, xact, _, _, _, trilb, acum, dt_x, acum_x = _ssd_common(conv, dtr_ref[rs, :], dtb_ref, alog_ref, e3_ref)
            xs = xact[:, :SSM_W]
            acum_t = acum.T
            ea_x = jnp.exp2(acum_x)
            last_x = acum_x[BLOCK - 1:BLOCK, :]
            xdt = xs * dt_x
            xw = xdt * jnp.exp2(last_x - acum_x)
            cd_x = jnp.exp2(last_x)
            hprev = hst[...]
            hp_ref[j] = hprev
            sls = [slice(g * SSM_R * SSM_P, (g + 1) * SSM_R * SSM_P) for g in range(SSM_G)]
            bgs = [_bf(xact[:, SSM_W + g * SSM_N:SSM_W + (g + 1) * SSM_N]) for g in range(SSM_G)]
            cgs = [_bf(xact[:, SSM_W + SSM_G * SSM_N + g * SSM_N:SSM_W + SSM_G * SSM_N + (g + 1) * SSM_N])
                   for g in range(SSM_G)]
            xdt_b, xw_b, hprev_b = _bf(xdt), _bf(xw), _bf(hprev)
            low_half = lax.broadcasted_iota(jnp.int32, (BLOCK, 2 * SSM_P), 1) < SSM_P
            cbs = [_dot_nt(cgs[g], bgs[g]) for g in range(SSM_G)]
            for g in range(SSM_G):
                sl = sls[g]
                yoff_s[:, sl] = _dot(cgs[g], hprev_b[:, sl]) * ea_x[:, sl]
                hst[:, sl] = hprev[:, sl] * cd_x[:, sl] + _dot_tn(bgs[g], xw_b[:, sl])
            for g in range(SSM_G):
                hss = [slice((g * SSM_R + r) * SSM_P, (g * SSM_R + r + 1) * SSM_P) for r in range(SSM_R)]
                mms = [_bf(cbs[g] * jnp.exp2(jnp.where(trilb, acum[:, g * SSM_R + r:g * SSM_R + r + 1]
                                                      - acum_t[g * SSM_R + r:g * SSM_R + r + 1, :], -1e30)))
                       for r in range(SSM_R)]
                for r in range(0, SSM_R, 2):
                    pair = slice(hss[r].start, hss[r + 1].stop)
                    xp = xdt_b[:, pair]
                    rhs = jnp.concatenate([jnp.where(low_half, xp, 0), jnp.where(low_half, 0, xp)], axis=0)
                    yd_s[:, pair] = _dot(jnp.concatenate([mms[r], mms[r + 1]], axis=1), rhs)
            y_ref[rs, :] = yd_s[...] + yoff_s[...] + dsk_ref[...] * xs

    blk = lambda w: pl.BlockSpec((rows, w), lambda i: (i, 0))
    return pl.pallas_call(
        body, name="ssd_fwd", grid=(nc // ch,),
        in_specs=[blk(XBC_W), pl.BlockSpec((8, XBC_W), lambda i: (jnp.maximum(i * (rows // 8) - 1, 0), 0)),
                  blk(SSM_HEADS), _full((CONV_K, XBC_W)), _full((1, XBC_W)), _full((1, SSM_HEADS)),
                  _full((1, SSM_HEADS)), _full((1, SSM_W)), _full((3 * SSM_HEADS, SSM_W))],
        out_specs=[blk(SSM_W), pl.BlockSpec((ch, SSM_N, SSM_W), lambda i: (i, 0, 0)), blk(XBC_W)],
        out_shape=[jax.ShapeDtypeStruct((s, SSM_W), F32), jax.ShapeDtypeStruct((nc, SSM_N, SSM_W), F32),
                   jax.ShapeDtypeStruct((s, XBC_W), F32)],
        scratch_shapes=[pltpu.VMEM((SSM_N, SSM_W), F32), pltpu.VMEM((BLOCK, SSM_W), F32), pltpu.VMEM((BLOCK, SSM_W), F32)],
        compiler_params=_params(dimension_semantics=("arbitrary",)),
    )(xbc, xbc, dt_raw, conv_w, conv_b, dt_bias, a_log, dsk_x, e3t)


def _dsilu(z, sg, silu):
    return sg * (1.0 + (z - silu))


def _mid(x, tgt, o_att, zam, ypre, gab, gate, ssm_nw, rows_all, tm=256):
    s = x.shape[0]
    gw = SSM_W // SSM_G

    r_ap, r_sp = ATTN_W // N_DEV, SSM_W // N_DEV

    def body(x_ref, t_ref, o_ref, zam_ref, yp_ref, gab_ref, gate_ref, nw_ref, rows_h,
             dout_ref, do_ref, dzam_ref, dyp_ref, dgab_ref,
             yag_ref, dya_ref, yn_ref, dyb_ref, mg_ref, dob_ref, gnw_ref, dgate_ref, loss_ref,
             wap_v, wsp_v, wout_v, sem):
        i = pl.program_id(0)

        @pl.when(i == 0)
        def _():
            cps = []
            for d in range(N_DEV):
                for j, (dst, r0, rn) in enumerate(((wap_v, 0, r_ap), (wsp_v, r_ap, r_sp), (wout_v, r_ap + r_sp, r_ap))):
                    cps.append(pltpu.make_async_copy(rows_h.at[d, r0:r0 + rn, :], dst.at[d * rn:(d + 1) * rn, :], sem.at[j]))
            for cp in cps:
                cp.start()
            gnw_ref[...] = jnp.zeros_like(gnw_ref)
            dgate_ref[...] = jnp.zeros_like(dgate_ref)
            loss_ref[...] = jnp.zeros_like(loss_ref)
            for cp in cps:
                cp.wait()

        gate = gate_ref[...]
        nw = nw_ref[...]
        o_att = o_ref[...]
        z_a = zam_ref[:, :ATTN_W].astype(F32)
        s_a = _sig(z_a)
        silu_a = z_a * s_a
        yag = _bf(o_att * silu_a)
        yag_ref[...] = yag
        ypre = yp_ref[...]
        z_m = zam_ref[:, ATTN_W:].astype(F32)
        s_m = _sig(z_m)
        silu_m = z_m * s_m
        yg = ypre * silu_m
        rinv = jnp.concatenate(
            [jnp.broadcast_to(lax.rsqrt(jnp.mean(yg[:, g * gw:(g + 1) * gw] ** 2, axis=-1, keepdims=True) + EPS), (tm, gw))
             for g in range(SSM_G)], axis=1)
        ynr = yg * rinv
        yn = _bf(ynr * nw)
        yn_ref[...] = yn
        y_a = _dot(yag, wap_v[...])
        y_b = _dot(yn, wsp_v[...])
        g_a = _sig(gab_ref[:, :D_MODEL].astype(F32))
        g_b = _sig(gab_ref[:, D_MODEL:].astype(F32))
        merged = _bf(g_a * y_a + g_b * y_b)
        mg_ref[...] = merged
        o = _dot(merged, wout_v[...])
        diff = x_ref[...] + gate * o - t_ref[...]
        loss_ref[...] += (0.5 / D_MODEL) * jnp.sum(diff * diff, axis=(0, 1), keepdims=True)
        dout = diff * (1.0 / D_MODEL)
        dout_ref[...] = dout
        dgate_ref[...] += jnp.sum(dout * o, axis=0, keepdims=True)
        d_o = _bf(dout * gate)
        dob_ref[...] = d_o
        dmerged = _dot_nt(d_o, wout_v[...])
        dy_af = dmerged * g_a
        dy_bf = dmerged * g_b
        dy_a = _bf(dy_af)
        dy_b = _bf(dy_bf)
        dya_ref[...] = dy_a
        dyb_ref[...] = dy_b
        dyag = _dot_nt(dy_a, wap_v[...])
        dyn = _dot_nt(dy_b, wsp_v[...])
        dgab_ref[:, :D_MODEL] = _bf(dy_af * y_a * (1.0 - g_a))
        dgab_ref[:, D_MODEL:] = _bf(dy_bf * y_b * (1.0 - g_b))
        do_ref[...] = dyag * silu_a
        dzam_ref[:, :ATTN_W] = _bf(dyag * o_att * _dsilu(z_a, s_a, silu_a))
        gnw_ref[...] += jnp.sum(dyn * ynr, axis=0, keepdims=True)
        dynw = dyn * nw
        corr = jnp.concatenate(
            [jnp.broadcast_to(jnp.mean((dynw * ynr)[:, g * gw:(g + 1) * gw], axis=-1, keepdims=True), (tm, gw))
             for g in range(SSM_G)], axis=1)
        dyg = rinv * (dynw - ynr * corr)
        dyp_ref[...] = dyg * silu_m
        dzam_ref[:, ATTN_W:] = _bf(dyg * ypre * _dsilu(z_m, s_m, silu_m))

    r1, r2, r3 = _rows(tm, D_MODEL), _rows(tm, SSM_W), _rows(tm, ATTN_W + SSM_W)
    sd = jax.ShapeDtypeStruct
    return pl.pallas_call(
        body, name="mid", grid=(s // tm,),
        in_specs=[r1, r1, r1, r3, r2, r2, _full((1, D_MODEL)), _full((1, SSM_W)), ANY],
        out_specs=[r1, r1, r3, r2, r2, r1, r1, r2, r1, r1, r1,
                   _full((1, SSM_W)), _full((1, D_MODEL)), _full((1, 1))],
        out_shape=[sd((s, D_MODEL), F32), sd((s, ATTN_W), F32), sd((s, ATTN_W + SSM_W), BF), sd((s, SSM_W), F32),
                   sd((s, 2 * D_MODEL), BF),
                   sd((s, ATTN_W), BF), sd((s, D_MODEL), BF), sd((s, SSM_W), BF), sd((s, D_MODEL), BF),
                   sd((s, D_MODEL), BF), sd((s, D_MODEL), BF),
                   sd((1, SSM_W), F32), sd((1, D_MODEL), F32), sd((1, 1), F32)],
        scratch_shapes=[pltpu.VMEM((ATTN_W, D_MODEL), BF), pltpu.VMEM((SSM_W, D_MODEL), BF), pltpu.VMEM((D_MODEL, D_MODEL), BF),
                        pltpu.SemaphoreType.DMA((3,))],
        compiler_params=_params(dimension_semantics=("arbitrary",)),
    )(x, tgt, o_att, zam, ypre, gab, gate, ssm_nw, rows_all)


def _attn_bwd(q, kv, bias, sinks, consts, o_att, lse, d_o):
    s = q.shape[0]
    nb = s // BLOCK
    folds = (_fold(ATTN_W, HEAD_DIM), _fold(KV_W, HEAD_DIM))

    def body(q_ref, kp_ref, kc_ref, vp_ref, vc_ref, b_ref, skv_ref, qw_ref, kw_ref, eq_ref, eq3_ref, ek_ref, ek3_ref,
             fq_ref, fk_ref, o_ref, lse_ref, do_ref,
             dq_ref, dkv_ref, dss_ref, gqw_ref, gkw_ref, gsk_ref, ckn, cv, dqn_s, dkn_s, dv_s, gq_x, gk_x):
        i = pl.program_id(0)
        kw, ek, ek3 = kw_ref[...], ek_ref[...], ek3_ref[...]

        @pl.when(i == 0)
        def _():
            for ref in (ckn, cv, dss_ref, gq_x, gk_x, gsk_ref):
                ref[...] = jnp.zeros_like(ref)

        @pl.when(i < nb)
        def _():
            qw, eq, eq3 = qw_ref[...], eq_ref[...], eq3_ref[...]
            qf = q_ref[...]
            qnf, rq_x = _heads_norm(qf, qw, eq, eq3)
            qn = _bf(qnf * (HEAD_DIM ** -0.5))
            kf = jnp.concatenate([kp_ref[...], kc_ref[...]], axis=0)
            knf, rk_x = _heads_norm(kf, kw, ek, ek3)
            kn = _bf(knf)
            vv = _bf(jnp.concatenate([vp_ref[...], vc_ref[...]], axis=0))
            d_of = do_ref[...]
            d_ob = _bf(d_of)
            lse_all = lse_ref[...]
            delta = _dot(_bf(d_of * o_ref[...]), eq)
            gsk_ref[...] += jnp.sum(-jnp.exp(skv_ref[...] - lse_all) * delta, axis=0, keepdims=True)
            kss = [slice(hk * HEAD_DIM, (hk + 1) * HEAD_DIM) for hk in range(KV_HEADS)]
            qgs = [_stack_heads(qn, hk) for hk in range(KV_HEADS)]
            d_ogs = [_stack_heads(d_ob, hk) for hk in range(KV_HEADS)]
            scs = [_dot_nt(qgs[hk], kn[:, kss[hk]]) + b_ref[0, hk * GRP:(hk + 1) * GRP].reshape(GRP * BLOCK, 2 * BLOCK)
                   for hk in range(KV_HEADS)]
            dps = [_dot_nt(d_ogs[hk], vv[:, kss[hk]]) for hk in range(KV_HEADS)]
            ps = [jnp.exp(scs[hk] - _stack_cols(lse_all, hk)) for hk in range(KV_HEADS)]
            dss = [ps[hk] * (dps[hk] - _stack_cols(delta, hk)) for hk in range(KV_HEADS)]
            pbs = [_bf(p) for p in ps]
            dsbs = [_bf(ds) for ds in dss]
            for hk in range(KV_HEADS):
                dss_ref[hk * GRP:(hk + 1) * GRP] += dss[hk].reshape(GRP, BLOCK, 2 * BLOCK)
            for hk in range(KV_HEADS):
                dv_s[:, kss[hk]] = _dot_tn(pbs[hk], d_ogs[hk])
                dkn_s[:, kss[hk]] = _dot_tn(dsbs[hk], qgs[hk])
            dqns = [_dot(dsbs[hk], kn[:, kss[hk]]) * (HEAD_DIM ** -0.5) for hk in range(KV_HEADS)]
            for hk in range(KV_HEADS):
                for g in range(GRP):
                    h = hk * GRP + g
                    dqn_s[:, h * HEAD_DIM:(h + 1) * HEAD_DIM] = dqns[hk][g * BLOCK:(g + 1) * BLOCK]
            dq, gq = _heads_norm_bwd(qf, rq_x, qw, dqn_s[...], eq, eq3)
            dq_ref[...] = _bf(dq)
            gq_x[...] += gq
            dk, gk = _heads_norm_bwd(kf[:BLOCK], rk_x[:BLOCK], kw, ckn[...] + dkn_s[0:BLOCK, :], ek, ek3)
            dkv_ref[:, :KV_W] = _bf(dk)
            gk_x[...] += gk
            dkv_ref[:, KV_W:] = _bf(cv[...] + dv_s[0:BLOCK, :])
            ckn[...] = dkn_s[BLOCK:2 * BLOCK, :]
            cv[...] = dv_s[BLOCK:2 * BLOCK, :]

        @pl.when(i == nb)
        def _():
            kc = kc_ref[...]
            dk, gk = _heads_norm_bwd(kc, _heads_norm(kc, kw, ek, ek3)[1], kw, ckn[...], ek, ek3)
            dkv_ref[:, :KV_W] = _bf(dk)
            dkv_ref[:, KV_W:] = _bf(cv[...])
            gqw_ref[...] = _group_sum(jnp.broadcast_to(gq_x[...], (8, ATTN_W)), fq_ref[...])[0:1]
            gkw_ref[...] = _group_sum(jnp.broadcast_to(gk_x[...] + gk, (8, KV_W)), fk_ref[...])[0:1]

    last = nb - 1
    cur = lambda w, col=0: pl.BlockSpec((BLOCK, w), lambda i: (jnp.minimum(i, last), col))
    prev = lambda w, col=0: pl.BlockSpec((BLOCK, w), lambda i: (jnp.maximum(jnp.minimum(i, last) - 1, 0), col))
    late = lambda w: pl.BlockSpec((BLOCK, w), lambda i: (jnp.maximum(i - 1, 0), 0))
    sd = jax.ShapeDtypeStruct
    return pl.pallas_call(
        body, name="attn_bwd", grid=(nb + 1,),
        in_specs=[cur(ATTN_W), prev(KV_W, 0), cur(KV_W, 0), prev(KV_W, 1), cur(KV_W, 1),
                  pl.BlockSpec((1, ATTN_HEADS, BLOCK, 2 * BLOCK), lambda i: (jnp.minimum(i, 1), 0, 0, 0)),
                  _full((1, ATTN_HEADS))]
                 + [_full(c.shape) for c in consts + folds] + [cur(ATTN_W), cur(ATTN_HEADS), cur(ATTN_W)],
        out_specs=[cur(ATTN_W), late(2 * KV_W),
                   pl.BlockSpec((ATTN_HEADS, BLOCK, 2 * BLOCK), lambda i: (0, 0, 0)),
                   _full((1, HEAD_DIM)), _full((1, HEAD_DIM)), _full((1, ATTN_HEADS))],
        out_shape=[sd((s, ATTN_W), BF), sd((s, 2 * KV_W), BF),
                   sd((ATTN_HEADS, BLOCK, 2 * BLOCK), F32), sd((1, HEAD_DIM), F32), sd((1, HEAD_DIM), F32),
                   sd((1, ATTN_HEADS), F32)],
        scratch_shapes=[pltpu.VMEM((BLOCK, KV_W), F32), pltpu.VMEM((BLOCK, KV_W), F32),
                        pltpu.VMEM((BLOCK, ATTN_W), F32), pltpu.VMEM((2 * BLOCK, KV_W), F32),
                        pltpu.VMEM((2 * BLOCK, KV_W), F32), pltpu.VMEM((1, ATTN_W), F32), pltpu.VMEM((1, KV_W), F32)],
        compiler_params=_params(dimension_semantics=("arbitrary",)),
    )(q, kv, kv, kv, kv, bias, sinks, *consts, *folds, o_att, lse, d_o)


def _ssd_bwd(xbc, conv_all, dt_raw, conv_w, dt_bias, a_log, dsk_x, e_mat, e3t, hprev_all, dy_all):
    s = xbc.shape[0]
    nc = s // BLOCK
    ch = 1
    rows = ch * BLOCK
    nsteps = nc // ch
    gw = SSM_R * SSM_P
    b0, c0 = SSM_W, SSM_W + SSM_G * SSM_N

    def body(x_ref, conv_ref, dtr_ref, cw_ref, dtb_ref, alog_ref, dsk_ref, e_ref, e3_ref, hp_ref, dy_ref,
             dx_ref, ddt_ref, gcw_ref, gcb_ref, gdtb_ref, galog_ref, gdsk_ref,
             dh, nhead, gdskx, dxdt_s, dbc_s, dxd_s):
        def chunk_bwd(j):
            rs = slice(j * BLOCK, (j + 1) * BLOCK)
            conv = conv_ref[rs, :]
            sg, xact, u, dt, a, trilb, acum, dt_x, acum_x = _ssd_common(conv, dtr_ref[rs, :], dtb_ref, alog_ref, e3_ref)
            xs = xact[:, :SSM_W]
            acum_t = acum.T
            ea_x = jnp.exp2(acum_x)
            last_x = acum_x[BLOCK - 1:BLOCK, :]
            dte_x = jnp.exp2(last_x - acum_x)
            cd_x = jnp.exp2(last_x)
            xdt = xs * dt_x
            xw = xdt * dte_x
            hprev = hp_ref[j]
            dhn = dh[...]
            dy = dy_ref[rs, :]
            gdskx[...] += jnp.sum(dy * xs, axis=0, keepdims=True)
            dyea = dy * ea_x
            lane = lax.broadcasted_iota(jnp.int32, (BLOCK, SSM_HEADS), 1)
            dacum = jnp.zeros((BLOCK, SSM_HEADS), F32)
            dacc_x, dlast_x = [], []
            sls = [slice(g * gw, (g + 1) * gw) for g in range(SSM_G)]
            bgs = [_bf(xact[:, b0 + g * SSM_N:b0 + (g + 1) * SSM_N]) for g in range(SSM_G)]
            cgs = [_bf(xact[:, c0 + g * SSM_N:c0 + (g + 1) * SSM_N]) for g in range(SSM_G)]
            hpgs = [_bf(hprev[:, sl]) for sl in sls]
            dhgs = [_bf(dhn[:, sl]) for sl in sls]
            dyeags = [_bf(dyea[:, sl]) for sl in sls]
            xwgs = [_bf(xw[:, sl]) for sl in sls]
            xdt_b, dy_b = _bf(xdt), _bf(dy)
            low_half = lax.broadcasted_iota(jnp.int32, (BLOCK, 2 * SSM_P), 1) < SSM_P
            cbs = [_dot_nt(cgs[g], bgs[g]) for g in range(SSM_G)]
            gmats = [_dot(cgs[g], hpgs[g]) for g in range(SSM_G)]
            dxws = [_dot(bgs[g], dhgs[g]) for g in range(SSM_G)]
            dcgs = [_dot_nt(dyeags[g], hpgs[g]) for g in range(SSM_G)]
            dbgs = [_dot_nt(xwgs[g], dhgs[g]) for g in range(SSM_G)]
            for g in range(SSM_G):
                sl = sls[g]
                dh[:, sl] = dhn[:, sl] * cd_x[:, sl] + _dot_tn(cgs[g], dyeags[g])
                dxdt_s[:, sl] = dxws[g] * dte_x[:, sl]
                dacc_x.append(dy[:, sl] * gmats[g] * ea_x[:, sl] - dxws[g] * xw[:, sl])
                dlast_x.append(jnp.sum(dxws[g] * xw[:, sl], axis=0, keepdims=True)
                               + jnp.sum(dhn[:, sl] * hprev[:, sl], axis=0, keepdims=True) * cd_x[:, sl])
            for g in range(SSM_G):
                bg, cg, cb, dbg, dcg = bgs[g], cgs[g], cbs[g], dbgs[g], dcgs[g]
                hss = [slice((g * SSM_R + r) * SSM_P, (g * SSM_R + r + 1) * SSM_P) for r in range(SSM_R)]
                lms = [jnp.exp2(jnp.where(trilb, acum[:, g * SSM_R + r:g * SSM_R + r + 1]
                                         - acum_t[g * SSM_R + r:g * SSM_R + r + 1, :], -1e30)) for r in range(SSM_R)]
                mms = [cb * lm for lm in lms]
                mmbs = [_bf(mm) for mm in mms]
                dms = []
                for r in range(0, SSM_R, 2):
                    pair = slice(hss[r].start, hss[r + 1].stop)
                    xp, dyp = xdt_b[:, pair], dy_b[:, pair]
                    dmp = _dot_nt(dyp, jnp.concatenate([jnp.where(low_half, xp, 0), jnp.where(low_half, 0, xp)], axis=0))
                    dms += [dmp[:, :BLOCK], dmp[:, BLOCK:]]
                    dxd_s[:, pair] = _dot_tn(jnp.concatenate([mmbs[r], mmbs[r + 1]], axis=0),
                                             jnp.concatenate([jnp.where(low_half, dyp, 0), jnp.where(low_half, 0, dyp)], axis=0))
                dcb = sum(dms[r] * lms[r] for r in range(SSM_R))
                wms = [dms[r] * mms[r] for r in range(SSM_R)]
                antis = [_bf(wm - wm.T) for wm in wms]
                for r in range(SSM_R):
                    dacum = dacum + _dot(antis[r], (lane == g * SSM_R + r).astype(BF))
                dcbb = _bf(dcb)
                dbc_s[:, g * SSM_N:(g + 1) * SSM_N] = dbg + _dot_tn(dcbb, cg)
                dbc_s[:, SSM_G * SSM_N + g * SSM_N:SSM_G * SSM_N + (g + 1) * SSM_N] = dcg + _dot(dcbb, bg)
            dxdt = dxdt_s[...] + dxd_s[...]
            dxs = dy * dsk_ref[...] + dxdt * dt_x
            red = _group_sum(jnp.concatenate(
                [dxdt * xs, jnp.concatenate(dacc_x, axis=1),
                 jnp.broadcast_to(jnp.concatenate(dlast_x, axis=1), (8, SSM_W))], axis=0), e_ref[...])
            row = lax.broadcasted_iota(jnp.int32, (BLOCK, SSM_HEADS), 0)
            dacum = dacum + red[BLOCK:2 * BLOCK] + jnp.where(row == BLOCK - 1, red[2 * BLOCK:2 * BLOCK + 1], 0.0)
            ddta = _exact_left(_triu().astype(BF), dacum)
            ddt = red[:BLOCK] + ddta * a
            galog_ref[...] += jnp.sum(ddta * dt, axis=0, keepdims=True) * a
            du = ddt * _sig(u)
            ddt_ref[rs, :] = _bf(du)
            gdtb_ref[...] += jnp.sum(du, axis=0, keepdims=True)
            dconv = jnp.concatenate([dxs, dbc_s[...]], axis=1) * _dsilu(conv, sg, xact)
            gcb_ref[...] += jnp.sum(dconv, axis=0, keepdims=True)
            ext2 = jnp.concatenate([dconv, nhead[...]], axis=0)
            ahead = [pltpu.roll(ext2, BLOCK + 8 - (CONV_K - 1 - j), axis=0)[0:BLOCK] if j < CONV_K - 1 else dconv
                     for j in range(CONV_K)]
            dx_ref[rs, :] = _bf(sum(ahead[j] * cw_ref[j:j + 1, :] for j in range(CONV_K)))
            xraw = x_ref[rs, :]
            gcw_ref[...] += jnp.concatenate([jnp.sum(ahead[j] * xraw, axis=0, keepdims=True) for j in range(CONV_K)], axis=0)
            nhead[...] = dconv[0:8]

        i = pl.program_id(0)

        @pl.when(i == 0)
        def _():
            for ref in (dh, nhead, gdskx, gcw_ref, gcb_ref, gdtb_ref, galog_ref, gdsk_ref):
                ref[...] = jnp.zeros_like(ref)

        for j in reversed(range(ch)):
            chunk_bwd(j)

        @pl.when(i == nsteps - 1)
        def _():
            gdsk_ref[...] = _group_sum(jnp.broadcast_to(gdskx[...], (8, SSM_W)), e_ref[...])[0:1]

    chunk = lambda w: pl.BlockSpec((rows, w), lambda i: (nsteps - 1 - i, 0))
    sd = jax.ShapeDtypeStruct
    return pl.pallas_call(
        body, name="ssd_bwd", grid=(nsteps,),
        in_specs=[chunk(XBC_W), chunk(XBC_W),
                  chunk(SSM_HEADS), _full((CONV_K, XBC_W)), _full((1, SSM_HEADS)),
                  _full((1, SSM_HEADS)), _full((1, SSM_W)), _full((SSM_W, SSM_HEADS)), _full((3 * SSM_HEADS, SSM_W)),
                  pl.BlockSpec((ch, SSM_N, SSM_W), lambda i: (nsteps - 1 - i, 0, 0)), chunk(SSM_W)],
        out_specs=[chunk(XBC_W), chunk(SSM_HEADS), _full((CONV_K, XBC_W)), _full((1, XBC_W)),
                   _full((1, SSM_HEADS)), _full((1, SSM_HEADS)), _full((1, SSM_HEADS))],
        out_shape=[sd((s, XBC_W), BF), sd((s, SSM_HEADS), BF), sd((CONV_K, XBC_W), F32), sd((1, XBC_W), F32),
                   sd((1, SSM_HEADS), F32), sd((1, SSM_HEADS), F32), sd((1, SSM_HEADS), F32)],
        scratch_shapes=[pltpu.VMEM((SSM_N, SSM_W), F32), pltpu.VMEM((8, XBC_W), F32),
                        pltpu.VMEM((1, SSM_W), F32), pltpu.VMEM((BLOCK, SSM_W), F32),
                        pltpu.VMEM((BLOCK, 2 * SSM_G * SSM_N), F32), pltpu.VMEM((BLOCK, SSM_W), F32)],
        compiler_params=_params(dimension_semantics=("arbitrary",)),
    )(xbc, conv_all, dt_raw, conv_w, dt_bias, a_log, dsk_x, e_mat, e3t, hprev_all, dy_all)


def _dh(x, dout, norm_w, scale, dsegs, w_t, tm=256):
    s = x.shape[0]

    def body(x_ref, dout_ref, nw_ref, sc_ref, *rest):
        d_refs, w_hbm = rest[:NSEG], rest[NSEG]
        gx_ref, dshift_ref, dscale_ref, gnw_ref = rest[NSEG + 1:NSEG + 5]
        w_vm, sem = rest[NSEG + 5], rest[NSEG + 6]
        first = pl.program_id(0) == 0
        cps = [pltpu.make_async_copy(w_hbm.at[SEG_OFF[j]:SEG_OFF[j + 1], :], w_vm.at[SEG_OFF[j]:SEG_OFF[j + 1], :], sem.at[j])
               for j in range(NSEG)]

        def tile(waiting):
            dh = None
            for j in range(NSEG):
                if waiting:
                    cps[j].wait()
                part = _dot(d_refs[j][...], w_vm[SEG_OFF[j]:SEG_OFF[j + 1], :])
                dh = part if dh is None else dh + part
            xv = x_ref[...]
            r = lax.rsqrt(jnp.mean(xv * xv, axis=-1, keepdims=True) + EPS)
            xn = xv * r
            nw = nw_ref[...]
            sc1 = 1.0 + sc_ref[...]
            dshift_ref[...] += jnp.sum(dh, axis=0, keepdims=True)
            dhxn = jnp.sum(dh * xn, axis=0, keepdims=True)
            dscale_ref[...] += dhxn * nw
            gnw_ref[...] += dhxn * sc1
            dxn = dh * (nw * sc1)
            gx_ref[...] = dout_ref[...] + r * (dxn - xn * jnp.mean(xn * dxn, axis=-1, keepdims=True))

        @pl.when(first)
        def _():
            for cp in cps:
                cp.start()
            for ref in (dshift_ref, dscale_ref, gnw_ref):
                ref[...] = jnp.zeros_like(ref)
            tile(True)

        @pl.when(jnp.logical_not(first))
        def _():
            tile(False)

    vec = _full((1, D_MODEL))
    sd = jax.ShapeDtypeStruct
    return pl.pallas_call(
        body, name="dh", grid=(s // tm,),
        in_specs=[_rows(tm, D_MODEL), _rows(tm, D_MODEL), vec, vec] + [_rows(tm, w) for w in SEG_W] + [ANY],
        out_specs=[_rows(tm, D_MODEL), vec, vec, vec],
        out_shape=[sd((s, D_MODEL), F32), sd((1, D_MODEL), F32), sd((1, D_MODEL), F32), sd((1, D_MODEL), F32)],
        scratch_shapes=[pltpu.VMEM((IN_W, D_MODEL), BF), pltpu.SemaphoreType.DMA((NSEG,))],
        compiler_params=_params(dimension_semantics=("arbitrary",)),
    )(x, dout, norm_w, scale, *dsegs, w_t)


def _gw_seg(h, dseg, name, tm=2048):
    s, w = dseg.shape
    tn = w if w <= 2048 else w // 2
    tm = min(tm, s)
    nm = s // tm

    def body(h_ref, d_ref, o_ref, acc):
        m = pl.program_id(1)

        @pl.when(m == 0)
        def _():
            acc[...] = jnp.zeros_like(acc)

        acc[...] += _dot_tn(d_ref[...], h_ref[...])

        @pl.when(m == nm - 1)
        def _():
            o_ref[...] = _bf(acc[...])

    return pl.pallas_call(
        body, name=name, grid=(w // tn, nm),
        in_specs=[pl.BlockSpec((tm, D_MODEL), lambda n, m: (m, 0)), pl.BlockSpec((tm, tn), lambda n, m: (m, n))],
        out_specs=pl.BlockSpec((tn, D_MODEL), lambda n, m: (n, 0)),
        out_shape=jax.ShapeDtypeStruct((w, D_MODEL), BF),
        scratch_shapes=[pltpu.VMEM((tn, D_MODEL), F32)],
        compiler_params=_params(dimension_semantics=("arbitrary", "arbitrary")),
    )(h, dseg)


def _gw_in(h, dsegs):
    return [_gw_seg(h, d, "gw_in_%d" % j) for j, d in enumerate(dsegs)]


def _local_step(x, tgt, shift, scale, gate, w_t, rows_fn, norm_w, qnw, knw, rel_bias, sinks,
                conv_w, conv_b, dt_bias, a_log, d_skip, ssm_nw, after_mid=None, after_gw=None):
    oh_t = _bucket_onehot_t()
    bias = _masked_bias(_bias_dense(rel_bias.T, oh_t).reshape(ATTN_HEADS, BLOCK, 2 * BLOCK))
    *segs, h = _inproj(x, norm_w, scale, shift, w_t)
    q, kv, zam, xbc, dtr, gab = segs
    consts = _attn_consts(qnw, knw)
    o_att, lse = _attn_fwd(q, kv, bias, sinks, consts)
    e_mat, e3t = _membership(SSM_W, SSM_P, SSM_HEADS)
    dsk_x = jnp.repeat(d_skip, SSM_P, axis=1)
    ypre, hprev, conv = _ssd_fwd(xbc, dtr, conv_w, conv_b, dt_bias, a_log, dsk_x, e3t)
    (dout, d_o, dzam, dyp, dgab, yag, dy_a, yn, dy_b, merged, dob, g_ssm_nw, dgate, loss) = _mid(
        x, tgt, o_att, zam, ypre, gab, gate, ssm_nw, rows_fn(ypre))
    g_wap = _gw_seg(dy_a, yag, "gw_attn_proj")
    g_wsp = _gw_seg(dy_b, yn, "gw_ssm_proj")
    g_wout = _gw_seg(dob, merged, "gw_out")
    zero = after_mid(g_wap, g_wsp, g_wout) if after_mid is not None else 0.0
    dq, dkv, dss, g_qnw, g_knw, g_sinks = _attn_bwd(q, kv, bias, sinks + zero, consts, o_att, lse, d_o)
    g_rel = _bias_grad(dss.reshape(ATTN_HEADS, BLOCK * 2 * BLOCK), oh_t).T
    dxbc, ddt, g_cw, g_cb, g_dtb, g_alog, g_dsk = _ssd_bwd(
        xbc, conv, dtr, conv_w, dt_bias, a_log, dsk_x, e_mat, e3t, hprev, dyp)
    dsegs = (dq, dkv, dzam, dxbc, ddt, dgab)
    g_ws = _gw_in(h, dsegs)
    zero = after_gw(g_ws) if after_gw is not None else 0.0
    gx, dshift, dscale, g_nw = _dh(x, dout, norm_w + zero, scale, dsegs, w_t)
    return dict(loss=loss, grad_x=gx, dmod=jnp.concatenate([dshift, dscale, dgate], axis=1), g_ws=g_ws,
                g_wap=g_wap, g_wsp=g_wsp, g_wout=g_wout, g_norm_w=g_nw, g_qnw=g_qnw, g_knw=g_knw, g_rel=g_rel,
                g_sinks=g_sinks, g_conv_w=g_cw, g_conv_b=g_cb, g_dt_bias=g_dtb, g_a_log=g_alog, g_d_skip=g_dsk,
                g_ssm_nw=g_ssm_nw)


def _me():
    return lax.axis_index("x"), lax.axis_index("y"), lax.axis_index("c")


def _flip(v, bit):
    return 1 - v if bit else v


def _ag_direct(v, name):
    def body(v_ref, out_ref, send_sems, recv_sems, local_sem):
        x, y, c = _me()
        me = 4 * x + 2 * y + c
        mine = pltpu.make_async_copy(v_ref, out_ref.at[me], local_sem)
        mine.start()
        peers = [(_flip(x, k >> 2 & 1), _flip(y, k >> 1 & 1), _flip(c, k & 1)) for k in range(1, N_DEV)]
        sends = [pltpu.make_async_remote_copy(
            src_ref=v_ref, dst_ref=out_ref.at[me], send_sem=send_sems.at[j], recv_sem=recv_sems.at[j],
            device_id=p, device_id_type=MESH) for j, p in enumerate(peers)]
        for cp in sends:
            cp.start()
        for j, (px, py, pc) in enumerate(peers):
            pltpu.make_async_remote_copy(
                src_ref=v_ref, dst_ref=out_ref.at[4 * px + 2 * py + pc], send_sem=send_sems.at[j],
                recv_sem=recv_sems.at[j], device_id=(px, py, pc), device_id_type=MESH).wait_recv()
        for cp in sends:
            cp.wait_send()
        mine.wait()

    vm = pl.BlockSpec(memory_space=pltpu.VMEM)
    return pl.pallas_call(
        body, name=name, out_shape=jax.ShapeDtypeStruct((N_DEV,) + v.shape, v.dtype),
        in_specs=[vm], out_specs=vm,
        scratch_shapes=[pltpu.SemaphoreType.DMA((N_DEV - 1,)), pltpu.SemaphoreType.DMA((N_DEV - 1,)),
                        pltpu.SemaphoreType.DMA],
        compiler_params=_params(),
    )(v)


def _gather_mod(v, w_ada, b_piece):
    ncols = w_ada.shape[1]

    def body(v_ref, w_ref, b_ref, rows_ref, mods_ref, piece, send_sems, recv_sems, local_sems):
        x, y, c = _me()
        me = 4 * x + 2 * y + c
        peers = _peers(x, y, c)

        def exchange(src, dst, rnd):
            mine = pltpu.make_async_copy(src, dst.at[me], local_sems.at[rnd])
            mine.start()
            sends = [pltpu.make_async_remote_copy(
                src_ref=src, dst_ref=dst.at[me], send_sem=send_sems.at[rnd, j], recv_sem=recv_sems.at[rnd, j],
                device_id=p, device_id_type=MESH) for j, p in enumerate(peers)]
            for cp in sends:
                cp.start()
            for j, (px, py, pc) in enumerate(peers):
                pltpu.make_async_remote_copy(
                    src_ref=src, dst_ref=dst.at[4 * px + 2 * py + pc], send_sem=send_sems.at[rnd, j],
                    recv_sem=recv_sems.at[rnd, j], device_id=(px, py, pc), device_id_type=MESH).wait_recv()
            for cp in sends:
                cp.wait_send()
            mine.wait()

        exchange(v_ref, rows_ref, 0)
        c_all = rows_ref[:, 0, :D_MODEL]
        piece[...] = _dot(_bf(_silu(c_all)), _bf(w_ref[...])) + b_ref[...]
        exchange(piece, mods_ref, 1)

    vm = pl.BlockSpec(memory_space=pltpu.VMEM)
    return pl.pallas_call(
        body, name="gather_mod",
        out_shape=(jax.ShapeDtypeStruct((N_DEV,) + v.shape, F32), jax.ShapeDtypeStruct((N_DEV, N_DEV, ncols), F32)),
        in_specs=[vm, vm, vm], out_specs=(vm, vm),
        scratch_shapes=[pltpu.VMEM((N_DEV, ncols), F32), pltpu.SemaphoreType.DMA((2, N_DEV - 1)),
                        pltpu.SemaphoreType.DMA((2, N_DEV - 1)), pltpu.SemaphoreType.DMA((2,))],
        compiler_params=_params(),
    )(v, w_ada, b_piece)


def _ag_relayed(v, name, chunks=1):
    rows = v.shape[0] // chunks
    assert rows * chunks == v.shape[0] and rows % 8 == 0

    def body(v_ref, out_ref, token, send_sems, recv_sems, local_sem):
        token[...] = jnp.zeros_like(token)
        x, y, c = _me()
        flip_x, flip_y = 1 - x, 1 - y
        ax, ay = c * x + (1 - c) * flip_x, c * flip_y + (1 - c) * y
        bx, by = c * flip_x + (1 - c) * x, c * y + (1 - c) * flip_y
        me, sib = (x, y, c), (x, y, 1 - c)
        a, b, dg = (ax, ay, c), (bx, by, c), (flip_x, flip_y, c)
        sa, sb, sdg = (bx, by, 1 - c), (ax, ay, 1 - c), (flip_x, flip_y, 1 - c)

        def piece(ref, k):
            return ref.at[pl.ds(k * rows, rows), :]

        def slot(px, py, pc):
            return out_ref.at[4 * px + 2 * py + pc]

        def copy(n, k, block, to, src=None):
            return pltpu.make_async_remote_copy(
                src_ref=piece(slot(*block) if src is None else src, k), dst_ref=piece(slot(*block), k),
                send_sem=send_sems.at[n * chunks + k], recv_sem=recv_sems.at[n * chunks + k],
                device_id=to, device_id_type=MESH)

        mine = pltpu.make_async_copy(v_ref, slot(*me), local_sem)
        mine.start()
        started = [copy(n, k, me, to, src=v_ref) for k in range(chunks) for n, to in ((1, a), (2, b), (0, sib))]
        for cp in started:
            cp.start()

        def arrived(n, k, block, then):
            copy(n, k, block, me).wait_recv()
            for n2, to in then:
                started.append(copy(n2, k, block, to))
                started[-1].start()

        for k in range(chunks):
            arrived(1, k, a, ((3, b), (4, sib)))
            arrived(2, k, b, ((5, sib),))
        for k in range(chunks):
            arrived(3, k, dg, ((6, sib),))
        for k in range(chunks):
            for n, block in ((0, sib), (4, sa), (5, sb), (6, sdg)):
                copy(n, k, block, me).wait_recv()
        for cp in started:
            cp.wait_send()
        mine.wait()

    out, token = pl.pallas_call(
        body, name=name,
        out_shape=(jax.ShapeDtypeStruct((N_DEV,) + v.shape, v.dtype), jax.ShapeDtypeStruct((8, 128), v.dtype)),
        in_specs=[ANY], out_specs=(ANY, pl.BlockSpec(memory_space=pltpu.VMEM)),
        scratch_shapes=[pltpu.SemaphoreType.DMA((7 * chunks,)), pltpu.SemaphoreType.DMA((7 * chunks,)),
                        pltpu.SemaphoreType.DMA],
        compiler_params=_params(),
    )(v)
    return out, token[0:1, 0:1]


HBM = pl.BlockSpec(memory_space=pltpu.HBM)
SEM = pl.BlockSpec(memory_space=pltpu.SEMAPHORE)
EFFECT = pltpu.SideEffectType.DATAFLOW_SIDE_EFFECTING


def _peers(x, y, c):
    return [(_flip(x, k >> 2 & 1), _flip(y, k >> 1 & 1), _flip(c, k & 1)) for k in range(1, N_DEV)]


def _exchange_start(src, land, gather, name):
    def body(src_ref, land_ref, send_sems, recv_sems, src_thru, land_thru, token):
        x, y, c = _me()
        me = 4 * x + 2 * y + c
        for j, (px, py, pc) in enumerate(_peers(x, y, c)):
            pltpu.make_async_remote_copy(
                src_ref=src_ref if gather else src_ref.at[4 * px + 2 * py + pc], dst_ref=land_ref.at[me],
                send_sem=send_sems.at[j], recv_sem=recv_sems.at[j], device_id=(px, py, pc), device_id_type=MESH).start()
        token[...] = jnp.zeros_like(token)

    sems = pltpu.SemaphoreType.DMA((N_DEV - 1,))
    out = pl.pallas_call(
        body, name=name,
        out_shape=(sems, sems, pltpu.HBM(src.shape, src.dtype), pltpu.HBM(land.shape, land.dtype),
                   jax.ShapeDtypeStruct((8, 128), F32)),
        in_specs=(HBM, HBM), out_specs=(SEM, SEM, HBM, HBM, pl.BlockSpec(memory_space=pltpu.VMEM)),
        input_output_aliases={0: 2, 1: 3},
        compiler_params=pltpu.CompilerParams(has_side_effects=EFFECT),
    )(pltpu.with_memory_space_constraint(src, pltpu.HBM), pltpu.with_memory_space_constraint(land, pltpu.HBM))
    return out[:4], out[4][0, 0]


def _exchange_wait(started, after, gather, name):
    send_sems, recv_sems, src_thru, land_thru = started

    def body(src_ref, land_ref, send_sems, recv_sems, after_ref, src_dead, got_ref):
        x, y, c = _me()
        for j, (px, py, pc) in enumerate(_peers(x, y, c)):
            pid = 4 * px + 2 * py + pc
            cp = pltpu.make_async_remote_copy(
                src_ref=src_ref if gather else src_ref.at[pid], dst_ref=land_ref.at[pid],
                send_sem=send_sems.at[j], recv_sem=recv_sems.at[j], device_id=(px, py, pc), device_id_type=MESH)
            cp.wait_send()
            cp.wait_recv()

    return pl.pallas_call(
        body, name=name,
        out_shape=(pltpu.HBM(src_thru.shape, src_thru.dtype), pltpu.HBM(land_thru.shape, land_thru.dtype)),
        in_specs=(HBM, HBM, SEM, SEM, ANY), out_specs=(HBM, HBM), input_output_aliases={0: 0, 1: 1},
        compiler_params=pltpu.CompilerParams(has_side_effects=EFFECT),
    )(src_thru, land_thru, send_sems, recv_sems, after)[1]


def _silu(a):
    return a * _sig(a)


def _gw_ada(c_all, dmod_piece):
    def body(c_ref, d_ref, o_ref):
        o_ref[...] = _dot_tn(_bf(_silu(c_ref[...])), _bf(d_ref[...]))

    return pl.pallas_call(
        body, name="gw_ada", out_shape=jax.ShapeDtypeStruct((c_all.shape[1], dmod_piece.shape[1]), F32),
        compiler_params=_params(),
    )(c_all, dmod_piece)


def _adam(parts, w, m, v, name):
    k, r, n = parts.shape
    if r <= 256 or r % 256 == 0:
        tr, tn = min(r, 256), n
    else:
        tr, tn = r, 256
    assert r % tr == 0 and n % tn == 0

    def body(p_ref, w_ref, m_ref, v_ref, g_ref, d_ref, nm_ref, nv_ref):
        g = p_ref[0].astype(F32)
        for j in range(1, k):
            g = g + p_ref[j].astype(F32)
        g_ref[...] = g
        d_ref[...], nm_ref[...], nv_ref[...] = _adam_math(g, w_ref[...], m_ref[...], v_ref[...])

    blk = pl.BlockSpec((tr, tn), lambda i, j: (i, j))
    return pl.pallas_call(
        body, name=name, grid=(r // tr, n // tn),
        in_specs=[pl.BlockSpec((k, tr, tn), lambda i, j: (0, i, j)), blk, blk, blk],
        out_specs=[blk, blk, blk, blk],
        out_shape=[jax.ShapeDtypeStruct((r, n), F32)] * 4,
        compiler_params=_params(dimension_semantics=("arbitrary", "arbitrary")),
    )(parts, w, m, v)


def _adam_math(g, w, m, v):
    m_new = ADAM_B1 * m + (1.0 - ADAM_B1) * g
    v_new = ADAM_B2 * v + (1.0 - ADAM_B2) * jnp.square(g)
    m_hat = m_new / (1.0 - ADAM_B1 ** ADAM_STEP)
    v_hat = v_new / (1.0 - ADAM_B2 ** ADAM_STEP)
    return -ADAM_LR * (m_hat / (jnp.sqrt(v_hat) + ADAM_EPS) + ADAM_WD * w), m_new, v_new


_SMALL = (("b_ada", 3 * D_MODEL), ("norm_w", D_MODEL), ("q_norm_w", HEAD_DIM), ("k_norm_w", HEAD_DIM),
          ("rel_bias", REL_BUCKETS * ATTN_HEADS), ("sinks", ATTN_HEADS), ("conv_b", XBC_W), ("dt_bias", SSM_HEADS),
          ("a_log", SSM_HEADS), ("d_skip", SSM_HEADS), ("ssm_norm_w", SSM_W))
_SLOT = tuple(-(-n // 128) * 128 for _, n in _SMALL)
_SLOT_OFF = tuple(int(o) for o in np.cumsum((0,) + _SLOT))
_LOSS_OFF = _SLOT_OFF[-1]
_CW_OFF = _LOSS_OFF + 128
_PACK_N = _CW_OFF + CONV_K * XBC_W


def _pack_partials(small, loss, g_conv_w):
    parts = []
    for (name, n), slot in zip(_SMALL, _SLOT):
        parts.append(small[name].reshape(1, n))
        if slot > n:
            parts.append(jnp.zeros((1, slot - n), F32))
    parts += [loss.reshape(1, 1), jnp.zeros((1, 127), F32), g_conv_w.reshape(1, CONV_K * XBC_W)]
    return jnp.concatenate(parts, axis=1)


def _adam_small(pack_all, w, m, v):
    names = [name for name, _ in _SMALL]

    def body(p_ref, *rest):
        ins, outs = rest[:3 * len(names)], rest[3 * len(names):]

        def total(off, n):
            g = p_ref[0, :, off:off + n]
            for d in range(1, N_DEV):
                g = g + p_ref[d, :, off:off + n]
            return g

        for j, (name, n) in enumerate(_SMALL):
            g = total(_SLOT_OFF[j], n)
            delta, m_new, v_new = _adam_math(g, ins[3 * j][...], ins[3 * j + 1][...], ins[3 * j + 2][...])
            outs[4 * j][...] = g
            outs[4 * j + 1][...] = delta
            outs[4 * j + 2][...] = m_new
            outs[4 * j + 3][...] = v_new
        outs[-1][...] = total(_LOSS_OFF, 1)

    flat = []
    for name, n in _SMALL:
        flat += [w[name].reshape(1, n), m[name].reshape(1, n), v[name].reshape(1, n)]
    out_shape = [jax.ShapeDtypeStruct((1, n), F32) for _, n in _SMALL for _ in range(4)] + [jax.ShapeDtypeStruct((1, 1), F32)]
    out = pl.pallas_call(body, name="adam_small", out_shape=out_shape, compiler_params=_params())(pack_all, *flat)
    res = {name: [out[4 * j + t].reshape(w[name].shape) for t in range(4)] for j, name in enumerate(names)}
    return res, out[-1]


WEIGHTS = ("w_ada", "b_ada", "norm_w", "w_in", "q_norm_w", "k_norm_w", "rel_bias", "sinks", "conv_w", "conv_b",
           "dt_bias", "a_log", "d_skip", "ssm_norm_w", "w_attn_proj", "w_ssm_proj", "w_out")


def kernel(x, c, w_ada, b_ada, norm_w, w_in, q_norm_w, k_norm_w, rel_bias, sinks, conv_w, conv_b, dt_bias, a_log, d_skip, ssm_norm_w, w_attn_proj, w_ssm_proj, w_out, loss_target, m_w_ada, m_b_ada, m_norm_w, m_w_in, m_q_norm_w, m_k_norm_w, m_rel_bias, m_sinks, m_conv_w, m_conv_b, m_dt_bias, m_a_log, m_d_skip, m_ssm_norm_w, m_w_attn_proj, m_w_ssm_proj, m_w_out, v_w_ada, v_b_ada, v_norm_w, v_w_in, v_q_norm_w, v_k_norm_w, v_rel_bias, v_sinks, v_conv_w, v_conv_b, v_dt_bias, v_a_log, v_d_skip, v_ssm_norm_w, v_w_attn_proj, v_w_ssm_proj, v_w_out):
    w = dict(w_ada=w_ada, b_ada=b_ada, norm_w=norm_w, w_in=w_in, q_norm_w=q_norm_w, k_norm_w=k_norm_w,
             rel_bias=rel_bias, sinks=sinks, conv_w=conv_w, conv_b=conv_b, dt_bias=dt_bias, a_log=a_log,
             d_skip=d_skip, ssm_norm_w=ssm_norm_w, w_attn_proj=w_attn_proj, w_ssm_proj=w_ssm_proj, w_out=w_out)
    m = dict(w_ada=m_w_ada, b_ada=m_b_ada, norm_w=m_norm_w, w_in=m_w_in, q_norm_w=m_q_norm_w, k_norm_w=m_k_norm_w,
             rel_bias=m_rel_bias, sinks=m_sinks, conv_w=m_conv_w, conv_b=m_conv_b, dt_bias=m_dt_bias, a_log=m_a_log,
             d_skip=m_d_skip, ssm_norm_w=m_ssm_norm_w, w_attn_proj=m_w_attn_proj, w_ssm_proj=m_w_ssm_proj, w_out=m_w_out)
    v = dict(w_ada=v_w_ada, b_ada=v_b_ada, norm_w=v_norm_w, w_in=v_w_in, q_norm_w=v_q_norm_w, k_norm_w=v_k_norm_w,
             rel_bias=v_rel_bias, sinks=v_sinks, conv_w=v_conv_w, conv_b=v_conv_b, dt_bias=v_dt_bias, a_log=v_a_log,
             d_skip=v_d_skip, ssm_norm_w=v_ssm_norm_w, w_attn_proj=v_w_attn_proj, w_ssm_proj=v_w_ssm_proj, w_out=v_w_out)
    me = 4 * lax.axis_index("x") + 2 * lax.axis_index("y") + lax.axis_index("c")
    ada_n = w_ada.shape[2]
    in_n = w_in.shape[2]
    cw_n = conv_w.shape[2]

    b_piece = lax.dynamic_slice_in_dim(b_ada, me * ada_n, ada_n, axis=1)
    first, mod_all = _gather_mod(jnp.concatenate([c, conv_w[0].reshape(1, CONV_K * cw_n)], axis=1), w_ada[0], b_piece)
    first = first[:, 0]
    c_all = first[:, :D_MODEL]
    conv_w_full = first[:, D_MODEL:].reshape(N_DEV, CONV_K, cw_n).transpose(1, 0, 2).reshape(CONV_K, XBC_W)
    mod = lax.dynamic_index_in_dim(mod_all, me, axis=1, keepdims=False).reshape(1, 3 * D_MODEL)
    shift, scale, gate = mod[:, :D_MODEL], mod[:, D_MODEL:2 * D_MODEL], mod[:, 2 * D_MODEL:]

    pad = -in_n % 24
    w_t, zero = _ag_relayed(jnp.pad(w_in[0].T.astype(BF), ((0, pad), (0, 0))), "ag_w_in", chunks=3)
    w_t = w_t[:, :in_n].reshape(N_DEV * in_n, D_MODEL)

    def with_mine(blocks, mine):
        return lax.dynamic_update_index_in_dim(lax.empty(blocks, mine.dtype), mine, me, axis=0)

    rows = jnp.concatenate([w_attn_proj[0], w_ssm_proj[0], w_out[0]], axis=0).astype(BF) + zero
    r_ap, r_sp = w_attn_proj.shape[1], w_ssm_proj.shape[1]
    rows_started, zero = _exchange_start(rows, with_mine((N_DEV,) + rows.shape, rows), True, "ag_rows_start")

    def rows_fn(after):
        return _exchange_wait(rows_started, after, True, "ag_rows_wait")

    started = {}

    def send_blocks(key, g, name):
        started[key], zero = _exchange_start(
            g, with_mine(g.shape, lax.dynamic_index_in_dim(g, me, axis=0, keepdims=False)), False, name)
        return zero

    def after_mid(g_wap, g_wsp, g_wout):
        return send_blocks("rows", jnp.concatenate(
            [g_wap.reshape(N_DEV, r_ap, D_MODEL), g_wsp.reshape(N_DEV, r_sp, D_MODEL),
             g_wout.reshape(N_DEV, r_ap, D_MODEL)], axis=1), "rs_rows_start")

    def after_gw(g_ws):
        return send_blocks("in", jnp.concatenate(g_ws, axis=0).reshape(N_DEV, in_n, D_MODEL), "rs_in_start")

    r = _local_step(x[0], loss_target[0], shift, scale + zero, gate, w_t, rows_fn, norm_w, q_norm_w, k_norm_w,
                    rel_bias, sinks, conv_w_full, conv_b, dt_bias, a_log, d_skip, ssm_norm_w, after_mid, after_gw)

    small = dict(b_ada=r["dmod"], norm_w=r["g_norm_w"], q_norm_w=r["g_qnw"], k_norm_w=r["g_knw"], rel_bias=r["g_rel"],
                 sinks=r["g_sinks"], conv_b=r["g_conv_b"], dt_bias=r["g_dt_bias"], a_log=r["g_a_log"],
                 d_skip=r["g_d_skip"], ssm_norm_w=r["g_ssm_nw"])
    pack_all = _ag_direct(_pack_partials(small, r["loss"], r["g_conv_w"]), "ag_small")
    res, loss = _adam_small(pack_all, w, m, v)
    loss = loss[0, 0]
    cw_parts = pack_all[:, 0, _CW_OFF:].reshape(N_DEV, CONV_K, XBC_W)
    cw_mine = lax.dynamic_slice_in_dim(cw_parts, me * cw_n, cw_n, axis=2)
    res["conv_w"] = [a[None] for a in _adam(cw_mine, conv_w[0], m_conv_w[0], v_conv_w[0], "adam_conv_w")]

    dmod_piece = lax.dynamic_slice_in_dim(pack_all[:, 0, :3 * D_MODEL], me * ada_n, ada_n, axis=1)
    g_ada = _gw_ada(c_all, dmod_piece)
    res["w_ada"] = [a[None] for a in _adam(g_ada[None], w_ada[0], m_w_ada[0], v_w_ada[0], "adam_w_ada")]

    cat = lambda d: jnp.concatenate([d["w_attn_proj"][0], d["w_ssm_proj"][0], d["w_out"][0]], axis=0)
    rows_res = _adam(_exchange_wait(started["rows"], g_ada, False, "rs_rows_wait"), cat(w), cat(m), cat(v), "adam_w_rows")
    res["w_in"] = [a.T[None] for a in _adam(_exchange_wait(started["in"], rows_res[0], False, "rs_in_wait"),
                                            w_in[0].T, m_w_in[0].T, v_w_in[0].T, "adam_w_in")]
    res["w_attn_proj"] = [a[None, :r_ap] for a in rows_res]
    res["w_ssm_proj"] = [a[None, r_ap:r_ap + r_sp] for a in rows_res]
    res["w_out"] = [a[None, r_ap + r_sp:] for a in rows_res]

    outs = [loss, r["grad_x"][None]]
    for j in range(4):
        outs += [res[name][j] for name in WEIGHTS]
    return tuple(outs)
```

```python
import math

import numpy as np
import jax
import jax.numpy as jnp
from jax import lax
from jax.experimental import pallas as pl
from jax.experimental.pallas import tpu as pltpu

F32 = jnp.float32
BF = jnp.bfloat16
HI = lax.Precision.HIGHEST

D_MODEL = 1024
ATTN_HEADS = 16
KV_HEADS = 4
GRP = ATTN_HEADS // KV_HEADS
HEAD_DIM = 64
ATTN_W = ATTN_HEADS * HEAD_DIM
KV_W = KV_HEADS * HEAD_DIM
BLOCK = 128
REL_BUCKETS = 32
REL_MAX_DIST = 128
SSM_W = 2048
SSM_P = 64
SSM_HEADS = 32
SSM_G = 4
SSM_R = 8
SSM_N = 128
CONV_K = 4
XBC_W = SSM_W + 2 * SSM_G * SSM_N
SEG_W = (ATTN_W, 2 * KV_W, ATTN_W + SSM_W, XBC_W, SSM_HEADS, 2 * D_MODEL)
NSEG = len(SEG_W)
SEG_OFF = tuple(int(v) for v in np.cumsum((0,) + SEG_W))
IN_W = SEG_OFF[-1]
GATE_SEGS = (2, 5)
EPS = 1e-6
N_DEV = 8
ADAM_LR, ADAM_B1, ADAM_B2, ADAM_EPS, ADAM_WD, ADAM_STEP = 0.001, 0.9, 0.999, 1e-08, 0.01, 10
VMEM_LIMIT = 60 * 1024 * 1024
MESH = pl.DeviceIdType.MESH
ANY = pl.BlockSpec(memory_space=pl.ANY)


def _dot(a, b, precision=None):
    return jnp.dot(a, b, preferred_element_type=F32, precision=precision)


def _dot_nt(a, b, precision=None):
    return lax.dot_general(a, b, (((1,), (1,)), ((), ())), preferred_element_type=F32, precision=precision)


def _dot_tn(a, b, precision=None):
    return lax.dot_general(a, b, (((0,), (0,)), ((), ())), preferred_element_type=F32, precision=precision)


def _bf(a):
    return a.astype(BF)


def _sig(a):
    return 0.5 * jnp.tanh(0.5 * a) + 0.5


def _params(**kw):
    return pltpu.CompilerParams(vmem_limit_bytes=VMEM_LIMIT, **kw)


def _full(shape):
    nd = len(shape)
    return pl.BlockSpec(shape, lambda i: (0,) * nd)


def _rows(tm, w):
    return pl.BlockSpec((tm, w), lambda i: (i, 0))


def _inproj(x, norm_w, scale, shift, w_t, tm=512):
    s = x.shape[0]

    def body(x_ref, nw_ref, sc_ref, sh_ref, w_hbm, *rest):
        outs, h_ref, w_vm, sem = rest[:NSEG], rest[NSEG], rest[NSEG + 1], rest[NSEG + 2]
        first = pl.program_id(0) == 0
        cps = [pltpu.make_async_copy(w_hbm.at[SEG_OFF[j]:SEG_OFF[j + 1], :], w_vm.at[SEG_OFF[j]:SEG_OFF[j + 1], :], sem.at[j])
               for j in range(NSEG)]

        def tile(waiting):
            xv = x_ref[...]
            r = lax.rsqrt(jnp.mean(xv * xv, axis=-1, keepdims=True) + EPS)
            h = xv * r * (nw_ref[...] * (1.0 + sc_ref[...])) + sh_ref[...]
            hb = _bf(h)
            h_ref[...] = hb
            for j in range(NSEG):
                if waiting:
                    cps[j].wait()
                outs[j][...] = _dot_nt(hb, w_vm[SEG_OFF[j]:SEG_OFF[j + 1], :]).astype(outs[j].dtype)

        @pl.when(first)
        def _():
            for cp in cps:
                cp.start()
            tile(True)

        @pl.when(jnp.logical_not(first))
        def _():
            tile(False)

    vec = _full((1, D_MODEL))
    return pl.pallas_call(
        body, name="inproj", grid=(s // tm,),
        in_specs=[_rows(tm, D_MODEL), vec, vec, vec, ANY],
        out_specs=[_rows(tm, w) for w in SEG_W] + [_rows(tm, D_MODEL)],
        out_shape=[jax.ShapeDtypeStruct((s, w), BF if j in GATE_SEGS else F32) for j, w in enumerate(SEG_W)]
                  + [jax.ShapeDtypeStruct((s, D_MODEL), BF)],
        scratch_shapes=[pltpu.VMEM((IN_W, D_MODEL), BF), pltpu.SemaphoreType.DMA((NSEG,))],
        compiler_params=_params(dimension_semantics=("arbitrary",)),
    )(x, norm_w, scale, shift, w_t)


def _bucket_onehot_t():
    qi = jnp.arange(BLOCK)[:, None]
    kj = jnp.arange(2 * BLOCK)[None, :]
    dist = qi + BLOCK - kj
    n = jnp.maximum(dist, 0)
    max_exact = REL_BUCKETS // 2
    nf = jnp.maximum(n, 1).astype(F32)
    large = max_exact + (jnp.log(nf / max_exact) / math.log(REL_MAX_DIST / max_exact)
                         * (REL_BUCKETS - max_exact)).astype(jnp.int32)
    large = jnp.minimum(large, REL_BUCKETS - 1)
    bucket = jnp.where(n < max_exact, n, large).reshape(1, BLOCK * 2 * BLOCK)
    return (bucket == jnp.arange(REL_BUCKETS)[:, None]).astype(F32)


def _bias_dense(rel_bias_t, oh_t):
    def body(rb_ref, oh_ref, o_ref):
        o_ref[...] = _dot(rb_ref[...], oh_ref[...], HI)

    return pl.pallas_call(
        body, name="bias_dense", out_shape=jax.ShapeDtypeStruct((ATTN_HEADS, BLOCK * 2 * BLOCK), F32),
        compiler_params=_params(),
    )(rel_bias_t, oh_t)


def _bias_grad(ds_sum, oh_t):
    def body(ds_ref, oh_ref, o_ref):
        o_ref[...] = _dot_nt(ds_ref[...], oh_ref[...], HI)

    return pl.pallas_call(
        body, name="bias_grad", out_shape=jax.ShapeDtypeStruct((ATTN_HEADS, REL_BUCKETS), F32),
        compiler_params=_params(),
    )(ds_sum, oh_t)


def _group_sum(a, e):
    hi = _bf(a)
    return _dot(hi, e) + _dot(_bf(a - hi.astype(F32)), e)


def _group_bcast(a, e3t):
    hi = _bf(a)
    r1 = a - hi.astype(F32)
    mid = _bf(r1)
    return _dot(jnp.concatenate([hi, mid, _bf(r1 - mid.astype(F32))], axis=1), e3t)


def _membership(width, group, ngroups):
    e = (jnp.arange(width)[:, None] // group == jnp.arange(ngroups)[None, :]).astype(BF)
    return e, jnp.tile(e.T, (3, 1))


def _fold(width, group):
    return (jnp.arange(width)[:, None] % group == jnp.arange(group)[None, :]).astype(BF)


def _heads_norm(t, w_x, e, e3t):
    r = lax.rsqrt(_dot(_bf(t * t), e) * (1.0 / HEAD_DIM) + EPS)
    r_x = _group_bcast(r, e3t)
    return t * r_x * w_x, r_x


def _heads_norm_bwd(t, r_x, w_x, d, e, e3t):
    wd = d * w_x
    corr = _group_bcast(_dot(_bf(t * wd), e) * (1.0 / HEAD_DIM), e3t)
    return r_x * wd - t * (r_x * r_x * r_x) * corr, jnp.sum(d * t * r_x, axis=0, keepdims=True)


def _stack_heads(a, hk):
    return jnp.concatenate([a[:, (hk * GRP + g) * HEAD_DIM:(hk * GRP + g + 1) * HEAD_DIM] for g in range(GRP)], axis=0)


def _stack_cols(a, hk):
    return jnp.concatenate([a[:, hk * GRP + g:hk * GRP + g + 1] for g in range(GRP)], axis=0)


def _masked_bias(bias):
    qi = jnp.arange(BLOCK)[:, None]
    kj = jnp.arange(2 * BLOCK)[None, :]
    cur_ok = jnp.logical_and(kj >= BLOCK, kj - BLOCK <= qi)
    both_ok = jnp.logical_or(jnp.logical_and(kj < BLOCK, kj > qi), cur_ok)
    return jnp.stack([jnp.where(cur_ok, bias, -1e30), jnp.where(both_ok, bias, -1e30)])


def _attn_consts(qnw, knw):
    eq, eq3t = _membership(ATTN_W, HEAD_DIM, ATTN_HEADS)
    ek, ek3t = _membership(KV_W, HEAD_DIM, ATTN_HEADS)
    return (jnp.tile(qnw, (1, ATTN_HEADS)), jnp.tile(knw, (1, KV_HEADS)), eq, eq3t, ek, ek3t)


def _attn_fwd(q, kv, bias, sinks, consts):
    s = q.shape[0]
    nb = s // BLOCK
    gq = GRP * BLOCK
    bias_t = bias.reshape(2, KV_HEADS, GRP, BLOCK, 2 * BLOCK).transpose(0, 1, 4, 2, 3).reshape(2, KV_HEADS, 2 * BLOCK, gq)
    sink_rows = jnp.repeat(sinks.reshape(KV_HEADS, GRP), BLOCK, axis=1).reshape(KV_HEADS, 1, gq)
    eye = jnp.eye(BLOCK, dtype=BF)

    def body(q_ref, kp_ref, kc_ref, vp_ref, vc_ref, b_ref, bt_ref, sk_ref, skr_ref, eye_ref,
             qw_ref, kw_ref, eq_ref, eq3_ref, ek_ref, ek3_ref, o_ref, lse_ref):
        qn = _bf(_heads_norm(q_ref[...], qw_ref[...], eq_ref[...], eq3_ref[...])[0] * (HEAD_DIM ** -0.5))
        kn = _bf(_heads_norm(jnp.concatenate([kp_ref[...], kc_ref[...]], axis=0), kw_ref[...], ek_ref[...], ek3_ref[...])[0])
        vv = _bf(jnp.concatenate([vp_ref[...], vc_ref[...]], axis=0))
        ones = jnp.ones((2 * BLOCK, HEAD_DIM), BF)
        kss = [slice(hk * HEAD_DIM, (hk + 1) * HEAD_DIM) for hk in range(KV_HEADS)]
        qgs = [_stack_heads(qn, hk) for hk in range(KV_HEADS)]
        sc_ts = [_dot_nt(kn[:, kss[hk]], qgs[hk]) + bt_ref[0, hk] for hk in range(KV_HEADS)]
        m_rows = [jnp.maximum(jnp.max(sc_ts[hk], axis=0, keepdims=True), skr_ref[hk]) for hk in range(KV_HEADS)]
        m_hq = _bf(jnp.concatenate([(m + jnp.abs(m) * (2.0 ** -7))[:, g * BLOCK:(g + 1) * BLOCK]
                                    for m in m_rows for g in range(GRP)], axis=0))
        m16 = _dot_nt(eye_ref[...], m_hq)
        ms = [_stack_cols(m16, hk) for hk in range(KV_HEADS)]
        scs = [_dot_nt(qgs[hk], kn[:, kss[hk]]) + b_ref[0, hk * GRP:(hk + 1) * GRP].reshape(gq, 2 * BLOCK)
               for hk in range(KV_HEADS)]
        ps = [_bf(jnp.exp(scs[hk] - ms[hk])) for hk in range(KV_HEADS)]
        pvs = [_dot(ps[hk], jnp.concatenate([vv[:, kss[hk]], ones], axis=1)) for hk in range(KV_HEADS)]
        den16 = jnp.concatenate([pvs[hk][g * BLOCK:(g + 1) * BLOCK, HEAD_DIM:HEAD_DIM + 1]
                                 for hk in range(KV_HEADS) for g in range(GRP)], axis=1)
        den16 = den16 + jnp.exp(sk_ref[...] - m16)
        lse_ref[...] = m16 + jnp.log(den16)
        inv16 = 1.0 / den16
        for hk in range(KV_HEADS):
            for g in range(GRP):
                h = hk * GRP + g
                o_ref[:, h * HEAD_DIM:(h + 1) * HEAD_DIM] = (pvs[hk][g * BLOCK:(g + 1) * BLOCK, :HEAD_DIM]
                                                             * inv16[:, h:h + 1])

    cur = lambda w, col=0: pl.BlockSpec((BLOCK, w), lambda i: (i, col))
    prev = lambda w, col=0: pl.BlockSpec((BLOCK, w), lambda i: (jnp.maximum(i - 1, 0), col))
    whole = lambda a: pl.BlockSpec(a.shape, lambda i: (0,) * a.ndim)
    first_or_not = lambda a: pl.BlockSpec((1,) + a.shape[1:], lambda i: (jnp.minimum(i, 1),) + (0,) * (a.ndim - 1))
    return pl.pallas_call(
        body, name="attn_fwd", grid=(nb,),
        in_specs=[cur(ATTN_W), prev(KV_W, 0), cur(KV_W, 0), prev(KV_W, 1), cur(KV_W, 1),
                  first_or_not(bias), first_or_not(bias_t),
                  whole(sinks), whole(sink_rows), whole(eye)] + [_full(c.shape) for c in consts],
        out_specs=[cur(ATTN_W), cur(ATTN_HEADS)],
        out_shape=[jax.ShapeDtypeStruct((s, ATTN_W), F32), jax.ShapeDtypeStruct((s, ATTN_HEADS), F32)],
        compiler_params=_params(dimension_semantics=("arbitrary",)),
    )(q, kv, kv, kv, kv, bias, bias_t, sinks, sink_rows, eye, *consts)


def _conv_taps(xbc, tail):
    ext = jnp.concatenate([tail, xbc], axis=0)
    return [pltpu.roll(ext, CONV_K - 1 - j, axis=0)[8:8 + BLOCK] if j < CONV_K - 1 else xbc for j in range(CONV_K)]


def _softplus(u):
    return jnp.maximum(u, 0.0) + jnp.log(1.0 + jnp.exp(-jnp.abs(u)))


def _tril():
    r = lax.broadcasted_iota(jnp.int32, (BLOCK, BLOCK), 0)
    c = lax.broadcasted_iota(jnp.int32, (BLOCK, BLOCK), 1)
    return r >= c


def _triu():
    r = lax.broadcasted_iota(jnp.int32, (BLOCK, BLOCK), 0)
    c = lax.broadcasted_iota(jnp.int32, (BLOCK, BLOCK), 1)
    return r <= c


def _exact_left(m01, a):
    hi = _bf(a)
    r1 = a - hi.astype(F32)
    mid = _bf(r1)
    return _dot(m01, hi) + _dot(m01, mid) + _dot(m01, _bf(r1 - mid.astype(F32)))


def _ssd_common(conv, dtr, dtb_ref, alog_ref, e3_ref):
    sg = _sig(conv)
    xact = conv * sg
    u = dtr + dtb_ref[...]
    dt = _softplus(u)
    a = -jnp.exp(alog_ref[...])
    trilb = _tril()
    acum = _exact_left(trilb.astype(BF), dt * a) * math.log2(math.e)
    both = _group_bcast(jnp.concatenate([dt, acum], axis=0), e3_ref[...])
    dt_x, acum_x = both[:BLOCK], both[BLOCK:]
    return sg, xact, u, dt, a, trilb, acum, dt_x, acum_x


SSD_CH = 2


def _ssd_fwd(xbc, dt_raw, conv_w, conv_b, dt_bias, a_log, dsk_x, e3t):
    s = xbc.shape[0]
    nc = s // BLOCK
    ch = SSD_CH if nc % SSD_CH == 0 else 1
    rows = ch * BLOCK

    def body(x_ref, tail_ref, dtr_ref, cw_ref, cb_ref, dtb_ref, alog_ref, dsk_ref, e3_ref,
             y_ref, hp_ref, conv_ref, hst, yd_s, yoff_s):
        i = pl.program_id(0)

        @pl.when(i == 0)
        def _():
            hst[...] = jnp.zeros_like(hst)

        for j in range(ch):
            rs = slice(j * BLOCK, (j + 1) * BLOCK)
            tail = jnp.where(i > 0, tail_ref[...], 0.0) if j == 0 else x_ref[j * BLOCK - 8:j * BLOCK, :]
            taps = _conv_taps(x_ref[rs, :], tail)
            conv = cb_ref[...] + sum(taps[t] * cw_ref[t:t + 1, :] for t in range(CONV_K))
            conv_ref[rs, :] = conv
            _, xact, _, _, _, trilb, acum, dt_x, acum_x = _ssd_common(conv, dtr_ref[rs, :], dtb_ref, alog_ref, e3_ref)
            xs = xact[:, :SSM_W]
            acum_t = acum.T
            ea_x = jnp.exp2(acum_x)
            last_x = acum_x[BLOCK - 1:BLOCK, :]
            xdt = xs * dt_x
            xw = xdt * jnp.exp2(last_x - acum_x)
            cd_x = jnp.exp2(last_x)
            hprev = hst[...]
            hp_ref[j] = hprev
            sls = [slice(g * SSM_R * SSM_P, (g + 1) * SSM_R * SSM_P) for g in range(SSM_G)]
            bgs = [_bf(xact[:, SSM_W + g * SSM_N:SSM_W + (g + 1) * SSM_N]) for g in range(SSM_G)]
            cgs = [_bf(xact[:, SSM_W + SSM_G * SSM_N + g * SSM_N:SSM_W + SSM_G * SSM_N + (g + 1) * SSM_N])
                   for g in range(SSM_G)]
            xdt_b, xw_b, hprev_b = _bf(xdt), _bf(xw), _bf(hprev)
            low_half = lax.broadcasted_iota(jnp.int32, (BLOCK, 2 * SSM_P), 1) < SSM_P
            cbs = [_dot_nt(cgs[g], bgs[g]) for g in range(SSM_G)]
            for g in range(SSM_G):
                sl = sls[g]
                yoff_s[:, sl] = _dot(cgs[g], hprev_b[:, sl]) * ea_x[:, sl]
                hst[:, sl] = hprev[:, sl] * cd_x[:, sl] + _dot_tn(bgs[g], xw_b[:, sl])
            for g in range(SSM_G):
                hss = [slice((g * SSM_R + r) * SSM_P, (g * SSM_R + r + 1) * SSM_P) for r in range(SSM_R)]
                mms = [_bf(cbs[g] * jnp.exp2(jnp.where(trilb, acum[:, g * SSM_R + r:g * SSM_R + r + 1]
                                                      - acum_t[g * SSM_R + r:g * SSM_R + r + 1, :], -1e30)))
                       for r in range(SSM_R)]
                for r in range(0, SSM_R, 2):
                    pair = slice(hss[r].start, hss[r + 1].stop)
                    xp = xdt_b[:, pair]
                    rhs = jnp.concatenate([jnp.where(low_half, xp, 0), jnp.where(low_half, 0, xp)], axis=0)
                    yd_s[:, pair] = _dot(jnp.concatenate([mms[r], mms[r + 1]], axis=1), rhs)
            y_ref[rs, :] = yd_s[...] + yoff_s[...] + dsk_ref[...] * xs

    blk = lambda w: pl.BlockSpec((rows, w), lambda i: (i, 0))
    return pl.pallas_call(
        body, name="ssd_fwd", grid=(nc // ch,),
        in_specs=[blk(XBC_W), pl.BlockSpec((8, XBC_W), lambda i: (jnp.maximum(i * (rows // 8) - 1, 0), 0)),
                  blk(SSM_HEADS), _full((CONV_K, XBC_W)), _full((1, XBC_W)), _full((1, SSM_HEADS)),
                  _full((1, SSM_HEADS)), _full((1, SSM_W)), _full((3 * SSM_HEADS, SSM_W))],
        out_specs=[blk(SSM_W), pl.BlockSpec((ch, SSM_N, SSM_W), lambda i: (i, 0, 0)), blk(XBC_W)],
        out_shape=[jax.ShapeDtypeStruct((s, SSM_W), F32), jax.ShapeDtypeStruct((nc, SSM_N, SSM_W), F32),
                   jax.ShapeDtypeStruct((s, XBC_W), F32)],
        scratch_shapes=[pltpu.VMEM((SSM_N, SSM_W), F32), pltpu.VMEM((BLOCK, SSM_W), F32), pltpu.VMEM((BLOCK, SSM_W), F32)],
        compiler_params=_params(dimension_semantics=("arbitrary",)),
    )(xbc, xbc, dt_raw, conv_w, conv_b, dt_bias, a_log, dsk_x, e3t)


def _dsilu(z, sg, silu):
    return sg * (1.0 + (z - silu))


def _mid(x, tgt, o_att, zam, ypre, gab, gate, ssm_nw, rows_all, tm=256):
    s = x.shape[0]
    gw = SSM_W // SSM_G

    r_ap, r_sp = ATTN_W // N_DEV, SSM_W // N_DEV

    def body(x_ref, t_ref, o_ref, zam_ref, yp_ref, gab_ref, gate_ref, nw_ref, rows_h,
             dout_ref, do_ref, dzam_ref, dyp_ref, dgab_ref,
             yag_ref, dya_ref, yn_ref, dyb_ref, mg_ref, dob_ref, gnw_ref, dgate_ref, loss_ref,
             wap_v, wsp_v, wout_v, sem):
        i = pl.program_id(0)

        @pl.when(i == 0)
        def _():
            cps = []
            for d in range(N_DEV):
                for j, (dst, r0, rn) in enumerate(((wap_v, 0, r_ap), (wsp_v, r_ap, r_sp), (wout_v, r_ap + r_sp, r_ap))):
                    cps.append(pltpu.make_async_copy(rows_h.at[d, r0:r0 + rn, :], dst.at[d * rn:(d + 1) * rn, :], sem.at[j]))
            for cp in cps:
                cp.start()
            gnw_ref[...] = jnp.zeros_like(gnw_ref)
            dgate_ref[...] = jnp.zeros_like(dgate_ref)
            loss_ref[...] = jnp.zeros_like(loss_ref)
            for cp in cps:
                cp.wait()

        gate = gate_ref[...]
        nw = nw_ref[...]
        o_att = o_ref[...]
        z_a = zam_ref[:, :ATTN_W].astype(F32)
        s_a = _sig(z_a)
        silu_a = z_a * s_a
        yag = _bf(o_att * silu_a)
        yag_ref[...] = yag
        ypre = yp_ref[...]
        z_m = zam_ref[:, ATTN_W:].astype(F32)
        s_m = _sig(z_m)
        silu_m = z_m * s_m
        yg = ypre * silu_m
        rinv = jnp.concatenate(
            [jnp.broadcast_to(lax.rsqrt(jnp.mean(yg[:, g * gw:(g + 1) * gw] ** 2, axis=-1, keepdims=True) + EPS), (tm, gw))
             for g in range(SSM_G)], axis=1)
        ynr = yg * rinv
        yn = _bf(ynr * nw)
        yn_ref[...] = yn
        y_a = _dot(yag, wap_v[...])
        y_b = _dot(yn, wsp_v[...])
        g_a = _sig(gab_ref[:, :D_MODEL].astype(F32))
        g_b = _sig(gab_ref[:, D_MODEL:].astype(F32))
        merged = _bf(g_a * y_a + g_b * y_b)
        mg_ref[...] = merged
        o = _dot(merged, wout_v[...])
        diff = x_ref[...] + gate * o - t_ref[...]
        loss_ref[...] += (0.5 / D_MODEL) * jnp.sum(diff * diff, axis=(0, 1), keepdims=True)
        dout = diff * (1.0 / D_MODEL)
        dout_ref[...] = dout
        dgate_ref[...] += jnp.sum(dout * o, axis=0, keepdims=True)
        d_o = _bf(dout * gate)
        dob_ref[...] = d_o
        dmerged = _dot_nt(d_o, wout_v[...])
        dy_af = dmerged * g_a
        dy_bf = dmerged * g_b
        dy_a = _bf(dy_af)
        dy_b = _bf(dy_bf)
        dya_ref[...] = dy_a
        dyb_ref[...] = dy_b
        dyag = _dot_nt(dy_a, wap_v[...])
        dyn = _dot_nt(dy_b, wsp_v[...])
        dgab_ref[:, :D_MODEL] = _bf(dy_af * y_a * (1.0 - g_a))
        dgab_ref[:, D_MODEL:] = _bf(dy_bf * y_b * (1.0 - g_b))
        do_ref[...] = dyag * silu_a
        dzam_ref[:, :ATTN_W] = _bf(dyag * o_att * _dsilu(z_a, s_a, silu_a))
        gnw_ref[...] += jnp.sum(dyn * ynr, axis=0, keepdims=True)
        dynw = dyn * nw
        corr = jnp.concatenate(
            [jnp.broadcast_to(jnp.mean((dynw * ynr)[:, g * gw:(g + 1) * gw], axis=-1, keepdims=True), (tm, gw))
             for g in range(SSM_G)], axis=1)
        dyg = rinv * (dynw - ynr * corr)
        dyp_ref[...] = dyg * silu_m
        dzam_ref[:, ATTN_W:] = _bf(dyg * ypre * _dsilu(z_m, s_m, silu_m))

    r1, r2, r3 = _rows(tm, D_MODEL), _rows(tm, SSM_W), _rows(tm, ATTN_W + SSM_W)
    sd = jax.ShapeDtypeStruct
    return pl.pallas_call(
        body, name="mid", grid=(s // tm,),
        in_specs=[r1, r1, r1, r3, r2, r2, _full((1, D_MODEL)), _full((1, SSM_W)), ANY],
        out_specs=[r1, r1, r3, r2, r2, r1, r1, r2, r1, r1, r1,
                   _full((1, SSM_W)), _full((1, D_MODEL)), _full((1, 1))],
        out_shape=[sd((s, D_MODEL), F32), sd((s, ATTN_W), F32), sd((s, ATTN_W + SSM_W), BF), sd((s, SSM_W), F32),
                   sd((s, 2 * D_MODEL), BF),
                   sd((s, ATTN_W), BF), sd((s, D_MODEL), BF), sd((s, SSM_W), BF), sd((s, D_MODEL), BF),
                   sd((s, D_MODEL), BF), sd((s, D_MODEL), BF),
                   sd((1, SSM_W), F32), sd((1, D_MODEL), F32), sd((1, 1), F32)],
        scratch_shapes=[pltpu.VMEM((ATTN_W, D_MODEL), BF), pltpu.VMEM((SSM_W, D_MODEL), BF), pltpu.VMEM((D_MODEL, D_MODEL), BF),
                        pltpu.SemaphoreType.DMA((3,))],
        compiler_params=_params(dimension_semantics=("arbitrary",)),
    )(x, tgt, o_att, zam, ypre, gab, gate, ssm_nw, rows_all)


def _attn_bwd(q, kv, bias, sinks, consts, o_att, lse, d_o):
    s = q.shape[0]
    nb = s // BLOCK
    folds = (_fold(ATTN_W, HEAD_DIM), _fold(KV_W, HEAD_DIM))

    def body(q_ref, kp_ref, kc_ref, vp_ref, vc_ref, b_ref, skv_ref, qw_ref, kw_ref, eq_ref, eq3_ref, ek_ref, ek3_ref,
             fq_ref, fk_ref, o_ref, lse_ref, do_ref,
             dq_ref, dkv_ref, dss_ref, gqw_ref, gkw_ref, gsk_ref, ckn, cv, dqn_s, dkn_s, dv_s, gq_x, gk_x):
        i = pl.program_id(0)
        kw, ek, ek3 = kw_ref[...], ek_ref[...], ek3_ref[...]

        @pl.when(i == 0)
        def _():
            for ref in (ckn, cv, dss_ref, gq_x, gk_x, gsk_ref):
                ref[...] = jnp.zeros_like(ref)

        @pl.when(i < nb)
        def _():
            qw, eq, eq3 = qw_ref[...], eq_ref[...], eq3_ref[...]
            qf = q_ref[...]
            qnf, rq_x = _heads_norm(qf, qw, eq, eq3)
            qn = _bf(qnf * (HEAD_DIM ** -0.5))
            kf = jnp.concatenate([kp_ref[...], kc_ref[...]], axis=0)
            knf, rk_x = _heads_norm(kf, kw, ek, ek3)
            kn = _bf(knf)
            vv = _bf(jnp.concatenate([vp_ref[...], vc_ref[...]], axis=0))
            d_of = do_ref[...]
            d_ob = _bf(d_of)
            lse_all = lse_ref[...]
            delta = _dot(_bf(d_of * o_ref[...]), eq)
            gsk_ref[...] += jnp.sum(-jnp.exp(skv_ref[...] - lse_all) * delta, axis=0, keepdims=True)
            kss = [slice(hk * HEAD_DIM, (hk + 1) * HEAD_DIM) for hk in range(KV_HEADS)]
            qgs = [_stack_heads(qn, hk) for hk in range(KV_HEADS)]
            d_ogs = [_stack_heads(d_ob, hk) for hk in range(KV_HEADS)]
            scs = [_dot_nt(qgs[hk], kn[:, kss[hk]]) + b_ref[0, hk * GRP:(hk + 1) * GRP].reshape(GRP * BLOCK, 2 * BLOCK)
                   for hk in range(KV_HEADS)]
            dps = [_dot_nt(d_ogs[hk], vv[:, kss[hk]]) for hk in range(KV_HEADS)]
            ps = [jnp.exp(scs[hk] - _stack_cols(lse_all, hk)) for hk in range(KV_HEADS)]
            dss = [ps[hk] * (dps[hk] - _stack_cols(delta, hk)) for hk in range(KV_HEADS)]
            pbs = [_bf(p) for p in ps]
            dsbs = [_bf(ds) for ds in dss]
            for hk in range(KV_HEADS):
                dss_ref[hk * GRP:(hk + 1) * GRP] += dss[hk].reshape(GRP, BLOCK, 2 * BLOCK)
            for hk in range(KV_HEADS):
                dv_s[:, kss[hk]] = _dot_tn(pbs[hk], d_ogs[hk])
                dkn_s[:, kss[hk]] = _dot_tn(dsbs[hk], qgs[hk])
            dqns = [_dot(dsbs[hk], kn[:, kss[hk]]) * (HEAD_DIM ** -0.5) for hk in range(KV_HEADS)]
            for hk in range(KV_HEADS):
                for g in range(GRP):
                    h = hk * GRP + g
                    dqn_s[:, h * HEAD_DIM:(h + 1) * HEAD_DIM] = dqns[hk][g * BLOCK:(g + 1) * BLOCK]
            dq, gq = _heads_norm_bwd(qf, rq_x, qw, dqn_s[...], eq, eq3)
            dq_ref[...] = _bf(dq)
            gq_x[...] += gq
            dk, gk = _heads_norm_bwd(kf[:BLOCK], rk_x[:BLOCK], kw, ckn[...] + dkn_s[0:BLOCK, :], ek, ek3)
            dkv_ref[:, :KV_W] = _bf(dk)
            gk_x[...] += gk
            dkv_ref[:, KV_W:] = _bf(cv[...] + dv_s[0:BLOCK, :])
            ckn[...] = dkn_s[BLOCK:2 * BLOCK, :]
            cv[...] = dv_s[BLOCK:2 * BLOCK, :]

        @pl.when(i == nb)
        def _():
            kc = kc_ref[...]
            dk, gk = _heads_norm_bwd(kc, _heads_norm(kc, kw, ek, ek3)[1], kw, ckn[...], ek, ek3)
            dkv_ref[:, :KV_W] = _bf(dk)
            dkv_ref[:, KV_W:] = _bf(cv[...])
            gqw_ref[...] = _group_sum(jnp.broadcast_to(gq_x[...], (8, ATTN_W)), fq_ref[...])[0:1]
            gkw_ref[...] = _group_sum(jnp.broadcast_to(gk_x[...] + gk, (8, KV_W)), fk_ref[...])[0:1]

    last = nb - 1
    cur = lambda w, col=0: pl.BlockSpec((BLOCK, w), lambda i: (jnp.minimum(i, last), col))
    prev = lambda w, col=0: pl.BlockSpec((BLOCK, w), lambda i: (jnp.maximum(jnp.minimum(i, last) - 1, 0), col))
    late = lambda w: pl.BlockSpec((BLOCK, w), lambda i: (jnp.maximum(i - 1, 0), 0))
    sd = jax.ShapeDtypeStruct
    return pl.pallas_call(
        body, name="attn_bwd", grid=(nb + 1,),
        in_specs=[cur(ATTN_W), prev(KV_W, 0), cur(KV_W, 0), prev(KV_W, 1), cur(KV_W, 1),
                  pl.BlockSpec((1, ATTN_HEADS, BLOCK, 2 * BLOCK), lambda i: (jnp.minimum(i, 1), 0, 0, 0)),
                  _full((1, ATTN_HEADS))]
                 + [_full(c.shape) for c in consts + folds] + [cur(ATTN_W), cur(ATTN_HEADS), cur(ATTN_W)],
        out_specs=[cur(ATTN_W), late(2 * KV_W),
                   pl.BlockSpec((ATTN_HEADS, BLOCK, 2 * BLOCK), lambda i: (0, 0, 0)),
                   _full((1, HEAD_DIM)), _full((1, HEAD_DIM)), _full((1, ATTN_HEADS))],
        out_shape=[sd((s, ATTN_W), BF), sd((s, 2 * KV_W), BF),
                   sd((ATTN_HEADS, BLOCK, 2 * BLOCK), F32), sd((1, HEAD_DIM), F32), sd((1, HEAD_DIM), F32),
                   sd((1, ATTN_HEADS), F32)],
        scratch_shapes=[pltpu.VMEM((BLOCK, KV_W), F32), pltpu.VMEM((BLOCK, KV_W), F32),
                        pltpu.VMEM((BLOCK, ATTN_W), F32), pltpu.VMEM((2 * BLOCK, KV_W), F32),
                        pltpu.VMEM((2 * BLOCK, KV_W), F32), pltpu.VMEM((1, ATTN_W), F32), pltpu.VMEM((1, KV_W), F32)],
        compiler_params=_params(dimension_semantics=("arbitrary",)),
    )(q, kv, kv, kv, kv, bias, sinks, *consts, *folds, o_att, lse, d_o)


def _ssd_bwd(xbc, conv_all, dt_raw, conv_w, dt_bias, a_log, dsk_x, e_mat, e3t, hprev_all, dy_all):
    s = xbc.shape[0]
    nc = s // BLOCK
    ch = 1
    rows = ch * BLOCK
    nsteps = nc // ch
    gw = SSM_R * SSM_P
    b0, c0 = SSM_W, SSM_W + SSM_G * SSM_N

    def body(x_ref, conv_ref, dtr_ref, cw_ref, dtb_ref, alog_ref, dsk_ref, e_ref, e3_ref, hp_ref, dy_ref,
             dx_ref, ddt_ref, gcw_ref, gcb_ref, gdtb_ref, galog_ref, gdsk_ref,
             dh, nhead, gdskx, dxdt_s, dbc_s, dxd_s):
        def chunk_bwd(j):
            rs = slice(j * BLOCK, (j + 1) * BLOCK)
            conv = conv_ref[rs, :]
            sg, xact, u, dt, a, trilb, acum, dt_x, acum_x = _ssd_common(conv, dtr_ref[rs, :], dtb_ref, alog_ref, e3_ref)
            xs = xact[:, :SSM_W]
            acum_t = acum.T
            ea_x = jnp.exp2(acum_x)
            last_x = acum_x[BLOCK - 1:BLOCK, :]
            dte_x = jnp.exp2(last_x - acum_x)
            cd_x = jnp.exp2(last_x)
            xdt = xs * dt_x
            xw = xdt * dte_x
            hprev = hp_ref[j]
            dhn = dh[...]
            dy = dy_ref[rs, :]
            gdskx[...] += jnp.sum(dy * xs, axis=0, keepdims=True)
            dyea = dy * ea_x
            lane = lax.broadcasted_iota(jnp.int32, (BLOCK, SSM_HEADS), 1)
            dacum = jnp.zeros((BLOCK, SSM_HEADS), F32)
            dacc_x, dlast_x = [], []
            sls = [slice(g * gw, (g + 1) * gw) for g in range(SSM_G)]
            bgs = [_bf(xact[:, b0 + g * SSM_N:b0 + (g + 1) * SSM_N]) for g in range(SSM_G)]
            cgs = [_bf(xact[:, c0 + g * SSM_N:c0 + (g + 1) * SSM_N]) for g in range(SSM_G)]
            hpgs = [_bf(hprev[:, sl]) for sl in sls]
            dhgs = [_bf(dhn[:, sl]) for sl in sls]
            dyeags = [_bf(dyea[:, sl]) for sl in sls]
            xwgs = [_bf(xw[:, sl]) for sl in sls]
            xdt_b, dy_b = _bf(xdt), _bf(dy)
            low_half = lax.broadcasted_iota(jnp.int32, (BLOCK, 2 * SSM_P), 1) < SSM_P
            cbs = [_dot_nt(cgs[g], bgs[g]) for g in range(SSM_G)]
            gmats = [_dot(cgs[g], hpgs[g]) for g in range(SSM_G)]
            dxws = [_dot(bgs[g], dhgs[g]) for g in range(SSM_G)]
            dcgs = [_dot_nt(dyeags[g], hpgs[g]) for g in range(SSM_G)]
            dbgs = [_dot_nt(xwgs[g], dhgs[g]) for g in range(SSM_G)]
            for g in range(SSM_G):
                sl = sls[g]
                dh[:, sl] = dhn[:, sl] * cd_x[:, sl] + _dot_tn(cgs[g], dyeags[g])
                dxdt_s[:, sl] = dxws[g] * dte_x[:, sl]
                dacc_x.append(dy[:, sl] * gmats[g] * ea_x[:, sl] - dxws[g] * xw[:, sl])
                dlast_x.append(jnp.sum(dxws[g] * xw[:, sl], axis=0, keepdims=True)
                               + jnp.sum(dhn[:, sl] * hprev[:, sl], axis=0, keepdims=True) * cd_x[:, sl])
            for g in range(SSM_G):
                bg, cg, cb, dbg, dcg = bgs[g], cgs[g], cbs[g], dbgs[g], dcgs[g]
                hss = [slice((g * SSM_R + r) * SSM_P, (g * SSM_R + r + 1) * SSM_P) for r in range(SSM_R)]
                lms = [jnp.exp2(jnp.where(trilb, acum[:, g * SSM_R + r:g * SSM_R + r + 1]
                                         - acum_t[g * SSM_R + r:g * SSM_R + r + 1, :], -1e30)) for r in range(SSM_R)]
                mms = [cb * lm for lm in lms]
                mmbs = [_bf(mm) for mm in mms]
                dms = []
                for r in range(0, SSM_R, 2):
                    pair = slice(hss[r].start, hss[r + 1].stop)
                    xp, dyp = xdt_b[:, pair], dy_b[:, pair]
                    dmp = _dot_nt(dyp, jnp.concatenate([jnp.where(low_half, xp, 0), jnp.where(low_half, 0, xp)], axis=0))
                    dms += [dmp[:, :BLOCK], dmp[:, BLOCK:]]
                    dxd_s[:, pair] = _dot_tn(jnp.concatenate([mmbs[r], mmbs[r + 1]], axis=0),
                                             jnp.concatenate([jnp.where(low_half, dyp, 0), jnp.where(low_half, 0, dyp)], axis=0))
                dcb = sum(dms[r] * lms[r] for r in range(SSM_R))
                wms = [dms[r] * mms[r] for r in range(SSM_R)]
                antis = [_bf(wm - wm.T) for wm in wms]
                for r in range(SSM_R):
                    dacum = dacum + _dot(antis[r], (lane == g * SSM_R + r).astype(BF))
                dcbb = _bf(dcb)
                dbc_s[:, g * SSM_N:(g + 1) * SSM_N] = dbg + _dot_tn(dcbb, cg)
                dbc_s[:, SSM_G * SSM_N + g * SSM_N:SSM_G * SSM_N + (g + 1) * SSM_N] = dcg + _dot(dcbb, bg)
            dxdt = dxdt_s[...] + dxd_s[...]
            dxs = dy * dsk_ref[...] + dxdt * dt_x
            red = _group_sum(jnp.concatenate(
                [dxdt * xs, jnp.concatenate(dacc_x, axis=1),
                 jnp.broadcast_to(jnp.concatenate(dlast_x, axis=1), (8, SSM_W))], axis=0), e_ref[...])
            row = lax.broadcasted_iota(jnp.int32, (BLOCK, SSM_HEADS), 0)
            dacum = dacum + red[BLOCK:2 * BLOCK] + jnp.where(row == BLOCK - 1, red[2 * BLOCK:2 * BLOCK + 1], 0.0)
            ddta = _exact_left(_triu().astype(BF), dacum)
            ddt = red[:BLOCK] + ddta * a
            galog_ref[...] += jnp.sum(ddta * dt, axis=0, keepdims=True) * a
            du = ddt * _sig(u)
            ddt_ref[rs, :] = _bf(du)
            gdtb_ref[...] += jnp.sum(du, axis=0, keepdims=True)
            dconv = jnp.concatenate([dxs, dbc_s[...]], axis=1) * _dsilu(conv, sg, xact)
            gcb_ref[...] += jnp.sum(dconv, axis=0, keepdims=True)
            ext2 = jnp.concatenate([dconv, nhead[...]], axis=0)
            ahead = [pltpu.roll(ext2, BLOCK + 8 - (CONV_K - 1 - j), axis=0)[0:BLOCK] if j < CONV_K - 1 else dconv
                     for j in range(CONV_K)]
            dx_ref[rs, :] = _bf(sum(ahead[j] * cw_ref[j:j + 1, :] for j in range(CONV_K)))
            xraw = x_ref[rs, :]
            gcw_ref[...] += jnp.concatenate([jnp.sum(ahead[j] * xraw, axis=0, keepdims=True) for j in range(CONV_K)], axis=0)
            nhead[...] = dconv[0:8]

        i = pl.program_id(0)

        @pl.when(i == 0)
        def _():
            for ref in (dh, nhead, gdskx, gcw_ref, gcb_ref, gdtb_ref, galog_ref, gdsk_ref):
                ref[...] = jnp.zeros_like(ref)

        for j in reversed(range(ch)):
            chunk_bwd(j)

        @pl.when(i == nsteps - 1)
        def _():
            gdsk_ref[...] = _group_sum(jnp.broadcast_to(gdskx[...], (8, SSM_W)), e_ref[...])[0:1]

    chunk = lambda w: pl.BlockSpec((rows, w), lambda i: (nsteps - 1 - i, 0))
    sd = jax.ShapeDtypeStruct
    return pl.pallas_call(
        body, name="ssd_bwd", grid=(nsteps,),
        in_specs=[chunk(XBC_W), chunk(XBC_W),
                  chunk(SSM_HEADS), _full((CONV_K, XBC_W)), _full((1, SSM_HEADS)),
                  _full((1, SSM_HEADS)), _full((1, SSM_W)), _full((SSM_W, SSM_HEADS)), _full((3 * SSM_HEADS, SSM_W)),
                  pl.BlockSpec((ch, SSM_N, SSM_W), lambda i: (nsteps - 1 - i, 0, 0)), chunk(SSM_W)],
        out_specs=[chunk(XBC_W), chunk(SSM_HEADS), _full((CONV_K, XBC_W)), _full((1, XBC_W)),
                   _full((1, SSM_HEADS)), _full((1, SSM_HEADS)), _full((1, SSM_HEADS))],
        out_shape=[sd((s, XBC_W), BF), sd((s, SSM_HEADS), BF), sd((CONV_K, XBC_W), F32), sd((1, XBC_W), F32),
                   sd((1, SSM_HEADS), F32), sd((1, SSM_HEADS), F32), sd((1, SSM_HEADS), F32)],
        scratch_shapes=[pltpu.VMEM((SSM_N, SSM_W), F32), pltpu.VMEM((8, XBC_W), F32),
                        pltpu.VMEM((1, SSM_W), F32), pltpu.VMEM((BLOCK, SSM_W), F32),
                        pltpu.VMEM((BLOCK, 2 * SSM_G * SSM_N), F32), pltpu.VMEM((BLOCK, SSM_W), F32)],
        compiler_params=_params(dimension_semantics=("arbitrary",)),
    )(xbc, conv_all, dt_raw, conv_w, dt_bias, a_log, dsk_x, e_mat, e3t, hprev_all, dy_all)


def _dh(x, dout, norm_w, scale, dsegs, w_t, tm=256):
    s = x.shape[0]

    def body(x_ref, dout_ref, nw_ref, sc_ref, *rest):
        d_refs, w_hbm = rest[:NSEG], rest[NSEG]
        gx_ref, dshift_ref, dscale_ref, gnw_ref = rest[NSEG + 1:NSEG + 5]
        w_vm, sem = rest[NSEG + 5], rest[NSEG + 6]
        first = pl.program_id(0) == 0
        cps = [pltpu.make_async_copy(w_hbm.at[SEG_OFF[j]:SEG_OFF[j + 1], :], w_vm.at[SEG_OFF[j]:SEG_OFF[j + 1], :], sem.at[j])
               for j in range(NSEG)]

        def tile(waiting):
            dh = None
            for j in range(NSEG):
                if waiting:
                    cps[j].wait()
                part = _dot(d_refs[j][...], w_vm[SEG_OFF[j]:SEG_OFF[j + 1], :])
                dh = part if dh is None else dh + part
            xv = x_ref[...]
            r = lax.rsqrt(jnp.mean(xv * xv, axis=-1, keepdims=True) + EPS)
            xn = xv * r
            nw = nw_ref[...]
            sc1 = 1.0 + sc_ref[...]
            dshift_ref[...] += jnp.sum(dh, axis=0, keepdims=True)
            dhxn = jnp.sum(dh * xn, axis=0, keepdims=True)
            dscale_ref[...] += dhxn * nw
            gnw_ref[...] += dhxn * sc1
            dxn = dh * (nw * sc1)
            gx_ref[...] = dout_ref[...] + r * (dxn - xn * jnp.mean(xn * dxn, axis=-1, keepdims=True))

        @pl.when(first)
        def _():
            for cp in cps:
                cp.start()
            for ref in (dshift_ref, dscale_ref, gnw_ref):
                ref[...] = jnp.zeros_like(ref)
            tile(True)

        @pl.when(jnp.logical_not(first))
        def _():
            tile(False)

    vec = _full((1, D_MODEL))
    sd = jax.ShapeDtypeStruct
    return pl.pallas_call(
        body, name="dh", grid=(s // tm,),
        in_specs=[_rows(tm, D_MODEL), _rows(tm, D_MODEL), vec, vec] + [_rows(tm, w) for w in SEG_W] + [ANY],
        out_specs=[_rows(tm, D_MODEL), vec, vec, vec],
        out_shape=[sd((s, D_MODEL), F32), sd((1, D_MODEL), F32), sd((1, D_MODEL), F32), sd((1, D_MODEL), F32)],
        scratch_shapes=[pltpu.VMEM((IN_W, D_MODEL), BF), pltpu.SemaphoreType.DMA((NSEG,))],
        compiler_params=_params(dimension_semantics=("arbitrary",)),
    )(x, dout, norm_w, scale, *dsegs, w_t)


def _gw_seg(h, dseg, name, tm=2048):
    s, w = dseg.shape
    tn = w if w <= 2048 else w // 2
    tm = min(tm, s)
    nm = s // tm

    def body(h_ref, d_ref, o_ref, acc):
        m = pl.program_id(1)

        @pl.when(m == 0)
        def _():
            acc[...] = jnp.zeros_like(acc)

        acc[...] += _dot_tn(d_ref[...], h_ref[...])

        @pl.when(m == nm - 1)
        def _():
            o_ref[...] = _bf(acc[...])

    return pl.pallas_call(
        body, name=name, grid=(w // tn, nm),
        in_specs=[pl.BlockSpec((tm, D_MODEL), lambda n, m: (m, 0)), pl.BlockSpec((tm, tn), lambda n, m: (m, n))],
        out_specs=pl.BlockSpec((tn, D_MODEL), lambda n, m: (n, 0)),
        out_shape=jax.ShapeDtypeStruct((w, D_MODEL), BF),
        scratch_shapes=[pltpu.VMEM((tn, D_MODEL), F32)],
        compiler_params=_params(dimension_semantics=("arbitrary", "arbitrary")),
    )(h, dseg)


def _gw_in(h, dsegs):
    return [_gw_seg(h, d, "gw_in_%d" % j) for j, d in enumerate(dsegs)]


def _local_step(x, tgt, shift, scale, gate, w_t, rows_fn, norm_w, qnw, knw, rel_bias, sinks,
                conv_w, conv_b, dt_bias, a_log, d_skip, ssm_nw, after_mid=None, after_gw=None):
    oh_t = _bucket_onehot_t()
    bias = _masked_bias(_bias_dense(rel_bias.T, oh_t).reshape(ATTN_HEADS, BLOCK, 2 * BLOCK))
    *segs, h = _inproj(x, norm_w, scale, shift, w_t)
    q, kv, zam, xbc, dtr, gab = segs
    consts = _attn_consts(qnw, knw)
    o_att, lse = _attn_fwd(q, kv, bias, sinks, consts)
    e_mat, e3t = _membership(SSM_W, SSM_P, SSM_HEADS)
    dsk_x = jnp.repeat(d_skip, SSM_P, axis=1)
    ypre, hprev, conv = _ssd_fwd(xbc, dtr, conv_w, conv_b, dt_bias, a_log, dsk_x, e3t)
    (dout, d_o, dzam, dyp, dgab, yag, dy_a, yn, dy_b, merged, dob, g_ssm_nw, dgate, loss) = _mid(
        x, tgt, o_att, zam, ypre, gab, gate, ssm_nw, rows_fn(ypre))
    g_wap = _gw_seg(dy_a, yag, "gw_attn_proj")
    g_wsp = _gw_seg(dy_b, yn, "gw_ssm_proj")
    g_wout = _gw_seg(dob, merged, "gw_out")
    zero = after_mid(g_wap, g_wsp, g_wout) if after_mid is not None else 0.0
    dq, dkv, dss, g_qnw, g_knw, g_sinks = _attn_bwd(q, kv, bias, sinks + zero, consts, o_att, lse, d_o)
    g_rel = _bias_grad(dss.reshape(ATTN_HEADS, BLOCK * 2 * BLOCK), oh_t).T
    dxbc, ddt, g_cw, g_cb, g_dtb, g_alog, g_dsk = _ssd_bwd(
        xbc, conv, dtr, conv_w, dt_bias, a_log, dsk_x, e_mat, e3t, hprev, dyp)
    dsegs = (dq, dkv, dzam, dxbc, ddt, dgab)
    g_ws = _gw_in(h, dsegs)
    zero = after_gw(g_ws) if after_gw is not None else 0.0
    gx, dshift, dscale, g_nw = _dh(x, dout, norm_w + zero, scale, dsegs, w_t)
    return dict(loss=loss, grad_x=gx, dmod=jnp.concatenate([dshift, dscale, dgate], axis=1), g_ws=g_ws,
                g_wap=g_wap, g_wsp=g_wsp, g_wout=g_wout, g_norm_w=g_nw, g_qnw=g_qnw, g_knw=g_knw, g_rel=g_rel,
                g_sinks=g_sinks, g_conv_w=g_cw, g_conv_b=g_cb, g_dt_bias=g_dtb, g_a_log=g_alog, g_d_skip=g_dsk,
                g_ssm_nw=g_ssm_nw)


def _me():
    return lax.axis_index("x"), lax.axis_index("y"), lax.axis_index("c")


def _flip(v, bit):
    return 1 - v if bit else v


def _ag_direct(v, name):
    def body(v_ref, out_ref, send_sems, recv_sems, local_sem):
        x, y, c = _me()
        me = 4 * x + 2 * y + c
        mine = pltpu.make_async_copy(v_ref, out_ref.at[me], local_sem)
        mine.start()
        peers = [(_flip(x, k >> 2 & 1), _flip(y, k >> 1 & 1), _flip(c, k & 1)) for k in range(1, N_DEV)]
        sends = [pltpu.make_async_remote_copy(
            src_ref=v_ref, dst_ref=out_ref.at[me], send_sem=send_sems.at[j], recv_sem=recv_sems.at[j],
            device_id=p, device_id_type=MESH) for j, p in enumerate(peers)]
        for cp in sends:
            cp.start()
        for j, (px, py, pc) in enumerate(peers):
            pltpu.make_async_remote_copy(
                src_ref=v_ref, dst_ref=out_ref.at[4 * px + 2 * py + pc], send_sem=send_sems.at[j],
                recv_sem=recv_sems.at[j], device_id=(px, py, pc), device_id_type=MESH).wait_recv()
        for cp in sends:
            cp.wait_send()
        mine.wait()

    vm = pl.BlockSpec(memory_space=pltpu.VMEM)
    return pl.pallas_call(
        body, name=name, out_shape=jax.ShapeDtypeStruct((N_DEV,) + v.shape, v.dtype),
        in_specs=[vm], out_specs=vm,
        scratch_shapes=[pltpu.SemaphoreType.DMA((N_DEV - 1,)), pltpu.SemaphoreType.DMA((N_DEV - 1,)),
                        pltpu.SemaphoreType.DMA],
        compiler_params=_params(),
    )(v)


def _gather_mod(v, w_ada, b_piece):
    ncols = w_ada.shape[1]

    def body(v_ref, w_ref, b_ref, rows_ref, mods_ref, piece, send_sems, recv_sems, local_sems):
        x, y, c = _me()
        me = 4 * x + 2 * y + c
        peers = _peers(x, y, c)

        def exchange(src, dst, rnd):
            mine = pltpu.make_async_copy(src, dst.at[me], local_sems.at[rnd])
            mine.start()
            sends = [pltpu.make_async_remote_copy(
                src_ref=src, dst_ref=dst.at[me], send_sem=send_sems.at[rnd, j], recv_sem=recv_sems.at[rnd, j],
                device_id=p, device_id_type=MESH) for j, p in enumerate(peers)]
            for cp in sends:
                cp.start()
            for j, (px, py, pc) in enumerate(peers):
                pltpu.make_async_remote_copy(
                    src_ref=src, dst_ref=dst.at[4 * px + 2 * py + pc], send_sem=send_sems.at[rnd, j],
                    recv_sem=recv_sems.at[rnd, j], device_id=(px, py, pc), device_id_type=MESH).wait_recv()
            for cp in sends:
                cp.wait_send()
            mine.wait()

        exchange(v_ref, rows_ref, 0)
        c_all = rows_ref[:, 0, :D_MODEL]
        piece[...] = _dot(_bf(_silu(c_all)), _bf(w_ref[...])) + b_ref[...]
        exchange(piece, mods_ref, 1)

    vm = pl.BlockSpec(memory_space=pltpu.VMEM)
    return pl.pallas_call(
        body, name="gather_mod",
        out_shape=(jax.ShapeDtypeStruct((N_DEV,) + v.shape, F32), jax.ShapeDtypeStruct((N_DEV, N_DEV, ncols), F32)),
        in_specs=[vm, vm, vm], out_specs=(vm, vm),
        scratch_shapes=[pltpu.VMEM((N_DEV, ncols), F32), pltpu.SemaphoreType.DMA((2, N_DEV - 1)),
                        pltpu.SemaphoreType.DMA((2, N_DEV - 1)), pltpu.SemaphoreType.DMA((2,))],
        compiler_params=_params(),
    )(v, w_ada, b_piece)


def _ag_relayed(v, name, chunks=1):
    rows = v.shape[0] // chunks
    assert rows * chunks == v.shape[0] and rows % 8 == 0

    def body(v_ref, out_ref, token, send_sems, recv_sems, local_sem):
        token[...] = jnp.zeros_like(token)
        x, y, c = _me()
        flip_x, flip_y = 1 - x, 1 - y
        ax, ay = c * x + (1 - c) * flip_x, c * flip_y + (1 - c) * y
        bx, by = c * flip_x + (1 - c) * x, c * y + (1 - c) * flip_y
        me, sib = (x, y, c), (x, y, 1 - c)
        a, b, dg = (ax, ay, c), (bx, by, c), (flip_x, flip_y, c)
        sa, sb, sdg = (bx, by, 1 - c), (ax, ay, 1 - c), (flip_x, flip_y, 1 - c)

        def piece(ref, k):
            return ref.at[pl.ds(k * rows, rows), :]

        def slot(px, py, pc):
            return out_ref.at[4 * px + 2 * py + pc]

        def copy(n, k, block, to, src=None):
            return pltpu.make_async_remote_copy(
                src_ref=piece(slot(*block) if src is None else src, k), dst_ref=piece(slot(*block), k),
                send_sem=send_sems.at[n * chunks + k], recv_sem=recv_sems.at[n * chunks + k],
                device_id=to, device_id_type=MESH)

        mine = pltpu.make_async_copy(v_ref, slot(*me), local_sem)
        mine.start()
        started = [copy(n, k, me, to, src=v_ref) for k in range(chunks) for n, to in ((1, a), (2, b), (0, sib))]
        for cp in started:
            cp.start()

        def arrived(n, k, block, then):
            copy(n, k, block, me).wait_recv()
            for n2, to in then:
                started.append(copy(n2, k, block, to))
                started[-1].start()

        for k in range(chunks):
            arrived(1, k, a, ((3, b), (4, sib)))
            arrived(2, k, b, ((5, sib),))
        for k in range(chunks):
            arrived(3, k, dg, ((6, sib),))
        for k in range(chunks):
            for n, block in ((0, sib), (4, sa), (5, sb), (6, sdg)):
                copy(n, k, block, me).wait_recv()
        for cp in started:
            cp.wait_send()
        mine.wait()

    out, token = pl.pallas_call(
        body, name=name,
        out_shape=(jax.ShapeDtypeStruct((N_DEV,) + v.shape, v.dtype), jax.ShapeDtypeStruct((8, 128), v.dtype)),
        in_specs=[ANY], out_specs=(ANY, pl.BlockSpec(memory_space=pltpu.VMEM)),
        scratch_shapes=[pltpu.SemaphoreType.DMA((7 * chunks,)), pltpu.SemaphoreType.DMA((7 * chunks,)),
                        pltpu.SemaphoreType.DMA],
        compiler_params=_params(),
    )(v)
    return out, token[0:1, 0:1]


HBM = pl.BlockSpec(memory_space=pltpu.HBM)
SEM = pl.BlockSpec(memory_space=pltpu.SEMAPHORE)
EFFECT = pltpu.SideEffectType.DATAFLOW_SIDE_EFFECTING


def _peers(x, y, c):
    return [(_flip(x, k >> 2 & 1), _flip(y, k >> 1 & 1), _flip(c, k & 1)) for k in range(1, N_DEV)]


def _exchange_start(src, land, gather, name):
    def body(src_ref, land_ref, send_sems, recv_sems, src_thru, land_thru, token):
        x, y, c = _me()
        me = 4 * x + 2 * y + c
        for j, (px, py, pc) in enumerate(_peers(x, y, c)):
            pltpu.make_async_remote_copy(
                src_ref=src_ref if gather else src_ref.at[4 * px + 2 * py + pc], dst_ref=land_ref.at[me],
                send_sem=send_sems.at[j], recv_sem=recv_sems.at[j], device_id=(px, py, pc), device_id_type=MESH).start()
        token[...] = jnp.zeros_like(token)

    sems = pltpu.SemaphoreType.DMA((N_DEV - 1,))
    out = pl.pallas_call(
        body, name=name,
        out_shape=(sems, sems, pltpu.HBM(src.shape, src.dtype), pltpu.HBM(land.shape, land.dtype),
                   jax.ShapeDtypeStruct((8, 128), F32)),
        in_specs=(HBM, HBM), out_specs=(SEM, SEM, HBM, HBM, pl.BlockSpec(memory_space=pltpu.VMEM)),
        input_output_aliases={0: 2, 1: 3},
        compiler_params=pltpu.CompilerParams(has_side_effects=EFFECT),
    )(pltpu.with_memory_space_constraint(src, pltpu.HBM), pltpu.with_memory_space_constraint(land, pltpu.HBM))
    return out[:4], out[4][0, 0]


def _exchange_wait(started, after, gather, name):
    send_sems, recv_sems, src_thru, land_thru = started

    def body(src_ref, land_ref, send_sems, recv_sems, after_ref, src_dead, got_ref):
        x, y, c = _me()
        for j, (px, py, pc) in enumerate(_peers(x, y, c)):
            pid = 4 * px + 2 * py + pc
            cp = pltpu.make_async_remote_copy(
                src_ref=src_ref if gather else src_ref.at[pid], dst_ref=land_ref.at[pid],
                send_sem=send_sems.at[j], recv_sem=recv_sems.at[j], device_id=(px, py, pc), device_id_type=MESH)
            cp.wait_send()
            cp.wait_recv()

    return pl.pallas_call(
        body, name=name,
        out_shape=(pltpu.HBM(src_thru.shape, src_thru.dtype), pltpu.HBM(land_thru.shape, land_thru.dtype)),
        in_specs=(HBM, HBM, SEM, SEM, ANY), out_specs=(HBM, HBM), input_output_aliases={0: 0, 1: 1},
        compiler_params=pltpu.CompilerParams(has_side_effects=EFFECT),
    )(src_thru, land_thru, send_sems, recv_sems, after)[1]


def _silu(a):
    return a * _sig(a)


def _gw_ada(c_all, dmod_piece):
    def body(c_ref, d_ref, o_ref):
        o_ref[...] = _dot_tn(_bf(_silu(c_ref[...])), _bf(d_ref[...]))

    return pl.pallas_call(
        body, name="gw_ada", out_shape=jax.ShapeDtypeStruct((c_all.shape[1], dmod_piece.shape[1]), F32),
        compiler_params=_params(),
    )(c_all, dmod_piece)


def _adam(parts, w, m, v, name):
    k, r, n = parts.shape
    if r <= 256 or r % 256 == 0:
        tr, tn = min(r, 256), n
    else:
        tr, tn = r, 256
    assert r % tr == 0 and n % tn == 0

    def body(p_ref, w_ref, m_ref, v_ref, g_ref, d_ref, nm_ref, nv_ref):
        g = p_ref[0].astype(F32)
        for j in range(1, k):
            g = g + p_ref[j].astype(F32)
        g_ref[...] = g
        d_ref[...], nm_ref[...], nv_ref[...] = _adam_math(g, w_ref[...], m_ref[...], v_ref[...])

    blk = pl.BlockSpec((tr, tn), lambda i, j: (i, j))
    return pl.pallas_call(
        body, name=name, grid=(r // tr, n // tn),
        in_specs=[pl.BlockSpec((k, tr, tn), lambda i, j: (0, i, j)), blk, blk, blk],
        out_specs=[blk, blk, blk, blk],
        out_shape=[jax.ShapeDtypeStruct((r, n), F32)] * 4,
        compiler_params=_params(dimension_semantics=("arbitrary", "arbitrary")),
    )(parts, w, m, v)


def _adam_math(g, w, m, v):
    m_new = ADAM_B1 * m + (1.0 - ADAM_B1) * g
    v_new = ADAM_B2 * v + (1.0 - ADAM_B2) * jnp.square(g)
    m_hat = m_new / (1.0 - ADAM_B1 ** ADAM_STEP)
    v_hat = v_new / (1.0 - ADAM_B2 ** ADAM_STEP)
    return -ADAM_LR * (m_hat / (jnp.sqrt(v_hat) + ADAM_EPS) + ADAM_WD * w), m_new, v_new


_SMALL = (("b_ada", 3 * D_MODEL), ("norm_w", D_MODEL), ("q_norm_w", HEAD_DIM), ("k_norm_w", HEAD_DIM),
          ("rel_bias", REL_BUCKETS * ATTN_HEADS), ("sinks", ATTN_HEADS), ("conv_b", XBC_W), ("dt_bias", SSM_HEADS),
          ("a_log", SSM_HEADS), ("d_skip", SSM_HEADS), ("ssm_norm_w", SSM_W))
_SLOT = tuple(-(-n // 128) * 128 for _, n in _SMALL)
_SLOT_OFF = tuple(int(o) for o in np.cumsum((0,) + _SLOT))
_LOSS_OFF = _SLOT_OFF[-1]
_CW_OFF = _LOSS_OFF + 128
_PACK_N = _CW_OFF + CONV_K * XBC_W


def _pack_partials(small, loss, g_conv_w):
    parts = []
    for (name, n), slot in zip(_SMALL, _SLOT):
        parts.append(small[name].reshape(1, n))
        if slot > n:
            parts.append(jnp.zeros((1, slot - n), F32))
    parts += [loss.reshape(1, 1), jnp.zeros((1, 127), F32), g_conv_w.reshape(1, CONV_K * XBC_W)]
    return jnp.concatenate(parts, axis=1)


def _adam_small(pack_all, w, m, v):
    names = [name for name, _ in _SMALL]

    def body(p_ref, *rest):
        ins, outs = rest[:3 * len(names)], rest[3 * len(names):]

        def total(off, n):
            g = p_ref[0, :, off:off + n]
            for d in range(1, N_DEV):
                g = g + p_ref[d, :, off:off + n]
            return g

        for j, (name, n) in enumerate(_SMALL):
            g = total(_SLOT_OFF[j], n)
            delta, m_new, v_new = _adam_math(g, ins[3 * j][...], ins[3 * j + 1][...], ins[3 * j + 2][...])
            outs[4 * j][...] = g
            outs[4 * j + 1][...] = delta
            outs[4 * j + 2][...] = m_new
            outs[4 * j + 3][...] = v_new
        outs[-1][...] = total(_LOSS_OFF, 1)

    flat = []
    for name, n in _SMALL:
        flat += [w[name].reshape(1, n), m[name].reshape(1, n), v[name].reshape(1, n)]
    out_shape = [jax.ShapeDtypeStruct((1, n), F32) for _, n in _SMALL for _ in range(4)] + [jax.ShapeDtypeStruct((1, 1), F32)]
    out = pl.pallas_call(body, name="adam_small", out_shape=out_shape, compiler_params=_params())(pack_all, *flat)
    res = {name: [out[4 * j + t].reshape(w[name].shape) for t in range(4)] for j, name in enumerate(names)}
    return res, out[-1]


WEIGHTS = ("w_ada", "b_ada", "norm_w", "w_in", "q_norm_w", "k_norm_w", "rel_bias", "sinks", "conv_w", "conv_b",
           "dt_bias", "a_log", "d_skip", "ssm_norm_w", "w_attn_proj", "w_ssm_proj", "w_out")


def kernel(x, c, w_ada, b_ada, norm_w, w_in, q_norm_w, k_norm_w, rel_bias, sinks, conv_w, conv_b, dt_bias, a_log, d_skip, ssm_norm_w, w_attn_proj, w_ssm_proj, w_out, loss_target, m_w_ada, m_b_ada, m_norm_w, m_w_in, m_q_norm_w, m_k_norm_w, m_rel_bias, m_sinks, m_conv_w, m_conv_b, m_dt_bias, m_a_log, m_d_skip, m_ssm_norm_w, m_w_attn_proj, m_w_ssm_proj, m_w_out, v_w_ada, v_b_ada, v_norm_w, v_w_in, v_q_norm_w, v_k_norm_w, v_rel_bias, v_sinks, v_conv_w, v_conv_b, v_dt_bias, v_a_log, v_d_skip, v_ssm_norm_w, v_w_attn_proj, v_w_ssm_proj, v_w_out):
    w = dict(w_ada=w_ada, b_ada=b_ada, norm_w=norm_w, w_in=w_in, q_norm_w=q_norm_w, k_norm_w=k_norm_w,
             rel_bias=rel_bias, sinks=sinks, conv_w=conv_w, conv_b=conv_b, dt_bias=dt_bias, a_log=a_log,
             d_skip=d_skip, ssm_norm_w=ssm_norm_w, w_attn_proj=w_attn_proj, w_ssm_proj=w_ssm_proj, w_out=w_out)
    m = dict(w_ada=m_w_ada, b_ada=m_b_ada, norm_w=m_norm_w, w_in=m_w_in, q_norm_w=m_q_norm_w, k_norm_w=m_k_norm_w,
             rel_bias=m_rel_bias, sinks=m_sinks, conv_w=m_conv_w, conv_b=m_conv_b, dt_bias=m_dt_bias, a_log=m_a_log,
             d_skip=m_d_skip, ssm_norm_w=m_ssm_norm_w, w_attn_proj=m_w_attn_proj, w_ssm_proj=m_w_ssm_proj, w_out=m_w_out)
    v = dict(w_ada=v_w_ada, b_ada=v_b_ada, norm_w=v_norm_w, w_in=v_w_in, q_norm_w=v_q_norm_w, k_norm_w=v_k_norm_w,
             rel_bias=v_rel_bias, sinks=v_sinks, conv_w=v_conv_w, conv_b=v_conv_b, dt_bias=v_dt_bias, a_log=v_a_log,
             d_skip=v_d_skip, ssm_norm_w=v_ssm_norm_w, w_attn_proj=v_w_attn_proj, w_ssm_proj=v_w_ssm_proj, w_out=v_w_out)
    me = 4 * lax.axis_index("x") + 2 * lax.axis_index("y") + lax.axis_index("c")
    ada_n = w_ada.shape[2]
    in_n = w_in.shape[2]
    cw_n = conv_w.shape[2]

    b_piece = lax.dynamic_slice_in_dim(b_ada, me * ada_n, ada_n, axis=1)
    first, mod_all = _gather_mod(jnp.concatenate([c, conv_w[0].reshape(1, CONV_K * cw_n)], axis=1), w_ada[0], b_piece)
    first = first[:, 0]
    c_all = first[:, :D_MODEL]
    conv_w_full = first[:, D_MODEL:].reshape(N_DEV, CONV_K, cw_n).transpose(1, 0, 2).reshape(CONV_K, XBC_W)
    mod = lax.dynamic_index_in_dim(mod_all, me, axis=1, keepdims=False).reshape(1, 3 * D_MODEL)
    shift, scale, gate = mod[:, :D_MODEL], mod[:, D_MODEL:2 * D_MODEL], mod[:, 2 * D_MODEL:]

    pad = -in_n % 24
    w_t, zero = _ag_relayed(jnp.pad(w_in[0].T.astype(BF), ((0, pad), (0, 0))), "ag_w_in", chunks=3)
    w_t = w_t[:, :in_n].reshape(N_DEV * in_n, D_MODEL)

    def with_mine(blocks, mine):
        return lax.dynamic_update_index_in_dim(lax.empty(blocks, mine.dtype), mine, me, axis=0)

    rows = jnp.concatenate([w_attn_proj[0], w_ssm_proj[0], w_out[0]], axis=0).astype(BF) + zero
    r_ap, r_sp = w_attn_proj.shape[1], w_ssm_proj.shape[1]
    rows_started, zero = _exchange_start(rows, with_mine((N_DEV,) + rows.shape, rows), True, "ag_rows_start")

    def rows_fn(after):
        return _exchange_wait(rows_started, after, True, "ag_rows_wait")

    started = {}

    def send_blocks(key, g, name):
        started[key], zero = _exchange_start(
            g, with_mine(g.shape, lax.dynamic_index_in_dim(g, me, axis=0, keepdims=False)), False, name)
        return zero

    def after_mid(g_wap, g_wsp, g_wout):
        return send_blocks("rows", jnp.concatenate(
            [g_wap.reshape(N_DEV, r_ap, D_MODEL), g_wsp.reshape(N_DEV, r_sp, D_MODEL),
             g_wout.reshape(N_DEV, r_ap, D_MODEL)], axis=1), "rs_rows_start")

    def after_gw(g_ws):
        return send_blocks("in", jnp.concatenate(g_ws, axis=0).reshape(N_DEV, in_n, D_MODEL), "rs_in_start")

    r = _local_step(x[0], loss_target[0], shift, scale + zero, gate, w_t, rows_fn, norm_w, q_norm_w, k_norm_w,
                    rel_bias, sinks, conv_w_full, conv_b, dt_bias, a_log, d_skip, ssm_norm_w, after_mid, after_gw)

    small = dict(b_ada=r["dmod"], norm_w=r["g_norm_w"], q_norm_w=r["g_qnw"], k_norm_w=r["g_knw"], rel_bias=r["g_rel"],
                 sinks=r["g_sinks"], conv_b=r["g_conv_b"], dt_bias=r["g_dt_bias"], a_log=r["g_a_log"],
                 d_skip=r["g_d_skip"], ssm_norm_w=r["g_ssm_nw"])
    pack_all = _ag_direct(_pack_partials(small, r["loss"], r["g_conv_w"]), "ag_small")
    res, loss = _adam_small(pack_all, w, m, v)
    loss = loss[0, 0]
    cw_parts = pack_all[:, 0, _CW_OFF:].reshape(N_DEV, CONV_K, XBC_W)
    cw_mine = lax.dynamic_slice_in_dim(cw_parts, me * cw_n, cw_n, axis=2)
    res["conv_w"] = [a[None] for a in _adam(cw_mine, conv_w[0], m_conv_w[0], v_conv_w[0], "adam_conv_w")]

    dmod_piece = lax.dynamic_slice_in_dim(pack_all[:, 0, :3 * D_MODEL], me * ada_n, ada_n, axis=1)
    g_ada = _gw_ada(c_all, dmod_piece)
    res["w_ada"] = [a[None] for a in _adam(g_ada[None], w_ada[0], m_w_ada[0], v_w_ada[0], "adam_w_ada")]

    cat = lambda d: jnp.concatenate([d["w_attn_proj"][0], d["w_ssm_proj"][0], d["w_out"][0]], axis=0)
    rows_res = _adam(_exchange_wait(started["rows"], g_ada, False, "rs_rows_wait"), cat(w), cat(m), cat(v), "adam_w_rows")
    res["w_in"] = [a.T[None] for a in _adam(_exchange_wait(started["in"], rows_res[0], False, "rs_in_wait"),
                                            w_in[0].T, m_w_in[0].T, v_w_in[0].T, "adam_w_in")]
    res["w_attn_proj"] = [a[None, :r_ap] for a in rows_res]
    res["w_ssm_proj"] = [a[None, r_ap:r_ap + r_sp] for a in rows_res]
    res["w_out"] = [a[None, r_ap + r_sp:] for a in rows_res]

    outs = [loss, r["grad_x"][None]]
    for j in range(4):
        outs += [res[name][j] for name in WEIGHTS]
    return tuple(outs)
```

```python
import math

import numpy as np
import jax
import jax.numpy as jnp
from jax import lax
from jax.experimental import pallas as pl
from jax.experimental.pallas import tpu as pltpu

F32 = jnp.float32
BF = jnp.bfloat16
HI = lax.Precision.HIGHEST

D_MODEL = 1024
ATTN_HEADS = 16
KV_HEADS = 4
GRP = ATTN_HEADS // KV_HEADS
HEAD_DIM = 64
ATTN_W = ATTN_HEADS * HEAD_DIM
KV_W = KV_HEADS * HEAD_DIM
BLOCK = 128
REL_BUCKETS = 32
REL_MAX_DIST = 128
SSM_W = 2048
SSM_P = 64
SSM_HEADS = 32
SSM_G = 4
SSM_R = 8
SSM_N = 128
CONV_K = 4
XBC_W = SSM_W + 2 * SSM_G * SSM_N
SEG_W = (ATTN_W, 2 * KV_W, ATTN_W + SSM_W, XBC_W, SSM_HEADS, 2 * D_MODEL)
NSEG = len(SEG_W)
SEG_OFF = tuple(int(v) for v in np.cumsum((0,) + SEG_W))
IN_W = SEG_OFF[-1]
GATE_SEGS = (2, 5)
EPS = 1e-6
N_DEV = 8
ADAM_LR, ADAM_B1, ADAM_B2, ADAM_EPS, ADAM_WD, ADAM_STEP = 0.001, 0.9, 0.999, 1e-08, 0.01, 10
VMEM_LIMIT = 60 * 1024 * 1024
MESH = pl.DeviceIdType.MESH
ANY = pl.BlockSpec(memory_space=pl.ANY)


def _dot(a, b, precision=None):
    return jnp.dot(a, b, preferred_element_type=F32, precision=precision)


def _dot_nt(a, b, precision=None):
    return lax.dot_general(a, b, (((1,), (1,)), ((), ())), preferred_element_type=F32, precision=precision)


def _dot_tn(a, b, precision=None):
    return lax.dot_general(a, b, (((0,), (0,)), ((), ())), preferred_element_type=F32, precision=precision)


def _bf(a):
    return a.astype(BF)


def _sig(a):
    return 0.5 * jnp.tanh(0.5 * a) + 0.5


def _params(**kw):
    return pltpu.CompilerParams(vmem_limit_bytes=VMEM_LIMIT, **kw)


def _full(shape):
    nd = len(shape)
    return pl.BlockSpec(shape, lambda i: (0,) * nd)


def _rows(tm, w):
    return pl.BlockSpec((tm, w), lambda i: (i, 0))


def _inproj(x, norm_w, scale, shift, w_t, tm=256):
    s = x.shape[0]

    def body(x_ref, nw_ref, sc_ref, sh_ref, w_hbm, *rest):
        outs, h_ref, w_vm, sem = rest[:NSEG], rest[NSEG], rest[NSEG + 1], rest[NSEG + 2]
        first = pl.program_id(0) == 0
        cps = [pltpu.make_async_copy(w_hbm.at[SEG_OFF[j]:SEG_OFF[j + 1], :], w_vm.at[SEG_OFF[j]:SEG_OFF[j + 1], :], sem.at[j])
               for j in range(NSEG)]

        def tile(waiting):
            xv = x_ref[...]
            r = lax.rsqrt(jnp.mean(xv * xv, axis=-1, keepdims=True) + EPS)
            h = xv * r * (nw_ref[...] * (1.0 + sc_ref[...])) + sh_ref[...]
            hb = _bf(h)
            h_ref[...] = hb
            for j in range(NSEG):
                if waiting:
                    cps[j].wait()
                outs[j][...] = _dot_nt(hb, w_vm[SEG_OFF[j]:SEG_OFF[j + 1], :]).astype(outs[j].dtype)

        @pl.when(first)
        def _():
            for cp in cps:
                cp.start(priority=1)
            tile(True)

        @pl.when(jnp.logical_not(first))
        def _():
            tile(False)

    vec = _full((1, D_MODEL))
    return pl.pallas_call(
        body, name="inproj", grid=(s // tm,),
        in_specs=[_rows(tm, D_MODEL), vec, vec, vec, ANY],
        out_specs=[_rows(tm, w) for w in SEG_W] + [_rows(tm, D_MODEL)],
        out_shape=[jax.ShapeDtypeStruct((s, w), BF if j in GATE_SEGS else F32) for j, w in enumerate(SEG_W)]
                  + [jax.ShapeDtypeStruct((s, D_MODEL), BF)],
        scratch_shapes=[pltpu.VMEM((IN_W, D_MODEL), BF), pltpu.SemaphoreType.DMA((NSEG,))],
        compiler_params=_params(dimension_semantics=("arbitrary",)),
    )(x, norm_w, scale, shift, w_t)


def _bucket_onehot_t():
    qi = jnp.arange(BLOCK)[:, None]
    kj = jnp.arange(2 * BLOCK)[None, :]
    dist = qi + BLOCK - kj
    n = jnp.maximum(dist, 0)
    max_exact = REL_BUCKETS // 2
    nf = jnp.maximum(n, 1).astype(F32)
    large = max_exact + (jnp.log(nf / max_exact) / math.log(REL_MAX_DIST / max_exact)
                         * (REL_BUCKETS - max_exact)).astype(jnp.int32)
    large = jnp.minimum(large, REL_BUCKETS - 1)
    bucket = jnp.where(n < max_exact, n, large).reshape(1, BLOCK * 2 * BLOCK)
    return (bucket == jnp.arange(REL_BUCKETS)[:, None]).astype(F32)


def _bias_dense(rel_bias_t, oh_t):
    def body(rb_ref, oh_ref, o_ref):
        o_ref[...] = _dot(rb_ref[...], oh_ref[...], HI)

    return pl.pallas_call(
        body, name="bias_dense", out_shape=jax.ShapeDtypeStruct((ATTN_HEADS, BLOCK * 2 * BLOCK), F32),
        compiler_params=_params(),
    )(rel_bias_t, oh_t)


def _bias_grad(ds_sum, oh_t):
    def body(ds_ref, oh_ref, o_ref):
        o_ref[...] = _dot_nt(ds_ref[...], oh_ref[...], HI)

    return pl.pallas_call(
        body, name="bias_grad", out_shape=jax.ShapeDtypeStruct((ATTN_HEADS, REL_BUCKETS), F32),
        compiler_params=_params(),
    )(ds_sum, oh_t)


def _group_sum(a, e):
    hi = _bf(a)
    return _dot(hi, e) + _dot(_bf(a - hi.astype(F32)), e)


def _group_bcast(a, e3t):
    hi = _bf(a)
    r1 = a - hi.astype(F32)
    mid = _bf(r1)
    return _dot(jnp.concatenate([hi, mid, _bf(r1 - mid.astype(F32))], axis=1), e3t)


def _membership(width, group, ngroups):
    e = (jnp.arange(width)[:, None] // group == jnp.arange(ngroups)[None, :]).astype(BF)
    return e, jnp.tile(e.T, (3, 1))


def _fold(width, group):
    return (jnp.arange(width)[:, None] % group == jnp.arange(group)[None, :]).astype(BF)


def _heads_norm(t, w_x, e, e3t):
    r = lax.rsqrt(_dot(_bf(t * t), e) * (1.0 / HEAD_DIM) + EPS)
    r_x = _group_bcast(r, e3t)
    return t * r_x * w_x, r_x


def _heads_norm_bwd(t, r_x, w_x, d, e, e3t):
    wd = d * w_x
    corr = _group_bcast(_dot(_bf(t * wd), e) * (1.0 / HEAD_DIM), e3t)
    return r_x * wd - t * (r_x * r_x * r_x) * corr, jnp.sum(d * t * r_x, axis=0, keepdims=True)


def _stack_heads(a, hk):
    return jnp.concatenate([a[:, (hk * GRP + g) * HEAD_DIM:(hk * GRP + g + 1) * HEAD_DIM] for g in range(GRP)], axis=0)


def _stack_cols(a, hk):
    return jnp.concatenate([a[:, hk * GRP + g:hk * GRP + g + 1] for g in range(GRP)], axis=0)


def _masked_bias(bias):
    qi = jnp.arange(BLOCK)[:, None]
    kj = jnp.arange(2 * BLOCK)[None, :]
    cur_ok = jnp.logical_and(kj >= BLOCK, kj - BLOCK <= qi)
    both_ok = jnp.logical_or(jnp.logical_and(kj < BLOCK, kj > qi), cur_ok)
    return jnp.stack([jnp.where(cur_ok, bias, -1e30), jnp.where(both_ok, bias, -1e30)])


def _attn_consts(qnw, knw):
    eq, eq3t = _membership(ATTN_W, HEAD_DIM, ATTN_HEADS)
    ek, ek3t = _membership(KV_W, HEAD_DIM, ATTN_HEADS)
    return (jnp.tile(qnw, (1, ATTN_HEADS)), jnp.tile(knw, (1, KV_HEADS)), eq, eq3t, ek, ek3t)


def _attn_fwd(q, kv, bias, sinks, consts):
    s = q.shape[0]
    nb = s // BLOCK
    gq = GRP * BLOCK
    bias_t = bias.reshape(2, KV_HEADS, GRP, BLOCK, 2 * BLOCK).transpose(0, 1, 4, 2, 3).reshape(2, KV_HEADS, 2 * BLOCK, gq)
    sink_rows = jnp.repeat(sinks.reshape(KV_HEADS, GRP), BLOCK, axis=1).reshape(KV_HEADS, 1, gq)
    eye = jnp.eye(BLOCK, dtype=BF)

    def body(q_ref, kp_ref, kc_ref, vp_ref, vc_ref, b_ref, bt_ref, sk_ref, skr_ref, eye_ref,
             qw_ref, kw_ref, eq_ref, eq3_ref, ek_ref, ek3_ref, o_ref, lse_ref):
        qn = _bf(_heads_norm(q_ref[...], qw_ref[...], eq_ref[...], eq3_ref[...])[0] * (HEAD_DIM ** -0.5))
        kn = _bf(_heads_norm(jnp.concatenate([kp_ref[...], kc_ref[...]], axis=0), kw_ref[...], ek_ref[...], ek3_ref[...])[0])
        vv = _bf(jnp.concatenate([vp_ref[...], vc_ref[...]], axis=0))
        ones = jnp.ones((2 * BLOCK, HEAD_DIM), BF)
        kss = [slice(hk * HEAD_DIM, (hk + 1) * HEAD_DIM) for hk in range(KV_HEADS)]
        qgs = [_stack_heads(qn, hk) for hk in range(KV_HEADS)]
        sc_ts = [_dot_nt(kn[:, kss[hk]], qgs[hk]) + bt_ref[0, hk] for hk in range(KV_HEADS)]
        m_rows = [jnp.maximum(jnp.max(sc_ts[hk], axis=0, keepdims=True), skr_ref[hk]) for hk in range(KV_HEADS)]
        m_hq = _bf(jnp.concatenate([(m + jnp.abs(m) * (2.0 ** -7))[:, g * BLOCK:(g + 1) * BLOCK]
                                    for m in m_rows for g in range(GRP)], axis=0))
        m16 = _dot_nt(eye_ref[...], m_hq)
        ms = [_stack_cols(m16, hk) for hk in range(KV_HEADS)]
        scs = [_dot_nt(qgs[hk], kn[:, kss[hk]]) + b_ref[0, hk * GRP:(hk + 1) * GRP].reshape(gq, 2 * BLOCK)
               for hk in range(KV_HEADS)]
        ps = [_bf(jnp.exp(scs[hk] - ms[hk])) for hk in range(KV_HEADS)]
        pvs = [_dot(ps[hk], jnp.concatenate([vv[:, kss[hk]], ones], axis=1)) for hk in range(KV_HEADS)]
        den16 = jnp.concatenate([pvs[hk][g * BLOCK:(g + 1) * BLOCK, HEAD_DIM:HEAD_DIM + 1]
                                 for hk in range(KV_HEADS) for g in range(GRP)], axis=1)
        den16 = den16 + jnp.exp(sk_ref[...] - m16)
        lse_ref[...] = m16 + jnp.log(den16)
        inv16 = 1.0 / den16
        for hk in range(KV_HEADS):
            for g in range(GRP):
                h = hk * GRP + g
                o_ref[:, h * HEAD_DIM:(h + 1) * HEAD_DIM] = (pvs[hk][g * BLOCK:(g + 1) * BLOCK, :HEAD_DIM]
                                                             * inv16[:, h:h + 1])

    cur = lambda w, col=0: pl.BlockSpec((BLOCK, w), lambda i: (i, col))
    prev = lambda w, col=0: pl.BlockSpec((BLOCK, w), lambda i: (jnp.maximum(i - 1, 0), col))
    whole = lambda a: pl.BlockSpec(a.shape, lambda i: (0,) * a.ndim)
    first_or_not = lambda a: pl.BlockSpec((1,) + a.shape[1:], lambda i: (jnp.minimum(i, 1),) + (0,) * (a.ndim - 1))
    return pl.pallas_call(
        body, name="attn_fwd", grid=(nb,),
        in_specs=[cur(ATTN_W), prev(KV_W, 0), cur(KV_W, 0), prev(KV_W, 1), cur(KV_W, 1),
                  first_or_not(bias), first_or_not(bias_t),
                  whole(sinks), whole(sink_rows), whole(eye)] + [_full(c.shape) for c in consts],
        out_specs=[cur(ATTN_W), cur(ATTN_HEADS)],
        out_shape=[jax.ShapeDtypeStruct((s, ATTN_W), F32), jax.ShapeDtypeStruct((s, ATTN_HEADS), F32)],
        compiler_params=_params(dimension_semantics=("arbitrary",)),
    )(q, kv, kv, kv, kv, bias, bias_t, sinks, sink_rows, eye, *consts)


def _conv_taps(xbc, tail):
    ext = jnp.concatenate([tail, xbc], axis=0)
    return [pltpu.roll(ext, CONV_K - 1 - j, axis=0)[8:8 + BLOCK] if j < CONV_K - 1 else xbc for j in range(CONV_K)]


def _softplus(u):
    return jnp.maximum(u, 0.0) + jnp.log(1.0 + jnp.exp(-jnp.abs(u)))


def _tril():
    r = lax.broadcasted_iota(jnp.int32, (BLOCK, BLOCK), 0)
    c = lax.broadcasted_iota(jnp.int32, (BLOCK, BLOCK), 1)
    return r >= c


def _triu():
    r = lax.broadcasted_iota(jnp.int32, (BLOCK, BLOCK), 0)
    c = lax.broadcasted_iota(jnp.int32, (BLOCK, BLOCK), 1)
    return r <= c


def _exact_left(m01, a):
    hi = _bf(a)
    r1 = a - hi.astype(F32)
    mid = _bf(r1)
    return _dot(m01, hi) + _dot(m01, mid) + _dot(m01, _bf(r1 - mid.astype(F32)))


def _ssd_common(conv, dtr, dtb_ref, alog_ref, e3_ref):
    sg = _sig(conv)
    xact = conv * sg
    u = dtr + dtb_ref[...]
    dt = _softplus(u)
    a = -jnp.exp(alog_ref[...])
    trilb = _tril()
    acum = _exact_left(trilb.astype(BF), dt * a) * math.log2(math.e)
    both = _group_bcast(jnp.concatenate([dt, acum], axis=0), e3_ref[...])
    dt_x, acum_x = both[:BLOCK], both[BLOCK:]
    return sg, xact, u, dt, a, trilb, acum, dt_x, acum_x


SSD_CH = 2


def _ssd_fwd(xbc, dt_raw, conv_w, conv_b, dt_bias, a_log, dsk_x, e3t):
    s = xbc.shape[0]
    nc = s // BLOCK
    ch = SSD_CH if nc % SSD_CH == 0 else 1
    rows = ch * BLOCK

    def body(x_ref, tail_ref, dtr_ref, cw_ref, cb_ref, dtb_ref, alog_ref, dsk_ref, e3_ref,
             y_ref, hp_ref, conv_ref, hst, yd_s, yoff_s):
        i = pl.program_id(0)

        @pl.when(i == 0)
        def _():
            hst[...] = jnp.zeros_like(hst)

        for j in range(ch):
            rs = slice(j * BLOCK, (j + 1) * BLOCK)
            tail = jnp.where(i > 0, tail_ref[...], 0.0) if j == 0 else x_ref[j * BLOCK - 8:j * BLOCK, :]
            taps = _conv_taps(x_ref[rs, :], tail)
            conv = cb_ref[...] + sum(taps[t] * cw_ref[t:t + 1, :] for t in range(CONV_K))
            conv_ref[rs, :] = conv
            _, xact, _, _, _, trilb, acum, dt_x, acum_x = _ssd_common(conv, dtr_ref[rs, :], dtb_ref, alog_ref, e3_ref)
            xs = xact[:, :SSM_W]
            acum_t = acum.T
            ea_x = jnp.exp2(acum_x)
            last_x = acum_x[BLOCK - 1:BLOCK, :]
            xdt = xs * dt_x
            xw = xdt * jnp.exp2(last_x - acum_x)
            cd_x = jnp.exp2(last_x)
            hprev = hst[...]
            hp_ref[j] = hprev
            sls = [slice(g * SSM_R * SSM_P, (g + 1) * SSM_R * SSM_P) for g in range(SSM_G)]
            bgs = [_bf(xact[:, SSM_W + g * SSM_N:SSM_W + (g + 1) * SSM_N]) for g in range(SSM_G)]
            cgs = [_bf(xact[:, SSM_W + SSM_G * SSM_N + g * SSM_N:SSM_W + SSM_G * SSM_N + (g + 1) * SSM_N])
                   for g in range(SSM_G)]
            xdt_b, xw_b, hprev_b = _bf(xdt), _bf(xw), _bf(hprev)
            low_half = lax.broadcasted_iota(jnp.int32, (BLOCK, 2 * SSM_P), 1) < SSM_P
            cbs = [_dot_nt(cgs[g], bgs[g]) for g in range(SSM_G)]
            for g in range(SSM_G):
                sl = sls[g]
                yoff_s[:, sl] = _dot(cgs[g], hprev_b[:, sl]) * ea_x[:, sl]
                hst[:, sl] = hprev[:, sl] * cd_x[:, sl] + _dot_tn(bgs[g], xw_b[:, sl])
            for g in range(SSM_G):
                hss = [slice((g * SSM_R + r) * SSM_P, (g * SSM_R + r + 1) * SSM_P) for r in range(SSM_R)]
                mms = [_bf(cbs[g] * jnp.exp2(jnp.where(trilb, acum[:, g * SSM_R + r:g * SSM_R + r + 1]
                                                      - acum_t[g * SSM_R + r:g * SSM_R + r + 1, :], -1e30)))
                       for r in range(SSM_R)]
                for r in range(0, SSM_R, 2):
                    pair = slice(hss[r].start, hss[r + 1].stop)
                    xp = xdt_b[:, pair]
                    rhs = jnp.concatenate([jnp.where(low_half, xp, 0), jnp.where(low_half, 0, xp)], axis=0)
                    yd_s[:, pair] = _dot(jnp.concatenate([mms[r], mms[r + 1]], axis=1), rhs)
            y_ref[rs, :] = yd_s[...] + yoff_s[...] + dsk_ref[...] * xs

    blk = lambda w: pl.BlockSpec((rows, w), lambda i: (i, 0))
    return pl.pallas_call(
        body, name="ssd_fwd", grid=(nc // ch,),
        in_specs=[blk(XBC_W), pl.BlockSpec((8, XBC_W), lambda i: (jnp.maximum(i * (rows // 8) - 1, 0), 0)),
                  blk(SSM_HEADS), _full((CONV_K, XBC_W)), _full((1, XBC_W)), _full((1, SSM_HEADS)),
                  _full((1, SSM_HEADS)), _full((1, SSM_W)), _full((3 * SSM_HEADS, SSM_W))],
        out_specs=[blk(SSM_W), pl.BlockSpec((ch, SSM_N, SSM_W), lambda i: (i, 0, 0)), blk(XBC_W)],
        out_shape=[jax.ShapeDtypeStruct((s, SSM_W), F32), jax.ShapeDtypeStruct((nc, SSM_N, SSM_W), F32),
                   jax.ShapeDtypeStruct((s, XBC_W), F32)],
        scratch_shapes=[pltpu.VMEM((SSM_N, SSM_W), F32), pltpu.VMEM((BLOCK, SSM_W), F32), pltpu.VMEM((BLOCK, SSM_W), F32)],
        compiler_params=_params(dimension_semantics=("arbitrary",)),
    )(xbc, xbc, dt_raw, conv_w, conv_b, dt_bias, a_log, dsk_x, e3t)


def _dsilu(z, sg, silu):
    return sg * (1.0 + (z - silu))


def _mid(x, tgt, o_att, zam, ypre, gab, gate, ssm_nw, rows_all, tm=256):
    s = x.shape[0]
    gw = SSM_W // SSM_G

    r_ap, r_sp = ATTN_W // N_DEV, SSM_W // N_DEV

    def body(x_ref, t_ref, o_ref, zam_ref, yp_ref, gab_ref, gate_ref, nw_ref, rows_h,
             dout_ref, do_ref, dzam_ref, dyp_ref, dgab_ref,
             yag_ref, dya_ref, yn_ref, dyb_ref, mg_ref, dob_ref, gnw_ref, dgate_ref, loss_ref,
             wap_v, wsp_v, wout_v, sem):
        i = pl.program_id(0)

        @pl.when(i == 0)
        def _():
            cps = []
            for d in range(N_DEV):
                for j, (dst, r0, rn) in enumerate(((wap_v, 0, r_ap), (wsp_v, r_ap, r_sp), (wout_v, r_ap + r_sp, r_ap))):
                    cps.append(pltpu.make_async_copy(rows_h.at[d, r0:r0 + rn, :], dst.at[d * rn:(d + 1) * rn, :], sem.at[j]))
            for cp in cps:
                cp.start(priority=1)
            gnw_ref[...] = jnp.zeros_like(gnw_ref)
            dgate_ref[...] = jnp.zeros_like(dgate_ref)
            loss_ref[...] = jnp.zeros_like(loss_ref)
            for cp in cps:
                cp.wait()

        gate = gate_ref[...]
        nw = nw_ref[...]
        o_att = o_ref[...]
        z_a = zam_ref[:, :ATTN_W].astype(F32)
        s_a = _sig(z_a)
        silu_a = z_a * s_a
        yag = _bf(o_att * silu_a)
        yag_ref[...] = yag
        ypre = yp_ref[...]
        z_m = zam_ref[:, ATTN_W:].astype(F32)
        s_m = _sig(z_m)
        silu_m = z_m * s_m
        yg = ypre * silu_m
        rinv = jnp.concatenate(
            [jnp.broadcast_to(lax.rsqrt(jnp.mean(yg[:, g * gw:(g + 1) * gw] ** 2, axis=-1, keepdims=True) + EPS), (tm, gw))
             for g in range(SSM_G)], axis=1)
        ynr = yg * rinv
        yn = _bf(ynr * nw)
        yn_ref[...] = yn
        y_a = _dot(yag, wap_v[...])
        y_b = _dot(yn, wsp_v[...])
        g_a = _sig(gab_ref[:, :D_MODEL].astype(F32))
        g_b = _sig(gab_ref[:, D_MODEL:].astype(F32))
        merged = _bf(g_a * y_a + g_b * y_b)
        mg_ref[...] = merged
        o = _dot(merged, wout_v[...])
        diff = x_ref[...] + gate * o - t_ref[...]
        loss_ref[...] += (0.5 / D_MODEL) * jnp.sum(diff * diff, axis=(0, 1), keepdims=True)
        dout = diff * (1.0 / D_MODEL)
        dout_ref[...] = dout
        dgate_ref[...] += jnp.sum(dout * o, axis=0, keepdims=True)
        d_o = _bf(dout * gate)
        dob_ref[...] = d_o
        dmerged = _dot_nt(d_o, wout_v[...])
        dy_af = dmerged * g_a
        dy_bf = dmerged * g_b
        dy_a = _bf(dy_af)
        dy_b = _bf(dy_bf)
        dya_ref[...] = dy_a
        dyb_ref[...] = dy_b
        dyag = _dot_nt(dy_a, wap_v[...])
        dyn = _dot_nt(dy_b, wsp_v[...])
        dgab_ref[:, :D_MODEL] = _bf(dy_af * y_a * (1.0 - g_a))
        dgab_ref[:, D_MODEL:] = _bf(dy_bf * y_b * (1.0 - g_b))
        do_ref[...] = dyag * silu_a
        dzam_ref[:, :ATTN_W] = _bf(dyag * o_att * _dsilu(z_a, s_a, silu_a))
        gnw_ref[...] += jnp.sum(dyn * ynr, axis=0, keepdims=True)
        dynw = dyn * nw
        corr = jnp.concatenate(
            [jnp.broadcast_to(jnp.mean((dynw * ynr)[:, g * gw:(g + 1) * gw], axis=-1, keepdims=True), (tm, gw))
             for g in range(SSM_G)], axis=1)
        dyg = rinv * (dynw - ynr * corr)
        dyp_ref[...] = dyg * silu_m
        dzam_ref[:, ATTN_W:] = _bf(dyg * ypre * _dsilu(z_m, s_m, silu_m))

    r1, r2, r3 = _rows(tm, D_MODEL), _rows(tm, SSM_W), _rows(tm, ATTN_W + SSM_W)
    sd = jax.ShapeDtypeStruct
    return pl.pallas_call(
        body, name="mid", grid=(s // tm,),
        in_specs=[r1, r1, r1, r3, r2, r2, _full((1, D_MODEL)), _full((1, SSM_W)), ANY],
        out_specs=[r1, r1, r3, r2, r2, r1, r1, r2, r1, r1, r1,
                   _full((1, SSM_W)), _full((1, D_MODEL)), _full((1, 1))],
        out_shape=[sd((s, D_MODEL), F32), sd((s, ATTN_W), F32), sd((s, ATTN_W + SSM_W), BF), sd((s, SSM_W), F32),
                   sd((s, 2 * D_MODEL), BF),
                   sd((s, ATTN_W), BF), sd((s, D_MODEL), BF), sd((s, SSM_W), BF), sd((s, D_MODEL), BF),
                   sd((s, D_MODEL), BF), sd((s, D_MODEL), BF),
                   sd((1, SSM_W), F32), sd((1, D_MODEL), F32), sd((1, 1), F32)],
        scratch_shapes=[pltpu.VMEM((ATTN_W, D_MODEL), BF), pltpu.VMEM((SSM_W, D_MODEL), BF), pltpu.VMEM((D_MODEL, D_MODEL), BF),
                        pltpu.SemaphoreType.DMA((3,))],
        compiler_params=_params(dimension_semantics=("arbitrary",)),
    )(x, tgt, o_att, zam, ypre, gab, gate, ssm_nw, rows_all)


def _attn_bwd(q, kv, bias, sinks, consts, o_att, lse, d_o):
    s = q.shape[0]
    nb = s // BLOCK
    folds = (_fold(ATTN_W, HEAD_DIM), _fold(KV_W, HEAD_DIM))

    def body(q_ref, kp_ref, kc_ref, vp_ref, vc_ref, b_ref, skv_ref, qw_ref, kw_ref, eq_ref, eq3_ref, ek_ref, ek3_ref,
             fq_ref, fk_ref, o_ref, lse_ref, do_ref,
             dq_ref, dkv_ref, dss_ref, gqw_ref, gkw_ref, gsk_ref, ckn, cv, dqn_s, dkn_s, dv_s, gq_x, gk_x):
        i = pl.program_id(0)
        kw, ek, ek3 = kw_ref[...], ek_ref[...], ek3_ref[...]

        @pl.when(i == 0)
        def _():
            for ref in (ckn, cv, dss_ref, gq_x, gk_x, gsk_ref):
                ref[...] = jnp.zeros_like(ref)

        @pl.when(i < nb)
        def _():
            qw, eq, eq3 = qw_ref[...], eq_ref[...], eq3_ref[...]
            qf = q_ref[...]
            qnf, rq_x = _heads_norm(qf, qw, eq, eq3)
            qn = _bf(qnf * (HEAD_DIM ** -0.5))
            kf = jnp.concatenate([kp_ref[...], kc_ref[...]], axis=0)
            knf, rk_x = _heads_norm(kf, kw, ek, ek3)
            kn = _bf(knf)
            vv = _bf(jnp.concatenate([vp_ref[...], vc_ref[...]], axis=0))
            d_of = do_ref[...]
            d_ob = _bf(d_of)
            lse_all = lse_ref[...]
            delta = _dot(_bf(d_of * o_ref[...]), eq)
            gsk_ref[...] += jnp.sum(-jnp.exp(skv_ref[...] - lse_all) * delta, axis=0, keepdims=True)
            kss = [slice(hk * HEAD_DIM, (hk + 1) * HEAD_DIM) for hk in range(KV_HEADS)]
            qgs = [_stack_heads(qn, hk) for hk in range(KV_HEADS)]
            d_ogs = [_stack_heads(d_ob, hk) for hk in range(KV_HEADS)]
            scs = [_dot_nt(qgs[hk], kn[:, kss[hk]]) + b_ref[0, hk * GRP:(hk + 1) * GRP].reshape(GRP * BLOCK, 2 * BLOCK)
                   for hk in range(KV_HEADS)]
            dps = [_dot_nt(d_ogs[hk], vv[:, kss[hk]]) for hk in range(KV_HEADS)]
            ps = [jnp.exp(scs[hk] - _stack_cols(lse_all, hk)) for hk in range(KV_HEADS)]
            dss = [ps[hk] * (dps[hk] - _stack_cols(delta, hk)) for hk in range(KV_HEADS)]
            pbs = [_bf(p) for p in ps]
            dsbs = [_bf(ds) for ds in dss]
            for hk in range(KV_HEADS):
                dss_ref[hk * GRP:(hk + 1) * GRP] += dss[hk].reshape(GRP, BLOCK, 2 * BLOCK)
            for hk in range(KV_HEADS):
                dv_s[:, kss[hk]] = _dot_tn(pbs[hk], d_ogs[hk])
                dkn_s[:, kss[hk]] = _dot_tn(dsbs[hk], qgs[hk])
            dqns = [_dot(dsbs[hk], kn[:, kss[hk]]) * (HEAD_DIM ** -0.5) for hk in range(KV_HEADS)]
            for hk in range(KV_HEADS):
                for g in range(GRP):
                    h = hk * GRP + g
                    dqn_s[:, h * HEAD_DIM:(h + 1) * HEAD_DIM] = dqns[hk][g * BLOCK:(g + 1) * BLOCK]
            dq, gq = _heads_norm_bwd(qf, rq_x, qw, dqn_s[...], eq, eq3)
            dq_ref[...] = _bf(dq)
            gq_x[...] += gq
            dk, gk = _heads_norm_bwd(kf[:BLOCK], rk_x[:BLOCK], kw, ckn[...] + dkn_s[0:BLOCK, :], ek, ek3)
            dkv_ref[:, :KV_W] = _bf(dk)
            gk_x[...] += gk
            dkv_ref[:, KV_W:] = _bf(cv[...] + dv_s[0:BLOCK, :])
            ckn[...] = dkn_s[BLOCK:2 * BLOCK, :]
            cv[...] = dv_s[BLOCK:2 * BLOCK, :]

        @pl.when(i == nb)
        def _():
            kc = kc_ref[...]
            dk, gk = _heads_norm_bwd(kc, _heads_norm(kc, kw, ek, ek3)[1], kw, ckn[...], ek, ek3)
            dkv_ref[:, :KV_W] = _bf(dk)
            dkv_ref[:, KV_W:] = _bf(cv[...])
            gqw_ref[...] = _group_sum(jnp.broadcast_to(gq_x[...], (8, ATTN_W)), fq_ref[...])[0:1]
            gkw_ref[...] = _group_sum(jnp.broadcast_to(gk_x[...] + gk, (8, KV_W)), fk_ref[...])[0:1]

    last = nb - 1
    cur = lambda w, col=0: pl.BlockSpec((BLOCK, w), lambda i: (jnp.minimum(i, last), col))
    prev = lambda w, col=0: pl.BlockSpec((BLOCK, w), lambda i: (jnp.maximum(jnp.minimum(i, last) - 1, 0), col))
    late = lambda w: pl.BlockSpec((BLOCK, w), lambda i: (jnp.maximum(i - 1, 0), 0))
    sd = jax.ShapeDtypeStruct
    return pl.pallas_call(
        body, name="attn_bwd", grid=(nb + 1,),
        in_specs=[cur(ATTN_W), prev(KV_W, 0), cur(KV_W, 0), prev(KV_W, 1), cur(KV_W, 1),
                  pl.BlockSpec((1, ATTN_HEADS, BLOCK, 2 * BLOCK), lambda i: (jnp.minimum(i, 1), 0, 0, 0)),
                  _full((1, ATTN_HEADS))]
                 + [_full(c.shape) for c in consts + folds] + [cur(ATTN_W), cur(ATTN_HEADS), cur(ATTN_W)],
        out_specs=[cur(ATTN_W), late(2 * KV_W),
                   pl.BlockSpec((ATTN_HEADS, BLOCK, 2 * BLOCK), lambda i: (0, 0, 0)),
                   _full((1, HEAD_DIM)), _full((1, HEAD_DIM)), _full((1, ATTN_HEADS))],
        out_shape=[sd((s, ATTN_W), BF), sd((s, 2 * KV_W), BF),
                   sd((ATTN_HEADS, BLOCK, 2 * BLOCK), F32), sd((1, HEAD_DIM), F32), sd((1, HEAD_DIM), F32),
                   sd((1, ATTN_HEADS), F32)],
        scratch_shapes=[pltpu.VMEM((BLOCK, KV_W), F32), pltpu.VMEM((BLOCK, KV_W), F32),
                        pltpu.VMEM((BLOCK, ATTN_W), F32), pltpu.VMEM((2 * BLOCK, KV_W), F32),
                        pltpu.VMEM((2 * BLOCK, KV_W), F32), pltpu.VMEM((1, ATTN_W), F32), pltpu.VMEM((1, KV_W), F32)],
        compiler_params=_params(dimension_semantics=("arbitrary",)),
    )(q, kv, kv, kv, kv, bias, sinks, *consts, *folds, o_att, lse, d_o)


def _ssd_bwd(xbc, conv_all, dt_raw, conv_w, dt_bias, a_log, dsk_x, e_mat, e3t, hprev_all, dy_all):
    s = xbc.shape[0]
    nc = s // BLOCK
    ch = 1
    rows = ch * BLOCK
    nsteps = nc // ch
    gw = SSM_R * SSM_P
    b0, c0 = SSM_W, SSM_W + SSM_G * SSM_N

    def body(x_ref, conv_ref, dtr_ref, cw_ref, dtb_ref, alog_ref, dsk_ref, e_ref, e3_ref, hp_ref, dy_ref,
             dx_ref, ddt_ref, gcw_ref, gcb_ref, gdtb_ref, galog_ref, gdsk_ref,
             dh, nhead, gdskx, dxdt_s, dbc_s, dxd_s):
        def chunk_bwd(j):
            rs = slice(j * BLOCK, (j + 1) * BLOCK)
            conv = conv_ref[rs, :]
            sg, xact, u, dt, a, trilb, acum, dt_x, acum_x = _ssd_common(conv, dtr_ref[rs, :], dtb_ref, alog_ref, e3_ref)
            xs = xact[:, :SSM_W]
            acum_t = acum.T
            ea_x = jnp.exp2(acum_x)
            last_x = acum_x[BLOCK - 1:BLOCK, :]
            dte_x = jnp.exp2(last_x - acum_x)
            cd_x = jnp.exp2(last_x)
            xdt = xs * dt_x
            xw = xdt * dte_x
            hprev = hp_ref[j]
            dhn = dh[...]
            dy = dy_ref[rs, :]
            gdskx[...] += jnp.sum(dy * xs, axis=0, keepdims=True)
            dyea = dy * ea_x
            lane = lax.broadcasted_iota(jnp.int32, (BLOCK, SSM_HEADS), 1)
            dacum = jnp.zeros((BLOCK, SSM_HEADS), F32)
            dacc_x, dlast_x = [], []
            sls = [slice(g * gw, (g + 1) * gw) for g in range(SSM_G)]
            bgs = [_bf(xact[:, b0 + g * SSM_N:b0 + (g + 1) * SSM_N]) for g in range(SSM_G)]
            cgs = [_bf(xact[:, c0 + g * SSM_N:c0 + (g + 1) * SSM_N]) for g in range(SSM_G)]
            hpgs = [_bf(hprev[:, sl]) for sl in sls]
            dhgs = [_bf(dhn[:, sl]) for sl in sls]
            dyeags = [_bf(dyea[:, sl]) for sl in sls]
            xwgs = [_bf(xw[:, sl]) for sl in sls]
            xdt_b, dy_b = _bf(xdt), _bf(dy)
            low_half = lax.broadcasted_iota(jnp.int32, (BLOCK, 2 * SSM_P), 1) < SSM_P
            cbs = [_dot_nt(cgs[g], bgs[g]) for g in range(SSM_G)]
            gmats = [_dot(cgs[g], hpgs[g]) for g in range(SSM_G)]
            dxws = [_dot(bgs[g], dhgs[g]) for g in range(SSM_G)]
            dcgs = [_dot_nt(dyeags[g], hpgs[g]) for g in range(SSM_G)]
            dbgs = [_dot_nt(xwgs[g], dhgs[g]) for g in range(SSM_G)]
            for g in range(SSM_G):
                sl = sls[g]
                dh[:, sl] = dhn[:, sl] * cd_x[:, sl] + _dot_tn(cgs[g], dyeags[g])
                dxdt_s[:, sl] = dxws[g] * dte_x[:, sl]
                dacc_x.append(dy[:, sl] * gmats[g] * ea_x[:, sl] - dxws[g] * xw[:, sl])
                dlast_x.append(jnp.sum(dxws[g] * xw[:, sl], axis=0, keepdims=True)
                               + jnp.sum(dhn[:, sl] * hprev[:, sl], axis=0, keepdims=True) * cd_x[:, sl])
            for g in range(SSM_G):
                bg, cg, cb, dbg, dcg = bgs[g], cgs[g], cbs[g], dbgs[g], dcgs[g]
                hss = [slice((g * SSM_R + r) * SSM_P, (g * SSM_R + r + 1) * SSM_P) for r in range(SSM_R)]
                lms = [jnp.exp2(jnp.where(trilb, acum[:, g * SSM_R + r:g * SSM_R + r + 1]
                                         - acum_t[g * SSM_R + r:g * SSM_R + r + 1, :], -1e30)) for r in range(SSM_R)]
                mms = [cb * lm for lm in lms]
                mmbs = [_bf(mm) for mm in mms]
                dms = []
                for r in range(0, SSM_R, 2):
                    pair = slice(hss[r].start, hss[r + 1].stop)
                    xp, dyp = xdt_b[:, pair], dy_b[:, pair]
                    dmp = _dot_nt(dyp, jnp.concatenate([jnp.where(low_half, xp, 0), jnp.where(low_half, 0, xp)], axis=0))
                    dms += [dmp[:, :BLOCK], dmp[:, BLOCK:]]
                    dxd_s[:, pair] = _dot_tn(jnp.concatenate([mmbs[r], mmbs[r + 1]], axis=0),
                                             jnp.concatenate([jnp.where(low_half, dyp, 0), jnp.where(low_half, 0, dyp)], axis=0))
                dcb = sum(dms[r] * lms[r] for r in range(SSM_R))
                wms = [dms[r] * mms[r] for r in range(SSM_R)]
                antis = [_bf(wm - wm.T) for wm in wms]
                for r in range(SSM_R):
                    dacum = dacum + _dot(antis[r], (lane == g * SSM_R + r).astype(BF))
                dcbb = _bf(dcb)
                dbc_s[:, g * SSM_N:(g + 1) * SSM_N] = dbg + _dot_tn(dcbb, cg)
                dbc_s[:, SSM_G * SSM_N + g * SSM_N:SSM_G * SSM_N + (g + 1) * SSM_N] = dcg + _dot(dcbb, bg)
            dxdt = dxdt_s[...] + dxd_s[...]
            dxs = dy * dsk_ref[...] + dxdt * dt_x
            red = _group_sum(jnp.concatenate(
                [dxdt * xs, jnp.concatenate(dacc_x, axis=1),
                 jnp.broadcast_to(jnp.concatenate(dlast_x, axis=1), (8, SSM_W))], axis=0), e_ref[...])
            row = lax.broadcasted_iota(jnp.int32, (BLOCK, SSM_HEADS), 0)
            dacum = dacum + red[BLOCK:2 * BLOCK] + jnp.where(row == BLOCK - 1, red[2 * BLOCK:2 * BLOCK + 1], 0.0)
            ddta = _exact_left(_triu().astype(BF), dacum)
            ddt = red[:BLOCK] + ddta * a
            galog_ref[...] += jnp.sum(ddta * dt, axis=0, keepdims=True) * a
            du = ddt * _sig(u)
            ddt_ref[rs, :] = _bf(du)
            gdtb_ref[...] += jnp.sum(du, axis=0, keepdims=True)
            dconv = jnp.concatenate([dxs, dbc_s[...]], axis=1) * _dsilu(conv, sg, xact)
            gcb_ref[...] += jnp.sum(dconv, axis=0, keepdims=True)
            ext2 = jnp.concatenate([dconv, nhead[...]], axis=0)
            ahead = [pltpu.roll(ext2, BLOCK + 8 - (CONV_K - 1 - j), axis=0)[0:BLOCK] if j < CONV_K - 1 else dconv
                     for j in range(CONV_K)]
            dx_ref[rs, :] = _bf(sum(ahead[j] * cw_ref[j:j + 1, :] for j in range(CONV_K)))
            xraw = x_ref[rs, :]
            gcw_ref[...] += jnp.concatenate([jnp.sum(ahead[j] * xraw, axis=0, keepdims=True) for j in range(CONV_K)], axis=0)
            nhead[...] = dconv[0:8]

        i = pl.program_id(0)

        @pl.when(i == 0)
        def _():
            for ref in (dh, nhead, gdskx, gcw_ref, gcb_ref, gdtb_ref, galog_ref, gdsk_ref):
                ref[...] = jnp.zeros_like(ref)

        for j in reversed(range(ch)):
            chunk_bwd(j)

        @pl.when(i == nsteps - 1)
        def _():
            gdsk_ref[...] = _group_sum(jnp.broadcast_to(gdskx[...], (8, SSM_W)), e_ref[...])[0:1]

    chunk = lambda w: pl.BlockSpec((rows, w), lambda i: (nsteps - 1 - i, 0))
    sd = jax.ShapeDtypeStruct
    return pl.pallas_call(
        body, name="ssd_bwd", grid=(nsteps,),
        in_specs=[chunk(XBC_W), chunk(XBC_W),
                  chunk(SSM_HEADS), _full((CONV_K, XBC_W)), _full((1, SSM_HEADS)),
                  _full((1, SSM_HEADS)), _full((1, SSM_W)), _full((SSM_W, SSM_HEADS)), _full((3 * SSM_HEADS, SSM_W)),
                  pl.BlockSpec((ch, SSM_N, SSM_W), lambda i: (nsteps - 1 - i, 0, 0)), chunk(SSM_W)],
        out_specs=[chunk(XBC_W), chunk(SSM_HEADS), _full((CONV_K, XBC_W)), _full((1, XBC_W)),
                   _full((1, SSM_HEADS)), _full((1, SSM_HEADS)), _full((1, SSM_HEADS))],
        out_shape=[sd((s, XBC_W), BF), sd((s, SSM_HEADS), BF), sd((CONV_K, XBC_W), F32), sd((1, XBC_W), F32),
                   sd((1, SSM_HEADS), F32), sd((1, SSM_HEADS), F32), sd((1, SSM_HEADS), F32)],
        scratch_shapes=[pltpu.VMEM((SSM_N, SSM_W), F32), pltpu.VMEM((8, XBC_W), F32),
                        pltpu.VMEM((1, SSM_W), F32), pltpu.VMEM((BLOCK, SSM_W), F32),
                        pltpu.VMEM((BLOCK, 2 * SSM_G * SSM_N), F32), pltpu.VMEM((BLOCK, SSM_W), F32)],
        compiler_params=_params(dimension_semantics=("arbitrary",)),
    )(xbc, conv_all, dt_raw, conv_w, dt_bias, a_log, dsk_x, e_mat, e3t, hprev_all, dy_all)


def _dh(x, dout, norm_w, scale, dsegs, w_t, tm=256):
    s = x.shape[0]

    def body(x_ref, dout_ref, nw_ref, sc_ref, *rest):
        d_refs, w_hbm = rest[:NSEG], rest[NSEG]
        gx_ref, dshift_ref, dscale_ref, gnw_ref = rest[NSEG + 1:NSEG + 5]
        w_vm, sem = rest[NSEG + 5], rest[NSEG + 6]
        first = pl.program_id(0) == 0
        cps = [pltpu.make_async_copy(w_hbm.at[SEG_OFF[j]:SEG_OFF[j + 1], :], w_vm.at[SEG_OFF[j]:SEG_OFF[j + 1], :], sem.at[j])
               for j in range(NSEG)]

        def tile(waiting):
            dh = None
            for j in range(NSEG):
                if waiting:
                    cps[j].wait()
                part = _dot(d_refs[j][...], w_vm[SEG_OFF[j]:SEG_OFF[j + 1], :])
                dh = part if dh is None else dh + part
            xv = x_ref[...]
            r = lax.rsqrt(jnp.mean(xv * xv, axis=-1, keepdims=True) + EPS)
            xn = xv * r
            nw = nw_ref[...]
            sc1 = 1.0 + sc_ref[...]
            dshift_ref[...] += jnp.sum(dh, axis=0, keepdims=True)
            dhxn = jnp.sum(dh * xn, axis=0, keepdims=True)
            dscale_ref[...] += dhxn * nw
            gnw_ref[...] += dhxn * sc1
            dxn = dh * (nw * sc1)
            gx_ref[...] = dout_ref[...] + r * (dxn - xn * jnp.mean(xn * dxn, axis=-1, keepdims=True))

        @pl.when(first)
        def _():
            for cp in cps:
                cp.start(priority=1)
            for ref in (dshift_ref, dscale_ref, gnw_ref):
                ref[...] = jnp.zeros_like(ref)
            tile(True)

        @pl.when(jnp.logical_not(first))
        def _():
            tile(False)

    vec = _full((1, D_MODEL))
    sd = jax.ShapeDtypeStruct
    return pl.pallas_call(
        body, name="dh", grid=(s // tm,),
        in_specs=[_rows(tm, D_MODEL), _rows(tm, D_MODEL), vec, vec] + [_rows(tm, w) for w in SEG_W] + [ANY],
        out_specs=[_rows(tm, D_MODEL), vec, vec, vec],
        out_shape=[sd((s, D_MODEL), F32), sd((1, D_MODEL), F32), sd((1, D_MODEL), F32), sd((1, D_MODEL), F32)],
        scratch_shapes=[pltpu.VMEM((IN_W, D_MODEL), BF), pltpu.SemaphoreType.DMA((NSEG,))],
        compiler_params=_params(dimension_semantics=("arbitrary",)),
    )(x, dout, norm_w, scale, *dsegs, w_t)


def _gw_seg(h, dseg, name, tm=2048):
    s, w = dseg.shape
    tn = w if w <= 2048 else w // 2
    tm = min(tm, s)
    nm = s // tm

    def body(h_ref, d_ref, o_ref, acc):
        m = pl.program_id(1)

        @pl.when(m == 0)
        def _():
            acc[...] = jnp.zeros_like(acc)

        acc[...] += _dot_tn(d_ref[...], h_ref[...])

        @pl.when(m == nm - 1)
        def _():
            o_ref[...] = _bf(acc[...])

    return pl.pallas_call(
        body, name=name, grid=(w // tn, nm),
        in_specs=[pl.BlockSpec((tm, D_MODEL), lambda n, m: (m, 0)), pl.BlockSpec((tm, tn), lambda n, m: (m, n))],
        out_specs=pl.BlockSpec((tn, D_MODEL), lambda n, m: (n, 0)),
        out_shape=jax.ShapeDtypeStruct((w, D_MODEL), BF),
        scratch_shapes=[pltpu.VMEM((tn, D_MODEL), F32)],
        compiler_params=_params(dimension_semantics=("arbitrary", "arbitrary")),
    )(h, dseg)


def _gw_in(h, dsegs):
    return [_gw_seg(h, d, "gw_in_%d" % j) for j, d in enumerate(dsegs)]


def _local_step(x, tgt, shift, scale, gate, w_t, rows_fn, norm_w, qnw, knw, rel_bias, sinks,
                conv_w, conv_b, dt_bias, a_log, d_skip, ssm_nw, after_mid=None, after_gw=None):
    oh_t = _bucket_onehot_t()
    bias = _masked_bias(_bias_dense(rel_bias.T, oh_t).reshape(ATTN_HEADS, BLOCK, 2 * BLOCK))
    *segs, h = _inproj(x, norm_w, scale, shift, w_t)
    q, kv, zam, xbc, dtr, gab = segs
    consts = _attn_consts(qnw, knw)
    o_att, lse = _attn_fwd(q, kv, bias, sinks, consts)
    e_mat, e3t = _membership(SSM_W, SSM_P, SSM_HEADS)
    dsk_x = jnp.repeat(d_skip, SSM_P, axis=1)
    ypre, hprev, conv = _ssd_fwd(xbc, dtr, conv_w, conv_b, dt_bias, a_log, dsk_x, e3t)
    (dout, d_o, dzam, dyp, dgab, yag, dy_a, yn, dy_b, merged, dob, g_ssm_nw, dgate, loss) = _mid(
        x, tgt, o_att, zam, ypre, gab, gate, ssm_nw, rows_fn(ypre))
    g_wap = _gw_seg(dy_a, yag, "gw_attn_proj")
    g_wsp = _gw_seg(dy_b, yn, "gw_ssm_proj")
    g_wout = _gw_seg(dob, merged, "gw_out")
    zero = after_mid(g_wap, g_wsp, g_wout) if after_mid is not None else 0.0
    dq, dkv, dss, g_qnw, g_knw, g_sinks = _attn_bwd(q, kv, bias, sinks + zero, consts, o_att, lse, d_o)
    g_rel = _bias_grad(dss.reshape(ATTN_HEADS, BLOCK * 2 * BLOCK), oh_t).T
    dxbc, ddt, g_cw, g_cb, g_dtb, g_alog, g_dsk = _ssd_bwd(
        xbc, conv, dtr, conv_w, dt_bias, a_log, dsk_x, e_mat, e3t, hprev, dyp)
    dsegs = (dq, dkv, dzam, dxbc, ddt, dgab)
    g_ws = _gw_in(h, dsegs)
    zero = after_gw(g_ws) if after_gw is not None else 0.0
    gx, dshift, dscale, g_nw = _dh(x, dout, norm_w + zero, scale, dsegs, w_t)
    return dict(loss=loss, grad_x=gx, dmod=jnp.concatenate([dshift, dscale, dgate], axis=1), g_ws=g_ws,
                g_wap=g_wap, g_wsp=g_wsp, g_wout=g_wout, g_norm_w=g_nw, g_qnw=g_qnw, g_knw=g_knw, g_rel=g_rel,
                g_sinks=g_sinks, g_conv_w=g_cw, g_conv_b=g_cb, g_dt_bias=g_dtb, g_a_log=g_alog, g_d_skip=g_dsk,
                g_ssm_nw=g_ssm_nw)


def _me():
    return lax.axis_index("x"), lax.axis_index("y"), lax.axis_index("c")


def _flip(v, bit):
    return 1 - v if bit else v


def _ag_direct(v, name):
    def body(v_ref, out_ref, send_sems, recv_sems, local_sem):
        x, y, c = _me()
        me = 4 * x + 2 * y + c
        mine = pltpu.make_async_copy(v_ref, out_ref.at[me], local_sem)
        mine.start()
        peers = [(_flip(x, k >> 2 & 1), _flip(y, k >> 1 & 1), _flip(c, k & 1)) for k in range(1, N_DEV)]
        sends = [pltpu.make_async_remote_copy(
            src_ref=v_ref, dst_ref=out_ref.at[me], send_sem=send_sems.at[j], recv_sem=recv_sems.at[j],
            device_id=p, device_id_type=MESH) for j, p in enumerate(peers)]
        for cp in sends:
            cp.start()
        for j, (px, py, pc) in enumerate(peers):
            pltpu.make_async_remote_copy(
                src_ref=v_ref, dst_ref=out_ref.at[4 * px + 2 * py + pc], send_sem=send_sems.at[j],
                recv_sem=recv_sems.at[j], device_id=(px, py, pc), device_id_type=MESH).wait_recv()
        for cp in sends:
            cp.wait_send()
        mine.wait()

    vm = pl.BlockSpec(memory_space=pltpu.VMEM)
    return pl.pallas_call(
        body, name=name, out_shape=jax.ShapeDtypeStruct((N_DEV,) + v.shape, v.dtype),
        in_specs=[vm], out_specs=vm,
        scratch_shapes=[pltpu.SemaphoreType.DMA((N_DEV - 1,)), pltpu.SemaphoreType.DMA((N_DEV - 1,)),
                        pltpu.SemaphoreType.DMA],
        compiler_params=_params(),
    )(v)


def _gather_mod(v, w_ada, b_piece):
    ncols = w_ada.shape[1]

    def body(v_ref, w_ref, b_ref, rows_ref, mods_ref, piece, send_sems, recv_sems, local_sems):
        x, y, c = _me()
        me = 4 * x + 2 * y + c
        peers = _peers(x, y, c)

        def exchange(src, dst, rnd):
            mine = pltpu.make_async_copy(src, dst.at[me], local_sems.at[rnd])
            mine.start()
            sends = [pltpu.make_async_remote_copy(
                src_ref=src, dst_ref=dst.at[me], send_sem=send_sems.at[rnd, j], recv_sem=recv_sems.at[rnd, j],
                device_id=p, device_id_type=MESH) for j, p in enumerate(peers)]
            for cp in sends:
                cp.start()
            for j, (px, py, pc) in enumerate(peers):
                pltpu.make_async_remote_copy(
                    src_ref=src, dst_ref=dst.at[4 * px + 2 * py + pc], send_sem=send_sems.at[rnd, j],
                    recv_sem=recv_sems.at[rnd, j], device_id=(px, py, pc), device_id_type=MESH).wait_recv()
            for cp in sends:
                cp.wait_send()
            mine.wait()

        exchange(v_ref, rows_ref, 0)
        c_all = rows_ref[:, 0, :D_MODEL]
        piece[...] = _dot(_bf(_silu(c_all)), _bf(w_ref[...])) + b_ref[...]
        exchange(piece, mods_ref, 1)

    vm = pl.BlockSpec(memory_space=pltpu.VMEM)
    return pl.pallas_call(
        body, name="gather_mod",
        out_shape=(jax.ShapeDtypeStruct((N_DEV,) + v.shape, F32), jax.ShapeDtypeStruct((N_DEV, N_DEV, ncols), F32)),
        in_specs=[vm, vm, vm], out_specs=(vm, vm),
        scratch_shapes=[pltpu.VMEM((N_DEV, ncols), F32), pltpu.SemaphoreType.DMA((2, N_DEV - 1)),
                        pltpu.SemaphoreType.DMA((2, N_DEV - 1)), pltpu.SemaphoreType.DMA((2,))],
        compiler_params=_params(),
    )(v, w_ada, b_piece)


def _ag_relayed(v, name, chunks=1):
    rows = v.shape[0] // chunks
    assert rows * chunks == v.shape[0] and rows % 8 == 0

    def body(v_ref, out_ref, token, send_sems, recv_sems, local_sem):
        token[...] = jnp.zeros_like(token)
        x, y, c = _me()
        flip_x, flip_y = 1 - x, 1 - y
        ax, ay = c * x + (1 - c) * flip_x, c * flip_y + (1 - c) * y
        bx, by = c * flip_x + (1 - c) * x, c * y + (1 - c) * flip_y
        me, sib = (x, y, c), (x, y, 1 - c)
        a, b, dg = (ax, ay, c), (bx, by, c), (flip_x, flip_y, c)
        sa, sb, sdg = (bx, by, 1 - c), (ax, ay, 1 - c), (flip_x, flip_y, 1 - c)

        def piece(ref, k):
            return ref.at[pl.ds(k * rows, rows), :]

        def slot(px, py, pc):
            return out_ref.at[4 * px + 2 * py + pc]

        def copy(n, k, block, to, src=None):
            return pltpu.make_async_remote_copy(
                src_ref=piece(slot(*block) if src is None else src, k), dst_ref=piece(slot(*block), k),
                send_sem=send_sems.at[n * chunks + k], recv_sem=recv_sems.at[n * chunks + k],
                device_id=to, device_id_type=MESH)

        mine = pltpu.make_async_copy(v_ref, slot(*me), local_sem)
        mine.start()
        started = [copy(n, k, me, to, src=v_ref) for k in range(chunks) for n, to in ((1, a), (2, b), (0, sib))]
        for cp in started:
            cp.start()

        def arrived(n, k, block, then):
            copy(n, k, block, me).wait_recv()
            for n2, to in then:
                started.append(copy(n2, k, block, to))
                started[-1].start()

        for k in range(chunks):
            arrived(1, k, a, ((3, b), (4, sib)))
            arrived(2, k, b, ((5, sib),))
        for k in range(chunks):
            arrived(3, k, dg, ((6, sib),))
        for k in range(chunks):
            for n, block in ((0, sib), (4, sa), (5, sb), (6, sdg)):
                copy(n, k, block, me).wait_recv()
        for cp in started:
            cp.wait_send()
        mine.wait()

    out, token = pl.pallas_call(
        body, name=name,
        out_shape=(jax.ShapeDtypeStruct((N_DEV,) + v.shape, v.dtype), jax.ShapeDtypeStruct((8, 128), v.dtype)),
        in_specs=[ANY], out_specs=(ANY, pl.BlockSpec(memory_space=pltpu.VMEM)),
        scratch_shapes=[pltpu.SemaphoreType.DMA((7 * chunks,)), pltpu.SemaphoreType.DMA((7 * chunks,)),
                        pltpu.SemaphoreType.DMA],
        compiler_params=_params(),
    )(v)
    return out, token[0:1, 0:1]


HBM = pl.BlockSpec(memory_space=pltpu.HBM)
SEM = pl.BlockSpec(memory_space=pltpu.SEMAPHORE)
EFFECT = pltpu.SideEffectType.DATAFLOW_SIDE_EFFECTING


def _peers(x, y, c):
    return [(_flip(x, k >> 2 & 1), _flip(y, k >> 1 & 1), _flip(c, k & 1)) for k in range(1, N_DEV)]


def _exchange_start(src, land, gather, name):
    def body(src_ref, land_ref, send_sems, recv_sems, src_thru, land_thru, token):
        x, y, c = _me()
        me = 4 * x + 2 * y + c
        for j, (px, py, pc) in enumerate(_peers(x, y, c)):
            pltpu.make_async_remote_copy(
                src_ref=src_ref if gather else src_ref.at[4 * px + 2 * py + pc], dst_ref=land_ref.at[me],
                send_sem=send_sems.at[j], recv_sem=recv_sems.at[j], device_id=(px, py, pc), device_id_type=MESH).start()
        token[...] = jnp.zeros_like(token)

    sems = pltpu.SemaphoreType.DMA((N_DEV - 1,))
    out = pl.pallas_call(
        body, name=name,
        out_shape=(sems, sems, pltpu.HBM(src.shape, src.dtype), pltpu.HBM(land.shape, land.dtype),
                   jax.ShapeDtypeStruct((8, 128), F32)),
        in_specs=(HBM, HBM), out_specs=(SEM, SEM, HBM, HBM, pl.BlockSpec(memory_space=pltpu.VMEM)),
        input_output_aliases={0: 2, 1: 3},
        compiler_params=pltpu.CompilerParams(has_side_effects=EFFECT),
    )(pltpu.with_memory_space_constraint(src, pltpu.HBM), pltpu.with_memory_space_constraint(land, pltpu.HBM))
    return out[:4], out[4][0, 0]


def _exchange_wait(started, after, gather, name):
    send_sems, recv_sems, src_thru, land_thru = started

    def body(src_ref, land_ref, send_sems, recv_sems, after_ref, src_dead, got_ref):
        x, y, c = _me()
        for j, (px, py, pc) in enumerate(_peers(x, y, c)):
            pid = 4 * px + 2 * py + pc
            cp = pltpu.make_async_remote_copy(
                src_ref=src_ref if gather else src_ref.at[pid], dst_ref=land_ref.at[pid],
                send_sem=send_sems.at[j], recv_sem=recv_sems.at[j], device_id=(px, py, pc), device_id_type=MESH)
            cp.wait_send()
            cp.wait_recv()

    return pl.pallas_call(
        body, name=name,
        out_shape=(pltpu.HBM(src_thru.shape, src_thru.dtype), pltpu.HBM(land_thru.shape, land_thru.dtype)),
        in_specs=(HBM, HBM, SEM, SEM, ANY), out_specs=(HBM, HBM), input_output_aliases={0: 0, 1: 1},
        compiler_params=pltpu.CompilerParams(has_side_effects=EFFECT),
    )(src_thru, land_thru, send_sems, recv_sems, after)[1]


def _silu(a):
    return a * _sig(a)


def _gw_ada(c_all, dmod_piece):
    def body(c_ref, d_ref, o_ref):
        o_ref[...] = _dot_tn(_bf(_silu(c_ref[...])), _bf(d_ref[...]))

    return pl.pallas_call(
        body, name="gw_ada", out_shape=jax.ShapeDtypeStruct((c_all.shape[1], dmod_piece.shape[1]), F32),
        compiler_params=_params(),
    )(c_all, dmod_piece)


def _adam(parts, w, m, v, name):
    k, r, n = parts.shape
    if r <= 256 or r % 256 == 0:
        tr, tn = min(r, 256), n
    else:
        tr, tn = r, 256
    assert r % tr == 0 and n % tn == 0

    def body(p_ref, w_ref, m_ref, v_ref, g_ref, d_ref, nm_ref, nv_ref):
        g = p_ref[0].astype(F32)
        for j in range(1, k):
            g = g + p_ref[j].astype(F32)
        g_ref[...] = g
        d_ref[...], nm_ref[...], nv_ref[...] = _adam_math(g, w_ref[...], m_ref[...], v_ref[...])

    blk = pl.BlockSpec((tr, tn), lambda i, j: (i, j))
    return pl.pallas_call(
        body, name=name, grid=(r // tr, n // tn),
        in_specs=[pl.BlockSpec((k, tr, tn), lambda i, j: (0, i, j)), blk, blk, blk],
        out_specs=[blk, blk, blk, blk],
        out_shape=[jax.ShapeDtypeStruct((r, n), F32)] * 4,
        compiler_params=_params(dimension_semantics=("arbitrary", "arbitrary")),
    )(parts, w, m, v)


def _adam_math(g, w, m, v):
    m_new = ADAM_B1 * m + (1.0 - ADAM_B1) * g
    v_new = ADAM_B2 * v + (1.0 - ADAM_B2) * jnp.square(g)
    m_hat = m_new / (1.0 - ADAM_B1 ** ADAM_STEP)
    v_hat = v_new / (1.0 - ADAM_B2 ** ADAM_STEP)
    return -ADAM_LR * (m_hat / (jnp.sqrt(v_hat) + ADAM_EPS) + ADAM_WD * w), m_new, v_new


_SMALL = (("b_ada", 3 * D_MODEL), ("norm_w", D_MODEL), ("q_norm_w", HEAD_DIM), ("k_norm_w", HEAD_DIM),
          ("rel_bias", REL_BUCKETS * ATTN_HEADS), ("sinks", ATTN_HEADS), ("conv_b", XBC_W), ("dt_bias", SSM_HEADS),
          ("a_log", SSM_HEADS), ("d_skip", SSM_HEADS), ("ssm_norm_w", SSM_W))
_SLOT = tuple(-(-n // 128) * 128 for _, n in _SMALL)
_SLOT_OFF = tuple(int(o) for o in np.cumsum((0,) + _SLOT))
_LOSS_OFF = _SLOT_OFF[-1]
_CW_OFF = _LOSS_OFF + 128
_PACK_N = _CW_OFF + CONV_K * XBC_W


def _pack_partials(small, loss, g_conv_w):
    parts = []
    for (name, n), slot in zip(_SMALL, _SLOT):
        parts.append(small[name].reshape(1, n))
        if slot > n:
            parts.append(jnp.zeros((1, slot - n), F32))
    parts += [loss.reshape(1, 1), jnp.zeros((1, 127), F32), g_conv_w.reshape(1, CONV_K * XBC_W)]
    return jnp.concatenate(parts, axis=1)


def _adam_small(pack_all, w, m, v):
    names = [name for name, _ in _SMALL]

    def body(p_ref, *rest):
        ins, outs = rest[:3 * len(names)], rest[3 * len(names):]

        def total(off, n):
            g = p_ref[0, :, off:off + n]
            for d in range(1, N_DEV):
                g = g + p_ref[d, :, off:off + n]
            return g

        for j, (name, n) in enumerate(_SMALL):
            g = total(_SLOT_OFF[j], n)
            delta, m_new, v_new = _adam_math(g, ins[3 * j][...], ins[3 * j + 1][...], ins[3 * j + 2][...])
            outs[4 * j][...] = g
            outs[4 * j + 1][...] = delta
            outs[4 * j + 2][...] = m_new
            outs[4 * j + 3][...] = v_new
        outs[-1][...] = total(_LOSS_OFF, 1)

    flat = []
    for name, n in _SMALL:
        flat += [w[name].reshape(1, n), m[name].reshape(1, n), v[name].reshape(1, n)]
    out_shape = [jax.ShapeDtypeStruct((1, n), F32) for _, n in _SMALL for _ in range(4)] + [jax.ShapeDtypeStruct((1, 1), F32)]
    out = pl.pallas_call(body, name="adam_small", out_shape=out_shape, compiler_params=_params())(pack_all, *flat)
    res = {name: [out[4 * j + t].reshape(w[name].shape) for t in range(4)] for j, name in enumerate(names)}
    return res, out[-1]


WEIGHTS = ("w_ada", "b_ada", "norm_w", "w_in", "q_norm_w", "k_norm_w", "rel_bias", "sinks", "conv_w", "conv_b",
           "dt_bias", "a_log", "d_skip", "ssm_norm_w", "w_attn_proj", "w_ssm_proj", "w_out")


def kernel(x, c, w_ada, b_ada, norm_w, w_in, q_norm_w, k_norm_w, rel_bias, sinks, conv_w, conv_b, dt_bias, a_log, d_skip, ssm_norm_w, w_attn_proj, w_ssm_proj, w_out, loss_target, m_w_ada, m_b_ada, m_norm_w, m_w_in, m_q_norm_w, m_k_norm_w, m_rel_bias, m_sinks, m_conv_w, m_conv_b, m_dt_bias, m_a_log, m_d_skip, m_ssm_norm_w, m_w_attn_proj, m_w_ssm_proj, m_w_out, v_w_ada, v_b_ada, v_norm_w, v_w_in, v_q_norm_w, v_k_norm_w, v_rel_bias, v_sinks, v_conv_w, v_conv_b, v_dt_bias, v_a_log, v_d_skip, v_ssm_norm_w, v_w_attn_proj, v_w_ssm_proj, v_w_out):
    w = dict(w_ada=w_ada, b_ada=b_ada, norm_w=norm_w, w_in=w_in, q_norm_w=q_norm_w, k_norm_w=k_norm_w,
             rel_bias=rel_bias, sinks=sinks, conv_w=conv_w, conv_b=conv_b, dt_bias=dt_bias, a_log=a_log,
             d_skip=d_skip, ssm_norm_w=ssm_norm_w, w_attn_proj=w_attn_proj, w_ssm_proj=w_ssm_proj, w_out=w_out)
    m = dict(w_ada=m_w_ada, b_ada=m_b_ada, norm_w=m_norm_w, w_in=m_w_in, q_norm_w=m_q_norm_w, k_norm_w=m_k_norm_w,
             rel_bias=m_rel_bias, sinks=m_sinks, conv_w=m_conv_w, conv_b=m_conv_b, dt_bias=m_dt_bias, a_log=m_a_log,
             d_skip=m_d_skip, ssm_norm_w=m_ssm_norm_w, w_attn_proj=m_w_attn_proj, w_ssm_proj=m_w_ssm_proj, w_out=m_w_out)
    v = dict(w_ada=v_w_ada, b_ada=v_b_ada, norm_w=v_norm_w, w_in=v_w_in, q_norm_w=v_q_norm_w, k_norm_w=v_k_norm_w,
             rel_bias=v_rel_bias, sinks=v_sinks, conv_w=v_conv_w, conv_b=v_conv_b, dt_bias=v_dt_bias, a_log=v_a_log,
             d_skip=v_d_skip, ssm_norm_w=v_ssm_norm_w, w_attn_proj=v_w_attn_proj, w_ssm_proj=v_w_ssm_proj, w_out=v_w_out)
    me = 4 * lax.axis_index("x") + 2 * lax.axis_index("y") + lax.axis_index("c")
    ada_n = w_ada.shape[2]
    in_n = w_in.shape[2]
    cw_n = conv_w.shape[2]

    b_piece = lax.dynamic_slice_in_dim(b_ada, me * ada_n, ada_n, axis=1)
    first, mod_all = _gather_mod(jnp.concatenate([c, conv_w[0].reshape(1, CONV_K * cw_n)], axis=1), w_ada[0], b_piece)
    first = first[:, 0]
    c_all = first[:, :D_MODEL]
    conv_w_full = first[:, D_MODEL:].reshape(N_DEV, CONV_K, cw_n).transpose(1, 0, 2).reshape(CONV_K, XBC_W)
    mod = lax.dynamic_index_in_dim(mod_all, me, axis=1, keepdims=False).reshape(1, 3 * D_MODEL)
    shift, scale, gate = mod[:, :D_MODEL], mod[:, D_MODEL:2 * D_MODEL], mod[:, 2 * D_MODEL:]

    pad = -in_n % 24
    w_t, zero = _ag_relayed(jnp.pad(w_in[0].T.astype(BF), ((0, pad), (0, 0))), "ag_w_in", chunks=3)
    w_t = w_t[:, :in_n].reshape(N_DEV * in_n, D_MODEL)

    def with_mine(blocks, mine):
        return lax.dynamic_update_index_in_dim(lax.empty(blocks, mine.dtype), mine, me, axis=0)

    rows = jnp.concatenate([w_attn_proj[0], w_ssm_proj[0], w_out[0]], axis=0).astype(BF) + zero
    r_ap, r_sp = w_attn_proj.shape[1], w_ssm_proj.shape[1]
    rows_started, zero = _exchange_start(rows, with_mine((N_DEV,) + rows.shape, rows), True, "ag_rows_start")

    def rows_fn(after):
        return _exchange_wait(rows_started, after, True, "ag_rows_wait")

    started = {}

    def send_blocks(key, g, name):
        started[key], zero = _exchange_start(
            g, with_mine(g.shape, lax.dynamic_index_in_dim(g, me, axis=0, keepdims=False)), False, name)
        return zero

    def after_mid(g_wap, g_wsp, g_wout):
        return send_blocks("rows", jnp.concatenate(
            [g_wap.reshape(N_DEV, r_ap, D_MODEL), g_wsp.reshape(N_DEV, r_sp, D_MODEL),
             g_wout.reshape(N_DEV, r_ap, D_MODEL)], axis=1), "rs_rows_start")

    def after_gw(g_ws):
        return send_blocks("in", jnp.concatenate(g_ws, axis=0).reshape(N_DEV, in_n, D_MODEL), "rs_in_start")

    r = _local_step(x[0], loss_target[0], shift, scale + zero, gate, w_t, rows_fn, norm_w, q_norm_w, k_norm_w,
                    rel_bias, sinks, conv_w_full, conv_b, dt_bias, a_log, d_skip, ssm_norm_w, after_mid, after_gw)

    small = dict(b_ada=r["dmod"], norm_w=r["g_norm_w"], q_norm_w=r["g_qnw"], k_norm_w=r["g_knw"], rel_bias=r["g_rel"],
                 sinks=r["g_sinks"], conv_b=r["g_conv_b"], dt_bias=r["g_dt_bias"], a_log=r["g_a_log"],
                 d_skip=r["g_d_skip"], ssm_norm_w=r["g_ssm_nw"])
    pack_all = _ag_direct(_pack_partials(small, r["loss"], r["g_conv_w"]), "ag_small")
    res, loss = _adam_small(pack_all, w, m, v)
    loss = loss[0, 0]
    cw_parts = pack_all[:, 0, _CW_OFF:].reshape(N_DEV, CONV_K, XBC_W)
    cw_mine = lax.dynamic_slice_in_dim(cw_parts, me * cw_n, cw_n, axis=2)
    res["conv_w"] = [a[None] for a in _adam(cw_mine, conv_w[0], m_conv_w[0], v_conv_w[0], "adam_conv_w")]

    dmod_piece = lax.dynamic_slice_in_dim(pack_all[:, 0, :3 * D_MODEL], me * ada_n, ada_n, axis=1)
    g_ada = _gw_ada(c_all, dmod_piece)
    res["w_ada"] = [a[None] for a in _adam(g_ada[None], w_ada[0], m_w_ada[0], v_w_ada[0], "adam_w_ada")]

    cat = lambda d: jnp.concatenate([d["w_attn_proj"][0], d["w_ssm_proj"][0], d["w_out"][0]], axis=0)
    rows_res = _adam(_exchange_wait(started["rows"], g_ada, False, "rs_rows_wait"), cat(w), cat(m), cat(v), "adam_w_rows")
    res["w_in"] = [a.T[None] for a in _adam(_exchange_wait(started["in"], rows_res[0], False, "rs_in_wait"),
                                            w_in[0].T, m_w_in[0].T, v_w_in[0].T, "adam_w_in")]
    res["w_attn_proj"] = [a[None, :r_ap] for a in rows_res]
    res["w_ssm_proj"] = [a[None, r_ap:r_ap + r_sp] for a in rows_res]
    res["w_out"] = [a[None, r_ap + r_sp:] for a in rows_res]

    outs = [loss, r["grad_x"][None]]
    for j in range(4):
        outs += [res[name][j] for name in WEIGHTS]
    return tuple(outs)
```

```python
import math

import numpy as np
import jax
import jax.numpy as jnp
from jax import lax
from jax.experimental import pallas as pl
from jax.experimental.pallas import tpu as pltpu

F32 = jnp.float32
BF = jnp.bfloat16
HI = lax.Precision.HIGHEST

D_MODEL = 1024
ATTN_HEADS = 16
KV_HEADS = 4
GRP = ATTN_HEADS // KV_HEADS
HEAD_DIM = 64
ATTN_W = ATTN_HEADS * HEAD_DIM
KV_W = KV_HEADS * HEAD_DIM
BLOCK = 128
REL_BUCKETS = 32
REL_MAX_DIST = 128
SSM_W = 2048
SSM_P = 64
SSM_HEADS = 32
SSM_G = 4
SSM_R = 8
SSM_N = 128
CONV_K = 4
XBC_W = SSM_W + 2 * SSM_G * SSM_N
SEG_W = (ATTN_W, 2 * KV_W, ATTN_W + SSM_W, XBC_W, SSM_HEADS, 2 * D_MODEL)
NSEG = len(SEG_W)
SEG_OFF = tuple(int(v) for v in np.cumsum((0,) + SEG_W))
IN_W = SEG_OFF[-1]
GATE_SEGS = (2, 5)
EPS = 1e-6
N_DEV = 8
ADAM_LR, ADAM_B1, ADAM_B2, ADAM_EPS, ADAM_WD, ADAM_STEP = 0.001, 0.9, 0.999, 1e-08, 0.01, 10
VMEM_LIMIT = 60 * 1024 * 1024
MESH = pl.DeviceIdType.MESH
ANY = pl.BlockSpec(memory_space=pl.ANY)


def _dot(a, b, precision=None):
    return jnp.dot(a, b, preferred_element_type=F32, precision=precision)


def _dot_nt(a, b, precision=None):
    return lax.dot_general(a, b, (((1,), (1,)), ((), ())), preferred_element_type=F32, precision=precision)


def _dot_tn(a, b, precision=None):
    return lax.dot_general(a, b, (((0,), (0,)), ((), ())), preferred_element_type=F32, precision=precision)


def _bf(a):
    return a.astype(BF)


def _sig(a):
    return 0.5 * jnp.tanh(0.5 * a) + 0.5


def _params(**kw):
    return pltpu.CompilerParams(vmem_limit_bytes=VMEM_LIMIT, **kw)


def _full(shape):
    nd = len(shape)
    return pl.BlockSpec(shape, lambda i: (0,) * nd)


def _rows(tm, w):
    return pl.BlockSpec((tm, w), lambda i: (i, 0))


def _inproj(x, norm_w, scale, shift, w_t, tm=256):
    s = x.shape[0]

    def body(x_ref, nw_ref, sc_ref, sh_ref, w_hbm, *rest):
        outs, h_ref, w_vm, sem = rest[:NSEG], rest[NSEG], rest[NSEG + 1], rest[NSEG + 2]
        first = pl.program_id(0) == 0
        cps = [pltpu.make_async_copy(w_hbm.at[SEG_OFF[j]:SEG_OFF[j + 1], :], w_vm.at[SEG_OFF[j]:SEG_OFF[j + 1], :], sem.at[j])
               for j in range(NSEG)]

        def tile(waiting):
            xv = x_ref[...]
            r = lax.rsqrt(jnp.mean(xv * xv, axis=-1, keepdims=True) + EPS)
            h = xv * r * (nw_ref[...] * (1.0 + sc_ref[...])) + sh_ref[...]
            hb = _bf(h)
            h_ref[...] = hb
            for j in range(NSEG):
                if waiting:
                    cps[j].wait()
                outs[j][...] = _dot_nt(hb, w_vm[SEG_OFF[j]:SEG_OFF[j + 1], :]).astype(outs[j].dtype)

        @pl.when(first)
        def _():
            for cp in cps:
                cp.start()
            tile(True)

        @pl.when(jnp.logical_not(first))
        def _():
            tile(False)

    vec = _full((1, D_MODEL))
    return pl.pallas_call(
        body, name="inproj", grid=(s // tm,),
        in_specs=[_rows(tm, D_MODEL), vec, vec, vec, ANY],
        out_specs=[_rows(tm, w) for w in SEG_W] + [_rows(tm, D_MODEL)],
        out_shape=[jax.ShapeDtypeStruct((s, w), BF if j in GATE_SEGS else F32) for j, w in enumerate(SEG_W)]
                  + [jax.ShapeDtypeStruct((s, D_MODEL), BF)],
        scratch_shapes=[pltpu.VMEM((IN_W, D_MODEL), BF), pltpu.SemaphoreType.DMA((NSEG,))],
        compiler_params=_params(dimension_semantics=("arbitrary",)),
    )(x, norm_w, scale, shift, w_t)


def _bucket_onehot_t():
    qi = jnp.arange(BLOCK)[:, None]
    kj = jnp.arange(2 * BLOCK)[None, :]
    dist = qi + BLOCK - kj
    n = jnp.maximum(dist, 0)
    max_exact = REL_BUCKETS // 2
    nf = jnp.maximum(n, 1).astype(F32)
    large = max_exact + (jnp.log(nf / max_exact) / math.log(REL_MAX_DIST / max_exact)
                         * (REL_BUCKETS - max_exact)).astype(jnp.int32)
    large = jnp.minimum(large, REL_BUCKETS - 1)
    bucket = jnp.where(n < max_exact, n, large).reshape(1, BLOCK * 2 * BLOCK)
    return (bucket == jnp.arange(REL_BUCKETS)[:, None]).astype(F32)


def _bias_dense(rel_bias_t, oh_t):
    def body(rb_ref, oh_ref, o_ref):
        o_ref[...] = _dot(rb_ref[...], oh_ref[...], HI)

    return pl.pallas_call(
        body, name="bias_dense", out_shape=jax.ShapeDtypeStruct((ATTN_HEADS, BLOCK * 2 * BLOCK), F32),
        compiler_params=_params(),
    )(rel_bias_t, oh_t)


def _bias_grad(ds_sum, oh_t):
    def body(ds_ref, oh_ref, o_ref):
        o_ref[...] = _dot_nt(ds_ref[...], oh_ref[...], HI)

    return pl.pallas_call(
        body, name="bias_grad", out_shape=jax.ShapeDtypeStruct((ATTN_HEADS, REL_BUCKETS), F32),
        compiler_params=_params(),
    )(ds_sum, oh_t)


def _group_sum(a, e):
    hi = _bf(a)
    return _dot(hi, e) + _dot(_bf(a - hi.astype(F32)), e)


def _group_bcast(a, e3t):
    hi = _bf(a)
    r1 = a - hi.astype(F32)
    mid = _bf(r1)
    return _dot(jnp.concatenate([hi, mid, _bf(r1 - mid.astype(F32))], axis=1), e3t)


def _membership(width, group, ngroups):
    e = (jnp.arange(width)[:, None] // group == jnp.arange(ngroups)[None, :]).astype(BF)
    return e, jnp.tile(e.T, (3, 1))


def _fold(width, group):
    return (jnp.arange(width)[:, None] % group == jnp.arange(group)[None, :]).astype(BF)


def _heads_norm(t, w_x, e, e3t):
    r = lax.rsqrt(_dot(_bf(t * t), e) * (1.0 / HEAD_DIM) + EPS)
    r_x = _group_bcast(r, e3t)
    return t * r_x * w_x, r_x


def _heads_norm_bwd(t, r_x, w_x, d, e, e3t):
    wd = d * w_x
    corr = _group_bcast(_dot(_bf(t * wd), e) * (1.0 / HEAD_DIM), e3t)
    return r_x * wd - t * (r_x * r_x * r_x) * corr, jnp.sum(d * t * r_x, axis=0, keepdims=True)


def _stack_heads(a, hk):
    return jnp.concatenate([a[:, (hk * GRP + g) * HEAD_DIM:(hk * GRP + g + 1) * HEAD_DIM] for g in range(GRP)], axis=0)


def _stack_cols(a, hk):
    return jnp.concatenate([a[:, hk * GRP + g:hk * GRP + g + 1] for g in range(GRP)], axis=0)


def _masked_bias(bias):
    qi = jnp.arange(BLOCK)[:, None]
    kj = jnp.arange(2 * BLOCK)[None, :]
    cur_ok = jnp.logical_and(kj >= BLOCK, kj - BLOCK <= qi)
    both_ok = jnp.logical_or(jnp.logical_and(kj < BLOCK, kj > qi), cur_ok)
    return jnp.stack([jnp.where(cur_ok, bias, -1e30), jnp.where(both_ok, bias, -1e30)])


def _attn_consts(qnw, knw):
    eq, eq3t = _membership(ATTN_W, HEAD_DIM, ATTN_HEADS)
    ek, ek3t = _membership(KV_W, HEAD_DIM, ATTN_HEADS)
    return (jnp.tile(qnw, (1, ATTN_HEADS)), jnp.tile(knw, (1, KV_HEADS)), eq, eq3t, ek, ek3t)


def _attn_fwd(q, kv, bias, sinks, consts):
    s = q.shape[0]
    nb = s // BLOCK
    gq = GRP * BLOCK
    bias_t = bias.reshape(2, KV_HEADS, GRP, BLOCK, 2 * BLOCK).transpose(0, 1, 4, 2, 3).reshape(2, KV_HEADS, 2 * BLOCK, gq)
    sink_rows = jnp.repeat(sinks.reshape(KV_HEADS, GRP), BLOCK, axis=1).reshape(KV_HEADS, 1, gq)
    eye = jnp.eye(BLOCK, dtype=BF)

    def body(q_ref, kp_ref, kc_ref, vp_ref, vc_ref, b_ref, bt_ref, sk_ref, skr_ref, eye_ref,
             qw_ref, kw_ref, eq_ref, eq3_ref, ek_ref, ek3_ref, o_ref, lse_ref):
        qn = _bf(_heads_norm(q_ref[...], qw_ref[...], eq_ref[...], eq3_ref[...])[0] * (HEAD_DIM ** -0.5))
        kn = _bf(_heads_norm(jnp.concatenate([kp_ref[...], kc_ref[...]], axis=0), kw_ref[...], ek_ref[...], ek3_ref[...])[0])
        vv = _bf(jnp.concatenate([vp_ref[...], vc_ref[...]], axis=0))
        ones = jnp.ones((2 * BLOCK, HEAD_DIM), BF)
        kss = [slice(hk * HEAD_DIM, (hk + 1) * HEAD_DIM) for hk in range(KV_HEADS)]
        qgs = [_stack_heads(qn, hk) for hk in range(KV_HEADS)]
        sc_ts = [_dot_nt(kn[:, kss[hk]], qgs[hk]) + bt_ref[0, hk] for hk in range(KV_HEADS)]
        m_rows = [jnp.maximum(jnp.max(sc_ts[hk], axis=0, keepdims=True), skr_ref[hk]) for hk in range(KV_HEADS)]
        m_hq = _bf(jnp.concatenate([(m + jnp.abs(m) * (2.0 ** -7))[:, g * BLOCK:(g + 1) * BLOCK]
                                    for m in m_rows for g in range(GRP)], axis=0))
        m16 = _dot_nt(eye_ref[...], m_hq)
        ms = [_stack_cols(m16, hk) for hk in range(KV_HEADS)]
        scs = [_dot_nt(qgs[hk], kn[:, kss[hk]]) + b_ref[0, hk * GRP:(hk + 1) * GRP].reshape(gq, 2 * BLOCK)
               for hk in range(KV_HEADS)]
        ps = [_bf(jnp.exp(scs[hk] - ms[hk])) for hk in range(KV_HEADS)]
        pvs = [_dot(ps[hk], jnp.concatenate([vv[:, kss[hk]], ones], axis=1)) for hk in range(KV_HEADS)]
        den16 = jnp.concatenate([pvs[hk][g * BLOCK:(g + 1) * BLOCK, HEAD_DIM:HEAD_DIM + 1]
                                 for hk in range(KV_HEADS) for g in range(GRP)], axis=1)
        den16 = den16 + jnp.exp(sk_ref[...] - m16)
        lse_ref[...] = m16 + jnp.log(den16)
        inv16 = 1.0 / den16
        for hk in range(KV_HEADS):
            for g in range(GRP):
                h = hk * GRP + g
                o_ref[:, h * HEAD_DIM:(h + 1) * HEAD_DIM] = (pvs[hk][g * BLOCK:(g + 1) * BLOCK, :HEAD_DIM]
                                                             * inv16[:, h:h + 1])

    cur = lambda w, col=0: pl.BlockSpec((BLOCK, w), lambda i: (i, col))
    prev = lambda w, col=0: pl.BlockSpec((BLOCK, w), lambda i: (jnp.maximum(i - 1, 0), col))
    whole = lambda a: pl.BlockSpec(a.shape, lambda i: (0,) * a.ndim)
    first_or_not = lambda a: pl.BlockSpec((1,) + a.shape[1:], lambda i: (jnp.minimum(i, 1),) + (0,) * (a.ndim - 1))
    return pl.pallas_call(
        body, name="attn_fwd", grid=(nb,),
        in_specs=[cur(ATTN_W), prev(KV_W, 0), cur(KV_W, 0), prev(KV_W, 1), cur(KV_W, 1),
                  first_or_not(bias), first_or_not(bias_t),
                  whole(sinks), whole(sink_rows), whole(eye)] + [_full(c.shape) for c in consts],
        out_specs=[cur(ATTN_W), cur(ATTN_HEADS)],
        out_shape=[jax.ShapeDtypeStruct((s, ATTN_W), F32), jax.ShapeDtypeStruct((s, ATTN_HEADS), F32)],
        compiler_params=_params(dimension_semantics=("arbitrary",)),
    )(q, kv, kv, kv, kv, bias, bias_t, sinks, sink_rows, eye, *consts)


def _conv_taps(xbc, tail):
    ext = jnp.concatenate([tail, xbc], axis=0)
    return [pltpu.roll(ext, CONV_K - 1 - j, axis=0)[8:8 + BLOCK] if j < CONV_K - 1 else xbc for j in range(CONV_K)]


def _softplus(u):
    return jnp.maximum(u, 0.0) + jnp.log(1.0 + jnp.exp(-jnp.abs(u)))


def _tril():
    r = lax.broadcasted_iota(jnp.int32, (BLOCK, BLOCK), 0)
    c = lax.broadcasted_iota(jnp.int32, (BLOCK, BLOCK), 1)
    return r >= c


def _triu():
    r = lax.broadcasted_iota(jnp.int32, (BLOCK, BLOCK), 0)
    c = lax.broadcasted_iota(jnp.int32, (BLOCK, BLOCK), 1)
    return r <= c


def _exact_left(m01, a):
    hi = _bf(a)
    r1 = a - hi.astype(F32)
    mid = _bf(r1)
    return _dot(m01, hi) + _dot(m01, mid) + _dot(m01, _bf(r1 - mid.astype(F32)))


def _ssd_common(conv, dtr, dtb_ref, alog_ref, e3_ref):
    sg = _sig(conv)
    xact = conv * sg
    u = dtr + dtb_ref[...]
    dt = _softplus(u)
    a = -jnp.exp(alog_ref[...])
    trilb = _tril()
    acum = _exact_left(trilb.astype(BF), dt * a) * math.log2(math.e)
    both = _group_bcast(jnp.concatenate([dt, acum], axis=0), e3_ref[...])
    dt_x, acum_x = both[:BLOCK], both[BLOCK:]
    return sg, xact, u, dt, a, trilb, acum, dt_x, acum_x


SSD_CH = 2


def _ssd_fwd(xbc, dt_raw, conv_w, conv_b, dt_bias, a_log, dsk_x, e3t):
    s = xbc.shape[0]
    nc = s // BLOCK
    ch = SSD_CH if nc % SSD_CH == 0 else 1
    rows = ch * BLOCK

    def body(x_ref, tail_ref, dtr_ref, cw_ref, cb_ref, dtb_ref, alog_ref, dsk_ref, e3_ref,
             y_ref, hp_ref, conv_ref, hst, yd_s, yoff_s):
        i = pl.program_id(0)

        @pl.when(i == 0)
        def _():
            hst[...] = jnp.zeros_like(hst)

        for j in range(ch):
            rs = slice(j * BLOCK, (j + 1) * BLOCK)
            tail = jnp.where(i > 0, tail_ref[...], 0.0) if j == 0 else x_ref[j * BLOCK - 8:j * BLOCK, :]
            taps = _conv_taps(x_ref[rs, :], tail)
            conv = cb_ref[...] + sum(taps[t] * cw_ref[t:t + 1, :] for t in range(CONV_K))
            conv_ref[rs, :] = conv
            _, xact, _, _, _, trilb, acum, dt_x, acum_x = _ssd_common(conv, dtr_ref[rs, :], dtb_ref, alog_ref, e3_ref)
            xs = xact[:, :SSM_W]
            acum_t = acum.T
            ea_x = jnp.exp2(acum_x)
            last_x = acum_x[BLOCK - 1:BLOCK, :]
            xdt = xs * dt_x
            xw = xdt * jnp.exp2(last_x - acum_x)
            cd_x = jnp.exp2(last_x)
            hprev = hst[...]
            hp_ref[j] = hprev
            sls = [slice(g * SSM_R * SSM_P, (g + 1) * SSM_R * SSM_P) for g in range(SSM_G)]
            bgs = [_bf(xact[:, SSM_W + g * SSM_N:SSM_W + (g + 1) * SSM_N]) for g in range(SSM_G)]
            cgs = [_bf(xact[:, SSM_W + SSM_G * SSM_N + g * SSM_N:SSM_W + SSM_G * SSM_N + (g + 1) * SSM_N])
                   for g in range(SSM_G)]
            xdt_b, xw_b, hprev_b = _bf(xdt), _bf(xw), _bf(hprev)
            low_half = lax.broadcasted_iota(jnp.int32, (BLOCK, 2 * SSM_P), 1) < SSM_P
            cbs = [_dot_nt(cgs[g], bgs[g]) for g in range(SSM_G)]
            for g in range(SSM_G):
                sl = sls[g]
                yoff_s[:, sl] = _dot(cgs[g], hprev_b[:, sl]) * ea_x[:, sl]
                hst[:, sl] = hprev[:, sl] * cd_x[:, sl] + _dot_tn(bgs[g], xw_b[:, sl])
            for g in range(SSM_G):
                hss = [slice((g * SSM_R + r) * SSM_P, (g * SSM_R + r + 1) * SSM_P) for r in range(SSM_R)]
                mms = [_bf(cbs[g] * jnp.exp2(jnp.where(trilb, acum[:, g * SSM_R + r:g * SSM_R + r + 1]
                                                      - acum_t[g * SSM_R + r:g * SSM_R + r + 1, :], -1e30)))
                       for r in range(SSM_R)]
                for r in range(0, SSM_R, 2):
                    pair = slice(hss[r].start, hss[r + 1].stop)
                    xp = xdt_b[:, pair]
                    rhs = jnp.concatenate([jnp.where(low_half, xp, 0), jnp.where(low_half, 0, xp)], axis=0)
                    yd_s[:, pair] = _dot(jnp.concatenate([mms[r], mms[r + 1]], axis=1), rhs)
            y_ref[rs, :] = yd_s[...] + yoff_s[...] + dsk_ref[...] * xs

    blk = lambda w: pl.BlockSpec((rows, w), lambda i: (i, 0))
    return pl.pallas_call(
        body, name="ssd_fwd", grid=(nc // ch,),
        in_specs=[blk(XBC_W), pl.BlockSpec((8, XBC_W), lambda i: (jnp.maximum(i * (rows // 8) - 1, 0), 0)),
                  blk(SSM_HEADS), _full((CONV_K, XBC_W)), _full((1, XBC_W)), _full((1, SSM_HEADS)),
                  _full((1, SSM_HEADS)), _full((1, SSM_W)), _full((3 * SSM_HEADS, SSM_W))],
        out_specs=[blk(SSM_W), pl.BlockSpec((ch, SSM_N, SSM_W), lambda i: (i, 0, 0)), blk(XBC_W)],
        out_shape=[jax.ShapeDtypeStruct((s, SSM_W), F32), jax.ShapeDtypeStruct((nc, SSM_N, SSM_W), F32),
                   jax.ShapeDtypeStruct((s, XBC_W), F32)],
        scratch_shapes=[pltpu.VMEM((SSM_N, SSM_W), F32), pltpu.VMEM((BLOCK, SSM_W), F32), pltpu.VMEM((BLOCK, SSM_W), F32)],
        compiler_params=_params(dimension_semantics=("arbitrary",)),
    )(xbc, xbc, dt_raw, conv_w, conv_b, dt_bias, a_log, dsk_x, e3t)


def _dsilu(z, sg, silu):
    return sg * (1.0 + (z - silu))


def _mid(x, tgt, o_att, zam, ypre, gab, gate, ssm_nw, rows_all, tm=256):
    s = x.shape[0]
    gw = SSM_W // SSM_G

    r_ap, r_sp = ATTN_W // N_DEV, SSM_W // N_DEV

    def body(x_ref, t_ref, o_ref, zam_ref, yp_ref, gab_ref, gate_ref, nw_ref, rows_h,
             dout_ref, do_ref, dzam_ref, dyp_ref, dgab_ref,
             yag_ref, dya_ref, yn_ref, dyb_ref, mg_ref, dob_ref, gnw_ref, dgate_ref, loss_ref,
             wap_v, wsp_v, wout_v, sem):
        i = pl.program_id(0)

        @pl.when(i == 0)
        def _():
            cps = []
            for d in range(N_DEV):
                for j, (dst, r0, rn) in enumerate(((wap_v, 0, r_ap), (wsp_v, r_ap, r_sp), (wout_v, r_ap + r_sp, r_ap))):
                    cps.append(pltpu.make_async_copy(rows_h.at[d, r0:r0 + rn, :], dst.at[d * rn:(d + 1) * rn, :], sem.at[j]))
            for cp in cps:
                cp.start()
            gnw_ref[...] = jnp.zeros_like(gnw_ref)
            dgate_ref[...] = jnp.zeros_like(dgate_ref)
            loss_ref[...] = jnp.zeros_like(loss_ref)
            for cp in cps:
                cp.wait()

        gate = gate_ref[...]
        nw = nw_ref[...]
        o_att = o_ref[...]
        z_a = zam_ref[:, :ATTN_W].astype(F32)
        s_a = _sig(z_a)
        silu_a = z_a * s_a
        yag = _bf(o_att * silu_a)
        yag_ref[...] = yag
        ypre = yp_ref[...]
        z_m = zam_ref[:, ATTN_W:].astype(F32)
        s_m = _sig(z_m)
        silu_m = z_m * s_m
        yg = ypre * silu_m
        rinv = jnp.concatenate(
            [jnp.broadcast_to(lax.rsqrt(jnp.mean(yg[:, g * gw:(g + 1) * gw] ** 2, axis=-1, keepdims=True) + EPS), (tm, gw))
             for g in range(SSM_G)], axis=1)
        ynr = yg * rinv
        yn = _bf(ynr * nw)
        yn_ref[...] = yn
        y_a = _dot(yag, wap_v[...])
        y_b = _dot(yn, wsp_v[...])
        g_a = _sig(gab_ref[:, :D_MODEL].astype(F32))
        g_b = _sig(gab_ref[:, D_MODEL:].astype(F32))
        merged = _bf(g_a * y_a + g_b * y_b)
        mg_ref[...] = merged
        o = _dot(merged, wout_v[...])
        diff = x_ref[...] + gate * o - t_ref[...]
        loss_ref[...] += (0.5 / D_MODEL) * jnp.sum(diff * diff, axis=(0, 1), keepdims=True)
        dout = diff * (1.0 / D_MODEL)
        dout_ref[...] = dout
        dgate_ref[...] += jnp.sum(dout * o, axis=0, keepdims=True)
        d_o = _bf(dout * gate)
        dob_ref[...] = d_o
        dmerged = _dot_nt(d_o, wout_v[...])
        dy_af = dmerged * g_a
        dy_bf = dmerged * g_b
        dy_a = _bf(dy_af)
        dy_b = _bf(dy_bf)
        dya_ref[...] = dy_a
        dyb_ref[...] = dy_b
        dyag = _dot_nt(dy_a, wap_v[...])
        dyn = _dot_nt(dy_b, wsp_v[...])
        dgab_ref[:, :D_MODEL] = _bf(dy_af * y_a * (1.0 - g_a))
        dgab_ref[:, D_MODEL:] = _bf(dy_bf * y_b * (1.0 - g_b))
        do_ref[...] = dyag * silu_a
        dzam_ref[:, :ATTN_W] = _bf(dyag * o_att * _dsilu(z_a, s_a, silu_a))
        gnw_ref[...] += jnp.sum(dyn * ynr, axis=0, keepdims=True)
        dynw = dyn * nw
        corr = jnp.concatenate(
            [jnp.broadcast_to(jnp.mean((dynw * ynr)[:, g * gw:(g + 1) * gw], axis=-1, keepdims=True), (tm, gw))
             for g in range(SSM_G)], axis=1)
        dyg = rinv * (dynw - ynr * corr)
        dyp_ref[...] = dyg * silu_m
        dzam_ref[:, ATTN_W:] = _bf(dyg * ypre * _dsilu(z_m, s_m, silu_m))

    r1, r2, r3 = _rows(tm, D_MODEL), _rows(tm, SSM_W), _rows(tm, ATTN_W + SSM_W)
    sd = jax.ShapeDtypeStruct
    return pl.pallas_call(
        body, name="mid", grid=(s // tm,),
        in_specs=[r1, r1, r1, r3, r2, r2, _full((1, D_MODEL)), _full((1, SSM_W)), ANY],
        out_specs=[r1, r1, r3, r2, r2, r1, r1, r2, r1, r1, r1,
                   _full((1, SSM_W)), _full((1, D_MODEL)), _full((1, 1))],
        out_shape=[sd((s, D_MODEL), F32), sd((s, ATTN_W), F32), sd((s, ATTN_W + SSM_W), BF), sd((s, SSM_W), F32),
                   sd((s, 2 * D_MODEL), BF),
                   sd((s, ATTN_W), BF), sd((s, D_MODEL), BF), sd((s, SSM_W), BF), sd((s, D_MODEL), BF),
                   sd((s, D_MODEL), BF), sd((s, D_MODEL), BF),
                   sd((1, SSM_W), F32), sd((1, D_MODEL), F32), sd((1, 1), F32)],
        scratch_shapes=[pltpu.VMEM((ATTN_W, D_MODEL), BF), pltpu.VMEM((SSM_W, D_MODEL), BF), pltpu.VMEM((D_MODEL, D_MODEL), BF),
                        pltpu.SemaphoreType.DMA((3,))],
        compiler_params=_params(dimension_semantics=("arbitrary",)),
    )(x, tgt, o_att, zam, ypre, gab, gate, ssm_nw, rows_all)


def _attn_bwd(q, kv, bias, sinks, consts, o_att, lse, d_o):
    s = q.shape[0]
    nb = s // BLOCK
    folds = (_fold(ATTN_W, HEAD_DIM), _fold(KV_W, HEAD_DIM))

    def body(q_ref, kp_ref, kc_ref, vp_ref, vc_ref, b_ref, skv_ref, qw_ref, kw_ref, eq_ref, eq3_ref, ek_ref, ek3_ref,
             fq_ref, fk_ref, o_ref, lse_ref, do_ref,
             dq_ref, dkv_ref, dss_ref, gqw_ref, gkw_ref, gsk_ref, ckn, cv, dqn_s, dkn_s, dv_s, gq_x, gk_x):
        i = pl.program_id(0)
        kw, ek, ek3 = kw_ref[...], ek_ref[...], ek3_ref[...]

        @pl.when(i == 0)
        def _():
            for ref in (ckn, cv, dss_ref, gq_x, gk_x, gsk_ref):
                ref[...] = jnp.zeros_like(ref)

        @pl.when(i < nb)
        def _():
            qw, eq, eq3 = qw_ref[...], eq_ref[...], eq3_ref[...]
            qf = q_ref[...]
            qnf, rq_x = _heads_norm(qf, qw, eq, eq3)
            qn = _bf(qnf * (HEAD_DIM ** -0.5))
            kf = jnp.concatenate([kp_ref[...], kc_ref[...]], axis=0)
            knf, rk_x = _heads_norm(kf, kw, ek, ek3)
            kn = _bf(knf)
            vv = _bf(jnp.concatenate([vp_ref[...], vc_ref[...]], axis=0))
            d_of = do_ref[...]
            d_ob = _bf(d_of)
            lse_all = lse_ref[...]
            delta = _dot(_bf(d_of * o_ref[...]), eq)
            gsk_ref[...] += jnp.sum(-jnp.exp(skv_ref[...] - lse_all) * delta, axis=0, keepdims=True)
            kss = [slice(hk * HEAD_DIM, (hk + 1) * HEAD_DIM) for hk in range(KV_HEADS)]
            qgs = [_stack_heads(qn, hk) for hk in range(KV_HEADS)]
            d_ogs = [_stack_heads(d_ob, hk) for hk in range(KV_HEADS)]
            scs = [_dot_nt(qgs[hk], kn[:, kss[hk]]) + b_ref[0, hk * GRP:(hk + 1) * GRP].reshape(GRP * BLOCK, 2 * BLOCK)
                   for hk in range(KV_HEADS)]
            dps = [_dot_nt(d_ogs[hk], vv[:, kss[hk]]) for hk in range(KV_HEADS)]
            ps = [jnp.exp(scs[hk] - _stack_cols(lse_all, hk)) for hk in range(KV_HEADS)]
            dss = [ps[hk] * (dps[hk] - _stack_cols(delta, hk)) for hk in range(KV_HEADS)]
            pbs = [_bf(p) for p in ps]
            dsbs = [_bf(ds) for ds in dss]
            for hk in range(KV_HEADS):
                dss_ref[hk * GRP:(hk + 1) * GRP] += dss[hk].reshape(GRP, BLOCK, 2 * BLOCK)
            for hk in range(KV_HEADS):
                dv_s[:, kss[hk]] = _dot_tn(pbs[hk], d_ogs[hk])
                dkn_s[:, kss[hk]] = _dot_tn(dsbs[hk], qgs[hk])
            dqns = [_dot(dsbs[hk], kn[:, kss[hk]]) * (HEAD_DIM ** -0.5) for hk in range(KV_HEADS)]
            for hk in range(KV_HEADS):
                for g in range(GRP):
                    h = hk * GRP + g
                    dqn_s[:, h * HEAD_DIM:(h + 1) * HEAD_DIM] = dqns[hk][g * BLOCK:(g + 1) * BLOCK]
            dq, gq = _heads_norm_bwd(qf, rq_x, qw, dqn_s[...], eq, eq3)
            dq_ref[...] = _bf(dq)
            gq_x[...] += gq
            dk, gk = _heads_norm_bwd(kf[:BLOCK], rk_x[:BLOCK], kw, ckn[...] + dkn_s[0:BLOCK, :], ek, ek3)
            dkv_ref[:, :KV_W] = _bf(dk)
            gk_x[...] += gk
            dkv_ref[:, KV_W:] = _bf(cv[...] + dv_s[0:BLOCK, :])
            ckn[...] = dkn_s[BLOCK:2 * BLOCK, :]
            cv[...] = dv_s[BLOCK:2 * BLOCK, :]

        @pl.when(i == nb)
        def _():
            kc = kc_ref[...]
            dk, gk = _heads_norm_bwd(kc, _heads_norm(kc, kw, ek, ek3)[1], kw, ckn[...], ek, ek3)
            dkv_ref[:, :KV_W] = _bf(dk)
            dkv_ref[:, KV_W:] = _bf(cv[...])
            gqw_ref[...] = _group_sum(jnp.broadcast_to(gq_x[...], (8, ATTN_W)), fq_ref[...])[0:1]
            gkw_ref[...] = _group_sum(jnp.broadcast_to(gk_x[...] + gk, (8, KV_W)), fk_ref[...])[0:1]

    last = nb - 1
    cur = lambda w, col=0: pl.BlockSpec((BLOCK, w), lambda i: (jnp.minimum(i, last), col))
    prev = lambda w, col=0: pl.BlockSpec((BLOCK, w), lambda i: (jnp.maximum(jnp.minimum(i, last) - 1, 0), col))
    late = lambda w: pl.BlockSpec((BLOCK, w), lambda i: (jnp.maximum(i - 1, 0), 0))
    sd = jax.ShapeDtypeStruct
    return pl.pallas_call(
        body, name="attn_bwd", grid=(nb + 1,),
        in_specs=[cur(ATTN_W), prev(KV_W, 0), cur(KV_W, 0), prev(KV_W, 1), cur(KV_W, 1),
                  pl.BlockSpec((1, ATTN_HEADS, BLOCK, 2 * BLOCK), lambda i: (jnp.minimum(i, 1), 0, 0, 0)),
                  _full((1, ATTN_HEADS))]
                 + [_full(c.shape) for c in consts + folds] + [cur(ATTN_W), cur(ATTN_HEADS), cur(ATTN_W)],
        out_specs=[cur(ATTN_W), late(2 * KV_W),
                   pl.BlockSpec((ATTN_HEADS, BLOCK, 2 * BLOCK), lambda i: (0, 0, 0)),
                   _full((1, HEAD_DIM)), _full((1, HEAD_DIM)), _full((1, ATTN_HEADS))],
        out_shape=[sd((s, ATTN_W), BF), sd((s, 2 * KV_W), BF),
                   sd((ATTN_HEADS, BLOCK, 2 * BLOCK), F32), sd((1, HEAD_DIM), F32), sd((1, HEAD_DIM), F32),
                   sd((1, ATTN_HEADS), F32)],
        scratch_shapes=[pltpu.VMEM((BLOCK, KV_W), F32), pltpu.VMEM((BLOCK, KV_W), F32),
                        pltpu.VMEM((BLOCK, ATTN_W), F32), pltpu.VMEM((2 * BLOCK, KV_W), F32),
                        pltpu.VMEM((2 * BLOCK, KV_W), F32), pltpu.VMEM((1, ATTN_W), F32), pltpu.VMEM((1, KV_W), F32)],
        compiler_params=_params(dimension_semantics=("arbitrary",)),
    )(q, kv, kv, kv, kv, bias, sinks, *consts, *folds, o_att, lse, d_o)


def _ssd_bwd(xbc, conv_all, dt_raw, conv_w, dt_bias, a_log, dsk_x, e_mat, e3t, hprev_all, dy_all):
    s = xbc.shape[0]
    nc = s // BLOCK
    ch = 1
    rows = ch * BLOCK
    nsteps = nc // ch
    gw = SSM_R * SSM_P
    b0, c0 = SSM_W, SSM_W + SSM_G * SSM_N

    def body(x_ref, conv_ref, dtr_ref, cw_ref, dtb_ref, alog_ref, dsk_ref, e_ref, e3_ref, hp_ref, dy_ref,
             dx_ref, ddt_ref, gcw_ref, gcb_ref, gdtb_ref, galog_ref, gdsk_ref,
             dh, nhead, gdskx, dxdt_s, dbc_s, dxd_s):
        def chunk_bwd(j):
            rs = slice(j * BLOCK, (j + 1) * BLOCK)
            conv = conv_ref[rs, :]
            sg, xact, u, dt, a, trilb, acum, dt_x, acum_x = _ssd_common(conv, dtr_ref[rs, :], dtb_ref, alog_ref, e3_ref)
            xs = xact[:, :SSM_W]
            acum_t = acum.T
            ea_x = jnp.exp2(acum_x)
            last_x = acum_x[BLOCK - 1:BLOCK, :]
            dte_x = jnp.exp2(last_x - acum_x)
            cd_x = jnp.exp2(last_x)
            xdt = xs * dt_x
            xw = xdt * dte_x
            hprev = hp_ref[j]
            dhn = dh[...]
            dy = dy_ref[rs, :]
            gdskx[...] += jnp.sum(dy * xs, axis=0, keepdims=True)
            dyea = dy * ea_x
            lane = lax.broadcasted_iota(jnp.int32, (BLOCK, SSM_HEADS), 1)
            dacum = jnp.zeros((BLOCK, SSM_HEADS), F32)
            dacc_x, dlast_x = [], []
            sls = [slice(g * gw, (g + 1) * gw) for g in range(SSM_G)]
            bgs = [_bf(xact[:, b0 + g * SSM_N:b0 + (g + 1) * SSM_N]) for g in range(SSM_G)]
            cgs = [_bf(xact[:, c0 + g * SSM_N:c0 + (g + 1) * SSM_N]) for g in range(SSM_G)]
            hpgs = [_bf(hprev[:, sl]) for sl in sls]
            dhgs = [_bf(dhn[:, sl]) for sl in sls]
            dyeags = [_bf(dyea[:, sl]) for sl in sls]
            xwgs = [_bf(xw[:, sl]) for sl in sls]
            xdt_b, dy_b = _bf(xdt), _bf(dy)
            low_half = lax.broadcasted_iota(jnp.int32, (BLOCK, 2 * SSM_P), 1) < SSM_P
            cbs = [_dot_nt(cgs[g], bgs[g]) for g in range(SSM_G)]
            gmats = [_dot(cgs[g], hpgs[g]) for g in range(SSM_G)]
            dxws = [_dot(bgs[g], dhgs[g]) for g in range(SSM_G)]
            dcgs = [_dot_nt(dyeags[g], hpgs[g]) for g in range(SSM_G)]
            dbgs = [_dot_nt(xwgs[g], dhgs[g]) for g in range(SSM_G)]
            for g in range(SSM_G):
                sl = sls[g]
                dh[:, sl] = dhn[:, sl] * cd_x[:, sl] + _dot_tn(cgs[g], dyeags[g])
                dxdt_s[:, sl] = dxws[g] * dte_x[:, sl]
                dacc_x.append(dy[:, sl] * gmats[g] * ea_x[:, sl] - dxws[g] * xw[:, sl])
                dlast_x.append(jnp.sum(dxws[g] * xw[:, sl], axis=0, keepdims=True)
                               + jnp.sum(dhn[:, sl] * hprev[:, sl], axis=0, keepdims=True) * cd_x[:, sl])
            for g in range(SSM_G):
                bg, cg, cb, dbg, dcg = bgs[g], cgs[g], cbs[g], dbgs[g], dcgs[g]
                hss = [slice((g * SSM_R + r) * SSM_P, (g * SSM_R + r + 1) * SSM_P) for r in range(SSM_R)]
                lms = [jnp.exp2(jnp.where(trilb, acum[:, g * SSM_R + r:g * SSM_R + r + 1]
                                         - acum_t[g * SSM_R + r:g * SSM_R + r + 1, :], -1e30)) for r in range(SSM_R)]
                mms = [cb * lm for lm in lms]
                mmbs = [_bf(mm) for mm in mms]
                dms = []
                for r in range(0, SSM_R, 2):
                    pair = slice(hss[r].start, hss[r + 1].stop)
                    xp, dyp = xdt_b[:, pair], dy_b[:, pair]
                    dmp = _dot_nt(dyp, jnp.concatenate([jnp.where(low_half, xp, 0), jnp.where(low_half, 0, xp)], axis=0))
                    dms += [dmp[:, :BLOCK], dmp[:, BLOCK:]]
                    dxd_s[:, pair] = _dot_tn(jnp.concatenate([mmbs[r], mmbs[r + 1]], axis=0),
                                             jnp.concatenate([jnp.where(low_half, dyp, 0), jnp.where(low_half, 0, dyp)], axis=0))
                dcb = sum(dms[r] * lms[r] for r in range(SSM_R))
                wms = [dms[r] * mms[r] for r in range(SSM_R)]
                antis = [_bf(wm - wm.T) for wm in wms]
                for r in range(SSM_R):
                    dacum = dacum + _dot(antis[r], (lane == g * SSM_R + r).astype(BF))
                dcbb = _bf(dcb)
                dbc_s[:, g * SSM_N:(g + 1) * SSM_N] = dbg + _dot_tn(dcbb, cg)
                dbc_s[:, SSM_G * SSM_N + g * SSM_N:SSM_G * SSM_N + (g + 1) * SSM_N] = dcg + _dot(dcbb, bg)
            dxdt = dxdt_s[...] + dxd_s[...]
            dxs = dy * dsk_ref[...] + dxdt * dt_x
            red = _group_sum(jnp.concatenate(
                [dxdt * xs, jnp.concatenate(dacc_x, axis=1),
                 jnp.broadcast_to(jnp.concatenate(dlast_x, axis=1), (8, SSM_W))], axis=0), e_ref[...])
            row = lax.broadcasted_iota(jnp.int32, (BLOCK, SSM_HEADS), 0)
            dacum = dacum + red[BLOCK:2 * BLOCK] + jnp.where(row == BLOCK - 1, red[2 * BLOCK:2 * BLOCK + 1], 0.0)
            ddta = _exact_left(_triu().astype(BF), dacum)
            ddt = red[:BLOCK] + ddta * a
            galog_ref[...] += jnp.sum(ddta * dt, axis=0, keepdims=True) * a
            du = ddt * _sig(u)
            ddt_ref[rs, :] = _bf(du)
            gdtb_ref[...] += jnp.sum(du, axis=0, keepdims=True)
            dconv = jnp.concatenate([dxs, dbc_s[...]], axis=1) * _dsilu(conv, sg, xact)
            gcb_ref[...] += jnp.sum(dconv, axis=0, keepdims=True)
            ext2 = jnp.concatenate([dconv, nhead[...]], axis=0)
            ahead = [pltpu.roll(ext2, BLOCK + 8 - (CONV_K - 1 - j), axis=0)[0:BLOCK] if j < CONV_K - 1 else dconv
                     for j in range(CONV_K)]
            dx_ref[rs, :] = _bf(sum(ahead[j] * cw_ref[j:j + 1, :] for j in range(CONV_K)))
            xraw = x_ref[rs, :]
            gcw_ref[...] += jnp.concatenate([jnp.sum(ahead[j] * xraw, axis=0, keepdims=True) for j in range(CONV_K)], axis=0)
            nhead[...] = dconv[0:8]

        i = pl.program_id(0)

        @pl.when(i == 0)
        def _():
            for ref in (dh, nhead, gdskx, gcw_ref, gcb_ref, gdtb_ref, galog_ref, gdsk_ref):
                ref[...] = jnp.zeros_like(ref)

        for j in reversed(range(ch)):
            chunk_bwd(j)

        @pl.when(i == nsteps - 1)
        def _():
            gdsk_ref[...] = _group_sum(jnp.broadcast_to(gdskx[...], (8, SSM_W)), e_ref[...])[0:1]

    chunk = lambda w: pl.BlockSpec((rows, w), lambda i: (nsteps - 1 - i, 0))
    sd = jax.ShapeDtypeStruct
    return pl.pallas_call(
        body, name="ssd_bwd", grid=(nsteps,),
        in_specs=[chunk(XBC_W), chunk(XBC_W),
                  chunk(SSM_HEADS), _full((CONV_K, XBC_W)), _full((1, SSM_HEADS)),
                  _full((1, SSM_HEADS)), _full((1, SSM_W)), _full((SSM_W, SSM_HEADS)), _full((3 * SSM_HEADS, SSM_W)),
                  pl.BlockSpec((ch, SSM_N, SSM_W), lambda i: (nsteps - 1 - i, 0, 0)), chunk(SSM_W)],
        out_specs=[chunk(XBC_W), chunk(SSM_HEADS), _full((CONV_K, XBC_W)), _full((1, XBC_W)),
                   _full((1, SSM_HEADS)), _full((1, SSM_HEADS)), _full((1, SSM_HEADS))],
        out_shape=[sd((s, XBC_W), BF), sd((s, SSM_HEADS), BF), sd((CONV_K, XBC_W), F32), sd((1, XBC_W), F32),
                   sd((1, SSM_HEADS), F32), sd((1, SSM_HEADS), F32), sd((1, SSM_HEADS), F32)],
        scratch_shapes=[pltpu.VMEM((SSM_N, SSM_W), F32), pltpu.VMEM((8, XBC_W), F32),
                        pltpu.VMEM((1, SSM_W), F32), pltpu.VMEM((BLOCK, SSM_W), F32),
                        pltpu.VMEM((BLOCK, 2 * SSM_G * SSM_N), F32), pltpu.VMEM((BLOCK, SSM_W), F32)],
        compiler_params=_params(dimension_semantics=("arbitrary",)),
    )(xbc, conv_all, dt_raw, conv_w, dt_bias, a_log, dsk_x, e_mat, e3t, hprev_all, dy_all)


def _dh(x, dout, norm_w, scale, dsegs, w_t, tm=256):
    s = x.shape[0]

    def body(x_ref, dout_ref, nw_ref, sc_ref, *rest):
        d_refs, w_hbm = rest[:NSEG], rest[NSEG]
        gx_ref, dshift_ref, dscale_ref, gnw_ref = rest[NSEG + 1:NSEG + 5]
        w_vm, sem = rest[NSEG + 5], rest[NSEG + 6]
        first = pl.program_id(0) == 0
        cps = [pltpu.make_async_copy(w_hbm.at[SEG_OFF[j]:SEG_OFF[j + 1], :], w_vm.at[SEG_OFF[j]:SEG_OFF[j + 1], :], sem.at[j])
               for j in range(NSEG)]

        def tile(waiting):
            dh = None
            for j in range(NSEG):
                if waiting:
                    cps[j].wait()
                part = _dot(d_refs[j][...], w_vm[SEG_OFF[j]:SEG_OFF[j + 1], :])
                dh = part if dh is None else dh + part
            xv = x_ref[...]
            r = lax.rsqrt(jnp.mean(xv * xv, axis=-1, keepdims=True) + EPS)
            xn = xv * r
            nw = nw_ref[...]
            sc1 = 1.0 + sc_ref[...]
            dshift_ref[...] += jnp.sum(dh, axis=0, keepdims=True)
            dhxn = jnp.sum(dh * xn, axis=0, keepdims=True)
            dscale_ref[...] += dhxn * nw
            gnw_ref[...] += dhxn * sc1
            dxn = dh * (nw * sc1)
            gx_ref[...] = dout_ref[...] + r * (dxn - xn * jnp.mean(xn * dxn, axis=-1, keepdims=True))

        @pl.when(first)
        def _():
            for cp in cps:
                cp.start()
            for ref in (dshift_ref, dscale_ref, gnw_ref):
                ref[...] = jnp.zeros_like(ref)
            tile(True)

        @pl.when(jnp.logical_not(first))
        def _():
            tile(False)

    vec = _full((1, D_MODEL))
    sd = jax.ShapeDtypeStruct
    return pl.pallas_call(
        body, name="dh", grid=(s // tm,),
        in_specs=[_rows(tm, D_MODEL), _rows(tm, D_MODEL), vec, vec] + [_rows(tm, w) for w in SEG_W] + [ANY],
        out_specs=[_rows(tm, D_MODEL), vec, vec, vec],
        out_shape=[sd((s, D_MODEL), F32), sd((1, D_MODEL), F32), sd((1, D_MODEL), F32), sd((1, D_MODEL), F32)],
        scratch_shapes=[pltpu.VMEM((IN_W, D_MODEL), BF), pltpu.SemaphoreType.DMA((NSEG,))],
        compiler_params=_params(dimension_semantics=("arbitrary",)),
    )(x, dout, norm_w, scale, *dsegs, w_t)


def _gw_seg(h, dseg, name, tm=2048):
    s, w = dseg.shape
    tn = w if w <= 2048 else w // 2
    tm = min(tm, s)
    nm = s // tm

    def body(h_ref, d_ref, o_ref, acc):
        m = pl.program_id(1)

        @pl.when(m == 0)
        def _():
            acc[...] = jnp.zeros_like(acc)

        acc[...] += _dot_tn(d_ref[...], h_ref[...])

        @pl.when(m == nm - 1)
        def _():
            o_ref[...] = _bf(acc[...])

    return pl.pallas_call(
        body, name=name, grid=(w // tn, nm),
        in_specs=[pl.BlockSpec((tm, D_MODEL), lambda n, m: (m, 0)), pl.BlockSpec((tm, tn), lambda n, m: (m, n))],
        out_specs=pl.BlockSpec((tn, D_MODEL), lambda n, m: (n, 0)),
        out_shape=jax.ShapeDtypeStruct((w, D_MODEL), BF),
        scratch_shapes=[pltpu.VMEM((tn, D_MODEL), F32)],
        compiler_params=_params(dimension_semantics=("arbitrary", "arbitrary")),
    )(h, dseg)


def _gw_in(h, dsegs):
    return [_gw_seg(h, d, "gw_in_%d" % j) for j, d in enumerate(dsegs)]


def _local_step(x, tgt, shift, scale, gate, w_t, rows_fn, norm_w, qnw, knw, rel_bias, sinks,
                conv_w, conv_b, dt_bias, a_log, d_skip, ssm_nw, after_mid=None, after_gw=None):
    oh_t = _bucket_onehot_t()
    bias = _masked_bias(_bias_dense(rel_bias.T, oh_t).reshape(ATTN_HEADS, BLOCK, 2 * BLOCK))
    *segs, h = _inproj(x, norm_w, scale, shift, w_t)
    q, kv, zam, xbc, dtr, gab = segs
    consts = _attn_consts(qnw, knw)
    o_att, lse = _attn_fwd(q, kv, bias, sinks, consts)
    e_mat, e3t = _membership(SSM_W, SSM_P, SSM_HEADS)
    dsk_x = jnp.repeat(d_skip, SSM_P, axis=1)
    ypre, hprev, conv = _ssd_fwd(xbc, dtr, conv_w, conv_b, dt_bias, a_log, dsk_x, e3t)
    (dout, d_o, dzam, dyp, dgab, yag, dy_a, yn, dy_b, merged, dob, g_ssm_nw, dgate, loss) = _mid(
        x, tgt, o_att, zam, ypre, gab, gate, ssm_nw, rows_fn(ypre))
    g_wap = _gw_seg(dy_a, yag, "gw_attn_proj")
    g_wsp = _gw_seg(dy_b, yn, "gw_ssm_proj")
    g_wout = _gw_seg(dob, merged, "gw_out")
    zero = after_mid(g_wap, g_wsp, g_wout) if after_mid is not None else 0.0
    dq, dkv, dss, g_qnw, g_knw, g_sinks = _attn_bwd(q, kv, bias, sinks + zero, consts, o_att, lse, d_o)
    g_rel = _bias_grad(dss.reshape(ATTN_HEADS, BLOCK * 2 * BLOCK), oh_t).T
    dxbc, ddt, g_cw, g_cb, g_dtb, g_alog, g_dsk = _ssd_bwd(
        xbc, conv, dtr, conv_w, dt_bias, a_log, dsk_x, e_mat, e3t, hprev, dyp)
    dsegs = (dq, dkv, dzam, dxbc, ddt, dgab)
    g_ws = _gw_in(h, dsegs)
    zero = after_gw(g_ws) if after_gw is not None else 0.0
    gx, dshift, dscale, g_nw = _dh(x, dout, norm_w + zero, scale, dsegs, w_t)
    return dict(loss=loss, grad_x=gx, dmod=jnp.concatenate([dshift, dscale, dgate], axis=1), g_ws=g_ws,
                g_wap=g_wap, g_wsp=g_wsp, g_wout=g_wout, g_norm_w=g_nw, g_qnw=g_qnw, g_knw=g_knw, g_rel=g_rel,
                g_sinks=g_sinks, g_conv_w=g_cw, g_conv_b=g_cb, g_dt_bias=g_dtb, g_a_log=g_alog, g_d_skip=g_dsk,
                g_ssm_nw=g_ssm_nw)


def _me():
    return lax.axis_index("x"), lax.axis_index("y"), lax.axis_index("c")


def _flip(v, bit):
    return 1 - v if bit else v


def _ag_direct(v, name):
    def body(v_ref, out_ref, send_sems, recv_sems, local_sem):
        x, y, c = _me()
        me = 4 * x + 2 * y + c
        mine = pltpu.make_async_copy(v_ref, out_ref.at[me], local_sem)
        mine.start()
        peers = [(_flip(x, k >> 2 & 1), _flip(y, k >> 1 & 1), _flip(c, k & 1)) for k in range(1, N_DEV)]
        sends = [pltpu.make_async_remote_copy(
            src_ref=v_ref, dst_ref=out_ref.at[me], send_sem=send_sems.at[j], recv_sem=recv_sems.at[j],
            device_id=p, device_id_type=MESH) for j, p in enumerate(peers)]
        for cp in sends:
            cp.start()
        for j, (px, py, pc) in enumerate(peers):
            pltpu.make_async_remote_copy(
                src_ref=v_ref, dst_ref=out_ref.at[4 * px + 2 * py + pc], send_sem=send_sems.at[j],
                recv_sem=recv_sems.at[j], device_id=(px, py, pc), device_id_type=MESH).wait_recv()
        for cp in sends:
            cp.wait_send()
        mine.wait()

    vm = pl.BlockSpec(memory_space=pltpu.VMEM)
    return pl.pallas_call(
        body, name=name, out_shape=jax.ShapeDtypeStruct((N_DEV,) + v.shape, v.dtype),
        in_specs=[vm], out_specs=vm,
        scratch_shapes=[pltpu.SemaphoreType.DMA((N_DEV - 1,)), pltpu.SemaphoreType.DMA((N_DEV - 1,)),
                        pltpu.SemaphoreType.DMA],
        compiler_params=_params(),
    )(v)


def _gather_mod(v, w_ada, b_piece):
    ncols = w_ada.shape[1]

    def body(v_ref, w_ref, b_ref, rows_ref, mods_ref, piece, send_sems, recv_sems, local_sems):
        x, y, c = _me()
        me = 4 * x + 2 * y + c
        peers = _peers(x, y, c)

        def exchange(src, dst, rnd):
            mine = pltpu.make_async_copy(src, dst.at[me], local_sems.at[rnd])
            mine.start()
            sends = [pltpu.make_async_remote_copy(
                src_ref=src, dst_ref=dst.at[me], send_sem=send_sems.at[rnd, j], recv_sem=recv_sems.at[rnd, j],
                device_id=p, device_id_type=MESH) for j, p in enumerate(peers)]
            for cp in sends:
                cp.start()
            for j, (px, py, pc) in enumerate(peers):
                pltpu.make_async_remote_copy(
                    src_ref=src, dst_ref=dst.at[4 * px + 2 * py + pc], send_sem=send_sems.at[rnd, j],
                    recv_sem=recv_sems.at[rnd, j], device_id=(px, py, pc), device_id_type=MESH).wait_recv()
            for cp in sends:
                cp.wait_send()
            mine.wait()

        exchange(v_ref, rows_ref, 0)
        c_all = rows_ref[:, 0, :D_MODEL]
        piece[...] = _dot(_bf(_silu(c_all)), _bf(w_ref[...])) + b_ref[...]
        exchange(piece, mods_ref, 1)

    vm = pl.BlockSpec(memory_space=pltpu.VMEM)
    return pl.pallas_call(
        body, name="gather_mod",
        out_shape=(jax.ShapeDtypeStruct((N_DEV,) + v.shape, F32), jax.ShapeDtypeStruct((N_DEV, N_DEV, ncols), F32)),
        in_specs=[vm, vm, vm], out_specs=(vm, vm),
        scratch_shapes=[pltpu.VMEM((N_DEV, ncols), F32), pltpu.SemaphoreType.DMA((2, N_DEV - 1)),
                        pltpu.SemaphoreType.DMA((2, N_DEV - 1)), pltpu.SemaphoreType.DMA((2,))],
        compiler_params=_params(),
    )(v, w_ada, b_piece)


def _ag_relayed(v, name, chunks=1):
    rows = v.shape[0] // chunks
    assert rows * chunks == v.shape[0] and rows % 8 == 0

    def body(v_ref, out_ref, token, send_sems, recv_sems, local_sem):
        token[...] = jnp.zeros_like(token)
        x, y, c = _me()
        flip_x, flip_y = 1 - x, 1 - y
        ax, ay = c * x + (1 - c) * flip_x, c * flip_y + (1 - c) * y
        bx, by = c * flip_x + (1 - c) * x, c * y + (1 - c) * flip_y
        me, sib = (x, y, c), (x, y, 1 - c)
        a, b, dg = (ax, ay, c), (bx, by, c), (flip_x, flip_y, c)
        sa, sb, sdg = (bx, by, 1 - c), (ax, ay, 1 - c), (flip_x, flip_y, 1 - c)

        def piece(ref, k):
            return ref.at[pl.ds(k * rows, rows), :]

        def slot(px, py, pc):
            return out_ref.at[4 * px + 2 * py + pc]

        def copy(n, k, block, to, src=None):
            return pltpu.make_async_remote_copy(
                src_ref=piece(slot(*block) if src is None else src, k), dst_ref=piece(slot(*block), k),
                send_sem=send_sems.at[n * chunks + k], recv_sem=recv_sems.at[n * chunks + k],
                device_id=to, device_id_type=MESH)

        mine = pltpu.make_async_copy(v_ref, slot(*me), local_sem)
        mine.start()
        started = [copy(n, k, me, to, src=v_ref) for k in range(chunks) for n, to in ((1, a), (2, b), (0, sib))]
        for cp in started:
            cp.start()

        def arrived(n, k, block, then):
            copy(n, k, block, me).wait_recv()
            for n2, to in then:
                started.append(copy(n2, k, block, to))
                started[-1].start()

        for k in range(chunks):
            arrived(1, k, a, ((3, b), (4, sib)))
            arrived(2, k, b, ((5, sib),))
        for k in range(chunks):
            arrived(3, k, dg, ((6, sib),))
        for k in range(chunks):
            for n, block in ((0, sib), (4, sa), (5, sb), (6, sdg)):
                copy(n, k, block, me).wait_recv()
        for cp in started:
            cp.wait_send()
        mine.wait()

    out, token = pl.pallas_call(
        body, name=name,
        out_shape=(jax.ShapeDtypeStruct((N_DEV,) + v.shape, v.dtype), jax.ShapeDtypeStruct((8, 128), v.dtype)),
        in_specs=[ANY], out_specs=(ANY, pl.BlockSpec(memory_space=pltpu.VMEM)),
        scratch_shapes=[pltpu.SemaphoreType.DMA((7 * chunks,)), pltpu.SemaphoreType.DMA((7 * chunks,)),
                        pltpu.SemaphoreType.DMA],
        compiler_params=_params(),
    )(v)
    return out, token[0:1, 0:1]


HBM = pl.BlockSpec(memory_space=pltpu.HBM)
SEM = pl.BlockSpec(memory_space=pltpu.SEMAPHORE)
EFFECT = pltpu.SideEffectType.DATAFLOW_SIDE_EFFECTING


def _peers(x, y, c):
    return [(_flip(x, k >> 2 & 1), _flip(y, k >> 1 & 1), _flip(c, k & 1)) for k in range(1, N_DEV)]


def _exchange_start(src, land, gather, name):
    def body(src_ref, land_ref, send_sems, recv_sems, src_thru, land_thru, token):
        x, y, c = _me()
        me = 4 * x + 2 * y + c
        for j, (px, py, pc) in enumerate(_peers(x, y, c)):
            pltpu.make_async_remote_copy(
                src_ref=src_ref if gather else src_ref.at[4 * px + 2 * py + pc], dst_ref=land_ref.at[me],
                send_sem=send_sems.at[j], recv_sem=recv_sems.at[j], device_id=(px, py, pc), device_id_type=MESH).start()
        token[...] = jnp.zeros_like(token)

    sems = pltpu.SemaphoreType.DMA((N_DEV - 1,))
    out = pl.pallas_call(
        body, name=name,
        out_shape=(sems, sems, pltpu.HBM(src.shape, src.dtype), pltpu.HBM(land.shape, land.dtype),
                   jax.ShapeDtypeStruct((8, 128), F32)),
        in_specs=(HBM, HBM), out_specs=(SEM, SEM, HBM, HBM, pl.BlockSpec(memory_space=pltpu.VMEM)),
        input_output_aliases={0: 2, 1: 3},
        compiler_params=pltpu.CompilerParams(has_side_effects=EFFECT),
    )(pltpu.with_memory_space_constraint(src, pltpu.HBM), pltpu.with_memory_space_constraint(land, pltpu.HBM))
    return out[:4], out[4][0, 0]


def _exchange_wait(started, after, gather, name):
    send_sems, recv_sems, src_thru, land_thru = started

    def body(src_ref, land_ref, send_sems, recv_sems, after_ref, src_dead, got_ref):
        x, y, c = _me()
        for j, (px, py, pc) in enumerate(_peers(x, y, c)):
            pid = 4 * px + 2 * py + pc
            cp = pltpu.make_async_remote_copy(
                src_ref=src_ref if gather else src_ref.at[pid], dst_ref=land_ref.at[pid],
                send_sem=send_sems.at[j], recv_sem=recv_sems.at[j], device_id=(px, py, pc), device_id_type=MESH)
            cp.wait_send()
            cp.wait_recv()

    return pl.pallas_call(
        body, name=name,
        out_shape=(pltpu.HBM(src_thru.shape, src_thru.dtype), pltpu.HBM(land_thru.shape, land_thru.dtype)),
        in_specs=(HBM, HBM, SEM, SEM, ANY), out_specs=(HBM, HBM), input_output_aliases={0: 0, 1: 1},
        compiler_params=pltpu.CompilerParams(has_side_effects=EFFECT),
    )(src_thru, land_thru, send_sems, recv_sems, after)[1]


def _silu(a):
    return a * _sig(a)


def _gw_ada(c_all, dmod_piece):
    def body(c_ref, d_ref, o_ref):
        o_ref[...] = _dot_tn(_bf(_silu(c_ref[...])), _bf(d_ref[...]))

    return pl.pallas_call(
        body, name="gw_ada", out_shape=jax.ShapeDtypeStruct((c_all.shape[1], dmod_piece.shape[1]), F32),
        compiler_params=_params(),
    )(c_all, dmod_piece)


def _adam(parts, w, m, v, name):
    k, r, n = parts.shape
    if r <= 256 or r % 256 == 0:
        tr, tn = min(r, 256), n
    else:
        tr, tn = r, 256
    assert r % tr == 0 and n % tn == 0

    def body(p_ref, w_ref, m_ref, v_ref, g_ref, d_ref, nm_ref, nv_ref):
        g = p_ref[0].astype(F32)
        for j in range(1, k):
            g = g + p_ref[j].astype(F32)
        g_ref[...] = g
        d_ref[...], nm_ref[...], nv_ref[...] = _adam_math(g, w_ref[...], m_ref[...], v_ref[...])

    blk = pl.BlockSpec((tr, tn), lambda i, j: (i, j))
    return pl.pallas_call(
        body, name=name, grid=(r // tr, n // tn),
        in_specs=[pl.BlockSpec((k, tr, tn), lambda i, j: (0, i, j)), blk, blk, blk],
        out_specs=[blk, blk, blk, blk],
        out_shape=[jax.ShapeDtypeStruct((r, n), F32)] * 4,
        compiler_params=_params(dimension_semantics=("arbitrary", "arbitrary"),
                                allow_input_fusion=[False, True, True, True]),
    )(parts, w, m, v)


def _adam_math(g, w, m, v):
    m_new = ADAM_B1 * m + (1.0 - ADAM_B1) * g
    v_new = ADAM_B2 * v + (1.0 - ADAM_B2) * jnp.square(g)
    m_hat = m_new / (1.0 - ADAM_B1 ** ADAM_STEP)
    v_hat = v_new / (1.0 - ADAM_B2 ** ADAM_STEP)
    return -ADAM_LR * (m_hat / (jnp.sqrt(v_hat) + ADAM_EPS) + ADAM_WD * w), m_new, v_new


_SMALL = (("b_ada", 3 * D_MODEL), ("norm_w", D_MODEL), ("q_norm_w", HEAD_DIM), ("k_norm_w", HEAD_DIM),
          ("rel_bias", REL_BUCKETS * ATTN_HEADS), ("sinks", ATTN_HEADS), ("conv_b", XBC_W), ("dt_bias", SSM_HEADS),
          ("a_log", SSM_HEADS), ("d_skip", SSM_HEADS), ("ssm_norm_w", SSM_W))
_SLOT = tuple(-(-n // 128) * 128 for _, n in _SMALL)
_SLOT_OFF = tuple(int(o) for o in np.cumsum((0,) + _SLOT))
_LOSS_OFF = _SLOT_OFF[-1]
_CW_OFF = _LOSS_OFF + 128
_PACK_N = _CW_OFF + CONV_K * XBC_W


def _pack_partials(small, loss, g_conv_w):
    parts = []
    for (name, n), slot in zip(_SMALL, _SLOT):
        parts.append(small[name].reshape(1, n))
        if slot > n:
            parts.append(jnp.zeros((1, slot - n), F32))
    parts += [loss.reshape(1, 1), jnp.zeros((1, 127), F32), g_conv_w.reshape(1, CONV_K * XBC_W)]
    return jnp.concatenate(parts, axis=1)


def _adam_small(pack_all, w, m, v):
    names = [name for name, _ in _SMALL]

    def body(p_ref, *rest):
        ins, outs = rest[:3 * len(names)], rest[3 * len(names):]

        def total(off, n):
            g = p_ref[0, :, off:off + n]
            for d in range(1, N_DEV):
                g = g + p_ref[d, :, off:off + n]
            return g

        for j, (name, n) in enumerate(_SMALL):
            g = total(_SLOT_OFF[j], n)
            delta, m_new, v_new = _adam_math(g, ins[3 * j][...], ins[3 * j + 1][...], ins[3 * j + 2][...])
            outs[4 * j][...] = g
            outs[4 * j + 1][...] = delta
            outs[4 * j + 2][...] = m_new
            outs[4 * j + 3][...] = v_new
        outs[-1][...] = total(_LOSS_OFF, 1)

    flat = []
    for name, n in _SMALL:
        flat += [w[name].reshape(1, n), m[name].reshape(1, n), v[name].reshape(1, n)]
    out_shape = [jax.ShapeDtypeStruct((1, n), F32) for _, n in _SMALL for _ in range(4)] + [jax.ShapeDtypeStruct((1, 1), F32)]
    out = pl.pallas_call(body, name="adam_small", out_shape=out_shape, compiler_params=_params())(pack_all, *flat)
    res = {name: [out[4 * j + t].reshape(w[name].shape) for t in range(4)] for j, name in enumerate(names)}
    return res, out[-1]


WEIGHTS = ("w_ada", "b_ada", "norm_w", "w_in", "q_norm_w", "k_norm_w", "rel_bias", "sinks", "conv_w", "conv_b",
           "dt_bias", "a_log", "d_skip", "ssm_norm_w", "w_attn_proj", "w_ssm_proj", "w_out")


def kernel(x, c, w_ada, b_ada, norm_w, w_in, q_norm_w, k_norm_w, rel_bias, sinks, conv_w, conv_b, dt_bias, a_log, d_skip, ssm_norm_w, w_attn_proj, w_ssm_proj, w_out, loss_target, m_w_ada, m_b_ada, m_norm_w, m_w_in, m_q_norm_w, m_k_norm_w, m_rel_bias, m_sinks, m_conv_w, m_conv_b, m_dt_bias, m_a_log, m_d_skip, m_ssm_norm_w, m_w_attn_proj, m_w_ssm_proj, m_w_out, v_w_ada, v_b_ada, v_norm_w, v_w_in, v_q_norm_w, v_k_norm_w, v_rel_bias, v_sinks, v_conv_w, v_conv_b, v_dt_bias, v_a_log, v_d_skip, v_ssm_norm_w, v_w_attn_proj, v_w_ssm_proj, v_w_out):
    w = dict(w_ada=w_ada, b_ada=b_ada, norm_w=norm_w, w_in=w_in, q_norm_w=q_norm_w, k_norm_w=k_norm_w,
             rel_bias=rel_bias, sinks=sinks, conv_w=conv_w, conv_b=conv_b, dt_bias=dt_bias, a_log=a_log,
             d_skip=d_skip, ssm_norm_w=ssm_norm_w, w_attn_proj=w_attn_proj, w_ssm_proj=w_ssm_proj, w_out=w_out)
    m = dict(w_ada=m_w_ada, b_ada=m_b_ada, norm_w=m_norm_w, w_in=m_w_in, q_norm_w=m_q_norm_w, k_norm_w=m_k_norm_w,
             rel_bias=m_rel_bias, sinks=m_sinks, conv_w=m_conv_w, conv_b=m_conv_b, dt_bias=m_dt_bias, a_log=m_a_log,
             d_skip=m_d_skip, ssm_norm_w=m_ssm_norm_w, w_attn_proj=m_w_attn_proj, w_ssm_proj=m_w_ssm_proj, w_out=m_w_out)
    v = dict(w_ada=v_w_ada, b_ada=v_b_ada, norm_w=v_norm_w, w_in=v_w_in, q_norm_w=v_q_norm_w, k_norm_w=v_k_norm_w,
             rel_bias=v_rel_bias, sinks=v_sinks, conv_w=v_conv_w, conv_b=v_conv_b, dt_bias=v_dt_bias, a_log=v_a_log,
             d_skip=v_d_skip, ssm_norm_w=v_ssm_norm_w, w_attn_proj=v_w_attn_proj, w_ssm_proj=v_w_ssm_proj, w_out=v_w_out)
    me = 4 * lax.axis_index("x") + 2 * lax.axis_index("y") + lax.axis_index("c")
    ada_n = w_ada.shape[2]
    in_n = w_in.shape[2]
    cw_n = conv_w.shape[2]

    b_piece = lax.dynamic_slice_in_dim(b_ada, me * ada_n, ada_n, axis=1)
    first, mod_all = _gather_mod(jnp.concatenate([c, conv_w[0].reshape(1, CONV_K * cw_n)], axis=1), w_ada[0], b_piece)
    first = first[:, 0]
    c_all = first[:, :D_MODEL]
    conv_w_full = first[:, D_MODEL:].reshape(N_DEV, CONV_K, cw_n).transpose(1, 0, 2).reshape(CONV_K, XBC_W)
    mod = lax.dynamic_index_in_dim(mod_all, me, axis=1, keepdims=False).reshape(1, 3 * D_MODEL)
    shift, scale, gate = mod[:, :D_MODEL], mod[:, D_MODEL:2 * D_MODEL], mod[:, 2 * D_MODEL:]

    pad = -in_n % 24
    w_t, zero = _ag_relayed(jnp.pad(w_in[0].T.astype(BF), ((0, pad), (0, 0))), "ag_w_in", chunks=3)
    w_t = w_t[:, :in_n].reshape(N_DEV * in_n, D_MODEL)

    def with_mine(blocks, mine):
        return lax.dynamic_update_index_in_dim(lax.empty(blocks, mine.dtype), mine, me, axis=0)

    rows = jnp.concatenate([w_attn_proj[0], w_ssm_proj[0], w_out[0]], axis=0).astype(BF) + zero
    r_ap, r_sp = w_attn_proj.shape[1], w_ssm_proj.shape[1]
    rows_started, zero = _exchange_start(rows, with_mine((N_DEV,) + rows.shape, rows), True, "ag_rows_start")

    def rows_fn(after):
        return _exchange_wait(rows_started, after, True, "ag_rows_wait")

    started = {}

    def send_blocks(key, g, name):
        started[key], zero = _exchange_start(
            g, with_mine(g.shape, lax.dynamic_index_in_dim(g, me, axis=0, keepdims=False)), False, name)
        return zero

    def after_mid(g_wap, g_wsp, g_wout):
        return send_blocks("rows", jnp.concatenate(
            [g_wap.reshape(N_DEV, r_ap, D_MODEL), g_wsp.reshape(N_DEV, r_sp, D_MODEL),
             g_wout.reshape(N_DEV, r_ap, D_MODEL)], axis=1), "rs_rows_start")

    def after_gw(g_ws):
        return send_blocks("in", jnp.concatenate(g_ws, axis=0).reshape(N_DEV, in_n, D_MODEL), "rs_in_start")

    r = _local_step(x[0], loss_target[0], shift, scale + zero, gate, w_t, rows_fn, norm_w, q_norm_w, k_norm_w,
                    rel_bias, sinks, conv_w_full, conv_b, dt_bias, a_log, d_skip, ssm_norm_w, after_mid, after_gw)

    small = dict(b_ada=r["dmod"], norm_w=r["g_norm_w"], q_norm_w=r["g_qnw"], k_norm_w=r["g_knw"], rel_bias=r["g_rel"],
                 sinks=r["g_sinks"], conv_b=r["g_conv_b"], dt_bias=r["g_dt_bias"], a_log=r["g_a_log"],
                 d_skip=r["g_d_skip"], ssm_norm_w=r["g_ssm_nw"])
    pack_all = _ag_direct(_pack_partials(small, r["loss"], r["g_conv_w"]), "ag_small")
    res, loss = _adam_small(pack_all, w, m, v)
    loss = loss[0, 0]
    cw_parts = pack_all[:, 0, _CW_OFF:].reshape(N_DEV, CONV_K, XBC_W)
    cw_mine = lax.dynamic_slice_in_dim(cw_parts, me * cw_n, cw_n, axis=2)
    res["conv_w"] = [a[None] for a in _adam(cw_mine, conv_w[0], m_conv_w[0], v_conv_w[0], "adam_conv_w")]

    dmod_piece = lax.dynamic_slice_in_dim(pack_all[:, 0, :3 * D_MODEL], me * ada_n, ada_n, axis=1)
    g_ada = _gw_ada(c_all, dmod_piece)
    res["w_ada"] = [a[None] for a in _adam(g_ada[None], w_ada[0], m_w_ada[0], v_w_ada[0], "adam_w_ada")]

    cat = lambda d: jnp.concatenate([d["w_attn_proj"][0], d["w_ssm_proj"][0], d["w_out"][0]], axis=0)
    rows_res = _adam(_exchange_wait(started["rows"], g_ada, False, "rs_rows_wait"), cat(w), cat(m), cat(v), "adam_w_rows")
    res["w_in"] = [a.T[None] for a in _adam(_exchange_wait(started["in"], rows_res[0], False, "rs_in_wait"),
                                            w_in[0].T, m_w_in[0].T, v_w_in[0].T, "adam_w_in")]
    res["w_attn_proj"] = [a[None, :r_ap] for a in rows_res]
    res["w_ssm_proj"] = [a[None, r_ap:r_ap + r_sp] for a in rows_res]
    res["w_out"] = [a[None, r_ap + r_sp:] for a in rows_res]

    outs = [loss, r["grad_x"][None]]
    for j in range(4):
        outs += [res[name][j] for name in WEIGHTS]
    return tuple(outs)
```
